```python
import math
import jax, jax.numpy as jnp
from jax import lax
import numpy as np

D_MODEL = 1024
BATCH = 8
SEQ = 8192
DEPTH = 1

D_MIX = D_MODEL
RET_HEADS = 4
RET_HEAD_DIM = (D_MIX // 2) // RET_HEADS
RET_WIDTH = RET_HEADS * RET_HEAD_DIM
RET_CHUNK = 128
ATT_HEADS = 8
ATT_HEAD_DIM = (D_MIX - RET_WIDTH) // ATT_HEADS
ATT_WIDTH = ATT_HEADS * ATT_HEAD_DIM
DILATED_PATTERN = ((128, 1), (512, 4), (2048, 16))
BAND_BLOCK = 128
D_FF = 2816
ROPE_BASE = 10000.0
NORM_EPS = 1e-6
GN_EPS = 1e-6
IN_COLS = 4 * RET_WIDTH + 3 * ATT_WIDTH

kernel_name = "hybrid_retention_dilated_macaron"


def rms_norm(x, g):
    xf = x.astype(jnp.float32)
    y = xf * lax.rsqrt(jnp.mean(xf * xf, axis=-1, keepdims=True) + NORM_EPS)
    return (y * g.astype(jnp.float32)).astype(x.dtype)


def swiglu(x, w_gate, w_up, w_down):
    return (jax.nn.silu(x @ w_gate) * (x @ w_up)) @ w_down


def split_heads(t, n_heads):
    b, s, _ = t.shape
    return t.reshape(b, s, n_heads, -1).transpose(0, 2, 1, 3)


def rotate_every_two(t):
    t1 = t[..., ::2]
    t2 = t[..., 1::2]
    return jnp.stack((-t2, t1), axis=-1).reshape(t.shape)


def apply_rotary(t):
    s, d = t.shape[-2], t.shape[-1]
    pos = jnp.arange(s, dtype=jnp.float32)
    inv_freq = ROPE_BASE ** (-jnp.arange(0, d, 2, dtype=jnp.float32) / d)
    ang = jnp.repeat(pos[:, None] * inv_freq[None, :], 2, axis=-1)
    cos = jnp.cos(ang).astype(t.dtype)
    sin = jnp.sin(ang).astype(t.dtype)
    return t * cos + rotate_every_two(t) * sin


def chunkwise_retention(q, k, v):
    b, h, s, dk = q.shape
    dv = v.shape[-1]
    c = RET_CHUNK
    n = s // c
    dt = q.dtype
    log_g = jnp.log(1.0 - 2.0 ** (-5.0 - jnp.arange(h, dtype=jnp.float32)))
    idx = jnp.arange(c, dtype=jnp.float32)
    rel = idx[:, None] - idx[None, :]
    decay_in = jnp.where(rel >= 0, jnp.exp(log_g[:, None, None] * jnp.maximum(rel, 0.0)), 0.0)
    zeta = jnp.exp(log_g[:, None] * (c - 1 - idx)[None, :])
    xi = jnp.exp(log_g[:, None] * (idx + 1)[None, :])
    chunk_decay = jnp.exp(log_g * c)
    qc = q.reshape(b, h, n, c, dk)
    kc = k.reshape(b, h, n, c, dk)
    vc = v.reshape(b, h, n, c, dv)
    scores = jnp.einsum('bhncd,bhnmd->bhncm', qc, kc) * decay_in[None, :, None].astype(dt)
    intra = jnp.einsum('bhncm,bhnme->bhnce', scores, vc)
    kv = jnp.einsum('bhncd,bhnce->nbhde', kc * zeta[None, :, None, :, None].astype(dt), vc)
    gamma_c = chunk_decay[None, :, None, None].astype(dt)

    def step(state, kv_n):
        return state * gamma_c + kv_n, state

    _, prev_states = lax.scan(step, jnp.zeros((b, h, dk, dv), dt), kv)
    inter = jnp.einsum('bhncd,nbhde->bhnce', qc, prev_states) * xi[None, :, None, :, None].astype(dt)
    return (intra + inter).reshape(b, h, s, dv)


def dilated_window_branch(q, k, v, window, dilation):
    b, h, s, hd = q.shape
    L = s // dilation
    span = window // dilation
    blk = BAND_BLOCK
    nb = -(-L // blk)
    lp = nb * blk

    def to_blocks(t):
        t = t.reshape(b, h, L, dilation, hd).transpose(0, 1, 3, 2, 4)
        t = jnp.pad(t, ((0, 0), (0, 0), (0, 0), (0, lp - L), (0, 0)))
        return t.reshape(b, h, dilation, nb, blk, hd)

    def with_prev(t):
        prev = jnp.pad(t[:, :, :, :-1], ((0, 0), (0, 0), (0, 0), (1, 0), (0, 0), (0, 0)))
        return jnp.concatenate([prev, t], axis=4)

    qb = to_blocks(q)
    kb = with_prev(to_blocks(k))
    vb = with_prev(to_blocks(v))
    sc = jnp.einsum('bhgnqd,bhgnkd->bhgnqk', qb, kb).astype(jnp.float32)
    qi = jnp.arange(blk)[:, None]
    kj = jnp.arange(2 * blk)[None, :]
    dist = qi + blk - kj
    bidx = jnp.arange(nb)[:, None, None]
    mask = (dist >= 0) & (dist <= span) & (bidx * blk + kj - blk >= 0)
    sc = jnp.where(mask, sc, -jnp.inf)
    m = jnp.max(sc, axis=-1, keepdims=True)
    e = jnp.exp(sc - m)
    denom = jnp.sum(e, axis=-1, keepdims=True)
    lse = (m + jnp.log(denom))[..., 0]
    o = jnp.einsum('bhgnqk,bhgnkd->bhgnqd', (e / denom).astype(v.dtype), vb)
    o = o.reshape(b, h, dilation, lp, hd)[:, :, :, :L].transpose(0, 1, 3, 2, 4).reshape(b, h, s, hd)
    lse = lse.reshape(b, h, dilation, lp)[..., :L].transpose(0, 1, 3, 2).reshape(b, h, s)
    return o, lse


def dilated_attention(q, k, v):
    outs, lses = [], []
    for window, dilation in DILATED_PATTERN:
        o, l = dilated_window_branch(q, k, v, window, dilation)
        outs.append(o)
        lses.append(l)
    wts = jax.nn.softmax(jnp.stack(lses, axis=0), axis=0)
    return jnp.einsum('pbhs,pbhsd->bhsd', wts.astype(q.dtype), jnp.stack(outs, axis=0))


def _fwd_setup_inputs(seed: int = 0) -> dict:
    key = jax.random.key(seed)
    ks = jax.random.split(key, 16)
    f32 = jnp.float32

    def w(k_, shape, fan_in):
        return jax.random.normal(k_, shape, f32) * (fan_in ** -0.5)

    def gain(k_, shape):
        return 1.0 + 0.02 * jax.random.normal(k_, shape, f32)

    return {
        "x": jax.random.normal(ks[0], (BATCH, SEQ, D_MODEL), f32),
        "norm_ffn1": gain(ks[1], (DEPTH, D_MODEL)),
        "ffn1_w_gate": w(ks[2], (DEPTH, D_MODEL, D_FF), D_MODEL),
        "ffn1_w_up": w(ks[3], (DEPTH, D_MODEL, D_FF), D_MODEL),
        "ffn1_w_down": w(ks[4], (DEPTH, D_FF, D_MODEL), D_FF),
        "norm_mix": gain(ks[5], (DEPTH, D_MODEL)),
        "w_in": w(ks[6], (DEPTH, D_MODEL, IN_COLS), D_MODEL),
        "ret_norm_gain": gain(ks[7], (DEPTH, RET_WIDTH)),
        "w_out": w(ks[8], (DEPTH, D_MIX, D_MODEL), D_MIX),
        "norm_ffn2": gain(ks[9], (DEPTH, D_MODEL)),
        "ffn2_w_gate": w(ks[10], (DEPTH, D_MODEL, D_FF), D_MODEL),
        "ffn2_w_up": w(ks[11], (DEPTH, D_MODEL, D_FF), D_MODEL),
        "ffn2_w_down": w(ks[12], (DEPTH, D_FF, D_MODEL), D_FF),
        "norm_final": gain(ks[13], (D_MODEL,)),
    }


def _fwd_reference(x, norm_ffn1, ffn1_w_gate, ffn1_w_up, ffn1_w_down, norm_mix, w_in, ret_norm_gain,
              w_out, norm_ffn2, ffn2_w_gate, ffn2_w_up, ffn2_w_down, norm_final):
    b, s, _ = x.shape
    h = x
    for l in range(DEPTH):
        h = h + 0.5 * swiglu(rms_norm(h, norm_ffn1[l]), ffn1_w_gate[l], ffn1_w_up[l], ffn1_w_down[l])

        u = rms_norm(h, norm_mix[l]) @ w_in[l]
        rq, rk, rv, rg, aq, ak, av = jnp.split(
            u, np.cumsum([RET_WIDTH] * 4 + [ATT_WIDTH] * 2).tolist(), axis=-1)

        rq = apply_rotary(split_heads(rq, RET_HEADS))
        rk = apply_rotary(split_heads(rk, RET_HEADS)) * (RET_HEAD_DIM ** -0.5)
        ret = chunkwise_retention(rq, rk, split_heads(rv, RET_HEADS)).astype(jnp.float32)
        mu = jnp.mean(ret, axis=-1, keepdims=True)
        var = jnp.mean(jnp.square(ret - mu), axis=-1, keepdims=True)
        ret = ((ret - mu) * lax.rsqrt(var + GN_EPS)).transpose(0, 2, 1, 3).reshape(b, s, RET_WIDTH)
        ret = (ret * ret_norm_gain[l].astype(jnp.float32)).astype(x.dtype) * jax.nn.silu(rg)

        att = dilated_attention(split_heads(aq, ATT_HEADS) * (ATT_HEAD_DIM ** -0.5),
                                split_heads(ak, ATT_HEADS), split_heads(av, ATT_HEADS))
        att = att.transpose(0, 2, 1, 3).reshape(b, s, ATT_WIDTH)

        h = h + jnp.concatenate([ret, att], axis=-1) @ w_out[l]

        h = h + 0.5 * swiglu(rms_norm(h, norm_ffn2[l]), ffn2_w_gate[l], ffn2_w_up[l], ffn2_w_down[l])
    return rms_norm(h, norm_final)


import jax as _jax
import jax.numpy as _jnp

TWIN_FORMAT = 'train_step'
FWD_PARAMS = ['x', 'norm_ffn1', 'ffn1_w_gate', 'ffn1_w_up', 'ffn1_w_down', 'norm_mix', 'w_in', 'ret_norm_gain', 'w_out', 'norm_ffn2', 'ffn2_w_gate', 'ffn2_w_up', 'ffn2_w_down', 'norm_final']
TWIN_WEIGHTS = ['norm_ffn1', 'ffn1_w_gate', 'ffn1_w_up', 'ffn1_w_down', 'norm_mix', 'w_in', 'ret_norm_gain', 'w_out', 'norm_ffn2', 'ffn2_w_gate', 'ffn2_w_up', 'ffn2_w_down', 'norm_final']
TWIN_DIFF_INPUT = 'x'
TWIN_INPUTS = ['x', 'norm_ffn1', 'ffn1_w_gate', 'ffn1_w_up', 'ffn1_w_down', 'norm_mix', 'w_in', 'ret_norm_gain', 'w_out', 'norm_ffn2', 'ffn2_w_gate', 'ffn2_w_up', 'ffn2_w_down', 'norm_final', 'loss_target', 'm_norm_ffn1', 'm_ffn1_w_gate', 'm_ffn1_w_up', 'm_ffn1_w_down', 'm_norm_mix', 'm_w_in', 'm_ret_norm_gain', 'm_w_out', 'm_norm_ffn2', 'm_ffn2_w_gate', 'm_ffn2_w_up', 'm_ffn2_w_down', 'm_norm_final', 'v_norm_ffn1', 'v_ffn1_w_gate', 'v_ffn1_w_up', 'v_ffn1_w_down', 'v_norm_mix', 'v_w_in', 'v_ret_norm_gain', 'v_w_out', 'v_norm_ffn2', 'v_ffn2_w_gate', 'v_ffn2_w_up', 'v_ffn2_w_down', 'v_norm_final']
TWIN_OUTPUTS = ['loss', 'grad_x', 'grad_norm_ffn1', 'grad_ffn1_w_gate', 'grad_ffn1_w_up', 'grad_ffn1_w_down', 'grad_norm_mix', 'grad_w_in', 'grad_ret_norm_gain', 'grad_w_out', 'grad_norm_ffn2', 'grad_ffn2_w_gate', 'grad_ffn2_w_up', 'grad_ffn2_w_down', 'grad_norm_final', 'delta_norm_ffn1', 'delta_ffn1_w_gate', 'delta_ffn1_w_up', 'delta_ffn1_w_down', 'delta_norm_mix', 'delta_w_in', 'delta_ret_norm_gain', 'delta_w_out', 'delta_norm_ffn2', 'delta_ffn2_w_gate', 'delta_ffn2_w_up', 'delta_ffn2_w_down', 'delta_norm_final', 'new_m_norm_ffn1', 'new_m_ffn1_w_gate', 'new_m_ffn1_w_up', 'new_m_ffn1_w_down', 'new_m_norm_mix', 'new_m_w_in', 'new_m_ret_norm_gain', 'new_m_w_out', 'new_m_norm_ffn2', 'new_m_ffn2_w_gate', 'new_m_ffn2_w_up', 'new_m_ffn2_w_down', 'new_m_norm_final', 'new_v_norm_ffn1', 'new_v_ffn1_w_gate', 'new_v_ffn1_w_up', 'new_v_ffn1_w_down', 'new_v_norm_mix', 'new_v_w_in', 'new_v_ret_norm_gain', 'new_v_w_out', 'new_v_norm_ffn2', 'new_v_ffn2_w_gate', 'new_v_ffn2_w_up', 'new_v_ffn2_w_down', 'new_v_norm_final']
TWIN_LEAF_KINDS = {'loss': 'loss', 'grad_x': 'grad_x', 'grad_norm_ffn1': 'grad_w', 'grad_ffn1_w_gate': 'grad_w', 'grad_ffn1_w_up': 'grad_w', 'grad_ffn1_w_down': 'grad_w', 'grad_norm_mix': 'grad_w', 'grad_w_in': 'grad_w', 'grad_ret_norm_gain': 'grad_w', 'grad_w_out': 'grad_w', 'grad_norm_ffn2': 'grad_w', 'grad_ffn2_w_gate': 'grad_w', 'grad_ffn2_w_up': 'grad_w', 'grad_ffn2_w_down': 'grad_w', 'grad_norm_final': 'grad_w', 'delta_norm_ffn1': 'delta_w', 'delta_ffn1_w_gate': 'delta_w', 'delta_ffn1_w_up': 'delta_w', 'delta_ffn1_w_down': 'delta_w', 'delta_norm_mix': 'delta_w', 'delta_w_in': 'delta_w', 'delta_ret_norm_gain': 'delta_w', 'delta_w_out': 'delta_w', 'delta_norm_ffn2': 'delta_w', 'delta_ffn2_w_gate': 'delta_w', 'delta_ffn2_w_up': 'delta_w', 'delta_ffn2_w_down': 'delta_w', 'delta_norm_final': 'delta_w', 'new_m_norm_ffn1': 'new_m', 'new_m_ffn1_w_gate': 'new_m', 'new_m_ffn1_w_up': 'new_m', 'new_m_ffn1_w_down': 'new_m', 'new_m_norm_mix': 'new_m', 'new_m_w_in': 'new_m', 'new_m_ret_norm_gain': 'new_m', 'new_m_w_out': 'new_m', 'new_m_norm_ffn2': 'new_m', 'new_m_ffn2_w_gate': 'new_m', 'new_m_ffn2_w_up': 'new_m', 'new_m_ffn2_w_down': 'new_m', 'new_m_norm_final': 'new_m', 'new_v_norm_ffn1': 'new_v', 'new_v_ffn1_w_gate': 'new_v', 'new_v_ffn1_w_up': 'new_v', 'new_v_ffn1_w_down': 'new_v', 'new_v_norm_mix': 'new_v', 'new_v_w_in': 'new_v', 'new_v_ret_norm_gain': 'new_v', 'new_v_w_out': 'new_v', 'new_v_norm_ffn2': 'new_v', 'new_v_ffn2_w_gate': 'new_v', 'new_v_ffn2_w_up': 'new_v', 'new_v_ffn2_w_down': 'new_v', 'new_v_norm_final': 'new_v'}


def _forward(args):
    return _fwd_reference(*[args[k] for k in FWD_PARAMS])


def _output_shape():
    def fwd():
        inp = _fwd_setup_inputs(0)
        return _fwd_reference(*[inp[k] for k in FWD_PARAMS])
    out = _jax.eval_shape(fwd)
    return out.shape, out.dtype

N_MICROBATCH = 1
ADAM_LR = 0.001
ADAM_B1 = 0.9
ADAM_B2 = 0.999
ADAM_EPS = 1e-08
ADAM_WD = 0.01
ADAM_STEP = 10
PER_EXAMPLE_BATCH_AXIS = {'x': 0, 'loss_target': 0}
SHARED_INPUTS = []
_WEIGHT_DTYPES = {'norm_ffn1': _jnp.float32, 'ffn1_w_gate': _jnp.float32, 'ffn1_w_up': _jnp.float32, 'ffn1_w_down': _jnp.float32, 'norm_mix': _jnp.float32, 'w_in': _jnp.float32, 'ret_norm_gain': _jnp.float32, 'w_out': _jnp.float32, 'norm_ffn2': _jnp.float32, 'ffn2_w_gate': _jnp.float32, 'ffn2_w_up': _jnp.float32, 'ffn2_w_down': _jnp.float32, 'norm_final': _jnp.float32}
MOMENT_SCALE = {'norm_ffn1': 1.331000e-01, 'ffn1_w_gate': 5.542867e-02, 'ffn1_w_up': 5.373058e-02, 'ffn1_w_down': 8.893896e-02, 'norm_mix': 2.031868e-01, 'w_in': 1.086347e-01, 'ret_norm_gain': 1.449527e-01, 'w_out': 1.026505e-01, 'norm_ffn2': 9.639553e-02, 'ffn2_w_gate': 3.999221e-02, 'ffn2_w_up': 3.876955e-02, 'ffn2_w_down': 6.449702e-02, 'norm_final': 6.400086e+01}


def _to_microbatches(a, axis):
    t = _jnp.moveaxis(a, axis, 0)
    t = t.reshape((N_MICROBATCH, t.shape[0] // N_MICROBATCH) + t.shape[1:])
    return _jnp.moveaxis(t, 1, axis + 1)


def setup_inputs(seed: int = 0) -> dict:
    inp = _fwd_setup_inputs(seed)
    key = _jax.random.fold_in(_jax.random.key(seed), 7919)
    shape, _ = _output_shape()
    out = dict(inp)
    out["loss_target"] = _jax.random.normal(_jax.random.fold_in(key, 0), shape, _jnp.float32)
    for i, name in enumerate(TWIN_WEIGHTS):
        w = inp[name].astype(_jnp.float32)
        if MOMENT_SCALE is None:
            s = _jnp.sqrt(_jnp.mean(_jnp.square(w)) + 1e-30)
        else:
            s = MOMENT_SCALE[name]
        km, kv = _jax.random.split(_jax.random.fold_in(key, i + 1))
        out[name] = w
        out["m_" + name] = s * _jax.random.normal(km, w.shape, _jnp.float32)
        out["v_" + name] = (s * s) * _jax.random.uniform(kv, w.shape, _jnp.float32, 0.5, 1.5)
    if N_MICROBATCH > 1:
        for name, axis in PER_EXAMPLE_BATCH_AXIS.items():
            out[name] = _to_microbatches(out[name], axis)
    return {'x': out['x'], 'norm_ffn1': out['norm_ffn1'], 'ffn1_w_gate': out['ffn1_w_gate'], 'ffn1_w_up': out['ffn1_w_up'], 'ffn1_w_down': out['ffn1_w_down'], 'norm_mix': out['norm_mix'], 'w_in': out['w_in'], 'ret_norm_gain': out['ret_norm_gain'], 'w_out': out['w_out'], 'norm_ffn2': out['norm_ffn2'], 'ffn2_w_gate': out['ffn2_w_gate'], 'ffn2_w_up': out['ffn2_w_up'], 'ffn2_w_down': out['ffn2_w_down'], 'norm_final': out['norm_final'], 'loss_target': out['loss_target'], 'm_norm_ffn1': out['m_norm_ffn1'], 'm_ffn1_w_gate': out['m_ffn1_w_gate'], 'm_ffn1_w_up': out['m_ffn1_w_up'], 'm_ffn1_w_down': out['m_ffn1_w_down'], 'm_norm_mix': out['m_norm_mix'], 'm_w_in': out['m_w_in'], 'm_ret_norm_gain': out['m_ret_norm_gain'], 'm_w_out': out['m_w_out'], 'm_norm_ffn2': out['m_norm_ffn2'], 'm_ffn2_w_gate': out['m_ffn2_w_gate'], 'm_ffn2_w_up': out['m_ffn2_w_up'], 'm_ffn2_w_down': out['m_ffn2_w_down'], 'm_norm_final': out['m_norm_final'], 'v_norm_ffn1': out['v_norm_ffn1'], 'v_ffn1_w_gate': out['v_ffn1_w_gate'], 'v_ffn1_w_up': out['v_ffn1_w_up'], 'v_ffn1_w_down': out['v_ffn1_w_down'], 'v_norm_mix': out['v_norm_mix'], 'v_w_in': out['v_w_in'], 'v_ret_norm_gain': out['v_ret_norm_gain'], 'v_w_out': out['v_w_out'], 'v_norm_ffn2': out['v_norm_ffn2'], 'v_ffn2_w_gate': out['v_ffn2_w_gate'], 'v_ffn2_w_up': out['v_ffn2_w_up'], 'v_ffn2_w_down': out['v_ffn2_w_down'], 'v_norm_final': out['v_norm_final']}


def _loss(weights, diff, rest, loss_target):
    with _jax.named_scope("forward"):
        args = {**rest, TWIN_DIFF_INPUT: diff, **{k: w.astype(_WEIGHT_DTYPES[k]) for k, w in weights.items()}}
        y = _forward(args)
    with _jax.named_scope("loss_head"):
        err = _jnp.square(y.astype(_jnp.float32) - loss_target)
        return 0.5 * _jnp.sum(_jnp.mean(err, axis=-1)) if err.ndim else 0.5 * err


def _adamw(w, g, m, v):
    m = ADAM_B1 * m + (1.0 - ADAM_B1) * g
    v = ADAM_B2 * v + (1.0 - ADAM_B2) * _jnp.square(g)
    m_hat = m / (1.0 - ADAM_B1 ** ADAM_STEP)
    v_hat = v / (1.0 - ADAM_B2 ** ADAM_STEP)
    delta = -ADAM_LR * (m_hat / (_jnp.sqrt(v_hat) + ADAM_EPS) + ADAM_WD * w)
    return delta, m, v


def reference(x, norm_ffn1, ffn1_w_gate, ffn1_w_up, ffn1_w_down, norm_mix, w_in, ret_norm_gain, w_out, norm_ffn2, ffn2_w_gate, ffn2_w_up, ffn2_w_down, norm_final, loss_target, m_norm_ffn1, m_ffn1_w_gate, m_ffn1_w_up, m_ffn1_w_down, m_norm_mix, m_w_in, m_ret_norm_gain, m_w_out, m_norm_ffn2, m_ffn2_w_gate, m_ffn2_w_up, m_ffn2_w_down, m_norm_final, v_norm_ffn1, v_ffn1_w_gate, v_ffn1_w_up, v_ffn1_w_down, v_norm_mix, v_w_in, v_ret_norm_gain, v_w_out, v_norm_ffn2, v_ffn2_w_gate, v_ffn2_w_up, v_ffn2_w_down, v_norm_final):
    given = dict(x=x, norm_ffn1=norm_ffn1, ffn1_w_gate=ffn1_w_gate, ffn1_w_up=ffn1_w_up, ffn1_w_down=ffn1_w_down, norm_mix=norm_mix, w_in=w_in, ret_norm_gain=ret_norm_gain, w_out=w_out, norm_ffn2=norm_ffn2, ffn2_w_gate=ffn2_w_gate, ffn2_w_up=ffn2_w_up, ffn2_w_down=ffn2_w_down, norm_final=norm_final, loss_target=loss_target, m_norm_ffn1=m_norm_ffn1, m_ffn1_w_gate=m_ffn1_w_gate, m_ffn1_w_up=m_ffn1_w_up, m_ffn1_w_down=m_ffn1_w_down, m_norm_mix=m_norm_mix, m_w_in=m_w_in, m_ret_norm_gain=m_ret_norm_gain, m_w_out=m_w_out, m_norm_ffn2=m_norm_ffn2, m_ffn2_w_gate=m_ffn2_w_gate, m_ffn2_w_up=m_ffn2_w_up, m_ffn2_w_down=m_ffn2_w_down, m_norm_final=m_norm_final, v_norm_ffn1=v_norm_ffn1, v_ffn1_w_gate=v_ffn1_w_gate, v_ffn1_w_up=v_ffn1_w_up, v_ffn1_w_down=v_ffn1_w_down, v_norm_mix=v_norm_mix, v_w_in=v_w_in, v_ret_norm_gain=v_ret_norm_gain, v_w_out=v_w_out, v_norm_ffn2=v_norm_ffn2, v_ffn2_w_gate=v_ffn2_w_gate, v_ffn2_w_up=v_ffn2_w_up, v_ffn2_w_down=v_ffn2_w_down, v_norm_final=v_norm_final)
    weights = {n: given[n] for n in TWIN_WEIGHTS}
    shared = {n: given[n] for n in SHARED_INPUTS}
    per_example = {n: given[n] for n in ['x']}
    grad_fn = _jax.value_and_grad(_loss, argnums=(0, 1))

    def one_microbatch(ex, loss_target):
        ex = dict(ex)
        diff = ex.pop(TWIN_DIFF_INPUT)
        return grad_fn(weights, diff, {**shared, **ex}, loss_target)

    if N_MICROBATCH == 1:
        loss, (grad_w, grad_x) = one_microbatch(per_example, given["loss_target"])
    else:
        def body(carry, xs):
            loss_sum, grad_sum = carry
            l_k, (gw_k, gx_k) = one_microbatch(xs[0], xs[1])
            with _jax.named_scope("update"):
                return (loss_sum + l_k, _jax.tree.map(_jnp.add, grad_sum, gw_k)), gx_k

        init = (_jnp.zeros((), _jnp.float32), _jax.tree.map(_jnp.zeros_like, weights))
        (loss, grad_w), grad_x = _jax.lax.scan(body, init, (per_example, given["loss_target"]))
    with _jax.named_scope("update"):
        delta_w, new_m, new_v = {}, {}, {}
        for n in TWIN_WEIGHTS:
            delta_w[n], new_m[n], new_v[n] = _adamw(weights[n], grad_w[n], given["m_" + n], given["v_" + n])
    return (loss, grad_x, *[grad_w[n] for n in TWIN_WEIGHTS], *[delta_w[n] for n in TWIN_WEIGHTS],
            *[new_m[n] for n in TWIN_WEIGHTS], *[new_v[n] for n in TWIN_WEIGHTS])
```

```python
import functools
import math

import jax
import jax.numpy as jnp
from jax import lax
from jax.experimental import pallas as pl
from jax.experimental.pallas import tpu as pltpu

F32 = jnp.float32
BF16 = jnp.bfloat16
MESH = pl.DeviceIdType.MESH

NORM_EPS = 1e-6
GN_EPS = 1e-6
ROPE_BASE = 10000.0
RET_HEADS = 4
RET_DIM = 128
RET_WIDTH = 512
RET_CHUNK = 128
ATT_HEADS = 8
ATT_DIM = 64
ATT_WIDTH = 512
ATT_BLOCK = 128
DILATIONS = (1, 4, 16)
IN_COLS = 4 * RET_WIDTH + 3 * ATT_WIDTH
LANE = 128
N_SHARD = 4
ADAM_LR, ADAM_B1, ADAM_B2, ADAM_EPS, ADAM_WD, ADAM_STEP = 0.001, 0.9, 0.999, 1e-08, 0.01, 10

V7X_VMEM_BYTES = 64 * 1024 * 1024
VMEM_LIMIT = V7X_VMEM_BYTES - 8 * 1024 * 1024

NT = (((1,), (1,)), ((), ()))
TN = (((0,), (0,)), ((), ()))


def _params(*sem):
    return pltpu.CompilerParams(dimension_semantics=sem, vmem_limit_bytes=VMEM_LIMIT)


def _dot(a, b, dims=None):
    if dims is None:
        return jnp.dot(a, b, preferred_element_type=F32)
    return lax.dot_general(a, b, dims, preferred_element_type=F32)


def _sigmoid(x):
    return 1.0 / (1.0 + jnp.exp(-x))


def _load_weights(pairs, sems):
    copies = [pltpu.make_async_copy(src, dst, sems.at[k]) for k, (src, dst) in enumerate(pairs)]
    for cp in copies:
        cp.start()
    for cp in copies:
        cp.wait()


def _rows8(v):
    r, c = v.shape
    return v.reshape(r // 8, 8, c).sum(axis=0)


def _ffn_fwd(x, gain, wg, wu, wd, name):
    t, d = x.shape
    ns, _, fs = wg.shape
    tm = min(256, t)

    def body(x_ref, gain_ref, wg_hbm, wu_hbm, wd_hbm, h_ref, xn_ref, g_ref, u_ref, wg_v, wu_v, wd_v, sems):
        @pl.when(pl.program_id(0) == 0)
        def _():
            _load_weights([(wg_hbm, wg_v), (wu_hbm, wu_v), (wd_hbm, wd_v)], sems)

        xv = x_ref[...]
        r = lax.rsqrt(jnp.mean(xv * xv, axis=-1, keepdims=True) + NORM_EPS)
        xn = (xv * r * gain_ref[...]).astype(BF16)
        xn_ref[...] = xn
        acc = jnp.zeros((tm, d), F32)
        for j in range(ns):
            g = _dot(xn, wg_v[j])
            u = _dot(xn, wu_v[j])
            g_ref[j] = g.astype(BF16)
            u_ref[j] = u.astype(BF16)
            a = (g * _sigmoid(g) * u).astype(BF16)
            acc = acc + _dot(a, wd_v[j])
        h_ref[...] = xv + 0.5 * acc

    hbm = pl.BlockSpec(memory_space=pl.ANY)
    return pl.pallas_call(
        body, name=name, grid=(t // tm,),
        in_specs=[pl.BlockSpec((tm, d), lambda i: (i, 0)), pl.BlockSpec((1, d), lambda i: (0, 0)), hbm, hbm, hbm],
        out_specs=[pl.BlockSpec((tm, d), lambda i: (i, 0)), pl.BlockSpec((tm, d), lambda i: (i, 0)),
                   pl.BlockSpec((ns, tm, fs), lambda i: (0, i, 0)), pl.BlockSpec((ns, tm, fs), lambda i: (0, i, 0))],
        out_shape=[jax.ShapeDtypeStruct((t, d), F32), jax.ShapeDtypeStruct((t, d), BF16),
                   jax.ShapeDtypeStruct((ns, t, fs), BF16), jax.ShapeDtypeStruct((ns, t, fs), BF16)],
        scratch_shapes=[pltpu.VMEM(wg.shape, BF16), pltpu.VMEM(wu.shape, BF16), pltpu.VMEM(wd.shape, BF16),
                        pltpu.SemaphoreType.DMA((3,))],
        compiler_params=_params("arbitrary"),
    )(x, gain, wg, wu, wd)


def _ffn_bwd_data(dy, x, gain, g, u, wg, wu, wd, name):
    t, d = x.shape
    ns, _, fs = wg.shape
    tm = min(256, t)

    def body(dy_ref, x_ref, gain_ref, g_ref, u_ref, wg_hbm, wu_hbm, wd_hbm, dx_ref, dg_ref, du_ref, dgain_ref,
             wg_v, wu_v, wd_v, sems):
        @pl.when(pl.program_id(0) == 0)
        def _():
            _load_weights([(wg_hbm, wg_v), (wu_hbm, wu_v), (wd_hbm, wd_v)], sems)
            dgain_ref[...] = jnp.zeros_like(dgain_ref)

        dyv = dy_ref[...]
        dyh = (0.5 * dyv).astype(BF16)
        dxn = jnp.zeros((tm, d), F32)
        for j in range(ns):
            da = _dot(dyh, wd_v[j], NT)
            gj = g_ref[j].astype(F32)
            uj = u_ref[j].astype(F32)
            sig = _sigmoid(gj)
            dgj = (da * uj * (sig * (1.0 + gj * (1.0 - sig)))).astype(BF16)
            duj = (da * (gj * sig)).astype(BF16)
            dg_ref[j] = dgj
            du_ref[j] = duj
            dxn = dxn + _dot(dgj, wg_v[j], NT) + _dot(duj, wu_v[j], NT)
        xv = x_ref[...]
        r = lax.rsqrt(jnp.mean(xv * xv, axis=-1, keepdims=True) + NORM_EPS)
        xh = xv * r
        dgain_ref[...] += _rows8(dxn * xh)
        dxh = dxn * gain_ref[...]
        dx_ref[...] = dyv + r * (dxh - xh * jnp.mean(dxh * xh, axis=-1, keepdims=True))

    hbm = pl.BlockSpec(memory_space=pl.ANY)
    tile = pl.BlockSpec((tm, d), lambda i: (i, 0))
    hid = pl.BlockSpec((ns, tm, fs), lambda i: (0, i, 0))
    return pl.pallas_call(
        body, name=name, grid=(t // tm,),
        in_specs=[tile, tile, pl.BlockSpec((1, d), lambda i: (0, 0)), hid, hid, hbm, hbm, hbm],
        out_specs=[tile, hid, hid, pl.BlockSpec((8, d), lambda i: (0, 0))],
        out_shape=[jax.ShapeDtypeStruct((t, d), F32), jax.ShapeDtypeStruct((ns, t, fs), BF16),
                   jax.ShapeDtypeStruct((ns, t, fs), BF16), jax.ShapeDtypeStruct((8, d), F32)],
        scratch_shapes=[pltpu.VMEM(wg.shape, BF16), pltpu.VMEM(wu.shape, BF16), pltpu.VMEM(wd.shape, BF16),
                        pltpu.SemaphoreType.DMA((3,))],
        compiler_params=_params("arbitrary"),
    )(dy, x, gain, g, u, wg, wu, wd)


def _ffn_wgrad(xn, dy, g, u, dg, du, name):
    t, d = xn.shape
    ns, _, fs = g.shape
    tk = min(512, t)

    def body(xn_ref, dy_ref, g_ref, u_ref, dg_ref, du_ref, dwg_ref, dwu_ref, dwd_ref):
        @pl.when(pl.program_id(1) == 0)
        def _():
            dwg_ref[...] = jnp.zeros_like(dwg_ref)
            dwu_ref[...] = jnp.zeros_like(dwu_ref)
            dwd_ref[...] = jnp.zeros_like(dwd_ref)

        xnv = xn_ref[...]
        dwg_ref[...] += _dot(xnv, dg_ref[...], TN)
        dwu_ref[...] += _dot(xnv, du_ref[...], TN)
        gj = g_ref[...].astype(F32)
        a = (gj * _sigmoid(gj) * u_ref[...].astype(F32)).astype(BF16)
        dwd_ref[...] += _dot(a, (0.5 * dy_ref[...]).astype(BF16), TN)

    tile = pl.BlockSpec((tk, d), lambda j, k: (k, 0))
    hid = pl.BlockSpec((None, tk, fs), lambda j, k: (j, k, 0))
    return pl.pallas_call(
        body, name=name, grid=(ns, t // tk),
        in_specs=[tile, tile, hid, hid, hid, hid],
        out_specs=[pl.BlockSpec((None, d, fs), lambda j, k: (j, 0, 0)), pl.BlockSpec((None, d, fs), lambda j, k: (j, 0, 0)),
                   pl.BlockSpec((None, fs, d), lambda j, k: (j, 0, 0))],
        out_shape=[jax.ShapeDtypeStruct((ns, d, fs), F32), jax.ShapeDtypeStruct((ns, d, fs), F32),
                   jax.ShapeDtypeStruct((ns, fs, d), F32)],
        compiler_params=_params("arbitrary", "arbitrary"),
    )(xn, dy, g, u, dg, du)


def _tn_matmul(a, b, bn, name):
    t, m = a.shape
    n = b.shape[1]
    tk = min(512, t)

    def body(a_ref, b_ref, o_ref):
        @pl.when(pl.program_id(1) == 0)
        def _():
            o_ref[...] = jnp.zeros_like(o_ref)

        o_ref[...] += _dot(a_ref[...].astype(BF16), b_ref[...].astype(BF16), TN)

    return pl.pallas_call(
        body, name=name, grid=(n // bn, t // tk),
        in_specs=[pl.BlockSpec((tk, m), lambda j, k: (k, 0)), pl.BlockSpec((tk, bn), lambda j, k: (k, j))],
        out_specs=pl.BlockSpec((None, m, bn), lambda j, k: (j, 0, 0)),
        out_shape=jax.ShapeDtypeStruct((n // bn, m, bn), F32),
        compiler_params=_params("arbitrary", "arbitrary"),
    )(a, b)


def _inproj_fwd(h, gain, win):
    t, d = h.shape
    ns, _, cs = win.shape
    tm = min(512, t)

    def body(h_ref, gain_ref, w_ref, xn_ref, u_ref):
        hv = h_ref[...]
        r = lax.rsqrt(jnp.mean(hv * hv, axis=-1, keepdims=True) + NORM_EPS)
        xn = (hv * r * gain_ref[...]).astype(BF16)
        xn_ref[...] = xn
        for j in range(ns):
            u_ref[:, j * cs:(j + 1) * cs] = _dot(xn, w_ref[j])

    return pl.pallas_call(
        body, name="inproj_fwd", grid=(t // tm,),
        in_specs=[pl.BlockSpec((tm, d), lambda i: (i, 0)), pl.BlockSpec((1, d), lambda i: (0, 0)),
                  pl.BlockSpec(win.shape, lambda i: (0, 0, 0))],
        out_specs=[pl.BlockSpec((tm, d), lambda i: (i, 0)), pl.BlockSpec((tm, ns * cs), lambda i: (i, 0))],
        out_shape=[jax.ShapeDtypeStruct((t, d), BF16), jax.ShapeDtypeStruct((t, ns * cs), F32)],
        compiler_params=_params("arbitrary"),
    )(h, gain, win)


def _inproj_bwd(pieces, h, gain, dres, win):
    t, d = h.shape
    ns, _, cs = win.shape
    pw = pieces[0].shape[1]
    tm = min(512, t)
    npc = len(pieces)

    def body(*refs):
        p_refs = refs[:npc]
        h_ref, gain_ref, dres_ref, w_ref, dh_ref, du_ref, dgain_ref = refs[npc:]

        @pl.when(pl.program_id(0) == 0)
        def _():
            dgain_ref[...] = jnp.zeros_like(dgain_ref)

        for k in range(npc):
            du_ref[:, k * pw:(k + 1) * pw] = p_refs[k][...]
        dxn = jnp.zeros((tm, d), F32)
        for j in range(ns):
            dxn = dxn + _dot(du_ref[:, j * cs:(j + 1) * cs], w_ref[j], NT)
        hv = h_ref[...]
        r = lax.rsqrt(jnp.mean(hv * hv, axis=-1, keepdims=True) + NORM_EPS)
        xh = hv * r
        dgain_ref[...] += _rows8(dxn * xh)
        dxh = dxn * gain_ref[...]
        dh_ref[...] = dres_ref[...] + r * (dxh - xh * jnp.mean(dxh * xh, axis=-1, keepdims=True))

    tile = pl.BlockSpec((tm, d), lambda i: (i, 0))
    return pl.pallas_call(
        body, name="inproj_bwd", grid=(t // tm,),
        in_specs=[pl.BlockSpec((tm, pw), lambda i: (i, 0))] * npc + [
            tile, pl.BlockSpec((1, d), lambda i: (0, 0)), tile, pl.BlockSpec(win.shape, lambda i: (0, 0, 0))],
        out_specs=[tile, pl.BlockSpec((tm, npc * pw), lambda i: (i, 0)), pl.BlockSpec((8, d), lambda i: (0, 0))],
        out_shape=[jax.ShapeDtypeStruct((t, d), F32), jax.ShapeDtypeStruct((t, npc * pw), BF16),
                   jax.ShapeDtypeStruct((8, d), F32)],
        compiler_params=_params("arbitrary"),
    )(*pieces, h, gain, dres, win)


def _outproj_fwd(h, mix_r, mix_a, wo):
    t, d = h.shape
    hw = mix_r.shape[1]
    tm = min(512, t)

    def body(h_ref, mr_ref, ma_ref, w_ref, o_ref):
        o_ref[...] = h_ref[...] + _dot(mr_ref[...], w_ref[0:hw, :]) + _dot(ma_ref[...], w_ref[hw:2 * hw, :])

    tile = pl.BlockSpec((tm, d), lambda i: (i, 0))
    half = pl.BlockSpec((tm, hw), lambda i: (i, 0))
    return pl.pallas_call(
        body, name="outproj_fwd", grid=(t // tm,),
        in_specs=[tile, half, half, pl.BlockSpec(wo.shape, lambda i: (0, 0))],
        out_specs=tile, out_shape=jax.ShapeDtypeStruct((t, d), F32),
        compiler_params=_params("arbitrary"),
    )(h, mix_r, mix_a, wo)


def _outproj_bwd(dh, wo):
    t, d = dh.shape
    hw = wo.shape[0] // 2
    tm = min(512, t)

    def body(dh_ref, w_ref, dr_ref, da_ref):
        dhb = dh_ref[...].astype(BF16)
        dr_ref[...] = _dot(dhb, w_ref[0:hw, :], NT)
        da_ref[...] = _dot(dhb, w_ref[hw:2 * hw, :], NT)

    half = pl.BlockSpec((tm, hw), lambda i: (i, 0))
    return pl.pallas_call(
        body, name="outproj_bwd", grid=(t // tm,),
        in_specs=[pl.BlockSpec((tm, d), lambda i: (i, 0)), pl.BlockSpec(wo.shape, lambda i: (0, 0))],
        out_specs=[half, half],
        out_shape=[jax.ShapeDtypeStruct((t, hw), F32), jax.ShapeDtypeStruct((t, hw), F32)],
        compiler_params=_params("arbitrary"),
    )(dh, wo)


def _retention_tables(t):
    pos = jnp.arange(t, dtype=F32)
    inv_freq = ROPE_BASE ** (-jnp.arange(0, RET_DIM, 2, dtype=F32) / RET_DIM)
    ang = jnp.repeat(pos[:, None] * inv_freq[None, :], 2, axis=-1)
    c = RET_CHUNK
    log_g = jnp.log(1.0 - 2.0 ** (-5.0 - jnp.arange(RET_HEADS, dtype=F32)))
    idx = jnp.arange(c, dtype=F32)
    rel = idx[:, None] - idx[None, :]
    decay = jnp.where(rel >= 0, jnp.exp(log_g[:, None, None] * jnp.maximum(rel, 0.0)), 0.0)
    zeta = jnp.exp(log_g[:, None] * (c - 1 - idx)[None, :])
    xi = jnp.exp(log_g[:, None] * (idx + 1)[None, :])
    gc = jnp.exp(log_g * c)
    wide = lambda v: jnp.broadcast_to(v[:, :, None], (RET_HEADS, c, LANE))
    return (jnp.cos(ang), jnp.sin(ang), decay, wide(zeta), wide(xi),
            jnp.broadcast_to(gc[:, None, None], (RET_HEADS, c, LANE)))


def _rot(v):
    lane = lax.broadcasted_iota(jnp.int32, v.shape, 1)
    nxt = pltpu.roll(v, LANE - 1, 1)
    prv = pltpu.roll(v, 1, 1)
    return jnp.where(lane % 2 == 0, -nxt, prv)


def _ret_specs(tr, rev, nt):
    ti = (lambda i: nt - 1 - i) if rev else (lambda i: i)
    col = lambda off: pl.BlockSpec((tr, LANE), lambda h, i: (ti(i), off + h))
    tab = pl.BlockSpec((tr, LANE), lambda h, i: (ti(i), 0))
    head = pl.BlockSpec((None, RET_CHUNK, LANE), lambda h, i: (h, 0, 0))
    return col, tab, head


def _ret_fwd(u, gain, tabs):
    t = u.shape[0]
    tr = min(1024, t)
    nt = t // tr
    cos, sin, decay, zeta, xi, gc = tabs
    scale = RET_DIM ** -0.5

    def body(q_ref, k_ref, v_ref, gt_ref, cos_ref, sin_ref, gain_ref, dec_ref, zeta_ref, xi_ref, gc_ref,
             raw_ref, mix_ref, state):
        @pl.when(pl.program_id(1) == 0)
        def _():
            state[...] = jnp.zeros_like(state)

        for ci in range(tr // RET_CHUNK):
            sl = pl.ds(ci * RET_CHUNK, RET_CHUNK)
            cs, sn = cos_ref[sl, :], sin_ref[sl, :]
            q, k = q_ref[sl, :], k_ref[sl, :]
            qb = (q * cs + _rot(q) * sn).astype(BF16)
            kr = (k * cs + _rot(k) * sn) * scale
            kb = kr.astype(BF16)
            vb = v_ref[sl, :].astype(BF16)
            s = _dot(qb, kb, NT) * dec_ref[...]
            st = state[...]
            o = _dot(s.astype(BF16), vb) + _dot(qb, st.astype(BF16)) * xi_ref[...]
            state[...] = st * gc_ref[...] + _dot((kr * zeta_ref[...]).astype(BF16), vb, TN)
            raw_ref[sl, :] = o
            mu = jnp.mean(o, axis=-1, keepdims=True)
            var = jnp.mean(jnp.square(o - mu), axis=-1, keepdims=True)
            y = (o - mu) * lax.rsqrt(var + GN_EPS) * gain_ref[...]
            gt = gt_ref[sl, :]
            mix_ref[sl, :] = (y * (gt * _sigmoid(gt))).astype(BF16)

    col, tab, head = _ret_specs(tr, False, nt)
    out = pl.BlockSpec((tr, LANE), lambda h, i: (i, h))
    return pl.pallas_call(
        body, name="ret_fwd", grid=(RET_HEADS, nt),
        in_specs=[col(0), col(4), col(8), col(12), tab, tab, pl.BlockSpec((1, LANE), lambda h, i: (0, h)),
                  head, head, head, head],
        out_specs=[out, out],
        out_shape=[jax.ShapeDtypeStruct((t, RET_WIDTH), F32), jax.ShapeDtypeStruct((t, RET_WIDTH), BF16)],
        scratch_shapes=[pltpu.VMEM((RET_DIM, RET_DIM), F32)],
        compiler_params=_params("arbitrary", "arbitrary"),
    )(u, u, u, u, cos, sin, gain, decay, zeta, xi, gc)


def _ret_bwd_q(dmix, raw, u, gain, tabs):
    t = u.shape[0]
    tr = min(1024, t)
    nt = t // tr
    cos, sin, decay, zeta, xi, gc = tabs
    scale = RET_DIM ** -0.5

    def body(dm_ref, raw_ref, q_ref, k_ref, v_ref, gt_ref, cos_ref, sin_ref, gain_ref, dec_ref, zeta_ref, xi_ref, gc_ref,
             dq_ref, dgt_ref, dret_ref, dgain_ref, state):
        @pl.when(pl.program_id(1) == 0)
        def _():
            state[...] = jnp.zeros_like(state)
            dgain_ref[...] = jnp.zeros_like(dgain_ref)

        for ci in range(tr // RET_CHUNK):
            sl = pl.ds(ci * RET_CHUNK, RET_CHUNK)
            cs, sn = cos_ref[sl, :], sin_ref[sl, :]
            q, k = q_ref[sl, :], k_ref[sl, :]
            qb = (q * cs + _rot(q) * sn).astype(BF16)
            kr = (k * cs + _rot(k) * sn) * scale
            kb = kr.astype(BF16)
            vb = v_ref[sl, :].astype(BF16)
            o = raw_ref[sl, :]
            mu = jnp.mean(o, axis=-1, keepdims=True)
            var = jnp.mean(jnp.square(o - mu), axis=-1, keepdims=True)
            rs = lax.rsqrt(var + GN_EPS)
            n = (o - mu) * rs
            gt = gt_ref[sl, :]
            sig = _sigmoid(gt)
            dout = dm_ref[sl, :]
            dgt_ref[sl, :] = (dout * (n * gain_ref[...]) * (sig * (1.0 + gt * (1.0 - sig)))).astype(BF16)
            dy = dout * (gt * sig)
            dgain_ref[...] += _rows8(dy * n)
            dn = dy * gain_ref[...]
            do = rs * (dn - jnp.mean(dn, axis=-1, keepdims=True) - n * jnp.mean(dn * n, axis=-1, keepdims=True))
            dret_ref[sl, :] = do
            ds = _dot(do.astype(BF16), vb, NT) * dec_ref[...]
            st = state[...]
            dqr = _dot(ds.astype(BF16), kb) + _dot((do * xi_ref[...]).astype(BF16), st.astype(BF16), NT)
            dq_ref[sl, :] = (dqr * cs - _rot(dqr * sn)).astype(BF16)
            state[...] = st * gc_ref[...] + _dot((kr * zeta_ref[...]).astype(BF16), vb, TN)

    col, tab, head = _ret_specs(tr, False, nt)
    out = pl.BlockSpec((tr, LANE), lambda h, i: (i, h))
    return pl.pallas_call(
        body, name="ret_bwd_q", grid=(RET_HEADS, nt),
        in_specs=[out, out, col(0), col(4), col(8), col(12), tab, tab, pl.BlockSpec((1, LANE), lambda h, i: (0, h)),
                  head, head, head, head],
        out_specs=[out, out, out, pl.BlockSpec((8, LANE), lambda h, i: (0, h))],
        out_shape=[jax.ShapeDtypeStruct((t, RET_WIDTH), BF16), jax.ShapeDtypeStruct((t, RET_WIDTH), BF16),
                   jax.ShapeDtypeStruct((t, RET_WIDTH), F32), jax.ShapeDtypeStruct((8, RET_WIDTH), F32)],
        scratch_shapes=[pltpu.VMEM((RET_DIM, RET_DIM), F32)],
        compiler_params=_params("arbitrary", "arbitrary"),
    )(dmix, raw, u, u, u, u, cos, sin, gain, decay, zeta, xi, gc)


def _ret_bwd_kv(dret, u, tabs):
    t = u.shape[0]
    tr = min(1024, t)
    nt = t // tr
    cos, sin, decay, zeta, xi, gc = tabs
    scale = RET_DIM ** -0.5

    def body(do_ref, q_ref, k_ref, v_ref, cos_ref, sin_ref, dec_ref, zeta_ref, xi_ref, gc_ref, dk_ref, dv_ref, gst):
        @pl.when(pl.program_id(1) == 0)
        def _():
            gst[...] = jnp.zeros_like(gst)

        for ci in reversed(range(tr // RET_CHUNK)):
            sl = pl.ds(ci * RET_CHUNK, RET_CHUNK)
            cs, sn = cos_ref[sl, :], sin_ref[sl, :]
            q, k = q_ref[sl, :], k_ref[sl, :]
            qb = (q * cs + _rot(q) * sn).astype(BF16)
            kr = (k * cs + _rot(k) * sn) * scale
            kb = kr.astype(BF16)
            vb = v_ref[sl, :].astype(BF16)
            do = do_ref[sl, :]
            dob = do.astype(BF16)
            s = (_dot(qb, kb, NT) * dec_ref[...]).astype(BF16)
            ds = (_dot(dob, vb, NT) * dec_ref[...]).astype(BF16)
            gb = gst[...].astype(BF16)
            dv_ref[sl, :] = (_dot(s, dob, TN) + _dot((kr * zeta_ref[...]).astype(BF16), gb)).astype(BF16)
            dkr = (_dot(ds, qb, TN) + _dot(vb, gb, NT) * zeta_ref[...]) * scale
            dk_ref[sl, :] = (dkr * cs - _rot(dkr * sn)).astype(BF16)
            gst[...] = gst[...] * gc_ref[...] + _dot(qb, (do * xi_ref[...]).astype(BF16), TN)

    col, tab, head = _ret_specs(tr, True, nt)
    out = pl.BlockSpec((tr, LANE), lambda h, i: (nt - 1 - i, h))
    return pl.pallas_call(
        body, name="ret_bwd_kv", grid=(RET_HEADS, nt),
        in_specs=[out, col(0), col(4), col(8), tab, tab, head, head, head, head],
        out_specs=[out, out],
        out_shape=[jax.ShapeDtypeStruct((t, RET_WIDTH), BF16), jax.ShapeDtypeStruct((t, RET_WIDTH), BF16)],
        scratch_shapes=[pltpu.VMEM((RET_DIM, RET_DIM), F32)],
        compiler_params=_params("arbitrary", "arbitrary"),
    )(dret, u, u, u, cos, sin, decay, zeta, xi, gc)


ATT_Q_BLK, ATT_K_BLK, ATT_V_BLK = 16, 20, 24
U_BLKS = IN_COLS // LANE
PAIRS = ATT_WIDTH // LANE


def _att_tiles(t, dil):
    sub = t // dil
    tq = min(512, sub)
    return sub, tq, sub // tq, tq // ATT_BLOCK


def _att_in_specs(tq, qb, ti):
    cur = lambda off: pl.BlockSpec((tq, LANE), lambda g, p, i: (ti(i), g * U_BLKS + off + p))
    prev = lambda off: pl.BlockSpec((ATT_BLOCK, LANE), lambda g, p, i: (jnp.maximum(ti(i) * qb - 1, 0), g * U_BLKS + off + p))
    return [cur(ATT_Q_BLK), cur(ATT_K_BLK), prev(ATT_K_BLK), cur(ATT_V_BLK), prev(ATT_V_BLK)]


def _band_mask():
    row = lax.broadcasted_iota(jnp.int32, (ATT_BLOCK, 2 * ATT_BLOCK), 0)
    col = lax.broadcasted_iota(jnp.int32, (ATT_BLOCK, 2 * ATT_BLOCK), 1)
    dist = row + ATT_BLOCK - col
    return (dist >= 0) & (dist <= ATT_BLOCK), col >= ATT_BLOCK


def _att_fwd(u, dil):
    t = u.shape[0]
    sub, tq, nq, qb = _att_tiles(t, dil)
    ud = u.reshape(sub, dil * IN_COLS)

    def body(q_ref, kc_ref, kp_ref, vc_ref, vp_ref, o_ref, l_ref, kx, vx):
        tile = pl.program_id(2)
        kx[0:ATT_BLOCK, :] = kp_ref[...].astype(BF16)
        kx[ATT_BLOCK:, :] = kc_ref[...].astype(BF16)
        vx[0:ATT_BLOCK, :] = vp_ref[...].astype(BF16)
        vx[ATT_BLOCK:, :] = vc_ref[...].astype(BF16)
        band, cur_cols = _band_mask()
        for b in range(qb):
            rows = slice(b * ATT_BLOCK, (b + 1) * ATT_BLOCK)
            mask = band if b > 0 else band & (cur_cols | (tile > 0))
            qv = (q_ref[rows, :] * (ATT_DIM ** -0.5)).astype(BF16)
            kv = kx[b * ATT_BLOCK:(b + 2) * ATT_BLOCK, :]
            vv = vx[b * ATT_BLOCK:(b + 2) * ATT_BLOCK, :]
            outs, lses = [], []
            for e in range(LANE // ATT_DIM):
                hs = slice(e * ATT_DIM, (e + 1) * ATT_DIM)
                s = jnp.where(mask, _dot(qv[:, hs], kv[:, hs], NT), -1e30)
                m = jnp.max(s, axis=-1, keepdims=True)
                ex = jnp.exp(s - m)
                den = jnp.sum(ex, axis=-1, keepdims=True)
                outs.append(_dot((ex / den).astype(BF16), vv[:, hs]))
                lses.append(jnp.broadcast_to(m + jnp.log(den), (ATT_BLOCK, ATT_DIM)))
            o_ref[rows, :] = jnp.concatenate(outs, axis=1)
            l_ref[rows, :] = jnp.concatenate(lses, axis=1)

    out = pl.BlockSpec((tq, LANE), lambda g, p, i: (i, g * PAIRS + p))
    o, l = pl.pallas_call(
        body, name=f"att_fwd_d{dil}", grid=(dil, PAIRS, nq),
        in_specs=_att_in_specs(tq, qb, lambda i: i),
        out_specs=[out, out],
        out_shape=[jax.ShapeDtypeStruct((sub, dil * ATT_WIDTH), F32)] * 2,
        scratch_shapes=[pltpu.VMEM((tq + ATT_BLOCK, LANE), BF16)] * 2,
        compiler_params=_params("arbitrary", "arbitrary", "arbitrary"),
    )(ud, ud, ud, ud, ud)
    return o.reshape(t, ATT_WIDTH), l.reshape(t, ATT_WIDTH)


def _att_combine(outs, lses):
    t, w = outs[0].shape
    tm = min(512, t)
    nb = len(outs)

    def body(*refs):
        o_refs, l_refs = refs[:nb], refs[nb:2 * nb]
        mix_ref, att_ref, lse_ref = refs[2 * nb:]
        ls = [r[...] for r in l_refs]
        m = functools.reduce(jnp.maximum, ls)
        ws = [jnp.exp(l - m) for l in ls]
        den = functools.reduce(jnp.add, ws)
        att = functools.reduce(jnp.add, [(wt / den) * r[...] for wt, r in zip(ws, o_refs)])
        att_ref[...] = att
        mix_ref[...] = att.astype(BF16)
        lse_ref[...] = m + jnp.log(den)

    tile = pl.BlockSpec((tm, w), lambda i: (i, 0))
    return pl.pallas_call(
        body, name="att_combine", grid=(t // tm,),
        in_specs=[tile] * (2 * nb), out_specs=[tile, tile, tile],
        out_shape=[jax.ShapeDtypeStruct((t, w), BF16), jax.ShapeDtypeStruct((t, w), F32), jax.ShapeDtypeStruct((t, w), F32)],
        compiler_params=_params("arbitrary"),
    )(*outs, *lses)


def _att_bwd(u, datt, att, lse, dil):
    t = u.shape[0]
    sub, tq, nq, qb = _att_tiles(t, dil)
    ud = u.reshape(sub, dil * IN_COLS)
    view = lambda a: a.reshape(sub, dil * ATT_WIDTH)
    scale = ATT_DIM ** -0.5

    def body(q_ref, kc_ref, kp_ref, vc_ref, vp_ref, da_ref, at_ref, l_ref, dq_ref, dk_ref, dv_ref, kx, vx, dkx, dvx, ck, cv):
        step = pl.program_id(2)
        tile = nq - 1 - step

        @pl.when(step == 0)
        def _():
            ck[...] = jnp.zeros_like(ck)
            cv[...] = jnp.zeros_like(cv)

        kx[0:ATT_BLOCK, :] = kp_ref[...].astype(BF16)
        kx[ATT_BLOCK:, :] = kc_ref[...].astype(BF16)
        vx[0:ATT_BLOCK, :] = vp_ref[...].astype(BF16)
        vx[ATT_BLOCK:, :] = vc_ref[...].astype(BF16)
        dkx[...] = jnp.zeros_like(dkx)
        dvx[...] = jnp.zeros_like(dvx)
        band, cur_cols = _band_mask()
        for b in range(qb):
            rows = slice(b * ATT_BLOCK, (b + 1) * ATT_BLOCK)
            keys = slice(b * ATT_BLOCK, (b + 2) * ATT_BLOCK)
            mask = band if b > 0 else band & (cur_cols | (tile > 0))
            qv = (q_ref[rows, :] * scale).astype(BF16)
            kv, vv = kx[keys, :], vx[keys, :]
            dav = da_ref[rows, :]
            dab = dav.astype(BF16)
            prod = dav * at_ref[rows, :]
            lv = l_ref[rows, :]
            dqs, dks, dvs = [], [], []
            for e in range(LANE // ATT_DIM):
                hs = slice(e * ATT_DIM, (e + 1) * ATT_DIM)
                delta = jnp.sum(prod[:, hs], axis=-1, keepdims=True)
                s = _dot(qv[:, hs], kv[:, hs], NT)
                p = jnp.where(mask, jnp.exp(s - lv[:, e * ATT_DIM:e * ATT_DIM + 1]), 0.0)
                ds = (p * (_dot(dab[:, hs], vv[:, hs], NT) - delta)).astype(BF16)
                dqs.append(_dot(ds, kv[:, hs]) * scale)
                dks.append(_dot(ds, qv[:, hs], TN))
                dvs.append(_dot(p.astype(BF16), dab[:, hs], TN))
            dq_ref[rows, :] = jnp.concatenate(dqs, axis=1)
            dkx[keys, :] += jnp.concatenate(dks, axis=1)
            dvx[keys, :] += jnp.concatenate(dvs, axis=1)
        dkx[tq:, :] += ck[...]
        dvx[tq:, :] += cv[...]
        dk_ref[...] = dkx[ATT_BLOCK:, :]
        dv_ref[...] = dvx[ATT_BLOCK:, :]
        ck[...] = dkx[0:ATT_BLOCK, :]
        cv[...] = dvx[0:ATT_BLOCK, :]

    ti = lambda i: nq - 1 - i
    out = pl.BlockSpec((tq, LANE), lambda g, p, i: (ti(i), g * PAIRS + p))
    shape = jax.ShapeDtypeStruct((sub, dil * ATT_WIDTH), F32)
    res = pl.pallas_call(
        body, name=f"att_bwd_d{dil}", grid=(dil, PAIRS, nq),
        in_specs=_att_in_specs(tq, qb, ti) + [out, out, out],
        out_specs=[out, out, out], out_shape=[shape] * 3,
        scratch_shapes=[pltpu.VMEM((tq + ATT_BLOCK, LANE), BF16)] * 2 + [pltpu.VMEM((tq + ATT_BLOCK, LANE), F32)] * 2
        + [pltpu.VMEM((ATT_BLOCK, LANE), F32)] * 2,
        compiler_params=_params("arbitrary", "arbitrary", "arbitrary"),
    )(ud, ud, ud, ud, ud, view(datt), view(att), view(lse))
    return [r.reshape(t, ATT_WIDTH) for r in res]


def _sum_cast(groups):
    t, w = groups[0][0].shape
    tm = min(512, t)
    sizes = [len(g) for g in groups]
    flat = [a for g in groups for a in g]

    def body(*refs):
        ins, outs = refs[:len(flat)], refs[len(flat):]
        k = 0
        for n, o_ref in zip(sizes, outs):
            o_ref[...] = functools.reduce(jnp.add, [r[...] for r in ins[k:k + n]]).astype(BF16)
            k += n

    tile = pl.BlockSpec((tm, w), lambda i: (i, 0))
    return pl.pallas_call(
        body, name="att_bwd_sum", grid=(t // tm,),
        in_specs=[tile] * len(flat), out_specs=[tile] * len(groups),
        out_shape=[jax.ShapeDtypeStruct((t, w), BF16)] * len(groups),
        compiler_params=_params("arbitrary"),
    )(*flat)


def _loss_bwd(h, gain, target):
    t, d = h.shape
    tm = min(512, t)

    def body(h_ref, gain_ref, tg_ref, loss_ref, dh_ref, dgain_ref):
        @pl.when(pl.program_id(0) == 0)
        def _():
            loss_ref[...] = jnp.zeros_like(loss_ref)
            dgain_ref[...] = jnp.zeros_like(dgain_ref)

        hv = h_ref[...]
        r = lax.rsqrt(jnp.mean(hv * hv, axis=-1, keepdims=True) + NORM_EPS)
        xh = hv * r
        err = xh * gain_ref[...] - tg_ref[...]
        sq = _rows8(jnp.square(err))
        loss_ref[...] += 0.5 * functools.reduce(jnp.add, [sq[:, k * LANE:(k + 1) * LANE] for k in range(d // LANE)]) / d
        dy = err / d
        dgain_ref[...] += _rows8(dy * xh)
        dxh = dy * gain_ref[...]
        dh_ref[...] = r * (dxh - xh * jnp.mean(dxh * xh, axis=-1, keepdims=True))

    tile = pl.BlockSpec((tm, d), lambda i: (i, 0))
    return pl.pallas_call(
        body, name="loss_bwd", grid=(t // tm,),
        in_specs=[tile, pl.BlockSpec((1, d), lambda i: (0, 0)), tile],
        out_specs=[pl.BlockSpec((8, LANE), lambda i: (0, 0)), tile, pl.BlockSpec((8, d), lambda i: (0, 0))],
        out_shape=[jax.ShapeDtypeStruct((8, LANE), F32), jax.ShapeDtypeStruct((t, d), F32), jax.ShapeDtypeStruct((8, d), F32)],
        compiler_params=_params("arbitrary"),
    )(h, gain, target)


def _local_step(x, target, gains, w):
    t = x.shape[0]
    g_ffn1, g_mix, g_ret, g_ffn2, g_fin = gains
    wg1, wu1, wd1, win, wo, wg2, wu2, wd2 = w
    wo2 = wo.reshape(wo.shape[0] * wo.shape[1], wo.shape[2])
    tabs = _retention_tables(t)

    h1, xn1, ga1, ua1 = _ffn_fwd(x, g_ffn1, wg1, wu1, wd1, "ffn1_fwd")
    xnm, u = _inproj_fwd(h1, g_mix, win)
    raw, mix_r = _ret_fwd(u, g_ret, tabs)
    branches = [_att_fwd(u, dil) for dil in DILATIONS]
    mix_a, att, lse = _att_combine([b[0] for b in branches], [b[1] for b in branches])
    h2 = _outproj_fwd(h1, mix_r, mix_a, wo2)
    h3, xn2, ga2, ua2 = _ffn_fwd(h2, g_ffn2, wg2, wu2, wd2, "ffn2_fwd")
    loss_p, dh3, dg_fin = _loss_bwd(h3, g_fin, target)

    dh2, dga2, dua2, dg_ffn2 = _ffn_bwd_data(dh3, h2, g_ffn2, ga2, ua2, wg2, wu2, wd2, "ffn2_bwd")
    dwg2, dwu2, dwd2 = _ffn_wgrad(xn2, dh3, ga2, ua2, dga2, dua2, "ffn2_wgrad")
    dmix_r, dmix_a = _outproj_bwd(dh2, wo2)
    hw = RET_WIDTH // (wo.shape[1])
    dwo = jnp.concatenate([_tn_matmul(mix_r, dh2, dh2.shape[1], "wo_grad_r").reshape(hw, wo.shape[1], wo.shape[2]),
                           _tn_matmul(mix_a, dh2, dh2.shape[1], "wo_grad_a").reshape(hw, wo.shape[1], wo.shape[2])])
    dq_r, dgt_r, dret, dg_ret = _ret_bwd_q(dmix_r, raw, u, g_ret, tabs)
    dk_r, dv_r = _ret_bwd_kv(dret, u, tabs)
    parts = [_att_bwd(u, dmix_a, att, lse, dil) for dil in DILATIONS]
    dq_a, dk_a, dv_a = _sum_cast([[p[k] for p in parts] for k in range(3)])
    dh1, du, dg_mix = _inproj_bwd([dq_r, dk_r, dv_r, dgt_r, dq_a, dk_a, dv_a], h1, g_mix, dh2, win)
    dwin = _tn_matmul(xnm, du, win.shape[2], "win_grad")
    dx, dga1, dua1, dg_ffn1 = _ffn_bwd_data(dh1, x, g_ffn1, ga1, ua1, wg1, wu1, wd1, "ffn1_bwd")
    dwg1, dwu1, dwd1 = _ffn_wgrad(xn1, dh1, ga1, ua1, dga1, dua1, "ffn1_wgrad")
    return (loss_p, dx, [dwg1, dwu1, dwd1, dwin, dwo, dwg2, dwu2, dwd2],
            [dg_ffn1, dg_mix, dg_ret, dg_ffn2, dg_fin])


N_DEV = 8
GAIN_ROWS = 8


def _place():
    x, y, c = lax.axis_index("x"), lax.axis_index("y"), lax.axis_index("c")
    chips = [(1 - x, y), (x, 1 - y), (1 - x, 1 - y)]
    return x, y, c, chips


def _hbm_specs(n):
    return [pl.BlockSpec(memory_space=pl.ANY)] * n


def _all_gather_weights(shards):
    na = len(shards)

    def body(*refs):
        ins, outs = refs[:na], refs[na:2 * na]
        loc_sem, send_sem, recv_sem, fsend_sem, frecv_sem = refs[2 * na:]
        x, y, c, chips = _place()
        me = 2 * x + y

        def half(a, idx, which):
            hr = ins[a].shape[0] // 2
            return outs[a].at[idx, pl.ds(which * hr, hr)]

        def src_half(a):
            hr = ins[a].shape[0] // 2
            return ins[a].at[pl.ds(c * hr, hr)]

        def ici(a, j, idx, src=None):
            px, py = chips[j]
            return pltpu.make_async_remote_copy(
                src_ref=half(a, idx, c) if src is None else src, dst_ref=half(a, idx, c),
                send_sem=send_sem.at[a, j], recv_sem=recv_sem.at[a, j], device_id=(px, py, c), device_id_type=MESH)

        def d2d(a, j, idx, which):
            return pltpu.make_async_remote_copy(
                src_ref=half(a, idx, which), dst_ref=half(a, idx, which),
                send_sem=fsend_sem.at[a, j], recv_sem=frecv_sem.at[a, j], device_id=(x, y, 1 - c), device_id_type=MESH)

        own = [pltpu.make_async_copy(ins[a], outs[a].at[me], loc_sem.at[a]) for a in range(na)]
        for cp in own:
            cp.start()
        sends = [ici(a, j, me, src=src_half(a)) for a in range(na) for j in range(3)]
        for cp in sends:
            cp.start()
        passed = []
        for a in range(na):
            for j, (px, py) in enumerate(chips):
                ici(a, j, 2 * px + py).wait_recv()
                cp = d2d(a, j, 2 * px + py, c)
                cp.start()
                passed.append(cp)
        for a in range(na):
            for j, (px, py) in enumerate(chips):
                d2d(a, j, 2 * px + py, 1 - c).wait_recv()
        for cp in sends + passed:
            cp.wait_send()
        for cp in own:
            cp.wait()

    return pl.pallas_call(
        body, name="all_gather_weights",
        in_specs=_hbm_specs(na), out_specs=_hbm_specs(na),
        out_shape=[jax.ShapeDtypeStruct((N_SHARD,) + s.shape, s.dtype) for s in shards],
        scratch_shapes=[pltpu.SemaphoreType.DMA((na,))] + [pltpu.SemaphoreType.DMA((na, 3))] * 4,
    )(*shards)


def _pair_exchange(grads):
    na = len(grads)

    def body(*refs):
        ins, outs = refs[:na], refs[na:2 * na]
        send_sem, recv_sem = refs[2 * na:]
        x, y, c, _ = _place()
        copies = []
        for a in range(na):
            hr = ins[a].shape[1] // 2
            copies.append(pltpu.make_async_remote_copy(
                src_ref=ins[a].at[:, pl.ds((1 - c) * hr, hr)], dst_ref=outs[a],
                send_sem=send_sem.at[a], recv_sem=recv_sem.at[a], device_id=(x, y, 1 - c), device_id_type=MESH))
        for cp in copies:
            cp.start()
        for cp in copies:
            cp.wait()

    return pl.pallas_call(
        body, name="grad_pair_exchange",
        in_specs=_hbm_specs(na), out_specs=_hbm_specs(na),
        out_shape=[jax.ShapeDtypeStruct((g.shape[0], g.shape[1] // 2, g.shape[2]), g.dtype) for g in grads],
        scratch_shapes=[pltpu.SemaphoreType.DMA((na,))] * 2,
    )(*grads)


def _chip_exchange(sums):
    na = len(sums)

    def body(*refs):
        ins, outs = refs[:na], refs[na:2 * na]
        send_sem, recv_sem = refs[2 * na:]
        x, y, c, chips = _place()
        copies = []
        for a in range(na):
            for j, (px, py) in enumerate(chips):
                copies.append(pltpu.make_async_remote_copy(
                    src_ref=ins[a].at[2 * px + py], dst_ref=outs[a].at[j],
                    send_sem=send_sem.at[a, j], recv_sem=recv_sem.at[a, j], device_id=(px, py, c), device_id_type=MESH))
        for cp in copies:
            cp.start()
        for cp in copies:
            cp.wait()

    return pl.pallas_call(
        body, name="grad_chip_exchange",
        in_specs=_hbm_specs(na), out_specs=_hbm_specs(na),
        out_shape=[jax.ShapeDtypeStruct((3,) + s.shape[1:], s.dtype) for s in sums],
        scratch_shapes=[pltpu.SemaphoreType.DMA((na, 3))] * 2,
    )(*sums)


def _finish_exchange(halves, gpack):
    na = len(halves)

    def body(*refs):
        ins, g_in = refs[:na], refs[na]
        outs, g_out = refs[na + 1:2 * na + 1], refs[2 * na + 1]
        loc_sem, send_sem, recv_sem, gsend_sem, grecv_sem = refs[2 * na + 2:]
        x, y, c, _ = _place()
        dev = 4 * x + 2 * y + c
        local, remote = [], []
        for a in range(na):
            hr = ins[a].shape[0]
            rows = outs[a].at[pl.ds(c * hr, hr)]
            local.append(pltpu.make_async_copy(ins[a], rows, loc_sem.at[a]))
            remote.append(pltpu.make_async_remote_copy(
                src_ref=ins[a], dst_ref=rows, send_sem=send_sem.at[a], recv_sem=recv_sem.at[a],
                device_id=(x, y, 1 - c), device_id_type=MESH))
        local.append(pltpu.make_async_copy(g_in, g_out.at[dev], loc_sem.at[na]))
        for k in range(N_DEV - 1):
            bx, by, bc = (k + 1) // 4, ((k + 1) // 2) % 2, (k + 1) % 2
            peer = (jnp.bitwise_xor(x, bx), jnp.bitwise_xor(y, by), jnp.bitwise_xor(c, bc))
            remote.append(pltpu.make_async_remote_copy(
                src_ref=g_in, dst_ref=g_out.at[dev], send_sem=gsend_sem.at[k], recv_sem=grecv_sem.at[k],
                device_id=peer, device_id_type=MESH))
        for cp in local + remote:
            cp.start()
        for a in range(na):
            hr = ins[a].shape[0]
            rows = outs[a].at[pl.ds((1 - c) * hr, hr)]
            pltpu.make_async_remote_copy(src_ref=rows, dst_ref=rows, send_sem=send_sem.at[a], recv_sem=recv_sem.at[a],
                                         device_id=(x, y, 1 - c), device_id_type=MESH).wait_recv()
        for k in range(N_DEV - 1):
            slot = g_out.at[jnp.bitwise_xor(dev, k + 1)]
            pltpu.make_async_remote_copy(src_ref=slot, dst_ref=slot, send_sem=gsend_sem.at[k], recv_sem=grecv_sem.at[k],
                                         device_id=(x, y, 1 - c), device_id_type=MESH).wait_recv()
        for cp in remote:
            cp.wait_send()
        for cp in local:
            cp.wait()

    return pl.pallas_call(
        body, name="grad_finish_exchange",
        in_specs=_hbm_specs(na + 1), out_specs=_hbm_specs(na + 1),
        out_shape=[jax.ShapeDtypeStruct((2 * h.shape[0], h.shape[1]), h.dtype) for h in halves]
        + [jax.ShapeDtypeStruct((N_DEV,) + gpack.shape, gpack.dtype)],
        scratch_shapes=[pltpu.SemaphoreType.DMA((na + 1,)), pltpu.SemaphoreType.DMA((na,)), pltpu.SemaphoreType.DMA((na,)),
                        pltpu.SemaphoreType.DMA((N_DEV - 1,)), pltpu.SemaphoreType.DMA((N_DEV - 1,))],
    )(*halves, gpack)


def _pair_sum(place, grad, got, name):
    ns, r, cols = grad.shape
    hr = r // 2

    def body(place_ref, g_ref, r_ref, o_ref):
        o_ref[...] = (g_ref[...] + r_ref[...]).astype(BF16)

    return pl.pallas_call(
        body, name=name,
        grid_spec=pltpu.PrefetchScalarGridSpec(
            num_scalar_prefetch=1, grid=(ns,),
            in_specs=[pl.BlockSpec((None, hr, cols), lambda s, pr: (s, pr[1], 0)),
                      pl.BlockSpec((None, hr, cols), lambda s, pr: (s, 0, 0))],
            out_specs=pl.BlockSpec((None, hr, cols), lambda s, pr: (s, 0, 0))),
        out_shape=jax.ShapeDtypeStruct((ns, hr, cols), BF16),
        compiler_params=_params("arbitrary"),
    )(place, grad, got)


def _chip_sum(place, grad, got, others, name):
    ns, r, cols = grad.shape
    hr = r // 2
    nb = 2
    tr = hr // nb

    def body(place_ref, g_ref, r_ref, o3_ref, o_ref):
        acc = g_ref[...] + r_ref[...]
        for j in range(3):
            acc = acc + o3_ref[j].astype(F32)
        o_ref[...] = acc

    return pl.pallas_call(
        body, name=name,
        grid_spec=pltpu.PrefetchScalarGridSpec(
            num_scalar_prefetch=1, grid=(nb,),
            in_specs=[pl.BlockSpec((None, tr, cols), lambda i, pr: (pr[0], pr[1] * nb + i, 0)),
                      pl.BlockSpec((None, tr, cols), lambda i, pr: (pr[0], i, 0)),
                      pl.BlockSpec((3, tr, cols), lambda i, pr: (0, i, 0))],
            out_specs=pl.BlockSpec((tr, cols), lambda i, pr: (i, 0))),
        out_shape=jax.ShapeDtypeStruct((hr, cols), F32),
        compiler_params=_params("arbitrary"),
    )(place, grad, got, others)


def _pack_gains(parts, d):
    def body(*refs):
        ins, o_ref = refs[:-1], refs[-1]
        o_ref[...] = jnp.zeros_like(o_ref)
        for k, r in enumerate(ins):
            o_ref[k:k + 1, 0:r.shape[1]] = jnp.sum(r[...], axis=0, keepdims=True)

    return pl.pallas_call(
        body, name="pack_gains", out_shape=jax.ShapeDtypeStruct((GAIN_ROWS, d), F32),
    )(*parts)


def _adamw_math(w, g, m, v):
    m = ADAM_B1 * m + (1.0 - ADAM_B1) * g
    v = ADAM_B2 * v + (1.0 - ADAM_B2) * jnp.square(g)
    m_hat = m / (1.0 - ADAM_B1 ** ADAM_STEP)
    v_hat = v / (1.0 - ADAM_B2 ** ADAM_STEP)
    return -ADAM_LR * (m_hat / (jnp.sqrt(v_hat) + ADAM_EPS) + ADAM_WD * w), m, v


def _adamw(w, g, m, v, name):
    r, cols = w.shape
    tr = r // 4 if (r // 4) % 8 == 0 else r

    def body(w_ref, g_ref, m_ref, v_ref, d_ref, nm_ref, nv_ref):
        d_ref[...], nm_ref[...], nv_ref[...] = _adamw_math(w_ref[...], g_ref[...], m_ref[...], v_ref[...])

    tile = pl.BlockSpec((tr, cols), lambda i: (i, 0))
    return pl.pallas_call(
        body, name=name, grid=(r // tr,), in_specs=[tile] * 4, out_specs=[tile] * 3,
        out_shape=[jax.ShapeDtypeStruct((r, cols), F32)] * 3,
        compiler_params=_params("arbitrary"),
    )(w, g, m, v)


def _adamw_gain(gall, row, w, m, v, name):
    n = w.shape[1]

    def body(ga_ref, w_ref, m_ref, v_ref, g_ref, d_ref, nm_ref, nv_ref):
        g = ga_ref[0, row:row + 1, 0:n]
        for k in range(1, N_DEV):
            g = g + ga_ref[k, row:row + 1, 0:n]
        g_ref[...] = g
        d_ref[...], nm_ref[...], nv_ref[...] = _adamw_math(w_ref[...], g, m_ref[...], v_ref[...])

    return pl.pallas_call(
        body, name=name, out_shape=[jax.ShapeDtypeStruct((1, n), F32)] * 4,
    )(gall, w, m, v)


def kernel(x, norm_ffn1, ffn1_w_gate, ffn1_w_up, ffn1_w_down, norm_mix, w_in, ret_norm_gain, w_out, norm_ffn2, ffn2_w_gate, ffn2_w_up, ffn2_w_down, norm_final, loss_target, m_norm_ffn1, m_ffn1_w_gate, m_ffn1_w_up, m_ffn1_w_down, m_norm_mix, m_w_in, m_ret_norm_gain, m_w_out, m_norm_ffn2, m_ffn2_w_gate, m_ffn2_w_up, m_ffn2_w_down, m_norm_final, v_norm_ffn1, v_ffn1_w_gate, v_ffn1_w_up, v_ffn1_w_down, v_norm_mix, v_w_in, v_ret_norm_gain, v_w_out, v_norm_ffn2, v_ffn2_w_gate, v_ffn2_w_up, v_ffn2_w_down, v_norm_final):
    d = x.shape[-1]
    mats = [ffn1_w_gate, ffn1_w_up, ffn1_w_down, w_in, w_out, ffn2_w_gate, ffn2_w_up, ffn2_w_down]
    mats_m = [m_ffn1_w_gate, m_ffn1_w_up, m_ffn1_w_down, m_w_in, m_w_out, m_ffn2_w_gate, m_ffn2_w_up, m_ffn2_w_down]
    mats_v = [v_ffn1_w_gate, v_ffn1_w_up, v_ffn1_w_down, v_w_in, v_w_out, v_ffn2_w_gate, v_ffn2_w_up, v_ffn2_w_down]
    mat_names = ["ffn1_w_gate", "ffn1_w_up", "ffn1_w_down", "w_in", "w_out", "ffn2_w_gate", "ffn2_w_up", "ffn2_w_down"]
    gains = [norm_ffn1, norm_mix, ret_norm_gain, norm_ffn2, norm_final.reshape(1, d)]
    gains_m = [m_norm_ffn1, m_norm_mix, m_ret_norm_gain, m_norm_ffn2, m_norm_final.reshape(1, d)]
    gains_v = [v_norm_ffn1, v_norm_mix, v_ret_norm_gain, v_norm_ffn2, v_norm_final.reshape(1, d)]
    gain_names = ["norm_ffn1", "norm_mix", "ret_norm_gain", "norm_ffn2", "norm_final"]

    shards = [w[0] for w in mats]
    full = _all_gather_weights([s.astype(BF16) for s in shards])
    loss_p, dx, grads, gain_parts = _local_step(x[0], loss_target[0], gains, full)

    place = jnp.stack([2 * lax.axis_index("x") + lax.axis_index("y"), lax.axis_index("c")]).astype(jnp.int32)
    got = _pair_exchange(grads)
    pair = [_pair_sum(place, g, r, f"pair_sum_{n}") for g, r, n in zip(grads, got, mat_names)]
    others = _chip_exchange(pair)
    halves = [_chip_sum(place, g, r, o, f"chip_sum_{n}") for g, r, o, n in zip(grads, got, others, mat_names)]
    *shard_grads, gall = _finish_exchange(halves, _pack_gains(gain_parts, d))

    out_g, out_d, out_m, out_v = {}, {}, {}, {}
    for n, w, g, m, v in zip(mat_names, shards, shard_grads, mats_m, mats_v):
        dl, nm, nv = _adamw(w, g, m[0], v[0], f"adamw_{n}")
        out_g[n], out_d[n], out_m[n], out_v[n] = g[None], dl[None], nm[None], nv[None]
    for row, (n, w, m, v) in enumerate(zip(gain_names, gains, gains_m, gains_v)):
        res = _adamw_gain(gall, row, w, m, v, f"adamw_{n}")
        shape = (d,) if n == "norm_final" else w.shape
        out_g[n], out_d[n], out_m[n], out_v[n] = [r.reshape(shape) for r in res]

    loss = lax.psum(jnp.sum(loss_p), ("x", "y", "c"))
    order = ["norm_ffn1", "ffn1_w_gate", "ffn1_w_up", "ffn1_w_down", "norm_mix", "w_in", "ret_norm_gain", "w_out",
             "norm_ffn2", "ffn2_w_gate", "ffn2_w_up", "ffn2_w_down", "norm_final"]
    return (loss, dx[None], *[out_g[n] for n in order], *[out_d[n] for n in order],
            *[out_m[n] for n in order], *[out_v[n] for n in order])
```

```python
import functools
import math

import jax
import jax.numpy as jnp
from jax import lax
from jax.experimental import pallas as pl
from jax.experimental.pallas import tpu as pltpu

F32 = jnp.float32
BF16 = jnp.bfloat16
MESH = pl.DeviceIdType.MESH

NORM_EPS = 1e-6
GN_EPS = 1e-6
ROPE_BASE = 10000.0
RET_HEADS = 4
RET_DIM = 128
RET_WIDTH = 512
RET_CHUNK = 128
ATT_HEADS = 8
ATT_DIM = 64
ATT_WIDTH = 512
ATT_BLOCK = 128
DILATIONS = (1, 4, 16)
IN_COLS = 4 * RET_WIDTH + 3 * ATT_WIDTH
LANE = 128
N_SHARD = 4
ADAM_LR, ADAM_B1, ADAM_B2, ADAM_EPS, ADAM_WD, ADAM_STEP = 0.001, 0.9, 0.999, 1e-08, 0.01, 10

V7X_VMEM_BYTES = 64 * 1024 * 1024
VMEM_LIMIT = V7X_VMEM_BYTES - 8 * 1024 * 1024

NT = (((1,), (1,)), ((), ()))
TN = (((0,), (0,)), ((), ()))


def _params(*sem):
    return pltpu.CompilerParams(dimension_semantics=sem, vmem_limit_bytes=VMEM_LIMIT)


def _dot(a, b, dims=None):
    if dims is None:
        return jnp.dot(a, b, preferred_element_type=F32)
    return lax.dot_general(a, b, dims, preferred_element_type=F32)


def _sigmoid(x):
    return 1.0 / (1.0 + jnp.exp(-x))


def _load_weights(pairs, sems):
    copies = [pltpu.make_async_copy(src, dst, sems.at[k]) for k, (src, dst) in enumerate(pairs)]
    for cp in copies:
        cp.start()
    for cp in copies:
        cp.wait()


def _rows8(v):
    r, c = v.shape
    return v.reshape(r // 8, 8, c).sum(axis=0)


def _ffn_fwd(x, gain, wg, wu, wd, name):
    t, d = x.shape
    ns, _, fs = wg.shape
    tm = min(256, t)

    def body(x_ref, gain_ref, wg_hbm, wu_hbm, wd_hbm, h_ref, xn_ref, g_ref, u_ref, wg_v, wu_v, wd_v, sems):
        @pl.when(pl.program_id(0) == 0)
        def _():
            _load_weights([(wg_hbm, wg_v), (wu_hbm, wu_v), (wd_hbm, wd_v)], sems)

        xv = x_ref[...]
        r = lax.rsqrt(jnp.mean(xv * xv, axis=-1, keepdims=True) + NORM_EPS)
        xn = (xv * r * gain_ref[...]).astype(BF16)
        xn_ref[...] = xn
        acc = jnp.zeros((tm, d), F32)
        for j in range(ns):
            g = _dot(xn, wg_v[j])
            u = _dot(xn, wu_v[j])
            g_ref[j] = g.astype(BF16)
            u_ref[j] = u.astype(BF16)
            a = (g * _sigmoid(g) * u).astype(BF16)
            acc = acc + _dot(a, wd_v[j])
        h_ref[...] = xv + 0.5 * acc

    hbm = pl.BlockSpec(memory_space=pl.ANY)
    return pl.pallas_call(
        body, name=name, grid=(t // tm,),
        in_specs=[pl.BlockSpec((tm, d), lambda i: (i, 0)), pl.BlockSpec((1, d), lambda i: (0, 0)), hbm, hbm, hbm],
        out_specs=[pl.BlockSpec((tm, d), lambda i: (i, 0)), pl.BlockSpec((tm, d), lambda i: (i, 0)),
                   pl.BlockSpec((ns, tm, fs), lambda i: (0, i, 0)), pl.BlockSpec((ns, tm, fs), lambda i: (0, i, 0))],
        out_shape=[jax.ShapeDtypeStruct((t, d), F32), jax.ShapeDtypeStruct((t, d), BF16),
                   jax.ShapeDtypeStruct((ns, t, fs), BF16), jax.ShapeDtypeStruct((ns, t, fs), BF16)],
        scratch_shapes=[pltpu.VMEM(wg.shape, BF16), pltpu.VMEM(wu.shape, BF16), pltpu.VMEM(wd.shape, BF16),
                        pltpu.SemaphoreType.DMA((3,))],
        compiler_params=_params("arbitrary"),
    )(x, gain, wg, wu, wd)


def _ffn_bwd_data(dy, x, gain, g, u, wg, wu, wd, name):
    t, d = x.shape
    ns, _, fs = wg.shape
    tm = min(256, t)

    def body(dy_ref, x_ref, gain_ref, g_ref, u_ref, wg_hbm, wu_hbm, wd_hbm, dx_ref, dg_ref, du_ref, dgain_ref,
             wg_v, wu_v, wd_v, sems):
        @pl.when(pl.program_id(0) == 0)
        def _():
            _load_weights([(wg_hbm, wg_v), (wu_hbm, wu_v), (wd_hbm, wd_v)], sems)
            dgain_ref[...] = jnp.zeros_like(dgain_ref)

        dyv = dy_ref[...]
        dyh = (0.5 * dyv).astype(BF16)
        dxn = jnp.zeros((tm, d), F32)
        for j in range(ns):
            da = _dot(dyh, wd_v[j], NT)
            gj = g_ref[j].astype(F32)
            uj = u_ref[j].astype(F32)
            sig = _sigmoid(gj)
            dgj = (da * uj * (sig * (1.0 + gj * (1.0 - sig)))).astype(BF16)
            duj = (da * (gj * sig)).astype(BF16)
            dg_ref[j] = dgj
            du_ref[j] = duj
            dxn = dxn + _dot(dgj, wg_v[j], NT) + _dot(duj, wu_v[j], NT)
        xv = x_ref[...]
        r = lax.rsqrt(jnp.mean(xv * xv, axis=-1, keepdims=True) + NORM_EPS)
        xh = xv * r
        dgain_ref[...] += _rows8(dxn * xh)
        dxh = dxn * gain_ref[...]
        dx_ref[...] = dyv + r * (dxh - xh * jnp.mean(dxh * xh, axis=-1, keepdims=True))

    hbm = pl.BlockSpec(memory_space=pl.ANY)
    tile = pl.BlockSpec((tm, d), lambda i: (i, 0))
    hid = pl.BlockSpec((ns, tm, fs), lambda i: (0, i, 0))
    return pl.pallas_call(
        body, name=name, grid=(t // tm,),
        in_specs=[tile, tile, pl.BlockSpec((1, d), lambda i: (0, 0)), hid, hid, hbm, hbm, hbm],
        out_specs=[tile, hid, hid, pl.BlockSpec((8, d), lambda i: (0, 0))],
        out_shape=[jax.ShapeDtypeStruct((t, d), F32), jax.ShapeDtypeStruct((ns, t, fs), BF16),
                   jax.ShapeDtypeStruct((ns, t, fs), BF16), jax.ShapeDtypeStruct((8, d), F32)],
        scratch_shapes=[pltpu.VMEM(wg.shape, BF16), pltpu.VMEM(wu.shape, BF16), pltpu.VMEM(wd.shape, BF16),
                        pltpu.SemaphoreType.DMA((3,))],
        compiler_params=_params("arbitrary"),
    )(dy, x, gain, g, u, wg, wu, wd)


def _ffn_wgrad(xn, dy, g, u, dg, du, name):
    t, d = xn.shape
    ns, _, fs = g.shape
    tk = min(512, t)

    def body(xn_ref, dy_ref, g_ref, u_ref, dg_ref, du_ref, dwg_ref, dwu_ref, dwd_ref):
        @pl.when(pl.program_id(1) == 0)
        def _():
            dwg_ref[...] = jnp.zeros_like(dwg_ref)
            dwu_ref[...] = jnp.zeros_like(dwu_ref)
            dwd_ref[...] = jnp.zeros_like(dwd_ref)

        xnv = xn_ref[...]
        dwg_ref[...] += _dot(xnv, dg_ref[...], TN)
        dwu_ref[...] += _dot(xnv, du_ref[...], TN)
        gj = g_ref[...].astype(F32)
        a = (gj * _sigmoid(gj) * u_ref[...].astype(F32)).astype(BF16)
        dwd_ref[...] += _dot(a, (0.5 * dy_ref[...]).astype(BF16), TN)

    tile = pl.BlockSpec((tk, d), lambda j, k: (k, 0))
    hid = pl.BlockSpec((None, tk, fs), lambda j, k: (j, k, 0))
    return pl.pallas_call(
        body, name=name, grid=(ns, t // tk),
        in_specs=[tile, tile, hid, hid, hid, hid],
        out_specs=[pl.BlockSpec((None, d, fs), lambda j, k: (j, 0, 0)), pl.BlockSpec((None, d, fs), lambda j, k: (j, 0, 0)),
                   pl.BlockSpec((None, fs, d), lambda j, k: (j, 0, 0))],
        out_shape=[jax.ShapeDtypeStruct((ns, d, fs), F32), jax.ShapeDtypeStruct((ns, d, fs), F32),
                   jax.ShapeDtypeStruct((ns, fs, d), F32)],
        compiler_params=_params("arbitrary", "arbitrary"),
    )(xn, dy, g, u, dg, du)


def _tn_matmul(a, b, bn, name):
    t, m = a.shape
    n = b.shape[1]
    tk = min(512, t)

    def body(a_ref, b_ref, o_ref):
        @pl.when(pl.program_id(1) == 0)
        def _():
            o_ref[...] = jnp.zeros_like(o_ref)

        o_ref[...] += _dot(a_ref[...].astype(BF16), b_ref[...].astype(BF16), TN)

    return pl.pallas_call(
        body, name=name, grid=(n // bn, t // tk),
        in_specs=[pl.BlockSpec((tk, m), lambda j, k: (k, 0)), pl.BlockSpec((tk, bn), lambda j, k: (k, j))],
        out_specs=pl.BlockSpec((None, m, bn), lambda j, k: (j, 0, 0)),
        out_shape=jax.ShapeDtypeStruct((n // bn, m, bn), F32),
        compiler_params=_params("arbitrary", "arbitrary"),
    )(a, b)


def _chunk_scratch(tm, w):
    return pltpu.VMEM((w // LANE, tm, LANE), F32)


def _regroup_store(cbuf, out_ref, dil):
    n = out_ref.shape[1]
    for g in range(dil):
        for k in range(cbuf.shape[0]):
            rows = cbuf[k] if dil == 1 else cbuf[k, pl.ds(g, n, stride=dil), :]
            out_ref[g, :, k * LANE:(k + 1) * LANE] = rows.astype(out_ref.dtype)


def _natural_rows(ref, dil, cbuf):
    if dil == 1:
        return ref[0]
    n = ref.shape[1]
    for g in range(dil):
        for k in range(cbuf.shape[0]):
            cbuf[k, pl.ds(g, n, stride=dil), :] = ref[g, :, k * LANE:(k + 1) * LANE]
    return jnp.concatenate([cbuf[k] for k in range(cbuf.shape[0])], axis=1)


def _inproj_fwd(h, gain, win):
    t, d = h.shape
    ns, _, cs = win.shape
    tm = min(512, t)
    rw, aw = 4 * RET_WIDTH, 3 * ATT_WIDTH

    def body(h_ref, gain_ref, w_ref, xn_ref, ur_ref, *rest):
        a_refs, abuf = rest[:-1], rest[-1]
        hv = h_ref[...]
        r = lax.rsqrt(jnp.mean(hv * hv, axis=-1, keepdims=True) + NORM_EPS)
        xn = (hv * r * gain_ref[...]).astype(BF16)
        xn_ref[...] = xn
        for j in range(ns):
            res = _dot(xn, w_ref[j])
            for k in range(cs // LANE):
                chunk = j * (cs // LANE) + k
                piece = res[:, k * LANE:(k + 1) * LANE]
                if chunk < rw // LANE:
                    ur_ref[:, chunk * LANE:(chunk + 1) * LANE] = piece
                else:
                    abuf[chunk - rw // LANE] = piece
        for dil, a_ref in zip(DILATIONS, a_refs):
            _regroup_store(abuf, a_ref, dil)

    return pl.pallas_call(
        body, name="inproj_fwd", grid=(t // tm,),
        in_specs=[pl.BlockSpec((tm, d), lambda i: (i, 0)), pl.BlockSpec((1, d), lambda i: (0, 0)),
                  pl.BlockSpec(win.shape, lambda i: (0, 0, 0))],
        out_specs=[pl.BlockSpec((tm, d), lambda i: (i, 0)), pl.BlockSpec((tm, rw), lambda i: (i, 0))]
        + [pl.BlockSpec((dil, tm // dil, aw), lambda i: (0, i, 0)) for dil in DILATIONS],
        out_shape=[jax.ShapeDtypeStruct((t, d), BF16), jax.ShapeDtypeStruct((t, rw), F32)]
        + [jax.ShapeDtypeStruct((dil, t // dil, aw), BF16) for dil in DILATIONS],
        scratch_shapes=[_chunk_scratch(tm, aw)],
        compiler_params=_params("arbitrary"),
    )(h, gain, win)


def _inproj_bwd(pieces, h, gain, dres, win):
    t, d = h.shape
    ns, _, cs = win.shape
    pw = pieces[0].shape[1]
    tm = min(512, t)
    npc = len(pieces)

    def body(*refs):
        p_refs = refs[:npc]
        h_ref, gain_ref, dres_ref, w_ref, dh_ref, du_ref, dgain_ref = refs[npc:]

        @pl.when(pl.program_id(0) == 0)
        def _():
            dgain_ref[...] = jnp.zeros_like(dgain_ref)

        for k in range(npc):
            du_ref[:, k * pw:(k + 1) * pw] = p_refs[k][...]
        dxn = jnp.zeros((tm, d), F32)
        for j in range(ns):
            dxn = dxn + _dot(du_ref[:, j * cs:(j + 1) * cs], w_ref[j], NT)
        hv = h_ref[...]
        r = lax.rsqrt(jnp.mean(hv * hv, axis=-1, keepdims=True) + NORM_EPS)
        xh = hv * r
        dgain_ref[...] += _rows8(dxn * xh)
        dxh = dxn * gain_ref[...]
        dh_ref[...] = dres_ref[...] + r * (dxh - xh * jnp.mean(dxh * xh, axis=-1, keepdims=True))

    tile = pl.BlockSpec((tm, d), lambda i: (i, 0))
    return pl.pallas_call(
        body, name="inproj_bwd", grid=(t // tm,),
        in_specs=[pl.BlockSpec((tm, pw), lambda i: (i, 0))] * npc + [
            tile, pl.BlockSpec((1, d), lambda i: (0, 0)), tile, pl.BlockSpec(win.shape, lambda i: (0, 0, 0))],
        out_specs=[tile, pl.BlockSpec((tm, npc * pw), lambda i: (i, 0)), pl.BlockSpec((8, d), lambda i: (0, 0))],
        out_shape=[jax.ShapeDtypeStruct((t, d), F32), jax.ShapeDtypeStruct((t, npc * pw), BF16),
                   jax.ShapeDtypeStruct((8, d), F32)],
        compiler_params=_params("arbitrary"),
    )(*pieces, h, gain, dres, win)


def _outproj_fwd(h, mix_r, mix_a, wo):
    t, d = h.shape
    hw = mix_r.shape[1]
    tm = min(512, t)

    def body(h_ref, mr_ref, ma_ref, w_ref, o_ref):
        o_ref[...] = h_ref[...] + _dot(mr_ref[...], w_ref[0:hw, :]) + _dot(ma_ref[...], w_ref[hw:2 * hw, :])

    tile = pl.BlockSpec((tm, d), lambda i: (i, 0))
    half = pl.BlockSpec((tm, hw), lambda i: (i, 0))
    return pl.pallas_call(
        body, name="outproj_fwd", grid=(t // tm,),
        in_specs=[tile, half, half, pl.BlockSpec(wo.shape, lambda i: (0, 0))],
        out_specs=tile, out_shape=jax.ShapeDtypeStruct((t, d), F32),
        compiler_params=_params("arbitrary"),
    )(h, mix_r, mix_a, wo)


def _outproj_bwd(dh, wo):
    t, d = dh.shape
    hw = wo.shape[0] // 2
    tm = min(512, t)

    def body(dh_ref, w_ref, dr_ref, da_ref):
        dhb = dh_ref[...].astype(BF16)
        dr_ref[...] = _dot(dhb, w_ref[0:hw, :], NT)
        da_ref[...] = _dot(dhb, w_ref[hw:2 * hw, :], NT)

    half = pl.BlockSpec((tm, hw), lambda i: (i, 0))
    return pl.pallas_call(
        body, name="outproj_bwd", grid=(t // tm,),
        in_specs=[pl.BlockSpec((tm, d), lambda i: (i, 0)), pl.BlockSpec(wo.shape, lambda i: (0, 0))],
        out_specs=[half, half],
        out_shape=[jax.ShapeDtypeStruct((t, hw), F32), jax.ShapeDtypeStruct((t, hw), F32)],
        compiler_params=_params("arbitrary"),
    )(dh, wo)


def _retention_tables(t):
    pos = jnp.arange(t, dtype=F32)
    inv_freq = ROPE_BASE ** (-jnp.arange(0, RET_DIM, 2, dtype=F32) / RET_DIM)
    ang = jnp.repeat(pos[:, None] * inv_freq[None, :], 2, axis=-1)
    c = RET_CHUNK
    log_g = jnp.log(1.0 - 2.0 ** (-5.0 - jnp.arange(RET_HEADS, dtype=F32)))
    idx = jnp.arange(c, dtype=F32)
    rel = idx[:, None] - idx[None, :]
    decay = jnp.where(rel >= 0, jnp.exp(log_g[:, None, None] * jnp.maximum(rel, 0.0)), 0.0)
    zeta = jnp.exp(log_g[:, None] * (c - 1 - idx)[None, :])
    xi = jnp.exp(log_g[:, None] * (idx + 1)[None, :])
    gc = jnp.exp(log_g * c)
    wide = lambda v: jnp.broadcast_to(v[:, :, None], (RET_HEADS, c, LANE))
    return (jnp.cos(ang), jnp.sin(ang), decay, wide(zeta), wide(xi),
            jnp.broadcast_to(gc[:, None, None], (RET_HEADS, c, LANE)))


def _rot(v):
    lane = lax.broadcasted_iota(jnp.int32, v.shape, 1)
    nxt = pltpu.roll(v, LANE - 1, 1)
    prv = pltpu.roll(v, 1, 1)
    return jnp.where(lane % 2 == 0, -nxt, prv)


def _ret_specs(tr, rev, nt):
    ti = (lambda i: nt - 1 - i) if rev else (lambda i: i)
    col = lambda off: pl.BlockSpec((tr, LANE), lambda h, i: (ti(i), off + h))
    tab = pl.BlockSpec((tr, LANE), lambda h, i: (ti(i), 0))
    head = pl.BlockSpec((None, RET_CHUNK, LANE), lambda h, i: (h, 0, 0))
    return col, tab, head


def _ret_fwd(u, gain, tabs):
    t = u.shape[0]
    tr = min(1024, t)
    nt = t // tr
    cos, sin, decay, zeta, xi, gc = tabs
    scale = RET_DIM ** -0.5

    def body(q_ref, k_ref, v_ref, gt_ref, cos_ref, sin_ref, gain_ref, dec_ref, zeta_ref, xi_ref, gc_ref,
             raw_ref, mix_ref, state):
        @pl.when(pl.program_id(1) == 0)
        def _():
            state[...] = jnp.zeros_like(state)

        for ci in range(tr // RET_CHUNK):
            sl = pl.ds(ci * RET_CHUNK, RET_CHUNK)
            cs, sn = cos_ref[sl, :], sin_ref[sl, :]
            q, k = q_ref[sl, :], k_ref[sl, :]
            qb = (q * cs + _rot(q) * sn).astype(BF16)
            kr = (k * cs + _rot(k) * sn) * scale
            kb = kr.astype(BF16)
            vb = v_ref[sl, :].astype(BF16)
            s = _dot(qb, kb, NT) * dec_ref[...]
            st = state[...]
            o = _dot(s.astype(BF16), vb) + _dot(qb, st.astype(BF16)) * xi_ref[...]
            state[...] = st * gc_ref[...] + _dot((kr * zeta_ref[...]).astype(BF16), vb, TN)
            raw_ref[sl, :] = o
            mu = jnp.mean(o, axis=-1, keepdims=True)
            var = jnp.mean(jnp.square(o - mu), axis=-1, keepdims=True)
            y = (o - mu) * lax.rsqrt(var + GN_EPS) * gain_ref[...]
            gt = gt_ref[sl, :]
            mix_ref[sl, :] = (y * (gt * _sigmoid(gt))).astype(BF16)

    col, tab, head = _ret_specs(tr, False, nt)
    out = pl.BlockSpec((tr, LANE), lambda h, i: (i, h))
    return pl.pallas_call(
        body, name="ret_fwd", grid=(RET_HEADS, nt),
        in_specs=[col(0), col(4), col(8), col(12), tab, tab, pl.BlockSpec((1, LANE), lambda h, i: (0, h)),
                  head, head, head, head],
        out_specs=[out, out],
        out_shape=[jax.ShapeDtypeStruct((t, RET_WIDTH), F32), jax.ShapeDtypeStruct((t, RET_WIDTH), BF16)],
        scratch_shapes=[pltpu.VMEM((RET_DIM, RET_DIM), F32)],
        compiler_params=_params("arbitrary", "arbitrary"),
    )(u, u, u, u, cos, sin, gain, decay, zeta, xi, gc)


def _ret_bwd_q(dmix, raw, u, gain, tabs):
    t = u.shape[0]
    tr = min(1024, t)
    nt = t // tr
    cos, sin, decay, zeta, xi, gc = tabs
    scale = RET_DIM ** -0.5

    def body(dm_ref, raw_ref, q_ref, k_ref, v_ref, gt_ref, cos_ref, sin_ref, gain_ref, dec_ref, zeta_ref, xi_ref, gc_ref,
             dq_ref, dgt_ref, dret_ref, dgain_ref, state):
        @pl.when(pl.program_id(1) == 0)
        def _():
            state[...] = jnp.zeros_like(state)
            dgain_ref[...] = jnp.zeros_like(dgain_ref)

        for ci in range(tr // RET_CHUNK):
            sl = pl.ds(ci * RET_CHUNK, RET_CHUNK)
            cs, sn = cos_ref[sl, :], sin_ref[sl, :]
            q, k = q_ref[sl, :], k_ref[sl, :]
            qb = (q * cs + _rot(q) * sn).astype(BF16)
            kr = (k * cs + _rot(k) * sn) * scale
            kb = kr.astype(BF16)
            vb = v_ref[sl, :].astype(BF16)
            o = raw_ref[sl, :]
            mu = jnp.mean(o, axis=-1, keepdims=True)
            var = jnp.mean(jnp.square(o - mu), axis=-1, keepdims=True)
            rs = lax.rsqrt(var + GN_EPS)
            n = (o - mu) * rs
            gt = gt_ref[sl, :]
            sig = _sigmoid(gt)
            dout = dm_ref[sl, :]
            dgt_ref[sl, :] = (dout * (n * gain_ref[...]) * (sig * (1.0 + gt * (1.0 - sig)))).astype(BF16)
            dy = dout * (gt * sig)
            dgain_ref[...] += _rows8(dy * n)
            dn = dy * gain_ref[...]
            do = rs * (dn - jnp.mean(dn, axis=-1, keepdims=True) - n * jnp.mean(dn * n, axis=-1, keepdims=True))
            dret_ref[sl, :] = do
            ds = _dot(do.astype(BF16), vb, NT) * dec_ref[...]
            st = state[...]
            dqr = _dot(ds.astype(BF16), kb) + _dot((do * xi_ref[...]).astype(BF16), st.astype(BF16), NT)
            dq_ref[sl, :] = (dqr * cs - _rot(dqr * sn)).astype(BF16)
            state[...] = st * gc_ref[...] + _dot((kr * zeta_ref[...]).astype(BF16), vb, TN)

    col, tab, head = _ret_specs(tr, False, nt)
    out = pl.BlockSpec((tr, LANE), lambda h, i: (i, h))
    return pl.pallas_call(
        body, name="ret_bwd_q", grid=(RET_HEADS, nt),
        in_specs=[out, out, col(0), col(4), col(8), col(12), tab, tab, pl.BlockSpec((1, LANE), lambda h, i: (0, h)),
                  head, head, head, head],
        out_specs=[out, out, out, pl.BlockSpec((8, LANE), lambda h, i: (0, h))],
        out_shape=[jax.ShapeDtypeStruct((t, RET_WIDTH), BF16), jax.ShapeDtypeStruct((t, RET_WIDTH), BF16),
                   jax.ShapeDtypeStruct((t, RET_WIDTH), F32), jax.ShapeDtypeStruct((8, RET_WIDTH), F32)],
        scratch_shapes=[pltpu.VMEM((RET_DIM, RET_DIM), F32)],
        compiler_params=_params("arbitrary", "arbitrary"),
    )(dmix, raw, u, u, u, u, cos, sin, gain, decay, zeta, xi, gc)


def _ret_bwd_kv(dret, u, tabs):
    t = u.shape[0]
    tr = min(1024, t)
    nt = t // tr
    cos, sin, decay, zeta, xi, gc = tabs
    scale = RET_DIM ** -0.5

    def body(do_ref, q_ref, k_ref, v_ref, cos_ref, sin_ref, dec_ref, zeta_ref, xi_ref, gc_ref, dk_ref, dv_ref, gst):
        @pl.when(pl.program_id(1) == 0)
        def _():
            gst[...] = jnp.zeros_like(gst)

        for ci in reversed(range(tr // RET_CHUNK)):
            sl = pl.ds(ci * RET_CHUNK, RET_CHUNK)
            cs, sn = cos_ref[sl, :], sin_ref[sl, :]
            q, k = q_ref[sl, :], k_ref[sl, :]
            qb = (q * cs + _rot(q) * sn).astype(BF16)
            kr = (k * cs + _rot(k) * sn) * scale
            kb = kr.astype(BF16)
            vb = v_ref[sl, :].astype(BF16)
            do = do_ref[sl, :]
            dob = do.astype(BF16)
            s = (_dot(qb, kb, NT) * dec_ref[...]).astype(BF16)
            ds = (_dot(dob, vb, NT) * dec_ref[...]).astype(BF16)
            gb = gst[...].astype(BF16)
            dv_ref[sl, :] = (_dot(s, dob, TN) + _dot((kr * zeta_ref[...]).astype(BF16), gb)).astype(BF16)
            dkr = (_dot(ds, qb, TN) + _dot(vb, gb, NT) * zeta_ref[...]) * scale
            dk_ref[sl, :] = (dkr * cs - _rot(dkr * sn)).astype(BF16)
            gst[...] = gst[...] * gc_ref[...] + _dot(qb, (do * xi_ref[...]).astype(BF16), TN)

    col, tab, head = _ret_specs(tr, True, nt)
    out = pl.BlockSpec((tr, LANE), lambda h, i: (nt - 1 - i, h))
    return pl.pallas_call(
        body, name="ret_bwd_kv", grid=(RET_HEADS, nt),
        in_specs=[out, col(0), col(4), col(8), tab, tab, head, head, head, head],
        out_specs=[out, out],
        out_shape=[jax.ShapeDtypeStruct((t, RET_WIDTH), BF16), jax.ShapeDtypeStruct((t, RET_WIDTH), BF16)],
        scratch_shapes=[pltpu.VMEM((RET_DIM, RET_DIM), F32)],
        compiler_params=_params("arbitrary", "arbitrary"),
    )(dret, u, u, u, cos, sin, decay, zeta, xi, gc)


PAIRS = ATT_WIDTH // LANE
ATT_Q_BLK, ATT_K_BLK, ATT_V_BLK = 0, PAIRS, 2 * PAIRS
STAT_LANES = ATT_DIM // 2


def _att_tiles(t, dil):
    sub = t // dil
    tq = min(512, sub)
    return sub, tq, sub // tq, tq // ATT_BLOCK


def _att_in_specs(tq, qb, ti):
    cur = lambda off: pl.BlockSpec((None, tq, LANE), lambda g, p, i: (g, ti(i), off + p))
    prev = lambda off: pl.BlockSpec((None, ATT_BLOCK, LANE), lambda g, p, i: (g, jnp.maximum(ti(i) * qb - 1, 0), off + p))
    return [cur(ATT_Q_BLK), cur(ATT_K_BLK), prev(ATT_K_BLK), cur(ATT_V_BLK), prev(ATT_V_BLK)]


def _band_mask():
    row = lax.broadcasted_iota(jnp.int32, (ATT_BLOCK, 2 * ATT_BLOCK), 0)
    col = lax.broadcasted_iota(jnp.int32, (ATT_BLOCK, 2 * ATT_BLOCK), 1)
    dist = row + ATT_BLOCK - col
    return (dist >= 0) & (dist <= ATT_BLOCK), col >= ATT_BLOCK


def _att_fwd(ua, dil):
    sub = ua.shape[1]
    _, tq, nq, qb = _att_tiles(sub * dil, dil)

    def body(q_ref, kc_ref, kp_ref, vc_ref, vp_ref, o_ref, l_ref, kx, vx):
        tile = pl.program_id(2)
        kx[0:ATT_BLOCK, :] = kp_ref[...]
        kx[ATT_BLOCK:, :] = kc_ref[...]
        vx[0:ATT_BLOCK, :] = vp_ref[...]
        vx[ATT_BLOCK:, :] = vc_ref[...]
        band, cur_cols = _band_mask()
        for b in range(qb):
            rows = slice(b * ATT_BLOCK, (b + 1) * ATT_BLOCK)
            mask = band if b > 0 else band & (cur_cols | (tile > 0))
            qv = q_ref[rows, :] * jnp.asarray(ATT_DIM ** -0.5, BF16)
            kv = kx[b * ATT_BLOCK:(b + 2) * ATT_BLOCK, :]
            vv = vx[b * ATT_BLOCK:(b + 2) * ATT_BLOCK, :]
            outs, lses = [], []
            for e in range(LANE // ATT_DIM):
                hs = slice(e * ATT_DIM, (e + 1) * ATT_DIM)
                s = jnp.where(mask, _dot(qv[:, hs], kv[:, hs], NT), -1e30)
                m = jnp.max(s, axis=-1, keepdims=True)
                ex = jnp.exp(s - m)
                den = jnp.sum(ex, axis=-1, keepdims=True)
                outs.append(_dot((ex / den).astype(BF16), vv[:, hs]))
                lses.append(jnp.broadcast_to(m + jnp.log(den), (ATT_BLOCK, ATT_DIM)))
            o_ref[rows, :] = jnp.concatenate(outs, axis=1)
            l_ref[rows, :] = jnp.concatenate(lses, axis=1)

    out = pl.BlockSpec((None, tq, LANE), lambda g, p, i: (g, i, p))
    return pl.pallas_call(
        body, name=f"att_fwd_d{dil}", grid=(dil, PAIRS, nq),
        in_specs=_att_in_specs(tq, qb, lambda i: i),
        out_specs=[out, out],
        out_shape=[jax.ShapeDtypeStruct((dil, sub, ATT_WIDTH), F32)] * 2,
        scratch_shapes=[pltpu.VMEM((tq + ATT_BLOCK, LANE), BF16)] * 2,
        compiler_params=_params("arbitrary", "arbitrary", "arbitrary"),
    )(ua, ua, ua, ua, ua)


def _regrouped_spec(tm, dil, w):
    return pl.BlockSpec((dil, tm // dil, w), lambda i: (0, i, 0))


def _att_combine(outs, lses, t):
    w = ATT_WIDTH
    tm = min(512, t)
    nb = len(outs)

    def body(*refs):
        o_refs, l_refs = refs[:nb], refs[nb:2 * nb]
        mix_ref, att_ref, lse_ref, buf = refs[2 * nb:]
        ls = [_natural_rows(r, dil, buf) for r, dil in zip(l_refs, DILATIONS)]
        m = functools.reduce(jnp.maximum, ls)
        ws = [jnp.exp(l - m) for l in ls]
        den = functools.reduce(jnp.add, ws)
        att = functools.reduce(jnp.add, [(wt / den) * _natural_rows(r, dil, buf) for wt, r, dil in zip(ws, o_refs, DILATIONS)])
        att_ref[...] = att
        mix_ref[...] = att.astype(BF16)
        lse_ref[...] = m + jnp.log(den)

    tile = pl.BlockSpec((tm, w), lambda i: (i, 0))
    regrouped = [_regrouped_spec(tm, dil, w) for dil in DILATIONS]
    return pl.pallas_call(
        body, name="att_combine", grid=(t // tm,),
        in_specs=regrouped * 2, out_specs=[tile, tile, tile],
        out_shape=[jax.ShapeDtypeStruct((t, w), BF16), jax.ShapeDtypeStruct((t, w), F32), jax.ShapeDtypeStruct((t, w), F32)],
        scratch_shapes=[_chunk_scratch(tm, w)],
        compiler_params=_params("arbitrary"),
    )(*outs, *lses)


def _att_bwd_prep(datt, att, lse):
    t, w = datt.shape
    tm = min(512, t)

    def body(da_ref, at_ref, l_ref, *rest):
        outs, dbuf, sbuf = rest[:-2], rest[-2], rest[-1]
        dav = da_ref[...]
        prod = dav * at_ref[...]
        lane = lax.broadcasted_iota(jnp.int32, (tm, LANE), 1)
        for k in range(w // LANE):
            cols = slice(k * LANE, (k + 1) * LANE)
            dbuf[k] = dav[:, cols]
            delta = jnp.concatenate(
                [jnp.broadcast_to(jnp.sum(prod[:, k * LANE + e * ATT_DIM:k * LANE + (e + 1) * ATT_DIM], axis=-1, keepdims=True),
                                  (tm, ATT_DIM)) for e in range(LANE // ATT_DIM)], axis=1)
            sbuf[k] = jnp.where(lane % ATT_DIM < STAT_LANES, l_ref[:, cols], delta)
        for k, dil in enumerate(DILATIONS):
            _regroup_store(dbuf, outs[2 * k], dil)
            _regroup_store(sbuf, outs[2 * k + 1], dil)

    tile = pl.BlockSpec((tm, w), lambda i: (i, 0))
    res = pl.pallas_call(
        body, name="att_bwd_prep", grid=(t // tm,),
        in_specs=[tile] * 3,
        out_specs=[_regrouped_spec(tm, dil, w) for dil in DILATIONS for _ in range(2)],
        out_shape=[jax.ShapeDtypeStruct((dil, t // dil, w), dt) for dil in DILATIONS for dt in (BF16, F32)],
        scratch_shapes=[_chunk_scratch(tm, w)] * 2,
        compiler_params=_params("arbitrary"),
    )(datt, att, lse)
    return [(res[2 * k], res[2 * k + 1]) for k in range(len(DILATIONS))]


def _att_bwd(ua, da, stat, dil):
    sub = ua.shape[1]
    _, tq, nq, qb = _att_tiles(sub * dil, dil)
    scale = ATT_DIM ** -0.5

    def body(q_ref, kc_ref, kp_ref, vc_ref, vp_ref, da_ref, st_ref, dq_ref, dk_ref, dv_ref, kx, vx, dkx, dvx, ck, cv):
        step = pl.program_id(2)
        tile = nq - 1 - step

        @pl.when(step == 0)
        def _():
            ck[...] = jnp.zeros_like(ck)
            cv[...] = jnp.zeros_like(cv)

        kx[0:ATT_BLOCK, :] = kp_ref[...]
        kx[ATT_BLOCK:, :] = kc_ref[...]
        vx[0:ATT_BLOCK, :] = vp_ref[...]
        vx[ATT_BLOCK:, :] = vc_ref[...]
        dkx[...] = jnp.zeros_like(dkx)
        dvx[...] = jnp.zeros_like(dvx)
        band, cur_cols = _band_mask()
        for b in range(qb):
            rows = slice(b * ATT_BLOCK, (b + 1) * ATT_BLOCK)
            keys = slice(b * ATT_BLOCK, (b + 2) * ATT_BLOCK)
            mask = band if b > 0 else band & (cur_cols | (tile > 0))
            qv = q_ref[rows, :] * jnp.asarray(scale, BF16)
            kv, vv = kx[keys, :], vx[keys, :]
            dab = da_ref[rows, :]
            lv = st_ref[rows, :]
            dqs, dks, dvs = [], [], []
            for e in range(LANE // ATT_DIM):
                hs = slice(e * ATT_DIM, (e + 1) * ATT_DIM)
                delta = lv[:, e * ATT_DIM + STAT_LANES:e * ATT_DIM + STAT_LANES + 1]
                s = _dot(qv[:, hs], kv[:, hs], NT)
                p = jnp.where(mask, jnp.exp(s - lv[:, e * ATT_DIM:e * ATT_DIM + 1]), 0.0)
                ds = (p * (_dot(dab[:, hs], vv[:, hs], NT) - delta)).astype(BF16)
                dqs.append(_dot(ds, kv[:, hs]) * scale)
                dks.append(_dot(ds, qv[:, hs], TN))
                dvs.append(_dot(p.astype(BF16), dab[:, hs], TN))
            dq_ref[rows, :] = jnp.concatenate(dqs, axis=1)
            dkx[keys, :] += jnp.concatenate(dks, axis=1)
            dvx[keys, :] += jnp.concatenate(dvs, axis=1)
        dkx[tq:, :] += ck[...]
        dvx[tq:, :] += cv[...]
        dk_ref[...] = dkx[ATT_BLOCK:, :]
        dv_ref[...] = dvx[ATT_BLOCK:, :]
        ck[...] = dkx[0:ATT_BLOCK, :]
        cv[...] = dvx[0:ATT_BLOCK, :]

    ti = lambda i: nq - 1 - i
    out = pl.BlockSpec((None, tq, LANE), lambda g, p, i: (g, ti(i), p))
    shape = jax.ShapeDtypeStruct((dil, sub, ATT_WIDTH), F32)
    return pl.pallas_call(
        body, name=f"att_bwd_d{dil}", grid=(dil, PAIRS, nq),
        in_specs=_att_in_specs(tq, qb, ti) + [out, out],
        out_specs=[out, out, out], out_shape=[shape] * 3,
        scratch_shapes=[pltpu.VMEM((tq + ATT_BLOCK, LANE), BF16)] * 2 + [pltpu.VMEM((tq + ATT_BLOCK, LANE), F32)] * 2
        + [pltpu.VMEM((ATT_BLOCK, LANE), F32)] * 2,
        compiler_params=_params("arbitrary", "arbitrary", "arbitrary"),
    )(ua, ua, ua, ua, ua, da, stat)


def _att_bwd_sum(parts, t):
    w = ATT_WIDTH
    tm = min(512, t)
    nk = len(parts[0])

    def body(*refs):
        ins, outs, buf = refs[:-nk - 1], refs[-nk - 1:-1], refs[-1]
        for k in range(nk):
            acc = None
            for b, dil in enumerate(DILATIONS):
                rows = _natural_rows(ins[b * nk + k], dil, buf)
                acc = rows if acc is None else acc + rows
            outs[k][...] = acc.astype(BF16)

    tile = pl.BlockSpec((tm, w), lambda i: (i, 0))
    return pl.pallas_call(
        body, name="att_bwd_sum", grid=(t // tm,),
        in_specs=[_regrouped_spec(tm, dil, w) for dil in DILATIONS for _ in range(nk)], out_specs=[tile] * nk,
        out_shape=[jax.ShapeDtypeStruct((t, w), BF16)] * nk,
        scratch_shapes=[_chunk_scratch(tm, w)],
        compiler_params=_params("arbitrary"),
    )(*[a for p in parts for a in p])


def _loss_bwd(h, gain, target):
    t, d = h.shape
    tm = min(512, t)

    def body(h_ref, gain_ref, tg_ref, loss_ref, dh_ref, dgain_ref):
        @pl.when(pl.program_id(0) == 0)
        def _():
            loss_ref[...] = jnp.zeros_like(loss_ref)
            dgain_ref[...] = jnp.zeros_like(dgain_ref)

        hv = h_ref[...]
        r = lax.rsqrt(jnp.mean(hv * hv, axis=-1, keepdims=True) + NORM_EPS)
        xh = hv * r
        err = xh * gain_ref[...] - tg_ref[...]
        sq = _rows8(jnp.square(err))
        loss_ref[...] += 0.5 * functools.reduce(jnp.add, [sq[:, k * LANE:(k + 1) * LANE] for k in range(d // LANE)]) / d
        dy = err / d
        dgain_ref[...] += _rows8(dy * xh)
        dxh = dy * gain_ref[...]
        dh_ref[...] = r * (dxh - xh * jnp.mean(dxh * xh, axis=-1, keepdims=True))

    tile = pl.BlockSpec((tm, d), lambda i: (i, 0))
    return pl.pallas_call(
        body, name="loss_bwd", grid=(t // tm,),
        in_specs=[tile, pl.BlockSpec((1, d), lambda i: (0, 0)), tile],
        out_specs=[pl.BlockSpec((8, LANE), lambda i: (0, 0)), tile, pl.BlockSpec((8, d), lambda i: (0, 0))],
        out_shape=[jax.ShapeDtypeStruct((8, LANE), F32), jax.ShapeDtypeStruct((t, d), F32), jax.ShapeDtypeStruct((8, d), F32)],
        compiler_params=_params("arbitrary"),
    )(h, gain, target)


def _local_step(x, target, gains, w):
    t = x.shape[0]
    g_ffn1, g_mix, g_ret, g_ffn2, g_fin = gains
    wg1, wu1, wd1, win, wo, wg2, wu2, wd2 = w
    wo2 = wo.reshape(wo.shape[0] * wo.shape[1], wo.shape[2])
    tabs = _retention_tables(t)

    h1, xn1, ga1, ua1 = _ffn_fwd(x, g_ffn1, wg1, wu1, wd1, "ffn1_fwd")
    xnm, u, *uas = _inproj_fwd(h1, g_mix, win)
    raw, mix_r = _ret_fwd(u, g_ret, tabs)
    branches = [_att_fwd(ua, dil) for ua, dil in zip(uas, DILATIONS)]
    mix_a, att, lse = _att_combine([b[0] for b in branches], [b[1] for b in branches], t)
    h2 = _outproj_fwd(h1, mix_r, mix_a, wo2)
    h3, xn2, ga2, ua2 = _ffn_fwd(h2, g_ffn2, wg2, wu2, wd2, "ffn2_fwd")
    loss_p, dh3, dg_fin = _loss_bwd(h3, g_fin, target)

    dh2, dga2, dua2, dg_ffn2 = _ffn_bwd_data(dh3, h2, g_ffn2, ga2, ua2, wg2, wu2, wd2, "ffn2_bwd")
    dwg2, dwu2, dwd2 = _ffn_wgrad(xn2, dh3, ga2, ua2, dga2, dua2, "ffn2_wgrad")
    dmix_r, dmix_a = _outproj_bwd(dh2, wo2)
    hw = RET_WIDTH // (wo.shape[1])
    dwo = jnp.concatenate([_tn_matmul(mix_r, dh2, dh2.shape[1], "wo_grad_r").reshape(hw, wo.shape[1], wo.shape[2]),
                           _tn_matmul(mix_a, dh2, dh2.shape[1], "wo_grad_a").reshape(hw, wo.shape[1], wo.shape[2])])
    dq_r, dgt_r, dret, dg_ret = _ret_bwd_q(dmix_r, raw, u, g_ret, tabs)
    dk_r, dv_r = _ret_bwd_kv(dret, u, tabs)
    prep = _att_bwd_prep(dmix_a, att, lse)
    parts = [_att_bwd(ua, da, stat, dil) for ua, (da, stat), dil in zip(uas, prep, DILATIONS)]
    dq_a, dk_a, dv_a = _att_bwd_sum(parts, t)
    dh1, du, dg_mix = _inproj_bwd([dq_r, dk_r, dv_r, dgt_r, dq_a, dk_a, dv_a], h1, g_mix, dh2, win)
    dwin = _tn_matmul(xnm, du, win.shape[2], "win_grad")
    dx, dga1, dua1, dg_ffn1 = _ffn_bwd_data(dh1, x, g_ffn1, ga1, ua1, wg1, wu1, wd1, "ffn1_bwd")
    dwg1, dwu1, dwd1 = _ffn_wgrad(xn1, dh1, ga1, ua1, dga1, dua1, "ffn1_wgrad")
    return (loss_p, dx, [dwg1, dwu1, dwd1, dwin, dwo, dwg2, dwu2, dwd2],
            [dg_ffn1, dg_mix, dg_ret, dg_ffn2, dg_fin])


N_DEV = 8
GAIN_ROWS = 8


def _place():
    x, y, c = lax.axis_index("x"), lax.axis_index("y"), lax.axis_index("c")
    chips = [(1 - x, y), (x, 1 - y), (1 - x, 1 - y)]
    return x, y, c, chips


def _hbm_specs(n):
    return [pl.BlockSpec(memory_space=pl.ANY)] * n


def _place_shard(place, w, name):
    r, cols = w.shape
    tr = r // 4

    def body(place_ref, w_ref, o_ref):
        o_ref[...] = w_ref[...].astype(BF16)

    return pl.pallas_call(
        body, name=name,
        grid_spec=pltpu.PrefetchScalarGridSpec(
            num_scalar_prefetch=1, grid=(r // tr,),
            in_specs=[pl.BlockSpec((tr, cols), lambda i, pr: (i, 0))],
            out_specs=pl.BlockSpec((None, tr, cols), lambda i, pr: (pr[0], i, 0))),
        out_shape=jax.ShapeDtypeStruct((N_SHARD, r, cols), BF16),
        compiler_params=_params("arbitrary"),
    )(place, w)


def _all_gather_weights(bufs):
    na = len(bufs)

    def body(*refs):
        outs = refs[na:2 * na]
        send_sem, recv_sem, fsend_sem, frecv_sem = refs[2 * na:]
        x, y, c, chips = _place()
        me = 2 * x + y

        def half(a, idx, which):
            hr = outs[a].shape[1] // 2
            return outs[a].at[idx, pl.ds(which * hr, hr)]

        def ici(a, j, idx):
            px, py = chips[j]
            return pltpu.make_async_remote_copy(
                src_ref=half(a, idx, c), dst_ref=half(a, idx, c),
                send_sem=send_sem.at[a, j], recv_sem=recv_sem.at[a, j], device_id=(px, py, c), device_id_type=MESH)

        def d2d(a, j, idx, which):
            return pltpu.make_async_remote_copy(
                src_ref=half(a, idx, which), dst_ref=half(a, idx, which),
                send_sem=fsend_sem.at[a, j], recv_sem=frecv_sem.at[a, j], device_id=(x, y, 1 - c), device_id_type=MESH)

        sends = [ici(a, j, me) for a in range(na) for j in range(3)]
        for cp in sends:
            cp.start()
        passed = []
        for a in range(na):
            for j, (px, py) in enumerate(chips):
                ici(a, j, 2 * px + py).wait_recv()
                cp = d2d(a, j, 2 * px + py, c)
                cp.start()
                passed.append(cp)
        for a in range(na):
            for j, (px, py) in enumerate(chips):
                d2d(a, j, 2 * px + py, 1 - c).wait_recv()
        for cp in sends + passed:
            cp.wait_send()

    return pl.pallas_call(
        body, name="all_gather_weights",
        in_specs=_hbm_specs(na), out_specs=_hbm_specs(na),
        out_shape=[jax.ShapeDtypeStruct(b.shape, b.dtype) for b in bufs],
        input_output_aliases={a: a for a in range(na)},
        scratch_shapes=[pltpu.SemaphoreType.DMA((na, 3))] * 4,
    )(*bufs)


def _pair_exchange(grads):
    na = len(grads)

    def body(*refs):
        ins, outs = refs[:na], refs[na:2 * na]
        send_sem, recv_sem = refs[2 * na:]
        x, y, c, _ = _place()
        copies = []
        for a in range(na):
            hr = ins[a].shape[1] // 2
            copies.append(pltpu.make_async_remote_copy(
                src_ref=ins[a].at[:, pl.ds((1 - c) * hr, hr)], dst_ref=outs[a],
                send_sem=send_sem.at[a], recv_sem=recv_sem.at[a], device_id=(x, y, 1 - c), device_id_type=MESH))
        for cp in copies:
            cp.start()
        for cp in copies:
            cp.wait()

    return pl.pallas_call(
        body, name="grad_pair_exchange",
        in_specs=_hbm_specs(na), out_specs=_hbm_specs(na),
        out_shape=[jax.ShapeDtypeStruct((g.shape[0], g.shape[1] // 2, g.shape[2]), g.dtype) for g in grads],
        scratch_shapes=[pltpu.SemaphoreType.DMA((na,))] * 2,
    )(*grads)


def _chip_exchange(sums):
    na = len(sums)

    def body(*refs):
        ins, outs = refs[:na], refs[na:2 * na]
        send_sem, recv_sem = refs[2 * na:]
        x, y, c, chips = _place()
        copies = []
        for a in range(na):
            for j, (px, py) in enumerate(chips):
                copies.append(pltpu.make_async_remote_copy(
                    src_ref=ins[a].at[2 * px + py], dst_ref=outs[a].at[j],
                    send_sem=send_sem.at[a, j], recv_sem=recv_sem.at[a, j], device_id=(px, py, c), device_id_type=MESH))
        for cp in copies:
            cp.start()
        for cp in copies:
            cp.wait()

    return pl.pallas_call(
        body, name="grad_chip_exchange",
        in_specs=_hbm_specs(na), out_specs=_hbm_specs(na),
        out_shape=[jax.ShapeDtypeStruct((3,) + s.shape[1:], s.dtype) for s in sums],
        scratch_shapes=[pltpu.SemaphoreType.DMA((na, 3))] * 2,
    )(*sums)


def _finish_exchange(grads, gpack):
    na = len(grads)

    def body(*refs):
        g_in = refs[na]
        outs, g_out = refs[na + 1:2 * na + 1], refs[2 * na + 1]
        loc_sem, send_sem, recv_sem, gsend_sem, grecv_sem = refs[2 * na + 2:]
        x, y, c, _ = _place()
        dev = 4 * x + 2 * y + c
        local, remote = [], []
        for a in range(na):
            hr = outs[a].shape[0] // 2
            rows = outs[a].at[pl.ds(c * hr, hr)]
            remote.append(pltpu.make_async_remote_copy(
                src_ref=rows, dst_ref=rows, send_sem=send_sem.at[a], recv_sem=recv_sem.at[a],
                device_id=(x, y, 1 - c), device_id_type=MESH))
        local.append(pltpu.make_async_copy(g_in, g_out.at[dev], loc_sem))
        for k in range(N_DEV - 1):
            bx, by, bc = (k + 1) // 4, ((k + 1) // 2) % 2, (k + 1) % 2
            peer = (jnp.bitwise_xor(x, bx), jnp.bitwise_xor(y, by), jnp.bitwise_xor(c, bc))
            remote.append(pltpu.make_async_remote_copy(
                src_ref=g_in, dst_ref=g_out.at[dev], send_sem=gsend_sem.at[k], recv_sem=grecv_sem.at[k],
                device_id=peer, device_id_type=MESH))
        for cp in local + remote:
            cp.start()
        for a in range(na):
            hr = outs[a].shape[0] // 2
            rows = outs[a].at[pl.ds((1 - c) * hr, hr)]
            pltpu.make_async_remote_copy(src_ref=rows, dst_ref=rows, send_sem=send_sem.at[a], recv_sem=recv_sem.at[a],
                                         device_id=(x, y, 1 - c), device_id_type=MESH).wait_recv()
        for k in range(N_DEV - 1):
            slot = g_out.at[jnp.bitwise_xor(dev, k + 1)]
            pltpu.make_async_remote_copy(src_ref=slot, dst_ref=slot, send_sem=gsend_sem.at[k], recv_sem=grecv_sem.at[k],
                                         device_id=(x, y, 1 - c), device_id_type=MESH).wait_recv()
        for cp in remote:
            cp.wait_send()
        for cp in local:
            cp.wait()

    return pl.pallas_call(
        body, name="grad_finish_exchange",
        in_specs=_hbm_specs(na + 1), out_specs=_hbm_specs(na + 1),
        out_shape=[jax.ShapeDtypeStruct(g.shape, g.dtype) for g in grads]
        + [jax.ShapeDtypeStruct((N_DEV,) + gpack.shape, gpack.dtype)],
        input_output_aliases={a: a for a in range(na)},
        scratch_shapes=[pltpu.SemaphoreType.DMA, pltpu.SemaphoreType.DMA((na,)), pltpu.SemaphoreType.DMA((na,)),
                        pltpu.SemaphoreType.DMA((N_DEV - 1,)), pltpu.SemaphoreType.DMA((N_DEV - 1,))],
    )(*grads, gpack)


def _pair_sum(place, grad, got, name):
    ns, r, cols = grad.shape
    hr = r // 2

    def body(place_ref, g_ref, r_ref, o_ref):
        o_ref[...] = (g_ref[...] + r_ref[...]).astype(BF16)

    return pl.pallas_call(
        body, name=name,
        grid_spec=pltpu.PrefetchScalarGridSpec(
            num_scalar_prefetch=1, grid=(ns,),
            in_specs=[pl.BlockSpec((None, hr, cols), lambda s, pr: (s, pr[1], 0)),
                      pl.BlockSpec((None, hr, cols), lambda s, pr: (s, 0, 0))],
            out_specs=pl.BlockSpec((None, hr, cols), lambda s, pr: (s, 0, 0))),
        out_shape=jax.ShapeDtypeStruct((ns, hr, cols), BF16),
        compiler_params=_params("arbitrary"),
    )(place, grad, got)


def _chip_sum(place, grad, got, others, name):
    ns, r, cols = grad.shape
    hr = r // 2
    nb = 2
    tr = hr // nb

    def body(place_ref, g_ref, r_ref, o3_ref, o_ref):
        acc = g_ref[...] + r_ref[...]
        for j in range(3):
            acc = acc + o3_ref[j].astype(F32)
        o_ref[...] = acc

    return pl.pallas_call(
        body, name=name,
        grid_spec=pltpu.PrefetchScalarGridSpec(
            num_scalar_prefetch=1, grid=(nb,),
            in_specs=[pl.BlockSpec((None, tr, cols), lambda i, pr: (pr[0], pr[1] * nb + i, 0)),
                      pl.BlockSpec((None, tr, cols), lambda i, pr: (pr[0], i, 0)),
                      pl.BlockSpec((3, tr, cols), lambda i, pr: (0, i, 0))],
            out_specs=pl.BlockSpec((tr, cols), lambda i, pr: (pr[1] * nb + i, 0))),
        out_shape=jax.ShapeDtypeStruct((r, cols), F32),
        compiler_params=_params("arbitrary"),
    )(place, grad, got, others)


def _pack_gains(parts, d):
    def body(*refs):
        ins, o_ref = refs[:-1], refs[-1]
        o_ref[...] = jnp.zeros_like(o_ref)
        for k, r in enumerate(ins):
            o_ref[k:k + 1, 0:r.shape[1]] = jnp.sum(r[...], axis=0, keepdims=True)

    return pl.pallas_call(
        body, name="pack_gains", out_shape=jax.ShapeDtypeStruct((GAIN_ROWS, d), F32),
    )(*parts)


def _adamw_math(w, g, m, v):
    m = ADAM_B1 * m + (1.0 - ADAM_B1) * g
    v = ADAM_B2 * v + (1.0 - ADAM_B2) * jnp.square(g)
    m_hat = m / (1.0 - ADAM_B1 ** ADAM_STEP)
    v_hat = v / (1.0 - ADAM_B2 ** ADAM_STEP)
    return -ADAM_LR * (m_hat / (jnp.sqrt(v_hat) + ADAM_EPS) + ADAM_WD * w), m, v


def _adamw(w, g, m, v, name):
    r, cols = w.shape
    tr = r // 4 if (r // 4) % 8 == 0 else r

    def body(w_ref, g_ref, m_ref, v_ref, d_ref, nm_ref, nv_ref):
        d_ref[...], nm_ref[...], nv_ref[...] = _adamw_math(w_ref[...], g_ref[...], m_ref[...], v_ref[...])

    tile = pl.BlockSpec((tr, cols), lambda i: (i, 0))
    return pl.pallas_call(
        body, name=name, grid=(r // tr,), in_specs=[tile] * 4, out_specs=[tile] * 3,
        out_shape=[jax.ShapeDtypeStruct((r, cols), F32)] * 3,
        compiler_params=_params("arbitrary"),
    )(w, g, m, v)


def _adamw_gain(gall, row, w, m, v, name):
    n = w.shape[1]

    def body(ga_ref, w_ref, m_ref, v_ref, g_ref, d_ref, nm_ref, nv_ref):
        g = ga_ref[0, row:row + 1, 0:n]
        for k in range(1, N_DEV):
            g = g + ga_ref[k, row:row + 1, 0:n]
        g_ref[...] = g
        d_ref[...], nm_ref[...], nv_ref[...] = _adamw_math(w_ref[...], g, m_ref[...], v_ref[...])

    return pl.pallas_call(
        body, name=name, out_shape=[jax.ShapeDtypeStruct((1, n), F32)] * 4,
    )(gall, w, m, v)


def kernel(x, norm_ffn1, ffn1_w_gate, ffn1_w_up, ffn1_w_down, norm_mix, w_in, ret_norm_gain, w_out, norm_ffn2, ffn2_w_gate, ffn2_w_up, ffn2_w_down, norm_final, loss_target, m_norm_ffn1, m_ffn1_w_gate, m_ffn1_w_up, m_ffn1_w_down, m_norm_mix, m_w_in, m_ret_norm_gain, m_w_out, m_norm_ffn2, m_ffn2_w_gate, m_ffn2_w_up, m_ffn2_w_down, m_norm_final, v_norm_ffn1, v_ffn1_w_gate, v_ffn1_w_up, v_ffn1_w_down, v_norm_mix, v_w_in, v_ret_norm_gain, v_w_out, v_norm_ffn2, v_ffn2_w_gate, v_ffn2_w_up, v_ffn2_w_down, v_norm_final):
    d = x.shape[-1]
    mats = [ffn1_w_gate, ffn1_w_up, ffn1_w_down, w_in, w_out, ffn2_w_gate, ffn2_w_up, ffn2_w_down]
    mats_m = [m_ffn1_w_gate, m_ffn1_w_up, m_ffn1_w_down, m_w_in, m_w_out, m_ffn2_w_gate, m_ffn2_w_up, m_ffn2_w_down]
    mats_v = [v_ffn1_w_gate, v_ffn1_w_up, v_ffn1_w_down, v_w_in, v_w_out, v_ffn2_w_gate, v_ffn2_w_up, v_ffn2_w_down]
    mat_names = ["ffn1_w_gate", "ffn1_w_up", "ffn1_w_down", "w_in", "w_out", "ffn2_w_gate", "ffn2_w_up", "ffn2_w_down"]
    gains = [norm_ffn1, norm_mix, ret_norm_gain, norm_ffn2, norm_final.reshape(1, d)]
    gains_m = [m_norm_ffn1, m_norm_mix, m_ret_norm_gain, m_norm_ffn2, m_norm_final.reshape(1, d)]
    gains_v = [v_norm_ffn1, v_norm_mix, v_ret_norm_gain, v_norm_ffn2, v_norm_final.reshape(1, d)]
    gain_names = ["norm_ffn1", "norm_mix", "ret_norm_gain", "norm_ffn2", "norm_final"]

    shards = [w[0] for w in mats]
    place = jnp.stack([2 * lax.axis_index("x") + lax.axis_index("y"), lax.axis_index("c")]).astype(jnp.int32)
    full = _all_gather_weights([_place_shard(place, s, f"place_{n}") for s, n in zip(shards, mat_names)])
    loss_p, dx, grads, gain_parts = _local_step(x[0], loss_target[0], gains, full)

    got = _pair_exchange(grads)
    pair = [_pair_sum(place, g, r, f"pair_sum_{n}") for g, r, n in zip(grads, got, mat_names)]
    others = _chip_exchange(pair)
    halves = [_chip_sum(place, g, r, o, f"chip_sum_{n}") for g, r, o, n in zip(grads, got, others, mat_names)]
    *shard_grads, gall = _finish_exchange(halves, _pack_gains(gain_parts, d))

    out_g, out_d, out_m, out_v = {}, {}, {}, {}
    for n, w, g, m, v in zip(mat_names, shards, shard_grads, mats_m, mats_v):
        dl, nm, nv = _adamw(w, g, m[0], v[0], f"adamw_{n}")
        out_g[n], out_d[n], out_m[n], out_v[n] = g[None], dl[None], nm[None], nv[None]
    for row, (n, w, m, v) in enumerate(zip(gain_names, gains, gains_m, gains_v)):
        res = _adamw_gain(gall, row, w, m, v, f"adamw_{n}")
        shape = (d,) if n == "norm_final" else w.shape
        out_g[n], out_d[n], out_m[n], out_v[n] = [r.reshape(shape) for r in res]

    loss = lax.psum(jnp.sum(loss_p), ("x", "y", "c"))
    order = ["norm_ffn1", "ffn1_w_gate", "ffn1_w_up", "ffn1_w_down", "norm_mix", "w_in", "ret_norm_gain", "w_out",
             "norm_ffn2", "ffn2_w_gate", "ffn2_w_up", "ffn2_w_down", "norm_final"]
    return (loss, dx[None], *[out_g[n] for n in order], *[out_d[n] for n in order],
            *[out_m[n] for n in order], *[out_v[n] for n in order])
```

```python
import functools
import math

import jax
import jax.numpy as jnp
from jax import lax
from jax.experimental import pallas as pl
from jax.experimental.pallas import tpu as pltpu

F32 = jnp.float32
BF16 = jnp.bfloat16
MESH = pl.DeviceIdType.MESH

NORM_EPS = 1e-6
GN_EPS = 1e-6
ROPE_BASE = 10000.0
RET_HEADS = 4
RET_DIM = 128
RET_WIDTH = 512
RET_CHUNK = 128
ATT_HEADS = 8
ATT_DIM = 64
ATT_WIDTH = 512
ATT_BLOCK = 128
DILATIONS = (1, 4, 16)
IN_COLS = 4 * RET_WIDTH + 3 * ATT_WIDTH
LANE = 128
N_SHARD = 4
ADAM_LR, ADAM_B1, ADAM_B2, ADAM_EPS, ADAM_WD, ADAM_STEP = 0.001, 0.9, 0.999, 1e-08, 0.01, 10

V7X_VMEM_BYTES = 64 * 1024 * 1024
VMEM_LIMIT = V7X_VMEM_BYTES - 8 * 1024 * 1024

NT = (((1,), (1,)), ((), ()))
TN = (((0,), (0,)), ((), ()))


def _params(*sem):
    return pltpu.CompilerParams(dimension_semantics=sem, vmem_limit_bytes=VMEM_LIMIT)


def _dot(a, b, dims=None):
    if dims is None:
        return jnp.dot(a, b, preferred_element_type=F32)
    return lax.dot_general(a, b, dims, preferred_element_type=F32)


def _sigmoid(x):
    return 1.0 / (1.0 + jnp.exp(-x))


def _load_weights(pairs, sems):
    copies = [pltpu.make_async_copy(src, dst, sems.at[k]) for k, (src, dst) in enumerate(pairs)]
    for cp in copies:
        cp.start()
    for cp in copies:
        cp.wait()


def _rows8(v):
    r, c = v.shape
    return v.reshape(r // 8, 8, c).sum(axis=0)


class _Ride:
    def __init__(self, inputs, out_shapes, sems, start, finish, aliases=None):
        self.inputs, self.out_shapes, self.sems = list(inputs), list(out_shapes), list(sems)
        self.start, self.finish, self.aliases = start, finish, dict(aliases or {})


def _pallas(body, rides, *, name, in_specs, out_specs, out_shape, args, grid=(), scratch_shapes=(), sem=()):
    rides = [r for r in (rides or []) if r is not None]
    n_in, n_out, n_scr = len(args), len(out_shape), len(scratch_shapes)
    hbm = pl.BlockSpec(memory_space=pl.ANY)
    r_in = [a for r in rides for a in r.inputs]
    r_out = [s for r in rides for s in r.out_shapes]
    r_sem = [s for r in rides for s in r.sems]
    aliases, spans, ki, ko, ks = {}, [], 0, 0, 0
    for r in rides:
        aliases.update({n_in + ki + i: n_out + ko + o for i, o in r.aliases.items()})
        spans.append((ki, ko, ks))
        ki, ko, ks = ki + len(r.inputs), ko + len(r.out_shapes), ks + len(r.sems)

    def wrapped(*refs):
        ins, rin = refs[:n_in], refs[n_in:n_in + len(r_in)]
        o0 = n_in + len(r_in)
        outs, rout = refs[o0:o0 + n_out], refs[o0 + n_out:o0 + n_out + len(r_out)]
        s0 = o0 + n_out + len(r_out)
        scr, rsem = refs[s0:s0 + n_scr], refs[s0 + n_scr:]
        part = lambda r, k: (rin[spans[k][0]:spans[k][0] + len(r.inputs)], rout[spans[k][1]:spans[k][1] + len(r.out_shapes)],
                             rsem[spans[k][2]:spans[k][2] + len(r.sems)])
        first = functools.reduce(jnp.logical_and, [pl.program_id(k) == 0 for k in range(len(grid))], True)
        last = functools.reduce(jnp.logical_and, [pl.program_id(k) == grid[k] - 1 for k in range(len(grid))], True)
        if rides:
            @pl.when(first)
            def _():
                for k, r in enumerate(rides):
                    r.start(*part(r, k))

        body(*ins, *outs, *scr)
        if rides:
            @pl.when(last)
            def _():
                for k, r in enumerate(rides):
                    r.finish(*part(r, k))

    res = pl.pallas_call(
        wrapped, name=name, grid=grid,
        in_specs=list(in_specs) + [hbm] * len(r_in), out_specs=list(out_specs) + [hbm] * len(r_out),
        out_shape=list(out_shape) + r_out, input_output_aliases=aliases,
        scratch_shapes=list(scratch_shapes) + r_sem,
        compiler_params=pltpu.CompilerParams(dimension_semantics=sem, vmem_limit_bytes=VMEM_LIMIT) if grid else None,
    )(*args, *r_in)
    extras = [list(res[n_out + ko:n_out + ko + len(r.out_shapes)]) for r, (_, ko, _) in zip(rides, spans)]
    return list(res[:n_out]), extras


def _run(ride, name):
    def body(*refs):
        n_in, n_out = len(ride.inputs), len(ride.out_shapes)
        parts = refs[:n_in], refs[n_in:n_in + n_out], refs[n_in + n_out:]
        ride.start(*parts)
        ride.finish(*parts)

    hbm = pl.BlockSpec(memory_space=pl.ANY)
    return list(pl.pallas_call(
        body, name=name, in_specs=[hbm] * len(ride.inputs), out_specs=[hbm] * len(ride.out_shapes),
        out_shape=ride.out_shapes, input_output_aliases=ride.aliases, scratch_shapes=ride.sems,
    )(*ride.inputs))


def _ffn_fwd(x, gain, wg, wu, wd, name, rides=None):
    t, d = x.shape
    ns, _, fs = wg.shape
    tm = min(256, t)

    def body(x_ref, gain_ref, wg_hbm, wu_hbm, wd_hbm, h_ref, xn_ref, g_ref, u_ref, wg_v, wu_v, wd_v, sems):
        @pl.when(pl.program_id(0) == 0)
        def _():
            _load_weights([(wg_hbm, wg_v), (wu_hbm, wu_v), (wd_hbm, wd_v)], sems)

        xv = x_ref[...]
        r = lax.rsqrt(jnp.mean(xv * xv, axis=-1, keepdims=True) + NORM_EPS)
        xn = (xv * r * gain_ref[...]).astype(BF16)
        xn_ref[...] = xn
        acc = jnp.zeros((tm, d), F32)
        for j in range(ns):
            g = _dot(xn, wg_v[j])
            u = _dot(xn, wu_v[j])
            g_ref[j] = g.astype(BF16)
            u_ref[j] = u.astype(BF16)
            a = (g * _sigmoid(g) * u).astype(BF16)
            acc = acc + _dot(a, wd_v[j])
        h_ref[...] = xv + 0.5 * acc

    hbm = pl.BlockSpec(memory_space=pl.ANY)
    return _pallas(
        body, rides, name=name, grid=(t // tm,),
        in_specs=[pl.BlockSpec((tm, d), lambda i: (i, 0)), pl.BlockSpec((1, d), lambda i: (0, 0)), hbm, hbm, hbm],
        out_specs=[pl.BlockSpec((tm, d), lambda i: (i, 0)), pl.BlockSpec((tm, d), lambda i: (i, 0)),
                   pl.BlockSpec((ns, tm, fs), lambda i: (0, i, 0)), pl.BlockSpec((ns, tm, fs), lambda i: (0, i, 0))],
        out_shape=[jax.ShapeDtypeStruct((t, d), F32), jax.ShapeDtypeStruct((t, d), BF16),
                   jax.ShapeDtypeStruct((ns, t, fs), BF16), jax.ShapeDtypeStruct((ns, t, fs), BF16)],
        scratch_shapes=[pltpu.VMEM(wg.shape, BF16), pltpu.VMEM(wu.shape, BF16), pltpu.VMEM(wd.shape, BF16),
                        pltpu.SemaphoreType.DMA((3,))],
        sem=("arbitrary",), args=[x, gain, wg, wu, wd])


def _ffn_bwd_data(dy, x, gain, g, u, wg, wu, wd, name, rides=None):
    t, d = x.shape
    ns, _, fs = wg.shape
    tm = min(256, t)

    def body(dy_ref, x_ref, gain_ref, g_ref, u_ref, wg_hbm, wu_hbm, wd_hbm, dx_ref, dg_ref, du_ref, dgain_ref,
             wg_v, wu_v, wd_v, sems):
        @pl.when(pl.program_id(0) == 0)
        def _():
            _load_weights([(wg_hbm, wg_v), (wu_hbm, wu_v), (wd_hbm, wd_v)], sems)
            dgain_ref[...] = jnp.zeros_like(dgain_ref)

        dyv = dy_ref[...]
        dyh = (0.5 * dyv).astype(BF16)
        dxn = jnp.zeros((tm, d), F32)
        for j in range(ns):
            da = _dot(dyh, wd_v[j], NT)
            gj = g_ref[j].astype(F32)
            uj = u_ref[j].astype(F32)
            sig = _sigmoid(gj)
            dgj = (da * uj * (sig * (1.0 + gj * (1.0 - sig)))).astype(BF16)
            duj = (da * (gj * sig)).astype(BF16)
            dg_ref[j] = dgj
            du_ref[j] = duj
            dxn = dxn + _dot(dgj, wg_v[j], NT) + _dot(duj, wu_v[j], NT)
        xv = x_ref[...]
        r = lax.rsqrt(jnp.mean(xv * xv, axis=-1, keepdims=True) + NORM_EPS)
        xh = xv * r
        dgain_ref[...] += _rows8(dxn * xh)
        dxh = dxn * gain_ref[...]
        dx_ref[...] = dyv + r * (dxh - xh * jnp.mean(dxh * xh, axis=-1, keepdims=True))

    hbm = pl.BlockSpec(memory_space=pl.ANY)
    tile = pl.BlockSpec((tm, d), lambda i: (i, 0))
    hid = pl.BlockSpec((ns, tm, fs), lambda i: (0, i, 0))
    return _pallas(
        body, rides, name=name, grid=(t // tm,),
        in_specs=[tile, tile, pl.BlockSpec((1, d), lambda i: (0, 0)), hid, hid, hbm, hbm, hbm],
        out_specs=[tile, hid, hid, pl.BlockSpec((8, d), lambda i: (0, 0))],
        out_shape=[jax.ShapeDtypeStruct((t, d), F32), jax.ShapeDtypeStruct((ns, t, fs), BF16),
                   jax.ShapeDtypeStruct((ns, t, fs), BF16), jax.ShapeDtypeStruct((8, d), F32)],
        scratch_shapes=[pltpu.VMEM(wg.shape, BF16), pltpu.VMEM(wu.shape, BF16), pltpu.VMEM(wd.shape, BF16),
                        pltpu.SemaphoreType.DMA((3,))],
        sem=("arbitrary",), args=[dy, x, gain, g, u, wg, wu, wd])


def _ffn_wgrad_down(g, u, dy, name, rides=None):
    t, d = dy.shape
    ns, _, fs = g.shape
    tk = min(512, t)

    def body(dy_ref, g_ref, u_ref, dwd_ref):
        @pl.when(pl.program_id(1) == 0)
        def _():
            dwd_ref[...] = jnp.zeros_like(dwd_ref)

        gj = g_ref[...].astype(F32)
        a = (gj * _sigmoid(gj) * u_ref[...].astype(F32)).astype(BF16)
        dwd_ref[...] += _dot(a, (0.5 * dy_ref[...]).astype(BF16), TN)

    hid = pl.BlockSpec((None, tk, fs), lambda j, k: (j, k, 0))
    return _pallas(
        body, rides, name=name, grid=(ns, t // tk),
        in_specs=[pl.BlockSpec((tk, d), lambda j, k: (k, 0)), hid, hid],
        out_specs=[pl.BlockSpec((None, fs, d), lambda j, k: (j, 0, 0))],
        out_shape=[jax.ShapeDtypeStruct((ns, fs, d), F32)],
        sem=("arbitrary", "arbitrary"), args=[dy, g, u])


def _ffn_wgrad_gu(xn, dg, du, name, rides=None):
    t, d = xn.shape
    ns, _, fs = dg.shape
    tk = min(512, t)

    def body(xn_ref, dg_ref, du_ref, dwg_ref, dwu_ref):
        @pl.when(pl.program_id(1) == 0)
        def _():
            dwg_ref[...] = jnp.zeros_like(dwg_ref)
            dwu_ref[...] = jnp.zeros_like(dwu_ref)

        xnv = xn_ref[...]
        dwg_ref[...] += _dot(xnv, dg_ref[...], TN)
        dwu_ref[...] += _dot(xnv, du_ref[...], TN)

    hid = pl.BlockSpec((None, tk, fs), lambda j, k: (j, k, 0))
    out = pl.BlockSpec((None, d, fs), lambda j, k: (j, 0, 0))
    return _pallas(
        body, rides, name=name, grid=(ns, t // tk),
        in_specs=[pl.BlockSpec((tk, d), lambda j, k: (k, 0)), hid, hid],
        out_specs=[out, out], out_shape=[jax.ShapeDtypeStruct((ns, d, fs), F32)] * 2,
        sem=("arbitrary", "arbitrary"), args=[xn, dg, du])


def _tn_matmul(a, b, bn, name):
    t, m = a.shape
    n = b.shape[1]
    tk = min(512, t)

    def body(a_ref, b_ref, o_ref):
        @pl.when(pl.program_id(1) == 0)
        def _():
            o_ref[...] = jnp.zeros_like(o_ref)

        o_ref[...] += _dot(a_ref[...].astype(BF16), b_ref[...].astype(BF16), TN)

    return pl.pallas_call(
        body, name=name, grid=(n // bn, t // tk),
        in_specs=[pl.BlockSpec((tk, m), lambda j, k: (k, 0)), pl.BlockSpec((tk, bn), lambda j, k: (k, j))],
        out_specs=pl.BlockSpec((None, m, bn), lambda j, k: (j, 0, 0)),
        out_shape=jax.ShapeDtypeStruct((n // bn, m, bn), F32),
        compiler_params=_params("arbitrary", "arbitrary"),
    )(a, b)


def _chunk_scratch(tm, w):
    return pltpu.VMEM((w // LANE, tm, LANE), F32)


def _regroup_store(cbuf, out_ref, dil):
    n = out_ref.shape[1]
    for g in range(dil):
        for k in range(cbuf.shape[0]):
            rows = cbuf[k] if dil == 1 else cbuf[k, pl.ds(g, n, stride=dil), :]
            out_ref[g, :, k * LANE:(k + 1) * LANE] = rows.astype(out_ref.dtype)


def _natural_rows(ref, dil, cbuf):
    if dil == 1:
        return ref[0]
    n = ref.shape[1]
    for g in range(dil):
        for k in range(cbuf.shape[0]):
            cbuf[k, pl.ds(g, n, stride=dil), :] = ref[g, :, k * LANE:(k + 1) * LANE]
    return jnp.concatenate([cbuf[k] for k in range(cbuf.shape[0])], axis=1)


def _inproj_fwd(h, gain, win):
    t, d = h.shape
    ns, _, cs = win.shape
    tm = min(512, t)
    rw, aw = 4 * RET_WIDTH, 3 * ATT_WIDTH

    def body(h_ref, gain_ref, w_ref, xn_ref, ur_ref, *rest):
        a_refs, abuf = rest[:-1], rest[-1]
        hv = h_ref[...]
        r = lax.rsqrt(jnp.mean(hv * hv, axis=-1, keepdims=True) + NORM_EPS)
        xn = (hv * r * gain_ref[...]).astype(BF16)
        xn_ref[...] = xn
        for j in range(ns):
            res = _dot(xn, w_ref[j])
            for k in range(cs // LANE):
                chunk = j * (cs // LANE) + k
                piece = res[:, k * LANE:(k + 1) * LANE]
                if chunk < rw // LANE:
                    ur_ref[:, chunk * LANE:(chunk + 1) * LANE] = piece
                else:
                    abuf[chunk - rw // LANE] = piece
        for dil, a_ref in zip(DILATIONS, a_refs):
            _regroup_store(abuf, a_ref, dil)

    return pl.pallas_call(
        body, name="inproj_fwd", grid=(t // tm,),
        in_specs=[pl.BlockSpec((tm, d), lambda i: (i, 0)), pl.BlockSpec((1, d), lambda i: (0, 0)),
                  pl.BlockSpec(win.shape, lambda i: (0, 0, 0))],
        out_specs=[pl.BlockSpec((tm, d), lambda i: (i, 0)), pl.BlockSpec((tm, rw), lambda i: (i, 0))]
        + [pl.BlockSpec((dil, tm // dil, aw), lambda i: (0, i, 0)) for dil in DILATIONS],
        out_shape=[jax.ShapeDtypeStruct((t, d), BF16), jax.ShapeDtypeStruct((t, rw), F32)]
        + [jax.ShapeDtypeStruct((dil, t // dil, aw), BF16) for dil in DILATIONS],
        scratch_shapes=[_chunk_scratch(tm, aw)],
        compiler_params=_params("arbitrary"),
    )(h, gain, win)


def _inproj_bwd(pieces, h, gain, dres, win):
    t, d = h.shape
    ns, _, cs = win.shape
    pw = pieces[0].shape[1]
    tm = min(512, t)
    npc = len(pieces)

    def body(*refs):
        p_refs = refs[:npc]
        h_ref, gain_ref, dres_ref, w_ref, dh_ref, du_ref, dgain_ref = refs[npc:]

        @pl.when(pl.program_id(0) == 0)
        def _():
            dgain_ref[...] = jnp.zeros_like(dgain_ref)

        for k in range(npc):
            du_ref[:, k * pw:(k + 1) * pw] = p_refs[k][...]
        dxn = jnp.zeros((tm, d), F32)
        for j in range(ns):
            dxn = dxn + _dot(du_ref[:, j * cs:(j + 1) * cs], w_ref[j], NT)
        hv = h_ref[...]
        r = lax.rsqrt(jnp.mean(hv * hv, axis=-1, keepdims=True) + NORM_EPS)
        xh = hv * r
        dgain_ref[...] += _rows8(dxn * xh)
        dxh = dxn * gain_ref[...]
        dh_ref[...] = dres_ref[...] + r * (dxh - xh * jnp.mean(dxh * xh, axis=-1, keepdims=True))

    tile = pl.BlockSpec((tm, d), lambda i: (i, 0))
    return pl.pallas_call(
        body, name="inproj_bwd", grid=(t // tm,),
        in_specs=[pl.BlockSpec((tm, pw), lambda i: (i, 0))] * npc + [
            tile, pl.BlockSpec((1, d), lambda i: (0, 0)), tile, pl.BlockSpec(win.shape, lambda i: (0, 0, 0))],
        out_specs=[tile, pl.BlockSpec((tm, npc * pw), lambda i: (i, 0)), pl.BlockSpec((8, d), lambda i: (0, 0))],
        out_shape=[jax.ShapeDtypeStruct((t, d), F32), jax.ShapeDtypeStruct((t, npc * pw), BF16),
                   jax.ShapeDtypeStruct((8, d), F32)],
        compiler_params=_params("arbitrary"),
    )(*pieces, h, gain, dres, win)


def _outproj_fwd(h, mix_r, mix_a, wo):
    t, d = h.shape
    hw = mix_r.shape[1]
    tm = min(512, t)

    def body(h_ref, mr_ref, ma_ref, w_ref, o_ref):
        o_ref[...] = h_ref[...] + _dot(mr_ref[...], w_ref[0:hw, :]) + _dot(ma_ref[...], w_ref[hw:2 * hw, :])

    tile = pl.BlockSpec((tm, d), lambda i: (i, 0))
    half = pl.BlockSpec((tm, hw), lambda i: (i, 0))
    return pl.pallas_call(
        body, name="outproj_fwd", grid=(t // tm,),
        in_specs=[tile, half, half, pl.BlockSpec(wo.shape, lambda i: (0, 0))],
        out_specs=tile, out_shape=jax.ShapeDtypeStruct((t, d), F32),
        compiler_params=_params("arbitrary"),
    )(h, mix_r, mix_a, wo)


def _outproj_bwd(dh, wo, rides=None):
    t, d = dh.shape
    hw = wo.shape[0] // 2
    tm = min(512, t)

    def body(dh_ref, w_ref, dr_ref, da_ref):
        dhb = dh_ref[...].astype(BF16)
        dr_ref[...] = _dot(dhb, w_ref[0:hw, :], NT)
        da_ref[...] = _dot(dhb, w_ref[hw:2 * hw, :], NT)

    half = pl.BlockSpec((tm, hw), lambda i: (i, 0))
    return _pallas(
        body, rides, name="outproj_bwd", grid=(t // tm,),
        in_specs=[pl.BlockSpec((tm, d), lambda i: (i, 0)), pl.BlockSpec(wo.shape, lambda i: (0, 0))],
        out_specs=[half, half],
        out_shape=[jax.ShapeDtypeStruct((t, hw), F32), jax.ShapeDtypeStruct((t, hw), F32)],
        sem=("arbitrary",), args=[dh, wo])


def _retention_tables(t):
    pos = jnp.arange(t, dtype=F32)
    inv_freq = ROPE_BASE ** (-jnp.arange(0, RET_DIM, 2, dtype=F32) / RET_DIM)
    ang = jnp.repeat(pos[:, None] * inv_freq[None, :], 2, axis=-1)
    c = RET_CHUNK
    log_g = jnp.log(1.0 - 2.0 ** (-5.0 - jnp.arange(RET_HEADS, dtype=F32)))
    idx = jnp.arange(c, dtype=F32)
    rel = idx[:, None] - idx[None, :]
    decay = jnp.where(rel >= 0, jnp.exp(log_g[:, None, None] * jnp.maximum(rel, 0.0)), 0.0)
    zeta = jnp.exp(log_g[:, None] * (c - 1 - idx)[None, :])
    xi = jnp.exp(log_g[:, None] * (idx + 1)[None, :])
    gc = jnp.exp(log_g * c)
    wide = lambda v: jnp.broadcast_to(v[:, :, None], (RET_HEADS, c, LANE))
    return (jnp.cos(ang), jnp.sin(ang), decay, wide(zeta), wide(xi),
            jnp.broadcast_to(gc[:, None, None], (RET_HEADS, c, LANE)))


def _rot(v):
    lane = lax.broadcasted_iota(jnp.int32, v.shape, 1)
    nxt = pltpu.roll(v, LANE - 1, 1)
    prv = pltpu.roll(v, 1, 1)
    return jnp.where(lane % 2 == 0, -nxt, prv)


def _ret_specs(tr, rev, nt):
    ti = (lambda i: nt - 1 - i) if rev else (lambda i: i)
    col = lambda off: pl.BlockSpec((tr, LANE), lambda h, i: (ti(i), off + h))
    tab = pl.BlockSpec((tr, LANE), lambda h, i: (ti(i), 0))
    head = pl.BlockSpec((None, RET_CHUNK, LANE), lambda h, i: (h, 0, 0))
    return col, tab, head


def _ret_fwd(u, gain, tabs):
    t = u.shape[0]
    tr = min(1024, t)
    nt = t // tr
    cos, sin, decay, zeta, xi, gc = tabs
    scale = RET_DIM ** -0.5

    def body(q_ref, k_ref, v_ref, gt_ref, cos_ref, sin_ref, gain_ref, dec_ref, zeta_ref, xi_ref, gc_ref,
             raw_ref, mix_ref, state):
        @pl.when(pl.program_id(1) == 0)
        def _():
            state[...] = jnp.zeros_like(state)

        for ci in range(tr // RET_CHUNK):
            sl = pl.ds(ci * RET_CHUNK, RET_CHUNK)
            cs, sn = cos_ref[sl, :], sin_ref[sl, :]
            q, k = q_ref[sl, :], k_ref[sl, :]
            qb = (q * cs + _rot(q) * sn).astype(BF16)
            kr = (k * cs + _rot(k) * sn) * scale
            kb = kr.astype(BF16)
            vb = v_ref[sl, :].astype(BF16)
            s = _dot(qb, kb, NT) * dec_ref[...]
            st = state[...]
            o = _dot(s.astype(BF16), vb) + _dot(qb, st.astype(BF16)) * xi_ref[...]
            state[...] = st * gc_ref[...] + _dot((kr * zeta_ref[...]).astype(BF16), vb, TN)
            raw_ref[sl, :] = o
            mu = jnp.mean(o, axis=-1, keepdims=True)
            var = jnp.mean(jnp.square(o - mu), axis=-1, keepdims=True)
            y = (o - mu) * lax.rsqrt(var + GN_EPS) * gain_ref[...]
            gt = gt_ref[sl, :]
            mix_ref[sl, :] = (y * (gt * _sigmoid(gt))).astype(BF16)

    col, tab, head = _ret_specs(tr, False, nt)
    out = pl.BlockSpec((tr, LANE), lambda h, i: (i, h))
    return pl.pallas_call(
        body, name="ret_fwd", grid=(RET_HEADS, nt),
        in_specs=[col(0), col(4), col(8), col(12), tab, tab, pl.BlockSpec((1, LANE), lambda h, i: (0, h)),
                  head, head, head, head],
        out_specs=[out, out],
        out_shape=[jax.ShapeDtypeStruct((t, RET_WIDTH), F32), jax.ShapeDtypeStruct((t, RET_WIDTH), BF16)],
        scratch_shapes=[pltpu.VMEM((RET_DIM, RET_DIM), F32)],
        compiler_params=_params("arbitrary", "arbitrary"),
    )(u, u, u, u, cos, sin, gain, decay, zeta, xi, gc)


def _ret_bwd_q(dmix, raw, u, gain, tabs, rides=None):
    t = u.shape[0]
    tr = min(1024, t)
    nt = t // tr
    cos, sin, decay, zeta, xi, gc = tabs
    scale = RET_DIM ** -0.5

    def body(dm_ref, raw_ref, q_ref, k_ref, v_ref, gt_ref, cos_ref, sin_ref, gain_ref, dec_ref, zeta_ref, xi_ref, gc_ref,
             dq_ref, dgt_ref, dret_ref, dgain_ref, state):
        @pl.when(pl.program_id(1) == 0)
        def _():
            state[...] = jnp.zeros_like(state)
            dgain_ref[...] = jnp.zeros_like(dgain_ref)

        for ci in range(tr // RET_CHUNK):
            sl = pl.ds(ci * RET_CHUNK, RET_CHUNK)
            cs, sn = cos_ref[sl, :], sin_ref[sl, :]
            q, k = q_ref[sl, :], k_ref[sl, :]
            qb = (q * cs + _rot(q) * sn).astype(BF16)
            kr = (k * cs + _rot(k) * sn) * scale
            kb = kr.astype(BF16)
            vb = v_ref[sl, :].astype(BF16)
            o = raw_ref[sl, :]
            mu = jnp.mean(o, axis=-1, keepdims=True)
            var = jnp.mean(jnp.square(o - mu), axis=-1, keepdims=True)
            rs = lax.rsqrt(var + GN_EPS)
            n = (o - mu) * rs
            gt = gt_ref[sl, :]
            sig = _sigmoid(gt)
            dout = dm_ref[sl, :]
            dgt_ref[sl, :] = (dout * (n * gain_ref[...]) * (sig * (1.0 + gt * (1.0 - sig)))).astype(BF16)
            dy = dout * (gt * sig)
            dgain_ref[...] += _rows8(dy * n)
            dn = dy * gain_ref[...]
            do = rs * (dn - jnp.mean(dn, axis=-1, keepdims=True) - n * jnp.mean(dn * n, axis=-1, keepdims=True))
            dret_ref[sl, :] = do
            ds = _dot(do.astype(BF16), vb, NT) * dec_ref[...]
            st = state[...]
            dqr = _dot(ds.astype(BF16), kb) + _dot((do * xi_ref[...]).astype(BF16), st.astype(BF16), NT)
            dq_ref[sl, :] = (dqr * cs - _rot(dqr * sn)).astype(BF16)
            state[...] = st * gc_ref[...] + _dot((kr * zeta_ref[...]).astype(BF16), vb, TN)

    col, tab, head = _ret_specs(tr, False, nt)
    out = pl.BlockSpec((tr, LANE), lambda h, i: (i, h))
    return _pallas(
        body, rides, name="ret_bwd_q", grid=(RET_HEADS, nt),
        in_specs=[out, out, col(0), col(4), col(8), col(12), tab, tab, pl.BlockSpec((1, LANE), lambda h, i: (0, h)),
                  head, head, head, head],
        out_specs=[out, out, out, pl.BlockSpec((8, LANE), lambda h, i: (0, h))],
        out_shape=[jax.ShapeDtypeStruct((t, RET_WIDTH), BF16), jax.ShapeDtypeStruct((t, RET_WIDTH), BF16),
                   jax.ShapeDtypeStruct((t, RET_WIDTH), F32), jax.ShapeDtypeStruct((8, RET_WIDTH), F32)],
        scratch_shapes=[pltpu.VMEM((RET_DIM, RET_DIM), F32)],
        sem=("arbitrary", "arbitrary"), args=[dmix, raw, u, u, u, u, cos, sin, gain, decay, zeta, xi, gc])


def _ret_bwd_kv(dret, u, tabs, rides=None):
    t = u.shape[0]
    tr = min(1024, t)
    nt = t // tr
    cos, sin, decay, zeta, xi, gc = tabs
    scale = RET_DIM ** -0.5

    def body(do_ref, q_ref, k_ref, v_ref, cos_ref, sin_ref, dec_ref, zeta_ref, xi_ref, gc_ref, dk_ref, dv_ref, gst):
        @pl.when(pl.program_id(1) == 0)
        def _():
            gst[...] = jnp.zeros_like(gst)

        for ci in reversed(range(tr // RET_CHUNK)):
            sl = pl.ds(ci * RET_CHUNK, RET_CHUNK)
            cs, sn = cos_ref[sl, :], sin_ref[sl, :]
            q, k = q_ref[sl, :], k_ref[sl, :]
            qb = (q * cs + _rot(q) * sn).astype(BF16)
            kr = (k * cs + _rot(k) * sn) * scale
            kb = kr.astype(BF16)
            vb = v_ref[sl, :].astype(BF16)
            do = do_ref[sl, :]
            dob = do.astype(BF16)
            s = (_dot(qb, kb, NT) * dec_ref[...]).astype(BF16)
            ds = (_dot(dob, vb, NT) * dec_ref[...]).astype(BF16)
            gb = gst[...].astype(BF16)
            dv_ref[sl, :] = (_dot(s, dob, TN) + _dot((kr * zeta_ref[...]).astype(BF16), gb)).astype(BF16)
            dkr = (_dot(ds, qb, TN) + _dot(vb, gb, NT) * zeta_ref[...]) * scale
            dk_ref[sl, :] = (dkr * cs - _rot(dkr * sn)).astype(BF16)
            gst[...] = gst[...] * gc_ref[...] + _dot(qb, (do * xi_ref[...]).astype(BF16), TN)

    col, tab, head = _ret_specs(tr, True, nt)
    out = pl.BlockSpec((tr, LANE), lambda h, i: (nt - 1 - i, h))
    return _pallas(
        body, rides, name="ret_bwd_kv", grid=(RET_HEADS, nt),
        in_specs=[out, col(0), col(4), col(8), tab, tab, head, head, head, head],
        out_specs=[out, out],
        out_shape=[jax.ShapeDtypeStruct((t, RET_WIDTH), BF16), jax.ShapeDtypeStruct((t, RET_WIDTH), BF16)],
        scratch_shapes=[pltpu.VMEM((RET_DIM, RET_DIM), F32)],
        sem=("arbitrary", "arbitrary"), args=[dret, u, u, u, cos, sin, decay, zeta, xi, gc])


PAIRS = ATT_WIDTH // LANE
ATT_Q_BLK, ATT_K_BLK, ATT_V_BLK = 0, PAIRS, 2 * PAIRS
STAT_LANES = ATT_DIM // 2


def _att_tiles(t, dil):
    sub = t // dil
    tq = min(512, sub)
    return sub, tq, sub // tq, tq // ATT_BLOCK


def _att_in_specs(tq, qb, ti):
    cur = lambda off: pl.BlockSpec((None, tq, LANE), lambda g, p, i: (g, ti(i), off + p))
    prev = lambda off: pl.BlockSpec((None, ATT_BLOCK, LANE), lambda g, p, i: (g, jnp.maximum(ti(i) * qb - 1, 0), off + p))
    return [cur(ATT_Q_BLK), cur(ATT_K_BLK), prev(ATT_K_BLK), cur(ATT_V_BLK), prev(ATT_V_BLK)]


def _band_mask():
    row = lax.broadcasted_iota(jnp.int32, (ATT_BLOCK, 2 * ATT_BLOCK), 0)
    col = lax.broadcasted_iota(jnp.int32, (ATT_BLOCK, 2 * ATT_BLOCK), 1)
    dist = row + ATT_BLOCK - col
    return (dist >= 0) & (dist <= ATT_BLOCK), col >= ATT_BLOCK


def _att_fwd(ua, dil):
    sub = ua.shape[1]
    _, tq, nq, qb = _att_tiles(sub * dil, dil)

    def body(q_ref, kc_ref, kp_ref, vc_ref, vp_ref, o_ref, l_ref, kx, vx):
        tile = pl.program_id(2)
        kx[0:ATT_BLOCK, :] = kp_ref[...]
        kx[ATT_BLOCK:, :] = kc_ref[...]
        vx[0:ATT_BLOCK, :] = vp_ref[...]
        vx[ATT_BLOCK:, :] = vc_ref[...]
        band, cur_cols = _band_mask()
        for b in range(qb):
            rows = slice(b * ATT_BLOCK, (b + 1) * ATT_BLOCK)
            mask = band if b > 0 else band & (cur_cols | (tile > 0))
            qv = q_ref[rows, :] * jnp.asarray(ATT_DIM ** -0.5, BF16)
            kv = kx[b * ATT_BLOCK:(b + 2) * ATT_BLOCK, :]
            vv = vx[b * ATT_BLOCK:(b + 2) * ATT_BLOCK, :]
            outs, lses = [], []
            for e in range(LANE // ATT_DIM):
                hs = slice(e * ATT_DIM, (e + 1) * ATT_DIM)
                s = jnp.where(mask, _dot(qv[:, hs], kv[:, hs], NT), -1e30)
                m = jnp.max(s, axis=-1, keepdims=True)
                ex = jnp.exp(s - m)
                den = jnp.sum(ex, axis=-1, keepdims=True)
                outs.append(_dot((ex / den).astype(BF16), vv[:, hs]))
                lses.append(jnp.broadcast_to(m + jnp.log(den), (ATT_BLOCK, ATT_DIM)))
            o_ref[rows, :] = jnp.concatenate(outs, axis=1)
            l_ref[rows, :] = jnp.concatenate(lses, axis=1)

    out = pl.BlockSpec((None, tq, LANE), lambda g, p, i: (g, i, p))
    return pl.pallas_call(
        body, name=f"att_fwd_d{dil}", grid=(dil, PAIRS, nq),
        in_specs=_att_in_specs(tq, qb, lambda i: i),
        out_specs=[out, out],
        out_shape=[jax.ShapeDtypeStruct((dil, sub, ATT_WIDTH), F32)] * 2,
        scratch_shapes=[pltpu.VMEM((tq + ATT_BLOCK, LANE), BF16)] * 2,
        compiler_params=_params("arbitrary", "arbitrary", "arbitrary"),
    )(ua, ua, ua, ua, ua)


def _regrouped_spec(tm, dil, w):
    return pl.BlockSpec((dil, tm // dil, w), lambda i: (0, i, 0))


def _att_combine(outs, lses, t):
    w = ATT_WIDTH
    tm = min(512, t)
    nb = len(outs)

    def body(*refs):
        o_refs, l_refs = refs[:nb], refs[nb:2 * nb]
        mix_ref, att_ref, lse_ref, buf = refs[2 * nb:]
        ls = [_natural_rows(r, dil, buf) for r, dil in zip(l_refs, DILATIONS)]
        m = functools.reduce(jnp.maximum, ls)
        ws = [jnp.exp(l - m) for l in ls]
        den = functools.reduce(jnp.add, ws)
        att = functools.reduce(jnp.add, [(wt / den) * _natural_rows(r, dil, buf) for wt, r, dil in zip(ws, o_refs, DILATIONS)])
        att_ref[...] = att
        mix_ref[...] = att.astype(BF16)
        lse_ref[...] = m + jnp.log(den)

    tile = pl.BlockSpec((tm, w), lambda i: (i, 0))
    regrouped = [_regrouped_spec(tm, dil, w) for dil in DILATIONS]
    return pl.pallas_call(
        body, name="att_combine", grid=(t // tm,),
        in_specs=regrouped * 2, out_specs=[tile, tile, tile],
        out_shape=[jax.ShapeDtypeStruct((t, w), BF16), jax.ShapeDtypeStruct((t, w), F32), jax.ShapeDtypeStruct((t, w), F32)],
        scratch_shapes=[_chunk_scratch(tm, w)],
        compiler_params=_params("arbitrary"),
    )(*outs, *lses)


def _att_bwd_prep(datt, att, lse):
    t, w = datt.shape
    tm = min(512, t)

    def body(da_ref, at_ref, l_ref, *rest):
        outs, dbuf, sbuf = rest[:-2], rest[-2], rest[-1]
        dav = da_ref[...]
        prod = dav * at_ref[...]
        lane = lax.broadcasted_iota(jnp.int32, (tm, LANE), 1)
        for k in range(w // LANE):
            cols = slice(k * LANE, (k + 1) * LANE)
            dbuf[k] = dav[:, cols]
            delta = jnp.concatenate(
                [jnp.broadcast_to(jnp.sum(prod[:, k * LANE + e * ATT_DIM:k * LANE + (e + 1) * ATT_DIM], axis=-1, keepdims=True),
                                  (tm, ATT_DIM)) for e in range(LANE // ATT_DIM)], axis=1)
            sbuf[k] = jnp.where(lane % ATT_DIM < STAT_LANES, l_ref[:, cols], delta)
        for k, dil in enumerate(DILATIONS):
            _regroup_store(dbuf, outs[2 * k], dil)
            _regroup_store(sbuf, outs[2 * k + 1], dil)

    tile = pl.BlockSpec((tm, w), lambda i: (i, 0))
    res = pl.pallas_call(
        body, name="att_bwd_prep", grid=(t // tm,),
        in_specs=[tile] * 3,
        out_specs=[_regrouped_spec(tm, dil, w) for dil in DILATIONS for _ in range(2)],
        out_shape=[jax.ShapeDtypeStruct((dil, t // dil, w), dt) for dil in DILATIONS for dt in (BF16, F32)],
        scratch_shapes=[_chunk_scratch(tm, w)] * 2,
        compiler_params=_params("arbitrary"),
    )(datt, att, lse)
    return [(res[2 * k], res[2 * k + 1]) for k in range(len(DILATIONS))]


def _att_bwd(ua, da, stat, dil, rides=None):
    sub = ua.shape[1]
    _, tq, nq, qb = _att_tiles(sub * dil, dil)
    scale = ATT_DIM ** -0.5

    def body(q_ref, kc_ref, kp_ref, vc_ref, vp_ref, da_ref, st_ref, dq_ref, dk_ref, dv_ref, kx, vx, dkx, dvx, ck, cv):
        step = pl.program_id(2)
        tile = nq - 1 - step

        @pl.when(step == 0)
        def _():
            ck[...] = jnp.zeros_like(ck)
            cv[...] = jnp.zeros_like(cv)

        kx[0:ATT_BLOCK, :] = kp_ref[...]
        kx[ATT_BLOCK:, :] = kc_ref[...]
        vx[0:ATT_BLOCK, :] = vp_ref[...]
        vx[ATT_BLOCK:, :] = vc_ref[...]
        dkx[...] = jnp.zeros_like(dkx)
        dvx[...] = jnp.zeros_like(dvx)
        band, cur_cols = _band_mask()
        for b in range(qb):
            rows = slice(b * ATT_BLOCK, (b + 1) * ATT_BLOCK)
            keys = slice(b * ATT_BLOCK, (b + 2) * ATT_BLOCK)
            mask = band if b > 0 else band & (cur_cols | (tile > 0))
            qv = q_ref[rows, :] * jnp.asarray(scale, BF16)
            kv, vv = kx[keys, :], vx[keys, :]
            dab = da_ref[rows, :]
            lv = st_ref[rows, :]
            dqs, dks, dvs = [], [], []
            for e in range(LANE // ATT_DIM):
                hs = slice(e * ATT_DIM, (e + 1) * ATT_DIM)
                delta = lv[:, e * ATT_DIM + STAT_LANES:e * ATT_DIM + STAT_LANES + 1]
                s = _dot(qv[:, hs], kv[:, hs], NT)
                p = jnp.where(mask, jnp.exp(s - lv[:, e * ATT_DIM:e * ATT_DIM + 1]), 0.0)
                ds = (p * (_dot(dab[:, hs], vv[:, hs], NT) - delta)).astype(BF16)
                dqs.append(_dot(ds, kv[:, hs]) * scale)
                dks.append(_dot(ds, qv[:, hs], TN))
                dvs.append(_dot(p.astype(BF16), dab[:, hs], TN))
            dq_ref[rows, :] = jnp.concatenate(dqs, axis=1)
            dkx[keys, :] += jnp.concatenate(dks, axis=1)
            dvx[keys, :] += jnp.concatenate(dvs, axis=1)
        dkx[tq:, :] += ck[...]
        dvx[tq:, :] += cv[...]
        dk_ref[...] = dkx[ATT_BLOCK:, :]
        dv_ref[...] = dvx[ATT_BLOCK:, :]
        ck[...] = dkx[0:ATT_BLOCK, :]
        cv[...] = dvx[0:ATT_BLOCK, :]

    ti = lambda i: nq - 1 - i
    out = pl.BlockSpec((None, tq, LANE), lambda g, p, i: (g, ti(i), p))
    shape = jax.ShapeDtypeStruct((dil, sub, ATT_WIDTH), F32)
    return _pallas(
        body, rides, name=f"att_bwd_d{dil}", grid=(dil, PAIRS, nq),
        in_specs=_att_in_specs(tq, qb, ti) + [out, out],
        out_specs=[out, out, out], out_shape=[shape] * 3,
        scratch_shapes=[pltpu.VMEM((tq + ATT_BLOCK, LANE), BF16)] * 2 + [pltpu.VMEM((tq + ATT_BLOCK, LANE), F32)] * 2
        + [pltpu.VMEM((ATT_BLOCK, LANE), F32)] * 2,
        sem=("arbitrary", "arbitrary", "arbitrary"), args=[ua, ua, ua, ua, ua, da, stat])


def _att_bwd_sum(parts, t):
    w = ATT_WIDTH
    tm = min(512, t)
    nk = len(parts[0])

    def body(*refs):
        ins, outs, buf = refs[:-nk - 1], refs[-nk - 1:-1], refs[-1]
        for k in range(nk):
            acc = None
            for b, dil in enumerate(DILATIONS):
                rows = _natural_rows(ins[b * nk + k], dil, buf)
                acc = rows if acc is None else acc + rows
            outs[k][...] = acc.astype(BF16)

    tile = pl.BlockSpec((tm, w), lambda i: (i, 0))
    return pl.pallas_call(
        body, name="att_bwd_sum", grid=(t // tm,),
        in_specs=[_regrouped_spec(tm, dil, w) for dil in DILATIONS for _ in range(nk)], out_specs=[tile] * nk,
        out_shape=[jax.ShapeDtypeStruct((t, w), BF16)] * nk,
        scratch_shapes=[_chunk_scratch(tm, w)],
        compiler_params=_params("arbitrary"),
    )(*[a for p in parts for a in p])


def _loss_bwd(h, gain, target):
    t, d = h.shape
    tm = min(512, t)

    def body(h_ref, gain_ref, tg_ref, loss_ref, dh_ref, dgain_ref):
        @pl.when(pl.program_id(0) == 0)
        def _():
            loss_ref[...] = jnp.zeros_like(loss_ref)
            dgain_ref[...] = jnp.zeros_like(dgain_ref)

        hv = h_ref[...]
        r = lax.rsqrt(jnp.mean(hv * hv, axis=-1, keepdims=True) + NORM_EPS)
        xh = hv * r
        err = xh * gain_ref[...] - tg_ref[...]
        sq = _rows8(jnp.square(err))
        loss_ref[...] += 0.5 * functools.reduce(jnp.add, [sq[:, k * LANE:(k + 1) * LANE] for k in range(d // LANE)]) / d
        dy = err / d
        dgain_ref[...] += _rows8(dy * xh)
        dxh = dy * gain_ref[...]
        dh_ref[...] = r * (dxh - xh * jnp.mean(dxh * xh, axis=-1, keepdims=True))

    tile = pl.BlockSpec((tm, d), lambda i: (i, 0))
    return pl.pallas_call(
        body, name="loss_bwd", grid=(t // tm,),
        in_specs=[tile, pl.BlockSpec((1, d), lambda i: (0, 0)), tile],
        out_specs=[pl.BlockSpec((8, LANE), lambda i: (0, 0)), tile, pl.BlockSpec((8, d), lambda i: (0, 0))],
        out_shape=[jax.ShapeDtypeStruct((8, LANE), F32), jax.ShapeDtypeStruct((t, d), F32), jax.ShapeDtypeStruct((8, d), F32)],
        compiler_params=_params("arbitrary"),
    )(h, gain, target)


class _Reduction:
    def __init__(self, place, names, grads):
        self.place, self.names, self.grads = place, names, grads

    def pair(self):
        return _pair_ride(self.grads)

    def chips(self, got):
        self.got = got
        return _chip_ride([_pair_sum(self.place, g, r, f"pair_sum_{n}") for g, r, n in zip(self.grads, got, self.names)])

    def halves(self, others):
        return [_chip_sum(self.place, g, r, o, f"chip_sum_{n}")
                for g, r, o, n in zip(self.grads, self.got, others, self.names)]


def _step(x, target, gains, w, place=None):
    t = x.shape[0]
    ex = place is not None
    g_ffn1, g_mix, g_ret, g_ffn2, g_fin = gains
    w = list(w)
    tabs = _retention_tables(t)
    red = lambda names, grads: _Reduction(place, names, grads) if ex else None
    ride = lambda r: [r] if ex else None

    if ex:
        w[0:3] = _run(_gather_ride(w[0:3]), "gather_ffn1_weights")
    (h1, xn1, ga1, ua1), rest = _ffn_fwd(x, g_ffn1, *w[0:3], "ffn1_fwd", ride(_gather_ride(w[3:])) if ex else None)
    if ex:
        w[3:] = rest[0]
    wg1, wu1, wd1, win, wo, wg2, wu2, wd2 = w
    wo2 = wo.reshape(wo.shape[0] * wo.shape[1], wo.shape[2])
    xnm, u, *uas = _inproj_fwd(h1, g_mix, win)
    raw, mix_r = _ret_fwd(u, g_ret, tabs)
    branches = [_att_fwd(ua, dil) for ua, dil in zip(uas, DILATIONS)]
    mix_a, att, lse = _att_combine([b[0] for b in branches], [b[1] for b in branches], t)
    h2 = _outproj_fwd(h1, mix_r, mix_a, wo2)
    (h3, xn2, ga2, ua2), _ = _ffn_fwd(h2, g_ffn2, wg2, wu2, wd2, "ffn2_fwd")
    loss_p, dh3, dg_fin = _loss_bwd(h3, g_fin, target)

    (dwd2,), _ = _ffn_wgrad_down(ga2, ua2, dh3, "ffn2_wgrad_down")
    r_d2 = red(["ffn2_w_down"], [dwd2])
    (dh2, dga2, dua2, dg_ffn2), e = _ffn_bwd_data(dh3, h2, g_ffn2, ga2, ua2, wg2, wu2, wd2, "ffn2_bwd",
                                                  ex and [r_d2.pair()])
    (dwg2, dwu2), e = _ffn_wgrad_gu(xn2, dga2, dua2, "ffn2_wgrad_gu", ex and [r_d2.chips(e[0])])
    r_gu2 = red(["ffn2_w_gate", "ffn2_w_up"], [dwg2, dwu2])
    (dmix_r, dmix_a), e = _outproj_bwd(dh2, wo2, ex and [r_gu2.pair(), _finish_ride(r_d2.halves(e[0]))])
    if ex:
        got_gu2, (dwd2,) = e
    hw = RET_WIDTH // (wo.shape[1])
    dwo = jnp.concatenate([_tn_matmul(mix_r, dh2, dh2.shape[1], "wo_grad_r").reshape(hw, wo.shape[1], wo.shape[2]),
                           _tn_matmul(mix_a, dh2, dh2.shape[1], "wo_grad_a").reshape(hw, wo.shape[1], wo.shape[2])])
    r_wo = red(["w_out"], [dwo])
    (dq_r, dgt_r, dret, dg_ret), e = _ret_bwd_q(dmix_r, raw, u, g_ret, tabs, ex and [r_gu2.chips(got_gu2)])
    (dk_r, dv_r), e = _ret_bwd_kv(dret, u, tabs, ex and [r_wo.pair(), _finish_ride(r_gu2.halves(e[0]))])
    if ex:
        got_wo, (dwg2, dwu2) = e
    prep = _att_bwd_prep(dmix_a, att, lse)
    p1, e = _att_bwd(uas[0], *prep[0], DILATIONS[0], ex and [r_wo.chips(got_wo)])
    p4, e = _att_bwd(uas[1], *prep[1], DILATIONS[1], ex and [_finish_ride(r_wo.halves(e[0]))])
    if ex:
        (dwo,), = e
    p16, _ = _att_bwd(uas[2], *prep[2], DILATIONS[2])
    dq_a, dk_a, dv_a = _att_bwd_sum([p1, p4, p16], t)
    dh1, du, dg_mix = _inproj_bwd([dq_r, dk_r, dv_r, dgt_r, dq_a, dk_a, dv_a], h1, g_mix, dh2, win)
    dwin = _tn_matmul(xnm, du, win.shape[2], "win_grad")
    r_in = red(["w_in"], [dwin])
    (dwd1,), e = _ffn_wgrad_down(ga1, ua1, dh1, "ffn1_wgrad_down", ex and [r_in.pair()])
    r_d1 = red(["ffn1_w_down"], [dwd1])
    (dx, dga1, dua1, dg_ffn1), e = _ffn_bwd_data(dh1, x, g_ffn1, ga1, ua1, wg1, wu1, wd1, "ffn1_bwd",
                                                  ex and [r_in.chips(e[0]), r_d1.pair()])
    (dwg1, dwu1), e = _ffn_wgrad_gu(xn1, dga1, dua1, "ffn1_wgrad_gu",
                                    ex and [_finish_ride(r_in.halves(e[0])), r_d1.chips(e[1])])
    gain_parts = [dg_ffn1, dg_mix, dg_ret, dg_ffn2, dg_fin]
    if not ex:
        return loss_p, dx, [dwg1, dwu1, dwd1, dwin, dwo, dwg2, dwu2, dwd2], gain_parts
    (dwin,), oth_d1 = e
    r_gu1 = red(["ffn1_w_gate", "ffn1_w_up"], [dwg1, dwu1])
    got = _run(r_gu1.pair(), "pair_exchange_ffn1_gate_up")
    oth = _run(r_gu1.chips(got), "chip_exchange_ffn1_gate_up")
    last = r_gu1.halves(oth) + r_d1.halves(oth_d1)
    dwg1, dwu1, dwd1, gall = _run(_finish_ride(last, _pack_gains(gain_parts, x.shape[1])), "finish_exchange_ffn1")
    return loss_p, dx, [dwg1, dwu1, dwd1, dwin, dwo, dwg2, dwu2, dwd2], gall


N_DEV = 8
GAIN_ROWS = 8


def _place():
    x, y, c = lax.axis_index("x"), lax.axis_index("y"), lax.axis_index("c")
    chips = [(1 - x, y), (x, 1 - y), (1 - x, 1 - y)]
    return x, y, c, chips


def _hbm_specs(n):
    return [pl.BlockSpec(memory_space=pl.ANY)] * n


def _place_shard(place, w, name):
    r, cols = w.shape
    tr = r // 4

    def body(place_ref, w_ref, o_ref):
        o_ref[...] = w_ref[...].astype(BF16)

    return pl.pallas_call(
        body, name=name,
        grid_spec=pltpu.PrefetchScalarGridSpec(
            num_scalar_prefetch=1, grid=(r // tr,),
            in_specs=[pl.BlockSpec((tr, cols), lambda i, pr: (i, 0))],
            out_specs=pl.BlockSpec((None, tr, cols), lambda i, pr: (pr[0], i, 0))),
        out_shape=jax.ShapeDtypeStruct((N_SHARD, r, cols), BF16),
        compiler_params=_params("arbitrary"),
    )(place, w)


def _gather_ride(bufs):
    na = len(bufs)

    def legs(outs, sems):
        send_sem, recv_sem, fsend_sem, frecv_sem = sems
        x, y, c, chips = _place()

        def half(a, idx, which):
            hr = outs[a].shape[1] // 2
            return outs[a].at[idx, pl.ds(which * hr, hr)]

        def ici(a, j, idx):
            px, py = chips[j]
            return pltpu.make_async_remote_copy(
                src_ref=half(a, idx, c), dst_ref=half(a, idx, c),
                send_sem=send_sem.at[a, j], recv_sem=recv_sem.at[a, j], device_id=(px, py, c), device_id_type=MESH)

        def d2d(a, j, idx, which):
            return pltpu.make_async_remote_copy(
                src_ref=half(a, idx, which), dst_ref=half(a, idx, which),
                send_sem=fsend_sem.at[a, j], recv_sem=frecv_sem.at[a, j], device_id=(x, y, 1 - c), device_id_type=MESH)

        return 2 * x + y, c, chips, ici, d2d

    def start(ins, outs, sems):
        me, _, _, ici, _ = legs(outs, sems)
        for a in range(na):
            for j in range(3):
                ici(a, j, me).start()

    def finish(ins, outs, sems):
        me, c, chips, ici, d2d = legs(outs, sems)
        passed = []
        for a in range(na):
            for j, (px, py) in enumerate(chips):
                ici(a, j, 2 * px + py).wait_recv()
                cp = d2d(a, j, 2 * px + py, c)
                cp.start()
                passed.append(cp)
        for a in range(na):
            for j, (px, py) in enumerate(chips):
                d2d(a, j, 2 * px + py, 1 - c).wait_recv()
        for a in range(na):
            for j in range(3):
                ici(a, j, me).wait_send()
        for cp in passed:
            cp.wait_send()

    return _Ride(bufs, [jax.ShapeDtypeStruct(b.shape, b.dtype) for b in bufs], [pltpu.SemaphoreType.DMA((na, 3))] * 4,
                 start, finish, {a: a for a in range(na)})


def _pair_ride(grads):
    na = len(grads)

    def copies(ins, outs, sems):
        send_sem, recv_sem = sems
        x, y, c, _ = _place()
        res = []
        for a in range(na):
            hr = ins[a].shape[1] // 2
            res.append(pltpu.make_async_remote_copy(
                src_ref=ins[a].at[:, pl.ds((1 - c) * hr, hr)], dst_ref=outs[a],
                send_sem=send_sem.at[a], recv_sem=recv_sem.at[a], device_id=(x, y, 1 - c), device_id_type=MESH))
        return res

    def start(ins, outs, sems):
        for cp in copies(ins, outs, sems):
            cp.start()

    def finish(ins, outs, sems):
        for cp in copies(ins, outs, sems):
            cp.wait()

    return _Ride(grads, [jax.ShapeDtypeStruct((g.shape[0], g.shape[1] // 2, g.shape[2]), g.dtype) for g in grads],
                 [pltpu.SemaphoreType.DMA((na,))] * 2, start, finish)


def _chip_ride(sums):
    na = len(sums)

    def copies(ins, outs, sems):
        send_sem, recv_sem = sems
        x, y, c, chips = _place()
        res = []
        for a in range(na):
            for j, (px, py) in enumerate(chips):
                res.append(pltpu.make_async_remote_copy(
                    src_ref=ins[a].at[2 * px + py], dst_ref=outs[a].at[j],
                    send_sem=send_sem.at[a, j], recv_sem=recv_sem.at[a, j], device_id=(px, py, c), device_id_type=MESH))
        return res

    def start(ins, outs, sems):
        for cp in copies(ins, outs, sems):
            cp.start()

    def finish(ins, outs, sems):
        for cp in copies(ins, outs, sems):
            cp.wait()

    return _Ride(sums, [jax.ShapeDtypeStruct((3,) + s.shape[1:], s.dtype) for s in sums],
                 [pltpu.SemaphoreType.DMA((na, 3))] * 2, start, finish)


def _finish_ride(grads, gpack=None):
    na = len(grads)

    def halves(outs, sems, which):
        x, y, c, _ = _place()
        res = []
        for a in range(na):
            hr = outs[a].shape[0] // 2
            rows = outs[a].at[pl.ds((c if which == "mine" else 1 - c) * hr, hr)]
            res.append(pltpu.make_async_remote_copy(
                src_ref=rows, dst_ref=rows, send_sem=sems[0].at[a], recv_sem=sems[1].at[a],
                device_id=(x, y, 1 - c), device_id_type=MESH))
        return res

    def gains(ins, outs, sems):
        x, y, c, _ = _place()
        dev = 4 * x + 2 * y + c
        g_in, g_out = ins[na], outs[na]
        own = pltpu.make_async_copy(g_in, g_out.at[dev], sems[2])
        sends, lands = [], []
        for k in range(N_DEV - 1):
            bx, by, bc = (k + 1) // 4, ((k + 1) // 2) % 2, (k + 1) % 2
            peer = (jnp.bitwise_xor(x, bx), jnp.bitwise_xor(y, by), jnp.bitwise_xor(c, bc))
            sends.append(pltpu.make_async_remote_copy(
                src_ref=g_in, dst_ref=g_out.at[dev], send_sem=sems[3].at[k], recv_sem=sems[4].at[k],
                device_id=peer, device_id_type=MESH))
            slot = g_out.at[jnp.bitwise_xor(dev, k + 1)]
            lands.append(pltpu.make_async_remote_copy(
                src_ref=slot, dst_ref=slot, send_sem=sems[3].at[k], recv_sem=sems[4].at[k],
                device_id=peer, device_id_type=MESH))
        return own, sends, lands

    def start(ins, outs, sems):
        for cp in halves(outs, sems, "mine"):
            cp.start()
        if gpack is not None:
            own, sends, _ = gains(ins, outs, sems)
            own.start()
            for cp in sends:
                cp.start()

    def finish(ins, outs, sems):
        for cp in halves(outs, sems, "sibling's"):
            cp.wait_recv()
        if gpack is not None:
            own, sends, lands = gains(ins, outs, sems)
            for cp in lands:
                cp.wait_recv()
            for cp in sends:
                cp.wait_send()
            own.wait()
        for cp in halves(outs, sems, "mine"):
            cp.wait_send()

    shapes = [jax.ShapeDtypeStruct(g.shape, g.dtype) for g in grads]
    sems = [pltpu.SemaphoreType.DMA((na,))] * 2
    if gpack is None:
        return _Ride(grads, shapes, sems, start, finish, {a: a for a in range(na)})
    return _Ride(list(grads) + [gpack], shapes + [jax.ShapeDtypeStruct((N_DEV,) + gpack.shape, gpack.dtype)],
                 sems + [pltpu.SemaphoreType.DMA, pltpu.SemaphoreType.DMA((N_DEV - 1,)), pltpu.SemaphoreType.DMA((N_DEV - 1,))],
                 start, finish, {a: a for a in range(na)})


def _pair_sum(place, grad, got, name):
    ns, r, cols = grad.shape
    hr = r // 2

    def body(place_ref, g_ref, r_ref, o_ref):
        o_ref[...] = (g_ref[...] + r_ref[...]).astype(BF16)

    return pl.pallas_call(
        body, name=name,
        grid_spec=pltpu.PrefetchScalarGridSpec(
            num_scalar_prefetch=1, grid=(ns,),
            in_specs=[pl.BlockSpec((None, hr, cols), lambda s, pr: (s, pr[1], 0)),
                      pl.BlockSpec((None, hr, cols), lambda s, pr: (s, 0, 0))],
            out_specs=pl.BlockSpec((None, hr, cols), lambda s, pr: (s, 0, 0))),
        out_shape=jax.ShapeDtypeStruct((ns, hr, cols), BF16),
        compiler_params=_params("arbitrary"),
    )(place, grad, got)


def _chip_sum(place, grad, got, others, name):
    ns, r, cols = grad.shape
    hr = r // 2
    nb = 2
    tr = hr // nb

    def body(place_ref, g_ref, r_ref, o3_ref, o_ref):
        acc = g_ref[...] + r_ref[...]
        for j in range(3):
            acc = acc + o3_ref[j].astype(F32)
        o_ref[...] = acc

    return pl.pallas_call(
        body, name=name,
        grid_spec=pltpu.PrefetchScalarGridSpec(
            num_scalar_prefetch=1, grid=(nb,),
            in_specs=[pl.BlockSpec((None, tr, cols), lambda i, pr: (pr[0], pr[1] * nb + i, 0)),
                      pl.BlockSpec((None, tr, cols), lambda i, pr: (pr[0], i, 0)),
                      pl.BlockSpec((3, tr, cols), lambda i, pr: (0, i, 0))],
            out_specs=pl.BlockSpec((tr, cols), lambda i, pr: (pr[1] * nb + i, 0))),
        out_shape=jax.ShapeDtypeStruct((r, cols), F32),
        compiler_params=_params("arbitrary"),
    )(place, grad, got, others)


def _pack_gains(parts, d):
    def body(*refs):
        ins, o_ref = refs[:-1], refs[-1]
        o_ref[...] = jnp.zeros_like(o_ref)
        for k, r in enumerate(ins):
            o_ref[k:k + 1, 0:r.shape[1]] = jnp.sum(r[...], axis=0, keepdims=True)

    return pl.pallas_call(
        body, name="pack_gains", out_shape=jax.ShapeDtypeStruct((GAIN_ROWS, d), F32),
    )(*parts)


def _adamw_math(w, g, m, v):
    m = ADAM_B1 * m + (1.0 - ADAM_B1) * g
    v = ADAM_B2 * v + (1.0 - ADAM_B2) * jnp.square(g)
    m_hat = m / (1.0 - ADAM_B1 ** ADAM_STEP)
    v_hat = v / (1.0 - ADAM_B2 ** ADAM_STEP)
    return -ADAM_LR * (m_hat / (jnp.sqrt(v_hat) + ADAM_EPS) + ADAM_WD * w), m, v


def _adamw(w, g, m, v, name):
    r, cols = w.shape
    tr = r // 4 if (r // 4) % 8 == 0 else r

    def body(w_ref, g_ref, m_ref, v_ref, d_ref, nm_ref, nv_ref):
        d_ref[...], nm_ref[...], nv_ref[...] = _adamw_math(w_ref[...], g_ref[...], m_ref[...], v_ref[...])

    tile = pl.BlockSpec((tr, cols), lambda i: (i, 0))
    return pl.pallas_call(
        body, name=name, grid=(r // tr,), in_specs=[tile] * 4, out_specs=[tile] * 3,
        out_shape=[jax.ShapeDtypeStruct((r, cols), F32)] * 3,
        compiler_params=_params("arbitrary"),
    )(w, g, m, v)


def _adamw_gain(gall, row, w, m, v, name):
    n = w.shape[1]

    def body(ga_ref, w_ref, m_ref, v_ref, g_ref, d_ref, nm_ref, nv_ref):
        g = ga_ref[0, row:row + 1, 0:n]
        for k in range(1, N_DEV):
            g = g + ga_ref[k, row:row + 1, 0:n]
        g_ref[...] = g
        d_ref[...], nm_ref[...], nv_ref[...] = _adamw_math(w_ref[...], g, m_ref[...], v_ref[...])

    return pl.pallas_call(
        body, name=name, out_shape=[jax.ShapeDtypeStruct((1, n), F32)] * 4,
    )(gall, w, m, v)


def kernel(x, norm_ffn1, ffn1_w_gate, ffn1_w_up, ffn1_w_down, norm_mix, w_in, ret_norm_gain, w_out, norm_ffn2, ffn2_w_gate, ffn2_w_up, ffn2_w_down, norm_final, loss_target, m_norm_ffn1, m_ffn1_w_gate, m_ffn1_w_up, m_ffn1_w_down, m_norm_mix, m_w_in, m_ret_norm_gain, m_w_out, m_norm_ffn2, m_ffn2_w_gate, m_ffn2_w_up, m_ffn2_w_down, m_norm_final, v_norm_ffn1, v_ffn1_w_gate, v_ffn1_w_up, v_ffn1_w_down, v_norm_mix, v_w_in, v_ret_norm_gain, v_w_out, v_norm_ffn2, v_ffn2_w_gate, v_ffn2_w_up, v_ffn2_w_down, v_norm_final):
    d = x.shape[-1]
    mats = [ffn1_w_gate, ffn1_w_up, ffn1_w_down, w_in, w_out, ffn2_w_gate, ffn2_w_up, ffn2_w_down]
    mats_m = [m_ffn1_w_gate, m_ffn1_w_up, m_ffn1_w_down, m_w_in, m_w_out, m_ffn2_w_gate, m_ffn2_w_up, m_ffn2_w_down]
    mats_v = [v_ffn1_w_gate, v_ffn1_w_up, v_ffn1_w_down, v_w_in, v_w_out, v_ffn2_w_gate, v_ffn2_w_up, v_ffn2_w_down]
    mat_names = ["ffn1_w_gate", "ffn1_w_up", "ffn1_w_down", "w_in", "w_out", "ffn2_w_gate", "ffn2_w_up", "ffn2_w_down"]
    gains = [norm_ffn1, norm_mix, ret_norm_gain, norm_ffn2, norm_final.reshape(1, d)]
    gains_m = [m_norm_ffn1, m_norm_mix, m_ret_norm_gain, m_norm_ffn2, m_norm_final.reshape(1, d)]
    gains_v = [v_norm_ffn1, v_norm_mix, v_ret_norm_gain, v_norm_ffn2, v_norm_final.reshape(1, d)]
    gain_names = ["norm_ffn1", "norm_mix", "ret_norm_gain", "norm_ffn2", "norm_final"]

    shards = [w[0] for w in mats]
    place = jnp.stack([2 * lax.axis_index("x") + lax.axis_index("y"), lax.axis_index("c")]).astype(jnp.int32)
    placed = [_place_shard(place, s, f"place_{n}") for s, n in zip(shards, mat_names)]
    loss_p, dx, shard_grads, gall = _step(x[0], loss_target[0], gains, placed, place)

    out_g, out_d, out_m, out_v = {}, {}, {}, {}
    for n, w, g, m, v in zip(mat_names, shards, shard_grads, mats_m, mats_v):
        dl, nm, nv = _adamw(w, g, m[0], v[0], f"adamw_{n}")
        out_g[n], out_d[n], out_m[n], out_v[n] = g[None], dl[None], nm[None], nv[None]
    for row, (n, w, m, v) in enumerate(zip(gain_names, gains, gains_m, gains_v)):
        res = _adamw_gain(gall, row, w, m, v, f"adamw_{n}")
        shape = (d,) if n == "norm_final" else w.shape
        out_g[n], out_d[n], out_m[n], out_v[n] = [r.reshape(shape) for r in res]

    loss = lax.psum(jnp.sum(loss_p), ("x", "y", "c"))
    order = ["norm_ffn1", "ffn1_w_gate", "ffn1_w_up", "ffn1_w_down", "norm_mix", "w_in", "ret_norm_gain", "w_out",
             "norm_ffn2", "ffn2_w_gate", "ffn2_w_up", "ffn2_w_down", "norm_final"]
    return (loss, dx[None], *[out_g[n] for n in order], *[out_d[n] for n in order],
            *[out_m[n] for n in order], *[out_v[n] for n in order])
```

```python
import functools
import math

import jax
import jax.numpy as jnp
from jax import lax
from jax.experimental import pallas as pl
from jax.experimental.pallas import tpu as pltpu

F32 = jnp.float32
BF16 = jnp.bfloat16
MESH = pl.DeviceIdType.MESH

NORM_EPS = 1e-6
GN_EPS = 1e-6
ROPE_BASE = 10000.0
RET_HEADS = 4
RET_DIM = 128
RET_WIDTH = 512
RET_CHUNK = 128
ATT_HEADS = 8
ATT_DIM = 64
ATT_WIDTH = 512
ATT_BLOCK = 128
DILATIONS = (1, 4, 16)
IN_COLS = 4 * RET_WIDTH + 3 * ATT_WIDTH
LANE = 128
N_SHARD = 4
ADAM_LR, ADAM_B1, ADAM_B2, ADAM_EPS, ADAM_WD, ADAM_STEP = 0.001, 0.9, 0.999, 1e-08, 0.01, 10

V7X_VMEM_BYTES = 64 * 1024 * 1024
VMEM_LIMIT = V7X_VMEM_BYTES - 8 * 1024 * 1024

NT = (((1,), (1,)), ((), ()))
TN = (((0,), (0,)), ((), ()))


def _params(*sem):
    return pltpu.CompilerParams(dimension_semantics=sem, vmem_limit_bytes=VMEM_LIMIT)


def _dot(a, b, dims=None):
    if dims is None:
        return jnp.dot(a, b, preferred_element_type=F32)
    return lax.dot_general(a, b, dims, preferred_element_type=F32)


def _sigmoid(x):
    return 1.0 / (1.0 + jnp.exp(-x))


def _load_weights(pairs, sems):
    copies = [pltpu.make_async_copy(src, dst, sems.at[k]) for k, (src, dst) in enumerate(pairs)]
    for cp in copies:
        cp.start()
    for cp in copies:
        cp.wait()


def _rows8(v):
    r, c = v.shape
    return v.reshape(r // 8, 8, c).sum(axis=0)


class _Ride:
    def __init__(self, inputs, out_shapes, sems, start, finish, aliases=None):
        self.inputs, self.out_shapes, self.sems = list(inputs), list(out_shapes), list(sems)
        self.start, self.finish, self.aliases = start, finish, dict(aliases or {})


def _pallas(body, rides, *, name, in_specs, out_specs, out_shape, args, grid=(), scratch_shapes=(), sem=()):
    rides = [r for r in (rides or []) if r is not None]
    n_in, n_out, n_scr = len(args), len(out_shape), len(scratch_shapes)
    hbm = pl.BlockSpec(memory_space=pl.ANY)
    r_in = [a for r in rides for a in r.inputs]
    r_out = [s for r in rides for s in r.out_shapes]
    r_sem = [s for r in rides for s in r.sems]
    aliases, spans, ki, ko, ks = {}, [], 0, 0, 0
    for r in rides:
        aliases.update({n_in + ki + i: n_out + ko + o for i, o in r.aliases.items()})
        spans.append((ki, ko, ks))
        ki, ko, ks = ki + len(r.inputs), ko + len(r.out_shapes), ks + len(r.sems)

    def wrapped(*refs):
        ins, rin = refs[:n_in], refs[n_in:n_in + len(r_in)]
        o0 = n_in + len(r_in)
        outs, rout = refs[o0:o0 + n_out], refs[o0 + n_out:o0 + n_out + len(r_out)]
        s0 = o0 + n_out + len(r_out)
        scr, rsem = refs[s0:s0 + n_scr], refs[s0 + n_scr:]
        part = lambda r, k: (rin[spans[k][0]:spans[k][0] + len(r.inputs)], rout[spans[k][1]:spans[k][1] + len(r.out_shapes)],
                             rsem[spans[k][2]:spans[k][2] + len(r.sems)])
        first = functools.reduce(jnp.logical_and, [pl.program_id(k) == 0 for k in range(len(grid))], True)
        last = functools.reduce(jnp.logical_and, [pl.program_id(k) == grid[k] - 1 for k in range(len(grid))], True)
        if rides:
            @pl.when(first)
            def _():
                for k, r in enumerate(rides):
                    r.start(*part(r, k))

        body(*ins, *outs, *scr)
        if rides:
            @pl.when(last)
            def _():
                for k, r in enumerate(rides):
                    r.finish(*part(r, k))

    res = pl.pallas_call(
        wrapped, name=name, grid=grid,
        in_specs=list(in_specs) + [hbm] * len(r_in), out_specs=list(out_specs) + [hbm] * len(r_out),
        out_shape=list(out_shape) + r_out, input_output_aliases=aliases,
        scratch_shapes=list(scratch_shapes) + r_sem,
        compiler_params=pltpu.CompilerParams(dimension_semantics=sem, vmem_limit_bytes=VMEM_LIMIT) if grid else None,
    )(*args, *r_in)
    extras = [list(res[n_out + ko:n_out + ko + len(r.out_shapes)]) for r, (_, ko, _) in zip(rides, spans)]
    return list(res[:n_out]), extras


def _run(ride, name):
    def body(*refs):
        n_in, n_out = len(ride.inputs), len(ride.out_shapes)
        parts = refs[:n_in], refs[n_in:n_in + n_out], refs[n_in + n_out:]
        ride.start(*parts)
        ride.finish(*parts)

    hbm = pl.BlockSpec(memory_space=pl.ANY)
    return list(pl.pallas_call(
        body, name=name, in_specs=[hbm] * len(ride.inputs), out_specs=[hbm] * len(ride.out_shapes),
        out_shape=ride.out_shapes, input_output_aliases=ride.aliases, scratch_shapes=ride.sems,
    )(*ride.inputs))


def _ffn_fwd(x, gain, wg, wu, wd, name, rides=None):
    t, d = x.shape
    ns, _, fs = wg.shape
    tm = min(256, t)

    def body(x_ref, gain_ref, wg_hbm, wu_hbm, wd_hbm, h_ref, xn_ref, g_ref, u_ref, a_ref, wg_v, wu_v, wd_v, sems):
        @pl.when(pl.program_id(0) == 0)
        def _():
            _load_weights([(wg_hbm, wg_v), (wu_hbm, wu_v), (wd_hbm, wd_v)], sems)

        xv = x_ref[...]
        r = lax.rsqrt(jnp.mean(xv * xv, axis=-1, keepdims=True) + NORM_EPS)
        xn = (xv * r * gain_ref[...]).astype(BF16)
        xn_ref[...] = xn
        acc = jnp.zeros((tm, d), F32)
        for j in range(ns):
            g = _dot(xn, wg_v[j])
            u = _dot(xn, wu_v[j])
            g_ref[j] = g.astype(BF16)
            u_ref[j] = u.astype(BF16)
            a = (g * _sigmoid(g) * u).astype(BF16)
            a_ref[j] = a
            acc = acc + _dot(a, wd_v[j])
        h_ref[...] = xv + 0.5 * acc

    hbm = pl.BlockSpec(memory_space=pl.ANY)
    hid = pl.BlockSpec((ns, tm, fs), lambda i: (0, i, 0))
    return _pallas(
        body, rides, name=name, grid=(t // tm,),
        in_specs=[pl.BlockSpec((tm, d), lambda i: (i, 0)), pl.BlockSpec((1, d), lambda i: (0, 0)), hbm, hbm, hbm],
        out_specs=[pl.BlockSpec((tm, d), lambda i: (i, 0)), pl.BlockSpec((tm, d), lambda i: (i, 0)), hid, hid, hid],
        out_shape=[jax.ShapeDtypeStruct((t, d), F32), jax.ShapeDtypeStruct((t, d), BF16)]
        + [jax.ShapeDtypeStruct((ns, t, fs), BF16)] * 3,
        scratch_shapes=[pltpu.VMEM(wg.shape, BF16), pltpu.VMEM(wu.shape, BF16), pltpu.VMEM(wd.shape, BF16),
                        pltpu.SemaphoreType.DMA((3,))],
        sem=("arbitrary",), args=[x, gain, wg, wu, wd])


def _ffn_bwd_data(dy, x, gain, g, u, wg, wu, wd, name, rides=None):
    t, d = x.shape
    ns, _, fs = wg.shape
    tm = min(256, t)

    def body(dy_ref, x_ref, gain_ref, g_ref, u_ref, wg_hbm, wu_hbm, wd_hbm, dx_ref, dg_ref, du_ref, dgain_ref,
             wg_v, wu_v, wd_v, sems):
        @pl.when(pl.program_id(0) == 0)
        def _():
            _load_weights([(wg_hbm, wg_v), (wu_hbm, wu_v), (wd_hbm, wd_v)], sems)
            dgain_ref[...] = jnp.zeros_like(dgain_ref)

        dyv = dy_ref[...]
        dyh = (0.5 * dyv).astype(BF16)
        dxn = jnp.zeros((tm, d), F32)
        for j in range(ns):
            da = _dot(dyh, wd_v[j], NT)
            gj = g_ref[j].astype(F32)
            uj = u_ref[j].astype(F32)
            sig = _sigmoid(gj)
            dgj = (da * uj * (sig * (1.0 + gj * (1.0 - sig)))).astype(BF16)
            duj = (da * (gj * sig)).astype(BF16)
            dg_ref[j] = dgj
            du_ref[j] = duj
            dxn = dxn + _dot(dgj, wg_v[j], NT) + _dot(duj, wu_v[j], NT)
        xv = x_ref[...]
        r = lax.rsqrt(jnp.mean(xv * xv, axis=-1, keepdims=True) + NORM_EPS)
        xh = xv * r
        dgain_ref[...] += _rows8(dxn * xh)
        dxh = dxn * gain_ref[...]
        dx_ref[...] = dyv + r * (dxh - xh * jnp.mean(dxh * xh, axis=-1, keepdims=True))

    hbm = pl.BlockSpec(memory_space=pl.ANY)
    tile = pl.BlockSpec((tm, d), lambda i: (i, 0))
    hid = pl.BlockSpec((ns, tm, fs), lambda i: (0, i, 0))
    return _pallas(
        body, rides, name=name, grid=(t // tm,),
        in_specs=[tile, tile, pl.BlockSpec((1, d), lambda i: (0, 0)), hid, hid, hbm, hbm, hbm],
        out_specs=[tile, hid, hid, pl.BlockSpec((8, d), lambda i: (0, 0))],
        out_shape=[jax.ShapeDtypeStruct((t, d), F32), jax.ShapeDtypeStruct((ns, t, fs), BF16),
                   jax.ShapeDtypeStruct((ns, t, fs), BF16), jax.ShapeDtypeStruct((8, d), F32)],
        scratch_shapes=[pltpu.VMEM(wg.shape, BF16), pltpu.VMEM(wu.shape, BF16), pltpu.VMEM(wd.shape, BF16),
                        pltpu.SemaphoreType.DMA((3,))],
        sem=("arbitrary",), args=[dy, x, gain, g, u, wg, wu, wd])


def _ffn_wgrad_down(a, dy, name, rides=None):
    t, d = dy.shape
    ns, _, fs = a.shape
    tk = min(1024, t)

    def body(dy_ref, a_ref, dwd_ref):
        @pl.when(pl.program_id(1) == 0)
        def _():
            dwd_ref[...] = jnp.zeros_like(dwd_ref)

        dwd_ref[...] += _dot(a_ref[...], (0.5 * dy_ref[...]).astype(BF16), TN)

    return _pallas(
        body, rides, name=name, grid=(ns, t // tk),
        in_specs=[pl.BlockSpec((tk, d), lambda j, k: (k, 0)), pl.BlockSpec((None, tk, fs), lambda j, k: (j, k, 0))],
        out_specs=[pl.BlockSpec((None, fs, d), lambda j, k: (j, 0, 0))],
        out_shape=[jax.ShapeDtypeStruct((ns, fs, d), F32)],
        sem=("arbitrary", "arbitrary"), args=[dy, a])


def _ffn_wgrad_gu(xn, dg, du, name, rides=None):
    t, d = xn.shape
    ns, _, fs = dg.shape
    tk = min(2048, t)

    def body(xn_ref, dg_ref, du_ref, dwg_ref, dwu_ref):
        @pl.when(pl.program_id(1) == 0)
        def _():
            dwg_ref[...] = jnp.zeros_like(dwg_ref)
            dwu_ref[...] = jnp.zeros_like(dwu_ref)

        xnv = xn_ref[...]
        dwg_ref[...] += _dot(xnv, dg_ref[...], TN)
        dwu_ref[...] += _dot(xnv, du_ref[...], TN)

    hid = pl.BlockSpec((None, tk, fs), lambda j, k: (j, k, 0))
    out = pl.BlockSpec((None, d, fs), lambda j, k: (j, 0, 0))
    return _pallas(
        body, rides, name=name, grid=(ns, t // tk),
        in_specs=[pl.BlockSpec((tk, d), lambda j, k: (k, 0)), hid, hid],
        out_specs=[out, out], out_shape=[jax.ShapeDtypeStruct((ns, d, fs), F32)] * 2,
        sem=("arbitrary", "arbitrary"), args=[xn, dg, du])


def _tn_matmul(a, b, bn, name):
    t, m = a.shape
    n = b.shape[1]
    tk = min(2048, t)

    def body(a_ref, b_ref, o_ref):
        @pl.when(pl.program_id(1) == 0)
        def _():
            o_ref[...] = jnp.zeros_like(o_ref)

        o_ref[...] += _dot(a_ref[...].astype(BF16), b_ref[...].astype(BF16), TN)

    return pl.pallas_call(
        body, name=name, grid=(n // bn, t // tk),
        in_specs=[pl.BlockSpec((tk, m), lambda j, k: (k, 0)), pl.BlockSpec((tk, bn), lambda j, k: (k, j))],
        out_specs=pl.BlockSpec((None, m, bn), lambda j, k: (j, 0, 0)),
        out_shape=jax.ShapeDtypeStruct((n // bn, m, bn), F32),
        compiler_params=_params("arbitrary", "arbitrary"),
    )(a, b)


def _chunk_scratch(tm, w):
    return pltpu.VMEM((w // LANE, tm, LANE), F32)


def _regroup_store(cbuf, out_ref, dil):
    n = out_ref.shape[1]
    for g in range(dil):
        for k in range(cbuf.shape[0]):
            rows = cbuf[k] if dil == 1 else cbuf[k, pl.ds(g, n, stride=dil), :]
            out_ref[g, :, k * LANE:(k + 1) * LANE] = rows.astype(out_ref.dtype)


def _natural_rows(ref, dil, cbuf):
    if dil == 1:
        return ref[0]
    n = ref.shape[1]
    for g in range(dil):
        for k in range(cbuf.shape[0]):
            cbuf[k, pl.ds(g, n, stride=dil), :] = ref[g, :, k * LANE:(k + 1) * LANE]
    return jnp.concatenate([cbuf[k] for k in range(cbuf.shape[0])], axis=1)


def _inproj_fwd(h, gain, win):
    t, d = h.shape
    ns, _, cs = win.shape
    tm = min(512, t)
    rw, aw = 4 * RET_WIDTH, 3 * ATT_WIDTH

    def body(h_ref, gain_ref, w_ref, xn_ref, ur_ref, *rest):
        a_refs, abuf = rest[:-1], rest[-1]
        hv = h_ref[...]
        r = lax.rsqrt(jnp.mean(hv * hv, axis=-1, keepdims=True) + NORM_EPS)
        xn = (hv * r * gain_ref[...]).astype(BF16)
        xn_ref[...] = xn
        for j in range(ns):
            res = _dot(xn, w_ref[j])
            for k in range(cs // LANE):
                chunk = j * (cs // LANE) + k
                piece = res[:, k * LANE:(k + 1) * LANE]
                if chunk < rw // LANE:
                    ur_ref[:, chunk * LANE:(chunk + 1) * LANE] = piece
                else:
                    abuf[chunk - rw // LANE] = piece
        for dil, a_ref in zip(DILATIONS, a_refs):
            _regroup_store(abuf, a_ref, dil)

    return pl.pallas_call(
        body, name="inproj_fwd", grid=(t // tm,),
        in_specs=[pl.BlockSpec((tm, d), lambda i: (i, 0)), pl.BlockSpec((1, d), lambda i: (0, 0)),
                  pl.BlockSpec(win.shape, lambda i: (0, 0, 0))],
        out_specs=[pl.BlockSpec((tm, d), lambda i: (i, 0)), pl.BlockSpec((tm, rw), lambda i: (i, 0))]
        + [pl.BlockSpec((dil, tm // dil, aw), lambda i: (0, i, 0)) for dil in DILATIONS],
        out_shape=[jax.ShapeDtypeStruct((t, d), BF16), jax.ShapeDtypeStruct((t, rw), F32)]
        + [jax.ShapeDtypeStruct((dil, t // dil, aw), BF16) for dil in DILATIONS],
        scratch_shapes=[_chunk_scratch(tm, aw)],
        compiler_params=_params("arbitrary"),
    )(h, gain, win)


def _inproj_bwd(pieces, h, gain, dres, win):
    t, d = h.shape
    ns, _, cs = win.shape
    pw = pieces[0].shape[1]
    tm = min(512, t)
    npc = len(pieces)

    def body(*refs):
        p_refs = refs[:npc]
        h_ref, gain_ref, dres_ref, w_ref, dh_ref, du_ref, dgain_ref = refs[npc:]

        @pl.when(pl.program_id(0) == 0)
        def _():
            dgain_ref[...] = jnp.zeros_like(dgain_ref)

        for k in range(npc):
            du_ref[:, k * pw:(k + 1) * pw] = p_refs[k][...]
        dxn = jnp.zeros((tm, d), F32)
        for j in range(ns):
            dxn = dxn + _dot(du_ref[:, j * cs:(j + 1) * cs], w_ref[j], NT)
        hv = h_ref[...]
        r = lax.rsqrt(jnp.mean(hv * hv, axis=-1, keepdims=True) + NORM_EPS)
        xh = hv * r
        dgain_ref[...] += _rows8(dxn * xh)
        dxh = dxn * gain_ref[...]
        dh_ref[...] = dres_ref[...] + r * (dxh - xh * jnp.mean(dxh * xh, axis=-1, keepdims=True))

    tile = pl.BlockSpec((tm, d), lambda i: (i, 0))
    return pl.pallas_call(
        body, name="inproj_bwd", grid=(t // tm,),
        in_specs=[pl.BlockSpec((tm, pw), lambda i: (i, 0))] * npc + [
            tile, pl.BlockSpec((1, d), lambda i: (0, 0)), tile, pl.BlockSpec(win.shape, lambda i: (0, 0, 0))],
        out_specs=[tile, pl.BlockSpec((tm, npc * pw), lambda i: (i, 0)), pl.BlockSpec((8, d), lambda i: (0, 0))],
        out_shape=[jax.ShapeDtypeStruct((t, d), F32), jax.ShapeDtypeStruct((t, npc * pw), BF16),
                   jax.ShapeDtypeStruct((8, d), F32)],
        compiler_params=_params("arbitrary"),
    )(*pieces, h, gain, dres, win)


def _outproj_fwd(h, mix_r, mix_a, wo):
    t, d = h.shape
    hw = mix_r.shape[1]
    tm = min(512, t)

    def body(h_ref, mr_ref, ma_ref, w_ref, o_ref):
        o_ref[...] = h_ref[...] + _dot(mr_ref[...], w_ref[0:hw, :]) + _dot(ma_ref[...], w_ref[hw:2 * hw, :])

    tile = pl.BlockSpec((tm, d), lambda i: (i, 0))
    half = pl.BlockSpec((tm, hw), lambda i: (i, 0))
    return pl.pallas_call(
        body, name="outproj_fwd", grid=(t // tm,),
        in_specs=[tile, half, half, pl.BlockSpec(wo.shape, lambda i: (0, 0))],
        out_specs=tile, out_shape=jax.ShapeDtypeStruct((t, d), F32),
        compiler_params=_params("arbitrary"),
    )(h, mix_r, mix_a, wo)


def _outproj_bwd(dh, wo, rides=None):
    t, d = dh.shape
    hw = wo.shape[0] // 2
    tm = min(512, t)

    def body(dh_ref, w_ref, dr_ref, da_ref):
        dhb = dh_ref[...].astype(BF16)
        dr_ref[...] = _dot(dhb, w_ref[0:hw, :], NT)
        da_ref[...] = _dot(dhb, w_ref[hw:2 * hw, :], NT)

    half = pl.BlockSpec((tm, hw), lambda i: (i, 0))
    return _pallas(
        body, rides, name="outproj_bwd", grid=(t // tm,),
        in_specs=[pl.BlockSpec((tm, d), lambda i: (i, 0)), pl.BlockSpec(wo.shape, lambda i: (0, 0))],
        out_specs=[half, half],
        out_shape=[jax.ShapeDtypeStruct((t, hw), F32), jax.ShapeDtypeStruct((t, hw), F32)],
        sem=("arbitrary",), args=[dh, wo])


def _retention_tables(t):
    pos = jnp.arange(t, dtype=F32)
    inv_freq = ROPE_BASE ** (-jnp.arange(0, RET_DIM, 2, dtype=F32) / RET_DIM)
    ang = jnp.repeat(pos[:, None] * inv_freq[None, :], 2, axis=-1)
    c = RET_CHUNK
    log_g = jnp.log(1.0 - 2.0 ** (-5.0 - jnp.arange(RET_HEADS, dtype=F32)))
    idx = jnp.arange(c, dtype=F32)
    rel = idx[:, None] - idx[None, :]
    decay = jnp.where(rel >= 0, jnp.exp(log_g[:, None, None] * jnp.maximum(rel, 0.0)), 0.0)
    zeta = jnp.exp(log_g[:, None] * (c - 1 - idx)[None, :])
    xi = jnp.exp(log_g[:, None] * (idx + 1)[None, :])
    gc = jnp.exp(log_g * c)
    wide = lambda v: jnp.broadcast_to(v[:, :, None], (RET_HEADS, c, LANE))
    return (jnp.cos(ang), jnp.sin(ang), decay, wide(zeta), wide(xi),
            jnp.broadcast_to(gc[:, None, None], (RET_HEADS, c, LANE)))


def _rot(v):
    lane = lax.broadcasted_iota(jnp.int32, v.shape, 1)
    nxt = pltpu.roll(v, LANE - 1, 1)
    prv = pltpu.roll(v, 1, 1)
    return jnp.where(lane % 2 == 0, -nxt, prv)


def _ret_specs(tr, rev, nt):
    ti = (lambda i: nt - 1 - i) if rev else (lambda i: i)
    col = lambda off: pl.BlockSpec((tr, LANE), lambda h, i: (ti(i), off + h))
    tab = pl.BlockSpec((tr, LANE), lambda h, i: (ti(i), 0))
    head = pl.BlockSpec((None, RET_CHUNK, LANE), lambda h, i: (h, 0, 0))
    return col, tab, head


def _ret_fwd(u, gain, tabs):
    t = u.shape[0]
    tr = min(1024, t)
    nt = t // tr
    cos, sin, decay, zeta, xi, gc = tabs
    scale = RET_DIM ** -0.5

    def body(q_ref, k_ref, v_ref, gt_ref, cos_ref, sin_ref, gain_ref, dec_ref, zeta_ref, xi_ref, gc_ref,
             raw_ref, mix_ref, state):
        @pl.when(pl.program_id(1) == 0)
        def _():
            state[...] = jnp.zeros_like(state)

        for ci in range(tr // RET_CHUNK):
            sl = pl.ds(ci * RET_CHUNK, RET_CHUNK)
            cs, sn = cos_ref[sl, :], sin_ref[sl, :]
            q, k = q_ref[sl, :], k_ref[sl, :]
            qb = (q * cs + _rot(q) * sn).astype(BF16)
            kr = (k * cs + _rot(k) * sn) * scale
            kb = kr.astype(BF16)
            vb = v_ref[sl, :].astype(BF16)
            s = _dot(qb, kb, NT) * dec_ref[...]
            st = state[...]
            o = _dot(s.astype(BF16), vb) + _dot(qb, st.astype(BF16)) * xi_ref[...]
            state[...] = st * gc_ref[...] + _dot((kr * zeta_ref[...]).astype(BF16), vb, TN)
            raw_ref[sl, :] = o
            mu = jnp.mean(o, axis=-1, keepdims=True)
            var = jnp.mean(jnp.square(o - mu), axis=-1, keepdims=True)
            y = (o - mu) * lax.rsqrt(var + GN_EPS) * gain_ref[...]
            gt = gt_ref[sl, :]
            mix_ref[sl, :] = (y * (gt * _sigmoid(gt))).astype(BF16)

    col, tab, head = _ret_specs(tr, False, nt)
    out = pl.BlockSpec((tr, LANE), lambda h, i: (i, h))
    return pl.pallas_call(
        body, name="ret_fwd", grid=(RET_HEADS, nt),
        in_specs=[col(0), col(4), col(8), col(12), tab, tab, pl.BlockSpec((1, LANE), lambda h, i: (0, h)),
                  head, head, head, head],
        out_specs=[out, out],
        out_shape=[jax.ShapeDtypeStruct((t, RET_WIDTH), F32), jax.ShapeDtypeStruct((t, RET_WIDTH), BF16)],
        scratch_shapes=[pltpu.VMEM((RET_DIM, RET_DIM), F32)],
        compiler_params=_params("arbitrary", "arbitrary"),
    )(u, u, u, u, cos, sin, gain, decay, zeta, xi, gc)


def _ret_bwd_q(dmix, raw, u, gain, tabs, rides=None):
    t = u.shape[0]
    tr = min(1024, t)
    nt = t // tr
    cos, sin, decay, zeta, xi, gc = tabs
    scale = RET_DIM ** -0.5

    def body(dm_ref, raw_ref, q_ref, k_ref, v_ref, gt_ref, cos_ref, sin_ref, gain_ref, dec_ref, zeta_ref, xi_ref, gc_ref,
             dq_ref, dgt_ref, dret_ref, dgain_ref, state):
        @pl.when(pl.program_id(1) == 0)
        def _():
            state[...] = jnp.zeros_like(state)
            dgain_ref[...] = jnp.zeros_like(dgain_ref)

        for ci in range(tr // RET_CHUNK):
            sl = pl.ds(ci * RET_CHUNK, RET_CHUNK)
            cs, sn = cos_ref[sl, :], sin_ref[sl, :]
            q, k = q_ref[sl, :], k_ref[sl, :]
            qb = (q * cs + _rot(q) * sn).astype(BF16)
            kr = (k * cs + _rot(k) * sn) * scale
            kb = kr.astype(BF16)
            vb = v_ref[sl, :].astype(BF16)
            o = raw_ref[sl, :]
            mu = jnp.mean(o, axis=-1, keepdims=True)
            var = jnp.mean(jnp.square(o - mu), axis=-1, keepdims=True)
            rs = lax.rsqrt(var + GN_EPS)
            n = (o - mu) * rs
            gt = gt_ref[sl, :]
            sig = _sigmoid(gt)
            dout = dm_ref[sl, :]
            dgt_ref[sl, :] = (dout * (n * gain_ref[...]) * (sig * (1.0 + gt * (1.0 - sig)))).astype(BF16)
            dy = dout * (gt * sig)
            dgain_ref[...] += _rows8(dy * n)
            dn = dy * gain_ref[...]
            do = rs * (dn - jnp.mean(dn, axis=-1, keepdims=True) - n * jnp.mean(dn * n, axis=-1, keepdims=True))
            dret_ref[sl, :] = do
            ds = _dot(do.astype(BF16), vb, NT) * dec_ref[...]
            st = state[...]
            dqr = _dot(ds.astype(BF16), kb) + _dot((do * xi_ref[...]).astype(BF16), st.astype(BF16), NT)
            dq_ref[sl, :] = (dqr * cs - _rot(dqr * sn)).astype(BF16)
            state[...] = st * gc_ref[...] + _dot((kr * zeta_ref[...]).astype(BF16), vb, TN)

    col, tab, head = _ret_specs(tr, False, nt)
    out = pl.BlockSpec((tr, LANE), lambda h, i: (i, h))
    return _pallas(
        body, rides, name="ret_bwd_q", grid=(RET_HEADS, nt),
        in_specs=[out, out, col(0), col(4), col(8), col(12), tab, tab, pl.BlockSpec((1, LANE), lambda h, i: (0, h)),
                  head, head, head, head],
        out_specs=[out, out, out, pl.BlockSpec((8, LANE), lambda h, i: (0, h))],
        out_shape=[jax.ShapeDtypeStruct((t, RET_WIDTH), BF16), jax.ShapeDtypeStruct((t, RET_WIDTH), BF16),
                   jax.ShapeDtypeStruct((t, RET_WIDTH), F32), jax.ShapeDtypeStruct((8, RET_WIDTH), F32)],
        scratch_shapes=[pltpu.VMEM((RET_DIM, RET_DIM), F32)],
        sem=("arbitrary", "arbitrary"), args=[dmix, raw, u, u, u, u, cos, sin, gain, decay, zeta, xi, gc])


def _ret_bwd_kv(dret, u, tabs, rides=None):
    t = u.shape[0]
    tr = min(1024, t)
    nt = t // tr
    cos, sin, decay, zeta, xi, gc = tabs
    scale = RET_DIM ** -0.5

    def body(do_ref, q_ref, k_ref, v_ref, cos_ref, sin_ref, dec_ref, zeta_ref, xi_ref, gc_ref, dk_ref, dv_ref, gst):
        @pl.when(pl.program_id(1) == 0)
        def _():
            gst[...] = jnp.zeros_like(gst)

        for ci in reversed(range(tr // RET_CHUNK)):
            sl = pl.ds(ci * RET_CHUNK, RET_CHUNK)
            cs, sn = cos_ref[sl, :], sin_ref[sl, :]
            q, k = q_ref[sl, :], k_ref[sl, :]
            qb = (q * cs + _rot(q) * sn).astype(BF16)
            kr = (k * cs + _rot(k) * sn) * scale
            kb = kr.astype(BF16)
            vb = v_ref[sl, :].astype(BF16)
            do = do_ref[sl, :]
            dob = do.astype(BF16)
            s = (_dot(qb, kb, NT) * dec_ref[...]).astype(BF16)
            ds = (_dot(dob, vb, NT) * dec_ref[...]).astype(BF16)
            gb = gst[...].astype(BF16)
            dv_ref[sl, :] = (_dot(s, dob, TN) + _dot((kr * zeta_ref[...]).astype(BF16), gb)).astype(BF16)
            dkr = (_dot(ds, qb, TN) + _dot(vb, gb, NT) * zeta_ref[...]) * scale
            dk_ref[sl, :] = (dkr * cs - _rot(dkr * sn)).astype(BF16)
            gst[...] = gst[...] * gc_ref[...] + _dot(qb, (do * xi_ref[...]).astype(BF16), TN)

    col, tab, head = _ret_specs(tr, True, nt)
    out = pl.BlockSpec((tr, LANE), lambda h, i: (nt - 1 - i, h))
    return _pallas(
        body, rides, name="ret_bwd_kv", grid=(RET_HEADS, nt),
        in_specs=[out, col(0), col(4), col(8), tab, tab, head, head, head, head],
        out_specs=[out, out],
        out_shape=[jax.ShapeDtypeStruct((t, RET_WIDTH), BF16), jax.ShapeDtypeStruct((t, RET_WIDTH), BF16)],
        scratch_shapes=[pltpu.VMEM((RET_DIM, RET_DIM), F32)],
        sem=("arbitrary", "arbitrary"), args=[dret, u, u, u, cos, sin, decay, zeta, xi, gc])


PAIRS = ATT_WIDTH // LANE
ATT_Q_BLK, ATT_K_BLK, ATT_V_BLK = 0, PAIRS, 2 * PAIRS
STAT_LANES = ATT_DIM // 2


def _att_tiles(t, dil):
    sub = t // dil
    tq = min(512, sub)
    return sub, tq, sub // tq, tq // ATT_BLOCK


def _att_in_specs(tq, qb, ti):
    cur = lambda off: pl.BlockSpec((None, tq, LANE), lambda g, p, i: (g, ti(i), off + p))
    prev = lambda off: pl.BlockSpec((None, ATT_BLOCK, LANE), lambda g, p, i: (g, jnp.maximum(ti(i) * qb - 1, 0), off + p))
    return [cur(ATT_Q_BLK), cur(ATT_K_BLK), prev(ATT_K_BLK), cur(ATT_V_BLK), prev(ATT_V_BLK)]


def _band_mask():
    row = lax.broadcasted_iota(jnp.int32, (ATT_BLOCK, 2 * ATT_BLOCK), 0)
    col = lax.broadcasted_iota(jnp.int32, (ATT_BLOCK, 2 * ATT_BLOCK), 1)
    dist = row + ATT_BLOCK - col
    return (dist >= 0) & (dist <= ATT_BLOCK), col >= ATT_BLOCK


def _att_fwd(ua, dil):
    sub = ua.shape[1]
    _, tq, nq, qb = _att_tiles(sub * dil, dil)

    def body(q_ref, kc_ref, kp_ref, vc_ref, vp_ref, o_ref, l_ref, kx, vx):
        tile = pl.program_id(2)
        kx[0:ATT_BLOCK, :] = kp_ref[...]
        kx[ATT_BLOCK:, :] = kc_ref[...]
        vx[0:ATT_BLOCK, :] = vp_ref[...]
        vx[ATT_BLOCK:, :] = vc_ref[...]
        band, cur_cols = _band_mask()
        for b in range(qb):
            rows = slice(b * ATT_BLOCK, (b + 1) * ATT_BLOCK)
            mask = band if b > 0 else band & (cur_cols | (tile > 0))
            qv = q_ref[rows, :] * jnp.asarray(ATT_DIM ** -0.5, BF16)
            kv = kx[b * ATT_BLOCK:(b + 2) * ATT_BLOCK, :]
            vv = vx[b * ATT_BLOCK:(b + 2) * ATT_BLOCK, :]
            outs, lses = [], []
            for e in range(LANE // ATT_DIM):
                hs = slice(e * ATT_DIM, (e + 1) * ATT_DIM)
                s = jnp.where(mask, _dot(qv[:, hs], kv[:, hs], NT), -1e30)
                m = jnp.max(s, axis=-1, keepdims=True)
                ex = jnp.exp(s - m)
                den = jnp.sum(ex, axis=-1, keepdims=True)
                outs.append(_dot((ex / den).astype(BF16), vv[:, hs]))
                lses.append(jnp.broadcast_to(m + jnp.log(den), (ATT_BLOCK, ATT_DIM)))
            o_ref[rows, :] = jnp.concatenate(outs, axis=1)
            l_ref[rows, :] = jnp.concatenate(lses, axis=1)

    out = pl.BlockSpec((None, tq, LANE), lambda g, p, i: (g, i, p))
    return pl.pallas_call(
        body, name=f"att_fwd_d{dil}", grid=(dil, PAIRS, nq),
        in_specs=_att_in_specs(tq, qb, lambda i: i),
        out_specs=[out, out],
        out_shape=[jax.ShapeDtypeStruct((dil, sub, ATT_WIDTH), F32)] * 2,
        scratch_shapes=[pltpu.VMEM((tq + ATT_BLOCK, LANE), BF16)] * 2,
        compiler_params=_params("arbitrary", "arbitrary", "arbitrary"),
    )(ua, ua, ua, ua, ua)


def _regrouped_spec(tm, dil, w):
    return pl.BlockSpec((dil, tm // dil, w), lambda i: (0, i, 0))


def _att_combine(outs, lses, t):
    w = ATT_WIDTH
    tm = min(512, t)
    nb = len(outs)

    def body(*refs):
        o_refs, l_refs = refs[:nb], refs[nb:2 * nb]
        mix_ref, att_ref, lse_ref, buf = refs[2 * nb:]
        ls = [_natural_rows(r, dil, buf) for r, dil in zip(l_refs, DILATIONS)]
        m = functools.reduce(jnp.maximum, ls)
        ws = [jnp.exp(l - m) for l in ls]
        den = functools.reduce(jnp.add, ws)
        att = functools.reduce(jnp.add, [(wt / den) * _natural_rows(r, dil, buf) for wt, r, dil in zip(ws, o_refs, DILATIONS)])
        att_ref[...] = att
        mix_ref[...] = att.astype(BF16)
        lse_ref[...] = m + jnp.log(den)

    tile = pl.BlockSpec((tm, w), lambda i: (i, 0))
    regrouped = [_regrouped_spec(tm, dil, w) for dil in DILATIONS]
    return pl.pallas_call(
        body, name="att_combine", grid=(t // tm,),
        in_specs=regrouped * 2, out_specs=[tile, tile, tile],
        out_shape=[jax.ShapeDtypeStruct((t, w), BF16), jax.ShapeDtypeStruct((t, w), F32), jax.ShapeDtypeStruct((t, w), F32)],
        scratch_shapes=[_chunk_scratch(tm, w)],
        compiler_params=_params("arbitrary"),
    )(*outs, *lses)


def _att_bwd_prep(datt, att, lse):
    t, w = datt.shape
    tm = min(512, t)

    def body(da_ref, at_ref, l_ref, *rest):
        outs, dbuf, sbuf = rest[:-2], rest[-2], rest[-1]
        dav = da_ref[...]
        prod = dav * at_ref[...]
        lane = lax.broadcasted_iota(jnp.int32, (tm, LANE), 1)
        for k in range(w // LANE):
            cols = slice(k * LANE, (k + 1) * LANE)
            dbuf[k] = dav[:, cols]
            delta = jnp.concatenate(
                [jnp.broadcast_to(jnp.sum(prod[:, k * LANE + e * ATT_DIM:k * LANE + (e + 1) * ATT_DIM], axis=-1, keepdims=True),
                                  (tm, ATT_DIM)) for e in range(LANE // ATT_DIM)], axis=1)
            sbuf[k] = jnp.where(lane % ATT_DIM < STAT_LANES, l_ref[:, cols], delta)
        for k, dil in enumerate(DILATIONS):
            _regroup_store(dbuf, outs[2 * k], dil)
            _regroup_store(sbuf, outs[2 * k + 1], dil)

    tile = pl.BlockSpec((tm, w), lambda i: (i, 0))
    res = pl.pallas_call(
        body, name="att_bwd_prep", grid=(t // tm,),
        in_specs=[tile] * 3,
        out_specs=[_regrouped_spec(tm, dil, w) for dil in DILATIONS for _ in range(2)],
        out_shape=[jax.ShapeDtypeStruct((dil, t // dil, w), dt) for dil in DILATIONS for dt in (BF16, F32)],
        scratch_shapes=[_chunk_scratch(tm, w)] * 2,
        compiler_params=_params("arbitrary"),
    )(datt, att, lse)
    return [(res[2 * k], res[2 * k + 1]) for k in range(len(DILATIONS))]


def _att_bwd(ua, da, stat, dil, rides=None):
    sub = ua.shape[1]
    _, tq, nq, qb = _att_tiles(sub * dil, dil)
    scale = ATT_DIM ** -0.5

    def body(q_ref, kc_ref, kp_ref, vc_ref, vp_ref, da_ref, st_ref, dq_ref, dk_ref, dv_ref, kx, vx, dkx, dvx, ck, cv):
        step = pl.program_id(2)
        tile = nq - 1 - step

        @pl.when(step == 0)
        def _():
            ck[...] = jnp.zeros_like(ck)
            cv[...] = jnp.zeros_like(cv)

        kx[0:ATT_BLOCK, :] = kp_ref[...]
        kx[ATT_BLOCK:, :] = kc_ref[...]
        vx[0:ATT_BLOCK, :] = vp_ref[...]
        vx[ATT_BLOCK:, :] = vc_ref[...]
        dkx[...] = jnp.zeros_like(dkx)
        dvx[...] = jnp.zeros_like(dvx)
        band, cur_cols = _band_mask()
        for b in range(qb):
            rows = slice(b * ATT_BLOCK, (b + 1) * ATT_BLOCK)
            keys = slice(b * ATT_BLOCK, (b + 2) * ATT_BLOCK)
            mask = band if b > 0 else band & (cur_cols | (tile > 0))
            qv = q_ref[rows, :] * jnp.asarray(scale, BF16)
            kv, vv = kx[keys, :], vx[keys, :]
            dab = da_ref[rows, :]
            lv = st_ref[rows, :]
            dqs, dks, dvs = [], [], []
            for e in range(LANE // ATT_DIM):
                hs = slice(e * ATT_DIM, (e + 1) * ATT_DIM)
                delta = lv[:, e * ATT_DIM + STAT_LANES:e * ATT_DIM + STAT_LANES + 1]
                s = _dot(qv[:, hs], kv[:, hs], NT)
                p = jnp.where(mask, jnp.exp(s - lv[:, e * ATT_DIM:e * ATT_DIM + 1]), 0.0)
                ds = (p * (_dot(dab[:, hs], vv[:, hs], NT) - delta)).astype(BF16)
                dqs.append(_dot(ds, kv[:, hs]) * scale)
                dks.append(_dot(ds, qv[:, hs], TN))
                dvs.append(_dot(p.astype(BF16), dab[:, hs], TN))
            dq_ref[rows, :] = jnp.concatenate(dqs, axis=1)
            dkx[keys, :] += jnp.concatenate(dks, axis=1)
            dvx[keys, :] += jnp.concatenate(dvs, axis=1)
        dkx[tq:, :] += ck[...]
        dvx[tq:, :] += cv[...]
        dk_ref[...] = dkx[ATT_BLOCK:, :]
        dv_ref[...] = dvx[ATT_BLOCK:, :]
        ck[...] = dkx[0:ATT_BLOCK, :]
        cv[...] = dvx[0:ATT_BLOCK, :]

    ti = lambda i: nq - 1 - i
    out = pl.BlockSpec((None, tq, LANE), lambda g, p, i: (g, ti(i), p))
    shape = jax.ShapeDtypeStruct((dil, sub, ATT_WIDTH), F32)
    return _pallas(
        body, rides, name=f"att_bwd_d{dil}", grid=(dil, PAIRS, nq),
        in_specs=_att_in_specs(tq, qb, ti) + [out, out],
        out_specs=[out, out, out], out_shape=[shape] * 3,
        scratch_shapes=[pltpu.VMEM((tq + ATT_BLOCK, LANE), BF16)] * 2 + [pltpu.VMEM((tq + ATT_BLOCK, LANE), F32)] * 2
        + [pltpu.VMEM((ATT_BLOCK, LANE), F32)] * 2,
        sem=("arbitrary", "arbitrary", "arbitrary"), args=[ua, ua, ua, ua, ua, da, stat])


def _att_bwd_sum(parts, t):
    w = ATT_WIDTH
    tm = min(512, t)
    nk = len(parts[0])

    def body(*refs):
        ins, outs, buf = refs[:-nk - 1], refs[-nk - 1:-1], refs[-1]
        for k in range(nk):
            acc = None
            for b, dil in enumerate(DILATIONS):
                rows = _natural_rows(ins[b * nk + k], dil, buf)
                acc = rows if acc is None else acc + rows
            outs[k][...] = acc.astype(BF16)

    tile = pl.BlockSpec((tm, w), lambda i: (i, 0))
    return pl.pallas_call(
        body, name="att_bwd_sum", grid=(t // tm,),
        in_specs=[_regrouped_spec(tm, dil, w) for dil in DILATIONS for _ in range(nk)], out_specs=[tile] * nk,
        out_shape=[jax.ShapeDtypeStruct((t, w), BF16)] * nk,
        scratch_shapes=[_chunk_scratch(tm, w)],
        compiler_params=_params("arbitrary"),
    )(*[a for p in parts for a in p])


def _loss_bwd(h, gain, target):
    t, d = h.shape
    tm = min(512, t)

    def body(h_ref, gain_ref, tg_ref, loss_ref, dh_ref, dgain_ref):
        @pl.when(pl.program_id(0) == 0)
        def _():
            loss_ref[...] = jnp.zeros_like(loss_ref)
            dgain_ref[...] = jnp.zeros_like(dgain_ref)

        hv = h_ref[...]
        r = lax.rsqrt(jnp.mean(hv * hv, axis=-1, keepdims=True) + NORM_EPS)
        xh = hv * r
        err = xh * gain_ref[...] - tg_ref[...]
        sq = _rows8(jnp.square(err))
        loss_ref[...] += 0.5 * functools.reduce(jnp.add, [sq[:, k * LANE:(k + 1) * LANE] for k in range(d // LANE)]) / d
        dy = err / d
        dgain_ref[...] += _rows8(dy * xh)
        dxh = dy * gain_ref[...]
        dh_ref[...] = r * (dxh - xh * jnp.mean(dxh * xh, axis=-1, keepdims=True))

    tile = pl.BlockSpec((tm, d), lambda i: (i, 0))
    return pl.pallas_call(
        body, name="loss_bwd", grid=(t // tm,),
        in_specs=[tile, pl.BlockSpec((1, d), lambda i: (0, 0)), tile],
        out_specs=[pl.BlockSpec((8, LANE), lambda i: (0, 0)), tile, pl.BlockSpec((8, d), lambda i: (0, 0))],
        out_shape=[jax.ShapeDtypeStruct((8, LANE), F32), jax.ShapeDtypeStruct((t, d), F32), jax.ShapeDtypeStruct((8, d), F32)],
        compiler_params=_params("arbitrary"),
    )(h, gain, target)


class _Reduction:
    def __init__(self, place, names, grads):
        self.place, self.names, self.grads = place, names, grads

    def pair(self):
        return _pair_ride(self.grads)

    def chips(self, got):
        self.got = got
        return _chip_ride([_pair_sum(self.place, g, r, f"pair_sum_{n}") for g, r, n in zip(self.grads, got, self.names)])

    def halves(self, others):
        return [_chip_sum(self.place, g, r, o, f"chip_sum_{n}")
                for g, r, o, n in zip(self.grads, self.got, others, self.names)]


def _step(x, target, gains, w, place=None):
    t = x.shape[0]
    ex = place is not None
    g_ffn1, g_mix, g_ret, g_ffn2, g_fin = gains
    w = list(w)
    tabs = _retention_tables(t)
    red = lambda names, grads: _Reduction(place, names, grads) if ex else None
    ride = lambda r: [r] if ex else None

    if ex:
        w[0:3] = _run(_gather_ride(w[0:3]), "gather_ffn1_weights")
    (h1, xn1, ga1, ua1, act1), rest = _ffn_fwd(x, g_ffn1, *w[0:3], "ffn1_fwd", ride(_gather_ride(w[3:])) if ex else None)
    if ex:
        w[3:] = rest[0]
    wg1, wu1, wd1, win, wo, wg2, wu2, wd2 = w
    wo2 = wo.reshape(wo.shape[0] * wo.shape[1], wo.shape[2])
    xnm, u, *uas = _inproj_fwd(h1, g_mix, win)
    raw, mix_r = _ret_fwd(u, g_ret, tabs)
    branches = [_att_fwd(ua, dil) for ua, dil in zip(uas, DILATIONS)]
    mix_a, att, lse = _att_combine([b[0] for b in branches], [b[1] for b in branches], t)
    h2 = _outproj_fwd(h1, mix_r, mix_a, wo2)
    (h3, xn2, ga2, ua2, act2), _ = _ffn_fwd(h2, g_ffn2, wg2, wu2, wd2, "ffn2_fwd")
    loss_p, dh3, dg_fin = _loss_bwd(h3, g_fin, target)

    (dwd2,), _ = _ffn_wgrad_down(act2, dh3, "ffn2_wgrad_down")
    r_d2 = red(["ffn2_w_down"], [dwd2])
    (dh2, dga2, dua2, dg_ffn2), e = _ffn_bwd_data(dh3, h2, g_ffn2, ga2, ua2, wg2, wu2, wd2, "ffn2_bwd",
                                                  ex and [r_d2.pair()])
    (dwg2, dwu2), e = _ffn_wgrad_gu(xn2, dga2, dua2, "ffn2_wgrad_gu", ex and [r_d2.chips(e[0])])
    r_gu2 = red(["ffn2_w_gate", "ffn2_w_up"], [dwg2, dwu2])
    (dmix_r, dmix_a), e = _outproj_bwd(dh2, wo2, ex and [r_gu2.pair(), _finish_ride(r_d2.halves(e[0]))])
    if ex:
        got_gu2, (dwd2,) = e
    hw = RET_WIDTH // (wo.shape[1])
    dwo = jnp.concatenate([_tn_matmul(mix_r, dh2, dh2.shape[1], "wo_grad_r").reshape(hw, wo.shape[1], wo.shape[2]),
                           _tn_matmul(mix_a, dh2, dh2.shape[1], "wo_grad_a").reshape(hw, wo.shape[1], wo.shape[2])])
    r_wo = red(["w_out"], [dwo])
    (dq_r, dgt_r, dret, dg_ret), e = _ret_bwd_q(dmix_r, raw, u, g_ret, tabs, ex and [r_gu2.chips(got_gu2)])
    (dk_r, dv_r), e = _ret_bwd_kv(dret, u, tabs, ex and [r_wo.pair(), _finish_ride(r_gu2.halves(e[0]))])
    if ex:
        got_wo, (dwg2, dwu2) = e
    prep = _att_bwd_prep(dmix_a, att, lse)
    p1, e = _att_bwd(uas[0], *prep[0], DILATIONS[0], ex and [r_wo.chips(got_wo)])
    p4, e = _att_bwd(uas[1], *prep[1], DILATIONS[1], ex and [_finish_ride(r_wo.halves(e[0]))])
    if ex:
        (dwo,), = e
    p16, _ = _att_bwd(uas[2], *prep[2], DILATIONS[2])
    dq_a, dk_a, dv_a = _att_bwd_sum([p1, p4, p16], t)
    dh1, du, dg_mix = _inproj_bwd([dq_r, dk_r, dv_r, dgt_r, dq_a, dk_a, dv_a], h1, g_mix, dh2, win)
    dwin = _tn_matmul(xnm, du, win.shape[2], "win_grad")
    r_in = red(["w_in"], [dwin])
    (dwd1,), e = _ffn_wgrad_down(act1, dh1, "ffn1_wgrad_down", ex and [r_in.pair()])
    r_d1 = red(["ffn1_w_down"], [dwd1])
    (dx, dga1, dua1, dg_ffn1), e = _ffn_bwd_data(dh1, x, g_ffn1, ga1, ua1, wg1, wu1, wd1, "ffn1_bwd",
                                                  ex and [r_in.chips(e[0]), r_d1.pair()])
    (dwg1, dwu1), e = _ffn_wgrad_gu(xn1, dga1, dua1, "ffn1_wgrad_gu",
                                    ex and [_finish_ride(r_in.halves(e[0])), r_d1.chips(e[1])])
    gain_parts = [dg_ffn1, dg_mix, dg_ret, dg_ffn2, dg_fin]
    if not ex:
        return loss_p, dx, [dwg1, dwu1, dwd1, dwin, dwo, dwg2, dwu2, dwd2], gain_parts
    (dwin,), oth_d1 = e
    r_gu1 = red(["ffn1_w_gate", "ffn1_w_up"], [dwg1, dwu1])
    got = _run(r_gu1.pair(), "pair_exchange_ffn1_gate_up")
    oth = _run(r_gu1.chips(got), "chip_exchange_ffn1_gate_up")
    last = r_gu1.halves(oth) + r_d1.halves(oth_d1)
    dwg1, dwu1, dwd1, gall = _run(_finish_ride(last, _pack_gains(gain_parts, x.shape[1])), "finish_exchange_ffn1")
    return loss_p, dx, [dwg1, dwu1, dwd1, dwin, dwo, dwg2, dwu2, dwd2], gall


N_DEV = 8
GAIN_ROWS = 8


def _place():
    x, y, c = lax.axis_index("x"), lax.axis_index("y"), lax.axis_index("c")
    chips = [(1 - x, y), (x, 1 - y), (1 - x, 1 - y)]
    return x, y, c, chips


def _hbm_specs(n):
    return [pl.BlockSpec(memory_space=pl.ANY)] * n


def _place_shard(place, w, name):
    r, cols = w.shape
    tr = r // 4

    def body(place_ref, w_ref, o_ref):
        o_ref[...] = w_ref[...].astype(BF16)

    return pl.pallas_call(
        body, name=name,
        grid_spec=pltpu.PrefetchScalarGridSpec(
            num_scalar_prefetch=1, grid=(r // tr,),
            in_specs=[pl.BlockSpec((tr, cols), lambda i, pr: (i, 0))],
            out_specs=pl.BlockSpec((None, tr, cols), lambda i, pr: (pr[0], i, 0))),
        out_shape=jax.ShapeDtypeStruct((N_SHARD, r, cols), BF16),
        compiler_params=_params("arbitrary"),
    )(place, w)


def _gather_ride(bufs):
    na = len(bufs)

    def legs(outs, sems):
        send_sem, recv_sem, fsend_sem, frecv_sem = sems
        x, y, c, chips = _place()

        def half(a, idx, which):
            hr = outs[a].shape[1] // 2
            return outs[a].at[idx, pl.ds(which * hr, hr)]

        def ici(a, j, idx):
            px, py = chips[j]
            return pltpu.make_async_remote_copy(
                src_ref=half(a, idx, c), dst_ref=half(a, idx, c),
                send_sem=send_sem.at[a, j], recv_sem=recv_sem.at[a, j], device_id=(px, py, c), device_id_type=MESH)

        def d2d(a, j, idx, which):
            return pltpu.make_async_remote_copy(
                src_ref=half(a, idx, which), dst_ref=half(a, idx, which),
                send_sem=fsend_sem.at[a, j], recv_sem=frecv_sem.at[a, j], device_id=(x, y, 1 - c), device_id_type=MESH)

        return 2 * x + y, c, chips, ici, d2d

    def start(ins, outs, sems):
        me, _, _, ici, _ = legs(outs, sems)
        for a in range(na):
            for j in range(3):
                ici(a, j, me).start()

    def finish(ins, outs, sems):
        me, c, chips, ici, d2d = legs(outs, sems)
        passed = []
        for a in range(na):
            for j, (px, py) in enumerate(chips):
                ici(a, j, 2 * px + py).wait_recv()
                cp = d2d(a, j, 2 * px + py, c)
                cp.start()
                passed.append(cp)
        for a in range(na):
            for j, (px, py) in enumerate(chips):
                d2d(a, j, 2 * px + py, 1 - c).wait_recv()
        for a in range(na):
            for j in range(3):
                ici(a, j, me).wait_send()
        for cp in passed:
            cp.wait_send()

    return _Ride(bufs, [jax.ShapeDtypeStruct(b.shape, b.dtype) for b in bufs], [pltpu.SemaphoreType.DMA((na, 3))] * 4,
                 start, finish, {a: a for a in range(na)})


def _pair_ride(grads):
    na = len(grads)

    def copies(ins, outs, sems):
        send_sem, recv_sem = sems
        x, y, c, _ = _place()
        res = []
        for a in range(na):
            hr = ins[a].shape[1] // 2
            res.append(pltpu.make_async_remote_copy(
                src_ref=ins[a].at[:, pl.ds((1 - c) * hr, hr)], dst_ref=outs[a],
                send_sem=send_sem.at[a], recv_sem=recv_sem.at[a], device_id=(x, y, 1 - c), device_id_type=MESH))
        return res

    def start(ins, outs, sems):
        for cp in copies(ins, outs, sems):
            cp.start()

    def finish(ins, outs, sems):
        for cp in copies(ins, outs, sems):
            cp.wait()

    return _Ride(grads, [jax.ShapeDtypeStruct((g.shape[0], g.shape[1] // 2, g.shape[2]), g.dtype) for g in grads],
                 [pltpu.SemaphoreType.DMA((na,))] * 2, start, finish)


def _chip_ride(sums):
    na = len(sums)

    def copies(ins, outs, sems):
        send_sem, recv_sem = sems
        x, y, c, chips = _place()
        res = []
        for a in range(na):
            for j, (px, py) in enumerate(chips):
                res.append(pltpu.make_async_remote_copy(
                    src_ref=ins[a].at[2 * px + py], dst_ref=outs[a].at[j],
                    send_sem=send_sem.at[a, j], recv_sem=recv_sem.at[a, j], device_id=(px, py, c), device_id_type=MESH))
        return res

    def start(ins, outs, sems):
        for cp in copies(ins, outs, sems):
            cp.start()

    def finish(ins, outs, sems):
        for cp in copies(ins, outs, sems):
            cp.wait()

    return _Ride(sums, [jax.ShapeDtypeStruct((3,) + s.shape[1:], s.dtype) for s in sums],
                 [pltpu.SemaphoreType.DMA((na, 3))] * 2, start, finish)


def _finish_ride(grads, gpack=None):
    na = len(grads)

    def halves(outs, sems, which):
        x, y, c, _ = _place()
        res = []
        for a in range(na):
            hr = outs[a].shape[0] // 2
            rows = outs[a].at[pl.ds((c if which == "mine" else 1 - c) * hr, hr)]
            res.append(pltpu.make_async_remote_copy(
                src_ref=rows, dst_ref=rows, send_sem=sems[0].at[a], recv_sem=sems[1].at[a],
                device_id=(x, y, 1 - c), device_id_type=MESH))
        return res

    def gains(ins, outs, sems):
        x, y, c, _ = _place()
        dev = 4 * x + 2 * y + c
        g_in, g_out = ins[na], outs[na]
        own = pltpu.make_async_copy(g_in, g_out.at[dev], sems[2])
        sends, lands = [], []
        for k in range(N_DEV - 1):
            bx, by, bc = (k + 1) // 4, ((k + 1) // 2) % 2, (k + 1) % 2
            peer = (jnp.bitwise_xor(x, bx), jnp.bitwise_xor(y, by), jnp.bitwise_xor(c, bc))
            sends.append(pltpu.make_async_remote_copy(
                src_ref=g_in, dst_ref=g_out.at[dev], send_sem=sems[3].at[k], recv_sem=sems[4].at[k],
                device_id=peer, device_id_type=MESH))
            slot = g_out.at[jnp.bitwise_xor(dev, k + 1)]
            lands.append(pltpu.make_async_remote_copy(
                src_ref=slot, dst_ref=slot, send_sem=sems[3].at[k], recv_sem=sems[4].at[k],
                device_id=peer, device_id_type=MESH))
        return own, sends, lands

    def start(ins, outs, sems):
        for cp in halves(outs, sems, "mine"):
            cp.start()
        if gpack is not None:
            own, sends, _ = gains(ins, outs, sems)
            own.start()
            for cp in sends:
                cp.start()

    def finish(ins, outs, sems):
        for cp in halves(outs, sems, "sibling's"):
            cp.wait_recv()
        if gpack is not None:
            own, sends, lands = gains(ins, outs, sems)
            for cp in lands:
                cp.wait_recv()
            for cp in sends:
                cp.wait_send()
            own.wait()
        for cp in halves(outs, sems, "mine"):
            cp.wait_send()

    shapes = [jax.ShapeDtypeStruct(g.shape, g.dtype) for g in grads]
    sems = [pltpu.SemaphoreType.DMA((na,))] * 2
    if gpack is None:
        return _Ride(grads, shapes, sems, start, finish, {a: a for a in range(na)})
    return _Ride(list(grads) + [gpack], shapes + [jax.ShapeDtypeStruct((N_DEV,) + gpack.shape, gpack.dtype)],
                 sems + [pltpu.SemaphoreType.DMA, pltpu.SemaphoreType.DMA((N_DEV - 1,)), pltpu.SemaphoreType.DMA((N_DEV - 1,))],
                 start, finish, {a: a for a in range(na)})


def _pair_sum(place, grad, got, name):
    ns, r, cols = grad.shape
    hr = r // 2

    def body(place_ref, g_ref, r_ref, o_ref):
        o_ref[...] = (g_ref[...] + r_ref[...]).astype(BF16)

    return pl.pallas_call(
        body, name=name,
        grid_spec=pltpu.PrefetchScalarGridSpec(
            num_scalar_prefetch=1, grid=(ns,),
            in_specs=[pl.BlockSpec((None, hr, cols), lambda s, pr: (s, pr[1], 0)),
                      pl.BlockSpec((None, hr, cols), lambda s, pr: (s, 0, 0))],
            out_specs=pl.BlockSpec((None, hr, cols), lambda s, pr: (s, 0, 0))),
        out_shape=jax.ShapeDtypeStruct((ns, hr, cols), BF16),
        compiler_params=_params("arbitrary"),
    )(place, grad, got)


def _chip_sum(place, grad, got, others, name):
    ns, r, cols = grad.shape
    hr = r // 2
    nb = 2
    tr = hr // nb

    def body(place_ref, g_ref, r_ref, o3_ref, o_ref):
        acc = g_ref[...] + r_ref[...]
        for j in range(3):
            acc = acc + o3_ref[j].astype(F32)
        o_ref[...] = acc

    return pl.pallas_call(
        body, name=name,
        grid_spec=pltpu.PrefetchScalarGridSpec(
            num_scalar_prefetch=1, grid=(nb,),
            in_specs=[pl.BlockSpec((None, tr, cols), lambda i, pr: (pr[0], pr[1] * nb + i, 0)),
                      pl.BlockSpec((None, tr, cols), lambda i, pr: (pr[0], i, 0)),
                      pl.BlockSpec((3, tr, cols), lambda i, pr: (0, i, 0))],
            out_specs=pl.BlockSpec((tr, cols), lambda i, pr: (pr[1] * nb + i, 0))),
        out_shape=jax.ShapeDtypeStruct((r, cols), F32),
        compiler_params=_params("arbitrary"),
    )(place, grad, got, others)


def _pack_gains(parts, d):
    def body(*refs):
        ins, o_ref = refs[:-1], refs[-1]
        o_ref[...] = jnp.zeros_like(o_ref)
        for k, r in enumerate(ins):
            o_ref[k:k + 1, 0:r.shape[1]] = jnp.sum(r[...], axis=0, keepdims=True)

    return pl.pallas_call(
        body, name="pack_gains", out_shape=jax.ShapeDtypeStruct((GAIN_ROWS, d), F32),
    )(*parts)


def _adamw_math(w, g, m, v):
    m = ADAM_B1 * m + (1.0 - ADAM_B1) * g
    v = ADAM_B2 * v + (1.0 - ADAM_B2) * jnp.square(g)
    m_hat = m / (1.0 - ADAM_B1 ** ADAM_STEP)
    v_hat = v / (1.0 - ADAM_B2 ** ADAM_STEP)
    return -ADAM_LR * (m_hat / (jnp.sqrt(v_hat) + ADAM_EPS) + ADAM_WD * w), m, v


def _adamw(w, g, m, v, name):
    r, cols = w.shape
    tr = r // 4 if (r // 4) % 8 == 0 else r

    def body(w_ref, g_ref, m_ref, v_ref, d_ref, nm_ref, nv_ref):
        d_ref[...], nm_ref[...], nv_ref[...] = _adamw_math(w_ref[...], g_ref[...], m_ref[...], v_ref[...])

    tile = pl.BlockSpec((tr, cols), lambda i: (i, 0))
    return pl.pallas_call(
        body, name=name, grid=(r // tr,), in_specs=[tile] * 4, out_specs=[tile] * 3,
        out_shape=[jax.ShapeDtypeStruct((r, cols), F32)] * 3,
        compiler_params=_params("arbitrary"),
    )(w, g, m, v)


def _adamw_gain(gall, row, w, m, v, name):
    n = w.shape[1]

    def body(ga_ref, w_ref, m_ref, v_ref, g_ref, d_ref, nm_ref, nv_ref):
        g = ga_ref[0, row:row + 1, 0:n]
        for k in range(1, N_DEV):
            g = g + ga_ref[k, row:row + 1, 0:n]
        g_ref[...] = g
        d_ref[...], nm_ref[...], nv_ref[...] = _adamw_math(w_ref[...], g, m_ref[...], v_ref[...])

    return pl.pallas_call(
        body, name=name, out_shape=[jax.ShapeDtypeStruct((1, n), F32)] * 4,
    )(gall, w, m, v)


def kernel(x, norm_ffn1, ffn1_w_gate, ffn1_w_up, ffn1_w_down, norm_mix, w_in, ret_norm_gain, w_out, norm_ffn2, ffn2_w_gate, ffn2_w_up, ffn2_w_down, norm_final, loss_target, m_norm_ffn1, m_ffn1_w_gate, m_ffn1_w_up, m_ffn1_w_down, m_norm_mix, m_w_in, m_ret_norm_gain, m_w_out, m_norm_ffn2, m_ffn2_w_gate, m_ffn2_w_up, m_ffn2_w_down, m_norm_final, v_norm_ffn1, v_ffn1_w_gate, v_ffn1_w_up, v_ffn1_w_down, v_norm_mix, v_w_in, v_ret_norm_gain, v_w_out, v_norm_ffn2, v_ffn2_w_gate, v_ffn2_w_up, v_ffn2_w_down, v_norm_final):
    d = x.shape[-1]
    mats = [ffn1_w_gate, ffn1_w_up, ffn1_w_down, w_in, w_out, ffn2_w_gate, ffn2_w_up, ffn2_w_down]
    mats_m = [m_ffn1_w_gate, m_ffn1_w_up, m_ffn1_w_down, m_w_in, m_w_out, m_ffn2_w_gate, m_ffn2_w_up, m_ffn2_w_down]
    mats_v = [v_ffn1_w_gate, v_ffn1_w_up, v_ffn1_w_down, v_w_in, v_w_out, v_ffn2_w_gate, v_ffn2_w_up, v_ffn2_w_down]
    mat_names = ["ffn1_w_gate", "ffn1_w_up", "ffn1_w_down", "w_in", "w_out", "ffn2_w_gate", "ffn2_w_up", "ffn2_w_down"]
    gains = [norm_ffn1, norm_mix, ret_norm_gain, norm_ffn2, norm_final.reshape(1, d)]
    gains_m = [m_norm_ffn1, m_norm_mix, m_ret_norm_gain, m_norm_ffn2, m_norm_final.reshape(1, d)]
    gains_v = [v_norm_ffn1, v_norm_mix, v_ret_norm_gain, v_norm_ffn2, v_norm_final.reshape(1, d)]
    gain_names = ["norm_ffn1", "norm_mix", "ret_norm_gain", "norm_ffn2", "norm_final"]

    shards = [w[0] for w in mats]
    place = jnp.stack([2 * lax.axis_index("x") + lax.axis_index("y"), lax.axis_index("c")]).astype(jnp.int32)
    placed = [_place_shard(place, s, f"place_{n}") for s, n in zip(shards, mat_names)]
    loss_p, dx, shard_grads, gall = _step(x[0], loss_target[0], gains, placed, place)

    out_g, out_d, out_m, out_v = {}, {}, {}, {}
    for n, w, g, m, v in zip(mat_names, shards, shard_grads, mats_m, mats_v):
        dl, nm, nv = _adamw(w, g, m[0], v[0], f"adamw_{n}")
        out_g[n], out_d[n], out_m[n], out_v[n] = g[None], dl[None], nm[None], nv[None]
    for row, (n, w, m, v) in enumerate(zip(gain_names, gains, gains_m, gains_v)):
        res = _adamw_gain(gall, row, w, m, v, f"adamw_{n}")
        shape = (d,) if n == "norm_final" else w.shape
        out_g[n], out_d[n], out_m[n], out_v[n] = [r.reshape(shape) for r in res]

    loss = lax.psum(jnp.sum(loss_p), ("x", "y", "c"))
    order = ["norm_ffn1", "ffn1_w_gate", "ffn1_w_up", "ffn1_w_down", "norm_mix", "w_in", "ret_norm_gain", "w_out",
             "norm_ffn2", "ffn2_w_gate", "ffn2_w_up", "ffn2_w_down", "norm_final"]
    return (loss, dx[None], *[out_g[n] for n in order], *[out_d[n] for n in order],
            *[out_m[n] for n in order], *[out_v[n] for n in order])
```

```python
import functools
import math

import jax
import jax.numpy as jnp
from jax import lax
from jax.experimental import pallas as pl
from jax.experimental.pallas import tpu as pltpu

F32 = jnp.float32
BF16 = jnp.bfloat16
MESH = pl.DeviceIdType.MESH

NORM_EPS = 1e-6
GN_EPS = 1e-6
ROPE_BASE = 10000.0
RET_HEADS = 4
RET_DIM = 128
RET_WIDTH = 512
RET_CHUNK = 128
ATT_HEADS = 8
ATT_DIM = 64
ATT_WIDTH = 512
ATT_BLOCK = 128
DILATIONS = (1, 4, 16)
IN_COLS = 4 * RET_WIDTH + 3 * ATT_WIDTH
LANE = 128
N_SHARD = 4
ADAM_LR, ADAM_B1, ADAM_B2, ADAM_EPS, ADAM_WD, ADAM_STEP = 0.001, 0.9, 0.999, 1e-08, 0.01, 10

V7X_VMEM_BYTES = 64 * 1024 * 1024
VMEM_LIMIT = V7X_VMEM_BYTES - 8 * 1024 * 1024

NT = (((1,), (1,)), ((), ()))
TN = (((0,), (0,)), ((), ()))


def _params(*sem):
    return pltpu.CompilerParams(dimension_semantics=sem, vmem_limit_bytes=VMEM_LIMIT)


def _dot(a, b, dims=None):
    if dims is None:
        return jnp.dot(a, b, preferred_element_type=F32)
    return lax.dot_general(a, b, dims, preferred_element_type=F32)


def _sigmoid(x):
    return 1.0 / (1.0 + jnp.exp(-x))


def _load_weights(pairs, sems):
    copies = [pltpu.make_async_copy(src, dst, sems.at[k]) for k, (src, dst) in enumerate(pairs)]
    for cp in copies:
        cp.start()
    for cp in copies:
        cp.wait()


def _rows8(v):
    r, c = v.shape
    return v.reshape(r // 8, 8, c).sum(axis=0)


class _Ride:
    def __init__(self, inputs, out_shapes, sems, start, finish, aliases=None):
        self.inputs, self.out_shapes, self.sems = list(inputs), list(out_shapes), list(sems)
        self.start, self.finish, self.aliases = start, finish, dict(aliases or {})


def _pallas(body, rides, *, name, in_specs, out_specs, out_shape, args, grid=(), scratch_shapes=(), sem=()):
    rides = [r for r in (rides or []) if r is not None]
    n_in, n_out, n_scr = len(args), len(out_shape), len(scratch_shapes)
    hbm = pl.BlockSpec(memory_space=pl.ANY)
    r_in = [a for r in rides for a in r.inputs]
    r_out = [s for r in rides for s in r.out_shapes]
    r_sem = [s for r in rides for s in r.sems]
    aliases, spans, ki, ko, ks = {}, [], 0, 0, 0
    for r in rides:
        aliases.update({n_in + ki + i: n_out + ko + o for i, o in r.aliases.items()})
        spans.append((ki, ko, ks))
        ki, ko, ks = ki + len(r.inputs), ko + len(r.out_shapes), ks + len(r.sems)

    def wrapped(*refs):
        ins, rin = refs[:n_in], refs[n_in:n_in + len(r_in)]
        o0 = n_in + len(r_in)
        outs, rout = refs[o0:o0 + n_out], refs[o0 + n_out:o0 + n_out + len(r_out)]
        s0 = o0 + n_out + len(r_out)
        scr, rsem = refs[s0:s0 + n_scr], refs[s0 + n_scr:]
        part = lambda r, k: (rin[spans[k][0]:spans[k][0] + len(r.inputs)], rout[spans[k][1]:spans[k][1] + len(r.out_shapes)],
                             rsem[spans[k][2]:spans[k][2] + len(r.sems)])
        first = functools.reduce(jnp.logical_and, [pl.program_id(k) == 0 for k in range(len(grid))], True)
        last = functools.reduce(jnp.logical_and, [pl.program_id(k) == grid[k] - 1 for k in range(len(grid))], True)
        if rides:
            @pl.when(first)
            def _():
                for k, r in enumerate(rides):
                    r.start(*part(r, k))

        body(*ins, *outs, *scr)
        if rides:
            @pl.when(last)
            def _():
                for k, r in enumerate(rides):
                    r.finish(*part(r, k))

    res = pl.pallas_call(
        wrapped, name=name, grid=grid,
        in_specs=list(in_specs) + [hbm] * len(r_in), out_specs=list(out_specs) + [hbm] * len(r_out),
        out_shape=list(out_shape) + r_out, input_output_aliases=aliases,
        scratch_shapes=list(scratch_shapes) + r_sem,
        compiler_params=pltpu.CompilerParams(dimension_semantics=sem, vmem_limit_bytes=VMEM_LIMIT) if grid else None,
    )(*args, *r_in)
    extras = [list(res[n_out + ko:n_out + ko + len(r.out_shapes)]) for r, (_, ko, _) in zip(rides, spans)]
    return list(res[:n_out]), extras


def _run(ride, name):
    def body(*refs):
        n_in, n_out = len(ride.inputs), len(ride.out_shapes)
        parts = refs[:n_in], refs[n_in:n_in + n_out], refs[n_in + n_out:]
        ride.start(*parts)
        ride.finish(*parts)

    hbm = pl.BlockSpec(memory_space=pl.ANY)
    return list(pl.pallas_call(
        body, name=name, in_specs=[hbm] * len(ride.inputs), out_specs=[hbm] * len(ride.out_shapes),
        out_shape=ride.out_shapes, input_output_aliases=ride.aliases, scratch_shapes=ride.sems,
    )(*ride.inputs))


def _ffn_fwd(x, gain, wg, wu, wd, name, rides=None):
    t, d = x.shape
    ns, _, fs = wg.shape
    tm = min(256, t)

    def body(x_ref, gain_ref, wg_hbm, wu_hbm, wd_hbm, h_ref, xn_ref, g_ref, u_ref, a_ref, wg_v, wu_v, wd_v, sems):
        @pl.when(pl.program_id(0) == 0)
        def _():
            _load_weights([(wg_hbm, wg_v), (wu_hbm, wu_v), (wd_hbm, wd_v)], sems)

        xv = x_ref[...]
        r = lax.rsqrt(jnp.mean(xv * xv, axis=-1, keepdims=True) + NORM_EPS)
        xn = (xv * r * gain_ref[...]).astype(BF16)
        xn_ref[...] = xn
        acc = jnp.zeros((tm, d), F32)
        for j in range(ns):
            g = _dot(xn, wg_v[j])
            u = _dot(xn, wu_v[j])
            g_ref[j] = g.astype(BF16)
            u_ref[j] = u.astype(BF16)
            a = (g * _sigmoid(g) * u).astype(BF16)
            a_ref[j] = a
            acc = acc + _dot(a, wd_v[j])
        h_ref[...] = xv + 0.5 * acc

    hbm = pl.BlockSpec(memory_space=pl.ANY)
    hid = pl.BlockSpec((ns, tm, fs), lambda i: (0, i, 0))
    return _pallas(
        body, rides, name=name, grid=(t // tm,),
        in_specs=[pl.BlockSpec((tm, d), lambda i: (i, 0)), pl.BlockSpec((1, d), lambda i: (0, 0)), hbm, hbm, hbm],
        out_specs=[pl.BlockSpec((tm, d), lambda i: (i, 0)), pl.BlockSpec((tm, d), lambda i: (i, 0)), hid, hid, hid],
        out_shape=[jax.ShapeDtypeStruct((t, d), F32), jax.ShapeDtypeStruct((t, d), BF16)]
        + [jax.ShapeDtypeStruct((ns, t, fs), BF16)] * 3,
        scratch_shapes=[pltpu.VMEM(wg.shape, BF16), pltpu.VMEM(wu.shape, BF16), pltpu.VMEM(wd.shape, BF16),
                        pltpu.SemaphoreType.DMA((3,))],
        sem=("arbitrary",), args=[x, gain, wg, wu, wd])


def _ffn_bwd_data(dy, x, gain, g, u, wg, wu, wd, name, rides=None):
    t, d = x.shape
    ns, _, fs = wg.shape
    tm = min(256, t)

    def body(dy_ref, x_ref, gain_ref, g_ref, u_ref, wg_hbm, wu_hbm, wd_hbm, dx_ref, dg_ref, du_ref, dgain_ref,
             wg_v, wu_v, wd_v, sems):
        @pl.when(pl.program_id(0) == 0)
        def _():
            _load_weights([(wg_hbm, wg_v), (wu_hbm, wu_v), (wd_hbm, wd_v)], sems)
            dgain_ref[...] = jnp.zeros_like(dgain_ref)

        dyv = dy_ref[...]
        dyh = (0.5 * dyv).astype(BF16)
        dxn = jnp.zeros((tm, d), F32)
        for j in range(ns):
            da = _dot(dyh, wd_v[j], NT)
            gj = g_ref[j].astype(F32)
            uj = u_ref[j].astype(F32)
            sig = _sigmoid(gj)
            dgj = (da * uj * (sig * (1.0 + gj * (1.0 - sig)))).astype(BF16)
            duj = (da * (gj * sig)).astype(BF16)
            dg_ref[j] = dgj
            du_ref[j] = duj
            dxn = dxn + _dot(dgj, wg_v[j], NT) + _dot(duj, wu_v[j], NT)
        xv = x_ref[...]
        r = lax.rsqrt(jnp.mean(xv * xv, axis=-1, keepdims=True) + NORM_EPS)
        xh = xv * r
        dgain_ref[...] += _rows8(dxn * xh)
        dxh = dxn * gain_ref[...]
        dx_ref[...] = dyv + r * (dxh - xh * jnp.mean(dxh * xh, axis=-1, keepdims=True))

    hbm = pl.BlockSpec(memory_space=pl.ANY)
    tile = pl.BlockSpec((tm, d), lambda i: (i, 0))
    hid = pl.BlockSpec((ns, tm, fs), lambda i: (0, i, 0))
    return _pallas(
        body, rides, name=name, grid=(t // tm,),
        in_specs=[tile, tile, pl.BlockSpec((1, d), lambda i: (0, 0)), hid, hid, hbm, hbm, hbm],
        out_specs=[tile, hid, hid, pl.BlockSpec((8, d), lambda i: (0, 0))],
        out_shape=[jax.ShapeDtypeStruct((t, d), F32), jax.ShapeDtypeStruct((ns, t, fs), BF16),
                   jax.ShapeDtypeStruct((ns, t, fs), BF16), jax.ShapeDtypeStruct((8, d), F32)],
        scratch_shapes=[pltpu.VMEM(wg.shape, BF16), pltpu.VMEM(wu.shape, BF16), pltpu.VMEM(wd.shape, BF16),
                        pltpu.SemaphoreType.DMA((3,))],
        sem=("arbitrary",), args=[dy, x, gain, g, u, wg, wu, wd])


def _ffn_wgrad_down(a, dy, name, rides=None):
    t, d = dy.shape
    ns, _, fs = a.shape
    tk = min(1024, t)

    def body(dy_ref, a_ref, dwd_ref):
        @pl.when(pl.program_id(1) == 0)
        def _():
            dwd_ref[...] = jnp.zeros_like(dwd_ref)

        dwd_ref[...] += _dot(a_ref[...], (0.5 * dy_ref[...]).astype(BF16), TN)

    return _pallas(
        body, rides, name=name, grid=(ns, t // tk),
        in_specs=[pl.BlockSpec((tk, d), lambda j, k: (k, 0)), pl.BlockSpec((None, tk, fs), lambda j, k: (j, k, 0))],
        out_specs=[pl.BlockSpec((None, fs, d), lambda j, k: (j, 0, 0))],
        out_shape=[jax.ShapeDtypeStruct((ns, fs, d), F32)],
        sem=("arbitrary", "arbitrary"), args=[dy, a])


def _ffn_wgrad_gu(xn, dg, du, name, rides=None):
    t, d = xn.shape
    ns, _, fs = dg.shape
    tk = min(2048, t)

    def body(xn_ref, dg_ref, du_ref, dwg_ref, dwu_ref):
        @pl.when(pl.program_id(1) == 0)
        def _():
            dwg_ref[...] = jnp.zeros_like(dwg_ref)
            dwu_ref[...] = jnp.zeros_like(dwu_ref)

        xnv = xn_ref[...]
        dwg_ref[...] += _dot(xnv, dg_ref[...], TN)
        dwu_ref[...] += _dot(xnv, du_ref[...], TN)

    hid = pl.BlockSpec((None, tk, fs), lambda j, k: (j, k, 0))
    out = pl.BlockSpec((None, d, fs), lambda j, k: (j, 0, 0))
    return _pallas(
        body, rides, name=name, grid=(ns, t // tk),
        in_specs=[pl.BlockSpec((tk, d), lambda j, k: (k, 0)), hid, hid],
        out_specs=[out, out], out_shape=[jax.ShapeDtypeStruct((ns, d, fs), F32)] * 2,
        sem=("arbitrary", "arbitrary"), args=[xn, dg, du])


def _tn_matmul(a, b, bn, name):
    t, m = a.shape
    n = b.shape[1]
    tk = min(2048, t)

    def body(a_ref, b_ref, o_ref):
        @pl.when(pl.program_id(1) == 0)
        def _():
            o_ref[...] = jnp.zeros_like(o_ref)

        o_ref[...] += _dot(a_ref[...].astype(BF16), b_ref[...].astype(BF16), TN)

    return pl.pallas_call(
        body, name=name, grid=(n // bn, t // tk),
        in_specs=[pl.BlockSpec((tk, m), lambda j, k: (k, 0)), pl.BlockSpec((tk, bn), lambda j, k: (k, j))],
        out_specs=pl.BlockSpec((None, m, bn), lambda j, k: (j, 0, 0)),
        out_shape=jax.ShapeDtypeStruct((n // bn, m, bn), F32),
        compiler_params=_params("arbitrary", "arbitrary"),
    )(a, b)


def _chunk_scratch(tm, w):
    return pltpu.VMEM((w // LANE, tm, LANE), F32)


def _regroup_store(cbuf, out_ref, dil):
    n = out_ref.shape[1]
    for g in range(dil):
        for k in range(cbuf.shape[0]):
            rows = cbuf[k] if dil == 1 else cbuf[k, pl.ds(g, n, stride=dil), :]
            out_ref[g, :, k * LANE:(k + 1) * LANE] = rows.astype(out_ref.dtype)


def _natural_rows(ref, dil, cbuf):
    if dil == 1:
        return ref[0]
    n = ref.shape[1]
    for g in range(dil):
        for k in range(cbuf.shape[0]):
            cbuf[k, pl.ds(g, n, stride=dil), :] = ref[g, :, k * LANE:(k + 1) * LANE]
    return jnp.concatenate([cbuf[k] for k in range(cbuf.shape[0])], axis=1)


def _inproj_fwd(h, gain, win):
    t, d = h.shape
    ns, _, cs = win.shape
    tm = min(512, t)
    rw, aw = 4 * RET_WIDTH, 3 * ATT_WIDTH

    def body(h_ref, gain_ref, w_ref, xn_ref, ur_ref, *rest):
        a_refs, abuf = rest[:-1], rest[-1]
        hv = h_ref[...]
        r = lax.rsqrt(jnp.mean(hv * hv, axis=-1, keepdims=True) + NORM_EPS)
        xn = (hv * r * gain_ref[...]).astype(BF16)
        xn_ref[...] = xn
        for j in range(ns):
            res = _dot(xn, w_ref[j])
            for k in range(cs // LANE):
                chunk = j * (cs // LANE) + k
                piece = res[:, k * LANE:(k + 1) * LANE]
                if chunk < rw // LANE:
                    ur_ref[:, chunk * LANE:(chunk + 1) * LANE] = piece
                else:
                    abuf[chunk - rw // LANE] = piece
        for dil, a_ref in zip(DILATIONS, a_refs):
            _regroup_store(abuf, a_ref, dil)

    return pl.pallas_call(
        body, name="inproj_fwd", grid=(t // tm,),
        in_specs=[pl.BlockSpec((tm, d), lambda i: (i, 0)), pl.BlockSpec((1, d), lambda i: (0, 0)),
                  pl.BlockSpec(win.shape, lambda i: (0, 0, 0))],
        out_specs=[pl.BlockSpec((tm, d), lambda i: (i, 0)), pl.BlockSpec((tm, rw), lambda i: (i, 0))]
        + [pl.BlockSpec((dil, tm // dil, aw), lambda i: (0, i, 0)) for dil in DILATIONS],
        out_shape=[jax.ShapeDtypeStruct((t, d), BF16), jax.ShapeDtypeStruct((t, rw), F32)]
        + [jax.ShapeDtypeStruct((dil, t // dil, aw), BF16) for dil in DILATIONS],
        scratch_shapes=[_chunk_scratch(tm, aw)],
        compiler_params=_params("arbitrary"),
    )(h, gain, win)


def _inproj_bwd(pieces, h, gain, dres, win):
    t, d = h.shape
    ns, _, cs = win.shape
    pw = pieces[0].shape[1]
    tm = min(512, t)
    npc = len(pieces)

    def body(*refs):
        p_refs = refs[:npc]
        h_ref, gain_ref, dres_ref, w_ref, dh_ref, du_ref, dgain_ref = refs[npc:]

        @pl.when(pl.program_id(0) == 0)
        def _():
            dgain_ref[...] = jnp.zeros_like(dgain_ref)

        for k in range(npc):
            du_ref[:, k * pw:(k + 1) * pw] = p_refs[k][...]
        dxn = jnp.zeros((tm, d), F32)
        for j in range(ns):
            dxn = dxn + _dot(du_ref[:, j * cs:(j + 1) * cs], w_ref[j], NT)
        hv = h_ref[...]
        r = lax.rsqrt(jnp.mean(hv * hv, axis=-1, keepdims=True) + NORM_EPS)
        xh = hv * r
        dgain_ref[...] += _rows8(dxn * xh)
        dxh = dxn * gain_ref[...]
        dh_ref[...] = dres_ref[...] + r * (dxh - xh * jnp.mean(dxh * xh, axis=-1, keepdims=True))

    tile = pl.BlockSpec((tm, d), lambda i: (i, 0))
    return pl.pallas_call(
        body, name="inproj_bwd", grid=(t // tm,),
        in_specs=[pl.BlockSpec((tm, pw), lambda i: (i, 0))] * npc + [
            tile, pl.BlockSpec((1, d), lambda i: (0, 0)), tile, pl.BlockSpec(win.shape, lambda i: (0, 0, 0))],
        out_specs=[tile, pl.BlockSpec((tm, npc * pw), lambda i: (i, 0)), pl.BlockSpec((8, d), lambda i: (0, 0))],
        out_shape=[jax.ShapeDtypeStruct((t, d), F32), jax.ShapeDtypeStruct((t, npc * pw), BF16),
                   jax.ShapeDtypeStruct((8, d), F32)],
        compiler_params=_params("arbitrary"),
    )(*pieces, h, gain, dres, win)


def _outproj_fwd(h, mix_r, mix_a, wo):
    t, d = h.shape
    hw = mix_r.shape[1]
    tm = min(512, t)

    def body(h_ref, mr_ref, ma_ref, w_ref, o_ref):
        o_ref[...] = h_ref[...] + _dot(mr_ref[...], w_ref[0:hw, :]) + _dot(ma_ref[...], w_ref[hw:2 * hw, :])

    tile = pl.BlockSpec((tm, d), lambda i: (i, 0))
    half = pl.BlockSpec((tm, hw), lambda i: (i, 0))
    return pl.pallas_call(
        body, name="outproj_fwd", grid=(t // tm,),
        in_specs=[tile, half, half, pl.BlockSpec(wo.shape, lambda i: (0, 0))],
        out_specs=tile, out_shape=jax.ShapeDtypeStruct((t, d), F32),
        compiler_params=_params("arbitrary"),
    )(h, mix_r, mix_a, wo)


def _outproj_bwd(dh, wo, rides=None):
    t, d = dh.shape
    hw = wo.shape[0] // 2
    tm = min(512, t)

    def body(dh_ref, w_ref, dr_ref, da_ref):
        dhb = dh_ref[...].astype(BF16)
        dr_ref[...] = _dot(dhb, w_ref[0:hw, :], NT)
        da_ref[...] = _dot(dhb, w_ref[hw:2 * hw, :], NT)

    half = pl.BlockSpec((tm, hw), lambda i: (i, 0))
    return _pallas(
        body, rides, name="outproj_bwd", grid=(t // tm,),
        in_specs=[pl.BlockSpec((tm, d), lambda i: (i, 0)), pl.BlockSpec(wo.shape, lambda i: (0, 0))],
        out_specs=[half, half],
        out_shape=[jax.ShapeDtypeStruct((t, hw), F32), jax.ShapeDtypeStruct((t, hw), F32)],
        sem=("arbitrary",), args=[dh, wo])


def _retention_tables(t):
    pos = jnp.arange(t, dtype=F32)
    inv_freq = ROPE_BASE ** (-jnp.arange(0, RET_DIM, 2, dtype=F32) / RET_DIM)
    ang = jnp.repeat(pos[:, None] * inv_freq[None, :], 2, axis=-1)
    c = RET_CHUNK
    log_g = jnp.log(1.0 - 2.0 ** (-5.0 - jnp.arange(RET_HEADS, dtype=F32)))
    idx = jnp.arange(c, dtype=F32)
    rel = idx[:, None] - idx[None, :]
    decay = jnp.where(rel >= 0, jnp.exp(log_g[:, None, None] * jnp.maximum(rel, 0.0)), 0.0)
    zeta = jnp.exp(log_g[:, None] * (c - 1 - idx)[None, :])
    xi = jnp.exp(log_g[:, None] * (idx + 1)[None, :])
    gc = jnp.exp(log_g * c)
    wide = lambda v: jnp.broadcast_to(v[:, :, None], (RET_HEADS, c, LANE))
    return (jnp.cos(ang), jnp.sin(ang), decay, wide(zeta), wide(xi),
            jnp.broadcast_to(gc[:, None, None], (RET_HEADS, c, LANE)))


def _rot(v):
    lane = lax.broadcasted_iota(jnp.int32, v.shape, 1)
    nxt = pltpu.roll(v, LANE - 1, 1)
    prv = pltpu.roll(v, 1, 1)
    return jnp.where(lane % 2 == 0, -nxt, prv)


def _ret_specs(tr, rev, nt):
    ti = (lambda i: nt - 1 - i) if rev else (lambda i: i)
    col = lambda off: pl.BlockSpec((tr, LANE), lambda h, i: (ti(i), off + h))
    tab = pl.BlockSpec((tr, LANE), lambda h, i: (ti(i), 0))
    head = pl.BlockSpec((None, RET_CHUNK, LANE), lambda h, i: (h, 0, 0))
    return col, tab, head


def _ret_fwd(u, gain, tabs):
    t = u.shape[0]
    tr = min(1024, t)
    nt = t // tr
    cos, sin, decay, zeta, xi, gc = tabs
    scale = RET_DIM ** -0.5

    def body(q_ref, k_ref, v_ref, gt_ref, cos_ref, sin_ref, gain_ref, dec_ref, zeta_ref, xi_ref, gc_ref,
             raw_ref, mix_ref, state):
        @pl.when(pl.program_id(1) == 0)
        def _():
            state[...] = jnp.zeros_like(state)

        for ci in range(tr // RET_CHUNK):
            sl = pl.ds(ci * RET_CHUNK, RET_CHUNK)
            cs, sn = cos_ref[sl, :], sin_ref[sl, :]
            q, k = q_ref[sl, :], k_ref[sl, :]
            qb = (q * cs + _rot(q) * sn).astype(BF16)
            kr = (k * cs + _rot(k) * sn) * scale
            kb = kr.astype(BF16)
            vb = v_ref[sl, :].astype(BF16)
            s = _dot(qb, kb, NT) * dec_ref[...]
            st = state[...]
            o = _dot(s.astype(BF16), vb) + _dot(qb, st.astype(BF16)) * xi_ref[...]
            state[...] = st * gc_ref[...] + _dot((kr * zeta_ref[...]).astype(BF16), vb, TN)
            raw_ref[sl, :] = o
            mu = jnp.mean(o, axis=-1, keepdims=True)
            var = jnp.mean(jnp.square(o - mu), axis=-1, keepdims=True)
            y = (o - mu) * lax.rsqrt(var + GN_EPS) * gain_ref[...]
            gt = gt_ref[sl, :]
            mix_ref[sl, :] = (y * (gt * _sigmoid(gt))).astype(BF16)

    col, tab, head = _ret_specs(tr, False, nt)
    out = pl.BlockSpec((tr, LANE), lambda h, i: (i, h))
    return pl.pallas_call(
        body, name="ret_fwd", grid=(RET_HEADS, nt),
        in_specs=[col(0), col(4), col(8), col(12), tab, tab, pl.BlockSpec((1, LANE), lambda h, i: (0, h)),
                  head, head, head, head],
        out_specs=[out, out],
        out_shape=[jax.ShapeDtypeStruct((t, RET_WIDTH), F32), jax.ShapeDtypeStruct((t, RET_WIDTH), BF16)],
        scratch_shapes=[pltpu.VMEM((RET_DIM, RET_DIM), F32)],
        compiler_params=_params("arbitrary", "arbitrary"),
    )(u, u, u, u, cos, sin, gain, decay, zeta, xi, gc)


def _ret_bwd_q(dmix, raw, u, gain, tabs, rides=None):
    t = u.shape[0]
    tr = min(1024, t)
    nt = t // tr
    cos, sin, decay, zeta, xi, gc = tabs
    scale = RET_DIM ** -0.5

    def body(dm_ref, raw_ref, q_ref, k_ref, v_ref, gt_ref, cos_ref, sin_ref, gain_ref, dec_ref, zeta_ref, xi_ref, gc_ref,
             dq_ref, dgt_ref, dret_ref, dgain_ref, state):
        @pl.when(pl.program_id(1) == 0)
        def _():
            state[...] = jnp.zeros_like(state)
            dgain_ref[...] = jnp.zeros_like(dgain_ref)

        for ci in range(tr // RET_CHUNK):
            sl = pl.ds(ci * RET_CHUNK, RET_CHUNK)
            cs, sn = cos_ref[sl, :], sin_ref[sl, :]
            q, k = q_ref[sl, :], k_ref[sl, :]
            qb = (q * cs + _rot(q) * sn).astype(BF16)
            kr = (k * cs + _rot(k) * sn) * scale
            kb = kr.astype(BF16)
            vb = v_ref[sl, :].astype(BF16)
            o = raw_ref[sl, :]
            mu = jnp.mean(o, axis=-1, keepdims=True)
            var = jnp.mean(jnp.square(o - mu), axis=-1, keepdims=True)
            rs = lax.rsqrt(var + GN_EPS)
            n = (o - mu) * rs
            gt = gt_ref[sl, :]
            sig = _sigmoid(gt)
            dout = dm_ref[sl, :]
            dgt_ref[sl, :] = (dout * (n * gain_ref[...]) * (sig * (1.0 + gt * (1.0 - sig)))).astype(BF16)
            dy = dout * (gt * sig)
            dgain_ref[...] += _rows8(dy * n)
            dn = dy * gain_ref[...]
            do = rs * (dn - jnp.mean(dn, axis=-1, keepdims=True) - n * jnp.mean(dn * n, axis=-1, keepdims=True))
            dret_ref[sl, :] = do
            ds = _dot(do.astype(BF16), vb, NT) * dec_ref[...]
            st = state[...]
            dqr = _dot(ds.astype(BF16), kb) + _dot((do * xi_ref[...]).astype(BF16), st.astype(BF16), NT)
            dq_ref[sl, :] = (dqr * cs - _rot(dqr * sn)).astype(BF16)
            state[...] = st * gc_ref[...] + _dot((kr * zeta_ref[...]).astype(BF16), vb, TN)

    col, tab, head = _ret_specs(tr, False, nt)
    out = pl.BlockSpec((tr, LANE), lambda h, i: (i, h))
    return _pallas(
        body, rides, name="ret_bwd_q", grid=(RET_HEADS, nt),
        in_specs=[out, out, col(0), col(4), col(8), col(12), tab, tab, pl.BlockSpec((1, LANE), lambda h, i: (0, h)),
                  head, head, head, head],
        out_specs=[out, out, out, pl.BlockSpec((8, LANE), lambda h, i: (0, h))],
        out_shape=[jax.ShapeDtypeStruct((t, RET_WIDTH), BF16), jax.ShapeDtypeStruct((t, RET_WIDTH), BF16),
                   jax.ShapeDtypeStruct((t, RET_WIDTH), F32), jax.ShapeDtypeStruct((8, RET_WIDTH), F32)],
        scratch_shapes=[pltpu.VMEM((RET_DIM, RET_DIM), F32)],
        sem=("arbitrary", "arbitrary"), args=[dmix, raw, u, u, u, u, cos, sin, gain, decay, zeta, xi, gc])


def _ret_bwd_kv(dret, u, tabs, rides=None):
    t = u.shape[0]
    tr = min(1024, t)
    nt = t // tr
    cos, sin, decay, zeta, xi, gc = tabs
    scale = RET_DIM ** -0.5

    def body(do_ref, q_ref, k_ref, v_ref, cos_ref, sin_ref, dec_ref, zeta_ref, xi_ref, gc_ref, dk_ref, dv_ref, gst):
        @pl.when(pl.program_id(1) == 0)
        def _():
            gst[...] = jnp.zeros_like(gst)

        for ci in reversed(range(tr // RET_CHUNK)):
            sl = pl.ds(ci * RET_CHUNK, RET_CHUNK)
            cs, sn = cos_ref[sl, :], sin_ref[sl, :]
            q, k = q_ref[sl, :], k_ref[sl, :]
            qb = (q * cs + _rot(q) * sn).astype(BF16)
            kr = (k * cs + _rot(k) * sn) * scale
            kb = kr.astype(BF16)
            vb = v_ref[sl, :].astype(BF16)
            do = do_ref[sl, :]
            dob = do.astype(BF16)
            s = (_dot(qb, kb, NT) * dec_ref[...]).astype(BF16)
            ds = (_dot(dob, vb, NT) * dec_ref[...]).astype(BF16)
            gb = gst[...].astype(BF16)
            dv_ref[sl, :] = (_dot(s, dob, TN) + _dot((kr * zeta_ref[...]).astype(BF16), gb)).astype(BF16)
            dkr = (_dot(ds, qb, TN) + _dot(vb, gb, NT) * zeta_ref[...]) * scale
            dk_ref[sl, :] = (dkr * cs - _rot(dkr * sn)).astype(BF16)
            gst[...] = gst[...] * gc_ref[...] + _dot(qb, (do * xi_ref[...]).astype(BF16), TN)

    col, tab, head = _ret_specs(tr, True, nt)
    out = pl.BlockSpec((tr, LANE), lambda h, i: (nt - 1 - i, h))
    return _pallas(
        body, rides, name="ret_bwd_kv", grid=(RET_HEADS, nt),
        in_specs=[out, col(0), col(4), col(8), tab, tab, head, head, head, head],
        out_specs=[out, out],
        out_shape=[jax.ShapeDtypeStruct((t, RET_WIDTH), BF16), jax.ShapeDtypeStruct((t, RET_WIDTH), BF16)],
        scratch_shapes=[pltpu.VMEM((RET_DIM, RET_DIM), F32)],
        sem=("arbitrary", "arbitrary"), args=[dret, u, u, u, cos, sin, decay, zeta, xi, gc])


PAIRS = ATT_WIDTH // LANE
ATT_Q_BLK, ATT_K_BLK, ATT_V_BLK = 0, PAIRS, 2 * PAIRS
STAT_LANES = ATT_DIM // 2


def _att_tiles(t, dil):
    sub = t // dil
    tq = min(512, sub)
    return sub, tq, sub // tq, tq // ATT_BLOCK


def _att_in_specs(tq, qb, ti):
    cur = lambda off: pl.BlockSpec((None, tq, LANE), lambda g, p, i: (g, ti(i), off + p))
    prev = lambda off: pl.BlockSpec((None, ATT_BLOCK, LANE), lambda g, p, i: (g, jnp.maximum(ti(i) * qb - 1, 0), off + p))
    return [cur(ATT_Q_BLK), cur(ATT_K_BLK), prev(ATT_K_BLK), cur(ATT_V_BLK), prev(ATT_V_BLK)]


def _band_mask():
    key = lax.broadcasted_iota(jnp.int32, (2 * ATT_BLOCK, 2 * ATT_BLOCK), 0)
    qry = lax.broadcasted_iota(jnp.int32, (2 * ATT_BLOCK, 2 * ATT_BLOCK), 1) % ATT_BLOCK
    dist = qry + ATT_BLOCK - key
    return (dist >= 0) & (dist <= ATT_BLOCK), key >= ATT_BLOCK


def _head0_lanes():
    return lax.broadcasted_iota(jnp.int32, (ATT_BLOCK, LANE), 1) < ATT_DIM


def _stack_heads(v, head0):
    zero = jnp.zeros((), v.dtype)
    return jnp.concatenate([jnp.where(head0, v, zero), jnp.where(head0, zero, v)], axis=0)


def _unstack_heads(v, head0):
    return jnp.where(head0, v[0:ATT_BLOCK], v[ATT_BLOCK:])


def _att_fwd(ua, dil):
    sub = ua.shape[1]
    _, tq, nq, qb = _att_tiles(sub * dil, dil)

    def body(q_ref, kc_ref, kp_ref, vc_ref, vp_ref, o_ref, l_ref, kx, vx):
        tile = pl.program_id(2)
        kx[0:ATT_BLOCK, :] = kp_ref[...]
        kx[ATT_BLOCK:, :] = kc_ref[...]
        vx[0:ATT_BLOCK, :] = vp_ref[...]
        vx[ATT_BLOCK:, :] = vc_ref[...]
        band, cur_keys = _band_mask()
        head0 = _head0_lanes()
        for b in range(qb):
            rows = slice(b * ATT_BLOCK, (b + 1) * ATT_BLOCK)
            mask = band if b > 0 else band & (cur_keys | (tile > 0))
            qv = q_ref[rows, :] * jnp.asarray(ATT_DIM ** -0.5, BF16)
            kv = kx[b * ATT_BLOCK:(b + 2) * ATT_BLOCK, :]
            vv = vx[b * ATT_BLOCK:(b + 2) * ATT_BLOCK, :]
            st = jnp.where(mask, _dot(kv, _stack_heads(qv, head0), NT), -1e30)
            m = jnp.max(st, axis=0, keepdims=True)
            ex = jnp.exp(st - m)
            den = jnp.sum(ex, axis=0, keepdims=True)
            o_ref[rows, :] = _unstack_heads(_dot((ex * (1.0 / den)).astype(BF16), vv, TN), head0)
            lse = m + jnp.log(den)
            cols = [jnp.broadcast_to(lse[:, e * ATT_BLOCK:(e + 1) * ATT_BLOCK], (ATT_BLOCK, LANE)).T for e in range(2)]
            l_ref[rows, :] = jnp.where(head0, cols[0], cols[1])

    out = pl.BlockSpec((None, tq, LANE), lambda g, p, i: (g, i, p))
    return pl.pallas_call(
        body, name=f"att_fwd_d{dil}", grid=(dil, PAIRS, nq),
        in_specs=_att_in_specs(tq, qb, lambda i: i),
        out_specs=[out, out],
        out_shape=[jax.ShapeDtypeStruct((dil, sub, ATT_WIDTH), F32)] * 2,
        scratch_shapes=[pltpu.VMEM((tq + ATT_BLOCK, LANE), BF16)] * 2,
        compiler_params=_params("arbitrary", "arbitrary", "arbitrary"),
    )(ua, ua, ua, ua, ua)


def _regrouped_spec(tm, dil, w):
    return pl.BlockSpec((dil, tm // dil, w), lambda i: (0, i, 0))


def _att_combine(outs, lses, t):
    w = ATT_WIDTH
    tm = min(512, t)
    nb = len(outs)

    def body(*refs):
        o_refs, l_refs = refs[:nb], refs[nb:2 * nb]
        mix_ref, att_ref, lse_ref, buf = refs[2 * nb:]
        ls = [_natural_rows(r, dil, buf) for r, dil in zip(l_refs, DILATIONS)]
        m = functools.reduce(jnp.maximum, ls)
        ws = [jnp.exp(l - m) for l in ls]
        den = functools.reduce(jnp.add, ws)
        att = functools.reduce(jnp.add, [(wt / den) * _natural_rows(r, dil, buf) for wt, r, dil in zip(ws, o_refs, DILATIONS)])
        att_ref[...] = att
        mix_ref[...] = att.astype(BF16)
        lse_ref[...] = m + jnp.log(den)

    tile = pl.BlockSpec((tm, w), lambda i: (i, 0))
    regrouped = [_regrouped_spec(tm, dil, w) for dil in DILATIONS]
    return pl.pallas_call(
        body, name="att_combine", grid=(t // tm,),
        in_specs=regrouped * 2, out_specs=[tile, tile, tile],
        out_shape=[jax.ShapeDtypeStruct((t, w), BF16), jax.ShapeDtypeStruct((t, w), F32), jax.ShapeDtypeStruct((t, w), F32)],
        scratch_shapes=[_chunk_scratch(tm, w)],
        compiler_params=_params("arbitrary"),
    )(*outs, *lses)


def _att_bwd_prep(datt, att, lse):
    t, w = datt.shape
    tm = min(512, t)

    def body(da_ref, at_ref, l_ref, *rest):
        outs, dbuf, sbuf = rest[:-2], rest[-2], rest[-1]
        dav = da_ref[...]
        prod = dav * at_ref[...]
        lane = lax.broadcasted_iota(jnp.int32, (tm, LANE), 1)
        for k in range(w // LANE):
            cols = slice(k * LANE, (k + 1) * LANE)
            dbuf[k] = dav[:, cols]
            delta = jnp.concatenate(
                [jnp.broadcast_to(jnp.sum(prod[:, k * LANE + e * ATT_DIM:k * LANE + (e + 1) * ATT_DIM], axis=-1, keepdims=True),
                                  (tm, ATT_DIM)) for e in range(LANE // ATT_DIM)], axis=1)
            sbuf[k] = jnp.where(lane % ATT_DIM < STAT_LANES, l_ref[:, cols], delta)
        for k, dil in enumerate(DILATIONS):
            _regroup_store(dbuf, outs[2 * k], dil)
            _regroup_store(sbuf, outs[2 * k + 1], dil)

    tile = pl.BlockSpec((tm, w), lambda i: (i, 0))
    res = pl.pallas_call(
        body, name="att_bwd_prep", grid=(t // tm,),
        in_specs=[tile] * 3,
        out_specs=[_regrouped_spec(tm, dil, w) for dil in DILATIONS for _ in range(2)],
        out_shape=[jax.ShapeDtypeStruct((dil, t // dil, w), dt) for dil in DILATIONS for dt in (BF16, F32)],
        scratch_shapes=[_chunk_scratch(tm, w)] * 2,
        compiler_params=_params("arbitrary"),
    )(datt, att, lse)
    return [(res[2 * k], res[2 * k + 1]) for k in range(len(DILATIONS))]


def _att_bwd(ua, da, stat, dil, rides=None):
    sub = ua.shape[1]
    _, tq, nq, qb = _att_tiles(sub * dil, dil)
    scale = ATT_DIM ** -0.5

    def body(q_ref, kc_ref, kp_ref, vc_ref, vp_ref, da_ref, st_ref, dq_ref, dk_ref, dv_ref, kx, vx, dkx, dvx, ck, cv):
        step = pl.program_id(2)
        tile = nq - 1 - step

        @pl.when(step == 0)
        def _():
            ck[...] = jnp.zeros_like(ck)
            cv[...] = jnp.zeros_like(cv)

        kx[0:ATT_BLOCK, :] = kp_ref[...]
        kx[ATT_BLOCK:, :] = kc_ref[...]
        vx[0:ATT_BLOCK, :] = vp_ref[...]
        vx[ATT_BLOCK:, :] = vc_ref[...]
        dkx[...] = jnp.zeros_like(dkx)
        dvx[...] = jnp.zeros_like(dvx)
        band, cur_keys = _band_mask()
        head0 = _head0_lanes()
        for b in range(qb):
            rows = slice(b * ATT_BLOCK, (b + 1) * ATT_BLOCK)
            keys = slice(b * ATT_BLOCK, (b + 2) * ATT_BLOCK)
            mask = band if b > 0 else band & (cur_keys | (tile > 0))
            qq = _stack_heads(q_ref[rows, :] * jnp.asarray(scale, BF16), head0)
            dd = _stack_heads(da_ref[rows, :], head0)
            kv, vv = kx[keys, :], vx[keys, :]
            stat = st_ref[rows, :].T
            row = lambda k: jnp.concatenate([stat[e * ATT_DIM + k:e * ATT_DIM + k + 1, :] for e in range(2)], axis=1)
            pt = jnp.where(mask, jnp.exp(_dot(kv, qq, NT) - row(0)), 0.0)
            dst = (pt * (_dot(vv, dd, NT) - row(STAT_LANES))).astype(BF16)
            dq_ref[rows, :] = _unstack_heads(_dot(dst, kv, TN), head0) * scale
            dkx[keys, :] += _dot(dst, qq)
            dvx[keys, :] += _dot(pt.astype(BF16), dd)
        dkx[tq:, :] += ck[...]
        dvx[tq:, :] += cv[...]
        dk_ref[...] = dkx[ATT_BLOCK:, :]
        dv_ref[...] = dvx[ATT_BLOCK:, :]
        ck[...] = dkx[0:ATT_BLOCK, :]
        cv[...] = dvx[0:ATT_BLOCK, :]

    ti = lambda i: nq - 1 - i
    out = pl.BlockSpec((None, tq, LANE), lambda g, p, i: (g, ti(i), p))
    shape = jax.ShapeDtypeStruct((dil, sub, ATT_WIDTH), F32)
    return _pallas(
        body, rides, name=f"att_bwd_d{dil}", grid=(dil, PAIRS, nq),
        in_specs=_att_in_specs(tq, qb, ti) + [out, out],
        out_specs=[out, out, out], out_shape=[shape] * 3,
        scratch_shapes=[pltpu.VMEM((tq + ATT_BLOCK, LANE), BF16)] * 2 + [pltpu.VMEM((tq + ATT_BLOCK, LANE), F32)] * 2
        + [pltpu.VMEM((ATT_BLOCK, LANE), F32)] * 2,
        sem=("arbitrary", "arbitrary", "arbitrary"), args=[ua, ua, ua, ua, ua, da, stat])


def _att_bwd_sum(parts, t):
    w = ATT_WIDTH
    tm = min(512, t)
    nk = len(parts[0])

    def body(*refs):
        ins, outs, buf = refs[:-nk - 1], refs[-nk - 1:-1], refs[-1]
        for k in range(nk):
            acc = None
            for b, dil in enumerate(DILATIONS):
                rows = _natural_rows(ins[b * nk + k], dil, buf)
                acc = rows if acc is None else acc + rows
            outs[k][...] = acc.astype(BF16)

    tile = pl.BlockSpec((tm, w), lambda i: (i, 0))
    return pl.pallas_call(
        body, name="att_bwd_sum", grid=(t // tm,),
        in_specs=[_regrouped_spec(tm, dil, w) for dil in DILATIONS for _ in range(nk)], out_specs=[tile] * nk,
        out_shape=[jax.ShapeDtypeStruct((t, w), BF16)] * nk,
        scratch_shapes=[_chunk_scratch(tm, w)],
        compiler_params=_params("arbitrary"),
    )(*[a for p in parts for a in p])


def _loss_bwd(h, gain, target):
    t, d = h.shape
    tm = min(512, t)

    def body(h_ref, gain_ref, tg_ref, loss_ref, dh_ref, dgain_ref):
        @pl.when(pl.program_id(0) == 0)
        def _():
            loss_ref[...] = jnp.zeros_like(loss_ref)
            dgain_ref[...] = jnp.zeros_like(dgain_ref)

        hv = h_ref[...]
        r = lax.rsqrt(jnp.mean(hv * hv, axis=-1, keepdims=True) + NORM_EPS)
        xh = hv * r
        err = xh * gain_ref[...] - tg_ref[...]
        sq = _rows8(jnp.square(err))
        loss_ref[...] += 0.5 * functools.reduce(jnp.add, [sq[:, k * LANE:(k + 1) * LANE] for k in range(d // LANE)]) / d
        dy = err / d
        dgain_ref[...] += _rows8(dy * xh)
        dxh = dy * gain_ref[...]
        dh_ref[...] = r * (dxh - xh * jnp.mean(dxh * xh, axis=-1, keepdims=True))

    tile = pl.BlockSpec((tm, d), lambda i: (i, 0))
    return pl.pallas_call(
        body, name="loss_bwd", grid=(t // tm,),
        in_specs=[tile, pl.BlockSpec((1, d), lambda i: (0, 0)), tile],
        out_specs=[pl.BlockSpec((8, LANE), lambda i: (0, 0)), tile, pl.BlockSpec((8, d), lambda i: (0, 0))],
        out_shape=[jax.ShapeDtypeStruct((8, LANE), F32), jax.ShapeDtypeStruct((t, d), F32), jax.ShapeDtypeStruct((8, d), F32)],
        compiler_params=_params("arbitrary"),
    )(h, gain, target)


class _Reduction:
    def __init__(self, place, names, grads):
        self.place, self.names, self.grads = place, names, grads

    def pair(self):
        return _pair_ride(self.grads)

    def chips(self, got):
        self.got = got
        return _chip_ride([_pair_sum(self.place, g, r, f"pair_sum_{n}") for g, r, n in zip(self.grads, got, self.names)])

    def halves(self, others):
        return [_chip_sum(self.place, g, r, o, f"chip_sum_{n}")
                for g, r, o, n in zip(self.grads, self.got, others, self.names)]


def _step(x, target, gains, w, place=None):
    t = x.shape[0]
    ex = place is not None
    g_ffn1, g_mix, g_ret, g_ffn2, g_fin = gains
    w = list(w)
    tabs = _retention_tables(t)
    red = lambda names, grads: _Reduction(place, names, grads) if ex else None
    ride = lambda r: [r] if ex else None

    if ex:
        w[0:3] = _run(_gather_ride(w[0:3]), "gather_ffn1_weights")
    (h1, xn1, ga1, ua1, act1), rest = _ffn_fwd(x, g_ffn1, *w[0:3], "ffn1_fwd", ride(_gather_ride(w[3:])) if ex else None)
    if ex:
        w[3:] = rest[0]
    wg1, wu1, wd1, win, wo, wg2, wu2, wd2 = w
    wo2 = wo.reshape(wo.shape[0] * wo.shape[1], wo.shape[2])
    xnm, u, *uas = _inproj_fwd(h1, g_mix, win)
    raw, mix_r = _ret_fwd(u, g_ret, tabs)
    branches = [_att_fwd(ua, dil) for ua, dil in zip(uas, DILATIONS)]
    mix_a, att, lse = _att_combine([b[0] for b in branches], [b[1] for b in branches], t)
    h2 = _outproj_fwd(h1, mix_r, mix_a, wo2)
    (h3, xn2, ga2, ua2, act2), _ = _ffn_fwd(h2, g_ffn2, wg2, wu2, wd2, "ffn2_fwd")
    loss_p, dh3, dg_fin = _loss_bwd(h3, g_fin, target)

    (dwd2,), _ = _ffn_wgrad_down(act2, dh3, "ffn2_wgrad_down")
    r_d2 = red(["ffn2_w_down"], [dwd2])
    (dh2, dga2, dua2, dg_ffn2), e = _ffn_bwd_data(dh3, h2, g_ffn2, ga2, ua2, wg2, wu2, wd2, "ffn2_bwd",
                                                  ex and [r_d2.pair()])
    (dwg2, dwu2), e = _ffn_wgrad_gu(xn2, dga2, dua2, "ffn2_wgrad_gu", ex and [r_d2.chips(e[0])])
    r_gu2 = red(["ffn2_w_gate", "ffn2_w_up"], [dwg2, dwu2])
    (dmix_r, dmix_a), e = _outproj_bwd(dh2, wo2, ex and [r_gu2.pair(), _finish_ride(r_d2.halves(e[0]))])
    if ex:
        got_gu2, (dwd2,) = e
    hw = RET_WIDTH // (wo.shape[1])
    dwo = jnp.concatenate([_tn_matmul(mix_r, dh2, dh2.shape[1], "wo_grad_r").reshape(hw, wo.shape[1], wo.shape[2]),
                           _tn_matmul(mix_a, dh2, dh2.shape[1], "wo_grad_a").reshape(hw, wo.shape[1], wo.shape[2])])
    r_wo = red(["w_out"], [dwo])
    (dq_r, dgt_r, dret, dg_ret), e = _ret_bwd_q(dmix_r, raw, u, g_ret, tabs, ex and [r_gu2.chips(got_gu2)])
    (dk_r, dv_r), e = _ret_bwd_kv(dret, u, tabs, ex and [r_wo.pair(), _finish_ride(r_gu2.halves(e[0]))])
    if ex:
        got_wo, (dwg2, dwu2) = e
    prep = _att_bwd_prep(dmix_a, att, lse)
    p1, e = _att_bwd(uas[0], *prep[0], DILATIONS[0], ex and [r_wo.chips(got_wo)])
    p4, e = _att_bwd(uas[1], *prep[1], DILATIONS[1], ex and [_finish_ride(r_wo.halves(e[0]))])
    if ex:
        (dwo,), = e
    p16, _ = _att_bwd(uas[2], *prep[2], DILATIONS[2])
    dq_a, dk_a, dv_a = _att_bwd_sum([p1, p4, p16], t)
    dh1, du, dg_mix = _inproj_bwd([dq_r, dk_r, dv_r, dgt_r, dq_a, dk_a, dv_a], h1, g_mix, dh2, win)
    dwin = _tn_matmul(xnm, du, win.shape[2], "win_grad")
    r_in = red(["w_in"], [dwin])
    (dwd1,), e = _ffn_wgrad_down(act1, dh1, "ffn1_wgrad_down", ex and [r_in.pair()])
    r_d1 = red(["ffn1_w_down"], [dwd1])
    (dx, dga1, dua1, dg_ffn1), e = _ffn_bwd_data(dh1, x, g_ffn1, ga1, ua1, wg1, wu1, wd1, "ffn1_bwd",
                                                  ex and [r_in.chips(e[0]), r_d1.pair()])
    (dwg1, dwu1), e = _ffn_wgrad_gu(xn1, dga1, dua1, "ffn1_wgrad_gu",
                                    ex and [_finish_ride(r_in.halves(e[0])), r_d1.chips(e[1])])
    gain_parts = [dg_ffn1, dg_mix, dg_ret, dg_ffn2, dg_fin]
    if not ex:
        return loss_p, dx, [dwg1, dwu1, dwd1, dwin, dwo, dwg2, dwu2, dwd2], gain_parts
    (dwin,), oth_d1 = e
    r_gu1 = red(["ffn1_w_gate", "ffn1_w_up"], [dwg1, dwu1])
    got = _run(r_gu1.pair(), "pair_exchange_ffn1_gate_up")
    oth = _run(r_gu1.chips(got), "chip_exchange_ffn1_gate_up")
    last = r_gu1.halves(oth) + r_d1.halves(oth_d1)
    dwg1, dwu1, dwd1, gall = _run(_finish_ride(last, _pack_gains(gain_parts, x.shape[1])), "finish_exchange_ffn1")
    return loss_p, dx, [dwg1, dwu1, dwd1, dwin, dwo, dwg2, dwu2, dwd2], gall


N_DEV = 8
GAIN_ROWS = 8


def _place():
    x, y, c = lax.axis_index("x"), lax.axis_index("y"), lax.axis_index("c")
    chips = [(1 - x, y), (x, 1 - y), (1 - x, 1 - y)]
    return x, y, c, chips


def _hbm_specs(n):
    return [pl.BlockSpec(memory_space=pl.ANY)] * n


def _place_shard(place, w, name):
    r, cols = w.shape
    tr = r // 4

    def body(place_ref, w_ref, o_ref):
        o_ref[...] = w_ref[...].astype(BF16)

    return pl.pallas_call(
        body, name=name,
        grid_spec=pltpu.PrefetchScalarGridSpec(
            num_scalar_prefetch=1, grid=(r // tr,),
            in_specs=[pl.BlockSpec((tr, cols), lambda i, pr: (i, 0))],
            out_specs=pl.BlockSpec((None, tr, cols), lambda i, pr: (pr[0], i, 0))),
        out_shape=jax.ShapeDtypeStruct((N_SHARD, r, cols), BF16),
        compiler_params=_params("arbitrary"),
    )(place, w)


def _gather_ride(bufs):
    na = len(bufs)

    def legs(outs, sems):
        send_sem, recv_sem, fsend_sem, frecv_sem = sems
        x, y, c, chips = _place()

        def half(a, idx, which):
            hr = outs[a].shape[1] // 2
            return outs[a].at[idx, pl.ds(which * hr, hr)]

        def ici(a, j, idx):
            px, py = chips[j]
            return pltpu.make_async_remote_copy(
                src_ref=half(a, idx, c), dst_ref=half(a, idx, c),
                send_sem=send_sem.at[a, j], recv_sem=recv_sem.at[a, j], device_id=(px, py, c), device_id_type=MESH)

        def d2d(a, j, idx, which):
            return pltpu.make_async_remote_copy(
                src_ref=half(a, idx, which), dst_ref=half(a, idx, which),
                send_sem=fsend_sem.at[a, j], recv_sem=frecv_sem.at[a, j], device_id=(x, y, 1 - c), device_id_type=MESH)

        return 2 * x + y, c, chips, ici, d2d

    def start(ins, outs, sems):
        me, _, _, ici, _ = legs(outs, sems)
        for a in range(na):
            for j in range(3):
                ici(a, j, me).start()

    def finish(ins, outs, sems):
        me, c, chips, ici, d2d = legs(outs, sems)
        passed = []
        for a in range(na):
            for j, (px, py) in enumerate(chips):
                ici(a, j, 2 * px + py).wait_recv()
                cp = d2d(a, j, 2 * px + py, c)
                cp.start()
                passed.append(cp)
        for a in range(na):
            for j, (px, py) in enumerate(chips):
                d2d(a, j, 2 * px + py, 1 - c).wait_recv()
        for a in range(na):
            for j in range(3):
                ici(a, j, me).wait_send()
        for cp in passed:
            cp.wait_send()

    return _Ride(bufs, [jax.ShapeDtypeStruct(b.shape, b.dtype) for b in bufs], [pltpu.SemaphoreType.DMA((na, 3))] * 4,
                 start, finish, {a: a for a in range(na)})


def _pair_ride(grads):
    na = len(grads)

    def copies(ins, outs, sems):
        send_sem, recv_sem = sems
        x, y, c, _ = _place()
        res = []
        for a in range(na):
            hr = ins[a].shape[1] // 2
            res.append(pltpu.make_async_remote_copy(
                src_ref=ins[a].at[:, pl.ds((1 - c) * hr, hr)], dst_ref=outs[a],
                send_sem=send_sem.at[a], recv_sem=recv_sem.at[a], device_id=(x, y, 1 - c), device_id_type=MESH))
        return res

    def start(ins, outs, sems):
        for cp in copies(ins, outs, sems):
            cp.start()

    def finish(ins, outs, sems):
        for cp in copies(ins, outs, sems):
            cp.wait()

    return _Ride(grads, [jax.ShapeDtypeStruct((g.shape[0], g.shape[1] // 2, g.shape[2]), g.dtype) for g in grads],
                 [pltpu.SemaphoreType.DMA((na,))] * 2, start, finish)


def _chip_ride(sums):
    na = len(sums)

    def copies(ins, outs, sems):
        send_sem, recv_sem = sems
        x, y, c, chips = _place()
        res = []
        for a in range(na):
            for j, (px, py) in enumerate(chips):
                res.append(pltpu.make_async_remote_copy(
                    src_ref=ins[a].at[2 * px + py], dst_ref=outs[a].at[j],
                    send_sem=send_sem.at[a, j], recv_sem=recv_sem.at[a, j], device_id=(px, py, c), device_id_type=MESH))
        return res

    def start(ins, outs, sems):
        for cp in copies(ins, outs, sems):
            cp.start()

    def finish(ins, outs, sems):
        for cp in copies(ins, outs, sems):
            cp.wait()

    return _Ride(sums, [jax.ShapeDtypeStruct((3,) + s.shape[1:], s.dtype) for s in sums],
                 [pltpu.SemaphoreType.DMA((na, 3))] * 2, start, finish)


def _finish_ride(grads, gpack=None):
    na = len(grads)

    def halves(outs, sems, which):
        x, y, c, _ = _place()
        res = []
        for a in range(na):
            hr = outs[a].shape[0] // 2
            rows = outs[a].at[pl.ds((c if which == "mine" else 1 - c) * hr, hr)]
            res.append(pltpu.make_async_remote_copy(
                src_ref=rows, dst_ref=rows, send_sem=sems[0].at[a], recv_sem=sems[1].at[a],
                device_id=(x, y, 1 - c), device_id_type=MESH))
        return res

    def gains(ins, outs, sems):
        x, y, c, _ = _place()
        dev = 4 * x + 2 * y + c
        g_in, g_out = ins[na], outs[na]
        own = pltpu.make_async_copy(g_in, g_out.at[dev], sems[2])
        sends, lands = [], []
        for k in range(N_DEV - 1):
            bx, by, bc = (k + 1) // 4, ((k + 1) // 2) % 2, (k + 1) % 2
            peer = (jnp.bitwise_xor(x, bx), jnp.bitwise_xor(y, by), jnp.bitwise_xor(c, bc))
            sends.append(pltpu.make_async_remote_copy(
                src_ref=g_in, dst_ref=g_out.at[dev], send_sem=sems[3].at[k], recv_sem=sems[4].at[k],
                device_id=peer, device_id_type=MESH))
            slot = g_out.at[jnp.bitwise_xor(dev, k + 1)]
            lands.append(pltpu.make_async_remote_copy(
                src_ref=slot, dst_ref=slot, send_sem=sems[3].at[k], recv_sem=sems[4].at[k],
                device_id=peer, device_id_type=MESH))
        return own, sends, lands

    def start(ins, outs, sems):
        for cp in halves(outs, sems, "mine"):
            cp.start()
        if gpack is not None:
            own, sends, _ = gains(ins, outs, sems)
            own.start()
            for cp in sends:
                cp.start()

    def finish(ins, outs, sems):
        for cp in halves(outs, sems, "sibling's"):
            cp.wait_recv()
        if gpack is not None:
            own, sends, lands = gains(ins, outs, sems)
            for cp in lands:
                cp.wait_recv()
            for cp in sends:
                cp.wait_send()
            own.wait()
        for cp in halves(outs, sems, "mine"):
            cp.wait_send()

    shapes = [jax.ShapeDtypeStruct(g.shape, g.dtype) for g in grads]
    sems = [pltpu.SemaphoreType.DMA((na,))] * 2
    if gpack is None:
        return _Ride(grads, shapes, sems, start, finish, {a: a for a in range(na)})
    return _Ride(list(grads) + [gpack], shapes + [jax.ShapeDtypeStruct((N_DEV,) + gpack.shape, gpack.dtype)],
                 sems + [pltpu.SemaphoreType.DMA, pltpu.SemaphoreType.DMA((N_DEV - 1,)), pltpu.SemaphoreType.DMA((N_DEV - 1,))],
                 start, finish, {a: a for a in range(na)})


def _pair_sum(place, grad, got, name):
    ns, r, cols = grad.shape
    hr = r // 2

    def body(place_ref, g_ref, r_ref, o_ref):
        o_ref[...] = (g_ref[...] + r_ref[...]).astype(BF16)

    return pl.pallas_call(
        body, name=name,
        grid_spec=pltpu.PrefetchScalarGridSpec(
            num_scalar_prefetch=1, grid=(ns,),
            in_specs=[pl.BlockSpec((None, hr, cols), lambda s, pr: (s, pr[1], 0)),
                      pl.BlockSpec((None, hr, cols), lambda s, pr: (s, 0, 0))],
            out_specs=pl.BlockSpec((None, hr, cols), lambda s, pr: (s, 0, 0))),
        out_shape=jax.ShapeDtypeStruct((ns, hr, cols), BF16),
        compiler_params=_params("arbitrary"),
    )(place, grad, got)


def _chip_sum(place, grad, got, others, name):
    ns, r, cols = grad.shape
    hr = r // 2
    nb = 2
    tr = hr // nb

    def body(place_ref, g_ref, r_ref, o3_ref, o_ref):
        acc = g_ref[...] + r_ref[...]
        for j in range(3):
            acc = acc + o3_ref[j].astype(F32)
        o_ref[...] = acc

    return pl.pallas_call(
        body, name=name,
        grid_spec=pltpu.PrefetchScalarGridSpec(
            num_scalar_prefetch=1, grid=(nb,),
            in_specs=[pl.BlockSpec((None, tr, cols), lambda i, pr: (pr[0], pr[1] * nb + i, 0)),
                      pl.BlockSpec((None, tr, cols), lambda i, pr: (pr[0], i, 0)),
                      pl.BlockSpec((3, tr, cols), lambda i, pr: (0, i, 0))],
            out_specs=pl.BlockSpec((tr, cols), lambda i, pr: (pr[1] * nb + i, 0))),
        out_shape=jax.ShapeDtypeStruct((r, cols), F32),
        compiler_params=_params("arbitrary"),
    )(place, grad, got, others)


def _pack_gains(parts, d):
    def body(*refs):
        ins, o_ref = refs[:-1], refs[-1]
        o_ref[...] = jnp.zeros_like(o_ref)
        for k, r in enumerate(ins):
            o_ref[k:k + 1, 0:r.shape[1]] = jnp.sum(r[...], axis=0, keepdims=True)

    return pl.pallas_call(
        body, name="pack_gains", out_shape=jax.ShapeDtypeStruct((GAIN_ROWS, d), F32),
    )(*parts)


def _adamw_math(w, g, m, v):
    m = ADAM_B1 * m + (1.0 - ADAM_B1) * g
    v = ADAM_B2 * v + (1.0 - ADAM_B2) * jnp.square(g)
    m_hat = m / (1.0 - ADAM_B1 ** ADAM_STEP)
    v_hat = v / (1.0 - ADAM_B2 ** ADAM_STEP)
    return -ADAM_LR * (m_hat / (jnp.sqrt(v_hat) + ADAM_EPS) + ADAM_WD * w), m, v


def _adamw(w, g, m, v, name):
    r, cols = w.shape
    tr = r // 4 if (r // 4) % 8 == 0 else r

    def body(w_ref, g_ref, m_ref, v_ref, d_ref, nm_ref, nv_ref):
        d_ref[...], nm_ref[...], nv_ref[...] = _adamw_math(w_ref[...], g_ref[...], m_ref[...], v_ref[...])

    tile = pl.BlockSpec((tr, cols), lambda i: (i, 0))
    return pl.pallas_call(
        body, name=name, grid=(r // tr,), in_specs=[tile] * 4, out_specs=[tile] * 3,
        out_shape=[jax.ShapeDtypeStruct((r, cols), F32)] * 3,
        compiler_params=_params("arbitrary"),
    )(w, g, m, v)


def _adamw_gain(gall, row, w, m, v, name):
    n = w.shape[1]

    def body(ga_ref, w_ref, m_ref, v_ref, g_ref, d_ref, nm_ref, nv_ref):
        g = ga_ref[0, row:row + 1, 0:n]
        for k in range(1, N_DEV):
            g = g + ga_ref[k, row:row + 1, 0:n]
        g_ref[...] = g
        d_ref[...], nm_ref[...], nv_ref[...] = _adamw_math(w_ref[...], g, m_ref[...], v_ref[...])

    return pl.pallas_call(
        body, name=name, out_shape=[jax.ShapeDtypeStruct((1, n), F32)] * 4,
    )(gall, w, m, v)


def kernel(x, norm_ffn1, ffn1_w_gate, ffn1_w_up, ffn1_w_down, norm_mix, w_in, ret_norm_gain, w_out, norm_ffn2, ffn2_w_gate, ffn2_w_up, ffn2_w_down, norm_final, loss_target, m_norm_ffn1, m_ffn1_w_gate, m_ffn1_w_up, m_ffn1_w_down, m_norm_mix, m_w_in, m_ret_norm_gain, m_w_out, m_norm_ffn2, m_ffn2_w_gate, m_ffn2_w_up, m_ffn2_w_down, m_norm_final, v_norm_ffn1, v_ffn1_w_gate, v_ffn1_w_up, v_ffn1_w_down, v_norm_mix, v_w_in, v_ret_norm_gain, v_w_out, v_norm_ffn2, v_ffn2_w_gate, v_ffn2_w_up, v_ffn2_w_down, v_norm_final):
    d = x.shape[-1]
    mats = [ffn1_w_gate, ffn1_w_up, ffn1_w_down, w_in, w_out, ffn2_w_gate, ffn2_w_up, ffn2_w_down]
    mats_m = [m_ffn1_w_gate, m_ffn1_w_up, m_ffn1_w_down, m_w_in, m_w_out, m_ffn2_w_gate, m_ffn2_w_up, m_ffn2_w_down]
    mats_v = [v_ffn1_w_gate, v_ffn1_w_up, v_ffn1_w_down, v_w_in, v_w_out, v_ffn2_w_gate, v_ffn2_w_up, v_ffn2_w_down]
    mat_names = ["ffn1_w_gate", "ffn1_w_up", "ffn1_w_down", "w_in", "w_out", "ffn2_w_gate", "ffn2_w_up", "ffn2_w_down"]
    gains = [norm_ffn1, norm_mix, ret_norm_gain, norm_ffn2, norm_final.reshape(1, d)]
    gains_m = [m_norm_ffn1, m_norm_mix, m_ret_norm_gain, m_norm_ffn2, m_norm_final.reshape(1, d)]
    gains_v = [v_norm_ffn1, v_norm_mix, v_ret_norm_gain, v_norm_ffn2, v_norm_final.reshape(1, d)]
    gain_names = ["norm_ffn1", "norm_mix", "ret_norm_gain", "norm_ffn2", "norm_final"]

    shards = [w[0] for w in mats]
    place = jnp.stack([2 * lax.axis_index("x") + lax.axis_index("y"), lax.axis_index("c")]).astype(jnp.int32)
    placed = [_place_shard(place, s, f"place_{n}") for s, n in zip(shards, mat_names)]
    loss_p, dx, shard_grads, gall = _step(x[0], loss_target[0], gains, placed, place)

    out_g, out_d, out_m, out_v = {}, {}, {}, {}
    for n, w, g, m, v in zip(mat_names, shards, shard_grads, mats_m, mats_v):
        dl, nm, nv = _adamw(w, g, m[0], v[0], f"adamw_{n}")
        out_g[n], out_d[n], out_m[n], out_v[n] = g[None], dl[None], nm[None], nv[None]
    for row, (n, w, m, v) in enumerate(zip(gain_names, gains, gains_m, gains_v)):
        res = _adamw_gain(gall, row, w, m, v, f"adamw_{n}")
        shape = (d,) if n == "norm_final" else w.shape
        out_g[n], out_d[n], out_m[n], out_v[n] = [r.reshape(shape) for r in res]

    loss = lax.psum(jnp.sum(loss_p), ("x", "y", "c"))
    order = ["norm_ffn1", "ffn1_w_gate", "ffn1_w_up", "ffn1_w_down", "norm_mix", "w_in", "ret_norm_gain", "w_out",
             "norm_ffn2", "ffn2_w_gate", "ffn2_w_up", "ffn2_w_down", "norm_final"]
    return (loss, dx[None], *[out_g[n] for n in order], *[out_d[n] for n in order],
            *[out_m[n] for n in order], *[out_v[n] for n in order])
```

```python
import functools
import math

import jax
import jax.numpy as jnp
from jax import lax
from jax.experimental import pallas as pl
from jax.experimental.pallas import tpu as pltpu

F32 = jnp.float32
BF16 = jnp.bfloat16
MESH = pl.DeviceIdType.MESH

NORM_EPS = 1e-6
GN_EPS = 1e-6
ROPE_BASE = 10000.0
RET_HEADS = 4
RET_DIM = 128
RET_WIDTH = 512
RET_CHUNK = 128
ATT_HEADS = 8
ATT_DIM = 64
ATT_WIDTH = 512
ATT_BLOCK = 128
DILATIONS = (1, 4, 16)
IN_COLS = 4 * RET_WIDTH + 3 * ATT_WIDTH
LANE = 128
N_SHARD = 4
ADAM_LR, ADAM_B1, ADAM_B2, ADAM_EPS, ADAM_WD, ADAM_STEP = 0.001, 0.9, 0.999, 1e-08, 0.01, 10

V7X_VMEM_BYTES = 64 * 1024 * 1024
VMEM_LIMIT = V7X_VMEM_BYTES - 8 * 1024 * 1024

NT = (((1,), (1,)), ((), ()))
TN = (((0,), (0,)), ((), ()))


def _params(*sem):
    return pltpu.CompilerParams(dimension_semantics=sem, vmem_limit_bytes=VMEM_LIMIT)


def _dot(a, b, dims=None):
    if dims is None:
        return jnp.dot(a, b, preferred_element_type=F32)
    return lax.dot_general(a, b, dims, preferred_element_type=F32)


def _sigmoid(x):
    return 1.0 / (1.0 + jnp.exp(-x))


def _load_weights(pairs, sems):
    copies = [pltpu.make_async_copy(src, dst, sems.at[k]) for k, (src, dst) in enumerate(pairs)]
    for cp in copies:
        cp.start()
    for cp in copies:
        cp.wait()


def _rows8(v):
    r, c = v.shape
    return v.reshape(r // 8, 8, c).sum(axis=0)


class _Ride:
    def __init__(self, inputs, out_shapes, sems, start, finish, aliases=None):
        self.inputs, self.out_shapes, self.sems = list(inputs), list(out_shapes), list(sems)
        self.start, self.finish, self.aliases = start, finish, dict(aliases or {})


def _pallas(body, rides, *, name, in_specs, out_specs, out_shape, args, grid=(), scratch_shapes=(), sem=()):
    rides = [r for r in (rides or []) if r is not None]
    n_in, n_out, n_scr = len(args), len(out_shape), len(scratch_shapes)
    hbm = pl.BlockSpec(memory_space=pl.ANY)
    r_in = [a for r in rides for a in r.inputs]
    r_out = [s for r in rides for s in r.out_shapes]
    r_sem = [s for r in rides for s in r.sems]
    aliases, spans, ki, ko, ks = {}, [], 0, 0, 0
    for r in rides:
        aliases.update({n_in + ki + i: n_out + ko + o for i, o in r.aliases.items()})
        spans.append((ki, ko, ks))
        ki, ko, ks = ki + len(r.inputs), ko + len(r.out_shapes), ks + len(r.sems)

    def wrapped(*refs):
        ins, rin = refs[:n_in], refs[n_in:n_in + len(r_in)]
        o0 = n_in + len(r_in)
        outs, rout = refs[o0:o0 + n_out], refs[o0 + n_out:o0 + n_out + len(r_out)]
        s0 = o0 + n_out + len(r_out)
        scr, rsem = refs[s0:s0 + n_scr], refs[s0 + n_scr:]
        part = lambda r, k: (rin[spans[k][0]:spans[k][0] + len(r.inputs)], rout[spans[k][1]:spans[k][1] + len(r.out_shapes)],
                             rsem[spans[k][2]:spans[k][2] + len(r.sems)])
        first = functools.reduce(jnp.logical_and, [pl.program_id(k) == 0 for k in range(len(grid))], True)
        last = functools.reduce(jnp.logical_and, [pl.program_id(k) == grid[k] - 1 for k in range(len(grid))], True)
        if rides:
            @pl.when(first)
            def _():
                for k, r in enumerate(rides):
                    r.start(*part(r, k))

        body(*ins, *outs, *scr)
        if rides:
            @pl.when(last)
            def _():
                for k, r in enumerate(rides):
                    r.finish(*part(r, k))

    res = pl.pallas_call(
        wrapped, name=name, grid=grid,
        in_specs=list(in_specs) + [hbm] * len(r_in), out_specs=list(out_specs) + [hbm] * len(r_out),
        out_shape=list(out_shape) + r_out, input_output_aliases=aliases,
        scratch_shapes=list(scratch_shapes) + r_sem,
        compiler_params=pltpu.CompilerParams(dimension_semantics=sem, vmem_limit_bytes=VMEM_LIMIT) if grid else None,
    )(*args, *r_in)
    extras = [list(res[n_out + ko:n_out + ko + len(r.out_shapes)]) for r, (_, ko, _) in zip(rides, spans)]
    return list(res[:n_out]), extras


def _run(ride, name):
    def body(*refs):
        n_in, n_out = len(ride.inputs), len(ride.out_shapes)
        parts = refs[:n_in], refs[n_in:n_in + n_out], refs[n_in + n_out:]
        ride.start(*parts)
        ride.finish(*parts)

    hbm = pl.BlockSpec(memory_space=pl.ANY)
    return list(pl.pallas_call(
        body, name=name, in_specs=[hbm] * len(ride.inputs), out_specs=[hbm] * len(ride.out_shapes),
        out_shape=ride.out_shapes, input_output_aliases=ride.aliases, scratch_shapes=ride.sems,
    )(*ride.inputs))


def _ffn_fwd(x, gain, wg, wu, wd, name, rides=None):
    t, d = x.shape
    ns, _, fs = wg.shape
    tm = min(256, t)

    def body(x_ref, gain_ref, wg_hbm, wu_hbm, wd_hbm, h_ref, xn_ref, g_ref, u_ref, a_ref, wg_v, wu_v, wd_v, sems):
        @pl.when(pl.program_id(0) == 0)
        def _():
            _load_weights([(wg_hbm, wg_v), (wu_hbm, wu_v), (wd_hbm, wd_v)], sems)

        xv = x_ref[...]
        r = lax.rsqrt(jnp.mean(xv * xv, axis=-1, keepdims=True) + NORM_EPS)
        xn = (xv * r * gain_ref[...]).astype(BF16)
        xn_ref[...] = xn
        acc = jnp.zeros((tm, d), F32)
        for j in range(ns):
            g = _dot(xn, wg_v[j])
            u = _dot(xn, wu_v[j])
            g_ref[j] = g.astype(BF16)
            u_ref[j] = u.astype(BF16)
            a = (g * _sigmoid(g) * u).astype(BF16)
            a_ref[j] = a
            acc = acc + _dot(a, wd_v[j])
        h_ref[...] = xv + 0.5 * acc

    hbm = pl.BlockSpec(memory_space=pl.ANY)
    hid = pl.BlockSpec((ns, tm, fs), lambda i: (0, i, 0))
    return _pallas(
        body, rides, name=name, grid=(t // tm,),
        in_specs=[pl.BlockSpec((tm, d), lambda i: (i, 0)), pl.BlockSpec((1, d), lambda i: (0, 0)), hbm, hbm, hbm],
        out_specs=[pl.BlockSpec((tm, d), lambda i: (i, 0)), pl.BlockSpec((tm, d), lambda i: (i, 0)), hid, hid, hid],
        out_shape=[jax.ShapeDtypeStruct((t, d), F32), jax.ShapeDtypeStruct((t, d), BF16)]
        + [jax.ShapeDtypeStruct((ns, t, fs), BF16)] * 3,
        scratch_shapes=[pltpu.VMEM(wg.shape, BF16), pltpu.VMEM(wu.shape, BF16), pltpu.VMEM(wd.shape, BF16),
                        pltpu.SemaphoreType.DMA((3,))],
        sem=("arbitrary",), args=[x, gain, wg, wu, wd])


def _ffn_bwd_data(dy, x, gain, g, u, wg, wu, wd, name, rides=None):
    t, d = x.shape
    ns, _, fs = wg.shape
    tm = min(256, t)

    def body(dy_ref, x_ref, gain_ref, g_ref, u_ref, wg_hbm, wu_hbm, wd_hbm, dx_ref, dg_ref, du_ref, dgain_ref,
             wg_v, wu_v, wd_v, sems):
        @pl.when(pl.program_id(0) == 0)
        def _():
            _load_weights([(wg_hbm, wg_v), (wu_hbm, wu_v), (wd_hbm, wd_v)], sems)
            dgain_ref[...] = jnp.zeros_like(dgain_ref)

        dyv = dy_ref[...]
        dyh = (0.5 * dyv).astype(BF16)
        dxn = jnp.zeros((tm, d), F32)
        for j in range(ns):
            da = _dot(dyh, wd_v[j], NT)
            gj = g_ref[j].astype(F32)
            uj = u_ref[j].astype(F32)
            sig = _sigmoid(gj)
            dgj = (da * uj * (sig * (1.0 + gj * (1.0 - sig)))).astype(BF16)
            duj = (da * (gj * sig)).astype(BF16)
            dg_ref[j] = dgj
            du_ref[j] = duj
            dxn = dxn + _dot(dgj, wg_v[j], NT) + _dot(duj, wu_v[j], NT)
        xv = x_ref[...]
        r = lax.rsqrt(jnp.mean(xv * xv, axis=-1, keepdims=True) + NORM_EPS)
        xh = xv * r
        dgain_ref[...] += _rows8(dxn * xh)
        dxh = dxn * gain_ref[...]
        dx_ref[...] = dyv + r * (dxh - xh * jnp.mean(dxh * xh, axis=-1, keepdims=True))

    hbm = pl.BlockSpec(memory_space=pl.ANY)
    tile = pl.BlockSpec((tm, d), lambda i: (i, 0))
    hid = pl.BlockSpec((ns, tm, fs), lambda i: (0, i, 0))
    return _pallas(
        body, rides, name=name, grid=(t // tm,),
        in_specs=[tile, tile, pl.BlockSpec((1, d), lambda i: (0, 0)), hid, hid, hbm, hbm, hbm],
        out_specs=[tile, hid, hid, pl.BlockSpec((8, d), lambda i: (0, 0))],
        out_shape=[jax.ShapeDtypeStruct((t, d), F32), jax.ShapeDtypeStruct((ns, t, fs), BF16),
                   jax.ShapeDtypeStruct((ns, t, fs), BF16), jax.ShapeDtypeStruct((8, d), F32)],
        scratch_shapes=[pltpu.VMEM(wg.shape, BF16), pltpu.VMEM(wu.shape, BF16), pltpu.VMEM(wd.shape, BF16),
                        pltpu.SemaphoreType.DMA((3,))],
        sem=("arbitrary",), args=[dy, x, gain, g, u, wg, wu, wd])


def _ffn_wgrad_down(a, dy, name, rides=None):
    t, d = dy.shape
    ns, _, fs = a.shape
    tk = min(1024, t)

    def body(dy_ref, a_ref, dwd_ref):
        @pl.when(pl.program_id(1) == 0)
        def _():
            dwd_ref[...] = jnp.zeros_like(dwd_ref)

        dwd_ref[...] += _dot(a_ref[...], (0.5 * dy_ref[...]).astype(BF16), TN)

    return _pallas(
        body, rides, name=name, grid=(ns, t // tk),
        in_specs=[pl.BlockSpec((tk, d), lambda j, k: (k, 0)), pl.BlockSpec((None, tk, fs), lambda j, k: (j, k, 0))],
        out_specs=[pl.BlockSpec((None, fs, d), lambda j, k: (j, 0, 0))],
        out_shape=[jax.ShapeDtypeStruct((ns, fs, d), F32)],
        sem=("arbitrary", "arbitrary"), args=[dy, a])


def _ffn_wgrad_gu(xn, dg, du, name, rides=None):
    t, d = xn.shape
    ns, _, fs = dg.shape
    tk = min(2048, t)

    def body(xn_ref, dg_ref, du_ref, dwg_ref, dwu_ref):
        @pl.when(pl.program_id(1) == 0)
        def _():
            dwg_ref[...] = jnp.zeros_like(dwg_ref)
            dwu_ref[...] = jnp.zeros_like(dwu_ref)

        xnv = xn_ref[...]
        dwg_ref[...] += _dot(xnv, dg_ref[...], TN)
        dwu_ref[...] += _dot(xnv, du_ref[...], TN)

    hid = pl.BlockSpec((None, tk, fs), lambda j, k: (j, k, 0))
    out = pl.BlockSpec((None, d, fs), lambda j, k: (j, 0, 0))
    return _pallas(
        body, rides, name=name, grid=(ns, t // tk),
        in_specs=[pl.BlockSpec((tk, d), lambda j, k: (k, 0)), hid, hid],
        out_specs=[out, out], out_shape=[jax.ShapeDtypeStruct((ns, d, fs), F32)] * 2,
        sem=("arbitrary", "arbitrary"), args=[xn, dg, du])


def _tn_matmul(a, b, bn, name):
    t, m = a.shape
    n = b.shape[1]
    tk = min(2048, t)

    def body(a_ref, b_ref, o_ref):
        @pl.when(pl.program_id(1) == 0)
        def _():
            o_ref[...] = jnp.zeros_like(o_ref)

        o_ref[...] += _dot(a_ref[...].astype(BF16), b_ref[...].astype(BF16), TN)

    return pl.pallas_call(
        body, name=name, grid=(n // bn, t // tk),
        in_specs=[pl.BlockSpec((tk, m), lambda j, k: (k, 0)), pl.BlockSpec((tk, bn), lambda j, k: (k, j))],
        out_specs=pl.BlockSpec((None, m, bn), lambda j, k: (j, 0, 0)),
        out_shape=jax.ShapeDtypeStruct((n // bn, m, bn), F32),
        compiler_params=_params("arbitrary", "arbitrary"),
    )(a, b)


def _chunk_scratch(tm, w):
    return pltpu.VMEM((w // LANE, tm, LANE), F32)


def _regroup_store(cbuf, out_ref, dil):
    n = out_ref.shape[1]
    for g in range(dil):
        for k in range(cbuf.shape[0]):
            rows = cbuf[k] if dil == 1 else cbuf[k, pl.ds(g, n, stride=dil), :]
            out_ref[g, :, k * LANE:(k + 1) * LANE] = rows.astype(out_ref.dtype)


def _natural_rows(ref, dil, cbuf):
    if dil == 1:
        return ref[0]
    n = ref.shape[1]
    for g in range(dil):
        for k in range(cbuf.shape[0]):
            cbuf[k, pl.ds(g, n, stride=dil), :] = ref[g, :, k * LANE:(k + 1) * LANE]
    return jnp.concatenate([cbuf[k] for k in range(cbuf.shape[0])], axis=1)


def _inproj_fwd(h, gain, win):
    t, d = h.shape
    ns, _, cs = win.shape
    tm = min(512, t)
    rw, aw = 4 * RET_WIDTH, 3 * ATT_WIDTH

    def body(h_ref, gain_ref, w_ref, xn_ref, ur_ref, *rest):
        a_refs, abuf = rest[:-1], rest[-1]
        hv = h_ref[...]
        r = lax.rsqrt(jnp.mean(hv * hv, axis=-1, keepdims=True) + NORM_EPS)
        xn = (hv * r * gain_ref[...]).astype(BF16)
        xn_ref[...] = xn
        for j in range(ns):
            res = _dot(xn, w_ref[j])
            for k in range(cs // LANE):
                chunk = j * (cs // LANE) + k
                piece = res[:, k * LANE:(k + 1) * LANE]
                if chunk < rw // LANE:
                    ur_ref[:, chunk * LANE:(chunk + 1) * LANE] = piece
                else:
                    abuf[chunk - rw // LANE] = piece
        for dil, a_ref in zip(DILATIONS, a_refs):
            _regroup_store(abuf, a_ref, dil)

    return pl.pallas_call(
        body, name="inproj_fwd", grid=(t // tm,),
        in_specs=[pl.BlockSpec((tm, d), lambda i: (i, 0)), pl.BlockSpec((1, d), lambda i: (0, 0)),
                  pl.BlockSpec(win.shape, lambda i: (0, 0, 0))],
        out_specs=[pl.BlockSpec((tm, d), lambda i: (i, 0)), pl.BlockSpec((tm, rw), lambda i: (i, 0))]
        + [pl.BlockSpec((dil, tm // dil, aw), lambda i: (0, i, 0)) for dil in DILATIONS],
        out_shape=[jax.ShapeDtypeStruct((t, d), BF16), jax.ShapeDtypeStruct((t, rw), F32)]
        + [jax.ShapeDtypeStruct((dil, t // dil, aw), BF16) for dil in DILATIONS],
        scratch_shapes=[_chunk_scratch(tm, aw)],
        compiler_params=_params("arbitrary"),
    )(h, gain, win)


def _inproj_bwd(pieces, h, gain, dres, win):
    t, d = h.shape
    ns, _, cs = win.shape
    pw = pieces[0].shape[1]
    tm = min(512, t)
    npc = len(pieces)

    def body(*refs):
        p_refs = refs[:npc]
        h_ref, gain_ref, dres_ref, w_ref, dh_ref, du_ref, dgain_ref = refs[npc:]

        @pl.when(pl.program_id(0) == 0)
        def _():
            dgain_ref[...] = jnp.zeros_like(dgain_ref)

        for k in range(npc):
            du_ref[:, k * pw:(k + 1) * pw] = p_refs[k][...]
        dxn = jnp.zeros((tm, d), F32)
        for j in range(ns):
            dxn = dxn + _dot(du_ref[:, j * cs:(j + 1) * cs], w_ref[j], NT)
        hv = h_ref[...]
        r = lax.rsqrt(jnp.mean(hv * hv, axis=-1, keepdims=True) + NORM_EPS)
        xh = hv * r
        dgain_ref[...] += _rows8(dxn * xh)
        dxh = dxn * gain_ref[...]
        dh_ref[...] = dres_ref[...] + r * (dxh - xh * jnp.mean(dxh * xh, axis=-1, keepdims=True))

    tile = pl.BlockSpec((tm, d), lambda i: (i, 0))
    return pl.pallas_call(
        body, name="inproj_bwd", grid=(t // tm,),
        in_specs=[pl.BlockSpec((tm, pw), lambda i: (i, 0))] * npc + [
            tile, pl.BlockSpec((1, d), lambda i: (0, 0)), tile, pl.BlockSpec(win.shape, lambda i: (0, 0, 0))],
        out_specs=[tile, pl.BlockSpec((tm, npc * pw), lambda i: (i, 0)), pl.BlockSpec((8, d), lambda i: (0, 0))],
        out_shape=[jax.ShapeDtypeStruct((t, d), F32), jax.ShapeDtypeStruct((t, npc * pw), BF16),
                   jax.ShapeDtypeStruct((8, d), F32)],
        compiler_params=_params("arbitrary"),
    )(*pieces, h, gain, dres, win)


def _outproj_fwd(h, mix_r, mix_a, wo):
    t, d = h.shape
    hw = mix_r.shape[1]
    tm = min(512, t)

    def body(h_ref, mr_ref, ma_ref, w_ref, o_ref):
        o_ref[...] = h_ref[...] + _dot(mr_ref[...], w_ref[0:hw, :]) + _dot(ma_ref[...], w_ref[hw:2 * hw, :])

    tile = pl.BlockSpec((tm, d), lambda i: (i, 0))
    half = pl.BlockSpec((tm, hw), lambda i: (i, 0))
    return pl.pallas_call(
        body, name="outproj_fwd", grid=(t // tm,),
        in_specs=[tile, half, half, pl.BlockSpec(wo.shape, lambda i: (0, 0))],
        out_specs=tile, out_shape=jax.ShapeDtypeStruct((t, d), F32),
        compiler_params=_params("arbitrary"),
    )(h, mix_r, mix_a, wo)


def _outproj_bwd(dh, wo, rides=None):
    t, d = dh.shape
    hw = wo.shape[0] // 2
    tm = min(512, t)

    def body(dh_ref, w_ref, dr_ref, da_ref):
        dhb = dh_ref[...].astype(BF16)
        dr_ref[...] = _dot(dhb, w_ref[0:hw, :], NT)
        da_ref[...] = _dot(dhb, w_ref[hw:2 * hw, :], NT)

    half = pl.BlockSpec((tm, hw), lambda i: (i, 0))
    return _pallas(
        body, rides, name="outproj_bwd", grid=(t // tm,),
        in_specs=[pl.BlockSpec((tm, d), lambda i: (i, 0)), pl.BlockSpec(wo.shape, lambda i: (0, 0))],
        out_specs=[half, half],
        out_shape=[jax.ShapeDtypeStruct((t, hw), F32), jax.ShapeDtypeStruct((t, hw), F32)],
        sem=("arbitrary",), args=[dh, wo])


def _retention_tables(t):
    pos = jnp.arange(t, dtype=F32)
    inv_freq = ROPE_BASE ** (-jnp.arange(0, RET_DIM, 2, dtype=F32) / RET_DIM)
    ang = jnp.repeat(pos[:, None] * inv_freq[None, :], 2, axis=-1)
    c = RET_CHUNK
    log_g = jnp.log(1.0 - 2.0 ** (-5.0 - jnp.arange(RET_HEADS, dtype=F32)))
    idx = jnp.arange(c, dtype=F32)
    rel = idx[:, None] - idx[None, :]
    decay = jnp.where(rel >= 0, jnp.exp(log_g[:, None, None] * jnp.maximum(rel, 0.0)), 0.0)
    zeta = jnp.exp(log_g[:, None] * (c - 1 - idx)[None, :])
    xi = jnp.exp(log_g[:, None] * (idx + 1)[None, :])
    gc = jnp.exp(log_g * c)
    wide = lambda v: jnp.broadcast_to(v[:, :, None], (RET_HEADS, c, LANE))
    return (jnp.cos(ang), jnp.sin(ang), decay, wide(zeta), wide(xi),
            jnp.broadcast_to(gc[:, None, None], (RET_HEADS, c, LANE)))


def _rot(v):
    lane = lax.broadcasted_iota(jnp.int32, v.shape, 1)
    nxt = pltpu.roll(v, LANE - 1, 1)
    prv = pltpu.roll(v, 1, 1)
    return jnp.where(lane % 2 == 0, -nxt, prv)


def _ret_specs(tr, rev, nt):
    ti = (lambda i: nt - 1 - i) if rev else (lambda i: i)
    col = lambda blk: pl.BlockSpec((tr, RET_WIDTH), lambda i: (ti(i), blk))
    tab = pl.BlockSpec((tr, LANE), lambda i: (ti(i), 0))
    head = pl.BlockSpec((RET_HEADS, RET_CHUNK, LANE), lambda i: (0, 0, 0))
    return col, tab, head


def _ret_chunks(tr, rev=False):
    order = list(range(tr // RET_CHUNK))
    return [(pl.ds(ci * RET_CHUNK, RET_CHUNK), slice(h * RET_DIM, (h + 1) * RET_DIM), h)
            for h in range(RET_HEADS) for ci in (reversed(order) if rev else order)]


def _ret_operands(items, q_ref, k_ref, v_ref, cos_ref, sin_ref, zeta_ref):
    scale = RET_DIM ** -0.5
    qbs, kbs, vbs, kzs = [], [], [], []
    for sl, hs, h in items:
        cs, sn = cos_ref[sl, :], sin_ref[sl, :]
        q, k = q_ref[sl, hs], k_ref[sl, hs]
        kr = (k * cs + _rot(k) * sn) * scale
        qbs.append((q * cs + _rot(q) * sn).astype(BF16))
        kbs.append(kr.astype(BF16))
        vbs.append(v_ref[sl, hs].astype(BF16))
        kzs.append((kr * zeta_ref[h]).astype(BF16))
    return qbs, kbs, vbs, kzs


def _ret_states(items, state, steps, gc_ref):
    cur, befores = {}, []
    for (sl, hs, h), step in zip(items, steps):
        st = cur[h] if h in cur else state[h]
        befores.append(st)
        cur[h] = st * gc_ref[h] + step
    for h, st in cur.items():
        state[h] = st
    return befores


def _ret_fwd(u, gain, tabs):
    t = u.shape[0]
    tr = min(512, t)
    nt = t // tr
    cos, sin, decay, zeta, xi, gc = tabs
    scale = RET_DIM ** -0.5

    def body(q_ref, k_ref, v_ref, gt_ref, cos_ref, sin_ref, gain_ref, dec_ref, zeta_ref, xi_ref, gc_ref,
             raw_ref, mix_ref, state):
        @pl.when(pl.program_id(0) == 0)
        def _():
            state[...] = jnp.zeros_like(state)

        items = _ret_chunks(tr)
        n = range(len(items))
        qbs, kbs, vbs, kzs = _ret_operands(items, q_ref, k_ref, v_ref, cos_ref, sin_ref, zeta_ref)
        ss = [_dot(qbs[i], kbs[i], NT) for i in n]
        kvs = [_dot(kzs[i], vbs[i], TN) for i in n]
        befores = _ret_states(items, state, kvs, gc_ref)
        intra = [_dot((ss[i] * dec_ref[items[i][2]]).astype(BF16), vbs[i]) for i in n]
        inter = [_dot(qbs[i], befores[i].astype(BF16)) for i in n]
        for i, (sl, hs, h) in enumerate(items):
            o = intra[i] + inter[i] * xi_ref[h]
            raw_ref[sl, hs] = o
            mu = jnp.mean(o, axis=-1, keepdims=True)
            var = jnp.mean(jnp.square(o - mu), axis=-1, keepdims=True)
            y = (o - mu) * lax.rsqrt(var + GN_EPS) * gain_ref[:, hs]
            gt = gt_ref[sl, hs]
            mix_ref[sl, hs] = (y * (gt * _sigmoid(gt))).astype(BF16)

    col, tab, head = _ret_specs(tr, False, nt)
    out = pl.BlockSpec((tr, RET_WIDTH), lambda i: (i, 0))
    return pl.pallas_call(
        body, name="ret_fwd", grid=(nt,),
        in_specs=[col(0), col(1), col(2), col(3), tab, tab, pl.BlockSpec((1, RET_WIDTH), lambda i: (0, 0)),
                  head, head, head, head],
        out_specs=[out, out],
        out_shape=[jax.ShapeDtypeStruct((t, RET_WIDTH), F32), jax.ShapeDtypeStruct((t, RET_WIDTH), BF16)],
        scratch_shapes=[pltpu.VMEM((RET_HEADS, RET_DIM, RET_DIM), F32)],
        compiler_params=_params("arbitrary"),
    )(u, u, u, u, cos, sin, gain, decay, zeta, xi, gc)


def _ret_bwd_q(dmix, raw, u, gain, tabs, rides=None):
    t = u.shape[0]
    tr = min(512, t)
    nt = t // tr
    cos, sin, decay, zeta, xi, gc = tabs
    scale = RET_DIM ** -0.5

    def body(dm_ref, raw_ref, q_ref, k_ref, v_ref, gt_ref, cos_ref, sin_ref, gain_ref, dec_ref, zeta_ref, xi_ref, gc_ref,
             dq_ref, dgt_ref, dret_ref, dgain_ref, state):
        @pl.when(pl.program_id(0) == 0)
        def _():
            state[...] = jnp.zeros_like(state)
            dgain_ref[...] = jnp.zeros_like(dgain_ref)

        items = _ret_chunks(tr)
        n_items = range(len(items))
        qbs, kbs, vbs, kzs = _ret_operands(items, q_ref, k_ref, v_ref, cos_ref, sin_ref, zeta_ref)
        dos, dgains = [], {}
        for sl, hs, h in items:
            o = raw_ref[sl, hs]
            mu = jnp.mean(o, axis=-1, keepdims=True)
            var = jnp.mean(jnp.square(o - mu), axis=-1, keepdims=True)
            rs = lax.rsqrt(var + GN_EPS)
            n = (o - mu) * rs
            gt = gt_ref[sl, hs]
            sig = _sigmoid(gt)
            dout = dm_ref[sl, hs]
            gain_h = gain_ref[:, hs]
            dgt_ref[sl, hs] = (dout * (n * gain_h) * (sig * (1.0 + gt * (1.0 - sig)))).astype(BF16)
            dy = dout * (gt * sig)
            dgains[h] = dgains[h] + _rows8(dy * n) if h in dgains else _rows8(dy * n)
            dn = dy * gain_h
            do = rs * (dn - jnp.mean(dn, axis=-1, keepdims=True) - n * jnp.mean(dn * n, axis=-1, keepdims=True))
            dret_ref[sl, hs] = do
            dos.append(do)
        for h, dg in dgains.items():
            dgain_ref[:, h * RET_DIM:(h + 1) * RET_DIM] += dg
        dss = [_dot(dos[i].astype(BF16), vbs[i], NT) for i in n_items]
        kvs = [_dot(kzs[i], vbs[i], TN) for i in n_items]
        befores = _ret_states(items, state, kvs, gc_ref)
        intra = [_dot((dss[i] * dec_ref[items[i][2]]).astype(BF16), kbs[i]) for i in n_items]
        inter = [_dot((dos[i] * xi_ref[items[i][2]]).astype(BF16), befores[i].astype(BF16), NT) for i in n_items]
        for i, (sl, hs, h) in enumerate(items):
            dqr = intra[i] + inter[i]
            dq_ref[sl, hs] = (dqr * cos_ref[sl, :] - _rot(dqr * sin_ref[sl, :])).astype(BF16)

    col, tab, head = _ret_specs(tr, False, nt)
    out = pl.BlockSpec((tr, RET_WIDTH), lambda i: (i, 0))
    return _pallas(
        body, rides, name="ret_bwd_q", grid=(nt,),
        in_specs=[out, out, col(0), col(1), col(2), col(3), tab, tab, pl.BlockSpec((1, RET_WIDTH), lambda i: (0, 0)),
                  head, head, head, head],
        out_specs=[out, out, out, pl.BlockSpec((8, RET_WIDTH), lambda i: (0, 0))],
        out_shape=[jax.ShapeDtypeStruct((t, RET_WIDTH), BF16), jax.ShapeDtypeStruct((t, RET_WIDTH), BF16),
                   jax.ShapeDtypeStruct((t, RET_WIDTH), F32), jax.ShapeDtypeStruct((8, RET_WIDTH), F32)],
        scratch_shapes=[pltpu.VMEM((RET_HEADS, RET_DIM, RET_DIM), F32)],
        sem=("arbitrary",), args=[dmix, raw, u, u, u, u, cos, sin, gain, decay, zeta, xi, gc])


def _ret_bwd_kv(dret, u, tabs, rides=None):
    t = u.shape[0]
    tr = min(512, t)
    nt = t // tr
    cos, sin, decay, zeta, xi, gc = tabs
    scale = RET_DIM ** -0.5

    def body(do_ref, q_ref, k_ref, v_ref, cos_ref, sin_ref, dec_ref, zeta_ref, xi_ref, gc_ref, dk_ref, dv_ref, gst):
        @pl.when(pl.program_id(0) == 0)
        def _():
            gst[...] = jnp.zeros_like(gst)

        items = _ret_chunks(tr, rev=True)
        n = range(len(items))
        qbs, kbs, vbs, kzs = _ret_operands(items, q_ref, k_ref, v_ref, cos_ref, sin_ref, zeta_ref)
        dos = [do_ref[sl, hs] for sl, hs, h in items]
        dobs = [do.astype(BF16) for do in dos]
        ss = [_dot(qbs[i], kbs[i], NT) for i in n]
        dss = [_dot(dobs[i], vbs[i], NT) for i in n]
        steps = [_dot(qbs[i], (dos[i] * xi_ref[items[i][2]]).astype(BF16), TN) for i in n]
        afters = [g.astype(BF16) for g in _ret_states(items, gst, steps, gc_ref)]
        dvs = [_dot((ss[i] * dec_ref[items[i][2]]).astype(BF16), dobs[i], TN) + _dot(kzs[i], afters[i]) for i in n]
        dks = [_dot((dss[i] * dec_ref[items[i][2]]).astype(BF16), qbs[i], TN) for i in n]
        dkz = [_dot(vbs[i], afters[i], NT) for i in n]
        for i, (sl, hs, h) in enumerate(items):
            dv_ref[sl, hs] = dvs[i].astype(BF16)
            dkr = (dks[i] + dkz[i] * zeta_ref[h]) * scale
            dk_ref[sl, hs] = (dkr * cos_ref[sl, :] - _rot(dkr * sin_ref[sl, :])).astype(BF16)

    col, tab, head = _ret_specs(tr, True, nt)
    out = pl.BlockSpec((tr, RET_WIDTH), lambda i: (nt - 1 - i, 0))
    return _pallas(
        body, rides, name="ret_bwd_kv", grid=(nt,),
        in_specs=[out, col(0), col(1), col(2), tab, tab, head, head, head, head],
        out_specs=[out, out],
        out_shape=[jax.ShapeDtypeStruct((t, RET_WIDTH), BF16), jax.ShapeDtypeStruct((t, RET_WIDTH), BF16)],
        scratch_shapes=[pltpu.VMEM((RET_HEADS, RET_DIM, RET_DIM), F32)],
        sem=("arbitrary",), args=[dret, u, u, u, cos, sin, decay, zeta, xi, gc])


PAIRS = ATT_WIDTH // LANE
ATT_Q_BLK, ATT_K_BLK, ATT_V_BLK = 0, PAIRS, 2 * PAIRS
STAT_LANES = ATT_DIM // 2


def _att_tiles(t, dil):
    sub = t // dil
    tq = min(512, sub)
    return sub, tq, sub // tq, tq // ATT_BLOCK


def _att_in_specs(tq, qb, ti):
    cur = lambda off: pl.BlockSpec((None, tq, LANE), lambda g, p, i: (g, ti(i), off + p))
    prev = lambda off: pl.BlockSpec((None, ATT_BLOCK, LANE), lambda g, p, i: (g, jnp.maximum(ti(i) * qb - 1, 0), off + p))
    return [cur(ATT_Q_BLK), cur(ATT_K_BLK), prev(ATT_K_BLK), cur(ATT_V_BLK), prev(ATT_V_BLK)]


def _band_mask():
    key = lax.broadcasted_iota(jnp.int32, (2 * ATT_BLOCK, 2 * ATT_BLOCK), 0)
    qry = lax.broadcasted_iota(jnp.int32, (2 * ATT_BLOCK, 2 * ATT_BLOCK), 1) % ATT_BLOCK
    dist = qry + ATT_BLOCK - key
    return (dist >= 0) & (dist <= ATT_BLOCK), key >= ATT_BLOCK


def _head0_lanes():
    return lax.broadcasted_iota(jnp.int32, (ATT_BLOCK, LANE), 1) < ATT_DIM


def _stack_heads(v, head0):
    zero = jnp.zeros((), v.dtype)
    return jnp.concatenate([jnp.where(head0, v, zero), jnp.where(head0, zero, v)], axis=0)


def _unstack_heads(v, head0):
    return jnp.where(head0, v[0:ATT_BLOCK], v[ATT_BLOCK:])


def _att_fwd(ua, dil):
    sub = ua.shape[1]
    _, tq, nq, qb = _att_tiles(sub * dil, dil)

    def body(q_ref, kc_ref, kp_ref, vc_ref, vp_ref, o_ref, l_ref, kx, vx):
        tile = pl.program_id(2)
        kx[0:ATT_BLOCK, :] = kp_ref[...]
        kx[ATT_BLOCK:, :] = kc_ref[...]
        vx[0:ATT_BLOCK, :] = vp_ref[...]
        vx[ATT_BLOCK:, :] = vc_ref[...]
        band, cur_keys = _band_mask()
        head0 = _head0_lanes()
        blocks = range(qb)
        rows = [slice(b * ATT_BLOCK, (b + 1) * ATT_BLOCK) for b in blocks]
        keys = [slice(b * ATT_BLOCK, (b + 2) * ATT_BLOCK) for b in blocks]
        sts = [_dot(kx[keys[b], :], _stack_heads(q_ref[rows[b], :] * jnp.asarray(ATT_DIM ** -0.5, BF16), head0), NT)
               for b in blocks]
        pts, lses = [], []
        for b in blocks:
            mask = band if b > 0 else band & (cur_keys | (tile > 0))
            st = jnp.where(mask, sts[b], -1e30)
            m = jnp.max(st, axis=0, keepdims=True)
            ex = jnp.exp(st - m)
            den = jnp.sum(ex, axis=0, keepdims=True)
            pts.append((ex * (1.0 / den)).astype(BF16))
            lses.append(m + jnp.log(den))
        outs = [_dot(pts[b], vx[keys[b], :], TN) for b in blocks]
        for b in blocks:
            o_ref[rows[b], :] = _unstack_heads(outs[b], head0)
            cols = [jnp.broadcast_to(lses[b][:, e * ATT_BLOCK:(e + 1) * ATT_BLOCK], (ATT_BLOCK, LANE)).T for e in range(2)]
            l_ref[rows[b], :] = jnp.where(head0, cols[0], cols[1])

    out = pl.BlockSpec((None, tq, LANE), lambda g, p, i: (g, i, p))
    return pl.pallas_call(
        body, name=f"att_fwd_d{dil}", grid=(dil, PAIRS, nq),
        in_specs=_att_in_specs(tq, qb, lambda i: i),
        out_specs=[out, out],
        out_shape=[jax.ShapeDtypeStruct((dil, sub, ATT_WIDTH), F32)] * 2,
        scratch_shapes=[pltpu.VMEM((tq + ATT_BLOCK, LANE), BF16)] * 2,
        compiler_params=_params("arbitrary", "arbitrary", "arbitrary"),
    )(ua, ua, ua, ua, ua)


def _regrouped_spec(tm, dil, w):
    return pl.BlockSpec((dil, tm // dil, w), lambda i: (0, i, 0))


def _att_combine(outs, lses, t):
    w = ATT_WIDTH
    tm = min(512, t)
    nb = len(outs)

    def body(*refs):
        o_refs, l_refs = refs[:nb], refs[nb:2 * nb]
        mix_ref, att_ref, lse_ref, buf = refs[2 * nb:]
        ls = [_natural_rows(r, dil, buf) for r, dil in zip(l_refs, DILATIONS)]
        m = functools.reduce(jnp.maximum, ls)
        ws = [jnp.exp(l - m) for l in ls]
        den = functools.reduce(jnp.add, ws)
        att = functools.reduce(jnp.add, [(wt / den) * _natural_rows(r, dil, buf) for wt, r, dil in zip(ws, o_refs, DILATIONS)])
        att_ref[...] = att
        mix_ref[...] = att.astype(BF16)
        lse_ref[...] = m + jnp.log(den)

    tile = pl.BlockSpec((tm, w), lambda i: (i, 0))
    regrouped = [_regrouped_spec(tm, dil, w) for dil in DILATIONS]
    return pl.pallas_call(
        body, name="att_combine", grid=(t // tm,),
        in_specs=regrouped * 2, out_specs=[tile, tile, tile],
        out_shape=[jax.ShapeDtypeStruct((t, w), BF16), jax.ShapeDtypeStruct((t, w), F32), jax.ShapeDtypeStruct((t, w), F32)],
        scratch_shapes=[_chunk_scratch(tm, w)],
        compiler_params=_params("arbitrary"),
    )(*outs, *lses)


def _att_bwd_prep(datt, att, lse):
    t, w = datt.shape
    tm = min(512, t)

    def body(da_ref, at_ref, l_ref, *rest):
        outs, dbuf, sbuf = rest[:-2], rest[-2], rest[-1]
        dav = da_ref[...]
        prod = dav * at_ref[...]
        lane = lax.broadcasted_iota(jnp.int32, (tm, LANE), 1)
        for k in range(w // LANE):
            cols = slice(k * LANE, (k + 1) * LANE)
            dbuf[k] = dav[:, cols]
            delta = jnp.concatenate(
                [jnp.broadcast_to(jnp.sum(prod[:, k * LANE + e * ATT_DIM:k * LANE + (e + 1) * ATT_DIM], axis=-1, keepdims=True),
                                  (tm, ATT_DIM)) for e in range(LANE // ATT_DIM)], axis=1)
            sbuf[k] = jnp.where(lane % ATT_DIM < STAT_LANES, l_ref[:, cols], delta)
        for k, dil in enumerate(DILATIONS):
            _regroup_store(dbuf, outs[2 * k], dil)
            _regroup_store(sbuf, outs[2 * k + 1], dil)

    tile = pl.BlockSpec((tm, w), lambda i: (i, 0))
    res = pl.pallas_call(
        body, name="att_bwd_prep", grid=(t // tm,),
        in_specs=[tile] * 3,
        out_specs=[_regrouped_spec(tm, dil, w) for dil in DILATIONS for _ in range(2)],
        out_shape=[jax.ShapeDtypeStruct((dil, t // dil, w), dt) for dil in DILATIONS for dt in (BF16, F32)],
        scratch_shapes=[_chunk_scratch(tm, w)] * 2,
        compiler_params=_params("arbitrary"),
    )(datt, att, lse)
    return [(res[2 * k], res[2 * k + 1]) for k in range(len(DILATIONS))]


def _att_bwd(ua, da, stat, dil, rides=None):
    sub = ua.shape[1]
    _, tq, nq, qb = _att_tiles(sub * dil, dil)
    scale = ATT_DIM ** -0.5

    def body(q_ref, kc_ref, kp_ref, vc_ref, vp_ref, da_ref, st_ref, dq_ref, dk_ref, dv_ref, kx, vx, ck, cv):
        step = pl.program_id(2)
        tile = nq - 1 - step

        @pl.when(step == 0)
        def _():
            ck[...] = jnp.zeros_like(ck)
            cv[...] = jnp.zeros_like(cv)

        kx[0:ATT_BLOCK, :] = kp_ref[...]
        kx[ATT_BLOCK:, :] = kc_ref[...]
        vx[0:ATT_BLOCK, :] = vp_ref[...]
        vx[ATT_BLOCK:, :] = vc_ref[...]
        band, cur_keys = _band_mask()
        head0 = _head0_lanes()
        blocks = range(qb)
        rows = [slice(b * ATT_BLOCK, (b + 1) * ATT_BLOCK) for b in blocks]
        keys = [slice(b * ATT_BLOCK, (b + 2) * ATT_BLOCK) for b in blocks]
        qqs = [_stack_heads(q_ref[rows[b], :] * jnp.asarray(scale, BF16), head0) for b in blocks]
        dds = [_stack_heads(da_ref[rows[b], :], head0) for b in blocks]
        sts = [_dot(kx[keys[b], :], qqs[b], NT) for b in blocks]
        dpts = [_dot(vx[keys[b], :], dds[b], NT) for b in blocks]
        pts, dsts = [], []
        for b in blocks:
            mask = band if b > 0 else band & (cur_keys | (tile > 0))
            stat = st_ref[rows[b], :].T
            row = lambda k: jnp.concatenate([stat[e * ATT_DIM + k:e * ATT_DIM + k + 1, :] for e in range(2)], axis=1)
            pt = jnp.where(mask, jnp.exp(sts[b] - row(0)), 0.0)
            dsts.append((pt * (dpts[b] - row(STAT_LANES))).astype(BF16))
            pts.append(pt.astype(BF16))
        dqs = [_dot(dsts[b], kx[keys[b], :], TN) for b in blocks]
        dkbs = [_dot(dsts[b], qqs[b]) for b in blocks]
        dvbs = [_dot(pts[b], dds[b]) for b in blocks]
        for b in blocks:
            dq_ref[rows[b], :] = _unstack_heads(dqs[b], head0) * scale
        for b in blocks[1:]:
            dk_ref[rows[b - 1], :] = dkbs[b - 1][ATT_BLOCK:] + dkbs[b][0:ATT_BLOCK]
            dv_ref[rows[b - 1], :] = dvbs[b - 1][ATT_BLOCK:] + dvbs[b][0:ATT_BLOCK]
        before_k, before_v = dkbs[0][0:ATT_BLOCK], dvbs[0][0:ATT_BLOCK]
        open_k, open_v = dkbs[-1][ATT_BLOCK:], dvbs[-1][ATT_BLOCK:]
        last = slice(tq - ATT_BLOCK, tq)
        dk_ref[last, :] = open_k + ck[...]
        dv_ref[last, :] = open_v + cv[...]
        ck[...] = before_k
        cv[...] = before_v

    ti = lambda i: nq - 1 - i
    out = pl.BlockSpec((None, tq, LANE), lambda g, p, i: (g, ti(i), p))
    shape = jax.ShapeDtypeStruct((dil, sub, ATT_WIDTH), F32)
    return _pallas(
        body, rides, name=f"att_bwd_d{dil}", grid=(dil, PAIRS, nq),
        in_specs=_att_in_specs(tq, qb, ti) + [out, out],
        out_specs=[out, out, out], out_shape=[shape] * 3,
        scratch_shapes=[pltpu.VMEM((tq + ATT_BLOCK, LANE), BF16)] * 2 + [pltpu.VMEM((ATT_BLOCK, LANE), F32)] * 2,
        sem=("arbitrary", "arbitrary", "arbitrary"), args=[ua, ua, ua, ua, ua, da, stat])


def _att_bwd_sum(parts, t):
    w = ATT_WIDTH
    tm = min(512, t)
    nk = len(parts[0])

    def body(*refs):
        ins, outs, buf = refs[:-nk - 1], refs[-nk - 1:-1], refs[-1]
        for k in range(nk):
            acc = None
            for b, dil in enumerate(DILATIONS):
                rows = _natural_rows(ins[b * nk + k], dil, buf)
                acc = rows if acc is None else acc + rows
            outs[k][...] = acc.astype(BF16)

    tile = pl.BlockSpec((tm, w), lambda i: (i, 0))
    return pl.pallas_call(
        body, name="att_bwd_sum", grid=(t // tm,),
        in_specs=[_regrouped_spec(tm, dil, w) for dil in DILATIONS for _ in range(nk)], out_specs=[tile] * nk,
        out_shape=[jax.ShapeDtypeStruct((t, w), BF16)] * nk,
        scratch_shapes=[_chunk_scratch(tm, w)],
        compiler_params=_params("arbitrary"),
    )(*[a for p in parts for a in p])


def _loss_bwd(h, gain, target):
    t, d = h.shape
    tm = min(512, t)

    def body(h_ref, gain_ref, tg_ref, loss_ref, dh_ref, dgain_ref):
        @pl.when(pl.program_id(0) == 0)
        def _():
            loss_ref[...] = jnp.zeros_like(loss_ref)
            dgain_ref[...] = jnp.zeros_like(dgain_ref)

        hv = h_ref[...]
        r = lax.rsqrt(jnp.mean(hv * hv, axis=-1, keepdims=True) + NORM_EPS)
        xh = hv * r
        err = xh * gain_ref[...] - tg_ref[...]
        sq = _rows8(jnp.square(err))
        loss_ref[...] += 0.5 * functools.reduce(jnp.add, [sq[:, k * LANE:(k + 1) * LANE] for k in range(d // LANE)]) / d
        dy = err / d
        dgain_ref[...] += _rows8(dy * xh)
        dxh = dy * gain_ref[...]
        dh_ref[...] = r * (dxh - xh * jnp.mean(dxh * xh, axis=-1, keepdims=True))

    tile = pl.BlockSpec((tm, d), lambda i: (i, 0))
    return pl.pallas_call(
        body, name="loss_bwd", grid=(t // tm,),
        in_specs=[tile, pl.BlockSpec((1, d), lambda i: (0, 0)), tile],
        out_specs=[pl.BlockSpec((8, LANE), lambda i: (0, 0)), tile, pl.BlockSpec((8, d), lambda i: (0, 0))],
        out_shape=[jax.ShapeDtypeStruct((8, LANE), F32), jax.ShapeDtypeStruct((t, d), F32), jax.ShapeDtypeStruct((8, d), F32)],
        compiler_params=_params("arbitrary"),
    )(h, gain, target)


class _Reduction:
    def __init__(self, place, names, grads):
        self.place, self.names, self.grads = place, names, grads

    def pair(self):
        return _pair_ride(self.grads)

    def chips(self, got):
        self.got = got
        return _chip_ride([_pair_sum(self.place, g, r, f"pair_sum_{n}") for g, r, n in zip(self.grads, got, self.names)])

    def halves(self, others):
        return [_chip_sum(self.place, g, r, o, f"chip_sum_{n}")
                for g, r, o, n in zip(self.grads, self.got, others, self.names)]


def _step(x, target, gains, w, place=None):
    t = x.shape[0]
    ex = place is not None
    g_ffn1, g_mix, g_ret, g_ffn2, g_fin = gains
    w = list(w)
    tabs = _retention_tables(t)
    red = lambda names, grads: _Reduction(place, names, grads) if ex else None
    ride = lambda r: [r] if ex else None

    if ex:
        w[0:3] = _run(_gather_ride(w[0:3]), "gather_ffn1_weights")
    (h1, xn1, ga1, ua1, act1), rest = _ffn_fwd(x, g_ffn1, *w[0:3], "ffn1_fwd", ride(_gather_ride(w[3:])) if ex else None)
    if ex:
        w[3:] = rest[0]
    wg1, wu1, wd1, win, wo, wg2, wu2, wd2 = w
    wo2 = wo.reshape(wo.shape[0] * wo.shape[1], wo.shape[2])
    xnm, u, *uas = _inproj_fwd(h1, g_mix, win)
    raw, mix_r = _ret_fwd(u, g_ret, tabs)
    branches = [_att_fwd(ua, dil) for ua, dil in zip(uas, DILATIONS)]
    mix_a, att, lse = _att_combine([b[0] for b in branches], [b[1] for b in branches], t)
    h2 = _outproj_fwd(h1, mix_r, mix_a, wo2)
    (h3, xn2, ga2, ua2, act2), _ = _ffn_fwd(h2, g_ffn2, wg2, wu2, wd2, "ffn2_fwd")
    loss_p, dh3, dg_fin = _loss_bwd(h3, g_fin, target)

    (dwd2,), _ = _ffn_wgrad_down(act2, dh3, "ffn2_wgrad_down")
    r_d2 = red(["ffn2_w_down"], [dwd2])
    (dh2, dga2, dua2, dg_ffn2), e = _ffn_bwd_data(dh3, h2, g_ffn2, ga2, ua2, wg2, wu2, wd2, "ffn2_bwd",
                                                  ex and [r_d2.pair()])
    (dwg2, dwu2), e = _ffn_wgrad_gu(xn2, dga2, dua2, "ffn2_wgrad_gu", ex and [r_d2.chips(e[0])])
    r_gu2 = red(["ffn2_w_gate", "ffn2_w_up"], [dwg2, dwu2])
    (dmix_r, dmix_a), e = _outproj_bwd(dh2, wo2, ex and [r_gu2.pair(), _finish_ride(r_d2.halves(e[0]))])
    if ex:
        got_gu2, (dwd2,) = e
    hw = RET_WIDTH // (wo.shape[1])
    dwo = jnp.concatenate([_tn_matmul(mix_r, dh2, dh2.shape[1], "wo_grad_r").reshape(hw, wo.shape[1], wo.shape[2]),
                           _tn_matmul(mix_a, dh2, dh2.shape[1], "wo_grad_a").reshape(hw, wo.shape[1], wo.shape[2])])
    r_wo = red(["w_out"], [dwo])
    (dq_r, dgt_r, dret, dg_ret), e = _ret_bwd_q(dmix_r, raw, u, g_ret, tabs, ex and [r_gu2.chips(got_gu2)])
    (dk_r, dv_r), e = _ret_bwd_kv(dret, u, tabs, ex and [r_wo.pair(), _finish_ride(r_gu2.halves(e[0]))])
    if ex:
        got_wo, (dwg2, dwu2) = e
    prep = _att_bwd_prep(dmix_a, att, lse)
    p1, e = _att_bwd(uas[0], *prep[0], DILATIONS[0], ex and [r_wo.chips(got_wo)])
    p4, e = _att_bwd(uas[1], *prep[1], DILATIONS[1], ex and [_finish_ride(r_wo.halves(e[0]))])
    if ex:
        (dwo,), = e
    p16, _ = _att_bwd(uas[2], *prep[2], DILATIONS[2])
    dq_a, dk_a, dv_a = _att_bwd_sum([p1, p4, p16], t)
    dh1, du, dg_mix = _inproj_bwd([dq_r, dk_r, dv_r, dgt_r, dq_a, dk_a, dv_a], h1, g_mix, dh2, win)
    dwin = _tn_matmul(xnm, du, win.shape[2], "win_grad")
    r_in = red(["w_in"], [dwin])
    (dwd1,), e = _ffn_wgrad_down(act1, dh1, "ffn1_wgrad_down", ex and [r_in.pair()])
    r_d1 = red(["ffn1_w_down"], [dwd1])
    (dx, dga1, dua1, dg_ffn1), e = _ffn_bwd_data(dh1, x, g_ffn1, ga1, ua1, wg1, wu1, wd1, "ffn1_bwd",
                                                  ex and [r_in.chips(e[0]), r_d1.pair()])
    (dwg1, dwu1), e = _ffn_wgrad_gu(xn1, dga1, dua1, "ffn1_wgrad_gu",
                                    ex and [_finish_ride(r_in.halves(e[0])), r_d1.chips(e[1])])
    gain_parts = [dg_ffn1, dg_mix, dg_ret, dg_ffn2, dg_fin]
    if not ex:
        return loss_p, dx, [dwg1, dwu1, dwd1, dwin, dwo, dwg2, dwu2, dwd2], gain_parts
    (dwin,), oth_d1 = e
    r_gu1 = red(["ffn1_w_gate", "ffn1_w_up"], [dwg1, dwu1])
    got = _run(r_gu1.pair(), "pair_exchange_ffn1_gate_up")
    oth = _run(r_gu1.chips(got), "chip_exchange_ffn1_gate_up")
    last = r_gu1.halves(oth) + r_d1.halves(oth_d1)
    dwg1, dwu1, dwd1, gall = _run(_finish_ride(last, _pack_gains(gain_parts, x.shape[1])), "finish_exchange_ffn1")
    return loss_p, dx, [dwg1, dwu1, dwd1, dwin, dwo, dwg2, dwu2, dwd2], gall


N_DEV = 8
GAIN_ROWS = 8


def _place():
    x, y, c = lax.axis_index("x"), lax.axis_index("y"), lax.axis_index("c")
    chips = [(1 - x, y), (x, 1 - y), (1 - x, 1 - y)]
    return x, y, c, chips


def _hbm_specs(n):
    return [pl.BlockSpec(memory_space=pl.ANY)] * n


def _place_shard(place, w, name):
    r, cols = w.shape
    tr = r // 4

    def body(place_ref, w_ref, o_ref):
        o_ref[...] = w_ref[...].astype(BF16)

    return pl.pallas_call(
        body, name=name,
        grid_spec=pltpu.PrefetchScalarGridSpec(
            num_scalar_prefetch=1, grid=(r // tr,),
            in_specs=[pl.BlockSpec((tr, cols), lambda i, pr: (i, 0))],
            out_specs=pl.BlockSpec((None, tr, cols), lambda i, pr: (pr[0], i, 0))),
        out_shape=jax.ShapeDtypeStruct((N_SHARD, r, cols), BF16),
        compiler_params=_params("arbitrary"),
    )(place, w)


def _gather_ride(bufs):
    na = len(bufs)

    def legs(outs, sems):
        send_sem, recv_sem, fsend_sem, frecv_sem = sems
        x, y, c, chips = _place()

        def half(a, idx, which):
            hr = outs[a].shape[1] // 2
            return outs[a].at[idx, pl.ds(which * hr, hr)]

        def ici(a, j, idx):
            px, py = chips[j]
            return pltpu.make_async_remote_copy(
                src_ref=half(a, idx, c), dst_ref=half(a, idx, c),
                send_sem=send_sem.at[a, j], recv_sem=recv_sem.at[a, j], device_id=(px, py, c), device_id_type=MESH)

        def d2d(a, j, idx, which):
            return pltpu.make_async_remote_copy(
                src_ref=half(a, idx, which), dst_ref=half(a, idx, which),
                send_sem=fsend_sem.at[a, j], recv_sem=frecv_sem.at[a, j], device_id=(x, y, 1 - c), device_id_type=MESH)

        return 2 * x + y, c, chips, ici, d2d

    def start(ins, outs, sems):
        me, _, _, ici, _ = legs(outs, sems)
        for a in range(na):
            for j in range(3):
                ici(a, j, me).start()

    def finish(ins, outs, sems):
        me, c, chips, ici, d2d = legs(outs, sems)
        passed = []
        for a in range(na):
            for j, (px, py) in enumerate(chips):
                ici(a, j, 2 * px + py).wait_recv()
                cp = d2d(a, j, 2 * px + py, c)
                cp.start()
                passed.append(cp)
        for a in range(na):
            for j, (px, py) in enumerate(chips):
                d2d(a, j, 2 * px + py, 1 - c).wait_recv()
        for a in range(na):
            for j in range(3):
                ici(a, j, me).wait_send()
        for cp in passed:
            cp.wait_send()

    return _Ride(bufs, [jax.ShapeDtypeStruct(b.shape, b.dtype) for b in bufs], [pltpu.SemaphoreType.DMA((na, 3))] * 4,
                 start, finish, {a: a for a in range(na)})


def _pair_ride(grads):
    na = len(grads)

    def copies(ins, outs, sems):
        send_sem, recv_sem = sems
        x, y, c, _ = _place()
        res = []
        for a in range(na):
            hr = ins[a].shape[1] // 2
            res.append(pltpu.make_async_remote_copy(
                src_ref=ins[a].at[:, pl.ds((1 - c) * hr, hr)], dst_ref=outs[a],
                send_sem=send_sem.at[a], recv_sem=recv_sem.at[a], device_id=(x, y, 1 - c), device_id_type=MESH))
        return res

    def start(ins, outs, sems):
        for cp in copies(ins, outs, sems):
            cp.start()

    def finish(ins, outs, sems):
        for cp in copies(ins, outs, sems):
            cp.wait()

    return _Ride(grads, [jax.ShapeDtypeStruct((g.shape[0], g.shape[1] // 2, g.shape[2]), g.dtype) for g in grads],
                 [pltpu.SemaphoreType.DMA((na,))] * 2, start, finish)


def _chip_ride(sums):
    na = len(sums)

    def copies(ins, outs, sems):
        send_sem, recv_sem = sems
        x, y, c, chips = _place()
        res = []
        for a in range(na):
            for j, (px, py) in enumerate(chips):
                res.append(pltpu.make_async_remote_copy(
                    src_ref=ins[a].at[2 * px + py], dst_ref=outs[a].at[j],
                    send_sem=send_sem.at[a, j], recv_sem=recv_sem.at[a, j], device_id=(px, py, c), device_id_type=MESH))
        return res

    def start(ins, outs, sems):
        for cp in copies(ins, outs, sems):
            cp.start()

    def finish(ins, outs, sems):
        for cp in copies(ins, outs, sems):
            cp.wait()

    return _Ride(sums, [jax.ShapeDtypeStruct((3,) + s.shape[1:], s.dtype) for s in sums],
                 [pltpu.SemaphoreType.DMA((na, 3))] * 2, start, finish)


def _finish_ride(grads, gpack=None):
    na = len(grads)

    def halves(outs, sems, which):
        x, y, c, _ = _place()
        res = []
        for a in range(na):
            hr = outs[a].shape[0] // 2
            rows = outs[a].at[pl.ds((c if which == "mine" else 1 - c) * hr, hr)]
            res.append(pltpu.make_async_remote_copy(
                src_ref=rows, dst_ref=rows, send_sem=sems[0].at[a], recv_sem=sems[1].at[a],
                device_id=(x, y, 1 - c), device_id_type=MESH))
        return res

    def gains(ins, outs, sems):
        x, y, c, _ = _place()
        dev = 4 * x + 2 * y + c
        g_in, g_out = ins[na], outs[na]
        own = pltpu.make_async_copy(g_in, g_out.at[dev], sems[2])
        sends, lands = [], []
        for k in range(N_DEV - 1):
            bx, by, bc = (k + 1) // 4, ((k + 1) // 2) % 2, (k + 1) % 2
            peer = (jnp.bitwise_xor(x, bx), jnp.bitwise_xor(y, by), jnp.bitwise_xor(c, bc))
            sends.append(pltpu.make_async_remote_copy(
                src_ref=g_in, dst_ref=g_out.at[dev], send_sem=sems[3].at[k], recv_sem=sems[4].at[k],
                device_id=peer, device_id_type=MESH))
            slot = g_out.at[jnp.bitwise_xor(dev, k + 1)]
            lands.append(pltpu.make_async_remote_copy(
                src_ref=slot, dst_ref=slot, send_sem=sems[3].at[k], recv_sem=sems[4].at[k],
                device_id=peer, device_id_type=MESH))
        return own, sends, lands

    def start(ins, outs, sems):
        for cp in halves(outs, sems, "mine"):
            cp.start()
        if gpack is not None:
            own, sends, _ = gains(ins, outs, sems)
            own.start()
            for cp in sends:
                cp.start()

    def finish(ins, outs, sems):
        for cp in halves(outs, sems, "sibling's"):
            cp.wait_recv()
        if gpack is not None:
            own, sends, lands = gains(ins, outs, sems)
            for cp in lands:
                cp.wait_recv()
            for cp in sends:
                cp.wait_send()
            own.wait()
        for cp in halves(outs, sems, "mine"):
            cp.wait_send()

    shapes = [jax.ShapeDtypeStruct(g.shape, g.dtype) for g in grads]
    sems = [pltpu.SemaphoreType.DMA((na,))] * 2
    if gpack is None:
        return _Ride(grads, shapes, sems, start, finish, {a: a for a in range(na)})
    return _Ride(list(grads) + [gpack], shapes + [jax.ShapeDtypeStruct((N_DEV,) + gpack.shape, gpack.dtype)],
                 sems + [pltpu.SemaphoreType.DMA, pltpu.SemaphoreType.DMA((N_DEV - 1,)), pltpu.SemaphoreType.DMA((N_DEV - 1,))],
                 start, finish, {a: a for a in range(na)})


def _pair_sum(place, grad, got, name):
    ns, r, cols = grad.shape
    hr = r // 2

    def body(place_ref, g_ref, r_ref, o_ref):
        o_ref[...] = (g_ref[...] + r_ref[...]).astype(BF16)

    return pl.pallas_call(
        body, name=name,
        grid_spec=pltpu.PrefetchScalarGridSpec(
            num_scalar_prefetch=1, grid=(ns,),
            in_specs=[pl.BlockSpec((None, hr, cols), lambda s, pr: (s, pr[1], 0)),
                      pl.BlockSpec((None, hr, cols), lambda s, pr: (s, 0, 0))],
            out_specs=pl.BlockSpec((None, hr, cols), lambda s, pr: (s, 0, 0))),
        out_shape=jax.ShapeDtypeStruct((ns, hr, cols), BF16),
        compiler_params=_params("arbitrary"),
    )(place, grad, got)


def _chip_sum(place, grad, got, others, name):
    ns, r, cols = grad.shape
    hr = r // 2
    nb = 2
    tr = hr // nb

    def body(place_ref, g_ref, r_ref, o3_ref, o_ref):
        acc = g_ref[...] + r_ref[...]
        for j in range(3):
            acc = acc + o3_ref[j].astype(F32)
        o_ref[...] = acc

    return pl.pallas_call(
        body, name=name,
        grid_spec=pltpu.PrefetchScalarGridSpec(
            num_scalar_prefetch=1, grid=(nb,),
            in_specs=[pl.BlockSpec((None, tr, cols), lambda i, pr: (pr[0], pr[1] * nb + i, 0)),
                      pl.BlockSpec((None, tr, cols), lambda i, pr: (pr[0], i, 0)),
                      pl.BlockSpec((3, tr, cols), lambda i, pr: (0, i, 0))],
            out_specs=pl.BlockSpec((tr, cols), lambda i, pr: (pr[1] * nb + i, 0))),
        out_shape=jax.ShapeDtypeStruct((r, cols), F32),
        compiler_params=_params("arbitrary"),
    )(place, grad, got, others)


def _pack_gains(parts, d):
    def body(*refs):
        ins, o_ref = refs[:-1], refs[-1]
        o_ref[...] = jnp.zeros_like(o_ref)
        for k, r in enumerate(ins):
            o_ref[k:k + 1, 0:r.shape[1]] = jnp.sum(r[...], axis=0, keepdims=True)

    return pl.pallas_call(
        body, name="pack_gains", out_shape=jax.ShapeDtypeStruct((GAIN_ROWS, d), F32),
    )(*parts)


def _adamw_math(w, g, m, v):
    m = ADAM_B1 * m + (1.0 - ADAM_B1) * g
    v = ADAM_B2 * v + (1.0 - ADAM_B2) * jnp.square(g)
    m_hat = m / (1.0 - ADAM_B1 ** ADAM_STEP)
    v_hat = v / (1.0 - ADAM_B2 ** ADAM_STEP)
    return -ADAM_LR * (m_hat / (jnp.sqrt(v_hat) + ADAM_EPS) + ADAM_WD * w), m, v


def _adamw(w, g, m, v, name):
    r, cols = w.shape
    tr = r // 4 if (r // 4) % 8 == 0 else r

    def body(w_ref, g_ref, m_ref, v_ref, d_ref, nm_ref, nv_ref):
        d_ref[...], nm_ref[...], nv_ref[...] = _adamw_math(w_ref[...], g_ref[...], m_ref[...], v_ref[...])

    tile = pl.BlockSpec((tr, cols), lambda i: (i, 0))
    return pl.pallas_call(
        body, name=name, grid=(r // tr,), in_specs=[tile] * 4, out_specs=[tile] * 3,
        out_shape=[jax.ShapeDtypeStruct((r, cols), F32)] * 3,
        compiler_params=_params("arbitrary"),
    )(w, g, m, v)


def _adamw_gain(gall, row, w, m, v, name):
    n = w.shape[1]

    def body(ga_ref, w_ref, m_ref, v_ref, g_ref, d_ref, nm_ref, nv_ref):
        g = ga_ref[0, row:row + 1, 0:n]
        for k in range(1, N_DEV):
            g = g + ga_ref[k, row:row + 1, 0:n]
        g_ref[...] = g
        d_ref[...], nm_ref[...], nv_ref[...] = _adamw_math(w_ref[...], g, m_ref[...], v_ref[...])

    return pl.pallas_call(
        body, name=name, out_shape=[jax.ShapeDtypeStruct((1, n), F32)] * 4,
    )(gall, w, m, v)


def kernel(x, norm_ffn1, ffn1_w_gate, ffn1_w_up, ffn1_w_down, norm_mix, w_in, ret_norm_gain, w_out, norm_ffn2, ffn2_w_gate, ffn2_w_up, ffn2_w_down, norm_final, loss_target, m_norm_ffn1, m_ffn1_w_gate, m_ffn1_w_up, m_ffn1_w_down, m_norm_mix, m_w_in, m_ret_norm_gain, m_w_out, m_norm_ffn2, m_ffn2_w_gate, m_ffn2_w_up, m_ffn2_w_down, m_norm_final, v_norm_ffn1, v_ffn1_w_gate, v_ffn1_w_up, v_ffn1_w_down, v_norm_mix, v_w_in, v_ret_norm_gain, v_w_out, v_norm_ffn2, v_ffn2_w_gate, v_ffn2_w_up, v_ffn2_w_down, v_norm_final):
    d = x.shape[-1]
    mats = [ffn1_w_gate, ffn1_w_up, ffn1_w_down, w_in, w_out, ffn2_w_gate, ffn2_w_up, ffn2_w_down]
    mats_m = [m_ffn1_w_gate, m_ffn1_w_up, m_ffn1_w_down, m_w_in, m_w_out, m_ffn2_w_gate, m_ffn2_w_up, m_ffn2_w_down]
    mats_v = [v_ffn1_w_gate, v_ffn1_w_up, v_ffn1_w_down, v_w_in, v_w_out, v_ffn2_w_gate, v_ffn2_w_up, v_ffn2_w_down]
    mat_names = ["ffn1_w_gate", "ffn1_w_up", "ffn1_w_down", "w_in", "w_out", "ffn2_w_gate", "ffn2_w_up", "ffn2_w_down"]
    gains = [norm_ffn1, norm_mix, ret_norm_gain, norm_ffn2, norm_final.reshape(1, d)]
    gains_m = [m_norm_ffn1, m_norm_mix, m_ret_norm_gain, m_norm_ffn2, m_norm_final.reshape(1, d)]
    gains_v = [v_norm_ffn1, v_norm_mix, v_ret_norm_gain, v_norm_ffn2, v_norm_final.reshape(1, d)]
    gain_names = ["norm_ffn1", "norm_mix", "ret_norm_gain", "norm_ffn2", "norm_final"]

    shards = [w[0] for w in mats]
    place = jnp.stack([2 * lax.axis_index("x") + lax.axis_index("y"), lax.axis_index("c")]).astype(jnp.int32)
    placed = [_place_shard(place, s, f"place_{n}") for s, n in zip(shards, mat_names)]
    loss_p, dx, shard_grads, gall = _step(x[0], loss_target[0], gains, placed, place)

    out_g, out_d, out_m, out_v = {}, {}, {}, {}
    for n, w, g, m, v in zip(mat_names, shards, shard_grads, mats_m, mats_v):
        dl, nm, nv = _adamw(w, g, m[0], v[0], f"adamw_{n}")
        out_g[n], out_d[n], out_m[n], out_v[n] = g[None], dl[None], nm[None], nv[None]
    for row, (n, w, m, v) in enumerate(zip(gain_names, gains, gains_m, gains_v)):
        res = _adamw_gain(gall, row, w, m, v, f"adamw_{n}")
        shape = (d,) if n == "norm_final" else w.shape
        out_g[n], out_d[n], out_m[n], out_v[n] = [r.reshape(shape) for r in res]

    loss = lax.psum(jnp.sum(loss_p), ("x", "y", "c"))
    order = ["norm_ffn1", "ffn1_w_gate", "ffn1_w_up", "ffn1_w_down", "norm_mix", "w_in", "ret_norm_gain", "w_out",
             "norm_ffn2", "ffn2_w_gate", "ffn2_w_up", "ffn2_w_down", "norm_final"]
    return (loss, dx[None], *[out_g[n] for n in order], *[out_d[n] for n in order],
            *[out_m[n] for n in order], *[out_v[n] for n in order])
```

```python
import functools
import math

import jax
import jax.numpy as jnp
from jax import lax
from jax.experimental import pallas as pl
from jax.experimental.pallas import tpu as pltpu

F32 = jnp.float32
BF16 = jnp.bfloat16
MESH = pl.DeviceIdType.MESH

NORM_EPS = 1e-6
GN_EPS = 1e-6
ROPE_BASE = 10000.0
RET_HEADS = 4
RET_DIM = 128
RET_WIDTH = 512
RET_CHUNK = 128
ATT_HEADS = 8
ATT_DIM = 64
ATT_WIDTH = 512
ATT_BLOCK = 128
DILATIONS = (1, 4, 16)
IN_COLS = 4 * RET_WIDTH + 3 * ATT_WIDTH
LANE = 128
N_SHARD = 4
ADAM_LR, ADAM_B1, ADAM_B2, ADAM_EPS, ADAM_WD, ADAM_STEP = 0.001, 0.9, 0.999, 1e-08, 0.01, 10

V7X_VMEM_BYTES = 64 * 1024 * 1024
VMEM_LIMIT = V7X_VMEM_BYTES - 8 * 1024 * 1024

NT = (((1,), (1,)), ((), ()))
TN = (((0,), (0,)), ((), ()))


def _params(*sem):
    return pltpu.CompilerParams(dimension_semantics=sem, vmem_limit_bytes=VMEM_LIMIT)


def _dot(a, b, dims=None):
    if dims is None:
        return jnp.dot(a, b, preferred_element_type=F32)
    return lax.dot_general(a, b, dims, preferred_element_type=F32)


def _sigmoid(x):
    return 1.0 / (1.0 + jnp.exp(-x))


def _load_weights(pairs, sems):
    copies = [pltpu.make_async_copy(src, dst, sems.at[k]) for k, (src, dst) in enumerate(pairs)]
    for cp in copies:
        cp.start()
    for cp in copies:
        cp.wait()


def _rows8(v):
    r, c = v.shape
    return v.reshape(r // 8, 8, c).sum(axis=0)


class _Ride:
    def __init__(self, inputs, out_shapes, sems, start, finish, aliases=None):
        self.inputs, self.out_shapes, self.sems = list(inputs), list(out_shapes), list(sems)
        self.start, self.finish, self.aliases = start, finish, dict(aliases or {})


def _pallas(body, rides, *, name, in_specs, out_specs, out_shape, args, grid=(), scratch_shapes=(), sem=()):
    rides = [r for r in (rides or []) if r is not None]
    n_in, n_out, n_scr = len(args), len(out_shape), len(scratch_shapes)
    hbm = pl.BlockSpec(memory_space=pl.ANY)
    r_in = [a for r in rides for a in r.inputs]
    r_out = [s for r in rides for s in r.out_shapes]
    r_sem = [s for r in rides for s in r.sems]
    aliases, spans, ki, ko, ks = {}, [], 0, 0, 0
    for r in rides:
        aliases.update({n_in + ki + i: n_out + ko + o for i, o in r.aliases.items()})
        spans.append((ki, ko, ks))
        ki, ko, ks = ki + len(r.inputs), ko + len(r.out_shapes), ks + len(r.sems)

    def wrapped(*refs):
        ins, rin = refs[:n_in], refs[n_in:n_in + len(r_in)]
        o0 = n_in + len(r_in)
        outs, rout = refs[o0:o0 + n_out], refs[o0 + n_out:o0 + n_out + len(r_out)]
        s0 = o0 + n_out + len(r_out)
        scr, rsem = refs[s0:s0 + n_scr], refs[s0 + n_scr:]
        part = lambda r, k: (rin[spans[k][0]:spans[k][0] + len(r.inputs)], rout[spans[k][1]:spans[k][1] + len(r.out_shapes)],
                             rsem[spans[k][2]:spans[k][2] + len(r.sems)])
        first = functools.reduce(jnp.logical_and, [pl.program_id(k) == 0 for k in range(len(grid))], True)
        last = functools.reduce(jnp.logical_and, [pl.program_id(k) == grid[k] - 1 for k in range(len(grid))], True)
        if rides:
            @pl.when(first)
            def _():
                for k, r in enumerate(rides):
                    r.start(*part(r, k))

        body(*ins, *outs, *scr)
        if rides:
            @pl.when(last)
            def _():
                for k, r in enumerate(rides):
                    r.finish(*part(r, k))

    res = pl.pallas_call(
        wrapped, name=name, grid=grid,
        in_specs=list(in_specs) + [hbm] * len(r_in), out_specs=list(out_specs) + [hbm] * len(r_out),
        out_shape=list(out_shape) + r_out, input_output_aliases=aliases,
        scratch_shapes=list(scratch_shapes) + r_sem,
        compiler_params=pltpu.CompilerParams(dimension_semantics=sem, vmem_limit_bytes=VMEM_LIMIT) if grid else None,
    )(*args, *r_in)
    extras = [list(res[n_out + ko:n_out + ko + len(r.out_shapes)]) for r, (_, ko, _) in zip(rides, spans)]
    return list(res[:n_out]), extras


def _run(ride, name):
    def body(*refs):
        n_in, n_out = len(ride.inputs), len(ride.out_shapes)
        parts = refs[:n_in], refs[n_in:n_in + n_out], refs[n_in + n_out:]
        ride.start(*parts)
        ride.finish(*parts)

    hbm = pl.BlockSpec(memory_space=pl.ANY)
    return list(pl.pallas_call(
        body, name=name, in_specs=[hbm] * len(ride.inputs), out_specs=[hbm] * len(ride.out_shapes),
        out_shape=ride.out_shapes, input_output_aliases=ride.aliases, scratch_shapes=ride.sems,
    )(*ride.inputs))


def _ffn_fwd(x, gain, wg, wu, wd, name, rides=None):
    t, d = x.shape
    ns, fs, _ = wg.shape
    tm = min(256, t)

    def body(x_ref, gain_ref, wg_hbm, wu_hbm, wd_hbm, h_ref, xn_ref, g_ref, u_ref, a_ref, wg_v, wu_v, wd_v, sems):
        @pl.when(pl.program_id(0) == 0)
        def _():
            _load_weights([(wg_hbm, wg_v), (wu_hbm, wu_v), (wd_hbm, wd_v)], sems)

        xv = x_ref[...]
        r = lax.rsqrt(jnp.mean(xv * xv, axis=-1, keepdims=True) + NORM_EPS)
        xn = (xv * r * gain_ref[...]).astype(BF16)
        xn_ref[...] = xn
        acc = jnp.zeros((tm, d), F32)
        for j in range(ns):
            g = _dot(xn, wg_v[j], NT)
            u = _dot(xn, wu_v[j], NT)
            g_ref[j] = g.astype(BF16)
            u_ref[j] = u.astype(BF16)
            a = (g * _sigmoid(g) * u).astype(BF16)
            a_ref[j] = a
            acc = acc + _dot(a, wd_v[j])
        h_ref[...] = xv + 0.5 * acc

    hbm = pl.BlockSpec(memory_space=pl.ANY)
    hid = pl.BlockSpec((ns, tm, fs), lambda i: (0, i, 0))
    return _pallas(
        body, rides, name=name, grid=(t // tm,),
        in_specs=[pl.BlockSpec((tm, d), lambda i: (i, 0)), pl.BlockSpec((1, d), lambda i: (0, 0)), hbm, hbm, hbm],
        out_specs=[pl.BlockSpec((tm, d), lambda i: (i, 0)), pl.BlockSpec((tm, d), lambda i: (i, 0)), hid, hid, hid],
        out_shape=[jax.ShapeDtypeStruct((t, d), F32), jax.ShapeDtypeStruct((t, d), BF16)]
        + [jax.ShapeDtypeStruct((ns, t, fs), BF16)] * 3,
        scratch_shapes=[pltpu.VMEM(wg.shape, BF16), pltpu.VMEM(wu.shape, BF16), pltpu.VMEM(wd.shape, BF16),
                        pltpu.SemaphoreType.DMA((3,))],
        sem=("arbitrary",), args=[x, gain, wg, wu, wd])


def _ffn_bwd_data(dy, x, gain, g, u, wg, wu, wd, name, rides=None):
    t, d = x.shape
    ns, fs, _ = wg.shape
    tm = min(256, t)

    def body(dy_ref, x_ref, gain_ref, g_ref, u_ref, wg_hbm, wu_hbm, wd_hbm, dx_ref, dg_ref, du_ref, dgain_ref,
             wg_v, wu_v, wd_v, sems):
        @pl.when(pl.program_id(0) == 0)
        def _():
            _load_weights([(wg_hbm, wg_v), (wu_hbm, wu_v), (wd_hbm, wd_v)], sems)
            dgain_ref[...] = jnp.zeros_like(dgain_ref)

        dyv = dy_ref[...]
        dyh = (0.5 * dyv).astype(BF16)
        dxn = jnp.zeros((tm, d), F32)
        for j in range(ns):
            da = _dot(dyh, wd_v[j], NT)
            gj = g_ref[j].astype(F32)
            uj = u_ref[j].astype(F32)
            sig = _sigmoid(gj)
            dgj = (da * uj * (sig * (1.0 + gj * (1.0 - sig)))).astype(BF16)
            duj = (da * (gj * sig)).astype(BF16)
            dg_ref[j] = dgj
            du_ref[j] = duj
            dxn = dxn + _dot(dgj, wg_v[j]) + _dot(duj, wu_v[j])
        xv = x_ref[...]
        r = lax.rsqrt(jnp.mean(xv * xv, axis=-1, keepdims=True) + NORM_EPS)
        xh = xv * r
        dgain_ref[...] += _rows8(dxn * xh)
        dxh = dxn * gain_ref[...]
        dx_ref[...] = dyv + r * (dxh - xh * jnp.mean(dxh * xh, axis=-1, keepdims=True))

    hbm = pl.BlockSpec(memory_space=pl.ANY)
    tile = pl.BlockSpec((tm, d), lambda i: (i, 0))
    hid = pl.BlockSpec((ns, tm, fs), lambda i: (0, i, 0))
    return _pallas(
        body, rides, name=name, grid=(t // tm,),
        in_specs=[tile, tile, pl.BlockSpec((1, d), lambda i: (0, 0)), hid, hid, hbm, hbm, hbm],
        out_specs=[tile, hid, hid, pl.BlockSpec((8, d), lambda i: (0, 0))],
        out_shape=[jax.ShapeDtypeStruct((t, d), F32), jax.ShapeDtypeStruct((ns, t, fs), BF16),
                   jax.ShapeDtypeStruct((ns, t, fs), BF16), jax.ShapeDtypeStruct((8, d), F32)],
        scratch_shapes=[pltpu.VMEM(wg.shape, BF16), pltpu.VMEM(wu.shape, BF16), pltpu.VMEM(wd.shape, BF16),
                        pltpu.SemaphoreType.DMA((3,))],
        sem=("arbitrary",), args=[dy, x, gain, g, u, wg, wu, wd])


def _ffn_wgrad_down(a, dy, name, rides=None):
    t, d = dy.shape
    ns, _, fs = a.shape
    tk = min(1024, t)

    def body(dy_ref, a_ref, dwd_ref):
        @pl.when(pl.program_id(1) == 0)
        def _():
            dwd_ref[...] = jnp.zeros_like(dwd_ref)

        dwd_ref[...] += _dot(a_ref[...], (0.5 * dy_ref[...]).astype(BF16), TN)

    return _pallas(
        body, rides, name=name, grid=(ns, t // tk),
        in_specs=[pl.BlockSpec((tk, d), lambda j, k: (k, 0)), pl.BlockSpec((None, tk, fs), lambda j, k: (j, k, 0))],
        out_specs=[pl.BlockSpec((None, fs, d), lambda j, k: (j, 0, 0))],
        out_shape=[jax.ShapeDtypeStruct((ns, fs, d), F32)],
        sem=("arbitrary", "arbitrary"), args=[dy, a])


def _ffn_wgrad_gu(xn, dg, du, name, rides=None):
    t, d = xn.shape
    ns, _, fs = dg.shape
    tk = min(2048, t)

    def body(xn_ref, dg_ref, du_ref, dwg_ref, dwu_ref):
        @pl.when(pl.program_id(1) == 0)
        def _():
            dwg_ref[...] = jnp.zeros_like(dwg_ref)
            dwu_ref[...] = jnp.zeros_like(dwu_ref)

        xnv = xn_ref[...]
        dwg_ref[...] += _dot(dg_ref[...], xnv, TN)
        dwu_ref[...] += _dot(du_ref[...], xnv, TN)

    hid = pl.BlockSpec((None, tk, fs), lambda j, k: (j, k, 0))
    out = pl.BlockSpec((None, fs, d), lambda j, k: (j, 0, 0))
    return _pallas(
        body, rides, name=name, grid=(ns, t // tk),
        in_specs=[pl.BlockSpec((tk, d), lambda j, k: (k, 0)), hid, hid],
        out_specs=[out, out], out_shape=[jax.ShapeDtypeStruct((ns, fs, d), F32)] * 2,
        sem=("arbitrary", "arbitrary"), args=[xn, dg, du])


def _tn_matmul(a, b, bn, name):
    t, m = a.shape
    n = b.shape[1]
    tk = min(2048, t)

    def body(a_ref, b_ref, o_ref):
        @pl.when(pl.program_id(1) == 0)
        def _():
            o_ref[...] = jnp.zeros_like(o_ref)

        o_ref[...] += _dot(a_ref[...].astype(BF16), b_ref[...].astype(BF16), TN)

    return pl.pallas_call(
        body, name=name, grid=(n // bn, t // tk),
        in_specs=[pl.BlockSpec((tk, m), lambda j, k: (k, 0)), pl.BlockSpec((tk, bn), lambda j, k: (k, j))],
        out_specs=pl.BlockSpec((None, m, bn), lambda j, k: (j, 0, 0)),
        out_shape=jax.ShapeDtypeStruct((n // bn, m, bn), F32),
        compiler_params=_params("arbitrary", "arbitrary"),
    )(a, b)


def _chunk_scratch(tm, w):
    return pltpu.VMEM((w // LANE, tm, LANE), F32)


def _regroup_store(cbuf, out_ref, dil):
    n = out_ref.shape[1]
    for g in range(dil):
        for k in range(cbuf.shape[0]):
            rows = cbuf[k] if dil == 1 else cbuf[k, pl.ds(g, n, stride=dil), :]
            out_ref[g, :, k * LANE:(k + 1) * LANE] = rows.astype(out_ref.dtype)


def _natural_rows(ref, dil, cbuf):
    if dil == 1:
        return ref[0]
    n = ref.shape[1]
    for g in range(dil):
        for k in range(cbuf.shape[0]):
            cbuf[k, pl.ds(g, n, stride=dil), :] = ref[g, :, k * LANE:(k + 1) * LANE]
    return jnp.concatenate([cbuf[k] for k in range(cbuf.shape[0])], axis=1)


def _inproj_fwd(h, gain, win):
    t, d = h.shape
    ns, _, cs = win.shape
    tm = min(512, t)
    rw, aw = 4 * RET_WIDTH, 3 * ATT_WIDTH

    def body(h_ref, gain_ref, w_ref, xn_ref, ur_ref, *rest):
        a_refs, abuf = rest[:-1], rest[-1]
        hv = h_ref[...]
        r = lax.rsqrt(jnp.mean(hv * hv, axis=-1, keepdims=True) + NORM_EPS)
        xn = (hv * r * gain_ref[...]).astype(BF16)
        xn_ref[...] = xn
        for j in range(ns):
            res = _dot(xn, w_ref[j])
            for k in range(cs // LANE):
                chunk = j * (cs // LANE) + k
                piece = res[:, k * LANE:(k + 1) * LANE]
                if chunk < rw // LANE:
                    ur_ref[:, chunk * LANE:(chunk + 1) * LANE] = piece
                else:
                    abuf[chunk - rw // LANE] = piece
        for dil, a_ref in zip(DILATIONS, a_refs):
            _regroup_store(abuf, a_ref, dil)

    return pl.pallas_call(
        body, name="inproj_fwd", grid=(t // tm,),
        in_specs=[pl.BlockSpec((tm, d), lambda i: (i, 0)), pl.BlockSpec((1, d), lambda i: (0, 0)),
                  pl.BlockSpec(win.shape, lambda i: (0, 0, 0))],
        out_specs=[pl.BlockSpec((tm, d), lambda i: (i, 0)), pl.BlockSpec((tm, rw), lambda i: (i, 0))]
        + [pl.BlockSpec((dil, tm // dil, aw), lambda i: (0, i, 0)) for dil in DILATIONS],
        out_shape=[jax.ShapeDtypeStruct((t, d), BF16), jax.ShapeDtypeStruct((t, rw), F32)]
        + [jax.ShapeDtypeStruct((dil, t // dil, aw), BF16) for dil in DILATIONS],
        scratch_shapes=[_chunk_scratch(tm, aw)],
        compiler_params=_params("arbitrary"),
    )(h, gain, win)


def _inproj_bwd(pieces, h, gain, dres, win):
    t, d = h.shape
    ns, _, cs = win.shape
    pw = pieces[0].shape[1]
    tm = min(512, t)
    npc = len(pieces)

    def body(*refs):
        p_refs = refs[:npc]
        h_ref, gain_ref, dres_ref, w_ref, dh_ref, du_ref, dgain_ref = refs[npc:]

        @pl.when(pl.program_id(0) == 0)
        def _():
            dgain_ref[...] = jnp.zeros_like(dgain_ref)

        for k in range(npc):
            du_ref[:, k * pw:(k + 1) * pw] = p_refs[k][...]
        dxn = jnp.zeros((tm, d), F32)
        for j in range(ns):
            dxn = dxn + _dot(du_ref[:, j * cs:(j + 1) * cs], w_ref[j], NT)
        hv = h_ref[...]
        r = lax.rsqrt(jnp.mean(hv * hv, axis=-1, keepdims=True) + NORM_EPS)
        xh = hv * r
        dgain_ref[...] += _rows8(dxn * xh)
        dxh = dxn * gain_ref[...]
        dh_ref[...] = dres_ref[...] + r * (dxh - xh * jnp.mean(dxh * xh, axis=-1, keepdims=True))

    tile = pl.BlockSpec((tm, d), lambda i: (i, 0))
    return pl.pallas_call(
        body, name="inproj_bwd", grid=(t // tm,),
        in_specs=[pl.BlockSpec((tm, pw), lambda i: (i, 0))] * npc + [
            tile, pl.BlockSpec((1, d), lambda i: (0, 0)), tile, pl.BlockSpec(win.shape, lambda i: (0, 0, 0))],
        out_specs=[tile, pl.BlockSpec((tm, npc * pw), lambda i: (i, 0)), pl.BlockSpec((8, d), lambda i: (0, 0))],
        out_shape=[jax.ShapeDtypeStruct((t, d), F32), jax.ShapeDtypeStruct((t, npc * pw), BF16),
                   jax.ShapeDtypeStruct((8, d), F32)],
        compiler_params=_params("arbitrary"),
    )(*pieces, h, gain, dres, win)


def _outproj_fwd(h, mix_r, mix_a, wo):
    t, d = h.shape
    hw = mix_r.shape[1]
    tm = min(512, t)

    def body(h_ref, mr_ref, ma_ref, w_ref, o_ref):
        o_ref[...] = h_ref[...] + _dot(mr_ref[...], w_ref[0:hw, :]) + _dot(ma_ref[...], w_ref[hw:2 * hw, :])

    tile = pl.BlockSpec((tm, d), lambda i: (i, 0))
    half = pl.BlockSpec((tm, hw), lambda i: (i, 0))
    return pl.pallas_call(
        body, name="outproj_fwd", grid=(t // tm,),
        in_specs=[tile, half, half, pl.BlockSpec(wo.shape, lambda i: (0, 0))],
        out_specs=tile, out_shape=jax.ShapeDtypeStruct((t, d), F32),
        compiler_params=_params("arbitrary"),
    )(h, mix_r, mix_a, wo)


def _outproj_bwd(dh, wo, rides=None):
    t, d = dh.shape
    hw = wo.shape[0] // 2
    tm = min(512, t)

    def body(dh_ref, w_ref, dr_ref, da_ref):
        dhb = dh_ref[...].astype(BF16)
        dr_ref[...] = _dot(dhb, w_ref[0:hw, :], NT)
        da_ref[...] = _dot(dhb, w_ref[hw:2 * hw, :], NT)

    half = pl.BlockSpec((tm, hw), lambda i: (i, 0))
    return _pallas(
        body, rides, name="outproj_bwd", grid=(t // tm,),
        in_specs=[pl.BlockSpec((tm, d), lambda i: (i, 0)), pl.BlockSpec(wo.shape, lambda i: (0, 0))],
        out_specs=[half, half],
        out_shape=[jax.ShapeDtypeStruct((t, hw), F32), jax.ShapeDtypeStruct((t, hw), F32)],
        sem=("arbitrary",), args=[dh, wo])


def _retention_tables(t):
    pos = jnp.arange(t, dtype=F32)
    inv_freq = ROPE_BASE ** (-jnp.arange(0, RET_DIM, 2, dtype=F32) / RET_DIM)
    ang = jnp.repeat(pos[:, None] * inv_freq[None, :], 2, axis=-1)
    c = RET_CHUNK
    log_g = jnp.log(1.0 - 2.0 ** (-5.0 - jnp.arange(RET_HEADS, dtype=F32)))
    idx = jnp.arange(c, dtype=F32)
    rel = idx[:, None] - idx[None, :]
    decay = jnp.where(rel >= 0, jnp.exp(log_g[:, None, None] * jnp.maximum(rel, 0.0)), 0.0)
    zeta = jnp.exp(log_g[:, None] * (c - 1 - idx)[None, :])
    xi = jnp.exp(log_g[:, None] * (idx + 1)[None, :])
    gc = jnp.exp(log_g * c)
    wide = lambda v: jnp.broadcast_to(v[:, :, None], (RET_HEADS, c, LANE))
    return (jnp.cos(ang), jnp.sin(ang), decay, wide(zeta), wide(xi),
            jnp.broadcast_to(gc[:, None, None], (RET_HEADS, c, LANE)))


def _rot(v):
    lane = lax.broadcasted_iota(jnp.int32, v.shape, 1)
    nxt = pltpu.roll(v, LANE - 1, 1)
    prv = pltpu.roll(v, 1, 1)
    return jnp.where(lane % 2 == 0, -nxt, prv)


def _ret_specs(tr, rev, nt):
    ti = (lambda i: nt - 1 - i) if rev else (lambda i: i)
    col = lambda blk: pl.BlockSpec((tr, RET_WIDTH), lambda i: (ti(i), blk))
    tab = pl.BlockSpec((tr, LANE), lambda i: (ti(i), 0))
    head = pl.BlockSpec((RET_HEADS, RET_CHUNK, LANE), lambda i: (0, 0, 0))
    return col, tab, head


def _ret_chunks(tr, rev=False):
    order = list(range(tr // RET_CHUNK))
    return [(pl.ds(ci * RET_CHUNK, RET_CHUNK), slice(h * RET_DIM, (h + 1) * RET_DIM), h)
            for h in range(RET_HEADS) for ci in (reversed(order) if rev else order)]


def _ret_operands(items, q_ref, k_ref, v_ref, cos_ref, sin_ref, zeta_ref):
    scale = RET_DIM ** -0.5
    qbs, kbs, vbs, kzs = [], [], [], []
    for sl, hs, h in items:
        cs, sn = cos_ref[sl, :], sin_ref[sl, :]
        q, k = q_ref[sl, hs], k_ref[sl, hs]
        kr = (k * cs + _rot(k) * sn) * scale
        qbs.append((q * cs + _rot(q) * sn).astype(BF16))
        kbs.append(kr.astype(BF16))
        vbs.append(v_ref[sl, hs].astype(BF16))
        kzs.append((kr * zeta_ref[h]).astype(BF16))
    return qbs, kbs, vbs, kzs


def _ret_states(items, state, steps, gc_ref):
    cur, befores = {}, []
    for (sl, hs, h), step in zip(items, steps):
        st = cur[h] if h in cur else state[h]
        befores.append(st)
        cur[h] = st * gc_ref[h] + step
    for h, st in cur.items():
        state[h] = st
    return befores


def _ret_fwd(u, gain, tabs):
    t = u.shape[0]
    tr = min(512, t)
    nt = t // tr
    cos, sin, decay, zeta, xi, gc = tabs
    scale = RET_DIM ** -0.5

    def body(q_ref, k_ref, v_ref, gt_ref, cos_ref, sin_ref, gain_ref, dec_ref, zeta_ref, xi_ref, gc_ref,
             raw_ref, mix_ref, state):
        @pl.when(pl.program_id(0) == 0)
        def _():
            state[...] = jnp.zeros_like(state)

        items = _ret_chunks(tr)
        n = range(len(items))
        qbs, kbs, vbs, kzs = _ret_operands(items, q_ref, k_ref, v_ref, cos_ref, sin_ref, zeta_ref)
        ss = [_dot(qbs[i], kbs[i], NT) for i in n]
        kvs = [_dot(kzs[i], vbs[i], TN) for i in n]
        befores = _ret_states(items, state, kvs, gc_ref)
        intra = [_dot((ss[i] * dec_ref[items[i][2]]).astype(BF16), vbs[i]) for i in n]
        inter = [_dot(qbs[i], befores[i].astype(BF16)) for i in n]
        for i, (sl, hs, h) in enumerate(items):
            o = intra[i] + inter[i] * xi_ref[h]
            raw_ref[sl, hs] = o
            mu = jnp.mean(o, axis=-1, keepdims=True)
            var = jnp.mean(jnp.square(o - mu), axis=-1, keepdims=True)
            y = (o - mu) * lax.rsqrt(var + GN_EPS) * gain_ref[:, hs]
            gt = gt_ref[sl, hs]
            mix_ref[sl, hs] = (y * (gt * _sigmoid(gt))).astype(BF16)

    col, tab, head = _ret_specs(tr, False, nt)
    out = pl.BlockSpec((tr, RET_WIDTH), lambda i: (i, 0))
    return pl.pallas_call(
        body, name="ret_fwd", grid=(nt,),
        in_specs=[col(0), col(1), col(2), col(3), tab, tab, pl.BlockSpec((1, RET_WIDTH), lambda i: (0, 0)),
                  head, head, head, head],
        out_specs=[out, out],
        out_shape=[jax.ShapeDtypeStruct((t, RET_WIDTH), F32), jax.ShapeDtypeStruct((t, RET_WIDTH), BF16)],
        scratch_shapes=[pltpu.VMEM((RET_HEADS, RET_DIM, RET_DIM), F32)],
        compiler_params=_params("arbitrary"),
    )(u, u, u, u, cos, sin, gain, decay, zeta, xi, gc)


def _ret_bwd_q(dmix, raw, u, gain, tabs, rides=None):
    t = u.shape[0]
    tr = min(512, t)
    nt = t // tr
    cos, sin, decay, zeta, xi, gc = tabs
    scale = RET_DIM ** -0.5

    def body(dm_ref, raw_ref, q_ref, k_ref, v_ref, gt_ref, cos_ref, sin_ref, gain_ref, dec_ref, zeta_ref, xi_ref, gc_ref,
             dq_ref, dgt_ref, dret_ref, dgain_ref, state):
        @pl.when(pl.program_id(0) == 0)
        def _():
            state[...] = jnp.zeros_like(state)
            dgain_ref[...] = jnp.zeros_like(dgain_ref)

        items = _ret_chunks(tr)
        n_items = range(len(items))
        qbs, kbs, vbs, kzs = _ret_operands(items, q_ref, k_ref, v_ref, cos_ref, sin_ref, zeta_ref)
        dos, dgains = [], {}
        for sl, hs, h in items:
            o = raw_ref[sl, hs]
            mu = jnp.mean(o, axis=-1, keepdims=True)
            var = jnp.mean(jnp.square(o - mu), axis=-1, keepdims=True)
            rs = lax.rsqrt(var + GN_EPS)
            n = (o - mu) * rs
            gt = gt_ref[sl, hs]
            sig = _sigmoid(gt)
            dout = dm_ref[sl, hs]
            gain_h = gain_ref[:, hs]
            dgt_ref[sl, hs] = (dout * (n * gain_h) * (sig * (1.0 + gt * (1.0 - sig)))).astype(BF16)
            dy = dout * (gt * sig)
            dgains[h] = dgains[h] + _rows8(dy * n) if h in dgains else _rows8(dy * n)
            dn = dy * gain_h
            do = rs * (dn - jnp.mean(dn, axis=-1, keepdims=True) - n * jnp.mean(dn * n, axis=-1, keepdims=True))
            dret_ref[sl, hs] = do
            dos.append(do)
        for h, dg in dgains.items():
            dgain_ref[:, h * RET_DIM:(h + 1) * RET_DIM] += dg
        dss = [_dot(dos[i].astype(BF16), vbs[i], NT) for i in n_items]
        kvs = [_dot(kzs[i], vbs[i], TN) for i in n_items]
        befores = _ret_states(items, state, kvs, gc_ref)
        intra = [_dot((dss[i] * dec_ref[items[i][2]]).astype(BF16), kbs[i]) for i in n_items]
        inter = [_dot((dos[i] * xi_ref[items[i][2]]).astype(BF16), befores[i].astype(BF16), NT) for i in n_items]
        for i, (sl, hs, h) in enumerate(items):
            dqr = intra[i] + inter[i]
            dq_ref[sl, hs] = (dqr * cos_ref[sl, :] - _rot(dqr * sin_ref[sl, :])).astype(BF16)

    col, tab, head = _ret_specs(tr, False, nt)
    out = pl.BlockSpec((tr, RET_WIDTH), lambda i: (i, 0))
    return _pallas(
        body, rides, name="ret_bwd_q", grid=(nt,),
        in_specs=[out, out, col(0), col(1), col(2), col(3), tab, tab, pl.BlockSpec((1, RET_WIDTH), lambda i: (0, 0)),
                  head, head, head, head],
        out_specs=[out, out, out, pl.BlockSpec((8, RET_WIDTH), lambda i: (0, 0))],
        out_shape=[jax.ShapeDtypeStruct((t, RET_WIDTH), BF16), jax.ShapeDtypeStruct((t, RET_WIDTH), BF16),
                   jax.ShapeDtypeStruct((t, RET_WIDTH), F32), jax.ShapeDtypeStruct((8, RET_WIDTH), F32)],
        scratch_shapes=[pltpu.VMEM((RET_HEADS, RET_DIM, RET_DIM), F32)],
        sem=("arbitrary",), args=[dmix, raw, u, u, u, u, cos, sin, gain, decay, zeta, xi, gc])


def _ret_bwd_kv(dret, u, tabs, rides=None):
    t = u.shape[0]
    tr = min(512, t)
    nt = t // tr
    cos, sin, decay, zeta, xi, gc = tabs
    scale = RET_DIM ** -0.5

    def body(do_ref, q_ref, k_ref, v_ref, cos_ref, sin_ref, dec_ref, zeta_ref, xi_ref, gc_ref, dk_ref, dv_ref, gst):
        @pl.when(pl.program_id(0) == 0)
        def _():
            gst[...] = jnp.zeros_like(gst)

        items = _ret_chunks(tr, rev=True)
        n = range(len(items))
        qbs, kbs, vbs, kzs = _ret_operands(items, q_ref, k_ref, v_ref, cos_ref, sin_ref, zeta_ref)
        dos = [do_ref[sl, hs] for sl, hs, h in items]
        dobs = [do.astype(BF16) for do in dos]
        ss = [_dot(qbs[i], kbs[i], NT) for i in n]
        dss = [_dot(dobs[i], vbs[i], NT) for i in n]
        steps = [_dot(qbs[i], (dos[i] * xi_ref[items[i][2]]).astype(BF16), TN) for i in n]
        afters = [g.astype(BF16) for g in _ret_states(items, gst, steps, gc_ref)]
        dvs = [_dot((ss[i] * dec_ref[items[i][2]]).astype(BF16), dobs[i], TN) + _dot(kzs[i], afters[i]) for i in n]
        dks = [_dot((dss[i] * dec_ref[items[i][2]]).astype(BF16), qbs[i], TN) for i in n]
        dkz = [_dot(vbs[i], afters[i], NT) for i in n]
        for i, (sl, hs, h) in enumerate(items):
            dv_ref[sl, hs] = dvs[i].astype(BF16)
            dkr = (dks[i] + dkz[i] * zeta_ref[h]) * scale
            dk_ref[sl, hs] = (dkr * cos_ref[sl, :] - _rot(dkr * sin_ref[sl, :])).astype(BF16)

    col, tab, head = _ret_specs(tr, True, nt)
    out = pl.BlockSpec((tr, RET_WIDTH), lambda i: (nt - 1 - i, 0))
    return _pallas(
        body, rides, name="ret_bwd_kv", grid=(nt,),
        in_specs=[out, col(0), col(1), col(2), tab, tab, head, head, head, head],
        out_specs=[out, out],
        out_shape=[jax.ShapeDtypeStruct((t, RET_WIDTH), BF16), jax.ShapeDtypeStruct((t, RET_WIDTH), BF16)],
        scratch_shapes=[pltpu.VMEM((RET_HEADS, RET_DIM, RET_DIM), F32)],
        sem=("arbitrary",), args=[dret, u, u, u, cos, sin, decay, zeta, xi, gc])


PAIRS = ATT_WIDTH // LANE
ATT_Q_BLK, ATT_K_BLK, ATT_V_BLK = 0, PAIRS, 2 * PAIRS
STAT_LANES = ATT_DIM // 2


def _att_tiles(t, dil):
    sub = t // dil
    tq = min(512, sub)
    return sub, tq, sub // tq, tq // ATT_BLOCK


def _att_in_specs(tq, qb, ti):
    cur = lambda off: pl.BlockSpec((None, tq, LANE), lambda g, p, i: (g, ti(i), off + p))
    prev = lambda off: pl.BlockSpec((None, ATT_BLOCK, LANE), lambda g, p, i: (g, jnp.maximum(ti(i) * qb - 1, 0), off + p))
    return [cur(ATT_Q_BLK), cur(ATT_K_BLK), prev(ATT_K_BLK), cur(ATT_V_BLK), prev(ATT_V_BLK)]


def _band_mask():
    key = lax.broadcasted_iota(jnp.int32, (2 * ATT_BLOCK, 2 * ATT_BLOCK), 0)
    qry = lax.broadcasted_iota(jnp.int32, (2 * ATT_BLOCK, 2 * ATT_BLOCK), 1) % ATT_BLOCK
    dist = qry + ATT_BLOCK - key
    return (dist >= 0) & (dist <= ATT_BLOCK), key >= ATT_BLOCK


def _head0_lanes():
    return lax.broadcasted_iota(jnp.int32, (ATT_BLOCK, LANE), 1) < ATT_DIM


def _stack_heads(v, head0):
    zero = jnp.zeros((), v.dtype)
    return jnp.concatenate([jnp.where(head0, v, zero), jnp.where(head0, zero, v)], axis=0)


def _unstack_heads(v, head0):
    return jnp.where(head0, v[0:ATT_BLOCK], v[ATT_BLOCK:])


def _att_fwd(ua, dil):
    sub = ua.shape[1]
    _, tq, nq, qb = _att_tiles(sub * dil, dil)

    def body(q_ref, kc_ref, kp_ref, vc_ref, vp_ref, o_ref, l_ref, kx, vx):
        tile = pl.program_id(2)
        kx[0:ATT_BLOCK, :] = kp_ref[...]
        kx[ATT_BLOCK:, :] = kc_ref[...]
        vx[0:ATT_BLOCK, :] = vp_ref[...]
        vx[ATT_BLOCK:, :] = vc_ref[...]
        band, cur_keys = _band_mask()
        head0 = _head0_lanes()
        blocks = range(qb)
        rows = [slice(b * ATT_BLOCK, (b + 1) * ATT_BLOCK) for b in blocks]
        keys = [slice(b * ATT_BLOCK, (b + 2) * ATT_BLOCK) for b in blocks]
        sts = [_dot(kx[keys[b], :], _stack_heads(q_ref[rows[b], :] * jnp.asarray(ATT_DIM ** -0.5, BF16), head0), NT)
               for b in blocks]
        pts, lses = [], []
        for b in blocks:
            mask = band if b > 0 else band & (cur_keys | (tile > 0))
            st = jnp.where(mask, sts[b], -1e30)
            m = jnp.max(st, axis=0, keepdims=True)
            ex = jnp.exp(st - m)
            den = jnp.sum(ex, axis=0, keepdims=True)
            pts.append((ex * (1.0 / den)).astype(BF16))
            lses.append(m + jnp.log(den))
        outs = [_dot(pts[b], vx[keys[b], :], TN) for b in blocks]
        for b in blocks:
            o_ref[rows[b], :] = _unstack_heads(outs[b], head0)
            cols = [jnp.broadcast_to(lses[b][:, e * ATT_BLOCK:(e + 1) * ATT_BLOCK], (ATT_BLOCK, LANE)).T for e in range(2)]
            l_ref[rows[b], :] = jnp.where(head0, cols[0], cols[1])

    out = pl.BlockSpec((None, tq, LANE), lambda g, p, i: (g, i, p))
    return pl.pallas_call(
        body, name=f"att_fwd_d{dil}", grid=(dil, PAIRS, nq),
        in_specs=_att_in_specs(tq, qb, lambda i: i),
        out_specs=[out, out],
        out_shape=[jax.ShapeDtypeStruct((dil, sub, ATT_WIDTH), F32)] * 2,
        scratch_shapes=[pltpu.VMEM((tq + ATT_BLOCK, LANE), BF16)] * 2,
        compiler_params=_params("arbitrary", "arbitrary", "arbitrary"),
    )(ua, ua, ua, ua, ua)


def _regrouped_spec(tm, dil, w):
    return pl.BlockSpec((dil, tm // dil, w), lambda i: (0, i, 0))


def _att_combine(outs, lses, t):
    w = ATT_WIDTH
    tm = min(512, t)
    nb = len(outs)

    def body(*refs):
        o_refs, l_refs = refs[:nb], refs[nb:2 * nb]
        mix_ref, att_ref, lse_ref, buf = refs[2 * nb:]
        ls = [_natural_rows(r, dil, buf) for r, dil in zip(l_refs, DILATIONS)]
        m = functools.reduce(jnp.maximum, ls)
        ws = [jnp.exp(l - m) for l in ls]
        den = functools.reduce(jnp.add, ws)
        att = functools.reduce(jnp.add, [(wt / den) * _natural_rows(r, dil, buf) for wt, r, dil in zip(ws, o_refs, DILATIONS)])
        att_ref[...] = att
        mix_ref[...] = att.astype(BF16)
        lse_ref[...] = m + jnp.log(den)

    tile = pl.BlockSpec((tm, w), lambda i: (i, 0))
    regrouped = [_regrouped_spec(tm, dil, w) for dil in DILATIONS]
    return pl.pallas_call(
        body, name="att_combine", grid=(t // tm,),
        in_specs=regrouped * 2, out_specs=[tile, tile, tile],
        out_shape=[jax.ShapeDtypeStruct((t, w), BF16), jax.ShapeDtypeStruct((t, w), F32), jax.ShapeDtypeStruct((t, w), F32)],
        scratch_shapes=[_chunk_scratch(tm, w)],
        compiler_params=_params("arbitrary"),
    )(*outs, *lses)


def _att_bwd_prep(datt, att, lse):
    t, w = datt.shape
    tm = min(512, t)

    def body(da_ref, at_ref, l_ref, *rest):
        outs, dbuf, sbuf = rest[:-2], rest[-2], rest[-1]
        dav = da_ref[...]
        prod = dav * at_ref[...]
        lane = lax.broadcasted_iota(jnp.int32, (tm, LANE), 1)
        for k in range(w // LANE):
            cols = slice(k * LANE, (k + 1) * LANE)
            dbuf[k] = dav[:, cols]
            delta = jnp.concatenate(
                [jnp.broadcast_to(jnp.sum(prod[:, k * LANE + e * ATT_DIM:k * LANE + (e + 1) * ATT_DIM], axis=-1, keepdims=True),
                                  (tm, ATT_DIM)) for e in range(LANE // ATT_DIM)], axis=1)
            sbuf[k] = jnp.where(lane % ATT_DIM < STAT_LANES, l_ref[:, cols], delta)
        for k, dil in enumerate(DILATIONS):
            _regroup_store(dbuf, outs[2 * k], dil)
            _regroup_store(sbuf, outs[2 * k + 1], dil)

    tile = pl.BlockSpec((tm, w), lambda i: (i, 0))
    res = pl.pallas_call(
        body, name="att_bwd_prep", grid=(t // tm,),
        in_specs=[tile] * 3,
        out_specs=[_regrouped_spec(tm, dil, w) for dil in DILATIONS for _ in range(2)],
        out_shape=[jax.ShapeDtypeStruct((dil, t // dil, w), dt) for dil in DILATIONS for dt in (BF16, F32)],
        scratch_shapes=[_chunk_scratch(tm, w)] * 2,
        compiler_params=_params("arbitrary"),
    )(datt, att, lse)
    return [(res[2 * k], res[2 * k + 1]) for k in range(len(DILATIONS))]


def _att_bwd(ua, da, stat, dil, rides=None):
    sub = ua.shape[1]
    _, tq, nq, qb = _att_tiles(sub * dil, dil)
    scale = ATT_DIM ** -0.5

    def body(q_ref, kc_ref, kp_ref, vc_ref, vp_ref, da_ref, st_ref, dq_ref, dk_ref, dv_ref, kx, vx, ck, cv):
        step = pl.program_id(2)
        tile = nq - 1 - step

        @pl.when(step == 0)
        def _():
            ck[...] = jnp.zeros_like(ck)
            cv[...] = jnp.zeros_like(cv)

        kx[0:ATT_BLOCK, :] = kp_ref[...]
        kx[ATT_BLOCK:, :] = kc_ref[...]
        vx[0:ATT_BLOCK, :] = vp_ref[...]
        vx[ATT_BLOCK:, :] = vc_ref[...]
        band, cur_keys = _band_mask()
        head0 = _head0_lanes()
        blocks = range(qb)
        rows = [slice(b * ATT_BLOCK, (b + 1) * ATT_BLOCK) for b in blocks]
        keys = [slice(b * ATT_BLOCK, (b + 2) * ATT_BLOCK) for b in blocks]
        qqs = [_stack_heads(q_ref[rows[b], :] * jnp.asarray(scale, BF16), head0) for b in blocks]
        dds = [_stack_heads(da_ref[rows[b], :], head0) for b in blocks]
        sts = [_dot(kx[keys[b], :], qqs[b], NT) for b in blocks]
        dpts = [_dot(vx[keys[b], :], dds[b], NT) for b in blocks]
        pts, dsts = [], []
        for b in blocks:
            mask = band if b > 0 else band & (cur_keys | (tile > 0))
            stat = st_ref[rows[b], :].T
            row = lambda k: jnp.concatenate([stat[e * ATT_DIM + k:e * ATT_DIM + k + 1, :] for e in range(2)], axis=1)
            pt = jnp.where(mask, jnp.exp(sts[b] - row(0)), 0.0)
            dsts.append((pt * (dpts[b] - row(STAT_LANES))).astype(BF16))
            pts.append(pt.astype(BF16))
        dqs = [_dot(dsts[b], kx[keys[b], :], TN) for b in blocks]
        dkbs = [_dot(dsts[b], qqs[b]) for b in blocks]
        dvbs = [_dot(pts[b], dds[b]) for b in blocks]
        for b in blocks:
            dq_ref[rows[b], :] = _unstack_heads(dqs[b], head0) * scale
        for b in blocks[1:]:
            dk_ref[rows[b - 1], :] = dkbs[b - 1][ATT_BLOCK:] + dkbs[b][0:ATT_BLOCK]
            dv_ref[rows[b - 1], :] = dvbs[b - 1][ATT_BLOCK:] + dvbs[b][0:ATT_BLOCK]
        before_k, before_v = dkbs[0][0:ATT_BLOCK], dvbs[0][0:ATT_BLOCK]
        open_k, open_v = dkbs[-1][ATT_BLOCK:], dvbs[-1][ATT_BLOCK:]
        last = slice(tq - ATT_BLOCK, tq)
        dk_ref[last, :] = open_k + ck[...]
        dv_ref[last, :] = open_v + cv[...]
        ck[...] = before_k
        cv[...] = before_v

    ti = lambda i: nq - 1 - i
    out = pl.BlockSpec((None, tq, LANE), lambda g, p, i: (g, ti(i), p))
    shape = jax.ShapeDtypeStruct((dil, sub, ATT_WIDTH), F32)
    return _pallas(
        body, rides, name=f"att_bwd_d{dil}", grid=(dil, PAIRS, nq),
        in_specs=_att_in_specs(tq, qb, ti) + [out, out],
        out_specs=[out, out, out], out_shape=[shape] * 3,
        scratch_shapes=[pltpu.VMEM((tq + ATT_BLOCK, LANE), BF16)] * 2 + [pltpu.VMEM((ATT_BLOCK, LANE), F32)] * 2,
        sem=("arbitrary", "arbitrary", "arbitrary"), args=[ua, ua, ua, ua, ua, da, stat])


def _att_bwd_sum(parts, t):
    w = ATT_WIDTH
    tm = min(512, t)
    nk = len(parts[0])

    def body(*refs):
        ins, outs, buf = refs[:-nk - 1], refs[-nk - 1:-1], refs[-1]
        for k in range(nk):
            acc = None
            for b, dil in enumerate(DILATIONS):
                rows = _natural_rows(ins[b * nk + k], dil, buf)
                acc = rows if acc is None else acc + rows
            outs[k][...] = acc.astype(BF16)

    tile = pl.BlockSpec((tm, w), lambda i: (i, 0))
    return pl.pallas_call(
        body, name="att_bwd_sum", grid=(t // tm,),
        in_specs=[_regrouped_spec(tm, dil, w) for dil in DILATIONS for _ in range(nk)], out_specs=[tile] * nk,
        out_shape=[jax.ShapeDtypeStruct((t, w), BF16)] * nk,
        scratch_shapes=[_chunk_scratch(tm, w)],
        compiler_params=_params("arbitrary"),
    )(*[a for p in parts for a in p])


def _loss_bwd(h, gain, target):
    t, d = h.shape
    tm = min(512, t)

    def body(h_ref, gain_ref, tg_ref, loss_ref, dh_ref, dgain_ref):
        @pl.when(pl.program_id(0) == 0)
        def _():
            loss_ref[...] = jnp.zeros_like(loss_ref)
            dgain_ref[...] = jnp.zeros_like(dgain_ref)

        hv = h_ref[...]
        r = lax.rsqrt(jnp.mean(hv * hv, axis=-1, keepdims=True) + NORM_EPS)
        xh = hv * r
        err = xh * gain_ref[...] - tg_ref[...]
        sq = _rows8(jnp.square(err))
        loss_ref[...] += 0.5 * functools.reduce(jnp.add, [sq[:, k * LANE:(k + 1) * LANE] for k in range(d // LANE)]) / d
        dy = err / d
        dgain_ref[...] += _rows8(dy * xh)
        dxh = dy * gain_ref[...]
        dh_ref[...] = r * (dxh - xh * jnp.mean(dxh * xh, axis=-1, keepdims=True))

    tile = pl.BlockSpec((tm, d), lambda i: (i, 0))
    return pl.pallas_call(
        body, name="loss_bwd", grid=(t // tm,),
        in_specs=[tile, pl.BlockSpec((1, d), lambda i: (0, 0)), tile],
        out_specs=[pl.BlockSpec((8, LANE), lambda i: (0, 0)), tile, pl.BlockSpec((8, d), lambda i: (0, 0))],
        out_shape=[jax.ShapeDtypeStruct((8, LANE), F32), jax.ShapeDtypeStruct((t, d), F32), jax.ShapeDtypeStruct((8, d), F32)],
        compiler_params=_params("arbitrary"),
    )(h, gain, target)


class _Reduction:
    def __init__(self, place, names, grads):
        self.place, self.names, self.grads = place, names, grads

    def pair(self):
        return _pair_ride(self.grads)

    def chips(self, got):
        self.got = got
        return _chip_ride([_pair_sum(self.place, g, r, f"pair_sum_{n}") for g, r, n in zip(self.grads, got, self.names)])

    def halves(self, others):
        return [_chip_sum(self.place, g, r, o, f"chip_sum_{n}")
                for g, r, o, n in zip(self.grads, self.got, others, self.names)]


def _step(x, target, gains, w, place=None):
    t = x.shape[0]
    ex = place is not None
    g_ffn1, g_mix, g_ret, g_ffn2, g_fin = gains
    w = list(w)
    tabs = _retention_tables(t)
    red = lambda names, grads: _Reduction(place, names, grads) if ex else None
    ride = lambda r: [r] if ex else None

    if ex:
        w[0:3] = _run(_gather_ride(w[0:3]), "gather_ffn1_weights")
    (h1, xn1, ga1, ua1, act1), rest = _ffn_fwd(x, g_ffn1, *w[0:3], "ffn1_fwd", ride(_gather_ride(w[3:])) if ex else None)
    if ex:
        w[3:] = rest[0]
    wg1, wu1, wd1, win, wo, wg2, wu2, wd2 = w
    wo2 = wo.reshape(wo.shape[0] * wo.shape[1], wo.shape[2])
    xnm, u, *uas = _inproj_fwd(h1, g_mix, win)
    raw, mix_r = _ret_fwd(u, g_ret, tabs)
    branches = [_att_fwd(ua, dil) for ua, dil in zip(uas, DILATIONS)]
    mix_a, att, lse = _att_combine([b[0] for b in branches], [b[1] for b in branches], t)
    h2 = _outproj_fwd(h1, mix_r, mix_a, wo2)
    (h3, xn2, ga2, ua2, act2), _ = _ffn_fwd(h2, g_ffn2, wg2, wu2, wd2, "ffn2_fwd")
    loss_p, dh3, dg_fin = _loss_bwd(h3, g_fin, target)

    (dwd2,), _ = _ffn_wgrad_down(act2, dh3, "ffn2_wgrad_down")
    r_d2 = red(["ffn2_w_down"], [dwd2])
    (dh2, dga2, dua2, dg_ffn2), e = _ffn_bwd_data(dh3, h2, g_ffn2, ga2, ua2, wg2, wu2, wd2, "ffn2_bwd",
                                                  ex and [r_d2.pair()])
    (dwg2, dwu2), e = _ffn_wgrad_gu(xn2, dga2, dua2, "ffn2_wgrad_gu", ex and [r_d2.chips(e[0])])
    r_gu2 = red(["ffn2_w_gate", "ffn2_w_up"], [dwg2, dwu2])
    (dmix_r, dmix_a), e = _outproj_bwd(dh2, wo2, ex and [r_gu2.pair(), _finish_ride(r_d2.halves(e[0]))])
    if ex:
        got_gu2, (dwd2,) = e
    hw = RET_WIDTH // (wo.shape[1])
    dwo = jnp.concatenate([_tn_matmul(mix_r, dh2, dh2.shape[1], "wo_grad_r").reshape(hw, wo.shape[1], wo.shape[2]),
                           _tn_matmul(mix_a, dh2, dh2.shape[1], "wo_grad_a").reshape(hw, wo.shape[1], wo.shape[2])])
    r_wo = red(["w_out"], [dwo])
    (dq_r, dgt_r, dret, dg_ret), e = _ret_bwd_q(dmix_r, raw, u, g_ret, tabs, ex and [r_gu2.chips(got_gu2)])
    (dk_r, dv_r), e = _ret_bwd_kv(dret, u, tabs, ex and [r_wo.pair(), _finish_ride(r_gu2.halves(e[0]))])
    if ex:
        got_wo, (dwg2, dwu2) = e
    prep = _att_bwd_prep(dmix_a, att, lse)
    p1, e = _att_bwd(uas[0], *prep[0], DILATIONS[0], ex and [r_wo.chips(got_wo)])
    p4, e = _att_bwd(uas[1], *prep[1], DILATIONS[1], ex and [_finish_ride(r_wo.halves(e[0]))])
    if ex:
        (dwo,), = e
    p16, _ = _att_bwd(uas[2], *prep[2], DILATIONS[2])
    dq_a, dk_a, dv_a = _att_bwd_sum([p1, p4, p16], t)
    dh1, du, dg_mix = _inproj_bwd([dq_r, dk_r, dv_r, dgt_r, dq_a, dk_a, dv_a], h1, g_mix, dh2, win)
    dwin = _tn_matmul(xnm, du, win.shape[2], "win_grad")
    r_in = red(["w_in"], [dwin])
    (dwd1,), e = _ffn_wgrad_down(act1, dh1, "ffn1_wgrad_down", ex and [r_in.pair()])
    r_d1 = red(["ffn1_w_down"], [dwd1])
    (dx, dga1, dua1, dg_ffn1), e = _ffn_bwd_data(dh1, x, g_ffn1, ga1, ua1, wg1, wu1, wd1, "ffn1_bwd",
                                                  ex and [r_in.chips(e[0]), r_d1.pair()])
    (dwg1, dwu1), e = _ffn_wgrad_gu(xn1, dga1, dua1, "ffn1_wgrad_gu",
                                    ex and [_finish_ride(r_in.halves(e[0])), r_d1.chips(e[1])])
    gain_parts = [dg_ffn1, dg_mix, dg_ret, dg_ffn2, dg_fin]
    if not ex:
        return loss_p, dx, [dwg1, dwu1, dwd1, dwin, dwo, dwg2, dwu2, dwd2], gain_parts
    (dwin,), oth_d1 = e
    r_gu1 = red(["ffn1_w_gate", "ffn1_w_up"], [dwg1, dwu1])
    got = _run(r_gu1.pair(), "pair_exchange_ffn1_gate_up")
    oth = _run(r_gu1.chips(got), "chip_exchange_ffn1_gate_up")
    last = r_gu1.halves(oth) + r_d1.halves(oth_d1)
    dwg1, dwu1, dwd1, gall = _run(_finish_ride(last, _pack_gains(gain_parts, x.shape[1])), "finish_exchange_ffn1")
    return loss_p, dx, [dwg1, dwu1, dwd1, dwin, dwo, dwg2, dwu2, dwd2], gall


N_DEV = 8
GAIN_ROWS = 8


def _place():
    x, y, c = lax.axis_index("x"), lax.axis_index("y"), lax.axis_index("c")
    chips = [(1 - x, y), (x, 1 - y), (1 - x, 1 - y)]
    return x, y, c, chips


def _hbm_specs(n):
    return [pl.BlockSpec(memory_space=pl.ANY)] * n


def _place_shard(place, w, name):
    r, cols = w.shape
    tr = r // 4

    def body(place_ref, w_ref, o_ref):
        o_ref[...] = w_ref[...].astype(BF16)

    return pl.pallas_call(
        body, name=name,
        grid_spec=pltpu.PrefetchScalarGridSpec(
            num_scalar_prefetch=1, grid=(r // tr,),
            in_specs=[pl.BlockSpec((tr, cols), lambda i, pr: (i, 0))],
            out_specs=pl.BlockSpec((None, tr, cols), lambda i, pr: (pr[0], i, 0))),
        out_shape=jax.ShapeDtypeStruct((N_SHARD, r, cols), BF16),
        compiler_params=_params("arbitrary"),
    )(place, w)


def _gather_ride(bufs):
    na = len(bufs)

    def legs(outs, sems):
        send_sem, recv_sem, fsend_sem, frecv_sem = sems
        x, y, c, chips = _place()

        def half(a, idx, which):
            hr = outs[a].shape[1] // 2
            return outs[a].at[idx, pl.ds(which * hr, hr)]

        def ici(a, j, idx):
            px, py = chips[j]
            return pltpu.make_async_remote_copy(
                src_ref=half(a, idx, c), dst_ref=half(a, idx, c),
                send_sem=send_sem.at[a, j], recv_sem=recv_sem.at[a, j], device_id=(px, py, c), device_id_type=MESH)

        def d2d(a, j, idx, which):
            return pltpu.make_async_remote_copy(
                src_ref=half(a, idx, which), dst_ref=half(a, idx, which),
                send_sem=fsend_sem.at[a, j], recv_sem=frecv_sem.at[a, j], device_id=(x, y, 1 - c), device_id_type=MESH)

        return 2 * x + y, c, chips, ici, d2d

    def start(ins, outs, sems):
        me, _, _, ici, _ = legs(outs, sems)
        for a in range(na):
            for j in range(3):
                ici(a, j, me).start()

    def finish(ins, outs, sems):
        me, c, chips, ici, d2d = legs(outs, sems)
        passed = []
        for a in range(na):
            for j, (px, py) in enumerate(chips):
                ici(a, j, 2 * px + py).wait_recv()
                cp = d2d(a, j, 2 * px + py, c)
                cp.start()
                passed.append(cp)
        for a in range(na):
            for j, (px, py) in enumerate(chips):
                d2d(a, j, 2 * px + py, 1 - c).wait_recv()
        for a in range(na):
            for j in range(3):
                ici(a, j, me).wait_send()
        for cp in passed:
            cp.wait_send()

    return _Ride(bufs, [jax.ShapeDtypeStruct(b.shape, b.dtype) for b in bufs], [pltpu.SemaphoreType.DMA((na, 3))] * 4,
                 start, finish, {a: a for a in range(na)})


def _pair_ride(grads):
    na = len(grads)

    def copies(ins, outs, sems):
        send_sem, recv_sem = sems
        x, y, c, _ = _place()
        res = []
        for a in range(na):
            hr = ins[a].shape[1] // 2
            res.append(pltpu.make_async_remote_copy(
                src_ref=ins[a].at[:, pl.ds((1 - c) * hr, hr)], dst_ref=outs[a],
                send_sem=send_sem.at[a], recv_sem=recv_sem.at[a], device_id=(x, y, 1 - c), device_id_type=MESH))
        return res

    def start(ins, outs, sems):
        for cp in copies(ins, outs, sems):
            cp.start()

    def finish(ins, outs, sems):
        for cp in copies(ins, outs, sems):
            cp.wait()

    return _Ride(grads, [jax.ShapeDtypeStruct((g.shape[0], g.shape[1] // 2, g.shape[2]), g.dtype) for g in grads],
                 [pltpu.SemaphoreType.DMA((na,))] * 2, start, finish)


def _chip_ride(sums):
    na = len(sums)

    def copies(ins, outs, sems):
        send_sem, recv_sem = sems
        x, y, c, chips = _place()
        res = []
        for a in range(na):
            for j, (px, py) in enumerate(chips):
                res.append(pltpu.make_async_remote_copy(
                    src_ref=ins[a].at[2 * px + py], dst_ref=outs[a].at[j],
                    send_sem=send_sem.at[a, j], recv_sem=recv_sem.at[a, j], device_id=(px, py, c), device_id_type=MESH))
        return res

    def start(ins, outs, sems):
        for cp in copies(ins, outs, sems):
            cp.start()

    def finish(ins, outs, sems):
        for cp in copies(ins, outs, sems):
            cp.wait()

    return _Ride(sums, [jax.ShapeDtypeStruct((3,) + s.shape[1:], s.dtype) for s in sums],
                 [pltpu.SemaphoreType.DMA((na, 3))] * 2, start, finish)


def _finish_ride(grads, gpack=None):
    na = len(grads)

    def halves(outs, sems, which):
        x, y, c, _ = _place()
        res = []
        for a in range(na):
            hr = outs[a].shape[0] // 2
            rows = outs[a].at[pl.ds((c if which == "mine" else 1 - c) * hr, hr)]
            res.append(pltpu.make_async_remote_copy(
                src_ref=rows, dst_ref=rows, send_sem=sems[0].at[a], recv_sem=sems[1].at[a],
                device_id=(x, y, 1 - c), device_id_type=MESH))
        return res

    def gains(ins, outs, sems):
        x, y, c, _ = _place()
        dev = 4 * x + 2 * y + c
        g_in, g_out = ins[na], outs[na]
        own = pltpu.make_async_copy(g_in, g_out.at[dev], sems[2])
        sends, lands = [], []
        for k in range(N_DEV - 1):
            bx, by, bc = (k + 1) // 4, ((k + 1) // 2) % 2, (k + 1) % 2
            peer = (jnp.bitwise_xor(x, bx), jnp.bitwise_xor(y, by), jnp.bitwise_xor(c, bc))
            sends.append(pltpu.make_async_remote_copy(
                src_ref=g_in, dst_ref=g_out.at[dev], send_sem=sems[3].at[k], recv_sem=sems[4].at[k],
                device_id=peer, device_id_type=MESH))
            slot = g_out.at[jnp.bitwise_xor(dev, k + 1)]
            lands.append(pltpu.make_async_remote_copy(
                src_ref=slot, dst_ref=slot, send_sem=sems[3].at[k], recv_sem=sems[4].at[k],
                device_id=peer, device_id_type=MESH))
        return own, sends, lands

    def start(ins, outs, sems):
        for cp in halves(outs, sems, "mine"):
            cp.start()
        if gpack is not None:
            own, sends, _ = gains(ins, outs, sems)
            own.start()
            for cp in sends:
                cp.start()

    def finish(ins, outs, sems):
        for cp in halves(outs, sems, "sibling's"):
            cp.wait_recv()
        if gpack is not None:
            own, sends, lands = gains(ins, outs, sems)
            for cp in lands:
                cp.wait_recv()
            for cp in sends:
                cp.wait_send()
            own.wait()
        for cp in halves(outs, sems, "mine"):
            cp.wait_send()

    shapes = [jax.ShapeDtypeStruct(g.shape, g.dtype) for g in grads]
    sems = [pltpu.SemaphoreType.DMA((na,))] * 2
    if gpack is None:
        return _Ride(grads, shapes, sems, start, finish, {a: a for a in range(na)})
    return _Ride(list(grads) + [gpack], shapes + [jax.ShapeDtypeStruct((N_DEV,) + gpack.shape, gpack.dtype)],
                 sems + [pltpu.SemaphoreType.DMA, pltpu.SemaphoreType.DMA((N_DEV - 1,)), pltpu.SemaphoreType.DMA((N_DEV - 1,))],
                 start, finish, {a: a for a in range(na)})


def _pair_sum(place, grad, got, name):
    ns, r, cols = grad.shape
    hr = r // 2

    def body(place_ref, g_ref, r_ref, o_ref):
        o_ref[...] = (g_ref[...] + r_ref[...]).astype(BF16)

    return pl.pallas_call(
        body, name=name,
        grid_spec=pltpu.PrefetchScalarGridSpec(
            num_scalar_prefetch=1, grid=(ns,),
            in_specs=[pl.BlockSpec((None, hr, cols), lambda s, pr: (s, pr[1], 0)),
                      pl.BlockSpec((None, hr, cols), lambda s, pr: (s, 0, 0))],
            out_specs=pl.BlockSpec((None, hr, cols), lambda s, pr: (s, 0, 0))),
        out_shape=jax.ShapeDtypeStruct((ns, hr, cols), BF16),
        compiler_params=_params("arbitrary"),
    )(place, grad, got)


def _chip_sum(place, grad, got, others, name):
    ns, r, cols = grad.shape
    hr = r // 2
    nb = 2
    tr = hr // nb

    def body(place_ref, g_ref, r_ref, o3_ref, o_ref):
        acc = g_ref[...] + r_ref[...]
        for j in range(3):
            acc = acc + o3_ref[j].astype(F32)
        o_ref[...] = acc

    return pl.pallas_call(
        body, name=name,
        grid_spec=pltpu.PrefetchScalarGridSpec(
            num_scalar_prefetch=1, grid=(nb,),
            in_specs=[pl.BlockSpec((None, tr, cols), lambda i, pr: (pr[0], pr[1] * nb + i, 0)),
                      pl.BlockSpec((None, tr, cols), lambda i, pr: (pr[0], i, 0)),
                      pl.BlockSpec((3, tr, cols), lambda i, pr: (0, i, 0))],
            out_specs=pl.BlockSpec((tr, cols), lambda i, pr: (pr[1] * nb + i, 0))),
        out_shape=jax.ShapeDtypeStruct((r, cols), F32),
        compiler_params=_params("arbitrary"),
    )(place, grad, got, others)


def _pack_gains(parts, d):
    def body(*refs):
        ins, o_ref = refs[:-1], refs[-1]
        o_ref[...] = jnp.zeros_like(o_ref)
        for k, r in enumerate(ins):
            o_ref[k:k + 1, 0:r.shape[1]] = jnp.sum(r[...], axis=0, keepdims=True)

    return pl.pallas_call(
        body, name="pack_gains", out_shape=jax.ShapeDtypeStruct((GAIN_ROWS, d), F32),
    )(*parts)


def _adamw_math(w, g, m, v):
    m = ADAM_B1 * m + (1.0 - ADAM_B1) * g
    v = ADAM_B2 * v + (1.0 - ADAM_B2) * jnp.square(g)
    m_hat = m / (1.0 - ADAM_B1 ** ADAM_STEP)
    v_hat = v / (1.0 - ADAM_B2 ** ADAM_STEP)
    return -ADAM_LR * (m_hat / (jnp.sqrt(v_hat) + ADAM_EPS) + ADAM_WD * w), m, v


def _adamw(w, g, m, v, name):
    r, cols = w.shape
    tr = r // 4 if (r // 4) % 8 == 0 else r

    def body(w_ref, g_ref, m_ref, v_ref, d_ref, nm_ref, nv_ref):
        d_ref[...], nm_ref[...], nv_ref[...] = _adamw_math(w_ref[...], g_ref[...], m_ref[...], v_ref[...])

    tile = pl.BlockSpec((tr, cols), lambda i: (i, 0))
    return pl.pallas_call(
        body, name=name, grid=(r // tr,), in_specs=[tile] * 4, out_specs=[tile] * 3,
        out_shape=[jax.ShapeDtypeStruct((r, cols), F32)] * 3,
        compiler_params=_params("arbitrary"),
    )(w, g, m, v)


def _adamw_gain(gall, row, w, m, v, name):
    n = w.shape[1]

    def body(ga_ref, w_ref, m_ref, v_ref, g_ref, d_ref, nm_ref, nv_ref):
        g = ga_ref[0, row:row + 1, 0:n]
        for k in range(1, N_DEV):
            g = g + ga_ref[k, row:row + 1, 0:n]
        g_ref[...] = g
        d_ref[...], nm_ref[...], nv_ref[...] = _adamw_math(w_ref[...], g, m_ref[...], v_ref[...])

    return pl.pallas_call(
        body, name=name, out_shape=[jax.ShapeDtypeStruct((1, n), F32)] * 4,
    )(gall, w, m, v)


def kernel(x, norm_ffn1, ffn1_w_gate, ffn1_w_up, ffn1_w_down, norm_mix, w_in, ret_norm_gain, w_out, norm_ffn2, ffn2_w_gate, ffn2_w_up, ffn2_w_down, norm_final, loss_target, m_norm_ffn1, m_ffn1_w_gate, m_ffn1_w_up, m_ffn1_w_down, m_norm_mix, m_w_in, m_ret_norm_gain, m_w_out, m_norm_ffn2, m_ffn2_w_gate, m_ffn2_w_up, m_ffn2_w_down, m_norm_final, v_norm_ffn1, v_ffn1_w_gate, v_ffn1_w_up, v_ffn1_w_down, v_norm_mix, v_w_in, v_ret_norm_gain, v_w_out, v_norm_ffn2, v_ffn2_w_gate, v_ffn2_w_up, v_ffn2_w_down, v_norm_final):
    d = x.shape[-1]
    mats = [ffn1_w_gate, ffn1_w_up, ffn1_w_down, w_in, w_out, ffn2_w_gate, ffn2_w_up, ffn2_w_down]
    mats_m = [m_ffn1_w_gate, m_ffn1_w_up, m_ffn1_w_down, m_w_in, m_w_out, m_ffn2_w_gate, m_ffn2_w_up, m_ffn2_w_down]
    mats_v = [v_ffn1_w_gate, v_ffn1_w_up, v_ffn1_w_down, v_w_in, v_w_out, v_ffn2_w_gate, v_ffn2_w_up, v_ffn2_w_down]
    mat_names = ["ffn1_w_gate", "ffn1_w_up", "ffn1_w_down", "w_in", "w_out", "ffn2_w_gate", "ffn2_w_up", "ffn2_w_down"]
    gains = [norm_ffn1, norm_mix, ret_norm_gain, norm_ffn2, norm_final.reshape(1, d)]
    gains_m = [m_norm_ffn1, m_norm_mix, m_ret_norm_gain, m_norm_ffn2, m_norm_final.reshape(1, d)]
    gains_v = [v_norm_ffn1, v_norm_mix, v_ret_norm_gain, v_norm_ffn2, v_norm_final.reshape(1, d)]
    gain_names = ["norm_ffn1", "norm_mix", "ret_norm_gain", "norm_ffn2", "norm_final"]

    turned = lambda n: n.endswith(("w_gate", "w_up"))
    local = lambda a, n: jnp.swapaxes(a, 1, 2)[0] if turned(n) else a[0]
    back = lambda a, n: jnp.swapaxes(a[None], 1, 2) if turned(n) else a[None]
    shards = [local(w, n) for w, n in zip(mats, mat_names)]
    place = jnp.stack([2 * lax.axis_index("x") + lax.axis_index("y"), lax.axis_index("c")]).astype(jnp.int32)
    placed = [_place_shard(place, s, f"place_{n}") for s, n in zip(shards, mat_names)]
    loss_p, dx, shard_grads, gall = _step(x[0], loss_target[0], gains, placed, place)

    out_g, out_d, out_m, out_v = {}, {}, {}, {}
    for n, w, g, m, v in zip(mat_names, shards, shard_grads, mats_m, mats_v):
        dl, nm, nv = _adamw(w, g, local(m, n), local(v, n), f"adamw_{n}")
        out_g[n], out_d[n], out_m[n], out_v[n] = [back(a, n) for a in (g, dl, nm, nv)]
    for row, (n, w, m, v) in enumerate(zip(gain_names, gains, gains_m, gains_v)):
        res = _adamw_gain(gall, row, w, m, v, f"adamw_{n}")
        shape = (d,) if n == "norm_final" else w.shape
        out_g[n], out_d[n], out_m[n], out_v[n] = [r.reshape(shape) for r in res]

    loss = lax.psum(jnp.sum(loss_p), ("x", "y", "c"))
    order = ["norm_ffn1", "ffn1_w_gate", "ffn1_w_up", "ffn1_w_down", "norm_mix", "w_in", "ret_norm_gain", "w_out",
             "norm_ffn2", "ffn2_w_gate", "ffn2_w_up", "ffn2_w_down", "norm_final"]
    return (loss, dx[None], *[out_g[n] for n in order], *[out_d[n] for n in order],
            *[out_m[n] for n in order], *[out_v[n] for n in order])
```

```python
import functools
import math

import jax
import jax.numpy as jnp
from jax import lax
from jax.experimental import pallas as pl
from jax.experimental.pallas import tpu as pltpu

F32 = jnp.float32
BF16 = jnp.bfloat16
MESH = pl.DeviceIdType.MESH

NORM_EPS = 1e-6
GN_EPS = 1e-6
ROPE_BASE = 10000.0
RET_HEADS = 4
RET_DIM = 128
RET_WIDTH = 512
RET_CHUNK = 128
ATT_HEADS = 8
ATT_DIM = 64
ATT_WIDTH = 512
ATT_BLOCK = 128
DILATIONS = (1, 4, 16)
IN_COLS = 4 * RET_WIDTH + 3 * ATT_WIDTH
LANE = 128
N_SHARD = 4
ADAM_LR, ADAM_B1, ADAM_B2, ADAM_EPS, ADAM_WD, ADAM_STEP = 0.001, 0.9, 0.999, 1e-08, 0.01, 10

V7X_VMEM_BYTES = 64 * 1024 * 1024
VMEM_LIMIT = V7X_VMEM_BYTES - 8 * 1024 * 1024

NT = (((1,), (1,)), ((), ()))
TN = (((0,), (0,)), ((), ()))


def _params(*sem):
    return pltpu.CompilerParams(dimension_semantics=sem, vmem_limit_bytes=VMEM_LIMIT)


def _dot(a, b, dims=None):
    if dims is None:
        return jnp.dot(a, b, preferred_element_type=F32)
    return lax.dot_general(a, b, dims, preferred_element_type=F32)


def _sigmoid(x):
    return 1.0 / (1.0 + jnp.exp(-x))


def _load_weights(pairs, sems):
    copies = [pltpu.make_async_copy(src, dst, sems.at[k]) for k, (src, dst) in enumerate(pairs)]
    for cp in copies:
        cp.start()
    for cp in copies:
        cp.wait()


def _rows8(v):
    r, c = v.shape
    return v.reshape(r // 8, 8, c).sum(axis=0)


class _Ride:
    def __init__(self, inputs, out_shapes, sems, start, finish, aliases=None):
        self.inputs, self.out_shapes, self.sems = list(inputs), list(out_shapes), list(sems)
        self.start, self.finish, self.aliases = start, finish, dict(aliases or {})


def _pallas(body, rides, *, name, in_specs, out_specs, out_shape, args, grid=(), scratch_shapes=(), sem=()):
    rides = [r for r in (rides or []) if r is not None]
    n_in, n_out, n_scr = len(args), len(out_shape), len(scratch_shapes)
    hbm = pl.BlockSpec(memory_space=pl.ANY)
    r_in = [a for r in rides for a in r.inputs]
    r_out = [s for r in rides for s in r.out_shapes]
    r_sem = [s for r in rides for s in r.sems]
    aliases, spans, ki, ko, ks = {}, [], 0, 0, 0
    for r in rides:
        aliases.update({n_in + ki + i: n_out + ko + o for i, o in r.aliases.items()})
        spans.append((ki, ko, ks))
        ki, ko, ks = ki + len(r.inputs), ko + len(r.out_shapes), ks + len(r.sems)

    def wrapped(*refs):
        ins, rin = refs[:n_in], refs[n_in:n_in + len(r_in)]
        o0 = n_in + len(r_in)
        outs, rout = refs[o0:o0 + n_out], refs[o0 + n_out:o0 + n_out + len(r_out)]
        s0 = o0 + n_out + len(r_out)
        scr, rsem = refs[s0:s0 + n_scr], refs[s0 + n_scr:]
        part = lambda r, k: (rin[spans[k][0]:spans[k][0] + len(r.inputs)], rout[spans[k][1]:spans[k][1] + len(r.out_shapes)],
                             rsem[spans[k][2]:spans[k][2] + len(r.sems)])
        first = functools.reduce(jnp.logical_and, [pl.program_id(k) == 0 for k in range(len(grid))], True)
        last = functools.reduce(jnp.logical_and, [pl.program_id(k) == grid[k] - 1 for k in range(len(grid))], True)
        if rides:
            @pl.when(first)
            def _():
                for k, r in enumerate(rides):
                    r.start(*part(r, k))

        body(*ins, *outs, *scr)
        if rides:
            @pl.when(last)
            def _():
                for k, r in enumerate(rides):
                    r.finish(*part(r, k))

    res = pl.pallas_call(
        wrapped, name=name, grid=grid,
        in_specs=list(in_specs) + [hbm] * len(r_in), out_specs=list(out_specs) + [hbm] * len(r_out),
        out_shape=list(out_shape) + r_out, input_output_aliases=aliases,
        scratch_shapes=list(scratch_shapes) + r_sem,
        compiler_params=pltpu.CompilerParams(dimension_semantics=sem, vmem_limit_bytes=VMEM_LIMIT) if grid else None,
    )(*args, *r_in)
    extras = [list(res[n_out + ko:n_out + ko + len(r.out_shapes)]) for r, (_, ko, _) in zip(rides, spans)]
    return list(res[:n_out]), extras


def _run(ride, name):
    def body(*refs):
        n_in, n_out = len(ride.inputs), len(ride.out_shapes)
        parts = refs[:n_in], refs[n_in:n_in + n_out], refs[n_in + n_out:]
        ride.start(*parts)
        ride.finish(*parts)

    hbm = pl.BlockSpec(memory_space=pl.ANY)
    return list(pl.pallas_call(
        body, name=name, in_specs=[hbm] * len(ride.inputs), out_specs=[hbm] * len(ride.out_shapes),
        out_shape=ride.out_shapes, input_output_aliases=ride.aliases, scratch_shapes=ride.sems,
    )(*ride.inputs))


def _loss_head(hv, gain_ref, tg_ref, loss_ref, dgain_ref):
    d = hv.shape[1]
    r = lax.rsqrt(jnp.mean(hv * hv, axis=-1, keepdims=True) + NORM_EPS)
    xh = hv * r
    err = xh * gain_ref[...] - tg_ref[...]
    sq = _rows8(jnp.square(err))
    loss_ref[...] += 0.5 * functools.reduce(jnp.add, [sq[:, k * LANE:(k + 1) * LANE] for k in range(d // LANE)]) / d
    dy = err / d
    dgain_ref[...] += _rows8(dy * xh)
    dxh = dy * gain_ref[...]
    return r * (dxh - xh * jnp.mean(dxh * xh, axis=-1, keepdims=True))


def _ffn_fwd(x, gain, wg, wu, wd, name, rides=None, head=None):
    t, d = x.shape
    ns, fs, _ = wg.shape
    tm = min(256, t)
    nh = 0 if head is None else 2

    def body(*refs):
        x_ref, gain_ref = refs[:2]
        wg_hbm, wu_hbm, wd_hbm, h_ref, xn_ref, g_ref, u_ref, a_ref = refs[2 + nh:10 + nh]
        wg_v, wu_v, wd_v, sems = refs[-4:]

        @pl.when(pl.program_id(0) == 0)
        def _():
            _load_weights([(wg_hbm, wg_v), (wu_hbm, wu_v), (wd_hbm, wd_v)], sems)
            if head is not None:
                refs[10 + nh][...] = jnp.zeros_like(refs[10 + nh])
                refs[11 + nh][...] = jnp.zeros_like(refs[11 + nh])

        xv = x_ref[...]
        r = lax.rsqrt(jnp.mean(xv * xv, axis=-1, keepdims=True) + NORM_EPS)
        xn = (xv * r * gain_ref[...]).astype(BF16)
        xn_ref[...] = xn
        acc = jnp.zeros((tm, d), F32)
        for j in range(ns):
            g = _dot(xn, wg_v[j], NT)
            u = _dot(xn, wu_v[j], NT)
            g_ref[j] = g.astype(BF16)
            u_ref[j] = u.astype(BF16)
            a = (g * _sigmoid(g) * u).astype(BF16)
            a_ref[j] = a
            acc = acc + _dot(a, wd_v[j])
        hv = xv + 0.5 * acc
        h_ref[...] = hv if head is None else _loss_head(hv, refs[2], refs[3], refs[10 + nh], refs[11 + nh])

    hbm = pl.BlockSpec(memory_space=pl.ANY)
    hid = pl.BlockSpec((ns, tm, fs), lambda i: (0, i, 0))
    tile = pl.BlockSpec((tm, d), lambda i: (i, 0))
    row = pl.BlockSpec((1, d), lambda i: (0, 0))
    sums = [] if head is None else [(pl.BlockSpec((8, LANE), lambda i: (0, 0)), jax.ShapeDtypeStruct((8, LANE), F32)),
                                    (pl.BlockSpec((8, d), lambda i: (0, 0)), jax.ShapeDtypeStruct((8, d), F32))]
    return _pallas(
        body, rides, name=name, grid=(t // tm,),
        in_specs=[tile, row] + ([] if head is None else [row, tile]) + [hbm, hbm, hbm],
        out_specs=[tile, tile, hid, hid, hid] + [s for s, _ in sums],
        out_shape=[jax.ShapeDtypeStruct((t, d), F32), jax.ShapeDtypeStruct((t, d), BF16)]
        + [jax.ShapeDtypeStruct((ns, t, fs), BF16)] * 3 + [s for _, s in sums],
        scratch_shapes=[pltpu.VMEM(wg.shape, BF16), pltpu.VMEM(wu.shape, BF16), pltpu.VMEM(wd.shape, BF16),
                        pltpu.SemaphoreType.DMA((3,))],
        sem=("arbitrary",), args=[x, gain] + ([] if head is None else list(head)) + [wg, wu, wd])


def _ffn_bwd_data(dy, x, gain, g, u, wg, wu, wd, name, rides=None):
    t, d = x.shape
    ns, fs, _ = wg.shape
    tm = min(256, t)

    def body(dy_ref, x_ref, gain_ref, g_ref, u_ref, wg_hbm, wu_hbm, wd_hbm, dx_ref, dg_ref, du_ref, dgain_ref,
             wg_v, wu_v, wd_v, sems):
        @pl.when(pl.program_id(0) == 0)
        def _():
            _load_weights([(wg_hbm, wg_v), (wu_hbm, wu_v), (wd_hbm, wd_v)], sems)
            dgain_ref[...] = jnp.zeros_like(dgain_ref)

        dyv = dy_ref[...]
        dyh = (0.5 * dyv).astype(BF16)
        dxn = jnp.zeros((tm, d), F32)
        das = [_dot(dyh, wd_v[j], NT) for j in range(ns)]
        for j in range(ns):
            da = das[j]
            gj = g_ref[j].astype(F32)
            uj = u_ref[j].astype(F32)
            sig = _sigmoid(gj)
            dgj = (da * uj * (sig * (1.0 + gj * (1.0 - sig)))).astype(BF16)
            duj = (da * (gj * sig)).astype(BF16)
            dg_ref[j] = dgj
            du_ref[j] = duj
            dxn = dxn + _dot(dgj, wg_v[j]) + _dot(duj, wu_v[j])
        xv = x_ref[...]
        r = lax.rsqrt(jnp.mean(xv * xv, axis=-1, keepdims=True) + NORM_EPS)
        xh = xv * r
        dgain_ref[...] += _rows8(dxn * xh)
        dxh = dxn * gain_ref[...]
        dx_ref[...] = dyv + r * (dxh - xh * jnp.mean(dxh * xh, axis=-1, keepdims=True))

    hbm = pl.BlockSpec(memory_space=pl.ANY)
    tile = pl.BlockSpec((tm, d), lambda i: (i, 0))
    hid = pl.BlockSpec((ns, tm, fs), lambda i: (0, i, 0))
    return _pallas(
        body, rides, name=name, grid=(t // tm,),
        in_specs=[tile, tile, pl.BlockSpec((1, d), lambda i: (0, 0)), hid, hid, hbm, hbm, hbm],
        out_specs=[tile, hid, hid, pl.BlockSpec((8, d), lambda i: (0, 0))],
        out_shape=[jax.ShapeDtypeStruct((t, d), F32), jax.ShapeDtypeStruct((ns, t, fs), BF16),
                   jax.ShapeDtypeStruct((ns, t, fs), BF16), jax.ShapeDtypeStruct((8, d), F32)],
        scratch_shapes=[pltpu.VMEM(wg.shape, BF16), pltpu.VMEM(wu.shape, BF16), pltpu.VMEM(wd.shape, BF16),
                        pltpu.SemaphoreType.DMA((3,))],
        sem=("arbitrary",), args=[dy, x, gain, g, u, wg, wu, wd])


def _ffn_wgrad_down(a, dy, name, rides=None):
    t, d = dy.shape
    ns, _, fs = a.shape
    tk = min(1024, t)

    def body(dy_ref, a_ref, dwd_ref):
        @pl.when(pl.program_id(1) == 0)
        def _():
            dwd_ref[...] = jnp.zeros_like(dwd_ref)

        dwd_ref[...] += _dot(a_ref[...], (0.5 * dy_ref[...]).astype(BF16), TN)

    return _pallas(
        body, rides, name=name, grid=(ns, t // tk),
        in_specs=[pl.BlockSpec((tk, d), lambda j, k: (k, 0)), pl.BlockSpec((None, tk, fs), lambda j, k: (j, k, 0))],
        out_specs=[pl.BlockSpec((None, fs, d), lambda j, k: (j, 0, 0))],
        out_shape=[jax.ShapeDtypeStruct((ns, fs, d), F32)],
        sem=("arbitrary", "arbitrary"), args=[dy, a])


def _ffn_wgrad_gu(xn, dg, du, name, rides=None):
    t, d = xn.shape
    ns, _, fs = dg.shape
    tk = min(2048, t)

    def body(xn_ref, dg_ref, du_ref, dwg_ref, dwu_ref):
        @pl.when(pl.program_id(1) == 0)
        def _():
            dwg_ref[...] = jnp.zeros_like(dwg_ref)
            dwu_ref[...] = jnp.zeros_like(dwu_ref)

        xnv = xn_ref[...]
        dwg_ref[...] += _dot(dg_ref[...], xnv, TN)
        dwu_ref[...] += _dot(du_ref[...], xnv, TN)

    hid = pl.BlockSpec((None, tk, fs), lambda j, k: (j, k, 0))
    out = pl.BlockSpec((None, fs, d), lambda j, k: (j, 0, 0))
    return _pallas(
        body, rides, name=name, grid=(ns, t // tk),
        in_specs=[pl.BlockSpec((tk, d), lambda j, k: (k, 0)), hid, hid],
        out_specs=[out, out], out_shape=[jax.ShapeDtypeStruct((ns, fs, d), F32)] * 2,
        sem=("arbitrary", "arbitrary"), args=[xn, dg, du])


def _tn_matmul(a, b, bn, name):
    t, m = a.shape
    n = b.shape[1]
    tk = min(2048, t)

    def body(a_ref, b_ref, o_ref):
        @pl.when(pl.program_id(1) == 0)
        def _():
            o_ref[...] = jnp.zeros_like(o_ref)

        o_ref[...] += _dot(a_ref[...].astype(BF16), b_ref[...].astype(BF16), TN)

    return pl.pallas_call(
        body, name=name, grid=(n // bn, t // tk),
        in_specs=[pl.BlockSpec((tk, m), lambda j, k: (k, 0)), pl.BlockSpec((tk, bn), lambda j, k: (k, j))],
        out_specs=pl.BlockSpec((None, m, bn), lambda j, k: (j, 0, 0)),
        out_shape=jax.ShapeDtypeStruct((n // bn, m, bn), F32),
        compiler_params=_params("arbitrary", "arbitrary"),
    )(a, b)


def _chunk_scratch(tm, w):
    return pltpu.VMEM((w // LANE, tm, LANE), F32)


def _regroup_store(cbuf, out_ref, dil):
    n = out_ref.shape[1]
    for g in range(dil):
        for k in range(cbuf.shape[0]):
            rows = cbuf[k] if dil == 1 else cbuf[k, pl.ds(g, n, stride=dil), :]
            out_ref[g, :, k * LANE:(k + 1) * LANE] = rows.astype(out_ref.dtype)


def _natural_rows(ref, dil, cbuf):
    if dil == 1:
        return ref[0].astype(F32)
    n = ref.shape[1]
    for g in range(dil):
        for k in range(cbuf.shape[0]):
            cbuf[k, pl.ds(g, n, stride=dil), :] = ref[g, :, k * LANE:(k + 1) * LANE].astype(F32)
    return jnp.concatenate([cbuf[k] for k in range(cbuf.shape[0])], axis=1)


def _inproj_fwd(h, gain, win):
    t, d = h.shape
    ns, _, cs = win.shape
    tm = min(512, t)
    rw, aw = 4 * RET_WIDTH, 3 * ATT_WIDTH

    def body(h_ref, gain_ref, w_ref, xn_ref, ur_ref, *rest):
        a_refs, abuf = rest[:-1], rest[-1]
        hv = h_ref[...]
        r = lax.rsqrt(jnp.mean(hv * hv, axis=-1, keepdims=True) + NORM_EPS)
        xn = (hv * r * gain_ref[...]).astype(BF16)
        xn_ref[...] = xn
        for j in range(ns):
            res = _dot(xn, w_ref[j])
            for k in range(cs // LANE):
                chunk = j * (cs // LANE) + k
                piece = res[:, k * LANE:(k + 1) * LANE]
                if chunk < rw // LANE:
                    ur_ref[:, chunk * LANE:(chunk + 1) * LANE] = piece
                else:
                    abuf[chunk - rw // LANE] = piece
        for dil, a_ref in zip(DILATIONS, a_refs):
            _regroup_store(abuf, a_ref, dil)

    return pl.pallas_call(
        body, name="inproj_fwd", grid=(t // tm,),
        in_specs=[pl.BlockSpec((tm, d), lambda i: (i, 0)), pl.BlockSpec((1, d), lambda i: (0, 0)),
                  pl.BlockSpec(win.shape, lambda i: (0, 0, 0))],
        out_specs=[pl.BlockSpec((tm, d), lambda i: (i, 0)), pl.BlockSpec((tm, rw), lambda i: (i, 0))]
        + [pl.BlockSpec((dil, tm // dil, aw), lambda i: (0, i, 0)) for dil in DILATIONS],
        out_shape=[jax.ShapeDtypeStruct((t, d), BF16), jax.ShapeDtypeStruct((t, rw), F32)]
        + [jax.ShapeDtypeStruct((dil, t // dil, aw), BF16) for dil in DILATIONS],
        scratch_shapes=[_chunk_scratch(tm, aw)],
        compiler_params=_params("arbitrary"),
    )(h, gain, win)


def _inproj_bwd(pieces, h, gain, dres, win):
    t, d = h.shape
    ns, _, cs = win.shape
    pw = pieces[0].shape[1]
    tm = min(512, t)
    npc = len(pieces)

    def body(*refs):
        p_refs = refs[:npc]
        h_ref, gain_ref, dres_ref, w_ref, dh_ref, du_ref, dgain_ref = refs[npc:]

        @pl.when(pl.program_id(0) == 0)
        def _():
            dgain_ref[...] = jnp.zeros_like(dgain_ref)

        for k in range(npc):
            du_ref[:, k * pw:(k + 1) * pw] = p_refs[k][...]
        dxn = jnp.zeros((tm, d), F32)
        for j in range(ns):
            dxn = dxn + _dot(du_ref[:, j * cs:(j + 1) * cs], w_ref[j], NT)
        hv = h_ref[...]
        r = lax.rsqrt(jnp.mean(hv * hv, axis=-1, keepdims=True) + NORM_EPS)
        xh = hv * r
        dgain_ref[...] += _rows8(dxn * xh)
        dxh = dxn * gain_ref[...]
        dh_ref[...] = dres_ref[...] + r * (dxh - xh * jnp.mean(dxh * xh, axis=-1, keepdims=True))

    tile = pl.BlockSpec((tm, d), lambda i: (i, 0))
    return pl.pallas_call(
        body, name="inproj_bwd", grid=(t // tm,),
        in_specs=[pl.BlockSpec((tm, pw), lambda i: (i, 0))] * npc + [
            tile, pl.BlockSpec((1, d), lambda i: (0, 0)), tile, pl.BlockSpec(win.shape, lambda i: (0, 0, 0))],
        out_specs=[tile, pl.BlockSpec((tm, npc * pw), lambda i: (i, 0)), pl.BlockSpec((8, d), lambda i: (0, 0))],
        out_shape=[jax.ShapeDtypeStruct((t, d), F32), jax.ShapeDtypeStruct((t, npc * pw), BF16),
                   jax.ShapeDtypeStruct((8, d), F32)],
        compiler_params=_params("arbitrary"),
    )(*pieces, h, gain, dres, win)


def _outproj_fwd(h, mix_r, mix_a, wo):
    t, d = h.shape
    hw = mix_r.shape[1]
    tm = min(512, t)

    def body(h_ref, mr_ref, ma_ref, w_ref, o_ref):
        o_ref[...] = h_ref[...] + _dot(mr_ref[...], w_ref[0:hw, :]) + _dot(ma_ref[...], w_ref[hw:2 * hw, :])

    tile = pl.BlockSpec((tm, d), lambda i: (i, 0))
    half = pl.BlockSpec((tm, hw), lambda i: (i, 0))
    return pl.pallas_call(
        body, name="outproj_fwd", grid=(t // tm,),
        in_specs=[tile, half, half, pl.BlockSpec(wo.shape, lambda i: (0, 0))],
        out_specs=tile, out_shape=jax.ShapeDtypeStruct((t, d), F32),
        compiler_params=_params("arbitrary"),
    )(h, mix_r, mix_a, wo)


def _outproj_bwd(dh, wo, rides=None):
    t, d = dh.shape
    hw = wo.shape[0] // 2
    tm = min(512, t)

    def body(dh_ref, w_ref, dr_ref, da_ref):
        dhb = dh_ref[...].astype(BF16)
        dr_ref[...] = _dot(dhb, w_ref[0:hw, :], NT)
        da_ref[...] = _dot(dhb, w_ref[hw:2 * hw, :], NT)

    half = pl.BlockSpec((tm, hw), lambda i: (i, 0))
    return _pallas(
        body, rides, name="outproj_bwd", grid=(t // tm,),
        in_specs=[pl.BlockSpec((tm, d), lambda i: (i, 0)), pl.BlockSpec(wo.shape, lambda i: (0, 0))],
        out_specs=[half, half],
        out_shape=[jax.ShapeDtypeStruct((t, hw), F32), jax.ShapeDtypeStruct((t, hw), F32)],
        sem=("arbitrary",), args=[dh, wo])


def _retention_tables(t):
    pos = jnp.arange(t, dtype=F32)
    inv_freq = ROPE_BASE ** (-jnp.arange(0, RET_DIM, 2, dtype=F32) / RET_DIM)
    ang = jnp.repeat(pos[:, None] * inv_freq[None, :], 2, axis=-1)
    c = RET_CHUNK
    log_g = jnp.log(1.0 - 2.0 ** (-5.0 - jnp.arange(RET_HEADS, dtype=F32)))
    idx = jnp.arange(c, dtype=F32)
    rel = idx[:, None] - idx[None, :]
    decay = jnp.where(rel >= 0, jnp.exp(log_g[:, None, None] * jnp.maximum(rel, 0.0)), 0.0)
    zeta = jnp.exp(log_g[:, None] * (c - 1 - idx)[None, :])
    xi = jnp.exp(log_g[:, None] * (idx + 1)[None, :])
    gc = jnp.exp(log_g * c)
    wide = lambda v: jnp.broadcast_to(v[:, :, None], (RET_HEADS, c, LANE))
    return (jnp.cos(ang), jnp.sin(ang), decay, wide(zeta), wide(xi),
            jnp.broadcast_to(gc[:, None, None], (RET_HEADS, c, LANE)))


def _rot(v):
    lane = lax.broadcasted_iota(jnp.int32, v.shape, 1)
    nxt = pltpu.roll(v, LANE - 1, 1)
    prv = pltpu.roll(v, 1, 1)
    return jnp.where(lane % 2 == 0, -nxt, prv)


def _ret_specs(tr, rev, nt):
    ti = (lambda i: nt - 1 - i) if rev else (lambda i: i)
    col = lambda blk: pl.BlockSpec((tr, RET_WIDTH), lambda i: (ti(i), blk))
    tab = pl.BlockSpec((tr, LANE), lambda i: (ti(i), 0))
    head = pl.BlockSpec((RET_HEADS, RET_CHUNK, LANE), lambda i: (0, 0, 0))
    return col, tab, head


def _ret_chunks(tr, rev=False):
    order = list(range(tr // RET_CHUNK))
    return [(pl.ds(ci * RET_CHUNK, RET_CHUNK), slice(h * RET_DIM, (h + 1) * RET_DIM), h)
            for h in range(RET_HEADS) for ci in (reversed(order) if rev else order)]


def _ret_operands(items, q_ref, k_ref, v_ref, cos_ref, sin_ref, zeta_ref):
    scale = RET_DIM ** -0.5
    qbs, kbs, vbs, kzs = [], [], [], []
    for sl, hs, h in items:
        cs, sn = cos_ref[sl, :], sin_ref[sl, :]
        q, k = q_ref[sl, hs], k_ref[sl, hs]
        kr = (k * cs + _rot(k) * sn) * scale
        qbs.append((q * cs + _rot(q) * sn).astype(BF16))
        kbs.append(kr.astype(BF16))
        vbs.append(v_ref[sl, hs].astype(BF16))
        kzs.append((kr * zeta_ref[h]).astype(BF16))
    return qbs, kbs, vbs, kzs


def _ret_states(items, state, steps, gc_ref):
    cur, befores = {}, []
    for (sl, hs, h), step in zip(items, steps):
        st = cur[h] if h in cur else state[h]
        befores.append(st)
        cur[h] = st * gc_ref[h] + step
    for h, st in cur.items():
        state[h] = st
    return befores


def _ret_fwd(u, gain, tabs):
    t = u.shape[0]
    tr = min(512, t)
    nt = t // tr
    cos, sin, decay, zeta, xi, gc = tabs
    scale = RET_DIM ** -0.5

    def body(q_ref, k_ref, v_ref, gt_ref, cos_ref, sin_ref, gain_ref, dec_ref, zeta_ref, xi_ref, gc_ref,
             raw_ref, mix_ref, state):
        @pl.when(pl.program_id(0) == 0)
        def _():
            state[...] = jnp.zeros_like(state)

        items = _ret_chunks(tr)
        n = range(len(items))
        qbs, kbs, vbs, kzs = _ret_operands(items, q_ref, k_ref, v_ref, cos_ref, sin_ref, zeta_ref)
        ss = [_dot(qbs[i], kbs[i], NT) for i in n]
        kvs = [_dot(kzs[i], vbs[i], TN) for i in n]
        befores = _ret_states(items, state, kvs, gc_ref)
        intra = [_dot((ss[i] * dec_ref[items[i][2]]).astype(BF16), vbs[i]) for i in n]
        inter = [_dot(qbs[i], befores[i].astype(BF16)) for i in n]
        for i, (sl, hs, h) in enumerate(items):
            o = intra[i] + inter[i] * xi_ref[h]
            raw_ref[sl, hs] = o
            mu = jnp.mean(o, axis=-1, keepdims=True)
            var = jnp.mean(jnp.square(o - mu), axis=-1, keepdims=True)
            y = (o - mu) * lax.rsqrt(var + GN_EPS) * gain_ref[:, hs]
            gt = gt_ref[sl, hs]
            mix_ref[sl, hs] = (y * (gt * _sigmoid(gt))).astype(BF16)

    col, tab, head = _ret_specs(tr, False, nt)
    out = pl.BlockSpec((tr, RET_WIDTH), lambda i: (i, 0))
    return pl.pallas_call(
        body, name="ret_fwd", grid=(nt,),
        in_specs=[col(0), col(1), col(2), col(3), tab, tab, pl.BlockSpec((1, RET_WIDTH), lambda i: (0, 0)),
                  head, head, head, head],
        out_specs=[out, out],
        out_shape=[jax.ShapeDtypeStruct((t, RET_WIDTH), F32), jax.ShapeDtypeStruct((t, RET_WIDTH), BF16)],
        scratch_shapes=[pltpu.VMEM((RET_HEADS, RET_DIM, RET_DIM), F32)],
        compiler_params=_params("arbitrary"),
    )(u, u, u, u, cos, sin, gain, decay, zeta, xi, gc)


def _ret_bwd_q(dmix, raw, u, gain, tabs, rides=None):
    t = u.shape[0]
    tr = min(512, t)
    nt = t // tr
    cos, sin, decay, zeta, xi, gc = tabs
    scale = RET_DIM ** -0.5

    def body(dm_ref, raw_ref, q_ref, k_ref, v_ref, gt_ref, cos_ref, sin_ref, gain_ref, dec_ref, zeta_ref, xi_ref, gc_ref,
             dq_ref, dgt_ref, dret_ref, dgain_ref, state):
        @pl.when(pl.program_id(0) == 0)
        def _():
            state[...] = jnp.zeros_like(state)
            dgain_ref[...] = jnp.zeros_like(dgain_ref)

        items = _ret_chunks(tr)
        n_items = range(len(items))
        qbs, kbs, vbs, kzs = _ret_operands(items, q_ref, k_ref, v_ref, cos_ref, sin_ref, zeta_ref)
        dos, dgains = [], {}
        for sl, hs, h in items:
            o = raw_ref[sl, hs]
            mu = jnp.mean(o, axis=-1, keepdims=True)
            var = jnp.mean(jnp.square(o - mu), axis=-1, keepdims=True)
            rs = lax.rsqrt(var + GN_EPS)
            n = (o - mu) * rs
            gt = gt_ref[sl, hs]
            sig = _sigmoid(gt)
            dout = dm_ref[sl, hs]
            gain_h = gain_ref[:, hs]
            dgt_ref[sl, hs] = (dout * (n * gain_h) * (sig * (1.0 + gt * (1.0 - sig)))).astype(BF16)
            dy = dout * (gt * sig)
            dgains[h] = dgains[h] + _rows8(dy * n) if h in dgains else _rows8(dy * n)
            dn = dy * gain_h
            do = rs * (dn - jnp.mean(dn, axis=-1, keepdims=True) - n * jnp.mean(dn * n, axis=-1, keepdims=True))
            dret_ref[sl, hs] = do
            dos.append(do)
        for h, dg in dgains.items():
            dgain_ref[:, h * RET_DIM:(h + 1) * RET_DIM] += dg
        dss = [_dot(dos[i].astype(BF16), vbs[i], NT) for i in n_items]
        kvs = [_dot(kzs[i], vbs[i], TN) for i in n_items]
        befores = _ret_states(items, state, kvs, gc_ref)
        intra = [_dot((dss[i] * dec_ref[items[i][2]]).astype(BF16), kbs[i]) for i in n_items]
        inter = [_dot((dos[i] * xi_ref[items[i][2]]).astype(BF16), befores[i].astype(BF16), NT) for i in n_items]
        for i, (sl, hs, h) in enumerate(items):
            dqr = intra[i] + inter[i]
            dq_ref[sl, hs] = (dqr * cos_ref[sl, :] - _rot(dqr * sin_ref[sl, :])).astype(BF16)

    col, tab, head = _ret_specs(tr, False, nt)
    out = pl.BlockSpec((tr, RET_WIDTH), lambda i: (i, 0))
    return _pallas(
        body, rides, name="ret_bwd_q", grid=(nt,),
        in_specs=[out, out, col(0), col(1), col(2), col(3), tab, tab, pl.BlockSpec((1, RET_WIDTH), lambda i: (0, 0)),
                  head, head, head, head],
        out_specs=[out, out, out, pl.BlockSpec((8, RET_WIDTH), lambda i: (0, 0))],
        out_shape=[jax.ShapeDtypeStruct((t, RET_WIDTH), BF16), jax.ShapeDtypeStruct((t, RET_WIDTH), BF16),
                   jax.ShapeDtypeStruct((t, RET_WIDTH), F32), jax.ShapeDtypeStruct((8, RET_WIDTH), F32)],
        scratch_shapes=[pltpu.VMEM((RET_HEADS, RET_DIM, RET_DIM), F32)],
        sem=("arbitrary",), args=[dmix, raw, u, u, u, u, cos, sin, gain, decay, zeta, xi, gc])


def _ret_bwd_kv(dret, u, tabs, rides=None):
    t = u.shape[0]
    tr = min(512, t)
    nt = t // tr
    cos, sin, decay, zeta, xi, gc = tabs
    scale = RET_DIM ** -0.5

    def body(do_ref, q_ref, k_ref, v_ref, cos_ref, sin_ref, dec_ref, zeta_ref, xi_ref, gc_ref, dk_ref, dv_ref, gst):
        @pl.when(pl.program_id(0) == 0)
        def _():
            gst[...] = jnp.zeros_like(gst)

        items = _ret_chunks(tr, rev=True)
        n = range(len(items))
        qbs, kbs, vbs, kzs = _ret_operands(items, q_ref, k_ref, v_ref, cos_ref, sin_ref, zeta_ref)
        dos = [do_ref[sl, hs] for sl, hs, h in items]
        dobs = [do.astype(BF16) for do in dos]
        ss = [_dot(qbs[i], kbs[i], NT) for i in n]
        dss = [_dot(dobs[i], vbs[i], NT) for i in n]
        steps = [_dot(qbs[i], (dos[i] * xi_ref[items[i][2]]).astype(BF16), TN) for i in n]
        afters = [g.astype(BF16) for g in _ret_states(items, gst, steps, gc_ref)]
        dvs = [_dot((ss[i] * dec_ref[items[i][2]]).astype(BF16), dobs[i], TN) + _dot(kzs[i], afters[i]) for i in n]
        dks = [_dot((dss[i] * dec_ref[items[i][2]]).astype(BF16), qbs[i], TN) for i in n]
        dkz = [_dot(vbs[i], afters[i], NT) for i in n]
        for i, (sl, hs, h) in enumerate(items):
            dv_ref[sl, hs] = dvs[i].astype(BF16)
            dkr = (dks[i] + dkz[i] * zeta_ref[h]) * scale
            dk_ref[sl, hs] = (dkr * cos_ref[sl, :] - _rot(dkr * sin_ref[sl, :])).astype(BF16)

    col, tab, head = _ret_specs(tr, True, nt)
    out = pl.BlockSpec((tr, RET_WIDTH), lambda i: (nt - 1 - i, 0))
    return _pallas(
        body, rides, name="ret_bwd_kv", grid=(nt,),
        in_specs=[out, col(0), col(1), col(2), tab, tab, head, head, head, head],
        out_specs=[out, out],
        out_shape=[jax.ShapeDtypeStruct((t, RET_WIDTH), BF16), jax.ShapeDtypeStruct((t, RET_WIDTH), BF16)],
        scratch_shapes=[pltpu.VMEM((RET_HEADS, RET_DIM, RET_DIM), F32)],
        sem=("arbitrary",), args=[dret, u, u, u, cos, sin, decay, zeta, xi, gc])


PAIRS = ATT_WIDTH // LANE
ATT_Q_BLK, ATT_K_BLK, ATT_V_BLK = 0, PAIRS, 2 * PAIRS
STAT_LANES = ATT_DIM // 2


def _att_tiles(t, dil):
    sub = t // dil
    tq = min(512, sub)
    return sub, tq, sub // tq, tq // ATT_BLOCK


def _att_in_specs(tq, qb, ti):
    cur = lambda off: pl.BlockSpec((None, tq, LANE), lambda g, p, i: (g, ti(i), off + p))
    prev = lambda off: pl.BlockSpec((None, ATT_BLOCK, LANE), lambda g, p, i: (g, jnp.maximum(ti(i) * qb - 1, 0), off + p))
    return [cur(ATT_Q_BLK), cur(ATT_K_BLK), prev(ATT_K_BLK), cur(ATT_V_BLK), prev(ATT_V_BLK)]


def _band_mask():
    key = lax.broadcasted_iota(jnp.int32, (2 * ATT_BLOCK, 2 * ATT_BLOCK), 0)
    qry = lax.broadcasted_iota(jnp.int32, (2 * ATT_BLOCK, 2 * ATT_BLOCK), 1) % ATT_BLOCK
    dist = qry + ATT_BLOCK - key
    return (dist >= 0) & (dist <= ATT_BLOCK), key >= ATT_BLOCK


def _head0_lanes():
    return lax.broadcasted_iota(jnp.int32, (ATT_BLOCK, LANE), 1) < ATT_DIM


def _stack_heads(v, head0):
    zero = jnp.zeros((), v.dtype)
    return jnp.concatenate([jnp.where(head0, v, zero), jnp.where(head0, zero, v)], axis=0)


def _unstack_heads(v, head0):
    return jnp.where(head0, v[0:ATT_BLOCK], v[ATT_BLOCK:])


def _att_fwd(ua, dil):
    sub = ua.shape[1]
    _, tq, nq, qb = _att_tiles(sub * dil, dil)

    def body(q_ref, kc_ref, kp_ref, vc_ref, vp_ref, o_ref, l_ref, kx, vx):
        tile = pl.program_id(2)
        kx[0:ATT_BLOCK, :] = kp_ref[...]
        kx[ATT_BLOCK:, :] = kc_ref[...]
        vx[0:ATT_BLOCK, :] = vp_ref[...]
        vx[ATT_BLOCK:, :] = vc_ref[...]
        band, cur_keys = _band_mask()
        head0 = _head0_lanes()
        blocks = range(qb)
        rows = [slice(b * ATT_BLOCK, (b + 1) * ATT_BLOCK) for b in blocks]
        keys = [slice(b * ATT_BLOCK, (b + 2) * ATT_BLOCK) for b in blocks]
        sts = [_dot(kx[keys[b], :], _stack_heads(q_ref[rows[b], :] * jnp.asarray(ATT_DIM ** -0.5, BF16), head0), NT)
               for b in blocks]
        pts, lses = [], []
        for b in blocks:
            mask = band if b > 0 else band & (cur_keys | (tile > 0))
            st = jnp.where(mask, sts[b], -1e30)
            m = jnp.max(st, axis=0, keepdims=True)
            ex = jnp.exp(st - m)
            den = jnp.sum(ex, axis=0, keepdims=True)
            pts.append((ex * (1.0 / den)).astype(BF16))
            lses.append(m + jnp.log(den))
        outs = [_dot(pts[b], vx[keys[b], :], TN) for b in blocks]
        for b in blocks:
            o_ref[rows[b], :] = _unstack_heads(outs[b], head0).astype(BF16)
            cols = [jnp.broadcast_to(lses[b][:, e * ATT_BLOCK:(e + 1) * ATT_BLOCK], (ATT_BLOCK, LANE)).T for e in range(2)]
            l_ref[rows[b], :] = jnp.where(head0, cols[0], cols[1])

    out = pl.BlockSpec((None, tq, LANE), lambda g, p, i: (g, i, p))
    return pl.pallas_call(
        body, name=f"att_fwd_d{dil}", grid=(dil, PAIRS, nq),
        in_specs=_att_in_specs(tq, qb, lambda i: i),
        out_specs=[out, out],
        out_shape=[jax.ShapeDtypeStruct((dil, sub, ATT_WIDTH), BF16), jax.ShapeDtypeStruct((dil, sub, ATT_WIDTH), F32)],
        scratch_shapes=[pltpu.VMEM((tq + ATT_BLOCK, LANE), BF16)] * 2,
        compiler_params=_params("arbitrary", "arbitrary", "arbitrary"),
    )(ua, ua, ua, ua, ua)


def _regrouped_spec(tm, dil, w):
    return pl.BlockSpec((dil, tm // dil, w), lambda i: (0, i, 0))


def _att_combine(outs, lses, t):
    w = ATT_WIDTH
    tm = min(512, t)
    nb = len(outs)

    def body(*refs):
        o_refs, l_refs = refs[:nb], refs[nb:2 * nb]
        mix_ref, att_ref, lse_ref, buf = refs[2 * nb:]
        ls = [_natural_rows(r, dil, buf) for r, dil in zip(l_refs, DILATIONS)]
        m = functools.reduce(jnp.maximum, ls)
        ws = [jnp.exp(l - m) for l in ls]
        den = functools.reduce(jnp.add, ws)
        att = functools.reduce(jnp.add, [(wt / den) * _natural_rows(r, dil, buf) for wt, r, dil in zip(ws, o_refs, DILATIONS)])
        att_ref[...] = att
        mix_ref[...] = att.astype(BF16)
        lse_ref[...] = m + jnp.log(den)

    tile = pl.BlockSpec((tm, w), lambda i: (i, 0))
    regrouped = [_regrouped_spec(tm, dil, w) for dil in DILATIONS]
    return pl.pallas_call(
        body, name="att_combine", grid=(t // tm,),
        in_specs=regrouped * 2, out_specs=[tile, tile, tile],
        out_shape=[jax.ShapeDtypeStruct((t, w), BF16), jax.ShapeDtypeStruct((t, w), F32), jax.ShapeDtypeStruct((t, w), F32)],
        scratch_shapes=[_chunk_scratch(tm, w)],
        compiler_params=_params("arbitrary"),
    )(*outs, *lses)


def _att_bwd_prep(datt, att, lse):
    t, w = datt.shape
    tm = min(512, t)

    def body(da_ref, at_ref, l_ref, *rest):
        outs, dbuf, sbuf = rest[:-2], rest[-2], rest[-1]
        dav = da_ref[...]
        prod = dav * at_ref[...]
        lane = lax.broadcasted_iota(jnp.int32, (tm, LANE), 1)
        for k in range(w // LANE):
            cols = slice(k * LANE, (k + 1) * LANE)
            dbuf[k] = dav[:, cols]
            delta = jnp.concatenate(
                [jnp.broadcast_to(jnp.sum(prod[:, k * LANE + e * ATT_DIM:k * LANE + (e + 1) * ATT_DIM], axis=-1, keepdims=True),
                                  (tm, ATT_DIM)) for e in range(LANE // ATT_DIM)], axis=1)
            sbuf[k] = jnp.where(lane % ATT_DIM < STAT_LANES, l_ref[:, cols], delta)
        for k, dil in enumerate(DILATIONS):
            _regroup_store(dbuf, outs[2 * k], dil)
            _regroup_store(sbuf, outs[2 * k + 1], dil)

    tile = pl.BlockSpec((tm, w), lambda i: (i, 0))
    res = pl.pallas_call(
        body, name="att_bwd_prep", grid=(t // tm,),
        in_specs=[tile] * 3,
        out_specs=[_regrouped_spec(tm, dil, w) for dil in DILATIONS for _ in range(2)],
        out_shape=[jax.ShapeDtypeStruct((dil, t // dil, w), dt) for dil in DILATIONS for dt in (BF16, F32)],
        scratch_shapes=[_chunk_scratch(tm, w)] * 2,
        compiler_params=_params("arbitrary"),
    )(datt, att, lse)
    return [(res[2 * k], res[2 * k + 1]) for k in range(len(DILATIONS))]


def _att_bwd(ua, da, stat, dil, rides=None):
    sub = ua.shape[1]
    _, tq, nq, qb = _att_tiles(sub * dil, dil)
    scale = ATT_DIM ** -0.5

    def body(q_ref, kc_ref, kp_ref, vc_ref, vp_ref, da_ref, st_ref, dq_ref, dk_ref, dv_ref, kx, vx, ck, cv):
        step = pl.program_id(2)
        tile = nq - 1 - step

        @pl.when(step == 0)
        def _():
            ck[...] = jnp.zeros_like(ck)
            cv[...] = jnp.zeros_like(cv)

        kx[0:ATT_BLOCK, :] = kp_ref[...]
        kx[ATT_BLOCK:, :] = kc_ref[...]
        vx[0:ATT_BLOCK, :] = vp_ref[...]
        vx[ATT_BLOCK:, :] = vc_ref[...]
        band, cur_keys = _band_mask()
        head0 = _head0_lanes()
        blocks = range(qb)
        rows = [slice(b * ATT_BLOCK, (b + 1) * ATT_BLOCK) for b in blocks]
        keys = [slice(b * ATT_BLOCK, (b + 2) * ATT_BLOCK) for b in blocks]
        qqs = [_stack_heads(q_ref[rows[b], :] * jnp.asarray(scale, BF16), head0) for b in blocks]
        dds = [_stack_heads(da_ref[rows[b], :], head0) for b in blocks]
        sts = [_dot(kx[keys[b], :], qqs[b], NT) for b in blocks]
        dpts = [_dot(vx[keys[b], :], dds[b], NT) for b in blocks]
        pts, dsts = [], []
        for b in blocks:
            mask = band if b > 0 else band & (cur_keys | (tile > 0))
            stat = st_ref[rows[b], :].T
            row = lambda k: jnp.concatenate([stat[e * ATT_DIM + k:e * ATT_DIM + k + 1, :] for e in range(2)], axis=1)
            pt = jnp.where(mask, jnp.exp(sts[b] - row(0)), 0.0)
            dsts.append((pt * (dpts[b] - row(STAT_LANES))).astype(BF16))
            pts.append(pt.astype(BF16))
        dqs = [_dot(dsts[b], kx[keys[b], :], TN) for b in blocks]
        dkbs = [_dot(dsts[b], qqs[b]) for b in blocks]
        dvbs = [_dot(pts[b], dds[b]) for b in blocks]
        for b in blocks:
            dq_ref[rows[b], :] = (_unstack_heads(dqs[b], head0) * scale).astype(BF16)
        for b in blocks[1:]:
            dk_ref[rows[b - 1], :] = (dkbs[b - 1][ATT_BLOCK:] + dkbs[b][0:ATT_BLOCK]).astype(BF16)
            dv_ref[rows[b - 1], :] = (dvbs[b - 1][ATT_BLOCK:] + dvbs[b][0:ATT_BLOCK]).astype(BF16)
        before_k, before_v = dkbs[0][0:ATT_BLOCK], dvbs[0][0:ATT_BLOCK]
        open_k, open_v = dkbs[-1][ATT_BLOCK:], dvbs[-1][ATT_BLOCK:]
        last = slice(tq - ATT_BLOCK, tq)
        dk_ref[last, :] = (open_k + ck[...]).astype(BF16)
        dv_ref[last, :] = (open_v + cv[...]).astype(BF16)
        ck[...] = before_k
        cv[...] = before_v

    ti = lambda i: nq - 1 - i
    out = pl.BlockSpec((None, tq, LANE), lambda g, p, i: (g, ti(i), p))
    shape = jax.ShapeDtypeStruct((dil, sub, ATT_WIDTH), BF16)
    return _pallas(
        body, rides, name=f"att_bwd_d{dil}", grid=(dil, PAIRS, nq),
        in_specs=_att_in_specs(tq, qb, ti) + [out, out],
        out_specs=[out, out, out], out_shape=[shape] * 3,
        scratch_shapes=[pltpu.VMEM((tq + ATT_BLOCK, LANE), BF16)] * 2 + [pltpu.VMEM((ATT_BLOCK, LANE), F32)] * 2,
        sem=("arbitrary", "arbitrary", "arbitrary"), args=[ua, ua, ua, ua, ua, da, stat])


def _att_bwd_sum(parts, t):
    w = ATT_WIDTH
    tm = min(512, t)
    nk = len(parts[0])

    def body(*refs):
        ins, outs, buf = refs[:-nk - 1], refs[-nk - 1:-1], refs[-1]
        for k in range(nk):
            acc = None
            for b, dil in enumerate(DILATIONS):
                rows = _natural_rows(ins[b * nk + k], dil, buf)
                acc = rows if acc is None else acc + rows
            outs[k][...] = acc.astype(BF16)

    tile = pl.BlockSpec((tm, w), lambda i: (i, 0))
    return pl.pallas_call(
        body, name="att_bwd_sum", grid=(t // tm,),
        in_specs=[_regrouped_spec(tm, dil, w) for dil in DILATIONS for _ in range(nk)], out_specs=[tile] * nk,
        out_shape=[jax.ShapeDtypeStruct((t, w), BF16)] * nk,
        scratch_shapes=[_chunk_scratch(tm, w)],
        compiler_params=_params("arbitrary"),
    )(*[a for p in parts for a in p])


class _Reduction:
    def __init__(self, place, names, grads):
        self.place, self.names, self.grads = place, names, grads

    def pair(self):
        return _pair_ride(self.grads)

    def chips(self, got):
        self.got = got
        return _chip_ride([_pair_sum(self.place, g, r, f"pair_sum_{n}") for g, r, n in zip(self.grads, got, self.names)])

    def halves(self, others):
        return [_chip_sum(self.place, g, r, o, f"chip_sum_{n}")
                for g, r, o, n in zip(self.grads, self.got, others, self.names)]


def _step(x, target, gains, w, place=None):
    t = x.shape[0]
    ex = place is not None
    g_ffn1, g_mix, g_ret, g_ffn2, g_fin = gains
    w = list(w)
    tabs = _retention_tables(t)
    red = lambda names, grads: _Reduction(place, names, grads) if ex else None
    ride = lambda r: [r] if ex else None

    if ex:
        w[0:3] = _run(_gather_ride(w[0:3]), "gather_ffn1_weights")
    (h1, xn1, ga1, ua1, act1), rest = _ffn_fwd(x, g_ffn1, *w[0:3], "ffn1_fwd", ride(_gather_ride(w[3:])) if ex else None)
    if ex:
        w[3:] = rest[0]
    wg1, wu1, wd1, win, wo, wg2, wu2, wd2 = w
    wo2 = wo.reshape(wo.shape[0] * wo.shape[1], wo.shape[2])
    xnm, u, *uas = _inproj_fwd(h1, g_mix, win)
    raw, mix_r = _ret_fwd(u, g_ret, tabs)
    branches = [_att_fwd(ua, dil) for ua, dil in zip(uas, DILATIONS)]
    mix_a, att, lse = _att_combine([b[0] for b in branches], [b[1] for b in branches], t)
    h2 = _outproj_fwd(h1, mix_r, mix_a, wo2)
    (dh3, xn2, ga2, ua2, act2, loss_p, dg_fin), _ = _ffn_fwd(h2, g_ffn2, wg2, wu2, wd2, "ffn2_fwd", head=(g_fin, target))

    (dwd2,), _ = _ffn_wgrad_down(act2, dh3, "ffn2_wgrad_down")
    r_d2 = red(["ffn2_w_down"], [dwd2])
    (dh2, dga2, dua2, dg_ffn2), e = _ffn_bwd_data(dh3, h2, g_ffn2, ga2, ua2, wg2, wu2, wd2, "ffn2_bwd",
                                                  ex and [r_d2.pair()])
    (dwg2, dwu2), e = _ffn_wgrad_gu(xn2, dga2, dua2, "ffn2_wgrad_gu", ex and [r_d2.chips(e[0])])
    r_gu2 = red(["ffn2_w_gate", "ffn2_w_up"], [dwg2, dwu2])
    (dmix_r, dmix_a), e = _outproj_bwd(dh2, wo2, ex and [r_gu2.pair(), _finish_ride(r_d2.halves(e[0]))])
    if ex:
        got_gu2, (dwd2,) = e
    hw = RET_WIDTH // (wo.shape[1])
    dwo = jnp.concatenate([_tn_matmul(mix_r, dh2, dh2.shape[1], "wo_grad_r").reshape(hw, wo.shape[1], wo.shape[2]),
                           _tn_matmul(mix_a, dh2, dh2.shape[1], "wo_grad_a").reshape(hw, wo.shape[1], wo.shape[2])])
    r_wo = red(["w_out"], [dwo])
    (dq_r, dgt_r, dret, dg_ret), e = _ret_bwd_q(dmix_r, raw, u, g_ret, tabs, ex and [r_gu2.chips(got_gu2)])
    (dk_r, dv_r), e = _ret_bwd_kv(dret, u, tabs, ex and [r_wo.pair(), _finish_ride(r_gu2.halves(e[0]))])
    if ex:
        got_wo, (dwg2, dwu2) = e
    prep = _att_bwd_prep(dmix_a, att, lse)
    p1, e = _att_bwd(uas[0], *prep[0], DILATIONS[0], ex and [r_wo.chips(got_wo)])
    p4, e = _att_bwd(uas[1], *prep[1], DILATIONS[1], ex and [_finish_ride(r_wo.halves(e[0]))])
    if ex:
        (dwo,), = e
    p16, _ = _att_bwd(uas[2], *prep[2], DILATIONS[2])
    dq_a, dk_a, dv_a = _att_bwd_sum([p1, p4, p16], t)
    dh1, du, dg_mix = _inproj_bwd([dq_r, dk_r, dv_r, dgt_r, dq_a, dk_a, dv_a], h1, g_mix, dh2, win)
    dwin = _tn_matmul(xnm, du, win.shape[2], "win_grad")
    r_in = red(["w_in"], [dwin])
    (dwd1,), e = _ffn_wgrad_down(act1, dh1, "ffn1_wgrad_down", ex and [r_in.pair()])
    r_d1 = red(["ffn1_w_down"], [dwd1])
    (dx, dga1, dua1, dg_ffn1), e = _ffn_bwd_data(dh1, x, g_ffn1, ga1, ua1, wg1, wu1, wd1, "ffn1_bwd",
                                                  ex and [r_in.chips(e[0]), r_d1.pair()])
    (dwg1, dwu1), e = _ffn_wgrad_gu(xn1, dga1, dua1, "ffn1_wgrad_gu",
                                    ex and [_finish_ride(r_in.halves(e[0])), r_d1.chips(e[1])])
    gain_parts = [dg_ffn1, dg_mix, dg_ret, dg_ffn2, dg_fin]
    if not ex:
        return loss_p, dx, [dwg1, dwu1, dwd1, dwin, dwo, dwg2, dwu2, dwd2], gain_parts
    (dwin,), oth_d1 = e
    r_gu1 = red(["ffn1_w_gate", "ffn1_w_up"], [dwg1, dwu1])
    got = _run(r_gu1.pair(), "pair_exchange_ffn1_gate_up")
    oth = _run(r_gu1.chips(got), "chip_exchange_ffn1_gate_up")
    last = r_gu1.halves(oth) + r_d1.halves(oth_d1)
    dwg1, dwu1, dwd1, gall = _run(_finish_ride(last, _pack_gains(gain_parts, x.shape[1])), "finish_exchange_ffn1")
    return loss_p, dx, [dwg1, dwu1, dwd1, dwin, dwo, dwg2, dwu2, dwd2], gall


N_DEV = 8
GAIN_ROWS = 8


def _place():
    x, y, c = lax.axis_index("x"), lax.axis_index("y"), lax.axis_index("c")
    chips = [(1 - x, y), (x, 1 - y), (1 - x, 1 - y)]
    return x, y, c, chips


def _hbm_specs(n):
    return [pl.BlockSpec(memory_space=pl.ANY)] * n


def _place_shard(place, w, name):
    r, cols = w.shape
    tr = r // 4

    def body(place_ref, w_ref, o_ref):
        o_ref[...] = w_ref[...].astype(BF16)

    return pl.pallas_call(
        body, name=name,
        grid_spec=pltpu.PrefetchScalarGridSpec(
            num_scalar_prefetch=1, grid=(r // tr,),
            in_specs=[pl.BlockSpec((tr, cols), lambda i, pr: (i, 0))],
            out_specs=pl.BlockSpec((None, tr, cols), lambda i, pr: (pr[0], i, 0))),
        out_shape=jax.ShapeDtypeStruct((N_SHARD, r, cols), BF16),
        compiler_params=_params("arbitrary"),
    )(place, w)


def _gather_ride(bufs):
    na = len(bufs)

    def legs(outs, sems):
        send_sem, recv_sem, fsend_sem, frecv_sem = sems
        x, y, c, chips = _place()

        def half(a, idx, which):
            hr = outs[a].shape[1] // 2
            return outs[a].at[idx, pl.ds(which * hr, hr)]

        def ici(a, j, idx):
            px, py = chips[j]
            return pltpu.make_async_remote_copy(
                src_ref=half(a, idx, c), dst_ref=half(a, idx, c),
                send_sem=send_sem.at[a, j], recv_sem=recv_sem.at[a, j], device_id=(px, py, c), device_id_type=MESH)

        def d2d(a, j, idx, which):
            return pltpu.make_async_remote_copy(
                src_ref=half(a, idx, which), dst_ref=half(a, idx, which),
                send_sem=fsend_sem.at[a, j], recv_sem=frecv_sem.at[a, j], device_id=(x, y, 1 - c), device_id_type=MESH)

        return 2 * x + y, c, chips, ici, d2d

    def start(ins, outs, sems):
        me, _, _, ici, _ = legs(outs, sems)
        for a in range(na):
            for j in range(3):
                ici(a, j, me).start()

    def finish(ins, outs, sems):
        me, c, chips, ici, d2d = legs(outs, sems)
        passed = []
        for a in range(na):
            for j, (px, py) in enumerate(chips):
                ici(a, j, 2 * px + py).wait_recv()
                cp = d2d(a, j, 2 * px + py, c)
                cp.start()
                passed.append(cp)
        for a in range(na):
            for j, (px, py) in enumerate(chips):
                d2d(a, j, 2 * px + py, 1 - c).wait_recv()
        for a in range(na):
            for j in range(3):
                ici(a, j, me).wait_send()
        for cp in passed:
            cp.wait_send()

    return _Ride(bufs, [jax.ShapeDtypeStruct(b.shape, b.dtype) for b in bufs], [pltpu.SemaphoreType.DMA((na, 3))] * 4,
                 start, finish, {a: a for a in range(na)})


def _pair_ride(grads):
    na = len(grads)

    def copies(ins, outs, sems):
        send_sem, recv_sem = sems
        x, y, c, _ = _place()
        res = []
        for a in range(na):
            hr = ins[a].shape[1] // 2
            res.append(pltpu.make_async_remote_copy(
                src_ref=ins[a].at[:, pl.ds((1 - c) * hr, hr)], dst_ref=outs[a],
                send_sem=send_sem.at[a], recv_sem=recv_sem.at[a], device_id=(x, y, 1 - c), device_id_type=MESH))
        return res

    def start(ins, outs, sems):
        for cp in copies(ins, outs, sems):
            cp.start()

    def finish(ins, outs, sems):
        for cp in copies(ins, outs, sems):
            cp.wait()

    return _Ride(grads, [jax.ShapeDtypeStruct((g.shape[0], g.shape[1] // 2, g.shape[2]), g.dtype) for g in grads],
                 [pltpu.SemaphoreType.DMA((na,))] * 2, start, finish)


def _chip_ride(sums):
    na = len(sums)

    def copies(ins, outs, sems):
        send_sem, recv_sem = sems
        x, y, c, chips = _place()
        res = []
        for a in range(na):
            for j, (px, py) in enumerate(chips):
                res.append(pltpu.make_async_remote_copy(
                    src_ref=ins[a].at[2 * px + py], dst_ref=outs[a].at[j],
                    send_sem=send_sem.at[a, j], recv_sem=recv_sem.at[a, j], device_id=(px, py, c), device_id_type=MESH))
        return res

    def start(ins, outs, sems):
        for cp in copies(ins, outs, sems):
            cp.start()

    def finish(ins, outs, sems):
        for cp in copies(ins, outs, sems):
            cp.wait()

    return _Ride(sums, [jax.ShapeDtypeStruct((3,) + s.shape[1:], s.dtype) for s in sums],
                 [pltpu.SemaphoreType.DMA((na, 3))] * 2, start, finish)


def _finish_ride(grads, gpack=None):
    na = len(grads)

    def halves(outs, sems, which):
        x, y, c, _ = _place()
        res = []
        for a in range(na):
            hr = outs[a].shape[0] // 2
            rows = outs[a].at[pl.ds((c if which == "mine" else 1 - c) * hr, hr)]
            res.append(pltpu.make_async_remote_copy(
                src_ref=rows, dst_ref=rows, send_sem=sems[0].at[a], recv_sem=sems[1].at[a],
                device_id=(x, y, 1 - c), device_id_type=MESH))
        return res

    def gains(ins, outs, sems):
        x, y, c, _ = _place()
        dev = 4 * x + 2 * y + c
        g_in, g_out = ins[na], outs[na]
        own = pltpu.make_async_copy(g_in, g_out.at[dev], sems[2])
        sends, lands = [], []
        for k in range(N_DEV - 1):
            bx, by, bc = (k + 1) // 4, ((k + 1) // 2) % 2, (k + 1) % 2
            peer = (jnp.bitwise_xor(x, bx), jnp.bitwise_xor(y, by), jnp.bitwise_xor(c, bc))
            sends.append(pltpu.make_async_remote_copy(
                src_ref=g_in, dst_ref=g_out.at[dev], send_sem=sems[3].at[k], recv_sem=sems[4].at[k],
                device_id=peer, device_id_type=MESH))
            slot = g_out.at[jnp.bitwise_xor(dev, k + 1)]
            lands.append(pltpu.make_async_remote_copy(
                src_ref=slot, dst_ref=slot, send_sem=sems[3].at[k], recv_sem=sems[4].at[k],
                device_id=peer, device_id_type=MESH))
        return own, sends, lands

    def start(ins, outs, sems):
        for cp in halves(outs, sems, "mine"):
            cp.start()
        if gpack is not None:
            own, sends, _ = gains(ins, outs, sems)
            own.start()
            for cp in sends:
                cp.start()

    def finish(ins, outs, sems):
        for cp in halves(outs, sems, "sibling's"):
            cp.wait_recv()
        if gpack is not None:
            own, sends, lands = gains(ins, outs, sems)
            for cp in lands:
                cp.wait_recv()
            for cp in sends:
                cp.wait_send()
            own.wait()
        for cp in halves(outs, sems, "mine"):
            cp.wait_send()

    shapes = [jax.ShapeDtypeStruct(g.shape, g.dtype) for g in grads]
    sems = [pltpu.SemaphoreType.DMA((na,))] * 2
    if gpack is None:
        return _Ride(grads, shapes, sems, start, finish, {a: a for a in range(na)})
    return _Ride(list(grads) + [gpack], shapes + [jax.ShapeDtypeStruct((N_DEV,) + gpack.shape, gpack.dtype)],
                 sems + [pltpu.SemaphoreType.DMA, pltpu.SemaphoreType.DMA((N_DEV - 1,)), pltpu.SemaphoreType.DMA((N_DEV - 1,))],
                 start, finish, {a: a for a in range(na)})


def _pair_sum(place, grad, got, name):
    ns, r, cols = grad.shape
    hr = r // 2

    def body(place_ref, g_ref, r_ref, o_ref):
        o_ref[...] = (g_ref[...] + r_ref[...]).astype(BF16)

    return pl.pallas_call(
        body, name=name,
        grid_spec=pltpu.PrefetchScalarGridSpec(
            num_scalar_prefetch=1, grid=(ns,),
            in_specs=[pl.BlockSpec((None, hr, cols), lambda s, pr: (s, pr[1], 0)),
                      pl.BlockSpec((None, hr, cols), lambda s, pr: (s, 0, 0))],
            out_specs=pl.BlockSpec((None, hr, cols), lambda s, pr: (s, 0, 0))),
        out_shape=jax.ShapeDtypeStruct((ns, hr, cols), BF16),
        compiler_params=_params("arbitrary"),
    )(place, grad, got)


def _chip_sum(place, grad, got, others, name):
    ns, r, cols = grad.shape
    hr = r // 2
    nb = 2
    tr = hr // nb

    def body(place_ref, g_ref, r_ref, o3_ref, o_ref):
        acc = g_ref[...] + r_ref[...]
        for j in range(3):
            acc = acc + o3_ref[j].astype(F32)
        o_ref[...] = acc

    return pl.pallas_call(
        body, name=name,
        grid_spec=pltpu.PrefetchScalarGridSpec(
            num_scalar_prefetch=1, grid=(nb,),
            in_specs=[pl.BlockSpec((None, tr, cols), lambda i, pr: (pr[0], pr[1] * nb + i, 0)),
                      pl.BlockSpec((None, tr, cols), lambda i, pr: (pr[0], i, 0)),
                      pl.BlockSpec((3, tr, cols), lambda i, pr: (0, i, 0))],
            out_specs=pl.BlockSpec((tr, cols), lambda i, pr: (pr[1] * nb + i, 0))),
        out_shape=jax.ShapeDtypeStruct((r, cols), F32),
        compiler_params=_params("arbitrary"),
    )(place, grad, got, others)


def _pack_gains(parts, d):
    def body(*refs):
        ins, o_ref = refs[:-1], refs[-1]
        o_ref[...] = jnp.zeros_like(o_ref)
        for k, r in enumerate(ins):
            o_ref[k:k + 1, 0:r.shape[1]] = jnp.sum(r[...], axis=0, keepdims=True)

    return pl.pallas_call(
        body, name="pack_gains", out_shape=jax.ShapeDtypeStruct((GAIN_ROWS, d), F32),
    )(*parts)


def _adamw_math(w, g, m, v):
    m = ADAM_B1 * m + (1.0 - ADAM_B1) * g
    v = ADAM_B2 * v + (1.0 - ADAM_B2) * jnp.square(g)
    m_hat = m / (1.0 - ADAM_B1 ** ADAM_STEP)
    v_hat = v / (1.0 - ADAM_B2 ** ADAM_STEP)
    return -ADAM_LR * (m_hat / (jnp.sqrt(v_hat) + ADAM_EPS) + ADAM_WD * w), m, v


def _adamw(w, g, m, v, name):
    r, cols = w.shape
    tr = r // 4 if (r // 4) % 8 == 0 else r

    def body(w_ref, g_ref, m_ref, v_ref, d_ref, nm_ref, nv_ref):
        d_ref[...], nm_ref[...], nv_ref[...] = _adamw_math(w_ref[...], g_ref[...], m_ref[...], v_ref[...])

    tile = pl.BlockSpec((tr, cols), lambda i: (i, 0))
    return pl.pallas_call(
        body, name=name, grid=(r // tr,), in_specs=[tile] * 4, out_specs=[tile] * 3,
        out_shape=[jax.ShapeDtypeStruct((r, cols), F32)] * 3,
        compiler_params=_params("arbitrary"),
    )(w, g, m, v)


def _adamw_gain(gall, row, w, m, v, name):
    n = w.shape[1]

    def body(ga_ref, w_ref, m_ref, v_ref, g_ref, d_ref, nm_ref, nv_ref):
        g = ga_ref[0, row:row + 1, 0:n]
        for k in range(1, N_DEV):
            g = g + ga_ref[k, row:row + 1, 0:n]
        g_ref[...] = g
        d_ref[...], nm_ref[...], nv_ref[...] = _adamw_math(w_ref[...], g, m_ref[...], v_ref[...])

    return pl.pallas_call(
        body, name=name, out_shape=[jax.ShapeDtypeStruct((1, n), F32)] * 4,
    )(gall, w, m, v)


def kernel(x, norm_ffn1, ffn1_w_gate, ffn1_w_up, ffn1_w_down, norm_mix, w_in, ret_norm_gain, w_out, norm_ffn2, ffn2_w_gate, ffn2_w_up, ffn2_w_down, norm_final, loss_target, m_norm_ffn1, m_ffn1_w_gate, m_ffn1_w_up, m_ffn1_w_down, m_norm_mix, m_w_in, m_ret_norm_gain, m_w_out, m_norm_ffn2, m_ffn2_w_gate, m_ffn2_w_up, m_ffn2_w_down, m_norm_final, v_norm_ffn1, v_ffn1_w_gate, v_ffn1_w_up, v_ffn1_w_down, v_norm_mix, v_w_in, v_ret_norm_gain, v_w_out, v_norm_ffn2, v_ffn2_w_gate, v_ffn2_w_up, v_ffn2_w_down, v_norm_final):
    d = x.shape[-1]
    mats = [ffn1_w_gate, ffn1_w_up, ffn1_w_down, w_in, w_out, ffn2_w_gate, ffn2_w_up, ffn2_w_down]
    mats_m = [m_ffn1_w_gate, m_ffn1_w_up, m_ffn1_w_down, m_w_in, m_w_out, m_ffn2_w_gate, m_ffn2_w_up, m_ffn2_w_down]
    mats_v = [v_ffn1_w_gate, v_ffn1_w_up, v_ffn1_w_down, v_w_in, v_w_out, v_ffn2_w_gate, v_ffn2_w_up, v_ffn2_w_down]
    mat_names = ["ffn1_w_gate", "ffn1_w_up", "ffn1_w_down", "w_in", "w_out", "ffn2_w_gate", "ffn2_w_up", "ffn2_w_down"]
    gains = [norm_ffn1, norm_mix, ret_norm_gain, norm_ffn2, norm_final.reshape(1, d)]
    gains_m = [m_norm_ffn1, m_norm_mix, m_ret_norm_gain, m_norm_ffn2, m_norm_final.reshape(1, d)]
    gains_v = [v_norm_ffn1, v_norm_mix, v_ret_norm_gain, v_norm_ffn2, v_norm_final.reshape(1, d)]
    gain_names = ["norm_ffn1", "norm_mix", "ret_norm_gain", "norm_ffn2", "norm_final"]

    turned = lambda n: n.endswith(("w_gate", "w_up"))
    local = lambda a, n: jnp.swapaxes(a, 1, 2)[0] if turned(n) else a[0]
    back = lambda a, n: jnp.swapaxes(a[None], 1, 2) if turned(n) else a[None]
    shards = [local(w, n) for w, n in zip(mats, mat_names)]
    place = jnp.stack([2 * lax.axis_index("x") + lax.axis_index("y"), lax.axis_index("c")]).astype(jnp.int32)
    placed = [_place_shard(place, s, f"place_{n}") for s, n in zip(shards, mat_names)]
    loss_p, dx, shard_grads, gall = _step(x[0], loss_target[0], gains, placed, place)

    out_g, out_d, out_m, out_v = {}, {}, {}, {}
    for n, w, g, m, v in zip(mat_names, shards, shard_grads, mats_m, mats_v):
        dl, nm, nv = _adamw(w, g, local(m, n), local(v, n), f"adamw_{n}")
        out_g[n], out_d[n], out_m[n], out_v[n] = [back(a, n) for a in (g, dl, nm, nv)]
    for row, (n, w, m, v) in enumerate(zip(gain_names, gains, gains_m, gains_v)):
        res = _adamw_gain(gall, row, w, m, v, f"adamw_{n}")
        shape = (d,) if n == "norm_final" else w.shape
        out_g[n], out_d[n], out_m[n], out_v[n] = [r.reshape(shape) for r in res]

    loss = lax.psum(jnp.sum(loss_p), ("x", "y", "c"))
    order = ["norm_ffn1", "ffn1_w_gate", "ffn1_w_up", "ffn1_w_down", "norm_mix", "w_in", "ret_norm_gain", "w_out",
             "norm_ffn2", "ffn2_w_gate", "ffn2_w_up", "ffn2_w_down", "norm_final"]
    return (loss, dx[None], *[out_g[n] for n in order], *[out_d[n] for n in order],
            *[out_m[n] for n in order], *[out_v[n] for n in order])
```

```python
import functools
import math

import jax
import jax.numpy as jnp
from jax import lax
from jax.experimental import pallas as pl
from jax.experimental.pallas import tpu as pltpu

F32 = jnp.float32
BF16 = jnp.bfloat16
MESH = pl.DeviceIdType.MESH

NORM_EPS = 1e-6
GN_EPS = 1e-6
ROPE_BASE = 10000.0
RET_HEADS = 4
RET_DIM = 128
RET_WIDTH = 512
RET_CHUNK = 128
ATT_HEADS = 8
ATT_DIM = 64
ATT_WIDTH = 512
ATT_BLOCK = 128
DILATIONS = (1, 4, 16)
IN_COLS = 4 * RET_WIDTH + 3 * ATT_WIDTH
LANE = 128
N_SHARD = 4
ADAM_LR, ADAM_B1, ADAM_B2, ADAM_EPS, ADAM_WD, ADAM_STEP = 0.001, 0.9, 0.999, 1e-08, 0.01, 10

V7X_VMEM_BYTES = 64 * 1024 * 1024
VMEM_LIMIT = V7X_VMEM_BYTES - 8 * 1024 * 1024

NT = (((1,), (1,)), ((), ()))
TN = (((0,), (0,)), ((), ()))


def _params(*sem):
    return pltpu.CompilerParams(dimension_semantics=sem, vmem_limit_bytes=VMEM_LIMIT)


def _dot(a, b, dims=None):
    if dims is None:
        return jnp.dot(a, b, preferred_element_type=F32)
    return lax.dot_general(a, b, dims, preferred_element_type=F32)


def _sigmoid(x):
    return 1.0 / (1.0 + jnp.exp(-x))


def _load_weights(pairs, sems):
    copies = [pltpu.make_async_copy(src, dst, sems.at[k]) for k, (src, dst) in enumerate(pairs)]
    for cp in copies:
        cp.start()
    for cp in copies:
        cp.wait()


def _rows8(v):
    r, c = v.shape
    return v.reshape(r // 8, 8, c).sum(axis=0)


class _Ride:
    def __init__(self, inputs, out_shapes, sems, start, finish, aliases=None):
        self.inputs, self.out_shapes, self.sems = list(inputs), list(out_shapes), list(sems)
        self.start, self.finish, self.aliases = start, finish, dict(aliases or {})


def _pallas(body, rides, *, name, in_specs, out_specs, out_shape, args, grid=(), scratch_shapes=(), sem=()):
    rides = [r for r in (rides or []) if r is not None]
    n_in, n_out, n_scr = len(args), len(out_shape), len(scratch_shapes)
    hbm = pl.BlockSpec(memory_space=pl.ANY)
    r_in = [a for r in rides for a in r.inputs]
    r_out = [s for r in rides for s in r.out_shapes]
    r_sem = [s for r in rides for s in r.sems]
    aliases, spans, ki, ko, ks = {}, [], 0, 0, 0
    for r in rides:
        aliases.update({n_in + ki + i: n_out + ko + o for i, o in r.aliases.items()})
        spans.append((ki, ko, ks))
        ki, ko, ks = ki + len(r.inputs), ko + len(r.out_shapes), ks + len(r.sems)

    def wrapped(*refs):
        ins, rin = refs[:n_in], refs[n_in:n_in + len(r_in)]
        o0 = n_in + len(r_in)
        outs, rout = refs[o0:o0 + n_out], refs[o0 + n_out:o0 + n_out + len(r_out)]
        s0 = o0 + n_out + len(r_out)
        scr, rsem = refs[s0:s0 + n_scr], refs[s0 + n_scr:]
        part = lambda r, k: (rin[spans[k][0]:spans[k][0] + len(r.inputs)], rout[spans[k][1]:spans[k][1] + len(r.out_shapes)],
                             rsem[spans[k][2]:spans[k][2] + len(r.sems)])
        first = functools.reduce(jnp.logical_and, [pl.program_id(k) == 0 for k in range(len(grid))], True)
        last = functools.reduce(jnp.logical_and, [pl.program_id(k) == grid[k] - 1 for k in range(len(grid))], True)
        if rides:
            @pl.when(first)
            def _():
                for k, r in enumerate(rides):
                    r.start(*part(r, k))

        body(*ins, *outs, *scr)
        if rides:
            @pl.when(last)
            def _():
                for k, r in enumerate(rides):
                    r.finish(*part(r, k))

    res = pl.pallas_call(
        wrapped, name=name, grid=grid,
        in_specs=list(in_specs) + [hbm] * len(r_in), out_specs=list(out_specs) + [hbm] * len(r_out),
        out_shape=list(out_shape) + r_out, input_output_aliases=aliases,
        scratch_shapes=list(scratch_shapes) + r_sem,
        compiler_params=pltpu.CompilerParams(dimension_semantics=sem, vmem_limit_bytes=VMEM_LIMIT) if grid else None,
    )(*args, *r_in)
    extras = [list(res[n_out + ko:n_out + ko + len(r.out_shapes)]) for r, (_, ko, _) in zip(rides, spans)]
    return list(res[:n_out]), extras


def _run(ride, name):
    def body(*refs):
        n_in, n_out = len(ride.inputs), len(ride.out_shapes)
        parts = refs[:n_in], refs[n_in:n_in + n_out], refs[n_in + n_out:]
        ride.start(*parts)
        ride.finish(*parts)

    hbm = pl.BlockSpec(memory_space=pl.ANY)
    return list(pl.pallas_call(
        body, name=name, in_specs=[hbm] * len(ride.inputs), out_specs=[hbm] * len(ride.out_shapes),
        out_shape=ride.out_shapes, input_output_aliases=ride.aliases, scratch_shapes=ride.sems,
    )(*ride.inputs))


def _loss_head(hv, gain_ref, tg_ref, loss_ref, dgain_ref):
    d = hv.shape[1]
    r = lax.rsqrt(jnp.mean(hv * hv, axis=-1, keepdims=True) + NORM_EPS)
    xh = hv * r
    err = xh * gain_ref[...] - tg_ref[...]
    sq = _rows8(jnp.square(err))
    loss_ref[...] += 0.5 * functools.reduce(jnp.add, [sq[:, k * LANE:(k + 1) * LANE] for k in range(d // LANE)]) / d
    dy = err / d
    dgain_ref[...] += _rows8(dy * xh)
    dxh = dy * gain_ref[...]
    return r * (dxh - xh * jnp.mean(dxh * xh, axis=-1, keepdims=True))


V7X_MXU_TILE = 256
FFN_CHUNK_TILES = 3


def _hidden_chunks(f):
    step = FFN_CHUNK_TILES * V7X_MXU_TILE
    return [slice(s, min(s + step, f)) for s in range(0, f, step)]


def _flat(w):
    return w.reshape(w.shape[0] * w.shape[1], w.shape[2])


def _ffn_fwd(x, gain, wg, wu, wd, name, rides=None, head=None):
    t, d = x.shape
    wg, wu, wd = _flat(wg), _flat(wu), _flat(wd)
    f = wg.shape[0]
    tm = min(256, t)
    nh = 0 if head is None else 2

    def body(*refs):
        x_ref, gain_ref = refs[:2]
        wg_hbm, wu_hbm, wd_hbm, h_ref, xn_ref, g_ref, u_ref, a_ref = refs[2 + nh:10 + nh]
        wg_v, wu_v, wd_v, sems = refs[-4:]

        @pl.when(pl.program_id(0) == 0)
        def _():
            _load_weights([(wg_hbm, wg_v), (wu_hbm, wu_v), (wd_hbm, wd_v)], sems)
            if head is not None:
                refs[10 + nh][...] = jnp.zeros_like(refs[10 + nh])
                refs[11 + nh][...] = jnp.zeros_like(refs[11 + nh])

        xv = x_ref[...]
        r = lax.rsqrt(jnp.mean(xv * xv, axis=-1, keepdims=True) + NORM_EPS)
        xn = (xv * r * gain_ref[...]).astype(BF16)
        xn_ref[...] = xn
        acc = jnp.zeros((tm, d), F32)
        for c in _hidden_chunks(f):
            g = _dot(xn, wg_v[c, :], NT)
            u = _dot(xn, wu_v[c, :], NT)
            g_ref[:, c] = g.astype(BF16)
            u_ref[:, c] = u.astype(BF16)
            a = (g * _sigmoid(g) * u).astype(BF16)
            a_ref[:, c] = a
            acc = acc + _dot(a, wd_v[c, :])
        hv = xv + 0.5 * acc
        h_ref[...] = hv if head is None else _loss_head(hv, refs[2], refs[3], refs[10 + nh], refs[11 + nh])

    hbm = pl.BlockSpec(memory_space=pl.ANY)
    hid = pl.BlockSpec((tm, f), lambda i: (i, 0))
    tile = pl.BlockSpec((tm, d), lambda i: (i, 0))
    row = pl.BlockSpec((1, d), lambda i: (0, 0))
    sums = [] if head is None else [(pl.BlockSpec((8, LANE), lambda i: (0, 0)), jax.ShapeDtypeStruct((8, LANE), F32)),
                                    (pl.BlockSpec((8, d), lambda i: (0, 0)), jax.ShapeDtypeStruct((8, d), F32))]
    return _pallas(
        body, rides, name=name, grid=(t // tm,),
        in_specs=[tile, row] + ([] if head is None else [row, tile]) + [hbm, hbm, hbm],
        out_specs=[tile, tile, hid, hid, hid] + [s for s, _ in sums],
        out_shape=[jax.ShapeDtypeStruct((t, d), F32), jax.ShapeDtypeStruct((t, d), BF16)]
        + [jax.ShapeDtypeStruct((t, f), BF16)] * 3 + [s for _, s in sums],
        scratch_shapes=[pltpu.VMEM(wg.shape, BF16), pltpu.VMEM(wu.shape, BF16), pltpu.VMEM(wd.shape, BF16),
                        pltpu.SemaphoreType.DMA((3,))],
        sem=("arbitrary",), args=[x, gain] + ([] if head is None else list(head)) + [wg, wu, wd])


def _ffn_bwd_data(dy, x, gain, g, u, wg, wu, wd, name, rides=None):
    t, d = x.shape
    wg, wu, wd = _flat(wg), _flat(wu), _flat(wd)
    f = wg.shape[0]
    tm = min(256, t)

    def body(dy_ref, x_ref, gain_ref, g_ref, u_ref, wg_hbm, wu_hbm, wd_hbm, dx_ref, dg_ref, du_ref, dgain_ref,
             wg_v, wu_v, wd_v, sems):
        @pl.when(pl.program_id(0) == 0)
        def _():
            _load_weights([(wg_hbm, wg_v), (wu_hbm, wu_v), (wd_hbm, wd_v)], sems)
            dgain_ref[...] = jnp.zeros_like(dgain_ref)

        dyv = dy_ref[...]
        dyh = (0.5 * dyv).astype(BF16)
        dxn = jnp.zeros((tm, d), F32)
        chunks = _hidden_chunks(f)
        das = [_dot(dyh, wd_v[c, :], NT) for c in chunks]
        for c, da in zip(chunks, das):
            gj = g_ref[:, c].astype(F32)
            uj = u_ref[:, c].astype(F32)
            sig = _sigmoid(gj)
            dgj = (da * uj * (sig * (1.0 + gj * (1.0 - sig)))).astype(BF16)
            duj = (da * (gj * sig)).astype(BF16)
            dg_ref[:, c] = dgj
            du_ref[:, c] = duj
            dxn = dxn + _dot(dgj, wg_v[c, :]) + _dot(duj, wu_v[c, :])
        xv = x_ref[...]
        r = lax.rsqrt(jnp.mean(xv * xv, axis=-1, keepdims=True) + NORM_EPS)
        xh = xv * r
        dgain_ref[...] += _rows8(dxn * xh)
        dxh = dxn * gain_ref[...]
        dx_ref[...] = dyv + r * (dxh - xh * jnp.mean(dxh * xh, axis=-1, keepdims=True))

    hbm = pl.BlockSpec(memory_space=pl.ANY)
    tile = pl.BlockSpec((tm, d), lambda i: (i, 0))
    hid = pl.BlockSpec((tm, f), lambda i: (i, 0))
    return _pallas(
        body, rides, name=name, grid=(t // tm,),
        in_specs=[tile, tile, pl.BlockSpec((1, d), lambda i: (0, 0)), hid, hid, hbm, hbm, hbm],
        out_specs=[tile, hid, hid, pl.BlockSpec((8, d), lambda i: (0, 0))],
        out_shape=[jax.ShapeDtypeStruct((t, d), F32), jax.ShapeDtypeStruct((t, f), BF16),
                   jax.ShapeDtypeStruct((t, f), BF16), jax.ShapeDtypeStruct((8, d), F32)],
        scratch_shapes=[pltpu.VMEM(wg.shape, BF16), pltpu.VMEM(wu.shape, BF16), pltpu.VMEM(wd.shape, BF16),
                        pltpu.SemaphoreType.DMA((3,))],
        sem=("arbitrary",), args=[dy, x, gain, g, u, wg, wu, wd])


WGRAD_ROW_BLOCKS = 2


def _ffn_wgrad_down(a, dy, name, rides=None):
    t, d = dy.shape
    f = a.shape[1]
    fb = f // WGRAD_ROW_BLOCKS
    tk = min(1024, t)

    def body(dy_ref, a_ref, dwd_ref):
        @pl.when(pl.program_id(1) == 0)
        def _():
            dwd_ref[...] = jnp.zeros_like(dwd_ref)

        dwd_ref[...] += _dot(a_ref[...], (0.5 * dy_ref[...]).astype(BF16), TN)

    return _pallas(
        body, rides, name=name, grid=(WGRAD_ROW_BLOCKS, t // tk),
        in_specs=[pl.BlockSpec((tk, d), lambda j, k: (k, 0)), pl.BlockSpec((tk, fb), lambda j, k: (k, j))],
        out_specs=[pl.BlockSpec((fb, d), lambda j, k: (j, 0))],
        out_shape=[jax.ShapeDtypeStruct((f, d), F32)],
        sem=("arbitrary", "arbitrary"), args=[dy, a])


def _ffn_wgrad_gu(xn, dg, du, name, rides=None):
    t, d = xn.shape
    f = dg.shape[1]
    fb = f // WGRAD_ROW_BLOCKS
    tk = min(1024, t)

    def body(xn_ref, dg_ref, du_ref, dwg_ref, dwu_ref):
        @pl.when(pl.program_id(1) == 0)
        def _():
            dwg_ref[...] = jnp.zeros_like(dwg_ref)
            dwu_ref[...] = jnp.zeros_like(dwu_ref)

        xnv = xn_ref[...]
        dwg_ref[...] += _dot(dg_ref[...], xnv, TN)
        dwu_ref[...] += _dot(du_ref[...], xnv, TN)

    hid = pl.BlockSpec((tk, fb), lambda j, k: (k, j))
    out = pl.BlockSpec((fb, d), lambda j, k: (j, 0))
    return _pallas(
        body, rides, name=name, grid=(WGRAD_ROW_BLOCKS, t // tk),
        in_specs=[pl.BlockSpec((tk, d), lambda j, k: (k, 0)), hid, hid],
        out_specs=[out, out], out_shape=[jax.ShapeDtypeStruct((f, d), F32)] * 2,
        sem=("arbitrary", "arbitrary"), args=[xn, dg, du])


def _tn_matmul(a, b, bn, name):
    t, m = a.shape
    n = b.shape[1]
    tk = min(2048, t)

    def body(a_ref, b_ref, o_ref):
        @pl.when(pl.program_id(1) == 0)
        def _():
            o_ref[...] = jnp.zeros_like(o_ref)

        o_ref[...] += _dot(a_ref[...].astype(BF16), b_ref[...].astype(BF16), TN)

    return pl.pallas_call(
        body, name=name, grid=(n // bn, t // tk),
        in_specs=[pl.BlockSpec((tk, m), lambda j, k: (k, 0)), pl.BlockSpec((tk, bn), lambda j, k: (k, j))],
        out_specs=pl.BlockSpec((None, m, bn), lambda j, k: (j, 0, 0)),
        out_shape=jax.ShapeDtypeStruct((n // bn, m, bn), F32),
        compiler_params=_params("arbitrary", "arbitrary"),
    )(a, b)


def _chunk_scratch(tm, w):
    return pltpu.VMEM((w // LANE, tm, LANE), F32)


def _regroup_store(cbuf, out_ref, dil):
    n = out_ref.shape[1]
    for g in range(dil):
        for k in range(cbuf.shape[0]):
            rows = cbuf[k] if dil == 1 else cbuf[k, pl.ds(g, n, stride=dil), :]
            out_ref[g, :, k * LANE:(k + 1) * LANE] = rows.astype(out_ref.dtype)


def _natural_rows(ref, dil, cbuf):
    if dil == 1:
        return ref[0].astype(F32)
    n = ref.shape[1]
    for g in range(dil):
        for k in range(cbuf.shape[0]):
            cbuf[k, pl.ds(g, n, stride=dil), :] = ref[g, :, k * LANE:(k + 1) * LANE].astype(F32)
    return jnp.concatenate([cbuf[k] for k in range(cbuf.shape[0])], axis=1)


def _inproj_fwd(h, gain, win):
    t, d = h.shape
    ns, _, cs = win.shape
    tm = min(512, t)
    rw, aw = 4 * RET_WIDTH, 3 * ATT_WIDTH

    def body(h_ref, gain_ref, w_ref, xn_ref, ur_ref, *rest):
        a_refs, abuf = rest[:-1], rest[-1]
        hv = h_ref[...]
        r = lax.rsqrt(jnp.mean(hv * hv, axis=-1, keepdims=True) + NORM_EPS)
        xn = (hv * r * gain_ref[...]).astype(BF16)
        xn_ref[...] = xn
        for j in range(ns):
            res = _dot(xn, w_ref[j])
            for k in range(cs // LANE):
                chunk = j * (cs // LANE) + k
                piece = res[:, k * LANE:(k + 1) * LANE]
                if chunk < rw // LANE:
                    ur_ref[:, chunk * LANE:(chunk + 1) * LANE] = piece
                else:
                    abuf[chunk - rw // LANE] = piece
        for dil, a_ref in zip(DILATIONS, a_refs):
            _regroup_store(abuf, a_ref, dil)

    return pl.pallas_call(
        body, name="inproj_fwd", grid=(t // tm,),
        in_specs=[pl.BlockSpec((tm, d), lambda i: (i, 0)), pl.BlockSpec((1, d), lambda i: (0, 0)),
                  pl.BlockSpec(win.shape, lambda i: (0, 0, 0))],
        out_specs=[pl.BlockSpec((tm, d), lambda i: (i, 0)), pl.BlockSpec((tm, rw), lambda i: (i, 0))]
        + [pl.BlockSpec((dil, tm // dil, aw), lambda i: (0, i, 0)) for dil in DILATIONS],
        out_shape=[jax.ShapeDtypeStruct((t, d), BF16), jax.ShapeDtypeStruct((t, rw), F32)]
        + [jax.ShapeDtypeStruct((dil, t // dil, aw), BF16) for dil in DILATIONS],
        scratch_shapes=[_chunk_scratch(tm, aw)],
        compiler_params=_params("arbitrary"),
    )(h, gain, win)


def _inproj_bwd(pieces, h, gain, dres, win):
    t, d = h.shape
    ns, _, cs = win.shape
    pw = pieces[0].shape[1]
    tm = min(512, t)
    npc = len(pieces)

    def body(*refs):
        p_refs = refs[:npc]
        h_ref, gain_ref, dres_ref, w_ref, dh_ref, du_ref, dgain_ref = refs[npc:]

        @pl.when(pl.program_id(0) == 0)
        def _():
            dgain_ref[...] = jnp.zeros_like(dgain_ref)

        for k in range(npc):
            du_ref[:, k * pw:(k + 1) * pw] = p_refs[k][...]
        dxn = jnp.zeros((tm, d), F32)
        for j in range(ns):
            dxn = dxn + _dot(du_ref[:, j * cs:(j + 1) * cs], w_ref[j], NT)
        hv = h_ref[...]
        r = lax.rsqrt(jnp.mean(hv * hv, axis=-1, keepdims=True) + NORM_EPS)
        xh = hv * r
        dgain_ref[...] += _rows8(dxn * xh)
        dxh = dxn * gain_ref[...]
        dh_ref[...] = dres_ref[...] + r * (dxh - xh * jnp.mean(dxh * xh, axis=-1, keepdims=True))

    tile = pl.BlockSpec((tm, d), lambda i: (i, 0))
    return pl.pallas_call(
        body, name="inproj_bwd", grid=(t // tm,),
        in_specs=[pl.BlockSpec((tm, pw), lambda i: (i, 0))] * npc + [
            tile, pl.BlockSpec((1, d), lambda i: (0, 0)), tile, pl.BlockSpec(win.shape, lambda i: (0, 0, 0))],
        out_specs=[tile, pl.BlockSpec((tm, npc * pw), lambda i: (i, 0)), pl.BlockSpec((8, d), lambda i: (0, 0))],
        out_shape=[jax.ShapeDtypeStruct((t, d), F32), jax.ShapeDtypeStruct((t, npc * pw), BF16),
                   jax.ShapeDtypeStruct((8, d), F32)],
        compiler_params=_params("arbitrary"),
    )(*pieces, h, gain, dres, win)


def _outproj_fwd(h, mix_r, mix_a, wo):
    t, d = h.shape
    hw = mix_r.shape[1]
    tm = min(512, t)

    def body(h_ref, mr_ref, ma_ref, w_ref, o_ref):
        o_ref[...] = h_ref[...] + _dot(mr_ref[...], w_ref[0:hw, :]) + _dot(ma_ref[...], w_ref[hw:2 * hw, :])

    tile = pl.BlockSpec((tm, d), lambda i: (i, 0))
    half = pl.BlockSpec((tm, hw), lambda i: (i, 0))
    return pl.pallas_call(
        body, name="outproj_fwd", grid=(t // tm,),
        in_specs=[tile, half, half, pl.BlockSpec(wo.shape, lambda i: (0, 0))],
        out_specs=tile, out_shape=jax.ShapeDtypeStruct((t, d), F32),
        compiler_params=_params("arbitrary"),
    )(h, mix_r, mix_a, wo)


def _outproj_bwd(dh, wo, rides=None):
    t, d = dh.shape
    hw = wo.shape[0] // 2
    tm = min(512, t)

    def body(dh_ref, w_ref, dr_ref, da_ref):
        dhb = dh_ref[...].astype(BF16)
        dr_ref[...] = _dot(dhb, w_ref[0:hw, :], NT)
        da_ref[...] = _dot(dhb, w_ref[hw:2 * hw, :], NT)

    half = pl.BlockSpec((tm, hw), lambda i: (i, 0))
    return _pallas(
        body, rides, name="outproj_bwd", grid=(t // tm,),
        in_specs=[pl.BlockSpec((tm, d), lambda i: (i, 0)), pl.BlockSpec(wo.shape, lambda i: (0, 0))],
        out_specs=[half, half],
        out_shape=[jax.ShapeDtypeStruct((t, hw), F32), jax.ShapeDtypeStruct((t, hw), F32)],
        sem=("arbitrary",), args=[dh, wo])


def _retention_tables(t):
    pos = jnp.arange(t, dtype=F32)
    inv_freq = ROPE_BASE ** (-jnp.arange(0, RET_DIM, 2, dtype=F32) / RET_DIM)
    ang = jnp.repeat(pos[:, None] * inv_freq[None, :], 2, axis=-1)
    c = RET_CHUNK
    log_g = jnp.log(1.0 - 2.0 ** (-5.0 - jnp.arange(RET_HEADS, dtype=F32)))
    idx = jnp.arange(c, dtype=F32)
    rel = idx[:, None] - idx[None, :]
    decay = jnp.where(rel >= 0, jnp.exp(log_g[:, None, None] * jnp.maximum(rel, 0.0)), 0.0)
    zeta = jnp.exp(log_g[:, None] * (c - 1 - idx)[None, :])
    xi = jnp.exp(log_g[:, None] * (idx + 1)[None, :])
    gc = jnp.exp(log_g * c)
    wide = lambda v: jnp.broadcast_to(v[:, :, None], (RET_HEADS, c, LANE))
    return (jnp.cos(ang), jnp.sin(ang), decay, wide(zeta), wide(xi),
            jnp.broadcast_to(gc[:, None, None], (RET_HEADS, c, LANE)))


def _rot(v):
    lane = lax.broadcasted_iota(jnp.int32, v.shape, 1)
    nxt = pltpu.roll(v, LANE - 1, 1)
    prv = pltpu.roll(v, 1, 1)
    return jnp.where(lane % 2 == 0, -nxt, prv)


def _ret_specs(tr, rev, nt):
    ti = (lambda i: nt - 1 - i) if rev else (lambda i: i)
    col = lambda blk: pl.BlockSpec((tr, RET_WIDTH), lambda i: (ti(i), blk))
    tab = pl.BlockSpec((tr, LANE), lambda i: (ti(i), 0))
    head = pl.BlockSpec((RET_HEADS, RET_CHUNK, LANE), lambda i: (0, 0, 0))
    return col, tab, head


def _ret_chunks(tr, rev=False):
    order = list(range(tr // RET_CHUNK))
    return [(pl.ds(ci * RET_CHUNK, RET_CHUNK), slice(h * RET_DIM, (h + 1) * RET_DIM), h)
            for h in range(RET_HEADS) for ci in (reversed(order) if rev else order)]


def _ret_operands(items, q_ref, k_ref, v_ref, cos_ref, sin_ref, zeta_ref):
    scale = RET_DIM ** -0.5
    qbs, kbs, vbs, kzs = [], [], [], []
    for sl, hs, h in items:
        cs, sn = cos_ref[sl, :], sin_ref[sl, :]
        q, k = q_ref[sl, hs], k_ref[sl, hs]
        kr = (k * cs + _rot(k) * sn) * scale
        qbs.append((q * cs + _rot(q) * sn).astype(BF16))
        kbs.append(kr.astype(BF16))
        vbs.append(v_ref[sl, hs].astype(BF16))
        kzs.append((kr * zeta_ref[h]).astype(BF16))
    return qbs, kbs, vbs, kzs


def _ret_states(items, state, steps, gc_ref):
    cur, befores = {}, []
    for (sl, hs, h), step in zip(items, steps):
        st = cur[h] if h in cur else state[h]
        befores.append(st)
        cur[h] = st * gc_ref[h] + step
    for h, st in cur.items():
        state[h] = st
    return befores


def _ret_fwd(u, gain, tabs):
    t = u.shape[0]
    tr = min(512, t)
    nt = t // tr
    cos, sin, decay, zeta, xi, gc = tabs
    scale = RET_DIM ** -0.5

    def body(q_ref, k_ref, v_ref, gt_ref, cos_ref, sin_ref, gain_ref, dec_ref, zeta_ref, xi_ref, gc_ref,
             raw_ref, mix_ref, state):
        @pl.when(pl.program_id(0) == 0)
        def _():
            state[...] = jnp.zeros_like(state)

        items = _ret_chunks(tr)
        n = range(len(items))
        qbs, kbs, vbs, kzs = _ret_operands(items, q_ref, k_ref, v_ref, cos_ref, sin_ref, zeta_ref)
        ss = [_dot(qbs[i], kbs[i], NT) for i in n]
        kvs = [_dot(kzs[i], vbs[i], TN) for i in n]
        befores = _ret_states(items, state, kvs, gc_ref)
        intra = [_dot((ss[i] * dec_ref[items[i][2]]).astype(BF16), vbs[i]) for i in n]
        inter = [_dot(qbs[i], befores[i].astype(BF16)) for i in n]
        for i, (sl, hs, h) in enumerate(items):
            o = intra[i] + inter[i] * xi_ref[h]
            raw_ref[sl, hs] = o
            mu = jnp.mean(o, axis=-1, keepdims=True)
            var = jnp.mean(jnp.square(o - mu), axis=-1, keepdims=True)
            y = (o - mu) * lax.rsqrt(var + GN_EPS) * gain_ref[:, hs]
            gt = gt_ref[sl, hs]
            mix_ref[sl, hs] = (y * (gt * _sigmoid(gt))).astype(BF16)

    col, tab, head = _ret_specs(tr, False, nt)
    out = pl.BlockSpec((tr, RET_WIDTH), lambda i: (i, 0))
    return pl.pallas_call(
        body, name="ret_fwd", grid=(nt,),
        in_specs=[col(0), col(1), col(2), col(3), tab, tab, pl.BlockSpec((1, RET_WIDTH), lambda i: (0, 0)),
                  head, head, head, head],
        out_specs=[out, out],
        out_shape=[jax.ShapeDtypeStruct((t, RET_WIDTH), F32), jax.ShapeDtypeStruct((t, RET_WIDTH), BF16)],
        scratch_shapes=[pltpu.VMEM((RET_HEADS, RET_DIM, RET_DIM), F32)],
        compiler_params=_params("arbitrary"),
    )(u, u, u, u, cos, sin, gain, decay, zeta, xi, gc)


def _ret_bwd_q(dmix, raw, u, gain, tabs, rides=None):
    t = u.shape[0]
    tr = min(512, t)
    nt = t // tr
    cos, sin, decay, zeta, xi, gc = tabs
    scale = RET_DIM ** -0.5

    def body(dm_ref, raw_ref, q_ref, k_ref, v_ref, gt_ref, cos_ref, sin_ref, gain_ref, dec_ref, zeta_ref, xi_ref, gc_ref,
             dq_ref, dgt_ref, dret_ref, dgain_ref, state):
        @pl.when(pl.program_id(0) == 0)
        def _():
            state[...] = jnp.zeros_like(state)
            dgain_ref[...] = jnp.zeros_like(dgain_ref)

        items = _ret_chunks(tr)
        n_items = range(len(items))
        qbs, kbs, vbs, kzs = _ret_operands(items, q_ref, k_ref, v_ref, cos_ref, sin_ref, zeta_ref)
        dos, dgains = [], {}
        for sl, hs, h in items:
            o = raw_ref[sl, hs]
            mu = jnp.mean(o, axis=-1, keepdims=True)
            var = jnp.mean(jnp.square(o - mu), axis=-1, keepdims=True)
            rs = lax.rsqrt(var + GN_EPS)
            n = (o - mu) * rs
            gt = gt_ref[sl, hs]
            sig = _sigmoid(gt)
            dout = dm_ref[sl, hs]
            gain_h = gain_ref[:, hs]
            dgt_ref[sl, hs] = (dout * (n * gain_h) * (sig * (1.0 + gt * (1.0 - sig)))).astype(BF16)
            dy = dout * (gt * sig)
            dgains[h] = dgains[h] + _rows8(dy * n) if h in dgains else _rows8(dy * n)
            dn = dy * gain_h
            do = rs * (dn - jnp.mean(dn, axis=-1, keepdims=True) - n * jnp.mean(dn * n, axis=-1, keepdims=True))
            dret_ref[sl, hs] = do
            dos.append(do)
        for h, dg in dgains.items():
            dgain_ref[:, h * RET_DIM:(h + 1) * RET_DIM] += dg
        dss = [_dot(dos[i].astype(BF16), vbs[i], NT) for i in n_items]
        kvs = [_dot(kzs[i], vbs[i], TN) for i in n_items]
        befores = _ret_states(items, state, kvs, gc_ref)
        intra = [_dot((dss[i] * dec_ref[items[i][2]]).astype(BF16), kbs[i]) for i in n_items]
        inter = [_dot((dos[i] * xi_ref[items[i][2]]).astype(BF16), befores[i].astype(BF16), NT) for i in n_items]
        for i, (sl, hs, h) in enumerate(items):
            dqr = intra[i] + inter[i]
            dq_ref[sl, hs] = (dqr * cos_ref[sl, :] - _rot(dqr * sin_ref[sl, :])).astype(BF16)

    col, tab, head = _ret_specs(tr, False, nt)
    out = pl.BlockSpec((tr, RET_WIDTH), lambda i: (i, 0))
    return _pallas(
        body, rides, name="ret_bwd_q", grid=(nt,),
        in_specs=[out, out, col(0), col(1), col(2), col(3), tab, tab, pl.BlockSpec((1, RET_WIDTH), lambda i: (0, 0)),
                  head, head, head, head],
        out_specs=[out, out, out, pl.BlockSpec((8, RET_WIDTH), lambda i: (0, 0))],
        out_shape=[jax.ShapeDtypeStruct((t, RET_WIDTH), BF16), jax.ShapeDtypeStruct((t, RET_WIDTH), BF16),
                   jax.ShapeDtypeStruct((t, RET_WIDTH), F32), jax.ShapeDtypeStruct((8, RET_WIDTH), F32)],
        scratch_shapes=[pltpu.VMEM((RET_HEADS, RET_DIM, RET_DIM), F32)],
        sem=("arbitrary",), args=[dmix, raw, u, u, u, u, cos, sin, gain, decay, zeta, xi, gc])


def _ret_bwd_kv(dret, u, tabs, rides=None):
    t = u.shape[0]
    tr = min(512, t)
    nt = t // tr
    cos, sin, decay, zeta, xi, gc = tabs
    scale = RET_DIM ** -0.5

    def body(do_ref, q_ref, k_ref, v_ref, cos_ref, sin_ref, dec_ref, zeta_ref, xi_ref, gc_ref, dk_ref, dv_ref, gst):
        @pl.when(pl.program_id(0) == 0)
        def _():
            gst[...] = jnp.zeros_like(gst)

        items = _ret_chunks(tr, rev=True)
        n = range(len(items))
        qbs, kbs, vbs, kzs = _ret_operands(items, q_ref, k_ref, v_ref, cos_ref, sin_ref, zeta_ref)
        dos = [do_ref[sl, hs] for sl, hs, h in items]
        dobs = [do.astype(BF16) for do in dos]
        ss = [_dot(qbs[i], kbs[i], NT) for i in n]
        dss = [_dot(dobs[i], vbs[i], NT) for i in n]
        steps = [_dot(qbs[i], (dos[i] * xi_ref[items[i][2]]).astype(BF16), TN) for i in n]
        afters = [g.astype(BF16) for g in _ret_states(items, gst, steps, gc_ref)]
        dvs = [_dot((ss[i] * dec_ref[items[i][2]]).astype(BF16), dobs[i], TN) + _dot(kzs[i], afters[i]) for i in n]
        dks = [_dot((dss[i] * dec_ref[items[i][2]]).astype(BF16), qbs[i], TN) for i in n]
        dkz = [_dot(vbs[i], afters[i], NT) for i in n]
        for i, (sl, hs, h) in enumerate(items):
            dv_ref[sl, hs] = dvs[i].astype(BF16)
            dkr = (dks[i] + dkz[i] * zeta_ref[h]) * scale
            dk_ref[sl, hs] = (dkr * cos_ref[sl, :] - _rot(dkr * sin_ref[sl, :])).astype(BF16)

    col, tab, head = _ret_specs(tr, True, nt)
    out = pl.BlockSpec((tr, RET_WIDTH), lambda i: (nt - 1 - i, 0))
    return _pallas(
        body, rides, name="ret_bwd_kv", grid=(nt,),
        in_specs=[out, col(0), col(1), col(2), tab, tab, head, head, head, head],
        out_specs=[out, out],
        out_shape=[jax.ShapeDtypeStruct((t, RET_WIDTH), BF16), jax.ShapeDtypeStruct((t, RET_WIDTH), BF16)],
        scratch_shapes=[pltpu.VMEM((RET_HEADS, RET_DIM, RET_DIM), F32)],
        sem=("arbitrary",), args=[dret, u, u, u, cos, sin, decay, zeta, xi, gc])


PAIRS = ATT_WIDTH // LANE
ATT_Q_BLK, ATT_K_BLK, ATT_V_BLK = 0, PAIRS, 2 * PAIRS
STAT_LANES = ATT_DIM // 2


def _att_tiles(t, dil):
    sub = t // dil
    tq = min(512, sub)
    return sub, tq, sub // tq, tq // ATT_BLOCK


def _att_in_specs(tq, qb, ti):
    cur = lambda off: pl.BlockSpec((None, tq, LANE), lambda g, p, i: (g, ti(i), off + p))
    prev = lambda off: pl.BlockSpec((None, ATT_BLOCK, LANE), lambda g, p, i: (g, jnp.maximum(ti(i) * qb - 1, 0), off + p))
    return [cur(ATT_Q_BLK), cur(ATT_K_BLK), prev(ATT_K_BLK), cur(ATT_V_BLK), prev(ATT_V_BLK)]


def _band_mask():
    key = lax.broadcasted_iota(jnp.int32, (2 * ATT_BLOCK, 2 * ATT_BLOCK), 0)
    qry = lax.broadcasted_iota(jnp.int32, (2 * ATT_BLOCK, 2 * ATT_BLOCK), 1) % ATT_BLOCK
    dist = qry + ATT_BLOCK - key
    return (dist >= 0) & (dist <= ATT_BLOCK), key >= ATT_BLOCK


def _head0_lanes():
    return lax.broadcasted_iota(jnp.int32, (ATT_BLOCK, LANE), 1) < ATT_DIM


def _stack_heads(v, head0):
    zero = jnp.zeros((), v.dtype)
    return jnp.concatenate([jnp.where(head0, v, zero), jnp.where(head0, zero, v)], axis=0)


def _unstack_heads(v, head0):
    return jnp.where(head0, v[0:ATT_BLOCK], v[ATT_BLOCK:])


def _att_fwd(ua, dil):
    sub = ua.shape[1]
    _, tq, nq, qb = _att_tiles(sub * dil, dil)

    def body(q_ref, kc_ref, kp_ref, vc_ref, vp_ref, o_ref, l_ref, kx, vx):
        tile = pl.program_id(2)
        kx[0:ATT_BLOCK, :] = kp_ref[...]
        kx[ATT_BLOCK:, :] = kc_ref[...]
        vx[0:ATT_BLOCK, :] = vp_ref[...]
        vx[ATT_BLOCK:, :] = vc_ref[...]
        band, cur_keys = _band_mask()
        head0 = _head0_lanes()
        blocks = range(qb)
        rows = [slice(b * ATT_BLOCK, (b + 1) * ATT_BLOCK) for b in blocks]
        keys = [slice(b * ATT_BLOCK, (b + 2) * ATT_BLOCK) for b in blocks]
        sts = [_dot(kx[keys[b], :], _stack_heads(q_ref[rows[b], :] * jnp.asarray(ATT_DIM ** -0.5, BF16), head0), NT)
               for b in blocks]
        pts, lses = [], []
        for b in blocks:
            mask = band if b > 0 else band & (cur_keys | (tile > 0))
            st = jnp.where(mask, sts[b], -1e30)
            m = jnp.max(st, axis=0, keepdims=True)
            ex = jnp.exp(st - m)
            den = jnp.sum(ex, axis=0, keepdims=True)
            pts.append((ex * (1.0 / den)).astype(BF16))
            lses.append(m + jnp.log(den))
        outs = [_dot(pts[b], vx[keys[b], :], TN) for b in blocks]
        for b in blocks:
            o_ref[rows[b], :] = _unstack_heads(outs[b], head0).astype(BF16)
            cols = [jnp.broadcast_to(lses[b][:, e * ATT_BLOCK:(e + 1) * ATT_BLOCK], (ATT_BLOCK, LANE)).T for e in range(2)]
            l_ref[rows[b], :] = jnp.where(head0, cols[0], cols[1])

    out = pl.BlockSpec((None, tq, LANE), lambda g, p, i: (g, i, p))
    return pl.pallas_call(
        body, name=f"att_fwd_d{dil}", grid=(dil, PAIRS, nq),
        in_specs=_att_in_specs(tq, qb, lambda i: i),
        out_specs=[out, out],
        out_shape=[jax.ShapeDtypeStruct((dil, sub, ATT_WIDTH), BF16), jax.ShapeDtypeStruct((dil, sub, ATT_WIDTH), F32)],
        scratch_shapes=[pltpu.VMEM((tq + ATT_BLOCK, LANE), BF16)] * 2,
        compiler_params=_params("arbitrary", "arbitrary", "arbitrary"),
    )(ua, ua, ua, ua, ua)


def _regrouped_spec(tm, dil, w):
    return pl.BlockSpec((dil, tm // dil, w), lambda i: (0, i, 0))


def _att_combine(outs, lses, t):
    w = ATT_WIDTH
    tm = min(512, t)
    nb = len(outs)

    def body(*refs):
        o_refs, l_refs = refs[:nb], refs[nb:2 * nb]
        mix_ref, att_ref, lse_ref, buf = refs[2 * nb:]
        ls = [_natural_rows(r, dil, buf) for r, dil in zip(l_refs, DILATIONS)]
        m = functools.reduce(jnp.maximum, ls)
        ws = [jnp.exp(l - m) for l in ls]
        den = functools.reduce(jnp.add, ws)
        att = functools.reduce(jnp.add, [(wt / den) * _natural_rows(r, dil, buf) for wt, r, dil in zip(ws, o_refs, DILATIONS)])
        att_ref[...] = att
        mix_ref[...] = att.astype(BF16)
        lse_ref[...] = m + jnp.log(den)

    tile = pl.BlockSpec((tm, w), lambda i: (i, 0))
    regrouped = [_regrouped_spec(tm, dil, w) for dil in DILATIONS]
    return pl.pallas_call(
        body, name="att_combine", grid=(t // tm,),
        in_specs=regrouped * 2, out_specs=[tile, tile, tile],
        out_shape=[jax.ShapeDtypeStruct((t, w), BF16), jax.ShapeDtypeStruct((t, w), F32), jax.ShapeDtypeStruct((t, w), F32)],
        scratch_shapes=[_chunk_scratch(tm, w)],
        compiler_params=_params("arbitrary"),
    )(*outs, *lses)


def _att_bwd_prep(datt, att, lse):
    t, w = datt.shape
    tm = min(512, t)

    def body(da_ref, at_ref, l_ref, *rest):
        outs, dbuf, sbuf = rest[:-2], rest[-2], rest[-1]
        dav = da_ref[...]
        prod = dav * at_ref[...]
        lane = lax.broadcasted_iota(jnp.int32, (tm, LANE), 1)
        for k in range(w // LANE):
            cols = slice(k * LANE, (k + 1) * LANE)
            dbuf[k] = dav[:, cols]
            delta = jnp.concatenate(
                [jnp.broadcast_to(jnp.sum(prod[:, k * LANE + e * ATT_DIM:k * LANE + (e + 1) * ATT_DIM], axis=-1, keepdims=True),
                                  (tm, ATT_DIM)) for e in range(LANE // ATT_DIM)], axis=1)
            sbuf[k] = jnp.where(lane % ATT_DIM < STAT_LANES, l_ref[:, cols], delta)
        for k, dil in enumerate(DILATIONS):
            _regroup_store(dbuf, outs[2 * k], dil)
            _regroup_store(sbuf, outs[2 * k + 1], dil)

    tile = pl.BlockSpec((tm, w), lambda i: (i, 0))
    res = pl.pallas_call(
        body, name="att_bwd_prep", grid=(t // tm,),
        in_specs=[tile] * 3,
        out_specs=[_regrouped_spec(tm, dil, w) for dil in DILATIONS for _ in range(2)],
        out_shape=[jax.ShapeDtypeStruct((dil, t // dil, w), dt) for dil in DILATIONS for dt in (BF16, F32)],
        scratch_shapes=[_chunk_scratch(tm, w)] * 2,
        compiler_params=_params("arbitrary"),
    )(datt, att, lse)
    return [(res[2 * k], res[2 * k + 1]) for k in range(len(DILATIONS))]


def _att_bwd(ua, da, stat, dil, rides=None):
    sub = ua.shape[1]
    _, tq, nq, qb = _att_tiles(sub * dil, dil)
    scale = ATT_DIM ** -0.5

    def body(q_ref, kc_ref, kp_ref, vc_ref, vp_ref, da_ref, st_ref, dq_ref, dk_ref, dv_ref, kx, vx, ck, cv):
        step = pl.program_id(2)
        tile = nq - 1 - step

        @pl.when(step == 0)
        def _():
            ck[...] = jnp.zeros_like(ck)
            cv[...] = jnp.zeros_like(cv)

        kx[0:ATT_BLOCK, :] = kp_ref[...]
        kx[ATT_BLOCK:, :] = kc_ref[...]
        vx[0:ATT_BLOCK, :] = vp_ref[...]
        vx[ATT_BLOCK:, :] = vc_ref[...]
        band, cur_keys = _band_mask()
        head0 = _head0_lanes()
        blocks = range(qb)
        rows = [slice(b * ATT_BLOCK, (b + 1) * ATT_BLOCK) for b in blocks]
        keys = [slice(b * ATT_BLOCK, (b + 2) * ATT_BLOCK) for b in blocks]
        qqs = [_stack_heads(q_ref[rows[b], :] * jnp.asarray(scale, BF16), head0) for b in blocks]
        dds = [_stack_heads(da_ref[rows[b], :], head0) for b in blocks]
        sts = [_dot(kx[keys[b], :], qqs[b], NT) for b in blocks]
        dpts = [_dot(vx[keys[b], :], dds[b], NT) for b in blocks]
        pts, dsts = [], []
        for b in blocks:
            mask = band if b > 0 else band & (cur_keys | (tile > 0))
            stat = st_ref[rows[b], :].T
            row = lambda k: jnp.concatenate([stat[e * ATT_DIM + k:e * ATT_DIM + k + 1, :] for e in range(2)], axis=1)
            pt = jnp.where(mask, jnp.exp(sts[b] - row(0)), 0.0)
            dsts.append((pt * (dpts[b] - row(STAT_LANES))).astype(BF16))
            pts.append(pt.astype(BF16))
        dqs = [_dot(dsts[b], kx[keys[b], :], TN) for b in blocks]
        dkbs = [_dot(dsts[b], qqs[b]) for b in blocks]
        dvbs = [_dot(pts[b], dds[b]) for b in blocks]
        for b in blocks:
            dq_ref[rows[b], :] = (_unstack_heads(dqs[b], head0) * scale).astype(BF16)
        for b in blocks[1:]:
            dk_ref[rows[b - 1], :] = (dkbs[b - 1][ATT_BLOCK:] + dkbs[b][0:ATT_BLOCK]).astype(BF16)
            dv_ref[rows[b - 1], :] = (dvbs[b - 1][ATT_BLOCK:] + dvbs[b][0:ATT_BLOCK]).astype(BF16)
        before_k, before_v = dkbs[0][0:ATT_BLOCK], dvbs[0][0:ATT_BLOCK]
        open_k, open_v = dkbs[-1][ATT_BLOCK:], dvbs[-1][ATT_BLOCK:]
        last = slice(tq - ATT_BLOCK, tq)
        dk_ref[last, :] = (open_k + ck[...]).astype(BF16)
        dv_ref[last, :] = (open_v + cv[...]).astype(BF16)
        ck[...] = before_k
        cv[...] = before_v

    ti = lambda i: nq - 1 - i
    out = pl.BlockSpec((None, tq, LANE), lambda g, p, i: (g, ti(i), p))
    shape = jax.ShapeDtypeStruct((dil, sub, ATT_WIDTH), BF16)
    return _pallas(
        body, rides, name=f"att_bwd_d{dil}", grid=(dil, PAIRS, nq),
        in_specs=_att_in_specs(tq, qb, ti) + [out, out],
        out_specs=[out, out, out], out_shape=[shape] * 3,
        scratch_shapes=[pltpu.VMEM((tq + ATT_BLOCK, LANE), BF16)] * 2 + [pltpu.VMEM((ATT_BLOCK, LANE), F32)] * 2,
        sem=("arbitrary", "arbitrary", "arbitrary"), args=[ua, ua, ua, ua, ua, da, stat])


def _att_bwd_sum(parts, t):
    w = ATT_WIDTH
    tm = min(512, t)
    nk = len(parts[0])

    def body(*refs):
        ins, outs, buf = refs[:-nk - 1], refs[-nk - 1:-1], refs[-1]
        for k in range(nk):
            acc = None
            for b, dil in enumerate(DILATIONS):
                rows = _natural_rows(ins[b * nk + k], dil, buf)
                acc = rows if acc is None else acc + rows
            outs[k][...] = acc.astype(BF16)

    tile = pl.BlockSpec((tm, w), lambda i: (i, 0))
    return pl.pallas_call(
        body, name="att_bwd_sum", grid=(t // tm,),
        in_specs=[_regrouped_spec(tm, dil, w) for dil in DILATIONS for _ in range(nk)], out_specs=[tile] * nk,
        out_shape=[jax.ShapeDtypeStruct((t, w), BF16)] * nk,
        scratch_shapes=[_chunk_scratch(tm, w)],
        compiler_params=_params("arbitrary"),
    )(*[a for p in parts for a in p])


class _Reduction:
    def __init__(self, place, names, grads):
        self.place, self.names, self.grads = place, names, grads

    def pair(self):
        return _pair_ride(self.grads)

    def chips(self, got):
        self.got = got
        return _chip_ride([_pair_sum(self.place, g, r, f"pair_sum_{n}") for g, r, n in zip(self.grads, got, self.names)])

    def halves(self, others):
        return [_chip_sum(self.place, g, r, o, f"chip_sum_{n}")
                for g, r, o, n in zip(self.grads, self.got, others, self.names)]


def _step(x, target, gains, w, place=None):
    t = x.shape[0]
    ex = place is not None
    g_ffn1, g_mix, g_ret, g_ffn2, g_fin = gains
    w = list(w)
    tabs = _retention_tables(t)
    red = lambda names, grads: _Reduction(place, names, grads) if ex else None
    ride = lambda r: [r] if ex else None

    if ex:
        w[0:3] = _run(_gather_ride(w[0:3]), "gather_ffn1_weights")
    (h1, xn1, ga1, ua1, act1), rest = _ffn_fwd(x, g_ffn1, *w[0:3], "ffn1_fwd", ride(_gather_ride(w[3:])) if ex else None)
    if ex:
        w[3:] = rest[0]
    wg1, wu1, wd1, win, wo, wg2, wu2, wd2 = w
    wo2 = wo.reshape(wo.shape[0] * wo.shape[1], wo.shape[2])
    xnm, u, *uas = _inproj_fwd(h1, g_mix, win)
    raw, mix_r = _ret_fwd(u, g_ret, tabs)
    branches = [_att_fwd(ua, dil) for ua, dil in zip(uas, DILATIONS)]
    mix_a, att, lse = _att_combine([b[0] for b in branches], [b[1] for b in branches], t)
    h2 = _outproj_fwd(h1, mix_r, mix_a, wo2)
    (dh3, xn2, ga2, ua2, act2, loss_p, dg_fin), _ = _ffn_fwd(h2, g_ffn2, wg2, wu2, wd2, "ffn2_fwd", head=(g_fin, target))

    (dwd2,), _ = _ffn_wgrad_down(act2, dh3, "ffn2_wgrad_down")
    dwd2 = dwd2.reshape(wd2.shape)
    r_d2 = red(["ffn2_w_down"], [dwd2])
    (dh2, dga2, dua2, dg_ffn2), e = _ffn_bwd_data(dh3, h2, g_ffn2, ga2, ua2, wg2, wu2, wd2, "ffn2_bwd",
                                                  ex and [r_d2.pair()])
    (dwg2, dwu2), e = _ffn_wgrad_gu(xn2, dga2, dua2, "ffn2_wgrad_gu", ex and [r_d2.chips(e[0])])
    dwg2, dwu2 = dwg2.reshape(wg2.shape), dwu2.reshape(wu2.shape)
    r_gu2 = red(["ffn2_w_gate", "ffn2_w_up"], [dwg2, dwu2])
    (dmix_r, dmix_a), e = _outproj_bwd(dh2, wo2, ex and [r_gu2.pair(), _finish_ride(r_d2.halves(e[0]))])
    if ex:
        got_gu2, (dwd2,) = e
    hw = RET_WIDTH // (wo.shape[1])
    dwo = jnp.concatenate([_tn_matmul(mix_r, dh2, dh2.shape[1], "wo_grad_r").reshape(hw, wo.shape[1], wo.shape[2]),
                           _tn_matmul(mix_a, dh2, dh2.shape[1], "wo_grad_a").reshape(hw, wo.shape[1], wo.shape[2])])
    r_wo = red(["w_out"], [dwo])
    (dq_r, dgt_r, dret, dg_ret), e = _ret_bwd_q(dmix_r, raw, u, g_ret, tabs, ex and [r_gu2.chips(got_gu2)])
    (dk_r, dv_r), e = _ret_bwd_kv(dret, u, tabs, ex and [r_wo.pair(), _finish_ride(r_gu2.halves(e[0]))])
    if ex:
        got_wo, (dwg2, dwu2) = e
    prep = _att_bwd_prep(dmix_a, att, lse)
    p1, e = _att_bwd(uas[0], *prep[0], DILATIONS[0], ex and [r_wo.chips(got_wo)])
    p4, e = _att_bwd(uas[1], *prep[1], DILATIONS[1], ex and [_finish_ride(r_wo.halves(e[0]))])
    if ex:
        (dwo,), = e
    p16, _ = _att_bwd(uas[2], *prep[2], DILATIONS[2])
    dq_a, dk_a, dv_a = _att_bwd_sum([p1, p4, p16], t)
    dh1, du, dg_mix = _inproj_bwd([dq_r, dk_r, dv_r, dgt_r, dq_a, dk_a, dv_a], h1, g_mix, dh2, win)
    dwin = _tn_matmul(xnm, du, win.shape[2], "win_grad")
    r_in = red(["w_in"], [dwin])
    (dwd1,), e = _ffn_wgrad_down(act1, dh1, "ffn1_wgrad_down", ex and [r_in.pair()])
    dwd1 = dwd1.reshape(wd1.shape)
    r_d1 = red(["ffn1_w_down"], [dwd1])
    (dx, dga1, dua1, dg_ffn1), e = _ffn_bwd_data(dh1, x, g_ffn1, ga1, ua1, wg1, wu1, wd1, "ffn1_bwd",
                                                  ex and [r_in.chips(e[0]), r_d1.pair()])
    (dwg1, dwu1), e = _ffn_wgrad_gu(xn1, dga1, dua1, "ffn1_wgrad_gu",
                                    ex and [_finish_ride(r_in.halves(e[0])), r_d1.chips(e[1])])
    dwg1, dwu1 = dwg1.reshape(wg1.shape), dwu1.reshape(wu1.shape)
    gain_parts = [dg_ffn1, dg_mix, dg_ret, dg_ffn2, dg_fin]
    if not ex:
        return loss_p, dx, [dwg1, dwu1, dwd1, dwin, dwo, dwg2, dwu2, dwd2], gain_parts
    (dwin,), oth_d1 = e
    r_gu1 = red(["ffn1_w_gate", "ffn1_w_up"], [dwg1, dwu1])
    got = _run(r_gu1.pair(), "pair_exchange_ffn1_gate_up")
    oth = _run(r_gu1.chips(got), "chip_exchange_ffn1_gate_up")
    last = r_gu1.halves(oth) + r_d1.halves(oth_d1)
    dwg1, dwu1, dwd1, gall = _run(_finish_ride(last, _pack_gains(gain_parts, x.shape[1])), "finish_exchange_ffn1")
    return loss_p, dx, [dwg1, dwu1, dwd1, dwin, dwo, dwg2, dwu2, dwd2], gall


N_DEV = 8
GAIN_ROWS = 8


def _place():
    x, y, c = lax.axis_index("x"), lax.axis_index("y"), lax.axis_index("c")
    chips = [(1 - x, y), (x, 1 - y), (1 - x, 1 - y)]
    return x, y, c, chips


def _hbm_specs(n):
    return [pl.BlockSpec(memory_space=pl.ANY)] * n


def _place_shard(place, w, name):
    r, cols = w.shape
    tr = r // 4

    def body(place_ref, w_ref, o_ref):
        o_ref[...] = w_ref[...].astype(BF16)

    return pl.pallas_call(
        body, name=name,
        grid_spec=pltpu.PrefetchScalarGridSpec(
            num_scalar_prefetch=1, grid=(r // tr,),
            in_specs=[pl.BlockSpec((tr, cols), lambda i, pr: (i, 0))],
            out_specs=pl.BlockSpec((None, tr, cols), lambda i, pr: (pr[0], i, 0))),
        out_shape=jax.ShapeDtypeStruct((N_SHARD, r, cols), BF16),
        compiler_params=_params("arbitrary"),
    )(place, w)


def _gather_ride(bufs):
    na = len(bufs)

    def legs(outs, sems):
        send_sem, recv_sem, fsend_sem, frecv_sem = sems
        x, y, c, chips = _place()

        def half(a, idx, which):
            hr = outs[a].shape[1] // 2
            return outs[a].at[idx, pl.ds(which * hr, hr)]

        def ici(a, j, idx):
            px, py = chips[j]
            return pltpu.make_async_remote_copy(
                src_ref=half(a, idx, c), dst_ref=half(a, idx, c),
                send_sem=send_sem.at[a, j], recv_sem=recv_sem.at[a, j], device_id=(px, py, c), device_id_type=MESH)

        def d2d(a, j, idx, which):
            return pltpu.make_async_remote_copy(
                src_ref=half(a, idx, which), dst_ref=half(a, idx, which),
                send_sem=fsend_sem.at[a, j], recv_sem=frecv_sem.at[a, j], device_id=(x, y, 1 - c), device_id_type=MESH)

        return 2 * x + y, c, chips, ici, d2d

    def start(ins, outs, sems):
        me, _, _, ici, _ = legs(outs, sems)
        for a in range(na):
            for j in range(3):
                ici(a, j, me).start()

    def finish(ins, outs, sems):
        me, c, chips, ici, d2d = legs(outs, sems)
        passed = []
        for a in range(na):
            for j, (px, py) in enumerate(chips):
                ici(a, j, 2 * px + py).wait_recv()
                cp = d2d(a, j, 2 * px + py, c)
                cp.start()
                passed.append(cp)
        for a in range(na):
            for j, (px, py) in enumerate(chips):
                d2d(a, j, 2 * px + py, 1 - c).wait_recv()
        for a in range(na):
            for j in range(3):
                ici(a, j, me).wait_send()
        for cp in passed:
            cp.wait_send()

    return _Ride(bufs, [jax.ShapeDtypeStruct(b.shape, b.dtype) for b in bufs], [pltpu.SemaphoreType.DMA((na, 3))] * 4,
                 start, finish, {a: a for a in range(na)})


def _pair_ride(grads):
    na = len(grads)

    def copies(ins, outs, sems):
        send_sem, recv_sem = sems
        x, y, c, _ = _place()
        res = []
        for a in range(na):
            hr = ins[a].shape[1] // 2
            res.append(pltpu.make_async_remote_copy(
                src_ref=ins[a].at[:, pl.ds((1 - c) * hr, hr)], dst_ref=outs[a],
                send_sem=send_sem.at[a], recv_sem=recv_sem.at[a], device_id=(x, y, 1 - c), device_id_type=MESH))
        return res

    def start(ins, outs, sems):
        for cp in copies(ins, outs, sems):
            cp.start()

    def finish(ins, outs, sems):
        for cp in copies(ins, outs, sems):
            cp.wait()

    return _Ride(grads, [jax.ShapeDtypeStruct((g.shape[0], g.shape[1] // 2, g.shape[2]), g.dtype) for g in grads],
                 [pltpu.SemaphoreType.DMA((na,))] * 2, start, finish)


def _chip_ride(sums):
    na = len(sums)

    def copies(ins, outs, sems):
        send_sem, recv_sem = sems
        x, y, c, chips = _place()
        res = []
        for a in range(na):
            for j, (px, py) in enumerate(chips):
                res.append(pltpu.make_async_remote_copy(
                    src_ref=ins[a].at[2 * px + py], dst_ref=outs[a].at[j],
                    send_sem=send_sem.at[a, j], recv_sem=recv_sem.at[a, j], device_id=(px, py, c), device_id_type=MESH))
        return res

    def start(ins, outs, sems):
        for cp in copies(ins, outs, sems):
            cp.start()

    def finish(ins, outs, sems):
        for cp in copies(ins, outs, sems):
            cp.wait()

    return _Ride(sums, [jax.ShapeDtypeStruct((3,) + s.shape[1:], s.dtype) for s in sums],
                 [pltpu.SemaphoreType.DMA((na, 3))] * 2, start, finish)


def _finish_ride(grads, gpack=None):
    na = len(grads)

    def halves(outs, sems, which):
        x, y, c, _ = _place()
        res = []
        for a in range(na):
            hr = outs[a].shape[0] // 2
            rows = outs[a].at[pl.ds((c if which == "mine" else 1 - c) * hr, hr)]
            res.append(pltpu.make_async_remote_copy(
                src_ref=rows, dst_ref=rows, send_sem=sems[0].at[a], recv_sem=sems[1].at[a],
                device_id=(x, y, 1 - c), device_id_type=MESH))
        return res

    def gains(ins, outs, sems):
        x, y, c, _ = _place()
        dev = 4 * x + 2 * y + c
        g_in, g_out = ins[na], outs[na]
        own = pltpu.make_async_copy(g_in, g_out.at[dev], sems[2])
        sends, lands = [], []
        for k in range(N_DEV - 1):
            bx, by, bc = (k + 1) // 4, ((k + 1) // 2) % 2, (k + 1) % 2
            peer = (jnp.bitwise_xor(x, bx), jnp.bitwise_xor(y, by), jnp.bitwise_xor(c, bc))
            sends.append(pltpu.make_async_remote_copy(
                src_ref=g_in, dst_ref=g_out.at[dev], send_sem=sems[3].at[k], recv_sem=sems[4].at[k],
                device_id=peer, device_id_type=MESH))
            slot = g_out.at[jnp.bitwise_xor(dev, k + 1)]
            lands.append(pltpu.make_async_remote_copy(
                src_ref=slot, dst_ref=slot, send_sem=sems[3].at[k], recv_sem=sems[4].at[k],
                device_id=peer, device_id_type=MESH))
        return own, sends, lands

    def start(ins, outs, sems):
        for cp in halves(outs, sems, "mine"):
            cp.start()
        if gpack is not None:
            own, sends, _ = gains(ins, outs, sems)
            own.start()
            for cp in sends:
                cp.start()

    def finish(ins, outs, sems):
        for cp in halves(outs, sems, "sibling's"):
            cp.wait_recv()
        if gpack is not None:
            own, sends, lands = gains(ins, outs, sems)
            for cp in lands:
                cp.wait_recv()
            for cp in sends:
                cp.wait_send()
            own.wait()
        for cp in halves(outs, sems, "mine"):
            cp.wait_send()

    shapes = [jax.ShapeDtypeStruct(g.shape, g.dtype) for g in grads]
    sems = [pltpu.SemaphoreType.DMA((na,))] * 2
    if gpack is None:
        return _Ride(grads, shapes, sems, start, finish, {a: a for a in range(na)})
    return _Ride(list(grads) + [gpack], shapes + [jax.ShapeDtypeStruct((N_DEV,) + gpack.shape, gpack.dtype)],
                 sems + [pltpu.SemaphoreType.DMA, pltpu.SemaphoreType.DMA((N_DEV - 1,)), pltpu.SemaphoreType.DMA((N_DEV - 1,))],
                 start, finish, {a: a for a in range(na)})


def _pair_sum(place, grad, got, name):
    ns, r, cols = grad.shape
    hr = r // 2

    def body(place_ref, g_ref, r_ref, o_ref):
        o_ref[...] = (g_ref[...] + r_ref[...]).astype(BF16)

    return pl.pallas_call(
        body, name=name,
        grid_spec=pltpu.PrefetchScalarGridSpec(
            num_scalar_prefetch=1, grid=(ns,),
            in_specs=[pl.BlockSpec((None, hr, cols), lambda s, pr: (s, pr[1], 0)),
                      pl.BlockSpec((None, hr, cols), lambda s, pr: (s, 0, 0))],
            out_specs=pl.BlockSpec((None, hr, cols), lambda s, pr: (s, 0, 0))),
        out_shape=jax.ShapeDtypeStruct((ns, hr, cols), BF16),
        compiler_params=_params("arbitrary"),
    )(place, grad, got)


def _chip_sum(place, grad, got, others, name):
    ns, r, cols = grad.shape
    hr = r // 2
    nb = 2
    tr = hr // nb

    def body(place_ref, g_ref, r_ref, o3_ref, o_ref):
        acc = g_ref[...] + r_ref[...]
        for j in range(3):
            acc = acc + o3_ref[j].astype(F32)
        o_ref[...] = acc

    return pl.pallas_call(
        body, name=name,
        grid_spec=pltpu.PrefetchScalarGridSpec(
            num_scalar_prefetch=1, grid=(nb,),
            in_specs=[pl.BlockSpec((None, tr, cols), lambda i, pr: (pr[0], pr[1] * nb + i, 0)),
                      pl.BlockSpec((None, tr, cols), lambda i, pr: (pr[0], i, 0)),
                      pl.BlockSpec((3, tr, cols), lambda i, pr: (0, i, 0))],
            out_specs=pl.BlockSpec((tr, cols), lambda i, pr: (pr[1] * nb + i, 0))),
        out_shape=jax.ShapeDtypeStruct((r, cols), F32),
        compiler_params=_params("arbitrary"),
    )(place, grad, got, others)


def _pack_gains(parts, d):
    def body(*refs):
        ins, o_ref = refs[:-1], refs[-1]
        o_ref[...] = jnp.zeros_like(o_ref)
        for k, r in enumerate(ins):
            o_ref[k:k + 1, 0:r.shape[1]] = jnp.sum(r[...], axis=0, keepdims=True)

    return pl.pallas_call(
        body, name="pack_gains", out_shape=jax.ShapeDtypeStruct((GAIN_ROWS, d), F32),
    )(*parts)


def _adamw_math(w, g, m, v):
    m = ADAM_B1 * m + (1.0 - ADAM_B1) * g
    v = ADAM_B2 * v + (1.0 - ADAM_B2) * jnp.square(g)
    m_hat = m / (1.0 - ADAM_B1 ** ADAM_STEP)
    v_hat = v / (1.0 - ADAM_B2 ** ADAM_STEP)
    return -ADAM_LR * (m_hat / (jnp.sqrt(v_hat) + ADAM_EPS) + ADAM_WD * w), m, v


def _adamw(w, g, m, v, name):
    r, cols = w.shape
    tr = r // 4 if (r // 4) % 8 == 0 else r

    def body(w_ref, g_ref, m_ref, v_ref, d_ref, nm_ref, nv_ref):
        d_ref[...], nm_ref[...], nv_ref[...] = _adamw_math(w_ref[...], g_ref[...], m_ref[...], v_ref[...])

    tile = pl.BlockSpec((tr, cols), lambda i: (i, 0))
    return pl.pallas_call(
        body, name=name, grid=(r // tr,), in_specs=[tile] * 4, out_specs=[tile] * 3,
        out_shape=[jax.ShapeDtypeStruct((r, cols), F32)] * 3,
        compiler_params=_params("arbitrary"),
    )(w, g, m, v)


def _adamw_gain(gall, row, w, m, v, name):
    n = w.shape[1]

    def body(ga_ref, w_ref, m_ref, v_ref, g_ref, d_ref, nm_ref, nv_ref):
        g = ga_ref[0, row:row + 1, 0:n]
        for k in range(1, N_DEV):
            g = g + ga_ref[k, row:row + 1, 0:n]
        g_ref[...] = g
        d_ref[...], nm_ref[...], nv_ref[...] = _adamw_math(w_ref[...], g, m_ref[...], v_ref[...])

    return pl.pallas_call(
        body, name=name, out_shape=[jax.ShapeDtypeStruct((1, n), F32)] * 4,
    )(gall, w, m, v)


def kernel(x, norm_ffn1, ffn1_w_gate, ffn1_w_up, ffn1_w_down, norm_mix, w_in, ret_norm_gain, w_out, norm_ffn2, ffn2_w_gate, ffn2_w_up, ffn2_w_down, norm_final, loss_target, m_norm_ffn1, m_ffn1_w_gate, m_ffn1_w_up, m_ffn1_w_down, m_norm_mix, m_w_in, m_ret_norm_gain, m_w_out, m_norm_ffn2, m_ffn2_w_gate, m_ffn2_w_up, m_ffn2_w_down, m_norm_final, v_norm_ffn1, v_ffn1_w_gate, v_ffn1_w_up, v_ffn1_w_down, v_norm_mix, v_w_in, v_ret_norm_gain, v_w_out, v_norm_ffn2, v_ffn2_w_gate, v_ffn2_w_up, v_ffn2_w_down, v_norm_final):
    d = x.shape[-1]
    mats = [ffn1_w_gate, ffn1_w_up, ffn1_w_down, w_in, w_out, ffn2_w_gate, ffn2_w_up, ffn2_w_down]
    mats_m = [m_ffn1_w_gate, m_ffn1_w_up, m_ffn1_w_down, m_w_in, m_w_out, m_ffn2_w_gate, m_ffn2_w_up, m_ffn2_w_down]
    mats_v = [v_ffn1_w_gate, v_ffn1_w_up, v_ffn1_w_down, v_w_in, v_w_out, v_ffn2_w_gate, v_ffn2_w_up, v_ffn2_w_down]
    mat_names = ["ffn1_w_gate", "ffn1_w_up", "ffn1_w_down", "w_in", "w_out", "ffn2_w_gate", "ffn2_w_up", "ffn2_w_down"]
    gains = [norm_ffn1, norm_mix, ret_norm_gain, norm_ffn2, norm_final.reshape(1, d)]
    gains_m = [m_norm_ffn1, m_norm_mix, m_ret_norm_gain, m_norm_ffn2, m_norm_final.reshape(1, d)]
    gains_v = [v_norm_ffn1, v_norm_mix, v_ret_norm_gain, v_norm_ffn2, v_norm_final.reshape(1, d)]
    gain_names = ["norm_ffn1", "norm_mix", "ret_norm_gain", "norm_ffn2", "norm_final"]

    turned = lambda n: n.endswith(("w_gate", "w_up"))
    local = lambda a, n: jnp.swapaxes(a, 1, 2)[0] if turned(n) else a[0]
    back = lambda a, n: jnp.swapaxes(a[None], 1, 2) if turned(n) else a[None]
    shards = [local(w, n) for w, n in zip(mats, mat_names)]
    place = jnp.stack([2 * lax.axis_index("x") + lax.axis_index("y"), lax.axis_index("c")]).astype(jnp.int32)
    placed = [_place_shard(place, s, f"place_{n}") for s, n in zip(shards, mat_names)]
    loss_p, dx, shard_grads, gall = _step(x[0], loss_target[0], gains, placed, place)

    out_g, out_d, out_m, out_v = {}, {}, {}, {}
    for n, w, g, m, v in zip(mat_names, shards, shard_grads, mats_m, mats_v):
        dl, nm, nv = _adamw(w, g, local(m, n), local(v, n), f"adamw_{n}")
        out_g[n], out_d[n], out_m[n], out_v[n] = [back(a, n) for a in (g, dl, nm, nv)]
    for row, (n, w, m, v) in enumerate(zip(gain_names, gains, gains_m, gains_v)):
        res = _adamw_gain(gall, row, w, m, v, f"adamw_{n}")
        shape = (d,) if n == "norm_final" else w.shape
        out_g[n], out_d[n], out_m[n], out_v[n] = [r.reshape(shape) for r in res]

    loss = lax.psum(jnp.sum(loss_p), ("x", "y", "c"))
    order = ["norm_ffn1", "ffn1_w_gate", "ffn1_w_up", "ffn1_w_down", "norm_mix", "w_in", "ret_norm_gain", "w_out",
             "norm_ffn2", "ffn2_w_gate", "ffn2_w_up", "ffn2_w_down", "norm_final"]
    return (loss, dx[None], *[out_g[n] for n in order], *[out_d[n] for n in order],
            *[out_m[n] for n in order], *[out_v[n] for n in order])
```

```python
import functools
import math

import jax
import jax.numpy as jnp
from jax import lax
from jax.experimental import pallas as pl
from jax.experimental.pallas import tpu as pltpu

F32 = jnp.float32
BF16 = jnp.bfloat16
MESH = pl.DeviceIdType.MESH

NORM_EPS = 1e-6
GN_EPS = 1e-6
ROPE_BASE = 10000.0
RET_HEADS = 4
RET_DIM = 128
RET_WIDTH = 512
RET_CHUNK = 128
ATT_HEADS = 8
ATT_DIM = 64
ATT_WIDTH = 512
ATT_BLOCK = 128
DILATIONS = (1, 4, 16)
IN_COLS = 4 * RET_WIDTH + 3 * ATT_WIDTH
LANE = 128
N_SHARD = 4
ADAM_LR, ADAM_B1, ADAM_B2, ADAM_EPS, ADAM_WD, ADAM_STEP = 0.001, 0.9, 0.999, 1e-08, 0.01, 10

V7X_VMEM_BYTES = 64 * 1024 * 1024
VMEM_LIMIT = V7X_VMEM_BYTES - 8 * 1024 * 1024

NT = (((1,), (1,)), ((), ()))
TN = (((0,), (0,)), ((), ()))


def _params(*sem):
    return pltpu.CompilerParams(dimension_semantics=sem, vmem_limit_bytes=VMEM_LIMIT)


def _dot(a, b, dims=None):
    if dims is None:
        return jnp.dot(a, b, preferred_element_type=F32)
    return lax.dot_general(a, b, dims, preferred_element_type=F32)


def _sigmoid(x):
    return 1.0 / (1.0 + jnp.exp(-x))


def _load_weights(pairs, sems):
    copies = [pltpu.make_async_copy(src, dst, sems.at[k]) for k, (src, dst) in enumerate(pairs)]
    for cp in copies:
        cp.start()
    for cp in copies:
        cp.wait()


def _rows8(v):
    r, c = v.shape
    return v.reshape(r // 8, 8, c).sum(axis=0)


class _Ride:
    def __init__(self, inputs, out_shapes, sems, start, finish, aliases=None):
        self.inputs, self.out_shapes, self.sems = list(inputs), list(out_shapes), list(sems)
        self.start, self.finish, self.aliases = start, finish, dict(aliases or {})


def _pallas(body, rides, *, name, in_specs, out_specs, out_shape, args, grid=(), scratch_shapes=(), sem=()):
    rides = [r for r in (rides or []) if r is not None]
    n_in, n_out, n_scr = len(args), len(out_shape), len(scratch_shapes)
    hbm = pl.BlockSpec(memory_space=pl.ANY)
    r_in = [a for r in rides for a in r.inputs]
    r_out = [s for r in rides for s in r.out_shapes]
    r_sem = [s for r in rides for s in r.sems]
    aliases, spans, ki, ko, ks = {}, [], 0, 0, 0
    for r in rides:
        aliases.update({n_in + ki + i: n_out + ko + o for i, o in r.aliases.items()})
        spans.append((ki, ko, ks))
        ki, ko, ks = ki + len(r.inputs), ko + len(r.out_shapes), ks + len(r.sems)

    def wrapped(*refs):
        ins, rin = refs[:n_in], refs[n_in:n_in + len(r_in)]
        o0 = n_in + len(r_in)
        outs, rout = refs[o0:o0 + n_out], refs[o0 + n_out:o0 + n_out + len(r_out)]
        s0 = o0 + n_out + len(r_out)
        scr, rsem = refs[s0:s0 + n_scr], refs[s0 + n_scr:]
        part = lambda r, k: (rin[spans[k][0]:spans[k][0] + len(r.inputs)], rout[spans[k][1]:spans[k][1] + len(r.out_shapes)],
                             rsem[spans[k][2]:spans[k][2] + len(r.sems)])
        first = functools.reduce(jnp.logical_and, [pl.program_id(k) == 0 for k in range(len(grid))], True)
        last = functools.reduce(jnp.logical_and, [pl.program_id(k) == grid[k] - 1 for k in range(len(grid))], True)
        if rides:
            @pl.when(first)
            def _():
                for k, r in enumerate(rides):
                    r.start(*part(r, k))

        body(*ins, *outs, *scr)
        if rides:
            @pl.when(last)
            def _():
                for k, r in enumerate(rides):
                    r.finish(*part(r, k))

    res = pl.pallas_call(
        wrapped, name=name, grid=grid,
        in_specs=list(in_specs) + [hbm] * len(r_in), out_specs=list(out_specs) + [hbm] * len(r_out),
        out_shape=list(out_shape) + r_out, input_output_aliases=aliases,
        scratch_shapes=list(scratch_shapes) + r_sem,
        compiler_params=pltpu.CompilerParams(dimension_semantics=sem, vmem_limit_bytes=VMEM_LIMIT) if grid else None,
    )(*args, *r_in)
    extras = [list(res[n_out + ko:n_out + ko + len(r.out_shapes)]) for r, (_, ko, _) in zip(rides, spans)]
    return list(res[:n_out]), extras


def _run(ride, name):
    def body(*refs):
        n_in, n_out = len(ride.inputs), len(ride.out_shapes)
        parts = refs[:n_in], refs[n_in:n_in + n_out], refs[n_in + n_out:]
        ride.start(*parts)
        ride.finish(*parts)

    hbm = pl.BlockSpec(memory_space=pl.ANY)
    return list(pl.pallas_call(
        body, name=name, in_specs=[hbm] * len(ride.inputs), out_specs=[hbm] * len(ride.out_shapes),
        out_shape=ride.out_shapes, input_output_aliases=ride.aliases, scratch_shapes=ride.sems,
    )(*ride.inputs))


def _loss_head(hv, gain_ref, tg_ref, loss_ref, dgain_ref):
    d = hv.shape[1]
    r = lax.rsqrt(jnp.mean(hv * hv, axis=-1, keepdims=True) + NORM_EPS)
    xh = hv * r
    err = xh * gain_ref[...] - tg_ref[...]
    sq = _rows8(jnp.square(err))
    loss_ref[...] += 0.5 * functools.reduce(jnp.add, [sq[:, k * LANE:(k + 1) * LANE] for k in range(d // LANE)]) / d
    dy = err / d
    dgain_ref[...] += _rows8(dy * xh)
    dxh = dy * gain_ref[...]
    return r * (dxh - xh * jnp.mean(dxh * xh, axis=-1, keepdims=True))


V7X_MXU_TILE = 256
FFN_CHUNK_TILES = 3


def _hidden_chunks(f):
    step = FFN_CHUNK_TILES * V7X_MXU_TILE
    return [slice(s, min(s + step, f)) for s in range(0, f, step)]


def _flat(w):
    return w.reshape(w.shape[0] * w.shape[1], w.shape[2])


def _ffn_fwd(x, gain, wg, wu, wd, name, rides=None, head=None):
    t, d = x.shape
    wg, wu, wd = _flat(wg), _flat(wu), _flat(wd)
    f = wg.shape[0]
    tm = min(256, t)
    nh = 0 if head is None else 2

    def body(*refs):
        x_ref, gain_ref = refs[:2]
        wg_hbm, wu_hbm, wd_hbm, h_ref, xn_ref, g_ref, u_ref, a_ref = refs[2 + nh:10 + nh]
        wg_v, wu_v, wd_v, sems = refs[-4:]

        @pl.when(pl.program_id(0) == 0)
        def _():
            _load_weights([(wg_hbm, wg_v), (wu_hbm, wu_v), (wd_hbm, wd_v)], sems)
            if head is not None:
                refs[10 + nh][...] = jnp.zeros_like(refs[10 + nh])
                refs[11 + nh][...] = jnp.zeros_like(refs[11 + nh])

        xv = x_ref[...]
        r = lax.rsqrt(jnp.mean(xv * xv, axis=-1, keepdims=True) + NORM_EPS)
        xn = (xv * r * gain_ref[...]).astype(BF16)
        xn_ref[...] = xn
        acc = jnp.zeros((tm, d), F32)
        for c in _hidden_chunks(f):
            g = _dot(xn, wg_v[c, :], NT)
            u = _dot(xn, wu_v[c, :], NT)
            g_ref[:, c] = g.astype(BF16)
            u_ref[:, c] = u.astype(BF16)
            a = (g * _sigmoid(g) * u).astype(BF16)
            a_ref[:, c] = a
            acc = acc + _dot(a, wd_v[c, :])
        hv = xv + 0.5 * acc
        h_ref[...] = hv if head is None else _loss_head(hv, refs[2], refs[3], refs[10 + nh], refs[11 + nh])

    hbm = pl.BlockSpec(memory_space=pl.ANY)
    hid = pl.BlockSpec((tm, f), lambda i: (i, 0))
    tile = pl.BlockSpec((tm, d), lambda i: (i, 0))
    row = pl.BlockSpec((1, d), lambda i: (0, 0))
    sums = [] if head is None else [(pl.BlockSpec((8, LANE), lambda i: (0, 0)), jax.ShapeDtypeStruct((8, LANE), F32)),
                                    (pl.BlockSpec((8, d), lambda i: (0, 0)), jax.ShapeDtypeStruct((8, d), F32))]
    return _pallas(
        body, rides, name=name, grid=(t // tm,),
        in_specs=[tile, row] + ([] if head is None else [row, tile]) + [hbm, hbm, hbm],
        out_specs=[tile, tile, hid, hid, hid] + [s for s, _ in sums],
        out_shape=[jax.ShapeDtypeStruct((t, d), F32), jax.ShapeDtypeStruct((t, d), BF16)]
        + [jax.ShapeDtypeStruct((t, f), BF16)] * 3 + [s for _, s in sums],
        scratch_shapes=[pltpu.VMEM(wg.shape, BF16), pltpu.VMEM(wu.shape, BF16), pltpu.VMEM(wd.shape, BF16),
                        pltpu.SemaphoreType.DMA((3,))],
        sem=("arbitrary",), args=[x, gain] + ([] if head is None else list(head)) + [wg, wu, wd])


def _ffn_bwd_data(dy, x, gain, g, u, wg, wu, wd, name, rides=None):
    t, d = x.shape
    wg, wu, wd = _flat(wg), _flat(wu), _flat(wd)
    f = wg.shape[0]
    tm = min(256, t)

    def body(dy_ref, x_ref, gain_ref, g_ref, u_ref, wg_hbm, wu_hbm, wd_hbm, dx_ref, dg_ref, du_ref, dgain_ref,
             wg_v, wu_v, wd_v, sems):
        @pl.when(pl.program_id(0) == 0)
        def _():
            _load_weights([(wg_hbm, wg_v), (wu_hbm, wu_v), (wd_hbm, wd_v)], sems)
            dgain_ref[...] = jnp.zeros_like(dgain_ref)

        dyv = dy_ref[...]
        dyh = (0.5 * dyv).astype(BF16)
        dxn = jnp.zeros((tm, d), F32)
        chunks = _hidden_chunks(f)
        das = [_dot(dyh, wd_v[c, :], NT) for c in chunks]
        for c, da in zip(chunks, das):
            gj = g_ref[:, c].astype(F32)
            uj = u_ref[:, c].astype(F32)
            sig = _sigmoid(gj)
            dgj = (da * uj * (sig * (1.0 + gj * (1.0 - sig)))).astype(BF16)
            duj = (da * (gj * sig)).astype(BF16)
            dg_ref[:, c] = dgj
            du_ref[:, c] = duj
            dxn = dxn + _dot(dgj, wg_v[c, :]) + _dot(duj, wu_v[c, :])
        xv = x_ref[...]
        r = lax.rsqrt(jnp.mean(xv * xv, axis=-1, keepdims=True) + NORM_EPS)
        xh = xv * r
        dgain_ref[...] += _rows8(dxn * xh)
        dxh = dxn * gain_ref[...]
        dx_ref[...] = dyv + r * (dxh - xh * jnp.mean(dxh * xh, axis=-1, keepdims=True))

    hbm = pl.BlockSpec(memory_space=pl.ANY)
    tile = pl.BlockSpec((tm, d), lambda i: (i, 0))
    hid = pl.BlockSpec((tm, f), lambda i: (i, 0))
    return _pallas(
        body, rides, name=name, grid=(t // tm,),
        in_specs=[tile, tile, pl.BlockSpec((1, d), lambda i: (0, 0)), hid, hid, hbm, hbm, hbm],
        out_specs=[tile, hid, hid, pl.BlockSpec((8, d), lambda i: (0, 0))],
        out_shape=[jax.ShapeDtypeStruct((t, d), F32), jax.ShapeDtypeStruct((t, f), BF16),
                   jax.ShapeDtypeStruct((t, f), BF16), jax.ShapeDtypeStruct((8, d), F32)],
        scratch_shapes=[pltpu.VMEM(wg.shape, BF16), pltpu.VMEM(wu.shape, BF16), pltpu.VMEM(wd.shape, BF16),
                        pltpu.SemaphoreType.DMA((3,))],
        sem=("arbitrary",), args=[dy, x, gain, g, u, wg, wu, wd])


WGRAD_ROW_BLOCKS = 2


def _ffn_wgrad_down(a, dy, name, rides=None):
    t, d = dy.shape
    f = a.shape[1]
    fb = f // WGRAD_ROW_BLOCKS
    tk = min(1024, t)

    def body(dy_ref, a_ref, dwd_ref):
        @pl.when(pl.program_id(1) == 0)
        def _():
            dwd_ref[...] = jnp.zeros_like(dwd_ref)

        dwd_ref[...] += _dot(a_ref[...], (0.5 * dy_ref[...]).astype(BF16), TN)

    return _pallas(
        body, rides, name=name, grid=(WGRAD_ROW_BLOCKS, t // tk),
        in_specs=[pl.BlockSpec((tk, d), lambda j, k: (k, 0)), pl.BlockSpec((tk, fb), lambda j, k: (k, j))],
        out_specs=[pl.BlockSpec((fb, d), lambda j, k: (j, 0))],
        out_shape=[jax.ShapeDtypeStruct((f, d), F32)],
        sem=("arbitrary", "arbitrary"), args=[dy, a])


def _ffn_wgrad_gu(xn, dhs, name, rides=None):
    t, d = xn.shape
    n = len(dhs)
    f = dhs[0].shape[1]
    fb = f // WGRAD_ROW_BLOCKS
    tk = min(2048 // n, t)

    def body(xn_ref, *refs):
        @pl.when(pl.program_id(1) == 0)
        def _():
            for o_ref in refs[n:]:
                o_ref[...] = jnp.zeros_like(o_ref)

        xnv = xn_ref[...]
        for dh_ref, o_ref in zip(refs[:n], refs[n:]):
            o_ref[...] += _dot(dh_ref[...], xnv, TN)

    hid = pl.BlockSpec((tk, fb), lambda j, k: (k, j))
    out = pl.BlockSpec((fb, d), lambda j, k: (j, 0))
    return _pallas(
        body, rides, name=name, grid=(WGRAD_ROW_BLOCKS, t // tk),
        in_specs=[pl.BlockSpec((tk, d), lambda j, k: (k, 0))] + [hid] * n,
        out_specs=[out] * n, out_shape=[jax.ShapeDtypeStruct((f, d), F32)] * n,
        sem=("arbitrary", "arbitrary"), args=[xn] + list(dhs))


def _tn_matmul(a, b, bn, name):
    t, m = a.shape
    n = b.shape[1]
    tk = min(2048, t)

    def body(a_ref, b_ref, o_ref):
        @pl.when(pl.program_id(1) == 0)
        def _():
            o_ref[...] = jnp.zeros_like(o_ref)

        o_ref[...] += _dot(a_ref[...].astype(BF16), b_ref[...].astype(BF16), TN)

    return pl.pallas_call(
        body, name=name, grid=(n // bn, t // tk),
        in_specs=[pl.BlockSpec((tk, m), lambda j, k: (k, 0)), pl.BlockSpec((tk, bn), lambda j, k: (k, j))],
        out_specs=pl.BlockSpec((None, m, bn), lambda j, k: (j, 0, 0)),
        out_shape=jax.ShapeDtypeStruct((n // bn, m, bn), F32),
        compiler_params=_params("arbitrary", "arbitrary"),
    )(a, b)


def _chunk_scratch(tm, w):
    return pltpu.VMEM((w // LANE, tm, LANE), F32)


def _regroup_store(cbuf, out_ref, dil):
    n = out_ref.shape[1]
    for g in range(dil):
        for k in range(cbuf.shape[0]):
            rows = cbuf[k] if dil == 1 else cbuf[k, pl.ds(g, n, stride=dil), :]
            out_ref[g, :, k * LANE:(k + 1) * LANE] = rows.astype(out_ref.dtype)


def _natural_rows(ref, dil, cbuf):
    if dil == 1:
        return ref[0].astype(F32)
    n = ref.shape[1]
    for g in range(dil):
        for k in range(cbuf.shape[0]):
            cbuf[k, pl.ds(g, n, stride=dil), :] = ref[g, :, k * LANE:(k + 1) * LANE].astype(F32)
    return jnp.concatenate([cbuf[k] for k in range(cbuf.shape[0])], axis=1)


def _inproj_fwd(h, gain, win):
    t, d = h.shape
    ns, _, cs = win.shape
    tm = min(512, t)
    rw, aw = 4 * RET_WIDTH, 3 * ATT_WIDTH

    def body(h_ref, gain_ref, w_ref, xn_ref, ur_ref, *rest):
        a_refs, abuf = rest[:-1], rest[-1]
        hv = h_ref[...]
        r = lax.rsqrt(jnp.mean(hv * hv, axis=-1, keepdims=True) + NORM_EPS)
        xn = (hv * r * gain_ref[...]).astype(BF16)
        xn_ref[...] = xn
        for j in range(ns):
            res = _dot(xn, w_ref[j])
            for k in range(cs // LANE):
                chunk = j * (cs // LANE) + k
                piece = res[:, k * LANE:(k + 1) * LANE]
                if chunk < rw // LANE:
                    ur_ref[:, chunk * LANE:(chunk + 1) * LANE] = piece
                else:
                    abuf[chunk - rw // LANE] = piece
        for dil, a_ref in zip(DILATIONS, a_refs):
            _regroup_store(abuf, a_ref, dil)

    return pl.pallas_call(
        body, name="inproj_fwd", grid=(t // tm,),
        in_specs=[pl.BlockSpec((tm, d), lambda i: (i, 0)), pl.BlockSpec((1, d), lambda i: (0, 0)),
                  pl.BlockSpec(win.shape, lambda i: (0, 0, 0))],
        out_specs=[pl.BlockSpec((tm, d), lambda i: (i, 0)), pl.BlockSpec((tm, rw), lambda i: (i, 0))]
        + [pl.BlockSpec((dil, tm // dil, aw), lambda i: (0, i, 0)) for dil in DILATIONS],
        out_shape=[jax.ShapeDtypeStruct((t, d), BF16), jax.ShapeDtypeStruct((t, rw), F32)]
        + [jax.ShapeDtypeStruct((dil, t // dil, aw), BF16) for dil in DILATIONS],
        scratch_shapes=[_chunk_scratch(tm, aw)],
        compiler_params=_params("arbitrary"),
    )(h, gain, win)


def _inproj_bwd(pieces, h, gain, dres, win):
    t, d = h.shape
    ns, _, cs = win.shape
    pw = pieces[0].shape[1]
    tm = min(512, t)
    npc = len(pieces)

    def body(*refs):
        p_refs = refs[:npc]
        h_ref, gain_ref, dres_ref, w_ref, dh_ref, du_ref, dgain_ref = refs[npc:]

        @pl.when(pl.program_id(0) == 0)
        def _():
            dgain_ref[...] = jnp.zeros_like(dgain_ref)

        for k in range(npc):
            du_ref[:, k * pw:(k + 1) * pw] = p_refs[k][...]
        dxn = jnp.zeros((tm, d), F32)
        for j in range(ns):
            dxn = dxn + _dot(du_ref[:, j * cs:(j + 1) * cs], w_ref[j], NT)
        hv = h_ref[...]
        r = lax.rsqrt(jnp.mean(hv * hv, axis=-1, keepdims=True) + NORM_EPS)
        xh = hv * r
        dgain_ref[...] += _rows8(dxn * xh)
        dxh = dxn * gain_ref[...]
        dh_ref[...] = dres_ref[...] + r * (dxh - xh * jnp.mean(dxh * xh, axis=-1, keepdims=True))

    tile = pl.BlockSpec((tm, d), lambda i: (i, 0))
    return pl.pallas_call(
        body, name="inproj_bwd", grid=(t // tm,),
        in_specs=[pl.BlockSpec((tm, pw), lambda i: (i, 0))] * npc + [
            tile, pl.BlockSpec((1, d), lambda i: (0, 0)), tile, pl.BlockSpec(win.shape, lambda i: (0, 0, 0))],
        out_specs=[tile, pl.BlockSpec((tm, npc * pw), lambda i: (i, 0)), pl.BlockSpec((8, d), lambda i: (0, 0))],
        out_shape=[jax.ShapeDtypeStruct((t, d), F32), jax.ShapeDtypeStruct((t, npc * pw), BF16),
                   jax.ShapeDtypeStruct((8, d), F32)],
        compiler_params=_params("arbitrary"),
    )(*pieces, h, gain, dres, win)


def _outproj_fwd(h, mix_r, mix_a, wo):
    t, d = h.shape
    hw = mix_r.shape[1]
    tm = min(512, t)

    def body(h_ref, mr_ref, ma_ref, w_ref, o_ref):
        o_ref[...] = h_ref[...] + _dot(mr_ref[...], w_ref[0:hw, :]) + _dot(ma_ref[...], w_ref[hw:2 * hw, :])

    tile = pl.BlockSpec((tm, d), lambda i: (i, 0))
    half = pl.BlockSpec((tm, hw), lambda i: (i, 0))
    return pl.pallas_call(
        body, name="outproj_fwd", grid=(t // tm,),
        in_specs=[tile, half, half, pl.BlockSpec(wo.shape, lambda i: (0, 0))],
        out_specs=tile, out_shape=jax.ShapeDtypeStruct((t, d), F32),
        compiler_params=_params("arbitrary"),
    )(h, mix_r, mix_a, wo)


def _outproj_bwd(dh, wo, rides=None):
    t, d = dh.shape
    hw = wo.shape[0] // 2
    tm = min(512, t)

    def body(dh_ref, w_ref, dr_ref, da_ref):
        dhb = dh_ref[...].astype(BF16)
        dr_ref[...] = _dot(dhb, w_ref[0:hw, :], NT)
        da_ref[...] = _dot(dhb, w_ref[hw:2 * hw, :], NT)

    half = pl.BlockSpec((tm, hw), lambda i: (i, 0))
    return _pallas(
        body, rides, name="outproj_bwd", grid=(t // tm,),
        in_specs=[pl.BlockSpec((tm, d), lambda i: (i, 0)), pl.BlockSpec(wo.shape, lambda i: (0, 0))],
        out_specs=[half, half],
        out_shape=[jax.ShapeDtypeStruct((t, hw), F32), jax.ShapeDtypeStruct((t, hw), F32)],
        sem=("arbitrary",), args=[dh, wo])


def _retention_tables(t):
    pos = jnp.arange(t, dtype=F32)
    pair = (jnp.arange(RET_DIM) // 2 * 2).astype(F32)
    ang = pos[:, None] * (ROPE_BASE ** (-pair / RET_DIM))[None, :]
    c = RET_CHUNK
    log_g = jnp.log(1.0 - 2.0 ** (-5.0 - jnp.arange(RET_HEADS, dtype=F32)))
    idx = jnp.arange(c, dtype=F32)
    rel = idx[:, None] - idx[None, :]
    decay = jnp.where(rel >= 0, jnp.exp(log_g[:, None, None] * jnp.maximum(rel, 0.0)), 0.0)
    zeta = jnp.exp(log_g[:, None] * (c - 1 - idx)[None, :])
    xi = jnp.exp(log_g[:, None] * (idx + 1)[None, :])
    gc = jnp.exp(log_g * c)
    wide = lambda v: jnp.broadcast_to(v[:, :, None], (RET_HEADS, c, LANE))
    return (jnp.cos(ang), jnp.sin(ang), decay, wide(zeta), wide(xi),
            jnp.broadcast_to(gc[:, None, None], (RET_HEADS, c, LANE)))


def _rot(v):
    lane = lax.broadcasted_iota(jnp.int32, v.shape, 1)
    nxt = pltpu.roll(v, LANE - 1, 1)
    prv = pltpu.roll(v, 1, 1)
    return jnp.where(lane % 2 == 0, -nxt, prv)


def _ret_specs(tr, rev, nt):
    ti = (lambda i: nt - 1 - i) if rev else (lambda i: i)
    col = lambda blk: pl.BlockSpec((tr, RET_WIDTH), lambda i: (ti(i), blk))
    tab = pl.BlockSpec((tr, LANE), lambda i: (ti(i), 0))
    head = pl.BlockSpec((RET_HEADS, RET_CHUNK, LANE), lambda i: (0, 0, 0))
    return col, tab, head


def _ret_chunks(tr, rev=False):
    order = list(range(tr // RET_CHUNK))
    return [(pl.ds(ci * RET_CHUNK, RET_CHUNK), slice(h * RET_DIM, (h + 1) * RET_DIM), h)
            for h in range(RET_HEADS) for ci in (reversed(order) if rev else order)]


def _ret_operands(items, q_ref, k_ref, v_ref, cos_ref, sin_ref, zeta_ref):
    scale = RET_DIM ** -0.5
    qbs, kbs, vbs, kzs = [], [], [], []
    for sl, hs, h in items:
        cs, sn = cos_ref[sl, :], sin_ref[sl, :]
        q, k = q_ref[sl, hs], k_ref[sl, hs]
        kr = (k * cs + _rot(k) * sn) * scale
        qbs.append((q * cs + _rot(q) * sn).astype(BF16))
        kbs.append(kr.astype(BF16))
        vbs.append(v_ref[sl, hs].astype(BF16))
        kzs.append((kr * zeta_ref[h]).astype(BF16))
    return qbs, kbs, vbs, kzs


def _ret_states(items, state, steps, gc_ref):
    cur, befores = {}, []
    for (sl, hs, h), step in zip(items, steps):
        st = cur[h] if h in cur else state[h]
        befores.append(st)
        cur[h] = st * gc_ref[h] + step
    for h, st in cur.items():
        state[h] = st
    return befores


def _ret_fwd(u, gain, tabs):
    t = u.shape[0]
    tr = min(512, t)
    nt = t // tr
    cos, sin, decay, zeta, xi, gc = tabs
    scale = RET_DIM ** -0.5

    def body(q_ref, k_ref, v_ref, gt_ref, cos_ref, sin_ref, gain_ref, dec_ref, zeta_ref, xi_ref, gc_ref,
             raw_ref, mix_ref, state):
        @pl.when(pl.program_id(0) == 0)
        def _():
            state[...] = jnp.zeros_like(state)

        items = _ret_chunks(tr)
        n = range(len(items))
        qbs, kbs, vbs, kzs = _ret_operands(items, q_ref, k_ref, v_ref, cos_ref, sin_ref, zeta_ref)
        ss = [_dot(qbs[i], kbs[i], NT) for i in n]
        kvs = [_dot(kzs[i], vbs[i], TN) for i in n]
        befores = _ret_states(items, state, kvs, gc_ref)
        intra = [_dot((ss[i] * dec_ref[items[i][2]]).astype(BF16), vbs[i]) for i in n]
        inter = [_dot(qbs[i], befores[i].astype(BF16)) for i in n]
        for i, (sl, hs, h) in enumerate(items):
            o = intra[i] + inter[i] * xi_ref[h]
            raw_ref[sl, hs] = o
            mu = jnp.mean(o, axis=-1, keepdims=True)
            var = jnp.mean(jnp.square(o - mu), axis=-1, keepdims=True)
            y = (o - mu) * lax.rsqrt(var + GN_EPS) * gain_ref[:, hs]
            gt = gt_ref[sl, hs]
            mix_ref[sl, hs] = (y * (gt * _sigmoid(gt))).astype(BF16)

    col, tab, head = _ret_specs(tr, False, nt)
    out = pl.BlockSpec((tr, RET_WIDTH), lambda i: (i, 0))
    return pl.pallas_call(
        body, name="ret_fwd", grid=(nt,),
        in_specs=[col(0), col(1), col(2), col(3), tab, tab, pl.BlockSpec((1, RET_WIDTH), lambda i: (0, 0)),
                  head, head, head, head],
        out_specs=[out, out],
        out_shape=[jax.ShapeDtypeStruct((t, RET_WIDTH), F32), jax.ShapeDtypeStruct((t, RET_WIDTH), BF16)],
        scratch_shapes=[pltpu.VMEM((RET_HEADS, RET_DIM, RET_DIM), F32)],
        compiler_params=_params("arbitrary"),
    )(u, u, u, u, cos, sin, gain, decay, zeta, xi, gc)


def _ret_bwd_q(dmix, raw, u, gain, tabs, rides=None):
    t = u.shape[0]
    tr = min(512, t)
    nt = t // tr
    cos, sin, decay, zeta, xi, gc = tabs
    scale = RET_DIM ** -0.5

    def body(dm_ref, raw_ref, q_ref, k_ref, v_ref, gt_ref, cos_ref, sin_ref, gain_ref, dec_ref, zeta_ref, xi_ref, gc_ref,
             dq_ref, dgt_ref, dret_ref, dgain_ref, state):
        @pl.when(pl.program_id(0) == 0)
        def _():
            state[...] = jnp.zeros_like(state)
            dgain_ref[...] = jnp.zeros_like(dgain_ref)

        items = _ret_chunks(tr)
        n_items = range(len(items))
        qbs, kbs, vbs, kzs = _ret_operands(items, q_ref, k_ref, v_ref, cos_ref, sin_ref, zeta_ref)
        dos, dgains = [], {}
        for sl, hs, h in items:
            o = raw_ref[sl, hs]
            mu = jnp.mean(o, axis=-1, keepdims=True)
            var = jnp.mean(jnp.square(o - mu), axis=-1, keepdims=True)
            rs = lax.rsqrt(var + GN_EPS)
            n = (o - mu) * rs
            gt = gt_ref[sl, hs]
            sig = _sigmoid(gt)
            dout = dm_ref[sl, hs]
            gain_h = gain_ref[:, hs]
            dgt_ref[sl, hs] = (dout * (n * gain_h) * (sig * (1.0 + gt * (1.0 - sig)))).astype(BF16)
            dy = dout * (gt * sig)
            dgains[h] = dgains[h] + _rows8(dy * n) if h in dgains else _rows8(dy * n)
            dn = dy * gain_h
            do = rs * (dn - jnp.mean(dn, axis=-1, keepdims=True) - n * jnp.mean(dn * n, axis=-1, keepdims=True))
            dret_ref[sl, hs] = do
            dos.append(do)
        for h, dg in dgains.items():
            dgain_ref[:, h * RET_DIM:(h + 1) * RET_DIM] += dg
        dss = [_dot(dos[i].astype(BF16), vbs[i], NT) for i in n_items]
        kvs = [_dot(kzs[i], vbs[i], TN) for i in n_items]
        befores = _ret_states(items, state, kvs, gc_ref)
        intra = [_dot((dss[i] * dec_ref[items[i][2]]).astype(BF16), kbs[i]) for i in n_items]
        inter = [_dot((dos[i] * xi_ref[items[i][2]]).astype(BF16), befores[i].astype(BF16), NT) for i in n_items]
        for i, (sl, hs, h) in enumerate(items):
            dqr = intra[i] + inter[i]
            dq_ref[sl, hs] = (dqr * cos_ref[sl, :] - _rot(dqr * sin_ref[sl, :])).astype(BF16)

    col, tab, head = _ret_specs(tr, False, nt)
    out = pl.BlockSpec((tr, RET_WIDTH), lambda i: (i, 0))
    return _pallas(
        body, rides, name="ret_bwd_q", grid=(nt,),
        in_specs=[out, out, col(0), col(1), col(2), col(3), tab, tab, pl.BlockSpec((1, RET_WIDTH), lambda i: (0, 0)),
                  head, head, head, head],
        out_specs=[out, out, out, pl.BlockSpec((8, RET_WIDTH), lambda i: (0, 0))],
        out_shape=[jax.ShapeDtypeStruct((t, RET_WIDTH), BF16), jax.ShapeDtypeStruct((t, RET_WIDTH), BF16),
                   jax.ShapeDtypeStruct((t, RET_WIDTH), F32), jax.ShapeDtypeStruct((8, RET_WIDTH), F32)],
        scratch_shapes=[pltpu.VMEM((RET_HEADS, RET_DIM, RET_DIM), F32)],
        sem=("arbitrary",), args=[dmix, raw, u, u, u, u, cos, sin, gain, decay, zeta, xi, gc])


def _ret_bwd_kv(dret, u, tabs, rides=None):
    t = u.shape[0]
    tr = min(512, t)
    nt = t // tr
    cos, sin, decay, zeta, xi, gc = tabs
    scale = RET_DIM ** -0.5

    def body(do_ref, q_ref, k_ref, v_ref, cos_ref, sin_ref, dec_ref, zeta_ref, xi_ref, gc_ref, dk_ref, dv_ref, gst):
        @pl.when(pl.program_id(0) == 0)
        def _():
            gst[...] = jnp.zeros_like(gst)

        items = _ret_chunks(tr, rev=True)
        n = range(len(items))
        qbs, kbs, vbs, kzs = _ret_operands(items, q_ref, k_ref, v_ref, cos_ref, sin_ref, zeta_ref)
        dos = [do_ref[sl, hs] for sl, hs, h in items]
        dobs = [do.astype(BF16) for do in dos]
        ss = [_dot(qbs[i], kbs[i], NT) for i in n]
        dss = [_dot(dobs[i], vbs[i], NT) for i in n]
        steps = [_dot(qbs[i], (dos[i] * xi_ref[items[i][2]]).astype(BF16), TN) for i in n]
        afters = [g.astype(BF16) for g in _ret_states(items, gst, steps, gc_ref)]
        dvs = [_dot((ss[i] * dec_ref[items[i][2]]).astype(BF16), dobs[i], TN) + _dot(kzs[i], afters[i]) for i in n]
        dks = [_dot((dss[i] * dec_ref[items[i][2]]).astype(BF16), qbs[i], TN) for i in n]
        dkz = [_dot(vbs[i], afters[i], NT) for i in n]
        for i, (sl, hs, h) in enumerate(items):
            dv_ref[sl, hs] = dvs[i].astype(BF16)
            dkr = (dks[i] + dkz[i] * zeta_ref[h]) * scale
            dk_ref[sl, hs] = (dkr * cos_ref[sl, :] - _rot(dkr * sin_ref[sl, :])).astype(BF16)

    col, tab, head = _ret_specs(tr, True, nt)
    out = pl.BlockSpec((tr, RET_WIDTH), lambda i: (nt - 1 - i, 0))
    return _pallas(
        body, rides, name="ret_bwd_kv", grid=(nt,),
        in_specs=[out, col(0), col(1), col(2), tab, tab, head, head, head, head],
        out_specs=[out, out],
        out_shape=[jax.ShapeDtypeStruct((t, RET_WIDTH), BF16), jax.ShapeDtypeStruct((t, RET_WIDTH), BF16)],
        scratch_shapes=[pltpu.VMEM((RET_HEADS, RET_DIM, RET_DIM), F32)],
        sem=("arbitrary",), args=[dret, u, u, u, cos, sin, decay, zeta, xi, gc])


PAIRS = ATT_WIDTH // LANE
ATT_Q_BLK, ATT_K_BLK, ATT_V_BLK = 0, PAIRS, 2 * PAIRS
STAT_LANES = ATT_DIM // 2


def _att_tiles(t, dil):
    sub = t // dil
    tq = min(512, sub)
    return sub, tq, sub // tq, tq // ATT_BLOCK


def _att_in_specs(tq, qb, ti):
    cur = lambda off: pl.BlockSpec((None, tq, LANE), lambda g, p, i: (g, ti(i), off + p))
    prev = lambda off: pl.BlockSpec((None, ATT_BLOCK, LANE), lambda g, p, i: (g, jnp.maximum(ti(i) * qb - 1, 0), off + p))
    return [cur(ATT_Q_BLK), cur(ATT_K_BLK), prev(ATT_K_BLK), cur(ATT_V_BLK), prev(ATT_V_BLK)]


def _band_mask():
    key = lax.broadcasted_iota(jnp.int32, (2 * ATT_BLOCK, 2 * ATT_BLOCK), 0)
    qry = lax.broadcasted_iota(jnp.int32, (2 * ATT_BLOCK, 2 * ATT_BLOCK), 1) % ATT_BLOCK
    dist = qry + ATT_BLOCK - key
    return (dist >= 0) & (dist <= ATT_BLOCK), key >= ATT_BLOCK


def _head0_lanes():
    return lax.broadcasted_iota(jnp.int32, (ATT_BLOCK, LANE), 1) < ATT_DIM


def _stack_heads(v, head0):
    zero = jnp.zeros((), v.dtype)
    return jnp.concatenate([jnp.where(head0, v, zero), jnp.where(head0, zero, v)], axis=0)


def _unstack_heads(v, head0):
    return jnp.where(head0, v[0:ATT_BLOCK], v[ATT_BLOCK:])


def _att_fwd(ua, dil):
    sub = ua.shape[1]
    _, tq, nq, qb = _att_tiles(sub * dil, dil)

    def body(q_ref, kc_ref, kp_ref, vc_ref, vp_ref, o_ref, l_ref, kx, vx):
        tile = pl.program_id(2)
        kx[0:ATT_BLOCK, :] = kp_ref[...]
        kx[ATT_BLOCK:, :] = kc_ref[...]
        vx[0:ATT_BLOCK, :] = vp_ref[...]
        vx[ATT_BLOCK:, :] = vc_ref[...]
        band, cur_keys = _band_mask()
        head0 = _head0_lanes()
        blocks = range(qb)
        rows = [slice(b * ATT_BLOCK, (b + 1) * ATT_BLOCK) for b in blocks]
        keys = [slice(b * ATT_BLOCK, (b + 2) * ATT_BLOCK) for b in blocks]
        sts = [_dot(kx[keys[b], :], _stack_heads(q_ref[rows[b], :] * jnp.asarray(ATT_DIM ** -0.5, BF16), head0), NT)
               for b in blocks]
        pts, lses = [], []
        for b in blocks:
            mask = band if b > 0 else band & (cur_keys | (tile > 0))
            st = jnp.where(mask, sts[b], -1e30)
            m = jnp.max(st, axis=0, keepdims=True)
            ex = jnp.exp(st - m)
            den = jnp.sum(ex, axis=0, keepdims=True)
            pts.append((ex * (1.0 / den)).astype(BF16))
            lses.append(m + jnp.log(den))
        outs = [_dot(pts[b], vx[keys[b], :], TN) for b in blocks]
        for b in blocks:
            o_ref[rows[b], :] = _unstack_heads(outs[b], head0).astype(BF16)
            cols = [jnp.broadcast_to(lses[b][:, e * ATT_BLOCK:(e + 1) * ATT_BLOCK], (ATT_BLOCK, LANE)).T for e in range(2)]
            l_ref[rows[b], :] = jnp.where(head0, cols[0], cols[1])

    out = pl.BlockSpec((None, tq, LANE), lambda g, p, i: (g, i, p))
    return pl.pallas_call(
        body, name=f"att_fwd_d{dil}", grid=(dil, PAIRS, nq),
        in_specs=_att_in_specs(tq, qb, lambda i: i),
        out_specs=[out, out],
        out_shape=[jax.ShapeDtypeStruct((dil, sub, ATT_WIDTH), BF16), jax.ShapeDtypeStruct((dil, sub, ATT_WIDTH), F32)],
        scratch_shapes=[pltpu.VMEM((tq + ATT_BLOCK, LANE), BF16)] * 2,
        compiler_params=_params("arbitrary", "arbitrary", "arbitrary"),
    )(ua, ua, ua, ua, ua)


def _regrouped_spec(tm, dil, w):
    return pl.BlockSpec((dil, tm // dil, w), lambda i: (0, i, 0))


def _att_combine(outs, lses, t):
    w = ATT_WIDTH
    tm = min(512, t)
    nb = len(outs)

    def body(*refs):
        o_refs, l_refs = refs[:nb], refs[nb:2 * nb]
        mix_ref, att_ref, lse_ref, buf = refs[2 * nb:]
        ls = [_natural_rows(r, dil, buf) for r, dil in zip(l_refs, DILATIONS)]
        m = functools.reduce(jnp.maximum, ls)
        ws = [jnp.exp(l - m) for l in ls]
        den = functools.reduce(jnp.add, ws)
        att = functools.reduce(jnp.add, [(wt / den) * _natural_rows(r, dil, buf) for wt, r, dil in zip(ws, o_refs, DILATIONS)])
        att_ref[...] = att
        mix_ref[...] = att.astype(BF16)
        lse_ref[...] = m + jnp.log(den)

    tile = pl.BlockSpec((tm, w), lambda i: (i, 0))
    regrouped = [_regrouped_spec(tm, dil, w) for dil in DILATIONS]
    return pl.pallas_call(
        body, name="att_combine", grid=(t // tm,),
        in_specs=regrouped * 2, out_specs=[tile, tile, tile],
        out_shape=[jax.ShapeDtypeStruct((t, w), BF16), jax.ShapeDtypeStruct((t, w), F32), jax.ShapeDtypeStruct((t, w), F32)],
        scratch_shapes=[_chunk_scratch(tm, w)],
        compiler_params=_params("arbitrary"),
    )(*outs, *lses)


def _att_bwd_prep(datt, att, lse):
    t, w = datt.shape
    tm = min(512, t)

    def body(da_ref, at_ref, l_ref, *rest):
        outs, dbuf, sbuf = rest[:-2], rest[-2], rest[-1]
        dav = da_ref[...]
        prod = dav * at_ref[...]
        lane = lax.broadcasted_iota(jnp.int32, (tm, LANE), 1)
        for k in range(w // LANE):
            cols = slice(k * LANE, (k + 1) * LANE)
            dbuf[k] = dav[:, cols]
            delta = jnp.concatenate(
                [jnp.broadcast_to(jnp.sum(prod[:, k * LANE + e * ATT_DIM:k * LANE + (e + 1) * ATT_DIM], axis=-1, keepdims=True),
                                  (tm, ATT_DIM)) for e in range(LANE // ATT_DIM)], axis=1)
            sbuf[k] = jnp.where(lane % ATT_DIM < STAT_LANES, l_ref[:, cols], delta)
        for k, dil in enumerate(DILATIONS):
            _regroup_store(dbuf, outs[2 * k], dil)
            _regroup_store(sbuf, outs[2 * k + 1], dil)

    tile = pl.BlockSpec((tm, w), lambda i: (i, 0))
    res = pl.pallas_call(
        body, name="att_bwd_prep", grid=(t // tm,),
        in_specs=[tile] * 3,
        out_specs=[_regrouped_spec(tm, dil, w) for dil in DILATIONS for _ in range(2)],
        out_shape=[jax.ShapeDtypeStruct((dil, t // dil, w), dt) for dil in DILATIONS for dt in (BF16, F32)],
        scratch_shapes=[_chunk_scratch(tm, w)] * 2,
        compiler_params=_params("arbitrary"),
    )(datt, att, lse)
    return [(res[2 * k], res[2 * k + 1]) for k in range(len(DILATIONS))]


def _att_bwd(ua, da, stat, dil, rides=None):
    sub = ua.shape[1]
    _, tq, nq, qb = _att_tiles(sub * dil, dil)
    scale = ATT_DIM ** -0.5

    def body(q_ref, kc_ref, kp_ref, vc_ref, vp_ref, da_ref, st_ref, dq_ref, dk_ref, dv_ref, kx, vx, ck, cv):
        step = pl.program_id(2)
        tile = nq - 1 - step

        @pl.when(step == 0)
        def _():
            ck[...] = jnp.zeros_like(ck)
            cv[...] = jnp.zeros_like(cv)

        kx[0:ATT_BLOCK, :] = kp_ref[...]
        kx[ATT_BLOCK:, :] = kc_ref[...]
        vx[0:ATT_BLOCK, :] = vp_ref[...]
        vx[ATT_BLOCK:, :] = vc_ref[...]
        band, cur_keys = _band_mask()
        head0 = _head0_lanes()
        blocks = range(qb)
        rows = [slice(b * ATT_BLOCK, (b + 1) * ATT_BLOCK) for b in blocks]
        keys = [slice(b * ATT_BLOCK, (b + 2) * ATT_BLOCK) for b in blocks]
        qqs = [_stack_heads(q_ref[rows[b], :] * jnp.asarray(scale, BF16), head0) for b in blocks]
        dds = [_stack_heads(da_ref[rows[b], :], head0) for b in blocks]
        sts = [_dot(kx[keys[b], :], qqs[b], NT) for b in blocks]
        dpts = [_dot(vx[keys[b], :], dds[b], NT) for b in blocks]
        pts, dsts = [], []
        for b in blocks:
            mask = band if b > 0 else band & (cur_keys | (tile > 0))
            stat = st_ref[rows[b], :].T
            row = lambda k: jnp.concatenate([stat[e * ATT_DIM + k:e * ATT_DIM + k + 1, :] for e in range(2)], axis=1)
            pt = jnp.where(mask, jnp.exp(sts[b] - row(0)), 0.0)
            dsts.append((pt * (dpts[b] - row(STAT_LANES))).astype(BF16))
            pts.append(pt.astype(BF16))
        dqs = [_dot(dsts[b], kx[keys[b], :], TN) for b in blocks]
        dkbs = [_dot(dsts[b], qqs[b]) for b in blocks]
        dvbs = [_dot(pts[b], dds[b]) for b in blocks]
        for b in blocks:
            dq_ref[rows[b], :] = (_unstack_heads(dqs[b], head0) * scale).astype(BF16)
        for b in blocks[1:]:
            dk_ref[rows[b - 1], :] = (dkbs[b - 1][ATT_BLOCK:] + dkbs[b][0:ATT_BLOCK]).astype(BF16)
            dv_ref[rows[b - 1], :] = (dvbs[b - 1][ATT_BLOCK:] + dvbs[b][0:ATT_BLOCK]).astype(BF16)
        before_k, before_v = dkbs[0][0:ATT_BLOCK], dvbs[0][0:ATT_BLOCK]
        open_k, open_v = dkbs[-1][ATT_BLOCK:], dvbs[-1][ATT_BLOCK:]
        last = slice(tq - ATT_BLOCK, tq)
        dk_ref[last, :] = (open_k + ck[...]).astype(BF16)
        dv_ref[last, :] = (open_v + cv[...]).astype(BF16)
        ck[...] = before_k
        cv[...] = before_v

    ti = lambda i: nq - 1 - i
    out = pl.BlockSpec((None, tq, LANE), lambda g, p, i: (g, ti(i), p))
    shape = jax.ShapeDtypeStruct((dil, sub, ATT_WIDTH), BF16)
    return _pallas(
        body, rides, name=f"att_bwd_d{dil}", grid=(dil, PAIRS, nq),
        in_specs=_att_in_specs(tq, qb, ti) + [out, out],
        out_specs=[out, out, out], out_shape=[shape] * 3,
        scratch_shapes=[pltpu.VMEM((tq + ATT_BLOCK, LANE), BF16)] * 2 + [pltpu.VMEM((ATT_BLOCK, LANE), F32)] * 2,
        sem=("arbitrary", "arbitrary", "arbitrary"), args=[ua, ua, ua, ua, ua, da, stat])


def _att_bwd_sum(parts, t):
    w = ATT_WIDTH
    tm = min(512, t)
    nk = len(parts[0])

    def body(*refs):
        ins, outs, buf = refs[:-nk - 1], refs[-nk - 1:-1], refs[-1]
        for k in range(nk):
            acc = None
            for b, dil in enumerate(DILATIONS):
                rows = _natural_rows(ins[b * nk + k], dil, buf)
                acc = rows if acc is None else acc + rows
            outs[k][...] = acc.astype(BF16)

    tile = pl.BlockSpec((tm, w), lambda i: (i, 0))
    return pl.pallas_call(
        body, name="att_bwd_sum", grid=(t // tm,),
        in_specs=[_regrouped_spec(tm, dil, w) for dil in DILATIONS for _ in range(nk)], out_specs=[tile] * nk,
        out_shape=[jax.ShapeDtypeStruct((t, w), BF16)] * nk,
        scratch_shapes=[_chunk_scratch(tm, w)],
        compiler_params=_params("arbitrary"),
    )(*[a for p in parts for a in p])


class _Reduction:
    def __init__(self, place, names, grads):
        self.place, self.names, self.grads = place, names, grads

    def pair(self):
        return _pair_ride(self.grads)

    def chips(self, got):
        self.got = got
        return _chip_ride([_pair_sum(self.place, g, r, f"pair_sum_{n}") for g, r, n in zip(self.grads, got, self.names)])

    def halves(self, others):
        return [_chip_sum(self.place, g, r, o, f"chip_sum_{n}")
                for g, r, o, n in zip(self.grads, self.got, others, self.names)]


def _step(x, target, gains, w, place=None):
    t = x.shape[0]
    ex = place is not None
    g_ffn1, g_mix, g_ret, g_ffn2, g_fin = gains
    w = list(w)
    tabs = _retention_tables(t)
    red = lambda names, grads: _Reduction(place, names, grads) if ex else None
    ride = lambda r: [r] if ex else None

    if ex:
        w[0:3] = _run(_gather_ride(w[0:3]), "gather_ffn1_weights")
    (h1, xn1, ga1, ua1, act1), rest = _ffn_fwd(x, g_ffn1, *w[0:3], "ffn1_fwd", ride(_gather_ride(w[3:])) if ex else None)
    if ex:
        w[3:] = rest[0]
    wg1, wu1, wd1, win, wo, wg2, wu2, wd2 = w
    wo2 = wo.reshape(wo.shape[0] * wo.shape[1], wo.shape[2])
    xnm, u, *uas = _inproj_fwd(h1, g_mix, win)
    raw, mix_r = _ret_fwd(u, g_ret, tabs)
    branches = [_att_fwd(ua, dil) for ua, dil in zip(uas, DILATIONS)]
    mix_a, att, lse = _att_combine([b[0] for b in branches], [b[1] for b in branches], t)
    h2 = _outproj_fwd(h1, mix_r, mix_a, wo2)
    (dh3, xn2, ga2, ua2, act2, loss_p, dg_fin), _ = _ffn_fwd(h2, g_ffn2, wg2, wu2, wd2, "ffn2_fwd", head=(g_fin, target))

    (dwd2,), _ = _ffn_wgrad_down(act2, dh3, "ffn2_wgrad_down")
    dwd2 = dwd2.reshape(wd2.shape)
    r_d2 = red(["ffn2_w_down"], [dwd2])
    (dh2, dga2, dua2, dg_ffn2), e = _ffn_bwd_data(dh3, h2, g_ffn2, ga2, ua2, wg2, wu2, wd2, "ffn2_bwd",
                                                  ex and [r_d2.pair()])
    (dwg2, dwu2), e = _ffn_wgrad_gu(xn2, [dga2, dua2], "ffn2_wgrad_gu", ex and [r_d2.chips(e[0])])
    dwg2, dwu2 = dwg2.reshape(wg2.shape), dwu2.reshape(wu2.shape)
    r_gu2 = red(["ffn2_w_gate", "ffn2_w_up"], [dwg2, dwu2])
    (dmix_r, dmix_a), e = _outproj_bwd(dh2, wo2, ex and [r_gu2.pair(), _finish_ride(r_d2.halves(e[0]))])
    if ex:
        got_gu2, (dwd2,) = e
    hw = RET_WIDTH // (wo.shape[1])
    dwo = jnp.concatenate([_tn_matmul(mix_r, dh2, dh2.shape[1], "wo_grad_r").reshape(hw, wo.shape[1], wo.shape[2]),
                           _tn_matmul(mix_a, dh2, dh2.shape[1], "wo_grad_a").reshape(hw, wo.shape[1], wo.shape[2])])
    r_wo = red(["w_out"], [dwo])
    (dq_r, dgt_r, dret, dg_ret), e = _ret_bwd_q(dmix_r, raw, u, g_ret, tabs, ex and [r_gu2.chips(got_gu2)])
    (dk_r, dv_r), e = _ret_bwd_kv(dret, u, tabs, ex and [r_wo.pair(), _finish_ride(r_gu2.halves(e[0]))])
    if ex:
        got_wo, (dwg2, dwu2) = e
    prep = _att_bwd_prep(dmix_a, att, lse)
    p1, e = _att_bwd(uas[0], *prep[0], DILATIONS[0], ex and [r_wo.chips(got_wo)])
    p4, e = _att_bwd(uas[1], *prep[1], DILATIONS[1], ex and [_finish_ride(r_wo.halves(e[0]))])
    if ex:
        (dwo,), = e
    p16, _ = _att_bwd(uas[2], *prep[2], DILATIONS[2])
    dq_a, dk_a, dv_a = _att_bwd_sum([p1, p4, p16], t)
    dh1, du, dg_mix = _inproj_bwd([dq_r, dk_r, dv_r, dgt_r, dq_a, dk_a, dv_a], h1, g_mix, dh2, win)
    dwin = _tn_matmul(xnm, du, win.shape[2], "win_grad")
    r_in = red(["w_in"], [dwin])
    (dwd1,), e = _ffn_wgrad_down(act1, dh1, "ffn1_wgrad_down", ex and [r_in.pair()])
    dwd1 = dwd1.reshape(wd1.shape)
    r_d1 = red(["ffn1_w_down"], [dwd1])
    (dx, dga1, dua1, dg_ffn1), e = _ffn_bwd_data(dh1, x, g_ffn1, ga1, ua1, wg1, wu1, wd1, "ffn1_bwd",
                                                  ex and [r_in.chips(e[0]), r_d1.pair()])
    (dwg1,), e = _ffn_wgrad_gu(xn1, [dga1], "ffn1_wgrad_gate", ex and [_finish_ride(r_in.halves(e[0])), r_d1.chips(e[1])])
    dwg1 = dwg1.reshape(wg1.shape)
    if ex:
        (dwin,), oth_d1 = e
        r_g1 = red(["ffn1_w_gate"], [dwg1])
        got_g1 = _run(r_g1.pair(), "pair_exchange_ffn1_gate")
    (dwu1,), e = _ffn_wgrad_gu(xn1, [dua1], "ffn1_wgrad_up", ex and [r_g1.chips(got_g1)])
    dwu1 = dwu1.reshape(wu1.shape)
    gain_parts = [dg_ffn1, dg_mix, dg_ret, dg_ffn2, dg_fin]
    if not ex:
        return loss_p, dx, [dwg1, dwu1, dwd1, dwin, dwo, dwg2, dwu2, dwd2], gain_parts
    r_u1 = red(["ffn1_w_up"], [dwu1])
    got_u1 = _run(r_u1.pair(), "pair_exchange_ffn1_up")
    oth_u1 = _run(r_u1.chips(got_u1), "chip_exchange_ffn1_up")
    last = r_g1.halves(e[0]) + r_u1.halves(oth_u1) + r_d1.halves(oth_d1)
    dwg1, dwu1, dwd1, gall = _run(_finish_ride(last, _pack_gains(gain_parts, x.shape[1])), "finish_exchange_ffn1")
    return loss_p, dx, [dwg1, dwu1, dwd1, dwin, dwo, dwg2, dwu2, dwd2], gall


N_DEV = 8
GAIN_ROWS = 8


def _place():
    x, y, c = lax.axis_index("x"), lax.axis_index("y"), lax.axis_index("c")
    chips = [(1 - x, y), (x, 1 - y), (1 - x, 1 - y)]
    return x, y, c, chips


def _hbm_specs(n):
    return [pl.BlockSpec(memory_space=pl.ANY)] * n


ROW_QUARTERS = 4


def _place_shards(place, ws):
    n = len(ws)

    def body(place_ref, *refs):
        for w_ref, o_ref in zip(refs[:n], refs[n:]):
            o_ref[...] = w_ref[...].astype(BF16)

    quarter = lambda w: (w.shape[0] // ROW_QUARTERS, w.shape[1])
    return pl.pallas_call(
        body, name="place_shards",
        grid_spec=pltpu.PrefetchScalarGridSpec(
            num_scalar_prefetch=1, grid=(ROW_QUARTERS,),
            in_specs=[pl.BlockSpec(quarter(w), lambda i, pr: (i, 0)) for w in ws],
            out_specs=[pl.BlockSpec((None,) + quarter(w), lambda i, pr: (pr[0], i, 0)) for w in ws]),
        out_shape=[jax.ShapeDtypeStruct((N_SHARD,) + w.shape, BF16) for w in ws],
        compiler_params=_params("arbitrary"),
    )(place, *ws)


def _gather_ride(bufs):
    na = len(bufs)

    def legs(outs, sems):
        send_sem, recv_sem, fsend_sem, frecv_sem = sems
        x, y, c, chips = _place()

        def half(a, idx, which):
            hr = outs[a].shape[1] // 2
            return outs[a].at[idx, pl.ds(which * hr, hr)]

        def ici(a, j, idx):
            px, py = chips[j]
            return pltpu.make_async_remote_copy(
                src_ref=half(a, idx, c), dst_ref=half(a, idx, c),
                send_sem=send_sem.at[a, j], recv_sem=recv_sem.at[a, j], device_id=(px, py, c), device_id_type=MESH)

        def d2d(a, j, idx, which):
            return pltpu.make_async_remote_copy(
                src_ref=half(a, idx, which), dst_ref=half(a, idx, which),
                send_sem=fsend_sem.at[a, j], recv_sem=frecv_sem.at[a, j], device_id=(x, y, 1 - c), device_id_type=MESH)

        return 2 * x + y, c, chips, ici, d2d

    def start(ins, outs, sems):
        me, _, _, ici, _ = legs(outs, sems)
        for a in range(na):
            for j in range(3):
                ici(a, j, me).start()

    def finish(ins, outs, sems):
        me, c, chips, ici, d2d = legs(outs, sems)
        passed = []
        for a in range(na):
            for j, (px, py) in enumerate(chips):
                ici(a, j, 2 * px + py).wait_recv()
                cp = d2d(a, j, 2 * px + py, c)
                cp.start()
                passed.append(cp)
        for a in range(na):
            for j, (px, py) in enumerate(chips):
                d2d(a, j, 2 * px + py, 1 - c).wait_recv()
        for a in range(na):
            for j in range(3):
                ici(a, j, me).wait_send()
        for cp in passed:
            cp.wait_send()

    return _Ride(bufs, [jax.ShapeDtypeStruct(b.shape, b.dtype) for b in bufs], [pltpu.SemaphoreType.DMA((na, 3))] * 4,
                 start, finish, {a: a for a in range(na)})


def _pair_ride(grads):
    na = len(grads)

    def copies(ins, outs, sems):
        send_sem, recv_sem = sems
        x, y, c, _ = _place()
        res = []
        for a in range(na):
            hr = ins[a].shape[1] // 2
            res.append(pltpu.make_async_remote_copy(
                src_ref=ins[a].at[:, pl.ds((1 - c) * hr, hr)], dst_ref=outs[a],
                send_sem=send_sem.at[a], recv_sem=recv_sem.at[a], device_id=(x, y, 1 - c), device_id_type=MESH))
        return res

    def start(ins, outs, sems):
        for cp in copies(ins, outs, sems):
            cp.start()

    def finish(ins, outs, sems):
        for cp in copies(ins, outs, sems):
            cp.wait()

    return _Ride(grads, [jax.ShapeDtypeStruct((g.shape[0], g.shape[1] // 2, g.shape[2]), g.dtype) for g in grads],
                 [pltpu.SemaphoreType.DMA((na,))] * 2, start, finish)


def _chip_ride(sums):
    na = len(sums)

    def copies(ins, outs, sems):
        send_sem, recv_sem = sems
        x, y, c, chips = _place()
        res = []
        for a in range(na):
            for j, (px, py) in enumerate(chips):
                res.append(pltpu.make_async_remote_copy(
                    src_ref=ins[a].at[2 * px + py], dst_ref=outs[a].at[j],
                    send_sem=send_sem.at[a, j], recv_sem=recv_sem.at[a, j], device_id=(px, py, c), device_id_type=MESH))
        return res

    def start(ins, outs, sems):
        for cp in copies(ins, outs, sems):
            cp.start()

    def finish(ins, outs, sems):
        for cp in copies(ins, outs, sems):
            cp.wait()

    return _Ride(sums, [jax.ShapeDtypeStruct((3,) + s.shape[1:], s.dtype) for s in sums],
                 [pltpu.SemaphoreType.DMA((na, 3))] * 2, start, finish)


def _finish_ride(grads, gpack=None):
    na = len(grads)

    def halves(outs, sems, which):
        x, y, c, _ = _place()
        res = []
        for a in range(na):
            hr = outs[a].shape[0] // 2
            rows = outs[a].at[pl.ds((c if which == "mine" else 1 - c) * hr, hr)]
            res.append(pltpu.make_async_remote_copy(
                src_ref=rows, dst_ref=rows, send_sem=sems[0].at[a], recv_sem=sems[1].at[a],
                device_id=(x, y, 1 - c), device_id_type=MESH))
        return res

    def gains(ins, outs, sems):
        x, y, c, _ = _place()
        dev = 4 * x + 2 * y + c
        g_in, g_out = ins[na], outs[na]
        own = pltpu.make_async_copy(g_in, g_out.at[dev], sems[2])
        sends, lands = [], []
        for k in range(N_DEV - 1):
            bx, by, bc = (k + 1) // 4, ((k + 1) // 2) % 2, (k + 1) % 2
            peer = (jnp.bitwise_xor(x, bx), jnp.bitwise_xor(y, by), jnp.bitwise_xor(c, bc))
            sends.append(pltpu.make_async_remote_copy(
                src_ref=g_in, dst_ref=g_out.at[dev], send_sem=sems[3].at[k], recv_sem=sems[4].at[k],
                device_id=peer, device_id_type=MESH))
            slot = g_out.at[jnp.bitwise_xor(dev, k + 1)]
            lands.append(pltpu.make_async_remote_copy(
                src_ref=slot, dst_ref=slot, send_sem=sems[3].at[k], recv_sem=sems[4].at[k],
                device_id=peer, device_id_type=MESH))
        return own, sends, lands

    def start(ins, outs, sems):
        for cp in halves(outs, sems, "mine"):
            cp.start()
        if gpack is not None:
            own, sends, _ = gains(ins, outs, sems)
            own.start()
            for cp in sends:
                cp.start()

    def finish(ins, outs, sems):
        for cp in halves(outs, sems, "sibling's"):
            cp.wait_recv()
        if gpack is not None:
            own, sends, lands = gains(ins, outs, sems)
            for cp in lands:
                cp.wait_recv()
            for cp in sends:
                cp.wait_send()
            own.wait()
        for cp in halves(outs, sems, "mine"):
            cp.wait_send()

    shapes = [jax.ShapeDtypeStruct(g.shape, g.dtype) for g in grads]
    sems = [pltpu.SemaphoreType.DMA((na,))] * 2
    if gpack is None:
        return _Ride(grads, shapes, sems, start, finish, {a: a for a in range(na)})
    return _Ride(list(grads) + [gpack], shapes + [jax.ShapeDtypeStruct((N_DEV,) + gpack.shape, gpack.dtype)],
                 sems + [pltpu.SemaphoreType.DMA, pltpu.SemaphoreType.DMA((N_DEV - 1,)), pltpu.SemaphoreType.DMA((N_DEV - 1,))],
                 start, finish, {a: a for a in range(na)})


def _pair_sum(place, grad, got, name):
    ns, r, cols = grad.shape
    hr = r // 2

    def body(place_ref, g_ref, r_ref, o_ref):
        o_ref[...] = (g_ref[...] + r_ref[...]).astype(BF16)

    return pl.pallas_call(
        body, name=name,
        grid_spec=pltpu.PrefetchScalarGridSpec(
            num_scalar_prefetch=1, grid=(ns,),
            in_specs=[pl.BlockSpec((None, hr, cols), lambda s, pr: (s, pr[1], 0)),
                      pl.BlockSpec((None, hr, cols), lambda s, pr: (s, 0, 0))],
            out_specs=pl.BlockSpec((None, hr, cols), lambda s, pr: (s, 0, 0))),
        out_shape=jax.ShapeDtypeStruct((ns, hr, cols), BF16),
        compiler_params=_params("arbitrary"),
    )(place, grad, got)


def _chip_sum(place, grad, got, others, name):
    ns, r, cols = grad.shape
    hr = r // 2
    nb = 2
    tr = hr // nb

    def body(place_ref, g_ref, r_ref, o3_ref, o_ref):
        acc = g_ref[...] + r_ref[...]
        for j in range(3):
            acc = acc + o3_ref[j].astype(F32)
        o_ref[...] = acc

    return pl.pallas_call(
        body, name=name,
        grid_spec=pltpu.PrefetchScalarGridSpec(
            num_scalar_prefetch=1, grid=(nb,),
            in_specs=[pl.BlockSpec((None, tr, cols), lambda i, pr: (pr[0], pr[1] * nb + i, 0)),
                      pl.BlockSpec((None, tr, cols), lambda i, pr: (pr[0], i, 0)),
                      pl.BlockSpec((3, tr, cols), lambda i, pr: (0, i, 0))],
            out_specs=pl.BlockSpec((tr, cols), lambda i, pr: (pr[1] * nb + i, 0))),
        out_shape=jax.ShapeDtypeStruct((r, cols), F32),
        compiler_params=_params("arbitrary"),
    )(place, grad, got, others)


def _pack_gains(parts, d):
    def body(*refs):
        ins, o_ref = refs[:-1], refs[-1]
        o_ref[...] = jnp.zeros_like(o_ref)
        for k, r in enumerate(ins):
            o_ref[k:k + 1, 0:r.shape[1]] = jnp.sum(r[...], axis=0, keepdims=True)

    return pl.pallas_call(
        body, name="pack_gains", out_shape=jax.ShapeDtypeStruct((GAIN_ROWS, d), F32),
    )(*parts)


def _adamw_math(w, g, m, v):
    m = ADAM_B1 * m + (1.0 - ADAM_B1) * g
    v = ADAM_B2 * v + (1.0 - ADAM_B2) * jnp.square(g)
    m_hat = m / (1.0 - ADAM_B1 ** ADAM_STEP)
    v_hat = v / (1.0 - ADAM_B2 ** ADAM_STEP)
    return -ADAM_LR * (m_hat / (jnp.sqrt(v_hat) + ADAM_EPS) + ADAM_WD * w), m, v


def _adamw(ws, gs, ms, vs):
    n = len(ws)

    def body(*refs):
        ins, outs = refs[:4 * n], refs[4 * n:]
        for k in range(n):
            w_ref, g_ref, m_ref, v_ref = ins[4 * k:4 * k + 4]
            d_ref, nm_ref, nv_ref = outs[3 * k:3 * k + 3]
            d_ref[...], nm_ref[...], nv_ref[...] = _adamw_math(w_ref[...], g_ref[...], m_ref[...], v_ref[...])

    parts = 2 * ROW_QUARTERS
    tile = lambda w: pl.BlockSpec((w.shape[0] // parts, w.shape[1]), lambda i: (i, 0))
    res = pl.pallas_call(
        body, name="adamw_shards", grid=(parts,),
        in_specs=[tile(w) for w in ws for _ in range(4)], out_specs=[tile(w) for w in ws for _ in range(3)],
        out_shape=[jax.ShapeDtypeStruct(w.shape, F32) for w in ws for _ in range(3)],
        compiler_params=_params("arbitrary"),
    )(*[a for quad in zip(ws, gs, ms, vs) for a in quad])
    return [res[3 * k:3 * k + 3] for k in range(n)]


def _adamw_gain(gall, row, w, m, v, name):
    n = w.shape[1]

    def body(ga_ref, w_ref, m_ref, v_ref, g_ref, d_ref, nm_ref, nv_ref):
        g = ga_ref[0, row:row + 1, 0:n]
        for k in range(1, N_DEV):
            g = g + ga_ref[k, row:row + 1, 0:n]
        g_ref[...] = g
        d_ref[...], nm_ref[...], nv_ref[...] = _adamw_math(w_ref[...], g, m_ref[...], v_ref[...])

    return pl.pallas_call(
        body, name=name, out_shape=[jax.ShapeDtypeStruct((1, n), F32)] * 4,
    )(gall, w, m, v)


def kernel(x, norm_ffn1, ffn1_w_gate, ffn1_w_up, ffn1_w_down, norm_mix, w_in, ret_norm_gain, w_out, norm_ffn2, ffn2_w_gate, ffn2_w_up, ffn2_w_down, norm_final, loss_target, m_norm_ffn1, m_ffn1_w_gate, m_ffn1_w_up, m_ffn1_w_down, m_norm_mix, m_w_in, m_ret_norm_gain, m_w_out, m_norm_ffn2, m_ffn2_w_gate, m_ffn2_w_up, m_ffn2_w_down, m_norm_final, v_norm_ffn1, v_ffn1_w_gate, v_ffn1_w_up, v_ffn1_w_down, v_norm_mix, v_w_in, v_ret_norm_gain, v_w_out, v_norm_ffn2, v_ffn2_w_gate, v_ffn2_w_up, v_ffn2_w_down, v_norm_final):
    d = x.shape[-1]
    mats = [ffn1_w_gate, ffn1_w_up, ffn1_w_down, w_in, w_out, ffn2_w_gate, ffn2_w_up, ffn2_w_down]
    mats_m = [m_ffn1_w_gate, m_ffn1_w_up, m_ffn1_w_down, m_w_in, m_w_out, m_ffn2_w_gate, m_ffn2_w_up, m_ffn2_w_down]
    mats_v = [v_ffn1_w_gate, v_ffn1_w_up, v_ffn1_w_down, v_w_in, v_w_out, v_ffn2_w_gate, v_ffn2_w_up, v_ffn2_w_down]
    mat_names = ["ffn1_w_gate", "ffn1_w_up", "ffn1_w_down", "w_in", "w_out", "ffn2_w_gate", "ffn2_w_up", "ffn2_w_down"]
    gains = [norm_ffn1, norm_mix, ret_norm_gain, norm_ffn2, norm_final.reshape(1, d)]
    gains_m = [m_norm_ffn1, m_norm_mix, m_ret_norm_gain, m_norm_ffn2, m_norm_final.reshape(1, d)]
    gains_v = [v_norm_ffn1, v_norm_mix, v_ret_norm_gain, v_norm_ffn2, v_norm_final.reshape(1, d)]
    gain_names = ["norm_ffn1", "norm_mix", "ret_norm_gain", "norm_ffn2", "norm_final"]

    turned = lambda n: n.endswith(("w_gate", "w_up"))
    local = lambda a, n: jnp.swapaxes(a, 1, 2)[0] if turned(n) else a[0]
    back = lambda a, n: jnp.swapaxes(a[None], 1, 2) if turned(n) else a[None]
    shards = [local(w, n) for w, n in zip(mats, mat_names)]
    place = jnp.stack([2 * lax.axis_index("x") + lax.axis_index("y"), lax.axis_index("c")]).astype(jnp.int32)
    placed = _place_shards(place, shards)
    loss_p, dx, shard_grads, gall = _step(x[0], loss_target[0], gains, placed, place)

    out_g, out_d, out_m, out_v = {}, {}, {}, {}
    updates = _adamw(shards, shard_grads, [local(m, n) for m, n in zip(mats_m, mat_names)],
                     [local(v, n) for v, n in zip(mats_v, mat_names)])
    for n, g, (dl, nm, nv) in zip(mat_names, shard_grads, updates):
        out_g[n], out_d[n], out_m[n], out_v[n] = [back(a, n) for a in (g, dl, nm, nv)]
    for row, (n, w, m, v) in enumerate(zip(gain_names, gains, gains_m, gains_v)):
        res = _adamw_gain(gall, row, w, m, v, f"adamw_{n}")
        shape = (d,) if n == "norm_final" else w.shape
        out_g[n], out_d[n], out_m[n], out_v[n] = [r.reshape(shape) for r in res]

    loss = lax.psum(jnp.sum(loss_p), ("x", "y", "c"))
    order = ["norm_ffn1", "ffn1_w_gate", "ffn1_w_up", "ffn1_w_down", "norm_mix", "w_in", "ret_norm_gain", "w_out",
             "norm_ffn2", "ffn2_w_gate", "ffn2_w_up", "ffn2_w_down", "norm_final"]
    return (loss, dx[None], *[out_g[n] for n in order], *[out_d[n] for n in order],
            *[out_m[n] for n in order], *[out_v[n] for n in order])
```

```python
import functools

import jax
import jax.numpy as jnp
from jax import lax
from jax.experimental import pallas as pl
from jax.experimental.pallas import tpu as pltpu

F32 = jnp.float32
BF16 = jnp.bfloat16
MESH = pl.DeviceIdType.MESH

NORM_EPS = 1e-6
GN_EPS = 1e-6
ROPE_BASE = 10000.0
RET_HEADS = 4
RET_DIM = 128
RET_WIDTH = 512
RET_CHUNK = 128
ATT_DIM = 64
ATT_WIDTH = 512
ATT_BLOCK = 128
DILATIONS = (1, 4, 16)
LANE = 128
N_SHARD = 4
ADAM_LR, ADAM_B1, ADAM_B2, ADAM_EPS, ADAM_WD, ADAM_STEP = 0.001, 0.9, 0.999, 1e-08, 0.01, 10

V7X_VMEM_BYTES = 64 * 1024 * 1024
VMEM_LIMIT = V7X_VMEM_BYTES - 8 * 1024 * 1024

NT = (((1,), (1,)), ((), ()))
TN = (((0,), (0,)), ((), ()))


def _params(*sem):
    return pltpu.CompilerParams(dimension_semantics=sem, vmem_limit_bytes=VMEM_LIMIT)


def _dot(a, b, dims=None):
    if dims is None:
        return jnp.dot(a, b, preferred_element_type=F32)
    return lax.dot_general(a, b, dims, preferred_element_type=F32)


def _sigmoid(x):
    return 1.0 / (1.0 + jnp.exp(-x))


def _load_weights(pairs, sems):
    copies = [pltpu.make_async_copy(src, dst, sems.at[k]) for k, (src, dst) in enumerate(pairs)]
    for cp in copies:
        cp.start()
    for cp in copies:
        cp.wait()


def _rows8(v):
    r, c = v.shape
    return v.reshape(r // 8, 8, c).sum(axis=0)


class _Ride:
    def __init__(self, inputs, out_shapes, sems, start, finish, aliases=None):
        self.inputs, self.out_shapes, self.sems = list(inputs), list(out_shapes), list(sems)
        self.start, self.finish, self.aliases = start, finish, dict(aliases or {})


def _pallas(body, rides, *, name, in_specs, out_specs, out_shape, args, grid=(), scratch_shapes=(), sem=()):
    rides = [r for r in (rides or []) if r is not None]
    n_in, n_out, n_scr = len(args), len(out_shape), len(scratch_shapes)
    hbm = pl.BlockSpec(memory_space=pl.ANY)
    r_in = [a for r in rides for a in r.inputs]
    r_out = [s for r in rides for s in r.out_shapes]
    r_sem = [s for r in rides for s in r.sems]
    aliases, spans, ki, ko, ks = {}, [], 0, 0, 0
    for r in rides:
        aliases.update({n_in + ki + i: n_out + ko + o for i, o in r.aliases.items()})
        spans.append((ki, ko, ks))
        ki, ko, ks = ki + len(r.inputs), ko + len(r.out_shapes), ks + len(r.sems)

    def wrapped(*refs):
        ins, rin = refs[:n_in], refs[n_in:n_in + len(r_in)]
        o0 = n_in + len(r_in)
        outs, rout = refs[o0:o0 + n_out], refs[o0 + n_out:o0 + n_out + len(r_out)]
        s0 = o0 + n_out + len(r_out)
        scr, rsem = refs[s0:s0 + n_scr], refs[s0 + n_scr:]
        part = lambda r, k: (rin[spans[k][0]:spans[k][0] + len(r.inputs)], rout[spans[k][1]:spans[k][1] + len(r.out_shapes)],
                             rsem[spans[k][2]:spans[k][2] + len(r.sems)])
        first = functools.reduce(jnp.logical_and, [pl.program_id(k) == 0 for k in range(len(grid))], True)
        last = functools.reduce(jnp.logical_and, [pl.program_id(k) == grid[k] - 1 for k in range(len(grid))], True)
        if rides:
            @pl.when(first)
            def _():
                for k, r in enumerate(rides):
                    r.start(*part(r, k))

        body(*ins, *outs, *scr)
        if rides:
            @pl.when(last)
            def _():
                for k, r in enumerate(rides):
                    r.finish(*part(r, k))

    res = pl.pallas_call(
        wrapped, name=name, grid=grid,
        in_specs=list(in_specs) + [hbm] * len(r_in), out_specs=list(out_specs) + [hbm] * len(r_out),
        out_shape=list(out_shape) + r_out, input_output_aliases=aliases,
        scratch_shapes=list(scratch_shapes) + r_sem,
        compiler_params=pltpu.CompilerParams(dimension_semantics=sem, vmem_limit_bytes=VMEM_LIMIT) if grid else None,
    )(*args, *r_in)
    extras = [list(res[n_out + ko:n_out + ko + len(r.out_shapes)]) for r, (_, ko, _) in zip(rides, spans)]
    return list(res[:n_out]), extras


def _run(ride, name):
    def body(*refs):
        n_in, n_out = len(ride.inputs), len(ride.out_shapes)
        parts = refs[:n_in], refs[n_in:n_in + n_out], refs[n_in + n_out:]
        ride.start(*parts)
        ride.finish(*parts)

    hbm = pl.BlockSpec(memory_space=pl.ANY)
    return list(pl.pallas_call(
        body, name=name, in_specs=[hbm] * len(ride.inputs), out_specs=[hbm] * len(ride.out_shapes),
        out_shape=ride.out_shapes, input_output_aliases=ride.aliases, scratch_shapes=ride.sems,
    )(*ride.inputs))


def _loss_head(hv, gain_ref, tg_ref, loss_ref, dgain_ref):
    d = hv.shape[1]
    r = lax.rsqrt(jnp.mean(hv * hv, axis=-1, keepdims=True) + NORM_EPS)
    xh = hv * r
    err = xh * gain_ref[...] - tg_ref[...]
    sq = _rows8(jnp.square(err))
    loss_ref[...] += 0.5 * functools.reduce(jnp.add, [sq[:, k * LANE:(k + 1) * LANE] for k in range(d // LANE)]) / d
    dy = err / d
    dgain_ref[...] += _rows8(dy * xh)
    dxh = dy * gain_ref[...]
    return r * (dxh - xh * jnp.mean(dxh * xh, axis=-1, keepdims=True))


V7X_MXU_TILE = 256
FFN_CHUNK_TILES = 3


def _hidden_chunks(f):
    step = FFN_CHUNK_TILES * V7X_MXU_TILE
    return [slice(s, min(s + step, f)) for s in range(0, f, step)]


def _flat(w):
    return w.reshape(w.shape[0] * w.shape[1], w.shape[2])


def _ffn_fwd(x, gain, wg, wu, wd, name, rides=None, head=None):
    t, d = x.shape
    wg, wu, wd = _flat(wg), _flat(wu), _flat(wd)
    f = wg.shape[0]
    tm = min(256, t)
    nh = 0 if head is None else 2

    def body(*refs):
        x_ref, gain_ref = refs[:2]
        wg_hbm, wu_hbm, wd_hbm, h_ref, xn_ref, g_ref, u_ref, a_ref = refs[2 + nh:10 + nh]
        sums = refs[10 + nh:12 + nh]
        wg_v, wu_v, wd_v, sems = refs[-4:]

        @pl.when(pl.program_id(0) == 0)
        def _():
            _load_weights([(wg_hbm, wg_v), (wu_hbm, wu_v), (wd_hbm, wd_v)], sems)
            if head is not None:
                for s_ref in sums:
                    s_ref[...] = jnp.zeros_like(s_ref)

        xv = x_ref[...]
        r = lax.rsqrt(jnp.mean(xv * xv, axis=-1, keepdims=True) + NORM_EPS)
        xn = (xv * r * gain_ref[...]).astype(BF16)
        xn_ref[...] = xn
        acc = jnp.zeros((tm, d), F32)
        for c in _hidden_chunks(f):
            g = _dot(xn, wg_v[c, :], NT)
            u = _dot(xn, wu_v[c, :], NT)
            g_ref[:, c] = g.astype(BF16)
            u_ref[:, c] = u.astype(BF16)
            a = (g * _sigmoid(g) * u).astype(BF16)
            a_ref[:, c] = a
            acc = acc + _dot(a, wd_v[c, :])
        hv = xv + 0.5 * acc
        h_ref[...] = hv if head is None else _loss_head(hv, refs[2], refs[3], *sums)

    hbm = pl.BlockSpec(memory_space=pl.ANY)
    hid = pl.BlockSpec((tm, f), lambda i: (i, 0))
    tile = pl.BlockSpec((tm, d), lambda i: (i, 0))
    row = pl.BlockSpec((1, d), lambda i: (0, 0))
    sums = [] if head is None else [(pl.BlockSpec((8, LANE), lambda i: (0, 0)), jax.ShapeDtypeStruct((8, LANE), F32)),
                                    (pl.BlockSpec((8, d), lambda i: (0, 0)), jax.ShapeDtypeStruct((8, d), F32))]
    return _pallas(
        body, rides, name=name, grid=(t // tm,),
        in_specs=[tile, row] + ([] if head is None else [row, tile]) + [hbm, hbm, hbm],
        out_specs=[tile, tile, hid, hid, hid] + [s for s, _ in sums],
        out_shape=[jax.ShapeDtypeStruct((t, d), F32), jax.ShapeDtypeStruct((t, d), BF16)]
        + [jax.ShapeDtypeStruct((t, f), BF16)] * 3 + [s for _, s in sums],
        scratch_shapes=[pltpu.VMEM(wg.shape, BF16), pltpu.VMEM(wu.shape, BF16), pltpu.VMEM(wd.shape, BF16),
                        pltpu.SemaphoreType.DMA((3,))],
        sem=("arbitrary",), args=[x, gain] + ([] if head is None else list(head)) + [wg, wu, wd])


def _ffn_bwd_data(dy, x, gain, g, u, wg, wu, wd, name, rides=None):
    t, d = x.shape
    wg, wu, wd = _flat(wg), _flat(wu), _flat(wd)
    f = wg.shape[0]
    tm = min(256, t)

    def body(dy_ref, x_ref, gain_ref, g_ref, u_ref, wg_hbm, wu_hbm, wd_hbm, dx_ref, dg_ref, du_ref, dgain_ref,
             wg_v, wu_v, wd_v, sems):
        @pl.when(pl.program_id(0) == 0)
        def _():
            _load_weights([(wg_hbm, wg_v), (wu_hbm, wu_v), (wd_hbm, wd_v)], sems)
            dgain_ref[...] = jnp.zeros_like(dgain_ref)

        dyv = dy_ref[...]
        dyh = (0.5 * dyv).astype(BF16)
        dxn = jnp.zeros((tm, d), F32)
        chunks = _hidden_chunks(f)
        das = [_dot(dyh, wd_v[c, :], NT) for c in chunks]
        for c, da in zip(chunks, das):
            gj = g_ref[:, c].astype(F32)
            uj = u_ref[:, c].astype(F32)
            sig = _sigmoid(gj)
            dgj = (da * uj * (sig * (1.0 + gj * (1.0 - sig)))).astype(BF16)
            duj = (da * (gj * sig)).astype(BF16)
            dg_ref[:, c] = dgj
            du_ref[:, c] = duj
            dxn = dxn + _dot(dgj, wg_v[c, :]) + _dot(duj, wu_v[c, :])
        xv = x_ref[...]
        r = lax.rsqrt(jnp.mean(xv * xv, axis=-1, keepdims=True) + NORM_EPS)
        xh = xv * r
        dgain_ref[...] += _rows8(dxn * xh)
        dxh = dxn * gain_ref[...]
        dx_ref[...] = dyv + r * (dxh - xh * jnp.mean(dxh * xh, axis=-1, keepdims=True))

    hbm = pl.BlockSpec(memory_space=pl.ANY)
    tile = pl.BlockSpec((tm, d), lambda i: (i, 0))
    hid = pl.BlockSpec((tm, f), lambda i: (i, 0))
    return _pallas(
        body, rides, name=name, grid=(t // tm,),
        in_specs=[tile, tile, pl.BlockSpec((1, d), lambda i: (0, 0)), hid, hid, hbm, hbm, hbm],
        out_specs=[tile, hid, hid, pl.BlockSpec((8, d), lambda i: (0, 0))],
        out_shape=[jax.ShapeDtypeStruct((t, d), F32), jax.ShapeDtypeStruct((t, f), BF16),
                   jax.ShapeDtypeStruct((t, f), BF16), jax.ShapeDtypeStruct((8, d), F32)],
        scratch_shapes=[pltpu.VMEM(wg.shape, BF16), pltpu.VMEM(wu.shape, BF16), pltpu.VMEM(wd.shape, BF16),
                        pltpu.SemaphoreType.DMA((3,))],
        sem=("arbitrary",), args=[dy, x, gain, g, u, wg, wu, wd])


WGRAD_ROW_BLOCKS = 2


def _ffn_wgrad_down(a, dy, name, rides=None):
    t, d = dy.shape
    f = a.shape[1]
    fb = f // WGRAD_ROW_BLOCKS
    tk = min(1024, t)

    def body(dy_ref, a_ref, dwd_ref):
        @pl.when(pl.program_id(1) == 0)
        def _():
            dwd_ref[...] = jnp.zeros_like(dwd_ref)

        dwd_ref[...] += _dot(a_ref[...], (0.5 * dy_ref[...]).astype(BF16), TN)

    return _pallas(
        body, rides, name=name, grid=(WGRAD_ROW_BLOCKS, t // tk),
        in_specs=[pl.BlockSpec((tk, d), lambda j, k: (k, 0)), pl.BlockSpec((tk, fb), lambda j, k: (k, j))],
        out_specs=[pl.BlockSpec((fb, d), lambda j, k: (j, 0))],
        out_shape=[jax.ShapeDtypeStruct((f, d), F32)],
        sem=("arbitrary", "arbitrary"), args=[dy, a])


def _ffn_wgrad_gu(xn, dhs, name, rides=None):
    t, d = xn.shape
    n = len(dhs)
    f = dhs[0].shape[1]
    fb = f // WGRAD_ROW_BLOCKS
    tk = min(2048 // n, t)

    def body(xn_ref, *refs):
        @pl.when(pl.program_id(1) == 0)
        def _():
            for o_ref in refs[n:]:
                o_ref[...] = jnp.zeros_like(o_ref)

        xnv = xn_ref[...]
        for dh_ref, o_ref in zip(refs[:n], refs[n:]):
            o_ref[...] += _dot(dh_ref[...], xnv, TN)

    hid = pl.BlockSpec((tk, fb), lambda j, k: (k, j))
    out = pl.BlockSpec((fb, d), lambda j, k: (j, 0))
    return _pallas(
        body, rides, name=name, grid=(WGRAD_ROW_BLOCKS, t // tk),
        in_specs=[pl.BlockSpec((tk, d), lambda j, k: (k, 0))] + [hid] * n,
        out_specs=[out] * n, out_shape=[jax.ShapeDtypeStruct((f, d), F32)] * n,
        sem=("arbitrary", "arbitrary"), args=[xn] + list(dhs))


def _tn_matmul(a, b, bn, name):
    t, m = a.shape
    n = b.shape[1]
    tk = min(2048, t)

    def body(a_ref, b_ref, o_ref):
        @pl.when(pl.program_id(1) == 0)
        def _():
            o_ref[...] = jnp.zeros_like(o_ref)

        o_ref[...] += _dot(a_ref[...].astype(BF16), b_ref[...].astype(BF16), TN)

    return pl.pallas_call(
        body, name=name, grid=(n // bn, t // tk),
        in_specs=[pl.BlockSpec((tk, m), lambda j, k: (k, 0)), pl.BlockSpec((tk, bn), lambda j, k: (k, j))],
        out_specs=pl.BlockSpec((None, m, bn), lambda j, k: (j, 0, 0)),
        out_shape=jax.ShapeDtypeStruct((n // bn, m, bn), F32),
        compiler_params=_params("arbitrary", "arbitrary"),
    )(a, b)


def _chunk_scratch(tm, w):
    return pltpu.VMEM((w // LANE, tm, LANE), F32)


def _regroup_store(cbuf, out_ref, dil):
    n = out_ref.shape[1]
    for g in range(dil):
        for k in range(cbuf.shape[0]):
            rows = cbuf[k] if dil == 1 else cbuf[k, pl.ds(g, n, stride=dil), :]
            out_ref[g, :, k * LANE:(k + 1) * LANE] = rows.astype(out_ref.dtype)


def _natural_rows(ref, dil, cbuf):
    if dil == 1:
        return ref[0].astype(F32)
    n = ref.shape[1]
    for g in range(dil):
        for k in range(cbuf.shape[0]):
            cbuf[k, pl.ds(g, n, stride=dil), :] = ref[g, :, k * LANE:(k + 1) * LANE].astype(F32)
    return jnp.concatenate([cbuf[k] for k in range(cbuf.shape[0])], axis=1)


def _inproj_fwd(h, gain, win):
    t, d = h.shape
    ns, _, cs = win.shape
    tm = min(512, t)
    rw, aw = 4 * RET_WIDTH, 3 * ATT_WIDTH

    def body(h_ref, gain_ref, w_ref, xn_ref, ur_ref, *rest):
        a_refs, abuf = rest[:-1], rest[-1]
        hv = h_ref[...]
        r = lax.rsqrt(jnp.mean(hv * hv, axis=-1, keepdims=True) + NORM_EPS)
        xn = (hv * r * gain_ref[...]).astype(BF16)
        xn_ref[...] = xn
        for j in range(ns):
            res = _dot(xn, w_ref[j])
            for k in range(cs // LANE):
                chunk = j * (cs // LANE) + k
                piece = res[:, k * LANE:(k + 1) * LANE]
                if chunk < rw // LANE:
                    ur_ref[:, chunk * LANE:(chunk + 1) * LANE] = piece
                else:
                    abuf[chunk - rw // LANE] = piece
        for dil, a_ref in zip(DILATIONS, a_refs):
            _regroup_store(abuf, a_ref, dil)

    return pl.pallas_call(
        body, name="inproj_fwd", grid=(t // tm,),
        in_specs=[pl.BlockSpec((tm, d), lambda i: (i, 0)), pl.BlockSpec((1, d), lambda i: (0, 0)),
                  pl.BlockSpec(win.shape, lambda i: (0, 0, 0))],
        out_specs=[pl.BlockSpec((tm, d), lambda i: (i, 0)), pl.BlockSpec((tm, rw), lambda i: (i, 0))]
        + [pl.BlockSpec((dil, tm // dil, aw), lambda i: (0, i, 0)) for dil in DILATIONS],
        out_shape=[jax.ShapeDtypeStruct((t, d), BF16), jax.ShapeDtypeStruct((t, rw), F32)]
        + [jax.ShapeDtypeStruct((dil, t // dil, aw), BF16) for dil in DILATIONS],
        scratch_shapes=[_chunk_scratch(tm, aw)],
        compiler_params=_params("arbitrary"),
    )(h, gain, win)


def _inproj_bwd(pieces, parts, h, gain, dres, win):
    t, d = h.shape
    ns, _, cs = win.shape
    pw = pieces[0].shape[1]
    tm = min(512, t)
    npc, nk = len(pieces), len(parts[0])
    flat_parts = [a for p in parts for a in p]

    def body(*refs):
        p_refs, a_refs = refs[:npc], refs[npc:npc + len(flat_parts)]
        h_ref, gain_ref, dres_ref, w_ref, dh_ref, du_ref, dgain_ref, buf = refs[npc + len(flat_parts):]

        @pl.when(pl.program_id(0) == 0)
        def _():
            dgain_ref[...] = jnp.zeros_like(dgain_ref)

        for k in range(npc):
            du_ref[:, k * pw:(k + 1) * pw] = p_refs[k][...]
        for k in range(nk):
            acc = None
            for b, dil in enumerate(DILATIONS):
                rows = _natural_rows(a_refs[b * nk + k], dil, buf)
                acc = rows if acc is None else acc + rows
            du_ref[:, (npc + k) * pw:(npc + k + 1) * pw] = acc.astype(BF16)
        dxn = jnp.zeros((tm, d), F32)
        for j in range(ns):
            dxn = dxn + _dot(du_ref[:, j * cs:(j + 1) * cs], w_ref[j], NT)
        hv = h_ref[...]
        r = lax.rsqrt(jnp.mean(hv * hv, axis=-1, keepdims=True) + NORM_EPS)
        xh = hv * r
        dgain_ref[...] += _rows8(dxn * xh)
        dxh = dxn * gain_ref[...]
        dh_ref[...] = dres_ref[...] + r * (dxh - xh * jnp.mean(dxh * xh, axis=-1, keepdims=True))

    tile = pl.BlockSpec((tm, d), lambda i: (i, 0))
    cols = (npc + nk) * pw
    return pl.pallas_call(
        body, name="inproj_bwd", grid=(t // tm,),
        in_specs=[pl.BlockSpec((tm, pw), lambda i: (i, 0))] * npc
        + [_regrouped_spec(tm, dil, pw) for dil in DILATIONS for _ in range(nk)]
        + [tile, pl.BlockSpec((1, d), lambda i: (0, 0)), tile, pl.BlockSpec(win.shape, lambda i: (0, 0, 0))],
        out_specs=[tile, pl.BlockSpec((tm, cols), lambda i: (i, 0)), pl.BlockSpec((8, d), lambda i: (0, 0))],
        out_shape=[jax.ShapeDtypeStruct((t, d), F32), jax.ShapeDtypeStruct((t, cols), BF16),
                   jax.ShapeDtypeStruct((8, d), F32)],
        scratch_shapes=[_chunk_scratch(tm, pw)],
        compiler_params=_params("arbitrary"),
    )(*pieces, *flat_parts, h, gain, dres, win)


def _outproj_fwd(h, mix_r, mix_a, wo):
    t, d = h.shape
    hw = mix_r.shape[1]
    tm = min(512, t)

    def body(h_ref, mr_ref, ma_ref, w_ref, o_ref):
        o_ref[...] = h_ref[...] + _dot(mr_ref[...], w_ref[0:hw, :]) + _dot(ma_ref[...], w_ref[hw:2 * hw, :])

    tile = pl.BlockSpec((tm, d), lambda i: (i, 0))
    half = pl.BlockSpec((tm, hw), lambda i: (i, 0))
    return pl.pallas_call(
        body, name="outproj_fwd", grid=(t // tm,),
        in_specs=[tile, half, half, pl.BlockSpec(wo.shape, lambda i: (0, 0))],
        out_specs=tile, out_shape=jax.ShapeDtypeStruct((t, d), F32),
        compiler_params=_params("arbitrary"),
    )(h, mix_r, mix_a, wo)


def _outproj_bwd(dh, wo, rides=None):
    t, d = dh.shape
    hw = wo.shape[0] // 2
    tm = min(512, t)

    def body(dh_ref, w_ref, dr_ref, da_ref):
        dhb = dh_ref[...].astype(BF16)
        dr_ref[...] = _dot(dhb, w_ref[0:hw, :], NT)
        da_ref[...] = _dot(dhb, w_ref[hw:2 * hw, :], NT)

    half = pl.BlockSpec((tm, hw), lambda i: (i, 0))
    return _pallas(
        body, rides, name="outproj_bwd", grid=(t // tm,),
        in_specs=[pl.BlockSpec((tm, d), lambda i: (i, 0)), pl.BlockSpec(wo.shape, lambda i: (0, 0))],
        out_specs=[half, half],
        out_shape=[jax.ShapeDtypeStruct((t, hw), F32), jax.ShapeDtypeStruct((t, hw), F32)],
        sem=("arbitrary",), args=[dh, wo])


def _retention_tables(t):
    pos = jnp.arange(t, dtype=F32)
    pair = (jnp.arange(RET_DIM) // 2 * 2).astype(F32)
    ang = pos[:, None] * (ROPE_BASE ** (-pair / RET_DIM))[None, :]
    c = RET_CHUNK
    log_g = jnp.log(1.0 - 2.0 ** (-5.0 - jnp.arange(RET_HEADS, dtype=F32)))
    idx = jnp.arange(c, dtype=F32)
    rel = idx[:, None] - idx[None, :]
    decay = jnp.where(rel >= 0, jnp.exp(log_g[:, None, None] * jnp.maximum(rel, 0.0)), 0.0)
    zeta = jnp.exp(log_g[:, None] * (c - 1 - idx)[None, :])
    xi = jnp.exp(log_g[:, None] * (idx + 1)[None, :])
    gc = jnp.exp(log_g * c)
    wide = lambda v: jnp.broadcast_to(v[:, :, None], (RET_HEADS, c, LANE))
    return (jnp.cos(ang), jnp.sin(ang), decay, wide(zeta), wide(xi),
            jnp.broadcast_to(gc[:, None, None], (RET_HEADS, c, LANE)))


def _rot(v):
    lane = lax.broadcasted_iota(jnp.int32, v.shape, 1)
    nxt = pltpu.roll(v, LANE - 1, 1)
    prv = pltpu.roll(v, 1, 1)
    return jnp.where(lane % 2 == 0, -nxt, prv)


def _ret_specs(tr, rev, nt):
    ti = (lambda i: nt - 1 - i) if rev else (lambda i: i)
    col = lambda blk: pl.BlockSpec((tr, RET_WIDTH), lambda i: (ti(i), blk))
    tab = pl.BlockSpec((tr, LANE), lambda i: (ti(i), 0))
    head = pl.BlockSpec((RET_HEADS, RET_CHUNK, LANE), lambda i: (0, 0, 0))
    return col, tab, head


def _ret_chunks(tr, rev=False):
    order = list(range(tr // RET_CHUNK))
    return [(pl.ds(ci * RET_CHUNK, RET_CHUNK), slice(h * RET_DIM, (h + 1) * RET_DIM), h)
            for h in range(RET_HEADS) for ci in (reversed(order) if rev else order)]


def _ret_operands(items, q_ref, k_ref, v_ref, cos_ref, sin_ref, zeta_ref):
    scale = RET_DIM ** -0.5
    qbs, kbs, vbs, kzs = [], [], [], []
    for sl, hs, h in items:
        cs, sn = cos_ref[sl, :], sin_ref[sl, :]
        q, k = q_ref[sl, hs], k_ref[sl, hs]
        kr = (k * cs + _rot(k) * sn) * scale
        qbs.append((q * cs + _rot(q) * sn).astype(BF16))
        kbs.append(kr.astype(BF16))
        vbs.append(v_ref[sl, hs].astype(BF16))
        kzs.append((kr * zeta_ref[h]).astype(BF16))
    return qbs, kbs, vbs, kzs


def _ret_states(items, state, steps, gc_ref):
    cur, befores = {}, []
    for (sl, hs, h), step in zip(items, steps):
        st = cur[h] if h in cur else state[h]
        befores.append(st)
        cur[h] = st * gc_ref[h] + step
    for h, st in cur.items():
        state[h] = st
    return befores


def _ret_fwd(u, gain, tabs):
    t = u.shape[0]
    tr = min(512, t)
    nt = t // tr
    cos, sin, decay, zeta, xi, gc = tabs

    def body(q_ref, k_ref, v_ref, gt_ref, cos_ref, sin_ref, gain_ref, dec_ref, zeta_ref, xi_ref, gc_ref,
             raw_ref, mix_ref, state):
        @pl.when(pl.program_id(0) == 0)
        def _():
            state[...] = jnp.zeros_like(state)

        items = _ret_chunks(tr)
        n = range(len(items))
        qbs, kbs, vbs, kzs = _ret_operands(items, q_ref, k_ref, v_ref, cos_ref, sin_ref, zeta_ref)
        ss = [_dot(qbs[i], kbs[i], NT) for i in n]
        kvs = [_dot(kzs[i], vbs[i], TN) for i in n]
        befores = _ret_states(items, state, kvs, gc_ref)
        intra = [_dot((ss[i] * dec_ref[items[i][2]]).astype(BF16), vbs[i]) for i in n]
        inter = [_dot(qbs[i], befores[i].astype(BF16)) for i in n]
        for i, (sl, hs, h) in enumerate(items):
            o = intra[i] + inter[i] * xi_ref[h]
            raw_ref[sl, hs] = o
            mu = jnp.mean(o, axis=-1, keepdims=True)
            var = jnp.mean(jnp.square(o - mu), axis=-1, keepdims=True)
            y = (o - mu) * lax.rsqrt(var + GN_EPS) * gain_ref[:, hs]
            gt = gt_ref[sl, hs]
            mix_ref[sl, hs] = (y * (gt * _sigmoid(gt))).astype(BF16)

    col, tab, head = _ret_specs(tr, False, nt)
    out = pl.BlockSpec((tr, RET_WIDTH), lambda i: (i, 0))
    return pl.pallas_call(
        body, name="ret_fwd", grid=(nt,),
        in_specs=[col(0), col(1), col(2), col(3), tab, tab, pl.BlockSpec((1, RET_WIDTH), lambda i: (0, 0)),
                  head, head, head, head],
        out_specs=[out, out],
        out_shape=[jax.ShapeDtypeStruct((t, RET_WIDTH), F32), jax.ShapeDtypeStruct((t, RET_WIDTH), BF16)],
        scratch_shapes=[pltpu.VMEM((RET_HEADS, RET_DIM, RET_DIM), F32)],
        compiler_params=_params("arbitrary"),
    )(u, u, u, u, cos, sin, gain, decay, zeta, xi, gc)


def _ret_bwd_q(dmix, raw, u, gain, tabs, rides=None):
    t = u.shape[0]
    tr = min(512, t)
    nt = t // tr
    cos, sin, decay, zeta, xi, gc = tabs

    def body(dm_ref, raw_ref, q_ref, k_ref, v_ref, gt_ref, cos_ref, sin_ref, gain_ref, dec_ref, zeta_ref, xi_ref, gc_ref,
             dq_ref, dgt_ref, dret_ref, dgain_ref, state):
        @pl.when(pl.program_id(0) == 0)
        def _():
            state[...] = jnp.zeros_like(state)
            dgain_ref[...] = jnp.zeros_like(dgain_ref)

        items = _ret_chunks(tr)
        n_items = range(len(items))
        qbs, kbs, vbs, kzs = _ret_operands(items, q_ref, k_ref, v_ref, cos_ref, sin_ref, zeta_ref)
        dos, dgains = [], {}
        for sl, hs, h in items:
            o = raw_ref[sl, hs]
            mu = jnp.mean(o, axis=-1, keepdims=True)
            var = jnp.mean(jnp.square(o - mu), axis=-1, keepdims=True)
            rs = lax.rsqrt(var + GN_EPS)
            n = (o - mu) * rs
            gt = gt_ref[sl, hs]
            sig = _sigmoid(gt)
            dout = dm_ref[sl, hs]
            gain_h = gain_ref[:, hs]
            dgt_ref[sl, hs] = (dout * (n * gain_h) * (sig * (1.0 + gt * (1.0 - sig)))).astype(BF16)
            dy = dout * (gt * sig)
            dgains[h] = dgains[h] + _rows8(dy * n) if h in dgains else _rows8(dy * n)
            dn = dy * gain_h
            do = rs * (dn - jnp.mean(dn, axis=-1, keepdims=True) - n * jnp.mean(dn * n, axis=-1, keepdims=True))
            dret_ref[sl, hs] = do
            dos.append(do)
        for h, dg in dgains.items():
            dgain_ref[:, h * RET_DIM:(h + 1) * RET_DIM] += dg
        dss = [_dot(dos[i].astype(BF16), vbs[i], NT) for i in n_items]
        kvs = [_dot(kzs[i], vbs[i], TN) for i in n_items]
        befores = _ret_states(items, state, kvs, gc_ref)
        intra = [_dot((dss[i] * dec_ref[items[i][2]]).astype(BF16), kbs[i]) for i in n_items]
        inter = [_dot((dos[i] * xi_ref[items[i][2]]).astype(BF16), befores[i].astype(BF16), NT) for i in n_items]
        for i, (sl, hs, h) in enumerate(items):
            dqr = intra[i] + inter[i]
            dq_ref[sl, hs] = (dqr * cos_ref[sl, :] - _rot(dqr * sin_ref[sl, :])).astype(BF16)

    col, tab, head = _ret_specs(tr, False, nt)
    out = pl.BlockSpec((tr, RET_WIDTH), lambda i: (i, 0))
    return _pallas(
        body, rides, name="ret_bwd_q", grid=(nt,),
        in_specs=[out, out, col(0), col(1), col(2), col(3), tab, tab, pl.BlockSpec((1, RET_WIDTH), lambda i: (0, 0)),
                  head, head, head, head],
        out_specs=[out, out, out, pl.BlockSpec((8, RET_WIDTH), lambda i: (0, 0))],
        out_shape=[jax.ShapeDtypeStruct((t, RET_WIDTH), BF16), jax.ShapeDtypeStruct((t, RET_WIDTH), BF16),
                   jax.ShapeDtypeStruct((t, RET_WIDTH), F32), jax.ShapeDtypeStruct((8, RET_WIDTH), F32)],
        scratch_shapes=[pltpu.VMEM((RET_HEADS, RET_DIM, RET_DIM), F32)],
        sem=("arbitrary",), args=[dmix, raw, u, u, u, u, cos, sin, gain, decay, zeta, xi, gc])


def _ret_bwd_kv(dret, u, tabs, rides=None):
    t = u.shape[0]
    tr = min(512, t)
    nt = t // tr
    cos, sin, decay, zeta, xi, gc = tabs
    scale = RET_DIM ** -0.5

    def body(do_ref, q_ref, k_ref, v_ref, cos_ref, sin_ref, dec_ref, zeta_ref, xi_ref, gc_ref, dk_ref, dv_ref, gst):
        @pl.when(pl.program_id(0) == 0)
        def _():
            gst[...] = jnp.zeros_like(gst)

        items = _ret_chunks(tr, rev=True)
        n = range(len(items))
        qbs, kbs, vbs, kzs = _ret_operands(items, q_ref, k_ref, v_ref, cos_ref, sin_ref, zeta_ref)
        dos = [do_ref[sl, hs] for sl, hs, h in items]
        dobs = [do.astype(BF16) for do in dos]
        ss = [_dot(qbs[i], kbs[i], NT) for i in n]
        dss = [_dot(dobs[i], vbs[i], NT) for i in n]
        steps = [_dot(qbs[i], (dos[i] * xi_ref[items[i][2]]).astype(BF16), TN) for i in n]
        afters = [g.astype(BF16) for g in _ret_states(items, gst, steps, gc_ref)]
        dvs = [_dot((ss[i] * dec_ref[items[i][2]]).astype(BF16), dobs[i], TN) + _dot(kzs[i], afters[i]) for i in n]
        dks = [_dot((dss[i] * dec_ref[items[i][2]]).astype(BF16), qbs[i], TN) for i in n]
        dkz = [_dot(vbs[i], afters[i], NT) for i in n]
        for i, (sl, hs, h) in enumerate(items):
            dv_ref[sl, hs] = dvs[i].astype(BF16)
            dkr = (dks[i] + dkz[i] * zeta_ref[h]) * scale
            dk_ref[sl, hs] = (dkr * cos_ref[sl, :] - _rot(dkr * sin_ref[sl, :])).astype(BF16)

    col, tab, head = _ret_specs(tr, True, nt)
    out = pl.BlockSpec((tr, RET_WIDTH), lambda i: (nt - 1 - i, 0))
    return _pallas(
        body, rides, name="ret_bwd_kv", grid=(nt,),
        in_specs=[out, col(0), col(1), col(2), tab, tab, head, head, head, head],
        out_specs=[out, out],
        out_shape=[jax.ShapeDtypeStruct((t, RET_WIDTH), BF16), jax.ShapeDtypeStruct((t, RET_WIDTH), BF16)],
        scratch_shapes=[pltpu.VMEM((RET_HEADS, RET_DIM, RET_DIM), F32)],
        sem=("arbitrary",), args=[dret, u, u, u, cos, sin, decay, zeta, xi, gc])


PAIRS = ATT_WIDTH // LANE
ATT_Q_BLK, ATT_K_BLK, ATT_V_BLK = 0, PAIRS, 2 * PAIRS
STAT_LANES = ATT_DIM // 2


def _att_tiles(t, dil):
    sub = t // dil
    tq = min(512, sub)
    return sub, tq, sub // tq, tq // ATT_BLOCK


def _att_in_specs(tq, qb, ti):
    cur = lambda off: pl.BlockSpec((None, tq, LANE), lambda g, p, i: (g, ti(i), off + p))
    prev = lambda off: pl.BlockSpec((None, ATT_BLOCK, LANE), lambda g, p, i: (g, jnp.maximum(ti(i) * qb - 1, 0), off + p))
    return [cur(ATT_Q_BLK), cur(ATT_K_BLK), prev(ATT_K_BLK), cur(ATT_V_BLK), prev(ATT_V_BLK)]


def _band_mask():
    key = lax.broadcasted_iota(jnp.int32, (2 * ATT_BLOCK, 2 * ATT_BLOCK), 0)
    qry = lax.broadcasted_iota(jnp.int32, (2 * ATT_BLOCK, 2 * ATT_BLOCK), 1) % ATT_BLOCK
    dist = qry + ATT_BLOCK - key
    return (dist >= 0) & (dist <= ATT_BLOCK), key >= ATT_BLOCK


def _head0_lanes():
    return lax.broadcasted_iota(jnp.int32, (ATT_BLOCK, LANE), 1) < ATT_DIM


def _stack_heads(v, head0):
    zero = jnp.zeros((), v.dtype)
    return jnp.concatenate([jnp.where(head0, v, zero), jnp.where(head0, zero, v)], axis=0)


def _unstack_heads(v, head0):
    return jnp.where(head0, v[0:ATT_BLOCK], v[ATT_BLOCK:])


def _att_fwd(ua, dil):
    sub = ua.shape[1]
    _, tq, nq, qb = _att_tiles(sub * dil, dil)

    def body(q_ref, kc_ref, kp_ref, vc_ref, vp_ref, o_ref, l_ref, kx, vx):
        tile = pl.program_id(2)
        kx[0:ATT_BLOCK, :] = kp_ref[...]
        kx[ATT_BLOCK:, :] = kc_ref[...]
        vx[0:ATT_BLOCK, :] = vp_ref[...]
        vx[ATT_BLOCK:, :] = vc_ref[...]
        band, cur_keys = _band_mask()
        head0 = _head0_lanes()
        blocks = range(qb)
        rows = [slice(b * ATT_BLOCK, (b + 1) * ATT_BLOCK) for b in blocks]
        keys = [slice(b * ATT_BLOCK, (b + 2) * ATT_BLOCK) for b in blocks]
        sts = [_dot(kx[keys[b], :], _stack_heads(q_ref[rows[b], :] * jnp.asarray(ATT_DIM ** -0.5, BF16), head0), NT)
               for b in blocks]
        pts, lses = [], []
        for b in blocks:
            mask = band if b > 0 else band & (cur_keys | (tile > 0))
            st = jnp.where(mask, sts[b], -1e30)
            m = jnp.max(st, axis=0, keepdims=True)
            ex = jnp.exp(st - m)
            den = jnp.sum(ex, axis=0, keepdims=True)
            pts.append((ex * (1.0 / den)).astype(BF16))
            lses.append(m + jnp.log(den))
        outs = [_dot(pts[b], vx[keys[b], :], TN) for b in blocks]
        for b in blocks:
            o_ref[rows[b], :] = _unstack_heads(outs[b], head0).astype(BF16)
            cols = [jnp.broadcast_to(lses[b][:, e * ATT_BLOCK:(e + 1) * ATT_BLOCK], (ATT_BLOCK, LANE)).T for e in range(2)]
            l_ref[rows[b], :] = jnp.where(head0, cols[0], cols[1])

    out = pl.BlockSpec((None, tq, LANE), lambda g, p, i: (g, i, p))
    return pl.pallas_call(
        body, name=f"att_fwd_d{dil}", grid=(dil, PAIRS, nq),
        in_specs=_att_in_specs(tq, qb, lambda i: i),
        out_specs=[out, out],
        out_shape=[jax.ShapeDtypeStruct((dil, sub, ATT_WIDTH), BF16), jax.ShapeDtypeStruct((dil, sub, ATT_WIDTH), F32)],
        scratch_shapes=[pltpu.VMEM((tq + ATT_BLOCK, LANE), BF16)] * 2,
        compiler_params=_params("arbitrary", "arbitrary", "arbitrary"),
    )(ua, ua, ua, ua, ua)


def _regrouped_spec(tm, dil, w):
    return pl.BlockSpec((dil, tm // dil, w), lambda i: (0, i, 0))


def _att_combine(outs, lses, t):
    w = ATT_WIDTH
    tm = min(512, t)
    nb = len(outs)

    def body(*refs):
        o_refs, l_refs = refs[:nb], refs[nb:2 * nb]
        mix_ref, att_ref, lse_ref, buf = refs[2 * nb:]
        ls = [_natural_rows(r, dil, buf) for r, dil in zip(l_refs, DILATIONS)]
        m = functools.reduce(jnp.maximum, ls)
        ws = [jnp.exp(l - m) for l in ls]
        den = functools.reduce(jnp.add, ws)
        att = functools.reduce(jnp.add, [(wt / den) * _natural_rows(r, dil, buf) for wt, r, dil in zip(ws, o_refs, DILATIONS)])
        att_ref[...] = att
        mix_ref[...] = att.astype(BF16)
        lse_ref[...] = m + jnp.log(den)

    tile = pl.BlockSpec((tm, w), lambda i: (i, 0))
    regrouped = [_regrouped_spec(tm, dil, w) for dil in DILATIONS]
    return pl.pallas_call(
        body, name="att_combine", grid=(t // tm,),
        in_specs=regrouped * 2, out_specs=[tile, tile, tile],
        out_shape=[jax.ShapeDtypeStruct((t, w), BF16), jax.ShapeDtypeStruct((t, w), F32), jax.ShapeDtypeStruct((t, w), F32)],
        scratch_shapes=[_chunk_scratch(tm, w)],
        compiler_params=_params("arbitrary"),
    )(*outs, *lses)


def _att_bwd_prep(datt, att, lse):
    t, w = datt.shape
    tm = min(512, t)

    def body(da_ref, at_ref, l_ref, *rest):
        outs, dbuf, sbuf = rest[:-2], rest[-2], rest[-1]
        dav = da_ref[...]
        prod = dav * at_ref[...]
        lane = lax.broadcasted_iota(jnp.int32, (tm, LANE), 1)
        for k in range(w // LANE):
            cols = slice(k * LANE, (k + 1) * LANE)
            dbuf[k] = dav[:, cols]
            delta = jnp.concatenate(
                [jnp.broadcast_to(jnp.sum(prod[:, k * LANE + e * ATT_DIM:k * LANE + (e + 1) * ATT_DIM], axis=-1, keepdims=True),
                                  (tm, ATT_DIM)) for e in range(LANE // ATT_DIM)], axis=1)
            sbuf[k] = jnp.where(lane % ATT_DIM < STAT_LANES, l_ref[:, cols], delta)
        for k, dil in enumerate(DILATIONS):
            _regroup_store(dbuf, outs[2 * k], dil)
            _regroup_store(sbuf, outs[2 * k + 1], dil)

    tile = pl.BlockSpec((tm, w), lambda i: (i, 0))
    res = pl.pallas_call(
        body, name="att_bwd_prep", grid=(t // tm,),
        in_specs=[tile] * 3,
        out_specs=[_regrouped_spec(tm, dil, w) for dil in DILATIONS for _ in range(2)],
        out_shape=[jax.ShapeDtypeStruct((dil, t // dil, w), dt) for dil in DILATIONS for dt in (BF16, F32)],
        scratch_shapes=[_chunk_scratch(tm, w)] * 2,
        compiler_params=_params("arbitrary"),
    )(datt, att, lse)
    return [(res[2 * k], res[2 * k + 1]) for k in range(len(DILATIONS))]


def _att_bwd(ua, da, stat, dil, rides=None):
    sub = ua.shape[1]
    _, tq, nq, qb = _att_tiles(sub * dil, dil)
    scale = ATT_DIM ** -0.5

    def body(q_ref, kc_ref, kp_ref, vc_ref, vp_ref, da_ref, st_ref, dq_ref, dk_ref, dv_ref, kx, vx, ck, cv):
        step = pl.program_id(2)
        tile = nq - 1 - step

        @pl.when(step == 0)
        def _():
            ck[...] = jnp.zeros_like(ck)
            cv[...] = jnp.zeros_like(cv)

        kx[0:ATT_BLOCK, :] = kp_ref[...]
        kx[ATT_BLOCK:, :] = kc_ref[...]
        vx[0:ATT_BLOCK, :] = vp_ref[...]
        vx[ATT_BLOCK:, :] = vc_ref[...]
        band, cur_keys = _band_mask()
        head0 = _head0_lanes()
        blocks = range(qb)
        rows = [slice(b * ATT_BLOCK, (b + 1) * ATT_BLOCK) for b in blocks]
        keys = [slice(b * ATT_BLOCK, (b + 2) * ATT_BLOCK) for b in blocks]
        qqs = [_stack_heads(q_ref[rows[b], :] * jnp.asarray(scale, BF16), head0) for b in blocks]
        dds = [_stack_heads(da_ref[rows[b], :], head0) for b in blocks]
        sts = [_dot(kx[keys[b], :], qqs[b], NT) for b in blocks]
        dpts = [_dot(vx[keys[b], :], dds[b], NT) for b in blocks]
        pts, dsts = [], []
        for b in blocks:
            mask = band if b > 0 else band & (cur_keys | (tile > 0))
            stat = st_ref[rows[b], :].T
            row = lambda k: jnp.concatenate([stat[e * ATT_DIM + k:e * ATT_DIM + k + 1, :] for e in range(2)], axis=1)
            pt = jnp.where(mask, jnp.exp(sts[b] - row(0)), 0.0)
            dsts.append((pt * (dpts[b] - row(STAT_LANES))).astype(BF16))
            pts.append(pt.astype(BF16))
        dqs = [_dot(dsts[b], kx[keys[b], :], TN) for b in blocks]
        dkbs = [_dot(dsts[b], qqs[b]) for b in blocks]
        dvbs = [_dot(pts[b], dds[b]) for b in blocks]
        for b in blocks:
            dq_ref[rows[b], :] = (_unstack_heads(dqs[b], head0) * scale).astype(BF16)
        for b in blocks[1:]:
            dk_ref[rows[b - 1], :] = (dkbs[b - 1][ATT_BLOCK:] + dkbs[b][0:ATT_BLOCK]).astype(BF16)
            dv_ref[rows[b - 1], :] = (dvbs[b - 1][ATT_BLOCK:] + dvbs[b][0:ATT_BLOCK]).astype(BF16)
        before_k, before_v = dkbs[0][0:ATT_BLOCK], dvbs[0][0:ATT_BLOCK]
        open_k, open_v = dkbs[-1][ATT_BLOCK:], dvbs[-1][ATT_BLOCK:]
        last = slice(tq - ATT_BLOCK, tq)
        dk_ref[last, :] = (open_k + ck[...]).astype(BF16)
        dv_ref[last, :] = (open_v + cv[...]).astype(BF16)
        ck[...] = before_k
        cv[...] = before_v

    ti = lambda i: nq - 1 - i
    out = pl.BlockSpec((None, tq, LANE), lambda g, p, i: (g, ti(i), p))
    shape = jax.ShapeDtypeStruct((dil, sub, ATT_WIDTH), BF16)
    return _pallas(
        body, rides, name=f"att_bwd_d{dil}", grid=(dil, PAIRS, nq),
        in_specs=_att_in_specs(tq, qb, ti) + [out, out],
        out_specs=[out, out, out], out_shape=[shape] * 3,
        scratch_shapes=[pltpu.VMEM((tq + ATT_BLOCK, LANE), BF16)] * 2 + [pltpu.VMEM((ATT_BLOCK, LANE), F32)] * 2,
        sem=("arbitrary", "arbitrary", "arbitrary"), args=[ua, ua, ua, ua, ua, da, stat])


class _Reduction:
    def __init__(self, place, names, grads):
        self.place, self.names, self.grads = place, names, grads

    def pair(self):
        return _pair_ride(self.grads)

    def chips(self, got):
        self.got = got
        return _chip_ride([_pair_sum(self.place, g, r, f"pair_sum_{n}") for g, r, n in zip(self.grads, got, self.names)])

    def halves(self, others):
        return [_chip_sum(self.place, g, r, o, f"chip_sum_{n}")
                for g, r, o, n in zip(self.grads, self.got, others, self.names)]


def _step(x, target, gains, w, place=None):
    t = x.shape[0]
    ex = place is not None
    g_ffn1, g_mix, g_ret, g_ffn2, g_fin = gains
    w = list(w)
    tabs = _retention_tables(t)
    red = lambda names, grads: _Reduction(place, names, grads) if ex else None
    ride = lambda r: [r] if ex else None

    if ex:
        w[0:3] = _run(_gather_ride(w[0:3]), "gather_ffn1_weights")
    (h1, xn1, *hid1, act1), rest = _ffn_fwd(x, g_ffn1, *w[0:3], "ffn1_fwd", ride(_gather_ride(w[3:])) if ex else None)
    if ex:
        w[3:] = rest[0]
    wg1, wu1, wd1, win, wo, wg2, wu2, wd2 = w
    wo2 = wo.reshape(wo.shape[0] * wo.shape[1], wo.shape[2])
    xnm, u, *uas = _inproj_fwd(h1, g_mix, win)
    raw, mix_r = _ret_fwd(u, g_ret, tabs)
    branches = [_att_fwd(ua, dil) for ua, dil in zip(uas, DILATIONS)]
    mix_a, att, lse = _att_combine([b[0] for b in branches], [b[1] for b in branches], t)
    h2 = _outproj_fwd(h1, mix_r, mix_a, wo2)
    (dh3, xn2, *hid2, act2, loss_p, dg_fin), _ = _ffn_fwd(h2, g_ffn2, wg2, wu2, wd2, "ffn2_fwd", head=(g_fin, target))

    (dwd2,), _ = _ffn_wgrad_down(act2, dh3, "ffn2_wgrad_down")
    dwd2 = dwd2.reshape(wd2.shape)
    r_d2 = red(["ffn2_w_down"], [dwd2])
    (dh2, dga2, dua2, dg_ffn2), e = _ffn_bwd_data(dh3, h2, g_ffn2, *hid2, wg2, wu2, wd2, "ffn2_bwd",
                                                  ex and [r_d2.pair()])
    (dwg2, dwu2), e = _ffn_wgrad_gu(xn2, [dga2, dua2], "ffn2_wgrad_gu", ex and [r_d2.chips(e[0])])
    dwg2, dwu2 = dwg2.reshape(wg2.shape), dwu2.reshape(wu2.shape)
    r_gu2 = red(["ffn2_w_gate", "ffn2_w_up"], [dwg2, dwu2])
    (dmix_r, dmix_a), e = _outproj_bwd(dh2, wo2, ex and [r_gu2.pair(), _finish_ride(r_d2.halves(e[0]))])
    if ex:
        got_gu2, (dwd2,) = e
    hw = RET_WIDTH // (wo.shape[1])
    dwo = jnp.concatenate([_tn_matmul(mix_r, dh2, dh2.shape[1], "wo_grad_r").reshape(hw, wo.shape[1], wo.shape[2]),
                           _tn_matmul(mix_a, dh2, dh2.shape[1], "wo_grad_a").reshape(hw, wo.shape[1], wo.shape[2])])
    r_wo = red(["w_out"], [dwo])
    (dq_r, dgt_r, dret, dg_ret), e = _ret_bwd_q(dmix_r, raw, u, g_ret, tabs, ex and [r_gu2.chips(got_gu2)])
    (dk_r, dv_r), e = _ret_bwd_kv(dret, u, tabs, ex and [r_wo.pair(), _finish_ride(r_gu2.halves(e[0]))])
    if ex:
        got_wo, (dwg2, dwu2) = e
    prep = _att_bwd_prep(dmix_a, att, lse)
    p1, e = _att_bwd(uas[0], *prep[0], DILATIONS[0], ex and [r_wo.chips(got_wo)])
    p4, e = _att_bwd(uas[1], *prep[1], DILATIONS[1], ex and [_finish_ride(r_wo.halves(e[0]))])
    if ex:
        (dwo,), = e
    p16, _ = _att_bwd(uas[2], *prep[2], DILATIONS[2])
    dh1, du, dg_mix = _inproj_bwd([dq_r, dk_r, dv_r, dgt_r], [p1, p4, p16], h1, g_mix, dh2, win)
    dwin = _tn_matmul(xnm, du, win.shape[2], "win_grad")
    r_in = red(["w_in"], [dwin])
    (dwd1,), e = _ffn_wgrad_down(act1, dh1, "ffn1_wgrad_down", ex and [r_in.pair()])
    dwd1 = dwd1.reshape(wd1.shape)
    r_d1 = red(["ffn1_w_down"], [dwd1])
    (dx, dga1, dua1, dg_ffn1), e = _ffn_bwd_data(dh1, x, g_ffn1, *hid1, wg1, wu1, wd1, "ffn1_bwd",
                                                  ex and [r_in.chips(e[0]), r_d1.pair()])
    (dwg1,), e = _ffn_wgrad_gu(xn1, [dga1], "ffn1_wgrad_gate", ex and [_finish_ride(r_in.halves(e[0])), r_d1.chips(e[1])])
    dwg1 = dwg1.reshape(wg1.shape)
    if ex:
        (dwin,), oth_d1 = e
        r_g1 = red(["ffn1_w_gate"], [dwg1])
        got_g1 = _run(r_g1.pair(), "pair_exchange_ffn1_gate")
    (dwu1,), e = _ffn_wgrad_gu(xn1, [dua1], "ffn1_wgrad_up", ex and [r_g1.chips(got_g1)])
    dwu1 = dwu1.reshape(wu1.shape)
    gain_parts = [dg_ffn1, dg_mix, dg_ret, dg_ffn2, dg_fin]
    if not ex:
        return loss_p, dx, [dwg1, dwu1, dwd1, dwin, dwo, dwg2, dwu2, dwd2], gain_parts
    r_u1 = red(["ffn1_w_up"], [dwu1])
    got_u1 = _run(r_u1.pair(), "pair_exchange_ffn1_up")
    oth_u1 = _run(r_u1.chips(got_u1), "chip_exchange_ffn1_up")
    last = r_g1.halves(e[0]) + r_u1.halves(oth_u1) + r_d1.halves(oth_d1)
    dwg1, dwu1, dwd1, gall = _run(_finish_ride(last, _pack_gains(gain_parts, x.shape[1])), "finish_exchange_ffn1")
    return loss_p, dx, [dwg1, dwu1, dwd1, dwin, dwo, dwg2, dwu2, dwd2], gall


N_DEV = 8
GAIN_ROWS = 8


def _place():
    x, y, c = lax.axis_index("x"), lax.axis_index("y"), lax.axis_index("c")
    chips = [(1 - x, y), (x, 1 - y), (1 - x, 1 - y)]
    return x, y, c, chips


ROW_QUARTERS = 4


def _place_shards(place, ws):
    n = len(ws)

    def body(place_ref, *refs):
        for w_ref, o_ref in zip(refs[:n], refs[n:]):
            o_ref[...] = w_ref[...].astype(BF16)

    quarter = lambda w: (w.shape[0] // ROW_QUARTERS, w.shape[1])
    return pl.pallas_call(
        body, name="place_shards",
        grid_spec=pltpu.PrefetchScalarGridSpec(
            num_scalar_prefetch=1, grid=(ROW_QUARTERS,),
            in_specs=[pl.BlockSpec(quarter(w), lambda i, pr: (i, 0)) for w in ws],
            out_specs=[pl.BlockSpec((None,) + quarter(w), lambda i, pr: (pr[0], i, 0)) for w in ws]),
        out_shape=[jax.ShapeDtypeStruct((N_SHARD,) + w.shape, BF16) for w in ws],
        compiler_params=_params("arbitrary"),
    )(place, *ws)


def _gather_ride(bufs):
    na = len(bufs)

    def legs(outs, sems):
        send_sem, recv_sem, fsend_sem, frecv_sem = sems
        x, y, c, chips = _place()

        def half(a, idx, which):
            hr = outs[a].shape[1] // 2
            return outs[a].at[idx, pl.ds(which * hr, hr)]

        def ici(a, j, idx):
            px, py = chips[j]
            return pltpu.make_async_remote_copy(
                src_ref=half(a, idx, c), dst_ref=half(a, idx, c),
                send_sem=send_sem.at[a, j], recv_sem=recv_sem.at[a, j], device_id=(px, py, c), device_id_type=MESH)

        def d2d(a, j, idx, which):
            return pltpu.make_async_remote_copy(
                src_ref=half(a, idx, which), dst_ref=half(a, idx, which),
                send_sem=fsend_sem.at[a, j], recv_sem=frecv_sem.at[a, j], device_id=(x, y, 1 - c), device_id_type=MESH)

        return 2 * x + y, c, chips, ici, d2d

    def start(ins, outs, sems):
        me, _, _, ici, _ = legs(outs, sems)
        for a in range(na):
            for j in range(3):
                ici(a, j, me).start()

    def finish(ins, outs, sems):
        me, c, chips, ici, d2d = legs(outs, sems)
        passed = []
        for a in range(na):
            for j, (px, py) in enumerate(chips):
                ici(a, j, 2 * px + py).wait_recv()
                cp = d2d(a, j, 2 * px + py, c)
                cp.start()
                passed.append(cp)
        for a in range(na):
            for j, (px, py) in enumerate(chips):
                d2d(a, j, 2 * px + py, 1 - c).wait_recv()
        for a in range(na):
            for j in range(3):
                ici(a, j, me).wait_send()
        for cp in passed:
            cp.wait_send()

    return _Ride(bufs, [jax.ShapeDtypeStruct(b.shape, b.dtype) for b in bufs], [pltpu.SemaphoreType.DMA((na, 3))] * 4,
                 start, finish, {a: a for a in range(na)})


def _pair_ride(grads):
    na = len(grads)

    def copies(ins, outs, sems):
        send_sem, recv_sem = sems
        x, y, c, _ = _place()
        res = []
        for a in range(na):
            hr = ins[a].shape[1] // 2
            res.append(pltpu.make_async_remote_copy(
                src_ref=ins[a].at[:, pl.ds((1 - c) * hr, hr)], dst_ref=outs[a],
                send_sem=send_sem.at[a], recv_sem=recv_sem.at[a], device_id=(x, y, 1 - c), device_id_type=MESH))
        return res

    def start(ins, outs, sems):
        for cp in copies(ins, outs, sems):
            cp.start()

    def finish(ins, outs, sems):
        for cp in copies(ins, outs, sems):
            cp.wait()

    return _Ride(grads, [jax.ShapeDtypeStruct((g.shape[0], g.shape[1] // 2, g.shape[2]), g.dtype) for g in grads],
                 [pltpu.SemaphoreType.DMA((na,))] * 2, start, finish)


def _chip_ride(sums):
    na = len(sums)

    def copies(ins, outs, sems):
        send_sem, recv_sem = sems
        x, y, c, chips = _place()
        res = []
        for a in range(na):
            for j, (px, py) in enumerate(chips):
                res.append(pltpu.make_async_remote_copy(
                    src_ref=ins[a].at[2 * px + py], dst_ref=outs[a].at[j],
                    send_sem=send_sem.at[a, j], recv_sem=recv_sem.at[a, j], device_id=(px, py, c), device_id_type=MESH))
        return res

    def start(ins, outs, sems):
        for cp in copies(ins, outs, sems):
            cp.start()

    def finish(ins, outs, sems):
        for cp in copies(ins, outs, sems):
            cp.wait()

    return _Ride(sums, [jax.ShapeDtypeStruct((3,) + s.shape[1:], s.dtype) for s in sums],
                 [pltpu.SemaphoreType.DMA((na, 3))] * 2, start, finish)


def _finish_ride(grads, gpack=None):
    na = len(grads)

    def halves(outs, sems, which):
        x, y, c, _ = _place()
        res = []
        for a in range(na):
            hr = outs[a].shape[0] // 2
            rows = outs[a].at[pl.ds((c if which == "mine" else 1 - c) * hr, hr)]
            res.append(pltpu.make_async_remote_copy(
                src_ref=rows, dst_ref=rows, send_sem=sems[0].at[a], recv_sem=sems[1].at[a],
                device_id=(x, y, 1 - c), device_id_type=MESH))
        return res

    def gains(ins, outs, sems):
        x, y, c, _ = _place()
        dev = 4 * x + 2 * y + c
        g_in, g_out = ins[na], outs[na]
        own = pltpu.make_async_copy(g_in, g_out.at[dev], sems[2])
        sends, lands = [], []
        for k in range(N_DEV - 1):
            bx, by, bc = (k + 1) // 4, ((k + 1) // 2) % 2, (k + 1) % 2
            peer = (jnp.bitwise_xor(x, bx), jnp.bitwise_xor(y, by), jnp.bitwise_xor(c, bc))
            sends.append(pltpu.make_async_remote_copy(
                src_ref=g_in, dst_ref=g_out.at[dev], send_sem=sems[3].at[k], recv_sem=sems[4].at[k],
                device_id=peer, device_id_type=MESH))
            slot = g_out.at[jnp.bitwise_xor(dev, k + 1)]
            lands.append(pltpu.make_async_remote_copy(
                src_ref=slot, dst_ref=slot, send_sem=sems[3].at[k], recv_sem=sems[4].at[k],
                device_id=peer, device_id_type=MESH))
        return own, sends, lands

    def start(ins, outs, sems):
        for cp in halves(outs, sems, "mine"):
            cp.start()
        if gpack is not None:
            own, sends, _ = gains(ins, outs, sems)
            own.start()
            for cp in sends:
                cp.start()

    def finish(ins, outs, sems):
        for cp in halves(outs, sems, "sibling's"):
            cp.wait_recv()
        if gpack is not None:
            own, sends, lands = gains(ins, outs, sems)
            for cp in lands:
                cp.wait_recv()
            for cp in sends:
                cp.wait_send()
            own.wait()
        for cp in halves(outs, sems, "mine"):
            cp.wait_send()

    shapes = [jax.ShapeDtypeStruct(g.shape, g.dtype) for g in grads]
    sems = [pltpu.SemaphoreType.DMA((na,))] * 2
    if gpack is None:
        return _Ride(grads, shapes, sems, start, finish, {a: a for a in range(na)})
    return _Ride(list(grads) + [gpack], shapes + [jax.ShapeDtypeStruct((N_DEV,) + gpack.shape, gpack.dtype)],
                 sems + [pltpu.SemaphoreType.DMA, pltpu.SemaphoreType.DMA((N_DEV - 1,)), pltpu.SemaphoreType.DMA((N_DEV - 1,))],
                 start, finish, {a: a for a in range(na)})


def _pair_sum(place, grad, got, name):
    ns, r, cols = grad.shape
    hr = r // 2

    def body(place_ref, g_ref, r_ref, o_ref):
        o_ref[...] = (g_ref[...] + r_ref[...]).astype(BF16)

    return pl.pallas_call(
        body, name=name,
        grid_spec=pltpu.PrefetchScalarGridSpec(
            num_scalar_prefetch=1, grid=(ns,),
            in_specs=[pl.BlockSpec((None, hr, cols), lambda s, pr: (s, pr[1], 0)),
                      pl.BlockSpec((None, hr, cols), lambda s, pr: (s, 0, 0))],
            out_specs=pl.BlockSpec((None, hr, cols), lambda s, pr: (s, 0, 0))),
        out_shape=jax.ShapeDtypeStruct((ns, hr, cols), BF16),
        compiler_params=_params("arbitrary"),
    )(place, grad, got)


def _chip_sum(place, grad, got, others, name):
    ns, r, cols = grad.shape
    hr = r // 2
    nb = 2
    tr = hr // nb

    def body(place_ref, g_ref, r_ref, o3_ref, o_ref):
        acc = g_ref[...] + r_ref[...]
        for j in range(3):
            acc = acc + o3_ref[j].astype(F32)
        o_ref[...] = acc

    return pl.pallas_call(
        body, name=name,
        grid_spec=pltpu.PrefetchScalarGridSpec(
            num_scalar_prefetch=1, grid=(nb,),
            in_specs=[pl.BlockSpec((None, tr, cols), lambda i, pr: (pr[0], pr[1] * nb + i, 0)),
                      pl.BlockSpec((None, tr, cols), lambda i, pr: (pr[0], i, 0)),
                      pl.BlockSpec((3, tr, cols), lambda i, pr: (0, i, 0))],
            out_specs=pl.BlockSpec((tr, cols), lambda i, pr: (pr[1] * nb + i, 0))),
        out_shape=jax.ShapeDtypeStruct((r, cols), F32),
        compiler_params=_params("arbitrary"),
    )(place, grad, got, others)


def _pack_gains(parts, d):
    def body(*refs):
        ins, o_ref = refs[:-1], refs[-1]
        o_ref[...] = jnp.zeros_like(o_ref)
        for k, r in enumerate(ins):
            o_ref[k:k + 1, 0:r.shape[1]] = jnp.sum(r[...], axis=0, keepdims=True)

    return pl.pallas_call(
        body, name="pack_gains", out_shape=jax.ShapeDtypeStruct((GAIN_ROWS, d), F32),
    )(*parts)


def _adamw_math(w, g, m, v):
    m = ADAM_B1 * m + (1.0 - ADAM_B1) * g
    v = ADAM_B2 * v + (1.0 - ADAM_B2) * jnp.square(g)
    m_hat = m / (1.0 - ADAM_B1 ** ADAM_STEP)
    v_hat = v / (1.0 - ADAM_B2 ** ADAM_STEP)
    return -ADAM_LR * (m_hat / (jnp.sqrt(v_hat) + ADAM_EPS) + ADAM_WD * w), m, v


def _adamw(ws, gs, ms, vs):
    n = len(ws)

    def body(*refs):
        ins, outs = refs[:4 * n], refs[4 * n:]
        for k in range(n):
            w_ref, g_ref, m_ref, v_ref = ins[4 * k:4 * k + 4]
            d_ref, nm_ref, nv_ref = outs[3 * k:3 * k + 3]
            d_ref[...], nm_ref[...], nv_ref[...] = _adamw_math(w_ref[...], g_ref[...], m_ref[...], v_ref[...])

    parts = 2 * ROW_QUARTERS
    tile = lambda w: pl.BlockSpec((w.shape[0] // parts, w.shape[1]), lambda i: (i, 0))
    res = pl.pallas_call(
        body, name="adamw_shards", grid=(parts,),
        in_specs=[tile(w) for w in ws for _ in range(4)], out_specs=[tile(w) for w in ws for _ in range(3)],
        out_shape=[jax.ShapeDtypeStruct(w.shape, F32) for w in ws for _ in range(3)],
        compiler_params=_params("arbitrary"),
    )(*[a for quad in zip(ws, gs, ms, vs) for a in quad])
    return [res[3 * k:3 * k + 3] for k in range(n)]


def _adamw_gain(gall, row, w, m, v, name):
    n = w.shape[1]

    def body(ga_ref, w_ref, m_ref, v_ref, g_ref, d_ref, nm_ref, nv_ref):
        g = ga_ref[0, row:row + 1, 0:n]
        for k in range(1, N_DEV):
            g = g + ga_ref[k, row:row + 1, 0:n]
        g_ref[...] = g
        d_ref[...], nm_ref[...], nv_ref[...] = _adamw_math(w_ref[...], g, m_ref[...], v_ref[...])

    return pl.pallas_call(
        body, name=name, out_shape=[jax.ShapeDtypeStruct((1, n), F32)] * 4,
    )(gall, w, m, v)


def kernel(x, norm_ffn1, ffn1_w_gate, ffn1_w_up, ffn1_w_down, norm_mix, w_in, ret_norm_gain, w_out, norm_ffn2, ffn2_w_gate, ffn2_w_up, ffn2_w_down, norm_final, loss_target, m_norm_ffn1, m_ffn1_w_gate, m_ffn1_w_up, m_ffn1_w_down, m_norm_mix, m_w_in, m_ret_norm_gain, m_w_out, m_norm_ffn2, m_ffn2_w_gate, m_ffn2_w_up, m_ffn2_w_down, m_norm_final, v_norm_ffn1, v_ffn1_w_gate, v_ffn1_w_up, v_ffn1_w_down, v_norm_mix, v_w_in, v_ret_norm_gain, v_w_out, v_norm_ffn2, v_ffn2_w_gate, v_ffn2_w_up, v_ffn2_w_down, v_norm_final):
    d = x.shape[-1]
    mats = [ffn1_w_gate, ffn1_w_up, ffn1_w_down, w_in, w_out, ffn2_w_gate, ffn2_w_up, ffn2_w_down]
    mats_m = [m_ffn1_w_gate, m_ffn1_w_up, m_ffn1_w_down, m_w_in, m_w_out, m_ffn2_w_gate, m_ffn2_w_up, m_ffn2_w_down]
    mats_v = [v_ffn1_w_gate, v_ffn1_w_up, v_ffn1_w_down, v_w_in, v_w_out, v_ffn2_w_gate, v_ffn2_w_up, v_ffn2_w_down]
    mat_names = ["ffn1_w_gate", "ffn1_w_up", "ffn1_w_down", "w_in", "w_out", "ffn2_w_gate", "ffn2_w_up", "ffn2_w_down"]
    gains = [norm_ffn1, norm_mix, ret_norm_gain, norm_ffn2, norm_final.reshape(1, d)]
    gains_m = [m_norm_ffn1, m_norm_mix, m_ret_norm_gain, m_norm_ffn2, m_norm_final.reshape(1, d)]
    gains_v = [v_norm_ffn1, v_norm_mix, v_ret_norm_gain, v_norm_ffn2, v_norm_final.reshape(1, d)]
    gain_names = ["norm_ffn1", "norm_mix", "ret_norm_gain", "norm_ffn2", "norm_final"]

    turned = lambda n: n.endswith(("w_gate", "w_up"))
    local = lambda a, n: jnp.swapaxes(a, 1, 2)[0] if turned(n) else a[0]
    back = lambda a, n: jnp.swapaxes(a[None], 1, 2) if turned(n) else a[None]
    shards = [local(w, n) for w, n in zip(mats, mat_names)]
    place = jnp.stack([2 * lax.axis_index("x") + lax.axis_index("y"), lax.axis_index("c")]).astype(jnp.int32)
    placed = _place_shards(place, shards)
    loss_p, dx, shard_grads, gall = _step(x[0], loss_target[0], gains, placed, place)

    out_g, out_d, out_m, out_v = {}, {}, {}, {}
    updates = _adamw(shards, shard_grads, [local(m, n) for m, n in zip(mats_m, mat_names)],
                     [local(v, n) for v, n in zip(mats_v, mat_names)])
    for n, g, (dl, nm, nv) in zip(mat_names, shard_grads, updates):
        out_g[n], out_d[n], out_m[n], out_v[n] = [back(a, n) for a in (g, dl, nm, nv)]
    for row, (n, w, m, v) in enumerate(zip(gain_names, gains, gains_m, gains_v)):
        res = _adamw_gain(gall, row, w, m, v, f"adamw_{n}")
        shape = (d,) if n == "norm_final" else w.shape
        out_g[n], out_d[n], out_m[n], out_v[n] = [r.reshape(shape) for r in res]

    loss = lax.psum(jnp.sum(loss_p), ("x", "y", "c"))
    order = ["norm_ffn1", "ffn1_w_gate", "ffn1_w_up", "ffn1_w_down", "norm_mix", "w_in", "ret_norm_gain", "w_out",
             "norm_ffn2", "ffn2_w_gate", "ffn2_w_up", "ffn2_w_down", "norm_final"]
    return (loss, dx[None], *[out_g[n] for n in order], *[out_d[n] for n in order],
            *[out_m[n] for n in order], *[out_v[n] for n in order])
```

```python
import functools

import jax
import jax.numpy as jnp
from jax import lax
from jax.experimental import pallas as pl
from jax.experimental.pallas import tpu as pltpu

F32 = jnp.float32
BF16 = jnp.bfloat16
MESH = pl.DeviceIdType.MESH

NORM_EPS = 1e-6
GN_EPS = 1e-6
ROPE_BASE = 10000.0
RET_HEADS = 4
RET_DIM = 128
RET_WIDTH = 512
RET_CHUNK = 128
ATT_DIM = 64
ATT_WIDTH = 512
ATT_BLOCK = 128
DILATIONS = (1, 4, 16)
LANE = 128
N_SHARD = 4
ADAM_LR, ADAM_B1, ADAM_B2, ADAM_EPS, ADAM_WD, ADAM_STEP = 0.001, 0.9, 0.999, 1e-08, 0.01, 10

V7X_VMEM_BYTES = 64 * 1024 * 1024
VMEM_LIMIT = V7X_VMEM_BYTES - 8 * 1024 * 1024

NT = (((1,), (1,)), ((), ()))
TN = (((0,), (0,)), ((), ()))


def _params(*sem):
    return pltpu.CompilerParams(dimension_semantics=sem, vmem_limit_bytes=VMEM_LIMIT)


def _dot(a, b, dims=None):
    if dims is None:
        return jnp.dot(a, b, preferred_element_type=F32)
    return lax.dot_general(a, b, dims, preferred_element_type=F32)


def _sigmoid(x):
    return 1.0 / (1.0 + jnp.exp(-x))


def _load_weights(pairs, sems):
    copies = [pltpu.make_async_copy(src, dst, sems.at[k]) for k, (src, dst) in enumerate(pairs)]
    for cp in copies:
        cp.start()
    for cp in copies:
        cp.wait()


def _rows8(v):
    r, c = v.shape
    return v.reshape(r // 8, 8, c).sum(axis=0)


class _Ride:
    def __init__(self, inputs, out_shapes, sems, start, finish, aliases=None):
        self.inputs, self.out_shapes, self.sems = list(inputs), list(out_shapes), list(sems)
        self.start, self.finish, self.aliases = start, finish, dict(aliases or {})


def _pallas(body, rides, *, name, in_specs, out_specs, out_shape, args, grid=(), scratch_shapes=(), sem=()):
    rides = [r for r in (rides or []) if r is not None]
    n_in, n_out, n_scr = len(args), len(out_shape), len(scratch_shapes)
    hbm = pl.BlockSpec(memory_space=pl.ANY)
    r_in = [a for r in rides for a in r.inputs]
    r_out = [s for r in rides for s in r.out_shapes]
    r_sem = [s for r in rides for s in r.sems]
    aliases, spans, ki, ko, ks = {}, [], 0, 0, 0
    for r in rides:
        aliases.update({n_in + ki + i: n_out + ko + o for i, o in r.aliases.items()})
        spans.append((ki, ko, ks))
        ki, ko, ks = ki + len(r.inputs), ko + len(r.out_shapes), ks + len(r.sems)

    def wrapped(*refs):
        ins, rin = refs[:n_in], refs[n_in:n_in + len(r_in)]
        o0 = n_in + len(r_in)
        outs, rout = refs[o0:o0 + n_out], refs[o0 + n_out:o0 + n_out + len(r_out)]
        s0 = o0 + n_out + len(r_out)
        scr, rsem = refs[s0:s0 + n_scr], refs[s0 + n_scr:]
        part = lambda r, k: (rin[spans[k][0]:spans[k][0] + len(r.inputs)], rout[spans[k][1]:spans[k][1] + len(r.out_shapes)],
                             rsem[spans[k][2]:spans[k][2] + len(r.sems)])
        first = functools.reduce(jnp.logical_and, [pl.program_id(k) == 0 for k in range(len(grid))], True)
        last = functools.reduce(jnp.logical_and, [pl.program_id(k) == grid[k] - 1 for k in range(len(grid))], True)
        if rides:
            @pl.when(first)
            def _():
                for k, r in enumerate(rides):
                    r.start(*part(r, k))

        body(*ins, *outs, *scr)
        if rides:
            @pl.when(last)
            def _():
                for k, r in enumerate(rides):
                    r.finish(*part(r, k))

    res = pl.pallas_call(
        wrapped, name=name, grid=grid,
        in_specs=list(in_specs) + [hbm] * len(r_in), out_specs=list(out_specs) + [hbm] * len(r_out),
        out_shape=list(out_shape) + r_out, input_output_aliases=aliases,
        scratch_shapes=list(scratch_shapes) + r_sem,
        compiler_params=pltpu.CompilerParams(dimension_semantics=sem, vmem_limit_bytes=VMEM_LIMIT) if grid else None,
    )(*args, *r_in)
    extras = [list(res[n_out + ko:n_out + ko + len(r.out_shapes)]) for r, (_, ko, _) in zip(rides, spans)]
    return list(res[:n_out]), extras


def _run(ride, name):
    def body(*refs):
        n_in, n_out = len(ride.inputs), len(ride.out_shapes)
        parts = refs[:n_in], refs[n_in:n_in + n_out], refs[n_in + n_out:]
        ride.start(*parts)
        ride.finish(*parts)

    hbm = pl.BlockSpec(memory_space=pl.ANY)
    return list(pl.pallas_call(
        body, name=name, in_specs=[hbm] * len(ride.inputs), out_specs=[hbm] * len(ride.out_shapes),
        out_shape=ride.out_shapes, input_output_aliases=ride.aliases, scratch_shapes=ride.sems,
    )(*ride.inputs))


def _loss_head(hv, gain_ref, tg_ref, loss_ref, dgain_ref):
    d = hv.shape[1]
    r = lax.rsqrt(jnp.mean(hv * hv, axis=-1, keepdims=True) + NORM_EPS)
    xh = hv * r
    err = xh * gain_ref[...] - tg_ref[...]
    sq = _rows8(jnp.square(err))
    loss_ref[...] += 0.5 * functools.reduce(jnp.add, [sq[:, k * LANE:(k + 1) * LANE] for k in range(d // LANE)]) / d
    dy = err / d
    dgain_ref[...] += _rows8(dy * xh)
    dxh = dy * gain_ref[...]
    return r * (dxh - xh * jnp.mean(dxh * xh, axis=-1, keepdims=True))


V7X_MXU_TILE = 256
FFN_CHUNK_TILES = 3


def _hidden_chunks(f):
    step = FFN_CHUNK_TILES * V7X_MXU_TILE
    return [slice(s, min(s + step, f)) for s in range(0, f, step)]


def _flat(w):
    return w.reshape(w.shape[0] * w.shape[1], w.shape[2])


def _ffn_fwd(x, gain, wg, wu, wd, name, rides=None, head=None):
    t, d = x.shape
    wg, wu, wd = _flat(wg), _flat(wu), _flat(wd)
    f = wg.shape[0]
    tm = min(256, t)
    nh = 0 if head is None else 2

    def body(*refs):
        x_ref, gain_ref = refs[:2]
        wg_hbm, wu_hbm, wd_hbm, h_ref, xn_ref, g_ref, u_ref, a_ref = refs[2 + nh:10 + nh]
        sums = refs[10 + nh:12 + nh]
        wg_v, wu_v, wd_v, sems = refs[-4:]

        @pl.when(pl.program_id(0) == 0)
        def _():
            _load_weights([(wg_hbm, wg_v), (wu_hbm, wu_v), (wd_hbm, wd_v)], sems)
            if head is not None:
                for s_ref in sums:
                    s_ref[...] = jnp.zeros_like(s_ref)

        xv = x_ref[...]
        r = lax.rsqrt(jnp.mean(xv * xv, axis=-1, keepdims=True) + NORM_EPS)
        xn = (xv * r * gain_ref[...]).astype(BF16)
        xn_ref[...] = xn
        acc = jnp.zeros((tm, d), F32)
        for c in _hidden_chunks(f):
            g = _dot(xn, wg_v[c, :], NT)
            u = _dot(xn, wu_v[c, :], NT)
            g_ref[:, c] = g.astype(BF16)
            u_ref[:, c] = u.astype(BF16)
            a = (g * _sigmoid(g) * u).astype(BF16)
            a_ref[:, c] = a
            acc = acc + _dot(a, wd_v[c, :])
        hv = xv + 0.5 * acc
        h_ref[...] = hv if head is None else _loss_head(hv, refs[2], refs[3], *sums)

    hbm = pl.BlockSpec(memory_space=pl.ANY)
    hid = pl.BlockSpec((tm, f), lambda i: (i, 0))
    tile = pl.BlockSpec((tm, d), lambda i: (i, 0))
    row = pl.BlockSpec((1, d), lambda i: (0, 0))
    sums = [] if head is None else [(pl.BlockSpec((8, LANE), lambda i: (0, 0)), jax.ShapeDtypeStruct((8, LANE), F32)),
                                    (pl.BlockSpec((8, d), lambda i: (0, 0)), jax.ShapeDtypeStruct((8, d), F32))]
    return _pallas(
        body, rides, name=name, grid=(t // tm,),
        in_specs=[tile, row] + ([] if head is None else [row, tile]) + [hbm, hbm, hbm],
        out_specs=[tile, tile, hid, hid, hid] + [s for s, _ in sums],
        out_shape=[jax.ShapeDtypeStruct((t, d), F32), jax.ShapeDtypeStruct((t, d), BF16)]
        + [jax.ShapeDtypeStruct((t, f), BF16)] * 3 + [s for _, s in sums],
        scratch_shapes=[pltpu.VMEM(wg.shape, BF16), pltpu.VMEM(wu.shape, BF16), pltpu.VMEM(wd.shape, BF16),
                        pltpu.SemaphoreType.DMA((3,))],
        sem=("arbitrary",), args=[x, gain] + ([] if head is None else list(head)) + [wg, wu, wd])


def _ffn_bwd_data(dy, x, gain, g, u, wg, wu, wd, name, rides=None):
    t, d = x.shape
    wg, wu, wd = _flat(wg), _flat(wu), _flat(wd)
    f = wg.shape[0]
    tm = min(256, t)

    def body(dy_ref, x_ref, gain_ref, g_ref, u_ref, wg_hbm, wu_hbm, wd_hbm, dx_ref, dg_ref, du_ref, dgain_ref,
             wg_v, wu_v, wd_v, sems):
        @pl.when(pl.program_id(0) == 0)
        def _():
            _load_weights([(wg_hbm, wg_v), (wu_hbm, wu_v), (wd_hbm, wd_v)], sems)
            dgain_ref[...] = jnp.zeros_like(dgain_ref)

        dyv = dy_ref[...]
        dyh = (0.5 * dyv).astype(BF16)
        dxn = jnp.zeros((tm, d), F32)
        chunks = _hidden_chunks(f)
        das = [_dot(dyh, wd_v[c, :], NT) for c in chunks]
        for c, da in zip(chunks, das):
            gj = g_ref[:, c].astype(F32)
            uj = u_ref[:, c].astype(F32)
            sig = _sigmoid(gj)
            dgj = (da * uj * (sig * (1.0 + gj * (1.0 - sig)))).astype(BF16)
            duj = (da * (gj * sig)).astype(BF16)
            dg_ref[:, c] = dgj
            du_ref[:, c] = duj
            dxn = dxn + _dot(dgj, wg_v[c, :]) + _dot(duj, wu_v[c, :])
        xv = x_ref[...]
        r = lax.rsqrt(jnp.mean(xv * xv, axis=-1, keepdims=True) + NORM_EPS)
        xh = xv * r
        dgain_ref[...] += _rows8(dxn * xh)
        dxh = dxn * gain_ref[...]
        dx_ref[...] = dyv + r * (dxh - xh * jnp.mean(dxh * xh, axis=-1, keepdims=True))

    hbm = pl.BlockSpec(memory_space=pl.ANY)
    tile = pl.BlockSpec((tm, d), lambda i: (i, 0))
    hid = pl.BlockSpec((tm, f), lambda i: (i, 0))
    return _pallas(
        body, rides, name=name, grid=(t // tm,),
        in_specs=[tile, tile, pl.BlockSpec((1, d), lambda i: (0, 0)), hid, hid, hbm, hbm, hbm],
        out_specs=[tile, hid, hid, pl.BlockSpec((8, d), lambda i: (0, 0))],
        out_shape=[jax.ShapeDtypeStruct((t, d), F32), jax.ShapeDtypeStruct((t, f), BF16),
                   jax.ShapeDtypeStruct((t, f), BF16), jax.ShapeDtypeStruct((8, d), F32)],
        scratch_shapes=[pltpu.VMEM(wg.shape, BF16), pltpu.VMEM(wu.shape, BF16), pltpu.VMEM(wd.shape, BF16),
                        pltpu.SemaphoreType.DMA((3,))],
        sem=("arbitrary",), args=[dy, x, gain, g, u, wg, wu, wd])


WGRAD_ROW_BLOCKS = 2


def _ffn_wgrad_down(a, dy, name, rides=None):
    t, d = dy.shape
    f = a.shape[1]
    fb = f // WGRAD_ROW_BLOCKS
    tk = min(1024, t)

    def body(dy_ref, a_ref, dwd_ref):
        @pl.when(pl.program_id(1) == 0)
        def _():
            dwd_ref[...] = jnp.zeros_like(dwd_ref)

        dwd_ref[...] += _dot(a_ref[...], (0.5 * dy_ref[...]).astype(BF16), TN)

    return _pallas(
        body, rides, name=name, grid=(WGRAD_ROW_BLOCKS, t // tk),
        in_specs=[pl.BlockSpec((tk, d), lambda j, k: (k, 0)), pl.BlockSpec((tk, fb), lambda j, k: (k, j))],
        out_specs=[pl.BlockSpec((fb, d), lambda j, k: (j, 0))],
        out_shape=[jax.ShapeDtypeStruct((f, d), F32)],
        sem=("arbitrary", "arbitrary"), args=[dy, a])


def _ffn_wgrad_gu(xn, dhs, name, rides=None):
    t, d = xn.shape
    n = len(dhs)
    f = dhs[0].shape[1]
    fb = f // WGRAD_ROW_BLOCKS
    tk = min(2048 // n, t)

    def body(xn_ref, *refs):
        @pl.when(pl.program_id(1) == 0)
        def _():
            for o_ref in refs[n:]:
                o_ref[...] = jnp.zeros_like(o_ref)

        xnv = xn_ref[...]
        for dh_ref, o_ref in zip(refs[:n], refs[n:]):
            o_ref[...] += _dot(dh_ref[...], xnv, TN)

    hid = pl.BlockSpec((tk, fb), lambda j, k: (k, j))
    out = pl.BlockSpec((fb, d), lambda j, k: (j, 0))
    return _pallas(
        body, rides, name=name, grid=(WGRAD_ROW_BLOCKS, t // tk),
        in_specs=[pl.BlockSpec((tk, d), lambda j, k: (k, 0))] + [hid] * n,
        out_specs=[out] * n, out_shape=[jax.ShapeDtypeStruct((f, d), F32)] * n,
        sem=("arbitrary", "arbitrary"), args=[xn] + list(dhs))


def _tn_matmul(a, b, bn, name):
    t, m = a.shape
    n = b.shape[1]
    tk = min(2048, t)

    def body(a_ref, b_ref, o_ref):
        @pl.when(pl.program_id(1) == 0)
        def _():
            o_ref[...] = jnp.zeros_like(o_ref)

        o_ref[...] += _dot(a_ref[...].astype(BF16), b_ref[...].astype(BF16), TN)

    return pl.pallas_call(
        body, name=name, grid=(n // bn, t // tk),
        in_specs=[pl.BlockSpec((tk, m), lambda j, k: (k, 0)), pl.BlockSpec((tk, bn), lambda j, k: (k, j))],
        out_specs=pl.BlockSpec((None, m, bn), lambda j, k: (j, 0, 0)),
        out_shape=jax.ShapeDtypeStruct((n // bn, m, bn), F32),
        compiler_params=_params("arbitrary", "arbitrary"),
    )(a, b)


def _chunk_scratch(tm, w):
    return pltpu.VMEM((w // LANE, tm, LANE), F32)


def _regroup_store(cbuf, out_ref, dil):
    n = out_ref.shape[1]
    for g in range(dil):
        for k in range(cbuf.shape[0]):
            rows = cbuf[k] if dil == 1 else cbuf[k, pl.ds(g, n, stride=dil), :]
            out_ref[g, :, k * LANE:(k + 1) * LANE] = rows.astype(out_ref.dtype)


def _natural_rows(ref, dil, cbuf):
    if dil == 1:
        return ref[0].astype(F32)
    n = ref.shape[1]
    for g in range(dil):
        for k in range(cbuf.shape[0]):
            cbuf[k, pl.ds(g, n, stride=dil), :] = ref[g, :, k * LANE:(k + 1) * LANE].astype(F32)
    return jnp.concatenate([cbuf[k] for k in range(cbuf.shape[0])], axis=1)


def _inproj_fwd(h, gain, win):
    t, d = h.shape
    ns, _, cs = win.shape
    tm = min(512, t)
    rw, aw = 4 * RET_WIDTH, 3 * ATT_WIDTH

    def body(h_ref, gain_ref, w_ref, xn_ref, ur_ref, *rest):
        a_refs, abuf = rest[:-1], rest[-1]
        hv = h_ref[...]
        r = lax.rsqrt(jnp.mean(hv * hv, axis=-1, keepdims=True) + NORM_EPS)
        xn = (hv * r * gain_ref[...]).astype(BF16)
        xn_ref[...] = xn
        for j in range(ns):
            res = _dot(xn, w_ref[j])
            for k in range(cs // LANE):
                chunk = j * (cs // LANE) + k
                piece = res[:, k * LANE:(k + 1) * LANE]
                if chunk < rw // LANE:
                    ur_ref[:, chunk * LANE:(chunk + 1) * LANE] = piece
                else:
                    abuf[chunk - rw // LANE] = piece
        for dil, a_ref in zip(DILATIONS, a_refs):
            _regroup_store(abuf, a_ref, dil)

    return pl.pallas_call(
        body, name="inproj_fwd", grid=(t // tm,),
        in_specs=[pl.BlockSpec((tm, d), lambda i: (i, 0)), pl.BlockSpec((1, d), lambda i: (0, 0)),
                  pl.BlockSpec(win.shape, lambda i: (0, 0, 0))],
        out_specs=[pl.BlockSpec((tm, d), lambda i: (i, 0)), pl.BlockSpec((tm, rw), lambda i: (i, 0))]
        + [pl.BlockSpec((dil, tm // dil, aw), lambda i: (0, i, 0)) for dil in DILATIONS],
        out_shape=[jax.ShapeDtypeStruct((t, d), BF16), jax.ShapeDtypeStruct((t, rw), F32)]
        + [jax.ShapeDtypeStruct((dil, t // dil, aw), BF16) for dil in DILATIONS],
        scratch_shapes=[_chunk_scratch(tm, aw)],
        compiler_params=_params("arbitrary"),
    )(h, gain, win)


def _inproj_bwd(pieces, parts, h, gain, dres, win):
    t, d = h.shape
    ns, _, cs = win.shape
    pw = pieces[0].shape[1]
    tm = min(512, t)
    npc, nk = len(pieces), len(parts[0])
    flat_parts = [a for p in parts for a in p]

    def body(*refs):
        p_refs, a_refs = refs[:npc], refs[npc:npc + len(flat_parts)]
        h_ref, gain_ref, dres_ref, w_ref, dh_ref, du_ref, dgain_ref, buf = refs[npc + len(flat_parts):]

        @pl.when(pl.program_id(0) == 0)
        def _():
            dgain_ref[...] = jnp.zeros_like(dgain_ref)

        for k in range(npc):
            du_ref[:, k * pw:(k + 1) * pw] = p_refs[k][...]
        for k in range(nk):
            acc = None
            for b, dil in enumerate(DILATIONS):
                rows = _natural_rows(a_refs[b * nk + k], dil, buf)
                acc = rows if acc is None else acc + rows
            du_ref[:, (npc + k) * pw:(npc + k + 1) * pw] = acc.astype(BF16)
        dxn = jnp.zeros((tm, d), F32)
        for j in range(ns):
            dxn = dxn + _dot(du_ref[:, j * cs:(j + 1) * cs], w_ref[j], NT)
        hv = h_ref[...]
        r = lax.rsqrt(jnp.mean(hv * hv, axis=-1, keepdims=True) + NORM_EPS)
        xh = hv * r
        dgain_ref[...] += _rows8(dxn * xh)
        dxh = dxn * gain_ref[...]
        dh_ref[...] = dres_ref[...] + r * (dxh - xh * jnp.mean(dxh * xh, axis=-1, keepdims=True))

    tile = pl.BlockSpec((tm, d), lambda i: (i, 0))
    cols = (npc + nk) * pw
    return pl.pallas_call(
        body, name="inproj_bwd", grid=(t // tm,),
        in_specs=[pl.BlockSpec((tm, pw), lambda i: (i, 0))] * npc
        + [_regrouped_spec(tm, dil, pw) for dil in DILATIONS for _ in range(nk)]
        + [tile, pl.BlockSpec((1, d), lambda i: (0, 0)), tile, pl.BlockSpec(win.shape, lambda i: (0, 0, 0))],
        out_specs=[tile, pl.BlockSpec((tm, cols), lambda i: (i, 0)), pl.BlockSpec((8, d), lambda i: (0, 0))],
        out_shape=[jax.ShapeDtypeStruct((t, d), F32), jax.ShapeDtypeStruct((t, cols), BF16),
                   jax.ShapeDtypeStruct((8, d), F32)],
        scratch_shapes=[_chunk_scratch(tm, pw)],
        compiler_params=_params("arbitrary"),
    )(*pieces, *flat_parts, h, gain, dres, win)


def _outproj_fwd(h, mix_r, mix_a, wo):
    t, d = h.shape
    hw = mix_r.shape[1]
    tm = min(512, t)

    def body(h_ref, mr_ref, ma_ref, w_ref, o_ref):
        o_ref[...] = h_ref[...] + _dot(mr_ref[...], w_ref[0:hw, :]) + _dot(ma_ref[...], w_ref[hw:2 * hw, :])

    tile = pl.BlockSpec((tm, d), lambda i: (i, 0))
    half = pl.BlockSpec((tm, hw), lambda i: (i, 0))
    return pl.pallas_call(
        body, name="outproj_fwd", grid=(t // tm,),
        in_specs=[tile, half, half, pl.BlockSpec(wo.shape, lambda i: (0, 0))],
        out_specs=tile, out_shape=jax.ShapeDtypeStruct((t, d), F32),
        compiler_params=_params("arbitrary"),
    )(h, mix_r, mix_a, wo)


def _outproj_bwd(dh, wo, rides=None):
    t, d = dh.shape
    hw = wo.shape[0] // 2
    tm = min(512, t)

    def body(dh_ref, w_ref, dr_ref, da_ref):
        dhb = dh_ref[...].astype(BF16)
        dr_ref[...] = _dot(dhb, w_ref[0:hw, :], NT)
        da_ref[...] = _dot(dhb, w_ref[hw:2 * hw, :], NT)

    half = pl.BlockSpec((tm, hw), lambda i: (i, 0))
    return _pallas(
        body, rides, name="outproj_bwd", grid=(t // tm,),
        in_specs=[pl.BlockSpec((tm, d), lambda i: (i, 0)), pl.BlockSpec(wo.shape, lambda i: (0, 0))],
        out_specs=[half, half],
        out_shape=[jax.ShapeDtypeStruct((t, hw), F32), jax.ShapeDtypeStruct((t, hw), F32)],
        sem=("arbitrary",), args=[dh, wo])


def _retention_tables(t):
    pos = jnp.arange(t, dtype=F32)
    pair = (jnp.arange(RET_DIM) // 2 * 2).astype(F32)
    ang = pos[:, None] * (ROPE_BASE ** (-pair / RET_DIM))[None, :]
    c = RET_CHUNK
    log_g = jnp.log(1.0 - 2.0 ** (-5.0 - jnp.arange(RET_HEADS, dtype=F32)))
    idx = jnp.arange(c, dtype=F32)
    rel = idx[:, None] - idx[None, :]
    decay = jnp.where(rel >= 0, jnp.exp(log_g[:, None, None] * jnp.maximum(rel, 0.0)), 0.0)
    zeta = jnp.exp(log_g[:, None] * (c - 1 - idx)[None, :])
    xi = jnp.exp(log_g[:, None] * (idx + 1)[None, :])
    gc = jnp.exp(log_g * c)
    wide = lambda v: jnp.broadcast_to(v[:, :, None], (RET_HEADS, c, LANE))
    return (jnp.cos(ang), jnp.sin(ang), decay, wide(zeta), wide(xi),
            jnp.broadcast_to(gc[:, None, None], (RET_HEADS, c, LANE)))


def _rot(v):
    lane = lax.broadcasted_iota(jnp.int32, v.shape, 1)
    nxt = pltpu.roll(v, LANE - 1, 1)
    prv = pltpu.roll(v, 1, 1)
    return jnp.where(lane % 2 == 0, -nxt, prv)


def _ret_specs(tr, rev, nt):
    ti = (lambda i: nt - 1 - i) if rev else (lambda i: i)
    col = lambda blk: pl.BlockSpec((tr, RET_WIDTH), lambda i: (ti(i), blk))
    tab = pl.BlockSpec((tr, LANE), lambda i: (ti(i), 0))
    head = pl.BlockSpec((RET_HEADS, RET_CHUNK, LANE), lambda i: (0, 0, 0))
    return col, tab, head


def _ret_chunks(tr, rev=False):
    order = list(range(tr // RET_CHUNK))
    return [(pl.ds(ci * RET_CHUNK, RET_CHUNK), slice(h * RET_DIM, (h + 1) * RET_DIM), h)
            for h in range(RET_HEADS) for ci in (reversed(order) if rev else order)]


def _ret_operands(items, q_ref, k_ref, v_ref, cos_ref, sin_ref, zeta_ref):
    scale = RET_DIM ** -0.5
    qbs, kbs, vbs, kzs = [], [], [], []
    for sl, hs, h in items:
        cs, sn = cos_ref[sl, :], sin_ref[sl, :]
        q, k = q_ref[sl, hs], k_ref[sl, hs]
        kr = (k * cs + _rot(k) * sn) * scale
        qbs.append((q * cs + _rot(q) * sn).astype(BF16))
        kbs.append(kr.astype(BF16))
        vbs.append(v_ref[sl, hs].astype(BF16))
        kzs.append((kr * zeta_ref[h]).astype(BF16))
    return qbs, kbs, vbs, kzs


def _ret_states(items, state, steps, gc_ref):
    cur, befores = {}, []
    for (sl, hs, h), step in zip(items, steps):
        st = cur[h] if h in cur else state[h]
        befores.append(st)
        cur[h] = st * gc_ref[h] + step
    for h, st in cur.items():
        state[h] = st
    return befores


def _ret_fwd(u, gain, tabs):
    t = u.shape[0]
    tr = min(512, t)
    nt = t // tr
    cos, sin, decay, zeta, xi, gc = tabs

    def body(q_ref, k_ref, v_ref, gt_ref, cos_ref, sin_ref, gain_ref, dec_ref, zeta_ref, xi_ref, gc_ref,
             raw_ref, mix_ref, state):
        @pl.when(pl.program_id(0) == 0)
        def _():
            state[...] = jnp.zeros_like(state)

        items = _ret_chunks(tr)
        n = range(len(items))
        qbs, kbs, vbs, kzs = _ret_operands(items, q_ref, k_ref, v_ref, cos_ref, sin_ref, zeta_ref)
        ss = [_dot(qbs[i], kbs[i], NT) for i in n]
        kvs = [_dot(kzs[i], vbs[i], TN) for i in n]
        befores = _ret_states(items, state, kvs, gc_ref)
        intra = [_dot((ss[i] * dec_ref[items[i][2]]).astype(BF16), vbs[i]) for i in n]
        inter = [_dot(qbs[i], befores[i].astype(BF16)) for i in n]
        for i, (sl, hs, h) in enumerate(items):
            o = intra[i] + inter[i] * xi_ref[h]
            raw_ref[sl, hs] = o
            mu = jnp.mean(o, axis=-1, keepdims=True)
            var = jnp.mean(jnp.square(o - mu), axis=-1, keepdims=True)
            y = (o - mu) * lax.rsqrt(var + GN_EPS) * gain_ref[:, hs]
            gt = gt_ref[sl, hs]
            mix_ref[sl, hs] = (y * (gt * _sigmoid(gt))).astype(BF16)

    col, tab, head = _ret_specs(tr, False, nt)
    out = pl.BlockSpec((tr, RET_WIDTH), lambda i: (i, 0))
    return pl.pallas_call(
        body, name="ret_fwd", grid=(nt,),
        in_specs=[col(0), col(1), col(2), col(3), tab, tab, pl.BlockSpec((1, RET_WIDTH), lambda i: (0, 0)),
                  head, head, head, head],
        out_specs=[out, out],
        out_shape=[jax.ShapeDtypeStruct((t, RET_WIDTH), F32), jax.ShapeDtypeStruct((t, RET_WIDTH), BF16)],
        scratch_shapes=[pltpu.VMEM((RET_HEADS, RET_DIM, RET_DIM), F32)],
        compiler_params=_params("arbitrary"),
    )(u, u, u, u, cos, sin, gain, decay, zeta, xi, gc)


def _ret_bwd_q(dmix, raw, u, gain, tabs, rides=None):
    t = u.shape[0]
    tr = min(512, t)
    nt = t // tr
    cos, sin, decay, zeta, xi, gc = tabs

    def body(dm_ref, raw_ref, q_ref, k_ref, v_ref, gt_ref, cos_ref, sin_ref, gain_ref, dec_ref, zeta_ref, xi_ref, gc_ref,
             dq_ref, dgt_ref, dret_ref, dgain_ref, state):
        @pl.when(pl.program_id(0) == 0)
        def _():
            state[...] = jnp.zeros_like(state)
            dgain_ref[...] = jnp.zeros_like(dgain_ref)

        items = _ret_chunks(tr)
        n_items = range(len(items))
        qbs, kbs, vbs, kzs = _ret_operands(items, q_ref, k_ref, v_ref, cos_ref, sin_ref, zeta_ref)
        dos, dgains = [], {}
        for sl, hs, h in items:
            o = raw_ref[sl, hs]
            mu = jnp.mean(o, axis=-1, keepdims=True)
            var = jnp.mean(jnp.square(o - mu), axis=-1, keepdims=True)
            rs = lax.rsqrt(var + GN_EPS)
            n = (o - mu) * rs
            gt = gt_ref[sl, hs]
            sig = _sigmoid(gt)
            dout = dm_ref[sl, hs]
            gain_h = gain_ref[:, hs]
            dgt_ref[sl, hs] = (dout * (n * gain_h) * (sig * (1.0 + gt * (1.0 - sig)))).astype(BF16)
            dy = dout * (gt * sig)
            dgains[h] = dgains[h] + _rows8(dy * n) if h in dgains else _rows8(dy * n)
            dn = dy * gain_h
            do = rs * (dn - jnp.mean(dn, axis=-1, keepdims=True) - n * jnp.mean(dn * n, axis=-1, keepdims=True))
            dret_ref[sl, hs] = do
            dos.append(do)
        for h, dg in dgains.items():
            dgain_ref[:, h * RET_DIM:(h + 1) * RET_DIM] += dg
        dss = [_dot(dos[i].astype(BF16), vbs[i], NT) for i in n_items]
        kvs = [_dot(kzs[i], vbs[i], TN) for i in n_items]
        befores = _ret_states(items, state, kvs, gc_ref)
        intra = [_dot((dss[i] * dec_ref[items[i][2]]).astype(BF16), kbs[i]) for i in n_items]
        inter = [_dot((dos[i] * xi_ref[items[i][2]]).astype(BF16), befores[i].astype(BF16), NT) for i in n_items]
        for i, (sl, hs, h) in enumerate(items):
            dqr = intra[i] + inter[i]
            dq_ref[sl, hs] = (dqr * cos_ref[sl, :] - _rot(dqr * sin_ref[sl, :])).astype(BF16)

    col, tab, head = _ret_specs(tr, False, nt)
    out = pl.BlockSpec((tr, RET_WIDTH), lambda i: (i, 0))
    return _pallas(
        body, rides, name="ret_bwd_q", grid=(nt,),
        in_specs=[out, out, col(0), col(1), col(2), col(3), tab, tab, pl.BlockSpec((1, RET_WIDTH), lambda i: (0, 0)),
                  head, head, head, head],
        out_specs=[out, out, out, pl.BlockSpec((8, RET_WIDTH), lambda i: (0, 0))],
        out_shape=[jax.ShapeDtypeStruct((t, RET_WIDTH), BF16), jax.ShapeDtypeStruct((t, RET_WIDTH), BF16),
                   jax.ShapeDtypeStruct((t, RET_WIDTH), F32), jax.ShapeDtypeStruct((8, RET_WIDTH), F32)],
        scratch_shapes=[pltpu.VMEM((RET_HEADS, RET_DIM, RET_DIM), F32)],
        sem=("arbitrary",), args=[dmix, raw, u, u, u, u, cos, sin, gain, decay, zeta, xi, gc])


def _ret_bwd_kv(dret, u, tabs, rides=None):
    t = u.shape[0]
    tr = min(512, t)
    nt = t // tr
    cos, sin, decay, zeta, xi, gc = tabs
    scale = RET_DIM ** -0.5

    def body(do_ref, q_ref, k_ref, v_ref, cos_ref, sin_ref, dec_ref, zeta_ref, xi_ref, gc_ref, dk_ref, dv_ref, gst):
        @pl.when(pl.program_id(0) == 0)
        def _():
            gst[...] = jnp.zeros_like(gst)

        items = _ret_chunks(tr, rev=True)
        n = range(len(items))
        qbs, kbs, vbs, kzs = _ret_operands(items, q_ref, k_ref, v_ref, cos_ref, sin_ref, zeta_ref)
        dos = [do_ref[sl, hs] for sl, hs, h in items]
        dobs = [do.astype(BF16) for do in dos]
        ss = [_dot(qbs[i], kbs[i], NT) for i in n]
        dss = [_dot(dobs[i], vbs[i], NT) for i in n]
        steps = [_dot(qbs[i], (dos[i] * xi_ref[items[i][2]]).astype(BF16), TN) for i in n]
        afters = [g.astype(BF16) for g in _ret_states(items, gst, steps, gc_ref)]
        dvs = [_dot((ss[i] * dec_ref[items[i][2]]).astype(BF16), dobs[i], TN) + _dot(kzs[i], afters[i]) for i in n]
        dks = [_dot((dss[i] * dec_ref[items[i][2]]).astype(BF16), qbs[i], TN) for i in n]
        dkz = [_dot(vbs[i], afters[i], NT) for i in n]
        for i, (sl, hs, h) in enumerate(items):
            dv_ref[sl, hs] = dvs[i].astype(BF16)
            dkr = (dks[i] + dkz[i] * zeta_ref[h]) * scale
            dk_ref[sl, hs] = (dkr * cos_ref[sl, :] - _rot(dkr * sin_ref[sl, :])).astype(BF16)

    col, tab, head = _ret_specs(tr, True, nt)
    out = pl.BlockSpec((tr, RET_WIDTH), lambda i: (nt - 1 - i, 0))
    return _pallas(
        body, rides, name="ret_bwd_kv", grid=(nt,),
        in_specs=[out, col(0), col(1), col(2), tab, tab, head, head, head, head],
        out_specs=[out, out],
        out_shape=[jax.ShapeDtypeStruct((t, RET_WIDTH), BF16), jax.ShapeDtypeStruct((t, RET_WIDTH), BF16)],
        scratch_shapes=[pltpu.VMEM((RET_HEADS, RET_DIM, RET_DIM), F32)],
        sem=("arbitrary",), args=[dret, u, u, u, cos, sin, decay, zeta, xi, gc])


PAIRS = ATT_WIDTH // LANE
ATT_Q_BLK, ATT_K_BLK, ATT_V_BLK = 0, PAIRS, 2 * PAIRS
STAT_LANES = ATT_DIM // 2


def _att_tiles(t, dil):
    sub = t // dil
    tq = min(1024, sub)
    return sub, tq, sub // tq, tq // ATT_BLOCK


def _att_in_specs(tq, qb, ti):
    cur = lambda off: pl.BlockSpec((None, tq, LANE), lambda g, p, i: (g, ti(i), off + p))
    prev = lambda off: pl.BlockSpec((None, ATT_BLOCK, LANE), lambda g, p, i: (g, jnp.maximum(ti(i) * qb - 1, 0), off + p))
    return [cur(ATT_Q_BLK), cur(ATT_K_BLK), prev(ATT_K_BLK), cur(ATT_V_BLK), prev(ATT_V_BLK)]


def _band_mask():
    key = lax.broadcasted_iota(jnp.int32, (2 * ATT_BLOCK, 2 * ATT_BLOCK), 0)
    qry = lax.broadcasted_iota(jnp.int32, (2 * ATT_BLOCK, 2 * ATT_BLOCK), 1) % ATT_BLOCK
    dist = qry + ATT_BLOCK - key
    return (dist >= 0) & (dist <= ATT_BLOCK), key >= ATT_BLOCK


def _head0_lanes():
    return lax.broadcasted_iota(jnp.int32, (ATT_BLOCK, LANE), 1) < ATT_DIM


def _stack_heads(v, head0):
    zero = jnp.zeros((), v.dtype)
    return jnp.concatenate([jnp.where(head0, v, zero), jnp.where(head0, zero, v)], axis=0)


def _unstack_heads(v, head0):
    return jnp.where(head0, v[0:ATT_BLOCK], v[ATT_BLOCK:])


def _att_fwd(ua, dil):
    sub = ua.shape[1]
    _, tq, nq, qb = _att_tiles(sub * dil, dil)

    def body(q_ref, kc_ref, kp_ref, vc_ref, vp_ref, o_ref, l_ref, kx, vx):
        tile = pl.program_id(2)
        kx[0:ATT_BLOCK, :] = kp_ref[...]
        kx[ATT_BLOCK:, :] = kc_ref[...]
        vx[0:ATT_BLOCK, :] = vp_ref[...]
        vx[ATT_BLOCK:, :] = vc_ref[...]
        band, cur_keys = _band_mask()
        head0 = _head0_lanes()
        blocks = range(qb)
        rows = [slice(b * ATT_BLOCK, (b + 1) * ATT_BLOCK) for b in blocks]
        keys = [slice(b * ATT_BLOCK, (b + 2) * ATT_BLOCK) for b in blocks]
        sts = [_dot(kx[keys[b], :], _stack_heads(q_ref[rows[b], :] * jnp.asarray(ATT_DIM ** -0.5, BF16), head0), NT)
               for b in blocks]
        pts, lses = [], []
        for b in blocks:
            mask = band if b > 0 else band & (cur_keys | (tile > 0))
            st = jnp.where(mask, sts[b], -1e30)
            m = jnp.max(st, axis=0, keepdims=True)
            ex = jnp.exp(st - m)
            den = jnp.sum(ex, axis=0, keepdims=True)
            pts.append((ex * (1.0 / den)).astype(BF16))
            lses.append(m + jnp.log(den))
        outs = [_dot(pts[b], vx[keys[b], :], TN) for b in blocks]
        for b in blocks:
            o_ref[rows[b], :] = _unstack_heads(outs[b], head0).astype(BF16)
            cols = [jnp.broadcast_to(lses[b][:, e * ATT_BLOCK:(e + 1) * ATT_BLOCK], (ATT_BLOCK, LANE)).T for e in range(2)]
            l_ref[rows[b], :] = jnp.where(head0, cols[0], cols[1])

    out = pl.BlockSpec((None, tq, LANE), lambda g, p, i: (g, i, p))
    return pl.pallas_call(
        body, name=f"att_fwd_d{dil}", grid=(dil, PAIRS, nq),
        in_specs=_att_in_specs(tq, qb, lambda i: i),
        out_specs=[out, out],
        out_shape=[jax.ShapeDtypeStruct((dil, sub, ATT_WIDTH), BF16), jax.ShapeDtypeStruct((dil, sub, ATT_WIDTH), F32)],
        scratch_shapes=[pltpu.VMEM((tq + ATT_BLOCK, LANE), BF16)] * 2,
        compiler_params=_params("arbitrary", "arbitrary", "arbitrary"),
    )(ua, ua, ua, ua, ua)


def _regrouped_spec(tm, dil, w):
    return pl.BlockSpec((dil, tm // dil, w), lambda i: (0, i, 0))


def _att_combine(outs, lses, t):
    w = ATT_WIDTH
    tm = min(512, t)
    nb = len(outs)

    def body(*refs):
        o_refs, l_refs = refs[:nb], refs[nb:2 * nb]
        mix_ref, att_ref, lse_ref, buf = refs[2 * nb:]
        ls = [_natural_rows(r, dil, buf) for r, dil in zip(l_refs, DILATIONS)]
        m = functools.reduce(jnp.maximum, ls)
        ws = [jnp.exp(l - m) for l in ls]
        den = functools.reduce(jnp.add, ws)
        att = functools.reduce(jnp.add, [(wt / den) * _natural_rows(r, dil, buf) for wt, r, dil in zip(ws, o_refs, DILATIONS)])
        att_ref[...] = att
        mix_ref[...] = att.astype(BF16)
        lse_ref[...] = m + jnp.log(den)

    tile = pl.BlockSpec((tm, w), lambda i: (i, 0))
    regrouped = [_regrouped_spec(tm, dil, w) for dil in DILATIONS]
    return pl.pallas_call(
        body, name="att_combine", grid=(t // tm,),
        in_specs=regrouped * 2, out_specs=[tile, tile, tile],
        out_shape=[jax.ShapeDtypeStruct((t, w), BF16), jax.ShapeDtypeStruct((t, w), F32), jax.ShapeDtypeStruct((t, w), F32)],
        scratch_shapes=[_chunk_scratch(tm, w)],
        compiler_params=_params("arbitrary"),
    )(*outs, *lses)


def _att_bwd_prep(datt, att, lse):
    t, w = datt.shape
    tm = min(512, t)

    def body(da_ref, at_ref, l_ref, *rest):
        outs, dbuf, sbuf = rest[:-2], rest[-2], rest[-1]
        dav = da_ref[...]
        prod = dav * at_ref[...]
        lane = lax.broadcasted_iota(jnp.int32, (tm, LANE), 1)
        for k in range(w // LANE):
            cols = slice(k * LANE, (k + 1) * LANE)
            dbuf[k] = dav[:, cols]
            delta = jnp.concatenate(
                [jnp.broadcast_to(jnp.sum(prod[:, k * LANE + e * ATT_DIM:k * LANE + (e + 1) * ATT_DIM], axis=-1, keepdims=True),
                                  (tm, ATT_DIM)) for e in range(LANE // ATT_DIM)], axis=1)
            sbuf[k] = jnp.where(lane % ATT_DIM < STAT_LANES, l_ref[:, cols], delta)
        for k, dil in enumerate(DILATIONS):
            _regroup_store(dbuf, outs[2 * k], dil)
            _regroup_store(sbuf, outs[2 * k + 1], dil)

    tile = pl.BlockSpec((tm, w), lambda i: (i, 0))
    res = pl.pallas_call(
        body, name="att_bwd_prep", grid=(t // tm,),
        in_specs=[tile] * 3,
        out_specs=[_regrouped_spec(tm, dil, w) for dil in DILATIONS for _ in range(2)],
        out_shape=[jax.ShapeDtypeStruct((dil, t // dil, w), dt) for dil in DILATIONS for dt in (BF16, F32)],
        scratch_shapes=[_chunk_scratch(tm, w)] * 2,
        compiler_params=_params("arbitrary"),
    )(datt, att, lse)
    return [(res[2 * k], res[2 * k + 1]) for k in range(len(DILATIONS))]


def _att_bwd(ua, da, stat, dil, rides=None):
    sub = ua.shape[1]
    _, tq, nq, qb = _att_tiles(sub * dil, dil)
    scale = ATT_DIM ** -0.5

    def body(q_ref, kc_ref, kp_ref, vc_ref, vp_ref, da_ref, st_ref, dq_ref, dk_ref, dv_ref, kx, vx, ck, cv):
        step = pl.program_id(2)
        tile = nq - 1 - step

        @pl.when(step == 0)
        def _():
            ck[...] = jnp.zeros_like(ck)
            cv[...] = jnp.zeros_like(cv)

        kx[0:ATT_BLOCK, :] = kp_ref[...]
        kx[ATT_BLOCK:, :] = kc_ref[...]
        vx[0:ATT_BLOCK, :] = vp_ref[...]
        vx[ATT_BLOCK:, :] = vc_ref[...]
        band, cur_keys = _band_mask()
        head0 = _head0_lanes()
        blocks = range(qb)
        rows = [slice(b * ATT_BLOCK, (b + 1) * ATT_BLOCK) for b in blocks]
        keys = [slice(b * ATT_BLOCK, (b + 2) * ATT_BLOCK) for b in blocks]
        qqs = [_stack_heads(q_ref[rows[b], :] * jnp.asarray(scale, BF16), head0) for b in blocks]
        dds = [_stack_heads(da_ref[rows[b], :], head0) for b in blocks]
        sts = [_dot(kx[keys[b], :], qqs[b], NT) for b in blocks]
        dpts = [_dot(vx[keys[b], :], dds[b], NT) for b in blocks]
        pts, dsts = [], []
        for b in blocks:
            mask = band if b > 0 else band & (cur_keys | (tile > 0))
            stat = st_ref[rows[b], :].T
            row = lambda k: jnp.concatenate([stat[e * ATT_DIM + k:e * ATT_DIM + k + 1, :] for e in range(2)], axis=1)
            pt = jnp.where(mask, jnp.exp(sts[b] - row(0)), 0.0)
            dsts.append((pt * (dpts[b] - row(STAT_LANES))).astype(BF16))
            pts.append(pt.astype(BF16))
        dqs = [_dot(dsts[b], kx[keys[b], :], TN) for b in blocks]
        dkbs = [_dot(dsts[b], qqs[b]) for b in blocks]
        dvbs = [_dot(pts[b], dds[b]) for b in blocks]
        for b in blocks:
            dq_ref[rows[b], :] = (_unstack_heads(dqs[b], head0) * scale).astype(BF16)
        for b in blocks[1:]:
            dk_ref[rows[b - 1], :] = (dkbs[b - 1][ATT_BLOCK:] + dkbs[b][0:ATT_BLOCK]).astype(BF16)
            dv_ref[rows[b - 1], :] = (dvbs[b - 1][ATT_BLOCK:] + dvbs[b][0:ATT_BLOCK]).astype(BF16)
        before_k, before_v = dkbs[0][0:ATT_BLOCK], dvbs[0][0:ATT_BLOCK]
        open_k, open_v = dkbs[-1][ATT_BLOCK:], dvbs[-1][ATT_BLOCK:]
        last = slice(tq - ATT_BLOCK, tq)
        dk_ref[last, :] = (open_k + ck[...]).astype(BF16)
        dv_ref[last, :] = (open_v + cv[...]).astype(BF16)
        ck[...] = before_k
        cv[...] = before_v

    ti = lambda i: nq - 1 - i
    out = pl.BlockSpec((None, tq, LANE), lambda g, p, i: (g, ti(i), p))
    shape = jax.ShapeDtypeStruct((dil, sub, ATT_WIDTH), BF16)
    return _pallas(
        body, rides, name=f"att_bwd_d{dil}", grid=(dil, PAIRS, nq),
        in_specs=_att_in_specs(tq, qb, ti) + [out, out],
        out_specs=[out, out, out], out_shape=[shape] * 3,
        scratch_shapes=[pltpu.VMEM((tq + ATT_BLOCK, LANE), BF16)] * 2 + [pltpu.VMEM((ATT_BLOCK, LANE), F32)] * 2,
        sem=("arbitrary", "arbitrary", "arbitrary"), args=[ua, ua, ua, ua, ua, da, stat])


class _Reduction:
    def __init__(self, place, names, grads):
        self.place, self.names, self.grads = place, names, grads

    def pair(self):
        return _pair_ride(self.grads)

    def chips(self, got):
        self.got = got
        return _chip_ride([_pair_sum(self.place, g, r, f"pair_sum_{n}") for g, r, n in zip(self.grads, got, self.names)])

    def halves(self, others):
        return [_chip_sum(self.place, g, r, o, f"chip_sum_{n}")
                for g, r, o, n in zip(self.grads, self.got, others, self.names)]


def _step(x, target, gains, w, place=None):
    t = x.shape[0]
    ex = place is not None
    g_ffn1, g_mix, g_ret, g_ffn2, g_fin = gains
    w = list(w)
    tabs = _retention_tables(t)
    red = lambda names, grads: _Reduction(place, names, grads) if ex else None
    ride = lambda r: [r] if ex else None

    if ex:
        w[0:3] = _run(_gather_ride(w[0:3]), "gather_ffn1_weights")
    (h1, xn1, *hid1, act1), rest = _ffn_fwd(x, g_ffn1, *w[0:3], "ffn1_fwd", ride(_gather_ride(w[3:])) if ex else None)
    if ex:
        w[3:] = rest[0]
    wg1, wu1, wd1, win, wo, wg2, wu2, wd2 = w
    wo2 = wo.reshape(wo.shape[0] * wo.shape[1], wo.shape[2])
    xnm, u, *uas = _inproj_fwd(h1, g_mix, win)
    raw, mix_r = _ret_fwd(u, g_ret, tabs)
    branches = [_att_fwd(ua, dil) for ua, dil in zip(uas, DILATIONS)]
    mix_a, att, lse = _att_combine([b[0] for b in branches], [b[1] for b in branches], t)
    h2 = _outproj_fwd(h1, mix_r, mix_a, wo2)
    (dh3, xn2, *hid2, act2, loss_p, dg_fin), _ = _ffn_fwd(h2, g_ffn2, wg2, wu2, wd2, "ffn2_fwd", head=(g_fin, target))

    (dwd2,), _ = _ffn_wgrad_down(act2, dh3, "ffn2_wgrad_down")
    dwd2 = dwd2.reshape(wd2.shape)
    r_d2 = red(["ffn2_w_down"], [dwd2])
    (dh2, dga2, dua2, dg_ffn2), e = _ffn_bwd_data(dh3, h2, g_ffn2, *hid2, wg2, wu2, wd2, "ffn2_bwd",
                                                  ex and [r_d2.pair()])
    (dwg2, dwu2), e = _ffn_wgrad_gu(xn2, [dga2, dua2], "ffn2_wgrad_gu", ex and [r_d2.chips(e[0])])
    dwg2, dwu2 = dwg2.reshape(wg2.shape), dwu2.reshape(wu2.shape)
    r_gu2 = red(["ffn2_w_gate", "ffn2_w_up"], [dwg2, dwu2])
    (dmix_r, dmix_a), e = _outproj_bwd(dh2, wo2, ex and [r_gu2.pair(), _finish_ride(r_d2.halves(e[0]))])
    if ex:
        got_gu2, (dwd2,) = e
    hw = RET_WIDTH // (wo.shape[1])
    dwo = jnp.concatenate([_tn_matmul(mix_r, dh2, dh2.shape[1], "wo_grad_r").reshape(hw, wo.shape[1], wo.shape[2]),
                           _tn_matmul(mix_a, dh2, dh2.shape[1], "wo_grad_a").reshape(hw, wo.shape[1], wo.shape[2])])
    r_wo = red(["w_out"], [dwo])
    (dq_r, dgt_r, dret, dg_ret), e = _ret_bwd_q(dmix_r, raw, u, g_ret, tabs, ex and [r_gu2.chips(got_gu2)])
    (dk_r, dv_r), e = _ret_bwd_kv(dret, u, tabs, ex and [r_wo.pair(), _finish_ride(r_gu2.halves(e[0]))])
    if ex:
        got_wo, (dwg2, dwu2) = e
    prep = _att_bwd_prep(dmix_a, att, lse)
    p1, e = _att_bwd(uas[0], *prep[0], DILATIONS[0], ex and [r_wo.chips(got_wo)])
    p4, e = _att_bwd(uas[1], *prep[1], DILATIONS[1], ex and [_finish_ride(r_wo.halves(e[0]))])
    if ex:
        (dwo,), = e
    p16, _ = _att_bwd(uas[2], *prep[2], DILATIONS[2])
    dh1, du, dg_mix = _inproj_bwd([dq_r, dk_r, dv_r, dgt_r], [p1, p4, p16], h1, g_mix, dh2, win)
    dwin = _tn_matmul(xnm, du, win.shape[2], "win_grad")
    r_in = red(["w_in"], [dwin])
    (dwd1,), e = _ffn_wgrad_down(act1, dh1, "ffn1_wgrad_down", ex and [r_in.pair()])
    dwd1 = dwd1.reshape(wd1.shape)
    r_d1 = red(["ffn1_w_down"], [dwd1])
    (dx, dga1, dua1, dg_ffn1), e = _ffn_bwd_data(dh1, x, g_ffn1, *hid1, wg1, wu1, wd1, "ffn1_bwd",
                                                  ex and [r_in.chips(e[0]), r_d1.pair()])
    (dwg1,), e = _ffn_wgrad_gu(xn1, [dga1], "ffn1_wgrad_gate", ex and [_finish_ride(r_in.halves(e[0])), r_d1.chips(e[1])])
    dwg1 = dwg1.reshape(wg1.shape)
    if ex:
        (dwin,), oth_d1 = e
        r_g1 = red(["ffn1_w_gate"], [dwg1])
        got_g1 = _run(r_g1.pair(), "pair_exchange_ffn1_gate")
    (dwu1,), e = _ffn_wgrad_gu(xn1, [dua1], "ffn1_wgrad_up", ex and [r_g1.chips(got_g1)])
    dwu1 = dwu1.reshape(wu1.shape)
    gain_parts = [dg_ffn1, dg_mix, dg_ret, dg_ffn2, dg_fin]
    if not ex:
        return loss_p, dx, [dwg1, dwu1, dwd1, dwin, dwo, dwg2, dwu2, dwd2], gain_parts
    r_u1 = red(["ffn1_w_up"], [dwu1])
    got_u1 = _run(r_u1.pair(), "pair_exchange_ffn1_up")
    oth_u1 = _run(r_u1.chips(got_u1), "chip_exchange_ffn1_up")
    last = r_g1.halves(e[0]) + r_u1.halves(oth_u1) + r_d1.halves(oth_d1)
    dwg1, dwu1, dwd1, gall = _run(_finish_ride(last, _pack_gains(gain_parts, x.shape[1])), "finish_exchange_ffn1")
    return loss_p, dx, [dwg1, dwu1, dwd1, dwin, dwo, dwg2, dwu2, dwd2], gall


N_DEV = 8
GAIN_ROWS = 8


def _place():
    x, y, c = lax.axis_index("x"), lax.axis_index("y"), lax.axis_index("c")
    chips = [(1 - x, y), (x, 1 - y), (1 - x, 1 - y)]
    return x, y, c, chips


ROW_QUARTERS = 4


def _place_shards(place, ws):
    n = len(ws)

    def body(place_ref, *refs):
        for w_ref, o_ref in zip(refs[:n], refs[n:]):
            o_ref[...] = w_ref[...].astype(BF16)

    quarter = lambda w: (w.shape[0] // ROW_QUARTERS, w.shape[1])
    return pl.pallas_call(
        body, name="place_shards",
        grid_spec=pltpu.PrefetchScalarGridSpec(
            num_scalar_prefetch=1, grid=(ROW_QUARTERS,),
            in_specs=[pl.BlockSpec(quarter(w), lambda i, pr: (i, 0)) for w in ws],
            out_specs=[pl.BlockSpec((None,) + quarter(w), lambda i, pr: (pr[0], i, 0)) for w in ws]),
        out_shape=[jax.ShapeDtypeStruct((N_SHARD,) + w.shape, BF16) for w in ws],
        compiler_params=_params("arbitrary"),
    )(place, *ws)


def _gather_ride(bufs):
    na = len(bufs)

    def legs(outs, sems):
        send_sem, recv_sem, fsend_sem, frecv_sem = sems
        x, y, c, chips = _place()

        def half(a, idx, which):
            hr = outs[a].shape[1] // 2
            return outs[a].at[idx, pl.ds(which * hr, hr)]

        def ici(a, j, idx):
            px, py = chips[j]
            return pltpu.make_async_remote_copy(
                src_ref=half(a, idx, c), dst_ref=half(a, idx, c),
                send_sem=send_sem.at[a, j], recv_sem=recv_sem.at[a, j], device_id=(px, py, c), device_id_type=MESH)

        def d2d(a, j, idx, which):
            return pltpu.make_async_remote_copy(
                src_ref=half(a, idx, which), dst_ref=half(a, idx, which),
                send_sem=fsend_sem.at[a, j], recv_sem=frecv_sem.at[a, j], device_id=(x, y, 1 - c), device_id_type=MESH)

        return 2 * x + y, c, chips, ici, d2d

    def start(ins, outs, sems):
        me, _, _, ici, _ = legs(outs, sems)
        for a in range(na):
            for j in range(3):
                ici(a, j, me).start()

    def finish(ins, outs, sems):
        me, c, chips, ici, d2d = legs(outs, sems)
        passed = []
        for a in range(na):
            for j, (px, py) in enumerate(chips):
                ici(a, j, 2 * px + py).wait_recv()
                cp = d2d(a, j, 2 * px + py, c)
                cp.start()
                passed.append(cp)
        for a in range(na):
            for j, (px, py) in enumerate(chips):
                d2d(a, j, 2 * px + py, 1 - c).wait_recv()
        for a in range(na):
            for j in range(3):
                ici(a, j, me).wait_send()
        for cp in passed:
            cp.wait_send()

    return _Ride(bufs, [jax.ShapeDtypeStruct(b.shape, b.dtype) for b in bufs], [pltpu.SemaphoreType.DMA((na, 3))] * 4,
                 start, finish, {a: a for a in range(na)})


def _pair_ride(grads):
    na = len(grads)

    def copies(ins, outs, sems):
        send_sem, recv_sem = sems
        x, y, c, _ = _place()
        res = []
        for a in range(na):
            hr = ins[a].shape[1] // 2
            res.append(pltpu.make_async_remote_copy(
                src_ref=ins[a].at[:, pl.ds((1 - c) * hr, hr)], dst_ref=outs[a],
                send_sem=send_sem.at[a], recv_sem=recv_sem.at[a], device_id=(x, y, 1 - c), device_id_type=MESH))
        return res

    def start(ins, outs, sems):
        for cp in copies(ins, outs, sems):
            cp.start()

    def finish(ins, outs, sems):
        for cp in copies(ins, outs, sems):
            cp.wait()

    return _Ride(grads, [jax.ShapeDtypeStruct((g.shape[0], g.shape[1] // 2, g.shape[2]), g.dtype) for g in grads],
                 [pltpu.SemaphoreType.DMA((na,))] * 2, start, finish)


def _chip_ride(sums):
    na = len(sums)

    def copies(ins, outs, sems):
        send_sem, recv_sem = sems
        x, y, c, chips = _place()
        res = []
        for a in range(na):
            for j, (px, py) in enumerate(chips):
                res.append(pltpu.make_async_remote_copy(
                    src_ref=ins[a].at[2 * px + py], dst_ref=outs[a].at[j],
                    send_sem=send_sem.at[a, j], recv_sem=recv_sem.at[a, j], device_id=(px, py, c), device_id_type=MESH))
        return res

    def start(ins, outs, sems):
        for cp in copies(ins, outs, sems):
            cp.start()

    def finish(ins, outs, sems):
        for cp in copies(ins, outs, sems):
            cp.wait()

    return _Ride(sums, [jax.ShapeDtypeStruct((3,) + s.shape[1:], s.dtype) for s in sums],
                 [pltpu.SemaphoreType.DMA((na, 3))] * 2, start, finish)


def _finish_ride(grads, gpack=None):
    na = len(grads)

    def halves(outs, sems, which):
        x, y, c, _ = _place()
        res = []
        for a in range(na):
            hr = outs[a].shape[0] // 2
            rows = outs[a].at[pl.ds((c if which == "mine" else 1 - c) * hr, hr)]
            res.append(pltpu.make_async_remote_copy(
                src_ref=rows, dst_ref=rows, send_sem=sems[0].at[a], recv_sem=sems[1].at[a],
                device_id=(x, y, 1 - c), device_id_type=MESH))
        return res

    def gains(ins, outs, sems):
        x, y, c, _ = _place()
        dev = 4 * x + 2 * y + c
        g_in, g_out = ins[na], outs[na]
        own = pltpu.make_async_copy(g_in, g_out.at[dev], sems[2])
        sends, lands = [], []
        for k in range(N_DEV - 1):
            bx, by, bc = (k + 1) // 4, ((k + 1) // 2) % 2, (k + 1) % 2
            peer = (jnp.bitwise_xor(x, bx), jnp.bitwise_xor(y, by), jnp.bitwise_xor(c, bc))
            sends.append(pltpu.make_async_remote_copy(
                src_ref=g_in, dst_ref=g_out.at[dev], send_sem=sems[3].at[k], recv_sem=sems[4].at[k],
                device_id=peer, device_id_type=MESH))
            slot = g_out.at[jnp.bitwise_xor(dev, k + 1)]
            lands.append(pltpu.make_async_remote_copy(
                src_ref=slot, dst_ref=slot, send_sem=sems[3].at[k], recv_sem=sems[4].at[k],
                device_id=peer, device_id_type=MESH))
        return own, sends, lands

    def start(ins, outs, sems):
        for cp in halves(outs, sems, "mine"):
            cp.start()
        if gpack is not None:
            own, sends, _ = gains(ins, outs, sems)
            own.start()
            for cp in sends:
                cp.start()

    def finish(ins, outs, sems):
        for cp in halves(outs, sems, "sibling's"):
            cp.wait_recv()
        if gpack is not None:
            own, sends, lands = gains(ins, outs, sems)
            for cp in lands:
                cp.wait_recv()
            for cp in sends:
                cp.wait_send()
            own.wait()
        for cp in halves(outs, sems, "mine"):
            cp.wait_send()

    shapes = [jax.ShapeDtypeStruct(g.shape, g.dtype) for g in grads]
    sems = [pltpu.SemaphoreType.DMA((na,))] * 2
    if gpack is None:
        return _Ride(grads, shapes, sems, start, finish, {a: a for a in range(na)})
    return _Ride(list(grads) + [gpack], shapes + [jax.ShapeDtypeStruct((N_DEV,) + gpack.shape, gpack.dtype)],
                 sems + [pltpu.SemaphoreType.DMA, pltpu.SemaphoreType.DMA((N_DEV - 1,)), pltpu.SemaphoreType.DMA((N_DEV - 1,))],
                 start, finish, {a: a for a in range(na)})


def _pair_sum(place, grad, got, name):
    ns, r, cols = grad.shape
    hr = r // 2

    def body(place_ref, g_ref, r_ref, o_ref):
        o_ref[...] = (g_ref[...] + r_ref[...]).astype(BF16)

    return pl.pallas_call(
        body, name=name,
        grid_spec=pltpu.PrefetchScalarGridSpec(
            num_scalar_prefetch=1, grid=(ns,),
            in_specs=[pl.BlockSpec((None, hr, cols), lambda s, pr: (s, pr[1], 0)),
                      pl.BlockSpec((None, hr, cols), lambda s, pr: (s, 0, 0))],
            out_specs=pl.BlockSpec((None, hr, cols), lambda s, pr: (s, 0, 0))),
        out_shape=jax.ShapeDtypeStruct((ns, hr, cols), BF16),
        compiler_params=_params("arbitrary"),
    )(place, grad, got)


def _chip_sum(place, grad, got, others, name):
    ns, r, cols = grad.shape
    hr = r // 2
    nb = 2
    tr = hr // nb

    def body(place_ref, g_ref, r_ref, o3_ref, o_ref):
        acc = g_ref[...] + r_ref[...]
        for j in range(3):
            acc = acc + o3_ref[j].astype(F32)
        o_ref[...] = acc

    return pl.pallas_call(
        body, name=name,
        grid_spec=pltpu.PrefetchScalarGridSpec(
            num_scalar_prefetch=1, grid=(nb,),
            in_specs=[pl.BlockSpec((None, tr, cols), lambda i, pr: (pr[0], pr[1] * nb + i, 0)),
                      pl.BlockSpec((None, tr, cols), lambda i, pr: (pr[0], i, 0)),
                      pl.BlockSpec((3, tr, cols), lambda i, pr: (0, i, 0))],
            out_specs=pl.BlockSpec((tr, cols), lambda i, pr: (pr[1] * nb + i, 0))),
        out_shape=jax.ShapeDtypeStruct((r, cols), F32),
        compiler_params=_params("arbitrary"),
    )(place, grad, got, others)


def _pack_gains(parts, d):
    def body(*refs):
        ins, o_ref = refs[:-1], refs[-1]
        o_ref[...] = jnp.zeros_like(o_ref)
        for k, r in enumerate(ins):
            o_ref[k:k + 1, 0:r.shape[1]] = jnp.sum(r[...], axis=0, keepdims=True)

    return pl.pallas_call(
        body, name="pack_gains", out_shape=jax.ShapeDtypeStruct((GAIN_ROWS, d), F32),
    )(*parts)


def _adamw_math(w, g, m, v):
    m = ADAM_B1 * m + (1.0 - ADAM_B1) * g
    v = ADAM_B2 * v + (1.0 - ADAM_B2) * jnp.square(g)
    m_hat = m / (1.0 - ADAM_B1 ** ADAM_STEP)
    v_hat = v / (1.0 - ADAM_B2 ** ADAM_STEP)
    return -ADAM_LR * (m_hat / (jnp.sqrt(v_hat) + ADAM_EPS) + ADAM_WD * w), m, v


def _adamw(ws, gs, ms, vs):
    n = len(ws)

    def body(*refs):
        ins, outs = refs[:4 * n], refs[4 * n:]
        for k in range(n):
            w_ref, g_ref, m_ref, v_ref = ins[4 * k:4 * k + 4]
            go_ref, d_ref, nm_ref, nv_ref = outs[4 * k:4 * k + 4]
            g = g_ref[...]
            go_ref[...] = g
            d_ref[...], nm_ref[...], nv_ref[...] = _adamw_math(w_ref[...], g, m_ref[...], v_ref[...])

    parts = 2 * ROW_QUARTERS
    tile = lambda w: pl.BlockSpec((w.shape[0] // parts, w.shape[1]), lambda i: (i, 0))
    res = pl.pallas_call(
        body, name="adamw_shards", grid=(parts,),
        in_specs=[tile(w) for w in ws for _ in range(4)], out_specs=[tile(w) for w in ws for _ in range(4)],
        out_shape=[jax.ShapeDtypeStruct(w.shape, F32) for w in ws for _ in range(4)],
        compiler_params=_params("arbitrary"),
    )(*[a for quad in zip(ws, gs, ms, vs) for a in quad])
    return [res[4 * k:4 * k + 4] for k in range(n)]


def _adamw_gain(gall, row, w, m, v, name):
    n = w.shape[1]

    def body(ga_ref, w_ref, m_ref, v_ref, g_ref, d_ref, nm_ref, nv_ref):
        g = ga_ref[0, row:row + 1, 0:n]
        for k in range(1, N_DEV):
            g = g + ga_ref[k, row:row + 1, 0:n]
        g_ref[...] = g
        d_ref[...], nm_ref[...], nv_ref[...] = _adamw_math(w_ref[...], g, m_ref[...], v_ref[...])

    return pl.pallas_call(
        body, name=name, out_shape=[jax.ShapeDtypeStruct((1, n), F32)] * 4,
    )(gall, w, m, v)


def kernel(x, norm_ffn1, ffn1_w_gate, ffn1_w_up, ffn1_w_down, norm_mix, w_in, ret_norm_gain, w_out, norm_ffn2, ffn2_w_gate, ffn2_w_up, ffn2_w_down, norm_final, loss_target, m_norm_ffn1, m_ffn1_w_gate, m_ffn1_w_up, m_ffn1_w_down, m_norm_mix, m_w_in, m_ret_norm_gain, m_w_out, m_norm_ffn2, m_ffn2_w_gate, m_ffn2_w_up, m_ffn2_w_down, m_norm_final, v_norm_ffn1, v_ffn1_w_gate, v_ffn1_w_up, v_ffn1_w_down, v_norm_mix, v_w_in, v_ret_norm_gain, v_w_out, v_norm_ffn2, v_ffn2_w_gate, v_ffn2_w_up, v_ffn2_w_down, v_norm_final):
    d = x.shape[-1]
    mats = [ffn1_w_gate, ffn1_w_up, ffn1_w_down, w_in, w_out, ffn2_w_gate, ffn2_w_up, ffn2_w_down]
    mats_m = [m_ffn1_w_gate, m_ffn1_w_up, m_ffn1_w_down, m_w_in, m_w_out, m_ffn2_w_gate, m_ffn2_w_up, m_ffn2_w_down]
    mats_v = [v_ffn1_w_gate, v_ffn1_w_up, v_ffn1_w_down, v_w_in, v_w_out, v_ffn2_w_gate, v_ffn2_w_up, v_ffn2_w_down]
    mat_names = ["ffn1_w_gate", "ffn1_w_up", "ffn1_w_down", "w_in", "w_out", "ffn2_w_gate", "ffn2_w_up", "ffn2_w_down"]
    gains = [norm_ffn1, norm_mix, ret_norm_gain, norm_ffn2, norm_final.reshape(1, d)]
    gains_m = [m_norm_ffn1, m_norm_mix, m_ret_norm_gain, m_norm_ffn2, m_norm_final.reshape(1, d)]
    gains_v = [v_norm_ffn1, v_norm_mix, v_ret_norm_gain, v_norm_ffn2, v_norm_final.reshape(1, d)]
    gain_names = ["norm_ffn1", "norm_mix", "ret_norm_gain", "norm_ffn2", "norm_final"]

    turned = lambda n: n.endswith(("w_gate", "w_up"))
    local = lambda a, n: jnp.swapaxes(a, 1, 2)[0] if turned(n) else a[0]
    back = lambda a, n: jnp.swapaxes(a[None], 1, 2) if turned(n) else a[None]
    shards = [local(w, n) for w, n in zip(mats, mat_names)]
    place = jnp.stack([2 * lax.axis_index("x") + lax.axis_index("y"), lax.axis_index("c")]).astype(jnp.int32)
    placed = _place_shards(place, shards)
    loss_p, dx, shard_grads, gall = _step(x[0], loss_target[0], gains, placed, place)

    out_g, out_d, out_m, out_v = {}, {}, {}, {}
    updates = _adamw(shards, shard_grads, [local(m, n) for m, n in zip(mats_m, mat_names)],
                     [local(v, n) for v, n in zip(mats_v, mat_names)])
    for n, quad in zip(mat_names, updates):
        out_g[n], out_d[n], out_m[n], out_v[n] = [back(a, n) for a in quad]
    for row, (n, w, m, v) in enumerate(zip(gain_names, gains, gains_m, gains_v)):
        res = _adamw_gain(gall, row, w, m, v, f"adamw_{n}")
        shape = (d,) if n == "norm_final" else w.shape
        out_g[n], out_d[n], out_m[n], out_v[n] = [r.reshape(shape) for r in res]

    loss = lax.psum(jnp.sum(loss_p), ("x", "y", "c"))
    order = ["norm_ffn1", "ffn1_w_gate", "ffn1_w_up", "ffn1_w_down", "norm_mix", "w_in", "ret_norm_gain", "w_out",
             "norm_ffn2", "ffn2_w_gate", "ffn2_w_up", "ffn2_w_down", "norm_final"]
    return (loss, dx[None], *[out_g[n] for n in order], *[out_d[n] for n in order],
            *[out_m[n] for n in order], *[out_v[n] for n in order])
```

```python
import functools

import jax
import jax.numpy as jnp
from jax import lax
from jax.experimental import pallas as pl
from jax.experimental.pallas import tpu as pltpu

F32 = jnp.float32
BF16 = jnp.bfloat16
MESH = pl.DeviceIdType.MESH

NORM_EPS = 1e-6
GN_EPS = 1e-6
ROPE_BASE = 10000.0
RET_HEADS = 4
RET_DIM = 128
RET_WIDTH = 512
RET_CHUNK = 128
ATT_DIM = 64
ATT_WIDTH = 512
ATT_BLOCK = 128
DILATIONS = (1, 4, 16)
LANE = 128
N_SHARD = 4
ADAM_LR, ADAM_B1, ADAM_B2, ADAM_EPS, ADAM_WD, ADAM_STEP = 0.001, 0.9, 0.999, 1e-08, 0.01, 10

V7X_VMEM_BYTES = 64 * 1024 * 1024
VMEM_LIMIT = V7X_VMEM_BYTES - 8 * 1024 * 1024

NT = (((1,), (1,)), ((), ()))
TN = (((0,), (0,)), ((), ()))


def _params(*sem):
    return pltpu.CompilerParams(dimension_semantics=sem, vmem_limit_bytes=VMEM_LIMIT)


def _dot(a, b, dims=None):
    if dims is None:
        return jnp.dot(a, b, preferred_element_type=F32)
    return lax.dot_general(a, b, dims, preferred_element_type=F32)


def _sigmoid(x):
    return 1.0 / (1.0 + jnp.exp(-x))


def _load_weights(pairs, sems):
    copies = [pltpu.make_async_copy(src, dst, sems.at[k]) for k, (src, dst) in enumerate(pairs)]
    for cp in copies:
        cp.start()
    for cp in copies:
        cp.wait()


def _rows8(v):
    r, c = v.shape
    return v.reshape(r // 8, 8, c).sum(axis=0)


class _Ride:
    def __init__(self, inputs, out_shapes, sems, start, finish, aliases=None):
        self.inputs, self.out_shapes, self.sems = list(inputs), list(out_shapes), list(sems)
        self.start, self.finish, self.aliases = start, finish, dict(aliases or {})


def _pallas(body, rides, *, name, in_specs, out_specs, out_shape, args, grid=(), scratch_shapes=(), sem=()):
    rides = [r for r in (rides or []) if r is not None]
    n_in, n_out, n_scr = len(args), len(out_shape), len(scratch_shapes)
    hbm = pl.BlockSpec(memory_space=pl.ANY)
    r_in = [a for r in rides for a in r.inputs]
    r_out = [s for r in rides for s in r.out_shapes]
    r_sem = [s for r in rides for s in r.sems]
    aliases, spans, ki, ko, ks = {}, [], 0, 0, 0
    for r in rides:
        aliases.update({n_in + ki + i: n_out + ko + o for i, o in r.aliases.items()})
        spans.append((ki, ko, ks))
        ki, ko, ks = ki + len(r.inputs), ko + len(r.out_shapes), ks + len(r.sems)

    def wrapped(*refs):
        ins, rin = refs[:n_in], refs[n_in:n_in + len(r_in)]
        o0 = n_in + len(r_in)
        outs, rout = refs[o0:o0 + n_out], refs[o0 + n_out:o0 + n_out + len(r_out)]
        s0 = o0 + n_out + len(r_out)
        scr, rsem = refs[s0:s0 + n_scr], refs[s0 + n_scr:]
        part = lambda r, k: (rin[spans[k][0]:spans[k][0] + len(r.inputs)], rout[spans[k][1]:spans[k][1] + len(r.out_shapes)],
                             rsem[spans[k][2]:spans[k][2] + len(r.sems)])
        first = functools.reduce(jnp.logical_and, [pl.program_id(k) == 0 for k in range(len(grid))], True)
        last = functools.reduce(jnp.logical_and, [pl.program_id(k) == grid[k] - 1 for k in range(len(grid))], True)
        if rides:
            @pl.when(first)
            def _():
                for k, r in enumerate(rides):
                    r.start(*part(r, k))

        body(*ins, *outs, *scr)
        if rides:
            @pl.when(last)
            def _():
                for k, r in enumerate(rides):
                    r.finish(*part(r, k))

    res = pl.pallas_call(
        wrapped, name=name, grid=grid,
        in_specs=list(in_specs) + [hbm] * len(r_in), out_specs=list(out_specs) + [hbm] * len(r_out),
        out_shape=list(out_shape) + r_out, input_output_aliases=aliases,
        scratch_shapes=list(scratch_shapes) + r_sem,
        compiler_params=pltpu.CompilerParams(dimension_semantics=sem, vmem_limit_bytes=VMEM_LIMIT) if grid else None,
    )(*args, *r_in)
    extras = [list(res[n_out + ko:n_out + ko + len(r.out_shapes)]) for r, (_, ko, _) in zip(rides, spans)]
    return list(res[:n_out]), extras


def _run(ride, name):
    def body(*refs):
        n_in, n_out = len(ride.inputs), len(ride.out_shapes)
        parts = refs[:n_in], refs[n_in:n_in + n_out], refs[n_in + n_out:]
        ride.start(*parts)
        ride.finish(*parts)

    hbm = pl.BlockSpec(memory_space=pl.ANY)
    return list(pl.pallas_call(
        body, name=name, in_specs=[hbm] * len(ride.inputs), out_specs=[hbm] * len(ride.out_shapes),
        out_shape=ride.out_shapes, input_output_aliases=ride.aliases, scratch_shapes=ride.sems,
    )(*ride.inputs))


def _loss_head(hv, gain_ref, tg_ref, loss_ref, dgain_ref):
    d = hv.shape[1]
    r = lax.rsqrt(jnp.mean(hv * hv, axis=-1, keepdims=True) + NORM_EPS)
    xh = hv * r
    err = xh * gain_ref[...] - tg_ref[...]
    sq = _rows8(jnp.square(err))
    loss_ref[...] += 0.5 * functools.reduce(jnp.add, [sq[:, k * LANE:(k + 1) * LANE] for k in range(d // LANE)]) / d
    dy = err / d
    dgain_ref[...] += _rows8(dy * xh)
    dxh = dy * gain_ref[...]
    return r * (dxh - xh * jnp.mean(dxh * xh, axis=-1, keepdims=True))


V7X_MXU_TILE = 256
FFN_CHUNK_TILES = 3


def _hidden_chunks(f):
    step = FFN_CHUNK_TILES * V7X_MXU_TILE
    return [slice(s, min(s + step, f)) for s in range(0, f, step)]


def _flat(w):
    return w.reshape(w.shape[0] * w.shape[1], w.shape[2])


def _ffn_fwd(x, gain, wg, wu, wd, name, rides=None, head=None):
    t, d = x.shape
    wg, wu, wd = _flat(wg), _flat(wu), _flat(wd)
    f = wg.shape[0]
    tm = min(256, t)
    nh = 0 if head is None else 2

    def body(*refs):
        x_ref, gain_ref = refs[:2]
        wg_hbm, wu_hbm, wd_hbm, h_ref, xn_ref, g_ref, u_ref, a_ref = refs[2 + nh:10 + nh]
        sums = refs[10 + nh:12 + nh]
        wg_v, wu_v, wd_v, sems = refs[-4:]

        @pl.when(pl.program_id(0) == 0)
        def _():
            _load_weights([(wg_hbm, wg_v), (wu_hbm, wu_v), (wd_hbm, wd_v)], sems)
            if head is not None:
                for s_ref in sums:
                    s_ref[...] = jnp.zeros_like(s_ref)

        xv = x_ref[...]
        r = lax.rsqrt(jnp.mean(xv * xv, axis=-1, keepdims=True) + NORM_EPS)
        xn = (xv * r * gain_ref[...]).astype(BF16)
        xn_ref[...] = xn
        acc = jnp.zeros((tm, d), F32)
        for c in _hidden_chunks(f):
            g = _dot(xn, wg_v[c, :], NT)
            u = _dot(xn, wu_v[c, :], NT)
            g_ref[:, c] = g.astype(BF16)
            u_ref[:, c] = u.astype(BF16)
            a = (g * _sigmoid(g) * u).astype(BF16)
            a_ref[:, c] = a
            acc = acc + _dot(a, wd_v[c, :])
        hv = xv + 0.5 * acc
        h_ref[...] = hv if head is None else _loss_head(hv, refs[2], refs[3], *sums)

    hbm = pl.BlockSpec(memory_space=pl.ANY)
    hid = pl.BlockSpec((tm, f), lambda i: (i, 0))
    tile = pl.BlockSpec((tm, d), lambda i: (i, 0))
    row = pl.BlockSpec((1, d), lambda i: (0, 0))
    sums = [] if head is None else [(pl.BlockSpec((8, LANE), lambda i: (0, 0)), jax.ShapeDtypeStruct((8, LANE), F32)),
                                    (pl.BlockSpec((8, d), lambda i: (0, 0)), jax.ShapeDtypeStruct((8, d), F32))]
    return _pallas(
        body, rides, name=name, grid=(t // tm,),
        in_specs=[tile, row] + ([] if head is None else [row, tile]) + [hbm, hbm, hbm],
        out_specs=[tile, tile, hid, hid, hid] + [s for s, _ in sums],
        out_shape=[jax.ShapeDtypeStruct((t, d), F32), jax.ShapeDtypeStruct((t, d), BF16)]
        + [jax.ShapeDtypeStruct((t, f), BF16)] * 3 + [s for _, s in sums],
        scratch_shapes=[pltpu.VMEM(wg.shape, BF16), pltpu.VMEM(wu.shape, BF16), pltpu.VMEM(wd.shape, BF16),
                        pltpu.SemaphoreType.DMA((3,))],
        sem=("arbitrary",), args=[x, gain] + ([] if head is None else list(head)) + [wg, wu, wd])


def _ffn_bwd_data(dy, x, gain, g, u, wg, wu, wd, name, rides=None):
    t, d = x.shape
    wg, wu, wd = _flat(wg), _flat(wu), _flat(wd)
    f = wg.shape[0]
    tm = min(256, t)

    def body(dy_ref, x_ref, gain_ref, g_ref, u_ref, wg_hbm, wu_hbm, wd_hbm, dx_ref, dg_ref, du_ref, dgain_ref,
             wg_v, wu_v, wd_v, sems):
        @pl.when(pl.program_id(0) == 0)
        def _():
            _load_weights([(wg_hbm, wg_v), (wu_hbm, wu_v), (wd_hbm, wd_v)], sems)
            dgain_ref[...] = jnp.zeros_like(dgain_ref)

        dyv = dy_ref[...]
        dyh = (0.5 * dyv).astype(BF16)
        dxn = jnp.zeros((tm, d), F32)
        chunks = _hidden_chunks(f)
        das = [_dot(dyh, wd_v[c, :], NT) for c in chunks]
        for c, da in zip(chunks, das):
            gj = g_ref[:, c].astype(F32)
            uj = u_ref[:, c].astype(F32)
            sig = _sigmoid(gj)
            dgj = (da * uj * (sig * (1.0 + gj * (1.0 - sig)))).astype(BF16)
            duj = (da * (gj * sig)).astype(BF16)
            dg_ref[:, c] = dgj
            du_ref[:, c] = duj
            dxn = dxn + _dot(dgj, wg_v[c, :]) + _dot(duj, wu_v[c, :])
        xv = x_ref[...]
        r = lax.rsqrt(jnp.mean(xv * xv, axis=-1, keepdims=True) + NORM_EPS)
        xh = xv * r
        dgain_ref[...] += _rows8(dxn * xh)
        dxh = dxn * gain_ref[...]
        dx_ref[...] = dyv + r * (dxh - xh * jnp.mean(dxh * xh, axis=-1, keepdims=True))

    hbm = pl.BlockSpec(memory_space=pl.ANY)
    tile = pl.BlockSpec((tm, d), lambda i: (i, 0))
    hid = pl.BlockSpec((tm, f), lambda i: (i, 0))
    return _pallas(
        body, rides, name=name, grid=(t // tm,),
        in_specs=[tile, tile, pl.BlockSpec((1, d), lambda i: (0, 0)), hid, hid, hbm, hbm, hbm],
        out_specs=[tile, hid, hid, pl.BlockSpec((8, d), lambda i: (0, 0))],
        out_shape=[jax.ShapeDtypeStruct((t, d), F32), jax.ShapeDtypeStruct((t, f), BF16),
                   jax.ShapeDtypeStruct((t, f), BF16), jax.ShapeDtypeStruct((8, d), F32)],
        scratch_shapes=[pltpu.VMEM(wg.shape, BF16), pltpu.VMEM(wu.shape, BF16), pltpu.VMEM(wd.shape, BF16),
                        pltpu.SemaphoreType.DMA((3,))],
        sem=("arbitrary",), args=[dy, x, gain, g, u, wg, wu, wd])


WGRAD_ROW_BLOCKS = 2


def _ffn_wgrad_down(a, dy, name, rides=None):
    t, d = dy.shape
    f = a.shape[1]
    fb = f // WGRAD_ROW_BLOCKS
    tk = min(1024, t)

    def body(dy_ref, a_ref, dwd_ref):
        @pl.when(pl.program_id(1) == 0)
        def _():
            dwd_ref[...] = jnp.zeros_like(dwd_ref)

        dwd_ref[...] += _dot(a_ref[...], (0.5 * dy_ref[...]).astype(BF16), TN)

    return _pallas(
        body, rides, name=name, grid=(WGRAD_ROW_BLOCKS, t // tk),
        in_specs=[pl.BlockSpec((tk, d), lambda j, k: (k, 0)), pl.BlockSpec((tk, fb), lambda j, k: (k, j))],
        out_specs=[pl.BlockSpec((fb, d), lambda j, k: (j, 0))],
        out_shape=[jax.ShapeDtypeStruct((f, d), F32)],
        sem=("arbitrary", "arbitrary"), args=[dy, a])


def _ffn_wgrad_gu(xn, dhs, name, rides=None):
    t, d = xn.shape
    n = len(dhs)
    f = dhs[0].shape[1]
    fb = f // WGRAD_ROW_BLOCKS
    tk = min(2048 // n, t)

    def body(xn_ref, *refs):
        @pl.when(pl.program_id(1) == 0)
        def _():
            for o_ref in refs[n:]:
                o_ref[...] = jnp.zeros_like(o_ref)

        xnv = xn_ref[...]
        for dh_ref, o_ref in zip(refs[:n], refs[n:]):
            o_ref[...] += _dot(dh_ref[...], xnv, TN)

    hid = pl.BlockSpec((tk, fb), lambda j, k: (k, j))
    out = pl.BlockSpec((fb, d), lambda j, k: (j, 0))
    return _pallas(
        body, rides, name=name, grid=(WGRAD_ROW_BLOCKS, t // tk),
        in_specs=[pl.BlockSpec((tk, d), lambda j, k: (k, 0))] + [hid] * n,
        out_specs=[out] * n, out_shape=[jax.ShapeDtypeStruct((f, d), F32)] * n,
        sem=("arbitrary", "arbitrary"), args=[xn] + list(dhs))


def _tn_matmul(a, b, bn, name):
    t, m = a.shape
    n = b.shape[1]
    tk = min(2048, t)

    def body(a_ref, b_ref, o_ref):
        @pl.when(pl.program_id(1) == 0)
        def _():
            o_ref[...] = jnp.zeros_like(o_ref)

        o_ref[...] += _dot(a_ref[...].astype(BF16), b_ref[...].astype(BF16), TN)

    return pl.pallas_call(
        body, name=name, grid=(n // bn, t // tk),
        in_specs=[pl.BlockSpec((tk, m), lambda j, k: (k, 0)), pl.BlockSpec((tk, bn), lambda j, k: (k, j))],
        out_specs=pl.BlockSpec((None, m, bn), lambda j, k: (j, 0, 0)),
        out_shape=jax.ShapeDtypeStruct((n // bn, m, bn), F32),
        compiler_params=_params("arbitrary", "arbitrary"),
    )(a, b)


def _chunk_scratch(tm, w):
    return pltpu.VMEM((w // LANE, tm, LANE), F32)


def _regroup_store(cbuf, out_ref, dil):
    n = out_ref.shape[1]
    for g in range(dil):
        for k in range(cbuf.shape[0]):
            rows = cbuf[k] if dil == 1 else cbuf[k, pl.ds(g, n, stride=dil), :]
            out_ref[g, :, k * LANE:(k + 1) * LANE] = rows.astype(out_ref.dtype)


def _natural_rows(ref, dil, cbuf):
    if dil == 1:
        return ref[0].astype(F32)
    n = ref.shape[1]
    for g in range(dil):
        for k in range(cbuf.shape[0]):
            cbuf[k, pl.ds(g, n, stride=dil), :] = ref[g, :, k * LANE:(k + 1) * LANE].astype(F32)
    return jnp.concatenate([cbuf[k] for k in range(cbuf.shape[0])], axis=1)


def _inproj_fwd(h, gain, win):
    t, d = h.shape
    ns, _, cs = win.shape
    tm = min(512, t)
    rw, aw = 4 * RET_WIDTH, 3 * ATT_WIDTH

    def body(h_ref, gain_ref, w_ref, xn_ref, ur_ref, *rest):
        a_refs, abuf = rest[:-1], rest[-1]
        hv = h_ref[...]
        r = lax.rsqrt(jnp.mean(hv * hv, axis=-1, keepdims=True) + NORM_EPS)
        xn = (hv * r * gain_ref[...]).astype(BF16)
        xn_ref[...] = xn
        for j in range(ns):
            res = _dot(xn, w_ref[j])
            for k in range(cs // LANE):
                chunk = j * (cs // LANE) + k
                piece = res[:, k * LANE:(k + 1) * LANE]
                if chunk < rw // LANE:
                    ur_ref[:, chunk * LANE:(chunk + 1) * LANE] = piece
                else:
                    abuf[chunk - rw // LANE] = piece
        for dil, a_ref in zip(DILATIONS, a_refs):
            _regroup_store(abuf, a_ref, dil)

    return pl.pallas_call(
        body, name="inproj_fwd", grid=(t // tm,),
        in_specs=[pl.BlockSpec((tm, d), lambda i: (i, 0)), pl.BlockSpec((1, d), lambda i: (0, 0)),
                  pl.BlockSpec(win.shape, lambda i: (0, 0, 0))],
        out_specs=[pl.BlockSpec((tm, d), lambda i: (i, 0)), pl.BlockSpec((tm, rw), lambda i: (i, 0))]
        + [pl.BlockSpec((dil, tm // dil, aw), lambda i: (0, i, 0)) for dil in DILATIONS],
        out_shape=[jax.ShapeDtypeStruct((t, d), BF16), jax.ShapeDtypeStruct((t, rw), F32)]
        + [jax.ShapeDtypeStruct((dil, t // dil, aw), BF16) for dil in DILATIONS],
        scratch_shapes=[_chunk_scratch(tm, aw)],
        compiler_params=_params("arbitrary"),
    )(h, gain, win)


def _inproj_bwd(pieces, parts, h, gain, dres, win):
    t, d = h.shape
    ns, _, cs = win.shape
    pw = pieces[0].shape[1]
    tm = min(512, t)
    npc, nk = len(pieces), len(parts[0])
    flat_parts = [a for p in parts for a in p]

    def body(*refs):
        p_refs, a_refs = refs[:npc], refs[npc:npc + len(flat_parts)]
        h_ref, gain_ref, dres_ref, w_ref, dh_ref, du_ref, dgain_ref, buf = refs[npc + len(flat_parts):]

        @pl.when(pl.program_id(0) == 0)
        def _():
            dgain_ref[...] = jnp.zeros_like(dgain_ref)

        for k in range(npc):
            du_ref[:, k * pw:(k + 1) * pw] = p_refs[k][...]
        for k in range(nk):
            acc = None
            for b, dil in enumerate(DILATIONS):
                rows = _natural_rows(a_refs[b * nk + k], dil, buf)
                acc = rows if acc is None else acc + rows
            du_ref[:, (npc + k) * pw:(npc + k + 1) * pw] = acc.astype(BF16)
        dxn = jnp.zeros((tm, d), F32)
        for j in range(ns):
            dxn = dxn + _dot(du_ref[:, j * cs:(j + 1) * cs], w_ref[j], NT)
        hv = h_ref[...]
        r = lax.rsqrt(jnp.mean(hv * hv, axis=-1, keepdims=True) + NORM_EPS)
        xh = hv * r
        dgain_ref[...] += _rows8(dxn * xh)
        dxh = dxn * gain_ref[...]
        dh_ref[...] = dres_ref[...] + r * (dxh - xh * jnp.mean(dxh * xh, axis=-1, keepdims=True))

    tile = pl.BlockSpec((tm, d), lambda i: (i, 0))
    cols = (npc + nk) * pw
    return pl.pallas_call(
        body, name="inproj_bwd", grid=(t // tm,),
        in_specs=[pl.BlockSpec((tm, pw), lambda i: (i, 0))] * npc
        + [_regrouped_spec(tm, dil, pw) for dil in DILATIONS for _ in range(nk)]
        + [tile, pl.BlockSpec((1, d), lambda i: (0, 0)), tile, pl.BlockSpec(win.shape, lambda i: (0, 0, 0))],
        out_specs=[tile, pl.BlockSpec((tm, cols), lambda i: (i, 0)), pl.BlockSpec((8, d), lambda i: (0, 0))],
        out_shape=[jax.ShapeDtypeStruct((t, d), F32), jax.ShapeDtypeStruct((t, cols), BF16),
                   jax.ShapeDtypeStruct((8, d), F32)],
        scratch_shapes=[_chunk_scratch(tm, pw)],
        compiler_params=_params("arbitrary"),
    )(*pieces, *flat_parts, h, gain, dres, win)


def _outproj_fwd(h, mix_r, mix_a, wo):
    t, d = h.shape
    hw = mix_r.shape[1]
    tm = min(512, t)

    def body(h_ref, mr_ref, ma_ref, w_ref, o_ref):
        o_ref[...] = h_ref[...] + _dot(mr_ref[...], w_ref[0:hw, :]) + _dot(ma_ref[...], w_ref[hw:2 * hw, :])

    tile = pl.BlockSpec((tm, d), lambda i: (i, 0))
    half = pl.BlockSpec((tm, hw), lambda i: (i, 0))
    return pl.pallas_call(
        body, name="outproj_fwd", grid=(t // tm,),
        in_specs=[tile, half, half, pl.BlockSpec(wo.shape, lambda i: (0, 0))],
        out_specs=tile, out_shape=jax.ShapeDtypeStruct((t, d), F32),
        compiler_params=_params("arbitrary"),
    )(h, mix_r, mix_a, wo)


def _outproj_bwd(dh, wo, rides=None):
    t, d = dh.shape
    hw = wo.shape[0] // 2
    tm = min(512, t)

    def body(dh_ref, w_ref, dr_ref, da_ref):
        dhb = dh_ref[...].astype(BF16)
        dr_ref[...] = _dot(dhb, w_ref[0:hw, :], NT)
        da_ref[...] = _dot(dhb, w_ref[hw:2 * hw, :], NT)

    half = pl.BlockSpec((tm, hw), lambda i: (i, 0))
    return _pallas(
        body, rides, name="outproj_bwd", grid=(t // tm,),
        in_specs=[pl.BlockSpec((tm, d), lambda i: (i, 0)), pl.BlockSpec(wo.shape, lambda i: (0, 0))],
        out_specs=[half, half],
        out_shape=[jax.ShapeDtypeStruct((t, hw), F32), jax.ShapeDtypeStruct((t, hw), F32)],
        sem=("arbitrary",), args=[dh, wo])


def _retention_tables(t):
    pos = jnp.arange(t, dtype=F32)
    pair = (jnp.arange(RET_DIM) // 2 * 2).astype(F32)
    ang = pos[:, None] * (ROPE_BASE ** (-pair / RET_DIM))[None, :]
    c = RET_CHUNK
    log_g = jnp.log(1.0 - 2.0 ** (-5.0 - jnp.arange(RET_HEADS, dtype=F32)))
    idx = jnp.arange(c, dtype=F32)
    rel = idx[:, None] - idx[None, :]
    decay = jnp.where(rel >= 0, jnp.exp(log_g[:, None, None] * jnp.maximum(rel, 0.0)), 0.0)
    zeta = jnp.exp(log_g[:, None] * (c - 1 - idx)[None, :])
    xi = jnp.exp(log_g[:, None] * (idx + 1)[None, :])
    gc = jnp.exp(log_g * c)
    wide = lambda v: jnp.broadcast_to(v[:, :, None], (RET_HEADS, c, LANE))
    return (jnp.cos(ang), jnp.sin(ang), decay, wide(zeta), wide(xi),
            jnp.broadcast_to(gc[:, None, None], (RET_HEADS, c, LANE)))


def _rot(v):
    lane = lax.broadcasted_iota(jnp.int32, v.shape, 1)
    nxt = pltpu.roll(v, LANE - 1, 1)
    prv = pltpu.roll(v, 1, 1)
    return jnp.where(lane % 2 == 0, -nxt, prv)


def _ret_specs(tr, rev, nt):
    ti = (lambda i: nt - 1 - i) if rev else (lambda i: i)
    col = lambda blk: pl.BlockSpec((tr, RET_WIDTH), lambda i: (ti(i), blk))
    tab = pl.BlockSpec((tr, LANE), lambda i: (ti(i), 0))
    head = pl.BlockSpec((RET_HEADS, RET_CHUNK, LANE), lambda i: (0, 0, 0))
    return col, tab, head


def _ret_chunks(tr, rev=False):
    order = list(range(tr // RET_CHUNK))
    return [(pl.ds(ci * RET_CHUNK, RET_CHUNK), slice(h * RET_DIM, (h + 1) * RET_DIM), h)
            for h in range(RET_HEADS) for ci in (reversed(order) if rev else order)]


def _ret_operands(items, q_ref, k_ref, v_ref, cos_ref, sin_ref, zeta_ref):
    scale = RET_DIM ** -0.5
    qbs, kbs, vbs, kzs = [], [], [], []
    for sl, hs, h in items:
        cs, sn = cos_ref[sl, :], sin_ref[sl, :]
        q, k = q_ref[sl, hs], k_ref[sl, hs]
        kr = (k * cs + _rot(k) * sn) * scale
        qbs.append((q * cs + _rot(q) * sn).astype(BF16))
        kbs.append(kr.astype(BF16))
        vbs.append(v_ref[sl, hs].astype(BF16))
        kzs.append((kr * zeta_ref[h]).astype(BF16))
    return qbs, kbs, vbs, kzs


def _ret_states(items, state, steps, gc_ref):
    cur, befores = {}, []
    for (sl, hs, h), step in zip(items, steps):
        st = cur[h] if h in cur else state[h]
        befores.append(st)
        cur[h] = st * gc_ref[h] + step
    for h, st in cur.items():
        state[h] = st
    return befores


def _ret_fwd(u, gain, tabs):
    t = u.shape[0]
    tr = min(512, t)
    nt = t // tr
    cos, sin, decay, zeta, xi, gc = tabs

    def body(q_ref, k_ref, v_ref, gt_ref, cos_ref, sin_ref, gain_ref, dec_ref, zeta_ref, xi_ref, gc_ref,
             raw_ref, mix_ref, state):
        @pl.when(pl.program_id(0) == 0)
        def _():
            state[...] = jnp.zeros_like(state)

        items = _ret_chunks(tr)
        n = range(len(items))
        qbs, kbs, vbs, kzs = _ret_operands(items, q_ref, k_ref, v_ref, cos_ref, sin_ref, zeta_ref)
        ss = [_dot(qbs[i], kbs[i], NT) for i in n]
        kvs = [_dot(kzs[i], vbs[i], TN) for i in n]
        befores = _ret_states(items, state, kvs, gc_ref)
        intra = [_dot((ss[i] * dec_ref[items[i][2]]).astype(BF16), vbs[i]) for i in n]
        inter = [_dot(qbs[i], befores[i].astype(BF16)) for i in n]
        for i, (sl, hs, h) in enumerate(items):
            o = intra[i] + inter[i] * xi_ref[h]
            raw_ref[sl, hs] = o
            mu = jnp.mean(o, axis=-1, keepdims=True)
            var = jnp.mean(jnp.square(o - mu), axis=-1, keepdims=True)
            y = (o - mu) * lax.rsqrt(var + GN_EPS) * gain_ref[:, hs]
            gt = gt_ref[sl, hs]
            mix_ref[sl, hs] = (y * (gt * _sigmoid(gt))).astype(BF16)

    col, tab, head = _ret_specs(tr, False, nt)
    out = pl.BlockSpec((tr, RET_WIDTH), lambda i: (i, 0))
    return pl.pallas_call(
        body, name="ret_fwd", grid=(nt,),
        in_specs=[col(0), col(1), col(2), col(3), tab, tab, pl.BlockSpec((1, RET_WIDTH), lambda i: (0, 0)),
                  head, head, head, head],
        out_specs=[out, out],
        out_shape=[jax.ShapeDtypeStruct((t, RET_WIDTH), F32), jax.ShapeDtypeStruct((t, RET_WIDTH), BF16)],
        scratch_shapes=[pltpu.VMEM((RET_HEADS, RET_DIM, RET_DIM), F32)],
        compiler_params=_params("arbitrary"),
    )(u, u, u, u, cos, sin, gain, decay, zeta, xi, gc)


def _ret_bwd_q(dmix, raw, u, gain, tabs, rides=None):
    t = u.shape[0]
    tr = min(512, t)
    nt = t // tr
    cos, sin, decay, zeta, xi, gc = tabs

    def body(dm_ref, raw_ref, q_ref, k_ref, v_ref, gt_ref, cos_ref, sin_ref, gain_ref, dec_ref, zeta_ref, xi_ref, gc_ref,
             dq_ref, dgt_ref, dret_ref, dgain_ref, state):
        @pl.when(pl.program_id(0) == 0)
        def _():
            state[...] = jnp.zeros_like(state)
            dgain_ref[...] = jnp.zeros_like(dgain_ref)

        items = _ret_chunks(tr)
        n_items = range(len(items))
        qbs, kbs, vbs, kzs = _ret_operands(items, q_ref, k_ref, v_ref, cos_ref, sin_ref, zeta_ref)
        dos, dgains = [], {}
        for sl, hs, h in items:
            o = raw_ref[sl, hs]
            mu = jnp.mean(o, axis=-1, keepdims=True)
            var = jnp.mean(jnp.square(o - mu), axis=-1, keepdims=True)
            rs = lax.rsqrt(var + GN_EPS)
            n = (o - mu) * rs
            gt = gt_ref[sl, hs]
            sig = _sigmoid(gt)
            dout = dm_ref[sl, hs]
            gain_h = gain_ref[:, hs]
            dgt_ref[sl, hs] = (dout * (n * gain_h) * (sig * (1.0 + gt * (1.0 - sig)))).astype(BF16)
            dy = dout * (gt * sig)
            dgains[h] = dgains[h] + _rows8(dy * n) if h in dgains else _rows8(dy * n)
            dn = dy * gain_h
            do = rs * (dn - jnp.mean(dn, axis=-1, keepdims=True) - n * jnp.mean(dn * n, axis=-1, keepdims=True))
            dret_ref[sl, hs] = do
            dos.append(do)
        for h, dg in dgains.items():
            dgain_ref[:, h * RET_DIM:(h + 1) * RET_DIM] += dg
        dss = [_dot(dos[i].astype(BF16), vbs[i], NT) for i in n_items]
        kvs = [_dot(kzs[i], vbs[i], TN) for i in n_items]
        befores = _ret_states(items, state, kvs, gc_ref)
        intra = [_dot((dss[i] * dec_ref[items[i][2]]).astype(BF16), kbs[i]) for i in n_items]
        inter = [_dot((dos[i] * xi_ref[items[i][2]]).astype(BF16), befores[i].astype(BF16), NT) for i in n_items]
        for i, (sl, hs, h) in enumerate(items):
            dqr = intra[i] + inter[i]
            dq_ref[sl, hs] = (dqr * cos_ref[sl, :] - _rot(dqr * sin_ref[sl, :])).astype(BF16)

    col, tab, head = _ret_specs(tr, False, nt)
    out = pl.BlockSpec((tr, RET_WIDTH), lambda i: (i, 0))
    return _pallas(
        body, rides, name="ret_bwd_q", grid=(nt,),
        in_specs=[out, out, col(0), col(1), col(2), col(3), tab, tab, pl.BlockSpec((1, RET_WIDTH), lambda i: (0, 0)),
                  head, head, head, head],
        out_specs=[out, out, out, pl.BlockSpec((8, RET_WIDTH), lambda i: (0, 0))],
        out_shape=[jax.ShapeDtypeStruct((t, RET_WIDTH), BF16), jax.ShapeDtypeStruct((t, RET_WIDTH), BF16),
                   jax.ShapeDtypeStruct((t, RET_WIDTH), F32), jax.ShapeDtypeStruct((8, RET_WIDTH), F32)],
        scratch_shapes=[pltpu.VMEM((RET_HEADS, RET_DIM, RET_DIM), F32)],
        sem=("arbitrary",), args=[dmix, raw, u, u, u, u, cos, sin, gain, decay, zeta, xi, gc])


def _ret_bwd_kv(dret, u, tabs, rides=None):
    t = u.shape[0]
    tr = min(512, t)
    nt = t // tr
    cos, sin, decay, zeta, xi, gc = tabs
    scale = RET_DIM ** -0.5

    def body(do_ref, q_ref, k_ref, v_ref, cos_ref, sin_ref, dec_ref, zeta_ref, xi_ref, gc_ref, dk_ref, dv_ref, gst):
        @pl.when(pl.program_id(0) == 0)
        def _():
            gst[...] = jnp.zeros_like(gst)

        items = _ret_chunks(tr, rev=True)
        n = range(len(items))
        qbs, kbs, vbs, kzs = _ret_operands(items, q_ref, k_ref, v_ref, cos_ref, sin_ref, zeta_ref)
        dos = [do_ref[sl, hs] for sl, hs, h in items]
        dobs = [do.astype(BF16) for do in dos]
        ss = [_dot(qbs[i], kbs[i], NT) for i in n]
        dss = [_dot(dobs[i], vbs[i], NT) for i in n]
        steps = [_dot(qbs[i], (dos[i] * xi_ref[items[i][2]]).astype(BF16), TN) for i in n]
        afters = [g.astype(BF16) for g in _ret_states(items, gst, steps, gc_ref)]
        dvs = [_dot((ss[i] * dec_ref[items[i][2]]).astype(BF16), dobs[i], TN) + _dot(kzs[i], afters[i]) for i in n]
        dks = [_dot((dss[i] * dec_ref[items[i][2]]).astype(BF16), qbs[i], TN) for i in n]
        dkz = [_dot(vbs[i], afters[i], NT) for i in n]
        for i, (sl, hs, h) in enumerate(items):
            dv_ref[sl, hs] = dvs[i].astype(BF16)
            dkr = (dks[i] + dkz[i] * zeta_ref[h]) * scale
            dk_ref[sl, hs] = (dkr * cos_ref[sl, :] - _rot(dkr * sin_ref[sl, :])).astype(BF16)

    col, tab, head = _ret_specs(tr, True, nt)
    out = pl.BlockSpec((tr, RET_WIDTH), lambda i: (nt - 1 - i, 0))
    return _pallas(
        body, rides, name="ret_bwd_kv", grid=(nt,),
        in_specs=[out, col(0), col(1), col(2), tab, tab, head, head, head, head],
        out_specs=[out, out],
        out_shape=[jax.ShapeDtypeStruct((t, RET_WIDTH), BF16), jax.ShapeDtypeStruct((t, RET_WIDTH), BF16)],
        scratch_shapes=[pltpu.VMEM((RET_HEADS, RET_DIM, RET_DIM), F32)],
        sem=("arbitrary",), args=[dret, u, u, u, cos, sin, decay, zeta, xi, gc])


PAIRS = ATT_WIDTH // LANE
ATT_Q_BLK, ATT_K_BLK, ATT_V_BLK = 0, PAIRS, 2 * PAIRS
STAT_LANES = ATT_DIM // 2


ATT_STEP_ROWS = 1024


def _att_tiles(t, dil):
    sub = t // dil
    tq = min(ATT_STEP_ROWS, sub)
    return sub, tq, sub // tq, tq // ATT_BLOCK, min(dil, ATT_STEP_ROWS // tq)


def _att_in_specs(tq, qb, ti, gs):
    cur = lambda off: pl.BlockSpec((gs, tq, LANE), lambda g, p, i: (g, ti(i), off + p))
    prev = lambda off: pl.BlockSpec((gs, ATT_BLOCK, LANE), lambda g, p, i: (g, jnp.maximum(ti(i) * qb - 1, 0), off + p))
    return [cur(ATT_Q_BLK), cur(ATT_K_BLK), prev(ATT_K_BLK), cur(ATT_V_BLK), prev(ATT_V_BLK)]


def _band_mask():
    key = lax.broadcasted_iota(jnp.int32, (2 * ATT_BLOCK, 2 * ATT_BLOCK), 0)
    qry = lax.broadcasted_iota(jnp.int32, (2 * ATT_BLOCK, 2 * ATT_BLOCK), 1) % ATT_BLOCK
    dist = qry + ATT_BLOCK - key
    return (dist >= 0) & (dist <= ATT_BLOCK), key >= ATT_BLOCK


def _head0_lanes():
    return lax.broadcasted_iota(jnp.int32, (ATT_BLOCK, LANE), 1) < ATT_DIM


def _stack_heads(v, head0):
    zero = jnp.zeros((), v.dtype)
    return jnp.concatenate([jnp.where(head0, v, zero), jnp.where(head0, zero, v)], axis=0)


def _unstack_heads(v, head0):
    return jnp.where(head0, v[0:ATT_BLOCK], v[ATT_BLOCK:])


def _att_fwd(ua, dil):
    sub = ua.shape[1]
    _, tq, nq, qb, gs = _att_tiles(sub * dil, dil)

    def body(q_ref, kc_ref, kp_ref, vc_ref, vp_ref, o_ref, l_ref, kx, vx):
        tile = pl.program_id(2)
        kx[:, 0:ATT_BLOCK, :] = kp_ref[...]
        kx[:, ATT_BLOCK:, :] = kc_ref[...]
        vx[:, 0:ATT_BLOCK, :] = vp_ref[...]
        vx[:, ATT_BLOCK:, :] = vc_ref[...]
        band, cur_keys = _band_mask()
        head0 = _head0_lanes()
        items = [(r, b) for r in range(gs) for b in range(qb)]
        rows = lambda b: slice(b * ATT_BLOCK, (b + 1) * ATT_BLOCK)
        keys = lambda b: slice(b * ATT_BLOCK, (b + 2) * ATT_BLOCK)
        sts = [_dot(kx[r, keys(b), :], _stack_heads(q_ref[r, rows(b), :] * jnp.asarray(ATT_DIM ** -0.5, BF16), head0), NT)
               for r, b in items]
        pts, lses = [], []
        for (r, b), st in zip(items, sts):
            mask = band if b > 0 else band & (cur_keys | (tile > 0))
            st = jnp.where(mask, st, -1e30)
            m = jnp.max(st, axis=0, keepdims=True)
            ex = jnp.exp(st - m)
            den = jnp.sum(ex, axis=0, keepdims=True)
            pts.append((ex * (1.0 / den)).astype(BF16))
            lses.append(m + jnp.log(den))
        outs = [_dot(pt, vx[r, keys(b), :], TN) for (r, b), pt in zip(items, pts)]
        for (r, b), out, lse in zip(items, outs, lses):
            o_ref[r, rows(b), :] = _unstack_heads(out, head0).astype(BF16)
            cols = [jnp.broadcast_to(lse[:, e * ATT_BLOCK:(e + 1) * ATT_BLOCK], (ATT_BLOCK, LANE)).T for e in range(2)]
            l_ref[r, rows(b), :] = jnp.where(head0, cols[0], cols[1])

    out = pl.BlockSpec((gs, tq, LANE), lambda g, p, i: (g, i, p))
    return pl.pallas_call(
        body, name=f"att_fwd_d{dil}", grid=(dil // gs, PAIRS, nq),
        in_specs=_att_in_specs(tq, qb, lambda i: i, gs),
        out_specs=[out, out],
        out_shape=[jax.ShapeDtypeStruct((dil, sub, ATT_WIDTH), BF16), jax.ShapeDtypeStruct((dil, sub, ATT_WIDTH), F32)],
        scratch_shapes=[pltpu.VMEM((gs, tq + ATT_BLOCK, LANE), BF16)] * 2,
        compiler_params=_params("arbitrary", "arbitrary", "arbitrary"),
    )(ua, ua, ua, ua, ua)


def _regrouped_spec(tm, dil, w):
    return pl.BlockSpec((dil, tm // dil, w), lambda i: (0, i, 0))


def _att_combine(outs, lses, t):
    w = ATT_WIDTH
    tm = min(512, t)
    nb = len(outs)

    def body(*refs):
        o_refs, l_refs = refs[:nb], refs[nb:2 * nb]
        mix_ref, att_ref, lse_ref, buf = refs[2 * nb:]
        ls = [_natural_rows(r, dil, buf) for r, dil in zip(l_refs, DILATIONS)]
        m = functools.reduce(jnp.maximum, ls)
        ws = [jnp.exp(l - m) for l in ls]
        den = functools.reduce(jnp.add, ws)
        att = functools.reduce(jnp.add, [(wt / den) * _natural_rows(r, dil, buf) for wt, r, dil in zip(ws, o_refs, DILATIONS)])
        att_ref[...] = att
        mix_ref[...] = att.astype(BF16)
        lse_ref[...] = m + jnp.log(den)

    tile = pl.BlockSpec((tm, w), lambda i: (i, 0))
    regrouped = [_regrouped_spec(tm, dil, w) for dil in DILATIONS]
    return pl.pallas_call(
        body, name="att_combine", grid=(t // tm,),
        in_specs=regrouped * 2, out_specs=[tile, tile, tile],
        out_shape=[jax.ShapeDtypeStruct((t, w), BF16), jax.ShapeDtypeStruct((t, w), F32), jax.ShapeDtypeStruct((t, w), F32)],
        scratch_shapes=[_chunk_scratch(tm, w)],
        compiler_params=_params("arbitrary"),
    )(*outs, *lses)


def _att_bwd_prep(datt, att, lse):
    t, w = datt.shape
    tm = min(512, t)

    def body(da_ref, at_ref, l_ref, *rest):
        outs, dbuf, sbuf = rest[:-2], rest[-2], rest[-1]
        dav = da_ref[...]
        prod = dav * at_ref[...]
        lane = lax.broadcasted_iota(jnp.int32, (tm, LANE), 1)
        for k in range(w // LANE):
            cols = slice(k * LANE, (k + 1) * LANE)
            dbuf[k] = dav[:, cols]
            delta = jnp.concatenate(
                [jnp.broadcast_to(jnp.sum(prod[:, k * LANE + e * ATT_DIM:k * LANE + (e + 1) * ATT_DIM], axis=-1, keepdims=True),
                                  (tm, ATT_DIM)) for e in range(LANE // ATT_DIM)], axis=1)
            sbuf[k] = jnp.where(lane % ATT_DIM < STAT_LANES, l_ref[:, cols], delta)
        for k, dil in enumerate(DILATIONS):
            _regroup_store(dbuf, outs[2 * k], dil)
            _regroup_store(sbuf, outs[2 * k + 1], dil)

    tile = pl.BlockSpec((tm, w), lambda i: (i, 0))
    res = pl.pallas_call(
        body, name="att_bwd_prep", grid=(t // tm,),
        in_specs=[tile] * 3,
        out_specs=[_regrouped_spec(tm, dil, w) for dil in DILATIONS for _ in range(2)],
        out_shape=[jax.ShapeDtypeStruct((dil, t // dil, w), dt) for dil in DILATIONS for dt in (BF16, F32)],
        scratch_shapes=[_chunk_scratch(tm, w)] * 2,
        compiler_params=_params("arbitrary"),
    )(datt, att, lse)
    return [(res[2 * k], res[2 * k + 1]) for k in range(len(DILATIONS))]


def _att_bwd(ua, da, stat, dil, rides=None):
    sub = ua.shape[1]
    _, tq, nq, qb, gs = _att_tiles(sub * dil, dil)
    scale = ATT_DIM ** -0.5

    def body(q_ref, kc_ref, kp_ref, vc_ref, vp_ref, da_ref, st_ref, dq_ref, dk_ref, dv_ref, kx, vx, ck, cv):
        step = pl.program_id(2)
        tile = nq - 1 - step

        @pl.when(step == 0)
        def _():
            ck[...] = jnp.zeros_like(ck)
            cv[...] = jnp.zeros_like(cv)

        kx[:, 0:ATT_BLOCK, :] = kp_ref[...]
        kx[:, ATT_BLOCK:, :] = kc_ref[...]
        vx[:, 0:ATT_BLOCK, :] = vp_ref[...]
        vx[:, ATT_BLOCK:, :] = vc_ref[...]
        band, cur_keys = _band_mask()
        head0 = _head0_lanes()
        items = [(r, b) for r in range(gs) for b in range(qb)]
        n = range(len(items))
        rows = lambda b: slice(b * ATT_BLOCK, (b + 1) * ATT_BLOCK)
        keys = lambda b: slice(b * ATT_BLOCK, (b + 2) * ATT_BLOCK)
        qqs = [_stack_heads(q_ref[r, rows(b), :] * jnp.asarray(scale, BF16), head0) for r, b in items]
        dds = [_stack_heads(da_ref[r, rows(b), :], head0) for r, b in items]
        sts = [_dot(kx[r, keys(b), :], qqs[i], NT) for i, (r, b) in enumerate(items)]
        dpts = [_dot(vx[r, keys(b), :], dds[i], NT) for i, (r, b) in enumerate(items)]
        pts, dsts = [], []
        for i, (r, b) in enumerate(items):
            mask = band if b > 0 else band & (cur_keys | (tile > 0))
            stat = st_ref[r, rows(b), :].T
            row = lambda k: jnp.concatenate([stat[e * ATT_DIM + k:e * ATT_DIM + k + 1, :] for e in range(2)], axis=1)
            pt = jnp.where(mask, jnp.exp(sts[i] - row(0)), 0.0)
            dsts.append((pt * (dpts[i] - row(STAT_LANES))).astype(BF16))
            pts.append(pt.astype(BF16))
        dqs = [_dot(dsts[i], kx[r, keys(b), :], TN) for i, (r, b) in enumerate(items)]
        dkbs = [_dot(dsts[i], qqs[i]) for i in n]
        dvbs = [_dot(pts[i], dds[i]) for i in n]
        for i, (r, b) in enumerate(items):
            dq_ref[r, rows(b), :] = (_unstack_heads(dqs[i], head0) * scale).astype(BF16)
            if b > 0:
                dk_ref[r, rows(b - 1), :] = (dkbs[i - 1][ATT_BLOCK:] + dkbs[i][0:ATT_BLOCK]).astype(BF16)
                dv_ref[r, rows(b - 1), :] = (dvbs[i - 1][ATT_BLOCK:] + dvbs[i][0:ATT_BLOCK]).astype(BF16)
        for r in range(gs):
            first, last = r * qb, r * qb + qb - 1
            dk_ref[r, rows(qb - 1), :] = (dkbs[last][ATT_BLOCK:] + ck[r]).astype(BF16)
            dv_ref[r, rows(qb - 1), :] = (dvbs[last][ATT_BLOCK:] + cv[r]).astype(BF16)
            ck[r] = dkbs[first][0:ATT_BLOCK]
            cv[r] = dvbs[first][0:ATT_BLOCK]

    ti = lambda i: nq - 1 - i
    out = pl.BlockSpec((gs, tq, LANE), lambda g, p, i: (g, ti(i), p))
    shape = jax.ShapeDtypeStruct((dil, sub, ATT_WIDTH), BF16)
    return _pallas(
        body, rides, name=f"att_bwd_d{dil}", grid=(dil // gs, PAIRS, nq),
        in_specs=_att_in_specs(tq, qb, ti, gs) + [out, out],
        out_specs=[out, out, out], out_shape=[shape] * 3,
        scratch_shapes=[pltpu.VMEM((gs, tq + ATT_BLOCK, LANE), BF16)] * 2 + [pltpu.VMEM((gs, ATT_BLOCK, LANE), F32)] * 2,
        sem=("arbitrary", "arbitrary", "arbitrary"), args=[ua, ua, ua, ua, ua, da, stat])


class _Reduction:
    def __init__(self, place, names, grads):
        self.place, self.names, self.grads = place, names, grads

    def pair(self):
        return _pair_ride(self.grads)

    def chips(self, got):
        self.got = got
        return _chip_ride([_pair_sum(self.place, g, r, f"pair_sum_{n}") for g, r, n in zip(self.grads, got, self.names)])

    def halves(self, others):
        return [_chip_sum(self.place, g, r, o, f"chip_sum_{n}")
                for g, r, o, n in zip(self.grads, self.got, others, self.names)]


def _step(x, target, gains, w, place=None):
    t = x.shape[0]
    ex = place is not None
    g_ffn1, g_mix, g_ret, g_ffn2, g_fin = gains
    w = list(w)
    tabs = _retention_tables(t)
    red = lambda names, grads: _Reduction(place, names, grads) if ex else None
    ride = lambda r: [r] if ex else None

    if ex:
        w[0:3] = _run(_gather_ride(w[0:3]), "gather_ffn1_weights")
    (h1, xn1, *hid1, act1), rest = _ffn_fwd(x, g_ffn1, *w[0:3], "ffn1_fwd", ride(_gather_ride(w[3:])) if ex else None)
    if ex:
        w[3:] = rest[0]
    wg1, wu1, wd1, win, wo, wg2, wu2, wd2 = w
    wo2 = wo.reshape(wo.shape[0] * wo.shape[1], wo.shape[2])
    xnm, u, *uas = _inproj_fwd(h1, g_mix, win)
    raw, mix_r = _ret_fwd(u, g_ret, tabs)
    branches = [_att_fwd(ua, dil) for ua, dil in zip(uas, DILATIONS)]
    mix_a, att, lse = _att_combine([b[0] for b in branches], [b[1] for b in branches], t)
    h2 = _outproj_fwd(h1, mix_r, mix_a, wo2)
    (dh3, xn2, *hid2, act2, loss_p, dg_fin), _ = _ffn_fwd(h2, g_ffn2, wg2, wu2, wd2, "ffn2_fwd", head=(g_fin, target))

    (dwd2,), _ = _ffn_wgrad_down(act2, dh3, "ffn2_wgrad_down")
    dwd2 = dwd2.reshape(wd2.shape)
    r_d2 = red(["ffn2_w_down"], [dwd2])
    (dh2, dga2, dua2, dg_ffn2), e = _ffn_bwd_data(dh3, h2, g_ffn2, *hid2, wg2, wu2, wd2, "ffn2_bwd",
                                                  ex and [r_d2.pair()])
    (dwg2, dwu2), e = _ffn_wgrad_gu(xn2, [dga2, dua2], "ffn2_wgrad_gu", ex and [r_d2.chips(e[0])])
    dwg2, dwu2 = dwg2.reshape(wg2.shape), dwu2.reshape(wu2.shape)
    r_gu2 = red(["ffn2_w_gate", "ffn2_w_up"], [dwg2, dwu2])
    (dmix_r, dmix_a), e = _outproj_bwd(dh2, wo2, ex and [r_gu2.pair(), _finish_ride(r_d2.halves(e[0]))])
    if ex:
        got_gu2, (dwd2,) = e
    hw = RET_WIDTH // (wo.shape[1])
    dwo = jnp.concatenate([_tn_matmul(mix_r, dh2, dh2.shape[1], "wo_grad_r").reshape(hw, wo.shape[1], wo.shape[2]),
                           _tn_matmul(mix_a, dh2, dh2.shape[1], "wo_grad_a").reshape(hw, wo.shape[1], wo.shape[2])])
    r_wo = red(["w_out"], [dwo])
    (dq_r, dgt_r, dret, dg_ret), e = _ret_bwd_q(dmix_r, raw, u, g_ret, tabs, ex and [r_gu2.chips(got_gu2)])
    (dk_r, dv_r), e = _ret_bwd_kv(dret, u, tabs, ex and [r_wo.pair(), _finish_ride(r_gu2.halves(e[0]))])
    if ex:
        got_wo, (dwg2, dwu2) = e
    prep = _att_bwd_prep(dmix_a, att, lse)
    p1, e = _att_bwd(uas[0], *prep[0], DILATIONS[0], ex and [r_wo.chips(got_wo)])
    p4, e = _att_bwd(uas[1], *prep[1], DILATIONS[1], ex and [_finish_ride(r_wo.halves(e[0]))])
    if ex:
        (dwo,), = e
    p16, _ = _att_bwd(uas[2], *prep[2], DILATIONS[2])
    dh1, du, dg_mix = _inproj_bwd([dq_r, dk_r, dv_r, dgt_r], [p1, p4, p16], h1, g_mix, dh2, win)
    dwin = _tn_matmul(xnm, du, win.shape[2], "win_grad")
    r_in = red(["w_in"], [dwin])
    (dwd1,), e = _ffn_wgrad_down(act1, dh1, "ffn1_wgrad_down", ex and [r_in.pair()])
    dwd1 = dwd1.reshape(wd1.shape)
    r_d1 = red(["ffn1_w_down"], [dwd1])
    (dx, dga1, dua1, dg_ffn1), e = _ffn_bwd_data(dh1, x, g_ffn1, *hid1, wg1, wu1, wd1, "ffn1_bwd",
                                                  ex and [r_in.chips(e[0]), r_d1.pair()])
    (dwg1,), e = _ffn_wgrad_gu(xn1, [dga1], "ffn1_wgrad_gate", ex and [_finish_ride(r_in.halves(e[0])), r_d1.chips(e[1])])
    dwg1 = dwg1.reshape(wg1.shape)
    if ex:
        (dwin,), oth_d1 = e
        r_g1 = red(["ffn1_w_gate"], [dwg1])
        got_g1 = _run(r_g1.pair(), "pair_exchange_ffn1_gate")
    (dwu1,), e = _ffn_wgrad_gu(xn1, [dua1], "ffn1_wgrad_up", ex and [r_g1.chips(got_g1)])
    dwu1 = dwu1.reshape(wu1.shape)
    gain_parts = [dg_ffn1, dg_mix, dg_ret, dg_ffn2, dg_fin]
    if not ex:
        return loss_p, dx, [dwg1, dwu1, dwd1, dwin, dwo, dwg2, dwu2, dwd2], gain_parts
    r_u1 = red(["ffn1_w_up"], [dwu1])
    got_u1 = _run(r_u1.pair(), "pair_exchange_ffn1_up")
    oth_u1 = _run(r_u1.chips(got_u1), "chip_exchange_ffn1_up")
    last = r_g1.halves(e[0]) + r_u1.halves(oth_u1) + r_d1.halves(oth_d1)
    dwg1, dwu1, dwd1, gall = _run(_finish_ride(last, _pack_gains(gain_parts, x.shape[1])), "finish_exchange_ffn1")
    return loss_p, dx, [dwg1, dwu1, dwd1, dwin, dwo, dwg2, dwu2, dwd2], gall


N_DEV = 8
GAIN_ROWS = 8


def _place():
    x, y, c = lax.axis_index("x"), lax.axis_index("y"), lax.axis_index("c")
    chips = [(1 - x, y), (x, 1 - y), (1 - x, 1 - y)]
    return x, y, c, chips


ROW_QUARTERS = 4


def _place_shards(place, ws):
    n = len(ws)

    def body(place_ref, *refs):
        for w_ref, o_ref in zip(refs[:n], refs[n:]):
            o_ref[...] = w_ref[...].astype(BF16)

    quarter = lambda w: (w.shape[0] // ROW_QUARTERS, w.shape[1])
    return pl.pallas_call(
        body, name="place_shards",
        grid_spec=pltpu.PrefetchScalarGridSpec(
            num_scalar_prefetch=1, grid=(ROW_QUARTERS,),
            in_specs=[pl.BlockSpec(quarter(w), lambda i, pr: (i, 0)) for w in ws],
            out_specs=[pl.BlockSpec((None,) + quarter(w), lambda i, pr: (pr[0], i, 0)) for w in ws]),
        out_shape=[jax.ShapeDtypeStruct((N_SHARD,) + w.shape, BF16) for w in ws],
        compiler_params=_params("arbitrary"),
    )(place, *ws)


def _gather_ride(bufs):
    na = len(bufs)

    def legs(outs, sems):
        send_sem, recv_sem, fsend_sem, frecv_sem = sems
        x, y, c, chips = _place()

        def half(a, idx, which):
            hr = outs[a].shape[1] // 2
            return outs[a].at[idx, pl.ds(which * hr, hr)]

        def ici(a, j, idx):
            px, py = chips[j]
            return pltpu.make_async_remote_copy(
                src_ref=half(a, idx, c), dst_ref=half(a, idx, c),
                send_sem=send_sem.at[a, j], recv_sem=recv_sem.at[a, j], device_id=(px, py, c), device_id_type=MESH)

        def d2d(a, j, idx, which):
            return pltpu.make_async_remote_copy(
                src_ref=half(a, idx, which), dst_ref=half(a, idx, which),
                send_sem=fsend_sem.at[a, j], recv_sem=frecv_sem.at[a, j], device_id=(x, y, 1 - c), device_id_type=MESH)

        return 2 * x + y, c, chips, ici, d2d

    def start(ins, outs, sems):
        me, _, _, ici, _ = legs(outs, sems)
        for a in range(na):
            for j in range(3):
                ici(a, j, me).start()

    def finish(ins, outs, sems):
        me, c, chips, ici, d2d = legs(outs, sems)
        passed = []
        for a in range(na):
            for j, (px, py) in enumerate(chips):
                ici(a, j, 2 * px + py).wait_recv()
                cp = d2d(a, j, 2 * px + py, c)
                cp.start()
                passed.append(cp)
        for a in range(na):
            for j, (px, py) in enumerate(chips):
                d2d(a, j, 2 * px + py, 1 - c).wait_recv()
        for a in range(na):
            for j in range(3):
                ici(a, j, me).wait_send()
        for cp in passed:
            cp.wait_send()

    return _Ride(bufs, [jax.ShapeDtypeStruct(b.shape, b.dtype) for b in bufs], [pltpu.SemaphoreType.DMA((na, 3))] * 4,
                 start, finish, {a: a for a in range(na)})


def _pair_ride(grads):
    na = len(grads)

    def copies(ins, outs, sems):
        send_sem, recv_sem = sems
        x, y, c, _ = _place()
        res = []
        for a in range(na):
            hr = ins[a].shape[1] // 2
            res.append(pltpu.make_async_remote_copy(
                src_ref=ins[a].at[:, pl.ds((1 - c) * hr, hr)], dst_ref=outs[a],
                send_sem=send_sem.at[a], recv_sem=recv_sem.at[a], device_id=(x, y, 1 - c), device_id_type=MESH))
        return res

    def start(ins, outs, sems):
        for cp in copies(ins, outs, sems):
            cp.start()

    def finish(ins, outs, sems):
        for cp in copies(ins, outs, sems):
            cp.wait()

    return _Ride(grads, [jax.ShapeDtypeStruct((g.shape[0], g.shape[1] // 2, g.shape[2]), g.dtype) for g in grads],
                 [pltpu.SemaphoreType.DMA((na,))] * 2, start, finish)


def _chip_ride(sums):
    na = len(sums)

    def copies(ins, outs, sems):
        send_sem, recv_sem = sems
        x, y, c, chips = _place()
        res = []
        for a in range(na):
            for j, (px, py) in enumerate(chips):
                res.append(pltpu.make_async_remote_copy(
                    src_ref=ins[a].at[2 * px + py], dst_ref=outs[a].at[j],
                    send_sem=send_sem.at[a, j], recv_sem=recv_sem.at[a, j], device_id=(px, py, c), device_id_type=MESH))
        return res

    def start(ins, outs, sems):
        for cp in copies(ins, outs, sems):
            cp.start()

    def finish(ins, outs, sems):
        for cp in copies(ins, outs, sems):
            cp.wait()

    return _Ride(sums, [jax.ShapeDtypeStruct((3,) + s.shape[1:], s.dtype) for s in sums],
                 [pltpu.SemaphoreType.DMA((na, 3))] * 2, start, finish)


def _finish_ride(grads, gpack=None):
    na = len(grads)

    def halves(outs, sems, which):
        x, y, c, _ = _place()
        res = []
        for a in range(na):
            hr = outs[a].shape[0] // 2
            rows = outs[a].at[pl.ds((c if which == "mine" else 1 - c) * hr, hr)]
            res.append(pltpu.make_async_remote_copy(
                src_ref=rows, dst_ref=rows, send_sem=sems[0].at[a], recv_sem=sems[1].at[a],
                device_id=(x, y, 1 - c), device_id_type=MESH))
        return res

    def gains(ins, outs, sems):
        x, y, c, _ = _place()
        dev = 4 * x + 2 * y + c
        g_in, g_out = ins[na], outs[na]
        own = pltpu.make_async_copy(g_in, g_out.at[dev], sems[2])
        sends, lands = [], []
        for k in range(N_DEV - 1):
            bx, by, bc = (k + 1) // 4, ((k + 1) // 2) % 2, (k + 1) % 2
            peer = (jnp.bitwise_xor(x, bx), jnp.bitwise_xor(y, by), jnp.bitwise_xor(c, bc))
            sends.append(pltpu.make_async_remote_copy(
                src_ref=g_in, dst_ref=g_out.at[dev], send_sem=sems[3].at[k], recv_sem=sems[4].at[k],
                device_id=peer, device_id_type=MESH))
            slot = g_out.at[jnp.bitwise_xor(dev, k + 1)]
            lands.append(pltpu.make_async_remote_copy(
                src_ref=slot, dst_ref=slot, send_sem=sems[3].at[k], recv_sem=sems[4].at[k],
                device_id=peer, device_id_type=MESH))
        return own, sends, lands

    def start(ins, outs, sems):
        for cp in halves(outs, sems, "mine"):
            cp.start()
        if gpack is not None:
            own, sends, _ = gains(ins, outs, sems)
            own.start()
            for cp in sends:
                cp.start()

    def finish(ins, outs, sems):
        for cp in halves(outs, sems, "sibling's"):
            cp.wait_recv()
        if gpack is not None:
            own, sends, lands = gains(ins, outs, sems)
            for cp in lands:
                cp.wait_recv()
            for cp in sends:
                cp.wait_send()
            own.wait()
        for cp in halves(outs, sems, "mine"):
            cp.wait_send()

    shapes = [jax.ShapeDtypeStruct(g.shape, g.dtype) for g in grads]
    sems = [pltpu.SemaphoreType.DMA((na,))] * 2
    if gpack is None:
        return _Ride(grads, shapes, sems, start, finish, {a: a for a in range(na)})
    return _Ride(list(grads) + [gpack], shapes + [jax.ShapeDtypeStruct((N_DEV,) + gpack.shape, gpack.dtype)],
                 sems + [pltpu.SemaphoreType.DMA, pltpu.SemaphoreType.DMA((N_DEV - 1,)), pltpu.SemaphoreType.DMA((N_DEV - 1,))],
                 start, finish, {a: a for a in range(na)})


def _pair_sum(place, grad, got, name):
    ns, r, cols = grad.shape
    hr = r // 2

    def body(place_ref, g_ref, r_ref, o_ref):
        o_ref[...] = (g_ref[...] + r_ref[...]).astype(BF16)

    return pl.pallas_call(
        body, name=name,
        grid_spec=pltpu.PrefetchScalarGridSpec(
            num_scalar_prefetch=1, grid=(ns,),
            in_specs=[pl.BlockSpec((None, hr, cols), lambda s, pr: (s, pr[1], 0)),
                      pl.BlockSpec((None, hr, cols), lambda s, pr: (s, 0, 0))],
            out_specs=pl.BlockSpec((None, hr, cols), lambda s, pr: (s, 0, 0))),
        out_shape=jax.ShapeDtypeStruct((ns, hr, cols), BF16),
        compiler_params=_params("arbitrary"),
    )(place, grad, got)


def _chip_sum(place, grad, got, others, name):
    ns, r, cols = grad.shape
    hr = r // 2
    nb = 2
    tr = hr // nb

    def body(place_ref, g_ref, r_ref, o3_ref, o_ref):
        acc = g_ref[...] + r_ref[...]
        for j in range(3):
            acc = acc + o3_ref[j].astype(F32)
        o_ref[...] = acc

    return pl.pallas_call(
        body, name=name,
        grid_spec=pltpu.PrefetchScalarGridSpec(
            num_scalar_prefetch=1, grid=(nb,),
            in_specs=[pl.BlockSpec((None, tr, cols), lambda i, pr: (pr[0], pr[1] * nb + i, 0)),
                      pl.BlockSpec((None, tr, cols), lambda i, pr: (pr[0], i, 0)),
                      pl.BlockSpec((3, tr, cols), lambda i, pr: (0, i, 0))],
            out_specs=pl.BlockSpec((tr, cols), lambda i, pr: (pr[1] * nb + i, 0))),
        out_shape=jax.ShapeDtypeStruct((r, cols), F32),
        compiler_params=_params("arbitrary"),
    )(place, grad, got, others)


def _pack_gains(parts, d):
    def body(*refs):
        ins, o_ref = refs[:-1], refs[-1]
        o_ref[...] = jnp.zeros_like(o_ref)
        for k, r in enumerate(ins):
            o_ref[k:k + 1, 0:r.shape[1]] = jnp.sum(r[...], axis=0, keepdims=True)

    return pl.pallas_call(
        body, name="pack_gains", out_shape=jax.ShapeDtypeStruct((GAIN_ROWS, d), F32),
    )(*parts)


def _adamw_math(w, g, m, v):
    m = ADAM_B1 * m + (1.0 - ADAM_B1) * g
    v = ADAM_B2 * v + (1.0 - ADAM_B2) * jnp.square(g)
    m_hat = m / (1.0 - ADAM_B1 ** ADAM_STEP)
    v_hat = v / (1.0 - ADAM_B2 ** ADAM_STEP)
    return -ADAM_LR * (m_hat / (jnp.sqrt(v_hat) + ADAM_EPS) + ADAM_WD * w), m, v


def _adamw(ws, gs, ms, vs):
    n = len(ws)

    def body(*refs):
        ins, outs = refs[:4 * n], refs[4 * n:]
        for k in range(n):
            w_ref, g_ref, m_ref, v_ref = ins[4 * k:4 * k + 4]
            go_ref, d_ref, nm_ref, nv_ref = outs[4 * k:4 * k + 4]
            g = g_ref[...]
            go_ref[...] = g
            d_ref[...], nm_ref[...], nv_ref[...] = _adamw_math(w_ref[...], g, m_ref[...], v_ref[...])

    parts = 2 * ROW_QUARTERS
    tile = lambda w: pl.BlockSpec((w.shape[0] // parts, w.shape[1]), lambda i: (i, 0))
    res = pl.pallas_call(
        body, name="adamw_shards", grid=(parts,),
        in_specs=[tile(w) for w in ws for _ in range(4)], out_specs=[tile(w) for w in ws for _ in range(4)],
        out_shape=[jax.ShapeDtypeStruct(w.shape, F32) for w in ws for _ in range(4)],
        compiler_params=_params("arbitrary"),
    )(*[a for quad in zip(ws, gs, ms, vs) for a in quad])
    return [res[4 * k:4 * k + 4] for k in range(n)]


def _adamw_gain(gall, row, w, m, v, name):
    n = w.shape[1]

    def body(ga_ref, w_ref, m_ref, v_ref, g_ref, d_ref, nm_ref, nv_ref):
        g = ga_ref[0, row:row + 1, 0:n]
        for k in range(1, N_DEV):
            g = g + ga_ref[k, row:row + 1, 0:n]
        g_ref[...] = g
        d_ref[...], nm_ref[...], nv_ref[...] = _adamw_math(w_ref[...], g, m_ref[...], v_ref[...])

    return pl.pallas_call(
        body, name=name, out_shape=[jax.ShapeDtypeStruct((1, n), F32)] * 4,
    )(gall, w, m, v)


def kernel(x, norm_ffn1, ffn1_w_gate, ffn1_w_up, ffn1_w_down, norm_mix, w_in, ret_norm_gain, w_out, norm_ffn2, ffn2_w_gate, ffn2_w_up, ffn2_w_down, norm_final, loss_target, m_norm_ffn1, m_ffn1_w_gate, m_ffn1_w_up, m_ffn1_w_down, m_norm_mix, m_w_in, m_ret_norm_gain, m_w_out, m_norm_ffn2, m_ffn2_w_gate, m_ffn2_w_up, m_ffn2_w_down, m_norm_final, v_norm_ffn1, v_ffn1_w_gate, v_ffn1_w_up, v_ffn1_w_down, v_norm_mix, v_w_in, v_ret_norm_gain, v_w_out, v_norm_ffn2, v_ffn2_w_gate, v_ffn2_w_up, v_ffn2_w_down, v_norm_final):
    d = x.shape[-1]
    mats = [ffn1_w_gate, ffn1_w_up, ffn1_w_down, w_in, w_out, ffn2_w_gate, ffn2_w_up, ffn2_w_down]
    mats_m = [m_ffn1_w_gate, m_ffn1_w_up, m_ffn1_w_down, m_w_in, m_w_out, m_ffn2_w_gate, m_ffn2_w_up, m_ffn2_w_down]
    mats_v = [v_ffn1_w_gate, v_ffn1_w_up, v_ffn1_w_down, v_w_in, v_w_out, v_ffn2_w_gate, v_ffn2_w_up, v_ffn2_w_down]
    mat_names = ["ffn1_w_gate", "ffn1_w_up", "ffn1_w_down", "w_in", "w_out", "ffn2_w_gate", "ffn2_w_up", "ffn2_w_down"]
    gains = [norm_ffn1, norm_mix, ret_norm_gain, norm_ffn2, norm_final.reshape(1, d)]
    gains_m = [m_norm_ffn1, m_norm_mix, m_ret_norm_gain, m_norm_ffn2, m_norm_final.reshape(1, d)]
    gains_v = [v_norm_ffn1, v_norm_mix, v_ret_norm_gain, v_norm_ffn2, v_norm_final.reshape(1, d)]
    gain_names = ["norm_ffn1", "norm_mix", "ret_norm_gain", "norm_ffn2", "norm_final"]

    turned = lambda n: n.endswith(("w_gate", "w_up"))
    local = lambda a, n: jnp.swapaxes(a, 1, 2)[0] if turned(n) else a[0]
    back = lambda a, n: jnp.swapaxes(a[None], 1, 2) if turned(n) else a[None]
    shards = [local(w, n) for w, n in zip(mats, mat_names)]
    place = jnp.stack([2 * lax.axis_index("x") + lax.axis_index("y"), lax.axis_index("c")]).astype(jnp.int32)
    placed = _place_shards(place, shards)
    loss_p, dx, shard_grads, gall = _step(x[0], loss_target[0], gains, placed, place)

    out_g, out_d, out_m, out_v = {}, {}, {}, {}
    updates = _adamw(shards, shard_grads, [local(m, n) for m, n in zip(mats_m, mat_names)],
                     [local(v, n) for v, n in zip(mats_v, mat_names)])
    for n, quad in zip(mat_names, updates):
        out_g[n], out_d[n], out_m[n], out_v[n] = [back(a, n) for a in quad]
    for row, (n, w, m, v) in enumerate(zip(gain_names, gains, gains_m, gains_v)):
        res = _adamw_gain(gall, row, w, m, v, f"adamw_{n}")
        shape = (d,) if n == "norm_final" else w.shape
        out_g[n], out_d[n], out_m[n], out_v[n] = [r.reshape(shape) for r in res]

    loss = lax.psum(jnp.sum(loss_p), ("x", "y", "c"))
    order = ["norm_ffn1", "ffn1_w_gate", "ffn1_w_up", "ffn1_w_down", "norm_mix", "w_in", "ret_norm_gain", "w_out",
             "norm_ffn2", "ffn2_w_gate", "ffn2_w_up", "ffn2_w_down", "norm_final"]
    return (loss, dx[None], *[out_g[n] for n in order], *[out_d[n] for n in order],
            *[out_m[n] for n in order], *[out_v[n] for n in order])
```

```python
import functools

import jax
import jax.numpy as jnp
from jax import lax
from jax.experimental import pallas as pl
from jax.experimental.pallas import tpu as pltpu

F32 = jnp.float32
BF16 = jnp.bfloat16
MESH = pl.DeviceIdType.MESH

NORM_EPS = 1e-6
GN_EPS = 1e-6
ROPE_BASE = 10000.0
RET_HEADS = 4
RET_DIM = 128
RET_WIDTH = 512
RET_CHUNK = 128
ATT_DIM = 64
ATT_WIDTH = 512
ATT_BLOCK = 128
DILATIONS = (1, 4, 16)
LANE = 128
N_SHARD = 4
ADAM_LR, ADAM_B1, ADAM_B2, ADAM_EPS, ADAM_WD, ADAM_STEP = 0.001, 0.9, 0.999, 1e-08, 0.01, 10

V7X_VMEM_BYTES = 64 * 1024 * 1024
VMEM_LIMIT = V7X_VMEM_BYTES - 8 * 1024 * 1024

NT = (((1,), (1,)), ((), ()))
TN = (((0,), (0,)), ((), ()))


def _params(*sem):
    return pltpu.CompilerParams(dimension_semantics=sem, vmem_limit_bytes=VMEM_LIMIT)


def _dot(a, b, dims=None):
    if dims is None:
        return jnp.dot(a, b, preferred_element_type=F32)
    return lax.dot_general(a, b, dims, preferred_element_type=F32)


def _sigmoid(x):
    return 1.0 / (1.0 + jnp.exp(-x))


def _load_weights(pairs, sems):
    copies = [pltpu.make_async_copy(src, dst, sems.at[k]) for k, (src, dst) in enumerate(pairs)]
    for cp in copies:
        cp.start()
    for cp in copies:
        cp.wait()


def _rows8(v):
    r, c = v.shape
    return v.reshape(r // 8, 8, c).sum(axis=0)


class _Ride:
    def __init__(self, inputs, out_shapes, sems, start, finish, aliases=None):
        self.inputs, self.out_shapes, self.sems = list(inputs), list(out_shapes), list(sems)
        self.start, self.finish, self.aliases = start, finish, dict(aliases or {})


def _pallas(body, rides, *, name, in_specs, out_specs, out_shape, args, grid=(), scratch_shapes=(), sem=()):
    rides = [r for r in (rides or []) if r is not None]
    n_in, n_out, n_scr = len(args), len(out_shape), len(scratch_shapes)
    hbm = pl.BlockSpec(memory_space=pl.ANY)
    r_in = [a for r in rides for a in r.inputs]
    r_out = [s for r in rides for s in r.out_shapes]
    r_sem = [s for r in rides for s in r.sems]
    aliases, spans, ki, ko, ks = {}, [], 0, 0, 0
    for r in rides:
        aliases.update({n_in + ki + i: n_out + ko + o for i, o in r.aliases.items()})
        spans.append((ki, ko, ks))
        ki, ko, ks = ki + len(r.inputs), ko + len(r.out_shapes), ks + len(r.sems)

    def wrapped(*refs):
        ins, rin = refs[:n_in], refs[n_in:n_in + len(r_in)]
        o0 = n_in + len(r_in)
        outs, rout = refs[o0:o0 + n_out], refs[o0 + n_out:o0 + n_out + len(r_out)]
        s0 = o0 + n_out + len(r_out)
        scr, rsem = refs[s0:s0 + n_scr], refs[s0 + n_scr:]
        part = lambda r, k: (rin[spans[k][0]:spans[k][0] + len(r.inputs)], rout[spans[k][1]:spans[k][1] + len(r.out_shapes)],
                             rsem[spans[k][2]:spans[k][2] + len(r.sems)])
        first = functools.reduce(jnp.logical_and, [pl.program_id(k) == 0 for k in range(len(grid))], True)
        last = functools.reduce(jnp.logical_and, [pl.program_id(k) == grid[k] - 1 for k in range(len(grid))], True)
        if rides:
            @pl.when(first)
            def _():
                for k, r in enumerate(rides):
                    r.start(*part(r, k))

        body(*ins, *outs, *scr)
        if rides:
            @pl.when(last)
            def _():
                for k, r in enumerate(rides):
                    r.finish(*part(r, k))

    res = pl.pallas_call(
        wrapped, name=name, grid=grid,
        in_specs=list(in_specs) + [hbm] * len(r_in), out_specs=list(out_specs) + [hbm] * len(r_out),
        out_shape=list(out_shape) + r_out, input_output_aliases=aliases,
        scratch_shapes=list(scratch_shapes) + r_sem,
        compiler_params=pltpu.CompilerParams(dimension_semantics=sem, vmem_limit_bytes=VMEM_LIMIT) if grid else None,
    )(*args, *r_in)
    extras = [list(res[n_out + ko:n_out + ko + len(r.out_shapes)]) for r, (_, ko, _) in zip(rides, spans)]
    return list(res[:n_out]), extras


def _run(ride, name):
    def body(*refs):
        n_in, n_out = len(ride.inputs), len(ride.out_shapes)
        parts = refs[:n_in], refs[n_in:n_in + n_out], refs[n_in + n_out:]
        ride.start(*parts)
        ride.finish(*parts)

    hbm = pl.BlockSpec(memory_space=pl.ANY)
    return list(pl.pallas_call(
        body, name=name, in_specs=[hbm] * len(ride.inputs), out_specs=[hbm] * len(ride.out_shapes),
        out_shape=ride.out_shapes, input_output_aliases=ride.aliases, scratch_shapes=ride.sems,
    )(*ride.inputs))


def _loss_head(hv, gain_ref, tg_ref, loss_ref, dgain_ref):
    d = hv.shape[1]
    r = lax.rsqrt(jnp.mean(hv * hv, axis=-1, keepdims=True) + NORM_EPS)
    xh = hv * r
    err = xh * gain_ref[...] - tg_ref[...]
    sq = _rows8(jnp.square(err))
    loss_ref[...] += 0.5 * functools.reduce(jnp.add, [sq[:, k * LANE:(k + 1) * LANE] for k in range(d // LANE)]) / d
    dy = err / d
    dgain_ref[...] += _rows8(dy * xh)
    dxh = dy * gain_ref[...]
    return r * (dxh - xh * jnp.mean(dxh * xh, axis=-1, keepdims=True))


V7X_MXU_TILE = 256
FFN_CHUNK_TILES = 3


def _hidden_chunks(f):
    step = FFN_CHUNK_TILES * V7X_MXU_TILE
    return [slice(s, min(s + step, f)) for s in range(0, f, step)]


def _flat(w):
    return w.reshape(w.shape[0] * w.shape[1], w.shape[2])


def _ffn_fwd(x, gain, wg, wu, wd, name, rides=None, head=None):
    t, d = x.shape
    wg, wu, wd = _flat(wg), _flat(wu), _flat(wd)
    f = wg.shape[0]
    tm = min(512, t)
    nh = 0 if head is None else 2

    def body(*refs):
        x_ref, gain_ref = refs[:2]
        wg_hbm, wu_hbm, wd_hbm, h_ref, xn_ref, g_ref, u_ref, a_ref = refs[2 + nh:10 + nh]
        sums = refs[10 + nh:12 + nh]
        wg_v, wu_v, wd_v, sems = refs[-4:]

        @pl.when(pl.program_id(0) == 0)
        def _():
            _load_weights([(wg_hbm, wg_v), (wu_hbm, wu_v), (wd_hbm, wd_v)], sems)
            if head is not None:
                for s_ref in sums:
                    s_ref[...] = jnp.zeros_like(s_ref)

        xv = x_ref[...]
        r = lax.rsqrt(jnp.mean(xv * xv, axis=-1, keepdims=True) + NORM_EPS)
        xn = (xv * r * gain_ref[...]).astype(BF16)
        xn_ref[...] = xn
        acc = jnp.zeros((tm, d), F32)
        for c in _hidden_chunks(f):
            g = _dot(xn, wg_v[c, :], NT)
            u = _dot(xn, wu_v[c, :], NT)
            g_ref[:, c] = g.astype(BF16)
            u_ref[:, c] = u.astype(BF16)
            a = (g * _sigmoid(g) * u).astype(BF16)
            a_ref[:, c] = a
            acc = acc + _dot(a, wd_v[c, :])
        hv = xv + 0.5 * acc
        h_ref[...] = hv if head is None else _loss_head(hv, refs[2], refs[3], *sums)

    hbm = pl.BlockSpec(memory_space=pl.ANY)
    hid = pl.BlockSpec((tm, f), lambda i: (i, 0))
    tile = pl.BlockSpec((tm, d), lambda i: (i, 0))
    row = pl.BlockSpec((1, d), lambda i: (0, 0))
    sums = [] if head is None else [(pl.BlockSpec((8, LANE), lambda i: (0, 0)), jax.ShapeDtypeStruct((8, LANE), F32)),
                                    (pl.BlockSpec((8, d), lambda i: (0, 0)), jax.ShapeDtypeStruct((8, d), F32))]
    return _pallas(
        body, rides, name=name, grid=(t // tm,),
        in_specs=[tile, row] + ([] if head is None else [row, tile]) + [hbm, hbm, hbm],
        out_specs=[tile, tile, hid, hid, hid] + [s for s, _ in sums],
        out_shape=[jax.ShapeDtypeStruct((t, d), F32), jax.ShapeDtypeStruct((t, d), BF16)]
        + [jax.ShapeDtypeStruct((t, f), BF16)] * 3 + [s for _, s in sums],
        scratch_shapes=[pltpu.VMEM(wg.shape, BF16), pltpu.VMEM(wu.shape, BF16), pltpu.VMEM(wd.shape, BF16),
                        pltpu.SemaphoreType.DMA((3,))],
        sem=("arbitrary",), args=[x, gain] + ([] if head is None else list(head)) + [wg, wu, wd])


def _ffn_bwd_data(dy, x, gain, g, u, wg, wu, wd, name, rides=None):
    t, d = x.shape
    wg, wu, wd = _flat(wg), _flat(wu), _flat(wd)
    f = wg.shape[0]
    tm = min(256, t)

    def body(dy_ref, x_ref, gain_ref, g_ref, u_ref, wg_hbm, wu_hbm, wd_hbm, dx_ref, dg_ref, du_ref, dgain_ref,
             wg_v, wu_v, wd_v, sems):
        @pl.when(pl.program_id(0) == 0)
        def _():
            _load_weights([(wg_hbm, wg_v), (wu_hbm, wu_v), (wd_hbm, wd_v)], sems)
            dgain_ref[...] = jnp.zeros_like(dgain_ref)

        dyv = dy_ref[...]
        dyh = (0.5 * dyv).astype(BF16)
        dxn = jnp.zeros((tm, d), F32)
        chunks = _hidden_chunks(f)
        das = [_dot(dyh, wd_v[c, :], NT) for c in chunks]
        for c, da in zip(chunks, das):
            gj = g_ref[:, c].astype(F32)
            uj = u_ref[:, c].astype(F32)
            sig = _sigmoid(gj)
            dgj = (da * uj * (sig * (1.0 + gj * (1.0 - sig)))).astype(BF16)
            duj = (da * (gj * sig)).astype(BF16)
            dg_ref[:, c] = dgj
            du_ref[:, c] = duj
            dxn = dxn + _dot(dgj, wg_v[c, :]) + _dot(duj, wu_v[c, :])
        xv = x_ref[...]
        r = lax.rsqrt(jnp.mean(xv * xv, axis=-1, keepdims=True) + NORM_EPS)
        xh = xv * r
        dgain_ref[...] += _rows8(dxn * xh)
        dxh = dxn * gain_ref[...]
        dx_ref[...] = dyv + r * (dxh - xh * jnp.mean(dxh * xh, axis=-1, keepdims=True))

    hbm = pl.BlockSpec(memory_space=pl.ANY)
    tile = pl.BlockSpec((tm, d), lambda i: (i, 0))
    hid = pl.BlockSpec((tm, f), lambda i: (i, 0))
    return _pallas(
        body, rides, name=name, grid=(t // tm,),
        in_specs=[tile, tile, pl.BlockSpec((1, d), lambda i: (0, 0)), hid, hid, hbm, hbm, hbm],
        out_specs=[tile, hid, hid, pl.BlockSpec((8, d), lambda i: (0, 0))],
        out_shape=[jax.ShapeDtypeStruct((t, d), F32), jax.ShapeDtypeStruct((t, f), BF16),
                   jax.ShapeDtypeStruct((t, f), BF16), jax.ShapeDtypeStruct((8, d), F32)],
        scratch_shapes=[pltpu.VMEM(wg.shape, BF16), pltpu.VMEM(wu.shape, BF16), pltpu.VMEM(wd.shape, BF16),
                        pltpu.SemaphoreType.DMA((3,))],
        sem=("arbitrary",), args=[dy, x, gain, g, u, wg, wu, wd])


WGRAD_ROW_BLOCKS = 2


def _ffn_wgrad_down(a, dy, name, rides=None):
    t, d = dy.shape
    f = a.shape[1]
    fb = f // WGRAD_ROW_BLOCKS
    tk = min(1024, t)

    def body(dy_ref, a_ref, dwd_ref):
        @pl.when(pl.program_id(1) == 0)
        def _():
            dwd_ref[...] = jnp.zeros_like(dwd_ref)

        dwd_ref[...] += _dot(a_ref[...], (0.5 * dy_ref[...]).astype(BF16), TN)

    return _pallas(
        body, rides, name=name, grid=(WGRAD_ROW_BLOCKS, t // tk),
        in_specs=[pl.BlockSpec((tk, d), lambda j, k: (k, 0)), pl.BlockSpec((tk, fb), lambda j, k: (k, j))],
        out_specs=[pl.BlockSpec((fb, d), lambda j, k: (j, 0))],
        out_shape=[jax.ShapeDtypeStruct((f, d), F32)],
        sem=("arbitrary", "arbitrary"), args=[dy, a])


def _ffn_wgrad_gu(xn, dhs, name, rides=None):
    t, d = xn.shape
    n = len(dhs)
    f = dhs[0].shape[1]
    fb = f // WGRAD_ROW_BLOCKS
    tk = min(2048 // n, t)

    def body(xn_ref, *refs):
        @pl.when(pl.program_id(1) == 0)
        def _():
            for o_ref in refs[n:]:
                o_ref[...] = jnp.zeros_like(o_ref)

        xnv = xn_ref[...]
        for dh_ref, o_ref in zip(refs[:n], refs[n:]):
            o_ref[...] += _dot(dh_ref[...], xnv, TN)

    hid = pl.BlockSpec((tk, fb), lambda j, k: (k, j))
    out = pl.BlockSpec((fb, d), lambda j, k: (j, 0))
    return _pallas(
        body, rides, name=name, grid=(WGRAD_ROW_BLOCKS, t // tk),
        in_specs=[pl.BlockSpec((tk, d), lambda j, k: (k, 0))] + [hid] * n,
        out_specs=[out] * n, out_shape=[jax.ShapeDtypeStruct((f, d), F32)] * n,
        sem=("arbitrary", "arbitrary"), args=[xn] + list(dhs))


def _tn_matmul(a, b, bn, name):
    t, m = a.shape
    n = b.shape[1]
    tk = min(2048, t)

    def body(a_ref, b_ref, o_ref):
        @pl.when(pl.program_id(1) == 0)
        def _():
            o_ref[...] = jnp.zeros_like(o_ref)

        o_ref[...] += _dot(a_ref[...].astype(BF16), b_ref[...].astype(BF16), TN)

    return pl.pallas_call(
        body, name=name, grid=(n // bn, t // tk),
        in_specs=[pl.BlockSpec((tk, m), lambda j, k: (k, 0)), pl.BlockSpec((tk, bn), lambda j, k: (k, j))],
        out_specs=pl.BlockSpec((None, m, bn), lambda j, k: (j, 0, 0)),
        out_shape=jax.ShapeDtypeStruct((n // bn, m, bn), F32),
        compiler_params=_params("arbitrary", "arbitrary"),
    )(a, b)


def _chunk_scratch(tm, w):
    return pltpu.VMEM((w // LANE, tm, LANE), F32)


def _regroup_store(cbuf, out_ref, dil):
    n = out_ref.shape[1]
    for g in range(dil):
        for k in range(cbuf.shape[0]):
            rows = cbuf[k] if dil == 1 else cbuf[k, pl.ds(g, n, stride=dil), :]
            out_ref[g, :, k * LANE:(k + 1) * LANE] = rows.astype(out_ref.dtype)


def _natural_rows(ref, dil, cbuf):
    if dil == 1:
        return ref[0].astype(F32)
    n = ref.shape[1]
    for g in range(dil):
        for k in range(cbuf.shape[0]):
            cbuf[k, pl.ds(g, n, stride=dil), :] = ref[g, :, k * LANE:(k + 1) * LANE].astype(F32)
    return jnp.concatenate([cbuf[k] for k in range(cbuf.shape[0])], axis=1)


def _inproj_fwd(h, gain, win):
    t, d = h.shape
    ns, _, cs = win.shape
    tm = min(512, t)
    rw, aw = 4 * RET_WIDTH, 3 * ATT_WIDTH

    def body(h_ref, gain_ref, w_ref, xn_ref, ur_ref, *rest):
        a_refs, abuf = rest[:-1], rest[-1]
        hv = h_ref[...]
        r = lax.rsqrt(jnp.mean(hv * hv, axis=-1, keepdims=True) + NORM_EPS)
        xn = (hv * r * gain_ref[...]).astype(BF16)
        xn_ref[...] = xn
        for j in range(ns):
            res = _dot(xn, w_ref[j])
            for k in range(cs // LANE):
                chunk = j * (cs // LANE) + k
                piece = res[:, k * LANE:(k + 1) * LANE]
                if chunk < rw // LANE:
                    ur_ref[:, chunk * LANE:(chunk + 1) * LANE] = piece
                else:
                    abuf[chunk - rw // LANE] = piece
        for dil, a_ref in zip(DILATIONS, a_refs):
            _regroup_store(abuf, a_ref, dil)

    return pl.pallas_call(
        body, name="inproj_fwd", grid=(t // tm,),
        in_specs=[pl.BlockSpec((tm, d), lambda i: (i, 0)), pl.BlockSpec((1, d), lambda i: (0, 0)),
                  pl.BlockSpec(win.shape, lambda i: (0, 0, 0))],
        out_specs=[pl.BlockSpec((tm, d), lambda i: (i, 0)), pl.BlockSpec((tm, rw), lambda i: (i, 0))]
        + [pl.BlockSpec((dil, tm // dil, aw), lambda i: (0, i, 0)) for dil in DILATIONS],
        out_shape=[jax.ShapeDtypeStruct((t, d), BF16), jax.ShapeDtypeStruct((t, rw), F32)]
        + [jax.ShapeDtypeStruct((dil, t // dil, aw), BF16) for dil in DILATIONS],
        scratch_shapes=[_chunk_scratch(tm, aw)],
        compiler_params=_params("arbitrary"),
    )(h, gain, win)


def _inproj_bwd(pieces, parts, h, gain, dres, win):
    t, d = h.shape
    ns, _, cs = win.shape
    pw = pieces[0].shape[1]
    tm = min(512, t)
    npc, nk = len(pieces), len(parts[0])
    flat_parts = [a for p in parts for a in p]

    def body(*refs):
        p_refs, a_refs = refs[:npc], refs[npc:npc + len(flat_parts)]
        h_ref, gain_ref, dres_ref, w_ref, dh_ref, du_ref, dgain_ref, buf = refs[npc + len(flat_parts):]

        @pl.when(pl.program_id(0) == 0)
        def _():
            dgain_ref[...] = jnp.zeros_like(dgain_ref)

        for k in range(npc):
            du_ref[:, k * pw:(k + 1) * pw] = p_refs[k][...]
        for k in range(nk):
            acc = None
            for b, dil in enumerate(DILATIONS):
                rows = _natural_rows(a_refs[b * nk + k], dil, buf)
                acc = rows if acc is None else acc + rows
            du_ref[:, (npc + k) * pw:(npc + k + 1) * pw] = acc.astype(BF16)
        dxn = jnp.zeros((tm, d), F32)
        for j in range(ns):
            dxn = dxn + _dot(du_ref[:, j * cs:(j + 1) * cs], w_ref[j], NT)
        hv = h_ref[...]
        r = lax.rsqrt(jnp.mean(hv * hv, axis=-1, keepdims=True) + NORM_EPS)
        xh = hv * r
        dgain_ref[...] += _rows8(dxn * xh)
        dxh = dxn * gain_ref[...]
        dh_ref[...] = dres_ref[...] + r * (dxh - xh * jnp.mean(dxh * xh, axis=-1, keepdims=True))

    tile = pl.BlockSpec((tm, d), lambda i: (i, 0))
    cols = (npc + nk) * pw
    return pl.pallas_call(
        body, name="inproj_bwd", grid=(t // tm,),
        in_specs=[pl.BlockSpec((tm, pw), lambda i: (i, 0))] * npc
        + [_regrouped_spec(tm, dil, pw) for dil in DILATIONS for _ in range(nk)]
        + [tile, pl.BlockSpec((1, d), lambda i: (0, 0)), tile, pl.BlockSpec(win.shape, lambda i: (0, 0, 0))],
        out_specs=[tile, pl.BlockSpec((tm, cols), lambda i: (i, 0)), pl.BlockSpec((8, d), lambda i: (0, 0))],
        out_shape=[jax.ShapeDtypeStruct((t, d), F32), jax.ShapeDtypeStruct((t, cols), BF16),
                   jax.ShapeDtypeStruct((8, d), F32)],
        scratch_shapes=[_chunk_scratch(tm, pw)],
        compiler_params=_params("arbitrary"),
    )(*pieces, *flat_parts, h, gain, dres, win)


def _outproj_fwd(h, mix_r, mix_a, wo):
    t, d = h.shape
    hw = mix_r.shape[1]
    tm = min(512, t)

    def body(h_ref, mr_ref, ma_ref, w_ref, o_ref):
        o_ref[...] = h_ref[...] + _dot(mr_ref[...], w_ref[0:hw, :]) + _dot(ma_ref[...], w_ref[hw:2 * hw, :])

    tile = pl.BlockSpec((tm, d), lambda i: (i, 0))
    half = pl.BlockSpec((tm, hw), lambda i: (i, 0))
    return pl.pallas_call(
        body, name="outproj_fwd", grid=(t // tm,),
        in_specs=[tile, half, half, pl.BlockSpec(wo.shape, lambda i: (0, 0))],
        out_specs=tile, out_shape=jax.ShapeDtypeStruct((t, d), F32),
        compiler_params=_params("arbitrary"),
    )(h, mix_r, mix_a, wo)


def _outproj_bwd(dh, wo, rides=None):
    t, d = dh.shape
    hw = wo.shape[0] // 2
    tm = min(512, t)

    def body(dh_ref, w_ref, dr_ref, da_ref):
        dhb = dh_ref[...].astype(BF16)
        dr_ref[...] = _dot(dhb, w_ref[0:hw, :], NT)
        da_ref[...] = _dot(dhb, w_ref[hw:2 * hw, :], NT)

    half = pl.BlockSpec((tm, hw), lambda i: (i, 0))
    return _pallas(
        body, rides, name="outproj_bwd", grid=(t // tm,),
        in_specs=[pl.BlockSpec((tm, d), lambda i: (i, 0)), pl.BlockSpec(wo.shape, lambda i: (0, 0))],
        out_specs=[half, half],
        out_shape=[jax.ShapeDtypeStruct((t, hw), F32), jax.ShapeDtypeStruct((t, hw), F32)],
        sem=("arbitrary",), args=[dh, wo])


def _retention_tables(t):
    pos = jnp.arange(t, dtype=F32)
    pair = (jnp.arange(RET_DIM) // 2 * 2).astype(F32)
    ang = pos[:, None] * (ROPE_BASE ** (-pair / RET_DIM))[None, :]
    c = RET_CHUNK
    log_g = jnp.log(1.0 - 2.0 ** (-5.0 - jnp.arange(RET_HEADS, dtype=F32)))
    idx = jnp.arange(c, dtype=F32)
    rel = idx[:, None] - idx[None, :]
    decay = jnp.where(rel >= 0, jnp.exp(log_g[:, None, None] * jnp.maximum(rel, 0.0)), 0.0)
    zeta = jnp.exp(log_g[:, None] * (c - 1 - idx)[None, :])
    xi = jnp.exp(log_g[:, None] * (idx + 1)[None, :])
    gc = jnp.exp(log_g * c)
    wide = lambda v: jnp.broadcast_to(v[:, :, None], (RET_HEADS, c, LANE))
    return (jnp.cos(ang), jnp.sin(ang), decay, wide(zeta), wide(xi),
            jnp.broadcast_to(gc[:, None, None], (RET_HEADS, c, LANE)))


def _rot(v):
    lane = lax.broadcasted_iota(jnp.int32, v.shape, 1)
    nxt = pltpu.roll(v, LANE - 1, 1)
    prv = pltpu.roll(v, 1, 1)
    return jnp.where(lane % 2 == 0, -nxt, prv)


def _ret_specs(tr, rev, nt):
    ti = (lambda i: nt - 1 - i) if rev else (lambda i: i)
    col = lambda blk: pl.BlockSpec((tr, RET_WIDTH), lambda i: (ti(i), blk))
    tab = pl.BlockSpec((tr, LANE), lambda i: (ti(i), 0))
    head = pl.BlockSpec((RET_HEADS, RET_CHUNK, LANE), lambda i: (0, 0, 0))
    return col, tab, head


def _ret_chunks(tr, rev=False):
    order = list(range(tr // RET_CHUNK))
    return [(pl.ds(ci * RET_CHUNK, RET_CHUNK), slice(h * RET_DIM, (h + 1) * RET_DIM), h)
            for h in range(RET_HEADS) for ci in (reversed(order) if rev else order)]


def _ret_operands(items, q_ref, k_ref, v_ref, cos_ref, sin_ref, zeta_ref):
    scale = RET_DIM ** -0.5
    qbs, kbs, vbs, kzs = [], [], [], []
    for sl, hs, h in items:
        cs, sn = cos_ref[sl, :], sin_ref[sl, :]
        q, k = q_ref[sl, hs], k_ref[sl, hs]
        kr = (k * cs + _rot(k) * sn) * scale
        qbs.append((q * cs + _rot(q) * sn).astype(BF16))
        kbs.append(kr.astype(BF16))
        vbs.append(v_ref[sl, hs].astype(BF16))
        kzs.append((kr * zeta_ref[h]).astype(BF16))
    return qbs, kbs, vbs, kzs


def _ret_states(items, state, steps, gc_ref):
    cur, befores = {}, []
    for (sl, hs, h), step in zip(items, steps):
        st = cur[h] if h in cur else state[h]
        befores.append(st)
        cur[h] = st * gc_ref[h] + step
    for h, st in cur.items():
        state[h] = st
    return befores


def _ret_fwd(u, gain, tabs):
    t = u.shape[0]
    tr = min(512, t)
    nt = t // tr
    cos, sin, decay, zeta, xi, gc = tabs

    def body(q_ref, k_ref, v_ref, gt_ref, cos_ref, sin_ref, gain_ref, dec_ref, zeta_ref, xi_ref, gc_ref,
             raw_ref, mix_ref, state):
        @pl.when(pl.program_id(0) == 0)
        def _():
            state[...] = jnp.zeros_like(state)

        items = _ret_chunks(tr)
        n = range(len(items))
        qbs, kbs, vbs, kzs = _ret_operands(items, q_ref, k_ref, v_ref, cos_ref, sin_ref, zeta_ref)
        ss = [_dot(qbs[i], kbs[i], NT) for i in n]
        kvs = [_dot(kzs[i], vbs[i], TN) for i in n]
        befores = _ret_states(items, state, kvs, gc_ref)
        intra = [_dot((ss[i] * dec_ref[items[i][2]]).astype(BF16), vbs[i]) for i in n]
        inter = [_dot(qbs[i], befores[i].astype(BF16)) for i in n]
        for i, (sl, hs, h) in enumerate(items):
            o = intra[i] + inter[i] * xi_ref[h]
            raw_ref[sl, hs] = o
            mu = jnp.mean(o, axis=-1, keepdims=True)
            var = jnp.mean(jnp.square(o - mu), axis=-1, keepdims=True)
            y = (o - mu) * lax.rsqrt(var + GN_EPS) * gain_ref[:, hs]
            gt = gt_ref[sl, hs]
            mix_ref[sl, hs] = (y * (gt * _sigmoid(gt))).astype(BF16)

    col, tab, head = _ret_specs(tr, False, nt)
    out = pl.BlockSpec((tr, RET_WIDTH), lambda i: (i, 0))
    return pl.pallas_call(
        body, name="ret_fwd", grid=(nt,),
        in_specs=[col(0), col(1), col(2), col(3), tab, tab, pl.BlockSpec((1, RET_WIDTH), lambda i: (0, 0)),
                  head, head, head, head],
        out_specs=[out, out],
        out_shape=[jax.ShapeDtypeStruct((t, RET_WIDTH), F32), jax.ShapeDtypeStruct((t, RET_WIDTH), BF16)],
        scratch_shapes=[pltpu.VMEM((RET_HEADS, RET_DIM, RET_DIM), F32)],
        compiler_params=_params("arbitrary"),
    )(u, u, u, u, cos, sin, gain, decay, zeta, xi, gc)


def _ret_bwd_q(dmix, raw, u, gain, tabs, rides=None):
    t = u.shape[0]
    tr = min(512, t)
    nt = t // tr
    cos, sin, decay, zeta, xi, gc = tabs

    def body(dm_ref, raw_ref, q_ref, k_ref, v_ref, gt_ref, cos_ref, sin_ref, gain_ref, dec_ref, zeta_ref, xi_ref, gc_ref,
             dq_ref, dgt_ref, dret_ref, dgain_ref, state):
        @pl.when(pl.program_id(0) == 0)
        def _():
            state[...] = jnp.zeros_like(state)
            dgain_ref[...] = jnp.zeros_like(dgain_ref)

        items = _ret_chunks(tr)
        n_items = range(len(items))
        qbs, kbs, vbs, kzs = _ret_operands(items, q_ref, k_ref, v_ref, cos_ref, sin_ref, zeta_ref)
        dos, dgains = [], {}
        for sl, hs, h in items:
            o = raw_ref[sl, hs]
            mu = jnp.mean(o, axis=-1, keepdims=True)
            var = jnp.mean(jnp.square(o - mu), axis=-1, keepdims=True)
            rs = lax.rsqrt(var + GN_EPS)
            n = (o - mu) * rs
            gt = gt_ref[sl, hs]
            sig = _sigmoid(gt)
            dout = dm_ref[sl, hs]
            gain_h = gain_ref[:, hs]
            dgt_ref[sl, hs] = (dout * (n * gain_h) * (sig * (1.0 + gt * (1.0 - sig)))).astype(BF16)
            dy = dout * (gt * sig)
            dgains[h] = dgains[h] + _rows8(dy * n) if h in dgains else _rows8(dy * n)
            dn = dy * gain_h
            do = rs * (dn - jnp.mean(dn, axis=-1, keepdims=True) - n * jnp.mean(dn * n, axis=-1, keepdims=True))
            dret_ref[sl, hs] = do
            dos.append(do)
        for h, dg in dgains.items():
            dgain_ref[:, h * RET_DIM:(h + 1) * RET_DIM] += dg
        dss = [_dot(dos[i].astype(BF16), vbs[i], NT) for i in n_items]
        kvs = [_dot(kzs[i], vbs[i], TN) for i in n_items]
        befores = _ret_states(items, state, kvs, gc_ref)
        intra = [_dot((dss[i] * dec_ref[items[i][2]]).astype(BF16), kbs[i]) for i in n_items]
        inter = [_dot((dos[i] * xi_ref[items[i][2]]).astype(BF16), befores[i].astype(BF16), NT) for i in n_items]
        for i, (sl, hs, h) in enumerate(items):
            dqr = intra[i] + inter[i]
            dq_ref[sl, hs] = (dqr * cos_ref[sl, :] - _rot(dqr * sin_ref[sl, :])).astype(BF16)

    col, tab, head = _ret_specs(tr, False, nt)
    out = pl.BlockSpec((tr, RET_WIDTH), lambda i: (i, 0))
    return _pallas(
        body, rides, name="ret_bwd_q", grid=(nt,),
        in_specs=[out, out, col(0), col(1), col(2), col(3), tab, tab, pl.BlockSpec((1, RET_WIDTH), lambda i: (0, 0)),
                  head, head, head, head],
        out_specs=[out, out, out, pl.BlockSpec((8, RET_WIDTH), lambda i: (0, 0))],
        out_shape=[jax.ShapeDtypeStruct((t, RET_WIDTH), BF16), jax.ShapeDtypeStruct((t, RET_WIDTH), BF16),
                   jax.ShapeDtypeStruct((t, RET_WIDTH), F32), jax.ShapeDtypeStruct((8, RET_WIDTH), F32)],
        scratch_shapes=[pltpu.VMEM((RET_HEADS, RET_DIM, RET_DIM), F32)],
        sem=("arbitrary",), args=[dmix, raw, u, u, u, u, cos, sin, gain, decay, zeta, xi, gc])


def _ret_bwd_kv(dret, u, tabs, rides=None):
    t = u.shape[0]
    tr = min(512, t)
    nt = t // tr
    cos, sin, decay, zeta, xi, gc = tabs
    scale = RET_DIM ** -0.5

    def body(do_ref, q_ref, k_ref, v_ref, cos_ref, sin_ref, dec_ref, zeta_ref, xi_ref, gc_ref, dk_ref, dv_ref, gst):
        @pl.when(pl.program_id(0) == 0)
        def _():
            gst[...] = jnp.zeros_like(gst)

        items = _ret_chunks(tr, rev=True)
        n = range(len(items))
        qbs, kbs, vbs, kzs = _ret_operands(items, q_ref, k_ref, v_ref, cos_ref, sin_ref, zeta_ref)
        dos = [do_ref[sl, hs] for sl, hs, h in items]
        dobs = [do.astype(BF16) for do in dos]
        ss = [_dot(qbs[i], kbs[i], NT) for i in n]
        dss = [_dot(dobs[i], vbs[i], NT) for i in n]
        steps = [_dot(qbs[i], (dos[i] * xi_ref[items[i][2]]).astype(BF16), TN) for i in n]
        afters = [g.astype(BF16) for g in _ret_states(items, gst, steps, gc_ref)]
        dvs = [_dot((ss[i] * dec_ref[items[i][2]]).astype(BF16), dobs[i], TN) + _dot(kzs[i], afters[i]) for i in n]
        dks = [_dot((dss[i] * dec_ref[items[i][2]]).astype(BF16), qbs[i], TN) for i in n]
        dkz = [_dot(vbs[i], afters[i], NT) for i in n]
        for i, (sl, hs, h) in enumerate(items):
            dv_ref[sl, hs] = dvs[i].astype(BF16)
            dkr = (dks[i] + dkz[i] * zeta_ref[h]) * scale
            dk_ref[sl, hs] = (dkr * cos_ref[sl, :] - _rot(dkr * sin_ref[sl, :])).astype(BF16)

    col, tab, head = _ret_specs(tr, True, nt)
    out = pl.BlockSpec((tr, RET_WIDTH), lambda i: (nt - 1 - i, 0))
    return _pallas(
        body, rides, name="ret_bwd_kv", grid=(nt,),
        in_specs=[out, col(0), col(1), col(2), tab, tab, head, head, head, head],
        out_specs=[out, out],
        out_shape=[jax.ShapeDtypeStruct((t, RET_WIDTH), BF16), jax.ShapeDtypeStruct((t, RET_WIDTH), BF16)],
        scratch_shapes=[pltpu.VMEM((RET_HEADS, RET_DIM, RET_DIM), F32)],
        sem=("arbitrary",), args=[dret, u, u, u, cos, sin, decay, zeta, xi, gc])


PAIRS = ATT_WIDTH // LANE
ATT_Q_BLK, ATT_K_BLK, ATT_V_BLK = 0, PAIRS, 2 * PAIRS
STAT_LANES = ATT_DIM // 2


ATT_STEP_ROWS = 2048


def _att_tiles(t, dil):
    sub = t // dil
    tq = min(ATT_STEP_ROWS, sub)
    return sub, tq, sub // tq, tq // ATT_BLOCK, min(dil, ATT_STEP_ROWS // tq)


def _att_in_specs(tq, qb, ti, gs):
    cur = lambda off: pl.BlockSpec((gs, tq, LANE), lambda g, p, i: (g, ti(i), off + p))
    prev = lambda off: pl.BlockSpec((gs, ATT_BLOCK, LANE), lambda g, p, i: (g, jnp.maximum(ti(i) * qb - 1, 0), off + p))
    return [cur(ATT_Q_BLK), cur(ATT_K_BLK), prev(ATT_K_BLK), cur(ATT_V_BLK), prev(ATT_V_BLK)]


def _band_mask():
    key = lax.broadcasted_iota(jnp.int32, (2 * ATT_BLOCK, 2 * ATT_BLOCK), 0)
    qry = lax.broadcasted_iota(jnp.int32, (2 * ATT_BLOCK, 2 * ATT_BLOCK), 1) % ATT_BLOCK
    dist = qry + ATT_BLOCK - key
    return (dist >= 0) & (dist <= ATT_BLOCK), key >= ATT_BLOCK


def _head0_lanes():
    return lax.broadcasted_iota(jnp.int32, (ATT_BLOCK, LANE), 1) < ATT_DIM


def _stack_heads(v, head0):
    zero = jnp.zeros((), v.dtype)
    return jnp.concatenate([jnp.where(head0, v, zero), jnp.where(head0, zero, v)], axis=0)


def _unstack_heads(v, head0):
    return jnp.where(head0, v[0:ATT_BLOCK], v[ATT_BLOCK:])


def _att_fwd(ua, dil):
    sub = ua.shape[1]
    _, tq, nq, qb, gs = _att_tiles(sub * dil, dil)

    def body(q_ref, kc_ref, kp_ref, vc_ref, vp_ref, o_ref, l_ref, kx, vx):
        tile = pl.program_id(2)
        kx[:, 0:ATT_BLOCK, :] = kp_ref[...]
        kx[:, ATT_BLOCK:, :] = kc_ref[...]
        vx[:, 0:ATT_BLOCK, :] = vp_ref[...]
        vx[:, ATT_BLOCK:, :] = vc_ref[...]
        band, cur_keys = _band_mask()
        head0 = _head0_lanes()
        items = [(r, b) for r in range(gs) for b in range(qb)]
        rows = lambda b: slice(b * ATT_BLOCK, (b + 1) * ATT_BLOCK)
        keys = lambda b: slice(b * ATT_BLOCK, (b + 2) * ATT_BLOCK)
        sts = [_dot(kx[r, keys(b), :], _stack_heads(q_ref[r, rows(b), :] * jnp.asarray(ATT_DIM ** -0.5, BF16), head0), NT)
               for r, b in items]
        pts, lses = [], []
        for (r, b), st in zip(items, sts):
            mask = band if b > 0 else band & (cur_keys | (tile > 0))
            st = jnp.where(mask, st, -1e30)
            m = jnp.max(st, axis=0, keepdims=True)
            ex = jnp.exp(st - m)
            den = jnp.sum(ex, axis=0, keepdims=True)
            pts.append((ex * (1.0 / den)).astype(BF16))
            lses.append(m + jnp.log(den))
        outs = [_dot(pt, vx[r, keys(b), :], TN) for (r, b), pt in zip(items, pts)]
        for (r, b), out, lse in zip(items, outs, lses):
            o_ref[r, rows(b), :] = _unstack_heads(out, head0).astype(BF16)
            cols = [jnp.broadcast_to(lse[:, e * ATT_BLOCK:(e + 1) * ATT_BLOCK], (ATT_BLOCK, LANE)).T for e in range(2)]
            l_ref[r, rows(b), :] = jnp.where(head0, cols[0], cols[1])

    out = pl.BlockSpec((gs, tq, LANE), lambda g, p, i: (g, i, p))
    return pl.pallas_call(
        body, name=f"att_fwd_d{dil}", grid=(dil // gs, PAIRS, nq),
        in_specs=_att_in_specs(tq, qb, lambda i: i, gs),
        out_specs=[out, out],
        out_shape=[jax.ShapeDtypeStruct((dil, sub, ATT_WIDTH), BF16), jax.ShapeDtypeStruct((dil, sub, ATT_WIDTH), F32)],
        scratch_shapes=[pltpu.VMEM((gs, tq + ATT_BLOCK, LANE), BF16)] * 2,
        compiler_params=_params("arbitrary", "arbitrary", "arbitrary"),
    )(ua, ua, ua, ua, ua)


def _regrouped_spec(tm, dil, w):
    return pl.BlockSpec((dil, tm // dil, w), lambda i: (0, i, 0))


def _att_combine(outs, lses, t):
    w = ATT_WIDTH
    tm = min(512, t)
    nb = len(outs)

    def body(*refs):
        o_refs, l_refs = refs[:nb], refs[nb:2 * nb]
        mix_ref, att_ref, lse_ref, buf = refs[2 * nb:]
        ls = [_natural_rows(r, dil, buf) for r, dil in zip(l_refs, DILATIONS)]
        m = functools.reduce(jnp.maximum, ls)
        ws = [jnp.exp(l - m) for l in ls]
        den = functools.reduce(jnp.add, ws)
        att = functools.reduce(jnp.add, [(wt / den) * _natural_rows(r, dil, buf) for wt, r, dil in zip(ws, o_refs, DILATIONS)])
        att_ref[...] = att
        mix_ref[...] = att.astype(BF16)
        lse_ref[...] = m + jnp.log(den)

    tile = pl.BlockSpec((tm, w), lambda i: (i, 0))
    regrouped = [_regrouped_spec(tm, dil, w) for dil in DILATIONS]
    return pl.pallas_call(
        body, name="att_combine", grid=(t // tm,),
        in_specs=regrouped * 2, out_specs=[tile, tile, tile],
        out_shape=[jax.ShapeDtypeStruct((t, w), BF16), jax.ShapeDtypeStruct((t, w), F32), jax.ShapeDtypeStruct((t, w), F32)],
        scratch_shapes=[_chunk_scratch(tm, w)],
        compiler_params=_params("arbitrary"),
    )(*outs, *lses)


def _att_bwd_prep(datt, att, lse):
    t, w = datt.shape
    tm = min(512, t)

    def body(da_ref, at_ref, l_ref, *rest):
        outs, dbuf, sbuf = rest[:-2], rest[-2], rest[-1]
        dav = da_ref[...]
        prod = dav * at_ref[...]
        lane = lax.broadcasted_iota(jnp.int32, (tm, LANE), 1)
        for k in range(w // LANE):
            cols = slice(k * LANE, (k + 1) * LANE)
            dbuf[k] = dav[:, cols]
            delta = jnp.concatenate(
                [jnp.broadcast_to(jnp.sum(prod[:, k * LANE + e * ATT_DIM:k * LANE + (e + 1) * ATT_DIM], axis=-1, keepdims=True),
                                  (tm, ATT_DIM)) for e in range(LANE // ATT_DIM)], axis=1)
            sbuf[k] = jnp.where(lane % ATT_DIM < STAT_LANES, l_ref[:, cols], delta)
        for k, dil in enumerate(DILATIONS):
            _regroup_store(dbuf, outs[2 * k], dil)
            _regroup_store(sbuf, outs[2 * k + 1], dil)

    tile = pl.BlockSpec((tm, w), lambda i: (i, 0))
    res = pl.pallas_call(
        body, name="att_bwd_prep", grid=(t // tm,),
        in_specs=[tile] * 3,
        out_specs=[_regrouped_spec(tm, dil, w) for dil in DILATIONS for _ in range(2)],
        out_shape=[jax.ShapeDtypeStruct((dil, t // dil, w), dt) for dil in DILATIONS for dt in (BF16, F32)],
        scratch_shapes=[_chunk_scratch(tm, w)] * 2,
        compiler_params=_params("arbitrary"),
    )(datt, att, lse)
    return [(res[2 * k], res[2 * k + 1]) for k in range(len(DILATIONS))]


def _att_bwd(ua, da, stat, dil, rides=None):
    sub = ua.shape[1]
    _, tq, nq, qb, gs = _att_tiles(sub * dil, dil)
    scale = ATT_DIM ** -0.5

    def body(q_ref, kc_ref, kp_ref, vc_ref, vp_ref, da_ref, st_ref, dq_ref, dk_ref, dv_ref, kx, vx, ck, cv):
        step = pl.program_id(2)
        tile = nq - 1 - step

        @pl.when(step == 0)
        def _():
            ck[...] = jnp.zeros_like(ck)
            cv[...] = jnp.zeros_like(cv)

        kx[:, 0:ATT_BLOCK, :] = kp_ref[...]
        kx[:, ATT_BLOCK:, :] = kc_ref[...]
        vx[:, 0:ATT_BLOCK, :] = vp_ref[...]
        vx[:, ATT_BLOCK:, :] = vc_ref[...]
        band, cur_keys = _band_mask()
        head0 = _head0_lanes()
        items = [(r, b) for r in range(gs) for b in range(qb)]
        n = range(len(items))
        rows = lambda b: slice(b * ATT_BLOCK, (b + 1) * ATT_BLOCK)
        keys = lambda b: slice(b * ATT_BLOCK, (b + 2) * ATT_BLOCK)
        qqs = [_stack_heads(q_ref[r, rows(b), :] * jnp.asarray(scale, BF16), head0) for r, b in items]
        dds = [_stack_heads(da_ref[r, rows(b), :], head0) for r, b in items]
        sts = [_dot(kx[r, keys(b), :], qqs[i], NT) for i, (r, b) in enumerate(items)]
        dpts = [_dot(vx[r, keys(b), :], dds[i], NT) for i, (r, b) in enumerate(items)]
        pts, dsts = [], []
        for i, (r, b) in enumerate(items):
            mask = band if b > 0 else band & (cur_keys | (tile > 0))
            stat = st_ref[r, rows(b), :].T
            row = lambda k: jnp.concatenate([stat[e * ATT_DIM + k:e * ATT_DIM + k + 1, :] for e in range(2)], axis=1)
            pt = jnp.where(mask, jnp.exp(sts[i] - row(0)), 0.0)
            dsts.append((pt * (dpts[i] - row(STAT_LANES))).astype(BF16))
            pts.append(pt.astype(BF16))
        dqs = [_dot(dsts[i], kx[r, keys(b), :], TN) for i, (r, b) in enumerate(items)]
        dkbs = [_dot(dsts[i], qqs[i]) for i in n]
        dvbs = [_dot(pts[i], dds[i]) for i in n]
        for i, (r, b) in enumerate(items):
            dq_ref[r, rows(b), :] = (_unstack_heads(dqs[i], head0) * scale).astype(BF16)
            if b > 0:
                dk_ref[r, rows(b - 1), :] = (dkbs[i - 1][ATT_BLOCK:] + dkbs[i][0:ATT_BLOCK]).astype(BF16)
                dv_ref[r, rows(b - 1), :] = (dvbs[i - 1][ATT_BLOCK:] + dvbs[i][0:ATT_BLOCK]).astype(BF16)
        for r in range(gs):
            first, last = r * qb, r * qb + qb - 1
            dk_ref[r, rows(qb - 1), :] = (dkbs[last][ATT_BLOCK:] + ck[r]).astype(BF16)
            dv_ref[r, rows(qb - 1), :] = (dvbs[last][ATT_BLOCK:] + cv[r]).astype(BF16)
            ck[r] = dkbs[first][0:ATT_BLOCK]
            cv[r] = dvbs[first][0:ATT_BLOCK]

    ti = lambda i: nq - 1 - i
    out = pl.BlockSpec((gs, tq, LANE), lambda g, p, i: (g, ti(i), p))
    shape = jax.ShapeDtypeStruct((dil, sub, ATT_WIDTH), BF16)
    return _pallas(
        body, rides, name=f"att_bwd_d{dil}", grid=(dil // gs, PAIRS, nq),
        in_specs=_att_in_specs(tq, qb, ti, gs) + [out, out],
        out_specs=[out, out, out], out_shape=[shape] * 3,
        scratch_shapes=[pltpu.VMEM((gs, tq + ATT_BLOCK, LANE), BF16)] * 2 + [pltpu.VMEM((gs, ATT_BLOCK, LANE), F32)] * 2,
        sem=("arbitrary", "arbitrary", "arbitrary"), args=[ua, ua, ua, ua, ua, da, stat])


class _Reduction:
    def __init__(self, place, names, grads):
        self.place, self.names, self.grads = place, names, grads

    def pair(self):
        return _pair_ride(self.grads)

    def chips(self, got):
        self.got = got
        return _chip_ride([_pair_sum(self.place, g, r, f"pair_sum_{n}") for g, r, n in zip(self.grads, got, self.names)])

    def halves(self, others):
        return [_chip_sum(self.place, g, r, o, f"chip_sum_{n}")
                for g, r, o, n in zip(self.grads, self.got, others, self.names)]


def _step(x, target, gains, w, place=None):
    t = x.shape[0]
    ex = place is not None
    g_ffn1, g_mix, g_ret, g_ffn2, g_fin = gains
    w = list(w)
    tabs = _retention_tables(t)
    red = lambda names, grads: _Reduction(place, names, grads) if ex else None
    ride = lambda r: [r] if ex else None

    if ex:
        w[0:3] = _run(_gather_ride(w[0:3]), "gather_ffn1_weights")
    (h1, xn1, *hid1, act1), rest = _ffn_fwd(x, g_ffn1, *w[0:3], "ffn1_fwd", ride(_gather_ride(w[3:])) if ex else None)
    if ex:
        w[3:] = rest[0]
    wg1, wu1, wd1, win, wo, wg2, wu2, wd2 = w
    wo2 = wo.reshape(wo.shape[0] * wo.shape[1], wo.shape[2])
    xnm, u, *uas = _inproj_fwd(h1, g_mix, win)
    raw, mix_r = _ret_fwd(u, g_ret, tabs)
    branches = [_att_fwd(ua, dil) for ua, dil in zip(uas, DILATIONS)]
    mix_a, att, lse = _att_combine([b[0] for b in branches], [b[1] for b in branches], t)
    h2 = _outproj_fwd(h1, mix_r, mix_a, wo2)
    (dh3, xn2, *hid2, act2, loss_p, dg_fin), _ = _ffn_fwd(h2, g_ffn2, wg2, wu2, wd2, "ffn2_fwd", head=(g_fin, target))

    (dwd2,), _ = _ffn_wgrad_down(act2, dh3, "ffn2_wgrad_down")
    dwd2 = dwd2.reshape(wd2.shape)
    r_d2 = red(["ffn2_w_down"], [dwd2])
    (dh2, dga2, dua2, dg_ffn2), e = _ffn_bwd_data(dh3, h2, g_ffn2, *hid2, wg2, wu2, wd2, "ffn2_bwd",
                                                  ex and [r_d2.pair()])
    (dwg2, dwu2), e = _ffn_wgrad_gu(xn2, [dga2, dua2], "ffn2_wgrad_gu", ex and [r_d2.chips(e[0])])
    dwg2, dwu2 = dwg2.reshape(wg2.shape), dwu2.reshape(wu2.shape)
    r_gu2 = red(["ffn2_w_gate", "ffn2_w_up"], [dwg2, dwu2])
    (dmix_r, dmix_a), e = _outproj_bwd(dh2, wo2, ex and [r_gu2.pair(), _finish_ride(r_d2.halves(e[0]))])
    if ex:
        got_gu2, (dwd2,) = e
    hw = RET_WIDTH // (wo.shape[1])
    dwo = jnp.concatenate([_tn_matmul(mix_r, dh2, dh2.shape[1], "wo_grad_r").reshape(hw, wo.shape[1], wo.shape[2]),
                           _tn_matmul(mix_a, dh2, dh2.shape[1], "wo_grad_a").reshape(hw, wo.shape[1], wo.shape[2])])
    r_wo = red(["w_out"], [dwo])
    (dq_r, dgt_r, dret, dg_ret), e = _ret_bwd_q(dmix_r, raw, u, g_ret, tabs, ex and [r_gu2.chips(got_gu2)])
    (dk_r, dv_r), e = _ret_bwd_kv(dret, u, tabs, ex and [r_wo.pair(), _finish_ride(r_gu2.halves(e[0]))])
    if ex:
        got_wo, (dwg2, dwu2) = e
    prep = _att_bwd_prep(dmix_a, att, lse)
    p1, e = _att_bwd(uas[0], *prep[0], DILATIONS[0], ex and [r_wo.chips(got_wo)])
    p4, e = _att_bwd(uas[1], *prep[1], DILATIONS[1], ex and [_finish_ride(r_wo.halves(e[0]))])
    if ex:
        (dwo,), = e
    p16, _ = _att_bwd(uas[2], *prep[2], DILATIONS[2])
    dh1, du, dg_mix = _inproj_bwd([dq_r, dk_r, dv_r, dgt_r], [p1, p4, p16], h1, g_mix, dh2, win)
    dwin = _tn_matmul(xnm, du, win.shape[2], "win_grad")
    r_in = red(["w_in"], [dwin])
    (dwd1,), e = _ffn_wgrad_down(act1, dh1, "ffn1_wgrad_down", ex and [r_in.pair()])
    dwd1 = dwd1.reshape(wd1.shape)
    r_d1 = red(["ffn1_w_down"], [dwd1])
    (dx, dga1, dua1, dg_ffn1), e = _ffn_bwd_data(dh1, x, g_ffn1, *hid1, wg1, wu1, wd1, "ffn1_bwd",
                                                  ex and [r_in.chips(e[0]), r_d1.pair()])
    (dwg1,), e = _ffn_wgrad_gu(xn1, [dga1], "ffn1_wgrad_gate", ex and [_finish_ride(r_in.halves(e[0])), r_d1.chips(e[1])])
    dwg1 = dwg1.reshape(wg1.shape)
    if ex:
        (dwin,), oth_d1 = e
        r_g1 = red(["ffn1_w_gate"], [dwg1])
        got_g1 = _run(r_g1.pair(), "pair_exchange_ffn1_gate")
    (dwu1,), e = _ffn_wgrad_gu(xn1, [dua1], "ffn1_wgrad_up", ex and [r_g1.chips(got_g1)])
    dwu1 = dwu1.reshape(wu1.shape)
    gain_parts = [dg_ffn1, dg_mix, dg_ret, dg_ffn2, dg_fin]
    if not ex:
        return loss_p, dx, [dwg1, dwu1, dwd1, dwin, dwo, dwg2, dwu2, dwd2], gain_parts
    r_u1 = red(["ffn1_w_up"], [dwu1])
    got_u1 = _run(r_u1.pair(), "pair_exchange_ffn1_up")
    oth_u1 = _run(r_u1.chips(got_u1), "chip_exchange_ffn1_up")
    last = r_g1.halves(e[0]) + r_u1.halves(oth_u1) + r_d1.halves(oth_d1)
    dwg1, dwu1, dwd1, gall = _run(_finish_ride(last, _pack_gains(gain_parts, x.shape[1])), "finish_exchange_ffn1")
    return loss_p, dx, [dwg1, dwu1, dwd1, dwin, dwo, dwg2, dwu2, dwd2], gall


N_DEV = 8
GAIN_ROWS = 8


def _place():
    x, y, c = lax.axis_index("x"), lax.axis_index("y"), lax.axis_index("c")
    chips = [(1 - x, y), (x, 1 - y), (1 - x, 1 - y)]
    return x, y, c, chips


ROW_QUARTERS = 4


def _place_shards(place, ws):
    n = len(ws)

    def body(place_ref, *refs):
        for w_ref, o_ref in zip(refs[:n], refs[n:]):
            o_ref[...] = w_ref[...].astype(BF16)

    quarter = lambda w: (w.shape[0] // ROW_QUARTERS, w.shape[1])
    return pl.pallas_call(
        body, name="place_shards",
        grid_spec=pltpu.PrefetchScalarGridSpec(
            num_scalar_prefetch=1, grid=(ROW_QUARTERS,),
            in_specs=[pl.BlockSpec(quarter(w), lambda i, pr: (i, 0)) for w in ws],
            out_specs=[pl.BlockSpec((None,) + quarter(w), lambda i, pr: (pr[0], i, 0)) for w in ws]),
        out_shape=[jax.ShapeDtypeStruct((N_SHARD,) + w.shape, BF16) for w in ws],
        compiler_params=_params("arbitrary"),
    )(place, *ws)


def _gather_ride(bufs):
    na = len(bufs)

    def legs(outs, sems):
        send_sem, recv_sem, fsend_sem, frecv_sem = sems
        x, y, c, chips = _place()

        def half(a, idx, which):
            hr = outs[a].shape[1] // 2
            return outs[a].at[idx, pl.ds(which * hr, hr)]

        def ici(a, j, idx):
            px, py = chips[j]
            return pltpu.make_async_remote_copy(
                src_ref=half(a, idx, c), dst_ref=half(a, idx, c),
                send_sem=send_sem.at[a, j], recv_sem=recv_sem.at[a, j], device_id=(px, py, c), device_id_type=MESH)

        def d2d(a, j, idx, which):
            return pltpu.make_async_remote_copy(
                src_ref=half(a, idx, which), dst_ref=half(a, idx, which),
                send_sem=fsend_sem.at[a, j], recv_sem=frecv_sem.at[a, j], device_id=(x, y, 1 - c), device_id_type=MESH)

        return 2 * x + y, c, chips, ici, d2d

    def start(ins, outs, sems):
        me, _, _, ici, _ = legs(outs, sems)
        for a in range(na):
            for j in range(3):
                ici(a, j, me).start()

    def finish(ins, outs, sems):
        me, c, chips, ici, d2d = legs(outs, sems)
        passed = []
        for a in range(na):
            for j, (px, py) in enumerate(chips):
                ici(a, j, 2 * px + py).wait_recv()
                cp = d2d(a, j, 2 * px + py, c)
                cp.start()
                passed.append(cp)
        for a in range(na):
            for j, (px, py) in enumerate(chips):
                d2d(a, j, 2 * px + py, 1 - c).wait_recv()
        for a in range(na):
            for j in range(3):
                ici(a, j, me).wait_send()
        for cp in passed:
            cp.wait_send()

    return _Ride(bufs, [jax.ShapeDtypeStruct(b.shape, b.dtype) for b in bufs], [pltpu.SemaphoreType.DMA((na, 3))] * 4,
                 start, finish, {a: a for a in range(na)})


def _pair_ride(grads):
    na = len(grads)

    def copies(ins, outs, sems):
        send_sem, recv_sem = sems
        x, y, c, _ = _place()
        res = []
        for a in range(na):
            hr = ins[a].shape[1] // 2
            res.append(pltpu.make_async_remote_copy(
                src_ref=ins[a].at[:, pl.ds((1 - c) * hr, hr)], dst_ref=outs[a],
                send_sem=send_sem.at[a], recv_sem=recv_sem.at[a], device_id=(x, y, 1 - c), device_id_type=MESH))
        return res

    def start(ins, outs, sems):
        for cp in copies(ins, outs, sems):
            cp.start()

    def finish(ins, outs, sems):
        for cp in copies(ins, outs, sems):
            cp.wait()

    return _Ride(grads, [jax.ShapeDtypeStruct((g.shape[0], g.shape[1] // 2, g.shape[2]), g.dtype) for g in grads],
                 [pltpu.SemaphoreType.DMA((na,))] * 2, start, finish)


def _chip_ride(sums):
    na = len(sums)

    def copies(ins, outs, sems):
        send_sem, recv_sem = sems
        x, y, c, chips = _place()
        res = []
        for a in range(na):
            for j, (px, py) in enumerate(chips):
                res.append(pltpu.make_async_remote_copy(
                    src_ref=ins[a].at[2 * px + py], dst_ref=outs[a].at[j],
                    send_sem=send_sem.at[a, j], recv_sem=recv_sem.at[a, j], device_id=(px, py, c), device_id_type=MESH))
        return res

    def start(ins, outs, sems):
        for cp in copies(ins, outs, sems):
            cp.start()

    def finish(ins, outs, sems):
        for cp in copies(ins, outs, sems):
            cp.wait()

    return _Ride(sums, [jax.ShapeDtypeStruct((3,) + s.shape[1:], s.dtype) for s in sums],
                 [pltpu.SemaphoreType.DMA((na, 3))] * 2, start, finish)


def _finish_ride(grads, gpack=None):
    na = len(grads)

    def halves(outs, sems, which):
        x, y, c, _ = _place()
        res = []
        for a in range(na):
            hr = outs[a].shape[0] // 2
            rows = outs[a].at[pl.ds((c if which == "mine" else 1 - c) * hr, hr)]
            res.append(pltpu.make_async_remote_copy(
                src_ref=rows, dst_ref=rows, send_sem=sems[0].at[a], recv_sem=sems[1].at[a],
                device_id=(x, y, 1 - c), device_id_type=MESH))
        return res

    def gains(ins, outs, sems):
        x, y, c, _ = _place()
        dev = 4 * x + 2 * y + c
        g_in, g_out = ins[na], outs[na]
        own = pltpu.make_async_copy(g_in, g_out.at[dev], sems[2])
        sends, lands = [], []
        for k in range(N_DEV - 1):
            bx, by, bc = (k + 1) // 4, ((k + 1) // 2) % 2, (k + 1) % 2
            peer = (jnp.bitwise_xor(x, bx), jnp.bitwise_xor(y, by), jnp.bitwise_xor(c, bc))
            sends.append(pltpu.make_async_remote_copy(
                src_ref=g_in, dst_ref=g_out.at[dev], send_sem=sems[3].at[k], recv_sem=sems[4].at[k],
                device_id=peer, device_id_type=MESH))
            slot = g_out.at[jnp.bitwise_xor(dev, k + 1)]
            lands.append(pltpu.make_async_remote_copy(
                src_ref=slot, dst_ref=slot, send_sem=sems[3].at[k], recv_sem=sems[4].at[k],
                device_id=peer, device_id_type=MESH))
        return own, sends, lands

    def start(ins, outs, sems):
        for cp in halves(outs, sems, "mine"):
            cp.start()
        if gpack is not None:
            own, sends, _ = gains(ins, outs, sems)
            own.start()
            for cp in sends:
                cp.start()

    def finish(ins, outs, sems):
        for cp in halves(outs, sems, "sibling's"):
            cp.wait_recv()
        if gpack is not None:
            own, sends, lands = gains(ins, outs, sems)
            for cp in lands:
                cp.wait_recv()
            for cp in sends:
                cp.wait_send()
            own.wait()
        for cp in halves(outs, sems, "mine"):
            cp.wait_send()

    shapes = [jax.ShapeDtypeStruct(g.shape, g.dtype) for g in grads]
    sems = [pltpu.SemaphoreType.DMA((na,))] * 2
    if gpack is None:
        return _Ride(grads, shapes, sems, start, finish, {a: a for a in range(na)})
    return _Ride(list(grads) + [gpack], shapes + [jax.ShapeDtypeStruct((N_DEV,) + gpack.shape, gpack.dtype)],
                 sems + [pltpu.SemaphoreType.DMA, pltpu.SemaphoreType.DMA((N_DEV - 1,)), pltpu.SemaphoreType.DMA((N_DEV - 1,))],
                 start, finish, {a: a for a in range(na)})


def _pair_sum(place, grad, got, name):
    ns, r, cols = grad.shape
    hr = r // 2

    def body(place_ref, g_ref, r_ref, o_ref):
        o_ref[...] = (g_ref[...] + r_ref[...]).astype(BF16)

    return pl.pallas_call(
        body, name=name,
        grid_spec=pltpu.PrefetchScalarGridSpec(
            num_scalar_prefetch=1, grid=(ns,),
            in_specs=[pl.BlockSpec((None, hr, cols), lambda s, pr: (s, pr[1], 0)),
                      pl.BlockSpec((None, hr, cols), lambda s, pr: (s, 0, 0))],
            out_specs=pl.BlockSpec((None, hr, cols), lambda s, pr: (s, 0, 0))),
        out_shape=jax.ShapeDtypeStruct((ns, hr, cols), BF16),
        compiler_params=_params("arbitrary"),
    )(place, grad, got)


def _chip_sum(place, grad, got, others, name):
    ns, r, cols = grad.shape
    hr = r // 2
    nb = 2
    tr = hr // nb

    def body(place_ref, g_ref, r_ref, o3_ref, o_ref):
        acc = g_ref[...] + r_ref[...]
        for j in range(3):
            acc = acc + o3_ref[j].astype(F32)
        o_ref[...] = acc

    return pl.pallas_call(
        body, name=name,
        grid_spec=pltpu.PrefetchScalarGridSpec(
            num_scalar_prefetch=1, grid=(nb,),
            in_specs=[pl.BlockSpec((None, tr, cols), lambda i, pr: (pr[0], pr[1] * nb + i, 0)),
                      pl.BlockSpec((None, tr, cols), lambda i, pr: (pr[0], i, 0)),
                      pl.BlockSpec((3, tr, cols), lambda i, pr: (0, i, 0))],
            out_specs=pl.BlockSpec((tr, cols), lambda i, pr: (pr[1] * nb + i, 0))),
        out_shape=jax.ShapeDtypeStruct((r, cols), F32),
        compiler_params=_params("arbitrary"),
    )(place, grad, got, others)


def _pack_gains(parts, d):
    def body(*refs):
        ins, o_ref = refs[:-1], refs[-1]
        o_ref[...] = jnp.zeros_like(o_ref)
        for k, r in enumerate(ins):
            o_ref[k:k + 1, 0:r.shape[1]] = jnp.sum(r[...], axis=0, keepdims=True)

    return pl.pallas_call(
        body, name="pack_gains", out_shape=jax.ShapeDtypeStruct((GAIN_ROWS, d), F32),
    )(*parts)


def _adamw_math(w, g, m, v):
    m = ADAM_B1 * m + (1.0 - ADAM_B1) * g
    v = ADAM_B2 * v + (1.0 - ADAM_B2) * jnp.square(g)
    m_hat = m / (1.0 - ADAM_B1 ** ADAM_STEP)
    v_hat = v / (1.0 - ADAM_B2 ** ADAM_STEP)
    return -ADAM_LR * (m_hat / (jnp.sqrt(v_hat) + ADAM_EPS) + ADAM_WD * w), m, v


def _adamw(ws, gs, ms, vs):
    n = len(ws)

    def body(*refs):
        ins, outs = refs[:4 * n], refs[4 * n:]
        for k in range(n):
            w_ref, g_ref, m_ref, v_ref = ins[4 * k:4 * k + 4]
            go_ref, d_ref, nm_ref, nv_ref = outs[4 * k:4 * k + 4]
            g = g_ref[...]
            go_ref[...] = g
            d_ref[...], nm_ref[...], nv_ref[...] = _adamw_math(w_ref[...], g, m_ref[...], v_ref[...])

    parts = 2 * ROW_QUARTERS
    tile = lambda w: pl.BlockSpec((w.shape[0] // parts, w.shape[1]), lambda i: (i, 0))
    res = pl.pallas_call(
        body, name="adamw_shards", grid=(parts,),
        in_specs=[tile(w) for w in ws for _ in range(4)], out_specs=[tile(w) for w in ws for _ in range(4)],
        out_shape=[jax.ShapeDtypeStruct(w.shape, F32) for w in ws for _ in range(4)],
        compiler_params=_params("arbitrary"),
    )(*[a for quad in zip(ws, gs, ms, vs) for a in quad])
    return [res[4 * k:4 * k + 4] for k in range(n)]


def _adamw_gain(gall, row, w, m, v, name):
    n = w.shape[1]

    def body(ga_ref, w_ref, m_ref, v_ref, g_ref, d_ref, nm_ref, nv_ref):
        g = ga_ref[0, row:row + 1, 0:n]
        for k in range(1, N_DEV):
            g = g + ga_ref[k, row:row + 1, 0:n]
        g_ref[...] = g
        d_ref[...], nm_ref[...], nv_ref[...] = _adamw_math(w_ref[...], g, m_ref[...], v_ref[...])

    return pl.pallas_call(
        body, name=name, out_shape=[jax.ShapeDtypeStruct((1, n), F32)] * 4,
    )(gall, w, m, v)


def kernel(x, norm_ffn1, ffn1_w_gate, ffn1_w_up, ffn1_w_down, norm_mix, w_in, ret_norm_gain, w_out, norm_ffn2, ffn2_w_gate, ffn2_w_up, ffn2_w_down, norm_final, loss_target, m_norm_ffn1, m_ffn1_w_gate, m_ffn1_w_up, m_ffn1_w_down, m_norm_mix, m_w_in, m_ret_norm_gain, m_w_out, m_norm_ffn2, m_ffn2_w_gate, m_ffn2_w_up, m_ffn2_w_down, m_norm_final, v_norm_ffn1, v_ffn1_w_gate, v_ffn1_w_up, v_ffn1_w_down, v_norm_mix, v_w_in, v_ret_norm_gain, v_w_out, v_norm_ffn2, v_ffn2_w_gate, v_ffn2_w_up, v_ffn2_w_down, v_norm_final):
    d = x.shape[-1]
    mats = [ffn1_w_gate, ffn1_w_up, ffn1_w_down, w_in, w_out, ffn2_w_gate, ffn2_w_up, ffn2_w_down]
    mats_m = [m_ffn1_w_gate, m_ffn1_w_up, m_ffn1_w_down, m_w_in, m_w_out, m_ffn2_w_gate, m_ffn2_w_up, m_ffn2_w_down]
    mats_v = [v_ffn1_w_gate, v_ffn1_w_up, v_ffn1_w_down, v_w_in, v_w_out, v_ffn2_w_gate, v_ffn2_w_up, v_ffn2_w_down]
    mat_names = ["ffn1_w_gate", "ffn1_w_up", "ffn1_w_down", "w_in", "w_out", "ffn2_w_gate", "ffn2_w_up", "ffn2_w_down"]
    gains = [norm_ffn1, norm_mix, ret_norm_gain, norm_ffn2, norm_final.reshape(1, d)]
    gains_m = [m_norm_ffn1, m_norm_mix, m_ret_norm_gain, m_norm_ffn2, m_norm_final.reshape(1, d)]
    gains_v = [v_norm_ffn1, v_norm_mix, v_ret_norm_gain, v_norm_ffn2, v_norm_final.reshape(1, d)]
    gain_names = ["norm_ffn1", "norm_mix", "ret_norm_gain", "norm_ffn2", "norm_final"]

    turned = lambda n: n.endswith(("w_gate", "w_up"))
    local = lambda a, n: jnp.swapaxes(a, 1, 2)[0] if turned(n) else a[0]
    back = lambda a, n: jnp.swapaxes(a[None], 1, 2) if turned(n) else a[None]
    shards = [local(w, n) for w, n in zip(mats, mat_names)]
    place = jnp.stack([2 * lax.axis_index("x") + lax.axis_index("y"), lax.axis_index("c")]).astype(jnp.int32)
    placed = _place_shards(place, shards)
    loss_p, dx, shard_grads, gall = _step(x[0], loss_target[0], gains, placed, place)

    out_g, out_d, out_m, out_v = {}, {}, {}, {}
    updates = _adamw(shards, shard_grads, [local(m, n) for m, n in zip(mats_m, mat_names)],
                     [local(v, n) for v, n in zip(mats_v, mat_names)])
    for n, quad in zip(mat_names, updates):
        out_g[n], out_d[n], out_m[n], out_v[n] = [back(a, n) for a in quad]
    for row, (n, w, m, v) in enumerate(zip(gain_names, gains, gains_m, gains_v)):
        res = _adamw_gain(gall, row, w, m, v, f"adamw_{n}")
        shape = (d,) if n == "norm_final" else w.shape
        out_g[n], out_d[n], out_m[n], out_v[n] = [r.reshape(shape) for r in res]

    loss = lax.psum(jnp.sum(loss_p), ("x", "y", "c"))
    order = ["norm_ffn1", "ffn1_w_gate", "ffn1_w_up", "ffn1_w_down", "norm_mix", "w_in", "ret_norm_gain", "w_out",
             "norm_ffn2", "ffn2_w_gate", "ffn2_w_up", "ffn2_w_down", "norm_final"]
    return (loss, dx[None], *[out_g[n] for n in order], *[out_d[n] for n in order],
            *[out_m[n] for n in order], *[out_v[n] for n in order])
```

```python
import functools

import jax
import jax.numpy as jnp
from jax import lax
from jax.experimental import pallas as pl
from jax.experimental.pallas import tpu as pltpu

F32 = jnp.float32
BF16 = jnp.bfloat16
MESH = pl.DeviceIdType.MESH

NORM_EPS = 1e-6
GN_EPS = 1e-6
ROPE_BASE = 10000.0
RET_HEADS = 4
RET_DIM = 128
RET_WIDTH = 512
RET_CHUNK = 128
ATT_DIM = 64
ATT_WIDTH = 512
ATT_BLOCK = 128
DILATIONS = (1, 4, 16)
LANE = 128
N_SHARD = 4
ADAM_LR, ADAM_B1, ADAM_B2, ADAM_EPS, ADAM_WD, ADAM_STEP = 0.001, 0.9, 0.999, 1e-08, 0.01, 10

V7X_VMEM_BYTES = 64 * 1024 * 1024
VMEM_LIMIT = V7X_VMEM_BYTES - 8 * 1024 * 1024

NT = (((1,), (1,)), ((), ()))
TN = (((0,), (0,)), ((), ()))


def _params(*sem):
    return pltpu.CompilerParams(dimension_semantics=sem, vmem_limit_bytes=VMEM_LIMIT)


def _dot(a, b, dims=None):
    if dims is None:
        return jnp.dot(a, b, preferred_element_type=F32)
    return lax.dot_general(a, b, dims, preferred_element_type=F32)


def _sigmoid(x):
    return 1.0 / (1.0 + jnp.exp(-x))


def _load_weights(pairs, sems):
    copies = [pltpu.make_async_copy(src, dst, sems.at[k]) for k, (src, dst) in enumerate(pairs)]
    for cp in copies:
        cp.start()
    for cp in copies:
        cp.wait()


def _rows8(v):
    r, c = v.shape
    return v.reshape(r // 8, 8, c).sum(axis=0)


class _Ride:
    def __init__(self, inputs, out_shapes, sems, start, finish, aliases=None):
        self.inputs, self.out_shapes, self.sems = list(inputs), list(out_shapes), list(sems)
        self.start, self.finish, self.aliases = start, finish, dict(aliases or {})


def _pallas(body, rides, *, name, in_specs, out_specs, out_shape, args, grid=(), scratch_shapes=(), sem=()):
    rides = [r for r in (rides or []) if r is not None]
    n_in, n_out, n_scr = len(args), len(out_shape), len(scratch_shapes)
    hbm = pl.BlockSpec(memory_space=pl.ANY)
    r_in = [a for r in rides for a in r.inputs]
    r_out = [s for r in rides for s in r.out_shapes]
    r_sem = [s for r in rides for s in r.sems]
    aliases, spans, ki, ko, ks = {}, [], 0, 0, 0
    for r in rides:
        aliases.update({n_in + ki + i: n_out + ko + o for i, o in r.aliases.items()})
        spans.append((ki, ko, ks))
        ki, ko, ks = ki + len(r.inputs), ko + len(r.out_shapes), ks + len(r.sems)

    def wrapped(*refs):
        ins, rin = refs[:n_in], refs[n_in:n_in + len(r_in)]
        o0 = n_in + len(r_in)
        outs, rout = refs[o0:o0 + n_out], refs[o0 + n_out:o0 + n_out + len(r_out)]
        s0 = o0 + n_out + len(r_out)
        scr, rsem = refs[s0:s0 + n_scr], refs[s0 + n_scr:]
        part = lambda r, k: (rin[spans[k][0]:spans[k][0] + len(r.inputs)], rout[spans[k][1]:spans[k][1] + len(r.out_shapes)],
                             rsem[spans[k][2]:spans[k][2] + len(r.sems)])
        first = functools.reduce(jnp.logical_and, [pl.program_id(k) == 0 for k in range(len(grid))], True)
        last = functools.reduce(jnp.logical_and, [pl.program_id(k) == grid[k] - 1 for k in range(len(grid))], True)
        if rides:
            @pl.when(first)
            def _():
                for k, r in enumerate(rides):
                    r.start(*part(r, k))

        body(*ins, *outs, *scr)
        if rides:
            @pl.when(last)
            def _():
                for k, r in enumerate(rides):
                    r.finish(*part(r, k))

    res = pl.pallas_call(
        wrapped, name=name, grid=grid,
        in_specs=list(in_specs) + [hbm] * len(r_in), out_specs=list(out_specs) + [hbm] * len(r_out),
        out_shape=list(out_shape) + r_out, input_output_aliases=aliases,
        scratch_shapes=list(scratch_shapes) + r_sem,
        compiler_params=pltpu.CompilerParams(dimension_semantics=sem, vmem_limit_bytes=VMEM_LIMIT) if grid else None,
    )(*args, *r_in)
    extras = [list(res[n_out + ko:n_out + ko + len(r.out_shapes)]) for r, (_, ko, _) in zip(rides, spans)]
    return list(res[:n_out]), extras


def _run(ride, name):
    def body(*refs):
        n_in, n_out = len(ride.inputs), len(ride.out_shapes)
        parts = refs[:n_in], refs[n_in:n_in + n_out], refs[n_in + n_out:]
        ride.start(*parts)
        ride.finish(*parts)

    hbm = pl.BlockSpec(memory_space=pl.ANY)
    return list(pl.pallas_call(
        body, name=name, in_specs=[hbm] * len(ride.inputs), out_specs=[hbm] * len(ride.out_shapes),
        out_shape=ride.out_shapes, input_output_aliases=ride.aliases, scratch_shapes=ride.sems,
    )(*ride.inputs))


def _loss_head(hv, gain_ref, tg_ref, loss_ref, dgain_ref):
    d = hv.shape[1]
    r = lax.rsqrt(jnp.mean(hv * hv, axis=-1, keepdims=True) + NORM_EPS)
    xh = hv * r
    err = xh * gain_ref[...] - tg_ref[...]
    sq = _rows8(jnp.square(err))
    loss_ref[...] += 0.5 * functools.reduce(jnp.add, [sq[:, k * LANE:(k + 1) * LANE] for k in range(d // LANE)]) / d
    dy = err / d
    dgain_ref[...] += _rows8(dy * xh)
    dxh = dy * gain_ref[...]
    return r * (dxh - xh * jnp.mean(dxh * xh, axis=-1, keepdims=True))


V7X_MXU_TILE = 256
FFN_CHUNK_TILES = 3


def _hidden_chunks(f):
    step = FFN_CHUNK_TILES * V7X_MXU_TILE
    return [slice(s, min(s + step, f)) for s in range(0, f, step)]


def _flat(w):
    return w.reshape(w.shape[0] * w.shape[1], w.shape[2])


def _ffn_fwd(x, gain, wg, wu, wd, name, rides=None, head=None):
    t, d = x.shape
    wg, wu, wd = _flat(wg), _flat(wu), _flat(wd)
    f = wg.shape[0]
    tm = min(512, t)
    nh = 0 if head is None else 2

    def body(*refs):
        x_ref, gain_ref = refs[:2]
        wg_hbm, wu_hbm, wd_hbm, h_ref, xn_ref, g_ref, u_ref, a_ref = refs[2 + nh:10 + nh]
        sums = refs[10 + nh:12 + nh]
        wg_v, wu_v, wd_v, sems = refs[-4:]

        @pl.when(pl.program_id(0) == 0)
        def _():
            _load_weights([(wg_hbm, wg_v), (wu_hbm, wu_v), (wd_hbm, wd_v)], sems)
            if head is not None:
                for s_ref in sums:
                    s_ref[...] = jnp.zeros_like(s_ref)

        xv = x_ref[...]
        r = lax.rsqrt(jnp.mean(xv * xv, axis=-1, keepdims=True) + NORM_EPS)
        xn = (xv * r * gain_ref[...]).astype(BF16)
        xn_ref[...] = xn
        acc = jnp.zeros((tm, d), F32)
        for c in _hidden_chunks(f):
            g = _dot(xn, wg_v[c, :], NT)
            u = _dot(xn, wu_v[c, :], NT)
            g_ref[:, c] = g.astype(BF16)
            u_ref[:, c] = u.astype(BF16)
            a = (g * _sigmoid(g) * u).astype(BF16)
            a_ref[:, c] = a
            acc = acc + _dot(a, wd_v[c, :])
        hv = xv + 0.5 * acc
        h_ref[...] = hv if head is None else _loss_head(hv, refs[2], refs[3], *sums)

    hbm = pl.BlockSpec(memory_space=pl.ANY)
    hid = pl.BlockSpec((tm, f), lambda i: (i, 0))
    tile = pl.BlockSpec((tm, d), lambda i: (i, 0))
    row = pl.BlockSpec((1, d), lambda i: (0, 0))
    sums = [] if head is None else [(pl.BlockSpec((8, LANE), lambda i: (0, 0)), jax.ShapeDtypeStruct((8, LANE), F32)),
                                    (pl.BlockSpec((8, d), lambda i: (0, 0)), jax.ShapeDtypeStruct((8, d), F32))]
    return _pallas(
        body, rides, name=name, grid=(t // tm,),
        in_specs=[tile, row] + ([] if head is None else [row, tile]) + [hbm, hbm, hbm],
        out_specs=[tile, tile, hid, hid, hid] + [s for s, _ in sums],
        out_shape=[jax.ShapeDtypeStruct((t, d), F32), jax.ShapeDtypeStruct((t, d), BF16)]
        + [jax.ShapeDtypeStruct((t, f), BF16)] * 3 + [s for _, s in sums],
        scratch_shapes=[pltpu.VMEM(wg.shape, BF16), pltpu.VMEM(wu.shape, BF16), pltpu.VMEM(wd.shape, BF16),
                        pltpu.SemaphoreType.DMA((3,))],
        sem=("arbitrary",), args=[x, gain] + ([] if head is None else list(head)) + [wg, wu, wd])


def _ffn_bwd_data(dy, x, gain, g, u, wg, wu, wd, name, rides=None):
    t, d = x.shape
    wg, wu, wd = _flat(wg), _flat(wu), _flat(wd)
    f = wg.shape[0]
    tm = min(256, t)

    def body(dy_ref, x_ref, gain_ref, g_ref, u_ref, wg_hbm, wu_hbm, wd_hbm, dx_ref, dg_ref, du_ref, dgain_ref,
             wg_v, wu_v, wd_v, sems):
        @pl.when(pl.program_id(0) == 0)
        def _():
            _load_weights([(wg_hbm, wg_v), (wu_hbm, wu_v), (wd_hbm, wd_v)], sems)
            dgain_ref[...] = jnp.zeros_like(dgain_ref)

        dyv = dy_ref[...]
        dyh = (0.5 * dyv).astype(BF16)
        dxn = jnp.zeros((tm, d), F32)
        chunks = _hidden_chunks(f)
        das = [_dot(dyh, wd_v[c, :], NT) for c in chunks]
        for c, da in zip(chunks, das):
            gj = g_ref[:, c].astype(F32)
            uj = u_ref[:, c].astype(F32)
            sig = _sigmoid(gj)
            dgj = (da * uj * (sig * (1.0 + gj * (1.0 - sig)))).astype(BF16)
            duj = (da * (gj * sig)).astype(BF16)
            dg_ref[:, c] = dgj
            du_ref[:, c] = duj
            dxn = dxn + _dot(dgj, wg_v[c, :]) + _dot(duj, wu_v[c, :])
        xv = x_ref[...]
        r = lax.rsqrt(jnp.mean(xv * xv, axis=-1, keepdims=True) + NORM_EPS)
        xh = xv * r
        dgain_ref[...] += _rows8(dxn * xh)
        dxh = dxn * gain_ref[...]
        dx_ref[...] = dyv + r * (dxh - xh * jnp.mean(dxh * xh, axis=-1, keepdims=True))

    hbm = pl.BlockSpec(memory_space=pl.ANY)
    tile = pl.BlockSpec((tm, d), lambda i: (i, 0))
    hid = pl.BlockSpec((tm, f), lambda i: (i, 0))
    return _pallas(
        body, rides, name=name, grid=(t // tm,),
        in_specs=[tile, tile, pl.BlockSpec((1, d), lambda i: (0, 0)), hid, hid, hbm, hbm, hbm],
        out_specs=[tile, hid, hid, pl.BlockSpec((8, d), lambda i: (0, 0))],
        out_shape=[jax.ShapeDtypeStruct((t, d), F32), jax.ShapeDtypeStruct((t, f), BF16),
                   jax.ShapeDtypeStruct((t, f), BF16), jax.ShapeDtypeStruct((8, d), F32)],
        scratch_shapes=[pltpu.VMEM(wg.shape, BF16), pltpu.VMEM(wu.shape, BF16), pltpu.VMEM(wd.shape, BF16),
                        pltpu.SemaphoreType.DMA((3,))],
        sem=("arbitrary",), args=[dy, x, gain, g, u, wg, wu, wd])


WGRAD_ROW_BLOCKS = 2


def _ffn_wgrad_down(a, dy, name, rides=None):
    t, d = dy.shape
    f = a.shape[1]
    fb = f // WGRAD_ROW_BLOCKS
    tk = min(1024, t)

    def body(dy_ref, a_ref, dwd_ref):
        @pl.when(pl.program_id(1) == 0)
        def _():
            dwd_ref[...] = jnp.zeros_like(dwd_ref)

        dwd_ref[...] += _dot(a_ref[...], (0.5 * dy_ref[...]).astype(BF16), TN)

    return _pallas(
        body, rides, name=name, grid=(WGRAD_ROW_BLOCKS, t // tk),
        in_specs=[pl.BlockSpec((tk, d), lambda j, k: (k, 0)), pl.BlockSpec((tk, fb), lambda j, k: (k, j))],
        out_specs=[pl.BlockSpec((fb, d), lambda j, k: (j, 0))],
        out_shape=[jax.ShapeDtypeStruct((f, d), F32)],
        sem=("arbitrary", "arbitrary"), args=[dy, a])


def _ffn_wgrad_gu(xn, dhs, name, rides=None):
    t, d = xn.shape
    n = len(dhs)
    f = dhs[0].shape[1]
    fb = f // WGRAD_ROW_BLOCKS
    tk = min(2048 // n, t)

    def body(xn_ref, *refs):
        @pl.when(pl.program_id(1) == 0)
        def _():
            for o_ref in refs[n:]:
                o_ref[...] = jnp.zeros_like(o_ref)

        xnv = xn_ref[...]
        for dh_ref, o_ref in zip(refs[:n], refs[n:]):
            o_ref[...] += _dot(dh_ref[...], xnv, TN)

    hid = pl.BlockSpec((tk, fb), lambda j, k: (k, j))
    out = pl.BlockSpec((fb, d), lambda j, k: (j, 0))
    return _pallas(
        body, rides, name=name, grid=(WGRAD_ROW_BLOCKS, t // tk),
        in_specs=[pl.BlockSpec((tk, d), lambda j, k: (k, 0))] + [hid] * n,
        out_specs=[out] * n, out_shape=[jax.ShapeDtypeStruct((f, d), F32)] * n,
        sem=("arbitrary", "arbitrary"), args=[xn] + list(dhs))


def _tn_matmul(a, b, bn, name):
    t, m = a.shape
    n = b.shape[1]
    tk = min(2048, t)

    def body(a_ref, b_ref, o_ref):
        @pl.when(pl.program_id(1) == 0)
        def _():
            o_ref[...] = jnp.zeros_like(o_ref)

        o_ref[...] += _dot(a_ref[...].astype(BF16), b_ref[...].astype(BF16), TN)

    return pl.pallas_call(
        body, name=name, grid=(n // bn, t // tk),
        in_specs=[pl.BlockSpec((tk, m), lambda j, k: (k, 0)), pl.BlockSpec((tk, bn), lambda j, k: (k, j))],
        out_specs=pl.BlockSpec((None, m, bn), lambda j, k: (j, 0, 0)),
        out_shape=jax.ShapeDtypeStruct((n // bn, m, bn), F32),
        compiler_params=_params("arbitrary", "arbitrary"),
    )(a, b)


def _chunk_scratch(tm, w):
    return pltpu.VMEM((w // LANE, tm, LANE), F32)


def _regroup_store(cbuf, out_ref, dil, chunks=None):
    n = out_ref.shape[1]
    for k in range(cbuf.shape[0]) if chunks is None else chunks:
        for g in range(dil):
            rows = cbuf[k] if dil == 1 else cbuf[k, pl.ds(g, n, stride=dil), :]
            out_ref[g, :, k * LANE:(k + 1) * LANE] = rows.astype(out_ref.dtype)


def _natural_rows(ref, dil, cbuf):
    if dil == 1:
        return ref[0].astype(F32)
    n = ref.shape[1]
    for g in range(dil):
        for k in range(cbuf.shape[0]):
            cbuf[k, pl.ds(g, n, stride=dil), :] = ref[g, :, k * LANE:(k + 1) * LANE].astype(F32)
    return jnp.concatenate([cbuf[k] for k in range(cbuf.shape[0])], axis=1)


IN_CHUNK_TILES = 4


def _column_chunks(n):
    step = IN_CHUNK_TILES * V7X_MXU_TILE
    return [slice(s, min(s + step, n)) for s in range(0, n, step)]


def _load_side_by_side(w_hbm, w_v, sems):
    ns, _, cs = w_hbm.shape
    copies = [pltpu.make_async_copy(w_hbm.at[j], w_v.at[:, pl.ds(j * cs, cs)], sems.at[j]) for j in range(ns)]
    for cp in copies:
        cp.start()
    for cp in copies:
        cp.wait()


def _inproj_fwd(h, gain, win):
    t, d = h.shape
    ns, _, cs = win.shape
    tm = min(512, t)
    rw, aw = 4 * RET_WIDTH, 3 * ATT_WIDTH

    def body(h_ref, gain_ref, w_hbm, xn_ref, ur_ref, *rest):
        a_refs, abuf, w_v, sems = rest[:-3], rest[-3], rest[-2], rest[-1]

        @pl.when(pl.program_id(0) == 0)
        def _():
            _load_side_by_side(w_hbm, w_v, sems)

        hv = h_ref[...]
        r = lax.rsqrt(jnp.mean(hv * hv, axis=-1, keepdims=True) + NORM_EPS)
        xn = (hv * r * gain_ref[...]).astype(BF16)
        xn_ref[...] = xn
        for c in reversed(_column_chunks(ns * cs)):
            res = _dot(xn, w_v[:, c])
            mine = []
            for k in range((c.stop - c.start) // LANE):
                chunk = c.start // LANE + k
                piece = res[:, k * LANE:(k + 1) * LANE]
                if chunk < rw // LANE:
                    ur_ref[:, chunk * LANE:(chunk + 1) * LANE] = piece
                else:
                    abuf[chunk - rw // LANE] = piece
                    mine.append(chunk - rw // LANE)
            for dil, a_ref in zip(DILATIONS, a_refs):
                _regroup_store(abuf, a_ref, dil, mine)

    return pl.pallas_call(
        body, name="inproj_fwd", grid=(t // tm,),
        in_specs=[pl.BlockSpec((tm, d), lambda i: (i, 0)), pl.BlockSpec((1, d), lambda i: (0, 0)),
                  pl.BlockSpec(memory_space=pl.ANY)],
        out_specs=[pl.BlockSpec((tm, d), lambda i: (i, 0)), pl.BlockSpec((tm, rw), lambda i: (i, 0))]
        + [pl.BlockSpec((dil, tm // dil, aw), lambda i: (0, i, 0)) for dil in DILATIONS],
        out_shape=[jax.ShapeDtypeStruct((t, d), BF16), jax.ShapeDtypeStruct((t, rw), F32)]
        + [jax.ShapeDtypeStruct((dil, t // dil, aw), BF16) for dil in DILATIONS],
        scratch_shapes=[_chunk_scratch(tm, aw), pltpu.VMEM((d, ns * cs), BF16), pltpu.SemaphoreType.DMA((ns,))],
        compiler_params=_params("arbitrary"),
    )(h, gain, win)


def _inproj_bwd(pieces, parts, h, gain, dres, win):
    t, d = h.shape
    ns, _, cs = win.shape
    pw = pieces[0].shape[1]
    tm = min(512, t)
    npc, nk = len(pieces), len(parts[0])
    flat_parts = [a for p in parts for a in p]

    def body(*refs):
        p_refs, a_refs = refs[:npc], refs[npc:npc + len(flat_parts)]
        h_ref, gain_ref, dres_ref, w_hbm, dh_ref, du_ref, dgain_ref, buf, w_v, sems = refs[npc + len(flat_parts):]

        @pl.when(pl.program_id(0) == 0)
        def _():
            _load_side_by_side(w_hbm, w_v, sems)
            dgain_ref[...] = jnp.zeros_like(dgain_ref)

        for k in range(npc):
            du_ref[:, k * pw:(k + 1) * pw] = p_refs[k][...]
        for k in range(nk):
            acc = None
            for b, dil in enumerate(DILATIONS):
                rows = _natural_rows(a_refs[b * nk + k], dil, buf)
                acc = rows if acc is None else acc + rows
            du_ref[:, (npc + k) * pw:(npc + k + 1) * pw] = acc.astype(BF16)
        dxn = jnp.zeros((tm, d), F32)
        for c in _column_chunks(ns * cs):
            dxn = dxn + _dot(du_ref[:, c], w_v[:, c], NT)
        hv = h_ref[...]
        r = lax.rsqrt(jnp.mean(hv * hv, axis=-1, keepdims=True) + NORM_EPS)
        xh = hv * r
        dgain_ref[...] += _rows8(dxn * xh)
        dxh = dxn * gain_ref[...]
        dh_ref[...] = dres_ref[...] + r * (dxh - xh * jnp.mean(dxh * xh, axis=-1, keepdims=True))

    tile = pl.BlockSpec((tm, d), lambda i: (i, 0))
    cols = (npc + nk) * pw
    return pl.pallas_call(
        body, name="inproj_bwd", grid=(t // tm,),
        in_specs=[pl.BlockSpec((tm, pw), lambda i: (i, 0))] * npc
        + [_regrouped_spec(tm, dil, pw) for dil in DILATIONS for _ in range(nk)]
        + [tile, pl.BlockSpec((1, d), lambda i: (0, 0)), tile, pl.BlockSpec(memory_space=pl.ANY)],
        out_specs=[tile, pl.BlockSpec((tm, cols), lambda i: (i, 0)), pl.BlockSpec((8, d), lambda i: (0, 0))],
        out_shape=[jax.ShapeDtypeStruct((t, d), F32), jax.ShapeDtypeStruct((t, cols), BF16),
                   jax.ShapeDtypeStruct((8, d), F32)],
        scratch_shapes=[_chunk_scratch(tm, pw), pltpu.VMEM((d, ns * cs), BF16), pltpu.SemaphoreType.DMA((ns,))],
        compiler_params=_params("arbitrary"),
    )(*pieces, *flat_parts, h, gain, dres, win)


def _outproj_fwd(h, mix_r, mix_a, wo):
    t, d = h.shape
    hw = mix_r.shape[1]
    tm = min(512, t)

    def body(h_ref, mr_ref, ma_ref, w_ref, o_ref):
        o_ref[...] = h_ref[...] + _dot(mr_ref[...], w_ref[0:hw, :]) + _dot(ma_ref[...], w_ref[hw:2 * hw, :])

    tile = pl.BlockSpec((tm, d), lambda i: (i, 0))
    half = pl.BlockSpec((tm, hw), lambda i: (i, 0))
    return pl.pallas_call(
        body, name="outproj_fwd", grid=(t // tm,),
        in_specs=[tile, half, half, pl.BlockSpec(wo.shape, lambda i: (0, 0))],
        out_specs=tile, out_shape=jax.ShapeDtypeStruct((t, d), F32),
        compiler_params=_params("arbitrary"),
    )(h, mix_r, mix_a, wo)


def _outproj_bwd(dh, wo, rides=None):
    t, d = dh.shape
    hw = wo.shape[0] // 2
    tm = min(512, t)

    def body(dh_ref, w_ref, dr_ref, da_ref):
        dhb = dh_ref[...].astype(BF16)
        dr_ref[...] = _dot(dhb, w_ref[0:hw, :], NT)
        da_ref[...] = _dot(dhb, w_ref[hw:2 * hw, :], NT)

    half = pl.BlockSpec((tm, hw), lambda i: (i, 0))
    return _pallas(
        body, rides, name="outproj_bwd", grid=(t // tm,),
        in_specs=[pl.BlockSpec((tm, d), lambda i: (i, 0)), pl.BlockSpec(wo.shape, lambda i: (0, 0))],
        out_specs=[half, half],
        out_shape=[jax.ShapeDtypeStruct((t, hw), F32), jax.ShapeDtypeStruct((t, hw), F32)],
        sem=("arbitrary",), args=[dh, wo])


def _retention_tables(t):
    pos = jnp.arange(t, dtype=F32)
    pair = (jnp.arange(RET_DIM) // 2 * 2).astype(F32)
    ang = pos[:, None] * (ROPE_BASE ** (-pair / RET_DIM))[None, :]
    c = RET_CHUNK
    log_g = jnp.log(1.0 - 2.0 ** (-5.0 - jnp.arange(RET_HEADS, dtype=F32)))
    idx = jnp.arange(c, dtype=F32)
    rel = idx[:, None] - idx[None, :]
    decay = jnp.where(rel >= 0, jnp.exp(log_g[:, None, None] * jnp.maximum(rel, 0.0)), 0.0)
    zeta = jnp.exp(log_g[:, None] * (c - 1 - idx)[None, :])
    xi = jnp.exp(log_g[:, None] * (idx + 1)[None, :])
    gc = jnp.exp(log_g * c)
    wide = lambda v: jnp.broadcast_to(v[:, :, None], (RET_HEADS, c, LANE))
    return (jnp.cos(ang), jnp.sin(ang), decay, wide(zeta), wide(xi),
            jnp.broadcast_to(gc[:, None, None], (RET_HEADS, c, LANE)))


def _rot(v):
    lane = lax.broadcasted_iota(jnp.int32, v.shape, 1)
    nxt = pltpu.roll(v, LANE - 1, 1)
    prv = pltpu.roll(v, 1, 1)
    return jnp.where(lane % 2 == 0, -nxt, prv)


def _ret_specs(tr, rev, nt):
    ti = (lambda i: nt - 1 - i) if rev else (lambda i: i)
    col = lambda blk: pl.BlockSpec((tr, RET_WIDTH), lambda i: (ti(i), blk))
    tab = pl.BlockSpec((tr, LANE), lambda i: (ti(i), 0))
    head = pl.BlockSpec((RET_HEADS, RET_CHUNK, LANE), lambda i: (0, 0, 0))
    return col, tab, head


def _ret_chunks(tr, rev=False):
    order = list(range(tr // RET_CHUNK))
    return [(pl.ds(ci * RET_CHUNK, RET_CHUNK), slice(h * RET_DIM, (h + 1) * RET_DIM), h)
            for h in range(RET_HEADS) for ci in (reversed(order) if rev else order)]


def _ret_operands(items, q_ref, k_ref, v_ref, cos_ref, sin_ref, zeta_ref):
    scale = RET_DIM ** -0.5
    qbs, kbs, vbs, kzs = [], [], [], []
    for sl, hs, h in items:
        cs, sn = cos_ref[sl, :], sin_ref[sl, :]
        q, k = q_ref[sl, hs], k_ref[sl, hs]
        kr = (k * cs + _rot(k) * sn) * scale
        qbs.append((q * cs + _rot(q) * sn).astype(BF16))
        kbs.append(kr.astype(BF16))
        vbs.append(v_ref[sl, hs].astype(BF16))
        kzs.append((kr * zeta_ref[h]).astype(BF16))
    return qbs, kbs, vbs, kzs


def _ret_states(items, state, steps, gc_ref):
    cur, befores = {}, []
    for (sl, hs, h), step in zip(items, steps):
        st = cur[h] if h in cur else state[h]
        befores.append(st)
        cur[h] = st * gc_ref[h] + step
    for h, st in cur.items():
        state[h] = st
    return befores


def _ret_fwd(u, gain, tabs):
    t = u.shape[0]
    tr = min(512, t)
    nt = t // tr
    cos, sin, decay, zeta, xi, gc = tabs

    def body(q_ref, k_ref, v_ref, gt_ref, cos_ref, sin_ref, gain_ref, dec_ref, zeta_ref, xi_ref, gc_ref,
             raw_ref, mix_ref, state):
        @pl.when(pl.program_id(0) == 0)
        def _():
            state[...] = jnp.zeros_like(state)

        items = _ret_chunks(tr)
        n = range(len(items))
        qbs, kbs, vbs, kzs = _ret_operands(items, q_ref, k_ref, v_ref, cos_ref, sin_ref, zeta_ref)
        ss = [_dot(qbs[i], kbs[i], NT) for i in n]
        kvs = [_dot(kzs[i], vbs[i], TN) for i in n]
        befores = _ret_states(items, state, kvs, gc_ref)
        intra = [_dot((ss[i] * dec_ref[items[i][2]]).astype(BF16), vbs[i]) for i in n]
        inter = [_dot(qbs[i], befores[i].astype(BF16)) for i in n]
        for i, (sl, hs, h) in enumerate(items):
            o = intra[i] + inter[i] * xi_ref[h]
            raw_ref[sl, hs] = o
            mu = jnp.mean(o, axis=-1, keepdims=True)
            var = jnp.mean(jnp.square(o - mu), axis=-1, keepdims=True)
            y = (o - mu) * lax.rsqrt(var + GN_EPS) * gain_ref[:, hs]
            gt = gt_ref[sl, hs]
            mix_ref[sl, hs] = (y * (gt * _sigmoid(gt))).astype(BF16)

    col, tab, head = _ret_specs(tr, False, nt)
    out = pl.BlockSpec((tr, RET_WIDTH), lambda i: (i, 0))
    return pl.pallas_call(
        body, name="ret_fwd", grid=(nt,),
        in_specs=[col(0), col(1), col(2), col(3), tab, tab, pl.BlockSpec((1, RET_WIDTH), lambda i: (0, 0)),
                  head, head, head, head],
        out_specs=[out, out],
        out_shape=[jax.ShapeDtypeStruct((t, RET_WIDTH), F32), jax.ShapeDtypeStruct((t, RET_WIDTH), BF16)],
        scratch_shapes=[pltpu.VMEM((RET_HEADS, RET_DIM, RET_DIM), F32)],
        compiler_params=_params("arbitrary"),
    )(u, u, u, u, cos, sin, gain, decay, zeta, xi, gc)


def _ret_bwd_q(dmix, raw, u, gain, tabs, rides=None):
    t = u.shape[0]
    tr = min(512, t)
    nt = t // tr
    cos, sin, decay, zeta, xi, gc = tabs

    def body(dm_ref, raw_ref, q_ref, k_ref, v_ref, gt_ref, cos_ref, sin_ref, gain_ref, dec_ref, zeta_ref, xi_ref, gc_ref,
             dq_ref, dgt_ref, dret_ref, dgain_ref, state):
        @pl.when(pl.program_id(0) == 0)
        def _():
            state[...] = jnp.zeros_like(state)
            dgain_ref[...] = jnp.zeros_like(dgain_ref)

        items = _ret_chunks(tr)
        n_items = range(len(items))
        qbs, kbs, vbs, kzs = _ret_operands(items, q_ref, k_ref, v_ref, cos_ref, sin_ref, zeta_ref)
        dos, dgains = [], {}
        for sl, hs, h in items:
            o = raw_ref[sl, hs]
            mu = jnp.mean(o, axis=-1, keepdims=True)
            var = jnp.mean(jnp.square(o - mu), axis=-1, keepdims=True)
            rs = lax.rsqrt(var + GN_EPS)
            n = (o - mu) * rs
            gt = gt_ref[sl, hs]
            sig = _sigmoid(gt)
            dout = dm_ref[sl, hs]
            gain_h = gain_ref[:, hs]
            dgt_ref[sl, hs] = (dout * (n * gain_h) * (sig * (1.0 + gt * (1.0 - sig)))).astype(BF16)
            dy = dout * (gt * sig)
            dgains[h] = dgains[h] + _rows8(dy * n) if h in dgains else _rows8(dy * n)
            dn = dy * gain_h
            do = rs * (dn - jnp.mean(dn, axis=-1, keepdims=True) - n * jnp.mean(dn * n, axis=-1, keepdims=True))
            dret_ref[sl, hs] = do
            dos.append(do)
        for h, dg in dgains.items():
            dgain_ref[:, h * RET_DIM:(h + 1) * RET_DIM] += dg
        dss = [_dot(dos[i].astype(BF16), vbs[i], NT) for i in n_items]
        kvs = [_dot(kzs[i], vbs[i], TN) for i in n_items]
        befores = _ret_states(items, state, kvs, gc_ref)
        intra = [_dot((dss[i] * dec_ref[items[i][2]]).astype(BF16), kbs[i]) for i in n_items]
        inter = [_dot((dos[i] * xi_ref[items[i][2]]).astype(BF16), befores[i].astype(BF16), NT) for i in n_items]
        for i, (sl, hs, h) in enumerate(items):
            dqr = intra[i] + inter[i]
            dq_ref[sl, hs] = (dqr * cos_ref[sl, :] - _rot(dqr * sin_ref[sl, :])).astype(BF16)

    col, tab, head = _ret_specs(tr, False, nt)
    out = pl.BlockSpec((tr, RET_WIDTH), lambda i: (i, 0))
    return _pallas(
        body, rides, name="ret_bwd_q", grid=(nt,),
        in_specs=[out, out, col(0), col(1), col(2), col(3), tab, tab, pl.BlockSpec((1, RET_WIDTH), lambda i: (0, 0)),
                  head, head, head, head],
        out_specs=[out, out, out, pl.BlockSpec((8, RET_WIDTH), lambda i: (0, 0))],
        out_shape=[jax.ShapeDtypeStruct((t, RET_WIDTH), BF16), jax.ShapeDtypeStruct((t, RET_WIDTH), BF16),
                   jax.ShapeDtypeStruct((t, RET_WIDTH), F32), jax.ShapeDtypeStruct((8, RET_WIDTH), F32)],
        scratch_shapes=[pltpu.VMEM((RET_HEADS, RET_DIM, RET_DIM), F32)],
        sem=("arbitrary",), args=[dmix, raw, u, u, u, u, cos, sin, gain, decay, zeta, xi, gc])


def _ret_bwd_kv(dret, u, tabs, rides=None):
    t = u.shape[0]
    tr = min(512, t)
    nt = t // tr
    cos, sin, decay, zeta, xi, gc = tabs
    scale = RET_DIM ** -0.5

    def body(do_ref, q_ref, k_ref, v_ref, cos_ref, sin_ref, dec_ref, zeta_ref, xi_ref, gc_ref, dk_ref, dv_ref, gst):
        @pl.when(pl.program_id(0) == 0)
        def _():
            gst[...] = jnp.zeros_like(gst)

        items = _ret_chunks(tr, rev=True)
        n = range(len(items))
        qbs, kbs, vbs, kzs = _ret_operands(items, q_ref, k_ref, v_ref, cos_ref, sin_ref, zeta_ref)
        dos = [do_ref[sl, hs] for sl, hs, h in items]
        dobs = [do.astype(BF16) for do in dos]
        ss = [_dot(qbs[i], kbs[i], NT) for i in n]
        dss = [_dot(dobs[i], vbs[i], NT) for i in n]
        steps = [_dot(qbs[i], (dos[i] * xi_ref[items[i][2]]).astype(BF16), TN) for i in n]
        afters = [g.astype(BF16) for g in _ret_states(items, gst, steps, gc_ref)]
        dvs = [_dot((ss[i] * dec_ref[items[i][2]]).astype(BF16), dobs[i], TN) + _dot(kzs[i], afters[i]) for i in n]
        dks = [_dot((dss[i] * dec_ref[items[i][2]]).astype(BF16), qbs[i], TN) for i in n]
        dkz = [_dot(vbs[i], afters[i], NT) for i in n]
        for i, (sl, hs, h) in enumerate(items):
            dv_ref[sl, hs] = dvs[i].astype(BF16)
            dkr = (dks[i] + dkz[i] * zeta_ref[h]) * scale
            dk_ref[sl, hs] = (dkr * cos_ref[sl, :] - _rot(dkr * sin_ref[sl, :])).astype(BF16)

    col, tab, head = _ret_specs(tr, True, nt)
    out = pl.BlockSpec((tr, RET_WIDTH), lambda i: (nt - 1 - i, 0))
    return _pallas(
        body, rides, name="ret_bwd_kv", grid=(nt,),
        in_specs=[out, col(0), col(1), col(2), tab, tab, head, head, head, head],
        out_specs=[out, out],
        out_shape=[jax.ShapeDtypeStruct((t, RET_WIDTH), BF16), jax.ShapeDtypeStruct((t, RET_WIDTH), BF16)],
        scratch_shapes=[pltpu.VMEM((RET_HEADS, RET_DIM, RET_DIM), F32)],
        sem=("arbitrary",), args=[dret, u, u, u, cos, sin, decay, zeta, xi, gc])


PAIRS = ATT_WIDTH // LANE
ATT_Q_BLK, ATT_K_BLK, ATT_V_BLK = 0, PAIRS, 2 * PAIRS
STAT_LANES = ATT_DIM // 2


ATT_STEP_ROWS = 2048


def _att_tiles(t, dil):
    sub = t // dil
    tq = min(ATT_STEP_ROWS, sub)
    return sub, tq, sub // tq, tq // ATT_BLOCK, min(dil, ATT_STEP_ROWS // tq)


def _att_in_specs(tq, qb, ti, gs):
    cur = lambda off: pl.BlockSpec((gs, tq, LANE), lambda g, p, i: (g, ti(i), off + p))
    prev = lambda off: pl.BlockSpec((gs, ATT_BLOCK, LANE), lambda g, p, i: (g, jnp.maximum(ti(i) * qb - 1, 0), off + p))
    return [cur(ATT_Q_BLK), cur(ATT_K_BLK), prev(ATT_K_BLK), cur(ATT_V_BLK), prev(ATT_V_BLK)]


def _band_mask():
    key = lax.broadcasted_iota(jnp.int32, (2 * ATT_BLOCK, 2 * ATT_BLOCK), 0)
    qry = lax.broadcasted_iota(jnp.int32, (2 * ATT_BLOCK, 2 * ATT_BLOCK), 1) % ATT_BLOCK
    dist = qry + ATT_BLOCK - key
    return (dist >= 0) & (dist <= ATT_BLOCK), key >= ATT_BLOCK


def _head0_lanes():
    return lax.broadcasted_iota(jnp.int32, (ATT_BLOCK, LANE), 1) < ATT_DIM


def _stack_heads(v, head0):
    zero = jnp.zeros((), v.dtype)
    return jnp.concatenate([jnp.where(head0, v, zero), jnp.where(head0, zero, v)], axis=0)


def _unstack_heads(v, head0):
    return jnp.where(head0, v[0:ATT_BLOCK], v[ATT_BLOCK:])


def _att_fwd(ua, dil):
    sub = ua.shape[1]
    _, tq, nq, qb, gs = _att_tiles(sub * dil, dil)

    def body(q_ref, kc_ref, kp_ref, vc_ref, vp_ref, o_ref, l_ref, kx, vx):
        tile = pl.program_id(2)
        kx[:, 0:ATT_BLOCK, :] = kp_ref[...]
        kx[:, ATT_BLOCK:, :] = kc_ref[...]
        vx[:, 0:ATT_BLOCK, :] = vp_ref[...]
        vx[:, ATT_BLOCK:, :] = vc_ref[...]
        band, cur_keys = _band_mask()
        head0 = _head0_lanes()
        items = [(r, b) for r in range(gs) for b in range(qb)]
        rows = lambda b: slice(b * ATT_BLOCK, (b + 1) * ATT_BLOCK)
        keys = lambda b: slice(b * ATT_BLOCK, (b + 2) * ATT_BLOCK)
        sts = [_dot(kx[r, keys(b), :], _stack_heads(q_ref[r, rows(b), :] * jnp.asarray(ATT_DIM ** -0.5, BF16), head0), NT)
               for r, b in items]
        pts, lses = [], []
        for (r, b), st in zip(items, sts):
            mask = band if b > 0 else band & (cur_keys | (tile > 0))
            st = jnp.where(mask, st, -1e30)
            m = jnp.max(st, axis=0, keepdims=True)
            ex = jnp.exp(st - m)
            den = jnp.sum(ex, axis=0, keepdims=True)
            pts.append((ex * (1.0 / den)).astype(BF16))
            lses.append(m + jnp.log(den))
        outs = [_dot(pt, vx[r, keys(b), :], TN) for (r, b), pt in zip(items, pts)]
        for (r, b), out, lse in zip(items, outs, lses):
            o_ref[r, rows(b), :] = _unstack_heads(out, head0).astype(BF16)
            cols = [jnp.broadcast_to(lse[:, e * ATT_BLOCK:(e + 1) * ATT_BLOCK], (ATT_BLOCK, LANE)).T for e in range(2)]
            l_ref[r, rows(b), :] = jnp.where(head0, cols[0], cols[1])

    out = pl.BlockSpec((gs, tq, LANE), lambda g, p, i: (g, i, p))
    return pl.pallas_call(
        body, name=f"att_fwd_d{dil}", grid=(dil // gs, PAIRS, nq),
        in_specs=_att_in_specs(tq, qb, lambda i: i, gs),
        out_specs=[out, out],
        out_shape=[jax.ShapeDtypeStruct((dil, sub, ATT_WIDTH), BF16), jax.ShapeDtypeStruct((dil, sub, ATT_WIDTH), F32)],
        scratch_shapes=[pltpu.VMEM((gs, tq + ATT_BLOCK, LANE), BF16)] * 2,
        compiler_params=_params("arbitrary", "arbitrary", "arbitrary"),
    )(ua, ua, ua, ua, ua)


def _regrouped_spec(tm, dil, w):
    return pl.BlockSpec((dil, tm // dil, w), lambda i: (0, i, 0))


def _att_combine(outs, lses, t):
    w = ATT_WIDTH
    tm = min(512, t)
    nb = len(outs)

    def body(*refs):
        o_refs, l_refs = refs[:nb], refs[nb:2 * nb]
        mix_ref, att_ref, lse_ref, buf = refs[2 * nb:]
        ls = [_natural_rows(r, dil, buf) for r, dil in zip(l_refs, DILATIONS)]
        m = functools.reduce(jnp.maximum, ls)
        ws = [jnp.exp(l - m) for l in ls]
        den = functools.reduce(jnp.add, ws)
        att = functools.reduce(jnp.add, [(wt / den) * _natural_rows(r, dil, buf) for wt, r, dil in zip(ws, o_refs, DILATIONS)])
        att_ref[...] = att
        mix_ref[...] = att.astype(BF16)
        lse_ref[...] = m + jnp.log(den)

    tile = pl.BlockSpec((tm, w), lambda i: (i, 0))
    regrouped = [_regrouped_spec(tm, dil, w) for dil in DILATIONS]
    return pl.pallas_call(
        body, name="att_combine", grid=(t // tm,),
        in_specs=regrouped * 2, out_specs=[tile, tile, tile],
        out_shape=[jax.ShapeDtypeStruct((t, w), BF16), jax.ShapeDtypeStruct((t, w), F32), jax.ShapeDtypeStruct((t, w), F32)],
        scratch_shapes=[_chunk_scratch(tm, w)],
        compiler_params=_params("arbitrary"),
    )(*outs, *lses)


def _att_bwd_prep(datt, att, lse):
    t, w = datt.shape
    tm = min(512, t)

    def body(da_ref, at_ref, l_ref, *rest):
        outs, dbuf, sbuf = rest[:-2], rest[-2], rest[-1]
        dav = da_ref[...]
        prod = dav * at_ref[...]
        lane = lax.broadcasted_iota(jnp.int32, (tm, LANE), 1)
        for k in range(w // LANE):
            cols = slice(k * LANE, (k + 1) * LANE)
            dbuf[k] = dav[:, cols]
            delta = jnp.concatenate(
                [jnp.broadcast_to(jnp.sum(prod[:, k * LANE + e * ATT_DIM:k * LANE + (e + 1) * ATT_DIM], axis=-1, keepdims=True),
                                  (tm, ATT_DIM)) for e in range(LANE // ATT_DIM)], axis=1)
            sbuf[k] = jnp.where(lane % ATT_DIM < STAT_LANES, l_ref[:, cols], delta)
        for k, dil in enumerate(DILATIONS):
            _regroup_store(dbuf, outs[2 * k], dil)
            _regroup_store(sbuf, outs[2 * k + 1], dil)

    tile = pl.BlockSpec((tm, w), lambda i: (i, 0))
    res = pl.pallas_call(
        body, name="att_bwd_prep", grid=(t // tm,),
        in_specs=[tile] * 3,
        out_specs=[_regrouped_spec(tm, dil, w) for dil in DILATIONS for _ in range(2)],
        out_shape=[jax.ShapeDtypeStruct((dil, t // dil, w), dt) for dil in DILATIONS for dt in (BF16, F32)],
        scratch_shapes=[_chunk_scratch(tm, w)] * 2,
        compiler_params=_params("arbitrary"),
    )(datt, att, lse)
    return [(res[2 * k], res[2 * k + 1]) for k in range(len(DILATIONS))]


def _att_bwd(ua, da, stat, dil, rides=None):
    sub = ua.shape[1]
    _, tq, nq, qb, gs = _att_tiles(sub * dil, dil)
    scale = ATT_DIM ** -0.5

    def body(q_ref, kc_ref, kp_ref, vc_ref, vp_ref, da_ref, st_ref, dq_ref, dk_ref, dv_ref, kx, vx, ck, cv):
        step = pl.program_id(2)
        tile = nq - 1 - step

        @pl.when(step == 0)
        def _():
            ck[...] = jnp.zeros_like(ck)
            cv[...] = jnp.zeros_like(cv)

        kx[:, 0:ATT_BLOCK, :] = kp_ref[...]
        kx[:, ATT_BLOCK:, :] = kc_ref[...]
        vx[:, 0:ATT_BLOCK, :] = vp_ref[...]
        vx[:, ATT_BLOCK:, :] = vc_ref[...]
        band, cur_keys = _band_mask()
        head0 = _head0_lanes()
        items = [(r, b) for r in range(gs) for b in range(qb)]
        n = range(len(items))
        rows = lambda b: slice(b * ATT_BLOCK, (b + 1) * ATT_BLOCK)
        keys = lambda b: slice(b * ATT_BLOCK, (b + 2) * ATT_BLOCK)
        qqs = [_stack_heads(q_ref[r, rows(b), :] * jnp.asarray(scale, BF16), head0) for r, b in items]
        dds = [_stack_heads(da_ref[r, rows(b), :], head0) for r, b in items]
        sts = [_dot(kx[r, keys(b), :], qqs[i], NT) for i, (r, b) in enumerate(items)]
        dpts = [_dot(vx[r, keys(b), :], dds[i], NT) for i, (r, b) in enumerate(items)]
        pts, dsts = [], []
        for i, (r, b) in enumerate(items):
            mask = band if b > 0 else band & (cur_keys | (tile > 0))
            stat = st_ref[r, rows(b), :].T
            row = lambda k: jnp.concatenate([stat[e * ATT_DIM + k:e * ATT_DIM + k + 1, :] for e in range(2)], axis=1)
            pt = jnp.where(mask, jnp.exp(sts[i] - row(0)), 0.0)
            dsts.append((pt * (dpts[i] - row(STAT_LANES))).astype(BF16))
            pts.append(pt.astype(BF16))
        dqs = [_dot(dsts[i], kx[r, keys(b), :], TN) for i, (r, b) in enumerate(items)]
        dkbs = [_dot(dsts[i], qqs[i]) for i in n]
        dvbs = [_dot(pts[i], dds[i]) for i in n]
        for i, (r, b) in enumerate(items):
            dq_ref[r, rows(b), :] = (_unstack_heads(dqs[i], head0) * scale).astype(BF16)
            if b > 0:
                dk_ref[r, rows(b - 1), :] = (dkbs[i - 1][ATT_BLOCK:] + dkbs[i][0:ATT_BLOCK]).astype(BF16)
                dv_ref[r, rows(b - 1), :] = (dvbs[i - 1][ATT_BLOCK:] + dvbs[i][0:ATT_BLOCK]).astype(BF16)
        for r in range(gs):
            first, last = r * qb, r * qb + qb - 1
            dk_ref[r, rows(qb - 1), :] = (dkbs[last][ATT_BLOCK:] + ck[r]).astype(BF16)
            dv_ref[r, rows(qb - 1), :] = (dvbs[last][ATT_BLOCK:] + cv[r]).astype(BF16)
            ck[r] = dkbs[first][0:ATT_BLOCK]
            cv[r] = dvbs[first][0:ATT_BLOCK]

    ti = lambda i: nq - 1 - i
    out = pl.BlockSpec((gs, tq, LANE), lambda g, p, i: (g, ti(i), p))
    shape = jax.ShapeDtypeStruct((dil, sub, ATT_WIDTH), BF16)
    return _pallas(
        body, rides, name=f"att_bwd_d{dil}", grid=(dil // gs, PAIRS, nq),
        in_specs=_att_in_specs(tq, qb, ti, gs) + [out, out],
        out_specs=[out, out, out], out_shape=[shape] * 3,
        scratch_shapes=[pltpu.VMEM((gs, tq + ATT_BLOCK, LANE), BF16)] * 2 + [pltpu.VMEM((gs, ATT_BLOCK, LANE), F32)] * 2,
        sem=("arbitrary", "arbitrary", "arbitrary"), args=[ua, ua, ua, ua, ua, da, stat])


class _Reduction:
    def __init__(self, place, names, grads):
        self.place, self.names, self.grads = place, names, grads

    def pair(self):
        return _pair_ride(self.grads)

    def chips(self, got):
        self.got = got
        return _chip_ride([_pair_sum(self.place, g, r, f"pair_sum_{n}") for g, r, n in zip(self.grads, got, self.names)])

    def halves(self, others):
        return [_chip_sum(self.place, g, r, o, f"chip_sum_{n}")
                for g, r, o, n in zip(self.grads, self.got, others, self.names)]


def _step(x, target, gains, w, place=None):
    t = x.shape[0]
    ex = place is not None
    g_ffn1, g_mix, g_ret, g_ffn2, g_fin = gains
    w = list(w)
    tabs = _retention_tables(t)
    red = lambda names, grads: _Reduction(place, names, grads) if ex else None
    ride = lambda r: [r] if ex else None

    if ex:
        w[0:3] = _run(_gather_ride(w[0:3]), "gather_ffn1_weights")
    (h1, xn1, *hid1, act1), rest = _ffn_fwd(x, g_ffn1, *w[0:3], "ffn1_fwd", ride(_gather_ride(w[3:])) if ex else None)
    if ex:
        w[3:] = rest[0]
    wg1, wu1, wd1, win, wo, wg2, wu2, wd2 = w
    wo2 = wo.reshape(wo.shape[0] * wo.shape[1], wo.shape[2])
    xnm, u, *uas = _inproj_fwd(h1, g_mix, win)
    raw, mix_r = _ret_fwd(u, g_ret, tabs)
    branches = [_att_fwd(ua, dil) for ua, dil in zip(uas, DILATIONS)]
    mix_a, att, lse = _att_combine([b[0] for b in branches], [b[1] for b in branches], t)
    h2 = _outproj_fwd(h1, mix_r, mix_a, wo2)
    (dh3, xn2, *hid2, act2, loss_p, dg_fin), _ = _ffn_fwd(h2, g_ffn2, wg2, wu2, wd2, "ffn2_fwd", head=(g_fin, target))

    (dwd2,), _ = _ffn_wgrad_down(act2, dh3, "ffn2_wgrad_down")
    dwd2 = dwd2.reshape(wd2.shape)
    r_d2 = red(["ffn2_w_down"], [dwd2])
    (dh2, dga2, dua2, dg_ffn2), e = _ffn_bwd_data(dh3, h2, g_ffn2, *hid2, wg2, wu2, wd2, "ffn2_bwd",
                                                  ex and [r_d2.pair()])
    (dwg2, dwu2), e = _ffn_wgrad_gu(xn2, [dga2, dua2], "ffn2_wgrad_gu", ex and [r_d2.chips(e[0])])
    dwg2, dwu2 = dwg2.reshape(wg2.shape), dwu2.reshape(wu2.shape)
    r_gu2 = red(["ffn2_w_gate", "ffn2_w_up"], [dwg2, dwu2])
    (dmix_r, dmix_a), e = _outproj_bwd(dh2, wo2, ex and [r_gu2.pair(), _finish_ride(r_d2.halves(e[0]))])
    if ex:
        got_gu2, (dwd2,) = e
    hw = RET_WIDTH // (wo.shape[1])
    dwo = jnp.concatenate([_tn_matmul(mix_r, dh2, dh2.shape[1], "wo_grad_r").reshape(hw, wo.shape[1], wo.shape[2]),
                           _tn_matmul(mix_a, dh2, dh2.shape[1], "wo_grad_a").reshape(hw, wo.shape[1], wo.shape[2])])
    r_wo = red(["w_out"], [dwo])
    (dq_r, dgt_r, dret, dg_ret), e = _ret_bwd_q(dmix_r, raw, u, g_ret, tabs, ex and [r_gu2.chips(got_gu2)])
    (dk_r, dv_r), e = _ret_bwd_kv(dret, u, tabs, ex and [r_wo.pair(), _finish_ride(r_gu2.halves(e[0]))])
    if ex:
        got_wo, (dwg2, dwu2) = e
    prep = _att_bwd_prep(dmix_a, att, lse)
    p1, e = _att_bwd(uas[0], *prep[0], DILATIONS[0], ex and [r_wo.chips(got_wo)])
    p4, e = _att_bwd(uas[1], *prep[1], DILATIONS[1], ex and [_finish_ride(r_wo.halves(e[0]))])
    if ex:
        (dwo,), = e
    p16, _ = _att_bwd(uas[2], *prep[2], DILATIONS[2])
    dh1, du, dg_mix = _inproj_bwd([dq_r, dk_r, dv_r, dgt_r], [p1, p4, p16], h1, g_mix, dh2, win)
    dwin = _tn_matmul(xnm, du, win.shape[2], "win_grad")
    r_in = red(["w_in"], [dwin])
    (dwd1,), e = _ffn_wgrad_down(act1, dh1, "ffn1_wgrad_down", ex and [r_in.pair()])
    dwd1 = dwd1.reshape(wd1.shape)
    r_d1 = red(["ffn1_w_down"], [dwd1])
    got_in = e
    (dx, dga1, dua1, dg_ffn1), _ = _ffn_bwd_data(dh1, x, g_ffn1, *hid1, wg1, wu1, wd1, "ffn1_bwd")
    (dwg1,), e = _ffn_wgrad_gu(xn1, [dga1], "ffn1_wgrad_gate", ex and [r_in.chips(got_in[0]), r_d1.pair()])
    dwg1 = dwg1.reshape(wg1.shape)
    if ex:
        oth_in, got_d1 = e
        r_g1 = red(["ffn1_w_gate"], [dwg1])
        got_g1 = _run(r_g1.pair(), "pair_exchange_ffn1_gate")
    (dwu1,), e = _ffn_wgrad_gu(xn1, [dua1], "ffn1_wgrad_up",
                               ex and [_finish_ride(r_in.halves(oth_in)), r_d1.chips(got_d1), r_g1.chips(got_g1)])
    dwu1 = dwu1.reshape(wu1.shape)
    gain_parts = [dg_ffn1, dg_mix, dg_ret, dg_ffn2, dg_fin]
    if not ex:
        return loss_p, dx, [dwg1, dwu1, dwd1, dwin, dwo, dwg2, dwu2, dwd2], gain_parts
    (dwin,), oth_d1, oth_g1 = e
    r_u1 = red(["ffn1_w_up"], [dwu1])
    got_u1 = _run(r_u1.pair(), "pair_exchange_ffn1_up")
    oth_u1 = _run(r_u1.chips(got_u1), "chip_exchange_ffn1_up")
    last = r_g1.halves(oth_g1) + r_u1.halves(oth_u1) + r_d1.halves(oth_d1)
    dwg1, dwu1, dwd1, gall = _run(_finish_ride(last, _pack_gains(gain_parts, x.shape[1])), "finish_exchange_ffn1")
    return loss_p, dx, [dwg1, dwu1, dwd1, dwin, dwo, dwg2, dwu2, dwd2], gall


N_DEV = 8
GAIN_ROWS = 8


def _place():
    x, y, c = lax.axis_index("x"), lax.axis_index("y"), lax.axis_index("c")
    chips = [(1 - x, y), (x, 1 - y), (1 - x, 1 - y)]
    return x, y, c, chips


ROW_QUARTERS = 4


def _place_shards(place, ws):
    n = len(ws)

    def body(place_ref, *refs):
        for w_ref, o_ref in zip(refs[:n], refs[n:]):
            o_ref[...] = w_ref[...].astype(BF16)

    quarter = lambda w: (w.shape[0] // ROW_QUARTERS, w.shape[1])
    return pl.pallas_call(
        body, name="place_shards",
        grid_spec=pltpu.PrefetchScalarGridSpec(
            num_scalar_prefetch=1, grid=(ROW_QUARTERS,),
            in_specs=[pl.BlockSpec(quarter(w), lambda i, pr: (i, 0)) for w in ws],
            out_specs=[pl.BlockSpec((None,) + quarter(w), lambda i, pr: (pr[0], i, 0)) for w in ws]),
        out_shape=[jax.ShapeDtypeStruct((N_SHARD,) + w.shape, BF16) for w in ws],
        compiler_params=_params("arbitrary"),
    )(place, *ws)


def _gather_ride(bufs):
    na = len(bufs)

    def legs(outs, sems):
        send_sem, recv_sem, fsend_sem, frecv_sem = sems
        x, y, c, chips = _place()

        def half(a, idx, which):
            hr = outs[a].shape[1] // 2
            return outs[a].at[idx, pl.ds(which * hr, hr)]

        def ici(a, j, idx):
            px, py = chips[j]
            return pltpu.make_async_remote_copy(
                src_ref=half(a, idx, c), dst_ref=half(a, idx, c),
                send_sem=send_sem.at[a, j], recv_sem=recv_sem.at[a, j], device_id=(px, py, c), device_id_type=MESH)

        def d2d(a, j, idx, which):
            return pltpu.make_async_remote_copy(
                src_ref=half(a, idx, which), dst_ref=half(a, idx, which),
                send_sem=fsend_sem.at[a, j], recv_sem=frecv_sem.at[a, j], device_id=(x, y, 1 - c), device_id_type=MESH)

        return 2 * x + y, c, chips, ici, d2d

    def start(ins, outs, sems):
        me, _, _, ici, _ = legs(outs, sems)
        for a in range(na):
            for j in range(3):
                ici(a, j, me).start()

    def finish(ins, outs, sems):
        me, c, chips, ici, d2d = legs(outs, sems)
        passed = []
        for a in range(na):
            for j, (px, py) in enumerate(chips):
                ici(a, j, 2 * px + py).wait_recv()
                cp = d2d(a, j, 2 * px + py, c)
                cp.start()
                passed.append(cp)
        for a in range(na):
            for j, (px, py) in enumerate(chips):
                d2d(a, j, 2 * px + py, 1 - c).wait_recv()
        for a in range(na):
            for j in range(3):
                ici(a, j, me).wait_send()
        for cp in passed:
            cp.wait_send()

    return _Ride(bufs, [jax.ShapeDtypeStruct(b.shape, b.dtype) for b in bufs], [pltpu.SemaphoreType.DMA((na, 3))] * 4,
                 start, finish, {a: a for a in range(na)})


def _pair_ride(grads):
    na = len(grads)

    def copies(ins, outs, sems):
        send_sem, recv_sem = sems
        x, y, c, _ = _place()
        res = []
        for a in range(na):
            hr = ins[a].shape[1] // 2
            res.append(pltpu.make_async_remote_copy(
                src_ref=ins[a].at[:, pl.ds((1 - c) * hr, hr)], dst_ref=outs[a],
                send_sem=send_sem.at[a], recv_sem=recv_sem.at[a], device_id=(x, y, 1 - c), device_id_type=MESH))
        return res

    def start(ins, outs, sems):
        for cp in copies(ins, outs, sems):
            cp.start()

    def finish(ins, outs, sems):
        for cp in copies(ins, outs, sems):
            cp.wait()

    return _Ride(grads, [jax.ShapeDtypeStruct((g.shape[0], g.shape[1] // 2, g.shape[2]), g.dtype) for g in grads],
                 [pltpu.SemaphoreType.DMA((na,))] * 2, start, finish)


def _chip_ride(sums):
    na = len(sums)

    def copies(ins, outs, sems):
        send_sem, recv_sem = sems
        x, y, c, chips = _place()
        res = []
        for a in range(na):
            for j, (px, py) in enumerate(chips):
                res.append(pltpu.make_async_remote_copy(
                    src_ref=ins[a].at[2 * px + py], dst_ref=outs[a].at[j],
                    send_sem=send_sem.at[a, j], recv_sem=recv_sem.at[a, j], device_id=(px, py, c), device_id_type=MESH))
        return res

    def start(ins, outs, sems):
        for cp in copies(ins, outs, sems):
            cp.start()

    def finish(ins, outs, sems):
        for cp in copies(ins, outs, sems):
            cp.wait()

    return _Ride(sums, [jax.ShapeDtypeStruct((3,) + s.shape[1:], s.dtype) for s in sums],
                 [pltpu.SemaphoreType.DMA((na, 3))] * 2, start, finish)


def _finish_ride(grads, gpack=None):
    na = len(grads)

    def halves(outs, sems, which):
        x, y, c, _ = _place()
        res = []
        for a in range(na):
            hr = outs[a].shape[0] // 2
            rows = outs[a].at[pl.ds((c if which == "mine" else 1 - c) * hr, hr)]
            res.append(pltpu.make_async_remote_copy(
                src_ref=rows, dst_ref=rows, send_sem=sems[0].at[a], recv_sem=sems[1].at[a],
                device_id=(x, y, 1 - c), device_id_type=MESH))
        return res

    def gains(ins, outs, sems):
        x, y, c, _ = _place()
        dev = 4 * x + 2 * y + c
        g_in, g_out = ins[na], outs[na]
        own = pltpu.make_async_copy(g_in, g_out.at[dev], sems[2])
        sends, lands = [], []
        for k in range(N_DEV - 1):
            bx, by, bc = (k + 1) // 4, ((k + 1) // 2) % 2, (k + 1) % 2
            peer = (jnp.bitwise_xor(x, bx), jnp.bitwise_xor(y, by), jnp.bitwise_xor(c, bc))
            sends.append(pltpu.make_async_remote_copy(
                src_ref=g_in, dst_ref=g_out.at[dev], send_sem=sems[3].at[k], recv_sem=sems[4].at[k],
                device_id=peer, device_id_type=MESH))
            slot = g_out.at[jnp.bitwise_xor(dev, k + 1)]
            lands.append(pltpu.make_async_remote_copy(
                src_ref=slot, dst_ref=slot, send_sem=sems[3].at[k], recv_sem=sems[4].at[k],
                device_id=peer, device_id_type=MESH))
        return own, sends, lands

    def start(ins, outs, sems):
        for cp in halves(outs, sems, "mine"):
            cp.start()
        if gpack is not None:
            own, sends, _ = gains(ins, outs, sems)
            own.start()
            for cp in sends:
                cp.start()

    def finish(ins, outs, sems):
        for cp in halves(outs, sems, "sibling's"):
            cp.wait_recv()
        if gpack is not None:
            own, sends, lands = gains(ins, outs, sems)
            for cp in lands:
                cp.wait_recv()
            for cp in sends:
                cp.wait_send()
            own.wait()
        for cp in halves(outs, sems, "mine"):
            cp.wait_send()

    shapes = [jax.ShapeDtypeStruct(g.shape, g.dtype) for g in grads]
    sems = [pltpu.SemaphoreType.DMA((na,))] * 2
    if gpack is None:
        return _Ride(grads, shapes, sems, start, finish, {a: a for a in range(na)})
    return _Ride(list(grads) + [gpack], shapes + [jax.ShapeDtypeStruct((N_DEV,) + gpack.shape, gpack.dtype)],
                 sems + [pltpu.SemaphoreType.DMA, pltpu.SemaphoreType.DMA((N_DEV - 1,)), pltpu.SemaphoreType.DMA((N_DEV - 1,))],
                 start, finish, {a: a for a in range(na)})


def _pair_sum(place, grad, got, name):
    ns, r, cols = grad.shape
    hr = r // 2

    def body(place_ref, g_ref, r_ref, o_ref):
        o_ref[...] = (g_ref[...] + r_ref[...]).astype(BF16)

    return pl.pallas_call(
        body, name=name,
        grid_spec=pltpu.PrefetchScalarGridSpec(
            num_scalar_prefetch=1, grid=(ns,),
            in_specs=[pl.BlockSpec((None, hr, cols), lambda s, pr: (s, pr[1], 0)),
                      pl.BlockSpec((None, hr, cols), lambda s, pr: (s, 0, 0))],
            out_specs=pl.BlockSpec((None, hr, cols), lambda s, pr: (s, 0, 0))),
        out_shape=jax.ShapeDtypeStruct((ns, hr, cols), BF16),
        compiler_params=_params("arbitrary"),
    )(place, grad, got)


def _chip_sum(place, grad, got, others, name):
    ns, r, cols = grad.shape
    hr = r // 2
    nb = 2
    tr = hr // nb

    def body(place_ref, g_ref, r_ref, o3_ref, o_ref):
        acc = g_ref[...] + r_ref[...]
        for j in range(3):
            acc = acc + o3_ref[j].astype(F32)
        o_ref[...] = acc

    return pl.pallas_call(
        body, name=name,
        grid_spec=pltpu.PrefetchScalarGridSpec(
            num_scalar_prefetch=1, grid=(nb,),
            in_specs=[pl.BlockSpec((None, tr, cols), lambda i, pr: (pr[0], pr[1] * nb + i, 0)),
                      pl.BlockSpec((None, tr, cols), lambda i, pr: (pr[0], i, 0)),
                      pl.BlockSpec((3, tr, cols), lambda i, pr: (0, i, 0))],
            out_specs=pl.BlockSpec((tr, cols), lambda i, pr: (pr[1] * nb + i, 0))),
        out_shape=jax.ShapeDtypeStruct((r, cols), F32),
        compiler_params=_params("arbitrary"),
    )(place, grad, got, others)


def _pack_gains(parts, d):
    def body(*refs):
        ins, o_ref = refs[:-1], refs[-1]
        o_ref[...] = jnp.zeros_like(o_ref)
        for k, r in enumerate(ins):
            o_ref[k:k + 1, 0:r.shape[1]] = jnp.sum(r[...], axis=0, keepdims=True)

    return pl.pallas_call(
        body, name="pack_gains", out_shape=jax.ShapeDtypeStruct((GAIN_ROWS, d), F32),
    )(*parts)


def _adamw_math(w, g, m, v):
    m = ADAM_B1 * m + (1.0 - ADAM_B1) * g
    v = ADAM_B2 * v + (1.0 - ADAM_B2) * jnp.square(g)
    m_hat = m / (1.0 - ADAM_B1 ** ADAM_STEP)
    v_hat = v / (1.0 - ADAM_B2 ** ADAM_STEP)
    return -ADAM_LR * (m_hat / (jnp.sqrt(v_hat) + ADAM_EPS) + ADAM_WD * w), m, v


def _adamw(ws, gs, ms, vs):
    n = len(ws)

    def body(*refs):
        ins, outs = refs[:4 * n], refs[4 * n:]
        for k in range(n):
            w_ref, g_ref, m_ref, v_ref = ins[4 * k:4 * k + 4]
            go_ref, d_ref, nm_ref, nv_ref = outs[4 * k:4 * k + 4]
            g = g_ref[...]
            go_ref[...] = g
            d_ref[...], nm_ref[...], nv_ref[...] = _adamw_math(w_ref[...], g, m_ref[...], v_ref[...])

    parts = 2 * ROW_QUARTERS
    tile = lambda w: pl.BlockSpec((w.shape[0] // parts, w.shape[1]), lambda i: (i, 0))
    res = pl.pallas_call(
        body, name="adamw_shards", grid=(parts,),
        in_specs=[tile(w) for w in ws for _ in range(4)], out_specs=[tile(w) for w in ws for _ in range(4)],
        out_shape=[jax.ShapeDtypeStruct(w.shape, F32) for w in ws for _ in range(4)],
        compiler_params=_params("arbitrary"),
    )(*[a for quad in zip(ws, gs, ms, vs) for a in quad])
    return [res[4 * k:4 * k + 4] for k in range(n)]


def _adamw_gain(gall, row, w, m, v, name):
    n = w.shape[1]

    def body(ga_ref, w_ref, m_ref, v_ref, g_ref, d_ref, nm_ref, nv_ref):
        g = ga_ref[0, row:row + 1, 0:n]
        for k in range(1, N_DEV):
            g = g + ga_ref[k, row:row + 1, 0:n]
        g_ref[...] = g
        d_ref[...], nm_ref[...], nv_ref[...] = _adamw_math(w_ref[...], g, m_ref[...], v_ref[...])

    return pl.pallas_call(
        body, name=name, out_shape=[jax.ShapeDtypeStruct((1, n), F32)] * 4,
    )(gall, w, m, v)


def kernel(x, norm_ffn1, ffn1_w_gate, ffn1_w_up, ffn1_w_down, norm_mix, w_in, ret_norm_gain, w_out, norm_ffn2, ffn2_w_gate, ffn2_w_up, ffn2_w_down, norm_final, loss_target, m_norm_ffn1, m_ffn1_w_gate, m_ffn1_w_up, m_ffn1_w_down, m_norm_mix, m_w_in, m_ret_norm_gain, m_w_out, m_norm_ffn2, m_ffn2_w_gate, m_ffn2_w_up, m_ffn2_w_down, m_norm_final, v_norm_ffn1, v_ffn1_w_gate, v_ffn1_w_up, v_ffn1_w_down, v_norm_mix, v_w_in, v_ret_norm_gain, v_w_out, v_norm_ffn2, v_ffn2_w_gate, v_ffn2_w_up, v_ffn2_w_down, v_norm_final):
    d = x.shape[-1]
    mats = [ffn1_w_gate, ffn1_w_up, ffn1_w_down, w_in, w_out, ffn2_w_gate, ffn2_w_up, ffn2_w_down]
    mats_m = [m_ffn1_w_gate, m_ffn1_w_up, m_ffn1_w_down, m_w_in, m_w_out, m_ffn2_w_gate, m_ffn2_w_up, m_ffn2_w_down]
    mats_v = [v_ffn1_w_gate, v_ffn1_w_up, v_ffn1_w_down, v_w_in, v_w_out, v_ffn2_w_gate, v_ffn2_w_up, v_ffn2_w_down]
    mat_names = ["ffn1_w_gate", "ffn1_w_up", "ffn1_w_down", "w_in", "w_out", "ffn2_w_gate", "ffn2_w_up", "ffn2_w_down"]
    gains = [norm_ffn1, norm_mix, ret_norm_gain, norm_ffn2, norm_final.reshape(1, d)]
    gains_m = [m_norm_ffn1, m_norm_mix, m_ret_norm_gain, m_norm_ffn2, m_norm_final.reshape(1, d)]
    gains_v = [v_norm_ffn1, v_norm_mix, v_ret_norm_gain, v_norm_ffn2, v_norm_final.reshape(1, d)]
    gain_names = ["norm_ffn1", "norm_mix", "ret_norm_gain", "norm_ffn2", "norm_final"]

    turned = lambda n: n.endswith(("w_gate", "w_up"))
    local = lambda a, n: jnp.swapaxes(a, 1, 2)[0] if turned(n) else a[0]
    back = lambda a, n: jnp.swapaxes(a[None], 1, 2) if turned(n) else a[None]
    shards = [local(w, n) for w, n in zip(mats, mat_names)]
    place = jnp.stack([2 * lax.axis_index("x") + lax.axis_index("y"), lax.axis_index("c")]).astype(jnp.int32)
    placed = _place_shards(place, shards)
    loss_p, dx, shard_grads, gall = _step(x[0], loss_target[0], gains, placed, place)

    out_g, out_d, out_m, out_v = {}, {}, {}, {}
    updates = _adamw(shards, shard_grads, [local(m, n) for m, n in zip(mats_m, mat_names)],
                     [local(v, n) for v, n in zip(mats_v, mat_names)])
    for n, quad in zip(mat_names, updates):
        out_g[n], out_d[n], out_m[n], out_v[n] = [back(a, n) for a in quad]
    for row, (n, w, m, v) in enumerate(zip(gain_names, gains, gains_m, gains_v)):
        res = _adamw_gain(gall, row, w, m, v, f"adamw_{n}")
        shape = (d,) if n == "norm_final" else w.shape
        out_g[n], out_d[n], out_m[n], out_v[n] = [r.reshape(shape) for r in res]

    loss = lax.psum(jnp.sum(loss_p), ("x", "y", "c"))
    order = ["norm_ffn1", "ffn1_w_gate", "ffn1_w_up", "ffn1_w_down", "norm_mix", "w_in", "ret_norm_gain", "w_out",
             "norm_ffn2", "ffn2_w_gate", "ffn2_w_up", "ffn2_w_down", "norm_final"]
    return (loss, dx[None], *[out_g[n] for n in order], *[out_d[n] for n in order],
            *[out_m[n] for n in order], *[out_v[n] for n in order])
```

```python
import functools

import jax
import jax.numpy as jnp
from jax import lax
from jax.experimental import pallas as pl
from jax.experimental.pallas import tpu as pltpu

F32 = jnp.float32
BF16 = jnp.bfloat16
MESH = pl.DeviceIdType.MESH

NORM_EPS = 1e-6
GN_EPS = 1e-6
ROPE_BASE = 10000.0
RET_HEADS = 4
RET_DIM = 128
RET_WIDTH = 512
RET_CHUNK = 128
ATT_DIM = 64
ATT_WIDTH = 512
ATT_BLOCK = 128
DILATIONS = (1, 4, 16)
LANE = 128
N_SHARD = 4
ADAM_LR, ADAM_B1, ADAM_B2, ADAM_EPS, ADAM_WD, ADAM_STEP = 0.001, 0.9, 0.999, 1e-08, 0.01, 10

V7X_VMEM_BYTES = 64 * 1024 * 1024
VMEM_LIMIT = V7X_VMEM_BYTES - 8 * 1024 * 1024

NT = (((1,), (1,)), ((), ()))
TN = (((0,), (0,)), ((), ()))


def _params(*sem):
    return pltpu.CompilerParams(dimension_semantics=sem, vmem_limit_bytes=VMEM_LIMIT)


def _dot(a, b, dims=None):
    if dims is None:
        return jnp.dot(a, b, preferred_element_type=F32)
    return lax.dot_general(a, b, dims, preferred_element_type=F32)


def _sigmoid(x):
    return 1.0 / (1.0 + jnp.exp(-x))


def _load_weights(pairs, sems):
    copies = [pltpu.make_async_copy(src, dst, sems.at[k]) for k, (src, dst) in enumerate(pairs)]
    for cp in copies:
        cp.start()
    for cp in copies:
        cp.wait()


def _rows8(v):
    r, c = v.shape
    return v.reshape(r // 8, 8, c).sum(axis=0)


class _Ride:
    def __init__(self, inputs, out_shapes, sems, start, finish, aliases=None):
        self.inputs, self.out_shapes, self.sems = list(inputs), list(out_shapes), list(sems)
        self.start, self.finish, self.aliases = start, finish, dict(aliases or {})


def _pallas(body, rides, *, name, in_specs, out_specs, out_shape, args, grid=(), scratch_shapes=(), sem=()):
    rides = [r for r in (rides or []) if r is not None]
    n_in, n_out, n_scr = len(args), len(out_shape), len(scratch_shapes)
    hbm = pl.BlockSpec(memory_space=pl.ANY)
    r_in = [a for r in rides for a in r.inputs]
    r_out = [s for r in rides for s in r.out_shapes]
    r_sem = [s for r in rides for s in r.sems]
    aliases, spans, ki, ko, ks = {}, [], 0, 0, 0
    for r in rides:
        aliases.update({n_in + ki + i: n_out + ko + o for i, o in r.aliases.items()})
        spans.append((ki, ko, ks))
        ki, ko, ks = ki + len(r.inputs), ko + len(r.out_shapes), ks + len(r.sems)

    def wrapped(*refs):
        ins, rin = refs[:n_in], refs[n_in:n_in + len(r_in)]
        o0 = n_in + len(r_in)
        outs, rout = refs[o0:o0 + n_out], refs[o0 + n_out:o0 + n_out + len(r_out)]
        s0 = o0 + n_out + len(r_out)
        scr, rsem = refs[s0:s0 + n_scr], refs[s0 + n_scr:]
        part = lambda r, k: (rin[spans[k][0]:spans[k][0] + len(r.inputs)], rout[spans[k][1]:spans[k][1] + len(r.out_shapes)],
                             rsem[spans[k][2]:spans[k][2] + len(r.sems)])
        first = functools.reduce(jnp.logical_and, [pl.program_id(k) == 0 for k in range(len(grid))], True)
        last = functools.reduce(jnp.logical_and, [pl.program_id(k) == grid[k] - 1 for k in range(len(grid))], True)
        if rides:
            @pl.when(first)
            def _():
                for k, r in enumerate(rides):
                    r.start(*part(r, k))

        body(*ins, *outs, *scr)
        if rides:
            @pl.when(last)
            def _():
                for k, r in enumerate(rides):
                    r.finish(*part(r, k))

    res = pl.pallas_call(
        wrapped, name=name, grid=grid,
        in_specs=list(in_specs) + [hbm] * len(r_in), out_specs=list(out_specs) + [hbm] * len(r_out),
        out_shape=list(out_shape) + r_out, input_output_aliases=aliases,
        scratch_shapes=list(scratch_shapes) + r_sem,
        compiler_params=pltpu.CompilerParams(dimension_semantics=sem, vmem_limit_bytes=VMEM_LIMIT) if grid else None,
    )(*args, *r_in)
    extras = [list(res[n_out + ko:n_out + ko + len(r.out_shapes)]) for r, (_, ko, _) in zip(rides, spans)]
    return list(res[:n_out]), extras


def _run(ride, name):
    def body(*refs):
        n_in, n_out = len(ride.inputs), len(ride.out_shapes)
        parts = refs[:n_in], refs[n_in:n_in + n_out], refs[n_in + n_out:]
        ride.start(*parts)
        ride.finish(*parts)

    hbm = pl.BlockSpec(memory_space=pl.ANY)
    return list(pl.pallas_call(
        body, name=name, in_specs=[hbm] * len(ride.inputs), out_specs=[hbm] * len(ride.out_shapes),
        out_shape=ride.out_shapes, input_output_aliases=ride.aliases, scratch_shapes=ride.sems,
    )(*ride.inputs))


def _loss_head(hv, gain_ref, tg_ref, loss_ref, dgain_ref):
    d = hv.shape[1]
    r = lax.rsqrt(jnp.mean(hv * hv, axis=-1, keepdims=True) + NORM_EPS)
    xh = hv * r
    err = xh * gain_ref[...] - tg_ref[...]
    sq = _rows8(jnp.square(err))
    loss_ref[...] += 0.5 * functools.reduce(jnp.add, [sq[:, k * LANE:(k + 1) * LANE] for k in range(d // LANE)]) / d
    dy = err / d
    dgain_ref[...] += _rows8(dy * xh)
    dxh = dy * gain_ref[...]
    return r * (dxh - xh * jnp.mean(dxh * xh, axis=-1, keepdims=True))


V7X_MXU_TILE = 256
FFN_CHUNK_TILES = 3


def _hidden_chunks(f):
    step = FFN_CHUNK_TILES * V7X_MXU_TILE
    return [slice(s, min(s + step, f)) for s in range(0, f, step)]


def _flat(w):
    return w.reshape(w.shape[0] * w.shape[1], w.shape[2])


def _ffn_fwd(x, gain, wg, wu, wd, name, rides=None, head=None):
    t, d = x.shape
    wg, wu, wd = _flat(wg), _flat(wu), _flat(wd)
    f = wg.shape[0]
    tm = min(512, t)
    nh = 0 if head is None else 2

    def body(*refs):
        x_ref, gain_ref = refs[:2]
        wg_hbm, wu_hbm, wd_hbm, h_ref, xn_ref, g_ref, u_ref, a_ref = refs[2 + nh:10 + nh]
        sums = refs[10 + nh:12 + nh]
        wg_v, wu_v, wd_v, sems = refs[-4:]

        @pl.when(pl.program_id(0) == 0)
        def _():
            _load_weights([(wg_hbm, wg_v), (wu_hbm, wu_v), (wd_hbm, wd_v)], sems)
            if head is not None:
                for s_ref in sums:
                    s_ref[...] = jnp.zeros_like(s_ref)

        xv = x_ref[...]
        r = lax.rsqrt(jnp.mean(xv * xv, axis=-1, keepdims=True) + NORM_EPS)
        xn = (xv * r * gain_ref[...]).astype(BF16)
        xn_ref[...] = xn
        acc = jnp.zeros((tm, d), F32)
        for c in _hidden_chunks(f):
            g = _dot(xn, wg_v[c, :], NT)
            u = _dot(xn, wu_v[c, :], NT)
            g_ref[:, c] = g.astype(BF16)
            u_ref[:, c] = u.astype(BF16)
            a = (g * _sigmoid(g) * u).astype(BF16)
            a_ref[:, c] = a
            acc = acc + _dot(a, wd_v[c, :])
        hv = xv + 0.5 * acc
        h_ref[...] = hv if head is None else _loss_head(hv, refs[2], refs[3], *sums)

    hbm = pl.BlockSpec(memory_space=pl.ANY)
    hid = pl.BlockSpec((tm, f), lambda i: (i, 0))
    tile = pl.BlockSpec((tm, d), lambda i: (i, 0))
    row = pl.BlockSpec((1, d), lambda i: (0, 0))
    sums = [] if head is None else [(pl.BlockSpec((8, LANE), lambda i: (0, 0)), jax.ShapeDtypeStruct((8, LANE), F32)),
                                    (pl.BlockSpec((8, d), lambda i: (0, 0)), jax.ShapeDtypeStruct((8, d), F32))]
    return _pallas(
        body, rides, name=name, grid=(t // tm,),
        in_specs=[tile, row] + ([] if head is None else [row, tile]) + [hbm, hbm, hbm],
        out_specs=[tile, tile, hid, hid, hid] + [s for s, _ in sums],
        out_shape=[jax.ShapeDtypeStruct((t, d), F32), jax.ShapeDtypeStruct((t, d), BF16)]
        + [jax.ShapeDtypeStruct((t, f), BF16)] * 3 + [s for _, s in sums],
        scratch_shapes=[pltpu.VMEM(wg.shape, BF16), pltpu.VMEM(wu.shape, BF16), pltpu.VMEM(wd.shape, BF16),
                        pltpu.SemaphoreType.DMA((3,))],
        sem=("arbitrary",), args=[x, gain] + ([] if head is None else list(head)) + [wg, wu, wd])


def _ffn_bwd_data(dy, x, gain, g, u, wg, wu, wd, name, rides=None):
    t, d = x.shape
    wg, wu, wd = _flat(wg), _flat(wu), _flat(wd)
    f = wg.shape[0]
    tm = min(256, t)

    def body(dy_ref, x_ref, gain_ref, g_ref, u_ref, wg_hbm, wu_hbm, wd_hbm, dx_ref, dg_ref, du_ref, dgain_ref,
             wg_v, wu_v, wd_v, sems):
        @pl.when(pl.program_id(0) == 0)
        def _():
            _load_weights([(wg_hbm, wg_v), (wu_hbm, wu_v), (wd_hbm, wd_v)], sems)
            dgain_ref[...] = jnp.zeros_like(dgain_ref)

        dyv = dy_ref[...]
        dyh = (0.5 * dyv).astype(BF16)
        dxn = jnp.zeros((tm, d), F32)
        chunks = _hidden_chunks(f)
        das = [_dot(dyh, wd_v[c, :], NT) for c in chunks]
        for c, da in zip(chunks, das):
            gj = g_ref[:, c].astype(F32)
            uj = u_ref[:, c].astype(F32)
            sig = _sigmoid(gj)
            dgj = (da * uj * (sig * (1.0 + gj * (1.0 - sig)))).astype(BF16)
            duj = (da * (gj * sig)).astype(BF16)
            dg_ref[:, c] = dgj
            du_ref[:, c] = duj
            dxn = dxn + _dot(dgj, wg_v[c, :]) + _dot(duj, wu_v[c, :])
        xv = x_ref[...]
        r = lax.rsqrt(jnp.mean(xv * xv, axis=-1, keepdims=True) + NORM_EPS)
        xh = xv * r
        dgain_ref[...] += _rows8(dxn * xh)
        dxh = dxn * gain_ref[...]
        dx_ref[...] = dyv + r * (dxh - xh * jnp.mean(dxh * xh, axis=-1, keepdims=True))

    hbm = pl.BlockSpec(memory_space=pl.ANY)
    tile = pl.BlockSpec((tm, d), lambda i: (i, 0))
    hid = pl.BlockSpec((tm, f), lambda i: (i, 0))
    return _pallas(
        body, rides, name=name, grid=(t // tm,),
        in_specs=[tile, tile, pl.BlockSpec((1, d), lambda i: (0, 0)), hid, hid, hbm, hbm, hbm],
        out_specs=[tile, hid, hid, pl.BlockSpec((8, d), lambda i: (0, 0))],
        out_shape=[jax.ShapeDtypeStruct((t, d), F32), jax.ShapeDtypeStruct((t, f), BF16),
                   jax.ShapeDtypeStruct((t, f), BF16), jax.ShapeDtypeStruct((8, d), F32)],
        scratch_shapes=[pltpu.VMEM(wg.shape, BF16), pltpu.VMEM(wu.shape, BF16), pltpu.VMEM(wd.shape, BF16),
                        pltpu.SemaphoreType.DMA((3,))],
        sem=("arbitrary",), args=[dy, x, gain, g, u, wg, wu, wd])


WGRAD_ROW_BLOCKS = 2


def _ffn_wgrad_down(a, dy, name, rides=None):
    t, d = dy.shape
    f = a.shape[1]
    fb = f // WGRAD_ROW_BLOCKS
    tk = min(1024, t)

    def body(dy_ref, a_ref, dwd_ref):
        @pl.when(pl.program_id(1) == 0)
        def _():
            dwd_ref[...] = jnp.zeros_like(dwd_ref)

        dwd_ref[...] += _dot(a_ref[...], (0.5 * dy_ref[...]).astype(BF16), TN)

    return _pallas(
        body, rides, name=name, grid=(WGRAD_ROW_BLOCKS, t // tk),
        in_specs=[pl.BlockSpec((tk, d), lambda j, k: (k, 0)), pl.BlockSpec((tk, fb), lambda j, k: (k, j))],
        out_specs=[pl.BlockSpec((fb, d), lambda j, k: (j, 0))],
        out_shape=[jax.ShapeDtypeStruct((f, d), F32)],
        sem=("arbitrary", "arbitrary"), args=[dy, a])


def _ffn_wgrad_gu(xn, dhs, name, rides=None):
    t, d = xn.shape
    n = len(dhs)
    f = dhs[0].shape[1]
    fb = f // WGRAD_ROW_BLOCKS
    tk = min(2048 // n, t)

    def body(xn_ref, *refs):
        @pl.when(pl.program_id(1) == 0)
        def _():
            for o_ref in refs[n:]:
                o_ref[...] = jnp.zeros_like(o_ref)

        xnv = xn_ref[...]
        for dh_ref, o_ref in zip(refs[:n], refs[n:]):
            o_ref[...] += _dot(dh_ref[...], xnv, TN)

    hid = pl.BlockSpec((tk, fb), lambda j, k: (k, j))
    out = pl.BlockSpec((fb, d), lambda j, k: (j, 0))
    return _pallas(
        body, rides, name=name, grid=(WGRAD_ROW_BLOCKS, t // tk),
        in_specs=[pl.BlockSpec((tk, d), lambda j, k: (k, 0))] + [hid] * n,
        out_specs=[out] * n, out_shape=[jax.ShapeDtypeStruct((f, d), F32)] * n,
        sem=("arbitrary", "arbitrary"), args=[xn] + list(dhs))


def _tn_matmul(a, b, bn, name):
    t, m = a.shape
    n = b.shape[1]
    tk = min(2048, t)

    def body(a_ref, b_ref, o_ref):
        @pl.when(pl.program_id(1) == 0)
        def _():
            o_ref[...] = jnp.zeros_like(o_ref)

        o_ref[...] += _dot(a_ref[...].astype(BF16), b_ref[...].astype(BF16), TN)

    return pl.pallas_call(
        body, name=name, grid=(n // bn, t // tk),
        in_specs=[pl.BlockSpec((tk, m), lambda j, k: (k, 0)), pl.BlockSpec((tk, bn), lambda j, k: (k, j))],
        out_specs=pl.BlockSpec((None, m, bn), lambda j, k: (j, 0, 0)),
        out_shape=jax.ShapeDtypeStruct((n // bn, m, bn), F32),
        compiler_params=_params("arbitrary", "arbitrary"),
    )(a, b)


def _chunk_scratch(tm, w):
    return pltpu.VMEM((w // LANE, tm, LANE), F32)


def _regroup_store(cbuf, out_ref, dil, chunks=None):
    n = out_ref.shape[1]
    for k in range(cbuf.shape[0]) if chunks is None else chunks:
        for g in range(dil):
            rows = cbuf[k] if dil == 1 else cbuf[k, pl.ds(g, n, stride=dil), :]
            out_ref[g, :, k * LANE:(k + 1) * LANE] = rows.astype(out_ref.dtype)


def _natural_rows(ref, dil, cbuf):
    if dil == 1:
        return ref[0].astype(F32)
    n = ref.shape[1]
    for g in range(dil):
        for k in range(cbuf.shape[0]):
            cbuf[k, pl.ds(g, n, stride=dil), :] = ref[g, :, k * LANE:(k + 1) * LANE].astype(F32)
    return jnp.concatenate([cbuf[k] for k in range(cbuf.shape[0])], axis=1)


IN_CHUNK_TILES = 4


def _column_chunks(n):
    step = IN_CHUNK_TILES * V7X_MXU_TILE
    return [slice(s, min(s + step, n)) for s in range(0, n, step)]


def _load_side_by_side(w_hbm, w_v, sems):
    ns, _, cs = w_hbm.shape
    copies = [pltpu.make_async_copy(w_hbm.at[j], w_v.at[:, pl.ds(j * cs, cs)], sems.at[j]) for j in range(ns)]
    for cp in copies:
        cp.start()
    for cp in copies:
        cp.wait()


def _inproj_fwd(h, gain, win):
    t, d = h.shape
    ns, _, cs = win.shape
    tm = min(512, t)
    rw, aw = 4 * RET_WIDTH, 3 * ATT_WIDTH

    def body(h_ref, gain_ref, w_hbm, xn_ref, ur_ref, *rest):
        a_refs, abuf, w_v, sems = rest[:-3], rest[-3], rest[-2], rest[-1]

        @pl.when(pl.program_id(0) == 0)
        def _():
            _load_side_by_side(w_hbm, w_v, sems)

        hv = h_ref[...]
        r = lax.rsqrt(jnp.mean(hv * hv, axis=-1, keepdims=True) + NORM_EPS)
        xn = (hv * r * gain_ref[...]).astype(BF16)
        xn_ref[...] = xn
        for c in reversed(_column_chunks(ns * cs)):
            res = _dot(xn, w_v[:, c])
            mine = []
            for k in range((c.stop - c.start) // LANE):
                chunk = c.start // LANE + k
                piece = res[:, k * LANE:(k + 1) * LANE]
                if chunk < rw // LANE:
                    ur_ref[:, chunk * LANE:(chunk + 1) * LANE] = piece
                else:
                    abuf[chunk - rw // LANE] = piece
                    mine.append(chunk - rw // LANE)
            for dil, a_ref in zip(DILATIONS, a_refs):
                _regroup_store(abuf, a_ref, dil, mine)

    return pl.pallas_call(
        body, name="inproj_fwd", grid=(t // tm,),
        in_specs=[pl.BlockSpec((tm, d), lambda i: (i, 0)), pl.BlockSpec((1, d), lambda i: (0, 0)),
                  pl.BlockSpec(memory_space=pl.ANY)],
        out_specs=[pl.BlockSpec((tm, d), lambda i: (i, 0)), pl.BlockSpec((tm, rw), lambda i: (i, 0))]
        + [pl.BlockSpec((dil, tm // dil, aw), lambda i: (0, i, 0)) for dil in DILATIONS],
        out_shape=[jax.ShapeDtypeStruct((t, d), BF16), jax.ShapeDtypeStruct((t, rw), F32)]
        + [jax.ShapeDtypeStruct((dil, t // dil, aw), BF16) for dil in DILATIONS],
        scratch_shapes=[_chunk_scratch(tm, aw), pltpu.VMEM((d, ns * cs), BF16), pltpu.SemaphoreType.DMA((ns,))],
        compiler_params=_params("arbitrary"),
    )(h, gain, win)


def _inproj_bwd(pieces, parts, h, gain, dres, win):
    t, d = h.shape
    ns, _, cs = win.shape
    pw = pieces[0].shape[1]
    tm = min(512, t)
    npc, nk = len(pieces), len(parts[0])
    flat_parts = [a for p in parts for a in p]

    def body(*refs):
        p_refs, a_refs = refs[:npc], refs[npc:npc + len(flat_parts)]
        h_ref, gain_ref, dres_ref, w_hbm, dh_ref, du_ref, dgain_ref, buf, w_v, sems = refs[npc + len(flat_parts):]

        @pl.when(pl.program_id(0) == 0)
        def _():
            _load_side_by_side(w_hbm, w_v, sems)
            dgain_ref[...] = jnp.zeros_like(dgain_ref)

        for k in range(npc):
            du_ref[:, k * pw:(k + 1) * pw] = p_refs[k][...]
        for k in range(nk):
            acc = None
            for b, dil in enumerate(DILATIONS):
                rows = _natural_rows(a_refs[b * nk + k], dil, buf)
                acc = rows if acc is None else acc + rows
            du_ref[:, (npc + k) * pw:(npc + k + 1) * pw] = acc.astype(BF16)
        dxn = jnp.zeros((tm, d), F32)
        for c in _column_chunks(ns * cs):
            dxn = dxn + _dot(du_ref[:, c], w_v[:, c], NT)
        hv = h_ref[...]
        r = lax.rsqrt(jnp.mean(hv * hv, axis=-1, keepdims=True) + NORM_EPS)
        xh = hv * r
        dgain_ref[...] += _rows8(dxn * xh)
        dxh = dxn * gain_ref[...]
        dh_ref[...] = dres_ref[...] + r * (dxh - xh * jnp.mean(dxh * xh, axis=-1, keepdims=True))

    tile = pl.BlockSpec((tm, d), lambda i: (i, 0))
    cols = (npc + nk) * pw
    return pl.pallas_call(
        body, name="inproj_bwd", grid=(t // tm,),
        in_specs=[pl.BlockSpec((tm, pw), lambda i: (i, 0))] * npc
        + [_regrouped_spec(tm, dil, pw) for dil in DILATIONS for _ in range(nk)]
        + [tile, pl.BlockSpec((1, d), lambda i: (0, 0)), tile, pl.BlockSpec(memory_space=pl.ANY)],
        out_specs=[tile, pl.BlockSpec((tm, cols), lambda i: (i, 0)), pl.BlockSpec((8, d), lambda i: (0, 0))],
        out_shape=[jax.ShapeDtypeStruct((t, d), F32), jax.ShapeDtypeStruct((t, cols), BF16),
                   jax.ShapeDtypeStruct((8, d), F32)],
        scratch_shapes=[_chunk_scratch(tm, pw), pltpu.VMEM((d, ns * cs), BF16), pltpu.SemaphoreType.DMA((ns,))],
        compiler_params=_params("arbitrary"),
    )(*pieces, *flat_parts, h, gain, dres, win)


def _outproj_fwd(h, mix_r, mix_a, wo):
    t, d = h.shape
    hw = mix_r.shape[1]
    tm = min(512, t)

    def body(h_ref, mr_ref, ma_ref, w_ref, o_ref):
        o_ref[...] = h_ref[...] + _dot(mr_ref[...], w_ref[0:hw, :]) + _dot(ma_ref[...], w_ref[hw:2 * hw, :])

    tile = pl.BlockSpec((tm, d), lambda i: (i, 0))
    half = pl.BlockSpec((tm, hw), lambda i: (i, 0))
    return pl.pallas_call(
        body, name="outproj_fwd", grid=(t // tm,),
        in_specs=[tile, half, half, pl.BlockSpec(wo.shape, lambda i: (0, 0))],
        out_specs=tile, out_shape=jax.ShapeDtypeStruct((t, d), F32),
        compiler_params=_params("arbitrary"),
    )(h, mix_r, mix_a, wo)


def _outproj_bwd(dh, wo, rides=None):
    t, d = dh.shape
    hw = wo.shape[0] // 2
    tm = min(512, t)

    def body(dh_ref, w_ref, dr_ref, da_ref):
        dhb = dh_ref[...].astype(BF16)
        dr_ref[...] = _dot(dhb, w_ref[0:hw, :], NT)
        da_ref[...] = _dot(dhb, w_ref[hw:2 * hw, :], NT)

    half = pl.BlockSpec((tm, hw), lambda i: (i, 0))
    return _pallas(
        body, rides, name="outproj_bwd", grid=(t // tm,),
        in_specs=[pl.BlockSpec((tm, d), lambda i: (i, 0)), pl.BlockSpec(wo.shape, lambda i: (0, 0))],
        out_specs=[half, half],
        out_shape=[jax.ShapeDtypeStruct((t, hw), F32), jax.ShapeDtypeStruct((t, hw), F32)],
        sem=("arbitrary",), args=[dh, wo])


def _retention_tables(t):
    pos = jnp.arange(t, dtype=F32)
    pair = (jnp.arange(RET_DIM) // 2 * 2).astype(F32)
    ang = pos[:, None] * (ROPE_BASE ** (-pair / RET_DIM))[None, :]
    c = RET_CHUNK
    log_g = jnp.log(1.0 - 2.0 ** (-5.0 - jnp.arange(RET_HEADS, dtype=F32)))
    idx = jnp.arange(c, dtype=F32)
    rel = idx[:, None] - idx[None, :]
    decay = jnp.where(rel >= 0, jnp.exp(log_g[:, None, None] * jnp.maximum(rel, 0.0)), 0.0)
    zeta = jnp.exp(log_g[:, None] * (c - 1 - idx)[None, :])
    xi = jnp.exp(log_g[:, None] * (idx + 1)[None, :])
    gc = jnp.exp(log_g * c)
    wide = lambda v: jnp.broadcast_to(v[:, :, None], (RET_HEADS, c, LANE))
    return (jnp.cos(ang), jnp.sin(ang), decay, wide(zeta), wide(xi),
            jnp.broadcast_to(gc[:, None, None], (RET_HEADS, c, LANE)))


def _rot(v):
    lane = lax.broadcasted_iota(jnp.int32, v.shape, 1)
    nxt = pltpu.roll(v, LANE - 1, 1)
    prv = pltpu.roll(v, 1, 1)
    return jnp.where(lane % 2 == 0, -nxt, prv)


def _ret_specs(tr, rev, nt):
    ti = (lambda i: nt - 1 - i) if rev else (lambda i: i)
    col = lambda blk: pl.BlockSpec((tr, RET_WIDTH), lambda i: (ti(i), blk))
    tab = pl.BlockSpec((tr, LANE), lambda i: (ti(i), 0))
    head = pl.BlockSpec((RET_HEADS, RET_CHUNK, LANE), lambda i: (0, 0, 0))
    return col, tab, head


def _ret_chunks(tr, rev=False):
    order = list(range(tr // RET_CHUNK))
    return [(pl.ds(ci * RET_CHUNK, RET_CHUNK), slice(h * RET_DIM, (h + 1) * RET_DIM), h)
            for h in range(RET_HEADS) for ci in (reversed(order) if rev else order)]


def _ret_operands(items, q_ref, k_ref, v_ref, cos_ref, sin_ref, zeta_ref):
    scale = RET_DIM ** -0.5
    qbs, kbs, vbs, kzs = [], [], [], []
    for sl, hs, h in items:
        cs, sn = cos_ref[sl, :], sin_ref[sl, :]
        q, k = q_ref[sl, hs], k_ref[sl, hs]
        kr = (k * cs + _rot(k) * sn) * scale
        qbs.append((q * cs + _rot(q) * sn).astype(BF16))
        kbs.append(kr.astype(BF16))
        vbs.append(v_ref[sl, hs].astype(BF16))
        kzs.append((kr * zeta_ref[h]).astype(BF16))
    return qbs, kbs, vbs, kzs


def _ret_states(items, state, steps, gc_ref):
    cur, befores = {}, []
    for (sl, hs, h), step in zip(items, steps):
        st = cur[h] if h in cur else state[h]
        befores.append(st)
        cur[h] = st * gc_ref[h] + step
    for h, st in cur.items():
        state[h] = st
    return befores


def _ret_fwd(u, gain, tabs):
    t = u.shape[0]
    tr = min(512, t)
    nt = t // tr
    cos, sin, decay, zeta, xi, gc = tabs

    def body(q_ref, k_ref, v_ref, gt_ref, cos_ref, sin_ref, gain_ref, dec_ref, zeta_ref, xi_ref, gc_ref,
             raw_ref, mix_ref, state):
        @pl.when(pl.program_id(0) == 0)
        def _():
            state[...] = jnp.zeros_like(state)

        items = _ret_chunks(tr)
        n = range(len(items))
        qbs, kbs, vbs, kzs = _ret_operands(items, q_ref, k_ref, v_ref, cos_ref, sin_ref, zeta_ref)
        ss = [_dot(qbs[i], kbs[i], NT) for i in n]
        kvs = [_dot(kzs[i], vbs[i], TN) for i in n]
        befores = _ret_states(items, state, kvs, gc_ref)
        intra = [_dot((ss[i] * dec_ref[items[i][2]]).astype(BF16), vbs[i]) for i in n]
        inter = [_dot(qbs[i], befores[i].astype(BF16)) for i in n]
        for i, (sl, hs, h) in enumerate(items):
            o = intra[i] + inter[i] * xi_ref[h]
            raw_ref[sl, hs] = o
            mu = jnp.mean(o, axis=-1, keepdims=True)
            var = jnp.mean(jnp.square(o - mu), axis=-1, keepdims=True)
            y = (o - mu) * lax.rsqrt(var + GN_EPS) * gain_ref[:, hs]
            gt = gt_ref[sl, hs]
            mix_ref[sl, hs] = (y * (gt * _sigmoid(gt))).astype(BF16)

    col, tab, head = _ret_specs(tr, False, nt)
    out = pl.BlockSpec((tr, RET_WIDTH), lambda i: (i, 0))
    return pl.pallas_call(
        body, name="ret_fwd", grid=(nt,),
        in_specs=[col(0), col(1), col(2), col(3), tab, tab, pl.BlockSpec((1, RET_WIDTH), lambda i: (0, 0)),
                  head, head, head, head],
        out_specs=[out, out],
        out_shape=[jax.ShapeDtypeStruct((t, RET_WIDTH), F32), jax.ShapeDtypeStruct((t, RET_WIDTH), BF16)],
        scratch_shapes=[pltpu.VMEM((RET_HEADS, RET_DIM, RET_DIM), F32)],
        compiler_params=_params("arbitrary"),
    )(u, u, u, u, cos, sin, gain, decay, zeta, xi, gc)


def _ret_bwd_q(dmix, raw, u, gain, tabs, rides=None):
    t = u.shape[0]
    tr = min(512, t)
    nt = t // tr
    cos, sin, decay, zeta, xi, gc = tabs

    def body(dm_ref, raw_ref, q_ref, k_ref, v_ref, gt_ref, cos_ref, sin_ref, gain_ref, dec_ref, zeta_ref, xi_ref, gc_ref,
             dq_ref, dgt_ref, dret_ref, dgain_ref, state):
        @pl.when(pl.program_id(0) == 0)
        def _():
            state[...] = jnp.zeros_like(state)
            dgain_ref[...] = jnp.zeros_like(dgain_ref)

        items = _ret_chunks(tr)
        n_items = range(len(items))
        qbs, kbs, vbs, kzs = _ret_operands(items, q_ref, k_ref, v_ref, cos_ref, sin_ref, zeta_ref)
        dos, dgains = [], {}
        for sl, hs, h in items:
            o = raw_ref[sl, hs]
            mu = jnp.mean(o, axis=-1, keepdims=True)
            var = jnp.mean(jnp.square(o - mu), axis=-1, keepdims=True)
            rs = lax.rsqrt(var + GN_EPS)
            n = (o - mu) * rs
            gt = gt_ref[sl, hs]
            sig = _sigmoid(gt)
            dout = dm_ref[sl, hs]
            gain_h = gain_ref[:, hs]
            dgt_ref[sl, hs] = (dout * (n * gain_h) * (sig * (1.0 + gt * (1.0 - sig)))).astype(BF16)
            dy = dout * (gt * sig)
            dgains[h] = dgains[h] + _rows8(dy * n) if h in dgains else _rows8(dy * n)
            dn = dy * gain_h
            do = rs * (dn - jnp.mean(dn, axis=-1, keepdims=True) - n * jnp.mean(dn * n, axis=-1, keepdims=True))
            dret_ref[sl, hs] = do
            dos.append(do)
        for h, dg in dgains.items():
            dgain_ref[:, h * RET_DIM:(h + 1) * RET_DIM] += dg
        dss = [_dot(dos[i].astype(BF16), vbs[i], NT) for i in n_items]
        kvs = [_dot(kzs[i], vbs[i], TN) for i in n_items]
        befores = _ret_states(items, state, kvs, gc_ref)
        intra = [_dot((dss[i] * dec_ref[items[i][2]]).astype(BF16), kbs[i]) for i in n_items]
        inter = [_dot((dos[i] * xi_ref[items[i][2]]).astype(BF16), befores[i].astype(BF16), NT) for i in n_items]
        for i, (sl, hs, h) in enumerate(items):
            dqr = intra[i] + inter[i]
            dq_ref[sl, hs] = (dqr * cos_ref[sl, :] - _rot(dqr * sin_ref[sl, :])).astype(BF16)

    col, tab, head = _ret_specs(tr, False, nt)
    out = pl.BlockSpec((tr, RET_WIDTH), lambda i: (i, 0))
    return _pallas(
        body, rides, name="ret_bwd_q", grid=(nt,),
        in_specs=[out, out, col(0), col(1), col(2), col(3), tab, tab, pl.BlockSpec((1, RET_WIDTH), lambda i: (0, 0)),
                  head, head, head, head],
        out_specs=[out, out, out, pl.BlockSpec((8, RET_WIDTH), lambda i: (0, 0))],
        out_shape=[jax.ShapeDtypeStruct((t, RET_WIDTH), BF16), jax.ShapeDtypeStruct((t, RET_WIDTH), BF16),
                   jax.ShapeDtypeStruct((t, RET_WIDTH), F32), jax.ShapeDtypeStruct((8, RET_WIDTH), F32)],
        scratch_shapes=[pltpu.VMEM((RET_HEADS, RET_DIM, RET_DIM), F32)],
        sem=("arbitrary",), args=[dmix, raw, u, u, u, u, cos, sin, gain, decay, zeta, xi, gc])


def _ret_bwd_kv(dret, u, tabs, rides=None):
    t = u.shape[0]
    tr = min(512, t)
    nt = t // tr
    cos, sin, decay, zeta, xi, gc = tabs
    scale = RET_DIM ** -0.5

    def body(do_ref, q_ref, k_ref, v_ref, cos_ref, sin_ref, dec_ref, zeta_ref, xi_ref, gc_ref, dk_ref, dv_ref, gst):
        @pl.when(pl.program_id(0) == 0)
        def _():
            gst[...] = jnp.zeros_like(gst)

        items = _ret_chunks(tr, rev=True)
        n = range(len(items))
        qbs, kbs, vbs, kzs = _ret_operands(items, q_ref, k_ref, v_ref, cos_ref, sin_ref, zeta_ref)
        dos = [do_ref[sl, hs] for sl, hs, h in items]
        dobs = [do.astype(BF16) for do in dos]
        ss = [_dot(qbs[i], kbs[i], NT) for i in n]
        dss = [_dot(dobs[i], vbs[i], NT) for i in n]
        steps = [_dot(qbs[i], (dos[i] * xi_ref[items[i][2]]).astype(BF16), TN) for i in n]
        afters = [g.astype(BF16) for g in _ret_states(items, gst, steps, gc_ref)]
        dvs = [_dot((ss[i] * dec_ref[items[i][2]]).astype(BF16), dobs[i], TN) + _dot(kzs[i], afters[i]) for i in n]
        dks = [_dot((dss[i] * dec_ref[items[i][2]]).astype(BF16), qbs[i], TN) for i in n]
        dkz = [_dot(vbs[i], afters[i], NT) for i in n]
        for i, (sl, hs, h) in enumerate(items):
            dv_ref[sl, hs] = dvs[i].astype(BF16)
            dkr = (dks[i] + dkz[i] * zeta_ref[h]) * scale
            dk_ref[sl, hs] = (dkr * cos_ref[sl, :] - _rot(dkr * sin_ref[sl, :])).astype(BF16)

    col, tab, head = _ret_specs(tr, True, nt)
    out = pl.BlockSpec((tr, RET_WIDTH), lambda i: (nt - 1 - i, 0))
    return _pallas(
        body, rides, name="ret_bwd_kv", grid=(nt,),
        in_specs=[out, col(0), col(1), col(2), tab, tab, head, head, head, head],
        out_specs=[out, out],
        out_shape=[jax.ShapeDtypeStruct((t, RET_WIDTH), BF16), jax.ShapeDtypeStruct((t, RET_WIDTH), BF16)],
        scratch_shapes=[pltpu.VMEM((RET_HEADS, RET_DIM, RET_DIM), F32)],
        sem=("arbitrary",), args=[dret, u, u, u, cos, sin, decay, zeta, xi, gc])


PAIRS = ATT_WIDTH // LANE
ATT_Q_BLK, ATT_K_BLK, ATT_V_BLK = 0, PAIRS, 2 * PAIRS
STAT_LANES = ATT_DIM // 2


ATT_STEP_ROWS = 4096


def _att_tiles(t, dil):
    sub = t // dil
    tq = min(ATT_STEP_ROWS, sub)
    return sub, tq, sub // tq, tq // ATT_BLOCK, min(dil, ATT_STEP_ROWS // tq)


def _att_in_specs(tq, qb, ti, gs):
    cur = lambda off: pl.BlockSpec((gs, tq, LANE), lambda g, p, i: (g, ti(i), off + p))
    prev = lambda off: pl.BlockSpec((gs, ATT_BLOCK, LANE), lambda g, p, i: (g, jnp.maximum(ti(i) * qb - 1, 0), off + p))
    return [cur(ATT_Q_BLK), cur(ATT_K_BLK), prev(ATT_K_BLK), cur(ATT_V_BLK), prev(ATT_V_BLK)]


def _band_mask():
    key = lax.broadcasted_iota(jnp.int32, (2 * ATT_BLOCK, 2 * ATT_BLOCK), 0)
    qry = lax.broadcasted_iota(jnp.int32, (2 * ATT_BLOCK, 2 * ATT_BLOCK), 1) % ATT_BLOCK
    dist = qry + ATT_BLOCK - key
    return (dist >= 0) & (dist <= ATT_BLOCK), key >= ATT_BLOCK


def _head0_lanes():
    return lax.broadcasted_iota(jnp.int32, (ATT_BLOCK, LANE), 1) < ATT_DIM


def _stack_heads(v, head0):
    zero = jnp.zeros((), v.dtype)
    return jnp.concatenate([jnp.where(head0, v, zero), jnp.where(head0, zero, v)], axis=0)


def _unstack_heads(v, head0):
    return jnp.where(head0, v[0:ATT_BLOCK], v[ATT_BLOCK:])


def _att_fwd(ua, dil):
    sub = ua.shape[1]
    _, tq, nq, qb, gs = _att_tiles(sub * dil, dil)

    def body(q_ref, kc_ref, kp_ref, vc_ref, vp_ref, o_ref, l_ref, kx, vx):
        tile = pl.program_id(2)
        kx[:, 0:ATT_BLOCK, :] = kp_ref[...]
        kx[:, ATT_BLOCK:, :] = kc_ref[...]
        vx[:, 0:ATT_BLOCK, :] = vp_ref[...]
        vx[:, ATT_BLOCK:, :] = vc_ref[...]
        band, cur_keys = _band_mask()
        head0 = _head0_lanes()
        items = [(r, b) for r in range(gs) for b in range(qb)]
        rows = lambda b: slice(b * ATT_BLOCK, (b + 1) * ATT_BLOCK)
        keys = lambda b: slice(b * ATT_BLOCK, (b + 2) * ATT_BLOCK)
        sts = [_dot(kx[r, keys(b), :], _stack_heads(q_ref[r, rows(b), :] * jnp.asarray(ATT_DIM ** -0.5, BF16), head0), NT)
               for r, b in items]
        pts, lses = [], []
        for (r, b), st in zip(items, sts):
            mask = band if b > 0 else band & (cur_keys | (tile > 0))
            st = jnp.where(mask, st, -1e30)
            m = jnp.max(st, axis=0, keepdims=True)
            ex = jnp.exp(st - m)
            den = jnp.sum(ex, axis=0, keepdims=True)
            pts.append((ex * (1.0 / den)).astype(BF16))
            lses.append(m + jnp.log(den))
        outs = [_dot(pt, vx[r, keys(b), :], TN) for (r, b), pt in zip(items, pts)]
        for (r, b), out, lse in zip(items, outs, lses):
            o_ref[r, rows(b), :] = _unstack_heads(out, head0).astype(BF16)
            cols = [jnp.broadcast_to(lse[:, e * ATT_BLOCK:(e + 1) * ATT_BLOCK], (ATT_BLOCK, LANE)).T for e in range(2)]
            l_ref[r, rows(b), :] = jnp.where(head0, cols[0], cols[1])

    out = pl.BlockSpec((gs, tq, LANE), lambda g, p, i: (g, i, p))
    return pl.pallas_call(
        body, name=f"att_fwd_d{dil}", grid=(dil // gs, PAIRS, nq),
        in_specs=_att_in_specs(tq, qb, lambda i: i, gs),
        out_specs=[out, out],
        out_shape=[jax.ShapeDtypeStruct((dil, sub, ATT_WIDTH), BF16), jax.ShapeDtypeStruct((dil, sub, ATT_WIDTH), F32)],
        scratch_shapes=[pltpu.VMEM((gs, tq + ATT_BLOCK, LANE), BF16)] * 2,
        compiler_params=_params("arbitrary", "arbitrary", "arbitrary"),
    )(ua, ua, ua, ua, ua)


def _regrouped_spec(tm, dil, w):
    return pl.BlockSpec((dil, tm // dil, w), lambda i: (0, i, 0))


def _att_combine(outs, lses, t):
    w = ATT_WIDTH
    tm = min(512, t)
    nb = len(outs)

    def body(*refs):
        o_refs, l_refs = refs[:nb], refs[nb:2 * nb]
        mix_ref, att_ref, lse_ref, buf = refs[2 * nb:]
        ls = [_natural_rows(r, dil, buf) for r, dil in zip(l_refs, DILATIONS)]
        m = functools.reduce(jnp.maximum, ls)
        ws = [jnp.exp(l - m) for l in ls]
        den = functools.reduce(jnp.add, ws)
        att = functools.reduce(jnp.add, [(wt / den) * _natural_rows(r, dil, buf) for wt, r, dil in zip(ws, o_refs, DILATIONS)])
        att_ref[...] = att
        mix_ref[...] = att.astype(BF16)
        lse_ref[...] = m + jnp.log(den)

    tile = pl.BlockSpec((tm, w), lambda i: (i, 0))
    regrouped = [_regrouped_spec(tm, dil, w) for dil in DILATIONS]
    return pl.pallas_call(
        body, name="att_combine", grid=(t // tm,),
        in_specs=regrouped * 2, out_specs=[tile, tile, tile],
        out_shape=[jax.ShapeDtypeStruct((t, w), BF16), jax.ShapeDtypeStruct((t, w), F32), jax.ShapeDtypeStruct((t, w), F32)],
        scratch_shapes=[_chunk_scratch(tm, w)],
        compiler_params=_params("arbitrary"),
    )(*outs, *lses)


def _att_bwd_prep(datt, att, lse):
    t, w = datt.shape
    tm = min(512, t)

    def body(da_ref, at_ref, l_ref, *rest):
        outs, dbuf, sbuf = rest[:-2], rest[-2], rest[-1]
        dav = da_ref[...]
        prod = dav * at_ref[...]
        lane = lax.broadcasted_iota(jnp.int32, (tm, LANE), 1)
        for k in range(w // LANE):
            cols = slice(k * LANE, (k + 1) * LANE)
            dbuf[k] = dav[:, cols]
            delta = jnp.concatenate(
                [jnp.broadcast_to(jnp.sum(prod[:, k * LANE + e * ATT_DIM:k * LANE + (e + 1) * ATT_DIM], axis=-1, keepdims=True),
                                  (tm, ATT_DIM)) for e in range(LANE // ATT_DIM)], axis=1)
            sbuf[k] = jnp.where(lane % ATT_DIM < STAT_LANES, l_ref[:, cols], delta)
        for k, dil in enumerate(DILATIONS):
            _regroup_store(dbuf, outs[2 * k], dil)
            _regroup_store(sbuf, outs[2 * k + 1], dil)

    tile = pl.BlockSpec((tm, w), lambda i: (i, 0))
    res = pl.pallas_call(
        body, name="att_bwd_prep", grid=(t // tm,),
        in_specs=[tile] * 3,
        out_specs=[_regrouped_spec(tm, dil, w) for dil in DILATIONS for _ in range(2)],
        out_shape=[jax.ShapeDtypeStruct((dil, t // dil, w), dt) for dil in DILATIONS for dt in (BF16, F32)],
        scratch_shapes=[_chunk_scratch(tm, w)] * 2,
        compiler_params=_params("arbitrary"),
    )(datt, att, lse)
    return [(res[2 * k], res[2 * k + 1]) for k in range(len(DILATIONS))]


def _att_bwd(ua, da, stat, dil, rides=None):
    sub = ua.shape[1]
    _, tq, nq, qb, gs = _att_tiles(sub * dil, dil)
    scale = ATT_DIM ** -0.5

    def body(q_ref, kc_ref, kp_ref, vc_ref, vp_ref, da_ref, st_ref, dq_ref, dk_ref, dv_ref, kx, vx, ck, cv):
        step = pl.program_id(2)
        tile = nq - 1 - step

        @pl.when(step == 0)
        def _():
            ck[...] = jnp.zeros_like(ck)
            cv[...] = jnp.zeros_like(cv)

        kx[:, 0:ATT_BLOCK, :] = kp_ref[...]
        kx[:, ATT_BLOCK:, :] = kc_ref[...]
        vx[:, 0:ATT_BLOCK, :] = vp_ref[...]
        vx[:, ATT_BLOCK:, :] = vc_ref[...]
        band, cur_keys = _band_mask()
        head0 = _head0_lanes()
        items = [(r, b) for r in range(gs) for b in range(qb)]
        n = range(len(items))
        rows = lambda b: slice(b * ATT_BLOCK, (b + 1) * ATT_BLOCK)
        keys = lambda b: slice(b * ATT_BLOCK, (b + 2) * ATT_BLOCK)
        qqs = [_stack_heads(q_ref[r, rows(b), :] * jnp.asarray(scale, BF16), head0) for r, b in items]
        dds = [_stack_heads(da_ref[r, rows(b), :], head0) for r, b in items]
        sts = [_dot(kx[r, keys(b), :], qqs[i], NT) for i, (r, b) in enumerate(items)]
        dpts = [_dot(vx[r, keys(b), :], dds[i], NT) for i, (r, b) in enumerate(items)]
        pts, dsts = [], []
        for i, (r, b) in enumerate(items):
            mask = band if b > 0 else band & (cur_keys | (tile > 0))
            stat = st_ref[r, rows(b), :].T
            row = lambda k: jnp.concatenate([stat[e * ATT_DIM + k:e * ATT_DIM + k + 1, :] for e in range(2)], axis=1)
            pt = jnp.where(mask, jnp.exp(sts[i] - row(0)), 0.0)
            dsts.append((pt * (dpts[i] - row(STAT_LANES))).astype(BF16))
            pts.append(pt.astype(BF16))
        dqs = [_dot(dsts[i], kx[r, keys(b), :], TN) for i, (r, b) in enumerate(items)]
        dkbs = [_dot(dsts[i], qqs[i]) for i in n]
        dvbs = [_dot(pts[i], dds[i]) for i in n]
        for i, (r, b) in enumerate(items):
            dq_ref[r, rows(b), :] = (_unstack_heads(dqs[i], head0) * scale).astype(BF16)
            if b > 0:
                dk_ref[r, rows(b - 1), :] = (dkbs[i - 1][ATT_BLOCK:] + dkbs[i][0:ATT_BLOCK]).astype(BF16)
                dv_ref[r, rows(b - 1), :] = (dvbs[i - 1][ATT_BLOCK:] + dvbs[i][0:ATT_BLOCK]).astype(BF16)
        for r in range(gs):
            first, last = r * qb, r * qb + qb - 1
            dk_ref[r, rows(qb - 1), :] = (dkbs[last][ATT_BLOCK:] + ck[r]).astype(BF16)
            dv_ref[r, rows(qb - 1), :] = (dvbs[last][ATT_BLOCK:] + cv[r]).astype(BF16)
            ck[r] = dkbs[first][0:ATT_BLOCK]
            cv[r] = dvbs[first][0:ATT_BLOCK]

    ti = lambda i: nq - 1 - i
    out = pl.BlockSpec((gs, tq, LANE), lambda g, p, i: (g, ti(i), p))
    shape = jax.ShapeDtypeStruct((dil, sub, ATT_WIDTH), BF16)
    return _pallas(
        body, rides, name=f"att_bwd_d{dil}", grid=(dil // gs, PAIRS, nq),
        in_specs=_att_in_specs(tq, qb, ti, gs) + [out, out],
        out_specs=[out, out, out], out_shape=[shape] * 3,
        scratch_shapes=[pltpu.VMEM((gs, tq + ATT_BLOCK, LANE), BF16)] * 2 + [pltpu.VMEM((gs, ATT_BLOCK, LANE), F32)] * 2,
        sem=("arbitrary", "arbitrary", "arbitrary"), args=[ua, ua, ua, ua, ua, da, stat])


class _Reduction:
    def __init__(self, place, names, grads):
        self.place, self.names, self.grads = place, names, grads

    def pair(self):
        return _pair_ride(self.grads)

    def chips(self, got):
        self.got = got
        return _chip_ride([_pair_sum(self.place, g, r, f"pair_sum_{n}") for g, r, n in zip(self.grads, got, self.names)])

    def halves(self, others):
        return [_chip_sum(self.place, g, r, o, f"chip_sum_{n}")
                for g, r, o, n in zip(self.grads, self.got, others, self.names)]


def _step(x, target, gains, w, place=None):
    t = x.shape[0]
    ex = place is not None
    g_ffn1, g_mix, g_ret, g_ffn2, g_fin = gains
    w = list(w)
    tabs = _retention_tables(t)
    red = lambda names, grads: _Reduction(place, names, grads) if ex else None
    ride = lambda r: [r] if ex else None

    if ex:
        w[0:3] = _run(_gather_ride(w[0:3]), "gather_ffn1_weights")
    (h1, xn1, *hid1, act1), rest = _ffn_fwd(x, g_ffn1, *w[0:3], "ffn1_fwd", ride(_gather_ride(w[3:])) if ex else None)
    if ex:
        w[3:] = rest[0]
    wg1, wu1, wd1, win, wo, wg2, wu2, wd2 = w
    wo2 = wo.reshape(wo.shape[0] * wo.shape[1], wo.shape[2])
    xnm, u, *uas = _inproj_fwd(h1, g_mix, win)
    raw, mix_r = _ret_fwd(u, g_ret, tabs)
    branches = [_att_fwd(ua, dil) for ua, dil in zip(uas, DILATIONS)]
    mix_a, att, lse = _att_combine([b[0] for b in branches], [b[1] for b in branches], t)
    h2 = _outproj_fwd(h1, mix_r, mix_a, wo2)
    (dh3, xn2, *hid2, act2, loss_p, dg_fin), _ = _ffn_fwd(h2, g_ffn2, wg2, wu2, wd2, "ffn2_fwd", head=(g_fin, target))

    (dwd2,), _ = _ffn_wgrad_down(act2, dh3, "ffn2_wgrad_down")
    dwd2 = dwd2.reshape(wd2.shape)
    r_d2 = red(["ffn2_w_down"], [dwd2])
    (dh2, dga2, dua2, dg_ffn2), e = _ffn_bwd_data(dh3, h2, g_ffn2, *hid2, wg2, wu2, wd2, "ffn2_bwd",
                                                  ex and [r_d2.pair()])
    (dwg2, dwu2), e = _ffn_wgrad_gu(xn2, [dga2, dua2], "ffn2_wgrad_gu", ex and [r_d2.chips(e[0])])
    dwg2, dwu2 = dwg2.reshape(wg2.shape), dwu2.reshape(wu2.shape)
    r_gu2 = red(["ffn2_w_gate", "ffn2_w_up"], [dwg2, dwu2])
    (dmix_r, dmix_a), e = _outproj_bwd(dh2, wo2, ex and [r_gu2.pair(), _finish_ride(r_d2.halves(e[0]))])
    if ex:
        got_gu2, (dwd2,) = e
    hw = RET_WIDTH // (wo.shape[1])
    dwo = jnp.concatenate([_tn_matmul(mix_r, dh2, dh2.shape[1], "wo_grad_r").reshape(hw, wo.shape[1], wo.shape[2]),
                           _tn_matmul(mix_a, dh2, dh2.shape[1], "wo_grad_a").reshape(hw, wo.shape[1], wo.shape[2])])
    r_wo = red(["w_out"], [dwo])
    (dq_r, dgt_r, dret, dg_ret), e = _ret_bwd_q(dmix_r, raw, u, g_ret, tabs, ex and [r_gu2.chips(got_gu2)])
    (dk_r, dv_r), e = _ret_bwd_kv(dret, u, tabs, ex and [r_wo.pair(), _finish_ride(r_gu2.halves(e[0]))])
    if ex:
        got_wo, (dwg2, dwu2) = e
    prep = _att_bwd_prep(dmix_a, att, lse)
    p1, e = _att_bwd(uas[0], *prep[0], DILATIONS[0], ex and [r_wo.chips(got_wo)])
    p4, e = _att_bwd(uas[1], *prep[1], DILATIONS[1], ex and [_finish_ride(r_wo.halves(e[0]))])
    if ex:
        (dwo,), = e
    p16, _ = _att_bwd(uas[2], *prep[2], DILATIONS[2])
    dh1, du, dg_mix = _inproj_bwd([dq_r, dk_r, dv_r, dgt_r], [p1, p4, p16], h1, g_mix, dh2, win)
    dwin = _tn_matmul(xnm, du, win.shape[2], "win_grad")
    r_in = red(["w_in"], [dwin])
    (dwd1,), e = _ffn_wgrad_down(act1, dh1, "ffn1_wgrad_down", ex and [r_in.pair()])
    dwd1 = dwd1.reshape(wd1.shape)
    r_d1 = red(["ffn1_w_down"], [dwd1])
    got_in = e
    (dx, dga1, dua1, dg_ffn1), _ = _ffn_bwd_data(dh1, x, g_ffn1, *hid1, wg1, wu1, wd1, "ffn1_bwd")
    (dwg1,), e = _ffn_wgrad_gu(xn1, [dga1], "ffn1_wgrad_gate", ex and [r_in.chips(got_in[0]), r_d1.pair()])
    dwg1 = dwg1.reshape(wg1.shape)
    if ex:
        oth_in, got_d1 = e
        r_g1 = red(["ffn1_w_gate"], [dwg1])
        got_g1 = _run(r_g1.pair(), "pair_exchange_ffn1_gate")
    (dwu1,), e = _ffn_wgrad_gu(xn1, [dua1], "ffn1_wgrad_up",
                               ex and [_finish_ride(r_in.halves(oth_in)), r_d1.chips(got_d1), r_g1.chips(got_g1)])
    dwu1 = dwu1.reshape(wu1.shape)
    gain_parts = [dg_ffn1, dg_mix, dg_ret, dg_ffn2, dg_fin]
    if not ex:
        return loss_p, dx, [dwg1, dwu1, dwd1, dwin, dwo, dwg2, dwu2, dwd2], gain_parts
    (dwin,), oth_d1, oth_g1 = e
    r_u1 = red(["ffn1_w_up"], [dwu1])
    got_u1 = _run(r_u1.pair(), "pair_exchange_ffn1_up")
    oth_u1 = _run(r_u1.chips(got_u1), "chip_exchange_ffn1_up")
    last = r_g1.halves(oth_g1) + r_u1.halves(oth_u1) + r_d1.halves(oth_d1)
    dwg1, dwu1, dwd1, gall = _run(_finish_ride(last, _pack_gains(gain_parts, x.shape[1])), "finish_exchange_ffn1")
    return loss_p, dx, [dwg1, dwu1, dwd1, dwin, dwo, dwg2, dwu2, dwd2], gall


N_DEV = 8
GAIN_ROWS = 8


def _place():
    x, y, c = lax.axis_index("x"), lax.axis_index("y"), lax.axis_index("c")
    chips = [(1 - x, y), (x, 1 - y), (1 - x, 1 - y)]
    return x, y, c, chips


ROW_QUARTERS = 4


def _place_shards(place, ws):
    n = len(ws)

    def body(place_ref, *refs):
        for w_ref, o_ref in zip(refs[:n], refs[n:]):
            o_ref[...] = w_ref[...].astype(BF16)

    quarter = lambda w: (w.shape[0] // ROW_QUARTERS, w.shape[1])
    return pl.pallas_call(
        body, name="place_shards",
        grid_spec=pltpu.PrefetchScalarGridSpec(
            num_scalar_prefetch=1, grid=(ROW_QUARTERS,),
            in_specs=[pl.BlockSpec(quarter(w), lambda i, pr: (i, 0)) for w in ws],
            out_specs=[pl.BlockSpec((None,) + quarter(w), lambda i, pr: (pr[0], i, 0)) for w in ws]),
        out_shape=[jax.ShapeDtypeStruct((N_SHARD,) + w.shape, BF16) for w in ws],
        compiler_params=_params("arbitrary"),
    )(place, *ws)


def _gather_ride(bufs):
    na = len(bufs)

    def legs(outs, sems):
        send_sem, recv_sem, fsend_sem, frecv_sem = sems
        x, y, c, chips = _place()

        def half(a, idx, which):
            hr = outs[a].shape[1] // 2
            return outs[a].at[idx, pl.ds(which * hr, hr)]

        def ici(a, j, idx):
            px, py = chips[j]
            return pltpu.make_async_remote_copy(
                src_ref=half(a, idx, c), dst_ref=half(a, idx, c),
                send_sem=send_sem.at[a, j], recv_sem=recv_sem.at[a, j], device_id=(px, py, c), device_id_type=MESH)

        def d2d(a, j, idx, which):
            return pltpu.make_async_remote_copy(
                src_ref=half(a, idx, which), dst_ref=half(a, idx, which),
                send_sem=fsend_sem.at[a, j], recv_sem=frecv_sem.at[a, j], device_id=(x, y, 1 - c), device_id_type=MESH)

        return 2 * x + y, c, chips, ici, d2d

    def start(ins, outs, sems):
        me, _, _, ici, _ = legs(outs, sems)
        for a in range(na):
            for j in range(3):
                ici(a, j, me).start()

    def finish(ins, outs, sems):
        me, c, chips, ici, d2d = legs(outs, sems)
        passed = []
        for a in range(na):
            for j, (px, py) in enumerate(chips):
                ici(a, j, 2 * px + py).wait_recv()
                cp = d2d(a, j, 2 * px + py, c)
                cp.start()
                passed.append(cp)
        for a in range(na):
            for j, (px, py) in enumerate(chips):
                d2d(a, j, 2 * px + py, 1 - c).wait_recv()
        for a in range(na):
            for j in range(3):
                ici(a, j, me).wait_send()
        for cp in passed:
            cp.wait_send()

    return _Ride(bufs, [jax.ShapeDtypeStruct(b.shape, b.dtype) for b in bufs], [pltpu.SemaphoreType.DMA((na, 3))] * 4,
                 start, finish, {a: a for a in range(na)})


def _pair_ride(grads):
    na = len(grads)

    def copies(ins, outs, sems):
        send_sem, recv_sem = sems
        x, y, c, _ = _place()
        res = []
        for a in range(na):
            hr = ins[a].shape[1] // 2
            res.append(pltpu.make_async_remote_copy(
                src_ref=ins[a].at[:, pl.ds((1 - c) * hr, hr)], dst_ref=outs[a],
                send_sem=send_sem.at[a], recv_sem=recv_sem.at[a], device_id=(x, y, 1 - c), device_id_type=MESH))
        return res

    def start(ins, outs, sems):
        for cp in copies(ins, outs, sems):
            cp.start()

    def finish(ins, outs, sems):
        for cp in copies(ins, outs, sems):
            cp.wait()

    return _Ride(grads, [jax.ShapeDtypeStruct((g.shape[0], g.shape[1] // 2, g.shape[2]), g.dtype) for g in grads],
                 [pltpu.SemaphoreType.DMA((na,))] * 2, start, finish)


def _chip_ride(sums):
    na = len(sums)

    def copies(ins, outs, sems):
        send_sem, recv_sem = sems
        x, y, c, chips = _place()
        res = []
        for a in range(na):
            for j, (px, py) in enumerate(chips):
                res.append(pltpu.make_async_remote_copy(
                    src_ref=ins[a].at[2 * px + py], dst_ref=outs[a].at[j],
                    send_sem=send_sem.at[a, j], recv_sem=recv_sem.at[a, j], device_id=(px, py, c), device_id_type=MESH))
        return res

    def start(ins, outs, sems):
        for cp in copies(ins, outs, sems):
            cp.start()

    def finish(ins, outs, sems):
        for cp in copies(ins, outs, sems):
            cp.wait()

    return _Ride(sums, [jax.ShapeDtypeStruct((3,) + s.shape[1:], s.dtype) for s in sums],
                 [pltpu.SemaphoreType.DMA((na, 3))] * 2, start, finish)


def _finish_ride(grads, gpack=None):
    na = len(grads)

    def halves(outs, sems, which):
        x, y, c, _ = _place()
        res = []
        for a in range(na):
            hr = outs[a].shape[0] // 2
            rows = outs[a].at[pl.ds((c if which == "mine" else 1 - c) * hr, hr)]
            res.append(pltpu.make_async_remote_copy(
                src_ref=rows, dst_ref=rows, send_sem=sems[0].at[a], recv_sem=sems[1].at[a],
                device_id=(x, y, 1 - c), device_id_type=MESH))
        return res

    def gains(ins, outs, sems):
        x, y, c, _ = _place()
        dev = 4 * x + 2 * y + c
        g_in, g_out = ins[na], outs[na]
        own = pltpu.make_async_copy(g_in, g_out.at[dev], sems[2])
        sends, lands = [], []
        for k in range(N_DEV - 1):
            bx, by, bc = (k + 1) // 4, ((k + 1) // 2) % 2, (k + 1) % 2
            peer = (jnp.bitwise_xor(x, bx), jnp.bitwise_xor(y, by), jnp.bitwise_xor(c, bc))
            sends.append(pltpu.make_async_remote_copy(
                src_ref=g_in, dst_ref=g_out.at[dev], send_sem=sems[3].at[k], recv_sem=sems[4].at[k],
                device_id=peer, device_id_type=MESH))
            slot = g_out.at[jnp.bitwise_xor(dev, k + 1)]
            lands.append(pltpu.make_async_remote_copy(
                src_ref=slot, dst_ref=slot, send_sem=sems[3].at[k], recv_sem=sems[4].at[k],
                device_id=peer, device_id_type=MESH))
        return own, sends, lands

    def start(ins, outs, sems):
        for cp in halves(outs, sems, "mine"):
            cp.start()
        if gpack is not None:
            own, sends, _ = gains(ins, outs, sems)
            own.start()
            for cp in sends:
                cp.start()

    def finish(ins, outs, sems):
        for cp in halves(outs, sems, "sibling's"):
            cp.wait_recv()
        if gpack is not None:
            own, sends, lands = gains(ins, outs, sems)
            for cp in lands:
                cp.wait_recv()
            for cp in sends:
                cp.wait_send()
            own.wait()
        for cp in halves(outs, sems, "mine"):
            cp.wait_send()

    shapes = [jax.ShapeDtypeStruct(g.shape, g.dtype) for g in grads]
    sems = [pltpu.SemaphoreType.DMA((na,))] * 2
    if gpack is None:
        return _Ride(grads, shapes, sems, start, finish, {a: a for a in range(na)})
    return _Ride(list(grads) + [gpack], shapes + [jax.ShapeDtypeStruct((N_DEV,) + gpack.shape, gpack.dtype)],
                 sems + [pltpu.SemaphoreType.DMA, pltpu.SemaphoreType.DMA((N_DEV - 1,)), pltpu.SemaphoreType.DMA((N_DEV - 1,))],
                 start, finish, {a: a for a in range(na)})


def _pair_sum(place, grad, got, name):
    ns, r, cols = grad.shape
    hr = r // 2

    def body(place_ref, g_ref, r_ref, o_ref):
        o_ref[...] = (g_ref[...] + r_ref[...]).astype(BF16)

    return pl.pallas_call(
        body, name=name,
        grid_spec=pltpu.PrefetchScalarGridSpec(
            num_scalar_prefetch=1, grid=(ns,),
            in_specs=[pl.BlockSpec((None, hr, cols), lambda s, pr: (s, pr[1], 0)),
                      pl.BlockSpec((None, hr, cols), lambda s, pr: (s, 0, 0))],
            out_specs=pl.BlockSpec((None, hr, cols), lambda s, pr: (s, 0, 0))),
        out_shape=jax.ShapeDtypeStruct((ns, hr, cols), BF16),
        compiler_params=_params("arbitrary"),
    )(place, grad, got)


def _chip_sum(place, grad, got, others, name):
    ns, r, cols = grad.shape
    hr = r // 2
    nb = 2
    tr = hr // nb

    def body(place_ref, g_ref, r_ref, o3_ref, o_ref):
        acc = g_ref[...] + r_ref[...]
        for j in range(3):
            acc = acc + o3_ref[j].astype(F32)
        o_ref[...] = acc

    return pl.pallas_call(
        body, name=name,
        grid_spec=pltpu.PrefetchScalarGridSpec(
            num_scalar_prefetch=1, grid=(nb,),
            in_specs=[pl.BlockSpec((None, tr, cols), lambda i, pr: (pr[0], pr[1] * nb + i, 0)),
                      pl.BlockSpec((None, tr, cols), lambda i, pr: (pr[0], i, 0)),
                      pl.BlockSpec((3, tr, cols), lambda i, pr: (0, i, 0))],
            out_specs=pl.BlockSpec((tr, cols), lambda i, pr: (pr[1] * nb + i, 0))),
        out_shape=jax.ShapeDtypeStruct((r, cols), F32),
        compiler_params=_params("arbitrary"),
    )(place, grad, got, others)


def _pack_gains(parts, d):
    def body(*refs):
        ins, o_ref = refs[:-1], refs[-1]
        o_ref[...] = jnp.zeros_like(o_ref)
        for k, r in enumerate(ins):
            o_ref[k:k + 1, 0:r.shape[1]] = jnp.sum(r[...], axis=0, keepdims=True)

    return pl.pallas_call(
        body, name="pack_gains", out_shape=jax.ShapeDtypeStruct((GAIN_ROWS, d), F32),
    )(*parts)


def _adamw_math(w, g, m, v):
    m = ADAM_B1 * m + (1.0 - ADAM_B1) * g
    v = ADAM_B2 * v + (1.0 - ADAM_B2) * jnp.square(g)
    m_hat = m / (1.0 - ADAM_B1 ** ADAM_STEP)
    v_hat = v / (1.0 - ADAM_B2 ** ADAM_STEP)
    return -ADAM_LR * (m_hat / (jnp.sqrt(v_hat) + ADAM_EPS) + ADAM_WD * w), m, v


def _adamw(ws, gs, ms, vs):
    n = len(ws)

    def body(*refs):
        ins, outs = refs[:4 * n], refs[4 * n:]
        for k in range(n):
            w_ref, g_ref, m_ref, v_ref = ins[4 * k:4 * k + 4]
            go_ref, d_ref, nm_ref, nv_ref = outs[4 * k:4 * k + 4]
            g = g_ref[...]
            go_ref[...] = g
            d_ref[...], nm_ref[...], nv_ref[...] = _adamw_math(w_ref[...], g, m_ref[...], v_ref[...])

    parts = 2 * ROW_QUARTERS
    tile = lambda w: pl.BlockSpec((w.shape[0] // parts, w.shape[1]), lambda i: (i, 0))
    res = pl.pallas_call(
        body, name="adamw_shards", grid=(parts,),
        in_specs=[tile(w) for w in ws for _ in range(4)], out_specs=[tile(w) for w in ws for _ in range(4)],
        out_shape=[jax.ShapeDtypeStruct(w.shape, F32) for w in ws for _ in range(4)],
        compiler_params=_params("arbitrary"),
    )(*[a for quad in zip(ws, gs, ms, vs) for a in quad])
    return [res[4 * k:4 * k + 4] for k in range(n)]


def _adamw_gain(gall, row, w, m, v, name):
    n = w.shape[1]

    def body(ga_ref, w_ref, m_ref, v_ref, g_ref, d_ref, nm_ref, nv_ref):
        g = ga_ref[0, row:row + 1, 0:n]
        for k in range(1, N_DEV):
            g = g + ga_ref[k, row:row + 1, 0:n]
        g_ref[...] = g
        d_ref[...], nm_ref[...], nv_ref[...] = _adamw_math(w_ref[...], g, m_ref[...], v_ref[...])

    return pl.pallas_call(
        body, name=name, out_shape=[jax.ShapeDtypeStruct((1, n), F32)] * 4,
    )(gall, w, m, v)


def kernel(x, norm_ffn1, ffn1_w_gate, ffn1_w_up, ffn1_w_down, norm_mix, w_in, ret_norm_gain, w_out, norm_ffn2, ffn2_w_gate, ffn2_w_up, ffn2_w_down, norm_final, loss_target, m_norm_ffn1, m_ffn1_w_gate, m_ffn1_w_up, m_ffn1_w_down, m_norm_mix, m_w_in, m_ret_norm_gain, m_w_out, m_norm_ffn2, m_ffn2_w_gate, m_ffn2_w_up, m_ffn2_w_down, m_norm_final, v_norm_ffn1, v_ffn1_w_gate, v_ffn1_w_up, v_ffn1_w_down, v_norm_mix, v_w_in, v_ret_norm_gain, v_w_out, v_norm_ffn2, v_ffn2_w_gate, v_ffn2_w_up, v_ffn2_w_down, v_norm_final):
    d = x.shape[-1]
    mats = [ffn1_w_gate, ffn1_w_up, ffn1_w_down, w_in, w_out, ffn2_w_gate, ffn2_w_up, ffn2_w_down]
    mats_m = [m_ffn1_w_gate, m_ffn1_w_up, m_ffn1_w_down, m_w_in, m_w_out, m_ffn2_w_gate, m_ffn2_w_up, m_ffn2_w_down]
    mats_v = [v_ffn1_w_gate, v_ffn1_w_up, v_ffn1_w_down, v_w_in, v_w_out, v_ffn2_w_gate, v_ffn2_w_up, v_ffn2_w_down]
    mat_names = ["ffn1_w_gate", "ffn1_w_up", "ffn1_w_down", "w_in", "w_out", "ffn2_w_gate", "ffn2_w_up", "ffn2_w_down"]
    gains = [norm_ffn1, norm_mix, ret_norm_gain, norm_ffn2, norm_final.reshape(1, d)]
    gains_m = [m_norm_ffn1, m_norm_mix, m_ret_norm_gain, m_norm_ffn2, m_norm_final.reshape(1, d)]
    gains_v = [v_norm_ffn1, v_norm_mix, v_ret_norm_gain, v_norm_ffn2, v_norm_final.reshape(1, d)]
    gain_names = ["norm_ffn1", "norm_mix", "ret_norm_gain", "norm_ffn2", "norm_final"]

    turned = lambda n: n.endswith(("w_gate", "w_up"))
    local = lambda a, n: jnp.swapaxes(a, 1, 2)[0] if turned(n) else a[0]
    back = lambda a, n: jnp.swapaxes(a[None], 1, 2) if turned(n) else a[None]
    shards = [local(w, n) for w, n in zip(mats, mat_names)]
    place = jnp.stack([2 * lax.axis_index("x") + lax.axis_index("y"), lax.axis_index("c")]).astype(jnp.int32)
    placed = _place_shards(place, shards)
    loss_p, dx, shard_grads, gall = _step(x[0], loss_target[0], gains, placed, place)

    out_g, out_d, out_m, out_v = {}, {}, {}, {}
    updates = _adamw(shards, shard_grads, [local(m, n) for m, n in zip(mats_m, mat_names)],
                     [local(v, n) for v, n in zip(mats_v, mat_names)])
    for n, quad in zip(mat_names, updates):
        out_g[n], out_d[n], out_m[n], out_v[n] = [back(a, n) for a in quad]
    for row, (n, w, m, v) in enumerate(zip(gain_names, gains, gains_m, gains_v)):
        res = _adamw_gain(gall, row, w, m, v, f"adamw_{n}")
        shape = (d,) if n == "norm_final" else w.shape
        out_g[n], out_d[n], out_m[n], out_v[n] = [r.reshape(shape) for r in res]

    loss = lax.psum(jnp.sum(loss_p), ("x", "y", "c"))
    order = ["norm_ffn1", "ffn1_w_gate", "ffn1_w_up", "ffn1_w_down", "norm_mix", "w_in", "ret_norm_gain", "w_out",
             "norm_ffn2", "ffn2_w_gate", "ffn2_w_up", "ffn2_w_down", "norm_final"]
    return (loss, dx[None], *[out_g[n] for n in order], *[out_d[n] for n in order],
            *[out_m[n] for n in order], *[out_v[n] for n in order])
```

```python
import functools

import jax
import jax.numpy as jnp
from jax import lax
from jax.experimental import pallas as pl
from jax.experimental.pallas import tpu as pltpu

F32 = jnp.float32
BF16 = jnp.bfloat16
MESH = pl.DeviceIdType.MESH

NORM_EPS = 1e-6
GN_EPS = 1e-6
ROPE_BASE = 10000.0
RET_HEADS = 4
RET_DIM = 128
RET_WIDTH = 512
RET_CHUNK = 128
ATT_DIM = 64
ATT_WIDTH = 512
ATT_BLOCK = 128
DILATIONS = (1, 4, 16)
LANE = 128
N_SHARD = 4
ADAM_LR, ADAM_B1, ADAM_B2, ADAM_EPS, ADAM_WD, ADAM_STEP = 0.001, 0.9, 0.999, 1e-08, 0.01, 10

V7X_VMEM_BYTES = 64 * 1024 * 1024
VMEM_LIMIT = V7X_VMEM_BYTES - 8 * 1024 * 1024

NT = (((1,), (1,)), ((), ()))
TN = (((0,), (0,)), ((), ()))


def _params(*sem):
    return pltpu.CompilerParams(dimension_semantics=sem, vmem_limit_bytes=VMEM_LIMIT)


def _dot(a, b, dims=None):
    if dims is None:
        return jnp.dot(a, b, preferred_element_type=F32)
    return lax.dot_general(a, b, dims, preferred_element_type=F32)


def _sigmoid(x):
    return 1.0 / (1.0 + jnp.exp(-x))


def _load_weights(pairs, sems):
    copies = [pltpu.make_async_copy(src, dst, sems.at[k]) for k, (src, dst) in enumerate(pairs)]
    for cp in copies:
        cp.start()
    for cp in copies:
        cp.wait()


def _rows8(v):
    r, c = v.shape
    return v.reshape(r // 8, 8, c).sum(axis=0)


class _Ride:
    def __init__(self, inputs, out_shapes, sems, start, finish, aliases=None):
        self.inputs, self.out_shapes, self.sems = list(inputs), list(out_shapes), list(sems)
        self.start, self.finish, self.aliases = start, finish, dict(aliases or {})


def _pallas(body, rides, *, name, in_specs, out_specs, out_shape, args, grid=(), scratch_shapes=(), sem=()):
    rides = [r for r in (rides or []) if r is not None]
    n_in, n_out, n_scr = len(args), len(out_shape), len(scratch_shapes)
    hbm = pl.BlockSpec(memory_space=pl.ANY)
    r_in = [a for r in rides for a in r.inputs]
    r_out = [s for r in rides for s in r.out_shapes]
    r_sem = [s for r in rides for s in r.sems]
    aliases, spans, ki, ko, ks = {}, [], 0, 0, 0
    for r in rides:
        aliases.update({n_in + ki + i: n_out + ko + o for i, o in r.aliases.items()})
        spans.append((ki, ko, ks))
        ki, ko, ks = ki + len(r.inputs), ko + len(r.out_shapes), ks + len(r.sems)

    def wrapped(*refs):
        ins, rin = refs[:n_in], refs[n_in:n_in + len(r_in)]
        o0 = n_in + len(r_in)
        outs, rout = refs[o0:o0 + n_out], refs[o0 + n_out:o0 + n_out + len(r_out)]
        s0 = o0 + n_out + len(r_out)
        scr, rsem = refs[s0:s0 + n_scr], refs[s0 + n_scr:]
        part = lambda r, k: (rin[spans[k][0]:spans[k][0] + len(r.inputs)], rout[spans[k][1]:spans[k][1] + len(r.out_shapes)],
                             rsem[spans[k][2]:spans[k][2] + len(r.sems)])
        first = functools.reduce(jnp.logical_and, [pl.program_id(k) == 0 for k in range(len(grid))], True)
        last = functools.reduce(jnp.logical_and, [pl.program_id(k) == grid[k] - 1 for k in range(len(grid))], True)
        if rides:
            @pl.when(first)
            def _():
                for k, r in enumerate(rides):
                    r.start(*part(r, k))

        body(*ins, *outs, *scr)
        if rides:
            @pl.when(last)
            def _():
                for k, r in enumerate(rides):
                    r.finish(*part(r, k))

    res = pl.pallas_call(
        wrapped, name=name, grid=grid,
        in_specs=list(in_specs) + [hbm] * len(r_in), out_specs=list(out_specs) + [hbm] * len(r_out),
        out_shape=list(out_shape) + r_out, input_output_aliases=aliases,
        scratch_shapes=list(scratch_shapes) + r_sem,
        compiler_params=pltpu.CompilerParams(dimension_semantics=sem, vmem_limit_bytes=VMEM_LIMIT) if grid else None,
    )(*args, *r_in)
    extras = [list(res[n_out + ko:n_out + ko + len(r.out_shapes)]) for r, (_, ko, _) in zip(rides, spans)]
    return list(res[:n_out]), extras


def _run(ride, name):
    def body(*refs):
        n_in, n_out = len(ride.inputs), len(ride.out_shapes)
        parts = refs[:n_in], refs[n_in:n_in + n_out], refs[n_in + n_out:]
        ride.start(*parts)
        ride.finish(*parts)

    hbm = pl.BlockSpec(memory_space=pl.ANY)
    return list(pl.pallas_call(
        body, name=name, in_specs=[hbm] * len(ride.inputs), out_specs=[hbm] * len(ride.out_shapes),
        out_shape=ride.out_shapes, input_output_aliases=ride.aliases, scratch_shapes=ride.sems,
    )(*ride.inputs))


def _loss_head(hv, gain_ref, tg_ref, loss_ref, dgain_ref):
    d = hv.shape[1]
    r = lax.rsqrt(jnp.mean(hv * hv, axis=-1, keepdims=True) + NORM_EPS)
    xh = hv * r
    err = xh * gain_ref[...] - tg_ref[...]
    sq = _rows8(jnp.square(err))
    loss_ref[...] += 0.5 * functools.reduce(jnp.add, [sq[:, k * LANE:(k + 1) * LANE] for k in range(d // LANE)]) / d
    dy = err / d
    dgain_ref[...] += _rows8(dy * xh)
    dxh = dy * gain_ref[...]
    return r * (dxh - xh * jnp.mean(dxh * xh, axis=-1, keepdims=True))


V7X_MXU_TILE = 256
FFN_CHUNK_TILES = 3


def _hidden_chunks(f):
    step = FFN_CHUNK_TILES * V7X_MXU_TILE
    return [slice(s, min(s + step, f)) for s in range(0, f, step)]


def _flat(w):
    return w.reshape(w.shape[0] * w.shape[1], w.shape[2])


def _ffn_fwd(x, gain, wg, wu, wd, name, rides=None, head=None):
    t, d = x.shape
    wg, wu, wd = _flat(wg), _flat(wu), _flat(wd)
    f = wg.shape[0]
    tm = min(512, t)
    nh = 0 if head is None else 2

    def body(*refs):
        x_ref, gain_ref = refs[:2]
        wg_hbm, wu_hbm, wd_hbm, h_ref, xn_ref, g_ref, u_ref, a_ref = refs[2 + nh:10 + nh]
        sums = refs[10 + nh:12 + nh]
        wg_v, wu_v, wd_v, sems = refs[-4:]

        @pl.when(pl.program_id(0) == 0)
        def _():
            _load_weights([(wg_hbm, wg_v), (wu_hbm, wu_v), (wd_hbm, wd_v)], sems)
            if head is not None:
                for s_ref in sums:
                    s_ref[...] = jnp.zeros_like(s_ref)

        xv = x_ref[...]
        r = lax.rsqrt(jnp.mean(xv * xv, axis=-1, keepdims=True) + NORM_EPS)
        xn = (xv * r * gain_ref[...]).astype(BF16)
        xn_ref[...] = xn
        acc = jnp.zeros((tm, d), F32)
        for c in _hidden_chunks(f):
            g = _dot(xn, wg_v[c, :], NT)
            u = _dot(xn, wu_v[c, :], NT)
            g_ref[:, c] = g.astype(BF16)
            u_ref[:, c] = u.astype(BF16)
            a = (g * _sigmoid(g) * u).astype(BF16)
            a_ref[:, c] = a
            acc = acc + _dot(a, wd_v[c, :])
        hv = xv + 0.5 * acc
        h_ref[...] = hv if head is None else _loss_head(hv, refs[2], refs[3], *sums)

    hbm = pl.BlockSpec(memory_space=pl.ANY)
    hid = pl.BlockSpec((tm, f), lambda i: (i, 0))
    tile = pl.BlockSpec((tm, d), lambda i: (i, 0))
    row = pl.BlockSpec((1, d), lambda i: (0, 0))
    sums = [] if head is None else [(pl.BlockSpec((8, LANE), lambda i: (0, 0)), jax.ShapeDtypeStruct((8, LANE), F32)),
                                    (pl.BlockSpec((8, d), lambda i: (0, 0)), jax.ShapeDtypeStruct((8, d), F32))]
    return _pallas(
        body, rides, name=name, grid=(t // tm,),
        in_specs=[tile, row] + ([] if head is None else [row, tile]) + [hbm, hbm, hbm],
        out_specs=[tile, tile, hid, hid, hid] + [s for s, _ in sums],
        out_shape=[jax.ShapeDtypeStruct((t, d), F32), jax.ShapeDtypeStruct((t, d), BF16)]
        + [jax.ShapeDtypeStruct((t, f), BF16)] * 3 + [s for _, s in sums],
        scratch_shapes=[pltpu.VMEM(wg.shape, BF16), pltpu.VMEM(wu.shape, BF16), pltpu.VMEM(wd.shape, BF16),
                        pltpu.SemaphoreType.DMA((3,))],
        sem=("arbitrary",), args=[x, gain] + ([] if head is None else list(head)) + [wg, wu, wd])


def _ffn_bwd_data(dy, x, gain, g, u, wg, wu, wd, name, rides=None):
    t, d = x.shape
    wg, wu, wd = _flat(wg), _flat(wu), _flat(wd)
    f = wg.shape[0]
    tm = min(256, t)

    def body(dy_ref, x_ref, gain_ref, g_ref, u_ref, wg_hbm, wu_hbm, wd_hbm, dx_ref, dg_ref, du_ref, dgain_ref,
             wg_v, wu_v, wd_v, sems):
        @pl.when(pl.program_id(0) == 0)
        def _():
            _load_weights([(wg_hbm, wg_v), (wu_hbm, wu_v), (wd_hbm, wd_v)], sems)
            dgain_ref[...] = jnp.zeros_like(dgain_ref)

        dyv = dy_ref[...]
        dyh = (0.5 * dyv).astype(BF16)
        dxn = jnp.zeros((tm, d), F32)
        chunks = _hidden_chunks(f)
        das = [_dot(dyh, wd_v[c, :], NT) for c in chunks]
        for c, da in zip(chunks, das):
            gj = g_ref[:, c].astype(F32)
            uj = u_ref[:, c].astype(F32)
            sig = _sigmoid(gj)
            dgj = (da * uj * (sig * (1.0 + gj * (1.0 - sig)))).astype(BF16)
            duj = (da * (gj * sig)).astype(BF16)
            dg_ref[:, c] = dgj
            du_ref[:, c] = duj
            dxn = dxn + _dot(dgj, wg_v[c, :]) + _dot(duj, wu_v[c, :])
        xv = x_ref[...]
        r = lax.rsqrt(jnp.mean(xv * xv, axis=-1, keepdims=True) + NORM_EPS)
        xh = xv * r
        dgain_ref[...] += _rows8(dxn * xh)
        dxh = dxn * gain_ref[...]
        dx_ref[...] = dyv + r * (dxh - xh * jnp.mean(dxh * xh, axis=-1, keepdims=True))

    hbm = pl.BlockSpec(memory_space=pl.ANY)
    tile = pl.BlockSpec((tm, d), lambda i: (i, 0))
    hid = pl.BlockSpec((tm, f), lambda i: (i, 0))
    return _pallas(
        body, rides, name=name, grid=(t // tm,),
        in_specs=[tile, tile, pl.BlockSpec((1, d), lambda i: (0, 0)), hid, hid, hbm, hbm, hbm],
        out_specs=[tile, hid, hid, pl.BlockSpec((8, d), lambda i: (0, 0))],
        out_shape=[jax.ShapeDtypeStruct((t, d), F32), jax.ShapeDtypeStruct((t, f), BF16),
                   jax.ShapeDtypeStruct((t, f), BF16), jax.ShapeDtypeStruct((8, d), F32)],
        scratch_shapes=[pltpu.VMEM(wg.shape, BF16), pltpu.VMEM(wu.shape, BF16), pltpu.VMEM(wd.shape, BF16),
                        pltpu.SemaphoreType.DMA((3,))],
        sem=("arbitrary",), args=[dy, x, gain, g, u, wg, wu, wd])


WGRAD_ROW_BLOCKS = 2


def _ffn_wgrad_down(a, dy, name, rides=None):
    t, d = dy.shape
    f = a.shape[1]
    fb = f // WGRAD_ROW_BLOCKS
    tk = min(1024, t)

    def body(dy_ref, a_ref, dwd_ref):
        @pl.when(pl.program_id(1) == 0)
        def _():
            dwd_ref[...] = jnp.zeros_like(dwd_ref)

        dwd_ref[...] += _dot(a_ref[...], (0.5 * dy_ref[...]).astype(BF16), TN)

    return _pallas(
        body, rides, name=name, grid=(WGRAD_ROW_BLOCKS, t // tk),
        in_specs=[pl.BlockSpec((tk, d), lambda j, k: (k, 0)), pl.BlockSpec((tk, fb), lambda j, k: (k, j))],
        out_specs=[pl.BlockSpec((fb, d), lambda j, k: (j, 0))],
        out_shape=[jax.ShapeDtypeStruct((f, d), F32)],
        sem=("arbitrary", "arbitrary"), args=[dy, a])


def _ffn_wgrad_gu(xn, dhs, name, rides=None):
    t, d = xn.shape
    n = len(dhs)
    f = dhs[0].shape[1]
    fb = f // WGRAD_ROW_BLOCKS
    tk = min(2048 // n, t)

    def body(xn_ref, *refs):
        @pl.when(pl.program_id(1) == 0)
        def _():
            for o_ref in refs[n:]:
                o_ref[...] = jnp.zeros_like(o_ref)

        xnv = xn_ref[...]
        for dh_ref, o_ref in zip(refs[:n], refs[n:]):
            o_ref[...] += _dot(dh_ref[...], xnv, TN)

    hid = pl.BlockSpec((tk, fb), lambda j, k: (k, j))
    out = pl.BlockSpec((fb, d), lambda j, k: (j, 0))
    return _pallas(
        body, rides, name=name, grid=(WGRAD_ROW_BLOCKS, t // tk),
        in_specs=[pl.BlockSpec((tk, d), lambda j, k: (k, 0))] + [hid] * n,
        out_specs=[out] * n, out_shape=[jax.ShapeDtypeStruct((f, d), F32)] * n,
        sem=("arbitrary", "arbitrary"), args=[xn] + list(dhs))


def _tn_matmul(a, b, bn, name):
    t, m = a.shape
    n = b.shape[1]
    tk = min(2048, t)

    def body(a_ref, b_ref, o_ref):
        @pl.when(pl.program_id(1) == 0)
        def _():
            o_ref[...] = jnp.zeros_like(o_ref)

        o_ref[...] += _dot(a_ref[...].astype(BF16), b_ref[...].astype(BF16), TN)

    return pl.pallas_call(
        body, name=name, grid=(n // bn, t // tk),
        in_specs=[pl.BlockSpec((tk, m), lambda j, k: (k, 0)), pl.BlockSpec((tk, bn), lambda j, k: (k, j))],
        out_specs=pl.BlockSpec((None, m, bn), lambda j, k: (j, 0, 0)),
        out_shape=jax.ShapeDtypeStruct((n // bn, m, bn), F32),
        compiler_params=_params("arbitrary", "arbitrary"),
    )(a, b)


def _chunk_scratch(tm, w):
    return pltpu.VMEM((w // LANE, tm, LANE), F32)


def _regroup_store(cbuf, out_ref, dil, chunks=None):
    n = out_ref.shape[1]
    for k in range(cbuf.shape[0]) if chunks is None else chunks:
        for g in range(dil):
            rows = cbuf[k] if dil == 1 else cbuf[k, pl.ds(g, n, stride=dil), :]
            out_ref[g, :, k * LANE:(k + 1) * LANE] = rows.astype(out_ref.dtype)


def _natural_rows(ref, dil, cbuf):
    if dil == 1:
        return ref[0].astype(F32)
    n = ref.shape[1]
    for g in range(dil):
        for k in range(cbuf.shape[0]):
            cbuf[k, pl.ds(g, n, stride=dil), :] = ref[g, :, k * LANE:(k + 1) * LANE].astype(F32)
    return jnp.concatenate([cbuf[k] for k in range(cbuf.shape[0])], axis=1)


IN_CHUNK_TILES = 4


def _column_chunks(n):
    step = IN_CHUNK_TILES * V7X_MXU_TILE
    return [slice(s, min(s + step, n)) for s in range(0, n, step)]


def _load_side_by_side(w_hbm, w_v, sems):
    ns, _, cs = w_hbm.shape
    copies = [pltpu.make_async_copy(w_hbm.at[j], w_v.at[:, pl.ds(j * cs, cs)], sems.at[j]) for j in range(ns)]
    for cp in copies:
        cp.start()
    for cp in copies:
        cp.wait()


def _inproj_fwd(h, gain, win):
    t, d = h.shape
    ns, _, cs = win.shape
    tm = min(512, t)
    rw, aw = 4 * RET_WIDTH, 3 * ATT_WIDTH

    def body(h_ref, gain_ref, w_hbm, xn_ref, ur_ref, *rest):
        a_refs, abuf, w_v, sems = rest[:-3], rest[-3], rest[-2], rest[-1]

        @pl.when(pl.program_id(0) == 0)
        def _():
            _load_side_by_side(w_hbm, w_v, sems)

        hv = h_ref[...]
        r = lax.rsqrt(jnp.mean(hv * hv, axis=-1, keepdims=True) + NORM_EPS)
        xn = (hv * r * gain_ref[...]).astype(BF16)
        xn_ref[...] = xn
        for c in reversed(_column_chunks(ns * cs)):
            res = _dot(xn, w_v[:, c])
            mine = []
            for k in range((c.stop - c.start) // LANE):
                chunk = c.start // LANE + k
                piece = res[:, k * LANE:(k + 1) * LANE]
                if chunk < rw // LANE:
                    ur_ref[:, chunk * LANE:(chunk + 1) * LANE] = piece
                else:
                    abuf[chunk - rw // LANE] = piece
                    mine.append(chunk - rw // LANE)
            for dil, a_ref in zip(DILATIONS, a_refs):
                _regroup_store(abuf, a_ref, dil, mine)

    return pl.pallas_call(
        body, name="inproj_fwd", grid=(t // tm,),
        in_specs=[pl.BlockSpec((tm, d), lambda i: (i, 0)), pl.BlockSpec((1, d), lambda i: (0, 0)),
                  pl.BlockSpec(memory_space=pl.ANY)],
        out_specs=[pl.BlockSpec((tm, d), lambda i: (i, 0)), pl.BlockSpec((tm, rw), lambda i: (i, 0))]
        + [pl.BlockSpec((dil, tm // dil, aw), lambda i: (0, i, 0)) for dil in DILATIONS],
        out_shape=[jax.ShapeDtypeStruct((t, d), BF16), jax.ShapeDtypeStruct((t, rw), F32)]
        + [jax.ShapeDtypeStruct((dil, t // dil, aw), BF16) for dil in DILATIONS],
        scratch_shapes=[_chunk_scratch(tm, aw), pltpu.VMEM((d, ns * cs), BF16), pltpu.SemaphoreType.DMA((ns,))],
        compiler_params=_params("arbitrary"),
    )(h, gain, win)


def _inproj_bwd(pieces, parts, h, gain, dres, win):
    t, d = h.shape
    ns, _, cs = win.shape
    pw = pieces[0].shape[1]
    tm = min(512, t)
    npc, nk = len(pieces), len(parts[0])
    flat_parts = [a for p in parts for a in p]

    def body(*refs):
        p_refs, a_refs = refs[:npc], refs[npc:npc + len(flat_parts)]
        h_ref, gain_ref, dres_ref, w_hbm, dh_ref, du_ref, dgain_ref, buf, w_v, sems = refs[npc + len(flat_parts):]

        @pl.when(pl.program_id(0) == 0)
        def _():
            _load_side_by_side(w_hbm, w_v, sems)
            dgain_ref[...] = jnp.zeros_like(dgain_ref)

        for k in range(npc):
            du_ref[:, k * pw:(k + 1) * pw] = p_refs[k][...]
        for k in range(nk):
            acc = None
            for b, dil in enumerate(DILATIONS):
                rows = _natural_rows(a_refs[b * nk + k], dil, buf)
                acc = rows if acc is None else acc + rows
            du_ref[:, (npc + k) * pw:(npc + k + 1) * pw] = acc.astype(BF16)
        dxn = jnp.zeros((tm, d), F32)
        for c in _column_chunks(ns * cs):
            dxn = dxn + _dot(du_ref[:, c], w_v[:, c], NT)
        hv = h_ref[...]
        r = lax.rsqrt(jnp.mean(hv * hv, axis=-1, keepdims=True) + NORM_EPS)
        xh = hv * r
        dgain_ref[...] += _rows8(dxn * xh)
        dxh = dxn * gain_ref[...]
        dh_ref[...] = dres_ref[...] + r * (dxh - xh * jnp.mean(dxh * xh, axis=-1, keepdims=True))

    tile = pl.BlockSpec((tm, d), lambda i: (i, 0))
    cols = (npc + nk) * pw
    return pl.pallas_call(
        body, name="inproj_bwd", grid=(t // tm,),
        in_specs=[pl.BlockSpec((tm, pw), lambda i: (i, 0))] * npc
        + [_regrouped_spec(tm, dil, pw) for dil in DILATIONS for _ in range(nk)]
        + [tile, pl.BlockSpec((1, d), lambda i: (0, 0)), tile, pl.BlockSpec(memory_space=pl.ANY)],
        out_specs=[tile, pl.BlockSpec((tm, cols), lambda i: (i, 0)), pl.BlockSpec((8, d), lambda i: (0, 0))],
        out_shape=[jax.ShapeDtypeStruct((t, d), F32), jax.ShapeDtypeStruct((t, cols), BF16),
                   jax.ShapeDtypeStruct((8, d), F32)],
        scratch_shapes=[_chunk_scratch(tm, pw), pltpu.VMEM((d, ns * cs), BF16), pltpu.SemaphoreType.DMA((ns,))],
        compiler_params=_params("arbitrary"),
    )(*pieces, *flat_parts, h, gain, dres, win)


def _outproj_fwd(h, mix_r, mix_a, wo):
    t, d = h.shape
    hw = mix_r.shape[1]
    tm = min(512, t)

    def body(h_ref, mr_ref, ma_ref, w_ref, o_ref):
        o_ref[...] = h_ref[...] + _dot(mr_ref[...], w_ref[0:hw, :]) + _dot(ma_ref[...], w_ref[hw:2 * hw, :])

    tile = pl.BlockSpec((tm, d), lambda i: (i, 0))
    half = pl.BlockSpec((tm, hw), lambda i: (i, 0))
    return pl.pallas_call(
        body, name="outproj_fwd", grid=(t // tm,),
        in_specs=[tile, half, half, pl.BlockSpec(wo.shape, lambda i: (0, 0))],
        out_specs=tile, out_shape=jax.ShapeDtypeStruct((t, d), F32),
        compiler_params=_params("arbitrary"),
    )(h, mix_r, mix_a, wo)


def _outproj_bwd(dh, wo, rides=None):
    t, d = dh.shape
    hw = wo.shape[0] // 2
    tm = min(512, t)

    def body(dh_ref, w_ref, dr_ref, da_ref):
        dhb = dh_ref[...].astype(BF16)
        dr_ref[...] = _dot(dhb, w_ref[0:hw, :], NT)
        da_ref[...] = _dot(dhb, w_ref[hw:2 * hw, :], NT)

    half = pl.BlockSpec((tm, hw), lambda i: (i, 0))
    return _pallas(
        body, rides, name="outproj_bwd", grid=(t // tm,),
        in_specs=[pl.BlockSpec((tm, d), lambda i: (i, 0)), pl.BlockSpec(wo.shape, lambda i: (0, 0))],
        out_specs=[half, half],
        out_shape=[jax.ShapeDtypeStruct((t, hw), F32), jax.ShapeDtypeStruct((t, hw), F32)],
        sem=("arbitrary",), args=[dh, wo])


def _retention_tables(t):
    pos = jnp.arange(t, dtype=F32)
    pair = (jnp.arange(RET_DIM) // 2 * 2).astype(F32)
    ang = pos[:, None] * (ROPE_BASE ** (-pair / RET_DIM))[None, :]
    c = RET_CHUNK
    log_g = jnp.log(1.0 - 2.0 ** (-5.0 - jnp.arange(RET_HEADS, dtype=F32)))
    idx = jnp.arange(c, dtype=F32)
    rel = idx[:, None] - idx[None, :]
    decay = jnp.where(rel >= 0, jnp.exp(log_g[:, None, None] * jnp.maximum(rel, 0.0)), 0.0)
    zeta = jnp.exp(log_g[:, None] * (c - 1 - idx)[None, :])
    xi = jnp.exp(log_g[:, None] * (idx + 1)[None, :])
    gc = jnp.exp(log_g * c)
    wide = lambda v: jnp.broadcast_to(v[:, :, None], (RET_HEADS, c, LANE))
    return (jnp.cos(ang), jnp.sin(ang), decay, wide(zeta), wide(xi),
            jnp.broadcast_to(gc[:, None, None], (RET_HEADS, c, LANE)))


def _rot(v):
    lane = lax.broadcasted_iota(jnp.int32, v.shape, 1)
    nxt = pltpu.roll(v, LANE - 1, 1)
    prv = pltpu.roll(v, 1, 1)
    return jnp.where(lane % 2 == 0, -nxt, prv)


def _ret_specs(tr, rev, nt):
    ti = (lambda i: nt - 1 - i) if rev else (lambda i: i)
    col = lambda blk: pl.BlockSpec((tr, RET_WIDTH), lambda i: (ti(i), blk))
    tab = pl.BlockSpec((tr, LANE), lambda i: (ti(i), 0))
    head = pl.BlockSpec((RET_HEADS, RET_CHUNK, LANE), lambda i: (0, 0, 0))
    return col, tab, head


def _ret_chunks(tr, rev=False):
    order = list(range(tr // RET_CHUNK))
    return [(pl.ds(ci * RET_CHUNK, RET_CHUNK), slice(h * RET_DIM, (h + 1) * RET_DIM), h)
            for h in range(RET_HEADS) for ci in (reversed(order) if rev else order)]


def _ret_operands(items, q_ref, k_ref, v_ref, cos_ref, sin_ref, zeta_ref):
    scale = RET_DIM ** -0.5
    qbs, kbs, vbs, kzs = [], [], [], []
    for sl, hs, h in items:
        cs, sn = cos_ref[sl, :], sin_ref[sl, :]
        q, k = q_ref[sl, hs], k_ref[sl, hs]
        kr = (k * cs + _rot(k) * sn) * scale
        qbs.append((q * cs + _rot(q) * sn).astype(BF16))
        kbs.append(kr.astype(BF16))
        vbs.append(v_ref[sl, hs].astype(BF16))
        kzs.append((kr * zeta_ref[h]).astype(BF16))
    return qbs, kbs, vbs, kzs


def _ret_states(items, state, steps, gc_ref):
    cur, befores = {}, []
    for (sl, hs, h), step in zip(items, steps):
        st = cur[h] if h in cur else state[h]
        befores.append(st)
        cur[h] = st * gc_ref[h] + step
    for h, st in cur.items():
        state[h] = st
    return befores


def _ret_fwd(u, gain, tabs):
    t = u.shape[0]
    tr = min(512, t)
    nt = t // tr
    cos, sin, decay, zeta, xi, gc = tabs

    def body(q_ref, k_ref, v_ref, gt_ref, cos_ref, sin_ref, gain_ref, dec_ref, zeta_ref, xi_ref, gc_ref,
             raw_ref, mix_ref, state):
        @pl.when(pl.program_id(0) == 0)
        def _():
            state[...] = jnp.zeros_like(state)

        items = _ret_chunks(tr)
        n = range(len(items))
        qbs, kbs, vbs, kzs = _ret_operands(items, q_ref, k_ref, v_ref, cos_ref, sin_ref, zeta_ref)
        ss = [_dot(qbs[i], kbs[i], NT) for i in n]
        kvs = [_dot(kzs[i], vbs[i], TN) for i in n]
        befores = _ret_states(items, state, kvs, gc_ref)
        intra = [_dot((ss[i] * dec_ref[items[i][2]]).astype(BF16), vbs[i]) for i in n]
        inter = [_dot(qbs[i], befores[i].astype(BF16)) for i in n]
        for i, (sl, hs, h) in enumerate(items):
            o = intra[i] + inter[i] * xi_ref[h]
            raw_ref[sl, hs] = o
            mu = jnp.mean(o, axis=-1, keepdims=True)
            var = jnp.mean(jnp.square(o - mu), axis=-1, keepdims=True)
            y = (o - mu) * lax.rsqrt(var + GN_EPS) * gain_ref[:, hs]
            gt = gt_ref[sl, hs]
            mix_ref[sl, hs] = (y * (gt * _sigmoid(gt))).astype(BF16)

    col, tab, head = _ret_specs(tr, False, nt)
    out = pl.BlockSpec((tr, RET_WIDTH), lambda i: (i, 0))
    return pl.pallas_call(
        body, name="ret_fwd", grid=(nt,),
        in_specs=[col(0), col(1), col(2), col(3), tab, tab, pl.BlockSpec((1, RET_WIDTH), lambda i: (0, 0)),
                  head, head, head, head],
        out_specs=[out, out],
        out_shape=[jax.ShapeDtypeStruct((t, RET_WIDTH), F32), jax.ShapeDtypeStruct((t, RET_WIDTH), BF16)],
        scratch_shapes=[pltpu.VMEM((RET_HEADS, RET_DIM, RET_DIM), F32)],
        compiler_params=_params("arbitrary"),
    )(u, u, u, u, cos, sin, gain, decay, zeta, xi, gc)


def _ret_bwd_q(dmix, raw, u, gain, tabs, rides=None):
    t = u.shape[0]
    tr = min(512, t)
    nt = t // tr
    cos, sin, decay, zeta, xi, gc = tabs

    def body(dm_ref, raw_ref, q_ref, k_ref, v_ref, gt_ref, cos_ref, sin_ref, gain_ref, dec_ref, zeta_ref, xi_ref, gc_ref,
             dq_ref, dgt_ref, dret_ref, dgain_ref, state):
        @pl.when(pl.program_id(0) == 0)
        def _():
            state[...] = jnp.zeros_like(state)
            dgain_ref[...] = jnp.zeros_like(dgain_ref)

        items = _ret_chunks(tr)
        n_items = range(len(items))
        qbs, kbs, vbs, kzs = _ret_operands(items, q_ref, k_ref, v_ref, cos_ref, sin_ref, zeta_ref)
        dos, dgains = [], {}
        for sl, hs, h in items:
            o = raw_ref[sl, hs]
            mu = jnp.mean(o, axis=-1, keepdims=True)
            var = jnp.mean(jnp.square(o - mu), axis=-1, keepdims=True)
            rs = lax.rsqrt(var + GN_EPS)
            n = (o - mu) * rs
            gt = gt_ref[sl, hs]
            sig = _sigmoid(gt)
            dout = dm_ref[sl, hs]
            gain_h = gain_ref[:, hs]
            dgt_ref[sl, hs] = (dout * (n * gain_h) * (sig * (1.0 + gt * (1.0 - sig)))).astype(BF16)
            dy = dout * (gt * sig)
            dgains[h] = dgains[h] + _rows8(dy * n) if h in dgains else _rows8(dy * n)
            dn = dy * gain_h
            do = rs * (dn - jnp.mean(dn, axis=-1, keepdims=True) - n * jnp.mean(dn * n, axis=-1, keepdims=True))
            dret_ref[sl, hs] = do
            dos.append(do)
        for h, dg in dgains.items():
            dgain_ref[:, h * RET_DIM:(h + 1) * RET_DIM] += dg
        dss = [_dot(dos[i].astype(BF16), vbs[i], NT) for i in n_items]
        kvs = [_dot(kzs[i], vbs[i], TN) for i in n_items]
        befores = _ret_states(items, state, kvs, gc_ref)
        intra = [_dot((dss[i] * dec_ref[items[i][2]]).astype(BF16), kbs[i]) for i in n_items]
        inter = [_dot((dos[i] * xi_ref[items[i][2]]).astype(BF16), befores[i].astype(BF16), NT) for i in n_items]
        for i, (sl, hs, h) in enumerate(items):
            dqr = intra[i] + inter[i]
            dq_ref[sl, hs] = (dqr * cos_ref[sl, :] - _rot(dqr * sin_ref[sl, :])).astype(BF16)

    col, tab, head = _ret_specs(tr, False, nt)
    out = pl.BlockSpec((tr, RET_WIDTH), lambda i: (i, 0))
    return _pallas(
        body, rides, name="ret_bwd_q", grid=(nt,),
        in_specs=[out, out, col(0), col(1), col(2), col(3), tab, tab, pl.BlockSpec((1, RET_WIDTH), lambda i: (0, 0)),
                  head, head, head, head],
        out_specs=[out, out, out, pl.BlockSpec((8, RET_WIDTH), lambda i: (0, 0))],
        out_shape=[jax.ShapeDtypeStruct((t, RET_WIDTH), BF16), jax.ShapeDtypeStruct((t, RET_WIDTH), BF16),
                   jax.ShapeDtypeStruct((t, RET_WIDTH), F32), jax.ShapeDtypeStruct((8, RET_WIDTH), F32)],
        scratch_shapes=[pltpu.VMEM((RET_HEADS, RET_DIM, RET_DIM), F32)],
        sem=("arbitrary",), args=[dmix, raw, u, u, u, u, cos, sin, gain, decay, zeta, xi, gc])


def _ret_bwd_kv(dret, u, tabs, rides=None):
    t = u.shape[0]
    tr = min(512, t)
    nt = t // tr
    cos, sin, decay, zeta, xi, gc = tabs
    scale = RET_DIM ** -0.5

    def body(do_ref, q_ref, k_ref, v_ref, cos_ref, sin_ref, dec_ref, zeta_ref, xi_ref, gc_ref, dk_ref, dv_ref, gst):
        @pl.when(pl.program_id(0) == 0)
        def _():
            gst[...] = jnp.zeros_like(gst)

        items = _ret_chunks(tr, rev=True)
        n = range(len(items))
        qbs, kbs, vbs, kzs = _ret_operands(items, q_ref, k_ref, v_ref, cos_ref, sin_ref, zeta_ref)
        dos = [do_ref[sl, hs] for sl, hs, h in items]
        dobs = [do.astype(BF16) for do in dos]
        ss = [_dot(qbs[i], kbs[i], NT) for i in n]
        dss = [_dot(dobs[i], vbs[i], NT) for i in n]
        steps = [_dot(qbs[i], (dos[i] * xi_ref[items[i][2]]).astype(BF16), TN) for i in n]
        afters = [g.astype(BF16) for g in _ret_states(items, gst, steps, gc_ref)]
        dvs = [_dot((ss[i] * dec_ref[items[i][2]]).astype(BF16), dobs[i], TN) + _dot(kzs[i], afters[i]) for i in n]
        dks = [_dot((dss[i] * dec_ref[items[i][2]]).astype(BF16), qbs[i], TN) for i in n]
        dkz = [_dot(vbs[i], afters[i], NT) for i in n]
        for i, (sl, hs, h) in enumerate(items):
            dv_ref[sl, hs] = dvs[i].astype(BF16)
            dkr = (dks[i] + dkz[i] * zeta_ref[h]) * scale
            dk_ref[sl, hs] = (dkr * cos_ref[sl, :] - _rot(dkr * sin_ref[sl, :])).astype(BF16)

    col, tab, head = _ret_specs(tr, True, nt)
    out = pl.BlockSpec((tr, RET_WIDTH), lambda i: (nt - 1 - i, 0))
    return _pallas(
        body, rides, name="ret_bwd_kv", grid=(nt,),
        in_specs=[out, col(0), col(1), col(2), tab, tab, head, head, head, head],
        out_specs=[out, out],
        out_shape=[jax.ShapeDtypeStruct((t, RET_WIDTH), BF16), jax.ShapeDtypeStruct((t, RET_WIDTH), BF16)],
        scratch_shapes=[pltpu.VMEM((RET_HEADS, RET_DIM, RET_DIM), F32)],
        sem=("arbitrary",), args=[dret, u, u, u, cos, sin, decay, zeta, xi, gc])


PAIRS = ATT_WIDTH // LANE
ATT_Q_BLK, ATT_K_BLK, ATT_V_BLK = 0, PAIRS, 2 * PAIRS
STAT_LANES = ATT_DIM // 2


ATT_STEP_ROWS = 4096


def _att_tiles(t, dil):
    sub = t // dil
    tq = min(ATT_STEP_ROWS, sub)
    return sub, tq, sub // tq, tq // ATT_BLOCK, min(dil, ATT_STEP_ROWS // tq)


def _att_in_specs(tq, qb, ti, gs):
    cur = lambda off: pl.BlockSpec((gs, tq, LANE), lambda g, p, i: (g, ti(i), off + p))
    prev = lambda off: pl.BlockSpec((gs, ATT_BLOCK, LANE), lambda g, p, i: (g, jnp.maximum(ti(i) * qb - 1, 0), off + p))
    return [cur(ATT_Q_BLK), cur(ATT_K_BLK), prev(ATT_K_BLK), cur(ATT_V_BLK), prev(ATT_V_BLK)]


def _band_mask():
    key = lax.broadcasted_iota(jnp.int32, (2 * ATT_BLOCK, 2 * ATT_BLOCK), 0)
    qry = lax.broadcasted_iota(jnp.int32, (2 * ATT_BLOCK, 2 * ATT_BLOCK), 1) % ATT_BLOCK
    dist = qry + ATT_BLOCK - key
    return (dist >= 0) & (dist <= ATT_BLOCK), key >= ATT_BLOCK


def _head0_lanes():
    return lax.broadcasted_iota(jnp.int32, (ATT_BLOCK, LANE), 1) < ATT_DIM


def _stack_heads(v, head0):
    zero = jnp.zeros((), v.dtype)
    return jnp.concatenate([jnp.where(head0, v, zero), jnp.where(head0, zero, v)], axis=0)


def _unstack_heads(v, head0):
    return jnp.where(head0, v[0:ATT_BLOCK], v[ATT_BLOCK:])


def _att_fwd(ua, dil):
    sub = ua.shape[1]
    _, tq, nq, qb, gs = _att_tiles(sub * dil, dil)

    def body(q_ref, kc_ref, kp_ref, vc_ref, vp_ref, o_ref, l_ref, kx, vx):
        tile = pl.program_id(2)
        kx[:, 0:ATT_BLOCK, :] = kp_ref[...]
        kx[:, ATT_BLOCK:, :] = kc_ref[...]
        vx[:, 0:ATT_BLOCK, :] = vp_ref[...]
        vx[:, ATT_BLOCK:, :] = vc_ref[...]
        band, cur_keys = _band_mask()
        head0 = _head0_lanes()
        items = [(r, b) for r in range(gs) for b in range(qb)]
        rows = lambda b: slice(b * ATT_BLOCK, (b + 1) * ATT_BLOCK)
        keys = lambda b: slice(b * ATT_BLOCK, (b + 2) * ATT_BLOCK)
        sts = [_dot(kx[r, keys(b), :], _stack_heads(q_ref[r, rows(b), :] * jnp.asarray(ATT_DIM ** -0.5, BF16), head0), NT)
               for r, b in items]
        pts, lses = [], []
        for (r, b), st in zip(items, sts):
            mask = band if b > 0 else band & (cur_keys | (tile > 0))
            st = jnp.where(mask, st, -1e30)
            m = jnp.max(st, axis=0, keepdims=True)
            ex = jnp.exp(st - m)
            den = jnp.sum(ex, axis=0, keepdims=True)
            pts.append((ex * (1.0 / den)).astype(BF16))
            lses.append(m + jnp.log(den))
        outs = [_dot(pt, vx[r, keys(b), :], TN) for (r, b), pt in zip(items, pts)]
        for (r, b), out, lse in zip(items, outs, lses):
            o_ref[r, rows(b), :] = _unstack_heads(out, head0).astype(BF16)
            cols = [jnp.broadcast_to(lse[:, e * ATT_BLOCK:(e + 1) * ATT_BLOCK], (ATT_BLOCK, LANE)).T for e in range(2)]
            l_ref[r, rows(b), :] = jnp.where(head0, cols[0], cols[1])

    out = pl.BlockSpec((gs, tq, LANE), lambda g, p, i: (g, i, p))
    return pl.pallas_call(
        body, name=f"att_fwd_d{dil}", grid=(dil // gs, PAIRS, nq),
        in_specs=_att_in_specs(tq, qb, lambda i: i, gs),
        out_specs=[out, out],
        out_shape=[jax.ShapeDtypeStruct((dil, sub, ATT_WIDTH), BF16), jax.ShapeDtypeStruct((dil, sub, ATT_WIDTH), F32)],
        scratch_shapes=[pltpu.VMEM((gs, tq + ATT_BLOCK, LANE), BF16)] * 2,
        compiler_params=_params("arbitrary", "arbitrary", "arbitrary"),
    )(ua, ua, ua, ua, ua)


def _regrouped_spec(tm, dil, w):
    return pl.BlockSpec((dil, tm // dil, w), lambda i: (0, i, 0))


def _att_combine(outs, lses, t):
    w = ATT_WIDTH
    tm = min(512, t)
    nb = len(outs)

    def body(*refs):
        o_refs, l_refs = refs[:nb], refs[nb:2 * nb]
        mix_ref, att_ref, lse_ref, buf = refs[2 * nb:]
        ls = [_natural_rows(r, dil, buf) for r, dil in zip(l_refs, DILATIONS)]
        m = functools.reduce(jnp.maximum, ls)
        ws = [jnp.exp(l - m) for l in ls]
        den = functools.reduce(jnp.add, ws)
        att = functools.reduce(jnp.add, [(wt / den) * _natural_rows(r, dil, buf) for wt, r, dil in zip(ws, o_refs, DILATIONS)])
        att_ref[...] = att
        mix_ref[...] = att.astype(BF16)
        lse_ref[...] = m + jnp.log(den)

    tile = pl.BlockSpec((tm, w), lambda i: (i, 0))
    regrouped = [_regrouped_spec(tm, dil, w) for dil in DILATIONS]
    return pl.pallas_call(
        body, name="att_combine", grid=(t // tm,),
        in_specs=regrouped * 2, out_specs=[tile, tile, tile],
        out_shape=[jax.ShapeDtypeStruct((t, w), BF16), jax.ShapeDtypeStruct((t, w), F32), jax.ShapeDtypeStruct((t, w), F32)],
        scratch_shapes=[_chunk_scratch(tm, w)],
        compiler_params=_params("arbitrary"),
    )(*outs, *lses)


def _att_bwd_prep(datt, att, lse):
    t, w = datt.shape
    tm = min(512, t)

    def body(da_ref, at_ref, l_ref, *rest):
        outs, dbuf, sbuf = rest[:-2], rest[-2], rest[-1]
        dav = da_ref[...]
        prod = dav * at_ref[...]
        lane = lax.broadcasted_iota(jnp.int32, (tm, LANE), 1)
        for k in range(w // LANE):
            cols = slice(k * LANE, (k + 1) * LANE)
            dbuf[k] = dav[:, cols]
            delta = jnp.concatenate(
                [jnp.broadcast_to(jnp.sum(prod[:, k * LANE + e * ATT_DIM:k * LANE + (e + 1) * ATT_DIM], axis=-1, keepdims=True),
                                  (tm, ATT_DIM)) for e in range(LANE // ATT_DIM)], axis=1)
            sbuf[k] = jnp.where(lane % ATT_DIM < STAT_LANES, l_ref[:, cols], delta)
        for k, dil in enumerate(DILATIONS):
            _regroup_store(dbuf, outs[2 * k], dil)
            _regroup_store(sbuf, outs[2 * k + 1], dil)

    tile = pl.BlockSpec((tm, w), lambda i: (i, 0))
    res = pl.pallas_call(
        body, name="att_bwd_prep", grid=(t // tm,),
        in_specs=[tile] * 3,
        out_specs=[_regrouped_spec(tm, dil, w) for dil in DILATIONS for _ in range(2)],
        out_shape=[jax.ShapeDtypeStruct((dil, t // dil, w), dt) for dil in DILATIONS for dt in (BF16, F32)],
        scratch_shapes=[_chunk_scratch(tm, w)] * 2,
        compiler_params=_params("arbitrary"),
    )(datt, att, lse)
    return [(res[2 * k], res[2 * k + 1]) for k in range(len(DILATIONS))]


def _att_bwd(ua, da, stat, dil, rides=None):
    sub = ua.shape[1]
    _, tq, nq, qb, gs = _att_tiles(sub * dil, dil)
    scale = ATT_DIM ** -0.5

    def body(q_ref, kc_ref, kp_ref, vc_ref, vp_ref, da_ref, st_ref, dq_ref, dk_ref, dv_ref, kx, vx, ck, cv):
        step = pl.program_id(2)
        tile = nq - 1 - step

        @pl.when(step == 0)
        def _():
            ck[...] = jnp.zeros_like(ck)
            cv[...] = jnp.zeros_like(cv)

        kx[:, 0:ATT_BLOCK, :] = kp_ref[...]
        kx[:, ATT_BLOCK:, :] = kc_ref[...]
        vx[:, 0:ATT_BLOCK, :] = vp_ref[...]
        vx[:, ATT_BLOCK:, :] = vc_ref[...]
        band, cur_keys = _band_mask()
        head0 = _head0_lanes()
        items = [(r, b) for r in range(gs) for b in range(qb)]
        n = range(len(items))
        rows = lambda b: slice(b * ATT_BLOCK, (b + 1) * ATT_BLOCK)
        keys = lambda b: slice(b * ATT_BLOCK, (b + 2) * ATT_BLOCK)
        qqs = [_stack_heads(q_ref[r, rows(b), :] * jnp.asarray(scale, BF16), head0) for r, b in items]
        dds = [_stack_heads(da_ref[r, rows(b), :], head0) for r, b in items]
        sts = [_dot(kx[r, keys(b), :], qqs[i], NT) for i, (r, b) in enumerate(items)]
        dpts = [_dot(vx[r, keys(b), :], dds[i], NT) for i, (r, b) in enumerate(items)]
        pts, dsts = [], []
        for i, (r, b) in enumerate(items):
            mask = band if b > 0 else band & (cur_keys | (tile > 0))
            stat = st_ref[r, rows(b), :].T
            row = lambda k: jnp.concatenate([stat[e * ATT_DIM + k:e * ATT_DIM + k + 1, :] for e in range(2)], axis=1)
            pt = jnp.where(mask, jnp.exp(sts[i] - row(0)), 0.0)
            dsts.append((pt * (dpts[i] - row(STAT_LANES))).astype(BF16))
            pts.append(pt.astype(BF16))
        dqs = [_dot(dsts[i], kx[r, keys(b), :], TN) for i, (r, b) in enumerate(items)]
        dkbs = [_dot(dsts[i], qqs[i]) for i in n]
        dvbs = [_dot(pts[i], dds[i]) for i in n]
        for i, (r, b) in enumerate(items):
            dq_ref[r, rows(b), :] = (_unstack_heads(dqs[i], head0) * scale).astype(BF16)
            if b > 0:
                dk_ref[r, rows(b - 1), :] = (dkbs[i - 1][ATT_BLOCK:] + dkbs[i][0:ATT_BLOCK]).astype(BF16)
                dv_ref[r, rows(b - 1), :] = (dvbs[i - 1][ATT_BLOCK:] + dvbs[i][0:ATT_BLOCK]).astype(BF16)
        for r in range(gs):
            first, last = r * qb, r * qb + qb - 1
            dk_ref[r, rows(qb - 1), :] = (dkbs[last][ATT_BLOCK:] + ck[r]).astype(BF16)
            dv_ref[r, rows(qb - 1), :] = (dvbs[last][ATT_BLOCK:] + cv[r]).astype(BF16)
            ck[r] = dkbs[first][0:ATT_BLOCK]
            cv[r] = dvbs[first][0:ATT_BLOCK]

    ti = lambda i: nq - 1 - i
    out = pl.BlockSpec((gs, tq, LANE), lambda g, p, i: (g, ti(i), p))
    shape = jax.ShapeDtypeStruct((dil, sub, ATT_WIDTH), BF16)
    return _pallas(
        body, rides, name=f"att_bwd_d{dil}", grid=(dil // gs, PAIRS, nq),
        in_specs=_att_in_specs(tq, qb, ti, gs) + [out, out],
        out_specs=[out, out, out], out_shape=[shape] * 3,
        scratch_shapes=[pltpu.VMEM((gs, tq + ATT_BLOCK, LANE), BF16)] * 2 + [pltpu.VMEM((gs, ATT_BLOCK, LANE), F32)] * 2,
        sem=("arbitrary", "arbitrary", "arbitrary"), args=[ua, ua, ua, ua, ua, da, stat])


class _Reduction:
    def __init__(self, place, names, grads):
        self.place, self.names, self.grads = place, names, grads

    def pair(self):
        return _pair_ride(self.grads)

    def chips(self, got):
        self.got = got
        return _chip_ride([_pair_sum(self.place, g, r, f"pair_sum_{n}") for g, r, n in zip(self.grads, got, self.names)])

    def halves(self, others):
        return [_chip_sum(self.place, g, r, o, f"chip_sum_{n}")
                for g, r, o, n in zip(self.grads, self.got, others, self.names)]


def _step(x, target, gains, w, place=None):
    t = x.shape[0]
    ex = place is not None
    g_ffn1, g_mix, g_ret, g_ffn2, g_fin = gains
    w = list(w)
    tabs = _retention_tables(t)
    red = lambda names, grads: _Reduction(place, names, grads) if ex else None
    ride = lambda r: [r] if ex else None

    if ex:
        w[0:3] = _run(_gather_ride(w[0:3]), "gather_ffn1_weights")
    (h1, xn1, *hid1, act1), rest = _ffn_fwd(x, g_ffn1, *w[0:3], "ffn1_fwd", ride(_gather_ride(w[3:])) if ex else None)
    if ex:
        w[3:] = rest[0]
    wg1, wu1, wd1, win, wo, wg2, wu2, wd2 = w
    wo2 = wo.reshape(wo.shape[0] * wo.shape[1], wo.shape[2])
    xnm, u, *uas = _inproj_fwd(h1, g_mix, win)
    raw, mix_r = _ret_fwd(u, g_ret, tabs)
    branches = [_att_fwd(ua, dil) for ua, dil in zip(uas, DILATIONS)]
    mix_a, att, lse = _att_combine([b[0] for b in branches], [b[1] for b in branches], t)
    h2 = _outproj_fwd(h1, mix_r, mix_a, wo2)
    (dh3, xn2, *hid2, act2, loss_p, dg_fin), _ = _ffn_fwd(h2, g_ffn2, wg2, wu2, wd2, "ffn2_fwd", head=(g_fin, target))

    (dwd2,), _ = _ffn_wgrad_down(act2, dh3, "ffn2_wgrad_down")
    dwd2 = dwd2.reshape(wd2.shape)
    r_d2 = red(["ffn2_w_down"], [dwd2])
    (dh2, dga2, dua2, dg_ffn2), e = _ffn_bwd_data(dh3, h2, g_ffn2, *hid2, wg2, wu2, wd2, "ffn2_bwd",
                                                  ex and [r_d2.pair()])
    (dwg2, dwu2), e = _ffn_wgrad_gu(xn2, [dga2, dua2], "ffn2_wgrad_gu", ex and [r_d2.chips(e[0])])
    dwg2, dwu2 = dwg2.reshape(wg2.shape), dwu2.reshape(wu2.shape)
    r_gu2 = red(["ffn2_w_gate", "ffn2_w_up"], [dwg2, dwu2])
    (dmix_r, dmix_a), e = _outproj_bwd(dh2, wo2, ex and [r_gu2.pair(), _finish_ride(r_d2.halves(e[0]))])
    if ex:
        got_gu2, (dwd2,) = e
    hw = RET_WIDTH // (wo.shape[1])
    dwo = jnp.concatenate([_tn_matmul(mix_r, dh2, dh2.shape[1], "wo_grad_r").reshape(hw, wo.shape[1], wo.shape[2]),
                           _tn_matmul(mix_a, dh2, dh2.shape[1], "wo_grad_a").reshape(hw, wo.shape[1], wo.shape[2])])
    r_wo = red(["w_out"], [dwo])
    (dq_r, dgt_r, dret, dg_ret), e = _ret_bwd_q(dmix_r, raw, u, g_ret, tabs, ex and [r_gu2.chips(got_gu2)])
    (dk_r, dv_r), e = _ret_bwd_kv(dret, u, tabs, ex and [r_wo.pair(), _finish_ride(r_gu2.halves(e[0]))])
    if ex:
        got_wo, (dwg2, dwu2) = e
    prep = _att_bwd_prep(dmix_a, att, lse)
    p1, e = _att_bwd(uas[0], *prep[0], DILATIONS[0], ex and [r_wo.chips(got_wo)])
    p4, e = _att_bwd(uas[1], *prep[1], DILATIONS[1], ex and [_finish_ride(r_wo.halves(e[0]))])
    if ex:
        (dwo,), = e
    p16, _ = _att_bwd(uas[2], *prep[2], DILATIONS[2])
    dh1, du, dg_mix = _inproj_bwd([dq_r, dk_r, dv_r, dgt_r], [p1, p4, p16], h1, g_mix, dh2, win)
    dwin = _tn_matmul(xnm, du, win.shape[2], "win_grad")
    r_in = red(["w_in"], [dwin])
    (dwd1,), e = _ffn_wgrad_down(act1, dh1, "ffn1_wgrad_down", ex and [r_in.pair()])
    dwd1 = dwd1.reshape(wd1.shape)
    r_d1 = red(["ffn1_w_down"], [dwd1])
    got_in = e
    (dx, dga1, dua1, dg_ffn1), _ = _ffn_bwd_data(dh1, x, g_ffn1, *hid1, wg1, wu1, wd1, "ffn1_bwd")
    (dwg1,), e = _ffn_wgrad_gu(xn1, [dga1], "ffn1_wgrad_gate", ex and [r_in.chips(got_in[0]), r_d1.pair()])
    dwg1 = dwg1.reshape(wg1.shape)
    if ex:
        oth_in, got_d1 = e
        r_g1 = red(["ffn1_w_gate"], [dwg1])
        got_g1 = _run(r_g1.pair(), "pair_exchange_ffn1_gate")
    (dwu1,), e = _ffn_wgrad_gu(xn1, [dua1], "ffn1_wgrad_up",
                               ex and [_finish_ride(r_in.halves(oth_in)), r_d1.chips(got_d1), r_g1.chips(got_g1)])
    dwu1 = dwu1.reshape(wu1.shape)
    gain_parts = [dg_ffn1, dg_mix, dg_ret, dg_ffn2, dg_fin]
    if not ex:
        return loss_p, dx, [dwg1, dwu1, dwd1, dwin, dwo, dwg2, dwu2, dwd2], gain_parts
    (dwin,), oth_d1, oth_g1 = e
    r_u1 = red(["ffn1_w_up"], [dwu1])
    got_u1 = _run(r_u1.pair(), "pair_exchange_ffn1_up")
    oth_u1 = _run(r_u1.chips(got_u1), "chip_exchange_ffn1_up")
    last = r_g1.halves(oth_g1) + r_u1.halves(oth_u1) + r_d1.halves(oth_d1)
    dwg1, dwu1, dwd1, gall = _run(_finish_ride(last, _pack_gains(gain_parts, x.shape[1])), "finish_exchange_ffn1")
    return loss_p, dx, [dwg1, dwu1, dwd1, dwin, dwo, dwg2, dwu2, dwd2], gall


N_DEV = 8
GAIN_ROWS = 8


def _place():
    x, y, c = lax.axis_index("x"), lax.axis_index("y"), lax.axis_index("c")
    chips = [(1 - x, y), (x, 1 - y), (1 - x, 1 - y)]
    return x, y, c, chips


ROW_QUARTERS = 4


def _place_shards(place, ws):
    n = len(ws)

    def body(place_ref, *refs):
        for w_ref, o_ref in zip(refs[:n], refs[n:]):
            o_ref[...] = w_ref[...].astype(BF16)

    quarter = lambda w: (w.shape[0] // ROW_QUARTERS, w.shape[1])
    return pl.pallas_call(
        body, name="place_shards",
        grid_spec=pltpu.PrefetchScalarGridSpec(
            num_scalar_prefetch=1, grid=(ROW_QUARTERS,),
            in_specs=[pl.BlockSpec(quarter(w), lambda i, pr: (i, 0)) for w in ws],
            out_specs=[pl.BlockSpec((None,) + quarter(w), lambda i, pr: (pr[0], i, 0)) for w in ws]),
        out_shape=[jax.ShapeDtypeStruct((N_SHARD,) + w.shape, BF16) for w in ws],
        compiler_params=_params("arbitrary"),
    )(place, *ws)


def _gather_ride(bufs):
    na = len(bufs)
    sent = [("me", "half", "x"), ("me", "half", "y"), ("x", "second quarter", "y"), ("y", "first quarter", "x")]
    landed = [("x", "half"), ("y", "half"), ("d", "second quarter"), ("d", "first quarter")]

    def legs(outs, sems):
        send_sem, recv_sem, fsend_sem, frecv_sem = sems
        x, y, c, _ = _place()
        slot = {"me": 2 * x + y, "x": 2 * (1 - x) + y, "y": 2 * x + (1 - y), "d": 2 * (1 - x) + (1 - y)}
        peer = {"x": (1 - x, y, c), "y": (x, 1 - y, c)}

        def rows(a, which, piece, core):
            hr = outs[a].shape[1] // 2
            lo, n = {"half": (0, hr), "first quarter": (0, hr // 2), "second quarter": (hr // 2, hr // 2)}[piece]
            return outs[a].at[slot[which], pl.ds(core * hr + lo, n)]

        def ici(a, k):
            which, piece, to = sent[k]
            ref = rows(a, which, piece, c)
            return pltpu.make_async_remote_copy(src_ref=ref, dst_ref=ref, send_sem=send_sem.at[a, k],
                                                recv_sem=recv_sem.at[a, k], device_id=peer[to], device_id_type=MESH)

        def arrival(a, k):
            ref = rows(a, *landed[k], c)
            return pltpu.make_async_remote_copy(src_ref=ref, dst_ref=ref, send_sem=send_sem.at[a, k],
                                                recv_sem=recv_sem.at[a, k], device_id=peer["x"], device_id_type=MESH)

        def d2d(a, k, core):
            ref = rows(a, *landed[k], core)
            return pltpu.make_async_remote_copy(src_ref=ref, dst_ref=ref, send_sem=fsend_sem.at[a, k],
                                                recv_sem=frecv_sem.at[a, k], device_id=(x, y, 1 - c), device_id_type=MESH)

        return c, ici, arrival, d2d

    def start(ins, outs, sems):
        _, ici, _, _ = legs(outs, sems)
        for a in range(na):
            ici(a, 0).start()
            ici(a, 1).start()

    def finish(ins, outs, sems):
        c, ici, arrival, d2d = legs(outs, sems)
        for a in range(na):
            for k in (0, 1):
                arrival(a, k).wait_recv()
                ici(a, 2 + k).start()
                d2d(a, k, c).start()
        for a in range(na):
            for k in (2, 3):
                arrival(a, k).wait_recv()
                d2d(a, k, c).start()
        for a in range(na):
            for k in range(len(landed)):
                d2d(a, k, 1 - c).wait_recv()
        for a in range(na):
            for k in range(len(sent)):
                ici(a, k).wait_send()
                d2d(a, k, c).wait_send()

    return _Ride(bufs, [jax.ShapeDtypeStruct(b.shape, b.dtype) for b in bufs], [pltpu.SemaphoreType.DMA((na, 4))] * 4,
                 start, finish, {a: a for a in range(na)})


def _pair_ride(grads):
    na = len(grads)

    def copies(ins, outs, sems):
        send_sem, recv_sem = sems
        x, y, c, _ = _place()
        res = []
        for a in range(na):
            hr = ins[a].shape[1] // 2
            res.append(pltpu.make_async_remote_copy(
                src_ref=ins[a].at[:, pl.ds((1 - c) * hr, hr)], dst_ref=outs[a],
                send_sem=send_sem.at[a], recv_sem=recv_sem.at[a], device_id=(x, y, 1 - c), device_id_type=MESH))
        return res

    def start(ins, outs, sems):
        for cp in copies(ins, outs, sems):
            cp.start()

    def finish(ins, outs, sems):
        for cp in copies(ins, outs, sems):
            cp.wait()

    return _Ride(grads, [jax.ShapeDtypeStruct((g.shape[0], g.shape[1] // 2, g.shape[2]), g.dtype) for g in grads],
                 [pltpu.SemaphoreType.DMA((na,))] * 2, start, finish)


def _chip_ride(sums):
    na = len(sums)

    def copies(ins, outs, sems):
        send_sem, recv_sem = sems
        x, y, c, chips = _place()
        res = []
        for a in range(na):
            for j, (px, py) in enumerate(chips):
                res.append(pltpu.make_async_remote_copy(
                    src_ref=ins[a].at[2 * px + py], dst_ref=outs[a].at[j],
                    send_sem=send_sem.at[a, j], recv_sem=recv_sem.at[a, j], device_id=(px, py, c), device_id_type=MESH))
        return res

    def start(ins, outs, sems):
        for cp in copies(ins, outs, sems):
            cp.start()

    def finish(ins, outs, sems):
        for cp in copies(ins, outs, sems):
            cp.wait()

    return _Ride(sums, [jax.ShapeDtypeStruct((3,) + s.shape[1:], s.dtype) for s in sums],
                 [pltpu.SemaphoreType.DMA((na, 3))] * 2, start, finish)


def _finish_ride(grads, gpack=None):
    na = len(grads)

    def halves(outs, sems, which):
        x, y, c, _ = _place()
        res = []
        for a in range(na):
            hr = outs[a].shape[0] // 2
            rows = outs[a].at[pl.ds((c if which == "mine" else 1 - c) * hr, hr)]
            res.append(pltpu.make_async_remote_copy(
                src_ref=rows, dst_ref=rows, send_sem=sems[0].at[a], recv_sem=sems[1].at[a],
                device_id=(x, y, 1 - c), device_id_type=MESH))
        return res

    def gains(ins, outs, sems):
        x, y, c, _ = _place()
        dev = 4 * x + 2 * y + c
        g_in, g_out = ins[na], outs[na]
        own = pltpu.make_async_copy(g_in, g_out.at[dev], sems[2])
        sends, lands = [], []
        for k in range(N_DEV - 1):
            bx, by, bc = (k + 1) // 4, ((k + 1) // 2) % 2, (k + 1) % 2
            peer = (jnp.bitwise_xor(x, bx), jnp.bitwise_xor(y, by), jnp.bitwise_xor(c, bc))
            sends.append(pltpu.make_async_remote_copy(
                src_ref=g_in, dst_ref=g_out.at[dev], send_sem=sems[3].at[k], recv_sem=sems[4].at[k],
                device_id=peer, device_id_type=MESH))
            slot = g_out.at[jnp.bitwise_xor(dev, k + 1)]
            lands.append(pltpu.make_async_remote_copy(
                src_ref=slot, dst_ref=slot, send_sem=sems[3].at[k], recv_sem=sems[4].at[k],
                device_id=peer, device_id_type=MESH))
        return own, sends, lands

    def start(ins, outs, sems):
        for cp in halves(outs, sems, "mine"):
            cp.start()
        if gpack is not None:
            own, sends, _ = gains(ins, outs, sems)
            own.start()
            for cp in sends:
                cp.start()

    def finish(ins, outs, sems):
        for cp in halves(outs, sems, "sibling's"):
            cp.wait_recv()
        if gpack is not None:
            own, sends, lands = gains(ins, outs, sems)
            for cp in lands:
                cp.wait_recv()
            for cp in sends:
                cp.wait_send()
            own.wait()
        for cp in halves(outs, sems, "mine"):
            cp.wait_send()

    shapes = [jax.ShapeDtypeStruct(g.shape, g.dtype) for g in grads]
    sems = [pltpu.SemaphoreType.DMA((na,))] * 2
    if gpack is None:
        return _Ride(grads, shapes, sems, start, finish, {a: a for a in range(na)})
    return _Ride(list(grads) + [gpack], shapes + [jax.ShapeDtypeStruct((N_DEV,) + gpack.shape, gpack.dtype)],
                 sems + [pltpu.SemaphoreType.DMA, pltpu.SemaphoreType.DMA((N_DEV - 1,)), pltpu.SemaphoreType.DMA((N_DEV - 1,))],
                 start, finish, {a: a for a in range(na)})


def _pair_sum(place, grad, got, name):
    ns, r, cols = grad.shape
    hr = r // 2

    def body(place_ref, g_ref, r_ref, o_ref):
        o_ref[...] = (g_ref[...] + r_ref[...]).astype(BF16)

    return pl.pallas_call(
        body, name=name,
        grid_spec=pltpu.PrefetchScalarGridSpec(
            num_scalar_prefetch=1, grid=(ns,),
            in_specs=[pl.BlockSpec((None, hr, cols), lambda s, pr: (s, pr[1], 0)),
                      pl.BlockSpec((None, hr, cols), lambda s, pr: (s, 0, 0))],
            out_specs=pl.BlockSpec((None, hr, cols), lambda s, pr: (s, 0, 0))),
        out_shape=jax.ShapeDtypeStruct((ns, hr, cols), BF16),
        compiler_params=_params("arbitrary"),
    )(place, grad, got)


def _chip_sum(place, grad, got, others, name):
    ns, r, cols = grad.shape
    hr = r // 2
    nb = 2
    tr = hr // nb

    def body(place_ref, g_ref, r_ref, o3_ref, o_ref):
        acc = g_ref[...] + r_ref[...]
        for j in range(3):
            acc = acc + o3_ref[j].astype(F32)
        o_ref[...] = acc

    return pl.pallas_call(
        body, name=name,
        grid_spec=pltpu.PrefetchScalarGridSpec(
            num_scalar_prefetch=1, grid=(nb,),
            in_specs=[pl.BlockSpec((None, tr, cols), lambda i, pr: (pr[0], pr[1] * nb + i, 0)),
                      pl.BlockSpec((None, tr, cols), lambda i, pr: (pr[0], i, 0)),
                      pl.BlockSpec((3, tr, cols), lambda i, pr: (0, i, 0))],
            out_specs=pl.BlockSpec((tr, cols), lambda i, pr: (pr[1] * nb + i, 0))),
        out_shape=jax.ShapeDtypeStruct((r, cols), F32),
        compiler_params=_params("arbitrary"),
    )(place, grad, got, others)


def _pack_gains(parts, d):
    def body(*refs):
        ins, o_ref = refs[:-1], refs[-1]
        o_ref[...] = jnp.zeros_like(o_ref)
        for k, r in enumerate(ins):
            o_ref[k:k + 1, 0:r.shape[1]] = jnp.sum(r[...], axis=0, keepdims=True)

    return pl.pallas_call(
        body, name="pack_gains", out_shape=jax.ShapeDtypeStruct((GAIN_ROWS, d), F32),
    )(*parts)


def _adamw_math(w, g, m, v):
    m = ADAM_B1 * m + (1.0 - ADAM_B1) * g
    v = ADAM_B2 * v + (1.0 - ADAM_B2) * jnp.square(g)
    m_hat = m / (1.0 - ADAM_B1 ** ADAM_STEP)
    v_hat = v / (1.0 - ADAM_B2 ** ADAM_STEP)
    return -ADAM_LR * (m_hat / (jnp.sqrt(v_hat) + ADAM_EPS) + ADAM_WD * w), m, v


def _adamw(ws, gs, ms, vs):
    n = len(ws)

    def body(*refs):
        ins, outs = refs[:4 * n], refs[4 * n:]
        for k in range(n):
            w_ref, g_ref, m_ref, v_ref = ins[4 * k:4 * k + 4]
            go_ref, d_ref, nm_ref, nv_ref = outs[4 * k:4 * k + 4]
            g = g_ref[...]
            go_ref[...] = g
            d_ref[...], nm_ref[...], nv_ref[...] = _adamw_math(w_ref[...], g, m_ref[...], v_ref[...])

    parts = 2 * ROW_QUARTERS
    tile = lambda w: pl.BlockSpec((w.shape[0] // parts, w.shape[1]), lambda i: (i, 0))
    res = pl.pallas_call(
        body, name="adamw_shards", grid=(parts,),
        in_specs=[tile(w) for w in ws for _ in range(4)], out_specs=[tile(w) for w in ws for _ in range(4)],
        out_shape=[jax.ShapeDtypeStruct(w.shape, F32) for w in ws for _ in range(4)],
        compiler_params=_params("arbitrary"),
    )(*[a for quad in zip(ws, gs, ms, vs) for a in quad])
    return [res[4 * k:4 * k + 4] for k in range(n)]


def _adamw_gain(gall, row, w, m, v, name):
    n = w.shape[1]

    def body(ga_ref, w_ref, m_ref, v_ref, g_ref, d_ref, nm_ref, nv_ref):
        g = ga_ref[0, row:row + 1, 0:n]
        for k in range(1, N_DEV):
            g = g + ga_ref[k, row:row + 1, 0:n]
        g_ref[...] = g
        d_ref[...], nm_ref[...], nv_ref[...] = _adamw_math(w_ref[...], g, m_ref[...], v_ref[...])

    return pl.pallas_call(
        body, name=name, out_shape=[jax.ShapeDtypeStruct((1, n), F32)] * 4,
    )(gall, w, m, v)


def kernel(x, norm_ffn1, ffn1_w_gate, ffn1_w_up, ffn1_w_down, norm_mix, w_in, ret_norm_gain, w_out, norm_ffn2, ffn2_w_gate, ffn2_w_up, ffn2_w_down, norm_final, loss_target, m_norm_ffn1, m_ffn1_w_gate, m_ffn1_w_up, m_ffn1_w_down, m_norm_mix, m_w_in, m_ret_norm_gain, m_w_out, m_norm_ffn2, m_ffn2_w_gate, m_ffn2_w_up, m_ffn2_w_down, m_norm_final, v_norm_ffn1, v_ffn1_w_gate, v_ffn1_w_up, v_ffn1_w_down, v_norm_mix, v_w_in, v_ret_norm_gain, v_w_out, v_norm_ffn2, v_ffn2_w_gate, v_ffn2_w_up, v_ffn2_w_down, v_norm_final):
    d = x.shape[-1]
    mats = [ffn1_w_gate, ffn1_w_up, ffn1_w_down, w_in, w_out, ffn2_w_gate, ffn2_w_up, ffn2_w_down]
    mats_m = [m_ffn1_w_gate, m_ffn1_w_up, m_ffn1_w_down, m_w_in, m_w_out, m_ffn2_w_gate, m_ffn2_w_up, m_ffn2_w_down]
    mats_v = [v_ffn1_w_gate, v_ffn1_w_up, v_ffn1_w_down, v_w_in, v_w_out, v_ffn2_w_gate, v_ffn2_w_up, v_ffn2_w_down]
    mat_names = ["ffn1_w_gate", "ffn1_w_up", "ffn1_w_down", "w_in", "w_out", "ffn2_w_gate", "ffn2_w_up", "ffn2_w_down"]
    gains = [norm_ffn1, norm_mix, ret_norm_gain, norm_ffn2, norm_final.reshape(1, d)]
    gains_m = [m_norm_ffn1, m_norm_mix, m_ret_norm_gain, m_norm_ffn2, m_norm_final.reshape(1, d)]
    gains_v = [v_norm_ffn1, v_norm_mix, v_ret_norm_gain, v_norm_ffn2, v_norm_final.reshape(1, d)]
    gain_names = ["norm_ffn1", "norm_mix", "ret_norm_gain", "norm_ffn2", "norm_final"]

    turned = lambda n: n.endswith(("w_gate", "w_up"))
    local = lambda a, n: jnp.swapaxes(a, 1, 2)[0] if turned(n) else a[0]
    back = lambda a, n: jnp.swapaxes(a[None], 1, 2) if turned(n) else a[None]
    shards = [local(w, n) for w, n in zip(mats, mat_names)]
    place = jnp.stack([2 * lax.axis_index("x") + lax.axis_index("y"), lax.axis_index("c")]).astype(jnp.int32)
    placed = _place_shards(place, shards)
    loss_p, dx, shard_grads, gall = _step(x[0], loss_target[0], gains, placed, place)

    out_g, out_d, out_m, out_v = {}, {}, {}, {}
    updates = _adamw(shards, shard_grads, [local(m, n) for m, n in zip(mats_m, mat_names)],
                     [local(v, n) for v, n in zip(mats_v, mat_names)])
    for n, quad in zip(mat_names, updates):
        out_g[n], out_d[n], out_m[n], out_v[n] = [back(a, n) for a in quad]
    for row, (n, w, m, v) in enumerate(zip(gain_names, gains, gains_m, gains_v)):
        res = _adamw_gain(gall, row, w, m, v, f"adamw_{n}")
        shape = (d,) if n == "norm_final" else w.shape
        out_g[n], out_d[n], out_m[n], out_v[n] = [r.reshape(shape) for r in res]

    loss = lax.psum(jnp.sum(loss_p), ("x", "y", "c"))
    order = ["norm_ffn1", "ffn1_w_gate", "ffn1_w_up", "ffn1_w_down", "norm_mix", "w_in", "ret_norm_gain", "w_out",
             "norm_ffn2", "ffn2_w_gate", "ffn2_w_up", "ffn2_w_down", "norm_final"]
    return (loss, dx[None], *[out_g[n] for n in order], *[out_d[n] for n in order],
            *[out_m[n] for n in order], *[out_v[n] for n in order])
```

```python
import functools

import jax
import jax.numpy as jnp
from jax import lax
from jax.experimental import pallas as pl
from jax.experimental.pallas import tpu as pltpu

F32 = jnp.float32
BF16 = jnp.bfloat16
MESH = pl.DeviceIdType.MESH

NORM_EPS = 1e-6
GN_EPS = 1e-6
ROPE_BASE = 10000.0
RET_HEADS = 4
RET_DIM = 128
RET_WIDTH = 512
RET_CHUNK = 128
ATT_DIM = 64
ATT_WIDTH = 512
ATT_BLOCK = 128
DILATIONS = (1, 4, 16)
LANE = 128
N_SHARD = 4
ADAM_LR, ADAM_B1, ADAM_B2, ADAM_EPS, ADAM_WD, ADAM_STEP = 0.001, 0.9, 0.999, 1e-08, 0.01, 10

V7X_VMEM_BYTES = 64 * 1024 * 1024
VMEM_LIMIT = V7X_VMEM_BYTES - 8 * 1024 * 1024

NT = (((1,), (1,)), ((), ()))
TN = (((0,), (0,)), ((), ()))


def _params(*sem):
    return pltpu.CompilerParams(dimension_semantics=sem, vmem_limit_bytes=VMEM_LIMIT)


def _dot(a, b, dims=None):
    if dims is None:
        return jnp.dot(a, b, preferred_element_type=F32)
    return lax.dot_general(a, b, dims, preferred_element_type=F32)


def _sigmoid(x):
    return 1.0 / (1.0 + jnp.exp(-x))


def _load_weights(pairs, sems):
    copies = [pltpu.make_async_copy(src, dst, sems.at[k]) for k, (src, dst) in enumerate(pairs)]
    for cp in copies:
        cp.start()
    for cp in copies:
        cp.wait()


def _rows8(v):
    r, c = v.shape
    return v.reshape(r // 8, 8, c).sum(axis=0)


class _Ride:
    def __init__(self, inputs, out_shapes, sems, start, finish, aliases=None, middle=None):
        self.inputs, self.out_shapes, self.sems = list(inputs), list(out_shapes), list(sems)
        self.start, self.middle, self.finish, self.aliases = start, middle, finish, dict(aliases or {})


def _pallas(body, rides, *, name, in_specs, out_specs, out_shape, args, grid=(), scratch_shapes=(), sem=()):
    rides = [r for r in (rides or []) if r is not None]
    n_in, n_out, n_scr = len(args), len(out_shape), len(scratch_shapes)
    hbm = pl.BlockSpec(memory_space=pl.ANY)
    r_in = [a for r in rides for a in r.inputs]
    r_out = [s for r in rides for s in r.out_shapes]
    r_sem = [s for r in rides for s in r.sems]
    aliases, spans, ki, ko, ks = {}, [], 0, 0, 0
    for r in rides:
        aliases.update({n_in + ki + i: n_out + ko + o for i, o in r.aliases.items()})
        spans.append((ki, ko, ks))
        ki, ko, ks = ki + len(r.inputs), ko + len(r.out_shapes), ks + len(r.sems)

    def wrapped(*refs):
        ins, rin = refs[:n_in], refs[n_in:n_in + len(r_in)]
        o0 = n_in + len(r_in)
        outs, rout = refs[o0:o0 + n_out], refs[o0 + n_out:o0 + n_out + len(r_out)]
        s0 = o0 + n_out + len(r_out)
        scr, rsem = refs[s0:s0 + n_scr], refs[s0 + n_scr:]
        part = lambda r, k: (rin[spans[k][0]:spans[k][0] + len(r.inputs)], rout[spans[k][1]:spans[k][1] + len(r.out_shapes)],
                             rsem[spans[k][2]:spans[k][2] + len(r.sems)])
        first = functools.reduce(jnp.logical_and, [pl.program_id(k) == 0 for k in range(len(grid))], True)
        last = functools.reduce(jnp.logical_and, [pl.program_id(k) == grid[k] - 1 for k in range(len(grid))], True)
        if rides:
            @pl.when(first)
            def _():
                for k, r in enumerate(rides):
                    r.start(*part(r, k))

        if any(r.middle for r in rides):
            halfway = functools.reduce(jnp.logical_and, [pl.program_id(k) == 0 for k in range(1, len(grid))],
                                       pl.program_id(0) == grid[0] // 2)

            @pl.when(halfway)
            def _():
                for k, r in enumerate(rides):
                    if r.middle:
                        r.middle(*part(r, k))

        body(*ins, *outs, *scr)
        if rides:
            @pl.when(last)
            def _():
                for k, r in enumerate(rides):
                    r.finish(*part(r, k))

    res = pl.pallas_call(
        wrapped, name=name, grid=grid,
        in_specs=list(in_specs) + [hbm] * len(r_in), out_specs=list(out_specs) + [hbm] * len(r_out),
        out_shape=list(out_shape) + r_out, input_output_aliases=aliases,
        scratch_shapes=list(scratch_shapes) + r_sem,
        compiler_params=pltpu.CompilerParams(dimension_semantics=sem, vmem_limit_bytes=VMEM_LIMIT) if grid else None,
    )(*args, *r_in)
    extras = [list(res[n_out + ko:n_out + ko + len(r.out_shapes)]) for r, (_, ko, _) in zip(rides, spans)]
    return list(res[:n_out]), extras


def _run(ride, name):
    def body(*refs):
        n_in, n_out = len(ride.inputs), len(ride.out_shapes)
        parts = refs[:n_in], refs[n_in:n_in + n_out], refs[n_in + n_out:]
        ride.start(*parts)
        if ride.middle:
            ride.middle(*parts)
        ride.finish(*parts)

    hbm = pl.BlockSpec(memory_space=pl.ANY)
    return list(pl.pallas_call(
        body, name=name, in_specs=[hbm] * len(ride.inputs), out_specs=[hbm] * len(ride.out_shapes),
        out_shape=ride.out_shapes, input_output_aliases=ride.aliases, scratch_shapes=ride.sems,
    )(*ride.inputs))


def _loss_head(hv, gain_ref, tg_ref, loss_ref, dgain_ref):
    d = hv.shape[1]
    r = lax.rsqrt(jnp.mean(hv * hv, axis=-1, keepdims=True) + NORM_EPS)
    xh = hv * r
    err = xh * gain_ref[...] - tg_ref[...]
    sq = _rows8(jnp.square(err))
    loss_ref[...] += 0.5 * functools.reduce(jnp.add, [sq[:, k * LANE:(k + 1) * LANE] for k in range(d // LANE)]) / d
    dy = err / d
    dgain_ref[...] += _rows8(dy * xh)
    dxh = dy * gain_ref[...]
    return r * (dxh - xh * jnp.mean(dxh * xh, axis=-1, keepdims=True))


V7X_MXU_TILE = 256
FFN_CHUNK_TILES = 3


def _hidden_chunks(f):
    step = FFN_CHUNK_TILES * V7X_MXU_TILE
    return [slice(s, min(s + step, f)) for s in range(0, f, step)]


def _flat(w):
    return w.reshape(w.shape[0] * w.shape[1], w.shape[2])


def _ffn_fwd(x, gain, wg, wu, wd, name, rides=None, head=None):
    t, d = x.shape
    wg, wu, wd = _flat(wg), _flat(wu), _flat(wd)
    f = wg.shape[0]
    tm = min(512, t)
    nh = 0 if head is None else 2

    def body(*refs):
        x_ref, gain_ref = refs[:2]
        wg_hbm, wu_hbm, wd_hbm, h_ref, xn_ref, g_ref, u_ref, a_ref = refs[2 + nh:10 + nh]
        sums = refs[10 + nh:12 + nh]
        wg_v, wu_v, wd_v, sems = refs[-4:]

        @pl.when(pl.program_id(0) == 0)
        def _():
            _load_weights([(wg_hbm, wg_v), (wu_hbm, wu_v), (wd_hbm, wd_v)], sems)
            if head is not None:
                for s_ref in sums:
                    s_ref[...] = jnp.zeros_like(s_ref)

        xv = x_ref[...]
        r = lax.rsqrt(jnp.mean(xv * xv, axis=-1, keepdims=True) + NORM_EPS)
        xn = (xv * r * gain_ref[...]).astype(BF16)
        xn_ref[...] = xn
        acc = jnp.zeros((tm, d), F32)
        for c in _hidden_chunks(f):
            g = _dot(xn, wg_v[c, :], NT)
            u = _dot(xn, wu_v[c, :], NT)
            g_ref[:, c] = g.astype(BF16)
            u_ref[:, c] = u.astype(BF16)
            a = (g * _sigmoid(g) * u).astype(BF16)
            a_ref[:, c] = a
            acc = acc + _dot(a, wd_v[c, :])
        hv = xv + 0.5 * acc
        h_ref[...] = hv if head is None else _loss_head(hv, refs[2], refs[3], *sums)

    hbm = pl.BlockSpec(memory_space=pl.ANY)
    hid = pl.BlockSpec((tm, f), lambda i: (i, 0))
    tile = pl.BlockSpec((tm, d), lambda i: (i, 0))
    row = pl.BlockSpec((1, d), lambda i: (0, 0))
    sums = [] if head is None else [(pl.BlockSpec((8, LANE), lambda i: (0, 0)), jax.ShapeDtypeStruct((8, LANE), F32)),
                                    (pl.BlockSpec((8, d), lambda i: (0, 0)), jax.ShapeDtypeStruct((8, d), F32))]
    return _pallas(
        body, rides, name=name, grid=(t // tm,),
        in_specs=[tile, row] + ([] if head is None else [row, tile]) + [hbm, hbm, hbm],
        out_specs=[tile, tile, hid, hid, hid] + [s for s, _ in sums],
        out_shape=[jax.ShapeDtypeStruct((t, d), F32), jax.ShapeDtypeStruct((t, d), BF16)]
        + [jax.ShapeDtypeStruct((t, f), BF16)] * 3 + [s for _, s in sums],
        scratch_shapes=[pltpu.VMEM(wg.shape, BF16), pltpu.VMEM(wu.shape, BF16), pltpu.VMEM(wd.shape, BF16),
                        pltpu.SemaphoreType.DMA((3,))],
        sem=("arbitrary",), args=[x, gain] + ([] if head is None else list(head)) + [wg, wu, wd])


def _ffn_bwd_data(dy, x, gain, g, u, wg, wu, wd, name, rides=None):
    t, d = x.shape
    wg, wu, wd = _flat(wg), _flat(wu), _flat(wd)
    f = wg.shape[0]
    tm = min(256, t)

    def body(dy_ref, x_ref, gain_ref, g_ref, u_ref, wg_hbm, wu_hbm, wd_hbm, dx_ref, dg_ref, du_ref, dgain_ref,
             wg_v, wu_v, wd_v, sems):
        @pl.when(pl.program_id(0) == 0)
        def _():
            _load_weights([(wg_hbm, wg_v), (wu_hbm, wu_v), (wd_hbm, wd_v)], sems)
            dgain_ref[...] = jnp.zeros_like(dgain_ref)

        dyv = dy_ref[...]
        dyh = (0.5 * dyv).astype(BF16)
        dxn = jnp.zeros((tm, d), F32)
        chunks = _hidden_chunks(f)
        das = [_dot(dyh, wd_v[c, :], NT) for c in chunks]
        for c, da in zip(chunks, das):
            gj = g_ref[:, c].astype(F32)
            uj = u_ref[:, c].astype(F32)
            sig = _sigmoid(gj)
            dgj = (da * uj * (sig * (1.0 + gj * (1.0 - sig)))).astype(BF16)
            duj = (da * (gj * sig)).astype(BF16)
            dg_ref[:, c] = dgj
            du_ref[:, c] = duj
            dxn = dxn + _dot(dgj, wg_v[c, :]) + _dot(duj, wu_v[c, :])
        xv = x_ref[...]
        r = lax.rsqrt(jnp.mean(xv * xv, axis=-1, keepdims=True) + NORM_EPS)
        xh = xv * r
        dgain_ref[...] += _rows8(dxn * xh)
        dxh = dxn * gain_ref[...]
        dx_ref[...] = dyv + r * (dxh - xh * jnp.mean(dxh * xh, axis=-1, keepdims=True))

    hbm = pl.BlockSpec(memory_space=pl.ANY)
    tile = pl.BlockSpec((tm, d), lambda i: (i, 0))
    hid = pl.BlockSpec((tm, f), lambda i: (i, 0))
    return _pallas(
        body, rides, name=name, grid=(t // tm,),
        in_specs=[tile, tile, pl.BlockSpec((1, d), lambda i: (0, 0)), hid, hid, hbm, hbm, hbm],
        out_specs=[tile, hid, hid, pl.BlockSpec((8, d), lambda i: (0, 0))],
        out_shape=[jax.ShapeDtypeStruct((t, d), F32), jax.ShapeDtypeStruct((t, f), BF16),
                   jax.ShapeDtypeStruct((t, f), BF16), jax.ShapeDtypeStruct((8, d), F32)],
        scratch_shapes=[pltpu.VMEM(wg.shape, BF16), pltpu.VMEM(wu.shape, BF16), pltpu.VMEM(wd.shape, BF16),
                        pltpu.SemaphoreType.DMA((3,))],
        sem=("arbitrary",), args=[dy, x, gain, g, u, wg, wu, wd])


WGRAD_ROW_BLOCKS = 2


def _ffn_wgrad_down(a, dy, name, rides=None):
    t, d = dy.shape
    f = a.shape[1]
    fb = f // WGRAD_ROW_BLOCKS
    tk = min(1024, t)

    def body(dy_ref, a_ref, dwd_ref):
        @pl.when(pl.program_id(1) == 0)
        def _():
            dwd_ref[...] = jnp.zeros_like(dwd_ref)

        dwd_ref[...] += _dot(a_ref[...], (0.5 * dy_ref[...]).astype(BF16), TN)

    return _pallas(
        body, rides, name=name, grid=(WGRAD_ROW_BLOCKS, t // tk),
        in_specs=[pl.BlockSpec((tk, d), lambda j, k: (k, 0)), pl.BlockSpec((tk, fb), lambda j, k: (k, j))],
        out_specs=[pl.BlockSpec((fb, d), lambda j, k: (j, 0))],
        out_shape=[jax.ShapeDtypeStruct((f, d), F32)],
        sem=("arbitrary", "arbitrary"), args=[dy, a])


def _ffn_wgrad_gu(xn, dhs, name, rides=None):
    t, d = xn.shape
    n = len(dhs)
    f = dhs[0].shape[1]
    fb = f // WGRAD_ROW_BLOCKS
    tk = min(2048 // n, t)

    def body(xn_ref, *refs):
        @pl.when(pl.program_id(1) == 0)
        def _():
            for o_ref in refs[n:]:
                o_ref[...] = jnp.zeros_like(o_ref)

        xnv = xn_ref[...]
        for dh_ref, o_ref in zip(refs[:n], refs[n:]):
            o_ref[...] += _dot(dh_ref[...], xnv, TN)

    hid = pl.BlockSpec((tk, fb), lambda j, k: (k, j))
    out = pl.BlockSpec((fb, d), lambda j, k: (j, 0))
    return _pallas(
        body, rides, name=name, grid=(WGRAD_ROW_BLOCKS, t // tk),
        in_specs=[pl.BlockSpec((tk, d), lambda j, k: (k, 0))] + [hid] * n,
        out_specs=[out] * n, out_shape=[jax.ShapeDtypeStruct((f, d), F32)] * n,
        sem=("arbitrary", "arbitrary"), args=[xn] + list(dhs))


def _tn_matmul(a, b, bn, name):
    t, m = a.shape
    n = b.shape[1]
    tk = min(2048, t)

    def body(a_ref, b_ref, o_ref):
        @pl.when(pl.program_id(1) == 0)
        def _():
            o_ref[...] = jnp.zeros_like(o_ref)

        o_ref[...] += _dot(a_ref[...].astype(BF16), b_ref[...].astype(BF16), TN)

    return pl.pallas_call(
        body, name=name, grid=(n // bn, t // tk),
        in_specs=[pl.BlockSpec((tk, m), lambda j, k: (k, 0)), pl.BlockSpec((tk, bn), lambda j, k: (k, j))],
        out_specs=pl.BlockSpec((None, m, bn), lambda j, k: (j, 0, 0)),
        out_shape=jax.ShapeDtypeStruct((n // bn, m, bn), F32),
        compiler_params=_params("arbitrary", "arbitrary"),
    )(a, b)


def _chunk_scratch(tm, w):
    return pltpu.VMEM((w // LANE, tm, LANE), F32)


def _regroup_store(cbuf, out_ref, dil, chunks=None):
    n = out_ref.shape[1]
    for k in range(cbuf.shape[0]) if chunks is None else chunks:
        for g in range(dil):
            rows = cbuf[k] if dil == 1 else cbuf[k, pl.ds(g, n, stride=dil), :]
            out_ref[g, :, k * LANE:(k + 1) * LANE] = rows.astype(out_ref.dtype)


def _natural_rows(ref, dil, cbuf):
    if dil == 1:
        return ref[0].astype(F32)
    n = ref.shape[1]
    for g in range(dil):
        for k in range(cbuf.shape[0]):
            cbuf[k, pl.ds(g, n, stride=dil), :] = ref[g, :, k * LANE:(k + 1) * LANE].astype(F32)
    return jnp.concatenate([cbuf[k] for k in range(cbuf.shape[0])], axis=1)


IN_CHUNK_TILES = 4


def _column_chunks(n):
    step = IN_CHUNK_TILES * V7X_MXU_TILE
    return [slice(s, min(s + step, n)) for s in range(0, n, step)]


def _load_side_by_side(w_hbm, w_v, sems):
    ns, _, cs = w_hbm.shape
    copies = [pltpu.make_async_copy(w_hbm.at[j], w_v.at[:, pl.ds(j * cs, cs)], sems.at[j]) for j in range(ns)]
    for cp in copies:
        cp.start()
    for cp in copies:
        cp.wait()


def _inproj_fwd(h, gain, win):
    t, d = h.shape
    ns, _, cs = win.shape
    tm = min(512, t)
    rw, aw = 4 * RET_WIDTH, 3 * ATT_WIDTH

    def body(h_ref, gain_ref, w_hbm, xn_ref, ur_ref, *rest):
        a_refs, abuf, w_v, sems = rest[:-3], rest[-3], rest[-2], rest[-1]

        @pl.when(pl.program_id(0) == 0)
        def _():
            _load_side_by_side(w_hbm, w_v, sems)

        hv = h_ref[...]
        r = lax.rsqrt(jnp.mean(hv * hv, axis=-1, keepdims=True) + NORM_EPS)
        xn = (hv * r * gain_ref[...]).astype(BF16)
        xn_ref[...] = xn
        for c in reversed(_column_chunks(ns * cs)):
            res = _dot(xn, w_v[:, c])
            mine = []
            for k in range((c.stop - c.start) // LANE):
                chunk = c.start // LANE + k
                piece = res[:, k * LANE:(k + 1) * LANE]
                if chunk < rw // LANE:
                    ur_ref[:, chunk * LANE:(chunk + 1) * LANE] = piece
                else:
                    abuf[chunk - rw // LANE] = piece
                    mine.append(chunk - rw // LANE)
            for dil, a_ref in zip(DILATIONS, a_refs):
                _regroup_store(abuf, a_ref, dil, mine)

    return pl.pallas_call(
        body, name="inproj_fwd", grid=(t // tm,),
        in_specs=[pl.BlockSpec((tm, d), lambda i: (i, 0)), pl.BlockSpec((1, d), lambda i: (0, 0)),
                  pl.BlockSpec(memory_space=pl.ANY)],
        out_specs=[pl.BlockSpec((tm, d), lambda i: (i, 0)), pl.BlockSpec((tm, rw), lambda i: (i, 0))]
        + [pl.BlockSpec((dil, tm // dil, aw), lambda i: (0, i, 0)) for dil in DILATIONS],
        out_shape=[jax.ShapeDtypeStruct((t, d), BF16), jax.ShapeDtypeStruct((t, rw), F32)]
        + [jax.ShapeDtypeStruct((dil, t // dil, aw), BF16) for dil in DILATIONS],
        scratch_shapes=[_chunk_scratch(tm, aw), pltpu.VMEM((d, ns * cs), BF16), pltpu.SemaphoreType.DMA((ns,))],
        compiler_params=_params("arbitrary"),
    )(h, gain, win)


def _inproj_bwd(pieces, parts, h, gain, dres, win):
    t, d = h.shape
    ns, _, cs = win.shape
    pw = pieces[0].shape[1]
    tm = min(512, t)
    npc, nk = len(pieces), len(parts[0])
    flat_parts = [a for p in parts for a in p]

    def body(*refs):
        p_refs, a_refs = refs[:npc], refs[npc:npc + len(flat_parts)]
        h_ref, gain_ref, dres_ref, w_hbm, dh_ref, du_ref, dgain_ref, buf, w_v, sems = refs[npc + len(flat_parts):]

        @pl.when(pl.program_id(0) == 0)
        def _():
            _load_side_by_side(w_hbm, w_v, sems)
            dgain_ref[...] = jnp.zeros_like(dgain_ref)

        for k in range(npc):
            du_ref[:, k * pw:(k + 1) * pw] = p_refs[k][...]
        for k in range(nk):
            acc = None
            for b, dil in enumerate(DILATIONS):
                rows = _natural_rows(a_refs[b * nk + k], dil, buf)
                acc = rows if acc is None else acc + rows
            du_ref[:, (npc + k) * pw:(npc + k + 1) * pw] = acc.astype(BF16)
        dxn = jnp.zeros((tm, d), F32)
        for c in _column_chunks(ns * cs):
            dxn = dxn + _dot(du_ref[:, c], w_v[:, c], NT)
        hv = h_ref[...]
        r = lax.rsqrt(jnp.mean(hv * hv, axis=-1, keepdims=True) + NORM_EPS)
        xh = hv * r
        dgain_ref[...] += _rows8(dxn * xh)
        dxh = dxn * gain_ref[...]
        dh_ref[...] = dres_ref[...] + r * (dxh - xh * jnp.mean(dxh * xh, axis=-1, keepdims=True))

    tile = pl.BlockSpec((tm, d), lambda i: (i, 0))
    cols = (npc + nk) * pw
    return pl.pallas_call(
        body, name="inproj_bwd", grid=(t // tm,),
        in_specs=[pl.BlockSpec((tm, pw), lambda i: (i, 0))] * npc
        + [_regrouped_spec(tm, dil, pw) for dil in DILATIONS for _ in range(nk)]
        + [tile, pl.BlockSpec((1, d), lambda i: (0, 0)), tile, pl.BlockSpec(memory_space=pl.ANY)],
        out_specs=[tile, pl.BlockSpec((tm, cols), lambda i: (i, 0)), pl.BlockSpec((8, d), lambda i: (0, 0))],
        out_shape=[jax.ShapeDtypeStruct((t, d), F32), jax.ShapeDtypeStruct((t, cols), BF16),
                   jax.ShapeDtypeStruct((8, d), F32)],
        scratch_shapes=[_chunk_scratch(tm, pw), pltpu.VMEM((d, ns * cs), BF16), pltpu.SemaphoreType.DMA((ns,))],
        compiler_params=_params("arbitrary"),
    )(*pieces, *flat_parts, h, gain, dres, win)


def _outproj_fwd(h, mix_r, mix_a, wo):
    t, d = h.shape
    hw = mix_r.shape[1]
    tm = min(512, t)

    def body(h_ref, mr_ref, ma_ref, w_ref, o_ref):
        o_ref[...] = h_ref[...] + _dot(mr_ref[...], w_ref[0:hw, :]) + _dot(ma_ref[...], w_ref[hw:2 * hw, :])

    tile = pl.BlockSpec((tm, d), lambda i: (i, 0))
    half = pl.BlockSpec((tm, hw), lambda i: (i, 0))
    return pl.pallas_call(
        body, name="outproj_fwd", grid=(t // tm,),
        in_specs=[tile, half, half, pl.BlockSpec(wo.shape, lambda i: (0, 0))],
        out_specs=tile, out_shape=jax.ShapeDtypeStruct((t, d), F32),
        compiler_params=_params("arbitrary"),
    )(h, mix_r, mix_a, wo)


def _outproj_bwd(dh, wo, rides=None):
    t, d = dh.shape
    hw = wo.shape[0] // 2
    tm = min(512, t)

    def body(dh_ref, w_ref, dr_ref, da_ref):
        dhb = dh_ref[...].astype(BF16)
        dr_ref[...] = _dot(dhb, w_ref[0:hw, :], NT)
        da_ref[...] = _dot(dhb, w_ref[hw:2 * hw, :], NT)

    half = pl.BlockSpec((tm, hw), lambda i: (i, 0))
    return _pallas(
        body, rides, name="outproj_bwd", grid=(t // tm,),
        in_specs=[pl.BlockSpec((tm, d), lambda i: (i, 0)), pl.BlockSpec(wo.shape, lambda i: (0, 0))],
        out_specs=[half, half],
        out_shape=[jax.ShapeDtypeStruct((t, hw), F32), jax.ShapeDtypeStruct((t, hw), F32)],
        sem=("arbitrary",), args=[dh, wo])


def _retention_tables(t):
    pos = jnp.arange(t, dtype=F32)
    pair = (jnp.arange(RET_DIM) // 2 * 2).astype(F32)
    ang = pos[:, None] * (ROPE_BASE ** (-pair / RET_DIM))[None, :]
    c = RET_CHUNK
    log_g = jnp.log(1.0 - 2.0 ** (-5.0 - jnp.arange(RET_HEADS, dtype=F32)))
    idx = jnp.arange(c, dtype=F32)
    rel = idx[:, None] - idx[None, :]
    decay = jnp.where(rel >= 0, jnp.exp(log_g[:, None, None] * jnp.maximum(rel, 0.0)), 0.0)
    zeta = jnp.exp(log_g[:, None] * (c - 1 - idx)[None, :])
    xi = jnp.exp(log_g[:, None] * (idx + 1)[None, :])
    gc = jnp.exp(log_g * c)
    wide = lambda v: jnp.broadcast_to(v[:, :, None], (RET_HEADS, c, LANE))
    return (jnp.cos(ang), jnp.sin(ang), decay, wide(zeta), wide(xi),
            jnp.broadcast_to(gc[:, None, None], (RET_HEADS, c, LANE)))


def _rot(v):
    lane = lax.broadcasted_iota(jnp.int32, v.shape, 1)
    nxt = pltpu.roll(v, LANE - 1, 1)
    prv = pltpu.roll(v, 1, 1)
    return jnp.where(lane % 2 == 0, -nxt, prv)


def _ret_specs(tr, rev, nt):
    ti = (lambda i: nt - 1 - i) if rev else (lambda i: i)
    col = lambda blk: pl.BlockSpec((tr, RET_WIDTH), lambda i: (ti(i), blk))
    tab = pl.BlockSpec((tr, LANE), lambda i: (ti(i), 0))
    head = pl.BlockSpec((RET_HEADS, RET_CHUNK, LANE), lambda i: (0, 0, 0))
    return col, tab, head


def _ret_chunks(tr, rev=False):
    order = list(range(tr // RET_CHUNK))
    return [(pl.ds(ci * RET_CHUNK, RET_CHUNK), slice(h * RET_DIM, (h + 1) * RET_DIM), h)
            for h in range(RET_HEADS) for ci in (reversed(order) if rev else order)]


def _ret_operands(items, q_ref, k_ref, v_ref, cos_ref, sin_ref, zeta_ref):
    scale = RET_DIM ** -0.5
    qbs, kbs, vbs, kzs = [], [], [], []
    for sl, hs, h in items:
        cs, sn = cos_ref[sl, :], sin_ref[sl, :]
        q, k = q_ref[sl, hs], k_ref[sl, hs]
        kr = (k * cs + _rot(k) * sn) * scale
        qbs.append((q * cs + _rot(q) * sn).astype(BF16))
        kbs.append(kr.astype(BF16))
        vbs.append(v_ref[sl, hs].astype(BF16))
        kzs.append((kr * zeta_ref[h]).astype(BF16))
    return qbs, kbs, vbs, kzs


def _ret_states(items, state, steps, gc_ref):
    cur, befores = {}, []
    for (sl, hs, h), step in zip(items, steps):
        st = cur[h] if h in cur else state[h]
        befores.append(st)
        cur[h] = st * gc_ref[h] + step
    for h, st in cur.items():
        state[h] = st
    return befores


def _ret_fwd(u, gain, tabs):
    t = u.shape[0]
    tr = min(512, t)
    nt = t // tr
    cos, sin, decay, zeta, xi, gc = tabs

    def body(q_ref, k_ref, v_ref, gt_ref, cos_ref, sin_ref, gain_ref, dec_ref, zeta_ref, xi_ref, gc_ref,
             raw_ref, mix_ref, state):
        @pl.when(pl.program_id(0) == 0)
        def _():
            state[...] = jnp.zeros_like(state)

        items = _ret_chunks(tr)
        n = range(len(items))
        qbs, kbs, vbs, kzs = _ret_operands(items, q_ref, k_ref, v_ref, cos_ref, sin_ref, zeta_ref)
        ss = [_dot(qbs[i], kbs[i], NT) for i in n]
        kvs = [_dot(kzs[i], vbs[i], TN) for i in n]
        befores = _ret_states(items, state, kvs, gc_ref)
        intra = [_dot((ss[i] * dec_ref[items[i][2]]).astype(BF16), vbs[i]) for i in n]
        inter = [_dot(qbs[i], befores[i].astype(BF16)) for i in n]
        for i, (sl, hs, h) in enumerate(items):
            o = intra[i] + inter[i] * xi_ref[h]
            raw_ref[sl, hs] = o
            mu = jnp.mean(o, axis=-1, keepdims=True)
            var = jnp.mean(jnp.square(o - mu), axis=-1, keepdims=True)
            y = (o - mu) * lax.rsqrt(var + GN_EPS) * gain_ref[:, hs]
            gt = gt_ref[sl, hs]
            mix_ref[sl, hs] = (y * (gt * _sigmoid(gt))).astype(BF16)

    col, tab, head = _ret_specs(tr, False, nt)
    out = pl.BlockSpec((tr, RET_WIDTH), lambda i: (i, 0))
    return pl.pallas_call(
        body, name="ret_fwd", grid=(nt,),
        in_specs=[col(0), col(1), col(2), col(3), tab, tab, pl.BlockSpec((1, RET_WIDTH), lambda i: (0, 0)),
                  head, head, head, head],
        out_specs=[out, out],
        out_shape=[jax.ShapeDtypeStruct((t, RET_WIDTH), F32), jax.ShapeDtypeStruct((t, RET_WIDTH), BF16)],
        scratch_shapes=[pltpu.VMEM((RET_HEADS, RET_DIM, RET_DIM), F32)],
        compiler_params=_params("arbitrary"),
    )(u, u, u, u, cos, sin, gain, decay, zeta, xi, gc)


def _ret_bwd_q(dmix, raw, u, gain, tabs, rides=None):
    t = u.shape[0]
    tr = min(512, t)
    nt = t // tr
    cos, sin, decay, zeta, xi, gc = tabs

    def body(dm_ref, raw_ref, q_ref, k_ref, v_ref, gt_ref, cos_ref, sin_ref, gain_ref, dec_ref, zeta_ref, xi_ref, gc_ref,
             dq_ref, dgt_ref, dret_ref, dgain_ref, state):
        @pl.when(pl.program_id(0) == 0)
        def _():
            state[...] = jnp.zeros_like(state)
            dgain_ref[...] = jnp.zeros_like(dgain_ref)

        items = _ret_chunks(tr)
        n_items = range(len(items))
        qbs, kbs, vbs, kzs = _ret_operands(items, q_ref, k_ref, v_ref, cos_ref, sin_ref, zeta_ref)
        dos, dgains = [], {}
        for sl, hs, h in items:
            o = raw_ref[sl, hs]
            mu = jnp.mean(o, axis=-1, keepdims=True)
            var = jnp.mean(jnp.square(o - mu), axis=-1, keepdims=True)
            rs = lax.rsqrt(var + GN_EPS)
            n = (o - mu) * rs
            gt = gt_ref[sl, hs]
            sig = _sigmoid(gt)
            dout = dm_ref[sl, hs]
            gain_h = gain_ref[:, hs]
            dgt_ref[sl, hs] = (dout * (n * gain_h) * (sig * (1.0 + gt * (1.0 - sig)))).astype(BF16)
            dy = dout * (gt * sig)
            dgains[h] = dgains[h] + _rows8(dy * n) if h in dgains else _rows8(dy * n)
            dn = dy * gain_h
            do = rs * (dn - jnp.mean(dn, axis=-1, keepdims=True) - n * jnp.mean(dn * n, axis=-1, keepdims=True))
            dret_ref[sl, hs] = do
            dos.append(do)
        for h, dg in dgains.items():
            dgain_ref[:, h * RET_DIM:(h + 1) * RET_DIM] += dg
        dss = [_dot(dos[i].astype(BF16), vbs[i], NT) for i in n_items]
        kvs = [_dot(kzs[i], vbs[i], TN) for i in n_items]
        befores = _ret_states(items, state, kvs, gc_ref)
        intra = [_dot((dss[i] * dec_ref[items[i][2]]).astype(BF16), kbs[i]) for i in n_items]
        inter = [_dot((dos[i] * xi_ref[items[i][2]]).astype(BF16), befores[i].astype(BF16), NT) for i in n_items]
        for i, (sl, hs, h) in enumerate(items):
            dqr = intra[i] + inter[i]
            dq_ref[sl, hs] = (dqr * cos_ref[sl, :] - _rot(dqr * sin_ref[sl, :])).astype(BF16)

    col, tab, head = _ret_specs(tr, False, nt)
    out = pl.BlockSpec((tr, RET_WIDTH), lambda i: (i, 0))
    return _pallas(
        body, rides, name="ret_bwd_q", grid=(nt,),
        in_specs=[out, out, col(0), col(1), col(2), col(3), tab, tab, pl.BlockSpec((1, RET_WIDTH), lambda i: (0, 0)),
                  head, head, head, head],
        out_specs=[out, out, out, pl.BlockSpec((8, RET_WIDTH), lambda i: (0, 0))],
        out_shape=[jax.ShapeDtypeStruct((t, RET_WIDTH), BF16), jax.ShapeDtypeStruct((t, RET_WIDTH), BF16),
                   jax.ShapeDtypeStruct((t, RET_WIDTH), F32), jax.ShapeDtypeStruct((8, RET_WIDTH), F32)],
        scratch_shapes=[pltpu.VMEM((RET_HEADS, RET_DIM, RET_DIM), F32)],
        sem=("arbitrary",), args=[dmix, raw, u, u, u, u, cos, sin, gain, decay, zeta, xi, gc])


def _ret_bwd_kv(dret, u, tabs, rides=None):
    t = u.shape[0]
    tr = min(512, t)
    nt = t // tr
    cos, sin, decay, zeta, xi, gc = tabs
    scale = RET_DIM ** -0.5

    def body(do_ref, q_ref, k_ref, v_ref, cos_ref, sin_ref, dec_ref, zeta_ref, xi_ref, gc_ref, dk_ref, dv_ref, gst):
        @pl.when(pl.program_id(0) == 0)
        def _():
            gst[...] = jnp.zeros_like(gst)

        items = _ret_chunks(tr, rev=True)
        n = range(len(items))
        qbs, kbs, vbs, kzs = _ret_operands(items, q_ref, k_ref, v_ref, cos_ref, sin_ref, zeta_ref)
        dos = [do_ref[sl, hs] for sl, hs, h in items]
        dobs = [do.astype(BF16) for do in dos]
        ss = [_dot(qbs[i], kbs[i], NT) for i in n]
        dss = [_dot(dobs[i], vbs[i], NT) for i in n]
        steps = [_dot(qbs[i], (dos[i] * xi_ref[items[i][2]]).astype(BF16), TN) for i in n]
        afters = [g.astype(BF16) for g in _ret_states(items, gst, steps, gc_ref)]
        dvs = [_dot((ss[i] * dec_ref[items[i][2]]).astype(BF16), dobs[i], TN) + _dot(kzs[i], afters[i]) for i in n]
        dks = [_dot((dss[i] * dec_ref[items[i][2]]).astype(BF16), qbs[i], TN) for i in n]
        dkz = [_dot(vbs[i], afters[i], NT) for i in n]
        for i, (sl, hs, h) in enumerate(items):
            dv_ref[sl, hs] = dvs[i].astype(BF16)
            dkr = (dks[i] + dkz[i] * zeta_ref[h]) * scale
            dk_ref[sl, hs] = (dkr * cos_ref[sl, :] - _rot(dkr * sin_ref[sl, :])).astype(BF16)

    col, tab, head = _ret_specs(tr, True, nt)
    out = pl.BlockSpec((tr, RET_WIDTH), lambda i: (nt - 1 - i, 0))
    return _pallas(
        body, rides, name="ret_bwd_kv", grid=(nt,),
        in_specs=[out, col(0), col(1), col(2), tab, tab, head, head, head, head],
        out_specs=[out, out],
        out_shape=[jax.ShapeDtypeStruct((t, RET_WIDTH), BF16), jax.ShapeDtypeStruct((t, RET_WIDTH), BF16)],
        scratch_shapes=[pltpu.VMEM((RET_HEADS, RET_DIM, RET_DIM), F32)],
        sem=("arbitrary",), args=[dret, u, u, u, cos, sin, decay, zeta, xi, gc])


PAIRS = ATT_WIDTH // LANE
ATT_Q_BLK, ATT_K_BLK, ATT_V_BLK = 0, PAIRS, 2 * PAIRS
STAT_LANES = ATT_DIM // 2


ATT_STEP_ROWS = 4096


def _att_tiles(t, dil):
    sub = t // dil
    tq = min(ATT_STEP_ROWS, sub)
    return sub, tq, sub // tq, tq // ATT_BLOCK, min(dil, ATT_STEP_ROWS // tq)


def _att_in_specs(tq, qb, ti, gs):
    cur = lambda off: pl.BlockSpec((gs, tq, LANE), lambda g, p, i: (g, ti(i), off + p))
    prev = lambda off: pl.BlockSpec((gs, ATT_BLOCK, LANE), lambda g, p, i: (g, jnp.maximum(ti(i) * qb - 1, 0), off + p))
    return [cur(ATT_Q_BLK), cur(ATT_K_BLK), prev(ATT_K_BLK), cur(ATT_V_BLK), prev(ATT_V_BLK)]


def _band_mask():
    key = lax.broadcasted_iota(jnp.int32, (2 * ATT_BLOCK, 2 * ATT_BLOCK), 0)
    qry = lax.broadcasted_iota(jnp.int32, (2 * ATT_BLOCK, 2 * ATT_BLOCK), 1) % ATT_BLOCK
    dist = qry + ATT_BLOCK - key
    return (dist >= 0) & (dist <= ATT_BLOCK), key >= ATT_BLOCK


def _head0_lanes():
    return lax.broadcasted_iota(jnp.int32, (ATT_BLOCK, LANE), 1) < ATT_DIM


def _stack_heads(v, head0):
    zero = jnp.zeros((), v.dtype)
    return jnp.concatenate([jnp.where(head0, v, zero), jnp.where(head0, zero, v)], axis=0)


def _unstack_heads(v, head0):
    return jnp.where(head0, v[0:ATT_BLOCK], v[ATT_BLOCK:])


def _att_fwd(ua, dil):
    sub = ua.shape[1]
    _, tq, nq, qb, gs = _att_tiles(sub * dil, dil)

    def body(q_ref, kc_ref, kp_ref, vc_ref, vp_ref, o_ref, l_ref, kx, vx):
        tile = pl.program_id(2)
        kx[:, 0:ATT_BLOCK, :] = kp_ref[...]
        kx[:, ATT_BLOCK:, :] = kc_ref[...]
        vx[:, 0:ATT_BLOCK, :] = vp_ref[...]
        vx[:, ATT_BLOCK:, :] = vc_ref[...]
        band, cur_keys = _band_mask()
        head0 = _head0_lanes()
        items = [(r, b) for r in range(gs) for b in range(qb)]
        rows = lambda b: slice(b * ATT_BLOCK, (b + 1) * ATT_BLOCK)
        keys = lambda b: slice(b * ATT_BLOCK, (b + 2) * ATT_BLOCK)
        sts = [_dot(kx[r, keys(b), :], _stack_heads(q_ref[r, rows(b), :] * jnp.asarray(ATT_DIM ** -0.5, BF16), head0), NT)
               for r, b in items]
        pts, lses = [], []
        for (r, b), st in zip(items, sts):
            mask = band if b > 0 else band & (cur_keys | (tile > 0))
            st = jnp.where(mask, st, -1e30)
            m = jnp.max(st, axis=0, keepdims=True)
            ex = jnp.exp(st - m)
            den = jnp.sum(ex, axis=0, keepdims=True)
            pts.append((ex * (1.0 / den)).astype(BF16))
            lses.append(m + jnp.log(den))
        outs = [_dot(pt, vx[r, keys(b), :], TN) for (r, b), pt in zip(items, pts)]
        for (r, b), out, lse in zip(items, outs, lses):
            o_ref[r, rows(b), :] = _unstack_heads(out, head0).astype(BF16)
            cols = [jnp.broadcast_to(lse[:, e * ATT_BLOCK:(e + 1) * ATT_BLOCK], (ATT_BLOCK, LANE)).T for e in range(2)]
            l_ref[r, rows(b), :] = jnp.where(head0, cols[0], cols[1])

    out = pl.BlockSpec((gs, tq, LANE), lambda g, p, i: (g, i, p))
    return pl.pallas_call(
        body, name=f"att_fwd_d{dil}", grid=(dil // gs, PAIRS, nq),
        in_specs=_att_in_specs(tq, qb, lambda i: i, gs),
        out_specs=[out, out],
        out_shape=[jax.ShapeDtypeStruct((dil, sub, ATT_WIDTH), BF16), jax.ShapeDtypeStruct((dil, sub, ATT_WIDTH), F32)],
        scratch_shapes=[pltpu.VMEM((gs, tq + ATT_BLOCK, LANE), BF16)] * 2,
        compiler_params=_params("arbitrary", "arbitrary", "arbitrary"),
    )(ua, ua, ua, ua, ua)


def _regrouped_spec(tm, dil, w):
    return pl.BlockSpec((dil, tm // dil, w), lambda i: (0, i, 0))


def _att_combine(outs, lses, t):
    w = ATT_WIDTH
    tm = min(512, t)
    nb = len(outs)

    def body(*refs):
        o_refs, l_refs = refs[:nb], refs[nb:2 * nb]
        mix_ref, att_ref, lse_ref, buf = refs[2 * nb:]
        ls = [_natural_rows(r, dil, buf) for r, dil in zip(l_refs, DILATIONS)]
        m = functools.reduce(jnp.maximum, ls)
        ws = [jnp.exp(l - m) for l in ls]
        den = functools.reduce(jnp.add, ws)
        att = functools.reduce(jnp.add, [(wt / den) * _natural_rows(r, dil, buf) for wt, r, dil in zip(ws, o_refs, DILATIONS)])
        att_ref[...] = att
        mix_ref[...] = att.astype(BF16)
        lse_ref[...] = m + jnp.log(den)

    tile = pl.BlockSpec((tm, w), lambda i: (i, 0))
    regrouped = [_regrouped_spec(tm, dil, w) for dil in DILATIONS]
    return pl.pallas_call(
        body, name="att_combine", grid=(t // tm,),
        in_specs=regrouped * 2, out_specs=[tile, tile, tile],
        out_shape=[jax.ShapeDtypeStruct((t, w), BF16), jax.ShapeDtypeStruct((t, w), F32), jax.ShapeDtypeStruct((t, w), F32)],
        scratch_shapes=[_chunk_scratch(tm, w)],
        compiler_params=_params("arbitrary"),
    )(*outs, *lses)


def _att_bwd_prep(datt, att, lse):
    t, w = datt.shape
    tm = min(512, t)

    def body(da_ref, at_ref, l_ref, *rest):
        outs, dbuf, sbuf = rest[:-2], rest[-2], rest[-1]
        dav = da_ref[...]
        prod = dav * at_ref[...]
        lane = lax.broadcasted_iota(jnp.int32, (tm, LANE), 1)
        for k in range(w // LANE):
            cols = slice(k * LANE, (k + 1) * LANE)
            dbuf[k] = dav[:, cols]
            delta = jnp.concatenate(
                [jnp.broadcast_to(jnp.sum(prod[:, k * LANE + e * ATT_DIM:k * LANE + (e + 1) * ATT_DIM], axis=-1, keepdims=True),
                                  (tm, ATT_DIM)) for e in range(LANE // ATT_DIM)], axis=1)
            sbuf[k] = jnp.where(lane % ATT_DIM < STAT_LANES, l_ref[:, cols], delta)
        for k, dil in enumerate(DILATIONS):
            _regroup_store(dbuf, outs[2 * k], dil)
            _regroup_store(sbuf, outs[2 * k + 1], dil)

    tile = pl.BlockSpec((tm, w), lambda i: (i, 0))
    res = pl.pallas_call(
        body, name="att_bwd_prep", grid=(t // tm,),
        in_specs=[tile] * 3,
        out_specs=[_regrouped_spec(tm, dil, w) for dil in DILATIONS for _ in range(2)],
        out_shape=[jax.ShapeDtypeStruct((dil, t // dil, w), dt) for dil in DILATIONS for dt in (BF16, F32)],
        scratch_shapes=[_chunk_scratch(tm, w)] * 2,
        compiler_params=_params("arbitrary"),
    )(datt, att, lse)
    return [(res[2 * k], res[2 * k + 1]) for k in range(len(DILATIONS))]


def _att_bwd(ua, da, stat, dil, rides=None):
    sub = ua.shape[1]
    _, tq, nq, qb, gs = _att_tiles(sub * dil, dil)
    scale = ATT_DIM ** -0.5

    def body(q_ref, kc_ref, kp_ref, vc_ref, vp_ref, da_ref, st_ref, dq_ref, dk_ref, dv_ref, kx, vx, ck, cv):
        step = pl.program_id(2)
        tile = nq - 1 - step

        @pl.when(step == 0)
        def _():
            ck[...] = jnp.zeros_like(ck)
            cv[...] = jnp.zeros_like(cv)

        kx[:, 0:ATT_BLOCK, :] = kp_ref[...]
        kx[:, ATT_BLOCK:, :] = kc_ref[...]
        vx[:, 0:ATT_BLOCK, :] = vp_ref[...]
        vx[:, ATT_BLOCK:, :] = vc_ref[...]
        band, cur_keys = _band_mask()
        head0 = _head0_lanes()
        items = [(r, b) for r in range(gs) for b in range(qb)]
        n = range(len(items))
        rows = lambda b: slice(b * ATT_BLOCK, (b + 1) * ATT_BLOCK)
        keys = lambda b: slice(b * ATT_BLOCK, (b + 2) * ATT_BLOCK)
        qqs = [_stack_heads(q_ref[r, rows(b), :] * jnp.asarray(scale, BF16), head0) for r, b in items]
        dds = [_stack_heads(da_ref[r, rows(b), :], head0) for r, b in items]
        sts = [_dot(kx[r, keys(b), :], qqs[i], NT) for i, (r, b) in enumerate(items)]
        dpts = [_dot(vx[r, keys(b), :], dds[i], NT) for i, (r, b) in enumerate(items)]
        pts, dsts = [], []
        for i, (r, b) in enumerate(items):
            mask = band if b > 0 else band & (cur_keys | (tile > 0))
            stat = st_ref[r, rows(b), :].T
            row = lambda k: jnp.concatenate([stat[e * ATT_DIM + k:e * ATT_DIM + k + 1, :] for e in range(2)], axis=1)
            pt = jnp.where(mask, jnp.exp(sts[i] - row(0)), 0.0)
            dsts.append((pt * (dpts[i] - row(STAT_LANES))).astype(BF16))
            pts.append(pt.astype(BF16))
        dqs = [_dot(dsts[i], kx[r, keys(b), :], TN) for i, (r, b) in enumerate(items)]
        dkbs = [_dot(dsts[i], qqs[i]) for i in n]
        dvbs = [_dot(pts[i], dds[i]) for i in n]
        for i, (r, b) in enumerate(items):
            dq_ref[r, rows(b), :] = (_unstack_heads(dqs[i], head0) * scale).astype(BF16)
            if b > 0:
                dk_ref[r, rows(b - 1), :] = (dkbs[i - 1][ATT_BLOCK:] + dkbs[i][0:ATT_BLOCK]).astype(BF16)
                dv_ref[r, rows(b - 1), :] = (dvbs[i - 1][ATT_BLOCK:] + dvbs[i][0:ATT_BLOCK]).astype(BF16)
        for r in range(gs):
            first, last = r * qb, r * qb + qb - 1
            dk_ref[r, rows(qb - 1), :] = (dkbs[last][ATT_BLOCK:] + ck[r]).astype(BF16)
            dv_ref[r, rows(qb - 1), :] = (dvbs[last][ATT_BLOCK:] + cv[r]).astype(BF16)
            ck[r] = dkbs[first][0:ATT_BLOCK]
            cv[r] = dvbs[first][0:ATT_BLOCK]

    ti = lambda i: nq - 1 - i
    out = pl.BlockSpec((gs, tq, LANE), lambda g, p, i: (g, ti(i), p))
    shape = jax.ShapeDtypeStruct((dil, sub, ATT_WIDTH), BF16)
    return _pallas(
        body, rides, name=f"att_bwd_d{dil}", grid=(dil // gs, PAIRS, nq),
        in_specs=_att_in_specs(tq, qb, ti, gs) + [out, out],
        out_specs=[out, out, out], out_shape=[shape] * 3,
        scratch_shapes=[pltpu.VMEM((gs, tq + ATT_BLOCK, LANE), BF16)] * 2 + [pltpu.VMEM((gs, ATT_BLOCK, LANE), F32)] * 2,
        sem=("arbitrary", "arbitrary", "arbitrary"), args=[ua, ua, ua, ua, ua, da, stat])


class _Reduction:
    def __init__(self, place, names, grads):
        self.place, self.names, self.grads = place, names, grads

    def pair(self):
        return _pair_ride(self.grads)

    def chips(self, got):
        self.got = got
        return _chip_ride([_pair_sum(self.place, g, r, f"pair_sum_{n}") for g, r, n in zip(self.grads, got, self.names)])

    def halves(self, others):
        return [_chip_sum(self.place, g, r, o, f"chip_sum_{n}")
                for g, r, o, n in zip(self.grads, self.got, others, self.names)]


def _step(x, target, gains, w, place=None):
    t = x.shape[0]
    ex = place is not None
    g_ffn1, g_mix, g_ret, g_ffn2, g_fin = gains
    w = list(w)
    tabs = _retention_tables(t)
    red = lambda names, grads: _Reduction(place, names, grads) if ex else None
    ride = lambda r: [r] if ex else None

    if ex:
        w[0:3] = _run(_gather_ride(w[0:3]), "gather_ffn1_weights")
    (h1, xn1, *hid1, act1), rest = _ffn_fwd(x, g_ffn1, *w[0:3], "ffn1_fwd", ride(_gather_ride(w[3:])) if ex else None)
    if ex:
        w[3:] = rest[0]
    wg1, wu1, wd1, win, wo, wg2, wu2, wd2 = w
    wo2 = wo.reshape(wo.shape[0] * wo.shape[1], wo.shape[2])
    xnm, u, *uas = _inproj_fwd(h1, g_mix, win)
    raw, mix_r = _ret_fwd(u, g_ret, tabs)
    branches = [_att_fwd(ua, dil) for ua, dil in zip(uas, DILATIONS)]
    mix_a, att, lse = _att_combine([b[0] for b in branches], [b[1] for b in branches], t)
    h2 = _outproj_fwd(h1, mix_r, mix_a, wo2)
    (dh3, xn2, *hid2, act2, loss_p, dg_fin), _ = _ffn_fwd(h2, g_ffn2, wg2, wu2, wd2, "ffn2_fwd", head=(g_fin, target))

    (dwd2,), _ = _ffn_wgrad_down(act2, dh3, "ffn2_wgrad_down")
    dwd2 = dwd2.reshape(wd2.shape)
    r_d2 = red(["ffn2_w_down"], [dwd2])
    (dh2, dga2, dua2, dg_ffn2), e = _ffn_bwd_data(dh3, h2, g_ffn2, *hid2, wg2, wu2, wd2, "ffn2_bwd",
                                                  ex and [r_d2.pair()])
    (dwg2, dwu2), e = _ffn_wgrad_gu(xn2, [dga2, dua2], "ffn2_wgrad_gu", ex and [r_d2.chips(e[0])])
    dwg2, dwu2 = dwg2.reshape(wg2.shape), dwu2.reshape(wu2.shape)
    r_gu2 = red(["ffn2_w_gate", "ffn2_w_up"], [dwg2, dwu2])
    (dmix_r, dmix_a), e = _outproj_bwd(dh2, wo2, ex and [r_gu2.pair(), _finish_ride(r_d2.halves(e[0]))])
    if ex:
        got_gu2, (dwd2,) = e
    hw = RET_WIDTH // (wo.shape[1])
    dwo = jnp.concatenate([_tn_matmul(mix_r, dh2, dh2.shape[1], "wo_grad_r").reshape(hw, wo.shape[1], wo.shape[2]),
                           _tn_matmul(mix_a, dh2, dh2.shape[1], "wo_grad_a").reshape(hw, wo.shape[1], wo.shape[2])])
    r_wo = red(["w_out"], [dwo])
    (dq_r, dgt_r, dret, dg_ret), e = _ret_bwd_q(dmix_r, raw, u, g_ret, tabs, ex and [r_gu2.chips(got_gu2)])
    (dk_r, dv_r), e = _ret_bwd_kv(dret, u, tabs, ex and [r_wo.pair(), _finish_ride(r_gu2.halves(e[0]))])
    if ex:
        got_wo, (dwg2, dwu2) = e
    prep = _att_bwd_prep(dmix_a, att, lse)
    p1, e = _att_bwd(uas[0], *prep[0], DILATIONS[0], ex and [r_wo.chips(got_wo)])
    p4, e = _att_bwd(uas[1], *prep[1], DILATIONS[1], ex and [_finish_ride(r_wo.halves(e[0]))])
    if ex:
        (dwo,), = e
    p16, _ = _att_bwd(uas[2], *prep[2], DILATIONS[2])
    dh1, du, dg_mix = _inproj_bwd([dq_r, dk_r, dv_r, dgt_r], [p1, p4, p16], h1, g_mix, dh2, win)
    dwin = _tn_matmul(xnm, du, win.shape[2], "win_grad")
    r_in = red(["w_in"], [dwin])
    (dwd1,), e = _ffn_wgrad_down(act1, dh1, "ffn1_wgrad_down", ex and [r_in.pair()])
    dwd1 = dwd1.reshape(wd1.shape)
    r_d1 = red(["ffn1_w_down"], [dwd1])
    got_in = e
    (dx, dga1, dua1, dg_ffn1), _ = _ffn_bwd_data(dh1, x, g_ffn1, *hid1, wg1, wu1, wd1, "ffn1_bwd")
    (dwg1,), e = _ffn_wgrad_gu(xn1, [dga1], "ffn1_wgrad_gate", ex and [r_in.chips(got_in[0]), r_d1.pair()])
    dwg1 = dwg1.reshape(wg1.shape)
    if ex:
        oth_in, got_d1 = e
        r_g1 = red(["ffn1_w_gate"], [dwg1])
        got_g1 = _run(r_g1.pair(), "pair_exchange_ffn1_gate")
    (dwu1,), e = _ffn_wgrad_gu(xn1, [dua1], "ffn1_wgrad_up",
                               ex and [_finish_ride(r_in.halves(oth_in)), r_d1.chips(got_d1), r_g1.chips(got_g1)])
    dwu1 = dwu1.reshape(wu1.shape)
    gain_parts = [dg_ffn1, dg_mix, dg_ret, dg_ffn2, dg_fin]
    if not ex:
        return loss_p, dx, [dwg1, dwu1, dwd1, dwin, dwo, dwg2, dwu2, dwd2], gain_parts
    (dwin,), oth_d1, oth_g1 = e
    r_u1 = red(["ffn1_w_up"], [dwu1])
    got_u1 = _run(r_u1.pair(), "pair_exchange_ffn1_up")
    oth_u1 = _run(r_u1.chips(got_u1), "chip_exchange_ffn1_up")
    last = r_g1.halves(oth_g1) + r_u1.halves(oth_u1) + r_d1.halves(oth_d1)
    dwg1, dwu1, dwd1, gall = _run(_finish_ride(last, _pack_gains(gain_parts, x.shape[1])), "finish_exchange_ffn1")
    return loss_p, dx, [dwg1, dwu1, dwd1, dwin, dwo, dwg2, dwu2, dwd2], gall


N_DEV = 8
GAIN_ROWS = 8


def _place():
    x, y, c = lax.axis_index("x"), lax.axis_index("y"), lax.axis_index("c")
    chips = [(1 - x, y), (x, 1 - y), (1 - x, 1 - y)]
    return x, y, c, chips


ROW_QUARTERS = 4


def _place_shards(place, ws):
    n = len(ws)

    def body(place_ref, *refs):
        for w_ref, o_ref in zip(refs[:n], refs[n:]):
            o_ref[...] = w_ref[...].astype(BF16)

    quarter = lambda w: (w.shape[0] // ROW_QUARTERS, w.shape[1])
    return pl.pallas_call(
        body, name="place_shards",
        grid_spec=pltpu.PrefetchScalarGridSpec(
            num_scalar_prefetch=1, grid=(ROW_QUARTERS,),
            in_specs=[pl.BlockSpec(quarter(w), lambda i, pr: (i, 0)) for w in ws],
            out_specs=[pl.BlockSpec((None,) + quarter(w), lambda i, pr: (pr[0], i, 0)) for w in ws]),
        out_shape=[jax.ShapeDtypeStruct((N_SHARD,) + w.shape, BF16) for w in ws],
        compiler_params=_params("arbitrary"),
    )(place, *ws)


def _gather_ride(bufs):
    na = len(bufs)
    sent = [("me", "half", "x"), ("me", "half", "y"), ("x", "second quarter", "y"), ("y", "first quarter", "x")]
    landed = [("x", "half"), ("y", "half"), ("d", "second quarter"), ("d", "first quarter")]

    def legs(outs, sems):
        send_sem, recv_sem, fsend_sem, frecv_sem = sems
        x, y, c, _ = _place()
        slot = {"me": 2 * x + y, "x": 2 * (1 - x) + y, "y": 2 * x + (1 - y), "d": 2 * (1 - x) + (1 - y)}
        peer = {"x": (1 - x, y, c), "y": (x, 1 - y, c)}

        def rows(a, which, piece, core):
            hr = outs[a].shape[1] // 2
            lo, n = {"half": (0, hr), "first quarter": (0, hr // 2), "second quarter": (hr // 2, hr // 2)}[piece]
            return outs[a].at[slot[which], pl.ds(core * hr + lo, n)]

        def ici(a, k):
            which, piece, to = sent[k]
            ref = rows(a, which, piece, c)
            return pltpu.make_async_remote_copy(src_ref=ref, dst_ref=ref, send_sem=send_sem.at[a, k],
                                                recv_sem=recv_sem.at[a, k], device_id=peer[to], device_id_type=MESH)

        def arrival(a, k):
            ref = rows(a, *landed[k], c)
            return pltpu.make_async_remote_copy(src_ref=ref, dst_ref=ref, send_sem=send_sem.at[a, k],
                                                recv_sem=recv_sem.at[a, k], device_id=peer["x"], device_id_type=MESH)

        def d2d(a, k, core):
            ref = rows(a, *landed[k], core)
            return pltpu.make_async_remote_copy(src_ref=ref, dst_ref=ref, send_sem=fsend_sem.at[a, k],
                                                recv_sem=frecv_sem.at[a, k], device_id=(x, y, 1 - c), device_id_type=MESH)

        return c, ici, arrival, d2d

    def start(ins, outs, sems):
        _, ici, _, _ = legs(outs, sems)
        for a in range(na):
            ici(a, 0).start()
            ici(a, 1).start()

    def middle(ins, outs, sems):
        c, ici, arrival, d2d = legs(outs, sems)
        for a in range(na):
            for k in (0, 1):
                arrival(a, k).wait_recv()
                ici(a, 2 + k).start()
                d2d(a, k, c).start()

    def finish(ins, outs, sems):
        c, ici, arrival, d2d = legs(outs, sems)
        for a in range(na):
            for k in (2, 3):
                arrival(a, k).wait_recv()
                d2d(a, k, c).start()
        for a in range(na):
            for k in range(len(landed)):
                d2d(a, k, 1 - c).wait_recv()
        for a in range(na):
            for k in range(len(sent)):
                ici(a, k).wait_send()
                d2d(a, k, c).wait_send()

    return _Ride(bufs, [jax.ShapeDtypeStruct(b.shape, b.dtype) for b in bufs], [pltpu.SemaphoreType.DMA((na, 4))] * 4,
                 start, finish, {a: a for a in range(na)}, middle)


def _pair_ride(grads):
    na = len(grads)

    def copies(ins, outs, sems):
        send_sem, recv_sem = sems
        x, y, c, _ = _place()
        res = []
        for a in range(na):
            hr = ins[a].shape[1] // 2
            res.append(pltpu.make_async_remote_copy(
                src_ref=ins[a].at[:, pl.ds((1 - c) * hr, hr)], dst_ref=outs[a],
                send_sem=send_sem.at[a], recv_sem=recv_sem.at[a], device_id=(x, y, 1 - c), device_id_type=MESH))
        return res

    def start(ins, outs, sems):
        for cp in copies(ins, outs, sems):
            cp.start()

    def finish(ins, outs, sems):
        for cp in copies(ins, outs, sems):
            cp.wait()

    return _Ride(grads, [jax.ShapeDtypeStruct((g.shape[0], g.shape[1] // 2, g.shape[2]), g.dtype) for g in grads],
                 [pltpu.SemaphoreType.DMA((na,))] * 2, start, finish)


def _chip_ride(sums):
    na = len(sums)

    def copies(ins, outs, sems):
        send_sem, recv_sem = sems
        x, y, c, chips = _place()
        res = []
        for a in range(na):
            for j, (px, py) in enumerate(chips):
                res.append(pltpu.make_async_remote_copy(
                    src_ref=ins[a].at[2 * px + py], dst_ref=outs[a].at[j],
                    send_sem=send_sem.at[a, j], recv_sem=recv_sem.at[a, j], device_id=(px, py, c), device_id_type=MESH))
        return res

    def start(ins, outs, sems):
        for cp in copies(ins, outs, sems):
            cp.start()

    def finish(ins, outs, sems):
        for cp in copies(ins, outs, sems):
            cp.wait()

    return _Ride(sums, [jax.ShapeDtypeStruct((3,) + s.shape[1:], s.dtype) for s in sums],
                 [pltpu.SemaphoreType.DMA((na, 3))] * 2, start, finish)


def _finish_ride(grads, gpack=None):
    na = len(grads)

    def halves(outs, sems, which):
        x, y, c, _ = _place()
        res = []
        for a in range(na):
            hr = outs[a].shape[0] // 2
            rows = outs[a].at[pl.ds((c if which == "mine" else 1 - c) * hr, hr)]
            res.append(pltpu.make_async_remote_copy(
                src_ref=rows, dst_ref=rows, send_sem=sems[0].at[a], recv_sem=sems[1].at[a],
                device_id=(x, y, 1 - c), device_id_type=MESH))
        return res

    def gains(ins, outs, sems):
        x, y, c, _ = _place()
        dev = 4 * x + 2 * y + c
        g_in, g_out = ins[na], outs[na]
        own = pltpu.make_async_copy(g_in, g_out.at[dev], sems[2])
        sends, lands = [], []
        for k in range(N_DEV - 1):
            bx, by, bc = (k + 1) // 4, ((k + 1) // 2) % 2, (k + 1) % 2
            peer = (jnp.bitwise_xor(x, bx), jnp.bitwise_xor(y, by), jnp.bitwise_xor(c, bc))
            sends.append(pltpu.make_async_remote_copy(
                src_ref=g_in, dst_ref=g_out.at[dev], send_sem=sems[3].at[k], recv_sem=sems[4].at[k],
                device_id=peer, device_id_type=MESH))
            slot = g_out.at[jnp.bitwise_xor(dev, k + 1)]
            lands.append(pltpu.make_async_remote_copy(
                src_ref=slot, dst_ref=slot, send_sem=sems[3].at[k], recv_sem=sems[4].at[k],
                device_id=peer, device_id_type=MESH))
        return own, sends, lands

    def start(ins, outs, sems):
        for cp in halves(outs, sems, "mine"):
            cp.start()
        if gpack is not None:
            own, sends, _ = gains(ins, outs, sems)
            own.start()
            for cp in sends:
                cp.start()

    def finish(ins, outs, sems):
        for cp in halves(outs, sems, "sibling's"):
            cp.wait_recv()
        if gpack is not None:
            own, sends, lands = gains(ins, outs, sems)
            for cp in lands:
                cp.wait_recv()
            for cp in sends:
                cp.wait_send()
            own.wait()
        for cp in halves(outs, sems, "mine"):
            cp.wait_send()

    shapes = [jax.ShapeDtypeStruct(g.shape, g.dtype) for g in grads]
    sems = [pltpu.SemaphoreType.DMA((na,))] * 2
    if gpack is None:
        return _Ride(grads, shapes, sems, start, finish, {a: a for a in range(na)})
    return _Ride(list(grads) + [gpack], shapes + [jax.ShapeDtypeStruct((N_DEV,) + gpack.shape, gpack.dtype)],
                 sems + [pltpu.SemaphoreType.DMA, pltpu.SemaphoreType.DMA((N_DEV - 1,)), pltpu.SemaphoreType.DMA((N_DEV - 1,))],
                 start, finish, {a: a for a in range(na)})


def _pair_sum(place, grad, got, name):
    ns, r, cols = grad.shape
    hr = r // 2

    def body(place_ref, g_ref, r_ref, o_ref):
        o_ref[...] = (g_ref[...] + r_ref[...]).astype(BF16)

    return pl.pallas_call(
        body, name=name,
        grid_spec=pltpu.PrefetchScalarGridSpec(
            num_scalar_prefetch=1, grid=(ns,),
            in_specs=[pl.BlockSpec((None, hr, cols), lambda s, pr: (s, pr[1], 0)),
                      pl.BlockSpec((None, hr, cols), lambda s, pr: (s, 0, 0))],
            out_specs=pl.BlockSpec((None, hr, cols), lambda s, pr: (s, 0, 0))),
        out_shape=jax.ShapeDtypeStruct((ns, hr, cols), BF16),
        compiler_params=_params("arbitrary"),
    )(place, grad, got)


def _chip_sum(place, grad, got, others, name):
    ns, r, cols = grad.shape
    hr = r // 2
    nb = 2
    tr = hr // nb

    def body(place_ref, g_ref, r_ref, o3_ref, o_ref):
        acc = g_ref[...] + r_ref[...]
        for j in range(3):
            acc = acc + o3_ref[j].astype(F32)
        o_ref[...] = acc

    return pl.pallas_call(
        body, name=name,
        grid_spec=pltpu.PrefetchScalarGridSpec(
            num_scalar_prefetch=1, grid=(nb,),
            in_specs=[pl.BlockSpec((None, tr, cols), lambda i, pr: (pr[0], pr[1] * nb + i, 0)),
                      pl.BlockSpec((None, tr, cols), lambda i, pr: (pr[0], i, 0)),
                      pl.BlockSpec((3, tr, cols), lambda i, pr: (0, i, 0))],
            out_specs=pl.BlockSpec((tr, cols), lambda i, pr: (pr[1] * nb + i, 0))),
        out_shape=jax.ShapeDtypeStruct((r, cols), F32),
        compiler_params=_params("arbitrary"),
    )(place, grad, got, others)


def _pack_gains(parts, d):
    def body(*refs):
        ins, o_ref = refs[:-1], refs[-1]
        o_ref[...] = jnp.zeros_like(o_ref)
        for k, r in enumerate(ins):
            o_ref[k:k + 1, 0:r.shape[1]] = jnp.sum(r[...], axis=0, keepdims=True)

    return pl.pallas_call(
        body, name="pack_gains", out_shape=jax.ShapeDtypeStruct((GAIN_ROWS, d), F32),
    )(*parts)


def _adamw_math(w, g, m, v):
    m = ADAM_B1 * m + (1.0 - ADAM_B1) * g
    v = ADAM_B2 * v + (1.0 - ADAM_B2) * jnp.square(g)
    m_hat = m / (1.0 - ADAM_B1 ** ADAM_STEP)
    v_hat = v / (1.0 - ADAM_B2 ** ADAM_STEP)
    return -ADAM_LR * (m_hat / (jnp.sqrt(v_hat) + ADAM_EPS) + ADAM_WD * w), m, v


def _adamw(ws, gs, ms, vs):
    n = len(ws)

    def body(*refs):
        ins, outs = refs[:4 * n], refs[4 * n:]
        for k in range(n):
            w_ref, g_ref, m_ref, v_ref = ins[4 * k:4 * k + 4]
            go_ref, d_ref, nm_ref, nv_ref = outs[4 * k:4 * k + 4]
            g = g_ref[...]
            go_ref[...] = g
            d_ref[...], nm_ref[...], nv_ref[...] = _adamw_math(w_ref[...], g, m_ref[...], v_ref[...])

    parts = 2 * ROW_QUARTERS
    tile = lambda w: pl.BlockSpec((w.shape[0] // parts, w.shape[1]), lambda i: (i, 0))
    res = pl.pallas_call(
        body, name="adamw_shards", grid=(parts,),
        in_specs=[tile(w) for w in ws for _ in range(4)], out_specs=[tile(w) for w in ws for _ in range(4)],
        out_shape=[jax.ShapeDtypeStruct(w.shape, F32) for w in ws for _ in range(4)],
        compiler_params=_params("arbitrary"),
    )(*[a for quad in zip(ws, gs, ms, vs) for a in quad])
    return [res[4 * k:4 * k + 4] for k in range(n)]


def _adamw_gain(gall, row, w, m, v, name):
    n = w.shape[1]

    def body(ga_ref, w_ref, m_ref, v_ref, g_ref, d_ref, nm_ref, nv_ref):
        g = ga_ref[0, row:row + 1, 0:n]
        for k in range(1, N_DEV):
            g = g + ga_ref[k, row:row + 1, 0:n]
        g_ref[...] = g
        d_ref[...], nm_ref[...], nv_ref[...] = _adamw_math(w_ref[...], g, m_ref[...], v_ref[...])

    return pl.pallas_call(
        body, name=name, out_shape=[jax.ShapeDtypeStruct((1, n), F32)] * 4,
    )(gall, w, m, v)


def kernel(x, norm_ffn1, ffn1_w_gate, ffn1_w_up, ffn1_w_down, norm_mix, w_in, ret_norm_gain, w_out, norm_ffn2, ffn2_w_gate, ffn2_w_up, ffn2_w_down, norm_final, loss_target, m_norm_ffn1, m_ffn1_w_gate, m_ffn1_w_up, m_ffn1_w_down, m_norm_mix, m_w_in, m_ret_norm_gain, m_w_out, m_norm_ffn2, m_ffn2_w_gate, m_ffn2_w_up, m_ffn2_w_down, m_norm_final, v_norm_ffn1, v_ffn1_w_gate, v_ffn1_w_up, v_ffn1_w_down, v_norm_mix, v_w_in, v_ret_norm_gain, v_w_out, v_norm_ffn2, v_ffn2_w_gate, v_ffn2_w_up, v_ffn2_w_down, v_norm_final):
    d = x.shape[-1]
    mats = [ffn1_w_gate, ffn1_w_up, ffn1_w_down, w_in, w_out, ffn2_w_gate, ffn2_w_up, ffn2_w_down]
    mats_m = [m_ffn1_w_gate, m_ffn1_w_up, m_ffn1_w_down, m_w_in, m_w_out, m_ffn2_w_gate, m_ffn2_w_up, m_ffn2_w_down]
    mats_v = [v_ffn1_w_gate, v_ffn1_w_up, v_ffn1_w_down, v_w_in, v_w_out, v_ffn2_w_gate, v_ffn2_w_up, v_ffn2_w_down]
    mat_names = ["ffn1_w_gate", "ffn1_w_up", "ffn1_w_down", "w_in", "w_out", "ffn2_w_gate", "ffn2_w_up", "ffn2_w_down"]
    gains = [norm_ffn1, norm_mix, ret_norm_gain, norm_ffn2, norm_final.reshape(1, d)]
    gains_m = [m_norm_ffn1, m_norm_mix, m_ret_norm_gain, m_norm_ffn2, m_norm_final.reshape(1, d)]
    gains_v = [v_norm_ffn1, v_norm_mix, v_ret_norm_gain, v_norm_ffn2, v_norm_final.reshape(1, d)]
    gain_names = ["norm_ffn1", "norm_mix", "ret_norm_gain", "norm_ffn2", "norm_final"]

    turned = lambda n: n.endswith(("w_gate", "w_up"))
    local = lambda a, n: jnp.swapaxes(a, 1, 2)[0] if turned(n) else a[0]
    back = lambda a, n: jnp.swapaxes(a[None], 1, 2) if turned(n) else a[None]
    shards = [local(w, n) for w, n in zip(mats, mat_names)]
    place = jnp.stack([2 * lax.axis_index("x") + lax.axis_index("y"), lax.axis_index("c")]).astype(jnp.int32)
    placed = _place_shards(place, shards)
    loss_p, dx, shard_grads, gall = _step(x[0], loss_target[0], gains, placed, place)

    out_g, out_d, out_m, out_v = {}, {}, {}, {}
    updates = _adamw(shards, shard_grads, [local(m, n) for m, n in zip(mats_m, mat_names)],
                     [local(v, n) for v, n in zip(mats_v, mat_names)])
    for n, quad in zip(mat_names, updates):
        out_g[n], out_d[n], out_m[n], out_v[n] = [back(a, n) for a in quad]
    for row, (n, w, m, v) in enumerate(zip(gain_names, gains, gains_m, gains_v)):
        res = _adamw_gain(gall, row, w, m, v, f"adamw_{n}")
        shape = (d,) if n == "norm_final" else w.shape
        out_g[n], out_d[n], out_m[n], out_v[n] = [r.reshape(shape) for r in res]

    loss = lax.psum(jnp.sum(loss_p), ("x", "y", "c"))
    order = ["norm_ffn1", "ffn1_w_gate", "ffn1_w_up", "ffn1_w_down", "norm_mix", "w_in", "ret_norm_gain", "w_out",
             "norm_ffn2", "ffn2_w_gate", "ffn2_w_up", "ffn2_w_down", "norm_final"]
    return (loss, dx[None], *[out_g[n] for n in order], *[out_d[n] for n in order],
            *[out_m[n] for n in order], *[out_v[n] for n in order])
```

```python
import functools

import jax
import jax.numpy as jnp
from jax import lax
from jax.experimental import pallas as pl
from jax.experimental.pallas import tpu as pltpu

F32 = jnp.float32
BF16 = jnp.bfloat16
MESH = pl.DeviceIdType.MESH

NORM_EPS = 1e-6
GN_EPS = 1e-6
ROPE_BASE = 10000.0
RET_HEADS = 4
RET_DIM = 128
RET_WIDTH = 512
RET_CHUNK = 128
ATT_DIM = 64
ATT_WIDTH = 512
ATT_BLOCK = 128
DILATIONS = (1, 4, 16)
LANE = 128
N_SHARD = 4
ADAM_LR, ADAM_B1, ADAM_B2, ADAM_EPS, ADAM_WD, ADAM_STEP = 0.001, 0.9, 0.999, 1e-08, 0.01, 10

V7X_VMEM_BYTES = 64 * 1024 * 1024
VMEM_LIMIT = V7X_VMEM_BYTES - 8 * 1024 * 1024

NT = (((1,), (1,)), ((), ()))
TN = (((0,), (0,)), ((), ()))


def _params(*sem):
    return pltpu.CompilerParams(dimension_semantics=sem, vmem_limit_bytes=VMEM_LIMIT)


def _dot(a, b, dims=None):
    if dims is None:
        return jnp.dot(a, b, preferred_element_type=F32)
    return lax.dot_general(a, b, dims, preferred_element_type=F32)


def _sigmoid(x):
    return 1.0 / (1.0 + jnp.exp(-x))


def _load_weights(pairs, sems):
    copies = [pltpu.make_async_copy(src, dst, sems.at[k]) for k, (src, dst) in enumerate(pairs)]
    for cp in copies:
        cp.start()
    for cp in copies:
        cp.wait()


def _rows8(v):
    r, c = v.shape
    return v.reshape(r // 8, 8, c).sum(axis=0)


class _Ride:
    def __init__(self, inputs, out_shapes, sems, start, finish, aliases=None, middle=None):
        self.inputs, self.out_shapes, self.sems = list(inputs), list(out_shapes), list(sems)
        self.start, self.middle, self.finish, self.aliases = start, middle, finish, dict(aliases or {})


def _pallas(body, rides, *, name, in_specs, out_specs, out_shape, args, grid=(), scratch_shapes=(), sem=()):
    rides = [r for r in (rides or []) if r is not None]
    n_in, n_out, n_scr = len(args), len(out_shape), len(scratch_shapes)
    hbm = pl.BlockSpec(memory_space=pl.ANY)
    r_in = [a for r in rides for a in r.inputs]
    r_out = [s for r in rides for s in r.out_shapes]
    r_sem = [s for r in rides for s in r.sems]
    aliases, spans, ki, ko, ks = {}, [], 0, 0, 0
    for r in rides:
        aliases.update({n_in + ki + i: n_out + ko + o for i, o in r.aliases.items()})
        spans.append((ki, ko, ks))
        ki, ko, ks = ki + len(r.inputs), ko + len(r.out_shapes), ks + len(r.sems)

    def wrapped(*refs):
        ins, rin = refs[:n_in], refs[n_in:n_in + len(r_in)]
        o0 = n_in + len(r_in)
        outs, rout = refs[o0:o0 + n_out], refs[o0 + n_out:o0 + n_out + len(r_out)]
        s0 = o0 + n_out + len(r_out)
        scr, rsem = refs[s0:s0 + n_scr], refs[s0 + n_scr:]
        part = lambda r, k: (rin[spans[k][0]:spans[k][0] + len(r.inputs)], rout[spans[k][1]:spans[k][1] + len(r.out_shapes)],
                             rsem[spans[k][2]:spans[k][2] + len(r.sems)])
        first = functools.reduce(jnp.logical_and, [pl.program_id(k) == 0 for k in range(len(grid))], True)
        last = functools.reduce(jnp.logical_and, [pl.program_id(k) == grid[k] - 1 for k in range(len(grid))], True)
        if rides:
            @pl.when(first)
            def _():
                for k, r in enumerate(rides):
                    r.start(*part(r, k))

        if any(r.middle for r in rides):
            halfway = functools.reduce(jnp.logical_and, [pl.program_id(k) == 0 for k in range(1, len(grid))],
                                       pl.program_id(0) == grid[0] // 2)

            @pl.when(halfway)
            def _():
                for k, r in enumerate(rides):
                    if r.middle:
                        r.middle(*part(r, k))

        body(*ins, *outs, *scr)
        if rides:
            @pl.when(last)
            def _():
                for k, r in enumerate(rides):
                    r.finish(*part(r, k))

    res = pl.pallas_call(
        wrapped, name=name, grid=grid,
        in_specs=list(in_specs) + [hbm] * len(r_in), out_specs=list(out_specs) + [hbm] * len(r_out),
        out_shape=list(out_shape) + r_out, input_output_aliases=aliases,
        scratch_shapes=list(scratch_shapes) + r_sem,
        compiler_params=pltpu.CompilerParams(dimension_semantics=sem, vmem_limit_bytes=VMEM_LIMIT) if grid else None,
    )(*args, *r_in)
    extras = [list(res[n_out + ko:n_out + ko + len(r.out_shapes)]) for r, (_, ko, _) in zip(rides, spans)]
    return list(res[:n_out]), extras


def _run(ride, name):
    def body(*refs):
        n_in, n_out = len(ride.inputs), len(ride.out_shapes)
        parts = refs[:n_in], refs[n_in:n_in + n_out], refs[n_in + n_out:]
        ride.start(*parts)
        if ride.middle:
            ride.middle(*parts)
        ride.finish(*parts)

    hbm = pl.BlockSpec(memory_space=pl.ANY)
    return list(pl.pallas_call(
        body, name=name, in_specs=[hbm] * len(ride.inputs), out_specs=[hbm] * len(ride.out_shapes),
        out_shape=ride.out_shapes, input_output_aliases=ride.aliases, scratch_shapes=ride.sems,
    )(*ride.inputs))


def _loss_head(hv, gain_ref, tg_ref, loss_ref, dgain_ref):
    d = hv.shape[1]
    r = lax.rsqrt(jnp.mean(hv * hv, axis=-1, keepdims=True) + NORM_EPS)
    xh = hv * r
    err = xh * gain_ref[...] - tg_ref[...]
    sq = _rows8(jnp.square(err))
    loss_ref[...] += 0.5 * functools.reduce(jnp.add, [sq[:, k * LANE:(k + 1) * LANE] for k in range(d // LANE)]) / d
    dy = err / d
    dgain_ref[...] += _rows8(dy * xh)
    dxh = dy * gain_ref[...]
    return r * (dxh - xh * jnp.mean(dxh * xh, axis=-1, keepdims=True))


V7X_MXU_TILE = 256
FFN_CHUNK_TILES = 3


def _hidden_chunks(f):
    step = FFN_CHUNK_TILES * V7X_MXU_TILE
    return [slice(s, min(s + step, f)) for s in range(0, f, step)]


def _flat(w):
    return w.reshape(w.shape[0] * w.shape[1], w.shape[2])


def _ffn_fwd(x, gain, wg, wu, wd, name, rides=None, head=None):
    t, d = x.shape
    wg, wu, wd = _flat(wg), _flat(wu), _flat(wd)
    f = wg.shape[0]
    tm = min(512, t)
    nh = 0 if head is None else 2

    def body(*refs):
        x_ref, gain_ref = refs[:2]
        wg_hbm, wu_hbm, wd_hbm, h_ref, xn_ref, g_ref, u_ref, a_ref = refs[2 + nh:10 + nh]
        sums = refs[10 + nh:12 + nh]
        wg_v, wu_v, wd_v, sems = refs[-4:]

        @pl.when(pl.program_id(0) == 0)
        def _():
            _load_weights([(wg_hbm, wg_v), (wu_hbm, wu_v), (wd_hbm, wd_v)], sems)
            if head is not None:
                for s_ref in sums:
                    s_ref[...] = jnp.zeros_like(s_ref)

        xv = x_ref[...]
        r = lax.rsqrt(jnp.mean(xv * xv, axis=-1, keepdims=True) + NORM_EPS)
        xn = (xv * r * gain_ref[...]).astype(BF16)
        xn_ref[...] = xn
        acc = jnp.zeros((tm, d), F32)
        for c in _hidden_chunks(f):
            g = _dot(xn, wg_v[c, :], NT)
            u = _dot(xn, wu_v[c, :], NT)
            g_ref[:, c] = g.astype(BF16)
            u_ref[:, c] = u.astype(BF16)
            a = (g * _sigmoid(g) * u).astype(BF16)
            a_ref[:, c] = a
            acc = acc + _dot(a, wd_v[c, :])
        hv = xv + 0.5 * acc
        h_ref[...] = hv if head is None else _loss_head(hv, refs[2], refs[3], *sums)

    hbm = pl.BlockSpec(memory_space=pl.ANY)
    hid = pl.BlockSpec((tm, f), lambda i: (i, 0))
    tile = pl.BlockSpec((tm, d), lambda i: (i, 0))
    row = pl.BlockSpec((1, d), lambda i: (0, 0))
    sums = [] if head is None else [(pl.BlockSpec((8, LANE), lambda i: (0, 0)), jax.ShapeDtypeStruct((8, LANE), F32)),
                                    (pl.BlockSpec((8, d), lambda i: (0, 0)), jax.ShapeDtypeStruct((8, d), F32))]
    return _pallas(
        body, rides, name=name, grid=(t // tm,),
        in_specs=[tile, row] + ([] if head is None else [row, tile]) + [hbm, hbm, hbm],
        out_specs=[tile, tile, hid, hid, hid] + [s for s, _ in sums],
        out_shape=[jax.ShapeDtypeStruct((t, d), F32), jax.ShapeDtypeStruct((t, d), BF16)]
        + [jax.ShapeDtypeStruct((t, f), BF16)] * 3 + [s for _, s in sums],
        scratch_shapes=[pltpu.VMEM(wg.shape, BF16), pltpu.VMEM(wu.shape, BF16), pltpu.VMEM(wd.shape, BF16),
                        pltpu.SemaphoreType.DMA((3,))],
        sem=("arbitrary",), args=[x, gain] + ([] if head is None else list(head)) + [wg, wu, wd])


def _ffn_bwd_data(dy, x, gain, g, u, wg, wu, wd, name, rides=None):
    t, d = x.shape
    wg, wu, wd = _flat(wg), _flat(wu), _flat(wd)
    f = wg.shape[0]
    tm = min(256, t)

    def body(dy_ref, x_ref, gain_ref, g_ref, u_ref, wg_hbm, wu_hbm, wd_hbm, dx_ref, dg_ref, du_ref, dgain_ref,
             wg_v, wu_v, wd_v, sems):
        @pl.when(pl.program_id(0) == 0)
        def _():
            _load_weights([(wg_hbm, wg_v), (wu_hbm, wu_v), (wd_hbm, wd_v)], sems)
            dgain_ref[...] = jnp.zeros_like(dgain_ref)

        dyv = dy_ref[...]
        dyh = (0.5 * dyv).astype(BF16)
        dxn = jnp.zeros((tm, d), F32)
        chunks = _hidden_chunks(f)
        das = [_dot(dyh, wd_v[c, :], NT) for c in chunks]
        for c, da in zip(chunks, das):
            gj = g_ref[:, c].astype(F32)
            uj = u_ref[:, c].astype(F32)
            sig = _sigmoid(gj)
            dgj = (da * uj * (sig * (1.0 + gj * (1.0 - sig)))).astype(BF16)
            duj = (da * (gj * sig)).astype(BF16)
            dg_ref[:, c] = dgj
            du_ref[:, c] = duj
            dxn = dxn + _dot(dgj, wg_v[c, :]) + _dot(duj, wu_v[c, :])
        xv = x_ref[...]
        r = lax.rsqrt(jnp.mean(xv * xv, axis=-1, keepdims=True) + NORM_EPS)
        xh = xv * r
        dgain_ref[...] += _rows8(dxn * xh)
        dxh = dxn * gain_ref[...]
        dx_ref[...] = dyv + r * (dxh - xh * jnp.mean(dxh * xh, axis=-1, keepdims=True))

    hbm = pl.BlockSpec(memory_space=pl.ANY)
    tile = pl.BlockSpec((tm, d), lambda i: (i, 0))
    hid = pl.BlockSpec((tm, f), lambda i: (i, 0))
    return _pallas(
        body, rides, name=name, grid=(t // tm,),
        in_specs=[tile, tile, pl.BlockSpec((1, d), lambda i: (0, 0)), hid, hid, hbm, hbm, hbm],
        out_specs=[tile, hid, hid, pl.BlockSpec((8, d), lambda i: (0, 0))],
        out_shape=[jax.ShapeDtypeStruct((t, d), F32), jax.ShapeDtypeStruct((t, f), BF16),
                   jax.ShapeDtypeStruct((t, f), BF16), jax.ShapeDtypeStruct((8, d), F32)],
        scratch_shapes=[pltpu.VMEM(wg.shape, BF16), pltpu.VMEM(wu.shape, BF16), pltpu.VMEM(wd.shape, BF16),
                        pltpu.SemaphoreType.DMA((3,))],
        sem=("arbitrary",), args=[dy, x, gain, g, u, wg, wu, wd])


WGRAD_ROW_BLOCKS = 2


def _ffn_wgrad_down(a, dy, name, rides=None):
    t, d = dy.shape
    f = a.shape[1]
    fb = f // WGRAD_ROW_BLOCKS
    tk = min(1024, t)

    def body(dy_ref, a_ref, dwd_ref):
        @pl.when(pl.program_id(1) == 0)
        def _():
            dwd_ref[...] = jnp.zeros_like(dwd_ref)

        dwd_ref[...] += _dot(a_ref[...], (0.5 * dy_ref[...]).astype(BF16), TN)

    return _pallas(
        body, rides, name=name, grid=(WGRAD_ROW_BLOCKS, t // tk),
        in_specs=[pl.BlockSpec((tk, d), lambda j, k: (k, 0)), pl.BlockSpec((tk, fb), lambda j, k: (k, j))],
        out_specs=[pl.BlockSpec((fb, d), lambda j, k: (j, 0))],
        out_shape=[jax.ShapeDtypeStruct((f, d), F32)],
        sem=("arbitrary", "arbitrary"), args=[dy, a])


def _ffn_wgrad_gu(xn, dhs, name, rides=None):
    t, d = xn.shape
    n = len(dhs)
    f = dhs[0].shape[1]
    fb = f // WGRAD_ROW_BLOCKS
    tk = min(2048 // n, t)

    def body(xn_ref, *refs):
        @pl.when(pl.program_id(1) == 0)
        def _():
            for o_ref in refs[n:]:
                o_ref[...] = jnp.zeros_like(o_ref)

        xnv = xn_ref[...]
        for dh_ref, o_ref in zip(refs[:n], refs[n:]):
            o_ref[...] += _dot(dh_ref[...], xnv, TN)

    hid = pl.BlockSpec((tk, fb), lambda j, k: (k, j))
    out = pl.BlockSpec((fb, d), lambda j, k: (j, 0))
    return _pallas(
        body, rides, name=name, grid=(WGRAD_ROW_BLOCKS, t // tk),
        in_specs=[pl.BlockSpec((tk, d), lambda j, k: (k, 0))] + [hid] * n,
        out_specs=[out] * n, out_shape=[jax.ShapeDtypeStruct((f, d), F32)] * n,
        sem=("arbitrary", "arbitrary"), args=[xn] + list(dhs))


def _tn_matmul(a, b, bn, name):
    t, m = a.shape
    n = b.shape[1]
    tk = min(2048, t)

    def body(a_ref, b_ref, o_ref):
        @pl.when(pl.program_id(1) == 0)
        def _():
            o_ref[...] = jnp.zeros_like(o_ref)

        o_ref[...] += _dot(a_ref[...].astype(BF16), b_ref[...].astype(BF16), TN)

    return pl.pallas_call(
        body, name=name, grid=(n // bn, t // tk),
        in_specs=[pl.BlockSpec((tk, m), lambda j, k: (k, 0)), pl.BlockSpec((tk, bn), lambda j, k: (k, j))],
        out_specs=pl.BlockSpec((None, m, bn), lambda j, k: (j, 0, 0)),
        out_shape=jax.ShapeDtypeStruct((n // bn, m, bn), F32),
        compiler_params=_params("arbitrary", "arbitrary"),
    )(a, b)


def _chunk_scratch(tm, w):
    return pltpu.VMEM((w // LANE, tm, LANE), F32)


def _regroup_store(cbuf, out_ref, dil, chunks=None):
    n = out_ref.shape[1]
    for k in range(cbuf.shape[0]) if chunks is None else chunks:
        for g in range(dil):
            rows = cbuf[k] if dil == 1 else cbuf[k, pl.ds(g, n, stride=dil), :]
            out_ref[g, :, k * LANE:(k + 1) * LANE] = rows.astype(out_ref.dtype)


def _natural_rows(ref, dil, cbuf):
    if dil == 1:
        return ref[0].astype(F32)
    n = ref.shape[1]
    for g in range(dil):
        for k in range(cbuf.shape[0]):
            cbuf[k, pl.ds(g, n, stride=dil), :] = ref[g, :, k * LANE:(k + 1) * LANE].astype(F32)
    return jnp.concatenate([cbuf[k] for k in range(cbuf.shape[0])], axis=1)


IN_CHUNK_TILES = 4


def _column_chunks(n):
    step = IN_CHUNK_TILES * V7X_MXU_TILE
    return [slice(s, min(s + step, n)) for s in range(0, n, step)]


def _load_side_by_side(w_hbm, w_v, sems):
    ns, _, cs = w_hbm.shape
    copies = [pltpu.make_async_copy(w_hbm.at[j], w_v.at[:, pl.ds(j * cs, cs)], sems.at[j]) for j in range(ns)]
    for cp in copies:
        cp.start()
    for cp in copies:
        cp.wait()


def _inproj_fwd(h, gain, win):
    t, d = h.shape
    ns, _, cs = win.shape
    tm = min(512, t)
    rw, aw = 4 * RET_WIDTH, 3 * ATT_WIDTH

    def body(h_ref, gain_ref, w_hbm, xn_ref, ur_ref, *rest):
        a_refs, abuf, w_v, sems = rest[:-3], rest[-3], rest[-2], rest[-1]

        @pl.when(pl.program_id(0) == 0)
        def _():
            _load_side_by_side(w_hbm, w_v, sems)

        hv = h_ref[...]
        r = lax.rsqrt(jnp.mean(hv * hv, axis=-1, keepdims=True) + NORM_EPS)
        xn = (hv * r * gain_ref[...]).astype(BF16)
        xn_ref[...] = xn
        for c in reversed(_column_chunks(ns * cs)):
            res = _dot(xn, w_v[:, c])
            mine = []
            for k in range((c.stop - c.start) // LANE):
                chunk = c.start // LANE + k
                piece = res[:, k * LANE:(k + 1) * LANE]
                if chunk < rw // LANE:
                    ur_ref[:, chunk * LANE:(chunk + 1) * LANE] = piece
                else:
                    abuf[chunk - rw // LANE] = piece
                    mine.append(chunk - rw // LANE)
            for dil, a_ref in zip(DILATIONS, a_refs):
                _regroup_store(abuf, a_ref, dil, mine)

    return pl.pallas_call(
        body, name="inproj_fwd", grid=(t // tm,),
        in_specs=[pl.BlockSpec((tm, d), lambda i: (i, 0)), pl.BlockSpec((1, d), lambda i: (0, 0)),
                  pl.BlockSpec(memory_space=pl.ANY)],
        out_specs=[pl.BlockSpec((tm, d), lambda i: (i, 0)), pl.BlockSpec((tm, rw), lambda i: (i, 0))]
        + [pl.BlockSpec((dil, tm // dil, aw), lambda i: (0, i, 0)) for dil in DILATIONS],
        out_shape=[jax.ShapeDtypeStruct((t, d), BF16), jax.ShapeDtypeStruct((t, rw), F32)]
        + [jax.ShapeDtypeStruct((dil, t // dil, aw), BF16) for dil in DILATIONS],
        scratch_shapes=[_chunk_scratch(tm, aw), pltpu.VMEM((d, ns * cs), BF16), pltpu.SemaphoreType.DMA((ns,))],
        compiler_params=_params("arbitrary"),
    )(h, gain, win)


def _inproj_bwd(pieces, parts, h, gain, dres, win):
    t, d = h.shape
    ns, _, cs = win.shape
    pw = pieces[0].shape[1]
    tm = min(512, t)
    npc, nk = len(pieces), len(parts[0])
    flat_parts = [a for p in parts for a in p]

    def body(*refs):
        p_refs, a_refs = refs[:npc], refs[npc:npc + len(flat_parts)]
        h_ref, gain_ref, dres_ref, w_hbm, dh_ref, du_ref, dgain_ref, buf, w_v, sems = refs[npc + len(flat_parts):]

        @pl.when(pl.program_id(0) == 0)
        def _():
            _load_side_by_side(w_hbm, w_v, sems)
            dgain_ref[...] = jnp.zeros_like(dgain_ref)

        for k in range(npc):
            du_ref[:, k * pw:(k + 1) * pw] = p_refs[k][...]
        for k in range(nk):
            acc = None
            for b, dil in enumerate(DILATIONS):
                rows = _natural_rows(a_refs[b * nk + k], dil, buf)
                acc = rows if acc is None else acc + rows
            du_ref[:, (npc + k) * pw:(npc + k + 1) * pw] = acc.astype(BF16)
        dxn = jnp.zeros((tm, d), F32)
        for c in _column_chunks(ns * cs):
            dxn = dxn + _dot(du_ref[:, c], w_v[:, c], NT)
        hv = h_ref[...]
        r = lax.rsqrt(jnp.mean(hv * hv, axis=-1, keepdims=True) + NORM_EPS)
        xh = hv * r
        dgain_ref[...] += _rows8(dxn * xh)
        dxh = dxn * gain_ref[...]
        dh_ref[...] = dres_ref[...] + r * (dxh - xh * jnp.mean(dxh * xh, axis=-1, keepdims=True))

    tile = pl.BlockSpec((tm, d), lambda i: (i, 0))
    cols = (npc + nk) * pw
    return pl.pallas_call(
        body, name="inproj_bwd", grid=(t // tm,),
        in_specs=[pl.BlockSpec((tm, pw), lambda i: (i, 0))] * npc
        + [_regrouped_spec(tm, dil, pw) for dil in DILATIONS for _ in range(nk)]
        + [tile, pl.BlockSpec((1, d), lambda i: (0, 0)), tile, pl.BlockSpec(memory_space=pl.ANY)],
        out_specs=[tile, pl.BlockSpec((tm, cols), lambda i: (i, 0)), pl.BlockSpec((8, d), lambda i: (0, 0))],
        out_shape=[jax.ShapeDtypeStruct((t, d), F32), jax.ShapeDtypeStruct((t, cols), BF16),
                   jax.ShapeDtypeStruct((8, d), F32)],
        scratch_shapes=[_chunk_scratch(tm, pw), pltpu.VMEM((d, ns * cs), BF16), pltpu.SemaphoreType.DMA((ns,))],
        compiler_params=_params("arbitrary"),
    )(*pieces, *flat_parts, h, gain, dres, win)


def _outproj_fwd(h, mix_r, mix_a, wo):
    t, d = h.shape
    hw = mix_r.shape[1]
    tm = min(512, t)

    def body(h_ref, mr_ref, ma_ref, w_ref, o_ref):
        o_ref[...] = h_ref[...] + _dot(mr_ref[...], w_ref[0:hw, :]) + _dot(ma_ref[...], w_ref[hw:2 * hw, :])

    tile = pl.BlockSpec((tm, d), lambda i: (i, 0))
    half = pl.BlockSpec((tm, hw), lambda i: (i, 0))
    return pl.pallas_call(
        body, name="outproj_fwd", grid=(t // tm,),
        in_specs=[tile, half, half, pl.BlockSpec(wo.shape, lambda i: (0, 0))],
        out_specs=tile, out_shape=jax.ShapeDtypeStruct((t, d), F32),
        compiler_params=_params("arbitrary"),
    )(h, mix_r, mix_a, wo)


def _outproj_bwd(dh, wo, rides=None):
    t, d = dh.shape
    hw = wo.shape[0] // 2
    tm = min(512, t)

    def body(dh_ref, w_ref, dr_ref, da_ref):
        dhb = dh_ref[...].astype(BF16)
        dr_ref[...] = _dot(dhb, w_ref[0:hw, :], NT)
        da_ref[...] = _dot(dhb, w_ref[hw:2 * hw, :], NT)

    half = pl.BlockSpec((tm, hw), lambda i: (i, 0))
    return _pallas(
        body, rides, name="outproj_bwd", grid=(t // tm,),
        in_specs=[pl.BlockSpec((tm, d), lambda i: (i, 0)), pl.BlockSpec(wo.shape, lambda i: (0, 0))],
        out_specs=[half, half],
        out_shape=[jax.ShapeDtypeStruct((t, hw), F32), jax.ShapeDtypeStruct((t, hw), F32)],
        sem=("arbitrary",), args=[dh, wo])


def _retention_tables(t):
    pos = jnp.arange(t, dtype=F32)
    pair = (jnp.arange(RET_DIM) // 2 * 2).astype(F32)
    ang = pos[:, None] * (ROPE_BASE ** (-pair / RET_DIM))[None, :]
    c = RET_CHUNK
    log_g = jnp.log(1.0 - 2.0 ** (-5.0 - jnp.arange(RET_HEADS, dtype=F32)))
    idx = jnp.arange(c, dtype=F32)
    rel = idx[:, None] - idx[None, :]
    decay = jnp.where(rel >= 0, jnp.exp(log_g[:, None, None] * jnp.maximum(rel, 0.0)), 0.0)
    zeta = jnp.exp(log_g[:, None] * (c - 1 - idx)[None, :])
    xi = jnp.exp(log_g[:, None] * (idx + 1)[None, :])
    gc = jnp.exp(log_g * c)
    wide = lambda v: jnp.broadcast_to(v[:, :, None], (RET_HEADS, c, LANE))
    return (jnp.cos(ang), jnp.sin(ang), decay, wide(zeta), wide(xi),
            jnp.broadcast_to(gc[:, None, None], (RET_HEADS, c, LANE)))


def _rot(v):
    lane = lax.broadcasted_iota(jnp.int32, v.shape, 1)
    nxt = pltpu.roll(v, LANE - 1, 1)
    prv = pltpu.roll(v, 1, 1)
    return jnp.where(lane % 2 == 0, -nxt, prv)


def _ret_specs(tr, rev, nt):
    ti = (lambda i: nt - 1 - i) if rev else (lambda i: i)
    col = lambda blk: pl.BlockSpec((tr, RET_WIDTH), lambda i: (ti(i), blk))
    tab = pl.BlockSpec((tr, LANE), lambda i: (ti(i), 0))
    head = pl.BlockSpec((RET_HEADS, RET_CHUNK, LANE), lambda i: (0, 0, 0))
    return col, tab, head


def _ret_chunks(tr, rev=False):
    order = list(range(tr // RET_CHUNK))
    return [(pl.ds(ci * RET_CHUNK, RET_CHUNK), slice(h * RET_DIM, (h + 1) * RET_DIM), h)
            for h in range(RET_HEADS) for ci in (reversed(order) if rev else order)]


def _ret_operands(items, q_ref, k_ref, v_ref, cos_ref, sin_ref, zeta_ref):
    scale = RET_DIM ** -0.5
    qbs, kbs, vbs, kzs = [], [], [], []
    for sl, hs, h in items:
        cs, sn = cos_ref[sl, :], sin_ref[sl, :]
        q, k = q_ref[sl, hs], k_ref[sl, hs]
        kr = (k * cs + _rot(k) * sn) * scale
        qbs.append((q * cs + _rot(q) * sn).astype(BF16))
        kbs.append(kr.astype(BF16))
        vbs.append(v_ref[sl, hs].astype(BF16))
        kzs.append((kr * zeta_ref[h]).astype(BF16))
    return qbs, kbs, vbs, kzs


def _ret_states(items, state, steps, gc_ref):
    cur, befores = {}, []
    for (sl, hs, h), step in zip(items, steps):
        st = cur[h] if h in cur else state[h]
        befores.append(st)
        cur[h] = st * gc_ref[h] + step
    for h, st in cur.items():
        state[h] = st
    return befores


def _ret_fwd(u, gain, tabs):
    t = u.shape[0]
    tr = min(512, t)
    nt = t // tr
    cos, sin, decay, zeta, xi, gc = tabs

    def body(q_ref, k_ref, v_ref, gt_ref, cos_ref, sin_ref, gain_ref, dec_ref, zeta_ref, xi_ref, gc_ref,
             raw_ref, mix_ref, state):
        @pl.when(pl.program_id(0) == 0)
        def _():
            state[...] = jnp.zeros_like(state)

        items = _ret_chunks(tr)
        n = range(len(items))
        qbs, kbs, vbs, kzs = _ret_operands(items, q_ref, k_ref, v_ref, cos_ref, sin_ref, zeta_ref)
        ss = [_dot(qbs[i], kbs[i], NT) for i in n]
        kvs = [_dot(kzs[i], vbs[i], TN) for i in n]
        befores = _ret_states(items, state, kvs, gc_ref)
        intra = [_dot((ss[i] * dec_ref[items[i][2]]).astype(BF16), vbs[i]) for i in n]
        inter = [_dot(qbs[i], befores[i].astype(BF16)) for i in n]
        for i, (sl, hs, h) in enumerate(items):
            o = intra[i] + inter[i] * xi_ref[h]
            raw_ref[sl, hs] = o
            mu = jnp.mean(o, axis=-1, keepdims=True)
            var = jnp.mean(jnp.square(o - mu), axis=-1, keepdims=True)
            y = (o - mu) * lax.rsqrt(var + GN_EPS) * gain_ref[:, hs]
            gt = gt_ref[sl, hs]
            mix_ref[sl, hs] = (y * (gt * _sigmoid(gt))).astype(BF16)

    col, tab, head = _ret_specs(tr, False, nt)
    out = pl.BlockSpec((tr, RET_WIDTH), lambda i: (i, 0))
    return pl.pallas_call(
        body, name="ret_fwd", grid=(nt,),
        in_specs=[col(0), col(1), col(2), col(3), tab, tab, pl.BlockSpec((1, RET_WIDTH), lambda i: (0, 0)),
                  head, head, head, head],
        out_specs=[out, out],
        out_shape=[jax.ShapeDtypeStruct((t, RET_WIDTH), F32), jax.ShapeDtypeStruct((t, RET_WIDTH), BF16)],
        scratch_shapes=[pltpu.VMEM((RET_HEADS, RET_DIM, RET_DIM), F32)],
        compiler_params=_params("arbitrary"),
    )(u, u, u, u, cos, sin, gain, decay, zeta, xi, gc)


def _ret_bwd_q(dmix, raw, u, gain, tabs, rides=None):
    t = u.shape[0]
    tr = min(512, t)
    nt = t // tr
    cos, sin, decay, zeta, xi, gc = tabs

    def body(dm_ref, raw_ref, q_ref, k_ref, v_ref, gt_ref, cos_ref, sin_ref, gain_ref, dec_ref, zeta_ref, xi_ref, gc_ref,
             dq_ref, dgt_ref, dret_ref, dgain_ref, state):
        @pl.when(pl.program_id(0) == 0)
        def _():
            state[...] = jnp.zeros_like(state)
            dgain_ref[...] = jnp.zeros_like(dgain_ref)

        items = _ret_chunks(tr)
        n_items = range(len(items))
        qbs, kbs, vbs, kzs = _ret_operands(items, q_ref, k_ref, v_ref, cos_ref, sin_ref, zeta_ref)
        dos, dgains = [], {}
        for sl, hs, h in items:
            o = raw_ref[sl, hs]
            mu = jnp.mean(o, axis=-1, keepdims=True)
            var = jnp.mean(jnp.square(o - mu), axis=-1, keepdims=True)
            rs = lax.rsqrt(var + GN_EPS)
            n = (o - mu) * rs
            gt = gt_ref[sl, hs]
            sig = _sigmoid(gt)
            dout = dm_ref[sl, hs]
            gain_h = gain_ref[:, hs]
            dgt_ref[sl, hs] = (dout * (n * gain_h) * (sig * (1.0 + gt * (1.0 - sig)))).astype(BF16)
            dy = dout * (gt * sig)
            dgains[h] = dgains[h] + _rows8(dy * n) if h in dgains else _rows8(dy * n)
            dn = dy * gain_h
            do = rs * (dn - jnp.mean(dn, axis=-1, keepdims=True) - n * jnp.mean(dn * n, axis=-1, keepdims=True))
            dret_ref[sl, hs] = do
            dos.append(do)
        for h, dg in dgains.items():
            dgain_ref[:, h * RET_DIM:(h + 1) * RET_DIM] += dg
        dss = [_dot(dos[i].astype(BF16), vbs[i], NT) for i in n_items]
        kvs = [_dot(kzs[i], vbs[i], TN) for i in n_items]
        befores = _ret_states(items, state, kvs, gc_ref)
        intra = [_dot((dss[i] * dec_ref[items[i][2]]).astype(BF16), kbs[i]) for i in n_items]
        inter = [_dot((dos[i] * xi_ref[items[i][2]]).astype(BF16), befores[i].astype(BF16), NT) for i in n_items]
        for i, (sl, hs, h) in enumerate(items):
            dqr = intra[i] + inter[i]
            dq_ref[sl, hs] = (dqr * cos_ref[sl, :] - _rot(dqr * sin_ref[sl, :])).astype(BF16)

    col, tab, head = _ret_specs(tr, False, nt)
    out = pl.BlockSpec((tr, RET_WIDTH), lambda i: (i, 0))
    return _pallas(
        body, rides, name="ret_bwd_q", grid=(nt,),
        in_specs=[out, out, col(0), col(1), col(2), col(3), tab, tab, pl.BlockSpec((1, RET_WIDTH), lambda i: (0, 0)),
                  head, head, head, head],
        out_specs=[out, out, out, pl.BlockSpec((8, RET_WIDTH), lambda i: (0, 0))],
        out_shape=[jax.ShapeDtypeStruct((t, RET_WIDTH), BF16), jax.ShapeDtypeStruct((t, RET_WIDTH), BF16),
                   jax.ShapeDtypeStruct((t, RET_WIDTH), F32), jax.ShapeDtypeStruct((8, RET_WIDTH), F32)],
        scratch_shapes=[pltpu.VMEM((RET_HEADS, RET_DIM, RET_DIM), F32)],
        sem=("arbitrary",), args=[dmix, raw, u, u, u, u, cos, sin, gain, decay, zeta, xi, gc])


def _ret_bwd_kv(dret, u, tabs, rides=None):
    t = u.shape[0]
    tr = min(512, t)
    nt = t // tr
    cos, sin, decay, zeta, xi, gc = tabs
    scale = RET_DIM ** -0.5

    def body(do_ref, q_ref, k_ref, v_ref, cos_ref, sin_ref, dec_ref, zeta_ref, xi_ref, gc_ref, dk_ref, dv_ref, gst):
        @pl.when(pl.program_id(0) == 0)
        def _():
            gst[...] = jnp.zeros_like(gst)

        items = _ret_chunks(tr, rev=True)
        n = range(len(items))
        qbs, kbs, vbs, kzs = _ret_operands(items, q_ref, k_ref, v_ref, cos_ref, sin_ref, zeta_ref)
        dos = [do_ref[sl, hs] for sl, hs, h in items]
        dobs = [do.astype(BF16) for do in dos]
        ss = [_dot(qbs[i], kbs[i], NT) for i in n]
        dss = [_dot(dobs[i], vbs[i], NT) for i in n]
        steps = [_dot(qbs[i], (dos[i] * xi_ref[items[i][2]]).astype(BF16), TN) for i in n]
        afters = [g.astype(BF16) for g in _ret_states(items, gst, steps, gc_ref)]
        dvs = [_dot((ss[i] * dec_ref[items[i][2]]).astype(BF16), dobs[i], TN) + _dot(kzs[i], afters[i]) for i in n]
        dks = [_dot((dss[i] * dec_ref[items[i][2]]).astype(BF16), qbs[i], TN) for i in n]
        dkz = [_dot(vbs[i], afters[i], NT) for i in n]
        for i, (sl, hs, h) in enumerate(items):
            dv_ref[sl, hs] = dvs[i].astype(BF16)
            dkr = (dks[i] + dkz[i] * zeta_ref[h]) * scale
            dk_ref[sl, hs] = (dkr * cos_ref[sl, :] - _rot(dkr * sin_ref[sl, :])).astype(BF16)

    col, tab, head = _ret_specs(tr, True, nt)
    out = pl.BlockSpec((tr, RET_WIDTH), lambda i: (nt - 1 - i, 0))
    return _pallas(
        body, rides, name="ret_bwd_kv", grid=(nt,),
        in_specs=[out, col(0), col(1), col(2), tab, tab, head, head, head, head],
        out_specs=[out, out],
        out_shape=[jax.ShapeDtypeStruct((t, RET_WIDTH), BF16), jax.ShapeDtypeStruct((t, RET_WIDTH), BF16)],
        scratch_shapes=[pltpu.VMEM((RET_HEADS, RET_DIM, RET_DIM), F32)],
        sem=("arbitrary",), args=[dret, u, u, u, cos, sin, decay, zeta, xi, gc])


PAIRS = ATT_WIDTH // LANE
ATT_Q_BLK, ATT_K_BLK, ATT_V_BLK = 0, PAIRS, 2 * PAIRS
STAT_LANES = ATT_DIM // 2


ATT_STEP_ROWS = 4096


def _att_tiles(t, dil):
    sub = t // dil
    tq = min(ATT_STEP_ROWS, sub)
    return sub, tq, sub // tq, tq // ATT_BLOCK, min(dil, ATT_STEP_ROWS // tq)


def _att_in_specs(tq, qb, ti, gs):
    cur = lambda off: pl.BlockSpec((gs, tq, LANE), lambda g, p, i: (g, ti(i), off + p))
    prev = lambda off: pl.BlockSpec((gs, ATT_BLOCK, LANE), lambda g, p, i: (g, jnp.maximum(ti(i) * qb - 1, 0), off + p))
    return [cur(ATT_Q_BLK), cur(ATT_K_BLK), prev(ATT_K_BLK), cur(ATT_V_BLK), prev(ATT_V_BLK)]


def _band_mask():
    key = lax.broadcasted_iota(jnp.int32, (2 * ATT_BLOCK, 2 * ATT_BLOCK), 0)
    qry = lax.broadcasted_iota(jnp.int32, (2 * ATT_BLOCK, 2 * ATT_BLOCK), 1) % ATT_BLOCK
    dist = qry + ATT_BLOCK - key
    return (dist >= 0) & (dist <= ATT_BLOCK), key >= ATT_BLOCK


def _head0_lanes():
    return lax.broadcasted_iota(jnp.int32, (ATT_BLOCK, LANE), 1) < ATT_DIM


def _stack_heads(v, head0):
    zero = jnp.zeros((), v.dtype)
    return jnp.concatenate([jnp.where(head0, v, zero), jnp.where(head0, zero, v)], axis=0)


def _unstack_heads(v, head0):
    return jnp.where(head0, v[0:ATT_BLOCK], v[ATT_BLOCK:])


def _att_fwd(ua, dil):
    sub = ua.shape[1]
    _, tq, nq, qb, gs = _att_tiles(sub * dil, dil)

    def body(q_ref, kc_ref, kp_ref, vc_ref, vp_ref, o_ref, l_ref, kx, vx):
        tile = pl.program_id(2)
        kx[:, 0:ATT_BLOCK, :] = kp_ref[...]
        kx[:, ATT_BLOCK:, :] = kc_ref[...]
        vx[:, 0:ATT_BLOCK, :] = vp_ref[...]
        vx[:, ATT_BLOCK:, :] = vc_ref[...]
        band, cur_keys = _band_mask()
        head0 = _head0_lanes()
        items = [(r, b) for r in range(gs) for b in range(qb)]
        rows = lambda b: slice(b * ATT_BLOCK, (b + 1) * ATT_BLOCK)
        keys = lambda b: slice(b * ATT_BLOCK, (b + 2) * ATT_BLOCK)
        sts = [_dot(kx[r, keys(b), :], _stack_heads(q_ref[r, rows(b), :] * jnp.asarray(ATT_DIM ** -0.5, BF16), head0), NT)
               for r, b in items]
        pts, lses = [], []
        for (r, b), st in zip(items, sts):
            mask = band if b > 0 else band & (cur_keys | (tile > 0))
            st = jnp.where(mask, st, -1e30)
            m = jnp.max(st, axis=0, keepdims=True)
            ex = jnp.exp(st - m)
            den = jnp.sum(ex, axis=0, keepdims=True)
            pts.append((ex * (1.0 / den)).astype(BF16))
            lses.append(m + jnp.log(den))
        outs = [_dot(pt, vx[r, keys(b), :], TN) for (r, b), pt in zip(items, pts)]
        for (r, b), out, lse in zip(items, outs, lses):
            o_ref[r, rows(b), :] = _unstack_heads(out, head0).astype(BF16)
            cols = [jnp.broadcast_to(lse[:, e * ATT_BLOCK:(e + 1) * ATT_BLOCK], (ATT_BLOCK, LANE)).T for e in range(2)]
            l_ref[r, rows(b), :] = jnp.where(head0, cols[0], cols[1])

    out = pl.BlockSpec((gs, tq, LANE), lambda g, p, i: (g, i, p))
    return pl.pallas_call(
        body, name=f"att_fwd_d{dil}", grid=(dil // gs, PAIRS, nq),
        in_specs=_att_in_specs(tq, qb, lambda i: i, gs),
        out_specs=[out, out],
        out_shape=[jax.ShapeDtypeStruct((dil, sub, ATT_WIDTH), BF16), jax.ShapeDtypeStruct((dil, sub, ATT_WIDTH), F32)],
        scratch_shapes=[pltpu.VMEM((gs, tq + ATT_BLOCK, LANE), BF16)] * 2,
        compiler_params=_params("arbitrary", "arbitrary", "arbitrary"),
    )(ua, ua, ua, ua, ua)


def _regrouped_spec(tm, dil, w):
    return pl.BlockSpec((dil, tm // dil, w), lambda i: (0, i, 0))


def _att_combine(outs, lses, t):
    w = ATT_WIDTH
    tm = min(512, t)
    nb = len(outs)

    def body(*refs):
        o_refs, l_refs = refs[:nb], refs[nb:2 * nb]
        mix_ref, att_ref, lse_ref, buf = refs[2 * nb:]
        ls = [_natural_rows(r, dil, buf) for r, dil in zip(l_refs, DILATIONS)]
        m = functools.reduce(jnp.maximum, ls)
        ws = [jnp.exp(l - m) for l in ls]
        den = functools.reduce(jnp.add, ws)
        att = functools.reduce(jnp.add, [(wt / den) * _natural_rows(r, dil, buf) for wt, r, dil in zip(ws, o_refs, DILATIONS)])
        att_ref[...] = att
        mix_ref[...] = att.astype(BF16)
        lse_ref[...] = m + jnp.log(den)

    tile = pl.BlockSpec((tm, w), lambda i: (i, 0))
    regrouped = [_regrouped_spec(tm, dil, w) for dil in DILATIONS]
    return pl.pallas_call(
        body, name="att_combine", grid=(t // tm,),
        in_specs=regrouped * 2, out_specs=[tile, tile, tile],
        out_shape=[jax.ShapeDtypeStruct((t, w), BF16), jax.ShapeDtypeStruct((t, w), F32), jax.ShapeDtypeStruct((t, w), F32)],
        scratch_shapes=[_chunk_scratch(tm, w)],
        compiler_params=_params("arbitrary"),
    )(*outs, *lses)


def _att_bwd_prep(datt, att, lse):
    t, w = datt.shape
    tm = min(512, t)

    def body(da_ref, at_ref, l_ref, *rest):
        outs, dbuf, sbuf = rest[:-2], rest[-2], rest[-1]
        dav = da_ref[...]
        prod = dav * at_ref[...]
        lane = lax.broadcasted_iota(jnp.int32, (tm, LANE), 1)
        for k in range(w // LANE):
            cols = slice(k * LANE, (k + 1) * LANE)
            dbuf[k] = dav[:, cols]
            delta = jnp.concatenate(
                [jnp.broadcast_to(jnp.sum(prod[:, k * LANE + e * ATT_DIM:k * LANE + (e + 1) * ATT_DIM], axis=-1, keepdims=True),
                                  (tm, ATT_DIM)) for e in range(LANE // ATT_DIM)], axis=1)
            sbuf[k] = jnp.where(lane % ATT_DIM < STAT_LANES, l_ref[:, cols], delta)
        for k, dil in enumerate(DILATIONS):
            _regroup_store(dbuf, outs[2 * k], dil)
            _regroup_store(sbuf, outs[2 * k + 1], dil)

    tile = pl.BlockSpec((tm, w), lambda i: (i, 0))
    res = pl.pallas_call(
        body, name="att_bwd_prep", grid=(t // tm,),
        in_specs=[tile] * 3,
        out_specs=[_regrouped_spec(tm, dil, w) for dil in DILATIONS for _ in range(2)],
        out_shape=[jax.ShapeDtypeStruct((dil, t // dil, w), dt) for dil in DILATIONS for dt in (BF16, F32)],
        scratch_shapes=[_chunk_scratch(tm, w)] * 2,
        compiler_params=_params("arbitrary"),
    )(datt, att, lse)
    return [(res[2 * k], res[2 * k + 1]) for k in range(len(DILATIONS))]


def _att_bwd(ua, da, stat, dil, rides=None):
    sub = ua.shape[1]
    _, tq, nq, qb, gs = _att_tiles(sub * dil, dil)
    scale = ATT_DIM ** -0.5

    def body(q_ref, kc_ref, kp_ref, vc_ref, vp_ref, da_ref, st_ref, dq_ref, dk_ref, dv_ref, kx, vx, ck, cv):
        step = pl.program_id(2)
        tile = nq - 1 - step

        @pl.when(step == 0)
        def _():
            ck[...] = jnp.zeros_like(ck)
            cv[...] = jnp.zeros_like(cv)

        kx[:, 0:ATT_BLOCK, :] = kp_ref[...]
        kx[:, ATT_BLOCK:, :] = kc_ref[...]
        vx[:, 0:ATT_BLOCK, :] = vp_ref[...]
        vx[:, ATT_BLOCK:, :] = vc_ref[...]
        band, cur_keys = _band_mask()
        head0 = _head0_lanes()
        items = [(r, b) for r in range(gs) for b in range(qb)]
        n = range(len(items))
        rows = lambda b: slice(b * ATT_BLOCK, (b + 1) * ATT_BLOCK)
        keys = lambda b: slice(b * ATT_BLOCK, (b + 2) * ATT_BLOCK)
        qqs = [_stack_heads(q_ref[r, rows(b), :] * jnp.asarray(scale, BF16), head0) for r, b in items]
        dds = [_stack_heads(da_ref[r, rows(b), :], head0) for r, b in items]
        sts = [_dot(kx[r, keys(b), :], qqs[i], NT) for i, (r, b) in enumerate(items)]
        dpts = [_dot(vx[r, keys(b), :], dds[i], NT) for i, (r, b) in enumerate(items)]
        pts, dsts = [], []
        for i, (r, b) in enumerate(items):
            mask = band if b > 0 else band & (cur_keys | (tile > 0))
            stat = st_ref[r, rows(b), :].T
            row = lambda k: jnp.concatenate([stat[e * ATT_DIM + k:e * ATT_DIM + k + 1, :] for e in range(2)], axis=1)
            pt = jnp.where(mask, jnp.exp(sts[i] - row(0)), 0.0)
            dsts.append((pt * (dpts[i] - row(STAT_LANES))).astype(BF16))
            pts.append(pt.astype(BF16))
        dqs = [_dot(dsts[i], kx[r, keys(b), :], TN) for i, (r, b) in enumerate(items)]
        dkbs = [_dot(dsts[i], qqs[i]) for i in n]
        dvbs = [_dot(pts[i], dds[i]) for i in n]
        for i, (r, b) in enumerate(items):
            dq_ref[r, rows(b), :] = (_unstack_heads(dqs[i], head0) * scale).astype(BF16)
            if b > 0:
                dk_ref[r, rows(b - 1), :] = (dkbs[i - 1][ATT_BLOCK:] + dkbs[i][0:ATT_BLOCK]).astype(BF16)
                dv_ref[r, rows(b - 1), :] = (dvbs[i - 1][ATT_BLOCK:] + dvbs[i][0:ATT_BLOCK]).astype(BF16)
        for r in range(gs):
            first, last = r * qb, r * qb + qb - 1
            dk_ref[r, rows(qb - 1), :] = (dkbs[last][ATT_BLOCK:] + ck[r]).astype(BF16)
            dv_ref[r, rows(qb - 1), :] = (dvbs[last][ATT_BLOCK:] + cv[r]).astype(BF16)
            ck[r] = dkbs[first][0:ATT_BLOCK]
            cv[r] = dvbs[first][0:ATT_BLOCK]

    ti = lambda i: nq - 1 - i
    out = pl.BlockSpec((gs, tq, LANE), lambda g, p, i: (g, ti(i), p))
    shape = jax.ShapeDtypeStruct((dil, sub, ATT_WIDTH), BF16)
    return _pallas(
        body, rides, name=f"att_bwd_d{dil}", grid=(dil // gs, PAIRS, nq),
        in_specs=_att_in_specs(tq, qb, ti, gs) + [out, out],
        out_specs=[out, out, out], out_shape=[shape] * 3,
        scratch_shapes=[pltpu.VMEM((gs, tq + ATT_BLOCK, LANE), BF16)] * 2 + [pltpu.VMEM((gs, ATT_BLOCK, LANE), F32)] * 2,
        sem=("arbitrary", "arbitrary", "arbitrary"), args=[ua, ua, ua, ua, ua, da, stat])


class _Reduction:
    def __init__(self, place, names, grads):
        self.place, self.names, self.grads = place, names, grads

    def pair(self):
        return _pair_ride(self.grads)

    def chips(self, got):
        self.got = got
        return _chip_ride([_pair_sum(self.place, g, r, f"pair_sum_{n}") for g, r, n in zip(self.grads, got, self.names)])

    def halves(self, others):
        return [_chip_sum(self.place, g, r, o, f"chip_sum_{n}")
                for g, r, o, n in zip(self.grads, self.got, others[:len(self.grads)], self.names)]


def _step(x, target, gains, w, place=None):
    t = x.shape[0]
    ex = place is not None
    g_ffn1, g_mix, g_ret, g_ffn2, g_fin = gains
    w = list(w)
    tabs = _retention_tables(t)
    red = lambda names, grads: _Reduction(place, names, grads) if ex else None
    ride = lambda r: [r] if ex else None

    if ex:
        w[0:3] = _run(_gather_ride(w[0:3]), "gather_ffn1_weights")
    (h1, xn1, *hid1, act1), rest = _ffn_fwd(x, g_ffn1, *w[0:3], "ffn1_fwd", ride(_gather_ride(w[3:])) if ex else None)
    if ex:
        w[3:] = rest[0]
    wg1, wu1, wd1, win, wo, wg2, wu2, wd2 = w
    wo2 = wo.reshape(wo.shape[0] * wo.shape[1], wo.shape[2])
    xnm, u, *uas = _inproj_fwd(h1, g_mix, win)
    raw, mix_r = _ret_fwd(u, g_ret, tabs)
    branches = [_att_fwd(ua, dil) for ua, dil in zip(uas, DILATIONS)]
    mix_a, att, lse = _att_combine([b[0] for b in branches], [b[1] for b in branches], t)
    h2 = _outproj_fwd(h1, mix_r, mix_a, wo2)
    (dh3, xn2, *hid2, act2, loss_p, dg_fin), _ = _ffn_fwd(h2, g_ffn2, wg2, wu2, wd2, "ffn2_fwd", head=(g_fin, target))

    (dwd2,), _ = _ffn_wgrad_down(act2, dh3, "ffn2_wgrad_down")
    dwd2 = dwd2.reshape(wd2.shape)
    r_d2 = red(["ffn2_w_down"], [dwd2])
    (dh2, dga2, dua2, dg_ffn2), e = _ffn_bwd_data(dh3, h2, g_ffn2, *hid2, wg2, wu2, wd2, "ffn2_bwd",
                                                  ex and [r_d2.pair()])
    (dwg2, dwu2), e = _ffn_wgrad_gu(xn2, [dga2, dua2], "ffn2_wgrad_gu", ex and [r_d2.chips(e[0])])
    dwg2, dwu2 = dwg2.reshape(wg2.shape), dwu2.reshape(wu2.shape)
    r_gu2 = red(["ffn2_w_gate", "ffn2_w_up"], [dwg2, dwu2])
    (dmix_r, dmix_a), e = _outproj_bwd(dh2, wo2, ex and [r_gu2.pair(), _finish_ride(r_d2.halves(e[0]))])
    if ex:
        got_gu2, (dwd2,) = e
    hw = RET_WIDTH // (wo.shape[1])
    dwo = jnp.concatenate([_tn_matmul(mix_r, dh2, dh2.shape[1], "wo_grad_r").reshape(hw, wo.shape[1], wo.shape[2]),
                           _tn_matmul(mix_a, dh2, dh2.shape[1], "wo_grad_a").reshape(hw, wo.shape[1], wo.shape[2])])
    r_wo = red(["w_out"], [dwo])
    (dq_r, dgt_r, dret, dg_ret), e = _ret_bwd_q(dmix_r, raw, u, g_ret, tabs, ex and [r_gu2.chips(got_gu2)])
    (dk_r, dv_r), e = _ret_bwd_kv(dret, u, tabs, ex and [r_wo.pair(), _finish_ride(r_gu2.halves(e[0]))])
    if ex:
        got_wo, (dwg2, dwu2) = e
    prep = _att_bwd_prep(dmix_a, att, lse)
    p1, e = _att_bwd(uas[0], *prep[0], DILATIONS[0], ex and [r_wo.chips(got_wo)])
    p4, e = _att_bwd(uas[1], *prep[1], DILATIONS[1], ex and [_finish_ride(r_wo.halves(e[0]))])
    if ex:
        (dwo,), = e
    p16, _ = _att_bwd(uas[2], *prep[2], DILATIONS[2])
    dh1, du, dg_mix = _inproj_bwd([dq_r, dk_r, dv_r, dgt_r], [p1, p4, p16], h1, g_mix, dh2, win)
    dwin = _tn_matmul(xnm, du, win.shape[2], "win_grad")
    r_in = red(["w_in"], [dwin])
    (dwd1,), e = _ffn_wgrad_down(act1, dh1, "ffn1_wgrad_down", ex and [r_in.pair()])
    dwd1 = dwd1.reshape(wd1.shape)
    r_d1 = red(["ffn1_w_down"], [dwd1])
    got_in = e
    (dx, dga1, dua1, dg_ffn1), _ = _ffn_bwd_data(dh1, x, g_ffn1, *hid1, wg1, wu1, wd1, "ffn1_bwd")
    (dwg1,), e = _ffn_wgrad_gu(xn1, [dga1], "ffn1_wgrad_gate", ex and [r_in.chips(got_in[0]), r_d1.pair()])
    dwg1 = dwg1.reshape(wg1.shape)
    if ex:
        oth_in, got_d1 = e
        r_g1 = red(["ffn1_w_gate"], [dwg1])
        got_g1 = _run(r_g1.pair(), "pair_exchange_ffn1_gate")
    (dwu1,), e = _ffn_wgrad_gu(xn1, [dua1], "ffn1_wgrad_up",
                               ex and [_finish_ride(r_in.halves(oth_in)), r_d1.chips(got_d1), r_g1.chips(got_g1)])
    dwu1 = dwu1.reshape(wu1.shape)
    gain_parts = [dg_ffn1, dg_mix, dg_ret, dg_ffn2, dg_fin]
    if not ex:
        return loss_p, dx, [dwg1, dwu1, dwd1, dwin, dwo, dwg2, dwu2, dwd2], gain_parts
    (dwin,), oth_d1, oth_g1 = e
    r_u1 = red(["ffn1_w_up"], [dwu1])
    got_u1 = _run(r_u1.pair(), "pair_exchange_ffn1_up")
    oth_u1 = _run(r_u1.chips(got_u1), "chip_exchange_ffn1_up")
    last = r_g1.halves(oth_g1) + r_u1.halves(oth_u1) + r_d1.halves(oth_d1)
    dwg1, dwu1, dwd1, gall = _run(_finish_ride(last, _pack_gains(gain_parts, x.shape[1])), "finish_exchange_ffn1")
    return loss_p, dx, [dwg1, dwu1, dwd1, dwin, dwo, dwg2, dwu2, dwd2], gall


N_DEV = 8
GAIN_ROWS = 8


def _place():
    x, y, c = lax.axis_index("x"), lax.axis_index("y"), lax.axis_index("c")
    chips = [(1 - x, y), (x, 1 - y), (1 - x, 1 - y)]
    return x, y, c, chips


ROW_QUARTERS = 4


def _place_shards(place, ws):
    n = len(ws)

    def body(place_ref, *refs):
        for w_ref, o_ref in zip(refs[:n], refs[n:]):
            o_ref[...] = w_ref[...].astype(BF16)

    quarter = lambda w: (w.shape[0] // ROW_QUARTERS, w.shape[1])
    return pl.pallas_call(
        body, name="place_shards",
        grid_spec=pltpu.PrefetchScalarGridSpec(
            num_scalar_prefetch=1, grid=(ROW_QUARTERS,),
            in_specs=[pl.BlockSpec(quarter(w), lambda i, pr: (i, 0)) for w in ws],
            out_specs=[pl.BlockSpec((None,) + quarter(w), lambda i, pr: (pr[0], i, 0)) for w in ws]),
        out_shape=[jax.ShapeDtypeStruct((N_SHARD,) + w.shape, BF16) for w in ws],
        compiler_params=_params("arbitrary"),
    )(place, *ws)


def _gather_ride(bufs):
    na = len(bufs)
    sent = [("me", "half", "x"), ("me", "half", "y"), ("x", "second quarter", "y"), ("y", "first quarter", "x")]
    landed = [("x", "half"), ("y", "half"), ("d", "second quarter"), ("d", "first quarter")]

    def legs(outs, sems):
        send_sem, recv_sem, fsend_sem, frecv_sem = sems
        x, y, c, _ = _place()
        slot = {"me": 2 * x + y, "x": 2 * (1 - x) + y, "y": 2 * x + (1 - y), "d": 2 * (1 - x) + (1 - y)}
        peer = {"x": (1 - x, y, c), "y": (x, 1 - y, c)}

        def rows(a, which, piece, core):
            hr = outs[a].shape[1] // 2
            lo, n = {"half": (0, hr), "first quarter": (0, hr // 2), "second quarter": (hr // 2, hr // 2)}[piece]
            return outs[a].at[slot[which], pl.ds(core * hr + lo, n)]

        def ici(a, k):
            which, piece, to = sent[k]
            ref = rows(a, which, piece, c)
            return pltpu.make_async_remote_copy(src_ref=ref, dst_ref=ref, send_sem=send_sem.at[a, k],
                                                recv_sem=recv_sem.at[a, k], device_id=peer[to], device_id_type=MESH)

        def arrival(a, k):
            ref = rows(a, *landed[k], c)
            return pltpu.make_async_remote_copy(src_ref=ref, dst_ref=ref, send_sem=send_sem.at[a, k],
                                                recv_sem=recv_sem.at[a, k], device_id=peer["x"], device_id_type=MESH)

        def d2d(a, k, core):
            ref = rows(a, *landed[k], core)
            return pltpu.make_async_remote_copy(src_ref=ref, dst_ref=ref, send_sem=fsend_sem.at[a, k],
                                                recv_sem=frecv_sem.at[a, k], device_id=(x, y, 1 - c), device_id_type=MESH)

        return c, ici, arrival, d2d

    def start(ins, outs, sems):
        _, ici, _, _ = legs(outs, sems)
        for a in range(na):
            ici(a, 0).start()
            ici(a, 1).start()

    def middle(ins, outs, sems):
        c, ici, arrival, d2d = legs(outs, sems)
        for a in range(na):
            for k in (0, 1):
                arrival(a, k).wait_recv()
                ici(a, 2 + k).start()
                d2d(a, k, c).start()

    def finish(ins, outs, sems):
        c, ici, arrival, d2d = legs(outs, sems)
        for a in range(na):
            for k in (2, 3):
                arrival(a, k).wait_recv()
                d2d(a, k, c).start()
        for a in range(na):
            for k in range(len(landed)):
                d2d(a, k, 1 - c).wait_recv()
        for a in range(na):
            for k in range(len(sent)):
                ici(a, k).wait_send()
                d2d(a, k, c).wait_send()

    return _Ride(bufs, [jax.ShapeDtypeStruct(b.shape, b.dtype) for b in bufs], [pltpu.SemaphoreType.DMA((na, 4))] * 4,
                 start, finish, {a: a for a in range(na)}, middle)


def _pair_ride(grads):
    na = len(grads)

    def copies(ins, outs, sems):
        send_sem, recv_sem = sems
        x, y, c, _ = _place()
        res = []
        for a in range(na):
            hr = ins[a].shape[1] // 2
            res.append(pltpu.make_async_remote_copy(
                src_ref=ins[a].at[:, pl.ds((1 - c) * hr, hr)], dst_ref=outs[a],
                send_sem=send_sem.at[a], recv_sem=recv_sem.at[a], device_id=(x, y, 1 - c), device_id_type=MESH))
        return res

    def start(ins, outs, sems):
        for cp in copies(ins, outs, sems):
            cp.start()

    def finish(ins, outs, sems):
        for cp in copies(ins, outs, sems):
            cp.wait()

    return _Ride(grads, [jax.ShapeDtypeStruct((g.shape[0], g.shape[1] // 2, g.shape[2]), g.dtype) for g in grads],
                 [pltpu.SemaphoreType.DMA((na,))] * 2, start, finish)


def _chip_ride(sums):
    na = len(sums)
    upper, lower = "upper half", "lower half"
    sent = [(("sum", "x", None), "x", ("got", 0, None)), (("sum", "y", None), "y", ("got", 1, None)),
            (("sum", "d", upper), "x", ("stage", 0, None)), (("sum", "d", lower), "y", ("stage", 1, None)),
            (("stage", 0, None), "y", ("got", 2, upper)), (("stage", 1, None), "x", ("got", 2, lower))]

    def copies(ins, outs, sems):
        send_sem, recv_sem = sems
        x, y, c, _ = _place()
        slot = {"x": 2 * (1 - x) + y, "y": 2 * x + (1 - y), "d": 2 * (1 - x) + (1 - y)}
        peer = {"x": (1 - x, y, c), "y": (x, 1 - y, c)}

        def ref(a, what):
            kind, k, part = what
            buf = {"sum": ins, "got": outs[:na], "stage": outs[na:]}[kind][a]
            whole = buf.at[slot[k] if kind == "sum" else k]
            if part is None:
                return whole
            h = whole.shape[0] // 2
            return whole.at[pl.ds(0 if part == upper else h, h)]

        return [[pltpu.make_async_remote_copy(
            src_ref=ref(a, src), dst_ref=ref(a, dst), send_sem=send_sem.at[a, k], recv_sem=recv_sem.at[a, k],
            device_id=peer[to], device_id_type=MESH) for k, (src, to, dst) in enumerate(sent)] for a in range(na)]

    def start(ins, outs, sems):
        for per_array in copies(ins, outs, sems):
            for cp in per_array[0:4]:
                cp.start()

    def middle(ins, outs, sems):
        for per_array in copies(ins, outs, sems):
            for k in (2, 3):
                per_array[k].wait_recv()
                per_array[k + 2].start()

    def finish(ins, outs, sems):
        for per_array in copies(ins, outs, sems):
            for k in (0, 1, 4, 5):
                per_array[k].wait_recv()
            for cp in per_array:
                cp.wait_send()

    return _Ride(sums, [jax.ShapeDtypeStruct((3,) + s.shape[1:], s.dtype) for s in sums]
                 + [jax.ShapeDtypeStruct((2, s.shape[1] // 2, s.shape[2]), s.dtype) for s in sums],
                 [pltpu.SemaphoreType.DMA((na, len(sent)))] * 2, start, finish, middle=middle)


def _finish_ride(grads, gpack=None):
    na = len(grads)

    def halves(outs, sems, which):
        x, y, c, _ = _place()
        res = []
        for a in range(na):
            hr = outs[a].shape[0] // 2
            rows = outs[a].at[pl.ds((c if which == "mine" else 1 - c) * hr, hr)]
            res.append(pltpu.make_async_remote_copy(
                src_ref=rows, dst_ref=rows, send_sem=sems[0].at[a], recv_sem=sems[1].at[a],
                device_id=(x, y, 1 - c), device_id_type=MESH))
        return res

    def gains(ins, outs, sems):
        x, y, c, _ = _place()
        dev = 4 * x + 2 * y + c
        g_in, g_out = ins[na], outs[na]
        own = pltpu.make_async_copy(g_in, g_out.at[dev], sems[2])
        sends, lands = [], []
        for k in range(N_DEV - 1):
            bx, by, bc = (k + 1) // 4, ((k + 1) // 2) % 2, (k + 1) % 2
            peer = (jnp.bitwise_xor(x, bx), jnp.bitwise_xor(y, by), jnp.bitwise_xor(c, bc))
            sends.append(pltpu.make_async_remote_copy(
                src_ref=g_in, dst_ref=g_out.at[dev], send_sem=sems[3].at[k], recv_sem=sems[4].at[k],
                device_id=peer, device_id_type=MESH))
            slot = g_out.at[jnp.bitwise_xor(dev, k + 1)]
            lands.append(pltpu.make_async_remote_copy(
                src_ref=slot, dst_ref=slot, send_sem=sems[3].at[k], recv_sem=sems[4].at[k],
                device_id=peer, device_id_type=MESH))
        return own, sends, lands

    def start(ins, outs, sems):
        for cp in halves(outs, sems, "mine"):
            cp.start()
        if gpack is not None:
            own, sends, _ = gains(ins, outs, sems)
            own.start()
            for cp in sends:
                cp.start()

    def finish(ins, outs, sems):
        for cp in halves(outs, sems, "sibling's"):
            cp.wait_recv()
        if gpack is not None:
            own, sends, lands = gains(ins, outs, sems)
            for cp in lands:
                cp.wait_recv()
            for cp in sends:
                cp.wait_send()
            own.wait()
        for cp in halves(outs, sems, "mine"):
            cp.wait_send()

    shapes = [jax.ShapeDtypeStruct(g.shape, g.dtype) for g in grads]
    sems = [pltpu.SemaphoreType.DMA((na,))] * 2
    if gpack is None:
        return _Ride(grads, shapes, sems, start, finish, {a: a for a in range(na)})
    return _Ride(list(grads) + [gpack], shapes + [jax.ShapeDtypeStruct((N_DEV,) + gpack.shape, gpack.dtype)],
                 sems + [pltpu.SemaphoreType.DMA, pltpu.SemaphoreType.DMA((N_DEV - 1,)), pltpu.SemaphoreType.DMA((N_DEV - 1,))],
                 start, finish, {a: a for a in range(na)})


def _pair_sum(place, grad, got, name):
    ns, r, cols = grad.shape
    hr = r // 2

    def body(place_ref, g_ref, r_ref, o_ref):
        o_ref[...] = (g_ref[...] + r_ref[...]).astype(BF16)

    return pl.pallas_call(
        body, name=name,
        grid_spec=pltpu.PrefetchScalarGridSpec(
            num_scalar_prefetch=1, grid=(ns,),
            in_specs=[pl.BlockSpec((None, hr, cols), lambda s, pr: (s, pr[1], 0)),
                      pl.BlockSpec((None, hr, cols), lambda s, pr: (s, 0, 0))],
            out_specs=pl.BlockSpec((None, hr, cols), lambda s, pr: (s, 0, 0))),
        out_shape=jax.ShapeDtypeStruct((ns, hr, cols), BF16),
        compiler_params=_params("arbitrary"),
    )(place, grad, got)


def _chip_sum(place, grad, got, others, name):
    ns, r, cols = grad.shape
    hr = r // 2
    nb = 2
    tr = hr // nb

    def body(place_ref, g_ref, r_ref, o3_ref, o_ref):
        acc = g_ref[...] + r_ref[...]
        for j in range(3):
            acc = acc + o3_ref[j].astype(F32)
        o_ref[...] = acc

    return pl.pallas_call(
        body, name=name,
        grid_spec=pltpu.PrefetchScalarGridSpec(
            num_scalar_prefetch=1, grid=(nb,),
            in_specs=[pl.BlockSpec((None, tr, cols), lambda i, pr: (pr[0], pr[1] * nb + i, 0)),
                      pl.BlockSpec((None, tr, cols), lambda i, pr: (pr[0], i, 0)),
                      pl.BlockSpec((3, tr, cols), lambda i, pr: (0, i, 0))],
            out_specs=pl.BlockSpec((tr, cols), lambda i, pr: (pr[1] * nb + i, 0))),
        out_shape=jax.ShapeDtypeStruct((r, cols), F32),
        compiler_params=_params("arbitrary"),
    )(place, grad, got, others)


def _pack_gains(parts, d):
    def body(*refs):
        ins, o_ref = refs[:-1], refs[-1]
        o_ref[...] = jnp.zeros_like(o_ref)
        for k, r in enumerate(ins):
            o_ref[k:k + 1, 0:r.shape[1]] = jnp.sum(r[...], axis=0, keepdims=True)

    return pl.pallas_call(
        body, name="pack_gains", out_shape=jax.ShapeDtypeStruct((GAIN_ROWS, d), F32),
    )(*parts)


def _adamw_math(w, g, m, v):
    m = ADAM_B1 * m + (1.0 - ADAM_B1) * g
    v = ADAM_B2 * v + (1.0 - ADAM_B2) * jnp.square(g)
    m_hat = m / (1.0 - ADAM_B1 ** ADAM_STEP)
    v_hat = v / (1.0 - ADAM_B2 ** ADAM_STEP)
    return -ADAM_LR * (m_hat / (jnp.sqrt(v_hat) + ADAM_EPS) + ADAM_WD * w), m, v


def _adamw(ws, gs, ms, vs):
    n = len(ws)

    def body(*refs):
        ins, outs = refs[:4 * n], refs[4 * n:]
        for k in range(n):
            w_ref, g_ref, m_ref, v_ref = ins[4 * k:4 * k + 4]
            go_ref, d_ref, nm_ref, nv_ref = outs[4 * k:4 * k + 4]
            g = g_ref[...]
            go_ref[...] = g
            d_ref[...], nm_ref[...], nv_ref[...] = _adamw_math(w_ref[...], g, m_ref[...], v_ref[...])

    parts = 2 * ROW_QUARTERS
    tile = lambda w: pl.BlockSpec((w.shape[0] // parts, w.shape[1]), lambda i: (i, 0))
    res = pl.pallas_call(
        body, name="adamw_shards", grid=(parts,),
        in_specs=[tile(w) for w in ws for _ in range(4)], out_specs=[tile(w) for w in ws for _ in range(4)],
        out_shape=[jax.ShapeDtypeStruct(w.shape, F32) for w in ws for _ in range(4)],
        compiler_params=_params("arbitrary"),
    )(*[a for quad in zip(ws, gs, ms, vs) for a in quad])
    return [res[4 * k:4 * k + 4] for k in range(n)]


def _adamw_gain(gall, row, w, m, v, name):
    n = w.shape[1]

    def body(ga_ref, w_ref, m_ref, v_ref, g_ref, d_ref, nm_ref, nv_ref):
        g = ga_ref[0, row:row + 1, 0:n]
        for k in range(1, N_DEV):
            g = g + ga_ref[k, row:row + 1, 0:n]
        g_ref[...] = g
        d_ref[...], nm_ref[...], nv_ref[...] = _adamw_math(w_ref[...], g, m_ref[...], v_ref[...])

    return pl.pallas_call(
        body, name=name, out_shape=[jax.ShapeDtypeStruct((1, n), F32)] * 4,
    )(gall, w, m, v)


def kernel(x, norm_ffn1, ffn1_w_gate, ffn1_w_up, ffn1_w_down, norm_mix, w_in, ret_norm_gain, w_out, norm_ffn2, ffn2_w_gate, ffn2_w_up, ffn2_w_down, norm_final, loss_target, m_norm_ffn1, m_ffn1_w_gate, m_ffn1_w_up, m_ffn1_w_down, m_norm_mix, m_w_in, m_ret_norm_gain, m_w_out, m_norm_ffn2, m_ffn2_w_gate, m_ffn2_w_up, m_ffn2_w_down, m_norm_final, v_norm_ffn1, v_ffn1_w_gate, v_ffn1_w_up, v_ffn1_w_down, v_norm_mix, v_w_in, v_ret_norm_gain, v_w_out, v_norm_ffn2, v_ffn2_w_gate, v_ffn2_w_up, v_ffn2_w_down, v_norm_final):
    d = x.shape[-1]
    mats = [ffn1_w_gate, ffn1_w_up, ffn1_w_down, w_in, w_out, ffn2_w_gate, ffn2_w_up, ffn2_w_down]
    mats_m = [m_ffn1_w_gate, m_ffn1_w_up, m_ffn1_w_down, m_w_in, m_w_out, m_ffn2_w_gate, m_ffn2_w_up, m_ffn2_w_down]
    mats_v = [v_ffn1_w_gate, v_ffn1_w_up, v_ffn1_w_down, v_w_in, v_w_out, v_ffn2_w_gate, v_ffn2_w_up, v_ffn2_w_down]
    mat_names = ["ffn1_w_gate", "ffn1_w_up", "ffn1_w_down", "w_in", "w_out", "ffn2_w_gate", "ffn2_w_up", "ffn2_w_down"]
    gains = [norm_ffn1, norm_mix, ret_norm_gain, norm_ffn2, norm_final.reshape(1, d)]
    gains_m = [m_norm_ffn1, m_norm_mix, m_ret_norm_gain, m_norm_ffn2, m_norm_final.reshape(1, d)]
    gains_v = [v_norm_ffn1, v_norm_mix, v_ret_norm_gain, v_norm_ffn2, v_norm_final.reshape(1, d)]
    gain_names = ["norm_ffn1", "norm_mix", "ret_norm_gain", "norm_ffn2", "norm_final"]

    turned = lambda n: n.endswith(("w_gate", "w_up"))
    local = lambda a, n: jnp.swapaxes(a, 1, 2)[0] if turned(n) else a[0]
    back = lambda a, n: jnp.swapaxes(a[None], 1, 2) if turned(n) else a[None]
    shards = [local(w, n) for w, n in zip(mats, mat_names)]
    place = jnp.stack([2 * lax.axis_index("x") + lax.axis_index("y"), lax.axis_index("c")]).astype(jnp.int32)
    placed = _place_shards(place, shards)
    loss_p, dx, shard_grads, gall = _step(x[0], loss_target[0], gains, placed, place)

    out_g, out_d, out_m, out_v = {}, {}, {}, {}
    updates = _adamw(shards, shard_grads, [local(m, n) for m, n in zip(mats_m, mat_names)],
                     [local(v, n) for v, n in zip(mats_v, mat_names)])
    for n, quad in zip(mat_names, updates):
        out_g[n], out_d[n], out_m[n], out_v[n] = [back(a, n) for a in quad]
    for row, (n, w, m, v) in enumerate(zip(gain_names, gains, gains_m, gains_v)):
        res = _adamw_gain(gall, row, w, m, v, f"adamw_{n}")
        shape = (d,) if n == "norm_final" else w.shape
        out_g[n], out_d[n], out_m[n], out_v[n] = [r.reshape(shape) for r in res]

    loss = lax.psum(jnp.sum(loss_p), ("x", "y", "c"))
    order = ["norm_ffn1", "ffn1_w_gate", "ffn1_w_up", "ffn1_w_down", "norm_mix", "w_in", "ret_norm_gain", "w_out",
             "norm_ffn2", "ffn2_w_gate", "ffn2_w_up", "ffn2_w_down", "norm_final"]
    return (loss, dx[None], *[out_g[n] for n in order], *[out_d[n] for n in order],
            *[out_m[n] for n in order], *[out_v[n] for n in order])
```

```python
import functools

import jax
import jax.numpy as jnp
from jax import lax
from jax.experimental import pallas as pl
from jax.experimental.pallas import tpu as pltpu

F32 = jnp.float32
BF16 = jnp.bfloat16
MESH = pl.DeviceIdType.MESH

NORM_EPS = 1e-6
GN_EPS = 1e-6
ROPE_BASE = 10000.0
RET_HEADS = 4
RET_DIM = 128
RET_WIDTH = 512
RET_CHUNK = 128
ATT_DIM = 64
ATT_WIDTH = 512
ATT_BLOCK = 128
DILATIONS = (1, 4, 16)
LANE = 128
N_SHARD = 4
ADAM_LR, ADAM_B1, ADAM_B2, ADAM_EPS, ADAM_WD, ADAM_STEP = 0.001, 0.9, 0.999, 1e-08, 0.01, 10

V7X_VMEM_BYTES = 64 * 1024 * 1024
VMEM_LIMIT = V7X_VMEM_BYTES - 8 * 1024 * 1024

NT = (((1,), (1,)), ((), ()))
TN = (((0,), (0,)), ((), ()))


def _params(*sem):
    return pltpu.CompilerParams(dimension_semantics=sem, vmem_limit_bytes=VMEM_LIMIT)


def _dot(a, b, dims=None):
    if dims is None:
        return jnp.dot(a, b, preferred_element_type=F32)
    return lax.dot_general(a, b, dims, preferred_element_type=F32)


def _sigmoid(x):
    return 1.0 / (1.0 + jnp.exp(-x))


def _load_weights(pairs, sems):
    copies = [pltpu.make_async_copy(src, dst, sems.at[k]) for k, (src, dst) in enumerate(pairs)]
    for cp in copies:
        cp.start()
    for cp in copies:
        cp.wait()


def _rows8(v):
    r, c = v.shape
    return v.reshape(r // 8, 8, c).sum(axis=0)


class _Ride:
    def __init__(self, inputs, out_shapes, sems, start, finish, aliases=None, middle=None):
        self.inputs, self.out_shapes, self.sems = list(inputs), list(out_shapes), list(sems)
        self.start, self.middle, self.finish, self.aliases = start, middle, finish, dict(aliases or {})


def _pallas(body, rides, *, name, in_specs, out_specs, out_shape, args, grid=(), scratch_shapes=(), sem=()):
    rides = [r for r in (rides or []) if r is not None]
    n_in, n_out, n_scr = len(args), len(out_shape), len(scratch_shapes)
    hbm = pl.BlockSpec(memory_space=pl.ANY)
    r_in = [a for r in rides for a in r.inputs]
    r_out = [s for r in rides for s in r.out_shapes]
    r_sem = [s for r in rides for s in r.sems]
    aliases, spans, ki, ko, ks = {}, [], 0, 0, 0
    for r in rides:
        aliases.update({n_in + ki + i: n_out + ko + o for i, o in r.aliases.items()})
        spans.append((ki, ko, ks))
        ki, ko, ks = ki + len(r.inputs), ko + len(r.out_shapes), ks + len(r.sems)

    def wrapped(*refs):
        ins, rin = refs[:n_in], refs[n_in:n_in + len(r_in)]
        o0 = n_in + len(r_in)
        outs, rout = refs[o0:o0 + n_out], refs[o0 + n_out:o0 + n_out + len(r_out)]
        s0 = o0 + n_out + len(r_out)
        scr, rsem = refs[s0:s0 + n_scr], refs[s0 + n_scr:]
        part = lambda r, k: (rin[spans[k][0]:spans[k][0] + len(r.inputs)], rout[spans[k][1]:spans[k][1] + len(r.out_shapes)],
                             rsem[spans[k][2]:spans[k][2] + len(r.sems)])
        first = functools.reduce(jnp.logical_and, [pl.program_id(k) == 0 for k in range(len(grid))], True)
        last = functools.reduce(jnp.logical_and, [pl.program_id(k) == grid[k] - 1 for k in range(len(grid))], True)
        if rides:
            @pl.when(first)
            def _():
                for k, r in enumerate(rides):
                    r.start(*part(r, k))

        if any(r.middle for r in rides):
            halfway = functools.reduce(jnp.logical_and, [pl.program_id(k) == 0 for k in range(1, len(grid))],
                                       pl.program_id(0) == grid[0] // 2)

            @pl.when(halfway)
            def _():
                for k, r in enumerate(rides):
                    if r.middle:
                        r.middle(*part(r, k))

        body(*ins, *outs, *scr)
        if rides:
            @pl.when(last)
            def _():
                for k, r in enumerate(rides):
                    r.finish(*part(r, k))

    res = pl.pallas_call(
        wrapped, name=name, grid=grid,
        in_specs=list(in_specs) + [hbm] * len(r_in), out_specs=list(out_specs) + [hbm] * len(r_out),
        out_shape=list(out_shape) + r_out, input_output_aliases=aliases,
        scratch_shapes=list(scratch_shapes) + r_sem,
        compiler_params=pltpu.CompilerParams(dimension_semantics=sem, vmem_limit_bytes=VMEM_LIMIT) if grid else None,
    )(*args, *r_in)
    extras = [list(res[n_out + ko:n_out + ko + len(r.out_shapes)]) for r, (_, ko, _) in zip(rides, spans)]
    return list(res[:n_out]), extras


def _run(ride, name):
    def body(*refs):
        n_in, n_out = len(ride.inputs), len(ride.out_shapes)
        parts = refs[:n_in], refs[n_in:n_in + n_out], refs[n_in + n_out:]
        ride.start(*parts)
        if ride.middle:
            ride.middle(*parts)
        ride.finish(*parts)

    hbm = pl.BlockSpec(memory_space=pl.ANY)
    return list(pl.pallas_call(
        body, name=name, in_specs=[hbm] * len(ride.inputs), out_specs=[hbm] * len(ride.out_shapes),
        out_shape=ride.out_shapes, input_output_aliases=ride.aliases, scratch_shapes=ride.sems,
    )(*ride.inputs))


def _loss_head(hv, gain_ref, tg_ref, loss_ref, dgain_ref):
    d = hv.shape[1]
    r = lax.rsqrt(jnp.mean(hv * hv, axis=-1, keepdims=True) + NORM_EPS)
    xh = hv * r
    err = xh * gain_ref[...] - tg_ref[...]
    sq = _rows8(jnp.square(err))
    loss_ref[...] += 0.5 * functools.reduce(jnp.add, [sq[:, k * LANE:(k + 1) * LANE] for k in range(d // LANE)]) / d
    dy = err / d
    dgain_ref[...] += _rows8(dy * xh)
    dxh = dy * gain_ref[...]
    return r * (dxh - xh * jnp.mean(dxh * xh, axis=-1, keepdims=True))


V7X_MXU_TILE = 256
FFN_FWD_CHUNK_TILES = 3
FFN_BWD_CHUNK_TILES = 4


def _hidden_chunks(f, tiles):
    step = tiles * V7X_MXU_TILE
    return [slice(s, min(s + step, f)) for s in range(0, f, step)]


def _flat(w):
    return w.reshape(w.shape[0] * w.shape[1], w.shape[2])


def _ffn_fwd(x, gain, wg, wu, wd, name, rides=None, head=None):
    t, d = x.shape
    wg, wu, wd = _flat(wg), _flat(wu), _flat(wd)
    f = wg.shape[0]
    tm = min(512, t)
    nh = 0 if head is None else 2

    def body(*refs):
        x_ref, gain_ref = refs[:2]
        wg_hbm, wu_hbm, wd_hbm, h_ref, xn_ref, g_ref, u_ref, a_ref = refs[2 + nh:10 + nh]
        sums = refs[10 + nh:12 + nh]
        wg_v, wu_v, wd_v, sems = refs[-4:]

        @pl.when(pl.program_id(0) == 0)
        def _():
            _load_weights([(wg_hbm, wg_v), (wu_hbm, wu_v), (wd_hbm, wd_v)], sems)
            if head is not None:
                for s_ref in sums:
                    s_ref[...] = jnp.zeros_like(s_ref)

        xv = x_ref[...]
        r = lax.rsqrt(jnp.mean(xv * xv, axis=-1, keepdims=True) + NORM_EPS)
        xn = (xv * r * gain_ref[...]).astype(BF16)
        xn_ref[...] = xn
        acc = jnp.zeros((tm, d), F32)
        for c in _hidden_chunks(f, FFN_FWD_CHUNK_TILES):
            g = _dot(xn, wg_v[c, :], NT)
            u = _dot(xn, wu_v[c, :], NT)
            g_ref[:, c] = g.astype(BF16)
            u_ref[:, c] = u.astype(BF16)
            a = (g * _sigmoid(g) * u).astype(BF16)
            a_ref[:, c] = a
            acc = acc + _dot(a, wd_v[c, :])
        hv = xv + 0.5 * acc
        h_ref[...] = hv if head is None else _loss_head(hv, refs[2], refs[3], *sums)

    hbm = pl.BlockSpec(memory_space=pl.ANY)
    hid = pl.BlockSpec((tm, f), lambda i: (i, 0))
    tile = pl.BlockSpec((tm, d), lambda i: (i, 0))
    row = pl.BlockSpec((1, d), lambda i: (0, 0))
    sums = [] if head is None else [(pl.BlockSpec((8, LANE), lambda i: (0, 0)), jax.ShapeDtypeStruct((8, LANE), F32)),
                                    (pl.BlockSpec((8, d), lambda i: (0, 0)), jax.ShapeDtypeStruct((8, d), F32))]
    return _pallas(
        body, rides, name=name, grid=(t // tm,),
        in_specs=[tile, row] + ([] if head is None else [row, tile]) + [hbm, hbm, hbm],
        out_specs=[tile, tile, hid, hid, hid] + [s for s, _ in sums],
        out_shape=[jax.ShapeDtypeStruct((t, d), F32), jax.ShapeDtypeStruct((t, d), BF16)]
        + [jax.ShapeDtypeStruct((t, f), BF16)] * 3 + [s for _, s in sums],
        scratch_shapes=[pltpu.VMEM(wg.shape, BF16), pltpu.VMEM(wu.shape, BF16), pltpu.VMEM(wd.shape, BF16),
                        pltpu.SemaphoreType.DMA((3,))],
        sem=("arbitrary",), args=[x, gain] + ([] if head is None else list(head)) + [wg, wu, wd])


def _ffn_bwd_data(dy, x, gain, g, u, wg, wu, wd, name, rides=None):
    t, d = x.shape
    wg, wu, wd = _flat(wg), _flat(wu), _flat(wd)
    f = wg.shape[0]
    tm = min(256, t)

    def body(dy_ref, x_ref, gain_ref, g_ref, u_ref, wg_hbm, wu_hbm, wd_hbm, dx_ref, dg_ref, du_ref, dgain_ref,
             wg_v, wu_v, wd_v, sems):
        @pl.when(pl.program_id(0) == 0)
        def _():
            _load_weights([(wg_hbm, wg_v), (wu_hbm, wu_v), (wd_hbm, wd_v)], sems)
            dgain_ref[...] = jnp.zeros_like(dgain_ref)

        dyv = dy_ref[...]
        dyh = (0.5 * dyv).astype(BF16)
        dxn = jnp.zeros((tm, d), F32)
        chunks = _hidden_chunks(f, FFN_BWD_CHUNK_TILES)
        das = [_dot(dyh, wd_v[c, :], NT) for c in chunks]
        for c, da in zip(chunks, das):
            gj = g_ref[:, c].astype(F32)
            uj = u_ref[:, c].astype(F32)
            sig = _sigmoid(gj)
            dgj = (da * uj * (sig * (1.0 + gj * (1.0 - sig)))).astype(BF16)
            duj = (da * (gj * sig)).astype(BF16)
            dg_ref[:, c] = dgj
            du_ref[:, c] = duj
            dxn = dxn + _dot(dgj, wg_v[c, :]) + _dot(duj, wu_v[c, :])
        xv = x_ref[...]
        r = lax.rsqrt(jnp.mean(xv * xv, axis=-1, keepdims=True) + NORM_EPS)
        xh = xv * r
        dgain_ref[...] += _rows8(dxn * xh)
        dxh = dxn * gain_ref[...]
        dx_ref[...] = dyv + r * (dxh - xh * jnp.mean(dxh * xh, axis=-1, keepdims=True))

    hbm = pl.BlockSpec(memory_space=pl.ANY)
    tile = pl.BlockSpec((tm, d), lambda i: (i, 0))
    hid = pl.BlockSpec((tm, f), lambda i: (i, 0))
    return _pallas(
        body, rides, name=name, grid=(t // tm,),
        in_specs=[tile, tile, pl.BlockSpec((1, d), lambda i: (0, 0)), hid, hid, hbm, hbm, hbm],
        out_specs=[tile, hid, hid, pl.BlockSpec((8, d), lambda i: (0, 0))],
        out_shape=[jax.ShapeDtypeStruct((t, d), F32), jax.ShapeDtypeStruct((t, f), BF16),
                   jax.ShapeDtypeStruct((t, f), BF16), jax.ShapeDtypeStruct((8, d), F32)],
        scratch_shapes=[pltpu.VMEM(wg.shape, BF16), pltpu.VMEM(wu.shape, BF16), pltpu.VMEM(wd.shape, BF16),
                        pltpu.SemaphoreType.DMA((3,))],
        sem=("arbitrary",), args=[dy, x, gain, g, u, wg, wu, wd])


WGRAD_ROW_BLOCKS = 2


def _ffn_wgrad_down(a, dy, name, rides=None):
    t, d = dy.shape
    f = a.shape[1]
    fb = f // WGRAD_ROW_BLOCKS
    tk = min(1024, t)

    def body(dy_ref, a_ref, dwd_ref):
        @pl.when(pl.program_id(1) == 0)
        def _():
            dwd_ref[...] = jnp.zeros_like(dwd_ref)

        dwd_ref[...] += _dot(a_ref[...], (0.5 * dy_ref[...]).astype(BF16), TN)

    return _pallas(
        body, rides, name=name, grid=(WGRAD_ROW_BLOCKS, t // tk),
        in_specs=[pl.BlockSpec((tk, d), lambda j, k: (k, 0)), pl.BlockSpec((tk, fb), lambda j, k: (k, j))],
        out_specs=[pl.BlockSpec((fb, d), lambda j, k: (j, 0))],
        out_shape=[jax.ShapeDtypeStruct((f, d), F32)],
        sem=("arbitrary", "arbitrary"), args=[dy, a])


def _ffn_wgrad_gu(xn, dhs, name, rides=None):
    t, d = xn.shape
    n = len(dhs)
    f = dhs[0].shape[1]
    fb = f // WGRAD_ROW_BLOCKS
    tk = min(2048 // n, t)

    def body(xn_ref, *refs):
        @pl.when(pl.program_id(1) == 0)
        def _():
            for o_ref in refs[n:]:
                o_ref[...] = jnp.zeros_like(o_ref)

        xnv = xn_ref[...]
        for dh_ref, o_ref in zip(refs[:n], refs[n:]):
            o_ref[...] += _dot(dh_ref[...], xnv, TN)

    hid = pl.BlockSpec((tk, fb), lambda j, k: (k, j))
    out = pl.BlockSpec((fb, d), lambda j, k: (j, 0))
    return _pallas(
        body, rides, name=name, grid=(WGRAD_ROW_BLOCKS, t // tk),
        in_specs=[pl.BlockSpec((tk, d), lambda j, k: (k, 0))] + [hid] * n,
        out_specs=[out] * n, out_shape=[jax.ShapeDtypeStruct((f, d), F32)] * n,
        sem=("arbitrary", "arbitrary"), args=[xn] + list(dhs))


def _tn_matmul(a, b, bn, name):
    t, m = a.shape
    n = b.shape[1]
    tk = min(2048, t)

    def body(a_ref, b_ref, o_ref):
        @pl.when(pl.program_id(1) == 0)
        def _():
            o_ref[...] = jnp.zeros_like(o_ref)

        o_ref[...] += _dot(a_ref[...].astype(BF16), b_ref[...].astype(BF16), TN)

    return pl.pallas_call(
        body, name=name, grid=(n // bn, t // tk),
        in_specs=[pl.BlockSpec((tk, m), lambda j, k: (k, 0)), pl.BlockSpec((tk, bn), lambda j, k: (k, j))],
        out_specs=pl.BlockSpec((None, m, bn), lambda j, k: (j, 0, 0)),
        out_shape=jax.ShapeDtypeStruct((n // bn, m, bn), F32),
        compiler_params=_params("arbitrary", "arbitrary"),
    )(a, b)


def _chunk_scratch(tm, w):
    return pltpu.VMEM((w // LANE, tm, LANE), F32)


def _regroup_store(cbuf, out_ref, dil, chunks=None):
    n = out_ref.shape[1]
    for k in range(cbuf.shape[0]) if chunks is None else chunks:
        for g in range(dil):
            rows = cbuf[k] if dil == 1 else cbuf[k, pl.ds(g, n, stride=dil), :]
            out_ref[g, :, k * LANE:(k + 1) * LANE] = rows.astype(out_ref.dtype)


def _natural_rows(ref, dil, cbuf):
    if dil == 1:
        return ref[0].astype(F32)
    n = ref.shape[1]
    for g in range(dil):
        for k in range(cbuf.shape[0]):
            cbuf[k, pl.ds(g, n, stride=dil), :] = ref[g, :, k * LANE:(k + 1) * LANE].astype(F32)
    return jnp.concatenate([cbuf[k] for k in range(cbuf.shape[0])], axis=1)


IN_CHUNK_TILES = 4


def _column_chunks(n):
    step = IN_CHUNK_TILES * V7X_MXU_TILE
    return [slice(s, min(s + step, n)) for s in range(0, n, step)]


def _load_side_by_side(w_hbm, w_v, sems):
    ns, _, cs = w_hbm.shape
    copies = [pltpu.make_async_copy(w_hbm.at[j], w_v.at[:, pl.ds(j * cs, cs)], sems.at[j]) for j in range(ns)]
    for cp in copies:
        cp.start()
    for cp in copies:
        cp.wait()


def _inproj_fwd(h, gain, win):
    t, d = h.shape
    ns, _, cs = win.shape
    tm = min(512, t)
    rw, aw = 4 * RET_WIDTH, 3 * ATT_WIDTH

    def body(h_ref, gain_ref, w_hbm, xn_ref, ur_ref, *rest):
        a_refs, abuf, w_v, sems = rest[:-3], rest[-3], rest[-2], rest[-1]

        @pl.when(pl.program_id(0) == 0)
        def _():
            _load_side_by_side(w_hbm, w_v, sems)

        hv = h_ref[...]
        r = lax.rsqrt(jnp.mean(hv * hv, axis=-1, keepdims=True) + NORM_EPS)
        xn = (hv * r * gain_ref[...]).astype(BF16)
        xn_ref[...] = xn
        for c in reversed(_column_chunks(ns * cs)):
            res = _dot(xn, w_v[:, c])
            mine = []
            for k in range((c.stop - c.start) // LANE):
                chunk = c.start // LANE + k
                piece = res[:, k * LANE:(k + 1) * LANE]
                if chunk < rw // LANE:
                    ur_ref[:, chunk * LANE:(chunk + 1) * LANE] = piece
                else:
                    abuf[chunk - rw // LANE] = piece
                    mine.append(chunk - rw // LANE)
            for dil, a_ref in zip(DILATIONS, a_refs):
                _regroup_store(abuf, a_ref, dil, mine)

    return pl.pallas_call(
        body, name="inproj_fwd", grid=(t // tm,),
        in_specs=[pl.BlockSpec((tm, d), lambda i: (i, 0)), pl.BlockSpec((1, d), lambda i: (0, 0)),
                  pl.BlockSpec(memory_space=pl.ANY)],
        out_specs=[pl.BlockSpec((tm, d), lambda i: (i, 0)), pl.BlockSpec((tm, rw), lambda i: (i, 0))]
        + [pl.BlockSpec((dil, tm // dil, aw), lambda i: (0, i, 0)) for dil in DILATIONS],
        out_shape=[jax.ShapeDtypeStruct((t, d), BF16), jax.ShapeDtypeStruct((t, rw), F32)]
        + [jax.ShapeDtypeStruct((dil, t // dil, aw), BF16) for dil in DILATIONS],
        scratch_shapes=[_chunk_scratch(tm, aw), pltpu.VMEM((d, ns * cs), BF16), pltpu.SemaphoreType.DMA((ns,))],
        compiler_params=_params("arbitrary"),
    )(h, gain, win)


def _inproj_bwd(pieces, parts, h, gain, dres, win):
    t, d = h.shape
    ns, _, cs = win.shape
    pw = pieces[0].shape[1]
    tm = min(512, t)
    npc, nk = len(pieces), len(parts[0])
    flat_parts = [a for p in parts for a in p]

    def body(*refs):
        p_refs, a_refs = refs[:npc], refs[npc:npc + len(flat_parts)]
        h_ref, gain_ref, dres_ref, w_hbm, dh_ref, du_ref, dgain_ref, buf, w_v, sems = refs[npc + len(flat_parts):]

        @pl.when(pl.program_id(0) == 0)
        def _():
            _load_side_by_side(w_hbm, w_v, sems)
            dgain_ref[...] = jnp.zeros_like(dgain_ref)

        for k in range(npc):
            du_ref[:, k * pw:(k + 1) * pw] = p_refs[k][...]
        for k in range(nk):
            acc = None
            for b, dil in enumerate(DILATIONS):
                rows = _natural_rows(a_refs[b * nk + k], dil, buf)
                acc = rows if acc is None else acc + rows
            du_ref[:, (npc + k) * pw:(npc + k + 1) * pw] = acc.astype(BF16)
        dxn = jnp.zeros((tm, d), F32)
        for c in _column_chunks(ns * cs):
            dxn = dxn + _dot(du_ref[:, c], w_v[:, c], NT)
        hv = h_ref[...]
        r = lax.rsqrt(jnp.mean(hv * hv, axis=-1, keepdims=True) + NORM_EPS)
        xh = hv * r
        dgain_ref[...] += _rows8(dxn * xh)
        dxh = dxn * gain_ref[...]
        dh_ref[...] = dres_ref[...] + r * (dxh - xh * jnp.mean(dxh * xh, axis=-1, keepdims=True))

    tile = pl.BlockSpec((tm, d), lambda i: (i, 0))
    cols = (npc + nk) * pw
    return pl.pallas_call(
        body, name="inproj_bwd", grid=(t // tm,),
        in_specs=[pl.BlockSpec((tm, pw), lambda i: (i, 0))] * npc
        + [_regrouped_spec(tm, dil, pw) for dil in DILATIONS for _ in range(nk)]
        + [tile, pl.BlockSpec((1, d), lambda i: (0, 0)), tile, pl.BlockSpec(memory_space=pl.ANY)],
        out_specs=[tile, pl.BlockSpec((tm, cols), lambda i: (i, 0)), pl.BlockSpec((8, d), lambda i: (0, 0))],
        out_shape=[jax.ShapeDtypeStruct((t, d), F32), jax.ShapeDtypeStruct((t, cols), BF16),
                   jax.ShapeDtypeStruct((8, d), F32)],
        scratch_shapes=[_chunk_scratch(tm, pw), pltpu.VMEM((d, ns * cs), BF16), pltpu.SemaphoreType.DMA((ns,))],
        compiler_params=_params("arbitrary"),
    )(*pieces, *flat_parts, h, gain, dres, win)


def _outproj_fwd(h, mix_r, mix_a, wo):
    t, d = h.shape
    hw = mix_r.shape[1]
    tm = min(512, t)

    def body(h_ref, mr_ref, ma_ref, w_ref, o_ref):
        o_ref[...] = h_ref[...] + _dot(mr_ref[...], w_ref[0:hw, :]) + _dot(ma_ref[...], w_ref[hw:2 * hw, :])

    tile = pl.BlockSpec((tm, d), lambda i: (i, 0))
    half = pl.BlockSpec((tm, hw), lambda i: (i, 0))
    return pl.pallas_call(
        body, name="outproj_fwd", grid=(t // tm,),
        in_specs=[tile, half, half, pl.BlockSpec(wo.shape, lambda i: (0, 0))],
        out_specs=tile, out_shape=jax.ShapeDtypeStruct((t, d), F32),
        compiler_params=_params("arbitrary"),
    )(h, mix_r, mix_a, wo)


def _outproj_bwd(dh, wo, rides=None):
    t, d = dh.shape
    hw = wo.shape[0] // 2
    tm = min(512, t)

    def body(dh_ref, w_ref, dr_ref, da_ref):
        dhb = dh_ref[...].astype(BF16)
        dr_ref[...] = _dot(dhb, w_ref[0:hw, :], NT)
        da_ref[...] = _dot(dhb, w_ref[hw:2 * hw, :], NT)

    half = pl.BlockSpec((tm, hw), lambda i: (i, 0))
    return _pallas(
        body, rides, name="outproj_bwd", grid=(t // tm,),
        in_specs=[pl.BlockSpec((tm, d), lambda i: (i, 0)), pl.BlockSpec(wo.shape, lambda i: (0, 0))],
        out_specs=[half, half],
        out_shape=[jax.ShapeDtypeStruct((t, hw), F32), jax.ShapeDtypeStruct((t, hw), F32)],
        sem=("arbitrary",), args=[dh, wo])


def _retention_tables(t):
    pos = jnp.arange(t, dtype=F32)
    pair = (jnp.arange(RET_DIM) // 2 * 2).astype(F32)
    ang = pos[:, None] * (ROPE_BASE ** (-pair / RET_DIM))[None, :]
    c = RET_CHUNK
    log_g = jnp.log(1.0 - 2.0 ** (-5.0 - jnp.arange(RET_HEADS, dtype=F32)))
    idx = jnp.arange(c, dtype=F32)
    rel = idx[:, None] - idx[None, :]
    decay = jnp.where(rel >= 0, jnp.exp(log_g[:, None, None] * jnp.maximum(rel, 0.0)), 0.0)
    zeta = jnp.exp(log_g[:, None] * (c - 1 - idx)[None, :])
    xi = jnp.exp(log_g[:, None] * (idx + 1)[None, :])
    gc = jnp.exp(log_g * c)
    wide = lambda v: jnp.broadcast_to(v[:, :, None], (RET_HEADS, c, LANE))
    return (jnp.cos(ang), jnp.sin(ang), decay, wide(zeta), wide(xi),
            jnp.broadcast_to(gc[:, None, None], (RET_HEADS, c, LANE)))


def _rot(v):
    lane = lax.broadcasted_iota(jnp.int32, v.shape, 1)
    nxt = pltpu.roll(v, LANE - 1, 1)
    prv = pltpu.roll(v, 1, 1)
    return jnp.where(lane % 2 == 0, -nxt, prv)


def _ret_specs(tr, rev, nt):
    ti = (lambda i: nt - 1 - i) if rev else (lambda i: i)
    col = lambda blk: pl.BlockSpec((tr, RET_WIDTH), lambda i: (ti(i), blk))
    tab = pl.BlockSpec((tr, LANE), lambda i: (ti(i), 0))
    head = pl.BlockSpec((RET_HEADS, RET_CHUNK, LANE), lambda i: (0, 0, 0))
    return col, tab, head


def _ret_chunks(tr, rev=False):
    order = list(range(tr // RET_CHUNK))
    return [(pl.ds(ci * RET_CHUNK, RET_CHUNK), slice(h * RET_DIM, (h + 1) * RET_DIM), h)
            for h in range(RET_HEADS) for ci in (reversed(order) if rev else order)]


def _ret_operands(items, q_ref, k_ref, v_ref, cos_ref, sin_ref, zeta_ref):
    scale = RET_DIM ** -0.5
    qbs, kbs, vbs, kzs = [], [], [], []
    for sl, hs, h in items:
        cs, sn = cos_ref[sl, :], sin_ref[sl, :]
        q, k = q_ref[sl, hs], k_ref[sl, hs]
        kr = (k * cs + _rot(k) * sn) * scale
        qbs.append((q * cs + _rot(q) * sn).astype(BF16))
        kbs.append(kr.astype(BF16))
        vbs.append(v_ref[sl, hs].astype(BF16))
        kzs.append((kr * zeta_ref[h]).astype(BF16))
    return qbs, kbs, vbs, kzs


def _ret_states(items, state, steps, gc_ref):
    cur, befores = {}, []
    for (sl, hs, h), step in zip(items, steps):
        st = cur[h] if h in cur else state[h]
        befores.append(st)
        cur[h] = st * gc_ref[h] + step
    for h, st in cur.items():
        state[h] = st
    return befores


def _ret_fwd(u, gain, tabs):
    t = u.shape[0]
    tr = min(512, t)
    nt = t // tr
    cos, sin, decay, zeta, xi, gc = tabs

    def body(q_ref, k_ref, v_ref, gt_ref, cos_ref, sin_ref, gain_ref, dec_ref, zeta_ref, xi_ref, gc_ref,
             raw_ref, mix_ref, state):
        @pl.when(pl.program_id(0) == 0)
        def _():
            state[...] = jnp.zeros_like(state)

        items = _ret_chunks(tr)
        n = range(len(items))
        qbs, kbs, vbs, kzs = _ret_operands(items, q_ref, k_ref, v_ref, cos_ref, sin_ref, zeta_ref)
        ss = [_dot(qbs[i], kbs[i], NT) for i in n]
        kvs = [_dot(kzs[i], vbs[i], TN) for i in n]
        befores = _ret_states(items, state, kvs, gc_ref)
        intra = [_dot((ss[i] * dec_ref[items[i][2]]).astype(BF16), vbs[i]) for i in n]
        inter = [_dot(qbs[i], befores[i].astype(BF16)) for i in n]
        for i, (sl, hs, h) in enumerate(items):
            o = intra[i] + inter[i] * xi_ref[h]
            raw_ref[sl, hs] = o
            mu = jnp.mean(o, axis=-1, keepdims=True)
            var = jnp.mean(jnp.square(o - mu), axis=-1, keepdims=True)
            y = (o - mu) * lax.rsqrt(var + GN_EPS) * gain_ref[:, hs]
            gt = gt_ref[sl, hs]
            mix_ref[sl, hs] = (y * (gt * _sigmoid(gt))).astype(BF16)

    col, tab, head = _ret_specs(tr, False, nt)
    out = pl.BlockSpec((tr, RET_WIDTH), lambda i: (i, 0))
    return pl.pallas_call(
        body, name="ret_fwd", grid=(nt,),
        in_specs=[col(0), col(1), col(2), col(3), tab, tab, pl.BlockSpec((1, RET_WIDTH), lambda i: (0, 0)),
                  head, head, head, head],
        out_specs=[out, out],
        out_shape=[jax.ShapeDtypeStruct((t, RET_WIDTH), F32), jax.ShapeDtypeStruct((t, RET_WIDTH), BF16)],
        scratch_shapes=[pltpu.VMEM((RET_HEADS, RET_DIM, RET_DIM), F32)],
        compiler_params=_params("arbitrary"),
    )(u, u, u, u, cos, sin, gain, decay, zeta, xi, gc)


def _ret_bwd_q(dmix, raw, u, gain, tabs, rides=None):
    t = u.shape[0]
    tr = min(512, t)
    nt = t // tr
    cos, sin, decay, zeta, xi, gc = tabs

    def body(dm_ref, raw_ref, q_ref, k_ref, v_ref, gt_ref, cos_ref, sin_ref, gain_ref, dec_ref, zeta_ref, xi_ref, gc_ref,
             dq_ref, dgt_ref, dret_ref, dgain_ref, state):
        @pl.when(pl.program_id(0) == 0)
        def _():
            state[...] = jnp.zeros_like(state)
            dgain_ref[...] = jnp.zeros_like(dgain_ref)

        items = _ret_chunks(tr)
        n_items = range(len(items))
        qbs, kbs, vbs, kzs = _ret_operands(items, q_ref, k_ref, v_ref, cos_ref, sin_ref, zeta_ref)
        dos, dgains = [], {}
        for sl, hs, h in items:
            o = raw_ref[sl, hs]
            mu = jnp.mean(o, axis=-1, keepdims=True)
            var = jnp.mean(jnp.square(o - mu), axis=-1, keepdims=True)
            rs = lax.rsqrt(var + GN_EPS)
            n = (o - mu) * rs
            gt = gt_ref[sl, hs]
            sig = _sigmoid(gt)
            dout = dm_ref[sl, hs]
            gain_h = gain_ref[:, hs]
            dgt_ref[sl, hs] = (dout * (n * gain_h) * (sig * (1.0 + gt * (1.0 - sig)))).astype(BF16)
            dy = dout * (gt * sig)
            dgains[h] = dgains[h] + _rows8(dy * n) if h in dgains else _rows8(dy * n)
            dn = dy * gain_h
            do = rs * (dn - jnp.mean(dn, axis=-1, keepdims=True) - n * jnp.mean(dn * n, axis=-1, keepdims=True))
            dret_ref[sl, hs] = do
            dos.append(do)
        for h, dg in dgains.items():
            dgain_ref[:, h * RET_DIM:(h + 1) * RET_DIM] += dg
        dss = [_dot(dos[i].astype(BF16), vbs[i], NT) for i in n_items]
        kvs = [_dot(kzs[i], vbs[i], TN) for i in n_items]
        befores = _ret_states(items, state, kvs, gc_ref)
        intra = [_dot((dss[i] * dec_ref[items[i][2]]).astype(BF16), kbs[i]) for i in n_items]
        inter = [_dot((dos[i] * xi_ref[items[i][2]]).astype(BF16), befores[i].astype(BF16), NT) for i in n_items]
        for i, (sl, hs, h) in enumerate(items):
            dqr = intra[i] + inter[i]
            dq_ref[sl, hs] = (dqr * cos_ref[sl, :] - _rot(dqr * sin_ref[sl, :])).astype(BF16)

    col, tab, head = _ret_specs(tr, False, nt)
    out = pl.BlockSpec((tr, RET_WIDTH), lambda i: (i, 0))
    return _pallas(
        body, rides, name="ret_bwd_q", grid=(nt,),
        in_specs=[out, out, col(0), col(1), col(2), col(3), tab, tab, pl.BlockSpec((1, RET_WIDTH), lambda i: (0, 0)),
                  head, head, head, head],
        out_specs=[out, out, out, pl.BlockSpec((8, RET_WIDTH), lambda i: (0, 0))],
        out_shape=[jax.ShapeDtypeStruct((t, RET_WIDTH), BF16), jax.ShapeDtypeStruct((t, RET_WIDTH), BF16),
                   jax.ShapeDtypeStruct((t, RET_WIDTH), F32), jax.ShapeDtypeStruct((8, RET_WIDTH), F32)],
        scratch_shapes=[pltpu.VMEM((RET_HEADS, RET_DIM, RET_DIM), F32)],
        sem=("arbitrary",), args=[dmix, raw, u, u, u, u, cos, sin, gain, decay, zeta, xi, gc])


def _ret_bwd_kv(dret, u, tabs, rides=None):
    t = u.shape[0]
    tr = min(512, t)
    nt = t // tr
    cos, sin, decay, zeta, xi, gc = tabs
    scale = RET_DIM ** -0.5

    def body(do_ref, q_ref, k_ref, v_ref, cos_ref, sin_ref, dec_ref, zeta_ref, xi_ref, gc_ref, dk_ref, dv_ref, gst):
        @pl.when(pl.program_id(0) == 0)
        def _():
            gst[...] = jnp.zeros_like(gst)

        items = _ret_chunks(tr, rev=True)
        n = range(len(items))
        qbs, kbs, vbs, kzs = _ret_operands(items, q_ref, k_ref, v_ref, cos_ref, sin_ref, zeta_ref)
        dos = [do_ref[sl, hs] for sl, hs, h in items]
        dobs = [do.astype(BF16) for do in dos]
        ss = [_dot(qbs[i], kbs[i], NT) for i in n]
        dss = [_dot(dobs[i], vbs[i], NT) for i in n]
        steps = [_dot(qbs[i], (dos[i] * xi_ref[items[i][2]]).astype(BF16), TN) for i in n]
        afters = [g.astype(BF16) for g in _ret_states(items, gst, steps, gc_ref)]
        dvs = [_dot((ss[i] * dec_ref[items[i][2]]).astype(BF16), dobs[i], TN) + _dot(kzs[i], afters[i]) for i in n]
        dks = [_dot((dss[i] * dec_ref[items[i][2]]).astype(BF16), qbs[i], TN) for i in n]
        dkz = [_dot(vbs[i], afters[i], NT) for i in n]
        for i, (sl, hs, h) in enumerate(items):
            dv_ref[sl, hs] = dvs[i].astype(BF16)
            dkr = (dks[i] + dkz[i] * zeta_ref[h]) * scale
            dk_ref[sl, hs] = (dkr * cos_ref[sl, :] - _rot(dkr * sin_ref[sl, :])).astype(BF16)

    col, tab, head = _ret_specs(tr, True, nt)
    out = pl.BlockSpec((tr, RET_WIDTH), lambda i: (nt - 1 - i, 0))
    return _pallas(
        body, rides, name="ret_bwd_kv", grid=(nt,),
        in_specs=[out, col(0), col(1), col(2), tab, tab, head, head, head, head],
        out_specs=[out, out],
        out_shape=[jax.ShapeDtypeStruct((t, RET_WIDTH), BF16), jax.ShapeDtypeStruct((t, RET_WIDTH), BF16)],
        scratch_shapes=[pltpu.VMEM((RET_HEADS, RET_DIM, RET_DIM), F32)],
        sem=("arbitrary",), args=[dret, u, u, u, cos, sin, decay, zeta, xi, gc])


PAIRS = ATT_WIDTH // LANE
ATT_Q_BLK, ATT_K_BLK, ATT_V_BLK = 0, PAIRS, 2 * PAIRS
STAT_LANES = ATT_DIM // 2


ATT_STEP_ROWS = 4096


def _att_tiles(t, dil):
    sub = t // dil
    tq = min(ATT_STEP_ROWS, sub)
    return sub, tq, sub // tq, tq // ATT_BLOCK, min(dil, ATT_STEP_ROWS // tq)


def _att_in_specs(tq, qb, ti, gs):
    cur = lambda off: pl.BlockSpec((gs, tq, LANE), lambda g, p, i: (g, ti(i), off + p))
    prev = lambda off: pl.BlockSpec((gs, ATT_BLOCK, LANE), lambda g, p, i: (g, jnp.maximum(ti(i) * qb - 1, 0), off + p))
    return [cur(ATT_Q_BLK), cur(ATT_K_BLK), prev(ATT_K_BLK), cur(ATT_V_BLK), prev(ATT_V_BLK)]


def _band_mask():
    key = lax.broadcasted_iota(jnp.int32, (2 * ATT_BLOCK, 2 * ATT_BLOCK), 0)
    qry = lax.broadcasted_iota(jnp.int32, (2 * ATT_BLOCK, 2 * ATT_BLOCK), 1) % ATT_BLOCK
    dist = qry + ATT_BLOCK - key
    return (dist >= 0) & (dist <= ATT_BLOCK), key >= ATT_BLOCK


def _head0_lanes():
    return lax.broadcasted_iota(jnp.int32, (ATT_BLOCK, LANE), 1) < ATT_DIM


def _stack_heads(v, head0):
    zero = jnp.zeros((), v.dtype)
    return jnp.concatenate([jnp.where(head0, v, zero), jnp.where(head0, zero, v)], axis=0)


def _unstack_heads(v, head0):
    return jnp.where(head0, v[0:ATT_BLOCK], v[ATT_BLOCK:])


def _att_fwd(ua, dil):
    sub = ua.shape[1]
    _, tq, nq, qb, gs = _att_tiles(sub * dil, dil)

    def body(q_ref, kc_ref, kp_ref, vc_ref, vp_ref, o_ref, l_ref, kx, vx):
        tile = pl.program_id(2)
        kx[:, 0:ATT_BLOCK, :] = kp_ref[...]
        kx[:, ATT_BLOCK:, :] = kc_ref[...]
        vx[:, 0:ATT_BLOCK, :] = vp_ref[...]
        vx[:, ATT_BLOCK:, :] = vc_ref[...]
        band, cur_keys = _band_mask()
        head0 = _head0_lanes()
        items = [(r, b) for r in range(gs) for b in range(qb)]
        rows = lambda b: slice(b * ATT_BLOCK, (b + 1) * ATT_BLOCK)
        keys = lambda b: slice(b * ATT_BLOCK, (b + 2) * ATT_BLOCK)
        sts = [_dot(kx[r, keys(b), :], _stack_heads(q_ref[r, rows(b), :] * jnp.asarray(ATT_DIM ** -0.5, BF16), head0), NT)
               for r, b in items]
        pts, lses = [], []
        for (r, b), st in zip(items, sts):
            mask = band if b > 0 else band & (cur_keys | (tile > 0))
            st = jnp.where(mask, st, -1e30)
            m = jnp.max(st, axis=0, keepdims=True)
            ex = jnp.exp(st - m)
            den = jnp.sum(ex, axis=0, keepdims=True)
            pts.append((ex * (1.0 / den)).astype(BF16))
            lses.append(m + jnp.log(den))
        outs = [_dot(pt, vx[r, keys(b), :], TN) for (r, b), pt in zip(items, pts)]
        for (r, b), out, lse in zip(items, outs, lses):
            o_ref[r, rows(b), :] = _unstack_heads(out, head0).astype(BF16)
            cols = [jnp.broadcast_to(lse[:, e * ATT_BLOCK:(e + 1) * ATT_BLOCK], (ATT_BLOCK, LANE)).T for e in range(2)]
            l_ref[r, rows(b), :] = jnp.where(head0, cols[0], cols[1])

    out = pl.BlockSpec((gs, tq, LANE), lambda g, p, i: (g, i, p))
    return pl.pallas_call(
        body, name=f"att_fwd_d{dil}", grid=(dil // gs, PAIRS, nq),
        in_specs=_att_in_specs(tq, qb, lambda i: i, gs),
        out_specs=[out, out],
        out_shape=[jax.ShapeDtypeStruct((dil, sub, ATT_WIDTH), BF16), jax.ShapeDtypeStruct((dil, sub, ATT_WIDTH), F32)],
        scratch_shapes=[pltpu.VMEM((gs, tq + ATT_BLOCK, LANE), BF16)] * 2,
        compiler_params=_params("arbitrary", "arbitrary", "arbitrary"),
    )(ua, ua, ua, ua, ua)


def _regrouped_spec(tm, dil, w):
    return pl.BlockSpec((dil, tm // dil, w), lambda i: (0, i, 0))


def _att_combine(outs, lses, t):
    w = ATT_WIDTH
    tm = min(512, t)
    nb = len(outs)

    def body(*refs):
        o_refs, l_refs = refs[:nb], refs[nb:2 * nb]
        mix_ref, att_ref, lse_ref, buf = refs[2 * nb:]
        ls = [_natural_rows(r, dil, buf) for r, dil in zip(l_refs, DILATIONS)]
        m = functools.reduce(jnp.maximum, ls)
        ws = [jnp.exp(l - m) for l in ls]
        den = functools.reduce(jnp.add, ws)
        att = functools.reduce(jnp.add, [(wt / den) * _natural_rows(r, dil, buf) for wt, r, dil in zip(ws, o_refs, DILATIONS)])
        att_ref[...] = att
        mix_ref[...] = att.astype(BF16)
        lse_ref[...] = m + jnp.log(den)

    tile = pl.BlockSpec((tm, w), lambda i: (i, 0))
    regrouped = [_regrouped_spec(tm, dil, w) for dil in DILATIONS]
    return pl.pallas_call(
        body, name="att_combine", grid=(t // tm,),
        in_specs=regrouped * 2, out_specs=[tile, tile, tile],
        out_shape=[jax.ShapeDtypeStruct((t, w), BF16), jax.ShapeDtypeStruct((t, w), F32), jax.ShapeDtypeStruct((t, w), F32)],
        scratch_shapes=[_chunk_scratch(tm, w)],
        compiler_params=_params("arbitrary"),
    )(*outs, *lses)


def _att_bwd_prep(datt, att, lse):
    t, w = datt.shape
    tm = min(512, t)

    def body(da_ref, at_ref, l_ref, *rest):
        outs, dbuf, sbuf = rest[:-2], rest[-2], rest[-1]
        dav = da_ref[...]
        prod = dav * at_ref[...]
        lane = lax.broadcasted_iota(jnp.int32, (tm, LANE), 1)
        for k in range(w // LANE):
            cols = slice(k * LANE, (k + 1) * LANE)
            dbuf[k] = dav[:, cols]
            delta = jnp.concatenate(
                [jnp.broadcast_to(jnp.sum(prod[:, k * LANE + e * ATT_DIM:k * LANE + (e + 1) * ATT_DIM], axis=-1, keepdims=True),
                                  (tm, ATT_DIM)) for e in range(LANE // ATT_DIM)], axis=1)
            sbuf[k] = jnp.where(lane % ATT_DIM < STAT_LANES, l_ref[:, cols], delta)
        for k, dil in enumerate(DILATIONS):
            _regroup_store(dbuf, outs[2 * k], dil)
            _regroup_store(sbuf, outs[2 * k + 1], dil)

    tile = pl.BlockSpec((tm, w), lambda i: (i, 0))
    res = pl.pallas_call(
        body, name="att_bwd_prep", grid=(t // tm,),
        in_specs=[tile] * 3,
        out_specs=[_regrouped_spec(tm, dil, w) for dil in DILATIONS for _ in range(2)],
        out_shape=[jax.ShapeDtypeStruct((dil, t // dil, w), dt) for dil in DILATIONS for dt in (BF16, F32)],
        scratch_shapes=[_chunk_scratch(tm, w)] * 2,
        compiler_params=_params("arbitrary"),
    )(datt, att, lse)
    return [(res[2 * k], res[2 * k + 1]) for k in range(len(DILATIONS))]


def _att_bwd(ua, da, stat, dil, rides=None):
    sub = ua.shape[1]
    _, tq, nq, qb, gs = _att_tiles(sub * dil, dil)
    scale = ATT_DIM ** -0.5

    def body(q_ref, kc_ref, kp_ref, vc_ref, vp_ref, da_ref, st_ref, dq_ref, dk_ref, dv_ref, kx, vx, ck, cv):
        step = pl.program_id(2)
        tile = nq - 1 - step

        @pl.when(step == 0)
        def _():
            ck[...] = jnp.zeros_like(ck)
            cv[...] = jnp.zeros_like(cv)

        kx[:, 0:ATT_BLOCK, :] = kp_ref[...]
        kx[:, ATT_BLOCK:, :] = kc_ref[...]
        vx[:, 0:ATT_BLOCK, :] = vp_ref[...]
        vx[:, ATT_BLOCK:, :] = vc_ref[...]
        band, cur_keys = _band_mask()
        head0 = _head0_lanes()
        items = [(r, b) for r in range(gs) for b in range(qb)]
        n = range(len(items))
        rows = lambda b: slice(b * ATT_BLOCK, (b + 1) * ATT_BLOCK)
        keys = lambda b: slice(b * ATT_BLOCK, (b + 2) * ATT_BLOCK)
        qqs = [_stack_heads(q_ref[r, rows(b), :] * jnp.asarray(scale, BF16), head0) for r, b in items]
        dds = [_stack_heads(da_ref[r, rows(b), :], head0) for r, b in items]
        sts = [_dot(kx[r, keys(b), :], qqs[i], NT) for i, (r, b) in enumerate(items)]
        dpts = [_dot(vx[r, keys(b), :], dds[i], NT) for i, (r, b) in enumerate(items)]
        pts, dsts = [], []
        for i, (r, b) in enumerate(items):
            mask = band if b > 0 else band & (cur_keys | (tile > 0))
            stat = st_ref[r, rows(b), :].T
            row = lambda k: jnp.concatenate([stat[e * ATT_DIM + k:e * ATT_DIM + k + 1, :] for e in range(2)], axis=1)
            pt = jnp.where(mask, jnp.exp(sts[i] - row(0)), 0.0)
            dsts.append((pt * (dpts[i] - row(STAT_LANES))).astype(BF16))
            pts.append(pt.astype(BF16))
        dqs = [_dot(dsts[i], kx[r, keys(b), :], TN) for i, (r, b) in enumerate(items)]
        dkbs = [_dot(dsts[i], qqs[i]) for i in n]
        dvbs = [_dot(pts[i], dds[i]) for i in n]
        for i, (r, b) in enumerate(items):
            dq_ref[r, rows(b), :] = (_unstack_heads(dqs[i], head0) * scale).astype(BF16)
            if b > 0:
                dk_ref[r, rows(b - 1), :] = (dkbs[i - 1][ATT_BLOCK:] + dkbs[i][0:ATT_BLOCK]).astype(BF16)
                dv_ref[r, rows(b - 1), :] = (dvbs[i - 1][ATT_BLOCK:] + dvbs[i][0:ATT_BLOCK]).astype(BF16)
        for r in range(gs):
            first, last = r * qb, r * qb + qb - 1
            dk_ref[r, rows(qb - 1), :] = (dkbs[last][ATT_BLOCK:] + ck[r]).astype(BF16)
            dv_ref[r, rows(qb - 1), :] = (dvbs[last][ATT_BLOCK:] + cv[r]).astype(BF16)
            ck[r] = dkbs[first][0:ATT_BLOCK]
            cv[r] = dvbs[first][0:ATT_BLOCK]

    ti = lambda i: nq - 1 - i
    out = pl.BlockSpec((gs, tq, LANE), lambda g, p, i: (g, ti(i), p))
    shape = jax.ShapeDtypeStruct((dil, sub, ATT_WIDTH), BF16)
    return _pallas(
        body, rides, name=f"att_bwd_d{dil}", grid=(dil // gs, PAIRS, nq),
        in_specs=_att_in_specs(tq, qb, ti, gs) + [out, out],
        out_specs=[out, out, out], out_shape=[shape] * 3,
        scratch_shapes=[pltpu.VMEM((gs, tq + ATT_BLOCK, LANE), BF16)] * 2 + [pltpu.VMEM((gs, ATT_BLOCK, LANE), F32)] * 2,
        sem=("arbitrary", "arbitrary", "arbitrary"), args=[ua, ua, ua, ua, ua, da, stat])


class _Reduction:
    def __init__(self, place, names, grads):
        self.place, self.names, self.grads = place, names, grads

    def pair(self):
        return _pair_ride(self.grads)

    def chips(self, got):
        self.got = got
        return _chip_ride([_pair_sum(self.place, g, r, f"pair_sum_{n}") for g, r, n in zip(self.grads, got, self.names)])

    def halves(self, others):
        return [_chip_sum(self.place, g, r, o, f"chip_sum_{n}")
                for g, r, o, n in zip(self.grads, self.got, others, self.names)]


def _step(x, target, gains, w, place=None):
    t = x.shape[0]
    ex = place is not None
    g_ffn1, g_mix, g_ret, g_ffn2, g_fin = gains
    w = list(w)
    tabs = _retention_tables(t)
    red = lambda names, grads: _Reduction(place, names, grads) if ex else None
    ride = lambda r: [r] if ex else None

    if ex:
        w[0:3] = _run(_gather_ride(w[0:3]), "gather_ffn1_weights")
    (h1, xn1, *hid1, act1), rest = _ffn_fwd(x, g_ffn1, *w[0:3], "ffn1_fwd", ride(_gather_ride(w[3:])) if ex else None)
    if ex:
        w[3:] = rest[0]
    wg1, wu1, wd1, win, wo, wg2, wu2, wd2 = w
    wo2 = wo.reshape(wo.shape[0] * wo.shape[1], wo.shape[2])
    xnm, u, *uas = _inproj_fwd(h1, g_mix, win)
    raw, mix_r = _ret_fwd(u, g_ret, tabs)
    branches = [_att_fwd(ua, dil) for ua, dil in zip(uas, DILATIONS)]
    mix_a, att, lse = _att_combine([b[0] for b in branches], [b[1] for b in branches], t)
    h2 = _outproj_fwd(h1, mix_r, mix_a, wo2)
    (dh3, xn2, *hid2, act2, loss_p, dg_fin), _ = _ffn_fwd(h2, g_ffn2, wg2, wu2, wd2, "ffn2_fwd", head=(g_fin, target))

    (dwd2,), _ = _ffn_wgrad_down(act2, dh3, "ffn2_wgrad_down")
    dwd2 = dwd2.reshape(wd2.shape)
    r_d2 = red(["ffn2_w_down"], [dwd2])
    (dh2, dga2, dua2, dg_ffn2), e = _ffn_bwd_data(dh3, h2, g_ffn2, *hid2, wg2, wu2, wd2, "ffn2_bwd",
                                                  ex and [r_d2.pair()])
    (dwg2, dwu2), e = _ffn_wgrad_gu(xn2, [dga2, dua2], "ffn2_wgrad_gu", ex and [r_d2.chips(e[0])])
    dwg2, dwu2 = dwg2.reshape(wg2.shape), dwu2.reshape(wu2.shape)
    r_gu2 = red(["ffn2_w_gate", "ffn2_w_up"], [dwg2, dwu2])
    (dmix_r, dmix_a), e = _outproj_bwd(dh2, wo2, ex and [r_gu2.pair(), _finish_ride(r_d2.halves(e[0]))])
    if ex:
        got_gu2, (dwd2,) = e
    hw = RET_WIDTH // (wo.shape[1])
    dwo = jnp.concatenate([_tn_matmul(mix_r, dh2, dh2.shape[1], "wo_grad_r").reshape(hw, wo.shape[1], wo.shape[2]),
                           _tn_matmul(mix_a, dh2, dh2.shape[1], "wo_grad_a").reshape(hw, wo.shape[1], wo.shape[2])])
    r_wo = red(["w_out"], [dwo])
    (dq_r, dgt_r, dret, dg_ret), e = _ret_bwd_q(dmix_r, raw, u, g_ret, tabs, ex and [r_gu2.chips(got_gu2)])
    (dk_r, dv_r), e = _ret_bwd_kv(dret, u, tabs, ex and [r_wo.pair(), _finish_ride(r_gu2.halves(e[0]))])
    if ex:
        got_wo, (dwg2, dwu2) = e
    prep = _att_bwd_prep(dmix_a, att, lse)
    p1, e = _att_bwd(uas[0], *prep[0], DILATIONS[0], ex and [r_wo.chips(got_wo)])
    p4, e = _att_bwd(uas[1], *prep[1], DILATIONS[1], ex and [_finish_ride(r_wo.halves(e[0]))])
    if ex:
        (dwo,), = e
    p16, _ = _att_bwd(uas[2], *prep[2], DILATIONS[2])
    dh1, du, dg_mix = _inproj_bwd([dq_r, dk_r, dv_r, dgt_r], [p1, p4, p16], h1, g_mix, dh2, win)
    dwin = _tn_matmul(xnm, du, win.shape[2], "win_grad")
    r_in = red(["w_in"], [dwin])
    (dwd1,), e = _ffn_wgrad_down(act1, dh1, "ffn1_wgrad_down", ex and [r_in.pair()])
    dwd1 = dwd1.reshape(wd1.shape)
    r_d1 = red(["ffn1_w_down"], [dwd1])
    got_in = e
    (dx, dga1, dua1, dg_ffn1), _ = _ffn_bwd_data(dh1, x, g_ffn1, *hid1, wg1, wu1, wd1, "ffn1_bwd")
    (dwg1,), e = _ffn_wgrad_gu(xn1, [dga1], "ffn1_wgrad_gate", ex and [r_in.chips(got_in[0]), r_d1.pair()])
    dwg1 = dwg1.reshape(wg1.shape)
    if ex:
        oth_in, got_d1 = e
        r_g1 = red(["ffn1_w_gate"], [dwg1])
        got_g1 = _run(r_g1.pair(), "pair_exchange_ffn1_gate")
    (dwu1,), e = _ffn_wgrad_gu(xn1, [dua1], "ffn1_wgrad_up",
                               ex and [_finish_ride(r_in.halves(oth_in)), r_d1.chips(got_d1), r_g1.chips(got_g1)])
    dwu1 = dwu1.reshape(wu1.shape)
    gain_parts = [dg_ffn1, dg_mix, dg_ret, dg_ffn2, dg_fin]
    if not ex:
        return loss_p, dx, [dwg1, dwu1, dwd1, dwin, dwo, dwg2, dwu2, dwd2], gain_parts
    (dwin,), oth_d1, oth_g1 = e
    r_u1 = red(["ffn1_w_up"], [dwu1])
    got_u1 = _run(r_u1.pair(), "pair_exchange_ffn1_up")
    oth_u1 = _run(r_u1.chips(got_u1), "chip_exchange_ffn1_up")
    last = r_g1.halves(oth_g1) + r_u1.halves(oth_u1) + r_d1.halves(oth_d1)
    dwg1, dwu1, dwd1, gall = _run(_finish_ride(last, _pack_gains(gain_parts, x.shape[1])), "finish_exchange_ffn1")
    return loss_p, dx, [dwg1, dwu1, dwd1, dwin, dwo, dwg2, dwu2, dwd2], gall


N_DEV = 8
GAIN_ROWS = 8


def _place():
    x, y, c = lax.axis_index("x"), lax.axis_index("y"), lax.axis_index("c")
    chips = [(1 - x, y), (x, 1 - y), (1 - x, 1 - y)]
    return x, y, c, chips


ROW_QUARTERS = 4


def _place_shards(place, ws):
    n = len(ws)

    def body(place_ref, *refs):
        for w_ref, o_ref in zip(refs[:n], refs[n:]):
            o_ref[...] = w_ref[...].astype(BF16)

    quarter = lambda w: (w.shape[0] // ROW_QUARTERS, w.shape[1])
    return pl.pallas_call(
        body, name="place_shards",
        grid_spec=pltpu.PrefetchScalarGridSpec(
            num_scalar_prefetch=1, grid=(ROW_QUARTERS,),
            in_specs=[pl.BlockSpec(quarter(w), lambda i, pr: (i, 0)) for w in ws],
            out_specs=[pl.BlockSpec((None,) + quarter(w), lambda i, pr: (pr[0], i, 0)) for w in ws]),
        out_shape=[jax.ShapeDtypeStruct((N_SHARD,) + w.shape, BF16) for w in ws],
        compiler_params=_params("arbitrary"),
    )(place, *ws)


def _gather_ride(bufs):
    na = len(bufs)
    sent = [("me", "half", "x"), ("me", "half", "y"), ("x", "second quarter", "y"), ("y", "first quarter", "x")]
    landed = [("x", "half"), ("y", "half"), ("d", "second quarter"), ("d", "first quarter")]

    def legs(outs, sems):
        send_sem, recv_sem, fsend_sem, frecv_sem = sems
        x, y, c, _ = _place()
        slot = {"me": 2 * x + y, "x": 2 * (1 - x) + y, "y": 2 * x + (1 - y), "d": 2 * (1 - x) + (1 - y)}
        peer = {"x": (1 - x, y, c), "y": (x, 1 - y, c)}

        def rows(a, which, piece, core):
            hr = outs[a].shape[1] // 2
            lo, n = {"half": (0, hr), "first quarter": (0, hr // 2), "second quarter": (hr // 2, hr // 2)}[piece]
            return outs[a].at[slot[which], pl.ds(core * hr + lo, n)]

        def ici(a, k):
            which, piece, to = sent[k]
            ref = rows(a, which, piece, c)
            return pltpu.make_async_remote_copy(src_ref=ref, dst_ref=ref, send_sem=send_sem.at[a, k],
                                                recv_sem=recv_sem.at[a, k], device_id=peer[to], device_id_type=MESH)

        def arrival(a, k):
            ref = rows(a, *landed[k], c)
            return pltpu.make_async_remote_copy(src_ref=ref, dst_ref=ref, send_sem=send_sem.at[a, k],
                                                recv_sem=recv_sem.at[a, k], device_id=peer["x"], device_id_type=MESH)

        def d2d(a, k, core):
            ref = rows(a, *landed[k], core)
            return pltpu.make_async_remote_copy(src_ref=ref, dst_ref=ref, send_sem=fsend_sem.at[a, k],
                                                recv_sem=frecv_sem.at[a, k], device_id=(x, y, 1 - c), device_id_type=MESH)

        return c, ici, arrival, d2d

    def start(ins, outs, sems):
        _, ici, _, _ = legs(outs, sems)
        for a in range(na):
            ici(a, 0).start()
            ici(a, 1).start()

    def middle(ins, outs, sems):
        c, ici, arrival, d2d = legs(outs, sems)
        for a in range(na):
            for k in (0, 1):
                arrival(a, k).wait_recv()
                ici(a, 2 + k).start()
                d2d(a, k, c).start()

    def finish(ins, outs, sems):
        c, ici, arrival, d2d = legs(outs, sems)
        for a in range(na):
            for k in (2, 3):
                arrival(a, k).wait_recv()
                d2d(a, k, c).start()
        for a in range(na):
            for k in range(len(landed)):
                d2d(a, k, 1 - c).wait_recv()
        for a in range(na):
            for k in range(len(sent)):
                ici(a, k).wait_send()
                d2d(a, k, c).wait_send()

    return _Ride(bufs, [jax.ShapeDtypeStruct(b.shape, b.dtype) for b in bufs], [pltpu.SemaphoreType.DMA((na, 4))] * 4,
                 start, finish, {a: a for a in range(na)}, middle)


def _pair_ride(grads):
    na = len(grads)

    def copies(ins, outs, sems):
        send_sem, recv_sem = sems
        x, y, c, _ = _place()
        res = []
        for a in range(na):
            hr = ins[a].shape[1] // 2
            res.append(pltpu.make_async_remote_copy(
                src_ref=ins[a].at[:, pl.ds((1 - c) * hr, hr)], dst_ref=outs[a],
                send_sem=send_sem.at[a], recv_sem=recv_sem.at[a], device_id=(x, y, 1 - c), device_id_type=MESH))
        return res

    def start(ins, outs, sems):
        for cp in copies(ins, outs, sems):
            cp.start()

    def finish(ins, outs, sems):
        for cp in copies(ins, outs, sems):
            cp.wait()

    return _Ride(grads, [jax.ShapeDtypeStruct((g.shape[0], g.shape[1] // 2, g.shape[2]), g.dtype) for g in grads],
                 [pltpu.SemaphoreType.DMA((na,))] * 2, start, finish)


def _chip_ride(sums):
    na = len(sums)

    def copies(ins, outs, sems):
        send_sem, recv_sem = sems
        x, y, c, chips = _place()
        res = []
        for a in range(na):
            for j, (px, py) in enumerate(chips):
                res.append(pltpu.make_async_remote_copy(
                    src_ref=ins[a].at[2 * px + py], dst_ref=outs[a].at[j],
                    send_sem=send_sem.at[a, j], recv_sem=recv_sem.at[a, j], device_id=(px, py, c), device_id_type=MESH))
        return res

    def start(ins, outs, sems):
        for cp in copies(ins, outs, sems):
            cp.start()

    def finish(ins, outs, sems):
        for cp in copies(ins, outs, sems):
            cp.wait()

    return _Ride(sums, [jax.ShapeDtypeStruct((3,) + s.shape[1:], s.dtype) for s in sums],
                 [pltpu.SemaphoreType.DMA((na, 3))] * 2, start, finish)


def _finish_ride(grads, gpack=None):
    na = len(grads)

    def halves(outs, sems, which):
        x, y, c, _ = _place()
        res = []
        for a in range(na):
            hr = outs[a].shape[0] // 2
            rows = outs[a].at[pl.ds((c if which == "mine" else 1 - c) * hr, hr)]
            res.append(pltpu.make_async_remote_copy(
                src_ref=rows, dst_ref=rows, send_sem=sems[0].at[a], recv_sem=sems[1].at[a],
                device_id=(x, y, 1 - c), device_id_type=MESH))
        return res

    def gains(ins, outs, sems):
        x, y, c, _ = _place()
        dev = 4 * x + 2 * y + c
        g_in, g_out = ins[na], outs[na]
        own = pltpu.make_async_copy(g_in, g_out.at[dev], sems[2])
        sends, lands = [], []
        for k in range(N_DEV - 1):
            bx, by, bc = (k + 1) // 4, ((k + 1) // 2) % 2, (k + 1) % 2
            peer = (jnp.bitwise_xor(x, bx), jnp.bitwise_xor(y, by), jnp.bitwise_xor(c, bc))
            sends.append(pltpu.make_async_remote_copy(
                src_ref=g_in, dst_ref=g_out.at[dev], send_sem=sems[3].at[k], recv_sem=sems[4].at[k],
                device_id=peer, device_id_type=MESH))
            slot = g_out.at[jnp.bitwise_xor(dev, k + 1)]
            lands.append(pltpu.make_async_remote_copy(
                src_ref=slot, dst_ref=slot, send_sem=sems[3].at[k], recv_sem=sems[4].at[k],
                device_id=peer, device_id_type=MESH))
        return own, sends, lands

    def start(ins, outs, sems):
        for cp in halves(outs, sems, "mine"):
            cp.start()
        if gpack is not None:
            own, sends, _ = gains(ins, outs, sems)
            own.start()
            for cp in sends:
                cp.start()

    def finish(ins, outs, sems):
        for cp in halves(outs, sems, "sibling's"):
            cp.wait_recv()
        if gpack is not None:
            own, sends, lands = gains(ins, outs, sems)
            for cp in lands:
                cp.wait_recv()
            for cp in sends:
                cp.wait_send()
            own.wait()
        for cp in halves(outs, sems, "mine"):
            cp.wait_send()

    shapes = [jax.ShapeDtypeStruct(g.shape, g.dtype) for g in grads]
    sems = [pltpu.SemaphoreType.DMA((na,))] * 2
    if gpack is None:
        return _Ride(grads, shapes, sems, start, finish, {a: a for a in range(na)})
    return _Ride(list(grads) + [gpack], shapes + [jax.ShapeDtypeStruct((N_DEV,) + gpack.shape, gpack.dtype)],
                 sems + [pltpu.SemaphoreType.DMA, pltpu.SemaphoreType.DMA((N_DEV - 1,)), pltpu.SemaphoreType.DMA((N_DEV - 1,))],
                 start, finish, {a: a for a in range(na)})


def _pair_sum(place, grad, got, name):
    ns, r, cols = grad.shape
    hr = r // 2

    def body(place_ref, g_ref, r_ref, o_ref):
        o_ref[...] = (g_ref[...] + r_ref[...]).astype(BF16)

    return pl.pallas_call(
        body, name=name,
        grid_spec=pltpu.PrefetchScalarGridSpec(
            num_scalar_prefetch=1, grid=(ns,),
            in_specs=[pl.BlockSpec((None, hr, cols), lambda s, pr: (s, pr[1], 0)),
                      pl.BlockSpec((None, hr, cols), lambda s, pr: (s, 0, 0))],
            out_specs=pl.BlockSpec((None, hr, cols), lambda s, pr: (s, 0, 0))),
        out_shape=jax.ShapeDtypeStruct((ns, hr, cols), BF16),
        compiler_params=_params("arbitrary"),
    )(place, grad, got)


def _chip_sum(place, grad, got, others, name):
    ns, r, cols = grad.shape
    hr = r // 2
    nb = 2
    tr = hr // nb

    def body(place_ref, g_ref, r_ref, o3_ref, o_ref):
        acc = g_ref[...] + r_ref[...]
        for j in range(3):
            acc = acc + o3_ref[j].astype(F32)
        o_ref[...] = acc

    return pl.pallas_call(
        body, name=name,
        grid_spec=pltpu.PrefetchScalarGridSpec(
            num_scalar_prefetch=1, grid=(nb,),
            in_specs=[pl.BlockSpec((None, tr, cols), lambda i, pr: (pr[0], pr[1] * nb + i, 0)),
                      pl.BlockSpec((None, tr, cols), lambda i, pr: (pr[0], i, 0)),
                      pl.BlockSpec((3, tr, cols), lambda i, pr: (0, i, 0))],
            out_specs=pl.BlockSpec((tr, cols), lambda i, pr: (pr[1] * nb + i, 0))),
        out_shape=jax.ShapeDtypeStruct((r, cols), F32),
        compiler_params=_params("arbitrary"),
    )(place, grad, got, others)


def _pack_gains(parts, d):
    def body(*refs):
        ins, o_ref = refs[:-1], refs[-1]
        o_ref[...] = jnp.zeros_like(o_ref)
        for k, r in enumerate(ins):
            o_ref[k:k + 1, 0:r.shape[1]] = jnp.sum(r[...], axis=0, keepdims=True)

    return pl.pallas_call(
        body, name="pack_gains", out_shape=jax.ShapeDtypeStruct((GAIN_ROWS, d), F32),
    )(*parts)


def _adamw_math(w, g, m, v):
    m = ADAM_B1 * m + (1.0 - ADAM_B1) * g
    v = ADAM_B2 * v + (1.0 - ADAM_B2) * jnp.square(g)
    m_hat = m / (1.0 - ADAM_B1 ** ADAM_STEP)
    v_hat = v / (1.0 - ADAM_B2 ** ADAM_STEP)
    return -ADAM_LR * (m_hat / (jnp.sqrt(v_hat) + ADAM_EPS) + ADAM_WD * w), m, v


def _adamw(ws, gs, ms, vs):
    n = len(ws)

    def body(*refs):
        ins, outs = refs[:4 * n], refs[4 * n:]
        for k in range(n):
            w_ref, g_ref, m_ref, v_ref = ins[4 * k:4 * k + 4]
            go_ref, d_ref, nm_ref, nv_ref = outs[4 * k:4 * k + 4]
            g = g_ref[...]
            go_ref[...] = g
            d_ref[...], nm_ref[...], nv_ref[...] = _adamw_math(w_ref[...], g, m_ref[...], v_ref[...])

    parts = 2 * ROW_QUARTERS
    tile = lambda w: pl.BlockSpec((w.shape[0] // parts, w.shape[1]), lambda i: (i, 0))
    res = pl.pallas_call(
        body, name="adamw_shards", grid=(parts,),
        in_specs=[tile(w) for w in ws for _ in range(4)], out_specs=[tile(w) for w in ws for _ in range(4)],
        out_shape=[jax.ShapeDtypeStruct(w.shape, F32) for w in ws for _ in range(4)],
        compiler_params=_params("arbitrary"),
    )(*[a for quad in zip(ws, gs, ms, vs) for a in quad])
    return [res[4 * k:4 * k + 4] for k in range(n)]


def _adamw_gain(gall, row, w, m, v, name):
    n = w.shape[1]

    def body(ga_ref, w_ref, m_ref, v_ref, g_ref, d_ref, nm_ref, nv_ref):
        g = ga_ref[0, row:row + 1, 0:n]
        for k in range(1, N_DEV):
            g = g + ga_ref[k, row:row + 1, 0:n]
        g_ref[...] = g
        d_ref[...], nm_ref[...], nv_ref[...] = _adamw_math(w_ref[...], g, m_ref[...], v_ref[...])

    return pl.pallas_call(
        body, name=name, out_shape=[jax.ShapeDtypeStruct((1, n), F32)] * 4,
    )(gall, w, m, v)


def kernel(x, norm_ffn1, ffn1_w_gate, ffn1_w_up, ffn1_w_down, norm_mix, w_in, ret_norm_gain, w_out, norm_ffn2, ffn2_w_gate, ffn2_w_up, ffn2_w_down, norm_final, loss_target, m_norm_ffn1, m_ffn1_w_gate, m_ffn1_w_up, m_ffn1_w_down, m_norm_mix, m_w_in, m_ret_norm_gain, m_w_out, m_norm_ffn2, m_ffn2_w_gate, m_ffn2_w_up, m_ffn2_w_down, m_norm_final, v_norm_ffn1, v_ffn1_w_gate, v_ffn1_w_up, v_ffn1_w_down, v_norm_mix, v_w_in, v_ret_norm_gain, v_w_out, v_norm_ffn2, v_ffn2_w_gate, v_ffn2_w_up, v_ffn2_w_down, v_norm_final):
    d = x.shape[-1]
    mats = [ffn1_w_gate, ffn1_w_up, ffn1_w_down, w_in, w_out, ffn2_w_gate, ffn2_w_up, ffn2_w_down]
    mats_m = [m_ffn1_w_gate, m_ffn1_w_up, m_ffn1_w_down, m_w_in, m_w_out, m_ffn2_w_gate, m_ffn2_w_up, m_ffn2_w_down]
    mats_v = [v_ffn1_w_gate, v_ffn1_w_up, v_ffn1_w_down, v_w_in, v_w_out, v_ffn2_w_gate, v_ffn2_w_up, v_ffn2_w_down]
    mat_names = ["ffn1_w_gate", "ffn1_w_up", "ffn1_w_down", "w_in", "w_out", "ffn2_w_gate", "ffn2_w_up", "ffn2_w_down"]
    gains = [norm_ffn1, norm_mix, ret_norm_gain, norm_ffn2, norm_final.reshape(1, d)]
    gains_m = [m_norm_ffn1, m_norm_mix, m_ret_norm_gain, m_norm_ffn2, m_norm_final.reshape(1, d)]
    gains_v = [v_norm_ffn1, v_norm_mix, v_ret_norm_gain, v_norm_ffn2, v_norm_final.reshape(1, d)]
    gain_names = ["norm_ffn1", "norm_mix", "ret_norm_gain", "norm_ffn2", "norm_final"]

    turned = lambda n: n.endswith(("w_gate", "w_up"))
    local = lambda a, n: jnp.swapaxes(a, 1, 2)[0] if turned(n) else a[0]
    back = lambda a, n: jnp.swapaxes(a[None], 1, 2) if turned(n) else a[None]
    shards = [local(w, n) for w, n in zip(mats, mat_names)]
    place = jnp.stack([2 * lax.axis_index("x") + lax.axis_index("y"), lax.axis_index("c")]).astype(jnp.int32)
    placed = _place_shards(place, shards)
    loss_p, dx, shard_grads, gall = _step(x[0], loss_target[0], gains, placed, place)

    out_g, out_d, out_m, out_v = {}, {}, {}, {}
    updates = _adamw(shards, shard_grads, [local(m, n) for m, n in zip(mats_m, mat_names)],
                     [local(v, n) for v, n in zip(mats_v, mat_names)])
    for n, quad in zip(mat_names, updates):
        out_g[n], out_d[n], out_m[n], out_v[n] = [back(a, n) for a in quad]
    for row, (n, w, m, v) in enumerate(zip(gain_names, gains, gains_m, gains_v)):
        res = _adamw_gain(gall, row, w, m, v, f"adamw_{n}")
        shape = (d,) if n == "norm_final" else w.shape
        out_g[n], out_d[n], out_m[n], out_v[n] = [r.reshape(shape) for r in res]

    loss = lax.psum(jnp.sum(loss_p), ("x", "y", "c"))
    order = ["norm_ffn1", "ffn1_w_gate", "ffn1_w_up", "ffn1_w_down", "norm_mix", "w_in", "ret_norm_gain", "w_out",
             "norm_ffn2", "ffn2_w_gate", "ffn2_w_up", "ffn2_w_down", "norm_final"]
    return (loss, dx[None], *[out_g[n] for n in order], *[out_d[n] for n in order],
            *[out_m[n] for n in order], *[out_v[n] for n in order])
```

```python
import functools

import jax
import jax.numpy as jnp
from jax import lax
from jax.experimental import pallas as pl
from jax.experimental.pallas import tpu as pltpu

F32 = jnp.float32
BF16 = jnp.bfloat16
MESH = pl.DeviceIdType.MESH

NORM_EPS = 1e-6
GN_EPS = 1e-6
ROPE_BASE = 10000.0
RET_HEADS = 4
RET_DIM = 128
RET_WIDTH = 512
RET_CHUNK = 128
ATT_DIM = 64
ATT_WIDTH = 512
ATT_BLOCK = 128
DILATIONS = (1, 4, 16)
LANE = 128
N_SHARD = 4
ADAM_LR, ADAM_B1, ADAM_B2, ADAM_EPS, ADAM_WD, ADAM_STEP = 0.001, 0.9, 0.999, 1e-08, 0.01, 10

V7X_VMEM_BYTES = 64 * 1024 * 1024
VMEM_LIMIT = V7X_VMEM_BYTES - 8 * 1024 * 1024

NT = (((1,), (1,)), ((), ()))
TN = (((0,), (0,)), ((), ()))


def _params(*sem):
    return pltpu.CompilerParams(dimension_semantics=sem, vmem_limit_bytes=VMEM_LIMIT)


def _dot(a, b, dims=None):
    if dims is None:
        return jnp.dot(a, b, preferred_element_type=F32)
    return lax.dot_general(a, b, dims, preferred_element_type=F32)


def _sigmoid(x):
    return 1.0 / (1.0 + jnp.exp(-x))


def _load_weights(pairs, sems):
    copies = [pltpu.make_async_copy(src, dst, sems.at[k]) for k, (src, dst) in enumerate(pairs)]
    for cp in copies:
        cp.start()
    for cp in copies:
        cp.wait()


def _rows8(v):
    r, c = v.shape
    return v.reshape(r // 8, 8, c).sum(axis=0)


class _Ride:
    def __init__(self, inputs, out_shapes, sems, start, finish, aliases=None, middle=None):
        self.inputs, self.out_shapes, self.sems = list(inputs), list(out_shapes), list(sems)
        self.start, self.middle, self.finish, self.aliases = start, middle, finish, dict(aliases or {})


def _pallas(body, rides, *, name, in_specs, out_specs, out_shape, args, grid=(), scratch_shapes=(), sem=()):
    rides = [r for r in (rides or []) if r is not None]
    n_in, n_out, n_scr = len(args), len(out_shape), len(scratch_shapes)
    hbm = pl.BlockSpec(memory_space=pl.ANY)
    r_in = [a for r in rides for a in r.inputs]
    r_out = [s for r in rides for s in r.out_shapes]
    r_sem = [s for r in rides for s in r.sems]
    aliases, spans, ki, ko, ks = {}, [], 0, 0, 0
    for r in rides:
        aliases.update({n_in + ki + i: n_out + ko + o for i, o in r.aliases.items()})
        spans.append((ki, ko, ks))
        ki, ko, ks = ki + len(r.inputs), ko + len(r.out_shapes), ks + len(r.sems)

    def wrapped(*refs):
        ins, rin = refs[:n_in], refs[n_in:n_in + len(r_in)]
        o0 = n_in + len(r_in)
        outs, rout = refs[o0:o0 + n_out], refs[o0 + n_out:o0 + n_out + len(r_out)]
        s0 = o0 + n_out + len(r_out)
        scr, rsem = refs[s0:s0 + n_scr], refs[s0 + n_scr:]
        part = lambda r, k: (rin[spans[k][0]:spans[k][0] + len(r.inputs)], rout[spans[k][1]:spans[k][1] + len(r.out_shapes)],
                             rsem[spans[k][2]:spans[k][2] + len(r.sems)])
        first = functools.reduce(jnp.logical_and, [pl.program_id(k) == 0 for k in range(len(grid))], True)
        last = functools.reduce(jnp.logical_and, [pl.program_id(k) == grid[k] - 1 for k in range(len(grid))], True)
        if rides:
            @pl.when(first)
            def _():
                for k, r in enumerate(rides):
                    r.start(*part(r, k))

        if any(r.middle for r in rides):
            halfway = functools.reduce(jnp.logical_and, [pl.program_id(k) == 0 for k in range(1, len(grid))],
                                       pl.program_id(0) == grid[0] // 2)

            @pl.when(halfway)
            def _():
                for k, r in enumerate(rides):
                    if r.middle:
                        r.middle(*part(r, k))

        body(*ins, *outs, *scr)
        if rides:
            @pl.when(last)
            def _():
                for k, r in enumerate(rides):
                    r.finish(*part(r, k))

    res = pl.pallas_call(
        wrapped, name=name, grid=grid,
        in_specs=list(in_specs) + [hbm] * len(r_in), out_specs=list(out_specs) + [hbm] * len(r_out),
        out_shape=list(out_shape) + r_out, input_output_aliases=aliases,
        scratch_shapes=list(scratch_shapes) + r_sem,
        compiler_params=pltpu.CompilerParams(dimension_semantics=sem, vmem_limit_bytes=VMEM_LIMIT) if grid else None,
    )(*args, *r_in)
    extras = [list(res[n_out + ko:n_out + ko + len(r.out_shapes)]) for r, (_, ko, _) in zip(rides, spans)]
    return list(res[:n_out]), extras


def _run(ride, name):
    def body(*refs):
        n_in, n_out = len(ride.inputs), len(ride.out_shapes)
        parts = refs[:n_in], refs[n_in:n_in + n_out], refs[n_in + n_out:]
        ride.start(*parts)
        if ride.middle:
            ride.middle(*parts)
        ride.finish(*parts)

    hbm = pl.BlockSpec(memory_space=pl.ANY)
    return list(pl.pallas_call(
        body, name=name, in_specs=[hbm] * len(ride.inputs), out_specs=[hbm] * len(ride.out_shapes),
        out_shape=ride.out_shapes, input_output_aliases=ride.aliases, scratch_shapes=ride.sems,
    )(*ride.inputs))


def _loss_head(hv, gain_ref, tg_ref, loss_ref, dgain_ref):
    d = hv.shape[1]
    r = lax.rsqrt(jnp.mean(hv * hv, axis=-1, keepdims=True) + NORM_EPS)
    xh = hv * r
    err = xh * gain_ref[...] - tg_ref[...]
    sq = _rows8(jnp.square(err))
    loss_ref[...] += 0.5 * functools.reduce(jnp.add, [sq[:, k * LANE:(k + 1) * LANE] for k in range(d // LANE)]) / d
    dy = err / d
    dgain_ref[...] += _rows8(dy * xh)
    dxh = dy * gain_ref[...]
    return r * (dxh - xh * jnp.mean(dxh * xh, axis=-1, keepdims=True))


V7X_MXU_TILE = 256
FFN_FWD_CHUNK_TILES = 3
FFN_BWD_CHUNK_TILES = 4


def _hidden_chunks(f, tiles):
    step = tiles * V7X_MXU_TILE
    return [slice(s, min(s + step, f)) for s in range(0, f, step)]


def _flat(w):
    return w.reshape(w.shape[0] * w.shape[1], w.shape[2])


def _ffn_fwd(x, gain, wg, wu, wd, name, rides=None, head=None):
    t, d = x.shape
    wg, wu, wd = _flat(wg), _flat(wu), _flat(wd)
    f = wg.shape[0]
    tm = min(512, t)
    nh = 0 if head is None else 2

    def body(*refs):
        x_ref, gain_ref = refs[:2]
        wg_hbm, wu_hbm, wd_hbm, h_ref, xn_ref, g_ref, u_ref, a_ref = refs[2 + nh:10 + nh]
        sums = refs[10 + nh:12 + nh]
        wg_v, wu_v, wd_v, sems = refs[-4:]

        @pl.when(pl.program_id(0) == 0)
        def _():
            _load_weights([(wg_hbm, wg_v), (wu_hbm, wu_v), (wd_hbm, wd_v)], sems)
            if head is not None:
                for s_ref in sums:
                    s_ref[...] = jnp.zeros_like(s_ref)

        xv = x_ref[...]
        r = lax.rsqrt(jnp.mean(xv * xv, axis=-1, keepdims=True) + NORM_EPS)
        xn = (xv * r * gain_ref[...]).astype(BF16)
        xn_ref[...] = xn
        acc = jnp.zeros((tm, d), F32)
        for c in _hidden_chunks(f, FFN_FWD_CHUNK_TILES):
            g = _dot(xn, wg_v[c, :], NT)
            u = _dot(xn, wu_v[c, :], NT)
            g_ref[:, c] = g.astype(BF16)
            u_ref[:, c] = u.astype(BF16)
            a = (g * _sigmoid(g) * u).astype(BF16)
            a_ref[:, c] = a
            acc = acc + _dot(a, wd_v[c, :])
        hv = xv + 0.5 * acc
        h_ref[...] = hv if head is None else _loss_head(hv, refs[2], refs[3], *sums)

    hbm = pl.BlockSpec(memory_space=pl.ANY)
    hid = pl.BlockSpec((tm, f), lambda i: (i, 0))
    tile = pl.BlockSpec((tm, d), lambda i: (i, 0))
    row = pl.BlockSpec((1, d), lambda i: (0, 0))
    sums = [] if head is None else [(pl.BlockSpec((8, LANE), lambda i: (0, 0)), jax.ShapeDtypeStruct((8, LANE), F32)),
                                    (pl.BlockSpec((8, d), lambda i: (0, 0)), jax.ShapeDtypeStruct((8, d), F32))]
    return _pallas(
        body, rides, name=name, grid=(t // tm,),
        in_specs=[tile, row] + ([] if head is None else [row, tile]) + [hbm, hbm, hbm],
        out_specs=[tile, tile, hid, hid, hid] + [s for s, _ in sums],
        out_shape=[jax.ShapeDtypeStruct((t, d), F32), jax.ShapeDtypeStruct((t, d), BF16)]
        + [jax.ShapeDtypeStruct((t, f), BF16)] * 3 + [s for _, s in sums],
        scratch_shapes=[pltpu.VMEM(wg.shape, BF16), pltpu.VMEM(wu.shape, BF16), pltpu.VMEM(wd.shape, BF16),
                        pltpu.SemaphoreType.DMA((3,))],
        sem=("arbitrary",), args=[x, gain] + ([] if head is None else list(head)) + [wg, wu, wd])


def _ffn_bwd_data(dy, x, gain, g, u, wg, wu, wd, name, rides=None):
    t, d = x.shape
    wg, wu, wd = _flat(wg), _flat(wu), _flat(wd)
    f = wg.shape[0]
    tm = min(256, t)

    def body(dy_ref, x_ref, gain_ref, g_ref, u_ref, wg_hbm, wu_hbm, wd_hbm, dx_ref, dg_ref, du_ref, dgain_ref,
             wg_v, wu_v, wd_v, sems):
        @pl.when(pl.program_id(0) == 0)
        def _():
            _load_weights([(wg_hbm, wg_v), (wu_hbm, wu_v), (wd_hbm, wd_v)], sems)
            dgain_ref[...] = jnp.zeros_like(dgain_ref)

        dyv = dy_ref[...]
        dyh = (0.5 * dyv).astype(BF16)
        dxn = jnp.zeros((tm, d), F32)
        chunks = _hidden_chunks(f, FFN_BWD_CHUNK_TILES)
        das = [_dot(dyh, wd_v[c, :], NT) for c in chunks]
        for c, da in zip(chunks, das):
            gj = g_ref[:, c].astype(F32)
            uj = u_ref[:, c].astype(F32)
            sig = _sigmoid(gj)
            dgj = (da * uj * (sig * (1.0 + gj * (1.0 - sig)))).astype(BF16)
            duj = (da * (gj * sig)).astype(BF16)
            dg_ref[:, c] = dgj
            du_ref[:, c] = duj
            dxn = dxn + _dot(dgj, wg_v[c, :]) + _dot(duj, wu_v[c, :])
        xv = x_ref[...]
        r = lax.rsqrt(jnp.mean(xv * xv, axis=-1, keepdims=True) + NORM_EPS)
        xh = xv * r
        dgain_ref[...] += _rows8(dxn * xh)
        dxh = dxn * gain_ref[...]
        dx_ref[...] = dyv + r * (dxh - xh * jnp.mean(dxh * xh, axis=-1, keepdims=True))

    hbm = pl.BlockSpec(memory_space=pl.ANY)
    tile = pl.BlockSpec((tm, d), lambda i: (i, 0))
    hid = pl.BlockSpec((tm, f), lambda i: (i, 0))
    return _pallas(
        body, rides, name=name, grid=(t // tm,),
        in_specs=[tile, tile, pl.BlockSpec((1, d), lambda i: (0, 0)), hid, hid, hbm, hbm, hbm],
        out_specs=[tile, hid, hid, pl.BlockSpec((8, d), lambda i: (0, 0))],
        out_shape=[jax.ShapeDtypeStruct((t, d), F32), jax.ShapeDtypeStruct((t, f), BF16),
                   jax.ShapeDtypeStruct((t, f), BF16), jax.ShapeDtypeStruct((8, d), F32)],
        scratch_shapes=[pltpu.VMEM(wg.shape, BF16), pltpu.VMEM(wu.shape, BF16), pltpu.VMEM(wd.shape, BF16),
                        pltpu.SemaphoreType.DMA((3,))],
        sem=("arbitrary",), args=[dy, x, gain, g, u, wg, wu, wd])


WGRAD_ROW_BLOCKS = 2


def _ffn_wgrad_down(a, dy, name, rides=None):
    t, d = dy.shape
    f = a.shape[1]
    fb = f // WGRAD_ROW_BLOCKS
    tk = min(1024, t)

    def body(dy_ref, a_ref, dwd_ref):
        @pl.when(pl.program_id(1) == 0)
        def _():
            dwd_ref[...] = jnp.zeros_like(dwd_ref)

        dwd_ref[...] += _dot(a_ref[...], (0.5 * dy_ref[...]).astype(BF16), TN)

    return _pallas(
        body, rides, name=name, grid=(WGRAD_ROW_BLOCKS, t // tk),
        in_specs=[pl.BlockSpec((tk, d), lambda j, k: (k, 0)), pl.BlockSpec((tk, fb), lambda j, k: (k, j))],
        out_specs=[pl.BlockSpec((fb, d), lambda j, k: (j, 0))],
        out_shape=[jax.ShapeDtypeStruct((f, d), F32)],
        sem=("arbitrary", "arbitrary"), args=[dy, a])


def _ffn_wgrad_gu(xn, dhs, name, rides=None):
    t, d = xn.shape
    n = len(dhs)
    f = dhs[0].shape[1]
    fb = f // WGRAD_ROW_BLOCKS
    tk = min(2048 // n, t)

    def body(xn_ref, *refs):
        @pl.when(pl.program_id(1) == 0)
        def _():
            for o_ref in refs[n:]:
                o_ref[...] = jnp.zeros_like(o_ref)

        xnv = xn_ref[...]
        for dh_ref, o_ref in zip(refs[:n], refs[n:]):
            o_ref[...] += _dot(dh_ref[...], xnv, TN)

    hid = pl.BlockSpec((tk, fb), lambda j, k: (k, j))
    out = pl.BlockSpec((fb, d), lambda j, k: (j, 0))
    return _pallas(
        body, rides, name=name, grid=(WGRAD_ROW_BLOCKS, t // tk),
        in_specs=[pl.BlockSpec((tk, d), lambda j, k: (k, 0))] + [hid] * n,
        out_specs=[out] * n, out_shape=[jax.ShapeDtypeStruct((f, d), F32)] * n,
        sem=("arbitrary", "arbitrary"), args=[xn] + list(dhs))


def _tn_matmul(a, b, bn, name):
    t, m = a.shape
    n = b.shape[1]
    tk = min(2048, t)

    def body(a_ref, b_ref, o_ref):
        @pl.when(pl.program_id(1) == 0)
        def _():
            o_ref[...] = jnp.zeros_like(o_ref)

        o_ref[...] += _dot(a_ref[...].astype(BF16), b_ref[...].astype(BF16), TN)

    return pl.pallas_call(
        body, name=name, grid=(n // bn, t // tk),
        in_specs=[pl.BlockSpec((tk, m), lambda j, k: (k, 0)), pl.BlockSpec((tk, bn), lambda j, k: (k, j))],
        out_specs=pl.BlockSpec((None, m, bn), lambda j, k: (j, 0, 0)),
        out_shape=jax.ShapeDtypeStruct((n // bn, m, bn), F32),
        compiler_params=_params("arbitrary", "arbitrary"),
    )(a, b)


def _chunk_scratch(tm, w):
    return pltpu.VMEM((w // LANE, tm, LANE), F32)


def _regroup_store(cbuf, out_ref, dil, chunks=None):
    n = out_ref.shape[1]
    for k in range(cbuf.shape[0]) if chunks is None else chunks:
        for g in range(dil):
            rows = cbuf[k] if dil == 1 else cbuf[k, pl.ds(g, n, stride=dil), :]
            out_ref[g, :, k * LANE:(k + 1) * LANE] = rows.astype(out_ref.dtype)


def _natural_rows(ref, dil, cbuf):
    if dil == 1:
        return ref[0].astype(F32)
    n = ref.shape[1]
    for g in range(dil):
        for k in range(cbuf.shape[0]):
            cbuf[k, pl.ds(g, n, stride=dil), :] = ref[g, :, k * LANE:(k + 1) * LANE].astype(F32)
    return jnp.concatenate([cbuf[k] for k in range(cbuf.shape[0])], axis=1)


IN_CHUNK_TILES = 7


def _column_chunks(n):
    step = IN_CHUNK_TILES * V7X_MXU_TILE
    return [slice(s, min(s + step, n)) for s in range(0, n, step)]


def _load_side_by_side(w_hbm, w_v, sems):
    ns, _, cs = w_hbm.shape
    copies = [pltpu.make_async_copy(w_hbm.at[j], w_v.at[:, pl.ds(j * cs, cs)], sems.at[j]) for j in range(ns)]
    for cp in copies:
        cp.start()
    for cp in copies:
        cp.wait()


def _inproj_fwd(h, gain, win):
    t, d = h.shape
    ns, _, cs = win.shape
    tm = min(512, t)
    rw, aw = 4 * RET_WIDTH, 3 * ATT_WIDTH

    def body(h_ref, gain_ref, w_hbm, xn_ref, ur_ref, *rest):
        a_refs, abuf, w_v, sems = rest[:-3], rest[-3], rest[-2], rest[-1]

        @pl.when(pl.program_id(0) == 0)
        def _():
            _load_side_by_side(w_hbm, w_v, sems)

        hv = h_ref[...]
        r = lax.rsqrt(jnp.mean(hv * hv, axis=-1, keepdims=True) + NORM_EPS)
        xn = (hv * r * gain_ref[...]).astype(BF16)
        xn_ref[...] = xn
        for c in reversed(_column_chunks(ns * cs)):
            res = _dot(xn, w_v[:, c])
            mine = []
            for k in range((c.stop - c.start) // LANE):
                chunk = c.start // LANE + k
                piece = res[:, k * LANE:(k + 1) * LANE]
                if chunk < rw // LANE:
                    ur_ref[:, chunk * LANE:(chunk + 1) * LANE] = piece
                else:
                    abuf[chunk - rw // LANE] = piece
                    mine.append(chunk - rw // LANE)
            for dil, a_ref in zip(DILATIONS, a_refs):
                _regroup_store(abuf, a_ref, dil, mine)

    return pl.pallas_call(
        body, name="inproj_fwd", grid=(t // tm,),
        in_specs=[pl.BlockSpec((tm, d), lambda i: (i, 0)), pl.BlockSpec((1, d), lambda i: (0, 0)),
                  pl.BlockSpec(memory_space=pl.ANY)],
        out_specs=[pl.BlockSpec((tm, d), lambda i: (i, 0)), pl.BlockSpec((tm, rw), lambda i: (i, 0))]
        + [pl.BlockSpec((dil, tm // dil, aw), lambda i: (0, i, 0)) for dil in DILATIONS],
        out_shape=[jax.ShapeDtypeStruct((t, d), BF16), jax.ShapeDtypeStruct((t, rw), F32)]
        + [jax.ShapeDtypeStruct((dil, t // dil, aw), BF16) for dil in DILATIONS],
        scratch_shapes=[_chunk_scratch(tm, aw), pltpu.VMEM((d, ns * cs), BF16), pltpu.SemaphoreType.DMA((ns,))],
        compiler_params=_params("arbitrary"),
    )(h, gain, win)


def _inproj_bwd(pieces, parts, h, gain, dres, win):
    t, d = h.shape
    ns, _, cs = win.shape
    pw = pieces[0].shape[1]
    tm = min(512, t)
    npc, nk = len(pieces), len(parts[0])
    flat_parts = [a for p in parts for a in p]

    def body(*refs):
        p_refs, a_refs = refs[:npc], refs[npc:npc + len(flat_parts)]
        h_ref, gain_ref, dres_ref, w_hbm, dh_ref, du_ref, dgain_ref, buf, w_v, sems = refs[npc + len(flat_parts):]

        @pl.when(pl.program_id(0) == 0)
        def _():
            _load_side_by_side(w_hbm, w_v, sems)
            dgain_ref[...] = jnp.zeros_like(dgain_ref)

        for k in range(npc):
            du_ref[:, k * pw:(k + 1) * pw] = p_refs[k][...]
        for k in range(nk):
            acc = None
            for b, dil in enumerate(DILATIONS):
                rows = _natural_rows(a_refs[b * nk + k], dil, buf)
                acc = rows if acc is None else acc + rows
            du_ref[:, (npc + k) * pw:(npc + k + 1) * pw] = acc.astype(BF16)
        dxn = jnp.zeros((tm, d), F32)
        for c in _column_chunks(ns * cs):
            dxn = dxn + _dot(du_ref[:, c], w_v[:, c], NT)
        hv = h_ref[...]
        r = lax.rsqrt(jnp.mean(hv * hv, axis=-1, keepdims=True) + NORM_EPS)
        xh = hv * r
        dgain_ref[...] += _rows8(dxn * xh)
        dxh = dxn * gain_ref[...]
        dh_ref[...] = dres_ref[...] + r * (dxh - xh * jnp.mean(dxh * xh, axis=-1, keepdims=True))

    tile = pl.BlockSpec((tm, d), lambda i: (i, 0))
    cols = (npc + nk) * pw
    return pl.pallas_call(
        body, name="inproj_bwd", grid=(t // tm,),
        in_specs=[pl.BlockSpec((tm, pw), lambda i: (i, 0))] * npc
        + [_regrouped_spec(tm, dil, pw) for dil in DILATIONS for _ in range(nk)]
        + [tile, pl.BlockSpec((1, d), lambda i: (0, 0)), tile, pl.BlockSpec(memory_space=pl.ANY)],
        out_specs=[tile, pl.BlockSpec((tm, cols), lambda i: (i, 0)), pl.BlockSpec((8, d), lambda i: (0, 0))],
        out_shape=[jax.ShapeDtypeStruct((t, d), F32), jax.ShapeDtypeStruct((t, cols), BF16),
                   jax.ShapeDtypeStruct((8, d), F32)],
        scratch_shapes=[_chunk_scratch(tm, pw), pltpu.VMEM((d, ns * cs), BF16), pltpu.SemaphoreType.DMA((ns,))],
        compiler_params=_params("arbitrary"),
    )(*pieces, *flat_parts, h, gain, dres, win)


def _outproj_fwd(h, mix_r, mix_a, wo):
    t, d = h.shape
    hw = mix_r.shape[1]
    tm = min(512, t)

    def body(h_ref, mr_ref, ma_ref, w_ref, o_ref):
        o_ref[...] = h_ref[...] + _dot(mr_ref[...], w_ref[0:hw, :]) + _dot(ma_ref[...], w_ref[hw:2 * hw, :])

    tile = pl.BlockSpec((tm, d), lambda i: (i, 0))
    half = pl.BlockSpec((tm, hw), lambda i: (i, 0))
    return pl.pallas_call(
        body, name="outproj_fwd", grid=(t // tm,),
        in_specs=[tile, half, half, pl.BlockSpec(wo.shape, lambda i: (0, 0))],
        out_specs=tile, out_shape=jax.ShapeDtypeStruct((t, d), F32),
        compiler_params=_params("arbitrary"),
    )(h, mix_r, mix_a, wo)


def _outproj_bwd(dh, wo, rides=None):
    t, d = dh.shape
    hw = wo.shape[0] // 2
    tm = min(512, t)

    def body(dh_ref, w_ref, dr_ref, da_ref):
        dhb = dh_ref[...].astype(BF16)
        dr_ref[...] = _dot(dhb, w_ref[0:hw, :], NT)
        da_ref[...] = _dot(dhb, w_ref[hw:2 * hw, :], NT)

    half = pl.BlockSpec((tm, hw), lambda i: (i, 0))
    return _pallas(
        body, rides, name="outproj_bwd", grid=(t // tm,),
        in_specs=[pl.BlockSpec((tm, d), lambda i: (i, 0)), pl.BlockSpec(wo.shape, lambda i: (0, 0))],
        out_specs=[half, half],
        out_shape=[jax.ShapeDtypeStruct((t, hw), F32), jax.ShapeDtypeStruct((t, hw), F32)],
        sem=("arbitrary",), args=[dh, wo])


def _retention_tables(t):
    pos = jnp.arange(t, dtype=F32)
    pair = (jnp.arange(RET_DIM) // 2 * 2).astype(F32)
    ang = pos[:, None] * (ROPE_BASE ** (-pair / RET_DIM))[None, :]
    c = RET_CHUNK
    log_g = jnp.log(1.0 - 2.0 ** (-5.0 - jnp.arange(RET_HEADS, dtype=F32)))
    idx = jnp.arange(c, dtype=F32)
    rel = idx[:, None] - idx[None, :]
    decay = jnp.where(rel >= 0, jnp.exp(log_g[:, None, None] * jnp.maximum(rel, 0.0)), 0.0)
    zeta = jnp.exp(log_g[:, None] * (c - 1 - idx)[None, :])
    xi = jnp.exp(log_g[:, None] * (idx + 1)[None, :])
    gc = jnp.exp(log_g * c)
    wide = lambda v: jnp.broadcast_to(v[:, :, None], (RET_HEADS, c, LANE))
    return (jnp.cos(ang), jnp.sin(ang), decay, wide(zeta), wide(xi),
            jnp.broadcast_to(gc[:, None, None], (RET_HEADS, c, LANE)))


def _rot(v):
    lane = lax.broadcasted_iota(jnp.int32, v.shape, 1)
    nxt = pltpu.roll(v, LANE - 1, 1)
    prv = pltpu.roll(v, 1, 1)
    return jnp.where(lane % 2 == 0, -nxt, prv)


def _ret_specs(tr, rev, nt):
    ti = (lambda i: nt - 1 - i) if rev else (lambda i: i)
    col = lambda blk: pl.BlockSpec((tr, RET_WIDTH), lambda i: (ti(i), blk))
    tab = pl.BlockSpec((tr, LANE), lambda i: (ti(i), 0))
    head = pl.BlockSpec((RET_HEADS, RET_CHUNK, LANE), lambda i: (0, 0, 0))
    return col, tab, head


def _ret_chunks(tr, rev=False):
    order = list(range(tr // RET_CHUNK))
    return [(pl.ds(ci * RET_CHUNK, RET_CHUNK), slice(h * RET_DIM, (h + 1) * RET_DIM), h)
            for h in range(RET_HEADS) for ci in (reversed(order) if rev else order)]


def _ret_operands(items, q_ref, k_ref, v_ref, cos_ref, sin_ref, zeta_ref):
    scale = RET_DIM ** -0.5
    qbs, kbs, vbs, kzs = [], [], [], []
    for sl, hs, h in items:
        cs, sn = cos_ref[sl, :], sin_ref[sl, :]
        q, k = q_ref[sl, hs], k_ref[sl, hs]
        kr = (k * cs + _rot(k) * sn) * scale
        qbs.append((q * cs + _rot(q) * sn).astype(BF16))
        kbs.append(kr.astype(BF16))
        vbs.append(v_ref[sl, hs].astype(BF16))
        kzs.append((kr * zeta_ref[h]).astype(BF16))
    return qbs, kbs, vbs, kzs


def _ret_states(items, state, steps, gc_ref):
    cur, befores = {}, []
    for (sl, hs, h), step in zip(items, steps):
        st = cur[h] if h in cur else state[h]
        befores.append(st)
        cur[h] = st * gc_ref[h] + step
    for h, st in cur.items():
        state[h] = st
    return befores


def _ret_fwd(u, gain, tabs):
    t = u.shape[0]
    tr = min(512, t)
    nt = t // tr
    cos, sin, decay, zeta, xi, gc = tabs

    def body(q_ref, k_ref, v_ref, gt_ref, cos_ref, sin_ref, gain_ref, dec_ref, zeta_ref, xi_ref, gc_ref,
             raw_ref, mix_ref, state):
        @pl.when(pl.program_id(0) == 0)
        def _():
            state[...] = jnp.zeros_like(state)

        items = _ret_chunks(tr)
        n = range(len(items))
        qbs, kbs, vbs, kzs = _ret_operands(items, q_ref, k_ref, v_ref, cos_ref, sin_ref, zeta_ref)
        ss = [_dot(qbs[i], kbs[i], NT) for i in n]
        kvs = [_dot(kzs[i], vbs[i], TN) for i in n]
        befores = _ret_states(items, state, kvs, gc_ref)
        intra = [_dot((ss[i] * dec_ref[items[i][2]]).astype(BF16), vbs[i]) for i in n]
        inter = [_dot(qbs[i], befores[i].astype(BF16)) for i in n]
        for i, (sl, hs, h) in enumerate(items):
            o = intra[i] + inter[i] * xi_ref[h]
            raw_ref[sl, hs] = o
            mu = jnp.mean(o, axis=-1, keepdims=True)
            var = jnp.mean(jnp.square(o - mu), axis=-1, keepdims=True)
            y = (o - mu) * lax.rsqrt(var + GN_EPS) * gain_ref[:, hs]
            gt = gt_ref[sl, hs]
            mix_ref[sl, hs] = (y * (gt * _sigmoid(gt))).astype(BF16)

    col, tab, head = _ret_specs(tr, False, nt)
    out = pl.BlockSpec((tr, RET_WIDTH), lambda i: (i, 0))
    return pl.pallas_call(
        body, name="ret_fwd", grid=(nt,),
        in_specs=[col(0), col(1), col(2), col(3), tab, tab, pl.BlockSpec((1, RET_WIDTH), lambda i: (0, 0)),
                  head, head, head, head],
        out_specs=[out, out],
        out_shape=[jax.ShapeDtypeStruct((t, RET_WIDTH), F32), jax.ShapeDtypeStruct((t, RET_WIDTH), BF16)],
        scratch_shapes=[pltpu.VMEM((RET_HEADS, RET_DIM, RET_DIM), F32)],
        compiler_params=_params("arbitrary"),
    )(u, u, u, u, cos, sin, gain, decay, zeta, xi, gc)


def _ret_bwd_q(dmix, raw, u, gain, tabs, rides=None):
    t = u.shape[0]
    tr = min(512, t)
    nt = t // tr
    cos, sin, decay, zeta, xi, gc = tabs

    def body(dm_ref, raw_ref, q_ref, k_ref, v_ref, gt_ref, cos_ref, sin_ref, gain_ref, dec_ref, zeta_ref, xi_ref, gc_ref,
             dq_ref, dgt_ref, dret_ref, dgain_ref, state):
        @pl.when(pl.program_id(0) == 0)
        def _():
            state[...] = jnp.zeros_like(state)
            dgain_ref[...] = jnp.zeros_like(dgain_ref)

        items = _ret_chunks(tr)
        n_items = range(len(items))
        qbs, kbs, vbs, kzs = _ret_operands(items, q_ref, k_ref, v_ref, cos_ref, sin_ref, zeta_ref)
        dos, dgains = [], {}
        for sl, hs, h in items:
            o = raw_ref[sl, hs]
            mu = jnp.mean(o, axis=-1, keepdims=True)
            var = jnp.mean(jnp.square(o - mu), axis=-1, keepdims=True)
            rs = lax.rsqrt(var + GN_EPS)
            n = (o - mu) * rs
            gt = gt_ref[sl, hs]
            sig = _sigmoid(gt)
            dout = dm_ref[sl, hs]
            gain_h = gain_ref[:, hs]
            dgt_ref[sl, hs] = (dout * (n * gain_h) * (sig * (1.0 + gt * (1.0 - sig)))).astype(BF16)
            dy = dout * (gt * sig)
            dgains[h] = dgains[h] + _rows8(dy * n) if h in dgains else _rows8(dy * n)
            dn = dy * gain_h
            do = rs * (dn - jnp.mean(dn, axis=-1, keepdims=True) - n * jnp.mean(dn * n, axis=-1, keepdims=True))
            dret_ref[sl, hs] = do
            dos.append(do)
        for h, dg in dgains.items():
            dgain_ref[:, h * RET_DIM:(h + 1) * RET_DIM] += dg
        dss = [_dot(dos[i].astype(BF16), vbs[i], NT) for i in n_items]
        kvs = [_dot(kzs[i], vbs[i], TN) for i in n_items]
        befores = _ret_states(items, state, kvs, gc_ref)
        intra = [_dot((dss[i] * dec_ref[items[i][2]]).astype(BF16), kbs[i]) for i in n_items]
        inter = [_dot((dos[i] * xi_ref[items[i][2]]).astype(BF16), befores[i].astype(BF16), NT) for i in n_items]
        for i, (sl, hs, h) in enumerate(items):
            dqr = intra[i] + inter[i]
            dq_ref[sl, hs] = (dqr * cos_ref[sl, :] - _rot(dqr * sin_ref[sl, :])).astype(BF16)

    col, tab, head = _ret_specs(tr, False, nt)
    out = pl.BlockSpec((tr, RET_WIDTH), lambda i: (i, 0))
    return _pallas(
        body, rides, name="ret_bwd_q", grid=(nt,),
        in_specs=[out, out, col(0), col(1), col(2), col(3), tab, tab, pl.BlockSpec((1, RET_WIDTH), lambda i: (0, 0)),
                  head, head, head, head],
        out_specs=[out, out, out, pl.BlockSpec((8, RET_WIDTH), lambda i: (0, 0))],
        out_shape=[jax.ShapeDtypeStruct((t, RET_WIDTH), BF16), jax.ShapeDtypeStruct((t, RET_WIDTH), BF16),
                   jax.ShapeDtypeStruct((t, RET_WIDTH), F32), jax.ShapeDtypeStruct((8, RET_WIDTH), F32)],
        scratch_shapes=[pltpu.VMEM((RET_HEADS, RET_DIM, RET_DIM), F32)],
        sem=("arbitrary",), args=[dmix, raw, u, u, u, u, cos, sin, gain, decay, zeta, xi, gc])


def _ret_bwd_kv(dret, u, tabs, rides=None):
    t = u.shape[0]
    tr = min(512, t)
    nt = t // tr
    cos, sin, decay, zeta, xi, gc = tabs
    scale = RET_DIM ** -0.5

    def body(do_ref, q_ref, k_ref, v_ref, cos_ref, sin_ref, dec_ref, zeta_ref, xi_ref, gc_ref, dk_ref, dv_ref, gst):
        @pl.when(pl.program_id(0) == 0)
        def _():
            gst[...] = jnp.zeros_like(gst)

        items = _ret_chunks(tr, rev=True)
        n = range(len(items))
        qbs, kbs, vbs, kzs = _ret_operands(items, q_ref, k_ref, v_ref, cos_ref, sin_ref, zeta_ref)
        dos = [do_ref[sl, hs] for sl, hs, h in items]
        dobs = [do.astype(BF16) for do in dos]
        ss = [_dot(qbs[i], kbs[i], NT) for i in n]
        dss = [_dot(dobs[i], vbs[i], NT) for i in n]
        steps = [_dot(qbs[i], (dos[i] * xi_ref[items[i][2]]).astype(BF16), TN) for i in n]
        afters = [g.astype(BF16) for g in _ret_states(items, gst, steps, gc_ref)]
        dvs = [_dot((ss[i] * dec_ref[items[i][2]]).astype(BF16), dobs[i], TN) + _dot(kzs[i], afters[i]) for i in n]
        dks = [_dot((dss[i] * dec_ref[items[i][2]]).astype(BF16), qbs[i], TN) for i in n]
        dkz = [_dot(vbs[i], afters[i], NT) for i in n]
        for i, (sl, hs, h) in enumerate(items):
            dv_ref[sl, hs] = dvs[i].astype(BF16)
            dkr = (dks[i] + dkz[i] * zeta_ref[h]) * scale
            dk_ref[sl, hs] = (dkr * cos_ref[sl, :] - _rot(dkr * sin_ref[sl, :])).astype(BF16)

    col, tab, head = _ret_specs(tr, True, nt)
    out = pl.BlockSpec((tr, RET_WIDTH), lambda i: (nt - 1 - i, 0))
    return _pallas(
        body, rides, name="ret_bwd_kv", grid=(nt,),
        in_specs=[out, col(0), col(1), col(2), tab, tab, head, head, head, head],
        out_specs=[out, out],
        out_shape=[jax.ShapeDtypeStruct((t, RET_WIDTH), BF16), jax.ShapeDtypeStruct((t, RET_WIDTH), BF16)],
        scratch_shapes=[pltpu.VMEM((RET_HEADS, RET_DIM, RET_DIM), F32)],
        sem=("arbitrary",), args=[dret, u, u, u, cos, sin, decay, zeta, xi, gc])


PAIRS = ATT_WIDTH // LANE
ATT_Q_BLK, ATT_K_BLK, ATT_V_BLK = 0, PAIRS, 2 * PAIRS
STAT_LANES = ATT_DIM // 2


ATT_STEP_ROWS = 4096


def _att_tiles(t, dil):
    sub = t // dil
    tq = min(ATT_STEP_ROWS, sub)
    return sub, tq, sub // tq, tq // ATT_BLOCK, min(dil, ATT_STEP_ROWS // tq)


def _att_in_specs(tq, qb, ti, gs):
    cur = lambda off: pl.BlockSpec((gs, tq, LANE), lambda g, p, i: (g, ti(i), off + p))
    prev = lambda off: pl.BlockSpec((gs, ATT_BLOCK, LANE), lambda g, p, i: (g, jnp.maximum(ti(i) * qb - 1, 0), off + p))
    return [cur(ATT_Q_BLK), cur(ATT_K_BLK), prev(ATT_K_BLK), cur(ATT_V_BLK), prev(ATT_V_BLK)]


def _band_mask():
    key = lax.broadcasted_iota(jnp.int32, (2 * ATT_BLOCK, 2 * ATT_BLOCK), 0)
    qry = lax.broadcasted_iota(jnp.int32, (2 * ATT_BLOCK, 2 * ATT_BLOCK), 1) % ATT_BLOCK
    dist = qry + ATT_BLOCK - key
    return (dist >= 0) & (dist <= ATT_BLOCK), key >= ATT_BLOCK


def _head0_lanes():
    return lax.broadcasted_iota(jnp.int32, (ATT_BLOCK, LANE), 1) < ATT_DIM


def _stack_heads(v, head0):
    zero = jnp.zeros((), v.dtype)
    return jnp.concatenate([jnp.where(head0, v, zero), jnp.where(head0, zero, v)], axis=0)


def _unstack_heads(v, head0):
    return jnp.where(head0, v[0:ATT_BLOCK], v[ATT_BLOCK:])


def _att_fwd(ua, dil):
    sub = ua.shape[1]
    _, tq, nq, qb, gs = _att_tiles(sub * dil, dil)

    def body(q_ref, kc_ref, kp_ref, vc_ref, vp_ref, o_ref, l_ref, kx, vx):
        tile = pl.program_id(2)
        kx[:, 0:ATT_BLOCK, :] = kp_ref[...]
        kx[:, ATT_BLOCK:, :] = kc_ref[...]
        vx[:, 0:ATT_BLOCK, :] = vp_ref[...]
        vx[:, ATT_BLOCK:, :] = vc_ref[...]
        band, cur_keys = _band_mask()
        head0 = _head0_lanes()
        items = [(r, b) for r in range(gs) for b in range(qb)]
        rows = lambda b: slice(b * ATT_BLOCK, (b + 1) * ATT_BLOCK)
        keys = lambda b: slice(b * ATT_BLOCK, (b + 2) * ATT_BLOCK)
        sts = [_dot(kx[r, keys(b), :], _stack_heads(q_ref[r, rows(b), :] * jnp.asarray(ATT_DIM ** -0.5, BF16), head0), NT)
               for r, b in items]
        pts, lses = [], []
        for (r, b), st in zip(items, sts):
            mask = band if b > 0 else band & (cur_keys | (tile > 0))
            st = jnp.where(mask, st, -1e30)
            m = jnp.max(st, axis=0, keepdims=True)
            ex = jnp.exp(st - m)
            den = jnp.sum(ex, axis=0, keepdims=True)
            pts.append((ex * (1.0 / den)).astype(BF16))
            lses.append(m + jnp.log(den))
        outs = [_dot(pt, vx[r, keys(b), :], TN) for (r, b), pt in zip(items, pts)]
        for (r, b), out, lse in zip(items, outs, lses):
            o_ref[r, rows(b), :] = _unstack_heads(out, head0).astype(BF16)
            cols = [jnp.broadcast_to(lse[:, e * ATT_BLOCK:(e + 1) * ATT_BLOCK], (ATT_BLOCK, LANE)).T for e in range(2)]
            l_ref[r, rows(b), :] = jnp.where(head0, cols[0], cols[1])

    out = pl.BlockSpec((gs, tq, LANE), lambda g, p, i: (g, i, p))
    return pl.pallas_call(
        body, name=f"att_fwd_d{dil}", grid=(dil // gs, PAIRS, nq),
        in_specs=_att_in_specs(tq, qb, lambda i: i, gs),
        out_specs=[out, out],
        out_shape=[jax.ShapeDtypeStruct((dil, sub, ATT_WIDTH), BF16), jax.ShapeDtypeStruct((dil, sub, ATT_WIDTH), F32)],
        scratch_shapes=[pltpu.VMEM((gs, tq + ATT_BLOCK, LANE), BF16)] * 2,
        compiler_params=_params("arbitrary", "arbitrary", "arbitrary"),
    )(ua, ua, ua, ua, ua)


def _regrouped_spec(tm, dil, w):
    return pl.BlockSpec((dil, tm // dil, w), lambda i: (0, i, 0))


def _att_combine(outs, lses, t):
    w = ATT_WIDTH
    tm = min(512, t)
    nb = len(outs)

    def body(*refs):
        o_refs, l_refs = refs[:nb], refs[nb:2 * nb]
        mix_ref, att_ref, lse_ref, buf = refs[2 * nb:]
        ls = [_natural_rows(r, dil, buf) for r, dil in zip(l_refs, DILATIONS)]
        m = functools.reduce(jnp.maximum, ls)
        ws = [jnp.exp(l - m) for l in ls]
        den = functools.reduce(jnp.add, ws)
        att = functools.reduce(jnp.add, [(wt / den) * _natural_rows(r, dil, buf) for wt, r, dil in zip(ws, o_refs, DILATIONS)])
        att_ref[...] = att
        mix_ref[...] = att.astype(BF16)
        lse_ref[...] = m + jnp.log(den)

    tile = pl.BlockSpec((tm, w), lambda i: (i, 0))
    regrouped = [_regrouped_spec(tm, dil, w) for dil in DILATIONS]
    return pl.pallas_call(
        body, name="att_combine", grid=(t // tm,),
        in_specs=regrouped * 2, out_specs=[tile, tile, tile],
        out_shape=[jax.ShapeDtypeStruct((t, w), BF16), jax.ShapeDtypeStruct((t, w), F32), jax.ShapeDtypeStruct((t, w), F32)],
        scratch_shapes=[_chunk_scratch(tm, w)],
        compiler_params=_params("arbitrary"),
    )(*outs, *lses)


def _att_bwd_prep(datt, att, lse):
    t, w = datt.shape
    tm = min(512, t)

    def body(da_ref, at_ref, l_ref, *rest):
        outs, dbuf, sbuf = rest[:-2], rest[-2], rest[-1]
        dav = da_ref[...]
        prod = dav * at_ref[...]
        lane = lax.broadcasted_iota(jnp.int32, (tm, LANE), 1)
        for k in range(w // LANE):
            cols = slice(k * LANE, (k + 1) * LANE)
            dbuf[k] = dav[:, cols]
            delta = jnp.concatenate(
                [jnp.broadcast_to(jnp.sum(prod[:, k * LANE + e * ATT_DIM:k * LANE + (e + 1) * ATT_DIM], axis=-1, keepdims=True),
                                  (tm, ATT_DIM)) for e in range(LANE // ATT_DIM)], axis=1)
            sbuf[k] = jnp.where(lane % ATT_DIM < STAT_LANES, l_ref[:, cols], delta)
        for k, dil in enumerate(DILATIONS):
            _regroup_store(dbuf, outs[2 * k], dil)
            _regroup_store(sbuf, outs[2 * k + 1], dil)

    tile = pl.BlockSpec((tm, w), lambda i: (i, 0))
    res = pl.pallas_call(
        body, name="att_bwd_prep", grid=(t // tm,),
        in_specs=[tile] * 3,
        out_specs=[_regrouped_spec(tm, dil, w) for dil in DILATIONS for _ in range(2)],
        out_shape=[jax.ShapeDtypeStruct((dil, t // dil, w), dt) for dil in DILATIONS for dt in (BF16, F32)],
        scratch_shapes=[_chunk_scratch(tm, w)] * 2,
        compiler_params=_params("arbitrary"),
    )(datt, att, lse)
    return [(res[2 * k], res[2 * k + 1]) for k in range(len(DILATIONS))]


def _att_bwd(ua, da, stat, dil, rides=None):
    sub = ua.shape[1]
    _, tq, nq, qb, gs = _att_tiles(sub * dil, dil)
    scale = ATT_DIM ** -0.5

    def body(q_ref, kc_ref, kp_ref, vc_ref, vp_ref, da_ref, st_ref, dq_ref, dk_ref, dv_ref, kx, vx, ck, cv):
        step = pl.program_id(2)
        tile = nq - 1 - step

        @pl.when(step == 0)
        def _():
            ck[...] = jnp.zeros_like(ck)
            cv[...] = jnp.zeros_like(cv)

        kx[:, 0:ATT_BLOCK, :] = kp_ref[...]
        kx[:, ATT_BLOCK:, :] = kc_ref[...]
        vx[:, 0:ATT_BLOCK, :] = vp_ref[...]
        vx[:, ATT_BLOCK:, :] = vc_ref[...]
        band, cur_keys = _band_mask()
        head0 = _head0_lanes()
        items = [(r, b) for r in range(gs) for b in range(qb)]
        n = range(len(items))
        rows = lambda b: slice(b * ATT_BLOCK, (b + 1) * ATT_BLOCK)
        keys = lambda b: slice(b * ATT_BLOCK, (b + 2) * ATT_BLOCK)
        qqs = [_stack_heads(q_ref[r, rows(b), :] * jnp.asarray(scale, BF16), head0) for r, b in items]
        dds = [_stack_heads(da_ref[r, rows(b), :], head0) for r, b in items]
        sts = [_dot(kx[r, keys(b), :], qqs[i], NT) for i, (r, b) in enumerate(items)]
        dpts = [_dot(vx[r, keys(b), :], dds[i], NT) for i, (r, b) in enumerate(items)]
        pts, dsts = [], []
        for i, (r, b) in enumerate(items):
            mask = band if b > 0 else band & (cur_keys | (tile > 0))
            stat = st_ref[r, rows(b), :].T
            row = lambda k: jnp.concatenate([stat[e * ATT_DIM + k:e * ATT_DIM + k + 1, :] for e in range(2)], axis=1)
            pt = jnp.where(mask, jnp.exp(sts[i] - row(0)), 0.0)
            dsts.append((pt * (dpts[i] - row(STAT_LANES))).astype(BF16))
            pts.append(pt.astype(BF16))
        dqs = [_dot(dsts[i], kx[r, keys(b), :], TN) for i, (r, b) in enumerate(items)]
        dkbs = [_dot(dsts[i], qqs[i]) for i in n]
        dvbs = [_dot(pts[i], dds[i]) for i in n]
        for i, (r, b) in enumerate(items):
            dq_ref[r, rows(b), :] = (_unstack_heads(dqs[i], head0) * scale).astype(BF16)
            if b > 0:
                dk_ref[r, rows(b - 1), :] = (dkbs[i - 1][ATT_BLOCK:] + dkbs[i][0:ATT_BLOCK]).astype(BF16)
                dv_ref[r, rows(b - 1), :] = (dvbs[i - 1][ATT_BLOCK:] + dvbs[i][0:ATT_BLOCK]).astype(BF16)
        for r in range(gs):
            first, last = r * qb, r * qb + qb - 1
            dk_ref[r, rows(qb - 1), :] = (dkbs[last][ATT_BLOCK:] + ck[r]).astype(BF16)
            dv_ref[r, rows(qb - 1), :] = (dvbs[last][ATT_BLOCK:] + cv[r]).astype(BF16)
            ck[r] = dkbs[first][0:ATT_BLOCK]
            cv[r] = dvbs[first][0:ATT_BLOCK]

    ti = lambda i: nq - 1 - i
    out = pl.BlockSpec((gs, tq, LANE), lambda g, p, i: (g, ti(i), p))
    shape = jax.ShapeDtypeStruct((dil, sub, ATT_WIDTH), BF16)
    return _pallas(
        body, rides, name=f"att_bwd_d{dil}", grid=(dil // gs, PAIRS, nq),
        in_specs=_att_in_specs(tq, qb, ti, gs) + [out, out],
        out_specs=[out, out, out], out_shape=[shape] * 3,
        scratch_shapes=[pltpu.VMEM((gs, tq + ATT_BLOCK, LANE), BF16)] * 2 + [pltpu.VMEM((gs, ATT_BLOCK, LANE), F32)] * 2,
        sem=("arbitrary", "arbitrary", "arbitrary"), args=[ua, ua, ua, ua, ua, da, stat])


class _Reduction:
    def __init__(self, place, names, grads):
        self.place, self.names, self.grads = place, names, grads

    def pair(self):
        return _pair_ride(self.grads)

    def chips(self, got):
        self.got = got
        return _chip_ride([_pair_sum(self.place, g, r, f"pair_sum_{n}") for g, r, n in zip(self.grads, got, self.names)])

    def halves(self, others):
        return [_chip_sum(self.place, g, r, o, f"chip_sum_{n}")
                for g, r, o, n in zip(self.grads, self.got, others, self.names)]


def _step(x, target, gains, w, place=None):
    t = x.shape[0]
    ex = place is not None
    g_ffn1, g_mix, g_ret, g_ffn2, g_fin = gains
    w = list(w)
    tabs = _retention_tables(t)
    red = lambda names, grads: _Reduction(place, names, grads) if ex else None
    ride = lambda r: [r] if ex else None

    if ex:
        w[0:3] = _run(_gather_ride(w[0:3]), "gather_ffn1_weights")
    (h1, xn1, *hid1, act1), rest = _ffn_fwd(x, g_ffn1, *w[0:3], "ffn1_fwd", ride(_gather_ride(w[3:])) if ex else None)
    if ex:
        w[3:] = rest[0]
    wg1, wu1, wd1, win, wo, wg2, wu2, wd2 = w
    wo2 = wo.reshape(wo.shape[0] * wo.shape[1], wo.shape[2])
    xnm, u, *uas = _inproj_fwd(h1, g_mix, win)
    raw, mix_r = _ret_fwd(u, g_ret, tabs)
    branches = [_att_fwd(ua, dil) for ua, dil in zip(uas, DILATIONS)]
    mix_a, att, lse = _att_combine([b[0] for b in branches], [b[1] for b in branches], t)
    h2 = _outproj_fwd(h1, mix_r, mix_a, wo2)
    (dh3, xn2, *hid2, act2, loss_p, dg_fin), _ = _ffn_fwd(h2, g_ffn2, wg2, wu2, wd2, "ffn2_fwd", head=(g_fin, target))

    (dwd2,), _ = _ffn_wgrad_down(act2, dh3, "ffn2_wgrad_down")
    dwd2 = dwd2.reshape(wd2.shape)
    r_d2 = red(["ffn2_w_down"], [dwd2])
    (dh2, dga2, dua2, dg_ffn2), e = _ffn_bwd_data(dh3, h2, g_ffn2, *hid2, wg2, wu2, wd2, "ffn2_bwd",
                                                  ex and [r_d2.pair()])
    (dwg2, dwu2), e = _ffn_wgrad_gu(xn2, [dga2, dua2], "ffn2_wgrad_gu", ex and [r_d2.chips(e[0])])
    dwg2, dwu2 = dwg2.reshape(wg2.shape), dwu2.reshape(wu2.shape)
    r_gu2 = red(["ffn2_w_gate", "ffn2_w_up"], [dwg2, dwu2])
    (dmix_r, dmix_a), e = _outproj_bwd(dh2, wo2, ex and [r_gu2.pair(), _finish_ride(r_d2.halves(e[0]))])
    if ex:
        got_gu2, (dwd2,) = e
    hw = RET_WIDTH // (wo.shape[1])
    dwo = jnp.concatenate([_tn_matmul(mix_r, dh2, dh2.shape[1], "wo_grad_r").reshape(hw, wo.shape[1], wo.shape[2]),
                           _tn_matmul(mix_a, dh2, dh2.shape[1], "wo_grad_a").reshape(hw, wo.shape[1], wo.shape[2])])
    r_wo = red(["w_out"], [dwo])
    (dq_r, dgt_r, dret, dg_ret), e = _ret_bwd_q(dmix_r, raw, u, g_ret, tabs, ex and [r_gu2.chips(got_gu2)])
    (dk_r, dv_r), e = _ret_bwd_kv(dret, u, tabs, ex and [r_wo.pair(), _finish_ride(r_gu2.halves(e[0]))])
    if ex:
        got_wo, (dwg2, dwu2) = e
    prep = _att_bwd_prep(dmix_a, att, lse)
    p1, e = _att_bwd(uas[0], *prep[0], DILATIONS[0], ex and [r_wo.chips(got_wo)])
    p4, e = _att_bwd(uas[1], *prep[1], DILATIONS[1], ex and [_finish_ride(r_wo.halves(e[0]))])
    if ex:
        (dwo,), = e
    p16, _ = _att_bwd(uas[2], *prep[2], DILATIONS[2])
    dh1, du, dg_mix = _inproj_bwd([dq_r, dk_r, dv_r, dgt_r], [p1, p4, p16], h1, g_mix, dh2, win)
    dwin = _tn_matmul(xnm, du, win.shape[2], "win_grad")
    r_in = red(["w_in"], [dwin])
    (dwd1,), e = _ffn_wgrad_down(act1, dh1, "ffn1_wgrad_down", ex and [r_in.pair()])
    dwd1 = dwd1.reshape(wd1.shape)
    r_d1 = red(["ffn1_w_down"], [dwd1])
    got_in = e
    (dx, dga1, dua1, dg_ffn1), _ = _ffn_bwd_data(dh1, x, g_ffn1, *hid1, wg1, wu1, wd1, "ffn1_bwd")
    (dwg1,), e = _ffn_wgrad_gu(xn1, [dga1], "ffn1_wgrad_gate", ex and [r_in.chips(got_in[0]), r_d1.pair()])
    dwg1 = dwg1.reshape(wg1.shape)
    if ex:
        oth_in, got_d1 = e
        r_g1 = red(["ffn1_w_gate"], [dwg1])
        got_g1 = _run(r_g1.pair(), "pair_exchange_ffn1_gate")
    (dwu1,), e = _ffn_wgrad_gu(xn1, [dua1], "ffn1_wgrad_up",
                               ex and [_finish_ride(r_in.halves(oth_in)), r_d1.chips(got_d1), r_g1.chips(got_g1)])
    dwu1 = dwu1.reshape(wu1.shape)
    gain_parts = [dg_ffn1, dg_mix, dg_ret, dg_ffn2, dg_fin]
    if not ex:
        return loss_p, dx, [dwg1, dwu1, dwd1, dwin, dwo, dwg2, dwu2, dwd2], gain_parts
    (dwin,), oth_d1, oth_g1 = e
    r_u1 = red(["ffn1_w_up"], [dwu1])
    got_u1 = _run(r_u1.pair(), "pair_exchange_ffn1_up")
    oth_u1 = _run(r_u1.chips(got_u1), "chip_exchange_ffn1_up")
    last = r_g1.halves(oth_g1) + r_u1.halves(oth_u1) + r_d1.halves(oth_d1)
    dwg1, dwu1, dwd1, gall = _run(_finish_ride(last, _pack_gains(gain_parts, x.shape[1])), "finish_exchange_ffn1")
    return loss_p, dx, [dwg1, dwu1, dwd1, dwin, dwo, dwg2, dwu2, dwd2], gall


N_DEV = 8
GAIN_ROWS = 8


def _place():
    x, y, c = lax.axis_index("x"), lax.axis_index("y"), lax.axis_index("c")
    chips = [(1 - x, y), (x, 1 - y), (1 - x, 1 - y)]
    return x, y, c, chips


ROW_QUARTERS = 4


def _place_shards(place, ws):
    n = len(ws)

    def body(place_ref, *refs):
        for w_ref, o_ref in zip(refs[:n], refs[n:]):
            o_ref[...] = w_ref[...].astype(BF16)

    quarter = lambda w: (w.shape[0] // ROW_QUARTERS, w.shape[1])
    return pl.pallas_call(
        body, name="place_shards",
        grid_spec=pltpu.PrefetchScalarGridSpec(
            num_scalar_prefetch=1, grid=(ROW_QUARTERS,),
            in_specs=[pl.BlockSpec(quarter(w), lambda i, pr: (i, 0)) for w in ws],
            out_specs=[pl.BlockSpec((None,) + quarter(w), lambda i, pr: (pr[0], i, 0)) for w in ws]),
        out_shape=[jax.ShapeDtypeStruct((N_SHARD,) + w.shape, BF16) for w in ws],
        compiler_params=_params("arbitrary"),
    )(place, *ws)


def _gather_ride(bufs):
    na = len(bufs)
    sent = [("me", "half", "x"), ("me", "half", "y"), ("x", "second quarter", "y"), ("y", "first quarter", "x")]
    landed = [("x", "half"), ("y", "half"), ("d", "second quarter"), ("d", "first quarter")]

    def legs(outs, sems):
        send_sem, recv_sem, fsend_sem, frecv_sem = sems
        x, y, c, _ = _place()
        slot = {"me": 2 * x + y, "x": 2 * (1 - x) + y, "y": 2 * x + (1 - y), "d": 2 * (1 - x) + (1 - y)}
        peer = {"x": (1 - x, y, c), "y": (x, 1 - y, c)}

        def rows(a, which, piece, core):
            hr = outs[a].shape[1] // 2
            lo, n = {"half": (0, hr), "first quarter": (0, hr // 2), "second quarter": (hr // 2, hr // 2)}[piece]
            return outs[a].at[slot[which], pl.ds(core * hr + lo, n)]

        def ici(a, k):
            which, piece, to = sent[k]
            ref = rows(a, which, piece, c)
            return pltpu.make_async_remote_copy(src_ref=ref, dst_ref=ref, send_sem=send_sem.at[a, k],
                                                recv_sem=recv_sem.at[a, k], device_id=peer[to], device_id_type=MESH)

        def arrival(a, k):
            ref = rows(a, *landed[k], c)
            return pltpu.make_async_remote_copy(src_ref=ref, dst_ref=ref, send_sem=send_sem.at[a, k],
                                                recv_sem=recv_sem.at[a, k], device_id=peer["x"], device_id_type=MESH)

        def d2d(a, k, core):
            ref = rows(a, *landed[k], core)
            return pltpu.make_async_remote_copy(src_ref=ref, dst_ref=ref, send_sem=fsend_sem.at[a, k],
                                                recv_sem=frecv_sem.at[a, k], device_id=(x, y, 1 - c), device_id_type=MESH)

        return c, ici, arrival, d2d

    def start(ins, outs, sems):
        _, ici, _, _ = legs(outs, sems)
        for a in range(na):
            ici(a, 0).start()
            ici(a, 1).start()

    def middle(ins, outs, sems):
        c, ici, arrival, d2d = legs(outs, sems)
        for a in range(na):
            for k in (0, 1):
                arrival(a, k).wait_recv()
                ici(a, 2 + k).start()
                d2d(a, k, c).start()

    def finish(ins, outs, sems):
        c, ici, arrival, d2d = legs(outs, sems)
        for a in range(na):
            for k in (2, 3):
                arrival(a, k).wait_recv()
                d2d(a, k, c).start()
        for a in range(na):
            for k in range(len(landed)):
                d2d(a, k, 1 - c).wait_recv()
        for a in range(na):
            for k in range(len(sent)):
                ici(a, k).wait_send()
                d2d(a, k, c).wait_send()

    return _Ride(bufs, [jax.ShapeDtypeStruct(b.shape, b.dtype) for b in bufs], [pltpu.SemaphoreType.DMA((na, 4))] * 4,
                 start, finish, {a: a for a in range(na)}, middle)


def _pair_ride(grads):
    na = len(grads)

    def copies(ins, outs, sems):
        send_sem, recv_sem = sems
        x, y, c, _ = _place()
        res = []
        for a in range(na):
            hr = ins[a].shape[1] // 2
            res.append(pltpu.make_async_remote_copy(
                src_ref=ins[a].at[:, pl.ds((1 - c) * hr, hr)], dst_ref=outs[a],
                send_sem=send_sem.at[a], recv_sem=recv_sem.at[a], device_id=(x, y, 1 - c), device_id_type=MESH))
        return res

    def start(ins, outs, sems):
        for cp in copies(ins, outs, sems):
            cp.start()

    def finish(ins, outs, sems):
        for cp in copies(ins, outs, sems):
            cp.wait()

    return _Ride(grads, [jax.ShapeDtypeStruct((g.shape[0], g.shape[1] // 2, g.shape[2]), g.dtype) for g in grads],
                 [pltpu.SemaphoreType.DMA((na,))] * 2, start, finish)


def _chip_ride(sums):
    na = len(sums)

    def copies(ins, outs, sems):
        send_sem, recv_sem = sems
        x, y, c, chips = _place()
        res = []
        for a in range(na):
            for j, (px, py) in enumerate(chips):
                res.append(pltpu.make_async_remote_copy(
                    src_ref=ins[a].at[2 * px + py], dst_ref=outs[a].at[j],
                    send_sem=send_sem.at[a, j], recv_sem=recv_sem.at[a, j], device_id=(px, py, c), device_id_type=MESH))
        return res

    def start(ins, outs, sems):
        for cp in copies(ins, outs, sems):
            cp.start()

    def finish(ins, outs, sems):
        for cp in copies(ins, outs, sems):
            cp.wait()

    return _Ride(sums, [jax.ShapeDtypeStruct((3,) + s.shape[1:], s.dtype) for s in sums],
                 [pltpu.SemaphoreType.DMA((na, 3))] * 2, start, finish)


def _finish_ride(grads, gpack=None):
    na = len(grads)

    def halves(outs, sems, which):
        x, y, c, _ = _place()
        res = []
        for a in range(na):
            hr = outs[a].shape[0] // 2
            rows = outs[a].at[pl.ds((c if which == "mine" else 1 - c) * hr, hr)]
            res.append(pltpu.make_async_remote_copy(
                src_ref=rows, dst_ref=rows, send_sem=sems[0].at[a], recv_sem=sems[1].at[a],
                device_id=(x, y, 1 - c), device_id_type=MESH))
        return res

    def gains(ins, outs, sems):
        x, y, c, _ = _place()
        dev = 4 * x + 2 * y + c
        g_in, g_out = ins[na], outs[na]
        own = pltpu.make_async_copy(g_in, g_out.at[dev], sems[2])
        sends, lands = [], []
        for k in range(N_DEV - 1):
            bx, by, bc = (k + 1) // 4, ((k + 1) // 2) % 2, (k + 1) % 2
            peer = (jnp.bitwise_xor(x, bx), jnp.bitwise_xor(y, by), jnp.bitwise_xor(c, bc))
            sends.append(pltpu.make_async_remote_copy(
                src_ref=g_in, dst_ref=g_out.at[dev], send_sem=sems[3].at[k], recv_sem=sems[4].at[k],
                device_id=peer, device_id_type=MESH))
            slot = g_out.at[jnp.bitwise_xor(dev, k + 1)]
            lands.append(pltpu.make_async_remote_copy(
                src_ref=slot, dst_ref=slot, send_sem=sems[3].at[k], recv_sem=sems[4].at[k],
                device_id=peer, device_id_type=MESH))
        return own, sends, lands

    def start(ins, outs, sems):
        for cp in halves(outs, sems, "mine"):
            cp.start()
        if gpack is not None:
            own, sends, _ = gains(ins, outs, sems)
            own.start()
            for cp in sends:
                cp.start()

    def finish(ins, outs, sems):
        for cp in halves(outs, sems, "sibling's"):
            cp.wait_recv()
        if gpack is not None:
            own, sends, lands = gains(ins, outs, sems)
            for cp in lands:
                cp.wait_recv()
            for cp in sends:
                cp.wait_send()
            own.wait()
        for cp in halves(outs, sems, "mine"):
            cp.wait_send()

    shapes = [jax.ShapeDtypeStruct(g.shape, g.dtype) for g in grads]
    sems = [pltpu.SemaphoreType.DMA((na,))] * 2
    if gpack is None:
        return _Ride(grads, shapes, sems, start, finish, {a: a for a in range(na)})
    return _Ride(list(grads) + [gpack], shapes + [jax.ShapeDtypeStruct((N_DEV,) + gpack.shape, gpack.dtype)],
                 sems + [pltpu.SemaphoreType.DMA, pltpu.SemaphoreType.DMA((N_DEV - 1,)), pltpu.SemaphoreType.DMA((N_DEV - 1,))],
                 start, finish, {a: a for a in range(na)})


def _pair_sum(place, grad, got, name):
    ns, r, cols = grad.shape
    hr = r // 2

    def body(place_ref, g_ref, r_ref, o_ref):
        o_ref[...] = (g_ref[...] + r_ref[...]).astype(BF16)

    return pl.pallas_call(
        body, name=name,
        grid_spec=pltpu.PrefetchScalarGridSpec(
            num_scalar_prefetch=1, grid=(ns,),
            in_specs=[pl.BlockSpec((None, hr, cols), lambda s, pr: (s, pr[1], 0)),
                      pl.BlockSpec((None, hr, cols), lambda s, pr: (s, 0, 0))],
            out_specs=pl.BlockSpec((None, hr, cols), lambda s, pr: (s, 0, 0))),
        out_shape=jax.ShapeDtypeStruct((ns, hr, cols), BF16),
        compiler_params=_params("arbitrary"),
    )(place, grad, got)


def _chip_sum(place, grad, got, others, name):
    ns, r, cols = grad.shape
    hr = r // 2
    nb = 2
    tr = hr // nb

    def body(place_ref, g_ref, r_ref, o3_ref, o_ref):
        acc = g_ref[...] + r_ref[...]
        for j in range(3):
            acc = acc + o3_ref[j].astype(F32)
        o_ref[...] = acc

    return pl.pallas_call(
        body, name=name,
        grid_spec=pltpu.PrefetchScalarGridSpec(
            num_scalar_prefetch=1, grid=(nb,),
            in_specs=[pl.BlockSpec((None, tr, cols), lambda i, pr: (pr[0], pr[1] * nb + i, 0)),
                      pl.BlockSpec((None, tr, cols), lambda i, pr: (pr[0], i, 0)),
                      pl.BlockSpec((3, tr, cols), lambda i, pr: (0, i, 0))],
            out_specs=pl.BlockSpec((tr, cols), lambda i, pr: (pr[1] * nb + i, 0))),
        out_shape=jax.ShapeDtypeStruct((r, cols), F32),
        compiler_params=_params("arbitrary"),
    )(place, grad, got, others)


def _pack_gains(parts, d):
    def body(*refs):
        ins, o_ref = refs[:-1], refs[-1]
        o_ref[...] = jnp.zeros_like(o_ref)
        for k, r in enumerate(ins):
            o_ref[k:k + 1, 0:r.shape[1]] = jnp.sum(r[...], axis=0, keepdims=True)

    return pl.pallas_call(
        body, name="pack_gains", out_shape=jax.ShapeDtypeStruct((GAIN_ROWS, d), F32),
    )(*parts)


def _adamw_math(w, g, m, v):
    m = ADAM_B1 * m + (1.0 - ADAM_B1) * g
    v = ADAM_B2 * v + (1.0 - ADAM_B2) * jnp.square(g)
    m_hat = m / (1.0 - ADAM_B1 ** ADAM_STEP)
    v_hat = v / (1.0 - ADAM_B2 ** ADAM_STEP)
    return -ADAM_LR * (m_hat / (jnp.sqrt(v_hat) + ADAM_EPS) + ADAM_WD * w), m, v


def _adamw(ws, gs, ms, vs):
    n = len(ws)

    def body(*refs):
        ins, outs = refs[:4 * n], refs[4 * n:]
        for k in range(n):
            w_ref, g_ref, m_ref, v_ref = ins[4 * k:4 * k + 4]
            go_ref, d_ref, nm_ref, nv_ref = outs[4 * k:4 * k + 4]
            g = g_ref[...]
            go_ref[...] = g
            d_ref[...], nm_ref[...], nv_ref[...] = _adamw_math(w_ref[...], g, m_ref[...], v_ref[...])

    parts = 2 * ROW_QUARTERS
    tile = lambda w: pl.BlockSpec((w.shape[0] // parts, w.shape[1]), lambda i: (i, 0))
    res = pl.pallas_call(
        body, name="adamw_shards", grid=(parts,),
        in_specs=[tile(w) for w in ws for _ in range(4)], out_specs=[tile(w) for w in ws for _ in range(4)],
        out_shape=[jax.ShapeDtypeStruct(w.shape, F32) for w in ws for _ in range(4)],
        compiler_params=_params("arbitrary"),
    )(*[a for quad in zip(ws, gs, ms, vs) for a in quad])
    return [res[4 * k:4 * k + 4] for k in range(n)]


def _adamw_gain(gall, row, w, m, v, name):
    n = w.shape[1]

    def body(ga_ref, w_ref, m_ref, v_ref, g_ref, d_ref, nm_ref, nv_ref):
        g = ga_ref[0, row:row + 1, 0:n]
        for k in range(1, N_DEV):
            g = g + ga_ref[k, row:row + 1, 0:n]
        g_ref[...] = g
        d_ref[...], nm_ref[...], nv_ref[...] = _adamw_math(w_ref[...], g, m_ref[...], v_ref[...])

    return pl.pallas_call(
        body, name=name, out_shape=[jax.ShapeDtypeStruct((1, n), F32)] * 4,
    )(gall, w, m, v)


def kernel(x, norm_ffn1, ffn1_w_gate, ffn1_w_up, ffn1_w_down, norm_mix, w_in, ret_norm_gain, w_out, norm_ffn2, ffn2_w_gate, ffn2_w_up, ffn2_w_down, norm_final, loss_target, m_norm_ffn1, m_ffn1_w_gate, m_ffn1_w_up, m_ffn1_w_down, m_norm_mix, m_w_in, m_ret_norm_gain, m_w_out, m_norm_ffn2, m_ffn2_w_gate, m_ffn2_w_up, m_ffn2_w_down, m_norm_final, v_norm_ffn1, v_ffn1_w_gate, v_ffn1_w_up, v_ffn1_w_down, v_norm_mix, v_w_in, v_ret_norm_gain, v_w_out, v_norm_ffn2, v_ffn2_w_gate, v_ffn2_w_up, v_ffn2_w_down, v_norm_final):
    d = x.shape[-1]
    mats = [ffn1_w_gate, ffn1_w_up, ffn1_w_down, w_in, w_out, ffn2_w_gate, ffn2_w_up, ffn2_w_down]
    mats_m = [m_ffn1_w_gate, m_ffn1_w_up, m_ffn1_w_down, m_w_in, m_w_out, m_ffn2_w_gate, m_ffn2_w_up, m_ffn2_w_down]
    mats_v = [v_ffn1_w_gate, v_ffn1_w_up, v_ffn1_w_down, v_w_in, v_w_out, v_ffn2_w_gate, v_ffn2_w_up, v_ffn2_w_down]
    mat_names = ["ffn1_w_gate", "ffn1_w_up", "ffn1_w_down", "w_in", "w_out", "ffn2_w_gate", "ffn2_w_up", "ffn2_w_down"]
    gains = [norm_ffn1, norm_mix, ret_norm_gain, norm_ffn2, norm_final.reshape(1, d)]
    gains_m = [m_norm_ffn1, m_norm_mix, m_ret_norm_gain, m_norm_ffn2, m_norm_final.reshape(1, d)]
    gains_v = [v_norm_ffn1, v_norm_mix, v_ret_norm_gain, v_norm_ffn2, v_norm_final.reshape(1, d)]
    gain_names = ["norm_ffn1", "norm_mix", "ret_norm_gain", "norm_ffn2", "norm_final"]

    turned = lambda n: n.endswith(("w_gate", "w_up"))
    local = lambda a, n: jnp.swapaxes(a, 1, 2)[0] if turned(n) else a[0]
    back = lambda a, n: jnp.swapaxes(a[None], 1, 2) if turned(n) else a[None]
    shards = [local(w, n) for w, n in zip(mats, mat_names)]
    place = jnp.stack([2 * lax.axis_index("x") + lax.axis_index("y"), lax.axis_index("c")]).astype(jnp.int32)
    placed = _place_shards(place, shards)
    loss_p, dx, shard_grads, gall = _step(x[0], loss_target[0], gains, placed, place)

    out_g, out_d, out_m, out_v = {}, {}, {}, {}
    updates = _adamw(shards, shard_grads, [local(m, n) for m, n in zip(mats_m, mat_names)],
                     [local(v, n) for v, n in zip(mats_v, mat_names)])
    for n, quad in zip(mat_names, updates):
        out_g[n], out_d[n], out_m[n], out_v[n] = [back(a, n) for a in quad]
    for row, (n, w, m, v) in enumerate(zip(gain_names, gains, gains_m, gains_v)):
        res = _adamw_gain(gall, row, w, m, v, f"adamw_{n}")
        shape = (d,) if n == "norm_final" else w.shape
        out_g[n], out_d[n], out_m[n], out_v[n] = [r.reshape(shape) for r in res]

    loss = lax.psum(jnp.sum(loss_p), ("x", "y", "c"))
    order = ["norm_ffn1", "ffn1_w_gate", "ffn1_w_up", "ffn1_w_down", "norm_mix", "w_in", "ret_norm_gain", "w_out",
             "norm_ffn2", "ffn2_w_gate", "ffn2_w_up", "ffn2_w_down", "norm_final"]
    return (loss, dx[None], *[out_g[n] for n in order], *[out_d[n] for n in order],
            *[out_m[n] for n in order], *[out_v[n] for n in order])
```

```python
import functools

import jax
import jax.numpy as jnp
from jax import lax
from jax.experimental import pallas as pl
from jax.experimental.pallas import tpu as pltpu

F32 = jnp.float32
BF16 = jnp.bfloat16
MESH = pl.DeviceIdType.MESH

NORM_EPS = 1e-6
GN_EPS = 1e-6
ROPE_BASE = 10000.0
RET_HEADS = 4
RET_DIM = 128
RET_WIDTH = 512
RET_CHUNK = 128
ATT_DIM = 64
ATT_WIDTH = 512
ATT_BLOCK = 128
DILATIONS = (1, 4, 16)
LANE = 128
N_SHARD = 4
ADAM_LR, ADAM_B1, ADAM_B2, ADAM_EPS, ADAM_WD, ADAM_STEP = 0.001, 0.9, 0.999, 1e-08, 0.01, 10

V7X_VMEM_BYTES = 64 * 1024 * 1024
VMEM_LIMIT = V7X_VMEM_BYTES - 8 * 1024 * 1024

NT = (((1,), (1,)), ((), ()))
TN = (((0,), (0,)), ((), ()))


def _params(*sem):
    return pltpu.CompilerParams(dimension_semantics=sem, vmem_limit_bytes=VMEM_LIMIT)


def _dot(a, b, dims=None):
    if dims is None:
        return jnp.dot(a, b, preferred_element_type=F32)
    return lax.dot_general(a, b, dims, preferred_element_type=F32)


def _sigmoid(x):
    return 1.0 / (1.0 + jnp.exp(-x))


def _load_weights(pairs, sems):
    copies = [pltpu.make_async_copy(src, dst, sems.at[k]) for k, (src, dst) in enumerate(pairs)]
    for cp in copies:
        cp.start()
    for cp in copies:
        cp.wait()


def _rows8(v):
    r, c = v.shape
    return v.reshape(r // 8, 8, c).sum(axis=0)


class _Ride:
    def __init__(self, inputs, out_shapes, sems, start, finish, aliases=None, middle=None):
        self.inputs, self.out_shapes, self.sems = list(inputs), list(out_shapes), list(sems)
        self.start, self.middle, self.finish, self.aliases = start, middle, finish, dict(aliases or {})


def _pallas(body, rides, *, name, in_specs, out_specs, out_shape, args, grid=(), scratch_shapes=(), sem=()):
    rides = [r for r in (rides or []) if r is not None]
    n_in, n_out, n_scr = len(args), len(out_shape), len(scratch_shapes)
    hbm = pl.BlockSpec(memory_space=pl.ANY)
    r_in = [a for r in rides for a in r.inputs]
    r_out = [s for r in rides for s in r.out_shapes]
    r_sem = [s for r in rides for s in r.sems]
    aliases, spans, ki, ko, ks = {}, [], 0, 0, 0
    for r in rides:
        aliases.update({n_in + ki + i: n_out + ko + o for i, o in r.aliases.items()})
        spans.append((ki, ko, ks))
        ki, ko, ks = ki + len(r.inputs), ko + len(r.out_shapes), ks + len(r.sems)

    def wrapped(*refs):
        ins, rin = refs[:n_in], refs[n_in:n_in + len(r_in)]
        o0 = n_in + len(r_in)
        outs, rout = refs[o0:o0 + n_out], refs[o0 + n_out:o0 + n_out + len(r_out)]
        s0 = o0 + n_out + len(r_out)
        scr, rsem = refs[s0:s0 + n_scr], refs[s0 + n_scr:]
        part = lambda r, k: (rin[spans[k][0]:spans[k][0] + len(r.inputs)], rout[spans[k][1]:spans[k][1] + len(r.out_shapes)],
                             rsem[spans[k][2]:spans[k][2] + len(r.sems)])
        first = functools.reduce(jnp.logical_and, [pl.program_id(k) == 0 for k in range(len(grid))], True)
        last = functools.reduce(jnp.logical_and, [pl.program_id(k) == grid[k] - 1 for k in range(len(grid))], True)
        if rides:
            @pl.when(first)
            def _():
                for k, r in enumerate(rides):
                    r.start(*part(r, k))

        if any(r.middle for r in rides):
            halfway = functools.reduce(jnp.logical_and, [pl.program_id(k) == 0 for k in range(1, len(grid))],
                                       pl.program_id(0) == grid[0] // 2)

            @pl.when(halfway)
            def _():
                for k, r in enumerate(rides):
                    if r.middle:
                        r.middle(*part(r, k))

        body(*ins, *outs, *scr)
        if rides:
            @pl.when(last)
            def _():
                for k, r in enumerate(rides):
                    r.finish(*part(r, k))

    res = pl.pallas_call(
        wrapped, name=name, grid=grid,
        in_specs=list(in_specs) + [hbm] * len(r_in), out_specs=list(out_specs) + [hbm] * len(r_out),
        out_shape=list(out_shape) + r_out, input_output_aliases=aliases,
        scratch_shapes=list(scratch_shapes) + r_sem,
        compiler_params=pltpu.CompilerParams(dimension_semantics=sem, vmem_limit_bytes=VMEM_LIMIT) if grid else None,
    )(*args, *r_in)
    extras = [list(res[n_out + ko:n_out + ko + len(r.out_shapes)]) for r, (_, ko, _) in zip(rides, spans)]
    return list(res[:n_out]), extras


def _run(ride, name):
    def body(*refs):
        n_in, n_out = len(ride.inputs), len(ride.out_shapes)
        parts = refs[:n_in], refs[n_in:n_in + n_out], refs[n_in + n_out:]
        ride.start(*parts)
        if ride.middle:
            ride.middle(*parts)
        ride.finish(*parts)

    hbm = pl.BlockSpec(memory_space=pl.ANY)
    return list(pl.pallas_call(
        body, name=name, in_specs=[hbm] * len(ride.inputs), out_specs=[hbm] * len(ride.out_shapes),
        out_shape=ride.out_shapes, input_output_aliases=ride.aliases, scratch_shapes=ride.sems,
    )(*ride.inputs))


def _loss_head(hv, gain_ref, tg_ref, loss_ref, dgain_ref):
    d = hv.shape[1]
    r = lax.rsqrt(jnp.mean(hv * hv, axis=-1, keepdims=True) + NORM_EPS)
    xh = hv * r
    err = xh * gain_ref[...] - tg_ref[...]
    sq = _rows8(jnp.square(err))
    loss_ref[...] += 0.5 * functools.reduce(jnp.add, [sq[:, k * LANE:(k + 1) * LANE] for k in range(d // LANE)]) / d
    dy = err / d
    dgain_ref[...] += _rows8(dy * xh)
    dxh = dy * gain_ref[...]
    return r * (dxh - xh * jnp.mean(dxh * xh, axis=-1, keepdims=True))


V7X_MXU_TILE = 256
FFN_FWD_CHUNK_TILES = 3
FFN_BWD_CHUNK_TILES = 4


def _hidden_chunks(f, tiles):
    step = tiles * V7X_MXU_TILE
    return [slice(s, min(s + step, f)) for s in range(0, f, step)]


def _flat(w):
    return w.reshape(w.shape[0] * w.shape[1], w.shape[2])


def _ffn_fwd(x, gain, wg, wu, wd, name, rides=None, head=None):
    t, d = x.shape
    wg, wu, wd = _flat(wg), _flat(wu), _flat(wd)
    f = wg.shape[0]
    tm = min(512, t)
    nh = 0 if head is None else 2

    def body(*refs):
        x_ref, gain_ref = refs[:2]
        wg_hbm, wu_hbm, wd_hbm, h_ref, xn_ref, g_ref, u_ref, a_ref = refs[2 + nh:10 + nh]
        sums = refs[10 + nh:12 + nh]
        wg_v, wu_v, wd_v, sems = refs[-4:]

        @pl.when(pl.program_id(0) == 0)
        def _():
            _load_weights([(wg_hbm, wg_v), (wu_hbm, wu_v), (wd_hbm, wd_v)], sems)
            if head is not None:
                for s_ref in sums:
                    s_ref[...] = jnp.zeros_like(s_ref)

        xv = x_ref[...]
        r = lax.rsqrt(jnp.mean(xv * xv, axis=-1, keepdims=True) + NORM_EPS)
        xn = (xv * r * gain_ref[...]).astype(BF16)
        xn_ref[...] = xn
        acc = jnp.zeros((tm, d), F32)
        for c in _hidden_chunks(f, FFN_FWD_CHUNK_TILES):
            g = _dot(xn, wg_v[c, :], NT)
            u = _dot(xn, wu_v[c, :], NT)
            g_ref[:, c] = g.astype(BF16)
            u_ref[:, c] = u.astype(BF16)
            a = (g * _sigmoid(g) * u).astype(BF16)
            a_ref[:, c] = a
            acc = acc + _dot(a, wd_v[c, :])
        hv = xv + 0.5 * acc
        h_ref[...] = hv if head is None else _loss_head(hv, refs[2], refs[3], *sums)

    hbm = pl.BlockSpec(memory_space=pl.ANY)
    hid = pl.BlockSpec((tm, f), lambda i: (i, 0))
    tile = pl.BlockSpec((tm, d), lambda i: (i, 0))
    row = pl.BlockSpec((1, d), lambda i: (0, 0))
    sums = [] if head is None else [(pl.BlockSpec((8, LANE), lambda i: (0, 0)), jax.ShapeDtypeStruct((8, LANE), F32)),
                                    (pl.BlockSpec((8, d), lambda i: (0, 0)), jax.ShapeDtypeStruct((8, d), F32))]
    return _pallas(
        body, rides, name=name, grid=(t // tm,),
        in_specs=[tile, row] + ([] if head is None else [row, tile]) + [hbm, hbm, hbm],
        out_specs=[tile, tile, hid, hid, hid] + [s for s, _ in sums],
        out_shape=[jax.ShapeDtypeStruct((t, d), F32), jax.ShapeDtypeStruct((t, d), BF16)]
        + [jax.ShapeDtypeStruct((t, f), BF16)] * 3 + [s for _, s in sums],
        scratch_shapes=[pltpu.VMEM(wg.shape, BF16), pltpu.VMEM(wu.shape, BF16), pltpu.VMEM(wd.shape, BF16),
                        pltpu.SemaphoreType.DMA((3,))],
        sem=("arbitrary",), args=[x, gain] + ([] if head is None else list(head)) + [wg, wu, wd])


def _ffn_bwd_data(dy, x, gain, g, u, wg, wu, wd, name, rides=None):
    t, d = x.shape
    wg, wu, wd = _flat(wg), _flat(wu), _flat(wd)
    f = wg.shape[0]
    tm = min(256, t)

    def body(dy_ref, x_ref, gain_ref, g_ref, u_ref, wg_hbm, wu_hbm, wd_hbm, dx_ref, dg_ref, du_ref, dgain_ref,
             wg_v, wu_v, wd_v, sems):
        @pl.when(pl.program_id(0) == 0)
        def _():
            _load_weights([(wg_hbm, wg_v), (wu_hbm, wu_v), (wd_hbm, wd_v)], sems)
            dgain_ref[...] = jnp.zeros_like(dgain_ref)

        dyv = dy_ref[...]
        dyh = (0.5 * dyv).astype(BF16)
        dxn = jnp.zeros((tm, d), F32)
        chunks = _hidden_chunks(f, FFN_BWD_CHUNK_TILES)
        das = [_dot(dyh, wd_v[c, :], NT) for c in chunks]
        for c, da in zip(chunks, das):
            gj = g_ref[:, c].astype(F32)
            uj = u_ref[:, c].astype(F32)
            sig = _sigmoid(gj)
            dgj = (da * uj * (sig * (1.0 + gj * (1.0 - sig)))).astype(BF16)
            duj = (da * (gj * sig)).astype(BF16)
            dg_ref[:, c] = dgj
            du_ref[:, c] = duj
            dxn = dxn + _dot(dgj, wg_v[c, :]) + _dot(duj, wu_v[c, :])
        xv = x_ref[...]
        r = lax.rsqrt(jnp.mean(xv * xv, axis=-1, keepdims=True) + NORM_EPS)
        xh = xv * r
        dgain_ref[...] += _rows8(dxn * xh)
        dxh = dxn * gain_ref[...]
        dx_ref[...] = dyv + r * (dxh - xh * jnp.mean(dxh * xh, axis=-1, keepdims=True))

    hbm = pl.BlockSpec(memory_space=pl.ANY)
    tile = pl.BlockSpec((tm, d), lambda i: (i, 0))
    hid = pl.BlockSpec((tm, f), lambda i: (i, 0))
    return _pallas(
        body, rides, name=name, grid=(t // tm,),
        in_specs=[tile, tile, pl.BlockSpec((1, d), lambda i: (0, 0)), hid, hid, hbm, hbm, hbm],
        out_specs=[tile, hid, hid, pl.BlockSpec((8, d), lambda i: (0, 0))],
        out_shape=[jax.ShapeDtypeStruct((t, d), F32), jax.ShapeDtypeStruct((t, f), BF16),
                   jax.ShapeDtypeStruct((t, f), BF16), jax.ShapeDtypeStruct((8, d), F32)],
        scratch_shapes=[pltpu.VMEM(wg.shape, BF16), pltpu.VMEM(wu.shape, BF16), pltpu.VMEM(wd.shape, BF16),
                        pltpu.SemaphoreType.DMA((3,))],
        sem=("arbitrary",), args=[dy, x, gain, g, u, wg, wu, wd])


WGRAD_ROW_BLOCKS = 2


def _ffn_wgrad_down(a, dy, name, rides=None):
    t, d = dy.shape
    f = a.shape[1]
    fb = f // WGRAD_ROW_BLOCKS
    tk = min(1024, t)

    def body(dy_ref, a_ref, dwd_ref):
        @pl.when(pl.program_id(1) == 0)
        def _():
            dwd_ref[...] = jnp.zeros_like(dwd_ref)

        dwd_ref[...] += _dot(a_ref[...], (0.5 * dy_ref[...]).astype(BF16), TN)

    return _pallas(
        body, rides, name=name, grid=(WGRAD_ROW_BLOCKS, t // tk),
        in_specs=[pl.BlockSpec((tk, d), lambda j, k: (k, 0)), pl.BlockSpec((tk, fb), lambda j, k: (k, j))],
        out_specs=[pl.BlockSpec((fb, d), lambda j, k: (j, 0))],
        out_shape=[jax.ShapeDtypeStruct((f, d), F32)],
        sem=("arbitrary", "arbitrary"), args=[dy, a])


def _ffn_wgrad_gu(xn, dhs, name, rides=None):
    t, d = xn.shape
    n = len(dhs)
    f = dhs[0].shape[1]
    fb = f // WGRAD_ROW_BLOCKS
    tk = min(2048 // n, t)

    def body(xn_ref, *refs):
        @pl.when(pl.program_id(1) == 0)
        def _():
            for o_ref in refs[n:]:
                o_ref[...] = jnp.zeros_like(o_ref)

        xnv = xn_ref[...]
        for dh_ref, o_ref in zip(refs[:n], refs[n:]):
            o_ref[...] += _dot(dh_ref[...], xnv, TN)

    hid = pl.BlockSpec((tk, fb), lambda j, k: (k, j))
    out = pl.BlockSpec((fb, d), lambda j, k: (j, 0))
    return _pallas(
        body, rides, name=name, grid=(WGRAD_ROW_BLOCKS, t // tk),
        in_specs=[pl.BlockSpec((tk, d), lambda j, k: (k, 0))] + [hid] * n,
        out_specs=[out] * n, out_shape=[jax.ShapeDtypeStruct((f, d), F32)] * n,
        sem=("arbitrary", "arbitrary"), args=[xn] + list(dhs))


def _tn_matmul(a, b, bn, name):
    t, m = a.shape
    n = b.shape[1]
    tk = min(2048, t)

    def body(a_ref, b_ref, o_ref):
        @pl.when(pl.program_id(1) == 0)
        def _():
            o_ref[...] = jnp.zeros_like(o_ref)

        o_ref[...] += _dot(a_ref[...].astype(BF16), b_ref[...].astype(BF16), TN)

    return pl.pallas_call(
        body, name=name, grid=(n // bn, t // tk),
        in_specs=[pl.BlockSpec((tk, m), lambda j, k: (k, 0)), pl.BlockSpec((tk, bn), lambda j, k: (k, j))],
        out_specs=pl.BlockSpec((None, m, bn), lambda j, k: (j, 0, 0)),
        out_shape=jax.ShapeDtypeStruct((n // bn, m, bn), F32),
        compiler_params=_params("arbitrary", "arbitrary"),
    )(a, b)


def _chunk_scratch(tm, w):
    return pltpu.VMEM((w // LANE, tm, LANE), F32)


def _regroup_store(cbuf, out_ref, dil, chunks=None):
    n = out_ref.shape[1]
    for k in range(cbuf.shape[0]) if chunks is None else chunks:
        for g in range(dil):
            rows = cbuf[k] if dil == 1 else cbuf[k, pl.ds(g, n, stride=dil), :]
            out_ref[g, :, k * LANE:(k + 1) * LANE] = rows.astype(out_ref.dtype)


def _natural_rows(ref, dil, cbuf):
    if dil == 1:
        return ref[0].astype(F32)
    n = ref.shape[1]
    for g in range(dil):
        for k in range(cbuf.shape[0]):
            cbuf[k, pl.ds(g, n, stride=dil), :] = ref[g, :, k * LANE:(k + 1) * LANE].astype(F32)
    return jnp.concatenate([cbuf[k] for k in range(cbuf.shape[0])], axis=1)


IN_CHUNK_TILES = 7


def _column_chunks(n):
    step = IN_CHUNK_TILES * V7X_MXU_TILE
    return [slice(s, min(s + step, n)) for s in range(0, n, step)]


def _load_side_by_side(w_hbm, w_v, sems):
    ns, _, cs = w_hbm.shape
    copies = [pltpu.make_async_copy(w_hbm.at[j], w_v.at[:, pl.ds(j * cs, cs)], sems.at[j]) for j in range(ns)]
    for cp in copies:
        cp.start()
    for cp in copies:
        cp.wait()


def _inproj_fwd(h, gain, win):
    t, d = h.shape
    ns, _, cs = win.shape
    tm = min(512, t)
    rw, aw = 4 * RET_WIDTH, 3 * ATT_WIDTH

    def body(h_ref, gain_ref, w_hbm, xn_ref, ur_ref, *rest):
        a_refs, abuf, w_v, sems = rest[:-3], rest[-3], rest[-2], rest[-1]

        @pl.when(pl.program_id(0) == 0)
        def _():
            _load_side_by_side(w_hbm, w_v, sems)

        hv = h_ref[...]
        r = lax.rsqrt(jnp.mean(hv * hv, axis=-1, keepdims=True) + NORM_EPS)
        xn = (hv * r * gain_ref[...]).astype(BF16)
        xn_ref[...] = xn
        for c in reversed(_column_chunks(ns * cs)):
            res = _dot(xn, w_v[:, c])
            mine = []
            for k in range((c.stop - c.start) // LANE):
                chunk = c.start // LANE + k
                piece = res[:, k * LANE:(k + 1) * LANE]
                if chunk < rw // LANE:
                    ur_ref[:, chunk * LANE:(chunk + 1) * LANE] = piece
                else:
                    abuf[chunk - rw // LANE] = piece
                    mine.append(chunk - rw // LANE)
            for dil, a_ref in zip(DILATIONS, a_refs):
                _regroup_store(abuf, a_ref, dil, mine)

    return pl.pallas_call(
        body, name="inproj_fwd", grid=(t // tm,),
        in_specs=[pl.BlockSpec((tm, d), lambda i: (i, 0)), pl.BlockSpec((1, d), lambda i: (0, 0)),
                  pl.BlockSpec(memory_space=pl.ANY)],
        out_specs=[pl.BlockSpec((tm, d), lambda i: (i, 0)), pl.BlockSpec((tm, rw), lambda i: (i, 0))]
        + [pl.BlockSpec((dil, tm // dil, aw), lambda i: (0, i, 0)) for dil in DILATIONS],
        out_shape=[jax.ShapeDtypeStruct((t, d), BF16), jax.ShapeDtypeStruct((t, rw), F32)]
        + [jax.ShapeDtypeStruct((dil, t // dil, aw), BF16) for dil in DILATIONS],
        scratch_shapes=[_chunk_scratch(tm, aw), pltpu.VMEM((d, ns * cs), BF16), pltpu.SemaphoreType.DMA((ns,))],
        compiler_params=_params("arbitrary"),
    )(h, gain, win)


def _inproj_bwd(pieces, parts, h, gain, dres, win):
    t, d = h.shape
    ns, _, cs = win.shape
    pw = pieces[0].shape[1]
    tm = min(512, t)
    npc, nk = len(pieces), len(parts[0])
    flat_parts = [a for p in parts for a in p]

    def body(*refs):
        p_refs, a_refs = refs[:npc], refs[npc:npc + len(flat_parts)]
        h_ref, gain_ref, dres_ref, w_hbm, dh_ref, du_ref, dgain_ref, buf, w_v, sems = refs[npc + len(flat_parts):]

        @pl.when(pl.program_id(0) == 0)
        def _():
            _load_side_by_side(w_hbm, w_v, sems)
            dgain_ref[...] = jnp.zeros_like(dgain_ref)

        for k in range(npc):
            du_ref[:, k * pw:(k + 1) * pw] = p_refs[k][...]
        for k in range(nk):
            acc = None
            for b, dil in enumerate(DILATIONS):
                rows = _natural_rows(a_refs[b * nk + k], dil, buf)
                acc = rows if acc is None else acc + rows
            du_ref[:, (npc + k) * pw:(npc + k + 1) * pw] = acc.astype(BF16)
        dxn = jnp.zeros((tm, d), F32)
        for c in _column_chunks(ns * cs):
            dxn = dxn + _dot(du_ref[:, c], w_v[:, c], NT)
        hv = h_ref[...]
        r = lax.rsqrt(jnp.mean(hv * hv, axis=-1, keepdims=True) + NORM_EPS)
        xh = hv * r
        dgain_ref[...] += _rows8(dxn * xh)
        dxh = dxn * gain_ref[...]
        dh_ref[...] = dres_ref[...] + r * (dxh - xh * jnp.mean(dxh * xh, axis=-1, keepdims=True))

    tile = pl.BlockSpec((tm, d), lambda i: (i, 0))
    cols = (npc + nk) * pw
    return pl.pallas_call(
        body, name="inproj_bwd", grid=(t // tm,),
        in_specs=[pl.BlockSpec((tm, pw), lambda i: (i, 0))] * npc
        + [_regrouped_spec(tm, dil, pw) for dil in DILATIONS for _ in range(nk)]
        + [tile, pl.BlockSpec((1, d), lambda i: (0, 0)), tile, pl.BlockSpec(memory_space=pl.ANY)],
        out_specs=[tile, pl.BlockSpec((tm, cols), lambda i: (i, 0)), pl.BlockSpec((8, d), lambda i: (0, 0))],
        out_shape=[jax.ShapeDtypeStruct((t, d), F32), jax.ShapeDtypeStruct((t, cols), BF16),
                   jax.ShapeDtypeStruct((8, d), F32)],
        scratch_shapes=[_chunk_scratch(tm, pw), pltpu.VMEM((d, ns * cs), BF16), pltpu.SemaphoreType.DMA((ns,))],
        compiler_params=_params("arbitrary"),
    )(*pieces, *flat_parts, h, gain, dres, win)


def _outproj_fwd(h, mix_r, mix_a, wo):
    t, d = h.shape
    hw = mix_r.shape[1]
    tm = min(512, t)

    def body(h_ref, mr_ref, ma_ref, w_ref, o_ref):
        o_ref[...] = h_ref[...] + _dot(mr_ref[...], w_ref[0:hw, :]) + _dot(ma_ref[...], w_ref[hw:2 * hw, :])

    tile = pl.BlockSpec((tm, d), lambda i: (i, 0))
    half = pl.BlockSpec((tm, hw), lambda i: (i, 0))
    return pl.pallas_call(
        body, name="outproj_fwd", grid=(t // tm,),
        in_specs=[tile, half, half, pl.BlockSpec(wo.shape, lambda i: (0, 0))],
        out_specs=tile, out_shape=jax.ShapeDtypeStruct((t, d), F32),
        compiler_params=_params("arbitrary"),
    )(h, mix_r, mix_a, wo)


def _wo_grad(mix_r, mix_a, dh):
    t, hw = mix_r.shape
    d = dh.shape[1]
    tk = min(2048, t)

    def body(mr_ref, ma_ref, dh_ref, o_ref):
        @pl.when(pl.program_id(0) == 0)
        def _():
            o_ref[...] = jnp.zeros_like(o_ref)

        dhb = dh_ref[...].astype(BF16)
        o_ref[0] += _dot(mr_ref[...], dhb, TN)
        o_ref[1] += _dot(ma_ref[...], dhb, TN)

    half = pl.BlockSpec((tk, hw), lambda k: (k, 0))
    return pl.pallas_call(
        body, name="wo_grad", grid=(t // tk,),
        in_specs=[half, half, pl.BlockSpec((tk, d), lambda k: (k, 0))],
        out_specs=pl.BlockSpec((2, hw, d), lambda k: (0, 0, 0)),
        out_shape=jax.ShapeDtypeStruct((2, hw, d), F32),
        compiler_params=_params("arbitrary"),
    )(mix_r, mix_a, dh)


def _outproj_bwd(dh, wo, rides=None):
    t, d = dh.shape
    hw = wo.shape[0] // 2
    tm = min(512, t)

    def body(dh_ref, w_ref, dr_ref, da_ref):
        dhb = dh_ref[...].astype(BF16)
        dr_ref[...] = _dot(dhb, w_ref[0:hw, :], NT)
        da_ref[...] = _dot(dhb, w_ref[hw:2 * hw, :], NT)

    half = pl.BlockSpec((tm, hw), lambda i: (i, 0))
    return _pallas(
        body, rides, name="outproj_bwd", grid=(t // tm,),
        in_specs=[pl.BlockSpec((tm, d), lambda i: (i, 0)), pl.BlockSpec(wo.shape, lambda i: (0, 0))],
        out_specs=[half, half],
        out_shape=[jax.ShapeDtypeStruct((t, hw), F32), jax.ShapeDtypeStruct((t, hw), F32)],
        sem=("arbitrary",), args=[dh, wo])


def _retention_tables(t):
    pos = jnp.arange(t, dtype=F32)
    pair = (jnp.arange(RET_DIM) // 2 * 2).astype(F32)
    ang = pos[:, None] * (ROPE_BASE ** (-pair / RET_DIM))[None, :]
    c = RET_CHUNK
    log_g = jnp.log(1.0 - 2.0 ** (-5.0 - jnp.arange(RET_HEADS, dtype=F32)))
    idx = jnp.arange(c, dtype=F32)
    rel = idx[:, None] - idx[None, :]
    decay = jnp.where(rel >= 0, jnp.exp(log_g[:, None, None] * jnp.maximum(rel, 0.0)), 0.0)
    zeta = jnp.exp(log_g[:, None] * (c - 1 - idx)[None, :])
    xi = jnp.exp(log_g[:, None] * (idx + 1)[None, :])
    gc = jnp.exp(log_g * c)
    wide = lambda v: jnp.broadcast_to(v[:, :, None], (RET_HEADS, c, LANE))
    return (jnp.cos(ang), jnp.sin(ang), decay, wide(zeta), wide(xi),
            jnp.broadcast_to(gc[:, None, None], (RET_HEADS, c, LANE)))


def _rot(v):
    lane = lax.broadcasted_iota(jnp.int32, v.shape, 1)
    nxt = pltpu.roll(v, LANE - 1, 1)
    prv = pltpu.roll(v, 1, 1)
    return jnp.where(lane % 2 == 0, -nxt, prv)


def _ret_specs(tr, rev, nt):
    ti = (lambda i: nt - 1 - i) if rev else (lambda i: i)
    col = lambda blk: pl.BlockSpec((tr, RET_WIDTH), lambda i: (ti(i), blk))
    tab = pl.BlockSpec((tr, LANE), lambda i: (ti(i), 0))
    head = pl.BlockSpec((RET_HEADS, RET_CHUNK, LANE), lambda i: (0, 0, 0))
    return col, tab, head


def _ret_chunks(tr, rev=False):
    order = list(range(tr // RET_CHUNK))
    return [(pl.ds(ci * RET_CHUNK, RET_CHUNK), slice(h * RET_DIM, (h + 1) * RET_DIM), h)
            for h in range(RET_HEADS) for ci in (reversed(order) if rev else order)]


def _ret_operands(items, q_ref, k_ref, v_ref, cos_ref, sin_ref, zeta_ref):
    scale = RET_DIM ** -0.5
    qbs, kbs, vbs, kzs = [], [], [], []
    for sl, hs, h in items:
        cs, sn = cos_ref[sl, :], sin_ref[sl, :]
        q, k = q_ref[sl, hs], k_ref[sl, hs]
        kr = (k * cs + _rot(k) * sn) * scale
        qbs.append((q * cs + _rot(q) * sn).astype(BF16))
        kbs.append(kr.astype(BF16))
        vbs.append(v_ref[sl, hs].astype(BF16))
        kzs.append((kr * zeta_ref[h]).astype(BF16))
    return qbs, kbs, vbs, kzs


def _ret_states(items, state, steps, gc_ref):
    cur, befores = {}, []
    for (sl, hs, h), step in zip(items, steps):
        st = cur[h] if h in cur else state[h]
        befores.append(st)
        cur[h] = st * gc_ref[h] + step
    for h, st in cur.items():
        state[h] = st
    return befores


def _ret_fwd(u, gain, tabs):
    t = u.shape[0]
    tr = min(512, t)
    nt = t // tr
    cos, sin, decay, zeta, xi, gc = tabs

    def body(q_ref, k_ref, v_ref, gt_ref, cos_ref, sin_ref, gain_ref, dec_ref, zeta_ref, xi_ref, gc_ref,
             raw_ref, mix_ref, state):
        @pl.when(pl.program_id(0) == 0)
        def _():
            state[...] = jnp.zeros_like(state)

        items = _ret_chunks(tr)
        n = range(len(items))
        qbs, kbs, vbs, kzs = _ret_operands(items, q_ref, k_ref, v_ref, cos_ref, sin_ref, zeta_ref)
        ss = [_dot(qbs[i], kbs[i], NT) for i in n]
        kvs = [_dot(kzs[i], vbs[i], TN) for i in n]
        befores = _ret_states(items, state, kvs, gc_ref)
        intra = [_dot((ss[i] * dec_ref[items[i][2]]).astype(BF16), vbs[i]) for i in n]
        inter = [_dot(qbs[i], befores[i].astype(BF16)) for i in n]
        for i, (sl, hs, h) in enumerate(items):
            o = intra[i] + inter[i] * xi_ref[h]
            raw_ref[sl, hs] = o
            mu = jnp.mean(o, axis=-1, keepdims=True)
            var = jnp.mean(jnp.square(o - mu), axis=-1, keepdims=True)
            y = (o - mu) * lax.rsqrt(var + GN_EPS) * gain_ref[:, hs]
            gt = gt_ref[sl, hs]
            mix_ref[sl, hs] = (y * (gt * _sigmoid(gt))).astype(BF16)

    col, tab, head = _ret_specs(tr, False, nt)
    out = pl.BlockSpec((tr, RET_WIDTH), lambda i: (i, 0))
    return pl.pallas_call(
        body, name="ret_fwd", grid=(nt,),
        in_specs=[col(0), col(1), col(2), col(3), tab, tab, pl.BlockSpec((1, RET_WIDTH), lambda i: (0, 0)),
                  head, head, head, head],
        out_specs=[out, out],
        out_shape=[jax.ShapeDtypeStruct((t, RET_WIDTH), F32), jax.ShapeDtypeStruct((t, RET_WIDTH), BF16)],
        scratch_shapes=[pltpu.VMEM((RET_HEADS, RET_DIM, RET_DIM), F32)],
        compiler_params=_params("arbitrary"),
    )(u, u, u, u, cos, sin, gain, decay, zeta, xi, gc)


def _ret_bwd_q(dmix, raw, u, gain, tabs, rides=None):
    t = u.shape[0]
    tr = min(512, t)
    nt = t // tr
    cos, sin, decay, zeta, xi, gc = tabs

    def body(dm_ref, raw_ref, q_ref, k_ref, v_ref, gt_ref, cos_ref, sin_ref, gain_ref, dec_ref, zeta_ref, xi_ref, gc_ref,
             dq_ref, dgt_ref, dret_ref, dgain_ref, state):
        @pl.when(pl.program_id(0) == 0)
        def _():
            state[...] = jnp.zeros_like(state)
            dgain_ref[...] = jnp.zeros_like(dgain_ref)

        items = _ret_chunks(tr)
        n_items = range(len(items))
        qbs, kbs, vbs, kzs = _ret_operands(items, q_ref, k_ref, v_ref, cos_ref, sin_ref, zeta_ref)
        dos, dgains = [], {}
        for sl, hs, h in items:
            o = raw_ref[sl, hs]
            mu = jnp.mean(o, axis=-1, keepdims=True)
            var = jnp.mean(jnp.square(o - mu), axis=-1, keepdims=True)
            rs = lax.rsqrt(var + GN_EPS)
            n = (o - mu) * rs
            gt = gt_ref[sl, hs]
            sig = _sigmoid(gt)
            dout = dm_ref[sl, hs]
            gain_h = gain_ref[:, hs]
            dgt_ref[sl, hs] = (dout * (n * gain_h) * (sig * (1.0 + gt * (1.0 - sig)))).astype(BF16)
            dy = dout * (gt * sig)
            dgains[h] = dgains[h] + _rows8(dy * n) if h in dgains else _rows8(dy * n)
            dn = dy * gain_h
            do = rs * (dn - jnp.mean(dn, axis=-1, keepdims=True) - n * jnp.mean(dn * n, axis=-1, keepdims=True))
            dret_ref[sl, hs] = do
            dos.append(do)
        for h, dg in dgains.items():
            dgain_ref[:, h * RET_DIM:(h + 1) * RET_DIM] += dg
        dss = [_dot(dos[i].astype(BF16), vbs[i], NT) for i in n_items]
        kvs = [_dot(kzs[i], vbs[i], TN) for i in n_items]
        befores = _ret_states(items, state, kvs, gc_ref)
        intra = [_dot((dss[i] * dec_ref[items[i][2]]).astype(BF16), kbs[i]) for i in n_items]
        inter = [_dot((dos[i] * xi_ref[items[i][2]]).astype(BF16), befores[i].astype(BF16), NT) for i in n_items]
        for i, (sl, hs, h) in enumerate(items):
            dqr = intra[i] + inter[i]
            dq_ref[sl, hs] = (dqr * cos_ref[sl, :] - _rot(dqr * sin_ref[sl, :])).astype(BF16)

    col, tab, head = _ret_specs(tr, False, nt)
    out = pl.BlockSpec((tr, RET_WIDTH), lambda i: (i, 0))
    return _pallas(
        body, rides, name="ret_bwd_q", grid=(nt,),
        in_specs=[out, out, col(0), col(1), col(2), col(3), tab, tab, pl.BlockSpec((1, RET_WIDTH), lambda i: (0, 0)),
                  head, head, head, head],
        out_specs=[out, out, out, pl.BlockSpec((8, RET_WIDTH), lambda i: (0, 0))],
        out_shape=[jax.ShapeDtypeStruct((t, RET_WIDTH), BF16), jax.ShapeDtypeStruct((t, RET_WIDTH), BF16),
                   jax.ShapeDtypeStruct((t, RET_WIDTH), F32), jax.ShapeDtypeStruct((8, RET_WIDTH), F32)],
        scratch_shapes=[pltpu.VMEM((RET_HEADS, RET_DIM, RET_DIM), F32)],
        sem=("arbitrary",), args=[dmix, raw, u, u, u, u, cos, sin, gain, decay, zeta, xi, gc])


def _ret_bwd_kv(dret, u, tabs, rides=None):
    t = u.shape[0]
    tr = min(512, t)
    nt = t // tr
    cos, sin, decay, zeta, xi, gc = tabs
    scale = RET_DIM ** -0.5

    def body(do_ref, q_ref, k_ref, v_ref, cos_ref, sin_ref, dec_ref, zeta_ref, xi_ref, gc_ref, dk_ref, dv_ref, gst):
        @pl.when(pl.program_id(0) == 0)
        def _():
            gst[...] = jnp.zeros_like(gst)

        items = _ret_chunks(tr, rev=True)
        n = range(len(items))
        qbs, kbs, vbs, kzs = _ret_operands(items, q_ref, k_ref, v_ref, cos_ref, sin_ref, zeta_ref)
        dos = [do_ref[sl, hs] for sl, hs, h in items]
        dobs = [do.astype(BF16) for do in dos]
        ss = [_dot(qbs[i], kbs[i], NT) for i in n]
        dss = [_dot(dobs[i], vbs[i], NT) for i in n]
        steps = [_dot(qbs[i], (dos[i] * xi_ref[items[i][2]]).astype(BF16), TN) for i in n]
        afters = [g.astype(BF16) for g in _ret_states(items, gst, steps, gc_ref)]
        dvs = [_dot((ss[i] * dec_ref[items[i][2]]).astype(BF16), dobs[i], TN) + _dot(kzs[i], afters[i]) for i in n]
        dks = [_dot((dss[i] * dec_ref[items[i][2]]).astype(BF16), qbs[i], TN) for i in n]
        dkz = [_dot(vbs[i], afters[i], NT) for i in n]
        for i, (sl, hs, h) in enumerate(items):
            dv_ref[sl, hs] = dvs[i].astype(BF16)
            dkr = (dks[i] + dkz[i] * zeta_ref[h]) * scale
            dk_ref[sl, hs] = (dkr * cos_ref[sl, :] - _rot(dkr * sin_ref[sl, :])).astype(BF16)

    col, tab, head = _ret_specs(tr, True, nt)
    out = pl.BlockSpec((tr, RET_WIDTH), lambda i: (nt - 1 - i, 0))
    return _pallas(
        body, rides, name="ret_bwd_kv", grid=(nt,),
        in_specs=[out, col(0), col(1), col(2), tab, tab, head, head, head, head],
        out_specs=[out, out],
        out_shape=[jax.ShapeDtypeStruct((t, RET_WIDTH), BF16), jax.ShapeDtypeStruct((t, RET_WIDTH), BF16)],
        scratch_shapes=[pltpu.VMEM((RET_HEADS, RET_DIM, RET_DIM), F32)],
        sem=("arbitrary",), args=[dret, u, u, u, cos, sin, decay, zeta, xi, gc])


PAIRS = ATT_WIDTH // LANE
ATT_Q_BLK, ATT_K_BLK, ATT_V_BLK = 0, PAIRS, 2 * PAIRS
STAT_LANES = ATT_DIM // 2


ATT_STEP_ROWS = 4096


def _att_tiles(t, dil):
    sub = t // dil
    tq = min(ATT_STEP_ROWS, sub)
    return sub, tq, sub // tq, tq // ATT_BLOCK, min(dil, ATT_STEP_ROWS // tq)


def _att_in_specs(tq, qb, ti, gs):
    cur = lambda off: pl.BlockSpec((gs, tq, LANE), lambda g, p, i: (g, ti(i), off + p))
    prev = lambda off: pl.BlockSpec((gs, ATT_BLOCK, LANE), lambda g, p, i: (g, jnp.maximum(ti(i) * qb - 1, 0), off + p))
    return [cur(ATT_Q_BLK), cur(ATT_K_BLK), prev(ATT_K_BLK), cur(ATT_V_BLK), prev(ATT_V_BLK)]


def _band_mask():
    key = lax.broadcasted_iota(jnp.int32, (2 * ATT_BLOCK, 2 * ATT_BLOCK), 0)
    qry = lax.broadcasted_iota(jnp.int32, (2 * ATT_BLOCK, 2 * ATT_BLOCK), 1) % ATT_BLOCK
    dist = qry + ATT_BLOCK - key
    return (dist >= 0) & (dist <= ATT_BLOCK), key >= ATT_BLOCK


def _head0_lanes():
    return lax.broadcasted_iota(jnp.int32, (ATT_BLOCK, LANE), 1) < ATT_DIM


def _stack_heads(v, head0):
    zero = jnp.zeros((), v.dtype)
    return jnp.concatenate([jnp.where(head0, v, zero), jnp.where(head0, zero, v)], axis=0)


def _unstack_heads(v, head0):
    return jnp.where(head0, v[0:ATT_BLOCK], v[ATT_BLOCK:])


def _att_fwd(ua, dil):
    sub = ua.shape[1]
    _, tq, nq, qb, gs = _att_tiles(sub * dil, dil)

    def body(q_ref, kc_ref, kp_ref, vc_ref, vp_ref, o_ref, l_ref, kx, vx):
        tile = pl.program_id(2)
        kx[:, 0:ATT_BLOCK, :] = kp_ref[...]
        kx[:, ATT_BLOCK:, :] = kc_ref[...]
        vx[:, 0:ATT_BLOCK, :] = vp_ref[...]
        vx[:, ATT_BLOCK:, :] = vc_ref[...]
        band, cur_keys = _band_mask()
        head0 = _head0_lanes()
        items = [(r, b) for r in range(gs) for b in range(qb)]
        rows = lambda b: slice(b * ATT_BLOCK, (b + 1) * ATT_BLOCK)
        keys = lambda b: slice(b * ATT_BLOCK, (b + 2) * ATT_BLOCK)
        sts = [_dot(kx[r, keys(b), :], _stack_heads(q_ref[r, rows(b), :] * jnp.asarray(ATT_DIM ** -0.5, BF16), head0), NT)
               for r, b in items]
        pts, lses = [], []
        for (r, b), st in zip(items, sts):
            mask = band if b > 0 else band & (cur_keys | (tile > 0))
            st = jnp.where(mask, st, -1e30)
            m = jnp.max(st, axis=0, keepdims=True)
            ex = jnp.exp(st - m)
            den = jnp.sum(ex, axis=0, keepdims=True)
            pts.append((ex * (1.0 / den)).astype(BF16))
            lses.append(m + jnp.log(den))
        outs = [_dot(pt, vx[r, keys(b), :], TN) for (r, b), pt in zip(items, pts)]
        for (r, b), out, lse in zip(items, outs, lses):
            o_ref[r, rows(b), :] = _unstack_heads(out, head0).astype(BF16)
            cols = [jnp.broadcast_to(lse[:, e * ATT_BLOCK:(e + 1) * ATT_BLOCK], (ATT_BLOCK, LANE)).T for e in range(2)]
            l_ref[r, rows(b), :] = jnp.where(head0, cols[0], cols[1])

    out = pl.BlockSpec((gs, tq, LANE), lambda g, p, i: (g, i, p))
    return pl.pallas_call(
        body, name=f"att_fwd_d{dil}", grid=(dil // gs, PAIRS, nq),
        in_specs=_att_in_specs(tq, qb, lambda i: i, gs),
        out_specs=[out, out],
        out_shape=[jax.ShapeDtypeStruct((dil, sub, ATT_WIDTH), BF16), jax.ShapeDtypeStruct((dil, sub, ATT_WIDTH), F32)],
        scratch_shapes=[pltpu.VMEM((gs, tq + ATT_BLOCK, LANE), BF16)] * 2,
        compiler_params=_params("arbitrary", "arbitrary", "arbitrary"),
    )(ua, ua, ua, ua, ua)


def _regrouped_spec(tm, dil, w):
    return pl.BlockSpec((dil, tm // dil, w), lambda i: (0, i, 0))


def _att_combine(outs, lses, t):
    w = ATT_WIDTH
    tm = min(512, t)
    nb = len(outs)

    def body(*refs):
        o_refs, l_refs = refs[:nb], refs[nb:2 * nb]
        mix_ref, att_ref, lse_ref, buf = refs[2 * nb:]
        ls = [_natural_rows(r, dil, buf) for r, dil in zip(l_refs, DILATIONS)]
        m = functools.reduce(jnp.maximum, ls)
        ws = [jnp.exp(l - m) for l in ls]
        den = functools.reduce(jnp.add, ws)
        att = functools.reduce(jnp.add, [(wt / den) * _natural_rows(r, dil, buf) for wt, r, dil in zip(ws, o_refs, DILATIONS)])
        att_ref[...] = att
        mix_ref[...] = att.astype(BF16)
        lse_ref[...] = m + jnp.log(den)

    tile = pl.BlockSpec((tm, w), lambda i: (i, 0))
    regrouped = [_regrouped_spec(tm, dil, w) for dil in DILATIONS]
    return pl.pallas_call(
        body, name="att_combine", grid=(t // tm,),
        in_specs=regrouped * 2, out_specs=[tile, tile, tile],
        out_shape=[jax.ShapeDtypeStruct((t, w), BF16), jax.ShapeDtypeStruct((t, w), F32), jax.ShapeDtypeStruct((t, w), F32)],
        scratch_shapes=[_chunk_scratch(tm, w)],
        compiler_params=_params("arbitrary"),
    )(*outs, *lses)


def _att_bwd_prep(datt, att, lse):
    t, w = datt.shape
    tm = min(512, t)

    def body(da_ref, at_ref, l_ref, *rest):
        outs, dbuf, sbuf = rest[:-2], rest[-2], rest[-1]
        dav = da_ref[...]
        prod = dav * at_ref[...]
        lane = lax.broadcasted_iota(jnp.int32, (tm, LANE), 1)
        for k in range(w // LANE):
            cols = slice(k * LANE, (k + 1) * LANE)
            dbuf[k] = dav[:, cols]
            delta = jnp.concatenate(
                [jnp.broadcast_to(jnp.sum(prod[:, k * LANE + e * ATT_DIM:k * LANE + (e + 1) * ATT_DIM], axis=-1, keepdims=True),
                                  (tm, ATT_DIM)) for e in range(LANE // ATT_DIM)], axis=1)
            sbuf[k] = jnp.where(lane % ATT_DIM < STAT_LANES, l_ref[:, cols], delta)
        for k, dil in enumerate(DILATIONS):
            _regroup_store(dbuf, outs[2 * k], dil)
            _regroup_store(sbuf, outs[2 * k + 1], dil)

    tile = pl.BlockSpec((tm, w), lambda i: (i, 0))
    res = pl.pallas_call(
        body, name="att_bwd_prep", grid=(t // tm,),
        in_specs=[tile] * 3,
        out_specs=[_regrouped_spec(tm, dil, w) for dil in DILATIONS for _ in range(2)],
        out_shape=[jax.ShapeDtypeStruct((dil, t // dil, w), dt) for dil in DILATIONS for dt in (BF16, F32)],
        scratch_shapes=[_chunk_scratch(tm, w)] * 2,
        compiler_params=_params("arbitrary"),
    )(datt, att, lse)
    return [(res[2 * k], res[2 * k + 1]) for k in range(len(DILATIONS))]


def _att_bwd(ua, da, stat, dil, rides=None):
    sub = ua.shape[1]
    _, tq, nq, qb, gs = _att_tiles(sub * dil, dil)
    scale = ATT_DIM ** -0.5

    def body(q_ref, kc_ref, kp_ref, vc_ref, vp_ref, da_ref, st_ref, dq_ref, dk_ref, dv_ref, kx, vx, ck, cv):
        step = pl.program_id(2)
        tile = nq - 1 - step

        @pl.when(step == 0)
        def _():
            ck[...] = jnp.zeros_like(ck)
            cv[...] = jnp.zeros_like(cv)

        kx[:, 0:ATT_BLOCK, :] = kp_ref[...]
        kx[:, ATT_BLOCK:, :] = kc_ref[...]
        vx[:, 0:ATT_BLOCK, :] = vp_ref[...]
        vx[:, ATT_BLOCK:, :] = vc_ref[...]
        band, cur_keys = _band_mask()
        head0 = _head0_lanes()
        items = [(r, b) for r in range(gs) for b in range(qb)]
        n = range(len(items))
        rows = lambda b: slice(b * ATT_BLOCK, (b + 1) * ATT_BLOCK)
        keys = lambda b: slice(b * ATT_BLOCK, (b + 2) * ATT_BLOCK)
        qqs = [_stack_heads(q_ref[r, rows(b), :] * jnp.asarray(scale, BF16), head0) for r, b in items]
        dds = [_stack_heads(da_ref[r, rows(b), :], head0) for r, b in items]
        sts = [_dot(kx[r, keys(b), :], qqs[i], NT) for i, (r, b) in enumerate(items)]
        dpts = [_dot(vx[r, keys(b), :], dds[i], NT) for i, (r, b) in enumerate(items)]
        pts, dsts = [], []
        for i, (r, b) in enumerate(items):
            mask = band if b > 0 else band & (cur_keys | (tile > 0))
            stat = st_ref[r, rows(b), :].T
            row = lambda k: jnp.concatenate([stat[e * ATT_DIM + k:e * ATT_DIM + k + 1, :] for e in range(2)], axis=1)
            pt = jnp.where(mask, jnp.exp(sts[i] - row(0)), 0.0)
            dsts.append((pt * (dpts[i] - row(STAT_LANES))).astype(BF16))
            pts.append(pt.astype(BF16))
        dqs = [_dot(dsts[i], kx[r, keys(b), :], TN) for i, (r, b) in enumerate(items)]
        dkbs = [_dot(dsts[i], qqs[i]) for i in n]
        dvbs = [_dot(pts[i], dds[i]) for i in n]
        for i, (r, b) in enumerate(items):
            dq_ref[r, rows(b), :] = (_unstack_heads(dqs[i], head0) * scale).astype(BF16)
            if b > 0:
                dk_ref[r, rows(b - 1), :] = (dkbs[i - 1][ATT_BLOCK:] + dkbs[i][0:ATT_BLOCK]).astype(BF16)
                dv_ref[r, rows(b - 1), :] = (dvbs[i - 1][ATT_BLOCK:] + dvbs[i][0:ATT_BLOCK]).astype(BF16)
        for r in range(gs):
            first, last = r * qb, r * qb + qb - 1
            dk_ref[r, rows(qb - 1), :] = (dkbs[last][ATT_BLOCK:] + ck[r]).astype(BF16)
            dv_ref[r, rows(qb - 1), :] = (dvbs[last][ATT_BLOCK:] + cv[r]).astype(BF16)
            ck[r] = dkbs[first][0:ATT_BLOCK]
            cv[r] = dvbs[first][0:ATT_BLOCK]

    ti = lambda i: nq - 1 - i
    out = pl.BlockSpec((gs, tq, LANE), lambda g, p, i: (g, ti(i), p))
    shape = jax.ShapeDtypeStruct((dil, sub, ATT_WIDTH), BF16)
    return _pallas(
        body, rides, name=f"att_bwd_d{dil}", grid=(dil // gs, PAIRS, nq),
        in_specs=_att_in_specs(tq, qb, ti, gs) + [out, out],
        out_specs=[out, out, out], out_shape=[shape] * 3,
        scratch_shapes=[pltpu.VMEM((gs, tq + ATT_BLOCK, LANE), BF16)] * 2 + [pltpu.VMEM((gs, ATT_BLOCK, LANE), F32)] * 2,
        sem=("arbitrary", "arbitrary", "arbitrary"), args=[ua, ua, ua, ua, ua, da, stat])


class _Reduction:
    def __init__(self, place, names, grads):
        self.place, self.names, self.grads = place, names, grads

    def pair(self):
        return _pair_ride(self.grads)

    def chips(self, got):
        self.got = got
        return _chip_ride([_pair_sum(self.place, g, r, f"pair_sum_{n}") for g, r, n in zip(self.grads, got, self.names)])

    def halves(self, others):
        return [_chip_sum(self.place, g, r, o, f"chip_sum_{n}")
                for g, r, o, n in zip(self.grads, self.got, others, self.names)]


def _step(x, target, gains, w, place=None):
    t = x.shape[0]
    ex = place is not None
    g_ffn1, g_mix, g_ret, g_ffn2, g_fin = gains
    w = list(w)
    tabs = _retention_tables(t)
    red = lambda names, grads: _Reduction(place, names, grads) if ex else None
    ride = lambda r: [r] if ex else None

    if ex:
        w[0:3] = _run(_gather_ride(w[0:3]), "gather_ffn1_weights")
    (h1, xn1, *hid1, act1), rest = _ffn_fwd(x, g_ffn1, *w[0:3], "ffn1_fwd", ride(_gather_ride(w[3:])) if ex else None)
    if ex:
        w[3:] = rest[0]
    wg1, wu1, wd1, win, wo, wg2, wu2, wd2 = w
    wo2 = wo.reshape(wo.shape[0] * wo.shape[1], wo.shape[2])
    xnm, u, *uas = _inproj_fwd(h1, g_mix, win)
    raw, mix_r = _ret_fwd(u, g_ret, tabs)
    branches = [_att_fwd(ua, dil) for ua, dil in zip(uas, DILATIONS)]
    mix_a, att, lse = _att_combine([b[0] for b in branches], [b[1] for b in branches], t)
    h2 = _outproj_fwd(h1, mix_r, mix_a, wo2)
    (dh3, xn2, *hid2, act2, loss_p, dg_fin), _ = _ffn_fwd(h2, g_ffn2, wg2, wu2, wd2, "ffn2_fwd", head=(g_fin, target))

    (dwd2,), _ = _ffn_wgrad_down(act2, dh3, "ffn2_wgrad_down")
    dwd2 = dwd2.reshape(wd2.shape)
    r_d2 = red(["ffn2_w_down"], [dwd2])
    (dh2, dga2, dua2, dg_ffn2), e = _ffn_bwd_data(dh3, h2, g_ffn2, *hid2, wg2, wu2, wd2, "ffn2_bwd",
                                                  ex and [r_d2.pair()])
    (dwg2, dwu2), e = _ffn_wgrad_gu(xn2, [dga2, dua2], "ffn2_wgrad_gu", ex and [r_d2.chips(e[0])])
    dwg2, dwu2 = dwg2.reshape(wg2.shape), dwu2.reshape(wu2.shape)
    r_gu2 = red(["ffn2_w_gate", "ffn2_w_up"], [dwg2, dwu2])
    (dmix_r, dmix_a), e = _outproj_bwd(dh2, wo2, ex and [r_gu2.pair(), _finish_ride(r_d2.halves(e[0]))])
    if ex:
        got_gu2, (dwd2,) = e
    dwo = _wo_grad(mix_r, mix_a, dh2).reshape(wo.shape)
    r_wo = red(["w_out"], [dwo])
    (dq_r, dgt_r, dret, dg_ret), e = _ret_bwd_q(dmix_r, raw, u, g_ret, tabs, ex and [r_gu2.chips(got_gu2)])
    (dk_r, dv_r), e = _ret_bwd_kv(dret, u, tabs, ex and [r_wo.pair(), _finish_ride(r_gu2.halves(e[0]))])
    if ex:
        got_wo, (dwg2, dwu2) = e
    prep = _att_bwd_prep(dmix_a, att, lse)
    p1, e = _att_bwd(uas[0], *prep[0], DILATIONS[0], ex and [r_wo.chips(got_wo)])
    p4, e = _att_bwd(uas[1], *prep[1], DILATIONS[1], ex and [_finish_ride(r_wo.halves(e[0]))])
    if ex:
        (dwo,), = e
    p16, _ = _att_bwd(uas[2], *prep[2], DILATIONS[2])
    dh1, du, dg_mix = _inproj_bwd([dq_r, dk_r, dv_r, dgt_r], [p1, p4, p16], h1, g_mix, dh2, win)
    dwin = _tn_matmul(xnm, du, win.shape[2], "win_grad")
    r_in = red(["w_in"], [dwin])
    (dwd1,), e = _ffn_wgrad_down(act1, dh1, "ffn1_wgrad_down", ex and [r_in.pair()])
    dwd1 = dwd1.reshape(wd1.shape)
    r_d1 = red(["ffn1_w_down"], [dwd1])
    got_in = e
    (dx, dga1, dua1, dg_ffn1), _ = _ffn_bwd_data(dh1, x, g_ffn1, *hid1, wg1, wu1, wd1, "ffn1_bwd")
    (dwg1,), e = _ffn_wgrad_gu(xn1, [dga1], "ffn1_wgrad_gate", ex and [r_in.chips(got_in[0]), r_d1.pair()])
    dwg1 = dwg1.reshape(wg1.shape)
    if ex:
        oth_in, got_d1 = e
        r_g1 = red(["ffn1_w_gate"], [dwg1])
        got_g1 = _run(r_g1.pair(), "pair_exchange_ffn1_gate")
    (dwu1,), e = _ffn_wgrad_gu(xn1, [dua1], "ffn1_wgrad_up",
                               ex and [_finish_ride(r_in.halves(oth_in)), r_d1.chips(got_d1), r_g1.chips(got_g1)])
    dwu1 = dwu1.reshape(wu1.shape)
    gain_parts = [dg_ffn1, dg_mix, dg_ret, dg_ffn2, dg_fin]
    if not ex:
        return loss_p, dx, [dwg1, dwu1, dwd1, dwin, dwo, dwg2, dwu2, dwd2], gain_parts
    (dwin,), oth_d1, oth_g1 = e
    r_u1 = red(["ffn1_w_up"], [dwu1])
    got_u1 = _run(r_u1.pair(), "pair_exchange_ffn1_up")
    oth_u1 = _run(r_u1.chips(got_u1), "chip_exchange_ffn1_up")
    last = r_g1.halves(oth_g1) + r_u1.halves(oth_u1) + r_d1.halves(oth_d1)
    dwg1, dwu1, dwd1, gall = _run(_finish_ride(last, _pack_gains(gain_parts, x.shape[1])), "finish_exchange_ffn1")
    return loss_p, dx, [dwg1, dwu1, dwd1, dwin, dwo, dwg2, dwu2, dwd2], gall


N_DEV = 8
GAIN_ROWS = 8


def _place():
    x, y, c = lax.axis_index("x"), lax.axis_index("y"), lax.axis_index("c")
    chips = [(1 - x, y), (x, 1 - y), (1 - x, 1 - y)]
    return x, y, c, chips


ROW_QUARTERS = 4


def _place_shards(place, ws):
    n = len(ws)

    def body(place_ref, *refs):
        for w_ref, o_ref in zip(refs[:n], refs[n:]):
            o_ref[...] = w_ref[...].astype(BF16)

    quarter = lambda w: (w.shape[0] // ROW_QUARTERS, w.shape[1])
    return pl.pallas_call(
        body, name="place_shards",
        grid_spec=pltpu.PrefetchScalarGridSpec(
            num_scalar_prefetch=1, grid=(ROW_QUARTERS,),
            in_specs=[pl.BlockSpec(quarter(w), lambda i, pr: (i, 0)) for w in ws],
            out_specs=[pl.BlockSpec((None,) + quarter(w), lambda i, pr: (pr[0], i, 0)) for w in ws]),
        out_shape=[jax.ShapeDtypeStruct((N_SHARD,) + w.shape, BF16) for w in ws],
        compiler_params=_params("arbitrary"),
    )(place, *ws)


def _gather_ride(bufs):
    na = len(bufs)
    sent = [("me", "half", "x"), ("me", "half", "y"), ("x", "second quarter", "y"), ("y", "first quarter", "x")]
    landed = [("x", "half"), ("y", "half"), ("d", "second quarter"), ("d", "first quarter")]

    def legs(outs, sems):
        send_sem, recv_sem, fsend_sem, frecv_sem = sems
        x, y, c, _ = _place()
        slot = {"me": 2 * x + y, "x": 2 * (1 - x) + y, "y": 2 * x + (1 - y), "d": 2 * (1 - x) + (1 - y)}
        peer = {"x": (1 - x, y, c), "y": (x, 1 - y, c)}

        def rows(a, which, piece, core):
            hr = outs[a].shape[1] // 2
            lo, n = {"half": (0, hr), "first quarter": (0, hr // 2), "second quarter": (hr // 2, hr // 2)}[piece]
            return outs[a].at[slot[which], pl.ds(core * hr + lo, n)]

        def ici(a, k):
            which, piece, to = sent[k]
            ref = rows(a, which, piece, c)
            return pltpu.make_async_remote_copy(src_ref=ref, dst_ref=ref, send_sem=send_sem.at[a, k],
                                                recv_sem=recv_sem.at[a, k], device_id=peer[to], device_id_type=MESH)

        def arrival(a, k):
            ref = rows(a, *landed[k], c)
            return pltpu.make_async_remote_copy(src_ref=ref, dst_ref=ref, send_sem=send_sem.at[a, k],
                                                recv_sem=recv_sem.at[a, k], device_id=peer["x"], device_id_type=MESH)

        def d2d(a, k, core):
            ref = rows(a, *landed[k], core)
            return pltpu.make_async_remote_copy(src_ref=ref, dst_ref=ref, send_sem=fsend_sem.at[a, k],
                                                recv_sem=frecv_sem.at[a, k], device_id=(x, y, 1 - c), device_id_type=MESH)

        return c, ici, arrival, d2d

    def start(ins, outs, sems):
        _, ici, _, _ = legs(outs, sems)
        for a in range(na):
            ici(a, 0).start()
            ici(a, 1).start()

    def middle(ins, outs, sems):
        c, ici, arrival, d2d = legs(outs, sems)
        for a in range(na):
            for k in (0, 1):
                arrival(a, k).wait_recv()
                ici(a, 2 + k).start()
                d2d(a, k, c).start()

    def finish(ins, outs, sems):
        c, ici, arrival, d2d = legs(outs, sems)
        for a in range(na):
            for k in (2, 3):
                arrival(a, k).wait_recv()
                d2d(a, k, c).start()
        for a in range(na):
            for k in range(len(landed)):
                d2d(a, k, 1 - c).wait_recv()
        for a in range(na):
            for k in range(len(sent)):
                ici(a, k).wait_send()
                d2d(a, k, c).wait_send()

    return _Ride(bufs, [jax.ShapeDtypeStruct(b.shape, b.dtype) for b in bufs], [pltpu.SemaphoreType.DMA((na, 4))] * 4,
                 start, finish, {a: a for a in range(na)}, middle)


def _pair_ride(grads):
    na = len(grads)

    def copies(ins, outs, sems):
        send_sem, recv_sem = sems
        x, y, c, _ = _place()
        res = []
        for a in range(na):
            hr = ins[a].shape[1] // 2
            res.append(pltpu.make_async_remote_copy(
                src_ref=ins[a].at[:, pl.ds((1 - c) * hr, hr)], dst_ref=outs[a],
                send_sem=send_sem.at[a], recv_sem=recv_sem.at[a], device_id=(x, y, 1 - c), device_id_type=MESH))
        return res

    def start(ins, outs, sems):
        for cp in copies(ins, outs, sems):
            cp.start()

    def finish(ins, outs, sems):
        for cp in copies(ins, outs, sems):
            cp.wait()

    return _Ride(grads, [jax.ShapeDtypeStruct((g.shape[0], g.shape[1] // 2, g.shape[2]), g.dtype) for g in grads],
                 [pltpu.SemaphoreType.DMA((na,))] * 2, start, finish)


def _chip_ride(sums):
    na = len(sums)

    def copies(ins, outs, sems):
        send_sem, recv_sem = sems
        x, y, c, chips = _place()
        res = []
        for a in range(na):
            for j, (px, py) in enumerate(chips):
                res.append(pltpu.make_async_remote_copy(
                    src_ref=ins[a].at[2 * px + py], dst_ref=outs[a].at[j],
                    send_sem=send_sem.at[a, j], recv_sem=recv_sem.at[a, j], device_id=(px, py, c), device_id_type=MESH))
        return res

    def start(ins, outs, sems):
        for cp in copies(ins, outs, sems):
            cp.start()

    def finish(ins, outs, sems):
        for cp in copies(ins, outs, sems):
            cp.wait()

    return _Ride(sums, [jax.ShapeDtypeStruct((3,) + s.shape[1:], s.dtype) for s in sums],
                 [pltpu.SemaphoreType.DMA((na, 3))] * 2, start, finish)


def _finish_ride(grads, gpack=None):
    na = len(grads)

    def halves(outs, sems, which):
        x, y, c, _ = _place()
        res = []
        for a in range(na):
            hr = outs[a].shape[0] // 2
            rows = outs[a].at[pl.ds((c if which == "mine" else 1 - c) * hr, hr)]
            res.append(pltpu.make_async_remote_copy(
                src_ref=rows, dst_ref=rows, send_sem=sems[0].at[a], recv_sem=sems[1].at[a],
                device_id=(x, y, 1 - c), device_id_type=MESH))
        return res

    def gains(ins, outs, sems):
        x, y, c, _ = _place()
        dev = 4 * x + 2 * y + c
        g_in, g_out = ins[na], outs[na]
        own = pltpu.make_async_copy(g_in, g_out.at[dev], sems[2])
        sends, lands = [], []
        for k in range(N_DEV - 1):
            bx, by, bc = (k + 1) // 4, ((k + 1) // 2) % 2, (k + 1) % 2
            peer = (jnp.bitwise_xor(x, bx), jnp.bitwise_xor(y, by), jnp.bitwise_xor(c, bc))
            sends.append(pltpu.make_async_remote_copy(
                src_ref=g_in, dst_ref=g_out.at[dev], send_sem=sems[3].at[k], recv_sem=sems[4].at[k],
                device_id=peer, device_id_type=MESH))
            slot = g_out.at[jnp.bitwise_xor(dev, k + 1)]
            lands.append(pltpu.make_async_remote_copy(
                src_ref=slot, dst_ref=slot, send_sem=sems[3].at[k], recv_sem=sems[4].at[k],
                device_id=peer, device_id_type=MESH))
        return own, sends, lands

    def start(ins, outs, sems):
        for cp in halves(outs, sems, "mine"):
            cp.start()
        if gpack is not None:
            own, sends, _ = gains(ins, outs, sems)
            own.start()
            for cp in sends:
                cp.start()

    def finish(ins, outs, sems):
        for cp in halves(outs, sems, "sibling's"):
            cp.wait_recv()
        if gpack is not None:
            own, sends, lands = gains(ins, outs, sems)
            for cp in lands:
                cp.wait_recv()
            for cp in sends:
                cp.wait_send()
            own.wait()
        for cp in halves(outs, sems, "mine"):
            cp.wait_send()

    shapes = [jax.ShapeDtypeStruct(g.shape, g.dtype) for g in grads]
    sems = [pltpu.SemaphoreType.DMA((na,))] * 2
    if gpack is None:
        return _Ride(grads, shapes, sems, start, finish, {a: a for a in range(na)})
    return _Ride(list(grads) + [gpack], shapes + [jax.ShapeDtypeStruct((N_DEV,) + gpack.shape, gpack.dtype)],
                 sems + [pltpu.SemaphoreType.DMA, pltpu.SemaphoreType.DMA((N_DEV - 1,)), pltpu.SemaphoreType.DMA((N_DEV - 1,))],
                 start, finish, {a: a for a in range(na)})


def _pair_sum(place, grad, got, name):
    ns, r, cols = grad.shape
    hr = r // 2

    def body(place_ref, g_ref, r_ref, o_ref):
        o_ref[...] = (g_ref[...] + r_ref[...]).astype(BF16)

    return pl.pallas_call(
        body, name=name,
        grid_spec=pltpu.PrefetchScalarGridSpec(
            num_scalar_prefetch=1, grid=(ns,),
            in_specs=[pl.BlockSpec((None, hr, cols), lambda s, pr: (s, pr[1], 0)),
                      pl.BlockSpec((None, hr, cols), lambda s, pr: (s, 0, 0))],
            out_specs=pl.BlockSpec((None, hr, cols), lambda s, pr: (s, 0, 0))),
        out_shape=jax.ShapeDtypeStruct((ns, hr, cols), BF16),
        compiler_params=_params("arbitrary"),
    )(place, grad, got)


def _chip_sum(place, grad, got, others, name):
    ns, r, cols = grad.shape
    hr = r // 2
    nb = 2
    tr = hr // nb

    def body(place_ref, g_ref, r_ref, o3_ref, o_ref):
        acc = g_ref[...] + r_ref[...]
        for j in range(3):
            acc = acc + o3_ref[j].astype(F32)
        o_ref[...] = acc

    return pl.pallas_call(
        body, name=name,
        grid_spec=pltpu.PrefetchScalarGridSpec(
            num_scalar_prefetch=1, grid=(nb,),
            in_specs=[pl.BlockSpec((None, tr, cols), lambda i, pr: (pr[0], pr[1] * nb + i, 0)),
                      pl.BlockSpec((None, tr, cols), lambda i, pr: (pr[0], i, 0)),
                      pl.BlockSpec((3, tr, cols), lambda i, pr: (0, i, 0))],
            out_specs=pl.BlockSpec((tr, cols), lambda i, pr: (pr[1] * nb + i, 0))),
        out_shape=jax.ShapeDtypeStruct((r, cols), F32),
        compiler_params=_params("arbitrary"),
    )(place, grad, got, others)


def _pack_gains(parts, d):
    def body(*refs):
        ins, o_ref = refs[:-1], refs[-1]
        o_ref[...] = jnp.zeros_like(o_ref)
        for k, r in enumerate(ins):
            o_ref[k:k + 1, 0:r.shape[1]] = jnp.sum(r[...], axis=0, keepdims=True)

    return pl.pallas_call(
        body, name="pack_gains", out_shape=jax.ShapeDtypeStruct((GAIN_ROWS, d), F32),
    )(*parts)


def _adamw_math(w, g, m, v):
    m = ADAM_B1 * m + (1.0 - ADAM_B1) * g
    v = ADAM_B2 * v + (1.0 - ADAM_B2) * jnp.square(g)
    m_hat = m / (1.0 - ADAM_B1 ** ADAM_STEP)
    v_hat = v / (1.0 - ADAM_B2 ** ADAM_STEP)
    return -ADAM_LR * (m_hat / (jnp.sqrt(v_hat) + ADAM_EPS) + ADAM_WD * w), m, v


def _adamw(ws, gs, ms, vs):
    n = len(ws)

    def body(*refs):
        ins, outs = refs[:4 * n], refs[4 * n:]
        for k in range(n):
            w_ref, g_ref, m_ref, v_ref = ins[4 * k:4 * k + 4]
            go_ref, d_ref, nm_ref, nv_ref = outs[4 * k:4 * k + 4]
            g = g_ref[...]
            go_ref[...] = g
            d_ref[...], nm_ref[...], nv_ref[...] = _adamw_math(w_ref[...], g, m_ref[...], v_ref[...])

    parts = 2 * ROW_QUARTERS
    tile = lambda w: pl.BlockSpec((w.shape[0] // parts, w.shape[1]), lambda i: (i, 0))
    res = pl.pallas_call(
        body, name="adamw_shards", grid=(parts,),
        in_specs=[tile(w) for w in ws for _ in range(4)], out_specs=[tile(w) for w in ws for _ in range(4)],
        out_shape=[jax.ShapeDtypeStruct(w.shape, F32) for w in ws for _ in range(4)],
        compiler_params=_params("arbitrary"),
    )(*[a for quad in zip(ws, gs, ms, vs) for a in quad])
    return [res[4 * k:4 * k + 4] for k in range(n)]


def _adamw_gain(gall, row, w, m, v, name):
    n = w.shape[1]

    def body(ga_ref, w_ref, m_ref, v_ref, g_ref, d_ref, nm_ref, nv_ref):
        g = ga_ref[0, row:row + 1, 0:n]
        for k in range(1, N_DEV):
            g = g + ga_ref[k, row:row + 1, 0:n]
        g_ref[...] = g
        d_ref[...], nm_ref[...], nv_ref[...] = _adamw_math(w_ref[...], g, m_ref[...], v_ref[...])

    return pl.pallas_call(
        body, name=name, out_shape=[jax.ShapeDtypeStruct((1, n), F32)] * 4,
    )(gall, w, m, v)


def kernel(x, norm_ffn1, ffn1_w_gate, ffn1_w_up, ffn1_w_down, norm_mix, w_in, ret_norm_gain, w_out, norm_ffn2, ffn2_w_gate, ffn2_w_up, ffn2_w_down, norm_final, loss_target, m_norm_ffn1, m_ffn1_w_gate, m_ffn1_w_up, m_ffn1_w_down, m_norm_mix, m_w_in, m_ret_norm_gain, m_w_out, m_norm_ffn2, m_ffn2_w_gate, m_ffn2_w_up, m_ffn2_w_down, m_norm_final, v_norm_ffn1, v_ffn1_w_gate, v_ffn1_w_up, v_ffn1_w_down, v_norm_mix, v_w_in, v_ret_norm_gain, v_w_out, v_norm_ffn2, v_ffn2_w_gate, v_ffn2_w_up, v_ffn2_w_down, v_norm_final):
    d = x.shape[-1]
    mats = [ffn1_w_gate, ffn1_w_up, ffn1_w_down, w_in, w_out, ffn2_w_gate, ffn2_w_up, ffn2_w_down]
    mats_m = [m_ffn1_w_gate, m_ffn1_w_up, m_ffn1_w_down, m_w_in, m_w_out, m_ffn2_w_gate, m_ffn2_w_up, m_ffn2_w_down]
    mats_v = [v_ffn1_w_gate, v_ffn1_w_up, v_ffn1_w_down, v_w_in, v_w_out, v_ffn2_w_gate, v_ffn2_w_up, v_ffn2_w_down]
    mat_names = ["ffn1_w_gate", "ffn1_w_up", "ffn1_w_down", "w_in", "w_out", "ffn2_w_gate", "ffn2_w_up", "ffn2_w_down"]
    gains = [norm_ffn1, norm_mix, ret_norm_gain, norm_ffn2, norm_final.reshape(1, d)]
    gains_m = [m_norm_ffn1, m_norm_mix, m_ret_norm_gain, m_norm_ffn2, m_norm_final.reshape(1, d)]
    gains_v = [v_norm_ffn1, v_norm_mix, v_ret_norm_gain, v_norm_ffn2, v_norm_final.reshape(1, d)]
    gain_names = ["norm_ffn1", "norm_mix", "ret_norm_gain", "norm_ffn2", "norm_final"]

    turned = lambda n: n.endswith(("w_gate", "w_up"))
    local = lambda a, n: jnp.swapaxes(a, 1, 2)[0] if turned(n) else a[0]
    back = lambda a, n: jnp.swapaxes(a[None], 1, 2) if turned(n) else a[None]
    shards = [local(w, n) for w, n in zip(mats, mat_names)]
    place = jnp.stack([2 * lax.axis_index("x") + lax.axis_index("y"), lax.axis_index("c")]).astype(jnp.int32)
    placed = _place_shards(place, shards)
    loss_p, dx, shard_grads, gall = _step(x[0], loss_target[0], gains, placed, place)

    out_g, out_d, out_m, out_v = {}, {}, {}, {}
    updates = _adamw(shards, shard_grads, [local(m, n) for m, n in zip(mats_m, mat_names)],
                     [local(v, n) for v, n in zip(mats_v, mat_names)])
    for n, quad in zip(mat_names, updates):
        out_g[n], out_d[n], out_m[n], out_v[n] = [back(a, n) for a in quad]
    for row, (n, w, m, v) in enumerate(zip(gain_names, gains, gains_m, gains_v)):
        res = _adamw_gain(gall, row, w, m, v, f"adamw_{n}")
        shape = (d,) if n == "norm_final" else w.shape
        out_g[n], out_d[n], out_m[n], out_v[n] = [r.reshape(shape) for r in res]

    loss = lax.psum(jnp.sum(loss_p), ("x", "y", "c"))
    order = ["norm_ffn1", "ffn1_w_gate", "ffn1_w_up", "ffn1_w_down", "norm_mix", "w_in", "ret_norm_gain", "w_out",
             "norm_ffn2", "ffn2_w_gate", "ffn2_w_up", "ffn2_w_down", "norm_final"]
    return (loss, dx[None], *[out_g[n] for n in order], *[out_d[n] for n in order],
            *[out_m[n] for n in order], *[out_v[n] for n in order])
```

```python
import functools

import jax
import jax.numpy as jnp
from jax import lax
from jax.experimental import pallas as pl
from jax.experimental.pallas import tpu as pltpu

F32 = jnp.float32
BF16 = jnp.bfloat16
MESH = pl.DeviceIdType.MESH

NORM_EPS = 1e-6
GN_EPS = 1e-6
ROPE_BASE = 10000.0
RET_HEADS = 4
RET_DIM = 128
RET_WIDTH = 512
RET_CHUNK = 128
ATT_DIM = 64
ATT_WIDTH = 512
ATT_BLOCK = 128
DILATIONS = (1, 4, 16)
LANE = 128
N_SHARD = 4
ADAM_LR, ADAM_B1, ADAM_B2, ADAM_EPS, ADAM_WD, ADAM_STEP = 0.001, 0.9, 0.999, 1e-08, 0.01, 10

V7X_VMEM_BYTES = 64 * 1024 * 1024
VMEM_LIMIT = V7X_VMEM_BYTES - 8 * 1024 * 1024

NT = (((1,), (1,)), ((), ()))
TN = (((0,), (0,)), ((), ()))


def _params(*sem):
    return pltpu.CompilerParams(dimension_semantics=sem, vmem_limit_bytes=VMEM_LIMIT)


def _dot(a, b, dims=None):
    if dims is None:
        return jnp.dot(a, b, preferred_element_type=F32)
    return lax.dot_general(a, b, dims, preferred_element_type=F32)


def _sigmoid(x):
    return 1.0 / (1.0 + jnp.exp(-x))


def _load_weights(pairs, sems):
    copies = [pltpu.make_async_copy(src, dst, sems.at[k]) for k, (src, dst) in enumerate(pairs)]
    for cp in copies:
        cp.start()
    for cp in copies:
        cp.wait()


def _rows8(v):
    r, c = v.shape
    return v.reshape(r // 8, 8, c).sum(axis=0)


class _Ride:
    def __init__(self, inputs, out_shapes, sems, start, finish, aliases=None, middle=None):
        self.inputs, self.out_shapes, self.sems = list(inputs), list(out_shapes), list(sems)
        self.start, self.middle, self.finish, self.aliases = start, middle, finish, dict(aliases or {})


def _pallas(body, rides, *, name, in_specs, out_specs, out_shape, args, grid=(), scratch_shapes=(), sem=()):
    rides = [r for r in (rides or []) if r is not None]
    n_in, n_out, n_scr = len(args), len(out_shape), len(scratch_shapes)
    hbm = pl.BlockSpec(memory_space=pl.ANY)
    r_in = [a for r in rides for a in r.inputs]
    r_out = [s for r in rides for s in r.out_shapes]
    r_sem = [s for r in rides for s in r.sems]
    aliases, spans, ki, ko, ks = {}, [], 0, 0, 0
    for r in rides:
        aliases.update({n_in + ki + i: n_out + ko + o for i, o in r.aliases.items()})
        spans.append((ki, ko, ks))
        ki, ko, ks = ki + len(r.inputs), ko + len(r.out_shapes), ks + len(r.sems)

    def wrapped(*refs):
        ins, rin = refs[:n_in], refs[n_in:n_in + len(r_in)]
        o0 = n_in + len(r_in)
        outs, rout = refs[o0:o0 + n_out], refs[o0 + n_out:o0 + n_out + len(r_out)]
        s0 = o0 + n_out + len(r_out)
        scr, rsem = refs[s0:s0 + n_scr], refs[s0 + n_scr:]
        part = lambda r, k: (rin[spans[k][0]:spans[k][0] + len(r.inputs)], rout[spans[k][1]:spans[k][1] + len(r.out_shapes)],
                             rsem[spans[k][2]:spans[k][2] + len(r.sems)])
        first = functools.reduce(jnp.logical_and, [pl.program_id(k) == 0 for k in range(len(grid))], True)
        last = functools.reduce(jnp.logical_and, [pl.program_id(k) == grid[k] - 1 for k in range(len(grid))], True)
        if rides:
            @pl.when(first)
            def _():
                for k, r in enumerate(rides):
                    r.start(*part(r, k))

        if any(r.middle for r in rides):
            halfway = functools.reduce(jnp.logical_and, [pl.program_id(k) == 0 for k in range(1, len(grid))],
                                       pl.program_id(0) == grid[0] // 2)

            @pl.when(halfway)
            def _():
                for k, r in enumerate(rides):
                    if r.middle:
                        r.middle(*part(r, k))

        body(*ins, *outs, *scr)
        if rides:
            @pl.when(last)
            def _():
                for k, r in enumerate(rides):
                    r.finish(*part(r, k))

    res = pl.pallas_call(
        wrapped, name=name, grid=grid,
        in_specs=list(in_specs) + [hbm] * len(r_in), out_specs=list(out_specs) + [hbm] * len(r_out),
        out_shape=list(out_shape) + r_out, input_output_aliases=aliases,
        scratch_shapes=list(scratch_shapes) + r_sem,
        compiler_params=pltpu.CompilerParams(dimension_semantics=sem, vmem_limit_bytes=VMEM_LIMIT) if grid else None,
    )(*args, *r_in)
    extras = [list(res[n_out + ko:n_out + ko + len(r.out_shapes)]) for r, (_, ko, _) in zip(rides, spans)]
    return list(res[:n_out]), extras


def _run(ride, name):
    def body(*refs):
        n_in, n_out = len(ride.inputs), len(ride.out_shapes)
        parts = refs[:n_in], refs[n_in:n_in + n_out], refs[n_in + n_out:]
        ride.start(*parts)
        if ride.middle:
            ride.middle(*parts)
        ride.finish(*parts)

    hbm = pl.BlockSpec(memory_space=pl.ANY)
    return list(pl.pallas_call(
        body, name=name, in_specs=[hbm] * len(ride.inputs), out_specs=[hbm] * len(ride.out_shapes),
        out_shape=ride.out_shapes, input_output_aliases=ride.aliases, scratch_shapes=ride.sems,
    )(*ride.inputs))


def _loss_head(hv, gain_ref, tg_ref, loss_ref, dgain_ref):
    d = hv.shape[1]
    r = lax.rsqrt(jnp.mean(hv * hv, axis=-1, keepdims=True) + NORM_EPS)
    xh = hv * r
    err = xh * gain_ref[...] - tg_ref[...]
    sq = _rows8(jnp.square(err))
    loss_ref[...] += 0.5 * functools.reduce(jnp.add, [sq[:, k * LANE:(k + 1) * LANE] for k in range(d // LANE)]) / d
    dy = err / d
    dgain_ref[...] += _rows8(dy * xh)
    dxh = dy * gain_ref[...]
    return r * (dxh - xh * jnp.mean(dxh * xh, axis=-1, keepdims=True))


V7X_MXU_TILE = 256
FFN_FWD_CHUNK_TILES = 3
FFN_BWD_CHUNK_TILES = 4


def _hidden_chunks(f, tiles):
    step = tiles * V7X_MXU_TILE
    return [slice(s, min(s + step, f)) for s in range(0, f, step)]


def _flat(w):
    return w.reshape(w.shape[0] * w.shape[1], w.shape[2])


def _ffn_fwd(x, gain, wg, wu, wd, name, rides=None, head=None):
    t, d = x.shape
    wg, wu, wd = _flat(wg), _flat(wu), _flat(wd)
    f = wg.shape[0]
    tm = min(512, t)
    nh = 0 if head is None else 2

    def body(*refs):
        x_ref, gain_ref = refs[:2]
        wg_hbm, wu_hbm, wd_hbm, h_ref, xn_ref, g_ref, u_ref, a_ref = refs[2 + nh:10 + nh]
        sums = refs[10 + nh:12 + nh]
        wg_v, wu_v, wd_v, sems = refs[-4:]

        @pl.when(pl.program_id(0) == 0)
        def _():
            _load_weights([(wg_hbm, wg_v), (wu_hbm, wu_v), (wd_hbm, wd_v)], sems)
            if head is not None:
                for s_ref in sums:
                    s_ref[...] = jnp.zeros_like(s_ref)

        xv = x_ref[...]
        r = lax.rsqrt(jnp.mean(xv * xv, axis=-1, keepdims=True) + NORM_EPS)
        xn = (xv * r * gain_ref[...]).astype(BF16)
        xn_ref[...] = xn
        acc = jnp.zeros((tm, d), F32)
        for c in _hidden_chunks(f, FFN_FWD_CHUNK_TILES):
            g = _dot(xn, wg_v[c, :], NT)
            u = _dot(xn, wu_v[c, :], NT)
            g_ref[:, c] = g.astype(BF16)
            u_ref[:, c] = u.astype(BF16)
            a = (g * _sigmoid(g) * u).astype(BF16)
            a_ref[:, c] = a
            acc = acc + _dot(a, wd_v[c, :])
        hv = xv + 0.5 * acc
        h_ref[...] = hv if head is None else _loss_head(hv, refs[2], refs[3], *sums)

    hbm = pl.BlockSpec(memory_space=pl.ANY)
    hid = pl.BlockSpec((tm, f), lambda i: (i, 0))
    tile = pl.BlockSpec((tm, d), lambda i: (i, 0))
    row = pl.BlockSpec((1, d), lambda i: (0, 0))
    sums = [] if head is None else [(pl.BlockSpec((8, LANE), lambda i: (0, 0)), jax.ShapeDtypeStruct((8, LANE), F32)),
                                    (pl.BlockSpec((8, d), lambda i: (0, 0)), jax.ShapeDtypeStruct((8, d), F32))]
    return _pallas(
        body, rides, name=name, grid=(t // tm,),
        in_specs=[tile, row] + ([] if head is None else [row, tile]) + [hbm, hbm, hbm],
        out_specs=[tile, tile, hid, hid, hid] + [s for s, _ in sums],
        out_shape=[jax.ShapeDtypeStruct((t, d), F32), jax.ShapeDtypeStruct((t, d), BF16)]
        + [jax.ShapeDtypeStruct((t, f), BF16)] * 3 + [s for _, s in sums],
        scratch_shapes=[pltpu.VMEM(wg.shape, BF16), pltpu.VMEM(wu.shape, BF16), pltpu.VMEM(wd.shape, BF16),
                        pltpu.SemaphoreType.DMA((3,))],
        sem=("arbitrary",), args=[x, gain] + ([] if head is None else list(head)) + [wg, wu, wd])


def _ffn_bwd_data(dy, x, gain, g, u, wg, wu, wd, name, rides=None):
    t, d = x.shape
    wg, wu, wd = _flat(wg), _flat(wu), _flat(wd)
    f = wg.shape[0]
    tm = min(256, t)

    def body(dy_ref, x_ref, gain_ref, g_ref, u_ref, wg_hbm, wu_hbm, wd_hbm, dx_ref, dg_ref, du_ref, dgain_ref,
             wg_v, wu_v, wd_v, sems):
        @pl.when(pl.program_id(0) == 0)
        def _():
            _load_weights([(wg_hbm, wg_v), (wu_hbm, wu_v), (wd_hbm, wd_v)], sems)
            dgain_ref[...] = jnp.zeros_like(dgain_ref)

        dyv = dy_ref[...]
        dyh = (0.5 * dyv).astype(BF16)
        dxn = jnp.zeros((tm, d), F32)
        chunks = _hidden_chunks(f, FFN_BWD_CHUNK_TILES)
        das = [_dot(dyh, wd_v[c, :], NT) for c in chunks]
        for c, da in zip(chunks, das):
            gj = g_ref[:, c].astype(F32)
            uj = u_ref[:, c].astype(F32)
            sig = _sigmoid(gj)
            dgj = (da * uj * (sig * (1.0 + gj * (1.0 - sig)))).astype(BF16)
            duj = (da * (gj * sig)).astype(BF16)
            dg_ref[:, c] = dgj
            du_ref[:, c] = duj
            dxn = dxn + _dot(dgj, wg_v[c, :]) + _dot(duj, wu_v[c, :])
        xv = x_ref[...]
        r = lax.rsqrt(jnp.mean(xv * xv, axis=-1, keepdims=True) + NORM_EPS)
        xh = xv * r
        dgain_ref[...] += _rows8(dxn * xh)
        dxh = dxn * gain_ref[...]
        dx_ref[...] = dyv + r * (dxh - xh * jnp.mean(dxh * xh, axis=-1, keepdims=True))

    hbm = pl.BlockSpec(memory_space=pl.ANY)
    tile = pl.BlockSpec((tm, d), lambda i: (i, 0))
    hid = pl.BlockSpec((tm, f), lambda i: (i, 0))
    return _pallas(
        body, rides, name=name, grid=(t // tm,),
        in_specs=[tile, tile, pl.BlockSpec((1, d), lambda i: (0, 0)), hid, hid, hbm, hbm, hbm],
        out_specs=[tile, hid, hid, pl.BlockSpec((8, d), lambda i: (0, 0))],
        out_shape=[jax.ShapeDtypeStruct((t, d), F32), jax.ShapeDtypeStruct((t, f), BF16),
                   jax.ShapeDtypeStruct((t, f), BF16), jax.ShapeDtypeStruct((8, d), F32)],
        scratch_shapes=[pltpu.VMEM(wg.shape, BF16), pltpu.VMEM(wu.shape, BF16), pltpu.VMEM(wd.shape, BF16),
                        pltpu.SemaphoreType.DMA((3,))],
        sem=("arbitrary",), args=[dy, x, gain, g, u, wg, wu, wd])


WGRAD_ROW_BLOCKS = 2


def _ffn_wgrad_down(a, dy, name, rides=None):
    t, d = dy.shape
    f = a.shape[1]
    fb = f // WGRAD_ROW_BLOCKS
    tk = min(2048, t)

    def body(dy_ref, a_ref, dwd_ref):
        @pl.when(pl.program_id(1) == 0)
        def _():
            dwd_ref[...] = jnp.zeros_like(dwd_ref)

        dwd_ref[...] += _dot(a_ref[...], (0.5 * dy_ref[...]).astype(BF16), TN)

    return _pallas(
        body, rides, name=name, grid=(WGRAD_ROW_BLOCKS, t // tk),
        in_specs=[pl.BlockSpec((tk, d), lambda j, k: (k, 0)), pl.BlockSpec((tk, fb), lambda j, k: (k, j))],
        out_specs=[pl.BlockSpec((fb, d), lambda j, k: (j, 0))],
        out_shape=[jax.ShapeDtypeStruct((f, d), F32)],
        sem=("arbitrary", "arbitrary"), args=[dy, a])


def _ffn_wgrad_gu(xn, dhs, name, rides=None):
    t, d = xn.shape
    n = len(dhs)
    f = dhs[0].shape[1]
    fb = f // WGRAD_ROW_BLOCKS
    tk = min(2048 // n, t)

    def body(xn_ref, *refs):
        @pl.when(pl.program_id(1) == 0)
        def _():
            for o_ref in refs[n:]:
                o_ref[...] = jnp.zeros_like(o_ref)

        xnv = xn_ref[...]
        for dh_ref, o_ref in zip(refs[:n], refs[n:]):
            o_ref[...] += _dot(dh_ref[...], xnv, TN)

    hid = pl.BlockSpec((tk, fb), lambda j, k: (k, j))
    out = pl.BlockSpec((fb, d), lambda j, k: (j, 0))
    return _pallas(
        body, rides, name=name, grid=(WGRAD_ROW_BLOCKS, t // tk),
        in_specs=[pl.BlockSpec((tk, d), lambda j, k: (k, 0))] + [hid] * n,
        out_specs=[out] * n, out_shape=[jax.ShapeDtypeStruct((f, d), F32)] * n,
        sem=("arbitrary", "arbitrary"), args=[xn] + list(dhs))


def _tn_matmul(a, b, bn, name):
    t, m = a.shape
    n = b.shape[1]
    tk = min(2048, t)

    def body(a_ref, b_ref, o_ref):
        @pl.when(pl.program_id(1) == 0)
        def _():
            o_ref[...] = jnp.zeros_like(o_ref)

        o_ref[...] += _dot(a_ref[...].astype(BF16), b_ref[...].astype(BF16), TN)

    return pl.pallas_call(
        body, name=name, grid=(n // bn, t // tk),
        in_specs=[pl.BlockSpec((tk, m), lambda j, k: (k, 0)), pl.BlockSpec((tk, bn), lambda j, k: (k, j))],
        out_specs=pl.BlockSpec((None, m, bn), lambda j, k: (j, 0, 0)),
        out_shape=jax.ShapeDtypeStruct((n // bn, m, bn), F32),
        compiler_params=_params("arbitrary", "arbitrary"),
    )(a, b)


def _chunk_scratch(tm, w):
    return pltpu.VMEM((w // LANE, tm, LANE), F32)


def _regroup_store(cbuf, out_ref, dil, chunks=None):
    n = out_ref.shape[1]
    for k in range(cbuf.shape[0]) if chunks is None else chunks:
        for g in range(dil):
            rows = cbuf[k] if dil == 1 else cbuf[k, pl.ds(g, n, stride=dil), :]
            out_ref[g, :, k * LANE:(k + 1) * LANE] = rows.astype(out_ref.dtype)


def _natural_rows(ref, dil, cbuf):
    if dil == 1:
        return ref[0].astype(F32)
    n = ref.shape[1]
    for g in range(dil):
        for k in range(cbuf.shape[0]):
            cbuf[k, pl.ds(g, n, stride=dil), :] = ref[g, :, k * LANE:(k + 1) * LANE].astype(F32)
    return jnp.concatenate([cbuf[k] for k in range(cbuf.shape[0])], axis=1)


IN_CHUNK_TILES = 7


def _column_chunks(n):
    step = IN_CHUNK_TILES * V7X_MXU_TILE
    return [slice(s, min(s + step, n)) for s in range(0, n, step)]


def _load_side_by_side(w_hbm, w_v, sems):
    ns, _, cs = w_hbm.shape
    copies = [pltpu.make_async_copy(w_hbm.at[j], w_v.at[:, pl.ds(j * cs, cs)], sems.at[j]) for j in range(ns)]
    for cp in copies:
        cp.start()
    for cp in copies:
        cp.wait()


def _inproj_fwd(h, gain, win):
    t, d = h.shape
    ns, _, cs = win.shape
    tm = min(512, t)
    rw, aw = 4 * RET_WIDTH, 3 * ATT_WIDTH

    def body(h_ref, gain_ref, w_hbm, xn_ref, ur_ref, *rest):
        a_refs, abuf, w_v, sems = rest[:-3], rest[-3], rest[-2], rest[-1]

        @pl.when(pl.program_id(0) == 0)
        def _():
            _load_side_by_side(w_hbm, w_v, sems)

        hv = h_ref[...]
        r = lax.rsqrt(jnp.mean(hv * hv, axis=-1, keepdims=True) + NORM_EPS)
        xn = (hv * r * gain_ref[...]).astype(BF16)
        xn_ref[...] = xn
        for c in reversed(_column_chunks(ns * cs)):
            res = _dot(xn, w_v[:, c])
            mine = []
            for k in range((c.stop - c.start) // LANE):
                chunk = c.start // LANE + k
                piece = res[:, k * LANE:(k + 1) * LANE]
                if chunk < rw // LANE:
                    ur_ref[:, chunk * LANE:(chunk + 1) * LANE] = piece
                else:
                    abuf[chunk - rw // LANE] = piece
                    mine.append(chunk - rw // LANE)
            for dil, a_ref in zip(DILATIONS, a_refs):
                _regroup_store(abuf, a_ref, dil, mine)

    return pl.pallas_call(
        body, name="inproj_fwd", grid=(t // tm,),
        in_specs=[pl.BlockSpec((tm, d), lambda i: (i, 0)), pl.BlockSpec((1, d), lambda i: (0, 0)),
                  pl.BlockSpec(memory_space=pl.ANY)],
        out_specs=[pl.BlockSpec((tm, d), lambda i: (i, 0)), pl.BlockSpec((tm, rw), lambda i: (i, 0))]
        + [pl.BlockSpec((dil, tm // dil, aw), lambda i: (0, i, 0)) for dil in DILATIONS],
        out_shape=[jax.ShapeDtypeStruct((t, d), BF16), jax.ShapeDtypeStruct((t, rw), F32)]
        + [jax.ShapeDtypeStruct((dil, t // dil, aw), BF16) for dil in DILATIONS],
        scratch_shapes=[_chunk_scratch(tm, aw), pltpu.VMEM((d, ns * cs), BF16), pltpu.SemaphoreType.DMA((ns,))],
        compiler_params=_params("arbitrary"),
    )(h, gain, win)


def _inproj_bwd(pieces, parts, h, gain, dres, win):
    t, d = h.shape
    ns, _, cs = win.shape
    pw = pieces[0].shape[1]
    tm = min(512, t)
    npc, nk = len(pieces), len(parts[0])
    flat_parts = [a for p in parts for a in p]

    def body(*refs):
        p_refs, a_refs = refs[:npc], refs[npc:npc + len(flat_parts)]
        h_ref, gain_ref, dres_ref, w_hbm, dh_ref, du_ref, dgain_ref, buf, w_v, sems = refs[npc + len(flat_parts):]

        @pl.when(pl.program_id(0) == 0)
        def _():
            _load_side_by_side(w_hbm, w_v, sems)
            dgain_ref[...] = jnp.zeros_like(dgain_ref)

        for k in range(npc):
            du_ref[:, k * pw:(k + 1) * pw] = p_refs[k][...]
        for k in range(nk):
            acc = None
            for b, dil in enumerate(DILATIONS):
                rows = _natural_rows(a_refs[b * nk + k], dil, buf)
                acc = rows if acc is None else acc + rows
            du_ref[:, (npc + k) * pw:(npc + k + 1) * pw] = acc.astype(BF16)
        dxn = jnp.zeros((tm, d), F32)
        for c in _column_chunks(ns * cs):
            dxn = dxn + _dot(du_ref[:, c], w_v[:, c], NT)
        hv = h_ref[...]
        r = lax.rsqrt(jnp.mean(hv * hv, axis=-1, keepdims=True) + NORM_EPS)
        xh = hv * r
        dgain_ref[...] += _rows8(dxn * xh)
        dxh = dxn * gain_ref[...]
        dh_ref[...] = dres_ref[...] + r * (dxh - xh * jnp.mean(dxh * xh, axis=-1, keepdims=True))

    tile = pl.BlockSpec((tm, d), lambda i: (i, 0))
    cols = (npc + nk) * pw
    return pl.pallas_call(
        body, name="inproj_bwd", grid=(t // tm,),
        in_specs=[pl.BlockSpec((tm, pw), lambda i: (i, 0))] * npc
        + [_regrouped_spec(tm, dil, pw) for dil in DILATIONS for _ in range(nk)]
        + [tile, pl.BlockSpec((1, d), lambda i: (0, 0)), tile, pl.BlockSpec(memory_space=pl.ANY)],
        out_specs=[tile, pl.BlockSpec((tm, cols), lambda i: (i, 0)), pl.BlockSpec((8, d), lambda i: (0, 0))],
        out_shape=[jax.ShapeDtypeStruct((t, d), F32), jax.ShapeDtypeStruct((t, cols), BF16),
                   jax.ShapeDtypeStruct((8, d), F32)],
        scratch_shapes=[_chunk_scratch(tm, pw), pltpu.VMEM((d, ns * cs), BF16), pltpu.SemaphoreType.DMA((ns,))],
        compiler_params=_params("arbitrary"),
    )(*pieces, *flat_parts, h, gain, dres, win)


def _outproj_fwd(h, mix_r, mix_a, wo):
    t, d = h.shape
    hw = mix_r.shape[1]
    tm = min(512, t)

    def body(h_ref, mr_ref, ma_ref, w_ref, o_ref):
        o_ref[...] = h_ref[...] + _dot(mr_ref[...], w_ref[0:hw, :]) + _dot(ma_ref[...], w_ref[hw:2 * hw, :])

    tile = pl.BlockSpec((tm, d), lambda i: (i, 0))
    half = pl.BlockSpec((tm, hw), lambda i: (i, 0))
    return pl.pallas_call(
        body, name="outproj_fwd", grid=(t // tm,),
        in_specs=[tile, half, half, pl.BlockSpec(wo.shape, lambda i: (0, 0))],
        out_specs=tile, out_shape=jax.ShapeDtypeStruct((t, d), F32),
        compiler_params=_params("arbitrary"),
    )(h, mix_r, mix_a, wo)


def _wo_grad(mix_r, mix_a, dh):
    t, hw = mix_r.shape
    d = dh.shape[1]
    tk = min(2048, t)

    def body(mr_ref, ma_ref, dh_ref, o_ref):
        @pl.when(pl.program_id(0) == 0)
        def _():
            o_ref[...] = jnp.zeros_like(o_ref)

        dhb = dh_ref[...].astype(BF16)
        o_ref[0] += _dot(mr_ref[...], dhb, TN)
        o_ref[1] += _dot(ma_ref[...], dhb, TN)

    half = pl.BlockSpec((tk, hw), lambda k: (k, 0))
    return pl.pallas_call(
        body, name="wo_grad", grid=(t // tk,),
        in_specs=[half, half, pl.BlockSpec((tk, d), lambda k: (k, 0))],
        out_specs=pl.BlockSpec((2, hw, d), lambda k: (0, 0, 0)),
        out_shape=jax.ShapeDtypeStruct((2, hw, d), F32),
        compiler_params=_params("arbitrary"),
    )(mix_r, mix_a, dh)


def _outproj_bwd(dh, wo, rides=None):
    t, d = dh.shape
    hw = wo.shape[0] // 2
    tm = min(512, t)

    def body(dh_ref, w_ref, dr_ref, da_ref):
        dhb = dh_ref[...].astype(BF16)
        dr_ref[...] = _dot(dhb, w_ref[0:hw, :], NT)
        da_ref[...] = _dot(dhb, w_ref[hw:2 * hw, :], NT)

    half = pl.BlockSpec((tm, hw), lambda i: (i, 0))
    return _pallas(
        body, rides, name="outproj_bwd", grid=(t // tm,),
        in_specs=[pl.BlockSpec((tm, d), lambda i: (i, 0)), pl.BlockSpec(wo.shape, lambda i: (0, 0))],
        out_specs=[half, half],
        out_shape=[jax.ShapeDtypeStruct((t, hw), F32), jax.ShapeDtypeStruct((t, hw), F32)],
        sem=("arbitrary",), args=[dh, wo])


def _retention_tables(t):
    pos = jnp.arange(t, dtype=F32)
    pair = (jnp.arange(RET_DIM) // 2 * 2).astype(F32)
    ang = pos[:, None] * (ROPE_BASE ** (-pair / RET_DIM))[None, :]
    c = RET_CHUNK
    log_g = jnp.log(1.0 - 2.0 ** (-5.0 - jnp.arange(RET_HEADS, dtype=F32)))
    idx = jnp.arange(c, dtype=F32)
    rel = idx[:, None] - idx[None, :]
    decay = jnp.where(rel >= 0, jnp.exp(log_g[:, None, None] * jnp.maximum(rel, 0.0)), 0.0)
    zeta = jnp.exp(log_g[:, None] * (c - 1 - idx)[None, :])
    xi = jnp.exp(log_g[:, None] * (idx + 1)[None, :])
    gc = jnp.exp(log_g * c)
    wide = lambda v: jnp.broadcast_to(v[:, :, None], (RET_HEADS, c, LANE))
    return (jnp.cos(ang), jnp.sin(ang), decay, wide(zeta), wide(xi),
            jnp.broadcast_to(gc[:, None, None], (RET_HEADS, c, LANE)))


def _rot(v):
    lane = lax.broadcasted_iota(jnp.int32, v.shape, 1)
    nxt = pltpu.roll(v, LANE - 1, 1)
    prv = pltpu.roll(v, 1, 1)
    return jnp.where(lane % 2 == 0, -nxt, prv)


def _ret_specs(tr, rev, nt):
    ti = (lambda i: nt - 1 - i) if rev else (lambda i: i)
    col = lambda blk: pl.BlockSpec((tr, RET_WIDTH), lambda i: (ti(i), blk))
    tab = pl.BlockSpec((tr, LANE), lambda i: (ti(i), 0))
    head = pl.BlockSpec((RET_HEADS, RET_CHUNK, LANE), lambda i: (0, 0, 0))
    return col, tab, head


def _ret_chunks(tr, rev=False):
    order = list(range(tr // RET_CHUNK))
    return [(pl.ds(ci * RET_CHUNK, RET_CHUNK), slice(h * RET_DIM, (h + 1) * RET_DIM), h)
            for h in range(RET_HEADS) for ci in (reversed(order) if rev else order)]


def _ret_operands(items, q_ref, k_ref, v_ref, cos_ref, sin_ref, zeta_ref):
    scale = RET_DIM ** -0.5
    qbs, kbs, vbs, kzs = [], [], [], []
    for sl, hs, h in items:
        cs, sn = cos_ref[sl, :], sin_ref[sl, :]
        q, k = q_ref[sl, hs], k_ref[sl, hs]
        kr = (k * cs + _rot(k) * sn) * scale
        qbs.append((q * cs + _rot(q) * sn).astype(BF16))
        kbs.append(kr.astype(BF16))
        vbs.append(v_ref[sl, hs].astype(BF16))
        kzs.append((kr * zeta_ref[h]).astype(BF16))
    return qbs, kbs, vbs, kzs


def _ret_states(items, state, steps, gc_ref):
    cur, befores = {}, []
    for (sl, hs, h), step in zip(items, steps):
        st = cur[h] if h in cur else state[h]
        befores.append(st)
        cur[h] = st * gc_ref[h] + step
    for h, st in cur.items():
        state[h] = st
    return befores


def _ret_fwd(u, gain, tabs):
    t = u.shape[0]
    tr = min(512, t)
    nt = t // tr
    cos, sin, decay, zeta, xi, gc = tabs

    def body(q_ref, k_ref, v_ref, gt_ref, cos_ref, sin_ref, gain_ref, dec_ref, zeta_ref, xi_ref, gc_ref,
             raw_ref, mix_ref, state):
        @pl.when(pl.program_id(0) == 0)
        def _():
            state[...] = jnp.zeros_like(state)

        items = _ret_chunks(tr)
        n = range(len(items))
        qbs, kbs, vbs, kzs = _ret_operands(items, q_ref, k_ref, v_ref, cos_ref, sin_ref, zeta_ref)
        ss = [_dot(qbs[i], kbs[i], NT) for i in n]
        kvs = [_dot(kzs[i], vbs[i], TN) for i in n]
        befores = _ret_states(items, state, kvs, gc_ref)
        intra = [_dot((ss[i] * dec_ref[items[i][2]]).astype(BF16), vbs[i]) for i in n]
        inter = [_dot(qbs[i], befores[i].astype(BF16)) for i in n]
        for i, (sl, hs, h) in enumerate(items):
            o = intra[i] + inter[i] * xi_ref[h]
            raw_ref[sl, hs] = o
            mu = jnp.mean(o, axis=-1, keepdims=True)
            var = jnp.mean(jnp.square(o - mu), axis=-1, keepdims=True)
            y = (o - mu) * lax.rsqrt(var + GN_EPS) * gain_ref[:, hs]
            gt = gt_ref[sl, hs]
            mix_ref[sl, hs] = (y * (gt * _sigmoid(gt))).astype(BF16)

    col, tab, head = _ret_specs(tr, False, nt)
    out = pl.BlockSpec((tr, RET_WIDTH), lambda i: (i, 0))
    return pl.pallas_call(
        body, name="ret_fwd", grid=(nt,),
        in_specs=[col(0), col(1), col(2), col(3), tab, tab, pl.BlockSpec((1, RET_WIDTH), lambda i: (0, 0)),
                  head, head, head, head],
        out_specs=[out, out],
        out_shape=[jax.ShapeDtypeStruct((t, RET_WIDTH), F32), jax.ShapeDtypeStruct((t, RET_WIDTH), BF16)],
        scratch_shapes=[pltpu.VMEM((RET_HEADS, RET_DIM, RET_DIM), F32)],
        compiler_params=_params("arbitrary"),
    )(u, u, u, u, cos, sin, gain, decay, zeta, xi, gc)


def _ret_bwd_q(dmix, raw, u, gain, tabs, rides=None):
    t = u.shape[0]
    tr = min(512, t)
    nt = t // tr
    cos, sin, decay, zeta, xi, gc = tabs

    def body(dm_ref, raw_ref, q_ref, k_ref, v_ref, gt_ref, cos_ref, sin_ref, gain_ref, dec_ref, zeta_ref, xi_ref, gc_ref,
             dq_ref, dgt_ref, dret_ref, dgain_ref, state):
        @pl.when(pl.program_id(0) == 0)
        def _():
            state[...] = jnp.zeros_like(state)
            dgain_ref[...] = jnp.zeros_like(dgain_ref)

        items = _ret_chunks(tr)
        n_items = range(len(items))
        qbs, kbs, vbs, kzs = _ret_operands(items, q_ref, k_ref, v_ref, cos_ref, sin_ref, zeta_ref)
        dos, dgains = [], {}
        for sl, hs, h in items:
            o = raw_ref[sl, hs]
            mu = jnp.mean(o, axis=-1, keepdims=True)
            var = jnp.mean(jnp.square(o - mu), axis=-1, keepdims=True)
            rs = lax.rsqrt(var + GN_EPS)
            n = (o - mu) * rs
            gt = gt_ref[sl, hs]
            sig = _sigmoid(gt)
            dout = dm_ref[sl, hs]
            gain_h = gain_ref[:, hs]
            dgt_ref[sl, hs] = (dout * (n * gain_h) * (sig * (1.0 + gt * (1.0 - sig)))).astype(BF16)
            dy = dout * (gt * sig)
            dgains[h] = dgains[h] + _rows8(dy * n) if h in dgains else _rows8(dy * n)
            dn = dy * gain_h
            do = rs * (dn - jnp.mean(dn, axis=-1, keepdims=True) - n * jnp.mean(dn * n, axis=-1, keepdims=True))
            dret_ref[sl, hs] = do
            dos.append(do)
        for h, dg in dgains.items():
            dgain_ref[:, h * RET_DIM:(h + 1) * RET_DIM] += dg
        dss = [_dot(dos[i].astype(BF16), vbs[i], NT) for i in n_items]
        kvs = [_dot(kzs[i], vbs[i], TN) for i in n_items]
        befores = _ret_states(items, state, kvs, gc_ref)
        intra = [_dot((dss[i] * dec_ref[items[i][2]]).astype(BF16), kbs[i]) for i in n_items]
        inter = [_dot((dos[i] * xi_ref[items[i][2]]).astype(BF16), befores[i].astype(BF16), NT) for i in n_items]
        for i, (sl, hs, h) in enumerate(items):
            dqr = intra[i] + inter[i]
            dq_ref[sl, hs] = (dqr * cos_ref[sl, :] - _rot(dqr * sin_ref[sl, :])).astype(BF16)

    col, tab, head = _ret_specs(tr, False, nt)
    out = pl.BlockSpec((tr, RET_WIDTH), lambda i: (i, 0))
    return _pallas(
        body, rides, name="ret_bwd_q", grid=(nt,),
        in_specs=[out, out, col(0), col(1), col(2), col(3), tab, tab, pl.BlockSpec((1, RET_WIDTH), lambda i: (0, 0)),
                  head, head, head, head],
        out_specs=[out, out, out, pl.BlockSpec((8, RET_WIDTH), lambda i: (0, 0))],
        out_shape=[jax.ShapeDtypeStruct((t, RET_WIDTH), BF16), jax.ShapeDtypeStruct((t, RET_WIDTH), BF16),
                   jax.ShapeDtypeStruct((t, RET_WIDTH), F32), jax.ShapeDtypeStruct((8, RET_WIDTH), F32)],
        scratch_shapes=[pltpu.VMEM((RET_HEADS, RET_DIM, RET_DIM), F32)],
        sem=("arbitrary",), args=[dmix, raw, u, u, u, u, cos, sin, gain, decay, zeta, xi, gc])


def _ret_bwd_kv(dret, u, tabs, rides=None):
    t = u.shape[0]
    tr = min(512, t)
    nt = t // tr
    cos, sin, decay, zeta, xi, gc = tabs
    scale = RET_DIM ** -0.5

    def body(do_ref, q_ref, k_ref, v_ref, cos_ref, sin_ref, dec_ref, zeta_ref, xi_ref, gc_ref, dk_ref, dv_ref, gst):
        @pl.when(pl.program_id(0) == 0)
        def _():
            gst[...] = jnp.zeros_like(gst)

        items = _ret_chunks(tr, rev=True)
        n = range(len(items))
        qbs, kbs, vbs, kzs = _ret_operands(items, q_ref, k_ref, v_ref, cos_ref, sin_ref, zeta_ref)
        dos = [do_ref[sl, hs] for sl, hs, h in items]
        dobs = [do.astype(BF16) for do in dos]
        ss = [_dot(qbs[i], kbs[i], NT) for i in n]
        dss = [_dot(dobs[i], vbs[i], NT) for i in n]
        steps = [_dot(qbs[i], (dos[i] * xi_ref[items[i][2]]).astype(BF16), TN) for i in n]
        afters = [g.astype(BF16) for g in _ret_states(items, gst, steps, gc_ref)]
        dvs = [_dot((ss[i] * dec_ref[items[i][2]]).astype(BF16), dobs[i], TN) + _dot(kzs[i], afters[i]) for i in n]
        dks = [_dot((dss[i] * dec_ref[items[i][2]]).astype(BF16), qbs[i], TN) for i in n]
        dkz = [_dot(vbs[i], afters[i], NT) for i in n]
        for i, (sl, hs, h) in enumerate(items):
            dv_ref[sl, hs] = dvs[i].astype(BF16)
            dkr = (dks[i] + dkz[i] * zeta_ref[h]) * scale
            dk_ref[sl, hs] = (dkr * cos_ref[sl, :] - _rot(dkr * sin_ref[sl, :])).astype(BF16)

    col, tab, head = _ret_specs(tr, True, nt)
    out = pl.BlockSpec((tr, RET_WIDTH), lambda i: (nt - 1 - i, 0))
    return _pallas(
        body, rides, name="ret_bwd_kv", grid=(nt,),
        in_specs=[out, col(0), col(1), col(2), tab, tab, head, head, head, head],
        out_specs=[out, out],
        out_shape=[jax.ShapeDtypeStruct((t, RET_WIDTH), BF16), jax.ShapeDtypeStruct((t, RET_WIDTH), BF16)],
        scratch_shapes=[pltpu.VMEM((RET_HEADS, RET_DIM, RET_DIM), F32)],
        sem=("arbitrary",), args=[dret, u, u, u, cos, sin, decay, zeta, xi, gc])


PAIRS = ATT_WIDTH // LANE
ATT_Q_BLK, ATT_K_BLK, ATT_V_BLK = 0, PAIRS, 2 * PAIRS
STAT_LANES = ATT_DIM // 2


ATT_STEP_ROWS = 4096


def _att_tiles(t, dil):
    sub = t // dil
    tq = min(ATT_STEP_ROWS, sub)
    return sub, tq, sub // tq, tq // ATT_BLOCK, min(dil, ATT_STEP_ROWS // tq)


def _att_in_specs(tq, qb, ti, gs):
    cur = lambda off: pl.BlockSpec((gs, tq, LANE), lambda g, p, i: (g, ti(i), off + p))
    prev = lambda off: pl.BlockSpec((gs, ATT_BLOCK, LANE), lambda g, p, i: (g, jnp.maximum(ti(i) * qb - 1, 0), off + p))
    return [cur(ATT_Q_BLK), cur(ATT_K_BLK), prev(ATT_K_BLK), cur(ATT_V_BLK), prev(ATT_V_BLK)]


def _band_mask():
    key = lax.broadcasted_iota(jnp.int32, (2 * ATT_BLOCK, 2 * ATT_BLOCK), 0)
    qry = lax.broadcasted_iota(jnp.int32, (2 * ATT_BLOCK, 2 * ATT_BLOCK), 1) % ATT_BLOCK
    dist = qry + ATT_BLOCK - key
    return (dist >= 0) & (dist <= ATT_BLOCK), key >= ATT_BLOCK


def _head0_lanes():
    return lax.broadcasted_iota(jnp.int32, (ATT_BLOCK, LANE), 1) < ATT_DIM


def _stack_heads(v, head0):
    zero = jnp.zeros((), v.dtype)
    return jnp.concatenate([jnp.where(head0, v, zero), jnp.where(head0, zero, v)], axis=0)


def _unstack_heads(v, head0):
    return jnp.where(head0, v[0:ATT_BLOCK], v[ATT_BLOCK:])


def _att_fwd(ua, dil):
    sub = ua.shape[1]
    _, tq, nq, qb, gs = _att_tiles(sub * dil, dil)

    def body(q_ref, kc_ref, kp_ref, vc_ref, vp_ref, o_ref, l_ref, kx, vx):
        tile = pl.program_id(2)
        kx[:, 0:ATT_BLOCK, :] = kp_ref[...]
        kx[:, ATT_BLOCK:, :] = kc_ref[...]
        vx[:, 0:ATT_BLOCK, :] = vp_ref[...]
        vx[:, ATT_BLOCK:, :] = vc_ref[...]
        band, cur_keys = _band_mask()
        head0 = _head0_lanes()
        items = [(r, b) for r in range(gs) for b in range(qb)]
        rows = lambda b: slice(b * ATT_BLOCK, (b + 1) * ATT_BLOCK)
        keys = lambda b: slice(b * ATT_BLOCK, (b + 2) * ATT_BLOCK)
        sts = [_dot(kx[r, keys(b), :], _stack_heads(q_ref[r, rows(b), :] * jnp.asarray(ATT_DIM ** -0.5, BF16), head0), NT)
               for r, b in items]
        pts, lses = [], []
        for (r, b), st in zip(items, sts):
            mask = band if b > 0 else band & (cur_keys | (tile > 0))
            st = jnp.where(mask, st, -1e30)
            m = jnp.max(st, axis=0, keepdims=True)
            ex = jnp.exp(st - m)
            den = jnp.sum(ex, axis=0, keepdims=True)
            pts.append((ex * (1.0 / den)).astype(BF16))
            lses.append(m + jnp.log(den))
        outs = [_dot(pt, vx[r, keys(b), :], TN) for (r, b), pt in zip(items, pts)]
        for (r, b), out, lse in zip(items, outs, lses):
            o_ref[r, rows(b), :] = _unstack_heads(out, head0).astype(BF16)
            cols = [jnp.broadcast_to(lse[:, e * ATT_BLOCK:(e + 1) * ATT_BLOCK], (ATT_BLOCK, LANE)).T for e in range(2)]
            l_ref[r, rows(b), :] = jnp.where(head0, cols[0], cols[1])

    out = pl.BlockSpec((gs, tq, LANE), lambda g, p, i: (g, i, p))
    return pl.pallas_call(
        body, name=f"att_fwd_d{dil}", grid=(dil // gs, PAIRS, nq),
        in_specs=_att_in_specs(tq, qb, lambda i: i, gs),
        out_specs=[out, out],
        out_shape=[jax.ShapeDtypeStruct((dil, sub, ATT_WIDTH), BF16), jax.ShapeDtypeStruct((dil, sub, ATT_WIDTH), F32)],
        scratch_shapes=[pltpu.VMEM((gs, tq + ATT_BLOCK, LANE), BF16)] * 2,
        compiler_params=_params("arbitrary", "arbitrary", "arbitrary"),
    )(ua, ua, ua, ua, ua)


def _regrouped_spec(tm, dil, w):
    return pl.BlockSpec((dil, tm // dil, w), lambda i: (0, i, 0))


def _att_combine(outs, lses, t):
    w = ATT_WIDTH
    tm = min(512, t)
    nb = len(outs)

    def body(*refs):
        o_refs, l_refs = refs[:nb], refs[nb:2 * nb]
        mix_ref, att_ref, lse_ref, buf = refs[2 * nb:]
        ls = [_natural_rows(r, dil, buf) for r, dil in zip(l_refs, DILATIONS)]
        m = functools.reduce(jnp.maximum, ls)
        ws = [jnp.exp(l - m) for l in ls]
        den = functools.reduce(jnp.add, ws)
        att = functools.reduce(jnp.add, [(wt / den) * _natural_rows(r, dil, buf) for wt, r, dil in zip(ws, o_refs, DILATIONS)])
        att_ref[...] = att
        mix_ref[...] = att.astype(BF16)
        lse_ref[...] = m + jnp.log(den)

    tile = pl.BlockSpec((tm, w), lambda i: (i, 0))
    regrouped = [_regrouped_spec(tm, dil, w) for dil in DILATIONS]
    return pl.pallas_call(
        body, name="att_combine", grid=(t // tm,),
        in_specs=regrouped * 2, out_specs=[tile, tile, tile],
        out_shape=[jax.ShapeDtypeStruct((t, w), BF16), jax.ShapeDtypeStruct((t, w), F32), jax.ShapeDtypeStruct((t, w), F32)],
        scratch_shapes=[_chunk_scratch(tm, w)],
        compiler_params=_params("arbitrary"),
    )(*outs, *lses)


def _att_bwd_prep(datt, att, lse):
    t, w = datt.shape
    tm = min(512, t)

    def body(da_ref, at_ref, l_ref, *rest):
        outs, dbuf, sbuf = rest[:-2], rest[-2], rest[-1]
        dav = da_ref[...]
        prod = dav * at_ref[...]
        lane = lax.broadcasted_iota(jnp.int32, (tm, LANE), 1)
        for k in range(w // LANE):
            cols = slice(k * LANE, (k + 1) * LANE)
            dbuf[k] = dav[:, cols]
            delta = jnp.concatenate(
                [jnp.broadcast_to(jnp.sum(prod[:, k * LANE + e * ATT_DIM:k * LANE + (e + 1) * ATT_DIM], axis=-1, keepdims=True),
                                  (tm, ATT_DIM)) for e in range(LANE // ATT_DIM)], axis=1)
            sbuf[k] = jnp.where(lane % ATT_DIM < STAT_LANES, l_ref[:, cols], delta)
        for k, dil in enumerate(DILATIONS):
            _regroup_store(dbuf, outs[2 * k], dil)
            _regroup_store(sbuf, outs[2 * k + 1], dil)

    tile = pl.BlockSpec((tm, w), lambda i: (i, 0))
    res = pl.pallas_call(
        body, name="att_bwd_prep", grid=(t // tm,),
        in_specs=[tile] * 3,
        out_specs=[_regrouped_spec(tm, dil, w) for dil in DILATIONS for _ in range(2)],
        out_shape=[jax.ShapeDtypeStruct((dil, t // dil, w), dt) for dil in DILATIONS for dt in (BF16, F32)],
        scratch_shapes=[_chunk_scratch(tm, w)] * 2,
        compiler_params=_params("arbitrary"),
    )(datt, att, lse)
    return [(res[2 * k], res[2 * k + 1]) for k in range(len(DILATIONS))]


def _att_bwd(ua, da, stat, dil, rides=None):
    sub = ua.shape[1]
    _, tq, nq, qb, gs = _att_tiles(sub * dil, dil)
    scale = ATT_DIM ** -0.5

    def body(q_ref, kc_ref, kp_ref, vc_ref, vp_ref, da_ref, st_ref, dq_ref, dk_ref, dv_ref, kx, vx, ck, cv):
        step = pl.program_id(2)
        tile = nq - 1 - step

        @pl.when(step == 0)
        def _():
            ck[...] = jnp.zeros_like(ck)
            cv[...] = jnp.zeros_like(cv)

        kx[:, 0:ATT_BLOCK, :] = kp_ref[...]
        kx[:, ATT_BLOCK:, :] = kc_ref[...]
        vx[:, 0:ATT_BLOCK, :] = vp_ref[...]
        vx[:, ATT_BLOCK:, :] = vc_ref[...]
        band, cur_keys = _band_mask()
        head0 = _head0_lanes()
        items = [(r, b) for r in range(gs) for b in range(qb)]
        n = range(len(items))
        rows = lambda b: slice(b * ATT_BLOCK, (b + 1) * ATT_BLOCK)
        keys = lambda b: slice(b * ATT_BLOCK, (b + 2) * ATT_BLOCK)
        qqs = [_stack_heads(q_ref[r, rows(b), :] * jnp.asarray(scale, BF16), head0) for r, b in items]
        dds = [_stack_heads(da_ref[r, rows(b), :], head0) for r, b in items]
        sts = [_dot(kx[r, keys(b), :], qqs[i], NT) for i, (r, b) in enumerate(items)]
        dpts = [_dot(vx[r, keys(b), :], dds[i], NT) for i, (r, b) in enumerate(items)]
        pts, dsts = [], []
        for i, (r, b) in enumerate(items):
            mask = band if b > 0 else band & (cur_keys | (tile > 0))
            stat = st_ref[r, rows(b), :].T
            row = lambda k: jnp.concatenate([stat[e * ATT_DIM + k:e * ATT_DIM + k + 1, :] for e in range(2)], axis=1)
            pt = jnp.where(mask, jnp.exp(sts[i] - row(0)), 0.0)
            dsts.append((pt * (dpts[i] - row(STAT_LANES))).astype(BF16))
            pts.append(pt.astype(BF16))
        dqs = [_dot(dsts[i], kx[r, keys(b), :], TN) for i, (r, b) in enumerate(items)]
        dkbs = [_dot(dsts[i], qqs[i]) for i in n]
        dvbs = [_dot(pts[i], dds[i]) for i in n]
        for i, (r, b) in enumerate(items):
            dq_ref[r, rows(b), :] = (_unstack_heads(dqs[i], head0) * scale).astype(BF16)
            if b > 0:
                dk_ref[r, rows(b - 1), :] = (dkbs[i - 1][ATT_BLOCK:] + dkbs[i][0:ATT_BLOCK]).astype(BF16)
                dv_ref[r, rows(b - 1), :] = (dvbs[i - 1][ATT_BLOCK:] + dvbs[i][0:ATT_BLOCK]).astype(BF16)
        for r in range(gs):
            first, last = r * qb, r * qb + qb - 1
            dk_ref[r, rows(qb - 1), :] = (dkbs[last][ATT_BLOCK:] + ck[r]).astype(BF16)
            dv_ref[r, rows(qb - 1), :] = (dvbs[last][ATT_BLOCK:] + cv[r]).astype(BF16)
            ck[r] = dkbs[first][0:ATT_BLOCK]
            cv[r] = dvbs[first][0:ATT_BLOCK]

    ti = lambda i: nq - 1 - i
    out = pl.BlockSpec((gs, tq, LANE), lambda g, p, i: (g, ti(i), p))
    shape = jax.ShapeDtypeStruct((dil, sub, ATT_WIDTH), BF16)
    return _pallas(
        body, rides, name=f"att_bwd_d{dil}", grid=(dil // gs, PAIRS, nq),
        in_specs=_att_in_specs(tq, qb, ti, gs) + [out, out],
        out_specs=[out, out, out], out_shape=[shape] * 3,
        scratch_shapes=[pltpu.VMEM((gs, tq + ATT_BLOCK, LANE), BF16)] * 2 + [pltpu.VMEM((gs, ATT_BLOCK, LANE), F32)] * 2,
        sem=("arbitrary", "arbitrary", "arbitrary"), args=[ua, ua, ua, ua, ua, da, stat])


class _Reduction:
    def __init__(self, place, names, grads):
        self.place, self.names, self.grads = place, names, grads

    def pair(self):
        return _pair_ride(self.grads)

    def chips(self, got):
        self.got = got
        return _chip_ride([_pair_sum(self.place, g, r, f"pair_sum_{n}") for g, r, n in zip(self.grads, got, self.names)])

    def halves(self, others):
        return [_chip_sum(self.place, g, r, o, f"chip_sum_{n}")
                for g, r, o, n in zip(self.grads, self.got, others, self.names)]


def _step(x, target, gains, w, place=None):
    t = x.shape[0]
    ex = place is not None
    g_ffn1, g_mix, g_ret, g_ffn2, g_fin = gains
    w = list(w)
    tabs = _retention_tables(t)
    red = lambda names, grads: _Reduction(place, names, grads) if ex else None
    ride = lambda r: [r] if ex else None

    if ex:
        w[0:3] = _run(_gather_ride(w[0:3]), "gather_ffn1_weights")
    (h1, xn1, *hid1, act1), rest = _ffn_fwd(x, g_ffn1, *w[0:3], "ffn1_fwd", ride(_gather_ride(w[3:])) if ex else None)
    if ex:
        w[3:] = rest[0]
    wg1, wu1, wd1, win, wo, wg2, wu2, wd2 = w
    wo2 = wo.reshape(wo.shape[0] * wo.shape[1], wo.shape[2])
    xnm, u, *uas = _inproj_fwd(h1, g_mix, win)
    raw, mix_r = _ret_fwd(u, g_ret, tabs)
    branches = [_att_fwd(ua, dil) for ua, dil in zip(uas, DILATIONS)]
    mix_a, att, lse = _att_combine([b[0] for b in branches], [b[1] for b in branches], t)
    h2 = _outproj_fwd(h1, mix_r, mix_a, wo2)
    (dh3, xn2, *hid2, act2, loss_p, dg_fin), _ = _ffn_fwd(h2, g_ffn2, wg2, wu2, wd2, "ffn2_fwd", head=(g_fin, target))

    (dwd2,), _ = _ffn_wgrad_down(act2, dh3, "ffn2_wgrad_down")
    dwd2 = dwd2.reshape(wd2.shape)
    r_d2 = red(["ffn2_w_down"], [dwd2])
    (dh2, dga2, dua2, dg_ffn2), e = _ffn_bwd_data(dh3, h2, g_ffn2, *hid2, wg2, wu2, wd2, "ffn2_bwd",
                                                  ex and [r_d2.pair()])
    (dwg2, dwu2), e = _ffn_wgrad_gu(xn2, [dga2, dua2], "ffn2_wgrad_gu", ex and [r_d2.chips(e[0])])
    dwg2, dwu2 = dwg2.reshape(wg2.shape), dwu2.reshape(wu2.shape)
    r_gu2 = red(["ffn2_w_gate", "ffn2_w_up"], [dwg2, dwu2])
    (dmix_r, dmix_a), e = _outproj_bwd(dh2, wo2, ex and [r_gu2.pair(), _finish_ride(r_d2.halves(e[0]))])
    if ex:
        got_gu2, (dwd2,) = e
    dwo = _wo_grad(mix_r, mix_a, dh2).reshape(wo.shape)
    r_wo = red(["w_out"], [dwo])
    (dq_r, dgt_r, dret, dg_ret), e = _ret_bwd_q(dmix_r, raw, u, g_ret, tabs, ex and [r_gu2.chips(got_gu2)])
    (dk_r, dv_r), e = _ret_bwd_kv(dret, u, tabs, ex and [r_wo.pair(), _finish_ride(r_gu2.halves(e[0]))])
    if ex:
        got_wo, (dwg2, dwu2) = e
    prep = _att_bwd_prep(dmix_a, att, lse)
    p1, e = _att_bwd(uas[0], *prep[0], DILATIONS[0], ex and [r_wo.chips(got_wo)])
    p4, e = _att_bwd(uas[1], *prep[1], DILATIONS[1], ex and [_finish_ride(r_wo.halves(e[0]))])
    if ex:
        (dwo,), = e
    p16, _ = _att_bwd(uas[2], *prep[2], DILATIONS[2])
    dh1, du, dg_mix = _inproj_bwd([dq_r, dk_r, dv_r, dgt_r], [p1, p4, p16], h1, g_mix, dh2, win)
    dwin = _tn_matmul(xnm, du, win.shape[2], "win_grad")
    r_in = red(["w_in"], [dwin])
    (dwd1,), e = _ffn_wgrad_down(act1, dh1, "ffn1_wgrad_down", ex and [r_in.pair()])
    dwd1 = dwd1.reshape(wd1.shape)
    r_d1 = red(["ffn1_w_down"], [dwd1])
    got_in = e
    (dx, dga1, dua1, dg_ffn1), _ = _ffn_bwd_data(dh1, x, g_ffn1, *hid1, wg1, wu1, wd1, "ffn1_bwd")
    (dwg1,), e = _ffn_wgrad_gu(xn1, [dga1], "ffn1_wgrad_gate", ex and [r_in.chips(got_in[0]), r_d1.pair()])
    dwg1 = dwg1.reshape(wg1.shape)
    if ex:
        oth_in, got_d1 = e
        r_g1 = red(["ffn1_w_gate"], [dwg1])
        got_g1 = _run(r_g1.pair(), "pair_exchange_ffn1_gate")
    (dwu1,), e = _ffn_wgrad_gu(xn1, [dua1], "ffn1_wgrad_up",
                               ex and [_finish_ride(r_in.halves(oth_in)), r_d1.chips(got_d1), r_g1.chips(got_g1)])
    dwu1 = dwu1.reshape(wu1.shape)
    gain_parts = [dg_ffn1, dg_mix, dg_ret, dg_ffn2, dg_fin]
    if not ex:
        return loss_p, dx, [dwg1, dwu1, dwd1, dwin, dwo, dwg2, dwu2, dwd2], gain_parts
    (dwin,), oth_d1, oth_g1 = e
    r_u1 = red(["ffn1_w_up"], [dwu1])
    got_u1 = _run(r_u1.pair(), "pair_exchange_ffn1_up")
    oth_u1 = _run(r_u1.chips(got_u1), "chip_exchange_ffn1_up")
    last = r_g1.halves(oth_g1) + r_u1.halves(oth_u1) + r_d1.halves(oth_d1)
    dwg1, dwu1, dwd1, gall = _run(_finish_ride(last, _pack_gains(gain_parts, x.shape[1])), "finish_exchange_ffn1")
    return loss_p, dx, [dwg1, dwu1, dwd1, dwin, dwo, dwg2, dwu2, dwd2], gall


N_DEV = 8
GAIN_ROWS = 8


def _place():
    x, y, c = lax.axis_index("x"), lax.axis_index("y"), lax.axis_index("c")
    chips = [(1 - x, y), (x, 1 - y), (1 - x, 1 - y)]
    return x, y, c, chips


ROW_QUARTERS = 4


def _place_shards(place, ws):
    n = len(ws)

    def body(place_ref, *refs):
        for w_ref, o_ref in zip(refs[:n], refs[n:]):
            o_ref[...] = w_ref[...].astype(BF16)

    quarter = lambda w: (w.shape[0] // ROW_QUARTERS, w.shape[1])
    return pl.pallas_call(
        body, name="place_shards",
        grid_spec=pltpu.PrefetchScalarGridSpec(
            num_scalar_prefetch=1, grid=(ROW_QUARTERS,),
            in_specs=[pl.BlockSpec(quarter(w), lambda i, pr: (i, 0)) for w in ws],
            out_specs=[pl.BlockSpec((None,) + quarter(w), lambda i, pr: (pr[0], i, 0)) for w in ws]),
        out_shape=[jax.ShapeDtypeStruct((N_SHARD,) + w.shape, BF16) for w in ws],
        compiler_params=_params("arbitrary"),
    )(place, *ws)


def _gather_ride(bufs):
    na = len(bufs)
    sent = [("me", "half", "x"), ("me", "half", "y"), ("x", "second quarter", "y"), ("y", "first quarter", "x")]
    landed = [("x", "half"), ("y", "half"), ("d", "second quarter"), ("d", "first quarter")]

    def legs(outs, sems):
        send_sem, recv_sem, fsend_sem, frecv_sem = sems
        x, y, c, _ = _place()
        slot = {"me": 2 * x + y, "x": 2 * (1 - x) + y, "y": 2 * x + (1 - y), "d": 2 * (1 - x) + (1 - y)}
        peer = {"x": (1 - x, y, c), "y": (x, 1 - y, c)}

        def rows(a, which, piece, core):
            hr = outs[a].shape[1] // 2
            lo, n = {"half": (0, hr), "first quarter": (0, hr // 2), "second quarter": (hr // 2, hr // 2)}[piece]
            return outs[a].at[slot[which], pl.ds(core * hr + lo, n)]

        def ici(a, k):
            which, piece, to = sent[k]
            ref = rows(a, which, piece, c)
            return pltpu.make_async_remote_copy(src_ref=ref, dst_ref=ref, send_sem=send_sem.at[a, k],
                                                recv_sem=recv_sem.at[a, k], device_id=peer[to], device_id_type=MESH)

        def arrival(a, k):
            ref = rows(a, *landed[k], c)
            return pltpu.make_async_remote_copy(src_ref=ref, dst_ref=ref, send_sem=send_sem.at[a, k],
                                                recv_sem=recv_sem.at[a, k], device_id=peer["x"], device_id_type=MESH)

        def d2d(a, k, core):
            ref = rows(a, *landed[k], core)
            return pltpu.make_async_remote_copy(src_ref=ref, dst_ref=ref, send_sem=fsend_sem.at[a, k],
                                                recv_sem=frecv_sem.at[a, k], device_id=(x, y, 1 - c), device_id_type=MESH)

        return c, ici, arrival, d2d

    def start(ins, outs, sems):
        _, ici, _, _ = legs(outs, sems)
        for a in range(na):
            ici(a, 0).start()
            ici(a, 1).start()

    def middle(ins, outs, sems):
        c, ici, arrival, d2d = legs(outs, sems)
        for a in range(na):
            for k in (0, 1):
                arrival(a, k).wait_recv()
                ici(a, 2 + k).start()
                d2d(a, k, c).start()

    def finish(ins, outs, sems):
        c, ici, arrival, d2d = legs(outs, sems)
        for a in range(na):
            for k in (2, 3):
                arrival(a, k).wait_recv()
                d2d(a, k, c).start()
        for a in range(na):
            for k in range(len(landed)):
                d2d(a, k, 1 - c).wait_recv()
        for a in range(na):
            for k in range(len(sent)):
                ici(a, k).wait_send()
                d2d(a, k, c).wait_send()

    return _Ride(bufs, [jax.ShapeDtypeStruct(b.shape, b.dtype) for b in bufs], [pltpu.SemaphoreType.DMA((na, 4))] * 4,
                 start, finish, {a: a for a in range(na)}, middle)


def _pair_ride(grads):
    na = len(grads)

    def copies(ins, outs, sems):
        send_sem, recv_sem = sems
        x, y, c, _ = _place()
        res = []
        for a in range(na):
            hr = ins[a].shape[1] // 2
            res.append(pltpu.make_async_remote_copy(
                src_ref=ins[a].at[:, pl.ds((1 - c) * hr, hr)], dst_ref=outs[a],
                send_sem=send_sem.at[a], recv_sem=recv_sem.at[a], device_id=(x, y, 1 - c), device_id_type=MESH))
        return res

    def start(ins, outs, sems):
        for cp in copies(ins, outs, sems):
            cp.start()

    def finish(ins, outs, sems):
        for cp in copies(ins, outs, sems):
            cp.wait()

    return _Ride(grads, [jax.ShapeDtypeStruct((g.shape[0], g.shape[1] // 2, g.shape[2]), g.dtype) for g in grads],
                 [pltpu.SemaphoreType.DMA((na,))] * 2, start, finish)


def _chip_ride(sums):
    na = len(sums)

    def copies(ins, outs, sems):
        send_sem, recv_sem = sems
        x, y, c, chips = _place()
        res = []
        for a in range(na):
            for j, (px, py) in enumerate(chips):
                res.append(pltpu.make_async_remote_copy(
                    src_ref=ins[a].at[2 * px + py], dst_ref=outs[a].at[j],
                    send_sem=send_sem.at[a, j], recv_sem=recv_sem.at[a, j], device_id=(px, py, c), device_id_type=MESH))
        return res

    def start(ins, outs, sems):
        for cp in copies(ins, outs, sems):
            cp.start()

    def finish(ins, outs, sems):
        for cp in copies(ins, outs, sems):
            cp.wait()

    return _Ride(sums, [jax.ShapeDtypeStruct((3,) + s.shape[1:], s.dtype) for s in sums],
                 [pltpu.SemaphoreType.DMA((na, 3))] * 2, start, finish)


def _finish_ride(grads, gpack=None):
    na = len(grads)

    def halves(outs, sems, which):
        x, y, c, _ = _place()
        res = []
        for a in range(na):
            hr = outs[a].shape[0] // 2
            rows = outs[a].at[pl.ds((c if which == "mine" else 1 - c) * hr, hr)]
            res.append(pltpu.make_async_remote_copy(
                src_ref=rows, dst_ref=rows, send_sem=sems[0].at[a], recv_sem=sems[1].at[a],
                device_id=(x, y, 1 - c), device_id_type=MESH))
        return res

    def gains(ins, outs, sems):
        x, y, c, _ = _place()
        dev = 4 * x + 2 * y + c
        g_in, g_out = ins[na], outs[na]
        own = pltpu.make_async_copy(g_in, g_out.at[dev], sems[2])
        sends, lands = [], []
        for k in range(N_DEV - 1):
            bx, by, bc = (k + 1) // 4, ((k + 1) // 2) % 2, (k + 1) % 2
            peer = (jnp.bitwise_xor(x, bx), jnp.bitwise_xor(y, by), jnp.bitwise_xor(c, bc))
            sends.append(pltpu.make_async_remote_copy(
                src_ref=g_in, dst_ref=g_out.at[dev], send_sem=sems[3].at[k], recv_sem=sems[4].at[k],
                device_id=peer, device_id_type=MESH))
            slot = g_out.at[jnp.bitwise_xor(dev, k + 1)]
            lands.append(pltpu.make_async_remote_copy(
                src_ref=slot, dst_ref=slot, send_sem=sems[3].at[k], recv_sem=sems[4].at[k],
                device_id=peer, device_id_type=MESH))
        return own, sends, lands

    def start(ins, outs, sems):
        for cp in halves(outs, sems, "mine"):
            cp.start()
        if gpack is not None:
            own, sends, _ = gains(ins, outs, sems)
            own.start()
            for cp in sends:
                cp.start()

    def finish(ins, outs, sems):
        for cp in halves(outs, sems, "sibling's"):
            cp.wait_recv()
        if gpack is not None:
            own, sends, lands = gains(ins, outs, sems)
            for cp in lands:
                cp.wait_recv()
            for cp in sends:
                cp.wait_send()
            own.wait()
        for cp in halves(outs, sems, "mine"):
            cp.wait_send()

    shapes = [jax.ShapeDtypeStruct(g.shape, g.dtype) for g in grads]
    sems = [pltpu.SemaphoreType.DMA((na,))] * 2
    if gpack is None:
        return _Ride(grads, shapes, sems, start, finish, {a: a for a in range(na)})
    return _Ride(list(grads) + [gpack], shapes + [jax.ShapeDtypeStruct((N_DEV,) + gpack.shape, gpack.dtype)],
                 sems + [pltpu.SemaphoreType.DMA, pltpu.SemaphoreType.DMA((N_DEV - 1,)), pltpu.SemaphoreType.DMA((N_DEV - 1,))],
                 start, finish, {a: a for a in range(na)})


def _pair_sum(place, grad, got, name):
    ns, r, cols = grad.shape
    hr = r // 2

    def body(place_ref, g_ref, r_ref, o_ref):
        o_ref[...] = (g_ref[...] + r_ref[...]).astype(BF16)

    return pl.pallas_call(
        body, name=name,
        grid_spec=pltpu.PrefetchScalarGridSpec(
            num_scalar_prefetch=1, grid=(ns,),
            in_specs=[pl.BlockSpec((None, hr, cols), lambda s, pr: (s, pr[1], 0)),
                      pl.BlockSpec((None, hr, cols), lambda s, pr: (s, 0, 0))],
            out_specs=pl.BlockSpec((None, hr, cols), lambda s, pr: (s, 0, 0))),
        out_shape=jax.ShapeDtypeStruct((ns, hr, cols), BF16),
        compiler_params=_params("arbitrary"),
    )(place, grad, got)


def _chip_sum(place, grad, got, others, name):
    ns, r, cols = grad.shape
    hr = r // 2
    nb = 2
    tr = hr // nb

    def body(place_ref, g_ref, r_ref, o3_ref, o_ref):
        acc = g_ref[...] + r_ref[...]
        for j in range(3):
            acc = acc + o3_ref[j].astype(F32)
        o_ref[...] = acc

    return pl.pallas_call(
        body, name=name,
        grid_spec=pltpu.PrefetchScalarGridSpec(
            num_scalar_prefetch=1, grid=(nb,),
            in_specs=[pl.BlockSpec((None, tr, cols), lambda i, pr: (pr[0], pr[1] * nb + i, 0)),
                      pl.BlockSpec((None, tr, cols), lambda i, pr: (pr[0], i, 0)),
                      pl.BlockSpec((3, tr, cols), lambda i, pr: (0, i, 0))],
            out_specs=pl.BlockSpec((tr, cols), lambda i, pr: (pr[1] * nb + i, 0))),
        out_shape=jax.ShapeDtypeStruct((r, cols), F32),
        compiler_params=_params("arbitrary"),
    )(place, grad, got, others)


def _pack_gains(parts, d):
    def body(*refs):
        ins, o_ref = refs[:-1], refs[-1]
        o_ref[...] = jnp.zeros_like(o_ref)
        for k, r in enumerate(ins):
            o_ref[k:k + 1, 0:r.shape[1]] = jnp.sum(r[...], axis=0, keepdims=True)

    return pl.pallas_call(
        body, name="pack_gains", out_shape=jax.ShapeDtypeStruct((GAIN_ROWS, d), F32),
    )(*parts)


def _adamw_math(w, g, m, v):
    m = ADAM_B1 * m + (1.0 - ADAM_B1) * g
    v = ADAM_B2 * v + (1.0 - ADAM_B2) * jnp.square(g)
    m_hat = m / (1.0 - ADAM_B1 ** ADAM_STEP)
    v_hat = v / (1.0 - ADAM_B2 ** ADAM_STEP)
    return -ADAM_LR * (m_hat / (jnp.sqrt(v_hat) + ADAM_EPS) + ADAM_WD * w), m, v


def _adamw(ws, gs, ms, vs):
    n = len(ws)

    def body(*refs):
        ins, outs = refs[:4 * n], refs[4 * n:]
        for k in range(n):
            w_ref, g_ref, m_ref, v_ref = ins[4 * k:4 * k + 4]
            go_ref, d_ref, nm_ref, nv_ref = outs[4 * k:4 * k + 4]
            g = g_ref[...]
            go_ref[...] = g
            d_ref[...], nm_ref[...], nv_ref[...] = _adamw_math(w_ref[...], g, m_ref[...], v_ref[...])

    parts = 2 * ROW_QUARTERS
    tile = lambda w: pl.BlockSpec((w.shape[0] // parts, w.shape[1]), lambda i: (i, 0))
    res = pl.pallas_call(
        body, name="adamw_shards", grid=(parts,),
        in_specs=[tile(w) for w in ws for _ in range(4)], out_specs=[tile(w) for w in ws for _ in range(4)],
        out_shape=[jax.ShapeDtypeStruct(w.shape, F32) for w in ws for _ in range(4)],
        compiler_params=_params("arbitrary"),
    )(*[a for quad in zip(ws, gs, ms, vs) for a in quad])
    return [res[4 * k:4 * k + 4] for k in range(n)]


def _adamw_gain(gall, row, w, m, v, name):
    n = w.shape[1]

    def body(ga_ref, w_ref, m_ref, v_ref, g_ref, d_ref, nm_ref, nv_ref):
        g = ga_ref[0, row:row + 1, 0:n]
        for k in range(1, N_DEV):
            g = g + ga_ref[k, row:row + 1, 0:n]
        g_ref[...] = g
        d_ref[...], nm_ref[...], nv_ref[...] = _adamw_math(w_ref[...], g, m_ref[...], v_ref[...])

    return pl.pallas_call(
        body, name=name, out_shape=[jax.ShapeDtypeStruct((1, n), F32)] * 4,
    )(gall, w, m, v)


def kernel(x, norm_ffn1, ffn1_w_gate, ffn1_w_up, ffn1_w_down, norm_mix, w_in, ret_norm_gain, w_out, norm_ffn2, ffn2_w_gate, ffn2_w_up, ffn2_w_down, norm_final, loss_target, m_norm_ffn1, m_ffn1_w_gate, m_ffn1_w_up, m_ffn1_w_down, m_norm_mix, m_w_in, m_ret_norm_gain, m_w_out, m_norm_ffn2, m_ffn2_w_gate, m_ffn2_w_up, m_ffn2_w_down, m_norm_final, v_norm_ffn1, v_ffn1_w_gate, v_ffn1_w_up, v_ffn1_w_down, v_norm_mix, v_w_in, v_ret_norm_gain, v_w_out, v_norm_ffn2, v_ffn2_w_gate, v_ffn2_w_up, v_ffn2_w_down, v_norm_final):
    d = x.shape[-1]
    mats = [ffn1_w_gate, ffn1_w_up, ffn1_w_down, w_in, w_out, ffn2_w_gate, ffn2_w_up, ffn2_w_down]
    mats_m = [m_ffn1_w_gate, m_ffn1_w_up, m_ffn1_w_down, m_w_in, m_w_out, m_ffn2_w_gate, m_ffn2_w_up, m_ffn2_w_down]
    mats_v = [v_ffn1_w_gate, v_ffn1_w_up, v_ffn1_w_down, v_w_in, v_w_out, v_ffn2_w_gate, v_ffn2_w_up, v_ffn2_w_down]
    mat_names = ["ffn1_w_gate", "ffn1_w_up", "ffn1_w_down", "w_in", "w_out", "ffn2_w_gate", "ffn2_w_up", "ffn2_w_down"]
    gains = [norm_ffn1, norm_mix, ret_norm_gain, norm_ffn2, norm_final.reshape(1, d)]
    gains_m = [m_norm_ffn1, m_norm_mix, m_ret_norm_gain, m_norm_ffn2, m_norm_final.reshape(1, d)]
    gains_v = [v_norm_ffn1, v_norm_mix, v_ret_norm_gain, v_norm_ffn2, v_norm_final.reshape(1, d)]
    gain_names = ["norm_ffn1", "norm_mix", "ret_norm_gain", "norm_ffn2", "norm_final"]

    turned = lambda n: n.endswith(("w_gate", "w_up"))
    local = lambda a, n: jnp.swapaxes(a, 1, 2)[0] if turned(n) else a[0]
    back = lambda a, n: jnp.swapaxes(a[None], 1, 2) if turned(n) else a[None]
    shards = [local(w, n) for w, n in zip(mats, mat_names)]
    place = jnp.stack([2 * lax.axis_index("x") + lax.axis_index("y"), lax.axis_index("c")]).astype(jnp.int32)
    placed = _place_shards(place, shards)
    loss_p, dx, shard_grads, gall = _step(x[0], loss_target[0], gains, placed, place)

    out_g, out_d, out_m, out_v = {}, {}, {}, {}
    updates = _adamw(shards, shard_grads, [local(m, n) for m, n in zip(mats_m, mat_names)],
                     [local(v, n) for v, n in zip(mats_v, mat_names)])
    for n, quad in zip(mat_names, updates):
        out_g[n], out_d[n], out_m[n], out_v[n] = [back(a, n) for a in quad]
    for row, (n, w, m, v) in enumerate(zip(gain_names, gains, gains_m, gains_v)):
        res = _adamw_gain(gall, row, w, m, v, f"adamw_{n}")
        shape = (d,) if n == "norm_final" else w.shape
        out_g[n], out_d[n], out_m[n], out_v[n] = [r.reshape(shape) for r in res]

    loss = lax.psum(jnp.sum(loss_p), ("x", "y", "c"))
    order = ["norm_ffn1", "ffn1_w_gate", "ffn1_w_up", "ffn1_w_down", "norm_mix", "w_in", "ret_norm_gain", "w_out",
             "norm_ffn2", "ffn2_w_gate", "ffn2_w_up", "ffn2_w_down", "norm_final"]
    return (loss, dx[None], *[out_g[n] for n in order], *[out_d[n] for n in order],
            *[out_m[n] for n in order], *[out_v[n] for n in order])
```
